```python
import jax, jax.numpy as jnp
from jax import lax
import numpy as np

D_MODEL = 2048
BATCH = 8
SEQ = 2048
DEPTH = 1

N_META = 16
POOL_WIDTH = D_MODEL
POOL_WINDOWS = (2, 4, 8, 16)
N_POOL_GROUPS = len(POOL_WINDOWS)
POOL_GROUP = POOL_WIDTH // N_POOL_GROUPS
CONV_WIDTH = D_MODEL
CONV_K = 3
N_BRANCHES = 2
FFN_HIDDEN = ((8 * D_MODEL // 3 + 255) // 256) * 256
IN_PROJ_WIDTH = POOL_WIDTH + 3 * CONV_WIDTH + N_BRANCHES * D_MODEL
EPS = 1e-6

kernel_name = "hybrid_pool_shortconv_gated_block"


def rms_norm(x, g):
    xf = x.astype(jnp.float32)
    y = xf * lax.rsqrt(jnp.mean(xf * xf, axis=-1, keepdims=True) + EPS)
    return (y * g.astype(jnp.float32)).astype(x.dtype)


def causal_multiscale_pool(u):
    b, l, _ = u.shape
    ug = u.reshape(b, l, N_POOL_GROUPS, POOL_GROUP).astype(jnp.float32)
    c = jnp.concatenate([jnp.zeros((b, 1, N_POOL_GROUPS, POOL_GROUP), jnp.float32),
                         jnp.cumsum(ug, axis=1)], axis=1)
    t1 = jnp.arange(1, l + 1, dtype=jnp.float32)
    outs = []
    for gi, w in enumerate(POOL_WINDOWS):
        cg = c[:, :, gi]
        c_lag = jnp.pad(cg, ((0, 0), (w - 1, 0), (0, 0)))[:, :l]
        win_sum = cg[:, 1:] - c_lag
        count = jnp.minimum(t1, jnp.float32(w))[None, :, None]
        outs.append(win_sum / count - ug[:, :, gi])
    return jnp.stack(outs, axis=2).astype(u.dtype)


def causal_depthwise_conv(v, w):
    l = v.shape[1]
    vp = jnp.pad(v, ((0, 0), (CONV_K - 1, 0), (0, 0)))
    return sum(w[k][None, None, :] * vp[:, k:k + l] for k in range(CONV_K))


def _fwd_setup_inputs(seed: int = 0) -> dict:
    key = jax.random.key(seed)
    ks = jax.random.split(key, 16)
    f32 = jnp.float32
    nrm = lambda k, shape, scale: jax.random.normal(k, shape, f32) * scale
    return {
        "x": nrm(ks[0], (BATCH, SEQ, D_MODEL), 1.0),
        "meta_tokens": nrm(ks[1], (N_META, D_MODEL), 1.0),
        "norm_mix_g": 1.0 + nrm(ks[2], (D_MODEL,), 0.02),
        "w_in": nrm(ks[3], (D_MODEL, IN_PROJ_WIDTH), D_MODEL ** -0.5),
        "b_gate": nrm(ks[4], (N_BRANCHES * D_MODEL,), 0.02),
        "pool_w": nrm(ks[5], (N_POOL_GROUPS, POOL_GROUP, POOL_GROUP), POOL_GROUP ** -0.5),
        "pool_scale": 1.0 + nrm(ks[6], (POOL_WIDTH,), 0.02),
        "conv_w": nrm(ks[7], (CONV_K, CONV_WIDTH), CONV_K ** -0.5),
        "conv_out_w": nrm(ks[8], (CONV_WIDTH, D_MODEL), CONV_WIDTH ** -0.5),
        "w_o": nrm(ks[9], (D_MODEL, D_MODEL), D_MODEL ** -0.5),
        "norm_ffn_g": 1.0 + nrm(ks[10], (D_MODEL,), 0.02),
        "w_gate_up": nrm(ks[11], (D_MODEL, 2 * FFN_HIDDEN), D_MODEL ** -0.5),
        "w_down": nrm(ks[12], (FFN_HIDDEN, D_MODEL), FFN_HIDDEN ** -0.5),
        "norm_final_g": 1.0 + nrm(ks[13], (D_MODEL,), 0.02),
    }


def _fwd_reference(x, meta_tokens, norm_mix_g, w_in, b_gate, pool_w, pool_scale, conv_w,
              conv_out_w, w_o, norm_ffn_g, w_gate_up, w_down, norm_final_g):
    b = x.shape[0]
    meta = jnp.broadcast_to(meta_tokens[None].astype(x.dtype), (b, N_META, D_MODEL))
    h = jnp.concatenate([meta, x], axis=1)

    splits = np.cumsum([POOL_WIDTH, CONV_WIDTH, CONV_WIDTH, CONV_WIDTH, D_MODEL]).tolist()
    for _ in range(DEPTH):
        hn = rms_norm(h, norm_mix_g)
        proj = hn @ w_in
        u, gb_in, gc_in, v_in, ga_lin, gbr_lin = jnp.split(proj, splits, axis=-1)

        pooled = causal_multiscale_pool(u)
        y_a = jnp.einsum("blgc,gcd->blgd", pooled, pool_w).reshape(b, -1, POOL_WIDTH)
        y_a = y_a * pool_scale

        y_b = (gb_in * causal_depthwise_conv(gc_in * v_in, conv_w)) @ conv_out_w

        gates = jax.nn.sigmoid(jnp.concatenate([ga_lin, gbr_lin], axis=-1) + b_gate)
        g_a, g_b = jnp.split(gates, 2, axis=-1)
        h = h + (g_a * y_a + g_b * y_b) @ w_o

        hn = rms_norm(h, norm_ffn_g)
        gate, up = jnp.split(hn @ w_gate_up, 2, axis=-1)
        h = h + (jax.nn.silu(gate) * up) @ w_down

    out = rms_norm(h, norm_final_g)
    return out[:, N_META:]


import jax as _jax
import jax.numpy as _jnp

TWIN_FORMAT = 'train_step'
FWD_PARAMS = ['x', 'meta_tokens', 'norm_mix_g', 'w_in', 'b_gate', 'pool_w', 'pool_scale', 'conv_w', 'conv_out_w', 'w_o', 'norm_ffn_g', 'w_gate_up', 'w_down', 'norm_final_g']
TWIN_WEIGHTS = ['meta_tokens', 'norm_mix_g', 'w_in', 'b_gate', 'pool_w', 'pool_scale', 'conv_w', 'conv_out_w', 'w_o', 'norm_ffn_g', 'w_gate_up', 'w_down', 'norm_final_g']
TWIN_DIFF_INPUT = 'x'
TWIN_INPUTS = ['x', 'meta_tokens', 'norm_mix_g', 'w_in', 'b_gate', 'pool_w', 'pool_scale', 'conv_w', 'conv_out_w', 'w_o', 'norm_ffn_g', 'w_gate_up', 'w_down', 'norm_final_g', 'loss_target', 'm_meta_tokens', 'm_norm_mix_g', 'm_w_in', 'm_b_gate', 'm_pool_w', 'm_pool_scale', 'm_conv_w', 'm_conv_out_w', 'm_w_o', 'm_norm_ffn_g', 'm_w_gate_up', 'm_w_down', 'm_norm_final_g', 'v_meta_tokens', 'v_norm_mix_g', 'v_w_in', 'v_b_gate', 'v_pool_w', 'v_pool_scale', 'v_conv_w', 'v_conv_out_w', 'v_w_o', 'v_norm_ffn_g', 'v_w_gate_up', 'v_w_down', 'v_norm_final_g']
TWIN_OUTPUTS = ['loss', 'grad_x', 'grad_meta_tokens', 'grad_norm_mix_g', 'grad_w_in', 'grad_b_gate', 'grad_pool_w', 'grad_pool_scale', 'grad_conv_w', 'grad_conv_out_w', 'grad_w_o', 'grad_norm_ffn_g', 'grad_w_gate_up', 'grad_w_down', 'grad_norm_final_g', 'delta_meta_tokens', 'delta_norm_mix_g', 'delta_w_in', 'delta_b_gate', 'delta_pool_w', 'delta_pool_scale', 'delta_conv_w', 'delta_conv_out_w', 'delta_w_o', 'delta_norm_ffn_g', 'delta_w_gate_up', 'delta_w_down', 'delta_norm_final_g', 'new_m_meta_tokens', 'new_m_norm_mix_g', 'new_m_w_in', 'new_m_b_gate', 'new_m_pool_w', 'new_m_pool_scale', 'new_m_conv_w', 'new_m_conv_out_w', 'new_m_w_o', 'new_m_norm_ffn_g', 'new_m_w_gate_up', 'new_m_w_down', 'new_m_norm_final_g', 'new_v_meta_tokens', 'new_v_norm_mix_g', 'new_v_w_in', 'new_v_b_gate', 'new_v_pool_w', 'new_v_pool_scale', 'new_v_conv_w', 'new_v_conv_out_w', 'new_v_w_o', 'new_v_norm_ffn_g', 'new_v_w_gate_up', 'new_v_w_down', 'new_v_norm_final_g']
TWIN_LEAF_KINDS = {'loss': 'loss', 'grad_x': 'grad_x', 'grad_meta_tokens': 'grad_w', 'grad_norm_mix_g': 'grad_w', 'grad_w_in': 'grad_w', 'grad_b_gate': 'grad_w', 'grad_pool_w': 'grad_w', 'grad_pool_scale': 'grad_w', 'grad_conv_w': 'grad_w', 'grad_conv_out_w': 'grad_w', 'grad_w_o': 'grad_w', 'grad_norm_ffn_g': 'grad_w', 'grad_w_gate_up': 'grad_w', 'grad_w_down': 'grad_w', 'grad_norm_final_g': 'grad_w', 'delta_meta_tokens': 'delta_w', 'delta_norm_mix_g': 'delta_w', 'delta_w_in': 'delta_w', 'delta_b_gate': 'delta_w', 'delta_pool_w': 'delta_w', 'delta_pool_scale': 'delta_w', 'delta_conv_w': 'delta_w', 'delta_conv_out_w': 'delta_w', 'delta_w_o': 'delta_w', 'delta_norm_ffn_g': 'delta_w', 'delta_w_gate_up': 'delta_w', 'delta_w_down': 'delta_w', 'delta_norm_final_g': 'delta_w', 'new_m_meta_tokens': 'new_m', 'new_m_norm_mix_g': 'new_m', 'new_m_w_in': 'new_m', 'new_m_b_gate': 'new_m', 'new_m_pool_w': 'new_m', 'new_m_pool_scale': 'new_m', 'new_m_conv_w': 'new_m', 'new_m_conv_out_w': 'new_m', 'new_m_w_o': 'new_m', 'new_m_norm_ffn_g': 'new_m', 'new_m_w_gate_up': 'new_m', 'new_m_w_down': 'new_m', 'new_m_norm_final_g': 'new_m', 'new_v_meta_tokens': 'new_v', 'new_v_norm_mix_g': 'new_v', 'new_v_w_in': 'new_v', 'new_v_b_gate': 'new_v', 'new_v_pool_w': 'new_v', 'new_v_pool_scale': 'new_v', 'new_v_conv_w': 'new_v', 'new_v_conv_out_w': 'new_v', 'new_v_w_o': 'new_v', 'new_v_norm_ffn_g': 'new_v', 'new_v_w_gate_up': 'new_v', 'new_v_w_down': 'new_v', 'new_v_norm_final_g': 'new_v'}


def _forward(args):
    return _fwd_reference(*[args[k] for k in FWD_PARAMS])


def _output_shape():
    out = _jax.eval_shape(lambda: _forward(_fwd_setup_inputs(0)))
    return out.shape, out.dtype

N_MICROBATCH = 1
ADAM_LR = 0.001
ADAM_B1 = 0.9
ADAM_B2 = 0.999
ADAM_EPS = 1e-08
ADAM_WD = 0.01
ADAM_STEP = 10
PER_EXAMPLE_BATCH_AXIS = {'x': 0, 'loss_target': 0}
SHARED_INPUTS = []
_WEIGHT_DTYPES = {'meta_tokens': _jnp.float32, 'norm_mix_g': _jnp.float32, 'w_in': _jnp.float32, 'b_gate': _jnp.float32, 'pool_w': _jnp.float32, 'pool_scale': _jnp.float32, 'conv_w': _jnp.float32, 'conv_out_w': _jnp.float32, 'w_o': _jnp.float32, 'norm_ffn_g': _jnp.float32, 'w_gate_up': _jnp.float32, 'w_down': _jnp.float32, 'norm_final_g': _jnp.float32}
MOMENT_SCALE = {'meta_tokens': 6.923907e-04, 'norm_mix_g': 6.059019e-02, 'w_in': 2.468618e-02, 'b_gate': 1.134938e-02, 'pool_w': 2.649638e-02, 'pool_scale': 2.756682e-02, 'conv_w': 3.034690e-02, 'conv_out_w': 3.014702e-02, 'w_o': 4.002632e-02, 'norm_ffn_g': 3.999320e-02, 'w_gate_up': 1.675879e-02, 'w_down': 2.735356e-02, 'norm_final_g': 8.001079e+00}


def _to_microbatches(a, axis):
    t = _jnp.moveaxis(a, axis, 0)
    t = t.reshape((N_MICROBATCH, t.shape[0] // N_MICROBATCH) + t.shape[1:])
    return _jnp.moveaxis(t, 1, axis + 1)


def setup_inputs(seed: int = 0) -> dict:
    inp = _fwd_setup_inputs(seed)
    key = _jax.random.fold_in(_jax.random.key(seed), 7919)
    shape, _ = _output_shape()
    out = dict(inp)
    out["loss_target"] = _jax.random.normal(_jax.random.fold_in(key, 0), shape, _jnp.float32)
    for i, name in enumerate(TWIN_WEIGHTS):
        w = inp[name].astype(_jnp.float32)
        if MOMENT_SCALE is None:
            s = _jnp.sqrt(_jnp.mean(_jnp.square(w)) + 1e-30)
        else:
            s = MOMENT_SCALE[name]
        km, kv = _jax.random.split(_jax.random.fold_in(key, i + 1))
        out[name] = w
        out["m_" + name] = s * _jax.random.normal(km, w.shape, _jnp.float32)
        out["v_" + name] = (s * s) * _jax.random.uniform(kv, w.shape, _jnp.float32, 0.5, 1.5)
    if N_MICROBATCH > 1:
        for name, axis in PER_EXAMPLE_BATCH_AXIS.items():
            out[name] = _to_microbatches(out[name], axis)
    return {'x': out['x'], 'meta_tokens': out['meta_tokens'], 'norm_mix_g': out['norm_mix_g'], 'w_in': out['w_in'], 'b_gate': out['b_gate'], 'pool_w': out['pool_w'], 'pool_scale': out['pool_scale'], 'conv_w': out['conv_w'], 'conv_out_w': out['conv_out_w'], 'w_o': out['w_o'], 'norm_ffn_g': out['norm_ffn_g'], 'w_gate_up': out['w_gate_up'], 'w_down': out['w_down'], 'norm_final_g': out['norm_final_g'], 'loss_target': out['loss_target'], 'm_meta_tokens': out['m_meta_tokens'], 'm_norm_mix_g': out['m_norm_mix_g'], 'm_w_in': out['m_w_in'], 'm_b_gate': out['m_b_gate'], 'm_pool_w': out['m_pool_w'], 'm_pool_scale': out['m_pool_scale'], 'm_conv_w': out['m_conv_w'], 'm_conv_out_w': out['m_conv_out_w'], 'm_w_o': out['m_w_o'], 'm_norm_ffn_g': out['m_norm_ffn_g'], 'm_w_gate_up': out['m_w_gate_up'], 'm_w_down': out['m_w_down'], 'm_norm_final_g': out['m_norm_final_g'], 'v_meta_tokens': out['v_meta_tokens'], 'v_norm_mix_g': out['v_norm_mix_g'], 'v_w_in': out['v_w_in'], 'v_b_gate': out['v_b_gate'], 'v_pool_w': out['v_pool_w'], 'v_pool_scale': out['v_pool_scale'], 'v_conv_w': out['v_conv_w'], 'v_conv_out_w': out['v_conv_out_w'], 'v_w_o': out['v_w_o'], 'v_norm_ffn_g': out['v_norm_ffn_g'], 'v_w_gate_up': out['v_w_gate_up'], 'v_w_down': out['v_w_down'], 'v_norm_final_g': out['v_norm_final_g']}


def _loss(weights, diff, rest, loss_target):
    with _jax.named_scope("forward"):
        args = {**rest, TWIN_DIFF_INPUT: diff, **{k: w.astype(_WEIGHT_DTYPES[k]) for k, w in weights.items()}}
        y = _forward(args)
    with _jax.named_scope("loss_head"):
        err = _jnp.square(y.astype(_jnp.float32) - loss_target)
        return 0.5 * _jnp.sum(_jnp.mean(err, axis=-1)) if err.ndim else 0.5 * err


def _adamw(w, g, m, v):
    m = ADAM_B1 * m + (1.0 - ADAM_B1) * g
    v = ADAM_B2 * v + (1.0 - ADAM_B2) * _jnp.square(g)
    m_hat = m / (1.0 - ADAM_B1 ** ADAM_STEP)
    v_hat = v / (1.0 - ADAM_B2 ** ADAM_STEP)
    delta = -ADAM_LR * (m_hat / (_jnp.sqrt(v_hat) + ADAM_EPS) + ADAM_WD * w)
    return delta, m, v


def reference(x, meta_tokens, norm_mix_g, w_in, b_gate, pool_w, pool_scale, conv_w, conv_out_w, w_o, norm_ffn_g, w_gate_up, w_down, norm_final_g, loss_target, m_meta_tokens, m_norm_mix_g, m_w_in, m_b_gate, m_pool_w, m_pool_scale, m_conv_w, m_conv_out_w, m_w_o, m_norm_ffn_g, m_w_gate_up, m_w_down, m_norm_final_g, v_meta_tokens, v_norm_mix_g, v_w_in, v_b_gate, v_pool_w, v_pool_scale, v_conv_w, v_conv_out_w, v_w_o, v_norm_ffn_g, v_w_gate_up, v_w_down, v_norm_final_g):
    given = dict(x=x, meta_tokens=meta_tokens, norm_mix_g=norm_mix_g, w_in=w_in, b_gate=b_gate, pool_w=pool_w, pool_scale=pool_scale, conv_w=conv_w, conv_out_w=conv_out_w, w_o=w_o, norm_ffn_g=norm_ffn_g, w_gate_up=w_gate_up, w_down=w_down, norm_final_g=norm_final_g, loss_target=loss_target, m_meta_tokens=m_meta_tokens, m_norm_mix_g=m_norm_mix_g, m_w_in=m_w_in, m_b_gate=m_b_gate, m_pool_w=m_pool_w, m_pool_scale=m_pool_scale, m_conv_w=m_conv_w, m_conv_out_w=m_conv_out_w, m_w_o=m_w_o, m_norm_ffn_g=m_norm_ffn_g, m_w_gate_up=m_w_gate_up, m_w_down=m_w_down, m_norm_final_g=m_norm_final_g, v_meta_tokens=v_meta_tokens, v_norm_mix_g=v_norm_mix_g, v_w_in=v_w_in, v_b_gate=v_b_gate, v_pool_w=v_pool_w, v_pool_scale=v_pool_scale, v_conv_w=v_conv_w, v_conv_out_w=v_conv_out_w, v_w_o=v_w_o, v_norm_ffn_g=v_norm_ffn_g, v_w_gate_up=v_w_gate_up, v_w_down=v_w_down, v_norm_final_g=v_norm_final_g)
    weights = {n: given[n] for n in TWIN_WEIGHTS}
    shared = {n: given[n] for n in SHARED_INPUTS}
    per_example = {n: given[n] for n in ['x']}
    grad_fn = _jax.value_and_grad(_loss, argnums=(0, 1))

    def one_microbatch(ex, loss_target):
        ex = dict(ex)
        diff = ex.pop(TWIN_DIFF_INPUT)
        return grad_fn(weights, diff, {**shared, **ex}, loss_target)

    if N_MICROBATCH == 1:
        loss, (grad_w, grad_x) = one_microbatch(per_example, given["loss_target"])
    else:
        def body(carry, xs):
            loss_sum, grad_sum = carry
            l_k, (gw_k, gx_k) = one_microbatch(xs[0], xs[1])
            with _jax.named_scope("update"):
                return (loss_sum + l_k, _jax.tree.map(_jnp.add, grad_sum, gw_k)), gx_k

        init = (_jnp.zeros((), _jnp.float32), _jax.tree.map(_jnp.zeros_like, weights))
        (loss, grad_w), grad_x = _jax.lax.scan(body, init, (per_example, given["loss_target"]))
    with _jax.named_scope("update"):
        delta_w, new_m, new_v = {}, {}, {}
        for n in TWIN_WEIGHTS:
            delta_w[n], new_m[n], new_v[n] = _adamw(weights[n], grad_w[n], given["m_" + n], given["v_" + n])
    return (loss, grad_x, *[grad_w[n] for n in TWIN_WEIGHTS], *[delta_w[n] for n in TWIN_WEIGHTS],
            *[new_m[n] for n in TWIN_WEIGHTS], *[new_v[n] for n in TWIN_WEIGHTS])
```

```python
import functools
import math

import jax
import jax.numpy as jnp
from jax import lax
from jax.experimental import pallas as pl
from jax.experimental.pallas import tpu as pltpu

F32 = jnp.float32
BF16 = jnp.bfloat16
N_META = 16
POOL_WINDOWS = (2, 4, 8, 16)
EPS = 1e-6
ADAM_LR, ADAM_B1, ADAM_B2, ADAM_EPS, ADAM_WD, ADAM_STEP = 0.001, 0.9, 0.999, 1e-08, 0.01, 10
LANES = 128
V7X_VMEM_BYTES = 64 * 1024 * 1024
VMEM_LIMIT = V7X_VMEM_BYTES - 8 * 1024 * 1024
MESH = pl.DeviceIdType.MESH
ANY = pl.BlockSpec(memory_space=pl.ANY)
CHIP_FLIPS = ((1, 0), (0, 1), (1, 1))
SMALL_ROWS = 8


def _pick(n, pref):
    best = None
    for t in range(LANES, min(n, pref) + 1, LANES):
        if n % t == 0:
            best = t
    assert best is not None, (n, pref)
    return best


def _params(n_axes=0):
    sem = ("arbitrary",) * n_axes if n_axes else None
    return pltpu.CompilerParams(dimension_semantics=sem, vmem_limit_bytes=VMEM_LIMIT)


_DIMS = {
    "nn": (((1,), (0,)), ((), ())),
    "nt": (((1,), (1,)), ((), ())),
    "tn": (((0,), (0,)), ((), ())),
}


def _matmul(name, mode, a, b, out_sds, grid, a_spec, b_spec, o_spec, nk, res=None, res_spec=None, acc_shape=None):
    out_dtype = out_sds.dtype
    in_place = nk > 1 and out_dtype == F32
    use_scratch = nk > 1 and not in_place
    rows = a_spec.block_shape[-2] if mode != "tn" else None
    chunk = _pick(rows, 768) if rows is not None and rows % LANES == 0 else rows

    def body(*refs):
        if res is not None:
            a_ref, b_ref, r_ref, o_ref, *scr = refs
        else:
            a_ref, b_ref, o_ref, *scr = refs
            r_ref = None
        k = pl.program_id(len(grid) - 1) if nk > 1 else None

        def emit(sl):
            if sl is None:
                part = lax.dot_general(a_ref[...], b_ref[...], _DIMS[mode], preferred_element_type=F32)
                idx = (slice(None), slice(None))
            else:
                part = lax.dot_general(a_ref[sl, :], b_ref[...], _DIMS[mode], preferred_element_type=F32)
                idx = (sl, slice(None))
            if nk == 1:
                if r_ref is not None:
                    part = part + r_ref[idx]
                o_ref[idx] = part.astype(out_dtype)
                return
            acc = scr[0] if use_scratch else o_ref

            @pl.when(k == 0)
            def _():
                first = part
                if r_ref is not None and in_place:
                    first = first + r_ref[idx]
                acc[idx] = first

            @pl.when(k > 0)
            def _():
                acc[idx] += part

            if use_scratch:

                @pl.when(k == nk - 1)
                def _():
                    o_ref[idx] = acc[idx].astype(out_dtype)

        if mode == "tn" or chunk == rows:
            emit(None)
        else:
            for m0 in range(0, rows, chunk):
                emit(pl.ds(m0, chunk))

    ins = [a, b] + ([res] if res is not None else [])
    in_specs = [a_spec, b_spec] + ([res_spec] if res is not None else [])
    scratch = [pltpu.VMEM(acc_shape, F32)] if use_scratch else []
    return pl.pallas_call(
        body, name=name, out_shape=out_sds, grid=grid, in_specs=in_specs, out_specs=o_spec,
        scratch_shapes=scratch, compiler_params=_params(len(grid)),
    )(*ins)


def _nn_sharded(name, a, w4, nseg):
    lp, kdim = a.shape
    s, _, nloc = w4.shape
    segw = s * nloc // nseg
    tn = _pick(math.gcd(nloc, segw), 1536)
    bw, bo = nloc // tn, segw // tn
    return _matmul(
        name, "nn", a, w4, jax.ShapeDtypeStruct((nseg, lp, segw), BF16), (s * bw,),
        pl.BlockSpec((lp, kdim), lambda j: (0, 0)),
        pl.BlockSpec((None, kdim, tn), lambda j: (j // bw, 0, j % bw)),
        pl.BlockSpec((None, lp, tn), lambda j: (j // bo, 0, j % bo)), 1)


def _nn_plain(name, a, w, out_dtype, res=None, tn_pref=512, tk_pref=2048):
    lp, kdim = a.shape
    n = w.shape[1]
    tn = _pick(n, tn_pref)
    tk = kdim if kdim <= tk_pref else _pick(kdim, tk_pref)
    nk = kdim // tk
    grid = (n // tn, nk) if nk > 1 else (n // tn,)
    if nk > 1:
        a_spec = pl.BlockSpec((lp, tk), lambda j, k: (0, k))
        w_spec = pl.BlockSpec((tk, tn), lambda j, k: (k, j))
        o_spec = pl.BlockSpec((lp, tn), lambda j, k: (0, j))
    else:
        a_spec = pl.BlockSpec((lp, tk), lambda j: (0, 0))
        w_spec = pl.BlockSpec((tk, tn), lambda j: (0, j))
        o_spec = pl.BlockSpec((lp, tn), lambda j: (0, j))
    return _matmul(name, "nn", a, w, jax.ShapeDtypeStruct((lp, n), out_dtype), grid, a_spec, w_spec, o_spec, nk,
                   res=res, res_spec=o_spec if res is not None else None, acc_shape=(lp, tn))


def _nt_plain(name, a, w, tn_pref=512):
    lp, kdim = a.shape
    n = w.shape[0]
    tn = _pick(n, tn_pref)
    return _matmul(
        name, "nt", a, w, jax.ShapeDtypeStruct((lp, n), BF16), (n // tn,),
        pl.BlockSpec((lp, kdim), lambda j: (0, 0)),
        pl.BlockSpec((tn, kdim), lambda j: (j, 0)),
        pl.BlockSpec((lp, tn), lambda j: (0, j)), 1)


def _nt_sharded(name, dseg, w4, to_pref=1024):
    nseg, lp, segw = dseg.shape
    s, kdim, nloc = w4.shape
    tr = _pick(math.gcd(nloc, segw), 1536)
    ba, bw = segw // tr, nloc // tr
    nr = s * bw
    to = _pick(kdim, to_pref)
    return _matmul(
        name, "nt", dseg, w4, jax.ShapeDtypeStruct((lp, kdim), F32), (kdim // to, nr),
        pl.BlockSpec((None, lp, tr), lambda j, r: (r // ba, 0, r % ba)),
        pl.BlockSpec((None, to, tr), lambda j, r: (r // bw, j, r % bw)),
        pl.BlockSpec((lp, to), lambda j, r: (0, j)), nr)


def _tn_plain(name, a, d, tk_pref=512):
    lp, kdim = a.shape
    n = d.shape[1]
    tk = _pick(kdim, tk_pref)
    return _matmul(
        name, "tn", a, d, jax.ShapeDtypeStruct((kdim, n), BF16), (kdim // tk,),
        pl.BlockSpec((lp, tk), lambda i: (0, i)),
        pl.BlockSpec((lp, n), lambda i: (0, 0)),
        pl.BlockSpec((tk, n), lambda i: (i, 0)), 1)


def _tn_sharded(name, a, dseg, s, tk_pref=512):
    lp, kdim = a.shape
    nseg, _, segw = dseg.shape
    nloc = nseg * segw // s
    tn = _pick(math.gcd(nloc, segw), 1536)
    bd, bo = segw // tn, nloc // tn
    tk = _pick(kdim, tk_pref)
    return _matmul(
        name, "tn", a, dseg, jax.ShapeDtypeStruct((s, kdim, nloc), BF16), (s * bo, kdim // tk),
        pl.BlockSpec((lp, tk), lambda j, i: (0, i)),
        pl.BlockSpec((None, lp, tn), lambda j, i: (j // bd, 0, j % bd)),
        pl.BlockSpec((None, tk, tn), lambda j, i: (j // bo, i, j % bo)), 1)


def _pool_fwd(name, pooled, pw4):
    lp, dm = pooled.shape
    s, g, rs, gw = pw4.shape
    return _matmul(
        name, "nn", pooled, pw4, jax.ShapeDtypeStruct((lp, dm), BF16), (g, s),
        pl.BlockSpec((lp, rs), lambda gi, si: (0, gi * s + si)),
        pl.BlockSpec((None, None, rs, gw), lambda gi, si: (si, gi, 0, 0)),
        pl.BlockSpec((lp, gw), lambda gi, si: (0, gi)), s, acc_shape=(lp, gw))


def _pool_bwd_act(name, dya, pw4):
    lp, dm = dya.shape
    s, g, rs, gw = pw4.shape
    return _matmul(
        name, "nt", dya, pw4, jax.ShapeDtypeStruct((lp, dm), BF16), (g, s),
        pl.BlockSpec((lp, gw), lambda gi, si: (0, gi)),
        pl.BlockSpec((None, None, rs, gw), lambda gi, si: (si, gi, 0, 0)),
        pl.BlockSpec((lp, rs), lambda gi, si: (0, gi * s + si)), 1)


def _pool_bwd_w(name, pooled, dya, s):
    lp, dm = pooled.shape
    g = len(POOL_WINDOWS)
    gw = dm // g
    rs = gw // s
    return _matmul(
        name, "tn", pooled, dya, jax.ShapeDtypeStruct((s, g, rs, gw), BF16), (g, s),
        pl.BlockSpec((lp, rs), lambda gi, si: (0, gi * s + si)),
        pl.BlockSpec((lp, gw), lambda gi, si: (0, gi)),
        pl.BlockSpec((None, None, rs, gw), lambda gi, si: (si, gi, 0, 0)), 1)


def _rms_fwd(name, h, g, tm):
    lp, dm = h.shape

    def body(h_ref, g_ref, o_ref):
        hv = h_ref[...]
        r = lax.rsqrt(jnp.mean(hv * hv, axis=-1, keepdims=True) + EPS)
        o_ref[...] = (hv * r * g_ref[...]).astype(BF16)

    row = pl.BlockSpec((tm, dm), lambda i: (i, 0))
    return pl.pallas_call(
        body, name=name, out_shape=jax.ShapeDtypeStruct((lp, dm), BF16), grid=(lp // tm,),
        in_specs=[row, pl.BlockSpec((1, dm), lambda i: (0, 0))], out_specs=row, compiler_params=_params(1),
    )(h, g)


def _rms_bwd(name, dy, h, g, dres, tm):
    lp, dm = h.shape

    def body(dy_ref, h_ref, g_ref, dr_ref, dh_ref, dhb_ref, dg_ref):
        hv = h_ref[...]
        r = lax.rsqrt(jnp.mean(hv * hv, axis=-1, keepdims=True) + EPS)
        xhat = hv * r
        dyv = dy_ref[...]
        dxh = dyv * g_ref[...]
        dh = dr_ref[...] + r * (dxh - xhat * jnp.mean(dxh * xhat, axis=-1, keepdims=True))
        dh_ref[...] = dh
        dhb_ref[...] = dh.astype(BF16)

        @pl.when(pl.program_id(0) == 0)
        def _():
            dg_ref[...] = jnp.zeros_like(dg_ref)

        dg_ref[0:1, :] += jnp.sum(dyv * xhat, axis=0, keepdims=True)

    row = pl.BlockSpec((tm, dm), lambda i: (i, 0))
    slab = pl.BlockSpec((SMALL_ROWS, dm), lambda i: (0, 0))
    return pl.pallas_call(
        body, name=name, grid=(lp // tm,),
        out_shape=(jax.ShapeDtypeStruct((lp, dm), F32), jax.ShapeDtypeStruct((lp, dm), BF16),
                   jax.ShapeDtypeStruct((SMALL_ROWS, dm), F32)),
        in_specs=[row, row, pl.BlockSpec((1, dm), lambda i: (0, 0)), row], out_specs=(row, row, slab),
        compiler_params=_params(1),
    )(dy, h, g, dres)


def _gate_mix(name, proj, b_gate2, ya, pool_scale, yb, tm):
    _, lp, dm = proj.shape

    def body(ga_ref, gr_ref, b_ref, ya_ref, ps_ref, yb_ref, o_ref):
        g_a = jax.nn.sigmoid(ga_ref[...].astype(F32) + b_ref[0:1, :])
        g_b = jax.nn.sigmoid(gr_ref[...].astype(F32) + b_ref[1:2, :])
        y_a = ya_ref[...].astype(F32) * ps_ref[...]
        o_ref[...] = (g_a * y_a + g_b * yb_ref[...].astype(F32)).astype(BF16)

    row = pl.BlockSpec((tm, dm), lambda i: (i, 0))
    return pl.pallas_call(
        body, name=name, out_shape=jax.ShapeDtypeStruct((lp, dm), BF16), grid=(lp // tm,),
        in_specs=[pl.BlockSpec((None, tm, dm), lambda i: (4, i, 0)), pl.BlockSpec((None, tm, dm), lambda i: (5, i, 0)),
                  pl.BlockSpec((2, dm), lambda i: (0, 0)), row, pl.BlockSpec((1, dm), lambda i: (0, 0)), row],
        out_specs=row, compiler_params=_params(1),
    )(proj, proj, b_gate2, ya, pool_scale, yb)


def _gate_bwd(name, dmix, proj, b_gate2, ya, pool_scale, yb, tm):
    _, lp, dm = proj.shape

    def body(dm_ref, ga_ref, gr_ref, b_ref, ya_ref, ps_ref, yb_ref, dp_ref, dyb_ref, dya_ref, db_ref, dps_ref):
        dmx = dm_ref[...].astype(F32)
        g_a = jax.nn.sigmoid(ga_ref[...].astype(F32) + b_ref[0:1, :])
        g_b = jax.nn.sigmoid(gr_ref[...].astype(F32) + b_ref[1:2, :])
        ya_pre = ya_ref[...].astype(F32)
        ybv = yb_ref[...].astype(F32)
        ps = ps_ref[...]
        dga = dmx * (ya_pre * ps) * (g_a * (1.0 - g_a))
        dgr = dmx * ybv * (g_b * (1.0 - g_b))
        dp_ref[0] = dga.astype(BF16)
        dp_ref[1] = dgr.astype(BF16)
        dyb_ref[...] = (dmx * g_b).astype(BF16)
        dya_ref[...] = (dmx * g_a * ps).astype(BF16)

        @pl.when(pl.program_id(0) == 0)
        def _():
            db_ref[...] = jnp.zeros_like(db_ref)
            dps_ref[...] = jnp.zeros_like(dps_ref)

        db_ref[0:1, :] += jnp.sum(dga, axis=0, keepdims=True)
        db_ref[1:2, :] += jnp.sum(dgr, axis=0, keepdims=True)
        dps_ref[0:1, :] += jnp.sum(dmx * g_a * ya_pre, axis=0, keepdims=True)

    row = pl.BlockSpec((tm, dm), lambda i: (i, 0))
    one = pl.BlockSpec((1, dm), lambda i: (0, 0))
    slab = pl.BlockSpec((SMALL_ROWS, dm), lambda i: (0, 0))
    return pl.pallas_call(
        body, name=name, grid=(lp // tm,),
        out_shape=(jax.ShapeDtypeStruct((6, lp, dm), BF16), jax.ShapeDtypeStruct((lp, dm), BF16),
                   jax.ShapeDtypeStruct((lp, dm), BF16), jax.ShapeDtypeStruct((SMALL_ROWS, dm), F32),
                   jax.ShapeDtypeStruct((SMALL_ROWS, dm), F32)),
        in_specs=[row, pl.BlockSpec((None, tm, dm), lambda i: (4, i, 0)), pl.BlockSpec((None, tm, dm), lambda i: (5, i, 0)),
                  pl.BlockSpec((2, dm), lambda i: (0, 0)), row, one, row],
        out_specs=(pl.BlockSpec((2, tm, dm), lambda i: (2, i, 0)), row, row, slab, slab),
        compiler_params=_params(1),
    )(dmix, proj, proj, b_gate2, ya, pool_scale, yb)


def _swiglu_fwd(name, gu, tm):
    _, lp, f = gu.shape

    def body(g_ref, u_ref, o_ref):
        gt = g_ref[...].astype(F32)
        o_ref[...] = (gt * jax.nn.sigmoid(gt) * u_ref[...].astype(F32)).astype(BF16)

    return pl.pallas_call(
        body, name=name, out_shape=jax.ShapeDtypeStruct((lp, f), BF16), grid=(lp // tm,),
        in_specs=[pl.BlockSpec((None, tm, f), lambda i: (0, i, 0)), pl.BlockSpec((None, tm, f), lambda i: (1, i, 0))],
        out_specs=pl.BlockSpec((tm, f), lambda i: (i, 0)), compiler_params=_params(1),
    )(gu, gu)


def _swiglu_bwd(name, dact, gu, tm):
    _, lp, f = gu.shape

    def body(d_ref, g_ref, u_ref, o_ref):
        d = d_ref[...].astype(F32)
        gt = g_ref[...].astype(F32)
        sg = jax.nn.sigmoid(gt)
        o_ref[0] = (d * u_ref[...].astype(F32) * (sg * (1.0 + gt * (1.0 - sg)))).astype(BF16)
        o_ref[1] = (d * (gt * sg)).astype(BF16)

    return pl.pallas_call(
        body, name=name, out_shape=jax.ShapeDtypeStruct((2, lp, f), BF16), grid=(lp // tm,),
        in_specs=[pl.BlockSpec((tm, f), lambda i: (i, 0)), pl.BlockSpec((None, tm, f), lambda i: (0, i, 0)),
                  pl.BlockSpec((None, tm, f), lambda i: (1, i, 0))],
        out_specs=pl.BlockSpec((2, tm, f), lambda i: (0, i, 0)), compiler_params=_params(1),
    )(dact, gu, gu)


def _final_loss(name, h2, g3, target, tm):
    lp, dm = h2.shape
    nx = target.shape[0] // tm

    def body(h_ref, g_ref, t_ref, dh_ref, dhb_ref, ls_ref, dg_ref):
        i = pl.program_id(0)

        @pl.when(i == 0)
        def _():
            ls_ref[...] = jnp.zeros_like(ls_ref)
            dg_ref[...] = jnp.zeros_like(dg_ref)

        @pl.when(i < nx)
        def _():
            hv = h_ref[...]
            gv = g_ref[...]
            r = lax.rsqrt(jnp.mean(hv * hv, axis=-1, keepdims=True) + EPS)
            xhat = hv * r
            err = xhat * gv - t_ref[...]
            dout = err * (1.0 / dm)
            dxh = dout * gv
            dh = r * (dxh - xhat * jnp.mean(dxh * xhat, axis=-1, keepdims=True))
            dh_ref[...] = dh
            dhb_ref[...] = dh.astype(BF16)
            ls_ref[0:1, :] += jnp.sum(err * err, axis=0, keepdims=True)
            dg_ref[0:1, :] += jnp.sum(dout * xhat, axis=0, keepdims=True)

        @pl.when(i >= nx)
        def _():
            dh_ref[...] = jnp.zeros_like(dh_ref)
            dhb_ref[...] = jnp.zeros_like(dhb_ref)

    row = pl.BlockSpec((tm, dm), lambda i: (i, 0))
    slab = pl.BlockSpec((SMALL_ROWS, dm), lambda i: (0, 0))
    return pl.pallas_call(
        body, name=name, grid=(lp // tm,),
        out_shape=(jax.ShapeDtypeStruct((lp, dm), F32), jax.ShapeDtypeStruct((lp, dm), BF16),
                   jax.ShapeDtypeStruct((SMALL_ROWS, dm), F32), jax.ShapeDtypeStruct((SMALL_ROWS, dm), F32)),
        in_specs=[row, pl.BlockSpec((1, dm), lambda i: (0, 0)), pl.BlockSpec((tm, dm), lambda i: (jnp.minimum(i, nx - 1), 0))],
        out_specs=(row, row, slab, slab), compiler_params=_params(1),
    )(h2, g3, target)


def _shift(v, k):
    return pltpu.roll(v, k % v.shape[0], axis=0)


def _window_sum(v, group, sign):
    s2 = v + _shift(v, sign * 1)
    s4 = s2 + _shift(s2, sign * 2)
    s8 = s4 + _shift(s4, sign * 4)
    s16 = s8 + _shift(s8, sign * 8)
    return jnp.where(group == 0, s2, jnp.where(group == 1, s4, jnp.where(group == 2, s8, s16)))


def _pool_count(lp, group):
    row = lax.broadcasted_iota(jnp.int32, (lp, 1), 0)
    window = jnp.left_shift(2, group).astype(F32)
    meta_pos = (row - (lp - N_META) + 1).astype(F32)
    return jnp.where(row >= lp - N_META, jnp.minimum(meta_pos, window), window)


def _mixer_fwd(name, proj, conv_w, tc):
    _, lp, dm = proj.shape
    per_group = dm // len(POOL_WINDOWS) // tc

    def body(u_ref, gb_ref, gc_ref, v_ref, cw_ref, p_ref, z_ref):
        group = pl.program_id(0) // per_group
        u = u_ref[...].astype(F32)
        p_ref[...] = (_window_sum(u, group, 1) / _pool_count(lp, group) - u).astype(BF16)
        cv = gc_ref[...].astype(F32) * v_ref[...].astype(F32)
        conv = cw_ref[0:1, :] * _shift(cv, 2) + cw_ref[1:2, :] * _shift(cv, 1) + cw_ref[2:3, :] * cv
        z_ref[...] = (gb_ref[...].astype(F32) * conv).astype(BF16)

    def seg(s):
        return pl.BlockSpec((None, lp, tc), lambda j: (s, 0, j))

    col = pl.BlockSpec((lp, tc), lambda j: (0, j))
    return pl.pallas_call(
        body, name=name, grid=(dm // tc,),
        out_shape=(jax.ShapeDtypeStruct((lp, dm), BF16), jax.ShapeDtypeStruct((lp, dm), BF16)),
        in_specs=[seg(0), seg(1), seg(2), seg(3), pl.BlockSpec((3, tc), lambda j: (0, j))],
        out_specs=(col, col), compiler_params=_params(1),
    )(proj, proj, proj, proj, conv_w)


def _mixer_bwd(name, dz, dpooled, proj, conv_w, dproj, tc):
    _, lp, dm = proj.shape
    per_group = dm // len(POOL_WINDOWS) // tc

    def body(dz_ref, dp_ref, gb_ref, gc_ref, v_ref, cw_ref, _, o_ref, dcw_ref):
        group = pl.program_id(0) // per_group
        dzv = dz_ref[...].astype(F32)
        gb = gb_ref[...].astype(F32)
        gc = gc_ref[...].astype(F32)
        vv = v_ref[...].astype(F32)
        cv = gc * vv
        c1 = _shift(cv, 1)
        c2 = _shift(cv, 2)
        w0, w1, w2 = cw_ref[0:1, :], cw_ref[1:2, :], cw_ref[2:3, :]
        o_ref[1] = (dzv * (w0 * c2 + w1 * c1 + w2 * cv)).astype(BF16)
        dconv = dzv * gb
        dcw_ref[...] = jnp.zeros_like(dcw_ref)
        dcw_ref[0:1, :] = jnp.sum(dconv * c2, axis=0, keepdims=True)
        dcw_ref[1:2, :] = jnp.sum(dconv * c1, axis=0, keepdims=True)
        dcw_ref[2:3, :] = jnp.sum(dconv * cv, axis=0, keepdims=True)
        dcv = w0 * _shift(dconv, -2) + w1 * _shift(dconv, -1) + w2 * dconv
        o_ref[2] = (dcv * vv).astype(BF16)
        o_ref[3] = (dcv * gc).astype(BF16)
        dpv = dp_ref[...].astype(F32)
        o_ref[0] = (_window_sum(dpv / _pool_count(lp, group), group, -1) - dpv).astype(BF16)

    def seg(s):
        return pl.BlockSpec((None, lp, tc), lambda j: (s, 0, j))

    col = pl.BlockSpec((lp, tc), lambda j: (0, j))
    return pl.pallas_call(
        body, name=name, grid=(dm // tc,),
        out_shape=(jax.ShapeDtypeStruct(dproj.shape, BF16), jax.ShapeDtypeStruct((SMALL_ROWS, dm), F32)),
        in_specs=[col, col, seg(1), seg(2), seg(3), pl.BlockSpec((3, tc), lambda j: (0, j)), ANY],
        out_specs=(pl.BlockSpec((4, lp, tc), lambda j: (0, 0, j)), pl.BlockSpec((SMALL_ROWS, tc), lambda j: (0, j))),
        input_output_aliases={6: 0}, compiler_params=_params(1),
    )(dz, dpooled, proj, proj, proj, conv_w, dproj)


def _row_tile(r, c, bytes_per_row_elem=4, budget=2 * 1024 * 1024):
    best = None
    for t in range(16, r + 1, 16):
        if r % t == 0 and t * c * bytes_per_row_elem <= budget:
            best = t
    return best if best is not None else r


def _pair_add(name, g4, recv, core):
    s, r, c = g4.shape
    h = r // 2
    tr = _row_tile(h, c)
    nb = h // tr

    def body(core_ref, g_ref, r_ref, o_ref):
        o_ref[...] = (g_ref[...].astype(F32) + r_ref[...].astype(F32)).astype(BF16)

    grid_spec = pltpu.PrefetchScalarGridSpec(
        num_scalar_prefetch=1, grid=(s, nb),
        in_specs=[pl.BlockSpec((None, tr, c), lambda si, j, core_ref: (si, core_ref[0] * nb + j, 0)),
                  pl.BlockSpec((None, tr, c), lambda si, j, core_ref: (si, j, 0))],
        out_specs=pl.BlockSpec((None, tr, c), lambda si, j, core_ref: (si, j, 0)))
    return pl.pallas_call(
        body, name=name, out_shape=jax.ShapeDtypeStruct((s, h, c), BF16), grid_spec=grid_spec,
        compiler_params=_params(2),
    )(core, g4, recv)


def _chip_sum(name, parts):
    s, h, c = parts.shape
    tr = _row_tile(h, c)

    def body(p_ref, o_ref):
        acc = p_ref[0].astype(F32)
        for i in range(1, s):
            acc = acc + p_ref[i].astype(F32)
        o_ref[...] = acc

    return pl.pallas_call(
        body, name=name, out_shape=jax.ShapeDtypeStruct((h, c), F32), grid=(h // tr,),
        in_specs=[pl.BlockSpec((s, tr, c), lambda j: (0, j, 0))], out_specs=pl.BlockSpec((tr, c), lambda j: (j, 0)),
        compiler_params=_params(1),
    )(parts)


def _adamw(name, w, g, m, v):
    r, c = w.shape
    tr = _row_tile(r, c, budget=1024 * 1024)
    c1 = 1.0 - ADAM_B1 ** ADAM_STEP
    c2 = 1.0 - ADAM_B2 ** ADAM_STEP

    def body(w_ref, g_ref, m_ref, v_ref, d_ref, nm_ref, nv_ref):
        gv = g_ref[...]
        nm = ADAM_B1 * m_ref[...] + (1.0 - ADAM_B1) * gv
        nv = ADAM_B2 * v_ref[...] + (1.0 - ADAM_B2) * (gv * gv)
        nm_ref[...] = nm
        nv_ref[...] = nv
        d_ref[...] = -ADAM_LR * ((nm / c1) / (jnp.sqrt(nv / c2) + ADAM_EPS) + ADAM_WD * w_ref[...])

    blk = pl.BlockSpec((tr, c), lambda j: (j, 0))
    sds = jax.ShapeDtypeStruct((r, c), F32)
    return pl.pallas_call(
        body, name=name, out_shape=(sds, sds, sds), grid=(r // tr,), in_specs=[blk] * 4, out_specs=(blk,) * 3,
        compiler_params=_params(1),
    )(w, g, m, v)


def _to_bf16(name, w):
    r, c = w.shape
    tr = _row_tile(r, c)

    def body(w_ref, o_ref):
        o_ref[...] = w_ref[...].astype(BF16)

    blk = pl.BlockSpec((tr, c), lambda j: (j, 0))
    return pl.pallas_call(
        body, name=name, out_shape=jax.ShapeDtypeStruct((r, c), BF16), grid=(r // tr,), in_specs=[blk], out_specs=blk,
        compiler_params=_params(1),
    )(w)


def _place():
    return lax.axis_index("x"), lax.axis_index("y"), lax.axis_index("c")


def _chip_of(x, y, flip):
    px, py = x ^ flip[0], y ^ flip[1]
    return px, py, 2 * px + py


def _half(ref, which):
    rows = ref.shape[0] // 2
    return ref.at[pl.ds(which * rows, rows)]


def _all_gather(locals_):
    n = len(locals_)

    def body(*refs):
        ins, outs = refs[:n], refs[n:2 * n]
        lsem, ssem, rsem = refs[2 * n:]
        x, y, c = _place()
        k = 2 * x + y
        mine = [pltpu.make_async_copy(ins[a], outs[a].at[k], lsem.at[a]) for a in range(n)]
        for cp in mine:
            cp.start()

        def copy(a, slot, src, dst, to):
            return pltpu.make_async_remote_copy(src_ref=src, dst_ref=dst, send_sem=ssem.at[a * 6 + slot],
                                                recv_sem=rsem.at[a * 6 + slot], device_id=to, device_id_type=MESH)

        sends = []
        for a in range(n):
            for j, flip in enumerate(CHIP_FLIPS):
                px, py, _ = _chip_of(x, y, flip)
                sends.append(copy(a, j, _half(ins[a], c), _half(outs[a].at[k], c), (px, py, c)))
        for cp in sends:
            cp.start()
        for a in range(n):
            for j, flip in enumerate(CHIP_FLIPS):
                _, _, kj = _chip_of(x, y, flip)
                landed = _half(outs[a].at[kj], c)
                copy(a, j, landed, landed, (x, y, c)).wait_recv()
                passed = copy(a, 3 + j, landed, landed, (x, y, 1 - c))
                passed.start()
                sends.append(passed)
        for a in range(n):
            for j, flip in enumerate(CHIP_FLIPS):
                _, _, kj = _chip_of(x, y, flip)
                theirs = _half(outs[a].at[kj], 1 - c)
                copy(a, 3 + j, theirs, theirs, (x, y, c)).wait_recv()
        for cp in sends:
            cp.wait_send()
        for cp in mine:
            cp.wait()

    return pl.pallas_call(
        body, name="all_gather_weights",
        out_shape=tuple(jax.ShapeDtypeStruct((4,) + t.shape, t.dtype) for t in locals_),
        in_specs=[ANY] * n, out_specs=(ANY,) * n,
        scratch_shapes=[pltpu.SemaphoreType.DMA((n,)), pltpu.SemaphoreType.DMA((6 * n,)), pltpu.SemaphoreType.DMA((6 * n,))],
    )(*locals_)


def _sibling_swap(grads):
    n = len(grads)

    def body(*refs):
        ins, outs = refs[:n], refs[n:2 * n]
        ssem, rsem = refs[2 * n:]
        x, y, c = _place()
        cps = []
        for a in range(n):
            h = ins[a].shape[1] // 2
            cps.append(pltpu.make_async_remote_copy(
                src_ref=ins[a].at[:, pl.ds((1 - c) * h, h)], dst_ref=outs[a], send_sem=ssem.at[a], recv_sem=rsem.at[a],
                device_id=(x, y, 1 - c), device_id_type=MESH))
        for cp in cps:
            cp.start()
        for cp in cps:
            cp.wait()

    return pl.pallas_call(
        body, name="grad_sibling_swap",
        out_shape=tuple(jax.ShapeDtypeStruct((g.shape[0], g.shape[1] // 2, g.shape[2]), g.dtype) for g in grads),
        in_specs=[ANY] * n, out_specs=(ANY,) * n,
        scratch_shapes=[pltpu.SemaphoreType.DMA((n,)), pltpu.SemaphoreType.DMA((n,))],
    )(*grads)


def _chip_scatter(parts):
    n = len(parts)

    def body(*refs):
        ins, outs = refs[:n], refs[n:2 * n]
        lsem, ssem, rsem = refs[2 * n:]
        x, y, c = _place()
        k = 2 * x + y
        mine = [pltpu.make_async_copy(ins[a].at[k], outs[a].at[k], lsem.at[a]) for a in range(n)]
        for cp in mine:
            cp.start()
        sends = []
        for a in range(n):
            for j, flip in enumerate(CHIP_FLIPS):
                px, py, kj = _chip_of(x, y, flip)
                sends.append(pltpu.make_async_remote_copy(
                    src_ref=ins[a].at[kj], dst_ref=outs[a].at[k], send_sem=ssem.at[a * 3 + j], recv_sem=rsem.at[a * 3 + j],
                    device_id=(px, py, c), device_id_type=MESH))
        for cp in sends:
            cp.start()
        for a in range(n):
            for j, flip in enumerate(CHIP_FLIPS):
                _, _, kj = _chip_of(x, y, flip)
                pltpu.make_async_remote_copy(
                    src_ref=outs[a].at[kj], dst_ref=outs[a].at[kj], send_sem=ssem.at[a * 3 + j], recv_sem=rsem.at[a * 3 + j],
                    device_id=(x, y, c), device_id_type=MESH).wait_recv()
        for cp in sends:
            cp.wait_send()
        for cp in mine:
            cp.wait()

    return pl.pallas_call(
        body, name="grad_chip_scatter",
        out_shape=tuple(jax.ShapeDtypeStruct(p.shape, p.dtype) for p in parts),
        in_specs=[ANY] * n, out_specs=(ANY,) * n,
        scratch_shapes=[pltpu.SemaphoreType.DMA((n,)), pltpu.SemaphoreType.DMA((3 * n,)), pltpu.SemaphoreType.DMA((3 * n,))],
    )(*parts)


def _sibling_join(halves):
    n = len(halves)

    def body(*refs):
        ins, outs = refs[:n], refs[n:2 * n]
        lsem, ssem, rsem = refs[2 * n:]
        x, y, c = _place()
        mine, cps = [], []
        for a in range(n):
            mine.append(pltpu.make_async_copy(ins[a], _half(outs[a], c), lsem.at[a]))
            cps.append(pltpu.make_async_remote_copy(
                src_ref=ins[a], dst_ref=_half(outs[a], c), send_sem=ssem.at[a], recv_sem=rsem.at[a],
                device_id=(x, y, 1 - c), device_id_type=MESH))
        for cp in mine + cps:
            cp.start()
        for a in range(n):
            theirs = _half(outs[a], 1 - c)
            pltpu.make_async_remote_copy(src_ref=theirs, dst_ref=theirs, send_sem=ssem.at[a], recv_sem=rsem.at[a],
                                         device_id=(x, y, c), device_id_type=MESH).wait_recv()
        for cp in cps:
            cp.wait_send()
        for cp in mine:
            cp.wait()

    return pl.pallas_call(
        body, name="grad_sibling_join",
        out_shape=tuple(jax.ShapeDtypeStruct((2 * h.shape[0], h.shape[1]), h.dtype) for h in halves),
        in_specs=[ANY] * n, out_specs=(ANY,) * n,
        scratch_shapes=[pltpu.SemaphoreType.DMA((n,)), pltpu.SemaphoreType.DMA((n,)), pltpu.SemaphoreType.DMA((n,))],
    )(*halves)


def _small_all_reduce(vec, loss_row, loss_scale):
    r, dm = vec.shape

    def body(v_ref, o_ref, l_ref, buf, ssem, rsem):
        x, y, c = _place()
        me = 4 * x + 2 * y + c
        buf[me] = v_ref[...]
        cps = []
        for mask in range(1, 8):
            fx, fy, fc = (mask >> 2) & 1, (mask >> 1) & 1, mask & 1
            cps.append(pltpu.make_async_remote_copy(
                src_ref=v_ref, dst_ref=buf.at[me], send_sem=ssem.at[mask - 1], recv_sem=rsem.at[mask - 1],
                device_id=(x ^ fx, y ^ fy, c ^ fc), device_id_type=MESH))
        for cp in cps:
            cp.start()
        for mask in range(1, 8):
            fx, fy, fc = (mask >> 2) & 1, (mask >> 1) & 1, mask & 1
            frm = 4 * (x ^ fx) + 2 * (y ^ fy) + (c ^ fc)
            pltpu.make_async_remote_copy(
                src_ref=v_ref, dst_ref=buf.at[frm], send_sem=ssem.at[mask - 1], recv_sem=rsem.at[mask - 1],
                device_id=(x, y, c), device_id_type=MESH).wait_recv()
        for cp in cps:
            cp.wait_send()
        acc = buf[0]
        for i in range(1, 8):
            acc = acc + buf[i]
        o_ref[...] = acc
        l_ref[...] = jnp.sum(acc[loss_row:loss_row + SMALL_ROWS, :], axis=(0, 1), keepdims=True) * loss_scale

    vm = pl.BlockSpec(memory_space=pltpu.VMEM)
    return pl.pallas_call(
        body, name="small_all_reduce",
        out_shape=(jax.ShapeDtypeStruct((r, dm), F32), jax.ShapeDtypeStruct((1, 1), F32)),
        in_specs=[vm], out_specs=(vm, vm),
        scratch_shapes=[pltpu.VMEM((8, r, dm), F32), pltpu.SemaphoreType.DMA((7,)), pltpu.SemaphoreType.DMA((7,))],
    )(vec)


def kernel(x, meta_tokens, norm_mix_g, w_in, b_gate, pool_w, pool_scale, conv_w, conv_out_w, w_o, norm_ffn_g, w_gate_up, w_down, norm_final_g, loss_target, m_meta_tokens, m_norm_mix_g, m_w_in, m_b_gate, m_pool_w, m_pool_scale, m_conv_w, m_conv_out_w, m_w_o, m_norm_ffn_g, m_w_gate_up, m_w_down, m_norm_final_g, v_meta_tokens, v_norm_mix_g, v_w_in, v_b_gate, v_pool_w, v_pool_scale, v_conv_w, v_conv_out_w, v_w_o, v_norm_ffn_g, v_w_gate_up, v_w_down, v_norm_final_g):
    seq, dm = x.shape[1], x.shape[2]
    tail = 256 if seq % 256 == 0 else LANES
    tm = tail
    lp = seq + tail
    n_chips = 4
    n_groups = len(POOL_WINDOWS)
    gw = dm // n_groups
    tc = min(256, gw)
    cx, cy, cc = _place()
    chip = 2 * cx + cy
    dloc = dm // n_chips

    pool2 = pool_w.reshape(n_groups * pool_w.shape[1], gw)
    big = {"w_in": w_in, "w_gate_up": w_gate_up, "pool_w": pool2, "conv_out_w": conv_out_w, "w_o": w_o, "w_down": w_down}
    big_b = {nme: _to_bf16("cast_" + nme, wv) for nme, wv in big.items()}
    small_loc = jnp.concatenate([meta_tokens, jnp.pad(conv_w, ((0, 8 - conv_w.shape[0]), (0, 0))),
                                 jnp.zeros((8, dloc), F32)], axis=0)
    gathered = _all_gather(list(big_b.values()) + [small_loc])
    w_in4, w_gu4, pool4, conv_out4, w_o4, w_down4, small4 = gathered
    pool4 = pool4.reshape(n_chips, n_groups, gw // n_chips, gw)
    conv_out_f = conv_out4.reshape(dm, dm)
    w_o_f = w_o4.reshape(dm, dm)
    w_down_f = w_down4.reshape(-1, dm)
    small_f = jnp.transpose(small4, (1, 0, 2)).reshape(small4.shape[1], dm)
    meta_f = small_f[:N_META]
    conv_w_f = small_f[N_META:N_META + 3]

    h0 = jnp.concatenate([x[0], jnp.zeros((tail - N_META, dm), F32), meta_f], axis=0)
    g1, g2, g3 = norm_mix_g.reshape(1, dm), norm_ffn_g.reshape(1, dm), norm_final_g.reshape(1, dm)
    b_gate2 = b_gate.reshape(2, dm)
    ps = pool_scale.reshape(1, dm)
    hn1 = _rms_fwd("rms_mix", h0, g1, tm)
    proj = _nn_sharded("proj", hn1, w_in4, 6)
    pooled, z = _mixer_fwd("mixer_fwd", proj, conv_w_f, tc)
    ya = _pool_fwd("pool_proj", pooled, pool4)
    yb = _nn_plain("conv_out", z, conv_out_f, BF16)
    mix = _gate_mix("gate_mix", proj, b_gate2, ya, ps, yb, tm)
    h1 = _nn_plain("attn_out", mix, w_o_f, F32, res=h0, tn_pref=256)
    hn2 = _rms_fwd("rms_ffn", h1, g2, tm)
    gu = _nn_sharded("gate_up", hn2, w_gu4, 2)
    act = _swiglu_fwd("swiglu", gu, tm)
    h2 = _nn_plain("ffn_down", act, w_down_f, F32, res=h1, tn_pref=512, tk_pref=1536)
    dh2, dh2b, loss_cols, dg3 = _final_loss("final_loss", h2, g3, loss_target[0], tm)

    dact = _nt_plain("d_act", dh2b, w_down_f)
    gw_down = _tn_plain("dw_down", act, dh2b)
    dgu = _swiglu_bwd("swiglu_bwd", dact, gu, tm)
    gw_gu = _tn_sharded("dw_gate_up", hn2, dgu, n_chips)
    dhn2 = _nt_sharded("d_hn2", dgu, w_gu4)
    dh1, dh1b, dg2 = _rms_bwd("rms_ffn_bwd", dhn2, h1, g2, dh2, tm)
    dmix = _nt_plain("d_mix", dh1b, w_o_f)
    gw_o = _tn_plain("dw_o", mix, dh1b)
    dproj, dyb, dya, db_gate, dps = _gate_bwd("gate_bwd", dmix, proj, b_gate2, ya, ps, yb, tm)
    dz = _nt_plain("d_z", dyb, conv_out_f)
    gw_conv_out = _tn_plain("dw_conv_out", z, dyb)
    dpooled = _pool_bwd_act("d_pooled", dya, pool4)
    gw_pool = _pool_bwd_w("dw_pool", pooled, dya, n_chips)
    dproj, dconv_w = _mixer_bwd("mixer_bwd", dz, dpooled, proj, conv_w_f, dproj, tc)
    gw_in = _tn_sharded("dw_in", hn1, dproj, n_chips)
    dhn1 = _nt_sharded("d_hn1", dproj, w_in4)
    dh0, _, dg1 = _rms_bwd("rms_mix_bwd", dhn1, h0, g1, dh1, tm)
    grad_x = dh0[:seq][None]
    dmeta = dh0[lp - N_META:]

    grads4 = [gw_in, gw_gu, gw_pool.reshape(n_chips, n_groups * (gw // n_chips), gw), gw_conv_out.reshape(n_chips, dloc, dm),
              gw_o.reshape(n_chips, dloc, dm), gw_down.reshape(n_chips, -1, dm)]
    names = ["w_in", "w_gate_up", "pool_w", "conv_out_w", "w_o", "w_down"]
    core = jnp.reshape(cc, (1,)).astype(jnp.int32)
    from_sibling = _sibling_swap(grads4)
    chip_parts = [_pair_add("pair_add_" + nme, g4, rv, core) for nme, g4, rv in zip(names, grads4, from_sibling)]
    all_parts = _chip_scatter(chip_parts)
    halves = [_chip_sum("chip_sum_" + nme, p) for nme, p in zip(names, all_parts)]
    full = _sibling_join(halves)
    g_big = dict(zip(names, full))

    vec = jnp.concatenate([dg1, dg2, dg3, db_gate, dps, loss_cols, dconv_w, dmeta], axis=0)
    loss_row = 5 * SMALL_ROWS
    red, loss11 = _small_all_reduce(vec, loss_row, 0.5 / dm)
    loss = loss11[0, 0]
    col0 = chip * dloc
    g_small = {
        "norm_mix_g": red[0], "norm_ffn_g": red[SMALL_ROWS], "norm_final_g": red[2 * SMALL_ROWS],
        "b_gate": red[3 * SMALL_ROWS:3 * SMALL_ROWS + 2].reshape(-1), "pool_scale": red[4 * SMALL_ROWS],
        "conv_w": lax.dynamic_slice(red, (6 * SMALL_ROWS, col0), (3, dloc)),
        "meta_tokens": lax.dynamic_slice(red, (7 * SMALL_ROWS, col0), (N_META, dloc)),
    }

    given = dict(meta_tokens=(meta_tokens, m_meta_tokens, v_meta_tokens), norm_mix_g=(norm_mix_g, m_norm_mix_g, v_norm_mix_g),
                 w_in=(w_in, m_w_in, v_w_in), b_gate=(b_gate, m_b_gate, v_b_gate), pool_w=(pool_w, m_pool_w, v_pool_w),
                 pool_scale=(pool_scale, m_pool_scale, v_pool_scale), conv_w=(conv_w, m_conv_w, v_conv_w),
                 conv_out_w=(conv_out_w, m_conv_out_w, v_conv_out_w), w_o=(w_o, m_w_o, v_w_o),
                 norm_ffn_g=(norm_ffn_g, m_norm_ffn_g, v_norm_ffn_g), w_gate_up=(w_gate_up, m_w_gate_up, v_w_gate_up),
                 w_down=(w_down, m_w_down, v_w_down), norm_final_g=(norm_final_g, m_norm_final_g, v_norm_final_g))
    order = list(given.keys())
    grad, delta, new_m, new_v = {}, {}, {}, {}
    for nme in names:
        w, m, v = given[nme]
        shape2 = g_big[nme].shape
        grad[nme] = g_big[nme].reshape(w.shape)
        d, nm, nv = _adamw("adamw_" + nme, w.reshape(shape2), g_big[nme], m.reshape(shape2), v.reshape(shape2))
        delta[nme], new_m[nme], new_v[nme] = d.reshape(w.shape), nm.reshape(w.shape), nv.reshape(w.shape)
    vec_names = ["norm_mix_g", "norm_ffn_g", "norm_final_g", "pool_scale"]

    def slab_vec(pick):
        rows = [pick(nme).reshape(1, dm) for nme in vec_names] + [pick("b_gate").reshape(2, dm), jnp.zeros((2, dm), F32)]
        return jnp.concatenate(rows, axis=0)

    def slab_col(pick):
        return jnp.concatenate([pick("meta_tokens"), pick("conv_w"), jnp.zeros((5, dloc), F32)], axis=0)

    for slab, tag in ((slab_vec, "vec"), (slab_col, "col")):
        d, nm, nv = _adamw("adamw_small_" + tag, slab(lambda nme: given[nme][0]), slab(lambda nme: g_small[nme]),
                           slab(lambda nme: given[nme][1]), slab(lambda nme: given[nme][2]))
        for out, res in ((delta, d), (new_m, nm), (new_v, nv)):
            if tag == "vec":
                for i, nme in enumerate(vec_names):
                    out[nme] = res[i]
                out["b_gate"] = res[4:6].reshape(-1)
            else:
                out["meta_tokens"] = res[:N_META]
                out["conv_w"] = res[N_META:N_META + 3]
    grad.update(g_small)
    return (loss, grad_x, *[grad[nme] for nme in order], *[delta[nme] for nme in order],
            *[new_m[nme] for nme in order], *[new_v[nme] for nme in order])
```

```python
import functools
import math

import jax
import jax.numpy as jnp
from jax import lax
from jax.experimental import pallas as pl
from jax.experimental.pallas import tpu as pltpu

F32 = jnp.float32
BF16 = jnp.bfloat16
N_META = 16
POOL_WINDOWS = (2, 4, 8, 16)
EPS = 1e-6
ADAM_LR, ADAM_B1, ADAM_B2, ADAM_EPS, ADAM_WD, ADAM_STEP = 0.001, 0.9, 0.999, 1e-08, 0.01, 10
LANES = 128
V7X_VMEM_BYTES = 64 * 1024 * 1024
VMEM_LIMIT = V7X_VMEM_BYTES - 8 * 1024 * 1024
MESH = pl.DeviceIdType.MESH
ANY = pl.BlockSpec(memory_space=pl.ANY)
CHIP_FLIPS = ((1, 0), (0, 1), (1, 1))
SMALL_ROWS = 8


def _pick(n, pref):
    best = None
    for t in range(LANES, min(n, pref) + 1, LANES):
        if n % t == 0:
            best = t
    assert best is not None, (n, pref)
    return best


def _params(n_axes=0):
    sem = ("arbitrary",) * n_axes if n_axes else None
    return pltpu.CompilerParams(dimension_semantics=sem, vmem_limit_bytes=VMEM_LIMIT)


_DIMS = {
    "nn": (((1,), (0,)), ((), ())),
    "nt": (((1,), (1,)), ((), ())),
    "tn": (((0,), (0,)), ((), ())),
}


def _matmul(name, mode, a, b, out_sds, grid, a_spec, b_spec, o_spec, nk, res=None, res_spec=None, acc_shape=None):
    out_dtype = out_sds.dtype
    in_place = nk > 1 and out_dtype == F32
    use_scratch = nk > 1 and not in_place
    rows = a_spec.block_shape[-2] if mode != "tn" else None
    chunk = _pick(rows, 768) if rows is not None and rows % LANES == 0 else rows

    def body(*refs):
        if res is not None:
            a_ref, b_ref, r_ref, o_ref, *scr = refs
        else:
            a_ref, b_ref, o_ref, *scr = refs
            r_ref = None
        k = pl.program_id(len(grid) - 1) if nk > 1 else None

        def emit(sl):
            if sl is None:
                part = lax.dot_general(a_ref[...], b_ref[...], _DIMS[mode], preferred_element_type=F32)
                idx = (slice(None), slice(None))
            else:
                part = lax.dot_general(a_ref[sl, :], b_ref[...], _DIMS[mode], preferred_element_type=F32)
                idx = (sl, slice(None))
            if nk == 1:
                if r_ref is not None:
                    part = part + r_ref[idx]
                o_ref[idx] = part.astype(out_dtype)
                return
            acc = scr[0] if use_scratch else o_ref

            @pl.when(k == 0)
            def _():
                first = part
                if r_ref is not None and in_place:
                    first = first + r_ref[idx]
                acc[idx] = first

            @pl.when(k > 0)
            def _():
                acc[idx] += part

            if use_scratch:

                @pl.when(k == nk - 1)
                def _():
                    o_ref[idx] = acc[idx].astype(out_dtype)

        if mode == "tn" or chunk == rows:
            emit(None)
        else:
            for m0 in range(0, rows, chunk):
                emit(pl.ds(m0, chunk))

    ins = [a, b] + ([res] if res is not None else [])
    in_specs = [a_spec, b_spec] + ([res_spec] if res is not None else [])
    scratch = [pltpu.VMEM(acc_shape, F32)] if use_scratch else []
    return pl.pallas_call(
        body, name=name, out_shape=out_sds, grid=grid, in_specs=in_specs, out_specs=o_spec,
        scratch_shapes=scratch, compiler_params=_params(len(grid)),
    )(*ins)


def _nn_sharded(name, a, w4, nseg):
    lp, kdim = a.shape
    s, _, nloc = w4.shape
    segw = s * nloc // nseg
    tn = _pick(math.gcd(nloc, segw), 1536)
    bw, bo = nloc // tn, segw // tn
    return _matmul(
        name, "nn", a, w4, jax.ShapeDtypeStruct((nseg, lp, segw), BF16), (s * bw,),
        pl.BlockSpec((lp, kdim), lambda j: (0, 0)),
        pl.BlockSpec((None, kdim, tn), lambda j: (j // bw, 0, j % bw)),
        pl.BlockSpec((None, lp, tn), lambda j: (j // bo, 0, j % bo)), 1)


def _nn_plain(name, a, w, out_dtype, res=None, tn_pref=512, tk_pref=2048):
    lp, kdim = a.shape
    n = w.shape[1]
    tn = _pick(n, tn_pref)
    tk = kdim if kdim <= tk_pref else _pick(kdim, tk_pref)
    nk = kdim // tk
    grid = (n // tn, nk) if nk > 1 else (n // tn,)
    if nk > 1:
        a_spec = pl.BlockSpec((lp, tk), lambda j, k: (0, k))
        w_spec = pl.BlockSpec((tk, tn), lambda j, k: (k, j))
        o_spec = pl.BlockSpec((lp, tn), lambda j, k: (0, j))
    else:
        a_spec = pl.BlockSpec((lp, tk), lambda j: (0, 0))
        w_spec = pl.BlockSpec((tk, tn), lambda j: (0, j))
        o_spec = pl.BlockSpec((lp, tn), lambda j: (0, j))
    return _matmul(name, "nn", a, w, jax.ShapeDtypeStruct((lp, n), out_dtype), grid, a_spec, w_spec, o_spec, nk,
                   res=res, res_spec=o_spec if res is not None else None, acc_shape=(lp, tn))


def _nt_plain(name, a, w, tn_pref=512):
    lp, kdim = a.shape
    n = w.shape[0]
    tn = _pick(n, tn_pref)
    return _matmul(
        name, "nt", a, w, jax.ShapeDtypeStruct((lp, n), BF16), (n // tn,),
        pl.BlockSpec((lp, kdim), lambda j: (0, 0)),
        pl.BlockSpec((tn, kdim), lambda j: (j, 0)),
        pl.BlockSpec((lp, tn), lambda j: (0, j)), 1)


def _nt_sharded(name, dseg, w4, to_pref=1024):
    nseg, lp, segw = dseg.shape
    s, kdim, nloc = w4.shape
    tr = _pick(math.gcd(nloc, segw), 1536)
    ba, bw = segw // tr, nloc // tr
    nr = s * bw
    to = _pick(kdim, to_pref)
    return _matmul(
        name, "nt", dseg, w4, jax.ShapeDtypeStruct((lp, kdim), F32), (kdim // to, nr),
        pl.BlockSpec((None, lp, tr), lambda j, r: (r // ba, 0, r % ba)),
        pl.BlockSpec((None, to, tr), lambda j, r: (r // bw, j, r % bw)),
        pl.BlockSpec((lp, to), lambda j, r: (0, j)), nr)


def _tn_plain(name, a, d, tk_pref=512):
    lp, kdim = a.shape
    n = d.shape[1]
    tk = _pick(kdim, tk_pref)
    return _matmul(
        name, "tn", a, d, jax.ShapeDtypeStruct((kdim, n), BF16), (kdim // tk,),
        pl.BlockSpec((lp, tk), lambda i: (0, i)),
        pl.BlockSpec((lp, n), lambda i: (0, 0)),
        pl.BlockSpec((tk, n), lambda i: (i, 0)), 1)


def _tn_sharded(name, a, dseg, s, tk_pref=512):
    lp, kdim = a.shape
    nseg, _, segw = dseg.shape
    nloc = nseg * segw // s
    tn = _pick(math.gcd(nloc, segw), 1536)
    bd, bo = segw // tn, nloc // tn
    tk = _pick(kdim, tk_pref)
    return _matmul(
        name, "tn", a, dseg, jax.ShapeDtypeStruct((s, kdim, nloc), BF16), (s * bo, kdim // tk),
        pl.BlockSpec((lp, tk), lambda j, i: (0, i)),
        pl.BlockSpec((None, lp, tn), lambda j, i: (j // bd, 0, j % bd)),
        pl.BlockSpec((None, tk, tn), lambda j, i: (j // bo, i, j % bo)), 1)


def _pool_fwd(name, pooled, pw4):
    lp, dm = pooled.shape
    s, g, rs, gw = pw4.shape
    return _matmul(
        name, "nn", pooled, pw4, jax.ShapeDtypeStruct((lp, dm), BF16), (g, s),
        pl.BlockSpec((lp, rs), lambda gi, si: (0, gi * s + si)),
        pl.BlockSpec((None, None, rs, gw), lambda gi, si: (si, gi, 0, 0)),
        pl.BlockSpec((lp, gw), lambda gi, si: (0, gi)), s, acc_shape=(lp, gw))


def _pool_bwd_act(name, dya, pw4):
    lp, dm = dya.shape
    s, g, rs, gw = pw4.shape
    return _matmul(
        name, "nt", dya, pw4, jax.ShapeDtypeStruct((lp, dm), BF16), (g, s),
        pl.BlockSpec((lp, gw), lambda gi, si: (0, gi)),
        pl.BlockSpec((None, None, rs, gw), lambda gi, si: (si, gi, 0, 0)),
        pl.BlockSpec((lp, rs), lambda gi, si: (0, gi * s + si)), 1)


def _pool_bwd_w(name, pooled, dya, s):
    lp, dm = pooled.shape
    g = len(POOL_WINDOWS)
    gw = dm // g
    rs = gw // s
    return _matmul(
        name, "tn", pooled, dya, jax.ShapeDtypeStruct((s, g, rs, gw), BF16), (g, s),
        pl.BlockSpec((lp, rs), lambda gi, si: (0, gi * s + si)),
        pl.BlockSpec((lp, gw), lambda gi, si: (0, gi)),
        pl.BlockSpec((None, None, rs, gw), lambda gi, si: (si, gi, 0, 0)), 1)


def _rms_fwd(name, h, g, tm):
    lp, dm = h.shape

    def body(h_ref, g_ref, o_ref):
        hv = h_ref[...]
        r = lax.rsqrt(jnp.mean(hv * hv, axis=-1, keepdims=True) + EPS)
        o_ref[...] = (hv * r * g_ref[...]).astype(BF16)

    row = pl.BlockSpec((tm, dm), lambda i: (i, 0))
    return pl.pallas_call(
        body, name=name, out_shape=jax.ShapeDtypeStruct((lp, dm), BF16), grid=(lp // tm,),
        in_specs=[row, pl.BlockSpec((1, dm), lambda i: (0, 0))], out_specs=row, compiler_params=_params(1),
    )(h, g)


def _rms_bwd(name, dy, h, g, dres, tm):
    lp, dm = h.shape

    def body(dy_ref, h_ref, g_ref, dr_ref, dh_ref, dhb_ref, dg_ref):
        hv = h_ref[...]
        r = lax.rsqrt(jnp.mean(hv * hv, axis=-1, keepdims=True) + EPS)
        xhat = hv * r
        dyv = dy_ref[...]
        dxh = dyv * g_ref[...]
        dh = dr_ref[...] + r * (dxh - xhat * jnp.mean(dxh * xhat, axis=-1, keepdims=True))
        dh_ref[...] = dh
        dhb_ref[...] = dh.astype(BF16)

        @pl.when(pl.program_id(0) == 0)
        def _():
            dg_ref[...] = jnp.zeros_like(dg_ref)

        dg_ref[0:1, :] += jnp.sum(dyv * xhat, axis=0, keepdims=True)

    row = pl.BlockSpec((tm, dm), lambda i: (i, 0))
    slab = pl.BlockSpec((SMALL_ROWS, dm), lambda i: (0, 0))
    return pl.pallas_call(
        body, name=name, grid=(lp // tm,),
        out_shape=(jax.ShapeDtypeStruct((lp, dm), F32), jax.ShapeDtypeStruct((lp, dm), BF16),
                   jax.ShapeDtypeStruct((SMALL_ROWS, dm), F32)),
        in_specs=[row, row, pl.BlockSpec((1, dm), lambda i: (0, 0)), row], out_specs=(row, row, slab),
        compiler_params=_params(1),
    )(dy, h, g, dres)


def _gate_mix(name, proj, b_gate2, ya, pool_scale, yb, tm):
    _, lp, dm = proj.shape

    def body(ga_ref, gr_ref, b_ref, ya_ref, ps_ref, yb_ref, o_ref):
        g_a = jax.nn.sigmoid(ga_ref[...].astype(F32) + b_ref[0:1, :])
        g_b = jax.nn.sigmoid(gr_ref[...].astype(F32) + b_ref[1:2, :])
        y_a = ya_ref[...].astype(F32) * ps_ref[...]
        o_ref[...] = (g_a * y_a + g_b * yb_ref[...].astype(F32)).astype(BF16)

    row = pl.BlockSpec((tm, dm), lambda i: (i, 0))
    return pl.pallas_call(
        body, name=name, out_shape=jax.ShapeDtypeStruct((lp, dm), BF16), grid=(lp // tm,),
        in_specs=[pl.BlockSpec((None, tm, dm), lambda i: (4, i, 0)), pl.BlockSpec((None, tm, dm), lambda i: (5, i, 0)),
                  pl.BlockSpec((2, dm), lambda i: (0, 0)), row, pl.BlockSpec((1, dm), lambda i: (0, 0)), row],
        out_specs=row, compiler_params=_params(1),
    )(proj, proj, b_gate2, ya, pool_scale, yb)


def _gate_bwd(name, dmix, proj, b_gate2, ya, pool_scale, yb, tm):
    _, lp, dm = proj.shape

    def body(dm_ref, ga_ref, gr_ref, b_ref, ya_ref, ps_ref, yb_ref, dp_ref, dyb_ref, dya_ref, db_ref, dps_ref):
        dmx = dm_ref[...].astype(F32)
        g_a = jax.nn.sigmoid(ga_ref[...].astype(F32) + b_ref[0:1, :])
        g_b = jax.nn.sigmoid(gr_ref[...].astype(F32) + b_ref[1:2, :])
        ya_pre = ya_ref[...].astype(F32)
        ybv = yb_ref[...].astype(F32)
        ps = ps_ref[...]
        dga = dmx * (ya_pre * ps) * (g_a * (1.0 - g_a))
        dgr = dmx * ybv * (g_b * (1.0 - g_b))
        dp_ref[0] = dga.astype(BF16)
        dp_ref[1] = dgr.astype(BF16)
        dyb_ref[...] = (dmx * g_b).astype(BF16)
        dya_ref[...] = (dmx * g_a * ps).astype(BF16)

        @pl.when(pl.program_id(0) == 0)
        def _():
            db_ref[...] = jnp.zeros_like(db_ref)
            dps_ref[...] = jnp.zeros_like(dps_ref)

        db_ref[0:1, :] += jnp.sum(dga, axis=0, keepdims=True)
        db_ref[1:2, :] += jnp.sum(dgr, axis=0, keepdims=True)
        dps_ref[0:1, :] += jnp.sum(dmx * g_a * ya_pre, axis=0, keepdims=True)

    row = pl.BlockSpec((tm, dm), lambda i: (i, 0))
    one = pl.BlockSpec((1, dm), lambda i: (0, 0))
    slab = pl.BlockSpec((SMALL_ROWS, dm), lambda i: (0, 0))
    return pl.pallas_call(
        body, name=name, grid=(lp // tm,),
        out_shape=(jax.ShapeDtypeStruct((6, lp, dm), BF16), jax.ShapeDtypeStruct((lp, dm), BF16),
                   jax.ShapeDtypeStruct((lp, dm), BF16), jax.ShapeDtypeStruct((SMALL_ROWS, dm), F32),
                   jax.ShapeDtypeStruct((SMALL_ROWS, dm), F32)),
        in_specs=[row, pl.BlockSpec((None, tm, dm), lambda i: (4, i, 0)), pl.BlockSpec((None, tm, dm), lambda i: (5, i, 0)),
                  pl.BlockSpec((2, dm), lambda i: (0, 0)), row, one, row],
        out_specs=(pl.BlockSpec((2, tm, dm), lambda i: (2, i, 0)), row, row, slab, slab),
        compiler_params=_params(1),
    )(dmix, proj, proj, b_gate2, ya, pool_scale, yb)


def _swiglu_fwd(name, gu, tm):
    _, lp, f = gu.shape

    def body(g_ref, u_ref, o_ref):
        gt = g_ref[...].astype(F32)
        o_ref[...] = (gt * jax.nn.sigmoid(gt) * u_ref[...].astype(F32)).astype(BF16)

    return pl.pallas_call(
        body, name=name, out_shape=jax.ShapeDtypeStruct((lp, f), BF16), grid=(lp // tm,),
        in_specs=[pl.BlockSpec((None, tm, f), lambda i: (0, i, 0)), pl.BlockSpec((None, tm, f), lambda i: (1, i, 0))],
        out_specs=pl.BlockSpec((tm, f), lambda i: (i, 0)), compiler_params=_params(1),
    )(gu, gu)


def _swiglu_bwd(name, dact, gu, tm):
    _, lp, f = gu.shape

    def body(d_ref, g_ref, u_ref, o_ref):
        d = d_ref[...].astype(F32)
        gt = g_ref[...].astype(F32)
        sg = jax.nn.sigmoid(gt)
        o_ref[0] = (d * u_ref[...].astype(F32) * (sg * (1.0 + gt * (1.0 - sg)))).astype(BF16)
        o_ref[1] = (d * (gt * sg)).astype(BF16)

    return pl.pallas_call(
        body, name=name, out_shape=jax.ShapeDtypeStruct((2, lp, f), BF16), grid=(lp // tm,),
        in_specs=[pl.BlockSpec((tm, f), lambda i: (i, 0)), pl.BlockSpec((None, tm, f), lambda i: (0, i, 0)),
                  pl.BlockSpec((None, tm, f), lambda i: (1, i, 0))],
        out_specs=pl.BlockSpec((2, tm, f), lambda i: (0, i, 0)), compiler_params=_params(1),
    )(dact, gu, gu)


def _final_loss(name, h2, g3, target, tm):
    lp, dm = h2.shape
    nx = target.shape[0] // tm

    def body(h_ref, g_ref, t_ref, dh_ref, dhb_ref, ls_ref, dg_ref):
        i = pl.program_id(0)

        @pl.when(i == 0)
        def _():
            ls_ref[...] = jnp.zeros_like(ls_ref)
            dg_ref[...] = jnp.zeros_like(dg_ref)

        @pl.when(i < nx)
        def _():
            hv = h_ref[...]
            gv = g_ref[...]
            r = lax.rsqrt(jnp.mean(hv * hv, axis=-1, keepdims=True) + EPS)
            xhat = hv * r
            err = xhat * gv - t_ref[...]
            dout = err * (1.0 / dm)
            dxh = dout * gv
            dh = r * (dxh - xhat * jnp.mean(dxh * xhat, axis=-1, keepdims=True))
            dh_ref[...] = dh
            dhb_ref[...] = dh.astype(BF16)
            ls_ref[0:1, :] += jnp.sum(err * err, axis=0, keepdims=True)
            dg_ref[0:1, :] += jnp.sum(dout * xhat, axis=0, keepdims=True)

        @pl.when(i >= nx)
        def _():
            dh_ref[...] = jnp.zeros_like(dh_ref)
            dhb_ref[...] = jnp.zeros_like(dhb_ref)

    row = pl.BlockSpec((tm, dm), lambda i: (i, 0))
    slab = pl.BlockSpec((SMALL_ROWS, dm), lambda i: (0, 0))
    return pl.pallas_call(
        body, name=name, grid=(lp // tm,),
        out_shape=(jax.ShapeDtypeStruct((lp, dm), F32), jax.ShapeDtypeStruct((lp, dm), BF16),
                   jax.ShapeDtypeStruct((SMALL_ROWS, dm), F32), jax.ShapeDtypeStruct((SMALL_ROWS, dm), F32)),
        in_specs=[row, pl.BlockSpec((1, dm), lambda i: (0, 0)), pl.BlockSpec((tm, dm), lambda i: (jnp.minimum(i, nx - 1), 0))],
        out_specs=(row, row, slab, slab), compiler_params=_params(1),
    )(h2, g3, target)


def _shift(v, k):
    return pltpu.roll(v, k % v.shape[0], axis=0)


def _window_sum(v, group, sign):
    s2 = v + _shift(v, sign * 1)
    s4 = s2 + _shift(s2, sign * 2)
    s8 = s4 + _shift(s4, sign * 4)
    s16 = s8 + _shift(s8, sign * 8)
    return jnp.where(group == 0, s2, jnp.where(group == 1, s4, jnp.where(group == 2, s8, s16)))


def _pool_count(lp, group):
    row = lax.broadcasted_iota(jnp.int32, (lp, 1), 0)
    window = jnp.left_shift(2, group).astype(F32)
    meta_pos = (row - (lp - N_META) + 1).astype(F32)
    return jnp.where(row >= lp - N_META, jnp.minimum(meta_pos, window), window)


def _mixer_fwd(name, proj, conv_w, tc):
    _, lp, dm = proj.shape
    per_group = dm // len(POOL_WINDOWS) // tc

    def body(u_ref, gb_ref, gc_ref, v_ref, cw_ref, p_ref, z_ref):
        group = pl.program_id(0) // per_group
        u = u_ref[...].astype(F32)
        p_ref[...] = (_window_sum(u, group, 1) / _pool_count(lp, group) - u).astype(BF16)
        cv = gc_ref[...].astype(F32) * v_ref[...].astype(F32)
        conv = cw_ref[0:1, :] * _shift(cv, 2) + cw_ref[1:2, :] * _shift(cv, 1) + cw_ref[2:3, :] * cv
        z_ref[...] = (gb_ref[...].astype(F32) * conv).astype(BF16)

    def seg(s):
        return pl.BlockSpec((None, lp, tc), lambda j: (s, 0, j))

    col = pl.BlockSpec((lp, tc), lambda j: (0, j))
    return pl.pallas_call(
        body, name=name, grid=(dm // tc,),
        out_shape=(jax.ShapeDtypeStruct((lp, dm), BF16), jax.ShapeDtypeStruct((lp, dm), BF16)),
        in_specs=[seg(0), seg(1), seg(2), seg(3), pl.BlockSpec((3, tc), lambda j: (0, j))],
        out_specs=(col, col), compiler_params=_params(1),
    )(proj, proj, proj, proj, conv_w)


def _mixer_bwd(name, dz, dpooled, proj, conv_w, dproj, tc):
    _, lp, dm = proj.shape
    per_group = dm // len(POOL_WINDOWS) // tc

    def body(dz_ref, dp_ref, gb_ref, gc_ref, v_ref, cw_ref, _, o_ref, dcw_ref):
        group = pl.program_id(0) // per_group
        dzv = dz_ref[...].astype(F32)
        gb = gb_ref[...].astype(F32)
        gc = gc_ref[...].astype(F32)
        vv = v_ref[...].astype(F32)
        cv = gc * vv
        c1 = _shift(cv, 1)
        c2 = _shift(cv, 2)
        w0, w1, w2 = cw_ref[0:1, :], cw_ref[1:2, :], cw_ref[2:3, :]
        o_ref[1] = (dzv * (w0 * c2 + w1 * c1 + w2 * cv)).astype(BF16)
        dconv = dzv * gb
        dcw_ref[...] = jnp.zeros_like(dcw_ref)
        dcw_ref[0:1, :] = jnp.sum(dconv * c2, axis=0, keepdims=True)
        dcw_ref[1:2, :] = jnp.sum(dconv * c1, axis=0, keepdims=True)
        dcw_ref[2:3, :] = jnp.sum(dconv * cv, axis=0, keepdims=True)
        dcv = w0 * _shift(dconv, -2) + w1 * _shift(dconv, -1) + w2 * dconv
        o_ref[2] = (dcv * vv).astype(BF16)
        o_ref[3] = (dcv * gc).astype(BF16)
        dpv = dp_ref[...].astype(F32)
        o_ref[0] = (_window_sum(dpv / _pool_count(lp, group), group, -1) - dpv).astype(BF16)

    def seg(s):
        return pl.BlockSpec((None, lp, tc), lambda j: (s, 0, j))

    col = pl.BlockSpec((lp, tc), lambda j: (0, j))
    return pl.pallas_call(
        body, name=name, grid=(dm // tc,),
        out_shape=(jax.ShapeDtypeStruct(dproj.shape, BF16), jax.ShapeDtypeStruct((SMALL_ROWS, dm), F32)),
        in_specs=[col, col, seg(1), seg(2), seg(3), pl.BlockSpec((3, tc), lambda j: (0, j)), ANY],
        out_specs=(pl.BlockSpec((4, lp, tc), lambda j: (0, 0, j)), pl.BlockSpec((SMALL_ROWS, tc), lambda j: (0, j))),
        input_output_aliases={6: 0}, compiler_params=_params(1),
    )(dz, dpooled, proj, proj, proj, conv_w, dproj)


def _row_tile(r, c, bytes_per_row_elem=4, budget=2 * 1024 * 1024):
    best = None
    for t in range(16, r + 1, 16):
        if r % t == 0 and t * c * bytes_per_row_elem <= budget:
            best = t
    return best if best is not None else r


def _pair_add(name, g4, recv, core):
    s, r, c = g4.shape
    h = r // 2
    tr = _row_tile(h, c)
    nb = h // tr

    def body(core_ref, g_ref, r_ref, o_ref):
        o_ref[...] = (g_ref[...].astype(F32) + r_ref[...].astype(F32)).astype(BF16)

    grid_spec = pltpu.PrefetchScalarGridSpec(
        num_scalar_prefetch=1, grid=(s, nb),
        in_specs=[pl.BlockSpec((None, tr, c), lambda si, j, core_ref: (si, core_ref[0] * nb + j, 0)),
                  pl.BlockSpec((None, tr, c), lambda si, j, core_ref: (si, j, 0))],
        out_specs=pl.BlockSpec((None, tr, c), lambda si, j, core_ref: (si, j, 0)))
    return pl.pallas_call(
        body, name=name, out_shape=jax.ShapeDtypeStruct((s, h, c), BF16), grid_spec=grid_spec,
        compiler_params=_params(2),
    )(core, g4, recv)


def _chip_sum(name, parts, recv, chip):
    _, h, c = parts.shape
    tr = _row_tile(h, c)

    def body(chip_ref, p_ref, r_ref, o_ref):
        acc = p_ref[...].astype(F32)
        for i in range(len(CHIP_FLIPS)):
            acc = acc + r_ref[i].astype(F32)
        o_ref[...] = acc

    grid_spec = pltpu.PrefetchScalarGridSpec(
        num_scalar_prefetch=1, grid=(h // tr,),
        in_specs=[pl.BlockSpec((None, tr, c), lambda j, chip_ref: (chip_ref[0], j, 0)),
                  pl.BlockSpec((len(CHIP_FLIPS), tr, c), lambda j, chip_ref: (0, j, 0))],
        out_specs=pl.BlockSpec((tr, c), lambda j, chip_ref: (j, 0)))
    return pl.pallas_call(
        body, name=name, out_shape=jax.ShapeDtypeStruct((h, c), F32), grid_spec=grid_spec, compiler_params=_params(1),
    )(chip, parts, recv)


def _adam_update(w, gv, m, v):
    c1 = 1.0 - ADAM_B1 ** ADAM_STEP
    c2 = 1.0 - ADAM_B2 ** ADAM_STEP
    nm = ADAM_B1 * m + (1.0 - ADAM_B1) * gv
    nv = ADAM_B2 * v + (1.0 - ADAM_B2) * (gv * gv)
    return -ADAM_LR * ((nm / c1) / (jnp.sqrt(nv / c2) + ADAM_EPS) + ADAM_WD * w), nm, nv


def _adamw_halves(name, w, g_own, g_sib, m, v, core):
    r, c = w.shape
    h = r // 2
    tr = _row_tile(h, c, budget=1024 * 1024)
    nbh = h // tr

    def body(core_ref, w_ref, go_ref, gs_ref, m_ref, v_ref, g_ref, d_ref, nm_ref, nv_ref):
        mine = (pl.program_id(0) // nbh) == core_ref[0]
        gv = jnp.where(mine, go_ref[...], gs_ref[...])
        g_ref[...] = gv
        d_ref[...], nm_ref[...], nv_ref[...] = _adam_update(w_ref[...], gv, m_ref[...], v_ref[...])

    def blk(fn):
        return pl.BlockSpec((tr, c), fn)

    full = blk(lambda j, core_ref: (j, 0))
    own = blk(lambda j, core_ref: (jnp.clip(j - core_ref[0] * nbh, 0, nbh - 1), 0))
    sib = blk(lambda j, core_ref: (jnp.clip(j - (1 - core_ref[0]) * nbh, 0, nbh - 1), 0))
    grid_spec = pltpu.PrefetchScalarGridSpec(
        num_scalar_prefetch=1, grid=(r // tr,), in_specs=[full, own, sib, full, full], out_specs=(full,) * 4)
    sds = jax.ShapeDtypeStruct((r, c), F32)
    return pl.pallas_call(
        body, name=name, out_shape=(sds,) * 4, grid_spec=grid_spec, compiler_params=_params(1),
    )(core, w, g_own, g_sib, m, v)


def _adamw(name, w, g, m, v):
    r, c = w.shape

    def body(w_ref, g_ref, m_ref, v_ref, d_ref, nm_ref, nv_ref):
        d_ref[...], nm_ref[...], nv_ref[...] = _adam_update(w_ref[...], g_ref[...], m_ref[...], v_ref[...])

    blk = pl.BlockSpec((r, c), lambda j: (0, 0))
    sds = jax.ShapeDtypeStruct((r, c), F32)
    return pl.pallas_call(
        body, name=name, out_shape=(sds, sds, sds), grid=(1,), in_specs=[blk] * 4, out_specs=(blk,) * 3,
        compiler_params=_params(1),
    )(w, g, m, v)


def _cast_into_slot(name, w, chip, dtype):
    r, c = w.shape
    tr = _row_tile(r, c)

    def body(chip_ref, w_ref, o_ref):
        o_ref[...] = w_ref[...].astype(dtype)

    grid_spec = pltpu.PrefetchScalarGridSpec(
        num_scalar_prefetch=1, grid=(r // tr,),
        in_specs=[pl.BlockSpec((tr, c), lambda j, chip_ref: (j, 0))],
        out_specs=pl.BlockSpec((None, tr, c), lambda j, chip_ref: (chip_ref[0], j, 0)))
    return pl.pallas_call(
        body, name=name, out_shape=jax.ShapeDtypeStruct((4, r, c), dtype), grid_spec=grid_spec, compiler_params=_params(1),
    )(chip, w)


def _place():
    return lax.axis_index("x"), lax.axis_index("y"), lax.axis_index("c")


def _chip_of(x, y, flip):
    px, py = x ^ flip[0], y ^ flip[1]
    return px, py, 2 * px + py


def _half(ref, which):
    rows = ref.shape[0] // 2
    return ref.at[pl.ds(which * rows, rows)]


def _all_gather(slabs):
    n = len(slabs)

    def body(*refs):
        outs = refs[n:2 * n]
        ssem, rsem = refs[2 * n:]
        x, y, c = _place()
        k = 2 * x + y

        def copy(a, slot, ref, to):
            return pltpu.make_async_remote_copy(src_ref=ref, dst_ref=ref, send_sem=ssem.at[a * 6 + slot],
                                                recv_sem=rsem.at[a * 6 + slot], device_id=to, device_id_type=MESH)

        sends = []
        for a in range(n):
            for j, flip in enumerate(CHIP_FLIPS):
                px, py, _ = _chip_of(x, y, flip)
                sends.append(copy(a, j, _half(outs[a].at[k], c), (px, py, c)))
        for cp in sends:
            cp.start()
        for a in range(n):
            for j, flip in enumerate(CHIP_FLIPS):
                _, _, kj = _chip_of(x, y, flip)
                landed = _half(outs[a].at[kj], c)
                copy(a, j, landed, (x, y, c)).wait_recv()
                passed = copy(a, 3 + j, landed, (x, y, 1 - c))
                passed.start()
                sends.append(passed)
        for a in range(n):
            for j, flip in enumerate(CHIP_FLIPS):
                _, _, kj = _chip_of(x, y, flip)
                copy(a, 3 + j, _half(outs[a].at[kj], 1 - c), (x, y, c)).wait_recv()
        for cp in sends:
            cp.wait_send()

    return pl.pallas_call(
        body, name="all_gather_weights",
        out_shape=tuple(jax.ShapeDtypeStruct(t.shape, t.dtype) for t in slabs),
        in_specs=[ANY] * n, out_specs=(ANY,) * n, input_output_aliases={a: a for a in range(n)},
        scratch_shapes=[pltpu.SemaphoreType.DMA((6 * n,)), pltpu.SemaphoreType.DMA((6 * n,))],
    )(*slabs)


def _sibling_swap(grads):
    n = len(grads)

    def body(*refs):
        ins, outs = refs[:n], refs[n:2 * n]
        ssem, rsem = refs[2 * n:]
        x, y, c = _place()
        cps = []
        for a in range(n):
            h = ins[a].shape[1] // 2
            cps.append(pltpu.make_async_remote_copy(
                src_ref=ins[a].at[:, pl.ds((1 - c) * h, h)], dst_ref=outs[a], send_sem=ssem.at[a], recv_sem=rsem.at[a],
                device_id=(x, y, 1 - c), device_id_type=MESH))
        for cp in cps:
            cp.start()
        for cp in cps:
            cp.wait()

    return pl.pallas_call(
        body, name="grad_sibling_swap",
        out_shape=tuple(jax.ShapeDtypeStruct((g.shape[0], g.shape[1] // 2, g.shape[2]), g.dtype) for g in grads),
        in_specs=[ANY] * n, out_specs=(ANY,) * n,
        scratch_shapes=[pltpu.SemaphoreType.DMA((n,)), pltpu.SemaphoreType.DMA((n,))],
    )(*grads)


def _chip_scatter(parts):
    n = len(parts)
    nf = len(CHIP_FLIPS)

    def body(*refs):
        ins, outs = refs[:n], refs[n:2 * n]
        ssem, rsem = refs[2 * n:]
        x, y, c = _place()
        sends = []
        for a in range(n):
            for j, flip in enumerate(CHIP_FLIPS):
                px, py, kj = _chip_of(x, y, flip)
                sends.append(pltpu.make_async_remote_copy(
                    src_ref=ins[a].at[kj], dst_ref=outs[a].at[j], send_sem=ssem.at[a * nf + j], recv_sem=rsem.at[a * nf + j],
                    device_id=(px, py, c), device_id_type=MESH))
        for cp in sends:
            cp.start()
        for cp in sends:
            cp.wait()

    return pl.pallas_call(
        body, name="grad_chip_scatter",
        out_shape=tuple(jax.ShapeDtypeStruct((nf,) + p.shape[1:], p.dtype) for p in parts),
        in_specs=[ANY] * n, out_specs=(ANY,) * n,
        scratch_shapes=[pltpu.SemaphoreType.DMA((nf * n,)), pltpu.SemaphoreType.DMA((nf * n,))],
    )(*parts)


def _sibling_send(halves):
    n = len(halves)

    def body(*refs):
        ins, outs = refs[:n], refs[n:2 * n]
        ssem, rsem = refs[2 * n:]
        x, y, c = _place()
        cps = [pltpu.make_async_remote_copy(src_ref=ins[a], dst_ref=outs[a], send_sem=ssem.at[a], recv_sem=rsem.at[a],
                                            device_id=(x, y, 1 - c), device_id_type=MESH) for a in range(n)]
        for cp in cps:
            cp.start()
        for cp in cps:
            cp.wait()

    return pl.pallas_call(
        body, name="grad_sibling_send",
        out_shape=tuple(jax.ShapeDtypeStruct(h.shape, h.dtype) for h in halves),
        in_specs=[ANY] * n, out_specs=(ANY,) * n,
        scratch_shapes=[pltpu.SemaphoreType.DMA((n,)), pltpu.SemaphoreType.DMA((n,))],
    )(*halves)


def _small_all_reduce(vec, loss_row, loss_scale):
    r, dm = vec.shape

    def body(v_ref, o_ref, l_ref, buf, ssem, rsem):
        x, y, c = _place()
        me = 4 * x + 2 * y + c
        buf[me] = v_ref[...]
        cps = []
        for mask in range(1, 8):
            fx, fy, fc = (mask >> 2) & 1, (mask >> 1) & 1, mask & 1
            cps.append(pltpu.make_async_remote_copy(
                src_ref=v_ref, dst_ref=buf.at[me], send_sem=ssem.at[mask - 1], recv_sem=rsem.at[mask - 1],
                device_id=(x ^ fx, y ^ fy, c ^ fc), device_id_type=MESH))
        for cp in cps:
            cp.start()
        for mask in range(1, 8):
            fx, fy, fc = (mask >> 2) & 1, (mask >> 1) & 1, mask & 1
            frm = 4 * (x ^ fx) + 2 * (y ^ fy) + (c ^ fc)
            pltpu.make_async_remote_copy(
                src_ref=v_ref, dst_ref=buf.at[frm], send_sem=ssem.at[mask - 1], recv_sem=rsem.at[mask - 1],
                device_id=(x, y, c), device_id_type=MESH).wait_recv()
        for cp in cps:
            cp.wait_send()
        acc = buf[0]
        for i in range(1, 8):
            acc = acc + buf[i]
        o_ref[...] = acc
        l_ref[...] = jnp.sum(acc[loss_row:loss_row + SMALL_ROWS, :], axis=(0, 1), keepdims=True) * loss_scale

    vm = pl.BlockSpec(memory_space=pltpu.VMEM)
    return pl.pallas_call(
        body, name="small_all_reduce",
        out_shape=(jax.ShapeDtypeStruct((r, dm), F32), jax.ShapeDtypeStruct((1, 1), F32)),
        in_specs=[vm], out_specs=(vm, vm),
        scratch_shapes=[pltpu.VMEM((8, r, dm), F32), pltpu.SemaphoreType.DMA((7,)), pltpu.SemaphoreType.DMA((7,))],
    )(vec)


def kernel(x, meta_tokens, norm_mix_g, w_in, b_gate, pool_w, pool_scale, conv_w, conv_out_w, w_o, norm_ffn_g, w_gate_up, w_down, norm_final_g, loss_target, m_meta_tokens, m_norm_mix_g, m_w_in, m_b_gate, m_pool_w, m_pool_scale, m_conv_w, m_conv_out_w, m_w_o, m_norm_ffn_g, m_w_gate_up, m_w_down, m_norm_final_g, v_meta_tokens, v_norm_mix_g, v_w_in, v_b_gate, v_pool_w, v_pool_scale, v_conv_w, v_conv_out_w, v_w_o, v_norm_ffn_g, v_w_gate_up, v_w_down, v_norm_final_g):
    seq, dm = x.shape[1], x.shape[2]
    tail = 256 if seq % 256 == 0 else LANES
    tm = tail
    lp = seq + tail
    n_chips = 4
    n_groups = len(POOL_WINDOWS)
    gw = dm // n_groups
    tc = min(256, gw)
    cx, cy, cc = _place()
    chip = 2 * cx + cy
    dloc = dm // n_chips

    pool2 = pool_w.reshape(n_groups * pool_w.shape[1], gw)
    big = {"w_in": w_in, "w_gate_up": w_gate_up, "pool_w": pool2, "conv_out_w": conv_out_w, "w_o": w_o, "w_down": w_down}
    chip1 = jnp.reshape(chip, (1,)).astype(jnp.int32)
    core = jnp.reshape(cc, (1,)).astype(jnp.int32)
    big_b = [_cast_into_slot("cast_" + nme, wv, chip1, BF16) for nme, wv in big.items()]
    small_loc = jnp.concatenate([meta_tokens, jnp.pad(conv_w, ((0, 8 - conv_w.shape[0]), (0, 0))),
                                 jnp.zeros((8, dloc), F32)], axis=0)
    gathered = _all_gather(big_b + [_cast_into_slot("place_small", small_loc, chip1, F32)])
    w_in4, w_gu4, pool4, conv_out4, w_o4, w_down4, small4 = gathered
    pool4 = pool4.reshape(n_chips, n_groups, gw // n_chips, gw)
    conv_out_f = conv_out4.reshape(dm, dm)
    w_o_f = w_o4.reshape(dm, dm)
    w_down_f = w_down4.reshape(-1, dm)
    small_f = jnp.transpose(small4, (1, 0, 2)).reshape(small4.shape[1], dm)
    meta_f = small_f[:N_META]
    conv_w_f = small_f[N_META:N_META + 3]

    h0 = jnp.concatenate([x[0], jnp.zeros((tail - N_META, dm), F32), meta_f], axis=0)
    g1, g2, g3 = norm_mix_g.reshape(1, dm), norm_ffn_g.reshape(1, dm), norm_final_g.reshape(1, dm)
    b_gate2 = b_gate.reshape(2, dm)
    ps = pool_scale.reshape(1, dm)
    hn1 = _rms_fwd("rms_mix", h0, g1, tm)
    proj = _nn_sharded("proj", hn1, w_in4, 6)
    pooled, z = _mixer_fwd("mixer_fwd", proj, conv_w_f, tc)
    ya = _pool_fwd("pool_proj", pooled, pool4)
    yb = _nn_plain("conv_out", z, conv_out_f, BF16)
    mix = _gate_mix("gate_mix", proj, b_gate2, ya, ps, yb, tm)
    h1 = _nn_plain("attn_out", mix, w_o_f, F32, res=h0, tn_pref=256)
    hn2 = _rms_fwd("rms_ffn", h1, g2, tm)
    gu = _nn_sharded("gate_up", hn2, w_gu4, 2)
    act = _swiglu_fwd("swiglu", gu, tm)
    h2 = _nn_plain("ffn_down", act, w_down_f, F32, res=h1, tn_pref=512, tk_pref=1536)
    dh2, dh2b, loss_cols, dg3 = _final_loss("final_loss", h2, g3, loss_target[0], tm)

    dact = _nt_plain("d_act", dh2b, w_down_f)
    gw_down = _tn_plain("dw_down", act, dh2b)
    dgu = _swiglu_bwd("swiglu_bwd", dact, gu, tm)
    gw_gu = _tn_sharded("dw_gate_up", hn2, dgu, n_chips)
    dhn2 = _nt_sharded("d_hn2", dgu, w_gu4)
    dh1, dh1b, dg2 = _rms_bwd("rms_ffn_bwd", dhn2, h1, g2, dh2, tm)
    dmix = _nt_plain("d_mix", dh1b, w_o_f)
    gw_o = _tn_plain("dw_o", mix, dh1b)
    dproj, dyb, dya, db_gate, dps = _gate_bwd("gate_bwd", dmix, proj, b_gate2, ya, ps, yb, tm)
    dz = _nt_plain("d_z", dyb, conv_out_f)
    gw_conv_out = _tn_plain("dw_conv_out", z, dyb)
    dpooled = _pool_bwd_act("d_pooled", dya, pool4)
    gw_pool = _pool_bwd_w("dw_pool", pooled, dya, n_chips)
    dproj, dconv_w = _mixer_bwd("mixer_bwd", dz, dpooled, proj, conv_w_f, dproj, tc)
    gw_in = _tn_sharded("dw_in", hn1, dproj, n_chips)
    dhn1 = _nt_sharded("d_hn1", dproj, w_in4)
    dh0, _, dg1 = _rms_bwd("rms_mix_bwd", dhn1, h0, g1, dh1, tm)
    grad_x = dh0[:seq][None]
    dmeta = dh0[lp - N_META:]

    grads4 = [gw_in, gw_gu, gw_pool.reshape(n_chips, n_groups * (gw // n_chips), gw), gw_conv_out.reshape(n_chips, dloc, dm),
              gw_o.reshape(n_chips, dloc, dm), gw_down.reshape(n_chips, -1, dm)]
    names = ["w_in", "w_gate_up", "pool_w", "conv_out_w", "w_o", "w_down"]
    from_sibling = _sibling_swap(grads4)
    chip_parts = [_pair_add("pair_add_" + nme, g4, rv, core) for nme, g4, rv in zip(names, grads4, from_sibling)]
    from_chips = _chip_scatter(chip_parts)
    halves = [_chip_sum("chip_sum_" + nme, p, rv, chip1) for nme, p, rv in zip(names, chip_parts, from_chips)]
    sib_halves = _sibling_send(halves)
    g_halves = dict(zip(names, zip(halves, sib_halves)))

    vec = jnp.concatenate([dg1, dg2, dg3, db_gate, dps, loss_cols, dconv_w, dmeta], axis=0)
    loss_row = 5 * SMALL_ROWS
    red, loss11 = _small_all_reduce(vec, loss_row, 0.5 / dm)
    loss = loss11[0, 0]
    col0 = chip * dloc
    g_small = {
        "norm_mix_g": red[0], "norm_ffn_g": red[SMALL_ROWS], "norm_final_g": red[2 * SMALL_ROWS],
        "b_gate": red[3 * SMALL_ROWS:3 * SMALL_ROWS + 2].reshape(-1), "pool_scale": red[4 * SMALL_ROWS],
        "conv_w": lax.dynamic_slice(red, (6 * SMALL_ROWS, col0), (3, dloc)),
        "meta_tokens": lax.dynamic_slice(red, (7 * SMALL_ROWS, col0), (N_META, dloc)),
    }

    given = dict(meta_tokens=(meta_tokens, m_meta_tokens, v_meta_tokens), norm_mix_g=(norm_mix_g, m_norm_mix_g, v_norm_mix_g),
                 w_in=(w_in, m_w_in, v_w_in), b_gate=(b_gate, m_b_gate, v_b_gate), pool_w=(pool_w, m_pool_w, v_pool_w),
                 pool_scale=(pool_scale, m_pool_scale, v_pool_scale), conv_w=(conv_w, m_conv_w, v_conv_w),
                 conv_out_w=(conv_out_w, m_conv_out_w, v_conv_out_w), w_o=(w_o, m_w_o, v_w_o),
                 norm_ffn_g=(norm_ffn_g, m_norm_ffn_g, v_norm_ffn_g), w_gate_up=(w_gate_up, m_w_gate_up, v_w_gate_up),
                 w_down=(w_down, m_w_down, v_w_down), norm_final_g=(norm_final_g, m_norm_final_g, v_norm_final_g))
    order = list(given.keys())
    grad, delta, new_m, new_v = {}, {}, {}, {}
    for nme in names:
        w, m, v = given[nme]
        g_own, g_sib = g_halves[nme]
        shape2 = (2 * g_own.shape[0], g_own.shape[1])
        res4 = _adamw_halves("adamw_" + nme, w.reshape(shape2), g_own, g_sib, m.reshape(shape2), v.reshape(shape2), core)
        grad[nme], delta[nme], new_m[nme], new_v[nme] = [t.reshape(w.shape) for t in res4]
    vec_names = ["norm_mix_g", "norm_ffn_g", "norm_final_g", "pool_scale"]

    def slab_vec(pick):
        rows = [pick(nme).reshape(1, dm) for nme in vec_names] + [pick("b_gate").reshape(2, dm), jnp.zeros((2, dm), F32)]
        return jnp.concatenate(rows, axis=0)

    def slab_col(pick):
        return jnp.concatenate([pick("meta_tokens"), pick("conv_w"), jnp.zeros((5, dloc), F32)], axis=0)

    for slab, tag in ((slab_vec, "vec"), (slab_col, "col")):
        d, nm, nv = _adamw("adamw_small_" + tag, slab(lambda nme: given[nme][0]), slab(lambda nme: g_small[nme]),
                           slab(lambda nme: given[nme][1]), slab(lambda nme: given[nme][2]))
        for out, res in ((delta, d), (new_m, nm), (new_v, nv)):
            if tag == "vec":
                for i, nme in enumerate(vec_names):
                    out[nme] = res[i]
                out["b_gate"] = res[4:6].reshape(-1)
            else:
                out["meta_tokens"] = res[:N_META]
                out["conv_w"] = res[N_META:N_META + 3]
    grad.update(g_small)
    return (loss, grad_x, *[grad[nme] for nme in order], *[delta[nme] for nme in order],
            *[new_m[nme] for nme in order], *[new_v[nme] for nme in order])
```

```python
import functools
import math

import jax
import jax.numpy as jnp
from jax import lax
from jax.experimental import pallas as pl
from jax.experimental.pallas import tpu as pltpu

F32 = jnp.float32
BF16 = jnp.bfloat16
N_META = 16
POOL_WINDOWS = (2, 4, 8, 16)
EPS = 1e-6
ADAM_LR, ADAM_B1, ADAM_B2, ADAM_EPS, ADAM_WD, ADAM_STEP = 0.001, 0.9, 0.999, 1e-08, 0.01, 10
LANES = 128
V7X_VMEM_BYTES = 64 * 1024 * 1024
VMEM_LIMIT = V7X_VMEM_BYTES - 8 * 1024 * 1024
MESH = pl.DeviceIdType.MESH
ANY = pl.BlockSpec(memory_space=pl.ANY)
CHIP_FLIPS = ((1, 0), (0, 1), (1, 1))
SMALL_ROWS = 8


def _pick(n, pref):
    best = None
    for t in range(LANES, min(n, pref) + 1, LANES):
        if n % t == 0:
            best = t
    assert best is not None, (n, pref)
    return best


def _params(n_axes=0):
    sem = ("arbitrary",) * n_axes if n_axes else None
    return pltpu.CompilerParams(dimension_semantics=sem, vmem_limit_bytes=VMEM_LIMIT)


_DIMS = {
    "nn": (((1,), (0,)), ((), ())),
    "nt": (((1,), (1,)), ((), ())),
    "tn": (((0,), (0,)), ((), ())),
}


def _matmul(name, mode, a, b, out_sds, grid, a_spec, b_spec, o_spec, nk, res=None, res_spec=None, acc_shape=None):
    out_dtype = out_sds.dtype
    in_place = nk > 1 and out_dtype == F32
    use_scratch = nk > 1 and not in_place
    rows = a_spec.block_shape[-2] if mode != "tn" else None
    chunk = _pick(rows, 768) if rows is not None and rows % LANES == 0 else rows

    def body(*refs):
        if res is not None:
            a_ref, b_ref, r_ref, o_ref, *scr = refs
        else:
            a_ref, b_ref, o_ref, *scr = refs
            r_ref = None
        k = pl.program_id(len(grid) - 1) if nk > 1 else None

        def emit(sl):
            if sl is None:
                part = lax.dot_general(a_ref[...], b_ref[...], _DIMS[mode], preferred_element_type=F32)
                idx = (slice(None), slice(None))
            else:
                part = lax.dot_general(a_ref[sl, :], b_ref[...], _DIMS[mode], preferred_element_type=F32)
                idx = (sl, slice(None))
            if nk == 1:
                if r_ref is not None:
                    part = part + r_ref[idx]
                o_ref[idx] = part.astype(out_dtype)
                return
            acc = scr[0] if use_scratch else o_ref

            @pl.when(k == 0)
            def _():
                first = part
                if r_ref is not None and in_place:
                    first = first + r_ref[idx]
                acc[idx] = first

            @pl.when(k > 0)
            def _():
                acc[idx] += part

            if use_scratch:

                @pl.when(k == nk - 1)
                def _():
                    o_ref[idx] = acc[idx].astype(out_dtype)

        if mode == "tn" or chunk == rows:
            emit(None)
        else:
            for m0 in range(0, rows, chunk):
                emit(pl.ds(m0, chunk))

    ins = [a, b] + ([res] if res is not None else [])
    in_specs = [a_spec, b_spec] + ([res_spec] if res is not None else [])
    scratch = [pltpu.VMEM(acc_shape, F32)] if use_scratch else []
    return pl.pallas_call(
        body, name=name, out_shape=out_sds, grid=grid, in_specs=in_specs, out_specs=o_spec,
        scratch_shapes=scratch, compiler_params=_params(len(grid)),
    )(*ins)


def _nn_sharded(name, a, w4, nseg):
    lp, kdim = a.shape
    s, _, nloc = w4.shape
    segw = s * nloc // nseg
    tn = _pick(math.gcd(nloc, segw), 1536)
    bw, bo = nloc // tn, segw // tn
    return _matmul(
        name, "nn", a, w4, jax.ShapeDtypeStruct((nseg, lp, segw), BF16), (s * bw,),
        pl.BlockSpec((lp, kdim), lambda j: (0, 0)),
        pl.BlockSpec((None, kdim, tn), lambda j: (j // bw, 0, j % bw)),
        pl.BlockSpec((None, lp, tn), lambda j: (j // bo, 0, j % bo)), 1)


def _nn_plain(name, a, w, out_dtype, res=None, tn_pref=512, tk_pref=2048):
    lp, kdim = a.shape
    n = w.shape[1]
    tn = _pick(n, tn_pref)
    tk = kdim if kdim <= tk_pref else _pick(kdim, tk_pref)
    nk = kdim // tk
    grid = (n // tn, nk) if nk > 1 else (n // tn,)
    if nk > 1:
        a_spec = pl.BlockSpec((lp, tk), lambda j, k: (0, k))
        w_spec = pl.BlockSpec((tk, tn), lambda j, k: (k, j))
        o_spec = pl.BlockSpec((lp, tn), lambda j, k: (0, j))
    else:
        a_spec = pl.BlockSpec((lp, tk), lambda j: (0, 0))
        w_spec = pl.BlockSpec((tk, tn), lambda j: (0, j))
        o_spec = pl.BlockSpec((lp, tn), lambda j: (0, j))
    return _matmul(name, "nn", a, w, jax.ShapeDtypeStruct((lp, n), out_dtype), grid, a_spec, w_spec, o_spec, nk,
                   res=res, res_spec=o_spec if res is not None else None, acc_shape=(lp, tn))


def _nt_plain(name, a, w, tn_pref=512):
    lp, kdim = a.shape
    n = w.shape[0]
    tn = _pick(n, tn_pref)
    return _matmul(
        name, "nt", a, w, jax.ShapeDtypeStruct((lp, n), BF16), (n // tn,),
        pl.BlockSpec((lp, kdim), lambda j: (0, 0)),
        pl.BlockSpec((tn, kdim), lambda j: (j, 0)),
        pl.BlockSpec((lp, tn), lambda j: (0, j)), 1)


def _nt_sharded(name, dseg, w4, to_pref=1024):
    nseg, lp, segw = dseg.shape
    s, kdim, nloc = w4.shape
    tr = _pick(math.gcd(nloc, segw), 1536)
    ba, bw = segw // tr, nloc // tr
    nr = s * bw
    to = _pick(kdim, to_pref)
    return _matmul(
        name, "nt", dseg, w4, jax.ShapeDtypeStruct((lp, kdim), F32), (kdim // to, nr),
        pl.BlockSpec((None, lp, tr), lambda j, r: (r // ba, 0, r % ba)),
        pl.BlockSpec((None, to, tr), lambda j, r: (r // bw, j, r % bw)),
        pl.BlockSpec((lp, to), lambda j, r: (0, j)), nr)


def _tn_plain(name, a, d, tk_pref=512):
    lp, kdim = a.shape
    n = d.shape[1]
    tk = _pick(kdim, tk_pref)
    return _matmul(
        name, "tn", a, d, jax.ShapeDtypeStruct((kdim, n), BF16), (kdim // tk,),
        pl.BlockSpec((lp, tk), lambda i: (0, i)),
        pl.BlockSpec((lp, n), lambda i: (0, 0)),
        pl.BlockSpec((tk, n), lambda i: (i, 0)), 1)


def _tn_sharded(name, a, dseg, s, tk_pref=512):
    lp, kdim = a.shape
    nseg, _, segw = dseg.shape
    nloc = nseg * segw // s
    tn = _pick(math.gcd(nloc, segw), 1536)
    bd, bo = segw // tn, nloc // tn
    tk = _pick(kdim, tk_pref)
    return _matmul(
        name, "tn", a, dseg, jax.ShapeDtypeStruct((s, kdim, nloc), BF16), (s * bo, kdim // tk),
        pl.BlockSpec((lp, tk), lambda j, i: (0, i)),
        pl.BlockSpec((None, lp, tn), lambda j, i: (j // bd, 0, j % bd)),
        pl.BlockSpec((None, tk, tn), lambda j, i: (j // bo, i, j % bo)), 1)


def _pool_fwd(name, pooled, pw4):
    lp, dm = pooled.shape
    s, g, rs, gw = pw4.shape
    return _matmul(
        name, "nn", pooled, pw4, jax.ShapeDtypeStruct((lp, dm), BF16), (g, s),
        pl.BlockSpec((lp, rs), lambda gi, si: (0, gi * s + si)),
        pl.BlockSpec((None, None, rs, gw), lambda gi, si: (si, gi, 0, 0)),
        pl.BlockSpec((lp, gw), lambda gi, si: (0, gi)), s, acc_shape=(lp, gw))


def _pool_bwd_act(name, dya, pw4):
    lp, dm = dya.shape
    s, g, rs, gw = pw4.shape
    return _matmul(
        name, "nt", dya, pw4, jax.ShapeDtypeStruct((lp, dm), BF16), (g, s),
        pl.BlockSpec((lp, gw), lambda gi, si: (0, gi)),
        pl.BlockSpec((None, None, rs, gw), lambda gi, si: (si, gi, 0, 0)),
        pl.BlockSpec((lp, rs), lambda gi, si: (0, gi * s + si)), 1)


def _pool_bwd_w(name, pooled, dya, s):
    lp, dm = pooled.shape
    g = len(POOL_WINDOWS)
    gw = dm // g
    rs = gw // s
    return _matmul(
        name, "tn", pooled, dya, jax.ShapeDtypeStruct((s, g, rs, gw), BF16), (g, s),
        pl.BlockSpec((lp, rs), lambda gi, si: (0, gi * s + si)),
        pl.BlockSpec((lp, gw), lambda gi, si: (0, gi)),
        pl.BlockSpec((None, None, rs, gw), lambda gi, si: (si, gi, 0, 0)), 1)


def _rms_fwd(name, h, g, tm):
    lp, dm = h.shape

    def body(h_ref, g_ref, o_ref):
        hv = h_ref[...]
        r = lax.rsqrt(jnp.mean(hv * hv, axis=-1, keepdims=True) + EPS)
        o_ref[...] = (hv * r * g_ref[...]).astype(BF16)

    row = pl.BlockSpec((tm, dm), lambda i: (i, 0))
    return pl.pallas_call(
        body, name=name, out_shape=jax.ShapeDtypeStruct((lp, dm), BF16), grid=(lp // tm,),
        in_specs=[row, pl.BlockSpec((1, dm), lambda i: (0, 0))], out_specs=row, compiler_params=_params(1),
    )(h, g)


def _rms_bwd(name, dy, h, g, dres, tm):
    lp, dm = h.shape

    def body(dy_ref, h_ref, g_ref, dr_ref, dh_ref, dhb_ref, dg_ref):
        hv = h_ref[...]
        r = lax.rsqrt(jnp.mean(hv * hv, axis=-1, keepdims=True) + EPS)
        xhat = hv * r
        dyv = dy_ref[...]
        dxh = dyv * g_ref[...]
        dh = dr_ref[...] + r * (dxh - xhat * jnp.mean(dxh * xhat, axis=-1, keepdims=True))
        dh_ref[...] = dh
        dhb_ref[...] = dh.astype(BF16)

        @pl.when(pl.program_id(0) == 0)
        def _():
            dg_ref[...] = jnp.zeros_like(dg_ref)

        dg_ref[0:1, :] += jnp.sum(dyv * xhat, axis=0, keepdims=True)

    row = pl.BlockSpec((tm, dm), lambda i: (i, 0))
    slab = pl.BlockSpec((SMALL_ROWS, dm), lambda i: (0, 0))
    return pl.pallas_call(
        body, name=name, grid=(lp // tm,),
        out_shape=(jax.ShapeDtypeStruct((lp, dm), F32), jax.ShapeDtypeStruct((lp, dm), BF16),
                   jax.ShapeDtypeStruct((SMALL_ROWS, dm), F32)),
        in_specs=[row, row, pl.BlockSpec((1, dm), lambda i: (0, 0)), row], out_specs=(row, row, slab),
        compiler_params=_params(1),
    )(dy, h, g, dres)


def _gate_mix(name, proj, b_gate2, ya, pool_scale, yb, tm):
    _, lp, dm = proj.shape

    def body(ga_ref, gr_ref, b_ref, ya_ref, ps_ref, yb_ref, o_ref):
        g_a = jax.nn.sigmoid(ga_ref[...].astype(F32) + b_ref[0:1, :])
        g_b = jax.nn.sigmoid(gr_ref[...].astype(F32) + b_ref[1:2, :])
        y_a = ya_ref[...].astype(F32) * ps_ref[...]
        o_ref[...] = (g_a * y_a + g_b * yb_ref[...].astype(F32)).astype(BF16)

    row = pl.BlockSpec((tm, dm), lambda i: (i, 0))
    return pl.pallas_call(
        body, name=name, out_shape=jax.ShapeDtypeStruct((lp, dm), BF16), grid=(lp // tm,),
        in_specs=[pl.BlockSpec((None, tm, dm), lambda i: (4, i, 0)), pl.BlockSpec((None, tm, dm), lambda i: (5, i, 0)),
                  pl.BlockSpec((2, dm), lambda i: (0, 0)), row, pl.BlockSpec((1, dm), lambda i: (0, 0)), row],
        out_specs=row, compiler_params=_params(1),
    )(proj, proj, b_gate2, ya, pool_scale, yb)


def _gate_bwd(name, dmix, proj, b_gate2, ya, pool_scale, yb, tm):
    _, lp, dm = proj.shape

    def body(dm_ref, ga_ref, gr_ref, b_ref, ya_ref, ps_ref, yb_ref, dp_ref, dyb_ref, dya_ref, db_ref, dps_ref):
        dmx = dm_ref[...].astype(F32)
        g_a = jax.nn.sigmoid(ga_ref[...].astype(F32) + b_ref[0:1, :])
        g_b = jax.nn.sigmoid(gr_ref[...].astype(F32) + b_ref[1:2, :])
        ya_pre = ya_ref[...].astype(F32)
        ybv = yb_ref[...].astype(F32)
        ps = ps_ref[...]
        dga = dmx * (ya_pre * ps) * (g_a * (1.0 - g_a))
        dgr = dmx * ybv * (g_b * (1.0 - g_b))
        dp_ref[0] = dga.astype(BF16)
        dp_ref[1] = dgr.astype(BF16)
        dyb_ref[...] = (dmx * g_b).astype(BF16)
        dya_ref[...] = (dmx * g_a * ps).astype(BF16)

        @pl.when(pl.program_id(0) == 0)
        def _():
            db_ref[...] = jnp.zeros_like(db_ref)
            dps_ref[...] = jnp.zeros_like(dps_ref)

        db_ref[0:1, :] += jnp.sum(dga, axis=0, keepdims=True)
        db_ref[1:2, :] += jnp.sum(dgr, axis=0, keepdims=True)
        dps_ref[0:1, :] += jnp.sum(dmx * g_a * ya_pre, axis=0, keepdims=True)

    row = pl.BlockSpec((tm, dm), lambda i: (i, 0))
    one = pl.BlockSpec((1, dm), lambda i: (0, 0))
    slab = pl.BlockSpec((SMALL_ROWS, dm), lambda i: (0, 0))
    return pl.pallas_call(
        body, name=name, grid=(lp // tm,),
        out_shape=(jax.ShapeDtypeStruct((6, lp, dm), BF16), jax.ShapeDtypeStruct((lp, dm), BF16),
                   jax.ShapeDtypeStruct((lp, dm), BF16), jax.ShapeDtypeStruct((SMALL_ROWS, dm), F32),
                   jax.ShapeDtypeStruct((SMALL_ROWS, dm), F32)),
        in_specs=[row, pl.BlockSpec((None, tm, dm), lambda i: (4, i, 0)), pl.BlockSpec((None, tm, dm), lambda i: (5, i, 0)),
                  pl.BlockSpec((2, dm), lambda i: (0, 0)), row, one, row],
        out_specs=(pl.BlockSpec((2, tm, dm), lambda i: (2, i, 0)), row, row, slab, slab),
        compiler_params=_params(1),
    )(dmix, proj, proj, b_gate2, ya, pool_scale, yb)


def _swiglu_fwd(name, gu, tm):
    _, lp, f = gu.shape

    def body(g_ref, u_ref, o_ref):
        gt = g_ref[...].astype(F32)
        o_ref[...] = (gt * jax.nn.sigmoid(gt) * u_ref[...].astype(F32)).astype(BF16)

    return pl.pallas_call(
        body, name=name, out_shape=jax.ShapeDtypeStruct((lp, f), BF16), grid=(lp // tm,),
        in_specs=[pl.BlockSpec((None, tm, f), lambda i: (0, i, 0)), pl.BlockSpec((None, tm, f), lambda i: (1, i, 0))],
        out_specs=pl.BlockSpec((tm, f), lambda i: (i, 0)), compiler_params=_params(1),
    )(gu, gu)


def _swiglu_bwd(name, dact, gu, tm):
    _, lp, f = gu.shape

    def body(d_ref, g_ref, u_ref, o_ref):
        d = d_ref[...].astype(F32)
        gt = g_ref[...].astype(F32)
        sg = jax.nn.sigmoid(gt)
        o_ref[0] = (d * u_ref[...].astype(F32) * (sg * (1.0 + gt * (1.0 - sg)))).astype(BF16)
        o_ref[1] = (d * (gt * sg)).astype(BF16)

    return pl.pallas_call(
        body, name=name, out_shape=jax.ShapeDtypeStruct((2, lp, f), BF16), grid=(lp // tm,),
        in_specs=[pl.BlockSpec((tm, f), lambda i: (i, 0)), pl.BlockSpec((None, tm, f), lambda i: (0, i, 0)),
                  pl.BlockSpec((None, tm, f), lambda i: (1, i, 0))],
        out_specs=pl.BlockSpec((2, tm, f), lambda i: (0, i, 0)), compiler_params=_params(1),
    )(dact, gu, gu)


def _final_loss(name, h2, g3, target, tm):
    lp, dm = h2.shape
    nx = target.shape[0] // tm

    def body(h_ref, g_ref, t_ref, dh_ref, dhb_ref, ls_ref, dg_ref):
        i = pl.program_id(0)

        @pl.when(i == 0)
        def _():
            ls_ref[...] = jnp.zeros_like(ls_ref)
            dg_ref[...] = jnp.zeros_like(dg_ref)

        @pl.when(i < nx)
        def _():
            hv = h_ref[...]
            gv = g_ref[...]
            r = lax.rsqrt(jnp.mean(hv * hv, axis=-1, keepdims=True) + EPS)
            xhat = hv * r
            err = xhat * gv - t_ref[...]
            dout = err * (1.0 / dm)
            dxh = dout * gv
            dh = r * (dxh - xhat * jnp.mean(dxh * xhat, axis=-1, keepdims=True))
            dh_ref[...] = dh
            dhb_ref[...] = dh.astype(BF16)
            ls_ref[0:1, :] += jnp.sum(err * err, axis=0, keepdims=True)
            dg_ref[0:1, :] += jnp.sum(dout * xhat, axis=0, keepdims=True)

        @pl.when(i >= nx)
        def _():
            dh_ref[...] = jnp.zeros_like(dh_ref)
            dhb_ref[...] = jnp.zeros_like(dhb_ref)

    row = pl.BlockSpec((tm, dm), lambda i: (i, 0))
    slab = pl.BlockSpec((SMALL_ROWS, dm), lambda i: (0, 0))
    return pl.pallas_call(
        body, name=name, grid=(lp // tm,),
        out_shape=(jax.ShapeDtypeStruct((lp, dm), F32), jax.ShapeDtypeStruct((lp, dm), BF16),
                   jax.ShapeDtypeStruct((SMALL_ROWS, dm), F32), jax.ShapeDtypeStruct((SMALL_ROWS, dm), F32)),
        in_specs=[row, pl.BlockSpec((1, dm), lambda i: (0, 0)), pl.BlockSpec((tm, dm), lambda i: (jnp.minimum(i, nx - 1), 0))],
        out_specs=(row, row, slab, slab), compiler_params=_params(1),
    )(h2, g3, target)


def _shift(v, k):
    return pltpu.roll(v, k % v.shape[0], axis=0)


def _window_sum(v, group, sign):
    s2 = v + _shift(v, sign * 1)
    s4 = s2 + _shift(s2, sign * 2)
    s8 = s4 + _shift(s4, sign * 4)
    s16 = s8 + _shift(s8, sign * 8)
    return jnp.where(group == 0, s2, jnp.where(group == 1, s4, jnp.where(group == 2, s8, s16)))


def _pool_count(lp, group):
    row = lax.broadcasted_iota(jnp.int32, (lp, 1), 0)
    window = jnp.left_shift(2, group).astype(F32)
    meta_pos = (row - (lp - N_META) + 1).astype(F32)
    return jnp.where(row >= lp - N_META, jnp.minimum(meta_pos, window), window)


def _mixer_fwd(name, proj, conv_w, tc):
    _, lp, dm = proj.shape
    per_group = dm // len(POOL_WINDOWS) // tc

    def body(u_ref, gb_ref, gc_ref, v_ref, cw_ref, p_ref, z_ref):
        group = pl.program_id(0) // per_group
        u = u_ref[...].astype(F32)
        p_ref[...] = (_window_sum(u, group, 1) / _pool_count(lp, group) - u).astype(BF16)
        cv = gc_ref[...].astype(F32) * v_ref[...].astype(F32)
        conv = cw_ref[0:1, :] * _shift(cv, 2) + cw_ref[1:2, :] * _shift(cv, 1) + cw_ref[2:3, :] * cv
        z_ref[...] = (gb_ref[...].astype(F32) * conv).astype(BF16)

    def seg(s):
        return pl.BlockSpec((None, lp, tc), lambda j: (s, 0, j))

    col = pl.BlockSpec((lp, tc), lambda j: (0, j))
    return pl.pallas_call(
        body, name=name, grid=(dm // tc,),
        out_shape=(jax.ShapeDtypeStruct((lp, dm), BF16), jax.ShapeDtypeStruct((lp, dm), BF16)),
        in_specs=[seg(0), seg(1), seg(2), seg(3), pl.BlockSpec((3, tc), lambda j: (0, j))],
        out_specs=(col, col), compiler_params=_params(1),
    )(proj, proj, proj, proj, conv_w)


def _mixer_bwd(name, dz, dpooled, proj, conv_w, dproj, tc):
    _, lp, dm = proj.shape
    per_group = dm // len(POOL_WINDOWS) // tc

    def body(dz_ref, dp_ref, gb_ref, gc_ref, v_ref, cw_ref, _, o_ref, dcw_ref):
        group = pl.program_id(0) // per_group
        dzv = dz_ref[...].astype(F32)
        gb = gb_ref[...].astype(F32)
        gc = gc_ref[...].astype(F32)
        vv = v_ref[...].astype(F32)
        cv = gc * vv
        c1 = _shift(cv, 1)
        c2 = _shift(cv, 2)
        w0, w1, w2 = cw_ref[0:1, :], cw_ref[1:2, :], cw_ref[2:3, :]
        o_ref[1] = (dzv * (w0 * c2 + w1 * c1 + w2 * cv)).astype(BF16)
        dconv = dzv * gb
        dcw_ref[...] = jnp.zeros_like(dcw_ref)
        dcw_ref[0:1, :] = jnp.sum(dconv * c2, axis=0, keepdims=True)
        dcw_ref[1:2, :] = jnp.sum(dconv * c1, axis=0, keepdims=True)
        dcw_ref[2:3, :] = jnp.sum(dconv * cv, axis=0, keepdims=True)
        dcv = w0 * _shift(dconv, -2) + w1 * _shift(dconv, -1) + w2 * dconv
        o_ref[2] = (dcv * vv).astype(BF16)
        o_ref[3] = (dcv * gc).astype(BF16)
        dpv = dp_ref[...].astype(F32)
        o_ref[0] = (_window_sum(dpv / _pool_count(lp, group), group, -1) - dpv).astype(BF16)

    def seg(s):
        return pl.BlockSpec((None, lp, tc), lambda j: (s, 0, j))

    col = pl.BlockSpec((lp, tc), lambda j: (0, j))
    return pl.pallas_call(
        body, name=name, grid=(dm // tc,),
        out_shape=(jax.ShapeDtypeStruct(dproj.shape, BF16), jax.ShapeDtypeStruct((SMALL_ROWS, dm), F32)),
        in_specs=[col, col, seg(1), seg(2), seg(3), pl.BlockSpec((3, tc), lambda j: (0, j)), ANY],
        out_specs=(pl.BlockSpec((4, lp, tc), lambda j: (0, 0, j)), pl.BlockSpec((SMALL_ROWS, tc), lambda j: (0, j))),
        input_output_aliases={6: 0}, compiler_params=_params(1),
    )(dz, dpooled, proj, proj, proj, conv_w, dproj)


def _row_tile(r, c, bytes_per_row_elem=4, budget=2 * 1024 * 1024):
    best = None
    for t in range(16, r + 1, 16):
        if r % t == 0 and t * c * bytes_per_row_elem <= budget:
            best = t
    return best if best is not None else r


def _pair_add(name, g4, recv, core):
    s, r, c = g4.shape
    h = r // 2
    tr = _row_tile(h, c)
    nb = h // tr

    def body(core_ref, g_ref, r_ref, o_ref):
        o_ref[...] = (g_ref[...].astype(F32) + r_ref[...].astype(F32)).astype(BF16)

    grid_spec = pltpu.PrefetchScalarGridSpec(
        num_scalar_prefetch=1, grid=(s, nb),
        in_specs=[pl.BlockSpec((None, tr, c), lambda si, j, core_ref: (si, core_ref[0] * nb + j, 0)),
                  pl.BlockSpec((None, tr, c), lambda si, j, core_ref: (si, j, 0))],
        out_specs=pl.BlockSpec((None, tr, c), lambda si, j, core_ref: (si, j, 0)))
    return pl.pallas_call(
        body, name=name, out_shape=jax.ShapeDtypeStruct((s, h, c), BF16), grid_spec=grid_spec,
        compiler_params=_params(2),
    )(core, g4, recv)


def _chip_sum(name, parts, recv, chip):
    _, h, c = parts.shape
    tr = _row_tile(h, c)

    def body(chip_ref, p_ref, r_ref, o_ref):
        acc = p_ref[...].astype(F32)
        for i in range(len(CHIP_FLIPS)):
            acc = acc + r_ref[i].astype(F32)
        o_ref[...] = acc

    grid_spec = pltpu.PrefetchScalarGridSpec(
        num_scalar_prefetch=1, grid=(h // tr,),
        in_specs=[pl.BlockSpec((None, tr, c), lambda j, chip_ref: (chip_ref[0], j, 0)),
                  pl.BlockSpec((len(CHIP_FLIPS), tr, c), lambda j, chip_ref: (0, j, 0))],
        out_specs=pl.BlockSpec((tr, c), lambda j, chip_ref: (j, 0)))
    return pl.pallas_call(
        body, name=name, out_shape=jax.ShapeDtypeStruct((h, c), F32), grid_spec=grid_spec, compiler_params=_params(1),
    )(chip, parts, recv)


def _adam_update(w, gv, m, v):
    c1 = 1.0 - ADAM_B1 ** ADAM_STEP
    c2 = 1.0 - ADAM_B2 ** ADAM_STEP
    nm = ADAM_B1 * m + (1.0 - ADAM_B1) * gv
    nv = ADAM_B2 * v + (1.0 - ADAM_B2) * (gv * gv)
    return -ADAM_LR * ((nm / c1) / (jnp.sqrt(nv / c2) + ADAM_EPS) + ADAM_WD * w), nm, nv


def _adamw_halves(name, w, g_own, g_sib, m, v, core):
    r, c = w.shape
    h = r // 2
    tr = _row_tile(h, c, budget=1024 * 1024)
    nbh = h // tr

    def body(core_ref, w_ref, go_ref, gs_ref, m_ref, v_ref, g_ref, d_ref, nm_ref, nv_ref):
        mine = (pl.program_id(0) // nbh) == core_ref[0]
        gv = jnp.where(mine, go_ref[...], gs_ref[...])
        g_ref[...] = gv
        d_ref[...], nm_ref[...], nv_ref[...] = _adam_update(w_ref[...], gv, m_ref[...], v_ref[...])

    def blk(fn):
        return pl.BlockSpec((tr, c), fn)

    full = blk(lambda j, core_ref: (j, 0))
    own = blk(lambda j, core_ref: (jnp.clip(j - core_ref[0] * nbh, 0, nbh - 1), 0))
    sib = blk(lambda j, core_ref: (jnp.clip(j - (1 - core_ref[0]) * nbh, 0, nbh - 1), 0))
    grid_spec = pltpu.PrefetchScalarGridSpec(
        num_scalar_prefetch=1, grid=(r // tr,), in_specs=[full, own, sib, full, full], out_specs=(full,) * 4)
    sds = jax.ShapeDtypeStruct((r, c), F32)
    return pl.pallas_call(
        body, name=name, out_shape=(sds,) * 4, grid_spec=grid_spec, compiler_params=_params(1),
    )(core, w, g_own, g_sib, m, v)


def _adamw(name, w, g, m, v):
    r, c = w.shape

    def body(w_ref, g_ref, m_ref, v_ref, d_ref, nm_ref, nv_ref):
        d_ref[...], nm_ref[...], nv_ref[...] = _adam_update(w_ref[...], g_ref[...], m_ref[...], v_ref[...])

    blk = pl.BlockSpec((r, c), lambda j: (0, 0))
    sds = jax.ShapeDtypeStruct((r, c), F32)
    return pl.pallas_call(
        body, name=name, out_shape=(sds, sds, sds), grid=(1,), in_specs=[blk] * 4, out_specs=(blk,) * 3,
        compiler_params=_params(1),
    )(w, g, m, v)


def _cast_into_slot(name, w, chip, dtype):
    r, c = w.shape
    tr = _row_tile(r, c)

    def body(chip_ref, w_ref, o_ref):
        o_ref[...] = w_ref[...].astype(dtype)

    grid_spec = pltpu.PrefetchScalarGridSpec(
        num_scalar_prefetch=1, grid=(r // tr,),
        in_specs=[pl.BlockSpec((tr, c), lambda j, chip_ref: (j, 0))],
        out_specs=pl.BlockSpec((None, tr, c), lambda j, chip_ref: (chip_ref[0], j, 0)))
    return pl.pallas_call(
        body, name=name, out_shape=jax.ShapeDtypeStruct((4, r, c), dtype), grid_spec=grid_spec, compiler_params=_params(1),
    )(chip, w)


def _place():
    return lax.axis_index("x"), lax.axis_index("y"), lax.axis_index("c")


def _chip_of(x, y, flip):
    px, py = x ^ flip[0], y ^ flip[1]
    return px, py, 2 * px + py


def _half(ref, which):
    rows = ref.shape[0] // 2
    return ref.at[pl.ds(which * rows, rows)]


def _all_gather(slabs):
    n = len(slabs)

    def body(*refs):
        outs = refs[n:2 * n]
        ssem, rsem = refs[2 * n:]
        x, y, c = _place()
        k = 2 * x + y

        def copy(a, slot, ref, to):
            return pltpu.make_async_remote_copy(src_ref=ref, dst_ref=ref, send_sem=ssem.at[a * 6 + slot],
                                                recv_sem=rsem.at[a * 6 + slot], device_id=to, device_id_type=MESH)

        sends = []
        for a in range(n):
            for j, flip in enumerate(CHIP_FLIPS):
                px, py, _ = _chip_of(x, y, flip)
                sends.append(copy(a, j, _half(outs[a].at[k], c), (px, py, c)))
        for cp in sends:
            cp.start()
        for a in range(n):
            for j, flip in enumerate(CHIP_FLIPS):
                _, _, kj = _chip_of(x, y, flip)
                landed = _half(outs[a].at[kj], c)
                copy(a, j, landed, (x, y, c)).wait_recv()
                passed = copy(a, 3 + j, landed, (x, y, 1 - c))
                passed.start()
                sends.append(passed)
        for a in range(n):
            for j, flip in enumerate(CHIP_FLIPS):
                _, _, kj = _chip_of(x, y, flip)
                copy(a, 3 + j, _half(outs[a].at[kj], 1 - c), (x, y, c)).wait_recv()
        for cp in sends:
            cp.wait_send()

    return pl.pallas_call(
        body, name="all_gather_weights",
        out_shape=tuple(jax.ShapeDtypeStruct(t.shape, t.dtype) for t in slabs),
        in_specs=[ANY] * n, out_specs=(ANY,) * n, input_output_aliases={a: a for a in range(n)},
        scratch_shapes=[pltpu.SemaphoreType.DMA((6 * n,)), pltpu.SemaphoreType.DMA((6 * n,))],
    )(*slabs)


HBM = pl.BlockSpec(memory_space=pltpu.HBM)
SEM = pl.BlockSpec(memory_space=pltpu.SEMAPHORE)
SPLIT_COPY = pltpu.CompilerParams(has_side_effects=pltpu.SideEffectType.DATAFLOW_SIDE_EFFECTING)


def _in_hbm(arrays):
    return [pltpu.with_memory_space_constraint(t, pltpu.HBM) for t in arrays]


def _gather_start(slabs, groups):
    n = len(slabs)
    ng = len(groups)
    nf = len(CHIP_FLIPS)

    def body(*refs):
        sems, outs = refs[n:n + 2 * ng], refs[n + 2 * ng:]
        x, y, c = _place()
        k = 2 * x + y
        for g, members in enumerate(groups):
            for i, a in enumerate(members):
                for j, flip in enumerate(CHIP_FLIPS):
                    px, py, _ = _chip_of(x, y, flip)
                    mine = _half(outs[a].at[k], c)
                    pltpu.make_async_remote_copy(
                        src_ref=mine, dst_ref=mine, send_sem=sems[2 * g].at[i * nf + j], recv_sem=sems[2 * g + 1].at[i * nf + j],
                        device_id=(px, py, c), device_id_type=MESH).start()

    sem_shapes = []
    for members in groups:
        sem_shapes += [pltpu.SemaphoreType.DMA((nf * len(members),))] * 2
    res = pl.pallas_call(
        body, name="gather_start",
        out_shape=tuple(sem_shapes) + tuple(pltpu.HBM(t.shape, t.dtype) for t in slabs),
        in_specs=[HBM] * n, out_specs=tuple([SEM] * (2 * ng) + [HBM] * n),
        input_output_aliases={a: 2 * ng + a for a in range(n)}, compiler_params=SPLIT_COPY,
    )(*_in_hbm(slabs))
    return [(res[2 * g], res[2 * g + 1]) for g in range(ng)], list(res[2 * ng:])


def _gather_wait(name, slabs, sems, after):
    n = len(slabs)
    nf = len(CHIP_FLIPS)

    def body(*refs):
        ins = refs[:n]
        ssem, rsem = refs[n], refs[n + 1]
        x, y, c = _place()
        k = 2 * x + y
        for a in range(n):
            for j, flip in enumerate(CHIP_FLIPS):
                _, _, kj = _chip_of(x, y, flip)
                cp = pltpu.make_async_remote_copy(
                    src_ref=_half(ins[a].at[k], c), dst_ref=_half(ins[a].at[kj], c), send_sem=ssem.at[a * nf + j],
                    recv_sem=rsem.at[a * nf + j], device_id=(x, y, c), device_id_type=MESH)
                cp.wait_send()
                cp.wait_recv()

    return pl.pallas_call(
        body, name=name, out_shape=tuple(pltpu.HBM(t.shape, t.dtype) for t in slabs),
        in_specs=[HBM] * n + [SEM, SEM, ANY], out_specs=tuple([HBM] * n),
        input_output_aliases={a: a for a in range(n)}, compiler_params=SPLIT_COPY,
    )(*slabs, sems[0], sems[1], after)


def _sibling_exchange(name, slabs):
    n = len(slabs)
    nf = len(CHIP_FLIPS)

    def body(*refs):
        outs = refs[n:2 * n]
        ssem, rsem = refs[2 * n:]
        x, y, c = _place()

        def copy(a, j, which, to):
            _, _, kj = _chip_of(x, y, CHIP_FLIPS[j])
            ref = _half(outs[a].at[kj], which)
            return pltpu.make_async_remote_copy(src_ref=ref, dst_ref=ref, send_sem=ssem.at[a * nf + j],
                                                recv_sem=rsem.at[a * nf + j], device_id=to, device_id_type=MESH)

        sends = [copy(a, j, c, (x, y, 1 - c)) for a in range(n) for j in range(nf)]
        for cp in sends:
            cp.start()
        for a in range(n):
            for j in range(nf):
                copy(a, j, 1 - c, (x, y, c)).wait_recv()
        for cp in sends:
            cp.wait_send()

    return pl.pallas_call(
        body, name=name, out_shape=tuple(jax.ShapeDtypeStruct(t.shape, t.dtype) for t in slabs),
        in_specs=[ANY] * n, out_specs=(ANY,) * n, input_output_aliases={a: a for a in range(n)},
        scratch_shapes=[pltpu.SemaphoreType.DMA((nf * n,)), pltpu.SemaphoreType.DMA((nf * n,))],
    )(*slabs)


def _sibling_swap(grads):
    n = len(grads)

    def body(*refs):
        ins, outs = refs[:n], refs[n:2 * n]
        ssem, rsem = refs[2 * n:]
        x, y, c = _place()
        cps = []
        for a in range(n):
            h = ins[a].shape[1] // 2
            cps.append(pltpu.make_async_remote_copy(
                src_ref=ins[a].at[:, pl.ds((1 - c) * h, h)], dst_ref=outs[a], send_sem=ssem.at[a], recv_sem=rsem.at[a],
                device_id=(x, y, 1 - c), device_id_type=MESH))
        for cp in cps:
            cp.start()
        for cp in cps:
            cp.wait()

    return pl.pallas_call(
        body, name="grad_sibling_swap",
        out_shape=tuple(jax.ShapeDtypeStruct((g.shape[0], g.shape[1] // 2, g.shape[2]), g.dtype) for g in grads),
        in_specs=[ANY] * n, out_specs=(ANY,) * n,
        scratch_shapes=[pltpu.SemaphoreType.DMA((n,)), pltpu.SemaphoreType.DMA((n,))],
    )(*grads)


def _chip_scatter(parts):
    n = len(parts)
    nf = len(CHIP_FLIPS)

    def body(*refs):
        ins, outs = refs[:n], refs[n:2 * n]
        ssem, rsem = refs[2 * n:]
        x, y, c = _place()
        sends = []
        for a in range(n):
            for j, flip in enumerate(CHIP_FLIPS):
                px, py, kj = _chip_of(x, y, flip)
                sends.append(pltpu.make_async_remote_copy(
                    src_ref=ins[a].at[kj], dst_ref=outs[a].at[j], send_sem=ssem.at[a * nf + j], recv_sem=rsem.at[a * nf + j],
                    device_id=(px, py, c), device_id_type=MESH))
        for cp in sends:
            cp.start()
        for cp in sends:
            cp.wait()

    return pl.pallas_call(
        body, name="grad_chip_scatter",
        out_shape=tuple(jax.ShapeDtypeStruct((nf,) + p.shape[1:], p.dtype) for p in parts),
        in_specs=[ANY] * n, out_specs=(ANY,) * n,
        scratch_shapes=[pltpu.SemaphoreType.DMA((nf * n,)), pltpu.SemaphoreType.DMA((nf * n,))],
    )(*parts)


def _sibling_send(halves):
    n = len(halves)

    def body(*refs):
        ins, outs = refs[:n], refs[n:2 * n]
        ssem, rsem = refs[2 * n:]
        x, y, c = _place()
        cps = [pltpu.make_async_remote_copy(src_ref=ins[a], dst_ref=outs[a], send_sem=ssem.at[a], recv_sem=rsem.at[a],
                                            device_id=(x, y, 1 - c), device_id_type=MESH) for a in range(n)]
        for cp in cps:
            cp.start()
        for cp in cps:
            cp.wait()

    return pl.pallas_call(
        body, name="grad_sibling_send",
        out_shape=tuple(jax.ShapeDtypeStruct(h.shape, h.dtype) for h in halves),
        in_specs=[ANY] * n, out_specs=(ANY,) * n,
        scratch_shapes=[pltpu.SemaphoreType.DMA((n,)), pltpu.SemaphoreType.DMA((n,))],
    )(*halves)


def _small_all_reduce(vec, loss_row, loss_scale):
    r, dm = vec.shape

    def body(v_ref, o_ref, l_ref, buf, ssem, rsem):
        x, y, c = _place()
        me = 4 * x + 2 * y + c
        buf[me] = v_ref[...]
        cps = []
        for mask in range(1, 8):
            fx, fy, fc = (mask >> 2) & 1, (mask >> 1) & 1, mask & 1
            cps.append(pltpu.make_async_remote_copy(
                src_ref=v_ref, dst_ref=buf.at[me], send_sem=ssem.at[mask - 1], recv_sem=rsem.at[mask - 1],
                device_id=(x ^ fx, y ^ fy, c ^ fc), device_id_type=MESH))
        for cp in cps:
            cp.start()
        for mask in range(1, 8):
            fx, fy, fc = (mask >> 2) & 1, (mask >> 1) & 1, mask & 1
            frm = 4 * (x ^ fx) + 2 * (y ^ fy) + (c ^ fc)
            pltpu.make_async_remote_copy(
                src_ref=v_ref, dst_ref=buf.at[frm], send_sem=ssem.at[mask - 1], recv_sem=rsem.at[mask - 1],
                device_id=(x, y, c), device_id_type=MESH).wait_recv()
        for cp in cps:
            cp.wait_send()
        acc = buf[0]
        for i in range(1, 8):
            acc = acc + buf[i]
        o_ref[...] = acc
        l_ref[...] = jnp.sum(acc[loss_row:loss_row + SMALL_ROWS, :], axis=(0, 1), keepdims=True) * loss_scale

    vm = pl.BlockSpec(memory_space=pltpu.VMEM)
    return pl.pallas_call(
        body, name="small_all_reduce",
        out_shape=(jax.ShapeDtypeStruct((r, dm), F32), jax.ShapeDtypeStruct((1, 1), F32)),
        in_specs=[vm], out_specs=(vm, vm),
        scratch_shapes=[pltpu.VMEM((8, r, dm), F32), pltpu.SemaphoreType.DMA((7,)), pltpu.SemaphoreType.DMA((7,))],
    )(vec)


def kernel(x, meta_tokens, norm_mix_g, w_in, b_gate, pool_w, pool_scale, conv_w, conv_out_w, w_o, norm_ffn_g, w_gate_up, w_down, norm_final_g, loss_target, m_meta_tokens, m_norm_mix_g, m_w_in, m_b_gate, m_pool_w, m_pool_scale, m_conv_w, m_conv_out_w, m_w_o, m_norm_ffn_g, m_w_gate_up, m_w_down, m_norm_final_g, v_meta_tokens, v_norm_mix_g, v_w_in, v_b_gate, v_pool_w, v_pool_scale, v_conv_w, v_conv_out_w, v_w_o, v_norm_ffn_g, v_w_gate_up, v_w_down, v_norm_final_g):
    seq, dm = x.shape[1], x.shape[2]
    tail = 256 if seq % 256 == 0 else LANES
    tm = tail
    lp = seq + tail
    n_chips = 4
    n_groups = len(POOL_WINDOWS)
    gw = dm // n_groups
    tc = min(256, gw)
    cx, cy, cc = _place()
    chip = 2 * cx + cy
    dloc = dm // n_chips

    pool2 = pool_w.reshape(n_groups * pool_w.shape[1], gw)
    big = {"w_in": w_in, "w_gate_up": w_gate_up, "pool_w": pool2, "conv_out_w": conv_out_w, "w_o": w_o, "w_down": w_down}
    chip1 = jnp.reshape(chip, (1,)).astype(jnp.int32)
    core = jnp.reshape(cc, (1,)).astype(jnp.int32)
    big_b = [_cast_into_slot("cast_" + nme, wv, chip1, BF16) for nme, wv in big.items()]
    small_loc = jnp.concatenate([meta_tokens, jnp.pad(conv_w, ((0, 8 - conv_w.shape[0]), (0, 0))),
                                 jnp.zeros((8, dloc), F32)], axis=0)
    slabs = big_b + [_cast_into_slot("place_small", small_loc, chip1, F32)]
    groups = ([0, 6], [2, 3, 4], [1], [5])
    sems, slabs = _gather_start(slabs, groups)

    def arrive(g, after):
        got = _gather_wait("gather_wait_%d" % g, [slabs[a] for a in groups[g]], sems[g], after)
        return _sibling_exchange("sibling_exchange_%d" % g, list(got))

    g1, g2, g3 = norm_mix_g.reshape(1, dm), norm_ffn_g.reshape(1, dm), norm_final_g.reshape(1, dm)
    b_gate2 = b_gate.reshape(2, dm)
    ps = pool_scale.reshape(1, dm)
    w_in4, small4 = arrive(0, chip1)
    small_f = jnp.transpose(small4, (1, 0, 2)).reshape(small4.shape[1], dm)
    meta_f = small_f[:N_META]
    conv_w_f = small_f[N_META:N_META + 3]
    h0 = jnp.concatenate([x[0], jnp.zeros((tail - N_META, dm), F32), meta_f], axis=0)
    hn1 = _rms_fwd("rms_mix", h0, g1, tm)
    proj = _nn_sharded("proj", hn1, w_in4, 6)
    pooled, z = _mixer_fwd("mixer_fwd", proj, conv_w_f, tc)
    pool4, conv_out4, w_o4 = arrive(1, pooled)
    pool4 = pool4.reshape(n_chips, n_groups, gw // n_chips, gw)
    conv_out_f = conv_out4.reshape(dm, dm)
    w_o_f = w_o4.reshape(dm, dm)
    ya = _pool_fwd("pool_proj", pooled, pool4)
    yb = _nn_plain("conv_out", z, conv_out_f, BF16)
    mix = _gate_mix("gate_mix", proj, b_gate2, ya, ps, yb, tm)
    h1 = _nn_plain("attn_out", mix, w_o_f, F32, res=h0, tn_pref=256)
    hn2 = _rms_fwd("rms_ffn", h1, g2, tm)
    (w_gu4,) = arrive(2, hn2)
    gu = _nn_sharded("gate_up", hn2, w_gu4, 2)
    act = _swiglu_fwd("swiglu", gu, tm)
    (w_down4,) = arrive(3, act)
    w_down_f = w_down4.reshape(-1, dm)
    h2 = _nn_plain("ffn_down", act, w_down_f, F32, res=h1, tn_pref=512, tk_pref=1536)
    dh2, dh2b, loss_cols, dg3 = _final_loss("final_loss", h2, g3, loss_target[0], tm)

    dact = _nt_plain("d_act", dh2b, w_down_f)
    gw_down = _tn_plain("dw_down", act, dh2b)
    dgu = _swiglu_bwd("swiglu_bwd", dact, gu, tm)
    gw_gu = _tn_sharded("dw_gate_up", hn2, dgu, n_chips)
    dhn2 = _nt_sharded("d_hn2", dgu, w_gu4)
    dh1, dh1b, dg2 = _rms_bwd("rms_ffn_bwd", dhn2, h1, g2, dh2, tm)
    dmix = _nt_plain("d_mix", dh1b, w_o_f)
    gw_o = _tn_plain("dw_o", mix, dh1b)
    dproj, dyb, dya, db_gate, dps = _gate_bwd("gate_bwd", dmix, proj, b_gate2, ya, ps, yb, tm)
    dz = _nt_plain("d_z", dyb, conv_out_f)
    gw_conv_out = _tn_plain("dw_conv_out", z, dyb)
    dpooled = _pool_bwd_act("d_pooled", dya, pool4)
    gw_pool = _pool_bwd_w("dw_pool", pooled, dya, n_chips)
    dproj, dconv_w = _mixer_bwd("mixer_bwd", dz, dpooled, proj, conv_w_f, dproj, tc)
    gw_in = _tn_sharded("dw_in", hn1, dproj, n_chips)
    dhn1 = _nt_sharded("d_hn1", dproj, w_in4)
    dh0, _, dg1 = _rms_bwd("rms_mix_bwd", dhn1, h0, g1, dh1, tm)
    grad_x = dh0[:seq][None]
    dmeta = dh0[lp - N_META:]

    grads4 = [gw_in, gw_gu, gw_pool.reshape(n_chips, n_groups * (gw // n_chips), gw), gw_conv_out.reshape(n_chips, dloc, dm),
              gw_o.reshape(n_chips, dloc, dm), gw_down.reshape(n_chips, -1, dm)]
    names = ["w_in", "w_gate_up", "pool_w", "conv_out_w", "w_o", "w_down"]
    from_sibling = _sibling_swap(grads4)
    chip_parts = [_pair_add("pair_add_" + nme, g4, rv, core) for nme, g4, rv in zip(names, grads4, from_sibling)]
    from_chips = _chip_scatter(chip_parts)
    halves = [_chip_sum("chip_sum_" + nme, p, rv, chip1) for nme, p, rv in zip(names, chip_parts, from_chips)]
    sib_halves = _sibling_send(halves)
    g_halves = dict(zip(names, zip(halves, sib_halves)))

    vec = jnp.concatenate([dg1, dg2, dg3, db_gate, dps, loss_cols, dconv_w, dmeta], axis=0)
    loss_row = 5 * SMALL_ROWS
    red, loss11 = _small_all_reduce(vec, loss_row, 0.5 / dm)
    loss = loss11[0, 0]
    col0 = chip * dloc
    g_small = {
        "norm_mix_g": red[0], "norm_ffn_g": red[SMALL_ROWS], "norm_final_g": red[2 * SMALL_ROWS],
        "b_gate": red[3 * SMALL_ROWS:3 * SMALL_ROWS + 2].reshape(-1), "pool_scale": red[4 * SMALL_ROWS],
        "conv_w": lax.dynamic_slice(red, (6 * SMALL_ROWS, col0), (3, dloc)),
        "meta_tokens": lax.dynamic_slice(red, (7 * SMALL_ROWS, col0), (N_META, dloc)),
    }

    given = dict(meta_tokens=(meta_tokens, m_meta_tokens, v_meta_tokens), norm_mix_g=(norm_mix_g, m_norm_mix_g, v_norm_mix_g),
                 w_in=(w_in, m_w_in, v_w_in), b_gate=(b_gate, m_b_gate, v_b_gate), pool_w=(pool_w, m_pool_w, v_pool_w),
                 pool_scale=(pool_scale, m_pool_scale, v_pool_scale), conv_w=(conv_w, m_conv_w, v_conv_w),
                 conv_out_w=(conv_out_w, m_conv_out_w, v_conv_out_w), w_o=(w_o, m_w_o, v_w_o),
                 norm_ffn_g=(norm_ffn_g, m_norm_ffn_g, v_norm_ffn_g), w_gate_up=(w_gate_up, m_w_gate_up, v_w_gate_up),
                 w_down=(w_down, m_w_down, v_w_down), norm_final_g=(norm_final_g, m_norm_final_g, v_norm_final_g))
    order = list(given.keys())
    grad, delta, new_m, new_v = {}, {}, {}, {}
    for nme in names:
        w, m, v = given[nme]
        g_own, g_sib = g_halves[nme]
        shape2 = (2 * g_own.shape[0], g_own.shape[1])
        res4 = _adamw_halves("adamw_" + nme, w.reshape(shape2), g_own, g_sib, m.reshape(shape2), v.reshape(shape2), core)
        grad[nme], delta[nme], new_m[nme], new_v[nme] = [t.reshape(w.shape) for t in res4]
    vec_names = ["norm_mix_g", "norm_ffn_g", "norm_final_g", "pool_scale"]

    def slab_vec(pick):
        rows = [pick(nme).reshape(1, dm) for nme in vec_names] + [pick("b_gate").reshape(2, dm), jnp.zeros((2, dm), F32)]
        return jnp.concatenate(rows, axis=0)

    def slab_col(pick):
        return jnp.concatenate([pick("meta_tokens"), pick("conv_w"), jnp.zeros((5, dloc), F32)], axis=0)

    for slab, tag in ((slab_vec, "vec"), (slab_col, "col")):
        d, nm, nv = _adamw("adamw_small_" + tag, slab(lambda nme: given[nme][0]), slab(lambda nme: g_small[nme]),
                           slab(lambda nme: given[nme][1]), slab(lambda nme: given[nme][2]))
        for out, res in ((delta, d), (new_m, nm), (new_v, nv)):
            if tag == "vec":
                for i, nme in enumerate(vec_names):
                    out[nme] = res[i]
                out["b_gate"] = res[4:6].reshape(-1)
            else:
                out["meta_tokens"] = res[:N_META]
                out["conv_w"] = res[N_META:N_META + 3]
    grad.update(g_small)
    return (loss, grad_x, *[grad[nme] for nme in order], *[delta[nme] for nme in order],
            *[new_m[nme] for nme in order], *[new_v[nme] for nme in order])
```

```python
import functools
import math

import jax
import jax.numpy as jnp
from jax import lax
from jax.experimental import pallas as pl
from jax.experimental.pallas import tpu as pltpu

F32 = jnp.float32
BF16 = jnp.bfloat16
N_META = 16
POOL_WINDOWS = (2, 4, 8, 16)
EPS = 1e-6
ADAM_LR, ADAM_B1, ADAM_B2, ADAM_EPS, ADAM_WD, ADAM_STEP = 0.001, 0.9, 0.999, 1e-08, 0.01, 10
LANES = 128
V7X_VMEM_BYTES = 64 * 1024 * 1024
VMEM_LIMIT = V7X_VMEM_BYTES - 8 * 1024 * 1024
MESH = pl.DeviceIdType.MESH
ANY = pl.BlockSpec(memory_space=pl.ANY)
CHIP_FLIPS = ((1, 0), (0, 1), (1, 1))
SMALL_ROWS = 8


def _pick(n, pref):
    best = None
    for t in range(LANES, min(n, pref) + 1, LANES):
        if n % t == 0:
            best = t
    assert best is not None, (n, pref)
    return best


def _params(n_axes=0):
    sem = ("arbitrary",) * n_axes if n_axes else None
    return pltpu.CompilerParams(dimension_semantics=sem, vmem_limit_bytes=VMEM_LIMIT)


_DIMS = {
    "nn": (((1,), (0,)), ((), ())),
    "nt": (((1,), (1,)), ((), ())),
    "tn": (((0,), (0,)), ((), ())),
}


def _matmul(name, mode, a, b, out_sds, grid, a_spec, b_spec, o_spec, nk, res=None, res_spec=None, acc_shape=None):
    out_dtype = out_sds.dtype
    in_place = nk > 1 and out_dtype == F32
    use_scratch = nk > 1 and not in_place
    rows = a_spec.block_shape[-2] if mode != "tn" else None
    chunk = _pick(rows, 768) if rows is not None and rows % LANES == 0 else rows

    def body(*refs):
        if res is not None:
            a_ref, b_ref, r_ref, o_ref, *scr = refs
        else:
            a_ref, b_ref, o_ref, *scr = refs
            r_ref = None
        k = pl.program_id(len(grid) - 1) if nk > 1 else None

        def emit(sl):
            if sl is None:
                part = lax.dot_general(a_ref[...], b_ref[...], _DIMS[mode], preferred_element_type=F32)
                idx = (slice(None), slice(None))
            else:
                part = lax.dot_general(a_ref[sl, :], b_ref[...], _DIMS[mode], preferred_element_type=F32)
                idx = (sl, slice(None))
            if nk == 1:
                if r_ref is not None:
                    part = part + r_ref[idx]
                o_ref[idx] = part.astype(out_dtype)
                return
            acc = scr[0] if use_scratch else o_ref

            @pl.when(k == 0)
            def _():
                first = part
                if r_ref is not None and in_place:
                    first = first + r_ref[idx]
                acc[idx] = first

            @pl.when(k > 0)
            def _():
                acc[idx] += part

            if use_scratch:

                @pl.when(k == nk - 1)
                def _():
                    o_ref[idx] = acc[idx].astype(out_dtype)

        if mode == "tn" or chunk == rows:
            emit(None)
        else:
            for m0 in range(0, rows, chunk):
                emit(pl.ds(m0, chunk))

    ins = [a, b] + ([res] if res is not None else [])
    in_specs = [a_spec, b_spec] + ([res_spec] if res is not None else [])
    scratch = [pltpu.VMEM(acc_shape, F32)] if use_scratch else []
    return pl.pallas_call(
        body, name=name, out_shape=out_sds, grid=grid, in_specs=in_specs, out_specs=o_spec,
        scratch_shapes=scratch, compiler_params=_params(len(grid)),
    )(*ins)


def _nn_sharded(name, a, w4, nseg):
    lp, kdim = a.shape
    s, _, nloc = w4.shape
    segw = s * nloc // nseg
    tn = _pick(math.gcd(nloc, segw), 1536)
    bw, bo = nloc // tn, segw // tn
    return _matmul(
        name, "nn", a, w4, jax.ShapeDtypeStruct((nseg, lp, segw), BF16), (s * bw,),
        pl.BlockSpec((lp, kdim), lambda j: (0, 0)),
        pl.BlockSpec((None, kdim, tn), lambda j: (j // bw, 0, j % bw)),
        pl.BlockSpec((None, lp, tn), lambda j: (j // bo, 0, j % bo)), 1)


def _nn_plain(name, a, w, out_dtype, res=None, tn_pref=512, tk_pref=2048):
    lp, kdim = a.shape
    n = w.shape[1]
    tn = _pick(n, tn_pref)
    tk = kdim if kdim <= tk_pref else _pick(kdim, tk_pref)
    nk = kdim // tk
    grid = (n // tn, nk) if nk > 1 else (n // tn,)
    if nk > 1:
        a_spec = pl.BlockSpec((lp, tk), lambda j, k: (0, k))
        w_spec = pl.BlockSpec((tk, tn), lambda j, k: (k, j))
        o_spec = pl.BlockSpec((lp, tn), lambda j, k: (0, j))
    else:
        a_spec = pl.BlockSpec((lp, tk), lambda j: (0, 0))
        w_spec = pl.BlockSpec((tk, tn), lambda j: (0, j))
        o_spec = pl.BlockSpec((lp, tn), lambda j: (0, j))
    return _matmul(name, "nn", a, w, jax.ShapeDtypeStruct((lp, n), out_dtype), grid, a_spec, w_spec, o_spec, nk,
                   res=res, res_spec=o_spec if res is not None else None, acc_shape=(lp, tn))


def _nt_plain(name, a, w, tn_pref=512):
    lp, kdim = a.shape
    n = w.shape[0]
    tn = _pick(n, tn_pref)
    return _matmul(
        name, "nt", a, w, jax.ShapeDtypeStruct((lp, n), BF16), (n // tn,),
        pl.BlockSpec((lp, kdim), lambda j: (0, 0)),
        pl.BlockSpec((tn, kdim), lambda j: (j, 0)),
        pl.BlockSpec((lp, tn), lambda j: (0, j)), 1)


def _nt_sharded(name, dseg, w4, to_pref=1024):
    nseg, lp, segw = dseg.shape
    s, kdim, nloc = w4.shape
    tr = _pick(math.gcd(nloc, segw), 1536)
    ba, bw = segw // tr, nloc // tr
    nr = s * bw
    to = _pick(kdim, to_pref)
    return _matmul(
        name, "nt", dseg, w4, jax.ShapeDtypeStruct((lp, kdim), F32), (kdim // to, nr),
        pl.BlockSpec((None, lp, tr), lambda j, r: (r // ba, 0, r % ba)),
        pl.BlockSpec((None, to, tr), lambda j, r: (r // bw, j, r % bw)),
        pl.BlockSpec((lp, to), lambda j, r: (0, j)), nr)


def _tn_plain(name, a, d, tk_pref=512):
    lp, kdim = a.shape
    n = d.shape[1]
    tk = _pick(kdim, tk_pref)
    return _matmul(
        name, "tn", a, d, jax.ShapeDtypeStruct((kdim, n), BF16), (kdim // tk,),
        pl.BlockSpec((lp, tk), lambda i: (0, i)),
        pl.BlockSpec((lp, n), lambda i: (0, 0)),
        pl.BlockSpec((tk, n), lambda i: (i, 0)), 1)


def _tn_sharded(name, a, dseg, s, tk_pref=512):
    lp, kdim = a.shape
    nseg, _, segw = dseg.shape
    nloc = nseg * segw // s
    tn = _pick(math.gcd(nloc, segw), 1536)
    bd, bo = segw // tn, nloc // tn
    tk = _pick(kdim, tk_pref)
    return _matmul(
        name, "tn", a, dseg, jax.ShapeDtypeStruct((s, kdim, nloc), BF16), (s * bo, kdim // tk),
        pl.BlockSpec((lp, tk), lambda j, i: (0, i)),
        pl.BlockSpec((None, lp, tn), lambda j, i: (j // bd, 0, j % bd)),
        pl.BlockSpec((None, tk, tn), lambda j, i: (j // bo, i, j % bo)), 1)


def _pool_fwd(name, pooled, pw4):
    lp, dm = pooled.shape
    s, g, rs, gw = pw4.shape
    return _matmul(
        name, "nn", pooled, pw4, jax.ShapeDtypeStruct((lp, dm), BF16), (g, s),
        pl.BlockSpec((lp, rs), lambda gi, si: (0, gi * s + si)),
        pl.BlockSpec((None, None, rs, gw), lambda gi, si: (si, gi, 0, 0)),
        pl.BlockSpec((lp, gw), lambda gi, si: (0, gi)), s, acc_shape=(lp, gw))


def _pool_bwd_act(name, dya, pw4):
    lp, dm = dya.shape
    s, g, rs, gw = pw4.shape
    return _matmul(
        name, "nt", dya, pw4, jax.ShapeDtypeStruct((lp, dm), BF16), (g, s),
        pl.BlockSpec((lp, gw), lambda gi, si: (0, gi)),
        pl.BlockSpec((None, None, rs, gw), lambda gi, si: (si, gi, 0, 0)),
        pl.BlockSpec((lp, rs), lambda gi, si: (0, gi * s + si)), 1)


def _pool_bwd_w(name, pooled, dya, s):
    lp, dm = pooled.shape
    g = len(POOL_WINDOWS)
    gw = dm // g
    rs = gw // s
    return _matmul(
        name, "tn", pooled, dya, jax.ShapeDtypeStruct((s, g, rs, gw), BF16), (g, s),
        pl.BlockSpec((lp, rs), lambda gi, si: (0, gi * s + si)),
        pl.BlockSpec((lp, gw), lambda gi, si: (0, gi)),
        pl.BlockSpec((None, None, rs, gw), lambda gi, si: (si, gi, 0, 0)), 1)


def _rms_fwd(name, h, g, tm):
    lp, dm = h.shape

    def body(h_ref, g_ref, o_ref):
        hv = h_ref[...]
        r = lax.rsqrt(jnp.mean(hv * hv, axis=-1, keepdims=True) + EPS)
        o_ref[...] = (hv * r * g_ref[...]).astype(BF16)

    row = pl.BlockSpec((tm, dm), lambda i: (i, 0))
    return pl.pallas_call(
        body, name=name, out_shape=jax.ShapeDtypeStruct((lp, dm), BF16), grid=(lp // tm,),
        in_specs=[row, pl.BlockSpec((1, dm), lambda i: (0, 0))], out_specs=row, compiler_params=_params(1),
    )(h, g)


def _rms_bwd(name, dy, h, g, dres, tm):
    lp, dm = h.shape

    def body(dy_ref, h_ref, g_ref, dr_ref, dh_ref, dhb_ref, dg_ref):
        hv = h_ref[...]
        r = lax.rsqrt(jnp.mean(hv * hv, axis=-1, keepdims=True) + EPS)
        xhat = hv * r
        dyv = dy_ref[...]
        dxh = dyv * g_ref[...]
        dh = dr_ref[...] + r * (dxh - xhat * jnp.mean(dxh * xhat, axis=-1, keepdims=True))
        dh_ref[...] = dh
        dhb_ref[...] = dh.astype(BF16)

        @pl.when(pl.program_id(0) == 0)
        def _():
            dg_ref[...] = jnp.zeros_like(dg_ref)

        dg_ref[0:1, :] += jnp.sum(dyv * xhat, axis=0, keepdims=True)

    row = pl.BlockSpec((tm, dm), lambda i: (i, 0))
    slab = pl.BlockSpec((SMALL_ROWS, dm), lambda i: (0, 0))
    return pl.pallas_call(
        body, name=name, grid=(lp // tm,),
        out_shape=(jax.ShapeDtypeStruct((lp, dm), F32), jax.ShapeDtypeStruct((lp, dm), BF16),
                   jax.ShapeDtypeStruct((SMALL_ROWS, dm), F32)),
        in_specs=[row, row, pl.BlockSpec((1, dm), lambda i: (0, 0)), row], out_specs=(row, row, slab),
        compiler_params=_params(1),
    )(dy, h, g, dres)


def _gate_mix(name, proj, b_gate2, ya, pool_scale, yb, tm):
    _, lp, dm = proj.shape

    def body(ga_ref, gr_ref, b_ref, ya_ref, ps_ref, yb_ref, o_ref):
        g_a = jax.nn.sigmoid(ga_ref[...].astype(F32) + b_ref[0:1, :])
        g_b = jax.nn.sigmoid(gr_ref[...].astype(F32) + b_ref[1:2, :])
        y_a = ya_ref[...].astype(F32) * ps_ref[...]
        o_ref[...] = (g_a * y_a + g_b * yb_ref[...].astype(F32)).astype(BF16)

    row = pl.BlockSpec((tm, dm), lambda i: (i, 0))
    return pl.pallas_call(
        body, name=name, out_shape=jax.ShapeDtypeStruct((lp, dm), BF16), grid=(lp // tm,),
        in_specs=[pl.BlockSpec((None, tm, dm), lambda i: (4, i, 0)), pl.BlockSpec((None, tm, dm), lambda i: (5, i, 0)),
                  pl.BlockSpec((2, dm), lambda i: (0, 0)), row, pl.BlockSpec((1, dm), lambda i: (0, 0)), row],
        out_specs=row, compiler_params=_params(1),
    )(proj, proj, b_gate2, ya, pool_scale, yb)


def _gate_bwd(name, dmix, proj, b_gate2, ya, pool_scale, yb, tm):
    _, lp, dm = proj.shape

    def body(dm_ref, ga_ref, gr_ref, b_ref, ya_ref, ps_ref, yb_ref, dp_ref, dyb_ref, dya_ref, db_ref, dps_ref):
        dmx = dm_ref[...].astype(F32)
        g_a = jax.nn.sigmoid(ga_ref[...].astype(F32) + b_ref[0:1, :])
        g_b = jax.nn.sigmoid(gr_ref[...].astype(F32) + b_ref[1:2, :])
        ya_pre = ya_ref[...].astype(F32)
        ybv = yb_ref[...].astype(F32)
        ps = ps_ref[...]
        dga = dmx * (ya_pre * ps) * (g_a * (1.0 - g_a))
        dgr = dmx * ybv * (g_b * (1.0 - g_b))
        dp_ref[0] = dga.astype(BF16)
        dp_ref[1] = dgr.astype(BF16)
        dyb_ref[...] = (dmx * g_b).astype(BF16)
        dya_ref[...] = (dmx * g_a * ps).astype(BF16)

        @pl.when(pl.program_id(0) == 0)
        def _():
            db_ref[...] = jnp.zeros_like(db_ref)
            dps_ref[...] = jnp.zeros_like(dps_ref)

        db_ref[0:1, :] += jnp.sum(dga, axis=0, keepdims=True)
        db_ref[1:2, :] += jnp.sum(dgr, axis=0, keepdims=True)
        dps_ref[0:1, :] += jnp.sum(dmx * g_a * ya_pre, axis=0, keepdims=True)

    row = pl.BlockSpec((tm, dm), lambda i: (i, 0))
    one = pl.BlockSpec((1, dm), lambda i: (0, 0))
    slab = pl.BlockSpec((SMALL_ROWS, dm), lambda i: (0, 0))
    return pl.pallas_call(
        body, name=name, grid=(lp // tm,),
        out_shape=(jax.ShapeDtypeStruct((6, lp, dm), BF16), jax.ShapeDtypeStruct((lp, dm), BF16),
                   jax.ShapeDtypeStruct((lp, dm), BF16), jax.ShapeDtypeStruct((SMALL_ROWS, dm), F32),
                   jax.ShapeDtypeStruct((SMALL_ROWS, dm), F32)),
        in_specs=[row, pl.BlockSpec((None, tm, dm), lambda i: (4, i, 0)), pl.BlockSpec((None, tm, dm), lambda i: (5, i, 0)),
                  pl.BlockSpec((2, dm), lambda i: (0, 0)), row, one, row],
        out_specs=(pl.BlockSpec((2, tm, dm), lambda i: (2, i, 0)), row, row, slab, slab),
        compiler_params=_params(1),
    )(dmix, proj, proj, b_gate2, ya, pool_scale, yb)


def _swiglu_fwd(name, gu, tm):
    _, lp, f = gu.shape

    def body(g_ref, u_ref, o_ref):
        gt = g_ref[...].astype(F32)
        o_ref[...] = (gt * jax.nn.sigmoid(gt) * u_ref[...].astype(F32)).astype(BF16)

    return pl.pallas_call(
        body, name=name, out_shape=jax.ShapeDtypeStruct((lp, f), BF16), grid=(lp // tm,),
        in_specs=[pl.BlockSpec((None, tm, f), lambda i: (0, i, 0)), pl.BlockSpec((None, tm, f), lambda i: (1, i, 0))],
        out_specs=pl.BlockSpec((tm, f), lambda i: (i, 0)), compiler_params=_params(1),
    )(gu, gu)


def _swiglu_bwd(name, dact, gu, tm):
    _, lp, f = gu.shape

    def body(d_ref, g_ref, u_ref, o_ref):
        d = d_ref[...].astype(F32)
        gt = g_ref[...].astype(F32)
        sg = jax.nn.sigmoid(gt)
        o_ref[0] = (d * u_ref[...].astype(F32) * (sg * (1.0 + gt * (1.0 - sg)))).astype(BF16)
        o_ref[1] = (d * (gt * sg)).astype(BF16)

    return pl.pallas_call(
        body, name=name, out_shape=jax.ShapeDtypeStruct((2, lp, f), BF16), grid=(lp // tm,),
        in_specs=[pl.BlockSpec((tm, f), lambda i: (i, 0)), pl.BlockSpec((None, tm, f), lambda i: (0, i, 0)),
                  pl.BlockSpec((None, tm, f), lambda i: (1, i, 0))],
        out_specs=pl.BlockSpec((2, tm, f), lambda i: (0, i, 0)), compiler_params=_params(1),
    )(dact, gu, gu)


def _final_loss(name, h2, g3, target, tm):
    lp, dm = h2.shape
    nx = target.shape[0] // tm

    def body(h_ref, g_ref, t_ref, dh_ref, dhb_ref, ls_ref, dg_ref):
        i = pl.program_id(0)

        @pl.when(i == 0)
        def _():
            ls_ref[...] = jnp.zeros_like(ls_ref)
            dg_ref[...] = jnp.zeros_like(dg_ref)

        @pl.when(i < nx)
        def _():
            hv = h_ref[...]
            gv = g_ref[...]
            r = lax.rsqrt(jnp.mean(hv * hv, axis=-1, keepdims=True) + EPS)
            xhat = hv * r
            err = xhat * gv - t_ref[...]
            dout = err * (1.0 / dm)
            dxh = dout * gv
            dh = r * (dxh - xhat * jnp.mean(dxh * xhat, axis=-1, keepdims=True))
            dh_ref[...] = dh
            dhb_ref[...] = dh.astype(BF16)
            ls_ref[0:1, :] += jnp.sum(err * err, axis=0, keepdims=True)
            dg_ref[0:1, :] += jnp.sum(dout * xhat, axis=0, keepdims=True)

        @pl.when(i >= nx)
        def _():
            dh_ref[...] = jnp.zeros_like(dh_ref)
            dhb_ref[...] = jnp.zeros_like(dhb_ref)

    row = pl.BlockSpec((tm, dm), lambda i: (i, 0))
    slab = pl.BlockSpec((SMALL_ROWS, dm), lambda i: (0, 0))
    return pl.pallas_call(
        body, name=name, grid=(lp // tm,),
        out_shape=(jax.ShapeDtypeStruct((lp, dm), F32), jax.ShapeDtypeStruct((lp, dm), BF16),
                   jax.ShapeDtypeStruct((SMALL_ROWS, dm), F32), jax.ShapeDtypeStruct((SMALL_ROWS, dm), F32)),
        in_specs=[row, pl.BlockSpec((1, dm), lambda i: (0, 0)), pl.BlockSpec((tm, dm), lambda i: (jnp.minimum(i, nx - 1), 0))],
        out_specs=(row, row, slab, slab), compiler_params=_params(1),
    )(h2, g3, target)


def _shift(v, k):
    return pltpu.roll(v, k % v.shape[0], axis=0)


def _window_sum(v, group, sign):
    s2 = v + _shift(v, sign * 1)
    s4 = s2 + _shift(s2, sign * 2)
    s8 = s4 + _shift(s4, sign * 4)
    s16 = s8 + _shift(s8, sign * 8)
    return jnp.where(group == 0, s2, jnp.where(group == 1, s4, jnp.where(group == 2, s8, s16)))


def _pool_count(lp, group):
    row = lax.broadcasted_iota(jnp.int32, (lp, 1), 0)
    window = jnp.left_shift(2, group).astype(F32)
    meta_pos = (row - (lp - N_META) + 1).astype(F32)
    return jnp.where(row >= lp - N_META, jnp.minimum(meta_pos, window), window)


def _mixer_fwd(name, proj, conv_w, tc):
    _, lp, dm = proj.shape
    per_group = dm // len(POOL_WINDOWS) // tc

    def body(u_ref, gb_ref, gc_ref, v_ref, cw_ref, p_ref, z_ref):
        group = pl.program_id(0) // per_group
        u = u_ref[...].astype(F32)
        p_ref[...] = (_window_sum(u, group, 1) / _pool_count(lp, group) - u).astype(BF16)
        cv = gc_ref[...].astype(F32) * v_ref[...].astype(F32)
        conv = cw_ref[0:1, :] * _shift(cv, 2) + cw_ref[1:2, :] * _shift(cv, 1) + cw_ref[2:3, :] * cv
        z_ref[...] = (gb_ref[...].astype(F32) * conv).astype(BF16)

    def seg(s):
        return pl.BlockSpec((None, lp, tc), lambda j: (s, 0, j))

    col = pl.BlockSpec((lp, tc), lambda j: (0, j))
    return pl.pallas_call(
        body, name=name, grid=(dm // tc,),
        out_shape=(jax.ShapeDtypeStruct((lp, dm), BF16), jax.ShapeDtypeStruct((lp, dm), BF16)),
        in_specs=[seg(0), seg(1), seg(2), seg(3), pl.BlockSpec((3, tc), lambda j: (0, j))],
        out_specs=(col, col), compiler_params=_params(1),
    )(proj, proj, proj, proj, conv_w)


def _mixer_bwd(name, dz, dpooled, proj, conv_w, dproj, tc):
    _, lp, dm = proj.shape
    per_group = dm // len(POOL_WINDOWS) // tc

    def body(dz_ref, dp_ref, gb_ref, gc_ref, v_ref, cw_ref, _, o_ref, dcw_ref):
        group = pl.program_id(0) // per_group
        dzv = dz_ref[...].astype(F32)
        gb = gb_ref[...].astype(F32)
        gc = gc_ref[...].astype(F32)
        vv = v_ref[...].astype(F32)
        cv = gc * vv
        c1 = _shift(cv, 1)
        c2 = _shift(cv, 2)
        w0, w1, w2 = cw_ref[0:1, :], cw_ref[1:2, :], cw_ref[2:3, :]
        o_ref[1] = (dzv * (w0 * c2 + w1 * c1 + w2 * cv)).astype(BF16)
        dconv = dzv * gb
        dcw_ref[...] = jnp.zeros_like(dcw_ref)
        dcw_ref[0:1, :] = jnp.sum(dconv * c2, axis=0, keepdims=True)
        dcw_ref[1:2, :] = jnp.sum(dconv * c1, axis=0, keepdims=True)
        dcw_ref[2:3, :] = jnp.sum(dconv * cv, axis=0, keepdims=True)
        dcv = w0 * _shift(dconv, -2) + w1 * _shift(dconv, -1) + w2 * dconv
        o_ref[2] = (dcv * vv).astype(BF16)
        o_ref[3] = (dcv * gc).astype(BF16)
        dpv = dp_ref[...].astype(F32)
        o_ref[0] = (_window_sum(dpv / _pool_count(lp, group), group, -1) - dpv).astype(BF16)

    def seg(s):
        return pl.BlockSpec((None, lp, tc), lambda j: (s, 0, j))

    col = pl.BlockSpec((lp, tc), lambda j: (0, j))
    return pl.pallas_call(
        body, name=name, grid=(dm // tc,),
        out_shape=(jax.ShapeDtypeStruct(dproj.shape, BF16), jax.ShapeDtypeStruct((SMALL_ROWS, dm), F32)),
        in_specs=[col, col, seg(1), seg(2), seg(3), pl.BlockSpec((3, tc), lambda j: (0, j)), ANY],
        out_specs=(pl.BlockSpec((4, lp, tc), lambda j: (0, 0, j)), pl.BlockSpec((SMALL_ROWS, tc), lambda j: (0, j))),
        input_output_aliases={6: 0}, compiler_params=_params(1),
    )(dz, dpooled, proj, proj, proj, conv_w, dproj)


def _row_tile(r, c, bytes_per_row_elem=4, budget=2 * 1024 * 1024):
    best = None
    for t in range(16, r + 1, 16):
        if r % t == 0 and t * c * bytes_per_row_elem <= budget:
            best = t
    return best if best is not None else r


def _pair_add(name, g4, recv, core):
    s, r, c = g4.shape
    h = r // 2
    tr = _row_tile(h, c)
    nb = h // tr

    def body(core_ref, g_ref, r_ref, o_ref):
        o_ref[...] = (g_ref[...].astype(F32) + r_ref[...].astype(F32)).astype(BF16)

    grid_spec = pltpu.PrefetchScalarGridSpec(
        num_scalar_prefetch=1, grid=(s, nb),
        in_specs=[pl.BlockSpec((None, tr, c), lambda si, j, core_ref: (si, core_ref[0] * nb + j, 0)),
                  pl.BlockSpec((None, tr, c), lambda si, j, core_ref: (si, j, 0))],
        out_specs=pl.BlockSpec((None, tr, c), lambda si, j, core_ref: (si, j, 0)))
    return pl.pallas_call(
        body, name=name, out_shape=jax.ShapeDtypeStruct((s, h, c), BF16), grid_spec=grid_spec,
        compiler_params=_params(2),
    )(core, g4, recv)


def _chip_sum(name, parts, recv, chip):
    _, h, c = parts.shape
    tr = _row_tile(h, c)

    def body(chip_ref, p_ref, r_ref, o_ref):
        acc = p_ref[...].astype(F32)
        for i in range(len(CHIP_FLIPS)):
            acc = acc + r_ref[i].astype(F32)
        o_ref[...] = acc

    grid_spec = pltpu.PrefetchScalarGridSpec(
        num_scalar_prefetch=1, grid=(h // tr,),
        in_specs=[pl.BlockSpec((None, tr, c), lambda j, chip_ref: (chip_ref[0], j, 0)),
                  pl.BlockSpec((len(CHIP_FLIPS), tr, c), lambda j, chip_ref: (0, j, 0))],
        out_specs=pl.BlockSpec((tr, c), lambda j, chip_ref: (j, 0)))
    return pl.pallas_call(
        body, name=name, out_shape=jax.ShapeDtypeStruct((h, c), F32), grid_spec=grid_spec, compiler_params=_params(1),
    )(chip, parts, recv)


def _adam_update(w, gv, m, v):
    c1 = 1.0 - ADAM_B1 ** ADAM_STEP
    c2 = 1.0 - ADAM_B2 ** ADAM_STEP
    nm = ADAM_B1 * m + (1.0 - ADAM_B1) * gv
    nv = ADAM_B2 * v + (1.0 - ADAM_B2) * (gv * gv)
    return -ADAM_LR * ((nm / c1) / (jnp.sqrt(nv / c2) + ADAM_EPS) + ADAM_WD * w), nm, nv


def _adamw_halves(name, w, g_own, g_sib, m, v, core):
    r, c = w.shape
    h = r // 2
    tr = _row_tile(h, c, budget=1024 * 1024)
    nbh = h // tr

    def body(core_ref, w_ref, go_ref, gs_ref, m_ref, v_ref, g_ref, d_ref, nm_ref, nv_ref):
        mine = (pl.program_id(0) // nbh) == core_ref[0]
        gv = jnp.where(mine, go_ref[...], gs_ref[...])
        g_ref[...] = gv
        d_ref[...], nm_ref[...], nv_ref[...] = _adam_update(w_ref[...], gv, m_ref[...], v_ref[...])

    def blk(fn):
        return pl.BlockSpec((tr, c), fn)

    full = blk(lambda j, core_ref: (j, 0))
    own = blk(lambda j, core_ref: (jnp.clip(j - core_ref[0] * nbh, 0, nbh - 1), 0))
    sib = blk(lambda j, core_ref: (jnp.clip(j - (1 - core_ref[0]) * nbh, 0, nbh - 1), 0))
    grid_spec = pltpu.PrefetchScalarGridSpec(
        num_scalar_prefetch=1, grid=(r // tr,), in_specs=[full, own, sib, full, full], out_specs=(full,) * 4)
    sds = jax.ShapeDtypeStruct((r, c), F32)
    return pl.pallas_call(
        body, name=name, out_shape=(sds,) * 4, grid_spec=grid_spec, compiler_params=_params(1),
    )(core, w, g_own, g_sib, m, v)


def _adamw(name, w, g, m, v):
    r, c = w.shape

    def body(w_ref, g_ref, m_ref, v_ref, d_ref, nm_ref, nv_ref):
        d_ref[...], nm_ref[...], nv_ref[...] = _adam_update(w_ref[...], g_ref[...], m_ref[...], v_ref[...])

    blk = pl.BlockSpec((r, c), lambda j: (0, 0))
    sds = jax.ShapeDtypeStruct((r, c), F32)
    return pl.pallas_call(
        body, name=name, out_shape=(sds, sds, sds), grid=(1,), in_specs=[blk] * 4, out_specs=(blk,) * 3,
        compiler_params=_params(1),
    )(w, g, m, v)


def _cast_into_slot(name, w, chip, dtype):
    r, c = w.shape
    tr = _row_tile(r, c)

    def body(chip_ref, w_ref, o_ref):
        o_ref[...] = w_ref[...].astype(dtype)

    grid_spec = pltpu.PrefetchScalarGridSpec(
        num_scalar_prefetch=1, grid=(r // tr,),
        in_specs=[pl.BlockSpec((tr, c), lambda j, chip_ref: (j, 0))],
        out_specs=pl.BlockSpec((None, tr, c), lambda j, chip_ref: (chip_ref[0], j, 0)))
    return pl.pallas_call(
        body, name=name, out_shape=jax.ShapeDtypeStruct((4, r, c), dtype), grid_spec=grid_spec, compiler_params=_params(1),
    )(chip, w)


def _place():
    return lax.axis_index("x"), lax.axis_index("y"), lax.axis_index("c")


def _chip_of(x, y, flip):
    px, py = x ^ flip[0], y ^ flip[1]
    return px, py, 2 * px + py


def _half(ref, which):
    rows = ref.shape[0] // 2
    return ref.at[pl.ds(which * rows, rows)]


HBM = pl.BlockSpec(memory_space=pltpu.HBM)
SEM = pl.BlockSpec(memory_space=pltpu.SEMAPHORE)
SPLIT_COPY = pltpu.CompilerParams(has_side_effects=pltpu.SideEffectType.DATAFLOW_SIDE_EFFECTING)


def _in_hbm(arrays):
    return [pltpu.with_memory_space_constraint(t, pltpu.HBM) for t in arrays]


def _gather_start(slabs, groups):
    n = len(slabs)
    ng = len(groups)
    nf = len(CHIP_FLIPS)

    def body(*refs):
        sems, outs = refs[n:n + 2 * ng], refs[n + 2 * ng:]
        x, y, c = _place()
        k = 2 * x + y
        for g, members in enumerate(groups):
            for i, a in enumerate(members):
                for j, flip in enumerate(CHIP_FLIPS):
                    px, py, _ = _chip_of(x, y, flip)
                    mine = _half(outs[a].at[k], c)
                    pltpu.make_async_remote_copy(
                        src_ref=mine, dst_ref=mine, send_sem=sems[2 * g].at[i * nf + j], recv_sem=sems[2 * g + 1].at[i * nf + j],
                        device_id=(px, py, c), device_id_type=MESH).start()

    sem_shapes = []
    for members in groups:
        sem_shapes += [pltpu.SemaphoreType.DMA((nf * len(members),))] * 2
    res = pl.pallas_call(
        body, name="gather_start",
        out_shape=tuple(sem_shapes) + tuple(pltpu.HBM(t.shape, t.dtype) for t in slabs),
        in_specs=[HBM] * n, out_specs=tuple([SEM] * (2 * ng) + [HBM] * n),
        input_output_aliases={a: 2 * ng + a for a in range(n)}, compiler_params=SPLIT_COPY,
    )(*_in_hbm(slabs))
    return [(res[2 * g], res[2 * g + 1]) for g in range(ng)], list(res[2 * ng:])


def _gather_wait(name, slabs, sems, after):
    n = len(slabs)
    nf = len(CHIP_FLIPS)

    def body(*refs):
        ins = refs[:n]
        ssem, rsem = refs[n], refs[n + 1]
        x, y, c = _place()
        k = 2 * x + y
        for a in range(n):
            for j, flip in enumerate(CHIP_FLIPS):
                _, _, kj = _chip_of(x, y, flip)
                cp = pltpu.make_async_remote_copy(
                    src_ref=_half(ins[a].at[k], c), dst_ref=_half(ins[a].at[kj], c), send_sem=ssem.at[a * nf + j],
                    recv_sem=rsem.at[a * nf + j], device_id=(x, y, c), device_id_type=MESH)
                cp.wait_send()
                cp.wait_recv()

    return pl.pallas_call(
        body, name=name, out_shape=tuple(pltpu.HBM(t.shape, t.dtype) for t in slabs),
        in_specs=[HBM] * n + [SEM, SEM, ANY], out_specs=tuple([HBM] * n),
        input_output_aliases={a: a for a in range(n)}, compiler_params=SPLIT_COPY,
    )(*slabs, sems[0], sems[1], after)


def _sibling_exchange(name, slabs):
    n = len(slabs)
    nf = len(CHIP_FLIPS)

    def body(*refs):
        outs = refs[n:2 * n]
        ssem, rsem = refs[2 * n:]
        x, y, c = _place()

        def copy(a, j, which, to):
            _, _, kj = _chip_of(x, y, CHIP_FLIPS[j])
            ref = _half(outs[a].at[kj], which)
            return pltpu.make_async_remote_copy(src_ref=ref, dst_ref=ref, send_sem=ssem.at[a * nf + j],
                                                recv_sem=rsem.at[a * nf + j], device_id=to, device_id_type=MESH)

        sends = [copy(a, j, c, (x, y, 1 - c)) for a in range(n) for j in range(nf)]
        for cp in sends:
            cp.start()
        for a in range(n):
            for j in range(nf):
                copy(a, j, 1 - c, (x, y, c)).wait_recv()
        for cp in sends:
            cp.wait_send()

    return pl.pallas_call(
        body, name=name, out_shape=tuple(jax.ShapeDtypeStruct(t.shape, t.dtype) for t in slabs),
        in_specs=[ANY] * n, out_specs=(ANY,) * n, input_output_aliases={a: a for a in range(n)},
        scratch_shapes=[pltpu.SemaphoreType.DMA((nf * n,)), pltpu.SemaphoreType.DMA((nf * n,))],
    )(*slabs)


def _sibling_swap(name, grads):
    n = len(grads)

    def body(*refs):
        ins, outs = refs[:n], refs[n:2 * n]
        ssem, rsem = refs[2 * n:]
        x, y, c = _place()
        cps = []
        for a in range(n):
            h = ins[a].shape[1] // 2
            cps.append(pltpu.make_async_remote_copy(
                src_ref=ins[a].at[:, pl.ds((1 - c) * h, h)], dst_ref=outs[a], send_sem=ssem.at[a], recv_sem=rsem.at[a],
                device_id=(x, y, 1 - c), device_id_type=MESH))
        for cp in cps:
            cp.start()
        for cp in cps:
            cp.wait()

    return pl.pallas_call(
        body, name=name,
        out_shape=tuple(jax.ShapeDtypeStruct((g.shape[0], g.shape[1] // 2, g.shape[2]), g.dtype) for g in grads),
        in_specs=[ANY] * n, out_specs=(ANY,) * n,
        scratch_shapes=[pltpu.SemaphoreType.DMA((n,)), pltpu.SemaphoreType.DMA((n,))],
    )(*grads)


def _scatter_start(name, parts):
    n = len(parts)
    nf = len(CHIP_FLIPS)

    def body(*refs):
        ssem, rsem = refs[2 * n], refs[2 * n + 1]
        src, land = refs[2 * n + 2:3 * n + 2], refs[3 * n + 2:]
        x, y, c = _place()
        for a in range(n):
            for j, flip in enumerate(CHIP_FLIPS):
                px, py, kj = _chip_of(x, y, flip)
                pltpu.make_async_remote_copy(
                    src_ref=src[a].at[kj], dst_ref=land[a].at[j], send_sem=ssem.at[a * nf + j], recv_sem=rsem.at[a * nf + j],
                    device_id=(px, py, c), device_id_type=MESH).start()

    zones = [lax.empty((nf,) + p.shape[1:], p.dtype) for p in parts]
    sem = pltpu.SemaphoreType.DMA((nf * n,))
    res = pl.pallas_call(
        body, name=name,
        out_shape=(sem, sem) + tuple(pltpu.HBM(t.shape, t.dtype) for t in list(parts) + zones),
        in_specs=[HBM] * (2 * n), out_specs=tuple([SEM, SEM] + [HBM] * (2 * n)),
        input_output_aliases={i: 2 + i for i in range(2 * n)}, compiler_params=SPLIT_COPY,
    )(*_in_hbm(list(parts) + zones))
    return res[0], res[1], list(res[2:2 + n]), list(res[2 + n:])


def _scatter_wait(name, ssem, rsem, parts, zones, after):
    n = len(parts)
    nf = len(CHIP_FLIPS)

    def body(*refs):
        src, land = refs[:n], refs[n:2 * n]
        ss, rs = refs[2 * n], refs[2 * n + 1]
        x, y, c = _place()
        for a in range(n):
            for j, flip in enumerate(CHIP_FLIPS):
                _, _, kj = _chip_of(x, y, flip)
                cp = pltpu.make_async_remote_copy(
                    src_ref=src[a].at[kj], dst_ref=land[a].at[j], send_sem=ss.at[a * nf + j], recv_sem=rs.at[a * nf + j],
                    device_id=(x, y, c), device_id_type=MESH)
                cp.wait_send()
                cp.wait_recv()

    res = pl.pallas_call(
        body, name=name, out_shape=tuple(pltpu.HBM(t.shape, t.dtype) for t in list(parts) + list(zones)),
        in_specs=[HBM] * (2 * n) + [SEM, SEM, ANY], out_specs=tuple([HBM] * (2 * n)),
        input_output_aliases={i: i for i in range(2 * n)}, compiler_params=SPLIT_COPY,
    )(*parts, *zones, ssem, rsem, after)
    return list(res[:n]), list(res[n:])


def _sibling_send(name, halves):
    n = len(halves)

    def body(*refs):
        ins, outs = refs[:n], refs[n:2 * n]
        ssem, rsem = refs[2 * n:]
        x, y, c = _place()
        cps = [pltpu.make_async_remote_copy(src_ref=ins[a], dst_ref=outs[a], send_sem=ssem.at[a], recv_sem=rsem.at[a],
                                            device_id=(x, y, 1 - c), device_id_type=MESH) for a in range(n)]
        for cp in cps:
            cp.start()
        for cp in cps:
            cp.wait()

    return pl.pallas_call(
        body, name=name,
        out_shape=tuple(jax.ShapeDtypeStruct(h.shape, h.dtype) for h in halves),
        in_specs=[ANY] * n, out_specs=(ANY,) * n,
        scratch_shapes=[pltpu.SemaphoreType.DMA((n,)), pltpu.SemaphoreType.DMA((n,))],
    )(*halves)


def _small_all_reduce(vec, loss_row, loss_scale):
    r, dm = vec.shape

    def body(v_ref, o_ref, l_ref, buf, ssem, rsem):
        x, y, c = _place()
        me = 4 * x + 2 * y + c
        buf[me] = v_ref[...]
        cps = []
        for mask in range(1, 8):
            fx, fy, fc = (mask >> 2) & 1, (mask >> 1) & 1, mask & 1
            cps.append(pltpu.make_async_remote_copy(
                src_ref=v_ref, dst_ref=buf.at[me], send_sem=ssem.at[mask - 1], recv_sem=rsem.at[mask - 1],
                device_id=(x ^ fx, y ^ fy, c ^ fc), device_id_type=MESH))
        for cp in cps:
            cp.start()
        for mask in range(1, 8):
            fx, fy, fc = (mask >> 2) & 1, (mask >> 1) & 1, mask & 1
            frm = 4 * (x ^ fx) + 2 * (y ^ fy) + (c ^ fc)
            pltpu.make_async_remote_copy(
                src_ref=v_ref, dst_ref=buf.at[frm], send_sem=ssem.at[mask - 1], recv_sem=rsem.at[mask - 1],
                device_id=(x, y, c), device_id_type=MESH).wait_recv()
        for cp in cps:
            cp.wait_send()
        acc = buf[0]
        for i in range(1, 8):
            acc = acc + buf[i]
        o_ref[...] = acc
        l_ref[...] = jnp.sum(acc[loss_row:loss_row + SMALL_ROWS, :], axis=(0, 1), keepdims=True) * loss_scale

    vm = pl.BlockSpec(memory_space=pltpu.VMEM)
    return pl.pallas_call(
        body, name="small_all_reduce",
        out_shape=(jax.ShapeDtypeStruct((r, dm), F32), jax.ShapeDtypeStruct((1, 1), F32)),
        in_specs=[vm], out_specs=(vm, vm),
        scratch_shapes=[pltpu.VMEM((8, r, dm), F32), pltpu.SemaphoreType.DMA((7,)), pltpu.SemaphoreType.DMA((7,))],
    )(vec)


def kernel(x, meta_tokens, norm_mix_g, w_in, b_gate, pool_w, pool_scale, conv_w, conv_out_w, w_o, norm_ffn_g, w_gate_up, w_down, norm_final_g, loss_target, m_meta_tokens, m_norm_mix_g, m_w_in, m_b_gate, m_pool_w, m_pool_scale, m_conv_w, m_conv_out_w, m_w_o, m_norm_ffn_g, m_w_gate_up, m_w_down, m_norm_final_g, v_meta_tokens, v_norm_mix_g, v_w_in, v_b_gate, v_pool_w, v_pool_scale, v_conv_w, v_conv_out_w, v_w_o, v_norm_ffn_g, v_w_gate_up, v_w_down, v_norm_final_g):
    seq, dm = x.shape[1], x.shape[2]
    tail = 256 if seq % 256 == 0 else LANES
    tm = tail
    lp = seq + tail
    n_chips = 4
    n_groups = len(POOL_WINDOWS)
    gw = dm // n_groups
    tc = min(256, gw)
    cx, cy, cc = _place()
    chip = 2 * cx + cy
    dloc = dm // n_chips

    pool2 = pool_w.reshape(n_groups * pool_w.shape[1], gw)
    big = {"w_in": w_in, "w_gate_up": w_gate_up, "pool_w": pool2, "conv_out_w": conv_out_w, "w_o": w_o, "w_down": w_down}
    chip1 = jnp.reshape(chip, (1,)).astype(jnp.int32)
    core = jnp.reshape(cc, (1,)).astype(jnp.int32)
    big_b = [_cast_into_slot("cast_" + nme, wv, chip1, BF16) for nme, wv in big.items()]
    small_loc = jnp.concatenate([meta_tokens, jnp.pad(conv_w, ((0, 8 - conv_w.shape[0]), (0, 0))),
                                 jnp.zeros((8, dloc), F32)], axis=0)
    slabs = big_b + [_cast_into_slot("place_small", small_loc, chip1, F32)]
    groups = ([0, 6], [2, 3, 4], [1], [5])
    sems, slabs = _gather_start(slabs, groups)

    def arrive(g, after):
        got = _gather_wait("gather_wait_%d" % g, [slabs[a] for a in groups[g]], sems[g], after)
        return _sibling_exchange("sibling_exchange_%d" % g, list(got))

    g1, g2, g3 = norm_mix_g.reshape(1, dm), norm_ffn_g.reshape(1, dm), norm_final_g.reshape(1, dm)
    b_gate2 = b_gate.reshape(2, dm)
    ps = pool_scale.reshape(1, dm)
    w_in4, small4 = arrive(0, chip1)
    small_f = jnp.transpose(small4, (1, 0, 2)).reshape(small4.shape[1], dm)
    meta_f = small_f[:N_META]
    conv_w_f = small_f[N_META:N_META + 3]
    h0 = jnp.concatenate([x[0], jnp.zeros((tail - N_META, dm), F32), meta_f], axis=0)
    hn1 = _rms_fwd("rms_mix", h0, g1, tm)
    proj = _nn_sharded("proj", hn1, w_in4, 6)
    pooled, z = _mixer_fwd("mixer_fwd", proj, conv_w_f, tc)
    pool4, conv_out4, w_o4 = arrive(1, pooled)
    pool4 = pool4.reshape(n_chips, n_groups, gw // n_chips, gw)
    conv_out_f = conv_out4.reshape(dm, dm)
    w_o_f = w_o4.reshape(dm, dm)
    ya = _pool_fwd("pool_proj", pooled, pool4)
    yb = _nn_plain("conv_out", z, conv_out_f, BF16)
    mix = _gate_mix("gate_mix", proj, b_gate2, ya, ps, yb, tm)
    h1 = _nn_plain("attn_out", mix, w_o_f, F32, res=h0, tn_pref=256)
    hn2 = _rms_fwd("rms_ffn", h1, g2, tm)
    (w_gu4,) = arrive(2, hn2)
    gu = _nn_sharded("gate_up", hn2, w_gu4, 2)
    act = _swiglu_fwd("swiglu", gu, tm)
    (w_down4,) = arrive(3, act)
    w_down_f = w_down4.reshape(-1, dm)
    h2 = _nn_plain("ffn_down", act, w_down_f, F32, res=h1, tn_pref=512, tk_pref=1536)
    dh2, dh2b, loss_cols, dg3 = _final_loss("final_loss", h2, g3, loss_target[0], tm)

    def scatter(tag, names_g, grads_g):
        got = _sibling_swap("sibling_swap_" + tag, grads_g)
        pairs = [_pair_add("pair_add_" + nme, g4, rv, core) for nme, g4, rv in zip(names_g, grads_g, got)]
        return _scatter_start("scatter_start_" + tag, pairs)

    dact = _nt_plain("d_act", dh2b, w_down_f)
    gw_down = _tn_plain("dw_down", act, dh2b)
    dgu = _swiglu_bwd("swiglu_bwd", dact, gu, tm)
    gw_gu = _tn_sharded("dw_gate_up", hn2, dgu, n_chips)
    flight_a = scatter("a", ["w_gate_up", "w_down"], [gw_gu, gw_down.reshape(n_chips, -1, dm)])
    dhn2 = _nt_sharded("d_hn2", dgu, w_gu4)
    dh1, dh1b, dg2 = _rms_bwd("rms_ffn_bwd", dhn2, h1, g2, dh2, tm)
    dmix = _nt_plain("d_mix", dh1b, w_o_f)
    gw_o = _tn_plain("dw_o", mix, dh1b)
    dproj, dyb, dya, db_gate, dps = _gate_bwd("gate_bwd", dmix, proj, b_gate2, ya, ps, yb, tm)
    dz = _nt_plain("d_z", dyb, conv_out_f)
    gw_conv_out = _tn_plain("dw_conv_out", z, dyb)
    dpooled = _pool_bwd_act("d_pooled", dya, pool4)
    gw_pool = _pool_bwd_w("dw_pool", pooled, dya, n_chips)
    flight_b = scatter("b", ["w_o", "conv_out_w", "pool_w"],
                       [gw_o.reshape(n_chips, dloc, dm), gw_conv_out.reshape(n_chips, dloc, dm),
                        gw_pool.reshape(n_chips, n_groups * (gw // n_chips), gw)])
    dproj, dconv_w = _mixer_bwd("mixer_bwd", dz, dpooled, proj, conv_w_f, dproj, tc)
    gw_in = _tn_sharded("dw_in", hn1, dproj, n_chips)
    flight_c = scatter("c", ["w_in"], [gw_in])
    dhn1 = _nt_sharded("d_hn1", dproj, w_in4)
    dh0, _, dg1 = _rms_bwd("rms_mix_bwd", dhn1, h0, g1, dh1, tm)
    grad_x = dh0[:seq][None]
    dmeta = dh0[lp - N_META:]

    names = ["w_in", "w_gate_up", "pool_w", "conv_out_w", "w_o", "w_down"]
    g_halves = {}
    after = dh0
    for tag, names_g, flight in (("a", ["w_gate_up", "w_down"], flight_a), ("b", ["w_o", "conv_out_w", "pool_w"], flight_b),
                                 ("c", ["w_in"], flight_c)):
        pairs, zones = _scatter_wait("scatter_wait_" + tag, *flight, after)
        halves = [_chip_sum("chip_sum_" + nme, p, rv, chip1) for nme, p, rv in zip(names_g, pairs, zones)]
        sib_halves = _sibling_send("sibling_send_" + tag, halves)
        g_halves.update(zip(names_g, zip(halves, sib_halves)))
        after = sib_halves[0]

    vec = jnp.concatenate([dg1, dg2, dg3, db_gate, dps, loss_cols, dconv_w, dmeta], axis=0)
    loss_row = 5 * SMALL_ROWS
    red, loss11 = _small_all_reduce(vec, loss_row, 0.5 / dm)
    loss = loss11[0, 0]
    col0 = chip * dloc
    g_small = {
        "norm_mix_g": red[0], "norm_ffn_g": red[SMALL_ROWS], "norm_final_g": red[2 * SMALL_ROWS],
        "b_gate": red[3 * SMALL_ROWS:3 * SMALL_ROWS + 2].reshape(-1), "pool_scale": red[4 * SMALL_ROWS],
        "conv_w": lax.dynamic_slice(red, (6 * SMALL_ROWS, col0), (3, dloc)),
        "meta_tokens": lax.dynamic_slice(red, (7 * SMALL_ROWS, col0), (N_META, dloc)),
    }

    given = dict(meta_tokens=(meta_tokens, m_meta_tokens, v_meta_tokens), norm_mix_g=(norm_mix_g, m_norm_mix_g, v_norm_mix_g),
                 w_in=(w_in, m_w_in, v_w_in), b_gate=(b_gate, m_b_gate, v_b_gate), pool_w=(pool_w, m_pool_w, v_pool_w),
                 pool_scale=(pool_scale, m_pool_scale, v_pool_scale), conv_w=(conv_w, m_conv_w, v_conv_w),
                 conv_out_w=(conv_out_w, m_conv_out_w, v_conv_out_w), w_o=(w_o, m_w_o, v_w_o),
                 norm_ffn_g=(norm_ffn_g, m_norm_ffn_g, v_norm_ffn_g), w_gate_up=(w_gate_up, m_w_gate_up, v_w_gate_up),
                 w_down=(w_down, m_w_down, v_w_down), norm_final_g=(norm_final_g, m_norm_final_g, v_norm_final_g))
    order = list(given.keys())
    grad, delta, new_m, new_v = {}, {}, {}, {}
    for nme in names:
        w, m, v = given[nme]
        g_own, g_sib = g_halves[nme]
        shape2 = (2 * g_own.shape[0], g_own.shape[1])
        res4 = _adamw_halves("adamw_" + nme, w.reshape(shape2), g_own, g_sib, m.reshape(shape2), v.reshape(shape2), core)
        grad[nme], delta[nme], new_m[nme], new_v[nme] = [t.reshape(w.shape) for t in res4]
    vec_names = ["norm_mix_g", "norm_ffn_g", "norm_final_g", "pool_scale"]

    def slab_vec(pick):
        rows = [pick(nme).reshape(1, dm) for nme in vec_names] + [pick("b_gate").reshape(2, dm), jnp.zeros((2, dm), F32)]
        return jnp.concatenate(rows, axis=0)

    def slab_col(pick):
        return jnp.concatenate([pick("meta_tokens"), pick("conv_w"), jnp.zeros((5, dloc), F32)], axis=0)

    for slab, tag in ((slab_vec, "vec"), (slab_col, "col")):
        d, nm, nv = _adamw("adamw_small_" + tag, slab(lambda nme: given[nme][0]), slab(lambda nme: g_small[nme]),
                           slab(lambda nme: given[nme][1]), slab(lambda nme: given[nme][2]))
        for out, res in ((delta, d), (new_m, nm), (new_v, nv)):
            if tag == "vec":
                for i, nme in enumerate(vec_names):
                    out[nme] = res[i]
                out["b_gate"] = res[4:6].reshape(-1)
            else:
                out["meta_tokens"] = res[:N_META]
                out["conv_w"] = res[N_META:N_META + 3]
    grad.update(g_small)
    return (loss, grad_x, *[grad[nme] for nme in order], *[delta[nme] for nme in order],
            *[new_m[nme] for nme in order], *[new_v[nme] for nme in order])
```

```python
import functools
import math

import jax
import jax.numpy as jnp
from jax import lax
from jax.experimental import pallas as pl
from jax.experimental.pallas import tpu as pltpu

F32 = jnp.float32
BF16 = jnp.bfloat16
N_META = 16
POOL_WINDOWS = (2, 4, 8, 16)
EPS = 1e-6
ADAM_LR, ADAM_B1, ADAM_B2, ADAM_EPS, ADAM_WD, ADAM_STEP = 0.001, 0.9, 0.999, 1e-08, 0.01, 10
LANES = 128
V7X_VMEM_BYTES = 64 * 1024 * 1024
VMEM_LIMIT = V7X_VMEM_BYTES - 8 * 1024 * 1024
MESH = pl.DeviceIdType.MESH
ANY = pl.BlockSpec(memory_space=pl.ANY)
CHIP_FLIPS = ((1, 0), (0, 1), (1, 1))
SMALL_ROWS = 8


def _pick(n, pref):
    best = None
    for t in range(LANES, min(n, pref) + 1, LANES):
        if n % t == 0:
            best = t
    assert best is not None, (n, pref)
    return best


def _params(n_axes=0):
    sem = ("arbitrary",) * n_axes if n_axes else None
    return pltpu.CompilerParams(dimension_semantics=sem, vmem_limit_bytes=VMEM_LIMIT)


_DIMS = {
    "nn": (((1,), (0,)), ((), ())),
    "nt": (((1,), (1,)), ((), ())),
    "tn": (((0,), (0,)), ((), ())),
}


def _matmul(name, mode, a, b, out_sds, grid, a_spec, b_spec, o_spec, nk, res=None, res_spec=None, acc_shape=None, deps=()):
    out_dtype = out_sds.dtype
    in_place = nk > 1 and out_dtype == F32
    use_scratch = nk > 1 and not in_place
    rows = a_spec.block_shape[-2] if mode != "tn" else None
    chunk = _pick(rows, 768) if rows is not None and rows % LANES == 0 else rows
    n_in = 2 + (res is not None) + len(deps)

    def body(*refs):
        a_ref, b_ref = refs[:2]
        r_ref = refs[2] if res is not None else None
        o_ref, *scr = refs[n_in:]
        k = pl.program_id(len(grid) - 1) if nk > 1 else None

        def emit(sl):
            if sl is None:
                part = lax.dot_general(a_ref[...], b_ref[...], _DIMS[mode], preferred_element_type=F32)
                idx = (slice(None), slice(None))
            else:
                part = lax.dot_general(a_ref[sl, :], b_ref[...], _DIMS[mode], preferred_element_type=F32)
                idx = (sl, slice(None))
            if nk == 1:
                if r_ref is not None:
                    part = part + r_ref[idx]
                o_ref[idx] = part.astype(out_dtype)
                return
            acc = scr[0] if use_scratch else o_ref

            @pl.when(k == 0)
            def _():
                first = part
                if r_ref is not None and in_place:
                    first = first + r_ref[idx]
                acc[idx] = first

            @pl.when(k > 0)
            def _():
                acc[idx] += part

            if use_scratch:

                @pl.when(k == nk - 1)
                def _():
                    o_ref[idx] = acc[idx].astype(out_dtype)

        if mode == "tn" or chunk == rows:
            emit(None)
        else:
            for m0 in range(0, rows, chunk):
                emit(pl.ds(m0, chunk))

    ins = [a, b] + ([res] if res is not None else []) + list(deps)
    in_specs = [a_spec, b_spec] + ([res_spec] if res is not None else []) + [ANY] * len(deps)
    scratch = [pltpu.VMEM(acc_shape, F32)] if use_scratch else []
    return pl.pallas_call(
        body, name=name, out_shape=out_sds, grid=grid, in_specs=in_specs, out_specs=o_spec,
        scratch_shapes=scratch, compiler_params=_params(len(grid)),
    )(*ins)


def _nn_sharded(name, a, w4, nseg):
    lp, kdim = a.shape
    s, _, nloc = w4.shape
    segw = s * nloc // nseg
    tn = _pick(math.gcd(nloc, segw), 1536)
    bw, bo = nloc // tn, segw // tn
    return _matmul(
        name, "nn", a, w4, jax.ShapeDtypeStruct((nseg, lp, segw), BF16), (s * bw,),
        pl.BlockSpec((lp, kdim), lambda j: (0, 0)),
        pl.BlockSpec((None, kdim, tn), lambda j: (j // bw, 0, j % bw)),
        pl.BlockSpec((None, lp, tn), lambda j: (j // bo, 0, j % bo)), 1)


def _nn_plain(name, a, w, out_dtype, res=None, tn_pref=512, tk_pref=2048):
    lp, kdim = a.shape
    n = w.shape[1]
    tn = _pick(n, tn_pref)
    tk = kdim if kdim <= tk_pref else _pick(kdim, tk_pref)
    nk = kdim // tk
    grid = (n // tn, nk) if nk > 1 else (n // tn,)
    if nk > 1:
        a_spec = pl.BlockSpec((lp, tk), lambda j, k: (0, k))
        w_spec = pl.BlockSpec((tk, tn), lambda j, k: (k, j))
        o_spec = pl.BlockSpec((lp, tn), lambda j, k: (0, j))
    else:
        a_spec = pl.BlockSpec((lp, tk), lambda j: (0, 0))
        w_spec = pl.BlockSpec((tk, tn), lambda j: (0, j))
        o_spec = pl.BlockSpec((lp, tn), lambda j: (0, j))
    return _matmul(name, "nn", a, w, jax.ShapeDtypeStruct((lp, n), out_dtype), grid, a_spec, w_spec, o_spec, nk,
                   res=res, res_spec=o_spec if res is not None else None, acc_shape=(lp, tn))


def _nt_plain(name, a, w, tn_pref=512):
    lp, kdim = a.shape
    n = w.shape[0]
    tn = _pick(n, tn_pref)
    return _matmul(
        name, "nt", a, w, jax.ShapeDtypeStruct((lp, n), BF16), (n // tn,),
        pl.BlockSpec((lp, kdim), lambda j: (0, 0)),
        pl.BlockSpec((tn, kdim), lambda j: (j, 0)),
        pl.BlockSpec((lp, tn), lambda j: (0, j)), 1)


def _nt_sharded(name, dseg, w4, to_pref=1024, deps=()):
    nseg, lp, segw = dseg.shape
    s, kdim, nloc = w4.shape
    tr = _pick(math.gcd(nloc, segw), 1536)
    ba, bw = segw // tr, nloc // tr
    nr = s * bw
    to = _pick(kdim, to_pref)
    return _matmul(
        name, "nt", dseg, w4, jax.ShapeDtypeStruct((lp, kdim), F32), (kdim // to, nr),
        pl.BlockSpec((None, lp, tr), lambda j, r: (r // ba, 0, r % ba)),
        pl.BlockSpec((None, to, tr), lambda j, r: (r // bw, j, r % bw)),
        pl.BlockSpec((lp, to), lambda j, r: (0, j)), nr, deps=deps)


def _tn_plain(name, a, d, tk_pref=512):
    lp, kdim = a.shape
    n = d.shape[1]
    tk = _pick(kdim, tk_pref)
    return _matmul(
        name, "tn", a, d, jax.ShapeDtypeStruct((kdim, n), BF16), (kdim // tk,),
        pl.BlockSpec((lp, tk), lambda i: (0, i)),
        pl.BlockSpec((lp, n), lambda i: (0, 0)),
        pl.BlockSpec((tk, n), lambda i: (i, 0)), 1)


def _tn_sharded(name, a, dseg, s, tk_pref=512):
    lp, kdim = a.shape
    nseg, _, segw = dseg.shape
    nloc = nseg * segw // s
    tn = _pick(math.gcd(nloc, segw), 1536)
    bd, bo = segw // tn, nloc // tn
    tk = _pick(kdim, tk_pref)
    return _matmul(
        name, "tn", a, dseg, jax.ShapeDtypeStruct((s, kdim, nloc), BF16), (s * bo, kdim // tk),
        pl.BlockSpec((lp, tk), lambda j, i: (0, i)),
        pl.BlockSpec((None, lp, tn), lambda j, i: (j // bd, 0, j % bd)),
        pl.BlockSpec((None, tk, tn), lambda j, i: (j // bo, i, j % bo)), 1)


def _pool_fwd(name, pooled, pw4):
    lp, dm = pooled.shape
    s, g, rs, gw = pw4.shape
    return _matmul(
        name, "nn", pooled, pw4, jax.ShapeDtypeStruct((lp, dm), BF16), (g, s),
        pl.BlockSpec((lp, rs), lambda gi, si: (0, gi * s + si)),
        pl.BlockSpec((None, None, rs, gw), lambda gi, si: (si, gi, 0, 0)),
        pl.BlockSpec((lp, gw), lambda gi, si: (0, gi)), s, acc_shape=(lp, gw))


def _pool_bwd_act(name, dya, pw4):
    lp, dm = dya.shape
    s, g, rs, gw = pw4.shape
    return _matmul(
        name, "nt", dya, pw4, jax.ShapeDtypeStruct((lp, dm), BF16), (g, s),
        pl.BlockSpec((lp, gw), lambda gi, si: (0, gi)),
        pl.BlockSpec((None, None, rs, gw), lambda gi, si: (si, gi, 0, 0)),
        pl.BlockSpec((lp, rs), lambda gi, si: (0, gi * s + si)), 1)


def _pool_bwd_w(name, pooled, dya, s):
    lp, dm = pooled.shape
    g = len(POOL_WINDOWS)
    gw = dm // g
    rs = gw // s
    return _matmul(
        name, "tn", pooled, dya, jax.ShapeDtypeStruct((s, g, rs, gw), BF16), (g, s),
        pl.BlockSpec((lp, rs), lambda gi, si: (0, gi * s + si)),
        pl.BlockSpec((lp, gw), lambda gi, si: (0, gi)),
        pl.BlockSpec((None, None, rs, gw), lambda gi, si: (si, gi, 0, 0)), 1)


def _rms_fwd(name, h, g, tm):
    lp, dm = h.shape

    def body(h_ref, g_ref, o_ref):
        hv = h_ref[...]
        r = lax.rsqrt(jnp.mean(hv * hv, axis=-1, keepdims=True) + EPS)
        o_ref[...] = (hv * r * g_ref[...]).astype(BF16)

    row = pl.BlockSpec((tm, dm), lambda i: (i, 0))
    return pl.pallas_call(
        body, name=name, out_shape=jax.ShapeDtypeStruct((lp, dm), BF16), grid=(lp // tm,),
        in_specs=[row, pl.BlockSpec((1, dm), lambda i: (0, 0))], out_specs=row, compiler_params=_params(1),
    )(h, g)


def _rms_bwd(name, dy, h, g, dres, tm):
    lp, dm = h.shape

    def body(dy_ref, h_ref, g_ref, dr_ref, dh_ref, dhb_ref, dg_ref):
        hv = h_ref[...]
        r = lax.rsqrt(jnp.mean(hv * hv, axis=-1, keepdims=True) + EPS)
        xhat = hv * r
        dyv = dy_ref[...]
        dxh = dyv * g_ref[...]
        dh = dr_ref[...] + r * (dxh - xhat * jnp.mean(dxh * xhat, axis=-1, keepdims=True))
        dh_ref[...] = dh
        dhb_ref[...] = dh.astype(BF16)

        @pl.when(pl.program_id(0) == 0)
        def _():
            dg_ref[...] = jnp.zeros_like(dg_ref)

        dg_ref[0:1, :] += jnp.sum(dyv * xhat, axis=0, keepdims=True)

    row = pl.BlockSpec((tm, dm), lambda i: (i, 0))
    slab = pl.BlockSpec((SMALL_ROWS, dm), lambda i: (0, 0))
    return pl.pallas_call(
        body, name=name, grid=(lp // tm,),
        out_shape=(jax.ShapeDtypeStruct((lp, dm), F32), jax.ShapeDtypeStruct((lp, dm), BF16),
                   jax.ShapeDtypeStruct((SMALL_ROWS, dm), F32)),
        in_specs=[row, row, pl.BlockSpec((1, dm), lambda i: (0, 0)), row], out_specs=(row, row, slab),
        compiler_params=_params(1),
    )(dy, h, g, dres)


def _gate_mix(name, proj, b_gate2, ya, pool_scale, yb, tm):
    _, lp, dm = proj.shape

    def body(ga_ref, gr_ref, b_ref, ya_ref, ps_ref, yb_ref, o_ref):
        g_a = jax.nn.sigmoid(ga_ref[...].astype(F32) + b_ref[0:1, :])
        g_b = jax.nn.sigmoid(gr_ref[...].astype(F32) + b_ref[1:2, :])
        y_a = ya_ref[...].astype(F32) * ps_ref[...]
        o_ref[...] = (g_a * y_a + g_b * yb_ref[...].astype(F32)).astype(BF16)

    row = pl.BlockSpec((tm, dm), lambda i: (i, 0))
    return pl.pallas_call(
        body, name=name, out_shape=jax.ShapeDtypeStruct((lp, dm), BF16), grid=(lp // tm,),
        in_specs=[pl.BlockSpec((None, tm, dm), lambda i: (4, i, 0)), pl.BlockSpec((None, tm, dm), lambda i: (5, i, 0)),
                  pl.BlockSpec((2, dm), lambda i: (0, 0)), row, pl.BlockSpec((1, dm), lambda i: (0, 0)), row],
        out_specs=row, compiler_params=_params(1),
    )(proj, proj, b_gate2, ya, pool_scale, yb)


def _gate_bwd(name, dmix, proj, b_gate2, ya, pool_scale, yb, tm):
    _, lp, dm = proj.shape

    def body(dm_ref, ga_ref, gr_ref, b_ref, ya_ref, ps_ref, yb_ref, dp_ref, dyb_ref, dya_ref, db_ref, dps_ref):
        dmx = dm_ref[...].astype(F32)
        g_a = jax.nn.sigmoid(ga_ref[...].astype(F32) + b_ref[0:1, :])
        g_b = jax.nn.sigmoid(gr_ref[...].astype(F32) + b_ref[1:2, :])
        ya_pre = ya_ref[...].astype(F32)
        ybv = yb_ref[...].astype(F32)
        ps = ps_ref[...]
        dga = dmx * (ya_pre * ps) * (g_a * (1.0 - g_a))
        dgr = dmx * ybv * (g_b * (1.0 - g_b))
        dp_ref[0] = dga.astype(BF16)
        dp_ref[1] = dgr.astype(BF16)
        dyb_ref[...] = (dmx * g_b).astype(BF16)
        dya_ref[...] = (dmx * g_a * ps).astype(BF16)

        @pl.when(pl.program_id(0) == 0)
        def _():
            db_ref[...] = jnp.zeros_like(db_ref)
            dps_ref[...] = jnp.zeros_like(dps_ref)

        db_ref[0:1, :] += jnp.sum(dga, axis=0, keepdims=True)
        db_ref[1:2, :] += jnp.sum(dgr, axis=0, keepdims=True)
        dps_ref[0:1, :] += jnp.sum(dmx * g_a * ya_pre, axis=0, keepdims=True)

    row = pl.BlockSpec((tm, dm), lambda i: (i, 0))
    one = pl.BlockSpec((1, dm), lambda i: (0, 0))
    slab = pl.BlockSpec((SMALL_ROWS, dm), lambda i: (0, 0))
    return pl.pallas_call(
        body, name=name, grid=(lp // tm,),
        out_shape=(jax.ShapeDtypeStruct((6, lp, dm), BF16), jax.ShapeDtypeStruct((lp, dm), BF16),
                   jax.ShapeDtypeStruct((lp, dm), BF16), jax.ShapeDtypeStruct((SMALL_ROWS, dm), F32),
                   jax.ShapeDtypeStruct((SMALL_ROWS, dm), F32)),
        in_specs=[row, pl.BlockSpec((None, tm, dm), lambda i: (4, i, 0)), pl.BlockSpec((None, tm, dm), lambda i: (5, i, 0)),
                  pl.BlockSpec((2, dm), lambda i: (0, 0)), row, one, row],
        out_specs=(pl.BlockSpec((2, tm, dm), lambda i: (2, i, 0)), row, row, slab, slab),
        compiler_params=_params(1),
    )(dmix, proj, proj, b_gate2, ya, pool_scale, yb)


def _swiglu_fwd(name, gu, tm):
    _, lp, f = gu.shape

    def body(g_ref, u_ref, o_ref):
        gt = g_ref[...].astype(F32)
        o_ref[...] = (gt * jax.nn.sigmoid(gt) * u_ref[...].astype(F32)).astype(BF16)

    return pl.pallas_call(
        body, name=name, out_shape=jax.ShapeDtypeStruct((lp, f), BF16), grid=(lp // tm,),
        in_specs=[pl.BlockSpec((None, tm, f), lambda i: (0, i, 0)), pl.BlockSpec((None, tm, f), lambda i: (1, i, 0))],
        out_specs=pl.BlockSpec((tm, f), lambda i: (i, 0)), compiler_params=_params(1),
    )(gu, gu)


def _swiglu_bwd(name, dact, gu, tm):
    _, lp, f = gu.shape

    def body(d_ref, g_ref, u_ref, o_ref):
        d = d_ref[...].astype(F32)
        gt = g_ref[...].astype(F32)
        sg = jax.nn.sigmoid(gt)
        o_ref[0] = (d * u_ref[...].astype(F32) * (sg * (1.0 + gt * (1.0 - sg)))).astype(BF16)
        o_ref[1] = (d * (gt * sg)).astype(BF16)

    return pl.pallas_call(
        body, name=name, out_shape=jax.ShapeDtypeStruct((2, lp, f), BF16), grid=(lp // tm,),
        in_specs=[pl.BlockSpec((tm, f), lambda i: (i, 0)), pl.BlockSpec((None, tm, f), lambda i: (0, i, 0)),
                  pl.BlockSpec((None, tm, f), lambda i: (1, i, 0))],
        out_specs=pl.BlockSpec((2, tm, f), lambda i: (0, i, 0)), compiler_params=_params(1),
    )(dact, gu, gu)


def _final_loss(name, h2, g3, target, tm):
    lp, dm = h2.shape
    nx = target.shape[0] // tm

    def body(h_ref, g_ref, t_ref, dh_ref, dhb_ref, ls_ref, dg_ref):
        i = pl.program_id(0)

        @pl.when(i == 0)
        def _():
            ls_ref[...] = jnp.zeros_like(ls_ref)
            dg_ref[...] = jnp.zeros_like(dg_ref)

        @pl.when(i < nx)
        def _():
            hv = h_ref[...]
            gv = g_ref[...]
            r = lax.rsqrt(jnp.mean(hv * hv, axis=-1, keepdims=True) + EPS)
            xhat = hv * r
            err = xhat * gv - t_ref[...]
            dout = err * (1.0 / dm)
            dxh = dout * gv
            dh = r * (dxh - xhat * jnp.mean(dxh * xhat, axis=-1, keepdims=True))
            dh_ref[...] = dh
            dhb_ref[...] = dh.astype(BF16)
            ls_ref[0:1, :] += jnp.sum(err * err, axis=0, keepdims=True)
            dg_ref[0:1, :] += jnp.sum(dout * xhat, axis=0, keepdims=True)

        @pl.when(i >= nx)
        def _():
            dh_ref[...] = jnp.zeros_like(dh_ref)
            dhb_ref[...] = jnp.zeros_like(dhb_ref)

    row = pl.BlockSpec((tm, dm), lambda i: (i, 0))
    slab = pl.BlockSpec((SMALL_ROWS, dm), lambda i: (0, 0))
    return pl.pallas_call(
        body, name=name, grid=(lp // tm,),
        out_shape=(jax.ShapeDtypeStruct((lp, dm), F32), jax.ShapeDtypeStruct((lp, dm), BF16),
                   jax.ShapeDtypeStruct((SMALL_ROWS, dm), F32), jax.ShapeDtypeStruct((SMALL_ROWS, dm), F32)),
        in_specs=[row, pl.BlockSpec((1, dm), lambda i: (0, 0)), pl.BlockSpec((tm, dm), lambda i: (jnp.minimum(i, nx - 1), 0))],
        out_specs=(row, row, slab, slab), compiler_params=_params(1),
    )(h2, g3, target)


def _shift(v, k):
    return pltpu.roll(v, k % v.shape[0], axis=0)


def _window_sum(v, group, sign):
    s2 = v + _shift(v, sign * 1)
    s4 = s2 + _shift(s2, sign * 2)
    s8 = s4 + _shift(s4, sign * 4)
    s16 = s8 + _shift(s8, sign * 8)
    return jnp.where(group == 0, s2, jnp.where(group == 1, s4, jnp.where(group == 2, s8, s16)))


def _pool_count(lp, group):
    row = lax.broadcasted_iota(jnp.int32, (lp, 1), 0)
    window = jnp.left_shift(2, group).astype(F32)
    meta_pos = (row - (lp - N_META) + 1).astype(F32)
    return jnp.where(row >= lp - N_META, jnp.minimum(meta_pos, window), window)


def _mixer_fwd(name, proj, conv_w, tc):
    _, lp, dm = proj.shape
    per_group = dm // len(POOL_WINDOWS) // tc

    def body(u_ref, gb_ref, gc_ref, v_ref, cw_ref, p_ref, z_ref):
        group = pl.program_id(0) // per_group
        u = u_ref[...].astype(F32)
        p_ref[...] = (_window_sum(u, group, 1) / _pool_count(lp, group) - u).astype(BF16)
        cv = gc_ref[...].astype(F32) * v_ref[...].astype(F32)
        conv = cw_ref[0:1, :] * _shift(cv, 2) + cw_ref[1:2, :] * _shift(cv, 1) + cw_ref[2:3, :] * cv
        z_ref[...] = (gb_ref[...].astype(F32) * conv).astype(BF16)

    def seg(s):
        return pl.BlockSpec((None, lp, tc), lambda j: (s, 0, j))

    col = pl.BlockSpec((lp, tc), lambda j: (0, j))
    return pl.pallas_call(
        body, name=name, grid=(dm // tc,),
        out_shape=(jax.ShapeDtypeStruct((lp, dm), BF16), jax.ShapeDtypeStruct((lp, dm), BF16)),
        in_specs=[seg(0), seg(1), seg(2), seg(3), pl.BlockSpec((3, tc), lambda j: (0, j))],
        out_specs=(col, col), compiler_params=_params(1),
    )(proj, proj, proj, proj, conv_w)


def _mixer_bwd(name, dz, dpooled, proj, conv_w, dproj, tc, dep):
    _, lp, dm = proj.shape
    per_group = dm // len(POOL_WINDOWS) // tc

    def body(dz_ref, dp_ref, gb_ref, gc_ref, v_ref, cw_ref, _, __, o_ref, dcw_ref):
        group = pl.program_id(0) // per_group
        dzv = dz_ref[...].astype(F32)
        gb = gb_ref[...].astype(F32)
        gc = gc_ref[...].astype(F32)
        vv = v_ref[...].astype(F32)
        cv = gc * vv
        c1 = _shift(cv, 1)
        c2 = _shift(cv, 2)
        w0, w1, w2 = cw_ref[0:1, :], cw_ref[1:2, :], cw_ref[2:3, :]
        o_ref[1] = (dzv * (w0 * c2 + w1 * c1 + w2 * cv)).astype(BF16)
        dconv = dzv * gb
        dcw_ref[...] = jnp.zeros_like(dcw_ref)
        dcw_ref[0:1, :] = jnp.sum(dconv * c2, axis=0, keepdims=True)
        dcw_ref[1:2, :] = jnp.sum(dconv * c1, axis=0, keepdims=True)
        dcw_ref[2:3, :] = jnp.sum(dconv * cv, axis=0, keepdims=True)
        dcv = w0 * _shift(dconv, -2) + w1 * _shift(dconv, -1) + w2 * dconv
        o_ref[2] = (dcv * vv).astype(BF16)
        o_ref[3] = (dcv * gc).astype(BF16)
        dpv = dp_ref[...].astype(F32)
        o_ref[0] = (_window_sum(dpv / _pool_count(lp, group), group, -1) - dpv).astype(BF16)

    def seg(s):
        return pl.BlockSpec((None, lp, tc), lambda j: (s, 0, j))

    col = pl.BlockSpec((lp, tc), lambda j: (0, j))
    return pl.pallas_call(
        body, name=name, grid=(dm // tc,),
        out_shape=(jax.ShapeDtypeStruct(dproj.shape, BF16), jax.ShapeDtypeStruct((SMALL_ROWS, dm), F32)),
        in_specs=[col, col, seg(1), seg(2), seg(3), pl.BlockSpec((3, tc), lambda j: (0, j)), ANY, ANY],
        out_specs=(pl.BlockSpec((4, lp, tc), lambda j: (0, 0, j)), pl.BlockSpec((SMALL_ROWS, tc), lambda j: (0, j))),
        input_output_aliases={6: 0}, compiler_params=_params(1),
    )(dz, dpooled, proj, proj, proj, conv_w, dproj, dep)


def _row_tile(r, c, bytes_per_row_elem=4, budget=2 * 1024 * 1024):
    best = None
    for t in range(16, r + 1, 16):
        if r % t == 0 and t * c * bytes_per_row_elem <= budget:
            best = t
    return best if best is not None else r


def _pair_add(name, g4, recv, core):
    s, r, c = g4.shape
    h = r // 2
    tr = _row_tile(h, c)
    nb = h // tr

    def body(core_ref, g_ref, r_ref, o_ref):
        o_ref[...] = (g_ref[...].astype(F32) + r_ref[...].astype(F32)).astype(BF16)

    grid_spec = pltpu.PrefetchScalarGridSpec(
        num_scalar_prefetch=1, grid=(s, nb),
        in_specs=[pl.BlockSpec((None, tr, c), lambda si, j, core_ref: (si, core_ref[0] * nb + j, 0)),
                  pl.BlockSpec((None, tr, c), lambda si, j, core_ref: (si, j, 0))],
        out_specs=pl.BlockSpec((None, tr, c), lambda si, j, core_ref: (si, j, 0)))
    return pl.pallas_call(
        body, name=name, out_shape=jax.ShapeDtypeStruct((s, h, c), BF16), grid_spec=grid_spec,
        compiler_params=_params(2),
    )(core, g4, recv)


def _chip_sum(name, parts, recv, chip):
    _, h, c = parts.shape
    tr = _row_tile(h, c)

    def body(chip_ref, p_ref, r_ref, o_ref):
        acc = p_ref[...].astype(F32)
        for i in range(len(CHIP_FLIPS)):
            acc = acc + r_ref[i].astype(F32)
        o_ref[...] = acc

    grid_spec = pltpu.PrefetchScalarGridSpec(
        num_scalar_prefetch=1, grid=(h // tr,),
        in_specs=[pl.BlockSpec((None, tr, c), lambda j, chip_ref: (chip_ref[0], j, 0)),
                  pl.BlockSpec((len(CHIP_FLIPS), tr, c), lambda j, chip_ref: (0, j, 0))],
        out_specs=pl.BlockSpec((tr, c), lambda j, chip_ref: (j, 0)))
    return pl.pallas_call(
        body, name=name, out_shape=jax.ShapeDtypeStruct((h, c), F32), grid_spec=grid_spec, compiler_params=_params(1),
    )(chip, parts, recv)


def _adam_update(w, gv, m, v):
    c1 = 1.0 - ADAM_B1 ** ADAM_STEP
    c2 = 1.0 - ADAM_B2 ** ADAM_STEP
    nm = ADAM_B1 * m + (1.0 - ADAM_B1) * gv
    nv = ADAM_B2 * v + (1.0 - ADAM_B2) * (gv * gv)
    return -ADAM_LR * ((nm / c1) / (jnp.sqrt(nv / c2) + ADAM_EPS) + ADAM_WD * w), nm, nv


def _adamw_halves(name, w, g_own, g_sib, m, v, core):
    r, c = w.shape
    h = r // 2
    tr = _row_tile(h, c, budget=1024 * 1024)
    nbh = h // tr

    def body(core_ref, w_ref, go_ref, gs_ref, m_ref, v_ref, g_ref, d_ref, nm_ref, nv_ref):
        mine = (pl.program_id(0) // nbh) == core_ref[0]
        gv = jnp.where(mine, go_ref[...], gs_ref[...])
        g_ref[...] = gv
        d_ref[...], nm_ref[...], nv_ref[...] = _adam_update(w_ref[...], gv, m_ref[...], v_ref[...])

    def blk(fn):
        return pl.BlockSpec((tr, c), fn)

    full = blk(lambda j, core_ref: (j, 0))
    own = blk(lambda j, core_ref: (jnp.clip(j - core_ref[0] * nbh, 0, nbh - 1), 0))
    sib = blk(lambda j, core_ref: (jnp.clip(j - (1 - core_ref[0]) * nbh, 0, nbh - 1), 0))
    grid_spec = pltpu.PrefetchScalarGridSpec(
        num_scalar_prefetch=1, grid=(r // tr,), in_specs=[full, own, sib, full, full], out_specs=(full,) * 4)
    sds = jax.ShapeDtypeStruct((r, c), F32)
    return pl.pallas_call(
        body, name=name, out_shape=(sds,) * 4, grid_spec=grid_spec, compiler_params=_params(1),
    )(core, w, g_own, g_sib, m, v)


def _adamw(name, w, g, m, v):
    r, c = w.shape

    def body(w_ref, g_ref, m_ref, v_ref, d_ref, nm_ref, nv_ref):
        d_ref[...], nm_ref[...], nv_ref[...] = _adam_update(w_ref[...], g_ref[...], m_ref[...], v_ref[...])

    blk = pl.BlockSpec((r, c), lambda j: (0, 0))
    sds = jax.ShapeDtypeStruct((r, c), F32)
    return pl.pallas_call(
        body, name=name, out_shape=(sds, sds, sds), grid=(1,), in_specs=[blk] * 4, out_specs=(blk,) * 3,
        compiler_params=_params(1),
    )(w, g, m, v)


def _cast_into_slot(name, w, chip, dtype):
    r, c = w.shape
    tr = _row_tile(r, c)

    def body(chip_ref, w_ref, o_ref):
        o_ref[...] = w_ref[...].astype(dtype)

    grid_spec = pltpu.PrefetchScalarGridSpec(
        num_scalar_prefetch=1, grid=(r // tr,),
        in_specs=[pl.BlockSpec((tr, c), lambda j, chip_ref: (j, 0))],
        out_specs=pl.BlockSpec((None, tr, c), lambda j, chip_ref: (chip_ref[0], j, 0)))
    return pl.pallas_call(
        body, name=name, out_shape=jax.ShapeDtypeStruct((4, r, c), dtype), grid_spec=grid_spec, compiler_params=_params(1),
    )(chip, w)


def _place():
    return lax.axis_index("x"), lax.axis_index("y"), lax.axis_index("c")


def _chip_of(x, y, flip):
    px, py = x ^ flip[0], y ^ flip[1]
    return px, py, 2 * px + py


def _half(ref, which):
    rows = ref.shape[0] // 2
    return ref.at[pl.ds(which * rows, rows)]


HBM = pl.BlockSpec(memory_space=pltpu.HBM)
SEM = pl.BlockSpec(memory_space=pltpu.SEMAPHORE)
SPLIT_COPY = pltpu.CompilerParams(has_side_effects=pltpu.SideEffectType.DATAFLOW_SIDE_EFFECTING)


def _in_hbm(arrays):
    return [pltpu.with_memory_space_constraint(t, pltpu.HBM) for t in arrays]


def _gather_start(slabs, groups):
    n = len(slabs)
    ng = len(groups)
    nf = len(CHIP_FLIPS)

    def body(*refs):
        sems, outs = refs[n:n + 2 * ng], refs[n + 2 * ng:]
        x, y, c = _place()
        k = 2 * x + y
        for g, members in enumerate(groups):
            for i, a in enumerate(members):
                for j, flip in enumerate(CHIP_FLIPS):
                    px, py, _ = _chip_of(x, y, flip)
                    mine = _half(outs[a].at[k], c)
                    pltpu.make_async_remote_copy(
                        src_ref=mine, dst_ref=mine, send_sem=sems[2 * g].at[i * nf + j], recv_sem=sems[2 * g + 1].at[i * nf + j],
                        device_id=(px, py, c), device_id_type=MESH).start()

    sem_shapes = []
    for members in groups:
        sem_shapes += [pltpu.SemaphoreType.DMA((nf * len(members),))] * 2
    res = pl.pallas_call(
        body, name="gather_start",
        out_shape=tuple(sem_shapes) + tuple(pltpu.HBM(t.shape, t.dtype) for t in slabs),
        in_specs=[HBM] * n, out_specs=tuple([SEM] * (2 * ng) + [HBM] * n),
        input_output_aliases={a: 2 * ng + a for a in range(n)}, compiler_params=SPLIT_COPY,
    )(*_in_hbm(slabs))
    return [(res[2 * g], res[2 * g + 1]) for g in range(ng)], list(res[2 * ng:])


def _gather_wait(name, slabs, sems, after):
    n = len(slabs)
    nf = len(CHIP_FLIPS)

    def body(*refs):
        ins = refs[:n]
        ssem, rsem = refs[n], refs[n + 1]
        x, y, c = _place()
        k = 2 * x + y
        for a in range(n):
            for j, flip in enumerate(CHIP_FLIPS):
                _, _, kj = _chip_of(x, y, flip)
                cp = pltpu.make_async_remote_copy(
                    src_ref=_half(ins[a].at[k], c), dst_ref=_half(ins[a].at[kj], c), send_sem=ssem.at[a * nf + j],
                    recv_sem=rsem.at[a * nf + j], device_id=(x, y, c), device_id_type=MESH)
                cp.wait_send()
                cp.wait_recv()

    return pl.pallas_call(
        body, name=name, out_shape=tuple(pltpu.HBM(t.shape, t.dtype) for t in slabs),
        in_specs=[HBM] * n + [SEM, SEM, ANY], out_specs=tuple([HBM] * n),
        input_output_aliases={a: a for a in range(n)}, compiler_params=SPLIT_COPY,
    )(*slabs, sems[0], sems[1], after)


def _sibling_exchange(name, slabs):
    n = len(slabs)
    nf = len(CHIP_FLIPS)

    def body(*refs):
        outs = refs[n:2 * n]
        ssem, rsem = refs[2 * n:]
        x, y, c = _place()

        def copy(a, j, which, to):
            _, _, kj = _chip_of(x, y, CHIP_FLIPS[j])
            ref = _half(outs[a].at[kj], which)
            return pltpu.make_async_remote_copy(src_ref=ref, dst_ref=ref, send_sem=ssem.at[a * nf + j],
                                                recv_sem=rsem.at[a * nf + j], device_id=to, device_id_type=MESH)

        sends = [copy(a, j, c, (x, y, 1 - c)) for a in range(n) for j in range(nf)]
        for cp in sends:
            cp.start()
        for a in range(n):
            for j in range(nf):
                copy(a, j, 1 - c, (x, y, c)).wait_recv()
        for cp in sends:
            cp.wait_send()

    return pl.pallas_call(
        body, name=name, out_shape=tuple(jax.ShapeDtypeStruct(t.shape, t.dtype) for t in slabs),
        in_specs=[ANY] * n, out_specs=(ANY,) * n, input_output_aliases={a: a for a in range(n)},
        scratch_shapes=[pltpu.SemaphoreType.DMA((nf * n,)), pltpu.SemaphoreType.DMA((nf * n,))],
    )(*slabs)


def _sibling_swap(name, grads):
    n = len(grads)

    def body(*refs):
        ins, outs = refs[:n], refs[n:2 * n]
        ssem, rsem = refs[2 * n:]
        x, y, c = _place()
        cps = []
        for a in range(n):
            h = ins[a].shape[1] // 2
            cps.append(pltpu.make_async_remote_copy(
                src_ref=ins[a].at[:, pl.ds((1 - c) * h, h)], dst_ref=outs[a], send_sem=ssem.at[a], recv_sem=rsem.at[a],
                device_id=(x, y, 1 - c), device_id_type=MESH))
        for cp in cps:
            cp.start()
        for cp in cps:
            cp.wait()

    return pl.pallas_call(
        body, name=name,
        out_shape=tuple(jax.ShapeDtypeStruct((g.shape[0], g.shape[1] // 2, g.shape[2]), g.dtype) for g in grads),
        in_specs=[ANY] * n, out_specs=(ANY,) * n,
        scratch_shapes=[pltpu.SemaphoreType.DMA((n,)), pltpu.SemaphoreType.DMA((n,))],
    )(*grads)


def _scatter_start(name, parts):
    n = len(parts)
    nf = len(CHIP_FLIPS)

    def body(*refs):
        ssem, rsem = refs[2 * n], refs[2 * n + 1]
        src, land = refs[2 * n + 2:3 * n + 2], refs[3 * n + 2:4 * n + 2]
        token = refs[4 * n + 2]
        token[...] = jnp.zeros_like(token)
        x, y, c = _place()
        for a in range(n):
            for j, flip in enumerate(CHIP_FLIPS):
                px, py, kj = _chip_of(x, y, flip)
                pltpu.make_async_remote_copy(
                    src_ref=src[a].at[kj], dst_ref=land[a].at[j], send_sem=ssem.at[a * nf + j], recv_sem=rsem.at[a * nf + j],
                    device_id=(px, py, c), device_id_type=MESH).start()

    zones = [lax.empty((nf,) + p.shape[1:], p.dtype) for p in parts]
    sem = pltpu.SemaphoreType.DMA((nf * n,))
    res = pl.pallas_call(
        body, name=name,
        out_shape=(sem, sem) + tuple(pltpu.HBM(t.shape, t.dtype) for t in list(parts) + zones)
        + (jax.ShapeDtypeStruct((SMALL_ROWS, LANES), F32),),
        in_specs=[HBM] * (2 * n),
        out_specs=tuple([SEM, SEM] + [HBM] * (2 * n) + [pl.BlockSpec(memory_space=pltpu.VMEM)]),
        input_output_aliases={i: 2 + i for i in range(2 * n)}, compiler_params=SPLIT_COPY,
    )(*_in_hbm(list(parts) + zones))
    return (res[0], res[1], list(res[2:2 + n]), list(res[2 + n:2 + 2 * n])), res[2 + 2 * n]


def _scatter_wait(name, ssem, rsem, parts, zones, after):
    n = len(parts)
    nf = len(CHIP_FLIPS)

    def body(*refs):
        src, land = refs[:n], refs[n:2 * n]
        ss, rs = refs[2 * n], refs[2 * n + 1]
        x, y, c = _place()
        for a in range(n):
            for j, flip in enumerate(CHIP_FLIPS):
                _, _, kj = _chip_of(x, y, flip)
                cp = pltpu.make_async_remote_copy(
                    src_ref=src[a].at[kj], dst_ref=land[a].at[j], send_sem=ss.at[a * nf + j], recv_sem=rs.at[a * nf + j],
                    device_id=(x, y, c), device_id_type=MESH)
                cp.wait_send()
                cp.wait_recv()

    res = pl.pallas_call(
        body, name=name, out_shape=tuple(pltpu.HBM(t.shape, t.dtype) for t in list(parts) + list(zones)),
        in_specs=[HBM] * (2 * n) + [SEM, SEM, ANY], out_specs=tuple([HBM] * (2 * n)),
        input_output_aliases={i: i for i in range(2 * n)}, compiler_params=SPLIT_COPY,
    )(*parts, *zones, ssem, rsem, after)
    return list(res[:n]), list(res[n:])


def _sibling_send(name, halves):
    n = len(halves)

    def body(*refs):
        ins, outs = refs[:n], refs[n:2 * n]
        ssem, rsem = refs[2 * n:]
        x, y, c = _place()
        cps = [pltpu.make_async_remote_copy(src_ref=ins[a], dst_ref=outs[a], send_sem=ssem.at[a], recv_sem=rsem.at[a],
                                            device_id=(x, y, 1 - c), device_id_type=MESH) for a in range(n)]
        for cp in cps:
            cp.start()
        for cp in cps:
            cp.wait()

    return pl.pallas_call(
        body, name=name,
        out_shape=tuple(jax.ShapeDtypeStruct(h.shape, h.dtype) for h in halves),
        in_specs=[ANY] * n, out_specs=(ANY,) * n,
        scratch_shapes=[pltpu.SemaphoreType.DMA((n,)), pltpu.SemaphoreType.DMA((n,))],
    )(*halves)


def _small_all_reduce(vec, loss_row, loss_scale):
    r, dm = vec.shape

    def body(v_ref, o_ref, l_ref, buf, ssem, rsem):
        x, y, c = _place()
        me = 4 * x + 2 * y + c
        buf[me] = v_ref[...]
        cps = []
        for mask in range(1, 8):
            fx, fy, fc = (mask >> 2) & 1, (mask >> 1) & 1, mask & 1
            cps.append(pltpu.make_async_remote_copy(
                src_ref=v_ref, dst_ref=buf.at[me], send_sem=ssem.at[mask - 1], recv_sem=rsem.at[mask - 1],
                device_id=(x ^ fx, y ^ fy, c ^ fc), device_id_type=MESH))
        for cp in cps:
            cp.start()
        for mask in range(1, 8):
            fx, fy, fc = (mask >> 2) & 1, (mask >> 1) & 1, mask & 1
            frm = 4 * (x ^ fx) + 2 * (y ^ fy) + (c ^ fc)
            pltpu.make_async_remote_copy(
                src_ref=v_ref, dst_ref=buf.at[frm], send_sem=ssem.at[mask - 1], recv_sem=rsem.at[mask - 1],
                device_id=(x, y, c), device_id_type=MESH).wait_recv()
        for cp in cps:
            cp.wait_send()
        acc = buf[0]
        for i in range(1, 8):
            acc = acc + buf[i]
        o_ref[...] = acc
        l_ref[...] = jnp.sum(acc[loss_row:loss_row + SMALL_ROWS, :], axis=(0, 1), keepdims=True) * loss_scale

    vm = pl.BlockSpec(memory_space=pltpu.VMEM)
    return pl.pallas_call(
        body, name="small_all_reduce",
        out_shape=(jax.ShapeDtypeStruct((r, dm), F32), jax.ShapeDtypeStruct((1, 1), F32)),
        in_specs=[vm], out_specs=(vm, vm),
        scratch_shapes=[pltpu.VMEM((8, r, dm), F32), pltpu.SemaphoreType.DMA((7,)), pltpu.SemaphoreType.DMA((7,))],
    )(vec)


def kernel(x, meta_tokens, norm_mix_g, w_in, b_gate, pool_w, pool_scale, conv_w, conv_out_w, w_o, norm_ffn_g, w_gate_up, w_down, norm_final_g, loss_target, m_meta_tokens, m_norm_mix_g, m_w_in, m_b_gate, m_pool_w, m_pool_scale, m_conv_w, m_conv_out_w, m_w_o, m_norm_ffn_g, m_w_gate_up, m_w_down, m_norm_final_g, v_meta_tokens, v_norm_mix_g, v_w_in, v_b_gate, v_pool_w, v_pool_scale, v_conv_w, v_conv_out_w, v_w_o, v_norm_ffn_g, v_w_gate_up, v_w_down, v_norm_final_g):
    seq, dm = x.shape[1], x.shape[2]
    tail = 256 if seq % 256 == 0 else LANES
    tm = tail
    lp = seq + tail
    n_chips = 4
    n_groups = len(POOL_WINDOWS)
    gw = dm // n_groups
    tc = min(256, gw)
    cx, cy, cc = _place()
    chip = 2 * cx + cy
    dloc = dm // n_chips

    pool2 = pool_w.reshape(n_groups * pool_w.shape[1], gw)
    big = {"w_in": w_in, "w_gate_up": w_gate_up, "pool_w": pool2, "conv_out_w": conv_out_w, "w_o": w_o, "w_down": w_down}
    chip1 = jnp.reshape(chip, (1,)).astype(jnp.int32)
    core = jnp.reshape(cc, (1,)).astype(jnp.int32)
    big_b = [_cast_into_slot("cast_" + nme, wv, chip1, BF16) for nme, wv in big.items()]
    small_loc = jnp.concatenate([meta_tokens, jnp.pad(conv_w, ((0, 8 - conv_w.shape[0]), (0, 0))),
                                 jnp.zeros((8, dloc), F32)], axis=0)
    slabs = big_b + [_cast_into_slot("place_small", small_loc, chip1, F32)]
    groups = ([0, 6], [2, 3, 4], [1], [5])
    sems, slabs = _gather_start(slabs, groups)

    def arrive(g, after):
        got = _gather_wait("gather_wait_%d" % g, [slabs[a] for a in groups[g]], sems[g], after)
        return _sibling_exchange("sibling_exchange_%d" % g, list(got))

    g1, g2, g3 = norm_mix_g.reshape(1, dm), norm_ffn_g.reshape(1, dm), norm_final_g.reshape(1, dm)
    b_gate2 = b_gate.reshape(2, dm)
    ps = pool_scale.reshape(1, dm)
    w_in4, small4 = arrive(0, chip1)
    small_f = jnp.transpose(small4, (1, 0, 2)).reshape(small4.shape[1], dm)
    meta_f = small_f[:N_META]
    conv_w_f = small_f[N_META:N_META + 3]
    h0 = jnp.concatenate([x[0], jnp.zeros((tail - N_META, dm), F32), meta_f], axis=0)
    hn1 = _rms_fwd("rms_mix", h0, g1, tm)
    proj = _nn_sharded("proj", hn1, w_in4, 6)
    pooled, z = _mixer_fwd("mixer_fwd", proj, conv_w_f, tc)
    pool4, conv_out4, w_o4 = arrive(1, pooled)
    pool4 = pool4.reshape(n_chips, n_groups, gw // n_chips, gw)
    conv_out_f = conv_out4.reshape(dm, dm)
    w_o_f = w_o4.reshape(dm, dm)
    ya = _pool_fwd("pool_proj", pooled, pool4)
    yb = _nn_plain("conv_out", z, conv_out_f, BF16)
    mix = _gate_mix("gate_mix", proj, b_gate2, ya, ps, yb, tm)
    h1 = _nn_plain("attn_out", mix, w_o_f, F32, res=h0, tn_pref=256)
    hn2 = _rms_fwd("rms_ffn", h1, g2, tm)
    (w_gu4,) = arrive(2, hn2)
    gu = _nn_sharded("gate_up", hn2, w_gu4, 2)
    act = _swiglu_fwd("swiglu", gu, tm)
    (w_down4,) = arrive(3, act)
    w_down_f = w_down4.reshape(-1, dm)
    h2 = _nn_plain("ffn_down", act, w_down_f, F32, res=h1, tn_pref=512, tk_pref=1536)
    dh2, dh2b, loss_cols, dg3 = _final_loss("final_loss", h2, g3, loss_target[0], tm)

    def scatter(tag, names_g, grads_g):
        got = _sibling_swap("sibling_swap_" + tag, grads_g)
        pairs = [_pair_add("pair_add_" + nme, g4, rv, core) for nme, g4, rv in zip(names_g, grads_g, got)]
        return _scatter_start("scatter_start_" + tag, pairs)

    dact = _nt_plain("d_act", dh2b, w_down_f)
    gw_down = _tn_plain("dw_down", act, dh2b)
    dgu = _swiglu_bwd("swiglu_bwd", dact, gu, tm)
    gw_gu = _tn_sharded("dw_gate_up", hn2, dgu, n_chips)
    flight_a, token_a = scatter("a", ["w_gate_up", "w_down"], [gw_gu, gw_down.reshape(n_chips, -1, dm)])
    dhn2 = _nt_sharded("d_hn2", dgu, w_gu4, deps=(token_a,))
    dh1, dh1b, dg2 = _rms_bwd("rms_ffn_bwd", dhn2, h1, g2, dh2, tm)
    dmix = _nt_plain("d_mix", dh1b, w_o_f)
    gw_o = _tn_plain("dw_o", mix, dh1b)
    dproj, dyb, dya, db_gate, dps = _gate_bwd("gate_bwd", dmix, proj, b_gate2, ya, ps, yb, tm)
    dz = _nt_plain("d_z", dyb, conv_out_f)
    gw_conv_out = _tn_plain("dw_conv_out", z, dyb)
    dpooled = _pool_bwd_act("d_pooled", dya, pool4)
    gw_pool = _pool_bwd_w("dw_pool", pooled, dya, n_chips)
    flight_b, token_b = scatter("b", ["w_o", "conv_out_w", "pool_w"],
                                [gw_o.reshape(n_chips, dloc, dm), gw_conv_out.reshape(n_chips, dloc, dm),
                                 gw_pool.reshape(n_chips, n_groups * (gw // n_chips), gw)])
    dproj, dconv_w = _mixer_bwd("mixer_bwd", dz, dpooled, proj, conv_w_f, dproj, tc, token_b)
    gw_in = _tn_sharded("dw_in", hn1, dproj, n_chips)
    flight_c, token_c = scatter("c", ["w_in"], [gw_in])
    dhn1 = _nt_sharded("d_hn1", dproj, w_in4, deps=(token_c,))
    dh0, _, dg1 = _rms_bwd("rms_mix_bwd", dhn1, h0, g1, dh1, tm)
    grad_x = dh0[:seq][None]
    dmeta = dh0[lp - N_META:]

    names = ["w_in", "w_gate_up", "pool_w", "conv_out_w", "w_o", "w_down"]
    g_halves = {}
    after = dh0
    for tag, names_g, flight in (("a", ["w_gate_up", "w_down"], flight_a), ("b", ["w_o", "conv_out_w", "pool_w"], flight_b),
                                 ("c", ["w_in"], flight_c)):
        pairs, zones = _scatter_wait("scatter_wait_" + tag, *flight, after)
        halves = [_chip_sum("chip_sum_" + nme, p, rv, chip1) for nme, p, rv in zip(names_g, pairs, zones)]
        sib_halves = _sibling_send("sibling_send_" + tag, halves)
        g_halves.update(zip(names_g, zip(halves, sib_halves)))
        after = sib_halves[0]

    vec = jnp.concatenate([dg1, dg2, dg3, db_gate, dps, loss_cols, dconv_w, dmeta], axis=0)
    loss_row = 5 * SMALL_ROWS
    red, loss11 = _small_all_reduce(vec, loss_row, 0.5 / dm)
    loss = loss11[0, 0]
    col0 = chip * dloc
    g_small = {
        "norm_mix_g": red[0], "norm_ffn_g": red[SMALL_ROWS], "norm_final_g": red[2 * SMALL_ROWS],
        "b_gate": red[3 * SMALL_ROWS:3 * SMALL_ROWS + 2].reshape(-1), "pool_scale": red[4 * SMALL_ROWS],
        "conv_w": lax.dynamic_slice(red, (6 * SMALL_ROWS, col0), (3, dloc)),
        "meta_tokens": lax.dynamic_slice(red, (7 * SMALL_ROWS, col0), (N_META, dloc)),
    }

    given = dict(meta_tokens=(meta_tokens, m_meta_tokens, v_meta_tokens), norm_mix_g=(norm_mix_g, m_norm_mix_g, v_norm_mix_g),
                 w_in=(w_in, m_w_in, v_w_in), b_gate=(b_gate, m_b_gate, v_b_gate), pool_w=(pool_w, m_pool_w, v_pool_w),
                 pool_scale=(pool_scale, m_pool_scale, v_pool_scale), conv_w=(conv_w, m_conv_w, v_conv_w),
                 conv_out_w=(conv_out_w, m_conv_out_w, v_conv_out_w), w_o=(w_o, m_w_o, v_w_o),
                 norm_ffn_g=(norm_ffn_g, m_norm_ffn_g, v_norm_ffn_g), w_gate_up=(w_gate_up, m_w_gate_up, v_w_gate_up),
                 w_down=(w_down, m_w_down, v_w_down), norm_final_g=(norm_final_g, m_norm_final_g, v_norm_final_g))
    order = list(given.keys())
    grad, delta, new_m, new_v = {}, {}, {}, {}
    for nme in names:
        w, m, v = given[nme]
        g_own, g_sib = g_halves[nme]
        shape2 = (2 * g_own.shape[0], g_own.shape[1])
        res4 = _adamw_halves("adamw_" + nme, w.reshape(shape2), g_own, g_sib, m.reshape(shape2), v.reshape(shape2), core)
        grad[nme], delta[nme], new_m[nme], new_v[nme] = [t.reshape(w.shape) for t in res4]
    vec_names = ["norm_mix_g", "norm_ffn_g", "norm_final_g", "pool_scale"]

    def slab_vec(pick):
        rows = [pick(nme).reshape(1, dm) for nme in vec_names] + [pick("b_gate").reshape(2, dm), jnp.zeros((2, dm), F32)]
        return jnp.concatenate(rows, axis=0)

    def slab_col(pick):
        return jnp.concatenate([pick("meta_tokens"), pick("conv_w"), jnp.zeros((5, dloc), F32)], axis=0)

    for slab, tag in ((slab_vec, "vec"), (slab_col, "col")):
        d, nm, nv = _adamw("adamw_small_" + tag, slab(lambda nme: given[nme][0]), slab(lambda nme: g_small[nme]),
                           slab(lambda nme: given[nme][1]), slab(lambda nme: given[nme][2]))
        for out, res in ((delta, d), (new_m, nm), (new_v, nv)):
            if tag == "vec":
                for i, nme in enumerate(vec_names):
                    out[nme] = res[i]
                out["b_gate"] = res[4:6].reshape(-1)
            else:
                out["meta_tokens"] = res[:N_META]
                out["conv_w"] = res[N_META:N_META + 3]
    grad.update(g_small)
    return (loss, grad_x, *[grad[nme] for nme in order], *[delta[nme] for nme in order],
            *[new_m[nme] for nme in order], *[new_v[nme] for nme in order])
```

```python
import functools
import math

import jax
import jax.numpy as jnp
from jax import lax
from jax.experimental import pallas as pl
from jax.experimental.pallas import tpu as pltpu

F32 = jnp.float32
BF16 = jnp.bfloat16
N_META = 16
POOL_WINDOWS = (2, 4, 8, 16)
EPS = 1e-6
ADAM_LR, ADAM_B1, ADAM_B2, ADAM_EPS, ADAM_WD, ADAM_STEP = 0.001, 0.9, 0.999, 1e-08, 0.01, 10
LANES = 128
V7X_VMEM_BYTES = 64 * 1024 * 1024
VMEM_LIMIT = V7X_VMEM_BYTES - 8 * 1024 * 1024
MESH = pl.DeviceIdType.MESH
ANY = pl.BlockSpec(memory_space=pl.ANY)
CHIP_FLIPS = ((1, 0), (0, 1), (1, 1))
SMALL_ROWS = 8


def _pick(n, pref):
    best = None
    for t in range(LANES, min(n, pref) + 1, LANES):
        if n % t == 0:
            best = t
    assert best is not None, (n, pref)
    return best


def _params(n_axes=0):
    sem = ("arbitrary",) * n_axes if n_axes else None
    return pltpu.CompilerParams(dimension_semantics=sem, vmem_limit_bytes=VMEM_LIMIT)


_DIMS = {
    "nn": (((1,), (0,)), ((), ())),
    "nt": (((1,), (1,)), ((), ())),
    "tn": (((0,), (0,)), ((), ())),
}


def _matmul(name, mode, a, b, out_sds, grid, a_spec, b_spec, o_spec, nk, res=None, res_spec=None, acc_shape=None, deps=()):
    out_dtype = out_sds.dtype
    in_place = nk > 1 and out_dtype == F32
    use_scratch = nk > 1 and not in_place
    rows = a_spec.block_shape[-2] if mode != "tn" else None
    chunk = _pick(rows, 768) if rows is not None and rows % LANES == 0 else rows
    n_in = 2 + (res is not None) + len(deps)

    def body(*refs):
        a_ref, b_ref = refs[:2]
        r_ref = refs[2] if res is not None else None
        o_ref, *scr = refs[n_in:]
        k = pl.program_id(len(grid) - 1) if nk > 1 else None

        def emit(sl):
            if sl is None:
                part = lax.dot_general(a_ref[...], b_ref[...], _DIMS[mode], preferred_element_type=F32)
                idx = (slice(None), slice(None))
            else:
                part = lax.dot_general(a_ref[sl, :], b_ref[...], _DIMS[mode], preferred_element_type=F32)
                idx = (sl, slice(None))
            if nk == 1:
                if r_ref is not None:
                    part = part + r_ref[idx]
                o_ref[idx] = part.astype(out_dtype)
                return
            acc = scr[0] if use_scratch else o_ref

            @pl.when(k == 0)
            def _():
                first = part
                if r_ref is not None and in_place:
                    first = first + r_ref[idx]
                acc[idx] = first

            @pl.when(k > 0)
            def _():
                acc[idx] += part

            if use_scratch:

                @pl.when(k == nk - 1)
                def _():
                    o_ref[idx] = acc[idx].astype(out_dtype)

        if mode == "tn" or chunk == rows:
            emit(None)
        else:
            for m0 in range(0, rows, chunk):
                emit(pl.ds(m0, chunk))

    ins = [a, b] + ([res] if res is not None else []) + list(deps)
    in_specs = [a_spec, b_spec] + ([res_spec] if res is not None else []) + [ANY] * len(deps)
    scratch = [pltpu.VMEM(acc_shape, F32)] if use_scratch else []
    return pl.pallas_call(
        body, name=name, out_shape=out_sds, grid=grid, in_specs=in_specs, out_specs=o_spec,
        scratch_shapes=scratch, compiler_params=_params(len(grid)),
    )(*ins)


def _nn_sharded(name, a, w4, nseg):
    lp, kdim = a.shape
    s, _, nloc = w4.shape
    segw = s * nloc // nseg
    tn = _pick(math.gcd(nloc, segw), 1536)
    bw, bo = nloc // tn, segw // tn
    return _matmul(
        name, "nn", a, w4, jax.ShapeDtypeStruct((nseg, lp, segw), BF16), (s * bw,),
        pl.BlockSpec((lp, kdim), lambda j: (0, 0)),
        pl.BlockSpec((None, kdim, tn), lambda j: (j // bw, 0, j % bw)),
        pl.BlockSpec((None, lp, tn), lambda j: (j // bo, 0, j % bo)), 1)


def _nn_plain(name, a, w, out_dtype, res=None, tn_pref=512, tk_pref=2048):
    lp, kdim = a.shape
    n = w.shape[1]
    tn = _pick(n, tn_pref)
    tk = kdim if kdim <= tk_pref else _pick(kdim, tk_pref)
    nk = kdim // tk
    grid = (n // tn, nk) if nk > 1 else (n // tn,)
    if nk > 1:
        a_spec = pl.BlockSpec((lp, tk), lambda j, k: (0, k))
        w_spec = pl.BlockSpec((tk, tn), lambda j, k: (k, j))
        o_spec = pl.BlockSpec((lp, tn), lambda j, k: (0, j))
    else:
        a_spec = pl.BlockSpec((lp, tk), lambda j: (0, 0))
        w_spec = pl.BlockSpec((tk, tn), lambda j: (0, j))
        o_spec = pl.BlockSpec((lp, tn), lambda j: (0, j))
    return _matmul(name, "nn", a, w, jax.ShapeDtypeStruct((lp, n), out_dtype), grid, a_spec, w_spec, o_spec, nk,
                   res=res, res_spec=o_spec if res is not None else None, acc_shape=(lp, tn))


def _nt_plain(name, a, w, tn_pref=512):
    lp, kdim = a.shape
    n = w.shape[0]
    tn = _pick(n, tn_pref)
    return _matmul(
        name, "nt", a, w, jax.ShapeDtypeStruct((lp, n), BF16), (n // tn,),
        pl.BlockSpec((lp, kdim), lambda j: (0, 0)),
        pl.BlockSpec((tn, kdim), lambda j: (j, 0)),
        pl.BlockSpec((lp, tn), lambda j: (0, j)), 1)


def _nt_sharded(name, dseg, w4, to_pref=1024, deps=()):
    nseg, lp, segw = dseg.shape
    s, kdim, nloc = w4.shape
    tr = _pick(math.gcd(nloc, segw), 1536)
    ba, bw = segw // tr, nloc // tr
    nr = s * bw
    to = _pick(kdim, to_pref)
    return _matmul(
        name, "nt", dseg, w4, jax.ShapeDtypeStruct((lp, kdim), F32), (kdim // to, nr),
        pl.BlockSpec((None, lp, tr), lambda j, r: (r // ba, 0, r % ba)),
        pl.BlockSpec((None, to, tr), lambda j, r: (r // bw, j, r % bw)),
        pl.BlockSpec((lp, to), lambda j, r: (0, j)), nr, deps=deps)


def _tn_plain(name, a, d, tk_pref=512):
    lp, kdim = a.shape
    n = d.shape[1]
    tk = _pick(kdim, tk_pref)
    return _matmul(
        name, "tn", a, d, jax.ShapeDtypeStruct((kdim, n), BF16), (kdim // tk,),
        pl.BlockSpec((lp, tk), lambda i: (0, i)),
        pl.BlockSpec((lp, n), lambda i: (0, 0)),
        pl.BlockSpec((tk, n), lambda i: (i, 0)), 1)


def _tn_sharded(name, a, dseg, s, tk_pref=512):
    lp, kdim = a.shape
    nseg, _, segw = dseg.shape
    nloc = nseg * segw // s
    tn = _pick(math.gcd(nloc, segw), 1536)
    bd, bo = segw // tn, nloc // tn
    tk = _pick(kdim, tk_pref)
    return _matmul(
        name, "tn", a, dseg, jax.ShapeDtypeStruct((s, kdim, nloc), BF16), (s * bo, kdim // tk),
        pl.BlockSpec((lp, tk), lambda j, i: (0, i)),
        pl.BlockSpec((None, lp, tn), lambda j, i: (j // bd, 0, j % bd)),
        pl.BlockSpec((None, tk, tn), lambda j, i: (j // bo, i, j % bo)), 1)


def _pool_fwd(name, pooled, pw4):
    lp, dm = pooled.shape
    s, g, rs, gw = pw4.shape
    return _matmul(
        name, "nn", pooled, pw4, jax.ShapeDtypeStruct((lp, dm), BF16), (g, s),
        pl.BlockSpec((lp, rs), lambda gi, si: (0, gi * s + si)),
        pl.BlockSpec((None, None, rs, gw), lambda gi, si: (si, gi, 0, 0)),
        pl.BlockSpec((lp, gw), lambda gi, si: (0, gi)), s, acc_shape=(lp, gw))


def _pool_bwd_act(name, dya, pw4, deps=()):
    lp, dm = dya.shape
    s, g, rs, gw = pw4.shape
    return _matmul(
        name, "nt", dya, pw4, jax.ShapeDtypeStruct((lp, dm), BF16), (g, s),
        pl.BlockSpec((lp, gw), lambda gi, si: (0, gi)),
        pl.BlockSpec((None, None, rs, gw), lambda gi, si: (si, gi, 0, 0)),
        pl.BlockSpec((lp, rs), lambda gi, si: (0, gi * s + si)), 1, deps=deps)


def _pool_bwd_w(name, pooled, dya, s):
    lp, dm = pooled.shape
    g = len(POOL_WINDOWS)
    gw = dm // g
    rs = gw // s
    return _matmul(
        name, "tn", pooled, dya, jax.ShapeDtypeStruct((s, g, rs, gw), BF16), (g, s),
        pl.BlockSpec((lp, rs), lambda gi, si: (0, gi * s + si)),
        pl.BlockSpec((lp, gw), lambda gi, si: (0, gi)),
        pl.BlockSpec((None, None, rs, gw), lambda gi, si: (si, gi, 0, 0)), 1)


def _rms_fwd(name, h, g, tm):
    lp, dm = h.shape

    def body(h_ref, g_ref, o_ref):
        hv = h_ref[...]
        r = lax.rsqrt(jnp.mean(hv * hv, axis=-1, keepdims=True) + EPS)
        o_ref[...] = (hv * r * g_ref[...]).astype(BF16)

    row = pl.BlockSpec((tm, dm), lambda i: (i, 0))
    return pl.pallas_call(
        body, name=name, out_shape=jax.ShapeDtypeStruct((lp, dm), BF16), grid=(lp // tm,),
        in_specs=[row, pl.BlockSpec((1, dm), lambda i: (0, 0))], out_specs=row, compiler_params=_params(1),
    )(h, g)


def _rms_bwd(name, dy, h, g, dres, tm, dep):
    lp, dm = h.shape

    def body(dy_ref, h_ref, g_ref, dr_ref, _, dh_ref, dhb_ref, dg_ref):
        hv = h_ref[...]
        r = lax.rsqrt(jnp.mean(hv * hv, axis=-1, keepdims=True) + EPS)
        xhat = hv * r
        dyv = dy_ref[...]
        dxh = dyv * g_ref[...]
        dh = dr_ref[...] + r * (dxh - xhat * jnp.mean(dxh * xhat, axis=-1, keepdims=True))
        dh_ref[...] = dh
        dhb_ref[...] = dh.astype(BF16)

        @pl.when(pl.program_id(0) == 0)
        def _():
            dg_ref[...] = jnp.zeros_like(dg_ref)

        dg_ref[0:1, :] += jnp.sum(dyv * xhat, axis=0, keepdims=True)

    row = pl.BlockSpec((tm, dm), lambda i: (i, 0))
    slab = pl.BlockSpec((SMALL_ROWS, dm), lambda i: (0, 0))
    return pl.pallas_call(
        body, name=name, grid=(lp // tm,),
        out_shape=(jax.ShapeDtypeStruct((lp, dm), F32), jax.ShapeDtypeStruct((lp, dm), BF16),
                   jax.ShapeDtypeStruct((SMALL_ROWS, dm), F32)),
        in_specs=[row, row, pl.BlockSpec((1, dm), lambda i: (0, 0)), row, ANY], out_specs=(row, row, slab),
        compiler_params=_params(1),
    )(dy, h, g, dres, dep)


def _gate_mix(name, proj, b_gate2, ya, pool_scale, yb, tm):
    _, lp, dm = proj.shape

    def body(ga_ref, gr_ref, b_ref, ya_ref, ps_ref, yb_ref, o_ref):
        g_a = jax.nn.sigmoid(ga_ref[...].astype(F32) + b_ref[0:1, :])
        g_b = jax.nn.sigmoid(gr_ref[...].astype(F32) + b_ref[1:2, :])
        y_a = ya_ref[...].astype(F32) * ps_ref[...]
        o_ref[...] = (g_a * y_a + g_b * yb_ref[...].astype(F32)).astype(BF16)

    row = pl.BlockSpec((tm, dm), lambda i: (i, 0))
    return pl.pallas_call(
        body, name=name, out_shape=jax.ShapeDtypeStruct((lp, dm), BF16), grid=(lp // tm,),
        in_specs=[pl.BlockSpec((None, tm, dm), lambda i: (4, i, 0)), pl.BlockSpec((None, tm, dm), lambda i: (5, i, 0)),
                  pl.BlockSpec((2, dm), lambda i: (0, 0)), row, pl.BlockSpec((1, dm), lambda i: (0, 0)), row],
        out_specs=row, compiler_params=_params(1),
    )(proj, proj, b_gate2, ya, pool_scale, yb)


def _gate_bwd(name, dmix, proj, b_gate2, ya, pool_scale, yb, tm):
    _, lp, dm = proj.shape

    def body(dm_ref, ga_ref, gr_ref, b_ref, ya_ref, ps_ref, yb_ref, dp_ref, dyb_ref, dya_ref, db_ref, dps_ref):
        dmx = dm_ref[...].astype(F32)
        g_a = jax.nn.sigmoid(ga_ref[...].astype(F32) + b_ref[0:1, :])
        g_b = jax.nn.sigmoid(gr_ref[...].astype(F32) + b_ref[1:2, :])
        ya_pre = ya_ref[...].astype(F32)
        ybv = yb_ref[...].astype(F32)
        ps = ps_ref[...]
        dga = dmx * (ya_pre * ps) * (g_a * (1.0 - g_a))
        dgr = dmx * ybv * (g_b * (1.0 - g_b))
        dp_ref[0] = dga.astype(BF16)
        dp_ref[1] = dgr.astype(BF16)
        dyb_ref[...] = (dmx * g_b).astype(BF16)
        dya_ref[...] = (dmx * g_a * ps).astype(BF16)

        @pl.when(pl.program_id(0) == 0)
        def _():
            db_ref[...] = jnp.zeros_like(db_ref)
            dps_ref[...] = jnp.zeros_like(dps_ref)

        db_ref[0:1, :] += jnp.sum(dga, axis=0, keepdims=True)
        db_ref[1:2, :] += jnp.sum(dgr, axis=0, keepdims=True)
        dps_ref[0:1, :] += jnp.sum(dmx * g_a * ya_pre, axis=0, keepdims=True)

    row = pl.BlockSpec((tm, dm), lambda i: (i, 0))
    one = pl.BlockSpec((1, dm), lambda i: (0, 0))
    slab = pl.BlockSpec((SMALL_ROWS, dm), lambda i: (0, 0))
    return pl.pallas_call(
        body, name=name, grid=(lp // tm,),
        out_shape=(jax.ShapeDtypeStruct((6, lp, dm), BF16), jax.ShapeDtypeStruct((lp, dm), BF16),
                   jax.ShapeDtypeStruct((lp, dm), BF16), jax.ShapeDtypeStruct((SMALL_ROWS, dm), F32),
                   jax.ShapeDtypeStruct((SMALL_ROWS, dm), F32)),
        in_specs=[row, pl.BlockSpec((None, tm, dm), lambda i: (4, i, 0)), pl.BlockSpec((None, tm, dm), lambda i: (5, i, 0)),
                  pl.BlockSpec((2, dm), lambda i: (0, 0)), row, one, row],
        out_specs=(pl.BlockSpec((2, tm, dm), lambda i: (2, i, 0)), row, row, slab, slab),
        compiler_params=_params(1),
    )(dmix, proj, proj, b_gate2, ya, pool_scale, yb)


def _swiglu_fwd(name, gu, tm):
    _, lp, f = gu.shape

    def body(g_ref, u_ref, o_ref):
        gt = g_ref[...].astype(F32)
        o_ref[...] = (gt * jax.nn.sigmoid(gt) * u_ref[...].astype(F32)).astype(BF16)

    return pl.pallas_call(
        body, name=name, out_shape=jax.ShapeDtypeStruct((lp, f), BF16), grid=(lp // tm,),
        in_specs=[pl.BlockSpec((None, tm, f), lambda i: (0, i, 0)), pl.BlockSpec((None, tm, f), lambda i: (1, i, 0))],
        out_specs=pl.BlockSpec((tm, f), lambda i: (i, 0)), compiler_params=_params(1),
    )(gu, gu)


def _swiglu_bwd(name, dact, gu, tm):
    _, lp, f = gu.shape

    def body(d_ref, g_ref, u_ref, o_ref):
        d = d_ref[...].astype(F32)
        gt = g_ref[...].astype(F32)
        sg = jax.nn.sigmoid(gt)
        o_ref[0] = (d * u_ref[...].astype(F32) * (sg * (1.0 + gt * (1.0 - sg)))).astype(BF16)
        o_ref[1] = (d * (gt * sg)).astype(BF16)

    return pl.pallas_call(
        body, name=name, out_shape=jax.ShapeDtypeStruct((2, lp, f), BF16), grid=(lp // tm,),
        in_specs=[pl.BlockSpec((tm, f), lambda i: (i, 0)), pl.BlockSpec((None, tm, f), lambda i: (0, i, 0)),
                  pl.BlockSpec((None, tm, f), lambda i: (1, i, 0))],
        out_specs=pl.BlockSpec((2, tm, f), lambda i: (0, i, 0)), compiler_params=_params(1),
    )(dact, gu, gu)


def _final_loss(name, h2, g3, target, tm):
    lp, dm = h2.shape
    nx = target.shape[0] // tm

    def body(h_ref, g_ref, t_ref, dh_ref, dhb_ref, ls_ref, dg_ref):
        i = pl.program_id(0)

        @pl.when(i == 0)
        def _():
            ls_ref[...] = jnp.zeros_like(ls_ref)
            dg_ref[...] = jnp.zeros_like(dg_ref)

        @pl.when(i < nx)
        def _():
            hv = h_ref[...]
            gv = g_ref[...]
            r = lax.rsqrt(jnp.mean(hv * hv, axis=-1, keepdims=True) + EPS)
            xhat = hv * r
            err = xhat * gv - t_ref[...]
            dout = err * (1.0 / dm)
            dxh = dout * gv
            dh = r * (dxh - xhat * jnp.mean(dxh * xhat, axis=-1, keepdims=True))
            dh_ref[...] = dh
            dhb_ref[...] = dh.astype(BF16)
            ls_ref[0:1, :] += jnp.sum(err * err, axis=0, keepdims=True)
            dg_ref[0:1, :] += jnp.sum(dout * xhat, axis=0, keepdims=True)

        @pl.when(i >= nx)
        def _():
            dh_ref[...] = jnp.zeros_like(dh_ref)
            dhb_ref[...] = jnp.zeros_like(dhb_ref)

    row = pl.BlockSpec((tm, dm), lambda i: (i, 0))
    slab = pl.BlockSpec((SMALL_ROWS, dm), lambda i: (0, 0))
    return pl.pallas_call(
        body, name=name, grid=(lp // tm,),
        out_shape=(jax.ShapeDtypeStruct((lp, dm), F32), jax.ShapeDtypeStruct((lp, dm), BF16),
                   jax.ShapeDtypeStruct((SMALL_ROWS, dm), F32), jax.ShapeDtypeStruct((SMALL_ROWS, dm), F32)),
        in_specs=[row, pl.BlockSpec((1, dm), lambda i: (0, 0)), pl.BlockSpec((tm, dm), lambda i: (jnp.minimum(i, nx - 1), 0))],
        out_specs=(row, row, slab, slab), compiler_params=_params(1),
    )(h2, g3, target)


def _shift(v, k):
    return pltpu.roll(v, k % v.shape[0], axis=0)


def _window_sum(v, group, sign):
    s2 = v + _shift(v, sign * 1)
    s4 = s2 + _shift(s2, sign * 2)
    s8 = s4 + _shift(s4, sign * 4)
    s16 = s8 + _shift(s8, sign * 8)
    return jnp.where(group == 0, s2, jnp.where(group == 1, s4, jnp.where(group == 2, s8, s16)))


def _pool_count(lp, group):
    row = lax.broadcasted_iota(jnp.int32, (lp, 1), 0)
    window = jnp.left_shift(2, group).astype(F32)
    meta_pos = (row - (lp - N_META) + 1).astype(F32)
    return jnp.where(row >= lp - N_META, jnp.minimum(meta_pos, window), window)


def _mixer_fwd(name, proj, conv_w, tc):
    _, lp, dm = proj.shape
    per_group = dm // len(POOL_WINDOWS) // tc

    def body(u_ref, gb_ref, gc_ref, v_ref, cw_ref, p_ref, z_ref):
        group = pl.program_id(0) // per_group
        u = u_ref[...].astype(F32)
        p_ref[...] = (_window_sum(u, group, 1) / _pool_count(lp, group) - u).astype(BF16)
        cv = gc_ref[...].astype(F32) * v_ref[...].astype(F32)
        conv = cw_ref[0:1, :] * _shift(cv, 2) + cw_ref[1:2, :] * _shift(cv, 1) + cw_ref[2:3, :] * cv
        z_ref[...] = (gb_ref[...].astype(F32) * conv).astype(BF16)

    def seg(s):
        return pl.BlockSpec((None, lp, tc), lambda j: (s, 0, j))

    col = pl.BlockSpec((lp, tc), lambda j: (0, j))
    return pl.pallas_call(
        body, name=name, grid=(dm // tc,),
        out_shape=(jax.ShapeDtypeStruct((lp, dm), BF16), jax.ShapeDtypeStruct((lp, dm), BF16)),
        in_specs=[seg(0), seg(1), seg(2), seg(3), pl.BlockSpec((3, tc), lambda j: (0, j))],
        out_specs=(col, col), compiler_params=_params(1),
    )(proj, proj, proj, proj, conv_w)


def _mixer_bwd(name, dz, dpooled, proj, conv_w, dproj, tc, dep):
    _, lp, dm = proj.shape
    per_group = dm // len(POOL_WINDOWS) // tc

    def body(dz_ref, dp_ref, gb_ref, gc_ref, v_ref, cw_ref, _, __, o_ref, dcw_ref):
        group = pl.program_id(0) // per_group
        dzv = dz_ref[...].astype(F32)
        gb = gb_ref[...].astype(F32)
        gc = gc_ref[...].astype(F32)
        vv = v_ref[...].astype(F32)
        cv = gc * vv
        c1 = _shift(cv, 1)
        c2 = _shift(cv, 2)
        w0, w1, w2 = cw_ref[0:1, :], cw_ref[1:2, :], cw_ref[2:3, :]
        o_ref[1] = (dzv * (w0 * c2 + w1 * c1 + w2 * cv)).astype(BF16)
        dconv = dzv * gb
        dcw_ref[...] = jnp.zeros_like(dcw_ref)
        dcw_ref[0:1, :] = jnp.sum(dconv * c2, axis=0, keepdims=True)
        dcw_ref[1:2, :] = jnp.sum(dconv * c1, axis=0, keepdims=True)
        dcw_ref[2:3, :] = jnp.sum(dconv * cv, axis=0, keepdims=True)
        dcv = w0 * _shift(dconv, -2) + w1 * _shift(dconv, -1) + w2 * dconv
        o_ref[2] = (dcv * vv).astype(BF16)
        o_ref[3] = (dcv * gc).astype(BF16)
        dpv = dp_ref[...].astype(F32)
        o_ref[0] = (_window_sum(dpv / _pool_count(lp, group), group, -1) - dpv).astype(BF16)

    def seg(s):
        return pl.BlockSpec((None, lp, tc), lambda j: (s, 0, j))

    col = pl.BlockSpec((lp, tc), lambda j: (0, j))
    return pl.pallas_call(
        body, name=name, grid=(dm // tc,),
        out_shape=(jax.ShapeDtypeStruct(dproj.shape, BF16), jax.ShapeDtypeStruct((SMALL_ROWS, dm), F32)),
        in_specs=[col, col, seg(1), seg(2), seg(3), pl.BlockSpec((3, tc), lambda j: (0, j)), ANY, ANY],
        out_specs=(pl.BlockSpec((4, lp, tc), lambda j: (0, 0, j)), pl.BlockSpec((SMALL_ROWS, tc), lambda j: (0, j))),
        input_output_aliases={6: 0}, compiler_params=_params(1),
    )(dz, dpooled, proj, proj, proj, conv_w, dproj, dep)


def _row_tile(r, c, bytes_per_row_elem=4, budget=2 * 1024 * 1024):
    best = None
    for t in range(16, r + 1, 16):
        if r % t == 0 and t * c * bytes_per_row_elem <= budget:
            best = t
    return best if best is not None else r


def _pair_add(name, g4, recv, core):
    s, r, c = g4.shape
    h = r // 2
    tr = _row_tile(h, c)
    nb = h // tr

    def body(core_ref, g_ref, r_ref, o_ref):
        o_ref[...] = (g_ref[...].astype(F32) + r_ref[...].astype(F32)).astype(BF16)

    grid_spec = pltpu.PrefetchScalarGridSpec(
        num_scalar_prefetch=1, grid=(s, nb),
        in_specs=[pl.BlockSpec((None, tr, c), lambda si, j, core_ref: (si, core_ref[0] * nb + j, 0)),
                  pl.BlockSpec((None, tr, c), lambda si, j, core_ref: (si, j, 0))],
        out_specs=pl.BlockSpec((None, tr, c), lambda si, j, core_ref: (si, j, 0)))
    return pl.pallas_call(
        body, name=name, out_shape=jax.ShapeDtypeStruct((s, h, c), BF16), grid_spec=grid_spec,
        compiler_params=_params(2),
    )(core, g4, recv)


def _chip_sum(name, parts, recv, chip):
    _, h, c = parts.shape
    tr = _row_tile(h, c)

    def body(chip_ref, p_ref, r_ref, o_ref):
        acc = p_ref[...].astype(F32)
        for i in range(len(CHIP_FLIPS)):
            acc = acc + r_ref[i].astype(F32)
        o_ref[...] = acc

    grid_spec = pltpu.PrefetchScalarGridSpec(
        num_scalar_prefetch=1, grid=(h // tr,),
        in_specs=[pl.BlockSpec((None, tr, c), lambda j, chip_ref: (chip_ref[0], j, 0)),
                  pl.BlockSpec((len(CHIP_FLIPS), tr, c), lambda j, chip_ref: (0, j, 0))],
        out_specs=pl.BlockSpec((tr, c), lambda j, chip_ref: (j, 0)))
    return pl.pallas_call(
        body, name=name, out_shape=jax.ShapeDtypeStruct((h, c), F32), grid_spec=grid_spec, compiler_params=_params(1),
    )(chip, parts, recv)


def _adam_update(w, gv, m, v):
    c1 = 1.0 - ADAM_B1 ** ADAM_STEP
    c2 = 1.0 - ADAM_B2 ** ADAM_STEP
    nm = ADAM_B1 * m + (1.0 - ADAM_B1) * gv
    nv = ADAM_B2 * v + (1.0 - ADAM_B2) * (gv * gv)
    return -ADAM_LR * ((nm / c1) / (jnp.sqrt(nv / c2) + ADAM_EPS) + ADAM_WD * w), nm, nv


def _adamw_halves(name, w, g_own, g_sib, m, v, core):
    r, c = w.shape
    h = r // 2
    tr = _row_tile(h, c, budget=1024 * 1024)
    nbh = h // tr

    def body(core_ref, w_ref, go_ref, gs_ref, m_ref, v_ref, g_ref, d_ref, nm_ref, nv_ref):
        mine = (pl.program_id(0) // nbh) == core_ref[0]
        gv = jnp.where(mine, go_ref[...], gs_ref[...])
        g_ref[...] = gv
        d_ref[...], nm_ref[...], nv_ref[...] = _adam_update(w_ref[...], gv, m_ref[...], v_ref[...])

    def blk(fn):
        return pl.BlockSpec((tr, c), fn)

    full = blk(lambda j, core_ref: (j, 0))
    own = blk(lambda j, core_ref: (jnp.clip(j - core_ref[0] * nbh, 0, nbh - 1), 0))
    sib = blk(lambda j, core_ref: (jnp.clip(j - (1 - core_ref[0]) * nbh, 0, nbh - 1), 0))
    grid_spec = pltpu.PrefetchScalarGridSpec(
        num_scalar_prefetch=1, grid=(r // tr,), in_specs=[full, own, sib, full, full], out_specs=(full,) * 4)
    sds = jax.ShapeDtypeStruct((r, c), F32)
    return pl.pallas_call(
        body, name=name, out_shape=(sds,) * 4, grid_spec=grid_spec, compiler_params=_params(1),
    )(core, w, g_own, g_sib, m, v)


def _adamw(name, w, g, m, v):
    r, c = w.shape

    def body(w_ref, g_ref, m_ref, v_ref, d_ref, nm_ref, nv_ref):
        d_ref[...], nm_ref[...], nv_ref[...] = _adam_update(w_ref[...], g_ref[...], m_ref[...], v_ref[...])

    blk = pl.BlockSpec((r, c), lambda j: (0, 0))
    sds = jax.ShapeDtypeStruct((r, c), F32)
    return pl.pallas_call(
        body, name=name, out_shape=(sds, sds, sds), grid=(1,), in_specs=[blk] * 4, out_specs=(blk,) * 3,
        compiler_params=_params(1),
    )(w, g, m, v)


def _cast_into_slot(name, w, chip, dtype, deps=()):
    r, c = w.shape
    tr = _row_tile(r, c)

    def body(chip_ref, w_ref, *rest):
        rest[-1][...] = w_ref[...].astype(dtype)

    grid_spec = pltpu.PrefetchScalarGridSpec(
        num_scalar_prefetch=1, grid=(r // tr,),
        in_specs=[pl.BlockSpec((tr, c), lambda j, chip_ref: (j, 0))] + [ANY] * len(deps),
        out_specs=pl.BlockSpec((None, tr, c), lambda j, chip_ref: (chip_ref[0], j, 0)))
    return pl.pallas_call(
        body, name=name, out_shape=jax.ShapeDtypeStruct((4, r, c), dtype), grid_spec=grid_spec, compiler_params=_params(1),
    )(chip, w, *deps)


def _place():
    return lax.axis_index("x"), lax.axis_index("y"), lax.axis_index("c")


def _chip_of(x, y, flip):
    px, py = x ^ flip[0], y ^ flip[1]
    return px, py, 2 * px + py


def _half(ref, which):
    rows = ref.shape[0] // 2
    return ref.at[pl.ds(which * rows, rows)]


HBM = pl.BlockSpec(memory_space=pltpu.HBM)
SEM = pl.BlockSpec(memory_space=pltpu.SEMAPHORE)
SPLIT_COPY = pltpu.CompilerParams(has_side_effects=pltpu.SideEffectType.DATAFLOW_SIDE_EFFECTING)


def _in_hbm(arrays):
    return [pltpu.with_memory_space_constraint(t, pltpu.HBM) for t in arrays]


TOKEN = jax.ShapeDtypeStruct((SMALL_ROWS, LANES), F32)
TOKEN_SPEC = pl.BlockSpec(memory_space=pltpu.VMEM)


def _gather_start(name, slabs, groups):
    n = len(slabs)
    ng = len(groups)
    nf = len(CHIP_FLIPS)

    def body(*refs):
        sems, outs = refs[n:n + 2 * ng], refs[n + 2 * ng:2 * n + 2 * ng]
        token = refs[2 * n + 2 * ng]
        token[...] = jnp.zeros_like(token)
        x, y, c = _place()
        k = 2 * x + y
        for g, members in enumerate(groups):
            for i, a in enumerate(members):
                for j, flip in enumerate(CHIP_FLIPS):
                    px, py, _ = _chip_of(x, y, flip)
                    mine = _half(outs[a].at[k], c)
                    pltpu.make_async_remote_copy(
                        src_ref=mine, dst_ref=mine, send_sem=sems[2 * g].at[i * nf + j], recv_sem=sems[2 * g + 1].at[i * nf + j],
                        device_id=(px, py, c), device_id_type=MESH).start()

    sem_shapes = []
    for members in groups:
        sem_shapes += [pltpu.SemaphoreType.DMA((nf * len(members),))] * 2
    res = pl.pallas_call(
        body, name=name,
        out_shape=tuple(sem_shapes) + tuple(pltpu.HBM(t.shape, t.dtype) for t in slabs) + (TOKEN,),
        in_specs=[HBM] * n, out_specs=tuple([SEM] * (2 * ng) + [HBM] * n + [TOKEN_SPEC]),
        input_output_aliases={a: 2 * ng + a for a in range(n)}, compiler_params=SPLIT_COPY,
    )(*_in_hbm(slabs))
    return [(res[2 * g], res[2 * g + 1]) for g in range(ng)], list(res[2 * ng:2 * ng + n]), res[2 * ng + n]


def _gather_wait(name, slabs, sems, after):
    n = len(slabs)
    nf = len(CHIP_FLIPS)

    def body(*refs):
        ins = refs[:n]
        ssem, rsem = refs[n], refs[n + 1]
        x, y, c = _place()
        k = 2 * x + y
        for a in range(n):
            for j, flip in enumerate(CHIP_FLIPS):
                _, _, kj = _chip_of(x, y, flip)
                cp = pltpu.make_async_remote_copy(
                    src_ref=_half(ins[a].at[k], c), dst_ref=_half(ins[a].at[kj], c), send_sem=ssem.at[a * nf + j],
                    recv_sem=rsem.at[a * nf + j], device_id=(x, y, c), device_id_type=MESH)
                cp.wait_send()
                cp.wait_recv()

    return pl.pallas_call(
        body, name=name, out_shape=tuple(pltpu.HBM(t.shape, t.dtype) for t in slabs),
        in_specs=[HBM] * n + [SEM, SEM, ANY], out_specs=tuple([HBM] * n),
        input_output_aliases={a: a for a in range(n)}, compiler_params=SPLIT_COPY,
    )(*slabs, sems[0], sems[1], after)


def _gather_pass(name, slabs, sems, after):
    n = len(slabs)
    nf = len(CHIP_FLIPS)

    def body(*refs):
        ins = refs[:n]
        ssem, rsem = refs[n], refs[n + 1]
        ssem2, rsem2 = refs[n + 3], refs[n + 4]
        x, y, c = _place()
        k = 2 * x + y
        for a in range(n):
            for j, flip in enumerate(CHIP_FLIPS):
                _, _, kj = _chip_of(x, y, flip)
                landed = _half(ins[a].at[kj], c)
                cp = pltpu.make_async_remote_copy(
                    src_ref=_half(ins[a].at[k], c), dst_ref=landed, send_sem=ssem.at[a * nf + j],
                    recv_sem=rsem.at[a * nf + j], device_id=(x, y, c), device_id_type=MESH)
                cp.wait_send()
                cp.wait_recv()
                pltpu.make_async_remote_copy(
                    src_ref=landed, dst_ref=landed, send_sem=ssem2.at[a * nf + j], recv_sem=rsem2.at[a * nf + j],
                    device_id=(x, y, 1 - c), device_id_type=MESH).start()

    sem = pltpu.SemaphoreType.DMA((nf * n,))
    res = pl.pallas_call(
        body, name=name, out_shape=(sem, sem) + tuple(pltpu.HBM(t.shape, t.dtype) for t in slabs),
        in_specs=[HBM] * n + [SEM, SEM, ANY], out_specs=tuple([SEM, SEM] + [HBM] * n),
        input_output_aliases={a: 2 + a for a in range(n)}, compiler_params=SPLIT_COPY,
    )(*slabs, sems[0], sems[1], after)
    return (res[0], res[1]), list(res[2:])


def _pass_wait(name, slabs, sems, after):
    n = len(slabs)
    nf = len(CHIP_FLIPS)

    def body(*refs):
        ins = refs[:n]
        ssem, rsem = refs[n], refs[n + 1]
        x, y, c = _place()
        for a in range(n):
            for j, flip in enumerate(CHIP_FLIPS):
                _, _, kj = _chip_of(x, y, flip)
                cp = pltpu.make_async_remote_copy(
                    src_ref=_half(ins[a].at[kj], c), dst_ref=_half(ins[a].at[kj], 1 - c), send_sem=ssem.at[a * nf + j],
                    recv_sem=rsem.at[a * nf + j], device_id=(x, y, c), device_id_type=MESH)
                cp.wait_send()
                cp.wait_recv()

    return pl.pallas_call(
        body, name=name, out_shape=tuple(pltpu.HBM(t.shape, t.dtype) for t in slabs),
        in_specs=[HBM] * n + [SEM, SEM, ANY], out_specs=tuple([HBM] * n),
        input_output_aliases={a: a for a in range(n)}, compiler_params=SPLIT_COPY,
    )(*slabs, sems[0], sems[1], after)


def _swap_start(name, grads):
    n = len(grads)

    def body(*refs):
        ssem, rsem = refs[2 * n], refs[2 * n + 1]
        src, land = refs[2 * n + 2:3 * n + 2], refs[3 * n + 2:4 * n + 2]
        token = refs[4 * n + 2]
        token[...] = jnp.zeros_like(token)
        x, y, c = _place()
        for a in range(n):
            h = src[a].shape[1] // 2
            pltpu.make_async_remote_copy(
                src_ref=src[a].at[:, pl.ds((1 - c) * h, h)], dst_ref=land[a], send_sem=ssem.at[a], recv_sem=rsem.at[a],
                device_id=(x, y, 1 - c), device_id_type=MESH).start()

    zones = [lax.empty((g.shape[0], g.shape[1] // 2, g.shape[2]), g.dtype) for g in grads]
    sem = pltpu.SemaphoreType.DMA((n,))
    res = pl.pallas_call(
        body, name=name,
        out_shape=(sem, sem) + tuple(pltpu.HBM(t.shape, t.dtype) for t in list(grads) + zones) + (TOKEN,),
        in_specs=[HBM] * (2 * n), out_specs=tuple([SEM, SEM] + [HBM] * (2 * n) + [TOKEN_SPEC]),
        input_output_aliases={i: 2 + i for i in range(2 * n)}, compiler_params=SPLIT_COPY,
    )(*_in_hbm(list(grads) + zones))
    return (res[0], res[1], list(res[2:2 + n]), list(res[2 + n:2 + 2 * n])), res[2 + 2 * n]


def _swap_wait(name, ssem, rsem, grads, zones, after):
    n = len(grads)

    def body(*refs):
        src, land = refs[:n], refs[n:2 * n]
        ss, rs = refs[2 * n], refs[2 * n + 1]
        x, y, c = _place()
        for a in range(n):
            h = src[a].shape[1] // 2
            cp = pltpu.make_async_remote_copy(
                src_ref=src[a].at[:, pl.ds((1 - c) * h, h)], dst_ref=land[a], send_sem=ss.at[a], recv_sem=rs.at[a],
                device_id=(x, y, c), device_id_type=MESH)
            cp.wait_send()
            cp.wait_recv()

    res = pl.pallas_call(
        body, name=name, out_shape=tuple(pltpu.HBM(t.shape, t.dtype) for t in list(grads) + list(zones)),
        in_specs=[HBM] * (2 * n) + [SEM, SEM, ANY], out_specs=tuple([HBM] * (2 * n)),
        input_output_aliases={i: i for i in range(2 * n)}, compiler_params=SPLIT_COPY,
    )(*grads, *zones, ssem, rsem, after)
    return list(res[:n]), list(res[n:])


def _sibling_exchange(name, slabs):
    n = len(slabs)
    nf = len(CHIP_FLIPS)

    def body(*refs):
        outs = refs[n:2 * n]
        ssem, rsem = refs[2 * n:]
        x, y, c = _place()

        def copy(a, j, which, to):
            _, _, kj = _chip_of(x, y, CHIP_FLIPS[j])
            ref = _half(outs[a].at[kj], which)
            return pltpu.make_async_remote_copy(src_ref=ref, dst_ref=ref, send_sem=ssem.at[a * nf + j],
                                                recv_sem=rsem.at[a * nf + j], device_id=to, device_id_type=MESH)

        sends = [copy(a, j, c, (x, y, 1 - c)) for a in range(n) for j in range(nf)]
        for cp in sends:
            cp.start()
        for a in range(n):
            for j in range(nf):
                copy(a, j, 1 - c, (x, y, c)).wait_recv()
        for cp in sends:
            cp.wait_send()

    return pl.pallas_call(
        body, name=name, out_shape=tuple(jax.ShapeDtypeStruct(t.shape, t.dtype) for t in slabs),
        in_specs=[ANY] * n, out_specs=(ANY,) * n, input_output_aliases={a: a for a in range(n)},
        scratch_shapes=[pltpu.SemaphoreType.DMA((nf * n,)), pltpu.SemaphoreType.DMA((nf * n,))],
    )(*slabs)


def _sibling_swap(name, grads):
    n = len(grads)

    def body(*refs):
        ins, outs = refs[:n], refs[n:2 * n]
        ssem, rsem = refs[2 * n:]
        x, y, c = _place()
        cps = []
        for a in range(n):
            h = ins[a].shape[1] // 2
            cps.append(pltpu.make_async_remote_copy(
                src_ref=ins[a].at[:, pl.ds((1 - c) * h, h)], dst_ref=outs[a], send_sem=ssem.at[a], recv_sem=rsem.at[a],
                device_id=(x, y, 1 - c), device_id_type=MESH))
        for cp in cps:
            cp.start()
        for cp in cps:
            cp.wait()

    return pl.pallas_call(
        body, name=name,
        out_shape=tuple(jax.ShapeDtypeStruct((g.shape[0], g.shape[1] // 2, g.shape[2]), g.dtype) for g in grads),
        in_specs=[ANY] * n, out_specs=(ANY,) * n,
        scratch_shapes=[pltpu.SemaphoreType.DMA((n,)), pltpu.SemaphoreType.DMA((n,))],
    )(*grads)


def _scatter_start(name, parts):
    n = len(parts)
    nf = len(CHIP_FLIPS)

    def body(*refs):
        ssem, rsem = refs[2 * n], refs[2 * n + 1]
        src, land = refs[2 * n + 2:3 * n + 2], refs[3 * n + 2:4 * n + 2]
        token = refs[4 * n + 2]
        token[...] = jnp.zeros_like(token)
        x, y, c = _place()
        for a in range(n):
            for j, flip in enumerate(CHIP_FLIPS):
                px, py, kj = _chip_of(x, y, flip)
                pltpu.make_async_remote_copy(
                    src_ref=src[a].at[kj], dst_ref=land[a].at[j], send_sem=ssem.at[a * nf + j], recv_sem=rsem.at[a * nf + j],
                    device_id=(px, py, c), device_id_type=MESH).start()

    zones = [lax.empty((nf,) + p.shape[1:], p.dtype) for p in parts]
    sem = pltpu.SemaphoreType.DMA((nf * n,))
    res = pl.pallas_call(
        body, name=name,
        out_shape=(sem, sem) + tuple(pltpu.HBM(t.shape, t.dtype) for t in list(parts) + zones)
        + (jax.ShapeDtypeStruct((SMALL_ROWS, LANES), F32),),
        in_specs=[HBM] * (2 * n),
        out_specs=tuple([SEM, SEM] + [HBM] * (2 * n) + [pl.BlockSpec(memory_space=pltpu.VMEM)]),
        input_output_aliases={i: 2 + i for i in range(2 * n)}, compiler_params=SPLIT_COPY,
    )(*_in_hbm(list(parts) + zones))
    return (res[0], res[1], list(res[2:2 + n]), list(res[2 + n:2 + 2 * n])), res[2 + 2 * n]


def _scatter_wait(name, ssem, rsem, parts, zones, after):
    n = len(parts)
    nf = len(CHIP_FLIPS)

    def body(*refs):
        src, land = refs[:n], refs[n:2 * n]
        ss, rs = refs[2 * n], refs[2 * n + 1]
        x, y, c = _place()
        for a in range(n):
            for j, flip in enumerate(CHIP_FLIPS):
                _, _, kj = _chip_of(x, y, flip)
                cp = pltpu.make_async_remote_copy(
                    src_ref=src[a].at[kj], dst_ref=land[a].at[j], send_sem=ss.at[a * nf + j], recv_sem=rs.at[a * nf + j],
                    device_id=(x, y, c), device_id_type=MESH)
                cp.wait_send()
                cp.wait_recv()

    res = pl.pallas_call(
        body, name=name, out_shape=tuple(pltpu.HBM(t.shape, t.dtype) for t in list(parts) + list(zones)),
        in_specs=[HBM] * (2 * n) + [SEM, SEM, ANY], out_specs=tuple([HBM] * (2 * n)),
        input_output_aliases={i: i for i in range(2 * n)}, compiler_params=SPLIT_COPY,
    )(*parts, *zones, ssem, rsem, after)
    return list(res[:n]), list(res[n:])


def _sibling_send(name, halves):
    n = len(halves)

    def body(*refs):
        ins, outs = refs[:n], refs[n:2 * n]
        ssem, rsem = refs[2 * n:]
        x, y, c = _place()
        cps = [pltpu.make_async_remote_copy(src_ref=ins[a], dst_ref=outs[a], send_sem=ssem.at[a], recv_sem=rsem.at[a],
                                            device_id=(x, y, 1 - c), device_id_type=MESH) for a in range(n)]
        for cp in cps:
            cp.start()
        for cp in cps:
            cp.wait()

    return pl.pallas_call(
        body, name=name,
        out_shape=tuple(jax.ShapeDtypeStruct(h.shape, h.dtype) for h in halves),
        in_specs=[ANY] * n, out_specs=(ANY,) * n,
        scratch_shapes=[pltpu.SemaphoreType.DMA((n,)), pltpu.SemaphoreType.DMA((n,))],
    )(*halves)


def _small_all_reduce(vec, loss_row, loss_scale):
    r, dm = vec.shape

    def body(v_ref, o_ref, l_ref, buf, ssem, rsem):
        x, y, c = _place()
        me = 4 * x + 2 * y + c
        buf[me] = v_ref[...]
        cps = []
        for mask in range(1, 8):
            fx, fy, fc = (mask >> 2) & 1, (mask >> 1) & 1, mask & 1
            cps.append(pltpu.make_async_remote_copy(
                src_ref=v_ref, dst_ref=buf.at[me], send_sem=ssem.at[mask - 1], recv_sem=rsem.at[mask - 1],
                device_id=(x ^ fx, y ^ fy, c ^ fc), device_id_type=MESH))
        for cp in cps:
            cp.start()
        for mask in range(1, 8):
            fx, fy, fc = (mask >> 2) & 1, (mask >> 1) & 1, mask & 1
            frm = 4 * (x ^ fx) + 2 * (y ^ fy) + (c ^ fc)
            pltpu.make_async_remote_copy(
                src_ref=v_ref, dst_ref=buf.at[frm], send_sem=ssem.at[mask - 1], recv_sem=rsem.at[mask - 1],
                device_id=(x, y, c), device_id_type=MESH).wait_recv()
        for cp in cps:
            cp.wait_send()
        acc = buf[0]
        for i in range(1, 8):
            acc = acc + buf[i]
        o_ref[...] = acc
        l_ref[...] = jnp.sum(acc[loss_row:loss_row + SMALL_ROWS, :], axis=(0, 1), keepdims=True) * loss_scale

    vm = pl.BlockSpec(memory_space=pltpu.VMEM)
    return pl.pallas_call(
        body, name="small_all_reduce",
        out_shape=(jax.ShapeDtypeStruct((r, dm), F32), jax.ShapeDtypeStruct((1, 1), F32)),
        in_specs=[vm], out_specs=(vm, vm),
        scratch_shapes=[pltpu.VMEM((8, r, dm), F32), pltpu.SemaphoreType.DMA((7,)), pltpu.SemaphoreType.DMA((7,))],
    )(vec)


def kernel(x, meta_tokens, norm_mix_g, w_in, b_gate, pool_w, pool_scale, conv_w, conv_out_w, w_o, norm_ffn_g, w_gate_up, w_down, norm_final_g, loss_target, m_meta_tokens, m_norm_mix_g, m_w_in, m_b_gate, m_pool_w, m_pool_scale, m_conv_w, m_conv_out_w, m_w_o, m_norm_ffn_g, m_w_gate_up, m_w_down, m_norm_final_g, v_meta_tokens, v_norm_mix_g, v_w_in, v_b_gate, v_pool_w, v_pool_scale, v_conv_w, v_conv_out_w, v_w_o, v_norm_ffn_g, v_w_gate_up, v_w_down, v_norm_final_g):
    seq, dm = x.shape[1], x.shape[2]
    tail = 256 if seq % 256 == 0 else LANES
    tm = tail
    lp = seq + tail
    n_chips = 4
    n_groups = len(POOL_WINDOWS)
    gw = dm // n_groups
    tc = min(256, gw)
    cx, cy, cc = _place()
    chip = 2 * cx + cy
    dloc = dm // n_chips

    pool2 = pool_w.reshape(n_groups * pool_w.shape[1], gw)
    big = {"w_in": w_in, "w_gate_up": w_gate_up, "pool_w": pool2, "conv_out_w": conv_out_w, "w_o": w_o, "w_down": w_down}
    chip1 = jnp.reshape(chip, (1,)).astype(jnp.int32)
    core = jnp.reshape(cc, (1,)).astype(jnp.int32)
    small_loc = jnp.concatenate([meta_tokens, jnp.pad(conv_w, ((0, 8 - conv_w.shape[0]), (0, 0))),
                                 jnp.zeros((8, dloc), F32)], axis=0)
    first = [_cast_into_slot("cast_w_in", w_in, chip1, BF16), _cast_into_slot("place_small", small_loc, chip1, F32)]
    (sems0,), (w_in4, small4), token0 = _gather_start("gather_start_first", first, ([0, 1],))
    rest_names = ["pool_w", "conv_out_w", "w_o", "w_gate_up", "w_down"]
    rest = [_cast_into_slot("cast_" + nme, big[nme], chip1, BF16, deps=(token0,)) for nme in rest_names]
    groups = ([0, 1, 2], [3], [4])
    sems, rest, _ = _gather_start("gather_start_rest", rest, groups)

    def passing(g, after):
        return _gather_pass("gather_pass_%d" % g, [rest[a] for a in groups[g]], sems[g], after)

    g1, g2, g3 = norm_mix_g.reshape(1, dm), norm_ffn_g.reshape(1, dm), norm_final_g.reshape(1, dm)
    b_gate2 = b_gate.reshape(2, dm)
    ps = pool_scale.reshape(1, dm)
    w_in4, small4 = _sibling_exchange("sibling_exchange_first", list(_gather_wait("gather_wait_first", [w_in4, small4], sems0, rest[0])))
    small_f = jnp.transpose(small4, (1, 0, 2)).reshape(small4.shape[1], dm)
    meta_f = small_f[:N_META]
    conv_w_f = small_f[N_META:N_META + 3]
    h0 = jnp.concatenate([x[0], jnp.zeros((tail - N_META, dm), F32), meta_f], axis=0)
    hn1 = _rms_fwd("rms_mix", h0, g1, tm)
    proj = _nn_sharded("proj", hn1, w_in4, 6)
    pass_sems, passed = passing(0, proj)
    pooled, z = _mixer_fwd("mixer_fwd", proj, conv_w_f, tc)
    pool4, conv_out4, w_o4 = _pass_wait("pass_wait_0", passed, pass_sems, pooled)
    pool4 = pool4.reshape(n_chips, n_groups, gw // n_chips, gw)
    conv_out_f = conv_out4.reshape(dm, dm)
    w_o_f = w_o4.reshape(dm, dm)
    ya = _pool_fwd("pool_proj", pooled, pool4)
    yb = _nn_plain("conv_out", z, conv_out_f, BF16)
    mix = _gate_mix("gate_mix", proj, b_gate2, ya, ps, yb, tm)
    pass_sems, passed = passing(1, mix)
    h1 = _nn_plain("attn_out", mix, w_o_f, F32, res=h0, tn_pref=256)
    hn2 = _rms_fwd("rms_ffn", h1, g2, tm)
    (w_gu4,) = _pass_wait("pass_wait_1", passed, pass_sems, hn2)
    gu = _nn_sharded("gate_up", hn2, w_gu4, 2)
    pass_sems, passed = passing(2, gu)
    act = _swiglu_fwd("swiglu", gu, tm)
    (w_down4,) = _pass_wait("pass_wait_2", passed, pass_sems, act)
    w_down_f = w_down4.reshape(-1, dm)
    h2 = _nn_plain("ffn_down", act, w_down_f, F32, res=h1, tn_pref=512, tk_pref=1536)
    dh2, dh2b, loss_cols, dg3 = _final_loss("final_loss", h2, g3, loss_target[0], tm)

    def scatter(tag, names_g, swap, after):
        grads_g, got = _swap_wait("swap_wait_" + tag, *swap, after)
        pairs = [_pair_add("pair_add_" + nme, g4, rv, core) for nme, g4, rv in zip(names_g, grads_g, got)]
        return _scatter_start("scatter_start_" + tag, pairs)

    dact = _nt_plain("d_act", dh2b, w_down_f)
    gw_down = _tn_plain("dw_down", act, dh2b)
    dgu = _swiglu_bwd("swiglu_bwd", dact, gu, tm)
    gw_gu = _tn_sharded("dw_gate_up", hn2, dgu, n_chips)
    swap_a, token = _swap_start("swap_start_a", [gw_gu, gw_down.reshape(n_chips, -1, dm)])
    dhn2 = _nt_sharded("d_hn2", dgu, w_gu4, deps=(token,))
    flight_a, token = scatter("a", ["w_gate_up", "w_down"], swap_a, dhn2)
    dh1, dh1b, dg2 = _rms_bwd("rms_ffn_bwd", dhn2, h1, g2, dh2, tm, token)
    dmix = _nt_plain("d_mix", dh1b, w_o_f)
    gw_o = _tn_plain("dw_o", mix, dh1b)
    dproj, dyb, dya, db_gate, dps = _gate_bwd("gate_bwd", dmix, proj, b_gate2, ya, ps, yb, tm)
    gw_conv_out = _tn_plain("dw_conv_out", z, dyb)
    gw_pool = _pool_bwd_w("dw_pool", pooled, dya, n_chips)
    swap_b, token = _swap_start("swap_start_b", [gw_o.reshape(n_chips, dloc, dm), gw_conv_out.reshape(n_chips, dloc, dm),
                                                 gw_pool.reshape(n_chips, n_groups * (gw // n_chips), gw)])
    dpooled = _pool_bwd_act("d_pooled", dya, pool4, deps=(token,))
    dz = _nt_plain("d_z", dyb, conv_out_f)
    flight_b, token = scatter("b", ["w_o", "conv_out_w", "pool_w"], swap_b, dz)
    dproj, dconv_w = _mixer_bwd("mixer_bwd", dz, dpooled, proj, conv_w_f, dproj, tc, token)
    gw_in = _tn_sharded("dw_in", hn1, dproj, n_chips)
    swap_c, token = _swap_start("swap_start_c", [gw_in])
    dhn1 = _nt_sharded("d_hn1", dproj, w_in4, deps=(token,))
    flight_c, token = scatter("c", ["w_in"], swap_c, dhn1)
    dh0, _, dg1 = _rms_bwd("rms_mix_bwd", dhn1, h0, g1, dh1, tm, token)
    grad_x = dh0[:seq][None]
    dmeta = dh0[lp - N_META:]

    names = ["w_in", "w_gate_up", "pool_w", "conv_out_w", "w_o", "w_down"]
    g_halves = {}
    after = dh0
    for tag, names_g, flight in (("a", ["w_gate_up", "w_down"], flight_a), ("b", ["w_o", "conv_out_w", "pool_w"], flight_b),
                                 ("c", ["w_in"], flight_c)):
        pairs, zones = _scatter_wait("scatter_wait_" + tag, *flight, after)
        halves = [_chip_sum("chip_sum_" + nme, p, rv, chip1) for nme, p, rv in zip(names_g, pairs, zones)]
        sib_halves = _sibling_send("sibling_send_" + tag, halves)
        g_halves.update(zip(names_g, zip(halves, sib_halves)))
        after = sib_halves[0]

    vec = jnp.concatenate([dg1, dg2, dg3, db_gate, dps, loss_cols, dconv_w, dmeta], axis=0)
    loss_row = 5 * SMALL_ROWS
    red, loss11 = _small_all_reduce(vec, loss_row, 0.5 / dm)
    loss = loss11[0, 0]
    col0 = chip * dloc
    g_small = {
        "norm_mix_g": red[0], "norm_ffn_g": red[SMALL_ROWS], "norm_final_g": red[2 * SMALL_ROWS],
        "b_gate": red[3 * SMALL_ROWS:3 * SMALL_ROWS + 2].reshape(-1), "pool_scale": red[4 * SMALL_ROWS],
        "conv_w": lax.dynamic_slice(red, (6 * SMALL_ROWS, col0), (3, dloc)),
        "meta_tokens": lax.dynamic_slice(red, (7 * SMALL_ROWS, col0), (N_META, dloc)),
    }

    given = dict(meta_tokens=(meta_tokens, m_meta_tokens, v_meta_tokens), norm_mix_g=(norm_mix_g, m_norm_mix_g, v_norm_mix_g),
                 w_in=(w_in, m_w_in, v_w_in), b_gate=(b_gate, m_b_gate, v_b_gate), pool_w=(pool_w, m_pool_w, v_pool_w),
                 pool_scale=(pool_scale, m_pool_scale, v_pool_scale), conv_w=(conv_w, m_conv_w, v_conv_w),
                 conv_out_w=(conv_out_w, m_conv_out_w, v_conv_out_w), w_o=(w_o, m_w_o, v_w_o),
                 norm_ffn_g=(norm_ffn_g, m_norm_ffn_g, v_norm_ffn_g), w_gate_up=(w_gate_up, m_w_gate_up, v_w_gate_up),
                 w_down=(w_down, m_w_down, v_w_down), norm_final_g=(norm_final_g, m_norm_final_g, v_norm_final_g))
    order = list(given.keys())
    grad, delta, new_m, new_v = {}, {}, {}, {}
    for nme in names:
        w, m, v = given[nme]
        g_own, g_sib = g_halves[nme]
        shape2 = (2 * g_own.shape[0], g_own.shape[1])
        res4 = _adamw_halves("adamw_" + nme, w.reshape(shape2), g_own, g_sib, m.reshape(shape2), v.reshape(shape2), core)
        grad[nme], delta[nme], new_m[nme], new_v[nme] = [t.reshape(w.shape) for t in res4]
    vec_names = ["norm_mix_g", "norm_ffn_g", "norm_final_g", "pool_scale"]

    def slab_vec(pick):
        rows = [pick(nme).reshape(1, dm) for nme in vec_names] + [pick("b_gate").reshape(2, dm), jnp.zeros((2, dm), F32)]
        return jnp.concatenate(rows, axis=0)

    def slab_col(pick):
        return jnp.concatenate([pick("meta_tokens"), pick("conv_w"), jnp.zeros((5, dloc), F32)], axis=0)

    for slab, tag in ((slab_vec, "vec"), (slab_col, "col")):
        d, nm, nv = _adamw("adamw_small_" + tag, slab(lambda nme: given[nme][0]), slab(lambda nme: g_small[nme]),
                           slab(lambda nme: given[nme][1]), slab(lambda nme: given[nme][2]))
        for out, res in ((delta, d), (new_m, nm), (new_v, nv)):
            if tag == "vec":
                for i, nme in enumerate(vec_names):
                    out[nme] = res[i]
                out["b_gate"] = res[4:6].reshape(-1)
            else:
                out["meta_tokens"] = res[:N_META]
                out["conv_w"] = res[N_META:N_META + 3]
    grad.update(g_small)
    return (loss, grad_x, *[grad[nme] for nme in order], *[delta[nme] for nme in order],
            *[new_m[nme] for nme in order], *[new_v[nme] for nme in order])
```

```python
import functools
import math

import jax
import jax.numpy as jnp
from jax import lax
from jax.experimental import pallas as pl
from jax.experimental.pallas import tpu as pltpu

F32 = jnp.float32
BF16 = jnp.bfloat16
N_META = 16
POOL_WINDOWS = (2, 4, 8, 16)
EPS = 1e-6
ADAM_LR, ADAM_B1, ADAM_B2, ADAM_EPS, ADAM_WD, ADAM_STEP = 0.001, 0.9, 0.999, 1e-08, 0.01, 10
LANES = 128
V7X_VMEM_BYTES = 64 * 1024 * 1024
VMEM_LIMIT = V7X_VMEM_BYTES - 8 * 1024 * 1024
MESH = pl.DeviceIdType.MESH
ANY = pl.BlockSpec(memory_space=pl.ANY)
CHIP_FLIPS = ((1, 0), (0, 1), (1, 1))
SMALL_ROWS = 8


def _pick(n, pref):
    best = None
    for t in range(LANES, min(n, pref) + 1, LANES):
        if n % t == 0:
            best = t
    assert best is not None, (n, pref)
    return best


def _params(n_axes=0):
    sem = ("arbitrary",) * n_axes if n_axes else None
    return pltpu.CompilerParams(dimension_semantics=sem, vmem_limit_bytes=VMEM_LIMIT)


_DIMS = {
    "nn": (((1,), (0,)), ((), ())),
    "nt": (((1,), (1,)), ((), ())),
    "tn": (((0,), (0,)), ((), ())),
}


def _matmul(name, mode, a, b, out_sds, grid, a_spec, b_spec, o_spec, nk, res=None, res_spec=None, acc_shape=None, deps=()):
    out_dtype = out_sds.dtype
    in_place = nk > 1 and out_dtype == F32
    use_scratch = nk > 1 and not in_place
    rows = a_spec.block_shape[-2] if mode != "tn" else None
    chunk = _row_tile(rows, 1, 1, 1152) if rows is not None else None
    n_in = 2 + (res is not None) + len(deps)

    def body(*refs):
        a_ref, b_ref = refs[:2]
        r_ref = refs[2] if res is not None else None
        o_ref, *scr = refs[n_in:]
        k = pl.program_id(len(grid) - 1) if nk > 1 else None

        def emit(sl):
            if sl is None:
                part = lax.dot_general(a_ref[...], b_ref[...], _DIMS[mode], preferred_element_type=F32)
                idx = (slice(None), slice(None))
            else:
                part = lax.dot_general(a_ref[sl, :], b_ref[...], _DIMS[mode], preferred_element_type=F32)
                idx = (sl, slice(None))
            if nk == 1:
                if r_ref is not None:
                    part = part + r_ref[idx]
                o_ref[idx] = part.astype(out_dtype)
                return
            acc = scr[0] if use_scratch else o_ref

            @pl.when(k == 0)
            def _():
                first = part
                if r_ref is not None and in_place:
                    first = first + r_ref[idx]
                acc[idx] = first

            @pl.when(k > 0)
            def _():
                acc[idx] += part

            if use_scratch:

                @pl.when(k == nk - 1)
                def _():
                    o_ref[idx] = acc[idx].astype(out_dtype)

        if mode == "tn" or chunk == rows:
            emit(None)
        else:
            for m0 in range(0, rows, chunk):
                emit(pl.ds(m0, chunk))

    ins = [a, b] + ([res] if res is not None else []) + list(deps)
    in_specs = [a_spec, b_spec] + ([res_spec] if res is not None else []) + [ANY] * len(deps)
    scratch = [pltpu.VMEM(acc_shape, F32)] if use_scratch else []
    return pl.pallas_call(
        body, name=name, out_shape=out_sds, grid=grid, in_specs=in_specs, out_specs=o_spec,
        scratch_shapes=scratch, compiler_params=_params(len(grid)),
    )(*ins)


def _nn_sharded(name, a, w4, nseg):
    lp, kdim = a.shape
    s, _, nloc = w4.shape
    segw = s * nloc // nseg
    tn = _pick(math.gcd(nloc, segw), 1536)
    bw, bo = nloc // tn, segw // tn
    return _matmul(
        name, "nn", a, w4, jax.ShapeDtypeStruct((nseg, lp, segw), BF16), (s * bw,),
        pl.BlockSpec((lp, kdim), lambda j: (0, 0)),
        pl.BlockSpec((None, kdim, tn), lambda j: (j // bw, 0, j % bw)),
        pl.BlockSpec((None, lp, tn), lambda j: (j // bo, 0, j % bo)), 1)


def _nn_plain(name, a, w, out_dtype, res=None, tn_pref=512, tk_pref=2048):
    lp, kdim = a.shape
    n = w.shape[1]
    tn = _pick(n, tn_pref)
    tk = kdim if kdim <= tk_pref else _pick(kdim, tk_pref)
    nk = kdim // tk
    grid = (n // tn, nk) if nk > 1 else (n // tn,)
    if nk > 1:
        a_spec = pl.BlockSpec((lp, tk), lambda j, k: (0, k))
        w_spec = pl.BlockSpec((tk, tn), lambda j, k: (k, j))
        o_spec = pl.BlockSpec((lp, tn), lambda j, k: (0, j))
    else:
        a_spec = pl.BlockSpec((lp, tk), lambda j: (0, 0))
        w_spec = pl.BlockSpec((tk, tn), lambda j: (0, j))
        o_spec = pl.BlockSpec((lp, tn), lambda j: (0, j))
    return _matmul(name, "nn", a, w, jax.ShapeDtypeStruct((lp, n), out_dtype), grid, a_spec, w_spec, o_spec, nk,
                   res=res, res_spec=o_spec if res is not None else None, acc_shape=(lp, tn))


def _nt_plain(name, a, w, tn_pref=512):
    lp, kdim = a.shape
    n = w.shape[0]
    tn = _pick(n, tn_pref)
    return _matmul(
        name, "nt", a, w, jax.ShapeDtypeStruct((lp, n), BF16), (n // tn,),
        pl.BlockSpec((lp, kdim), lambda j: (0, 0)),
        pl.BlockSpec((tn, kdim), lambda j: (j, 0)),
        pl.BlockSpec((lp, tn), lambda j: (0, j)), 1)


def _nt_sharded(name, dseg, w4, to_pref=1024, deps=()):
    nseg, lp, segw = dseg.shape
    s, kdim, nloc = w4.shape
    tr = _pick(math.gcd(nloc, segw), 1536)
    ba, bw = segw // tr, nloc // tr
    nr = s * bw
    to = _pick(kdim, to_pref)
    return _matmul(
        name, "nt", dseg, w4, jax.ShapeDtypeStruct((lp, kdim), F32), (kdim // to, nr),
        pl.BlockSpec((None, lp, tr), lambda j, r: (r // ba, 0, r % ba)),
        pl.BlockSpec((None, to, tr), lambda j, r: (r // bw, j, r % bw)),
        pl.BlockSpec((lp, to), lambda j, r: (0, j)), nr, deps=deps)


def _tn_plain(name, a, d, tk_pref=512):
    lp, kdim = a.shape
    n = d.shape[1]
    tk = _pick(kdim, tk_pref)
    return _matmul(
        name, "tn", a, d, jax.ShapeDtypeStruct((kdim, n), BF16), (kdim // tk,),
        pl.BlockSpec((lp, tk), lambda i: (0, i)),
        pl.BlockSpec((lp, n), lambda i: (0, 0)),
        pl.BlockSpec((tk, n), lambda i: (i, 0)), 1)


def _tn_sharded(name, a, dseg, s, tk_pref=512):
    lp, kdim = a.shape
    nseg, _, segw = dseg.shape
    nloc = nseg * segw // s
    tn = _pick(math.gcd(nloc, segw), 1536)
    bd, bo = segw // tn, nloc // tn
    tk = _pick(kdim, tk_pref)
    return _matmul(
        name, "tn", a, dseg, jax.ShapeDtypeStruct((s, kdim, nloc), BF16), (s * bo, kdim // tk),
        pl.BlockSpec((lp, tk), lambda j, i: (0, i)),
        pl.BlockSpec((None, lp, tn), lambda j, i: (j // bd, 0, j % bd)),
        pl.BlockSpec((None, tk, tn), lambda j, i: (j // bo, i, j % bo)), 1)


def _pool_fwd(name, pooled, pw4):
    lp, dm = pooled.shape
    s, g, rs, gw = pw4.shape
    return _matmul(
        name, "nn", pooled, pw4, jax.ShapeDtypeStruct((lp, dm), BF16), (g, s),
        pl.BlockSpec((lp, rs), lambda gi, si: (0, gi * s + si)),
        pl.BlockSpec((None, None, rs, gw), lambda gi, si: (si, gi, 0, 0)),
        pl.BlockSpec((lp, gw), lambda gi, si: (0, gi)), s, acc_shape=(lp, gw))


def _pool_bwd_act(name, dya, pw4, deps=()):
    lp, dm = dya.shape
    s, g, rs, gw = pw4.shape
    return _matmul(
        name, "nt", dya, pw4, jax.ShapeDtypeStruct((lp, dm), BF16), (g, s),
        pl.BlockSpec((lp, gw), lambda gi, si: (0, gi)),
        pl.BlockSpec((None, None, rs, gw), lambda gi, si: (si, gi, 0, 0)),
        pl.BlockSpec((lp, rs), lambda gi, si: (0, gi * s + si)), 1, deps=deps)


def _pool_bwd_w(name, pooled, dya, s):
    lp, dm = pooled.shape
    g = len(POOL_WINDOWS)
    gw = dm // g
    rs = gw // s
    return _matmul(
        name, "tn", pooled, dya, jax.ShapeDtypeStruct((s, g, rs, gw), BF16), (g, s),
        pl.BlockSpec((lp, rs), lambda gi, si: (0, gi * s + si)),
        pl.BlockSpec((lp, gw), lambda gi, si: (0, gi)),
        pl.BlockSpec((None, None, rs, gw), lambda gi, si: (si, gi, 0, 0)), 1)


def _rms_fwd(name, h, g, tm):
    lp, dm = h.shape

    def body(h_ref, g_ref, o_ref):
        hv = h_ref[...]
        r = lax.rsqrt(jnp.mean(hv * hv, axis=-1, keepdims=True) + EPS)
        o_ref[...] = (hv * r * g_ref[...]).astype(BF16)

    row = pl.BlockSpec((tm, dm), lambda i: (i, 0))
    return pl.pallas_call(
        body, name=name, out_shape=jax.ShapeDtypeStruct((lp, dm), BF16), grid=(lp // tm,),
        in_specs=[row, pl.BlockSpec((1, dm), lambda i: (0, 0))], out_specs=row, compiler_params=_params(1),
    )(h, g)


def _rms_bwd(name, dy, h, g, dres, tm, dep):
    lp, dm = h.shape

    def body(dy_ref, h_ref, g_ref, dr_ref, _, dh_ref, dhb_ref, dg_ref):
        hv = h_ref[...]
        r = lax.rsqrt(jnp.mean(hv * hv, axis=-1, keepdims=True) + EPS)
        xhat = hv * r
        dyv = dy_ref[...]
        dxh = dyv * g_ref[...]
        dh = dr_ref[...] + r * (dxh - xhat * jnp.mean(dxh * xhat, axis=-1, keepdims=True))
        dh_ref[...] = dh
        dhb_ref[...] = dh.astype(BF16)

        @pl.when(pl.program_id(0) == 0)
        def _():
            dg_ref[...] = jnp.zeros_like(dg_ref)

        dg_ref[0:1, :] += jnp.sum(dyv * xhat, axis=0, keepdims=True)

    row = pl.BlockSpec((tm, dm), lambda i: (i, 0))
    slab = pl.BlockSpec((SMALL_ROWS, dm), lambda i: (0, 0))
    return pl.pallas_call(
        body, name=name, grid=(lp // tm,),
        out_shape=(jax.ShapeDtypeStruct((lp, dm), F32), jax.ShapeDtypeStruct((lp, dm), BF16),
                   jax.ShapeDtypeStruct((SMALL_ROWS, dm), F32)),
        in_specs=[row, row, pl.BlockSpec((1, dm), lambda i: (0, 0)), row, ANY], out_specs=(row, row, slab),
        compiler_params=_params(1),
    )(dy, h, g, dres, dep)


def _gate_mix(name, proj, b_gate2, ya, pool_scale, yb, tm):
    _, lp, dm = proj.shape

    def body(ga_ref, gr_ref, b_ref, ya_ref, ps_ref, yb_ref, o_ref):
        g_a = jax.nn.sigmoid(ga_ref[...].astype(F32) + b_ref[0:1, :])
        g_b = jax.nn.sigmoid(gr_ref[...].astype(F32) + b_ref[1:2, :])
        y_a = ya_ref[...].astype(F32) * ps_ref[...]
        o_ref[...] = (g_a * y_a + g_b * yb_ref[...].astype(F32)).astype(BF16)

    row = pl.BlockSpec((tm, dm), lambda i: (i, 0))
    return pl.pallas_call(
        body, name=name, out_shape=jax.ShapeDtypeStruct((lp, dm), BF16), grid=(lp // tm,),
        in_specs=[pl.BlockSpec((None, tm, dm), lambda i: (4, i, 0)), pl.BlockSpec((None, tm, dm), lambda i: (5, i, 0)),
                  pl.BlockSpec((2, dm), lambda i: (0, 0)), row, pl.BlockSpec((1, dm), lambda i: (0, 0)), row],
        out_specs=row, compiler_params=_params(1),
    )(proj, proj, b_gate2, ya, pool_scale, yb)


def _gate_bwd(name, dmix, proj, b_gate2, ya, pool_scale, yb, tm):
    _, lp, dm = proj.shape

    def body(dm_ref, ga_ref, gr_ref, b_ref, ya_ref, ps_ref, yb_ref, dp_ref, dyb_ref, dya_ref, db_ref, dps_ref):
        dmx = dm_ref[...].astype(F32)
        g_a = jax.nn.sigmoid(ga_ref[...].astype(F32) + b_ref[0:1, :])
        g_b = jax.nn.sigmoid(gr_ref[...].astype(F32) + b_ref[1:2, :])
        ya_pre = ya_ref[...].astype(F32)
        ybv = yb_ref[...].astype(F32)
        ps = ps_ref[...]
        dga = dmx * (ya_pre * ps) * (g_a * (1.0 - g_a))
        dgr = dmx * ybv * (g_b * (1.0 - g_b))
        dp_ref[0] = dga.astype(BF16)
        dp_ref[1] = dgr.astype(BF16)
        dyb_ref[...] = (dmx * g_b).astype(BF16)
        dya_ref[...] = (dmx * g_a * ps).astype(BF16)

        @pl.when(pl.program_id(0) == 0)
        def _():
            db_ref[...] = jnp.zeros_like(db_ref)
            dps_ref[...] = jnp.zeros_like(dps_ref)

        db_ref[0:1, :] += jnp.sum(dga, axis=0, keepdims=True)
        db_ref[1:2, :] += jnp.sum(dgr, axis=0, keepdims=True)
        dps_ref[0:1, :] += jnp.sum(dmx * g_a * ya_pre, axis=0, keepdims=True)

    row = pl.BlockSpec((tm, dm), lambda i: (i, 0))
    one = pl.BlockSpec((1, dm), lambda i: (0, 0))
    slab = pl.BlockSpec((SMALL_ROWS, dm), lambda i: (0, 0))
    return pl.pallas_call(
        body, name=name, grid=(lp // tm,),
        out_shape=(jax.ShapeDtypeStruct((6, lp, dm), BF16), jax.ShapeDtypeStruct((lp, dm), BF16),
                   jax.ShapeDtypeStruct((lp, dm), BF16), jax.ShapeDtypeStruct((SMALL_ROWS, dm), F32),
                   jax.ShapeDtypeStruct((SMALL_ROWS, dm), F32)),
        in_specs=[row, pl.BlockSpec((None, tm, dm), lambda i: (4, i, 0)), pl.BlockSpec((None, tm, dm), lambda i: (5, i, 0)),
                  pl.BlockSpec((2, dm), lambda i: (0, 0)), row, one, row],
        out_specs=(pl.BlockSpec((2, tm, dm), lambda i: (2, i, 0)), row, row, slab, slab),
        compiler_params=_params(1),
    )(dmix, proj, proj, b_gate2, ya, pool_scale, yb)


def _swiglu_fwd(name, gu, tm):
    _, lp, f = gu.shape

    def body(g_ref, u_ref, o_ref):
        gt = g_ref[...].astype(F32)
        o_ref[...] = (gt * jax.nn.sigmoid(gt) * u_ref[...].astype(F32)).astype(BF16)

    return pl.pallas_call(
        body, name=name, out_shape=jax.ShapeDtypeStruct((lp, f), BF16), grid=(lp // tm,),
        in_specs=[pl.BlockSpec((None, tm, f), lambda i: (0, i, 0)), pl.BlockSpec((None, tm, f), lambda i: (1, i, 0))],
        out_specs=pl.BlockSpec((tm, f), lambda i: (i, 0)), compiler_params=_params(1),
    )(gu, gu)


def _swiglu_bwd(name, dact, gu, tm):
    _, lp, f = gu.shape

    def body(d_ref, g_ref, u_ref, o_ref):
        d = d_ref[...].astype(F32)
        gt = g_ref[...].astype(F32)
        sg = jax.nn.sigmoid(gt)
        o_ref[0] = (d * u_ref[...].astype(F32) * (sg * (1.0 + gt * (1.0 - sg)))).astype(BF16)
        o_ref[1] = (d * (gt * sg)).astype(BF16)

    return pl.pallas_call(
        body, name=name, out_shape=jax.ShapeDtypeStruct((2, lp, f), BF16), grid=(lp // tm,),
        in_specs=[pl.BlockSpec((tm, f), lambda i: (i, 0)), pl.BlockSpec((None, tm, f), lambda i: (0, i, 0)),
                  pl.BlockSpec((None, tm, f), lambda i: (1, i, 0))],
        out_specs=pl.BlockSpec((2, tm, f), lambda i: (0, i, 0)), compiler_params=_params(1),
    )(dact, gu, gu)


def _final_loss(name, h2, g3, target, tm):
    lp, dm = h2.shape
    nx = target.shape[0] // tm

    def body(h_ref, g_ref, t_ref, dh_ref, dhb_ref, ls_ref, dg_ref):
        i = pl.program_id(0)

        @pl.when(i == 0)
        def _():
            ls_ref[...] = jnp.zeros_like(ls_ref)
            dg_ref[...] = jnp.zeros_like(dg_ref)

        @pl.when(i < nx)
        def _():
            hv = h_ref[...]
            gv = g_ref[...]
            r = lax.rsqrt(jnp.mean(hv * hv, axis=-1, keepdims=True) + EPS)
            xhat = hv * r
            err = xhat * gv - t_ref[...]
            dout = err * (1.0 / dm)
            dxh = dout * gv
            dh = r * (dxh - xhat * jnp.mean(dxh * xhat, axis=-1, keepdims=True))
            dh_ref[...] = dh
            dhb_ref[...] = dh.astype(BF16)
            ls_ref[0:1, :] += jnp.sum(err * err, axis=0, keepdims=True)
            dg_ref[0:1, :] += jnp.sum(dout * xhat, axis=0, keepdims=True)

        @pl.when(i >= nx)
        def _():
            dh_ref[...] = jnp.zeros_like(dh_ref)
            dhb_ref[...] = jnp.zeros_like(dhb_ref)

    row = pl.BlockSpec((tm, dm), lambda i: (i, 0))
    slab = pl.BlockSpec((SMALL_ROWS, dm), lambda i: (0, 0))
    return pl.pallas_call(
        body, name=name, grid=(lp // tm,),
        out_shape=(jax.ShapeDtypeStruct((lp, dm), F32), jax.ShapeDtypeStruct((lp, dm), BF16),
                   jax.ShapeDtypeStruct((SMALL_ROWS, dm), F32), jax.ShapeDtypeStruct((SMALL_ROWS, dm), F32)),
        in_specs=[row, pl.BlockSpec((1, dm), lambda i: (0, 0)), pl.BlockSpec((tm, dm), lambda i: (jnp.minimum(i, nx - 1), 0))],
        out_specs=(row, row, slab, slab), compiler_params=_params(1),
    )(h2, g3, target)


def _shift(v, k):
    return pltpu.roll(v, k % v.shape[0], axis=0)


def _window_sum(v, group, sign):
    s2 = v + _shift(v, sign * 1)
    s4 = s2 + _shift(s2, sign * 2)
    s8 = s4 + _shift(s4, sign * 4)
    s16 = s8 + _shift(s8, sign * 8)
    return jnp.where(group == 0, s2, jnp.where(group == 1, s4, jnp.where(group == 2, s8, s16)))


def _pool_count(lp, group):
    row = lax.broadcasted_iota(jnp.int32, (lp, 1), 0)
    window = jnp.left_shift(2, group).astype(F32)
    meta_pos = (row - (lp - N_META) + 1).astype(F32)
    return jnp.where(row >= lp - N_META, jnp.minimum(meta_pos, window), window)


def _mixer_fwd(name, proj, conv_w, tc):
    _, lp, dm = proj.shape
    per_group = dm // len(POOL_WINDOWS) // tc

    def body(u_ref, gb_ref, gc_ref, v_ref, cw_ref, p_ref, z_ref):
        group = pl.program_id(0) // per_group
        u = u_ref[...].astype(F32)
        p_ref[...] = (_window_sum(u, group, 1) / _pool_count(lp, group) - u).astype(BF16)
        cv = gc_ref[...].astype(F32) * v_ref[...].astype(F32)
        conv = cw_ref[0:1, :] * _shift(cv, 2) + cw_ref[1:2, :] * _shift(cv, 1) + cw_ref[2:3, :] * cv
        z_ref[...] = (gb_ref[...].astype(F32) * conv).astype(BF16)

    def seg(s):
        return pl.BlockSpec((None, lp, tc), lambda j: (s, 0, j))

    col = pl.BlockSpec((lp, tc), lambda j: (0, j))
    return pl.pallas_call(
        body, name=name, grid=(dm // tc,),
        out_shape=(jax.ShapeDtypeStruct((lp, dm), BF16), jax.ShapeDtypeStruct((lp, dm), BF16)),
        in_specs=[seg(0), seg(1), seg(2), seg(3), pl.BlockSpec((3, tc), lambda j: (0, j))],
        out_specs=(col, col), compiler_params=_params(1),
    )(proj, proj, proj, proj, conv_w)


def _mixer_bwd(name, dz, dpooled, proj, conv_w, dproj, tc, dep):
    _, lp, dm = proj.shape
    per_group = dm // len(POOL_WINDOWS) // tc

    def body(dz_ref, dp_ref, gb_ref, gc_ref, v_ref, cw_ref, _, __, o_ref, dcw_ref):
        group = pl.program_id(0) // per_group
        dzv = dz_ref[...].astype(F32)
        gb = gb_ref[...].astype(F32)
        gc = gc_ref[...].astype(F32)
        vv = v_ref[...].astype(F32)
        cv = gc * vv
        c1 = _shift(cv, 1)
        c2 = _shift(cv, 2)
        w0, w1, w2 = cw_ref[0:1, :], cw_ref[1:2, :], cw_ref[2:3, :]
        o_ref[1] = (dzv * (w0 * c2 + w1 * c1 + w2 * cv)).astype(BF16)
        dconv = dzv * gb
        dcw_ref[...] = jnp.zeros_like(dcw_ref)
        dcw_ref[0:1, :] = jnp.sum(dconv * c2, axis=0, keepdims=True)
        dcw_ref[1:2, :] = jnp.sum(dconv * c1, axis=0, keepdims=True)
        dcw_ref[2:3, :] = jnp.sum(dconv * cv, axis=0, keepdims=True)
        dcv = w0 * _shift(dconv, -2) + w1 * _shift(dconv, -1) + w2 * dconv
        o_ref[2] = (dcv * vv).astype(BF16)
        o_ref[3] = (dcv * gc).astype(BF16)
        dpv = dp_ref[...].astype(F32)
        o_ref[0] = (_window_sum(dpv / _pool_count(lp, group), group, -1) - dpv).astype(BF16)

    def seg(s):
        return pl.BlockSpec((None, lp, tc), lambda j: (s, 0, j))

    col = pl.BlockSpec((lp, tc), lambda j: (0, j))
    return pl.pallas_call(
        body, name=name, grid=(dm // tc,),
        out_shape=(jax.ShapeDtypeStruct(dproj.shape, BF16), jax.ShapeDtypeStruct((SMALL_ROWS, dm), F32)),
        in_specs=[col, col, seg(1), seg(2), seg(3), pl.BlockSpec((3, tc), lambda j: (0, j)), ANY, ANY],
        out_specs=(pl.BlockSpec((4, lp, tc), lambda j: (0, 0, j)), pl.BlockSpec((SMALL_ROWS, tc), lambda j: (0, j))),
        input_output_aliases={6: 0}, compiler_params=_params(1),
    )(dz, dpooled, proj, proj, proj, conv_w, dproj, dep)


def _row_tile(r, c, bytes_per_row_elem=4, budget=2 * 1024 * 1024):
    best = None
    for t in range(16, r + 1, 16):
        if r % t == 0 and t * c * bytes_per_row_elem <= budget:
            best = t
    return best if best is not None else r


def _pair_add(name, g4, recv, core):
    s, r, c = g4.shape
    h = r // 2
    tr = _row_tile(h, c)
    nb = h // tr

    def body(core_ref, g_ref, r_ref, o_ref):
        o_ref[...] = (g_ref[...].astype(F32) + r_ref[...].astype(F32)).astype(BF16)

    grid_spec = pltpu.PrefetchScalarGridSpec(
        num_scalar_prefetch=1, grid=(s, nb),
        in_specs=[pl.BlockSpec((None, tr, c), lambda si, j, core_ref: (si, core_ref[0] * nb + j, 0)),
                  pl.BlockSpec((None, tr, c), lambda si, j, core_ref: (si, j, 0))],
        out_specs=pl.BlockSpec((None, tr, c), lambda si, j, core_ref: (si, j, 0)))
    return pl.pallas_call(
        body, name=name, out_shape=jax.ShapeDtypeStruct((s, h, c), BF16), grid_spec=grid_spec,
        compiler_params=_params(2),
    )(core, g4, recv)


def _chip_sum(name, parts, recv, chip):
    _, h, c = parts.shape
    tr = _row_tile(h, c)

    def body(chip_ref, p_ref, r_ref, o_ref):
        acc = p_ref[...].astype(F32)
        for i in range(len(CHIP_FLIPS)):
            acc = acc + r_ref[i].astype(F32)
        o_ref[...] = acc

    grid_spec = pltpu.PrefetchScalarGridSpec(
        num_scalar_prefetch=1, grid=(h // tr,),
        in_specs=[pl.BlockSpec((None, tr, c), lambda j, chip_ref: (chip_ref[0], j, 0)),
                  pl.BlockSpec((len(CHIP_FLIPS), tr, c), lambda j, chip_ref: (0, j, 0))],
        out_specs=pl.BlockSpec((tr, c), lambda j, chip_ref: (j, 0)))
    return pl.pallas_call(
        body, name=name, out_shape=jax.ShapeDtypeStruct((h, c), F32), grid_spec=grid_spec, compiler_params=_params(1),
    )(chip, parts, recv)


def _adam_update(w, gv, m, v):
    c1 = 1.0 - ADAM_B1 ** ADAM_STEP
    c2 = 1.0 - ADAM_B2 ** ADAM_STEP
    nm = ADAM_B1 * m + (1.0 - ADAM_B1) * gv
    nv = ADAM_B2 * v + (1.0 - ADAM_B2) * (gv * gv)
    return -ADAM_LR * ((nm / c1) / (jnp.sqrt(nv / c2) + ADAM_EPS) + ADAM_WD * w), nm, nv


def _adamw_halves(name, w, g_own, g_sib, m, v, core):
    r, c = w.shape
    h = r // 2
    tr = _row_tile(h, c, budget=1024 * 1024)
    nbh = h // tr

    def body(core_ref, w_ref, go_ref, gs_ref, m_ref, v_ref, g_ref, d_ref, nm_ref, nv_ref):
        mine = (pl.program_id(0) // nbh) == core_ref[0]
        gv = jnp.where(mine, go_ref[...], gs_ref[...])
        g_ref[...] = gv
        d_ref[...], nm_ref[...], nv_ref[...] = _adam_update(w_ref[...], gv, m_ref[...], v_ref[...])

    def blk(fn):
        return pl.BlockSpec((tr, c), fn)

    full = blk(lambda j, core_ref: (j, 0))
    own = blk(lambda j, core_ref: (jnp.clip(j - core_ref[0] * nbh, 0, nbh - 1), 0))
    sib = blk(lambda j, core_ref: (jnp.clip(j - (1 - core_ref[0]) * nbh, 0, nbh - 1), 0))
    grid_spec = pltpu.PrefetchScalarGridSpec(
        num_scalar_prefetch=1, grid=(r // tr,), in_specs=[full, own, sib, full, full], out_specs=(full,) * 4)
    sds = jax.ShapeDtypeStruct((r, c), F32)
    return pl.pallas_call(
        body, name=name, out_shape=(sds,) * 4, grid_spec=grid_spec, compiler_params=_params(1),
    )(core, w, g_own, g_sib, m, v)


def _adamw(name, w, g, m, v):
    r, c = w.shape

    def body(w_ref, g_ref, m_ref, v_ref, d_ref, nm_ref, nv_ref):
        d_ref[...], nm_ref[...], nv_ref[...] = _adam_update(w_ref[...], g_ref[...], m_ref[...], v_ref[...])

    blk = pl.BlockSpec((r, c), lambda j: (0, 0))
    sds = jax.ShapeDtypeStruct((r, c), F32)
    return pl.pallas_call(
        body, name=name, out_shape=(sds, sds, sds), grid=(1,), in_specs=[blk] * 4, out_specs=(blk,) * 3,
        compiler_params=_params(1),
    )(w, g, m, v)


def _cast_into_slot(name, w, chip, dtype, deps=()):
    r, c = w.shape
    tr = _row_tile(r, c)

    def body(chip_ref, w_ref, *rest):
        rest[-1][...] = w_ref[...].astype(dtype)

    grid_spec = pltpu.PrefetchScalarGridSpec(
        num_scalar_prefetch=1, grid=(r // tr,),
        in_specs=[pl.BlockSpec((tr, c), lambda j, chip_ref: (j, 0))] + [ANY] * len(deps),
        out_specs=pl.BlockSpec((None, tr, c), lambda j, chip_ref: (chip_ref[0], j, 0)))
    return pl.pallas_call(
        body, name=name, out_shape=jax.ShapeDtypeStruct((4, r, c), dtype), grid_spec=grid_spec, compiler_params=_params(1),
    )(chip, w, *deps)


def _place():
    return lax.axis_index("x"), lax.axis_index("y"), lax.axis_index("c")


def _chip_of(x, y, flip):
    px, py = x ^ flip[0], y ^ flip[1]
    return px, py, 2 * px + py


def _half(ref, which):
    rows = ref.shape[0] // 2
    return ref.at[pl.ds(which * rows, rows)]


HBM = pl.BlockSpec(memory_space=pltpu.HBM)
SEM = pl.BlockSpec(memory_space=pltpu.SEMAPHORE)
SPLIT_COPY = pltpu.CompilerParams(has_side_effects=pltpu.SideEffectType.DATAFLOW_SIDE_EFFECTING)


def _in_hbm(arrays):
    return [pltpu.with_memory_space_constraint(t, pltpu.HBM) for t in arrays]


TOKEN = jax.ShapeDtypeStruct((SMALL_ROWS, LANES), F32)
TOKEN_SPEC = pl.BlockSpec(memory_space=pltpu.VMEM)


def _gather_start(name, slabs, groups):
    n = len(slabs)
    ng = len(groups)
    nf = len(CHIP_FLIPS)

    def body(*refs):
        sems, outs = refs[n:n + 2 * ng], refs[n + 2 * ng:2 * n + 2 * ng]
        token = refs[2 * n + 2 * ng]
        token[...] = jnp.zeros_like(token)
        x, y, c = _place()
        k = 2 * x + y
        for g, members in enumerate(groups):
            for i, a in enumerate(members):
                for j, flip in enumerate(CHIP_FLIPS):
                    px, py, _ = _chip_of(x, y, flip)
                    mine = _half(outs[a].at[k], c)
                    pltpu.make_async_remote_copy(
                        src_ref=mine, dst_ref=mine, send_sem=sems[2 * g].at[i * nf + j], recv_sem=sems[2 * g + 1].at[i * nf + j],
                        device_id=(px, py, c), device_id_type=MESH).start()

    sem_shapes = []
    for members in groups:
        sem_shapes += [pltpu.SemaphoreType.DMA((nf * len(members),))] * 2
    res = pl.pallas_call(
        body, name=name,
        out_shape=tuple(sem_shapes) + tuple(pltpu.HBM(t.shape, t.dtype) for t in slabs) + (TOKEN,),
        in_specs=[HBM] * n, out_specs=tuple([SEM] * (2 * ng) + [HBM] * n + [TOKEN_SPEC]),
        input_output_aliases={a: 2 * ng + a for a in range(n)}, compiler_params=SPLIT_COPY,
    )(*_in_hbm(slabs))
    return [(res[2 * g], res[2 * g + 1]) for g in range(ng)], list(res[2 * ng:2 * ng + n]), res[2 * ng + n]


def _gather_wait(name, slabs, sems, after):
    n = len(slabs)
    nf = len(CHIP_FLIPS)

    def body(*refs):
        ins = refs[:n]
        ssem, rsem = refs[n], refs[n + 1]
        x, y, c = _place()
        k = 2 * x + y
        for a in range(n):
            for j, flip in enumerate(CHIP_FLIPS):
                _, _, kj = _chip_of(x, y, flip)
                cp = pltpu.make_async_remote_copy(
                    src_ref=_half(ins[a].at[k], c), dst_ref=_half(ins[a].at[kj], c), send_sem=ssem.at[a * nf + j],
                    recv_sem=rsem.at[a * nf + j], device_id=(x, y, c), device_id_type=MESH)
                cp.wait_send()
                cp.wait_recv()

    return pl.pallas_call(
        body, name=name, out_shape=tuple(pltpu.HBM(t.shape, t.dtype) for t in slabs),
        in_specs=[HBM] * n + [SEM, SEM, ANY], out_specs=tuple([HBM] * n),
        input_output_aliases={a: a for a in range(n)}, compiler_params=SPLIT_COPY,
    )(*slabs, sems[0], sems[1], after)


def _gather_pass(name, slabs, sems, after):
    n = len(slabs)
    nf = len(CHIP_FLIPS)

    def body(*refs):
        ins = refs[:n]
        ssem, rsem = refs[n], refs[n + 1]
        ssem2, rsem2 = refs[n + 3], refs[n + 4]
        x, y, c = _place()
        k = 2 * x + y
        for a in range(n):
            for j, flip in enumerate(CHIP_FLIPS):
                _, _, kj = _chip_of(x, y, flip)
                landed = _half(ins[a].at[kj], c)
                cp = pltpu.make_async_remote_copy(
                    src_ref=_half(ins[a].at[k], c), dst_ref=landed, send_sem=ssem.at[a * nf + j],
                    recv_sem=rsem.at[a * nf + j], device_id=(x, y, c), device_id_type=MESH)
                cp.wait_send()
                cp.wait_recv()
                pltpu.make_async_remote_copy(
                    src_ref=landed, dst_ref=landed, send_sem=ssem2.at[a * nf + j], recv_sem=rsem2.at[a * nf + j],
                    device_id=(x, y, 1 - c), device_id_type=MESH).start()

    sem = pltpu.SemaphoreType.DMA((nf * n,))
    res = pl.pallas_call(
        body, name=name, out_shape=(sem, sem) + tuple(pltpu.HBM(t.shape, t.dtype) for t in slabs),
        in_specs=[HBM] * n + [SEM, SEM, ANY], out_specs=tuple([SEM, SEM] + [HBM] * n),
        input_output_aliases={a: 2 + a for a in range(n)}, compiler_params=SPLIT_COPY,
    )(*slabs, sems[0], sems[1], after)
    return (res[0], res[1]), list(res[2:])


def _pass_wait(name, slabs, sems, after):
    n = len(slabs)
    nf = len(CHIP_FLIPS)

    def body(*refs):
        ins = refs[:n]
        ssem, rsem = refs[n], refs[n + 1]
        x, y, c = _place()
        for a in range(n):
            for j, flip in enumerate(CHIP_FLIPS):
                _, _, kj = _chip_of(x, y, flip)
                cp = pltpu.make_async_remote_copy(
                    src_ref=_half(ins[a].at[kj], c), dst_ref=_half(ins[a].at[kj], 1 - c), send_sem=ssem.at[a * nf + j],
                    recv_sem=rsem.at[a * nf + j], device_id=(x, y, c), device_id_type=MESH)
                cp.wait_send()
                cp.wait_recv()

    return pl.pallas_call(
        body, name=name, out_shape=tuple(pltpu.HBM(t.shape, t.dtype) for t in slabs),
        in_specs=[HBM] * n + [SEM, SEM, ANY], out_specs=tuple([HBM] * n),
        input_output_aliases={a: a for a in range(n)}, compiler_params=SPLIT_COPY,
    )(*slabs, sems[0], sems[1], after)


def _swap_start(name, grads):
    n = len(grads)

    def body(*refs):
        ssem, rsem = refs[2 * n], refs[2 * n + 1]
        src, land = refs[2 * n + 2:3 * n + 2], refs[3 * n + 2:4 * n + 2]
        token = refs[4 * n + 2]
        token[...] = jnp.zeros_like(token)
        x, y, c = _place()
        for a in range(n):
            h = src[a].shape[1] // 2
            pltpu.make_async_remote_copy(
                src_ref=src[a].at[:, pl.ds((1 - c) * h, h)], dst_ref=land[a], send_sem=ssem.at[a], recv_sem=rsem.at[a],
                device_id=(x, y, 1 - c), device_id_type=MESH).start()

    zones = [lax.empty((g.shape[0], g.shape[1] // 2, g.shape[2]), g.dtype) for g in grads]
    sem = pltpu.SemaphoreType.DMA((n,))
    res = pl.pallas_call(
        body, name=name,
        out_shape=(sem, sem) + tuple(pltpu.HBM(t.shape, t.dtype) for t in list(grads) + zones) + (TOKEN,),
        in_specs=[HBM] * (2 * n), out_specs=tuple([SEM, SEM] + [HBM] * (2 * n) + [TOKEN_SPEC]),
        input_output_aliases={i: 2 + i for i in range(2 * n)}, compiler_params=SPLIT_COPY,
    )(*_in_hbm(list(grads) + zones))
    return (res[0], res[1], list(res[2:2 + n]), list(res[2 + n:2 + 2 * n])), res[2 + 2 * n]


def _swap_wait(name, ssem, rsem, grads, zones, after):
    n = len(grads)

    def body(*refs):
        src, land = refs[:n], refs[n:2 * n]
        ss, rs = refs[2 * n], refs[2 * n + 1]
        x, y, c = _place()
        for a in range(n):
            h = src[a].shape[1] // 2
            cp = pltpu.make_async_remote_copy(
                src_ref=src[a].at[:, pl.ds((1 - c) * h, h)], dst_ref=land[a], send_sem=ss.at[a], recv_sem=rs.at[a],
                device_id=(x, y, c), device_id_type=MESH)
            cp.wait_send()
            cp.wait_recv()

    res = pl.pallas_call(
        body, name=name, out_shape=tuple(pltpu.HBM(t.shape, t.dtype) for t in list(grads) + list(zones)),
        in_specs=[HBM] * (2 * n) + [SEM, SEM, ANY], out_specs=tuple([HBM] * (2 * n)),
        input_output_aliases={i: i for i in range(2 * n)}, compiler_params=SPLIT_COPY,
    )(*grads, *zones, ssem, rsem, after)
    return list(res[:n]), list(res[n:])


def _sibling_exchange(name, slabs):
    n = len(slabs)
    nf = len(CHIP_FLIPS)

    def body(*refs):
        outs = refs[n:2 * n]
        ssem, rsem = refs[2 * n:]
        x, y, c = _place()

        def copy(a, j, which, to):
            _, _, kj = _chip_of(x, y, CHIP_FLIPS[j])
            ref = _half(outs[a].at[kj], which)
            return pltpu.make_async_remote_copy(src_ref=ref, dst_ref=ref, send_sem=ssem.at[a * nf + j],
                                                recv_sem=rsem.at[a * nf + j], device_id=to, device_id_type=MESH)

        sends = [copy(a, j, c, (x, y, 1 - c)) for a in range(n) for j in range(nf)]
        for cp in sends:
            cp.start()
        for a in range(n):
            for j in range(nf):
                copy(a, j, 1 - c, (x, y, c)).wait_recv()
        for cp in sends:
            cp.wait_send()

    return pl.pallas_call(
        body, name=name, out_shape=tuple(jax.ShapeDtypeStruct(t.shape, t.dtype) for t in slabs),
        in_specs=[ANY] * n, out_specs=(ANY,) * n, input_output_aliases={a: a for a in range(n)},
        scratch_shapes=[pltpu.SemaphoreType.DMA((nf * n,)), pltpu.SemaphoreType.DMA((nf * n,))],
    )(*slabs)


def _sibling_swap(name, grads):
    n = len(grads)

    def body(*refs):
        ins, outs = refs[:n], refs[n:2 * n]
        ssem, rsem = refs[2 * n:]
        x, y, c = _place()
        cps = []
        for a in range(n):
            h = ins[a].shape[1] // 2
            cps.append(pltpu.make_async_remote_copy(
                src_ref=ins[a].at[:, pl.ds((1 - c) * h, h)], dst_ref=outs[a], send_sem=ssem.at[a], recv_sem=rsem.at[a],
                device_id=(x, y, 1 - c), device_id_type=MESH))
        for cp in cps:
            cp.start()
        for cp in cps:
            cp.wait()

    return pl.pallas_call(
        body, name=name,
        out_shape=tuple(jax.ShapeDtypeStruct((g.shape[0], g.shape[1] // 2, g.shape[2]), g.dtype) for g in grads),
        in_specs=[ANY] * n, out_specs=(ANY,) * n,
        scratch_shapes=[pltpu.SemaphoreType.DMA((n,)), pltpu.SemaphoreType.DMA((n,))],
    )(*grads)


def _scatter_start(name, parts):
    n = len(parts)
    nf = len(CHIP_FLIPS)

    def body(*refs):
        ssem, rsem = refs[2 * n], refs[2 * n + 1]
        src, land = refs[2 * n + 2:3 * n + 2], refs[3 * n + 2:4 * n + 2]
        token = refs[4 * n + 2]
        token[...] = jnp.zeros_like(token)
        x, y, c = _place()
        for a in range(n):
            for j, flip in enumerate(CHIP_FLIPS):
                px, py, kj = _chip_of(x, y, flip)
                pltpu.make_async_remote_copy(
                    src_ref=src[a].at[kj], dst_ref=land[a].at[j], send_sem=ssem.at[a * nf + j], recv_sem=rsem.at[a * nf + j],
                    device_id=(px, py, c), device_id_type=MESH).start()

    zones = [lax.empty((nf,) + p.shape[1:], p.dtype) for p in parts]
    sem = pltpu.SemaphoreType.DMA((nf * n,))
    res = pl.pallas_call(
        body, name=name,
        out_shape=(sem, sem) + tuple(pltpu.HBM(t.shape, t.dtype) for t in list(parts) + zones)
        + (jax.ShapeDtypeStruct((SMALL_ROWS, LANES), F32),),
        in_specs=[HBM] * (2 * n),
        out_specs=tuple([SEM, SEM] + [HBM] * (2 * n) + [pl.BlockSpec(memory_space=pltpu.VMEM)]),
        input_output_aliases={i: 2 + i for i in range(2 * n)}, compiler_params=SPLIT_COPY,
    )(*_in_hbm(list(parts) + zones))
    return (res[0], res[1], list(res[2:2 + n]), list(res[2 + n:2 + 2 * n])), res[2 + 2 * n]


def _scatter_wait(name, ssem, rsem, parts, zones, after):
    n = len(parts)
    nf = len(CHIP_FLIPS)

    def body(*refs):
        src, land = refs[:n], refs[n:2 * n]
        ss, rs = refs[2 * n], refs[2 * n + 1]
        x, y, c = _place()
        for a in range(n):
            for j, flip in enumerate(CHIP_FLIPS):
                _, _, kj = _chip_of(x, y, flip)
                cp = pltpu.make_async_remote_copy(
                    src_ref=src[a].at[kj], dst_ref=land[a].at[j], send_sem=ss.at[a * nf + j], recv_sem=rs.at[a * nf + j],
                    device_id=(x, y, c), device_id_type=MESH)
                cp.wait_send()
                cp.wait_recv()

    res = pl.pallas_call(
        body, name=name, out_shape=tuple(pltpu.HBM(t.shape, t.dtype) for t in list(parts) + list(zones)),
        in_specs=[HBM] * (2 * n) + [SEM, SEM, ANY], out_specs=tuple([HBM] * (2 * n)),
        input_output_aliases={i: i for i in range(2 * n)}, compiler_params=SPLIT_COPY,
    )(*parts, *zones, ssem, rsem, after)
    return list(res[:n]), list(res[n:])


def _sibling_send(name, halves):
    n = len(halves)

    def body(*refs):
        ins, outs = refs[:n], refs[n:2 * n]
        ssem, rsem = refs[2 * n:]
        x, y, c = _place()
        cps = [pltpu.make_async_remote_copy(src_ref=ins[a], dst_ref=outs[a], send_sem=ssem.at[a], recv_sem=rsem.at[a],
                                            device_id=(x, y, 1 - c), device_id_type=MESH) for a in range(n)]
        for cp in cps:
            cp.start()
        for cp in cps:
            cp.wait()

    return pl.pallas_call(
        body, name=name,
        out_shape=tuple(jax.ShapeDtypeStruct(h.shape, h.dtype) for h in halves),
        in_specs=[ANY] * n, out_specs=(ANY,) * n,
        scratch_shapes=[pltpu.SemaphoreType.DMA((n,)), pltpu.SemaphoreType.DMA((n,))],
    )(*halves)


def _small_all_reduce(vec, loss_row, loss_scale, after):
    r, dm = vec.shape

    def body(v_ref, _, o_ref, l_ref, buf, ssem, rsem):
        x, y, c = _place()
        me = 4 * x + 2 * y + c
        buf[me] = v_ref[...]
        cps = []
        for mask in range(1, 8):
            fx, fy, fc = (mask >> 2) & 1, (mask >> 1) & 1, mask & 1
            cps.append(pltpu.make_async_remote_copy(
                src_ref=v_ref, dst_ref=buf.at[me], send_sem=ssem.at[mask - 1], recv_sem=rsem.at[mask - 1],
                device_id=(x ^ fx, y ^ fy, c ^ fc), device_id_type=MESH))
        for cp in cps:
            cp.start()
        for mask in range(1, 8):
            fx, fy, fc = (mask >> 2) & 1, (mask >> 1) & 1, mask & 1
            frm = 4 * (x ^ fx) + 2 * (y ^ fy) + (c ^ fc)
            pltpu.make_async_remote_copy(
                src_ref=v_ref, dst_ref=buf.at[frm], send_sem=ssem.at[mask - 1], recv_sem=rsem.at[mask - 1],
                device_id=(x, y, c), device_id_type=MESH).wait_recv()
        for cp in cps:
            cp.wait_send()
        acc = buf[0]
        for i in range(1, 8):
            acc = acc + buf[i]
        o_ref[...] = acc
        l_ref[...] = jnp.sum(acc[loss_row:loss_row + SMALL_ROWS, :], axis=(0, 1), keepdims=True) * loss_scale

    vm = pl.BlockSpec(memory_space=pltpu.VMEM)
    return pl.pallas_call(
        body, name="small_all_reduce",
        out_shape=(jax.ShapeDtypeStruct((r, dm), F32), jax.ShapeDtypeStruct((1, 1), F32)),
        in_specs=[vm, ANY], out_specs=(vm, vm),
        scratch_shapes=[pltpu.VMEM((8, r, dm), F32), pltpu.SemaphoreType.DMA((7,)), pltpu.SemaphoreType.DMA((7,))],
    )(vec, after)


def kernel(x, meta_tokens, norm_mix_g, w_in, b_gate, pool_w, pool_scale, conv_w, conv_out_w, w_o, norm_ffn_g, w_gate_up, w_down, norm_final_g, loss_target, m_meta_tokens, m_norm_mix_g, m_w_in, m_b_gate, m_pool_w, m_pool_scale, m_conv_w, m_conv_out_w, m_w_o, m_norm_ffn_g, m_w_gate_up, m_w_down, m_norm_final_g, v_meta_tokens, v_norm_mix_g, v_w_in, v_b_gate, v_pool_w, v_pool_scale, v_conv_w, v_conv_out_w, v_w_o, v_norm_ffn_g, v_w_gate_up, v_w_down, v_norm_final_g):
    seq, dm = x.shape[1], x.shape[2]
    tail = LANES
    tm = tail
    lp = seq + tail
    n_chips = 4
    n_groups = len(POOL_WINDOWS)
    gw = dm // n_groups
    tc = min(256, gw)
    cx, cy, cc = _place()
    chip = 2 * cx + cy
    dloc = dm // n_chips

    pool2 = pool_w.reshape(n_groups * pool_w.shape[1], gw)
    big = {"w_in": w_in, "w_gate_up": w_gate_up, "pool_w": pool2, "conv_out_w": conv_out_w, "w_o": w_o, "w_down": w_down}
    chip1 = jnp.reshape(chip, (1,)).astype(jnp.int32)
    core = jnp.reshape(cc, (1,)).astype(jnp.int32)
    small_loc = jnp.concatenate([meta_tokens, jnp.pad(conv_w, ((0, 8 - conv_w.shape[0]), (0, 0))),
                                 jnp.zeros((8, dloc), F32)], axis=0)
    first = [_cast_into_slot("cast_w_in", w_in, chip1, BF16), _cast_into_slot("place_small", small_loc, chip1, F32)]
    (sems0,), (w_in4, small4), token0 = _gather_start("gather_start_first", first, ([0, 1],))
    rest_names = ["pool_w", "conv_out_w", "w_o", "w_gate_up", "w_down"]
    rest = [_cast_into_slot("cast_" + nme, big[nme], chip1, BF16, deps=(token0,)) for nme in rest_names]
    groups = ([0, 1, 2], [3], [4])
    sems, rest, _ = _gather_start("gather_start_rest", rest, groups)

    def passing(g, after):
        return _gather_pass("gather_pass_%d" % g, [rest[a] for a in groups[g]], sems[g], after)

    g1, g2, g3 = norm_mix_g.reshape(1, dm), norm_ffn_g.reshape(1, dm), norm_final_g.reshape(1, dm)
    b_gate2 = b_gate.reshape(2, dm)
    ps = pool_scale.reshape(1, dm)
    w_in4, small4 = _sibling_exchange("sibling_exchange_first", list(_gather_wait("gather_wait_first", [w_in4, small4], sems0, rest[0])))
    small_f = jnp.transpose(small4, (1, 0, 2)).reshape(small4.shape[1], dm)
    meta_f = small_f[:N_META]
    conv_w_f = small_f[N_META:N_META + 3]
    h0 = jnp.concatenate([x[0], jnp.zeros((tail - N_META, dm), F32), meta_f], axis=0)
    hn1 = _rms_fwd("rms_mix", h0, g1, tm)
    proj = _nn_sharded("proj", hn1, w_in4, 6)
    pass_sems, passed = passing(0, proj)
    pooled, z = _mixer_fwd("mixer_fwd", proj, conv_w_f, tc)
    pool4, conv_out4, w_o4 = _pass_wait("pass_wait_0", passed, pass_sems, pooled)
    pool4 = pool4.reshape(n_chips, n_groups, gw // n_chips, gw)
    conv_out_f = conv_out4.reshape(dm, dm)
    w_o_f = w_o4.reshape(dm, dm)
    ya = _pool_fwd("pool_proj", pooled, pool4)
    yb = _nn_plain("conv_out", z, conv_out_f, BF16)
    mix = _gate_mix("gate_mix", proj, b_gate2, ya, ps, yb, tm)
    pass_sems, passed = passing(1, mix)
    h1 = _nn_plain("attn_out", mix, w_o_f, F32, res=h0, tn_pref=256)
    hn2 = _rms_fwd("rms_ffn", h1, g2, tm)
    (w_gu4,) = _pass_wait("pass_wait_1", passed, pass_sems, hn2)
    gu = _nn_sharded("gate_up", hn2, w_gu4, 2)
    pass_sems, passed = passing(2, gu)
    act = _swiglu_fwd("swiglu", gu, tm)
    (w_down4,) = _pass_wait("pass_wait_2", passed, pass_sems, act)
    w_down_f = w_down4.reshape(-1, dm)
    h2 = _nn_plain("ffn_down", act, w_down_f, F32, res=h1, tn_pref=512, tk_pref=1536)
    dh2, dh2b, loss_cols, dg3 = _final_loss("final_loss", h2, g3, loss_target[0], tm)

    def scatter(tag, names_g, swap, after):
        grads_g, got = _swap_wait("swap_wait_" + tag, *swap, after)
        pairs = [_pair_add("pair_add_" + nme, g4, rv, core) for nme, g4, rv in zip(names_g, grads_g, got)]
        return _scatter_start("scatter_start_" + tag, pairs)

    dact = _nt_plain("d_act", dh2b, w_down_f)
    gw_down = _tn_plain("dw_down", act, dh2b)
    dgu = _swiglu_bwd("swiglu_bwd", dact, gu, tm)
    gw_gu = _tn_sharded("dw_gate_up", hn2, dgu, n_chips)
    swap_a, token = _swap_start("swap_start_a", [gw_gu, gw_down.reshape(n_chips, -1, dm)])
    dhn2 = _nt_sharded("d_hn2", dgu, w_gu4, deps=(token,))
    flight_a, token = scatter("a", ["w_gate_up", "w_down"], swap_a, dhn2)
    dh1, dh1b, dg2 = _rms_bwd("rms_ffn_bwd", dhn2, h1, g2, dh2, tm, token)
    dmix = _nt_plain("d_mix", dh1b, w_o_f)
    gw_o = _tn_plain("dw_o", mix, dh1b)
    dproj, dyb, dya, db_gate, dps = _gate_bwd("gate_bwd", dmix, proj, b_gate2, ya, ps, yb, tm)
    gw_conv_out = _tn_plain("dw_conv_out", z, dyb)
    gw_pool = _pool_bwd_w("dw_pool", pooled, dya, n_chips)
    swap_b, token = _swap_start("swap_start_b", [gw_o.reshape(n_chips, dloc, dm), gw_conv_out.reshape(n_chips, dloc, dm),
                                                 gw_pool.reshape(n_chips, n_groups * (gw // n_chips), gw)])
    dpooled = _pool_bwd_act("d_pooled", dya, pool4, deps=(token,))
    dz = _nt_plain("d_z", dyb, conv_out_f)
    flight_b, token = scatter("b", ["w_o", "conv_out_w", "pool_w"], swap_b, dz)
    dproj, dconv_w = _mixer_bwd("mixer_bwd", dz, dpooled, proj, conv_w_f, dproj, tc, token)
    gw_in = _tn_sharded("dw_in", hn1, dproj, n_chips)
    swap_c, token = _swap_start("swap_start_c", [gw_in])
    dhn1 = _nt_sharded("d_hn1", dproj, w_in4, deps=(token,))
    flight_c, token = scatter("c", ["w_in"], swap_c, dhn1)
    dh0, _, dg1 = _rms_bwd("rms_mix_bwd", dhn1, h0, g1, dh1, tm, token)
    grad_x = dh0[:seq][None]
    dmeta = dh0[lp - N_META:]

    names = ["w_in", "w_gate_up", "pool_w", "conv_out_w", "w_o", "w_down"]
    vec = jnp.concatenate([dg1, dg2, dg3, db_gate, dps, loss_cols, dconv_w, dmeta], axis=0)
    loss_row = 5 * SMALL_ROWS
    g_halves = {}
    after = dh0
    for tag, names_g, flight in (("a", ["w_gate_up", "w_down"], flight_a), ("b", ["w_o", "conv_out_w", "pool_w"], flight_b),
                                 ("c", ["w_in"], flight_c)):
        pairs, zones = _scatter_wait("scatter_wait_" + tag, *flight, after)
        if tag == "c":
            red, loss11 = _small_all_reduce(vec, loss_row, 0.5 / dm, pairs[0])
        halves = [_chip_sum("chip_sum_" + nme, p, rv, chip1) for nme, p, rv in zip(names_g, pairs, zones)]
        sib_halves = _sibling_send("sibling_send_" + tag, halves)
        g_halves.update(zip(names_g, zip(halves, sib_halves)))
        after = sib_halves[0]
    loss = loss11[0, 0]
    col0 = chip * dloc
    g_small = {
        "norm_mix_g": red[0], "norm_ffn_g": red[SMALL_ROWS], "norm_final_g": red[2 * SMALL_ROWS],
        "b_gate": red[3 * SMALL_ROWS:3 * SMALL_ROWS + 2].reshape(-1), "pool_scale": red[4 * SMALL_ROWS],
        "conv_w": lax.dynamic_slice(red, (6 * SMALL_ROWS, col0), (3, dloc)),
        "meta_tokens": lax.dynamic_slice(red, (7 * SMALL_ROWS, col0), (N_META, dloc)),
    }

    given = dict(meta_tokens=(meta_tokens, m_meta_tokens, v_meta_tokens), norm_mix_g=(norm_mix_g, m_norm_mix_g, v_norm_mix_g),
                 w_in=(w_in, m_w_in, v_w_in), b_gate=(b_gate, m_b_gate, v_b_gate), pool_w=(pool_w, m_pool_w, v_pool_w),
                 pool_scale=(pool_scale, m_pool_scale, v_pool_scale), conv_w=(conv_w, m_conv_w, v_conv_w),
                 conv_out_w=(conv_out_w, m_conv_out_w, v_conv_out_w), w_o=(w_o, m_w_o, v_w_o),
                 norm_ffn_g=(norm_ffn_g, m_norm_ffn_g, v_norm_ffn_g), w_gate_up=(w_gate_up, m_w_gate_up, v_w_gate_up),
                 w_down=(w_down, m_w_down, v_w_down), norm_final_g=(norm_final_g, m_norm_final_g, v_norm_final_g))
    order = list(given.keys())
    grad, delta, new_m, new_v = {}, {}, {}, {}
    for nme in names:
        w, m, v = given[nme]
        g_own, g_sib = g_halves[nme]
        shape2 = (2 * g_own.shape[0], g_own.shape[1])
        res4 = _adamw_halves("adamw_" + nme, w.reshape(shape2), g_own, g_sib, m.reshape(shape2), v.reshape(shape2), core)
        grad[nme], delta[nme], new_m[nme], new_v[nme] = [t.reshape(w.shape) for t in res4]
    vec_names = ["norm_mix_g", "norm_ffn_g", "norm_final_g", "pool_scale"]

    def slab_vec(pick):
        rows = [pick(nme).reshape(1, dm) for nme in vec_names] + [pick("b_gate").reshape(2, dm), jnp.zeros((2, dm), F32)]
        return jnp.concatenate(rows, axis=0)

    def slab_col(pick):
        return jnp.concatenate([pick("meta_tokens"), pick("conv_w"), jnp.zeros((5, dloc), F32)], axis=0)

    for slab, tag in ((slab_vec, "vec"), (slab_col, "col")):
        d, nm, nv = _adamw("adamw_small_" + tag, slab(lambda nme: given[nme][0]), slab(lambda nme: g_small[nme]),
                           slab(lambda nme: given[nme][1]), slab(lambda nme: given[nme][2]))
        for out, res in ((delta, d), (new_m, nm), (new_v, nv)):
            if tag == "vec":
                for i, nme in enumerate(vec_names):
                    out[nme] = res[i]
                out["b_gate"] = res[4:6].reshape(-1)
            else:
                out["meta_tokens"] = res[:N_META]
                out["conv_w"] = res[N_META:N_META + 3]
    grad.update(g_small)
    return (loss, grad_x, *[grad[nme] for nme in order], *[delta[nme] for nme in order],
            *[new_m[nme] for nme in order], *[new_v[nme] for nme in order])
```

```python
import functools
import math

import jax
import jax.numpy as jnp
from jax import lax
from jax.experimental import pallas as pl
from jax.experimental.pallas import tpu as pltpu

F32 = jnp.float32
BF16 = jnp.bfloat16
N_META = 16
POOL_WINDOWS = (2, 4, 8, 16)
EPS = 1e-6
ADAM_LR, ADAM_B1, ADAM_B2, ADAM_EPS, ADAM_WD, ADAM_STEP = 0.001, 0.9, 0.999, 1e-08, 0.01, 10
LANES = 128
V7X_VMEM_BYTES = 64 * 1024 * 1024
VMEM_LIMIT = V7X_VMEM_BYTES - 8 * 1024 * 1024
MESH = pl.DeviceIdType.MESH
ANY = pl.BlockSpec(memory_space=pl.ANY)
CHIP_FLIPS = ((1, 0), (0, 1), (1, 1))
SMALL_ROWS = 8


def _pick(n, pref):
    best = None
    for t in range(LANES, min(n, pref) + 1, LANES):
        if n % t == 0:
            best = t
    assert best is not None, (n, pref)
    return best


def _params(n_axes=0):
    sem = ("arbitrary",) * n_axes if n_axes else None
    return pltpu.CompilerParams(dimension_semantics=sem, vmem_limit_bytes=VMEM_LIMIT)


_DIMS = {
    "nn": (((1,), (0,)), ((), ())),
    "nt": (((1,), (1,)), ((), ())),
    "tn": (((0,), (0,)), ((), ())),
}


def _matmul(name, mode, a, b, out_sds, grid, a_spec, b_spec, o_spec, nk, res=None, res_spec=None, acc_shape=None, deps=()):
    out_dtype = out_sds.dtype
    in_place = nk > 1 and out_dtype == F32
    use_scratch = nk > 1 and not in_place
    rows = a_spec.block_shape[-2] if mode != "tn" else None
    chunk = _row_tile(rows, 1, 1, 1152) if rows is not None else None
    n_in = 2 + (res is not None) + len(deps)

    def body(*refs):
        a_ref, b_ref = refs[:2]
        r_ref = refs[2] if res is not None else None
        o_ref, *scr = refs[n_in:]
        k = pl.program_id(len(grid) - 1) if nk > 1 else None

        def emit(sl):
            if sl is None:
                part = lax.dot_general(a_ref[...], b_ref[...], _DIMS[mode], preferred_element_type=F32)
                idx = (slice(None), slice(None))
            else:
                part = lax.dot_general(a_ref[sl, :], b_ref[...], _DIMS[mode], preferred_element_type=F32)
                idx = (sl, slice(None))
            if nk == 1:
                if r_ref is not None:
                    part = part + r_ref[idx]
                o_ref[idx] = part.astype(out_dtype)
                return
            acc = scr[0] if use_scratch else o_ref

            @pl.when(k == 0)
            def _():
                first = part
                if r_ref is not None and in_place:
                    first = first + r_ref[idx]
                acc[idx] = first

            @pl.when(k > 0)
            def _():
                acc[idx] += part

            if use_scratch:

                @pl.when(k == nk - 1)
                def _():
                    o_ref[idx] = acc[idx].astype(out_dtype)

        if mode == "tn" or chunk == rows:
            emit(None)
        else:
            for m0 in range(0, rows, chunk):
                emit(pl.ds(m0, chunk))

    ins = [a, b] + ([res] if res is not None else []) + list(deps)
    in_specs = [a_spec, b_spec] + ([res_spec] if res is not None else []) + [ANY] * len(deps)
    scratch = [pltpu.VMEM(acc_shape, F32)] if use_scratch else []
    return pl.pallas_call(
        body, name=name, out_shape=out_sds, grid=grid, in_specs=in_specs, out_specs=o_spec,
        scratch_shapes=scratch, compiler_params=_params(len(grid)),
    )(*ins)


def _nn_sharded(name, a, w4, nseg):
    lp, kdim = a.shape
    s, _, nloc = w4.shape
    segw = s * nloc // nseg
    tn = _pick(math.gcd(nloc, segw), 1536)
    bw, bo = nloc // tn, segw // tn
    return _matmul(
        name, "nn", a, w4, jax.ShapeDtypeStruct((nseg, lp, segw), BF16), (s * bw,),
        pl.BlockSpec((lp, kdim), lambda j: (0, 0)),
        pl.BlockSpec((None, kdim, tn), lambda j: (j // bw, 0, j % bw)),
        pl.BlockSpec((None, lp, tn), lambda j: (j // bo, 0, j % bo)), 1)


def _nn_plain(name, a, w, out_dtype, res=None, tn_pref=512, tk_pref=2048):
    lp, kdim = a.shape
    n = w.shape[1]
    tn = _pick(n, tn_pref)
    tk = kdim if kdim <= tk_pref else _pick(kdim, tk_pref)
    nk = kdim // tk
    grid = (n // tn, nk) if nk > 1 else (n // tn,)
    if nk > 1:
        a_spec = pl.BlockSpec((lp, tk), lambda j, k: (0, k))
        w_spec = pl.BlockSpec((tk, tn), lambda j, k: (k, j))
        o_spec = pl.BlockSpec((lp, tn), lambda j, k: (0, j))
    else:
        a_spec = pl.BlockSpec((lp, tk), lambda j: (0, 0))
        w_spec = pl.BlockSpec((tk, tn), lambda j: (0, j))
        o_spec = pl.BlockSpec((lp, tn), lambda j: (0, j))
    return _matmul(name, "nn", a, w, jax.ShapeDtypeStruct((lp, n), out_dtype), grid, a_spec, w_spec, o_spec, nk,
                   res=res, res_spec=o_spec if res is not None else None, acc_shape=(lp, tn))


def _nt_plain(name, a, w, tn_pref=512):
    lp, kdim = a.shape
    n = w.shape[0]
    tn = _pick(n, tn_pref)
    return _matmul(
        name, "nt", a, w, jax.ShapeDtypeStruct((lp, n), BF16), (n // tn,),
        pl.BlockSpec((lp, kdim), lambda j: (0, 0)),
        pl.BlockSpec((tn, kdim), lambda j: (j, 0)),
        pl.BlockSpec((lp, tn), lambda j: (0, j)), 1)


def _nt_sharded(name, dseg, w4, to_pref=1024, deps=()):
    nseg, lp, segw = dseg.shape
    s, kdim, nloc = w4.shape
    tr = _pick(math.gcd(nloc, segw), 1536)
    ba, bw = segw // tr, nloc // tr
    nr = s * bw
    to = _pick(kdim, to_pref)
    return _matmul(
        name, "nt", dseg, w4, jax.ShapeDtypeStruct((lp, kdim), F32), (kdim // to, nr),
        pl.BlockSpec((None, lp, tr), lambda j, r: (r // ba, 0, r % ba)),
        pl.BlockSpec((None, to, tr), lambda j, r: (r // bw, j, r % bw)),
        pl.BlockSpec((lp, to), lambda j, r: (0, j)), nr, deps=deps)


def _tn_plain(name, a, d, tk_pref=512):
    lp, kdim = a.shape
    n = d.shape[1]
    tk = _pick(kdim, tk_pref)
    return _matmul(
        name, "tn", a, d, jax.ShapeDtypeStruct((kdim, n), BF16), (kdim // tk,),
        pl.BlockSpec((lp, tk), lambda i: (0, i)),
        pl.BlockSpec((lp, n), lambda i: (0, 0)),
        pl.BlockSpec((tk, n), lambda i: (i, 0)), 1)


def _tn_sharded(name, a, dseg, s, tk_pref=512):
    lp, kdim = a.shape
    nseg, _, segw = dseg.shape
    nloc = nseg * segw // s
    tn = _pick(math.gcd(nloc, segw), 1536)
    bd, bo = segw // tn, nloc // tn
    tk = _pick(kdim, tk_pref)
    return _matmul(
        name, "tn", a, dseg, jax.ShapeDtypeStruct((s, kdim, nloc), BF16), (s * bo, kdim // tk),
        pl.BlockSpec((lp, tk), lambda j, i: (0, i)),
        pl.BlockSpec((None, lp, tn), lambda j, i: (j // bd, 0, j % bd)),
        pl.BlockSpec((None, tk, tn), lambda j, i: (j // bo, i, j % bo)), 1)


def _silu_parts(gt):
    sg = jax.nn.sigmoid(gt)
    return gt * sg, sg * (1.0 + gt * (1.0 - sg))


def _gate_up_swiglu(name, a, w4, tn_pref=256):
    lp, kdim = a.shape
    s, _, nloc = w4.shape
    f = s * nloc // 2
    tn = _pick(nloc, tn_pref)
    bw = nloc // tn
    chunk = _row_tile(lp, 1, 1, 1152)

    def body(a_ref, wg_ref, wu_ref, gu_ref, act_ref):
        for m0 in range(0, lp, chunk):
            sl = pl.ds(m0, chunk)
            gt = jnp.dot(a_ref[sl, :], wg_ref[...], preferred_element_type=F32)
            up = jnp.dot(a_ref[sl, :], wu_ref[...], preferred_element_type=F32)
            gu_ref[0, sl, :] = gt.astype(BF16)
            gu_ref[1, sl, :] = up.astype(BF16)
            act_ref[sl, :] = (_silu_parts(gt)[0] * up).astype(BF16)

    return pl.pallas_call(
        body, name=name, grid=(f // tn,),
        out_shape=(jax.ShapeDtypeStruct((2, lp, f), BF16), jax.ShapeDtypeStruct((lp, f), BF16)),
        in_specs=[pl.BlockSpec((lp, kdim), lambda j: (0, 0)),
                  pl.BlockSpec((None, kdim, tn), lambda j: (j // bw, 0, j % bw)),
                  pl.BlockSpec((None, kdim, tn), lambda j: (s // 2 + j // bw, 0, j % bw))],
        out_specs=(pl.BlockSpec((2, lp, tn), lambda j: (0, 0, j)), pl.BlockSpec((lp, tn), lambda j: (0, j))),
        compiler_params=_params(1),
    )(a, w4, w4)


def _dact_swiglu_bwd(name, d, w, gu, tn_pref=512):
    lp, dm = d.shape
    f = w.shape[0]
    tn = _pick(f, tn_pref)
    chunk = _row_tile(lp, 1, 1, 1152)

    def body(d_ref, w_ref, g_ref, u_ref, o_ref):
        for m0 in range(0, lp, chunk):
            sl = pl.ds(m0, chunk)
            dact = lax.dot_general(d_ref[sl, :], w_ref[...], _DIMS["nt"], preferred_element_type=F32)
            silu, dsilu = _silu_parts(g_ref[sl, :].astype(F32))
            o_ref[0, sl, :] = (dact * u_ref[sl, :].astype(F32) * dsilu).astype(BF16)
            o_ref[1, sl, :] = (dact * silu).astype(BF16)

    return pl.pallas_call(
        body, name=name, grid=(f // tn,), out_shape=jax.ShapeDtypeStruct((2, lp, f), BF16),
        in_specs=[pl.BlockSpec((lp, dm), lambda j: (0, 0)), pl.BlockSpec((tn, dm), lambda j: (j, 0)),
                  pl.BlockSpec((None, lp, tn), lambda j: (0, 0, j)), pl.BlockSpec((None, lp, tn), lambda j: (1, 0, j))],
        out_specs=pl.BlockSpec((2, lp, tn), lambda j: (0, 0, j)), compiler_params=_params(1),
    )(d, w, gu, gu)


def _pool_fwd(name, pooled, pw4):
    lp, dm = pooled.shape
    s, g, rs, gw = pw4.shape
    return _matmul(
        name, "nn", pooled, pw4, jax.ShapeDtypeStruct((lp, dm), BF16), (g, s),
        pl.BlockSpec((lp, rs), lambda gi, si: (0, gi * s + si)),
        pl.BlockSpec((None, None, rs, gw), lambda gi, si: (si, gi, 0, 0)),
        pl.BlockSpec((lp, gw), lambda gi, si: (0, gi)), s, acc_shape=(lp, gw))


def _pool_bwd_act(name, dya, pw4, deps=()):
    lp, dm = dya.shape
    s, g, rs, gw = pw4.shape
    return _matmul(
        name, "nt", dya, pw4, jax.ShapeDtypeStruct((lp, dm), BF16), (g, s),
        pl.BlockSpec((lp, gw), lambda gi, si: (0, gi)),
        pl.BlockSpec((None, None, rs, gw), lambda gi, si: (si, gi, 0, 0)),
        pl.BlockSpec((lp, rs), lambda gi, si: (0, gi * s + si)), 1, deps=deps)


def _pool_bwd_w(name, pooled, dya, s):
    lp, dm = pooled.shape
    g = len(POOL_WINDOWS)
    gw = dm // g
    rs = gw // s
    return _matmul(
        name, "tn", pooled, dya, jax.ShapeDtypeStruct((s, g, rs, gw), BF16), (g, s),
        pl.BlockSpec((lp, rs), lambda gi, si: (0, gi * s + si)),
        pl.BlockSpec((lp, gw), lambda gi, si: (0, gi)),
        pl.BlockSpec((None, None, rs, gw), lambda gi, si: (si, gi, 0, 0)), 1)


def _rms_fwd(name, h, g, tm):
    lp, dm = h.shape

    def body(h_ref, g_ref, o_ref):
        hv = h_ref[...]
        r = lax.rsqrt(jnp.mean(hv * hv, axis=-1, keepdims=True) + EPS)
        o_ref[...] = (hv * r * g_ref[...]).astype(BF16)

    row = pl.BlockSpec((tm, dm), lambda i: (i, 0))
    return pl.pallas_call(
        body, name=name, out_shape=jax.ShapeDtypeStruct((lp, dm), BF16), grid=(lp // tm,),
        in_specs=[row, pl.BlockSpec((1, dm), lambda i: (0, 0))], out_specs=row, compiler_params=_params(1),
    )(h, g)


def _rms_bwd(name, dy, h, g, dres, tm, dep):
    lp, dm = h.shape

    def body(dy_ref, h_ref, g_ref, dr_ref, _, dh_ref, dhb_ref, dg_ref):
        hv = h_ref[...]
        r = lax.rsqrt(jnp.mean(hv * hv, axis=-1, keepdims=True) + EPS)
        xhat = hv * r
        dyv = dy_ref[...]
        dxh = dyv * g_ref[...]
        dh = dr_ref[...] + r * (dxh - xhat * jnp.mean(dxh * xhat, axis=-1, keepdims=True))
        dh_ref[...] = dh
        dhb_ref[...] = dh.astype(BF16)

        @pl.when(pl.program_id(0) == 0)
        def _():
            dg_ref[...] = jnp.zeros_like(dg_ref)

        dg_ref[0:1, :] += jnp.sum(dyv * xhat, axis=0, keepdims=True)

    row = pl.BlockSpec((tm, dm), lambda i: (i, 0))
    slab = pl.BlockSpec((SMALL_ROWS, dm), lambda i: (0, 0))
    return pl.pallas_call(
        body, name=name, grid=(lp // tm,),
        out_shape=(jax.ShapeDtypeStruct((lp, dm), F32), jax.ShapeDtypeStruct((lp, dm), BF16),
                   jax.ShapeDtypeStruct((SMALL_ROWS, dm), F32)),
        in_specs=[row, row, pl.BlockSpec((1, dm), lambda i: (0, 0)), row, ANY], out_specs=(row, row, slab),
        compiler_params=_params(1),
    )(dy, h, g, dres, dep)


def _gate_mix(name, proj, b_gate2, ya, pool_scale, yb, tm):
    _, lp, dm = proj.shape

    def body(ga_ref, gr_ref, b_ref, ya_ref, ps_ref, yb_ref, o_ref):
        g_a = jax.nn.sigmoid(ga_ref[...].astype(F32) + b_ref[0:1, :])
        g_b = jax.nn.sigmoid(gr_ref[...].astype(F32) + b_ref[1:2, :])
        y_a = ya_ref[...].astype(F32) * ps_ref[...]
        o_ref[...] = (g_a * y_a + g_b * yb_ref[...].astype(F32)).astype(BF16)

    row = pl.BlockSpec((tm, dm), lambda i: (i, 0))
    return pl.pallas_call(
        body, name=name, out_shape=jax.ShapeDtypeStruct((lp, dm), BF16), grid=(lp // tm,),
        in_specs=[pl.BlockSpec((None, tm, dm), lambda i: (4, i, 0)), pl.BlockSpec((None, tm, dm), lambda i: (5, i, 0)),
                  pl.BlockSpec((2, dm), lambda i: (0, 0)), row, pl.BlockSpec((1, dm), lambda i: (0, 0)), row],
        out_specs=row, compiler_params=_params(1),
    )(proj, proj, b_gate2, ya, pool_scale, yb)


def _gate_bwd(name, dmix, proj, b_gate2, ya, pool_scale, yb, tm):
    _, lp, dm = proj.shape

    def body(dm_ref, ga_ref, gr_ref, b_ref, ya_ref, ps_ref, yb_ref, dp_ref, dyb_ref, dya_ref, db_ref, dps_ref):
        dmx = dm_ref[...].astype(F32)
        g_a = jax.nn.sigmoid(ga_ref[...].astype(F32) + b_ref[0:1, :])
        g_b = jax.nn.sigmoid(gr_ref[...].astype(F32) + b_ref[1:2, :])
        ya_pre = ya_ref[...].astype(F32)
        ybv = yb_ref[...].astype(F32)
        ps = ps_ref[...]
        dga = dmx * (ya_pre * ps) * (g_a * (1.0 - g_a))
        dgr = dmx * ybv * (g_b * (1.0 - g_b))
        dp_ref[0] = dga.astype(BF16)
        dp_ref[1] = dgr.astype(BF16)
        dyb_ref[...] = (dmx * g_b).astype(BF16)
        dya_ref[...] = (dmx * g_a * ps).astype(BF16)

        @pl.when(pl.program_id(0) == 0)
        def _():
            db_ref[...] = jnp.zeros_like(db_ref)
            dps_ref[...] = jnp.zeros_like(dps_ref)

        db_ref[0:1, :] += jnp.sum(dga, axis=0, keepdims=True)
        db_ref[1:2, :] += jnp.sum(dgr, axis=0, keepdims=True)
        dps_ref[0:1, :] += jnp.sum(dmx * g_a * ya_pre, axis=0, keepdims=True)

    row = pl.BlockSpec((tm, dm), lambda i: (i, 0))
    one = pl.BlockSpec((1, dm), lambda i: (0, 0))
    slab = pl.BlockSpec((SMALL_ROWS, dm), lambda i: (0, 0))
    return pl.pallas_call(
        body, name=name, grid=(lp // tm,),
        out_shape=(jax.ShapeDtypeStruct((6, lp, dm), BF16), jax.ShapeDtypeStruct((lp, dm), BF16),
                   jax.ShapeDtypeStruct((lp, dm), BF16), jax.ShapeDtypeStruct((SMALL_ROWS, dm), F32),
                   jax.ShapeDtypeStruct((SMALL_ROWS, dm), F32)),
        in_specs=[row, pl.BlockSpec((None, tm, dm), lambda i: (4, i, 0)), pl.BlockSpec((None, tm, dm), lambda i: (5, i, 0)),
                  pl.BlockSpec((2, dm), lambda i: (0, 0)), row, one, row],
        out_specs=(pl.BlockSpec((2, tm, dm), lambda i: (2, i, 0)), row, row, slab, slab),
        compiler_params=_params(1),
    )(dmix, proj, proj, b_gate2, ya, pool_scale, yb)


def _swiglu_fwd(name, gu, tm):
    _, lp, f = gu.shape

    def body(g_ref, u_ref, o_ref):
        gt = g_ref[...].astype(F32)
        o_ref[...] = (gt * jax.nn.sigmoid(gt) * u_ref[...].astype(F32)).astype(BF16)

    return pl.pallas_call(
        body, name=name, out_shape=jax.ShapeDtypeStruct((lp, f), BF16), grid=(lp // tm,),
        in_specs=[pl.BlockSpec((None, tm, f), lambda i: (0, i, 0)), pl.BlockSpec((None, tm, f), lambda i: (1, i, 0))],
        out_specs=pl.BlockSpec((tm, f), lambda i: (i, 0)), compiler_params=_params(1),
    )(gu, gu)


def _swiglu_bwd(name, dact, gu, tm):
    _, lp, f = gu.shape

    def body(d_ref, g_ref, u_ref, o_ref):
        d = d_ref[...].astype(F32)
        gt = g_ref[...].astype(F32)
        sg = jax.nn.sigmoid(gt)
        o_ref[0] = (d * u_ref[...].astype(F32) * (sg * (1.0 + gt * (1.0 - sg)))).astype(BF16)
        o_ref[1] = (d * (gt * sg)).astype(BF16)

    return pl.pallas_call(
        body, name=name, out_shape=jax.ShapeDtypeStruct((2, lp, f), BF16), grid=(lp // tm,),
        in_specs=[pl.BlockSpec((tm, f), lambda i: (i, 0)), pl.BlockSpec((None, tm, f), lambda i: (0, i, 0)),
                  pl.BlockSpec((None, tm, f), lambda i: (1, i, 0))],
        out_specs=pl.BlockSpec((2, tm, f), lambda i: (0, i, 0)), compiler_params=_params(1),
    )(dact, gu, gu)


def _final_loss(name, h2, g3, target, tm):
    lp, dm = h2.shape
    nx = target.shape[0] // tm

    def body(h_ref, g_ref, t_ref, dh_ref, dhb_ref, ls_ref, dg_ref):
        i = pl.program_id(0)

        @pl.when(i == 0)
        def _():
            ls_ref[...] = jnp.zeros_like(ls_ref)
            dg_ref[...] = jnp.zeros_like(dg_ref)

        @pl.when(i < nx)
        def _():
            hv = h_ref[...]
            gv = g_ref[...]
            r = lax.rsqrt(jnp.mean(hv * hv, axis=-1, keepdims=True) + EPS)
            xhat = hv * r
            err = xhat * gv - t_ref[...]
            dout = err * (1.0 / dm)
            dxh = dout * gv
            dh = r * (dxh - xhat * jnp.mean(dxh * xhat, axis=-1, keepdims=True))
            dh_ref[...] = dh
            dhb_ref[...] = dh.astype(BF16)
            ls_ref[0:1, :] += jnp.sum(err * err, axis=0, keepdims=True)
            dg_ref[0:1, :] += jnp.sum(dout * xhat, axis=0, keepdims=True)

        @pl.when(i >= nx)
        def _():
            dh_ref[...] = jnp.zeros_like(dh_ref)
            dhb_ref[...] = jnp.zeros_like(dhb_ref)

    row = pl.BlockSpec((tm, dm), lambda i: (i, 0))
    slab = pl.BlockSpec((SMALL_ROWS, dm), lambda i: (0, 0))
    return pl.pallas_call(
        body, name=name, grid=(lp // tm,),
        out_shape=(jax.ShapeDtypeStruct((lp, dm), F32), jax.ShapeDtypeStruct((lp, dm), BF16),
                   jax.ShapeDtypeStruct((SMALL_ROWS, dm), F32), jax.ShapeDtypeStruct((SMALL_ROWS, dm), F32)),
        in_specs=[row, pl.BlockSpec((1, dm), lambda i: (0, 0)), pl.BlockSpec((tm, dm), lambda i: (jnp.minimum(i, nx - 1), 0))],
        out_specs=(row, row, slab, slab), compiler_params=_params(1),
    )(h2, g3, target)


def _shift(v, k):
    return pltpu.roll(v, k % v.shape[0], axis=0)


def _window_sum(v, group, sign):
    s2 = v + _shift(v, sign * 1)
    s4 = s2 + _shift(s2, sign * 2)
    s8 = s4 + _shift(s4, sign * 4)
    s16 = s8 + _shift(s8, sign * 8)
    return jnp.where(group == 0, s2, jnp.where(group == 1, s4, jnp.where(group == 2, s8, s16)))


def _pool_count(lp, group):
    row = lax.broadcasted_iota(jnp.int32, (lp, 1), 0)
    window = jnp.left_shift(2, group).astype(F32)
    meta_pos = (row - (lp - N_META) + 1).astype(F32)
    return jnp.where(row >= lp - N_META, jnp.minimum(meta_pos, window), window)


def _mixer_fwd(name, proj, conv_w, tc):
    _, lp, dm = proj.shape
    per_group = dm // len(POOL_WINDOWS) // tc

    def body(u_ref, gb_ref, gc_ref, v_ref, cw_ref, p_ref, z_ref):
        group = pl.program_id(0) // per_group
        u = u_ref[...].astype(F32)
        p_ref[...] = (_window_sum(u, group, 1) / _pool_count(lp, group) - u).astype(BF16)
        cv = gc_ref[...].astype(F32) * v_ref[...].astype(F32)
        conv = cw_ref[0:1, :] * _shift(cv, 2) + cw_ref[1:2, :] * _shift(cv, 1) + cw_ref[2:3, :] * cv
        z_ref[...] = (gb_ref[...].astype(F32) * conv).astype(BF16)

    def seg(s):
        return pl.BlockSpec((None, lp, tc), lambda j: (s, 0, j))

    col = pl.BlockSpec((lp, tc), lambda j: (0, j))
    return pl.pallas_call(
        body, name=name, grid=(dm // tc,),
        out_shape=(jax.ShapeDtypeStruct((lp, dm), BF16), jax.ShapeDtypeStruct((lp, dm), BF16)),
        in_specs=[seg(0), seg(1), seg(2), seg(3), pl.BlockSpec((3, tc), lambda j: (0, j))],
        out_specs=(col, col), compiler_params=_params(1),
    )(proj, proj, proj, proj, conv_w)


def _mixer_bwd(name, dz, dpooled, proj, conv_w, dproj, tc, dep):
    _, lp, dm = proj.shape
    per_group = dm // len(POOL_WINDOWS) // tc

    def body(dz_ref, dp_ref, gb_ref, gc_ref, v_ref, cw_ref, _, __, o_ref, dcw_ref):
        group = pl.program_id(0) // per_group
        dzv = dz_ref[...].astype(F32)
        gb = gb_ref[...].astype(F32)
        gc = gc_ref[...].astype(F32)
        vv = v_ref[...].astype(F32)
        cv = gc * vv
        c1 = _shift(cv, 1)
        c2 = _shift(cv, 2)
        w0, w1, w2 = cw_ref[0:1, :], cw_ref[1:2, :], cw_ref[2:3, :]
        o_ref[1] = (dzv * (w0 * c2 + w1 * c1 + w2 * cv)).astype(BF16)
        dconv = dzv * gb
        dcw_ref[...] = jnp.zeros_like(dcw_ref)
        dcw_ref[0:1, :] = jnp.sum(dconv * c2, axis=0, keepdims=True)
        dcw_ref[1:2, :] = jnp.sum(dconv * c1, axis=0, keepdims=True)
        dcw_ref[2:3, :] = jnp.sum(dconv * cv, axis=0, keepdims=True)
        dcv = w0 * _shift(dconv, -2) + w1 * _shift(dconv, -1) + w2 * dconv
        o_ref[2] = (dcv * vv).astype(BF16)
        o_ref[3] = (dcv * gc).astype(BF16)
        dpv = dp_ref[...].astype(F32)
        o_ref[0] = (_window_sum(dpv / _pool_count(lp, group), group, -1) - dpv).astype(BF16)

    def seg(s):
        return pl.BlockSpec((None, lp, tc), lambda j: (s, 0, j))

    col = pl.BlockSpec((lp, tc), lambda j: (0, j))
    return pl.pallas_call(
        body, name=name, grid=(dm // tc,),
        out_shape=(jax.ShapeDtypeStruct(dproj.shape, BF16), jax.ShapeDtypeStruct((SMALL_ROWS, dm), F32)),
        in_specs=[col, col, seg(1), seg(2), seg(3), pl.BlockSpec((3, tc), lambda j: (0, j)), ANY, ANY],
        out_specs=(pl.BlockSpec((4, lp, tc), lambda j: (0, 0, j)), pl.BlockSpec((SMALL_ROWS, tc), lambda j: (0, j))),
        input_output_aliases={6: 0}, compiler_params=_params(1),
    )(dz, dpooled, proj, proj, proj, conv_w, dproj, dep)


def _row_tile(r, c, bytes_per_row_elem=4, budget=2 * 1024 * 1024):
    best = None
    for t in range(16, r + 1, 16):
        if r % t == 0 and t * c * bytes_per_row_elem <= budget:
            best = t
    return best if best is not None else r


def _pair_add(name, g4, recv, core):
    s, r, c = g4.shape
    h = r // 2
    tr = _row_tile(h, c)
    nb = h // tr

    def body(core_ref, g_ref, r_ref, o_ref):
        o_ref[...] = (g_ref[...].astype(F32) + r_ref[...].astype(F32)).astype(BF16)

    grid_spec = pltpu.PrefetchScalarGridSpec(
        num_scalar_prefetch=1, grid=(s, nb),
        in_specs=[pl.BlockSpec((None, tr, c), lambda si, j, core_ref: (si, core_ref[0] * nb + j, 0)),
                  pl.BlockSpec((None, tr, c), lambda si, j, core_ref: (si, j, 0))],
        out_specs=pl.BlockSpec((None, tr, c), lambda si, j, core_ref: (si, j, 0)))
    return pl.pallas_call(
        body, name=name, out_shape=jax.ShapeDtypeStruct((s, h, c), BF16), grid_spec=grid_spec,
        compiler_params=_params(2),
    )(core, g4, recv)


def _chip_sum(name, parts, recv, chip):
    _, h, c = parts.shape
    tr = _row_tile(h, c)

    def body(chip_ref, p_ref, r_ref, o_ref):
        acc = p_ref[...].astype(F32)
        for i in range(len(CHIP_FLIPS)):
            acc = acc + r_ref[i].astype(F32)
        o_ref[...] = acc

    grid_spec = pltpu.PrefetchScalarGridSpec(
        num_scalar_prefetch=1, grid=(h // tr,),
        in_specs=[pl.BlockSpec((None, tr, c), lambda j, chip_ref: (chip_ref[0], j, 0)),
                  pl.BlockSpec((len(CHIP_FLIPS), tr, c), lambda j, chip_ref: (0, j, 0))],
        out_specs=pl.BlockSpec((tr, c), lambda j, chip_ref: (j, 0)))
    return pl.pallas_call(
        body, name=name, out_shape=jax.ShapeDtypeStruct((h, c), F32), grid_spec=grid_spec, compiler_params=_params(1),
    )(chip, parts, recv)


def _adam_update(w, gv, m, v):
    c1 = 1.0 - ADAM_B1 ** ADAM_STEP
    c2 = 1.0 - ADAM_B2 ** ADAM_STEP
    nm = ADAM_B1 * m + (1.0 - ADAM_B1) * gv
    nv = ADAM_B2 * v + (1.0 - ADAM_B2) * (gv * gv)
    return -ADAM_LR * ((nm / c1) / (jnp.sqrt(nv / c2) + ADAM_EPS) + ADAM_WD * w), nm, nv


def _adamw_halves(name, w, g_own, g_sib, m, v, core):
    r, c = w.shape
    h = r // 2
    tr = _row_tile(h, c, budget=1024 * 1024)
    nbh = h // tr

    def body(core_ref, w_ref, go_ref, gs_ref, m_ref, v_ref, g_ref, d_ref, nm_ref, nv_ref):
        mine = (pl.program_id(0) // nbh) == core_ref[0]
        gv = jnp.where(mine, go_ref[...], gs_ref[...])
        g_ref[...] = gv
        d_ref[...], nm_ref[...], nv_ref[...] = _adam_update(w_ref[...], gv, m_ref[...], v_ref[...])

    def blk(fn):
        return pl.BlockSpec((tr, c), fn)

    full = blk(lambda j, core_ref: (j, 0))
    own = blk(lambda j, core_ref: (jnp.clip(j - core_ref[0] * nbh, 0, nbh - 1), 0))
    sib = blk(lambda j, core_ref: (jnp.clip(j - (1 - core_ref[0]) * nbh, 0, nbh - 1), 0))
    grid_spec = pltpu.PrefetchScalarGridSpec(
        num_scalar_prefetch=1, grid=(r // tr,), in_specs=[full, own, sib, full, full], out_specs=(full,) * 4)
    sds = jax.ShapeDtypeStruct((r, c), F32)
    return pl.pallas_call(
        body, name=name, out_shape=(sds,) * 4, grid_spec=grid_spec, compiler_params=_params(1),
    )(core, w, g_own, g_sib, m, v)


def _adamw(name, w, g, m, v):
    r, c = w.shape

    def body(w_ref, g_ref, m_ref, v_ref, d_ref, nm_ref, nv_ref):
        d_ref[...], nm_ref[...], nv_ref[...] = _adam_update(w_ref[...], g_ref[...], m_ref[...], v_ref[...])

    blk = pl.BlockSpec((r, c), lambda j: (0, 0))
    sds = jax.ShapeDtypeStruct((r, c), F32)
    return pl.pallas_call(
        body, name=name, out_shape=(sds, sds, sds), grid=(1,), in_specs=[blk] * 4, out_specs=(blk,) * 3,
        compiler_params=_params(1),
    )(w, g, m, v)


def _cast_into_slot(name, w, chip, dtype, deps=()):
    r, c = w.shape
    tr = _row_tile(r, c)

    def body(chip_ref, w_ref, *rest):
        rest[-1][...] = w_ref[...].astype(dtype)

    grid_spec = pltpu.PrefetchScalarGridSpec(
        num_scalar_prefetch=1, grid=(r // tr,),
        in_specs=[pl.BlockSpec((tr, c), lambda j, chip_ref: (j, 0))] + [ANY] * len(deps),
        out_specs=pl.BlockSpec((None, tr, c), lambda j, chip_ref: (chip_ref[0], j, 0)))
    return pl.pallas_call(
        body, name=name, out_shape=jax.ShapeDtypeStruct((4, r, c), dtype), grid_spec=grid_spec, compiler_params=_params(1),
    )(chip, w, *deps)


def _place():
    return lax.axis_index("x"), lax.axis_index("y"), lax.axis_index("c")


def _chip_of(x, y, flip):
    px, py = x ^ flip[0], y ^ flip[1]
    return px, py, 2 * px + py


def _half(ref, which):
    rows = ref.shape[0] // 2
    return ref.at[pl.ds(which * rows, rows)]


HBM = pl.BlockSpec(memory_space=pltpu.HBM)
SEM = pl.BlockSpec(memory_space=pltpu.SEMAPHORE)
SPLIT_COPY = pltpu.CompilerParams(has_side_effects=pltpu.SideEffectType.DATAFLOW_SIDE_EFFECTING)


def _in_hbm(arrays):
    return [pltpu.with_memory_space_constraint(t, pltpu.HBM) for t in arrays]


TOKEN = jax.ShapeDtypeStruct((SMALL_ROWS, LANES), F32)
TOKEN_SPEC = pl.BlockSpec(memory_space=pltpu.VMEM)


def _gather_start(name, slabs, groups):
    n = len(slabs)
    ng = len(groups)
    nf = len(CHIP_FLIPS)

    def body(*refs):
        sems, outs = refs[n:n + 2 * ng], refs[n + 2 * ng:2 * n + 2 * ng]
        token = refs[2 * n + 2 * ng]
        token[...] = jnp.zeros_like(token)
        x, y, c = _place()
        k = 2 * x + y
        for g, members in enumerate(groups):
            for i, a in enumerate(members):
                for j, flip in enumerate(CHIP_FLIPS):
                    px, py, _ = _chip_of(x, y, flip)
                    mine = _half(outs[a].at[k], c)
                    pltpu.make_async_remote_copy(
                        src_ref=mine, dst_ref=mine, send_sem=sems[2 * g].at[i * nf + j], recv_sem=sems[2 * g + 1].at[i * nf + j],
                        device_id=(px, py, c), device_id_type=MESH).start()

    sem_shapes = []
    for members in groups:
        sem_shapes += [pltpu.SemaphoreType.DMA((nf * len(members),))] * 2
    res = pl.pallas_call(
        body, name=name,
        out_shape=tuple(sem_shapes) + tuple(pltpu.HBM(t.shape, t.dtype) for t in slabs) + (TOKEN,),
        in_specs=[HBM] * n, out_specs=tuple([SEM] * (2 * ng) + [HBM] * n + [TOKEN_SPEC]),
        input_output_aliases={a: 2 * ng + a for a in range(n)}, compiler_params=SPLIT_COPY,
    )(*_in_hbm(slabs))
    return [(res[2 * g], res[2 * g + 1]) for g in range(ng)], list(res[2 * ng:2 * ng + n]), res[2 * ng + n]


def _gather_wait(name, slabs, sems, after):
    n = len(slabs)
    nf = len(CHIP_FLIPS)

    def body(*refs):
        ins = refs[:n]
        ssem, rsem = refs[n], refs[n + 1]
        x, y, c = _place()
        k = 2 * x + y
        for a in range(n):
            for j, flip in enumerate(CHIP_FLIPS):
                _, _, kj = _chip_of(x, y, flip)
                cp = pltpu.make_async_remote_copy(
                    src_ref=_half(ins[a].at[k], c), dst_ref=_half(ins[a].at[kj], c), send_sem=ssem.at[a * nf + j],
                    recv_sem=rsem.at[a * nf + j], device_id=(x, y, c), device_id_type=MESH)
                cp.wait_send()
                cp.wait_recv()

    return pl.pallas_call(
        body, name=name, out_shape=tuple(pltpu.HBM(t.shape, t.dtype) for t in slabs),
        in_specs=[HBM] * n + [SEM, SEM, ANY], out_specs=tuple([HBM] * n),
        input_output_aliases={a: a for a in range(n)}, compiler_params=SPLIT_COPY,
    )(*slabs, sems[0], sems[1], after)


def _gather_pass(name, slabs, sems, after):
    n = len(slabs)
    nf = len(CHIP_FLIPS)

    def body(*refs):
        ins = refs[:n]
        ssem, rsem = refs[n], refs[n + 1]
        ssem2, rsem2 = refs[n + 3], refs[n + 4]
        x, y, c = _place()
        k = 2 * x + y
        for a in range(n):
            for j, flip in enumerate(CHIP_FLIPS):
                _, _, kj = _chip_of(x, y, flip)
                landed = _half(ins[a].at[kj], c)
                cp = pltpu.make_async_remote_copy(
                    src_ref=_half(ins[a].at[k], c), dst_ref=landed, send_sem=ssem.at[a * nf + j],
                    recv_sem=rsem.at[a * nf + j], device_id=(x, y, c), device_id_type=MESH)
                cp.wait_send()
                cp.wait_recv()
                pltpu.make_async_remote_copy(
                    src_ref=landed, dst_ref=landed, send_sem=ssem2.at[a * nf + j], recv_sem=rsem2.at[a * nf + j],
                    device_id=(x, y, 1 - c), device_id_type=MESH).start()

    sem = pltpu.SemaphoreType.DMA((nf * n,))
    res = pl.pallas_call(
        body, name=name, out_shape=(sem, sem) + tuple(pltpu.HBM(t.shape, t.dtype) for t in slabs),
        in_specs=[HBM] * n + [SEM, SEM, ANY], out_specs=tuple([SEM, SEM] + [HBM] * n),
        input_output_aliases={a: 2 + a for a in range(n)}, compiler_params=SPLIT_COPY,
    )(*slabs, sems[0], sems[1], after)
    return (res[0], res[1]), list(res[2:])


def _pass_wait(name, slabs, sems, after):
    n = len(slabs)
    nf = len(CHIP_FLIPS)

    def body(*refs):
        ins = refs[:n]
        ssem, rsem = refs[n], refs[n + 1]
        x, y, c = _place()
        for a in range(n):
            for j, flip in enumerate(CHIP_FLIPS):
                _, _, kj = _chip_of(x, y, flip)
                cp = pltpu.make_async_remote_copy(
                    src_ref=_half(ins[a].at[kj], c), dst_ref=_half(ins[a].at[kj], 1 - c), send_sem=ssem.at[a * nf + j],
                    recv_sem=rsem.at[a * nf + j], device_id=(x, y, c), device_id_type=MESH)
                cp.wait_send()
                cp.wait_recv()

    return pl.pallas_call(
        body, name=name, out_shape=tuple(pltpu.HBM(t.shape, t.dtype) for t in slabs),
        in_specs=[HBM] * n + [SEM, SEM, ANY], out_specs=tuple([HBM] * n),
        input_output_aliases={a: a for a in range(n)}, compiler_params=SPLIT_COPY,
    )(*slabs, sems[0], sems[1], after)


def _swap_start(name, grads):
    n = len(grads)

    def body(*refs):
        ssem, rsem = refs[2 * n], refs[2 * n + 1]
        src, land = refs[2 * n + 2:3 * n + 2], refs[3 * n + 2:4 * n + 2]
        token = refs[4 * n + 2]
        token[...] = jnp.zeros_like(token)
        x, y, c = _place()
        for a in range(n):
            h = src[a].shape[1] // 2
            pltpu.make_async_remote_copy(
                src_ref=src[a].at[:, pl.ds((1 - c) * h, h)], dst_ref=land[a], send_sem=ssem.at[a], recv_sem=rsem.at[a],
                device_id=(x, y, 1 - c), device_id_type=MESH).start()

    zones = [lax.empty((g.shape[0], g.shape[1] // 2, g.shape[2]), g.dtype) for g in grads]
    sem = pltpu.SemaphoreType.DMA((n,))
    res = pl.pallas_call(
        body, name=name,
        out_shape=(sem, sem) + tuple(pltpu.HBM(t.shape, t.dtype) for t in list(grads) + zones) + (TOKEN,),
        in_specs=[HBM] * (2 * n), out_specs=tuple([SEM, SEM] + [HBM] * (2 * n) + [TOKEN_SPEC]),
        input_output_aliases={i: 2 + i for i in range(2 * n)}, compiler_params=SPLIT_COPY,
    )(*_in_hbm(list(grads) + zones))
    return (res[0], res[1], list(res[2:2 + n]), list(res[2 + n:2 + 2 * n])), res[2 + 2 * n]


def _swap_wait(name, ssem, rsem, grads, zones, after):
    n = len(grads)

    def body(*refs):
        src, land = refs[:n], refs[n:2 * n]
        ss, rs = refs[2 * n], refs[2 * n + 1]
        x, y, c = _place()
        for a in range(n):
            h = src[a].shape[1] // 2
            cp = pltpu.make_async_remote_copy(
                src_ref=src[a].at[:, pl.ds((1 - c) * h, h)], dst_ref=land[a], send_sem=ss.at[a], recv_sem=rs.at[a],
                device_id=(x, y, c), device_id_type=MESH)
            cp.wait_send()
            cp.wait_recv()

    res = pl.pallas_call(
        body, name=name, out_shape=tuple(pltpu.HBM(t.shape, t.dtype) for t in list(grads) + list(zones)),
        in_specs=[HBM] * (2 * n) + [SEM, SEM, ANY], out_specs=tuple([HBM] * (2 * n)),
        input_output_aliases={i: i for i in range(2 * n)}, compiler_params=SPLIT_COPY,
    )(*grads, *zones, ssem, rsem, after)
    return list(res[:n]), list(res[n:])


def _sibling_exchange(name, slabs):
    n = len(slabs)
    nf = len(CHIP_FLIPS)

    def body(*refs):
        outs = refs[n:2 * n]
        ssem, rsem = refs[2 * n:]
        x, y, c = _place()

        def copy(a, j, which, to):
            _, _, kj = _chip_of(x, y, CHIP_FLIPS[j])
            ref = _half(outs[a].at[kj], which)
            return pltpu.make_async_remote_copy(src_ref=ref, dst_ref=ref, send_sem=ssem.at[a * nf + j],
                                                recv_sem=rsem.at[a * nf + j], device_id=to, device_id_type=MESH)

        sends = [copy(a, j, c, (x, y, 1 - c)) for a in range(n) for j in range(nf)]
        for cp in sends:
            cp.start()
        for a in range(n):
            for j in range(nf):
                copy(a, j, 1 - c, (x, y, c)).wait_recv()
        for cp in sends:
            cp.wait_send()

    return pl.pallas_call(
        body, name=name, out_shape=tuple(jax.ShapeDtypeStruct(t.shape, t.dtype) for t in slabs),
        in_specs=[ANY] * n, out_specs=(ANY,) * n, input_output_aliases={a: a for a in range(n)},
        scratch_shapes=[pltpu.SemaphoreType.DMA((nf * n,)), pltpu.SemaphoreType.DMA((nf * n,))],
    )(*slabs)


def _sibling_swap(name, grads):
    n = len(grads)

    def body(*refs):
        ins, outs = refs[:n], refs[n:2 * n]
        ssem, rsem = refs[2 * n:]
        x, y, c = _place()
        cps = []
        for a in range(n):
            h = ins[a].shape[1] // 2
            cps.append(pltpu.make_async_remote_copy(
                src_ref=ins[a].at[:, pl.ds((1 - c) * h, h)], dst_ref=outs[a], send_sem=ssem.at[a], recv_sem=rsem.at[a],
                device_id=(x, y, 1 - c), device_id_type=MESH))
        for cp in cps:
            cp.start()
        for cp in cps:
            cp.wait()

    return pl.pallas_call(
        body, name=name,
        out_shape=tuple(jax.ShapeDtypeStruct((g.shape[0], g.shape[1] // 2, g.shape[2]), g.dtype) for g in grads),
        in_specs=[ANY] * n, out_specs=(ANY,) * n,
        scratch_shapes=[pltpu.SemaphoreType.DMA((n,)), pltpu.SemaphoreType.DMA((n,))],
    )(*grads)


def _scatter_start(name, parts):
    n = len(parts)
    nf = len(CHIP_FLIPS)

    def body(*refs):
        ssem, rsem = refs[2 * n], refs[2 * n + 1]
        src, land = refs[2 * n + 2:3 * n + 2], refs[3 * n + 2:4 * n + 2]
        token = refs[4 * n + 2]
        token[...] = jnp.zeros_like(token)
        x, y, c = _place()
        for a in range(n):
            for j, flip in enumerate(CHIP_FLIPS):
                px, py, kj = _chip_of(x, y, flip)
                pltpu.make_async_remote_copy(
                    src_ref=src[a].at[kj], dst_ref=land[a].at[j], send_sem=ssem.at[a * nf + j], recv_sem=rsem.at[a * nf + j],
                    device_id=(px, py, c), device_id_type=MESH).start()

    zones = [lax.empty((nf,) + p.shape[1:], p.dtype) for p in parts]
    sem = pltpu.SemaphoreType.DMA((nf * n,))
    res = pl.pallas_call(
        body, name=name,
        out_shape=(sem, sem) + tuple(pltpu.HBM(t.shape, t.dtype) for t in list(parts) + zones)
        + (jax.ShapeDtypeStruct((SMALL_ROWS, LANES), F32),),
        in_specs=[HBM] * (2 * n),
        out_specs=tuple([SEM, SEM] + [HBM] * (2 * n) + [pl.BlockSpec(memory_space=pltpu.VMEM)]),
        input_output_aliases={i: 2 + i for i in range(2 * n)}, compiler_params=SPLIT_COPY,
    )(*_in_hbm(list(parts) + zones))
    return (res[0], res[1], list(res[2:2 + n]), list(res[2 + n:2 + 2 * n])), res[2 + 2 * n]


def _scatter_wait(name, ssem, rsem, parts, zones, after):
    n = len(parts)
    nf = len(CHIP_FLIPS)

    def body(*refs):
        src, land = refs[:n], refs[n:2 * n]
        ss, rs = refs[2 * n], refs[2 * n + 1]
        x, y, c = _place()
        for a in range(n):
            for j, flip in enumerate(CHIP_FLIPS):
                _, _, kj = _chip_of(x, y, flip)
                cp = pltpu.make_async_remote_copy(
                    src_ref=src[a].at[kj], dst_ref=land[a].at[j], send_sem=ss.at[a * nf + j], recv_sem=rs.at[a * nf + j],
                    device_id=(x, y, c), device_id_type=MESH)
                cp.wait_send()
                cp.wait_recv()

    res = pl.pallas_call(
        body, name=name, out_shape=tuple(pltpu.HBM(t.shape, t.dtype) for t in list(parts) + list(zones)),
        in_specs=[HBM] * (2 * n) + [SEM, SEM, ANY], out_specs=tuple([HBM] * (2 * n)),
        input_output_aliases={i: i for i in range(2 * n)}, compiler_params=SPLIT_COPY,
    )(*parts, *zones, ssem, rsem, after)
    return list(res[:n]), list(res[n:])


def _sibling_send(name, halves):
    n = len(halves)

    def body(*refs):
        ins, outs = refs[:n], refs[n:2 * n]
        ssem, rsem = refs[2 * n:]
        x, y, c = _place()
        cps = [pltpu.make_async_remote_copy(src_ref=ins[a], dst_ref=outs[a], send_sem=ssem.at[a], recv_sem=rsem.at[a],
                                            device_id=(x, y, 1 - c), device_id_type=MESH) for a in range(n)]
        for cp in cps:
            cp.start()
        for cp in cps:
            cp.wait()

    return pl.pallas_call(
        body, name=name,
        out_shape=tuple(jax.ShapeDtypeStruct(h.shape, h.dtype) for h in halves),
        in_specs=[ANY] * n, out_specs=(ANY,) * n,
        scratch_shapes=[pltpu.SemaphoreType.DMA((n,)), pltpu.SemaphoreType.DMA((n,))],
    )(*halves)


def _small_all_reduce(vec, loss_row, loss_scale, after):
    r, dm = vec.shape

    def body(v_ref, _, o_ref, l_ref, buf, ssem, rsem):
        x, y, c = _place()
        me = 4 * x + 2 * y + c
        buf[me] = v_ref[...]
        cps = []
        for mask in range(1, 8):
            fx, fy, fc = (mask >> 2) & 1, (mask >> 1) & 1, mask & 1
            cps.append(pltpu.make_async_remote_copy(
                src_ref=v_ref, dst_ref=buf.at[me], send_sem=ssem.at[mask - 1], recv_sem=rsem.at[mask - 1],
                device_id=(x ^ fx, y ^ fy, c ^ fc), device_id_type=MESH))
        for cp in cps:
            cp.start()
        for mask in range(1, 8):
            fx, fy, fc = (mask >> 2) & 1, (mask >> 1) & 1, mask & 1
            frm = 4 * (x ^ fx) + 2 * (y ^ fy) + (c ^ fc)
            pltpu.make_async_remote_copy(
                src_ref=v_ref, dst_ref=buf.at[frm], send_sem=ssem.at[mask - 1], recv_sem=rsem.at[mask - 1],
                device_id=(x, y, c), device_id_type=MESH).wait_recv()
        for cp in cps:
            cp.wait_send()
        acc = buf[0]
        for i in range(1, 8):
            acc = acc + buf[i]
        o_ref[...] = acc
        l_ref[...] = jnp.sum(acc[loss_row:loss_row + SMALL_ROWS, :], axis=(0, 1), keepdims=True) * loss_scale

    vm = pl.BlockSpec(memory_space=pltpu.VMEM)
    return pl.pallas_call(
        body, name="small_all_reduce",
        out_shape=(jax.ShapeDtypeStruct((r, dm), F32), jax.ShapeDtypeStruct((1, 1), F32)),
        in_specs=[vm, ANY], out_specs=(vm, vm),
        scratch_shapes=[pltpu.VMEM((8, r, dm), F32), pltpu.SemaphoreType.DMA((7,)), pltpu.SemaphoreType.DMA((7,))],
    )(vec, after)


def kernel(x, meta_tokens, norm_mix_g, w_in, b_gate, pool_w, pool_scale, conv_w, conv_out_w, w_o, norm_ffn_g, w_gate_up, w_down, norm_final_g, loss_target, m_meta_tokens, m_norm_mix_g, m_w_in, m_b_gate, m_pool_w, m_pool_scale, m_conv_w, m_conv_out_w, m_w_o, m_norm_ffn_g, m_w_gate_up, m_w_down, m_norm_final_g, v_meta_tokens, v_norm_mix_g, v_w_in, v_b_gate, v_pool_w, v_pool_scale, v_conv_w, v_conv_out_w, v_w_o, v_norm_ffn_g, v_w_gate_up, v_w_down, v_norm_final_g):
    seq, dm = x.shape[1], x.shape[2]
    tail = LANES
    tm = tail
    lp = seq + tail
    n_chips = 4
    n_groups = len(POOL_WINDOWS)
    gw = dm // n_groups
    tc = min(256, gw)
    cx, cy, cc = _place()
    chip = 2 * cx + cy
    dloc = dm // n_chips

    pool2 = pool_w.reshape(n_groups * pool_w.shape[1], gw)
    big = {"w_in": w_in, "w_gate_up": w_gate_up, "pool_w": pool2, "conv_out_w": conv_out_w, "w_o": w_o, "w_down": w_down}
    chip1 = jnp.reshape(chip, (1,)).astype(jnp.int32)
    core = jnp.reshape(cc, (1,)).astype(jnp.int32)
    small_loc = jnp.concatenate([meta_tokens, jnp.pad(conv_w, ((0, 8 - conv_w.shape[0]), (0, 0))),
                                 jnp.zeros((8, dloc), F32)], axis=0)
    first = [_cast_into_slot("cast_w_in", w_in, chip1, BF16), _cast_into_slot("place_small", small_loc, chip1, F32)]
    (sems0,), (w_in4, small4), token0 = _gather_start("gather_start_first", first, ([0, 1],))
    rest_names = ["pool_w", "conv_out_w", "w_o", "w_gate_up", "w_down"]
    rest = [_cast_into_slot("cast_" + nme, big[nme], chip1, BF16, deps=(token0,)) for nme in rest_names]
    groups = ([0, 1, 2], [3], [4])
    sems, rest, _ = _gather_start("gather_start_rest", rest, groups)

    def passing(g, after):
        return _gather_pass("gather_pass_%d" % g, [rest[a] for a in groups[g]], sems[g], after)

    g1, g2, g3 = norm_mix_g.reshape(1, dm), norm_ffn_g.reshape(1, dm), norm_final_g.reshape(1, dm)
    b_gate2 = b_gate.reshape(2, dm)
    ps = pool_scale.reshape(1, dm)
    w_in4, small4 = _sibling_exchange("sibling_exchange_first", list(_gather_wait("gather_wait_first", [w_in4, small4], sems0, rest[0])))
    small_f = jnp.transpose(small4, (1, 0, 2)).reshape(small4.shape[1], dm)
    meta_f = small_f[:N_META]
    conv_w_f = small_f[N_META:N_META + 3]
    h0 = jnp.concatenate([x[0], jnp.zeros((tail - N_META, dm), F32), meta_f], axis=0)
    hn1 = _rms_fwd("rms_mix", h0, g1, tm)
    proj = _nn_sharded("proj", hn1, w_in4, 6)
    pass_sems, passed = passing(0, proj)
    pooled, z = _mixer_fwd("mixer_fwd", proj, conv_w_f, tc)
    pool4, conv_out4, w_o4 = _pass_wait("pass_wait_0", passed, pass_sems, pooled)
    pool4 = pool4.reshape(n_chips, n_groups, gw // n_chips, gw)
    conv_out_f = conv_out4.reshape(dm, dm)
    w_o_f = w_o4.reshape(dm, dm)
    ya = _pool_fwd("pool_proj", pooled, pool4)
    yb = _nn_plain("conv_out", z, conv_out_f, BF16)
    mix = _gate_mix("gate_mix", proj, b_gate2, ya, ps, yb, tm)
    pass_sems, passed = passing(1, mix)
    h1 = _nn_plain("attn_out", mix, w_o_f, F32, res=h0, tn_pref=256)
    hn2 = _rms_fwd("rms_ffn", h1, g2, tm)
    (w_gu4,) = _pass_wait("pass_wait_1", passed, pass_sems, hn2)
    gu, act = _gate_up_swiglu("gate_up", hn2, w_gu4)
    pass_sems, passed = passing(2, act)
    (w_down4,) = _pass_wait("pass_wait_2", passed, pass_sems, act)
    w_down_f = w_down4.reshape(-1, dm)
    h2 = _nn_plain("ffn_down", act, w_down_f, F32, res=h1, tn_pref=512, tk_pref=1536)
    dh2, dh2b, loss_cols, dg3 = _final_loss("final_loss", h2, g3, loss_target[0], tm)

    def scatter(tag, names_g, swap, after):
        grads_g, got = _swap_wait("swap_wait_" + tag, *swap, after)
        pairs = [_pair_add("pair_add_" + nme, g4, rv, core) for nme, g4, rv in zip(names_g, grads_g, got)]
        return _scatter_start("scatter_start_" + tag, pairs)

    dgu = _dact_swiglu_bwd("d_gate_up", dh2b, w_down_f, gu)
    gw_down = _tn_plain("dw_down", act, dh2b)
    gw_gu = _tn_sharded("dw_gate_up", hn2, dgu, n_chips)
    swap_a, token = _swap_start("swap_start_a", [gw_gu, gw_down.reshape(n_chips, -1, dm)])
    dhn2 = _nt_sharded("d_hn2", dgu, w_gu4, deps=(token,))
    flight_a, token = scatter("a", ["w_gate_up", "w_down"], swap_a, dhn2)
    dh1, dh1b, dg2 = _rms_bwd("rms_ffn_bwd", dhn2, h1, g2, dh2, tm, token)
    dmix = _nt_plain("d_mix", dh1b, w_o_f)
    gw_o = _tn_plain("dw_o", mix, dh1b)
    dproj, dyb, dya, db_gate, dps = _gate_bwd("gate_bwd", dmix, proj, b_gate2, ya, ps, yb, tm)
    gw_conv_out = _tn_plain("dw_conv_out", z, dyb)
    gw_pool = _pool_bwd_w("dw_pool", pooled, dya, n_chips)
    swap_b, token = _swap_start("swap_start_b", [gw_o.reshape(n_chips, dloc, dm), gw_conv_out.reshape(n_chips, dloc, dm),
                                                 gw_pool.reshape(n_chips, n_groups * (gw // n_chips), gw)])
    dpooled = _pool_bwd_act("d_pooled", dya, pool4, deps=(token,))
    dz = _nt_plain("d_z", dyb, conv_out_f)
    flight_b, token = scatter("b", ["w_o", "conv_out_w", "pool_w"], swap_b, dz)
    dproj, dconv_w = _mixer_bwd("mixer_bwd", dz, dpooled, proj, conv_w_f, dproj, tc, token)
    gw_in = _tn_sharded("dw_in", hn1, dproj, n_chips)
    swap_c, token = _swap_start("swap_start_c", [gw_in])
    dhn1 = _nt_sharded("d_hn1", dproj, w_in4, deps=(token,))
    flight_c, token = scatter("c", ["w_in"], swap_c, dhn1)
    dh0, _, dg1 = _rms_bwd("rms_mix_bwd", dhn1, h0, g1, dh1, tm, token)
    grad_x = dh0[:seq][None]
    dmeta = dh0[lp - N_META:]

    given = dict(meta_tokens=(meta_tokens, m_meta_tokens, v_meta_tokens), norm_mix_g=(norm_mix_g, m_norm_mix_g, v_norm_mix_g),
                 w_in=(w_in, m_w_in, v_w_in), b_gate=(b_gate, m_b_gate, v_b_gate), pool_w=(pool_w, m_pool_w, v_pool_w),
                 pool_scale=(pool_scale, m_pool_scale, v_pool_scale), conv_w=(conv_w, m_conv_w, v_conv_w),
                 conv_out_w=(conv_out_w, m_conv_out_w, v_conv_out_w), w_o=(w_o, m_w_o, v_w_o),
                 norm_ffn_g=(norm_ffn_g, m_norm_ffn_g, v_norm_ffn_g), w_gate_up=(w_gate_up, m_w_gate_up, v_w_gate_up),
                 w_down=(w_down, m_w_down, v_w_down), norm_final_g=(norm_final_g, m_norm_final_g, v_norm_final_g))
    order = list(given.keys())
    grad, delta, new_m, new_v = {}, {}, {}, {}
    vec = jnp.concatenate([dg1, dg2, dg3, db_gate, dps, loss_cols, dconv_w, dmeta], axis=0)
    loss_row = 5 * SMALL_ROWS
    after = dh0
    for tag, names_g, flight in (("a", ["w_gate_up", "w_down"], flight_a), ("b", ["w_o", "conv_out_w", "pool_w"], flight_b),
                                 ("c", ["w_in"], flight_c)):
        pairs, zones = _scatter_wait("scatter_wait_" + tag, *flight, after)
        if tag == "c":
            red, loss11 = _small_all_reduce(vec, loss_row, 0.5 / dm, pairs[0])
        halves = [_chip_sum("chip_sum_" + nme, p, rv, chip1) for nme, p, rv in zip(names_g, pairs, zones)]
        sib_halves = _sibling_send("sibling_send_" + tag, halves)
        for nme, g_own, g_sib in zip(names_g, halves, sib_halves):
            w, m, v = given[nme]
            shape2 = (2 * g_own.shape[0], g_own.shape[1])
            res4 = _adamw_halves("adamw_" + nme, w.reshape(shape2), g_own, g_sib, m.reshape(shape2), v.reshape(shape2), core)
            grad[nme], delta[nme], new_m[nme], new_v[nme] = [t.reshape(w.shape) for t in res4]
            after = res4[1]
    loss = loss11[0, 0]
    col0 = chip * dloc
    g_small = {
        "norm_mix_g": red[0], "norm_ffn_g": red[SMALL_ROWS], "norm_final_g": red[2 * SMALL_ROWS],
        "b_gate": red[3 * SMALL_ROWS:3 * SMALL_ROWS + 2].reshape(-1), "pool_scale": red[4 * SMALL_ROWS],
        "conv_w": lax.dynamic_slice(red, (6 * SMALL_ROWS, col0), (3, dloc)),
        "meta_tokens": lax.dynamic_slice(red, (7 * SMALL_ROWS, col0), (N_META, dloc)),
    }

    vec_names = ["norm_mix_g", "norm_ffn_g", "norm_final_g", "pool_scale"]

    def slab_vec(pick):
        rows = [pick(nme).reshape(1, dm) for nme in vec_names] + [pick("b_gate").reshape(2, dm), jnp.zeros((2, dm), F32)]
        return jnp.concatenate(rows, axis=0)

    def slab_col(pick):
        return jnp.concatenate([pick("meta_tokens"), pick("conv_w"), jnp.zeros((5, dloc), F32)], axis=0)

    for slab, tag in ((slab_vec, "vec"), (slab_col, "col")):
        d, nm, nv = _adamw("adamw_small_" + tag, slab(lambda nme: given[nme][0]), slab(lambda nme: g_small[nme]),
                           slab(lambda nme: given[nme][1]), slab(lambda nme: given[nme][2]))
        for out, res in ((delta, d), (new_m, nm), (new_v, nv)):
            if tag == "vec":
                for i, nme in enumerate(vec_names):
                    out[nme] = res[i]
                out["b_gate"] = res[4:6].reshape(-1)
            else:
                out["meta_tokens"] = res[:N_META]
                out["conv_w"] = res[N_META:N_META + 3]
    grad.update(g_small)
    return (loss, grad_x, *[grad[nme] for nme in order], *[delta[nme] for nme in order],
            *[new_m[nme] for nme in order], *[new_v[nme] for nme in order])
```

```python
import functools
import math

import jax
import jax.numpy as jnp
from jax import lax
from jax.experimental import pallas as pl
from jax.experimental.pallas import tpu as pltpu

F32 = jnp.float32
BF16 = jnp.bfloat16
N_META = 16
POOL_WINDOWS = (2, 4, 8, 16)
EPS = 1e-6
ADAM_LR, ADAM_B1, ADAM_B2, ADAM_EPS, ADAM_WD, ADAM_STEP = 0.001, 0.9, 0.999, 1e-08, 0.01, 10
LANES = 128
V7X_VMEM_BYTES = 64 * 1024 * 1024
VMEM_LIMIT = V7X_VMEM_BYTES - 8 * 1024 * 1024
MESH = pl.DeviceIdType.MESH
ANY = pl.BlockSpec(memory_space=pl.ANY)
CHIP_FLIPS = ((1, 0), (0, 1), (1, 1))
SMALL_ROWS = 8


def _pick(n, pref):
    best = None
    for t in range(LANES, min(n, pref) + 1, LANES):
        if n % t == 0:
            best = t
    assert best is not None, (n, pref)
    return best


def _params(n_axes=0):
    sem = ("arbitrary",) * n_axes if n_axes else None
    return pltpu.CompilerParams(dimension_semantics=sem, vmem_limit_bytes=VMEM_LIMIT)


_DIMS = {
    "nn": (((1,), (0,)), ((), ())),
    "nt": (((1,), (1,)), ((), ())),
    "tn": (((0,), (0,)), ((), ())),
}


def _matmul(name, mode, a, b, out_sds, grid, a_spec, b_spec, o_spec, nk, res=None, res_spec=None, acc_shape=None, deps=()):
    out_dtype = out_sds.dtype
    in_place = nk > 1 and out_dtype == F32
    use_scratch = nk > 1 and not in_place
    rows = a_spec.block_shape[-2] if mode != "tn" else None
    chunk = _row_tile(rows, 1, 1, 1152) if rows is not None else None
    n_in = 2 + (res is not None) + len(deps)

    def body(*refs):
        a_ref, b_ref = refs[:2]
        r_ref = refs[2] if res is not None else None
        o_ref, *scr = refs[n_in:]
        k = pl.program_id(len(grid) - 1) if nk > 1 else None

        def emit(sl):
            if sl is None:
                part = lax.dot_general(a_ref[...], b_ref[...], _DIMS[mode], preferred_element_type=F32)
                idx = (slice(None), slice(None))
            else:
                part = lax.dot_general(a_ref[sl, :], b_ref[...], _DIMS[mode], preferred_element_type=F32)
                idx = (sl, slice(None))
            if nk == 1:
                if r_ref is not None:
                    part = part + r_ref[idx]
                o_ref[idx] = part.astype(out_dtype)
                return
            acc = scr[0] if use_scratch else o_ref

            @pl.when(k == 0)
            def _():
                first = part
                if r_ref is not None and in_place:
                    first = first + r_ref[idx]
                acc[idx] = first

            @pl.when(k > 0)
            def _():
                acc[idx] += part

            if use_scratch:

                @pl.when(k == nk - 1)
                def _():
                    o_ref[idx] = acc[idx].astype(out_dtype)

        if mode == "tn" or chunk == rows:
            emit(None)
        else:
            for m0 in range(0, rows, chunk):
                emit(pl.ds(m0, chunk))

    ins = [a, b] + ([res] if res is not None else []) + list(deps)
    in_specs = [a_spec, b_spec] + ([res_spec] if res is not None else []) + [ANY] * len(deps)
    scratch = [pltpu.VMEM(acc_shape, F32)] if use_scratch else []
    return pl.pallas_call(
        body, name=name, out_shape=out_sds, grid=grid, in_specs=in_specs, out_specs=o_spec,
        scratch_shapes=scratch, compiler_params=_params(len(grid)),
    )(*ins)


def _nn_sharded(name, a, w4, nseg):
    lp, kdim = a.shape
    s, _, nloc = w4.shape
    segw = s * nloc // nseg
    tn = _pick(math.gcd(nloc, segw), 1536)
    bw, bo = nloc // tn, segw // tn
    return _matmul(
        name, "nn", a, w4, jax.ShapeDtypeStruct((nseg, lp, segw), BF16), (s * bw,),
        pl.BlockSpec((lp, kdim), lambda j: (0, 0)),
        pl.BlockSpec((None, kdim, tn), lambda j: (j // bw, 0, j % bw)),
        pl.BlockSpec((None, lp, tn), lambda j: (j // bo, 0, j % bo)), 1)


def _nn_plain(name, a, w, out_dtype, res=None, tn_pref=512, tk_pref=2048):
    lp, kdim = a.shape
    n = w.shape[1]
    tn = _pick(n, tn_pref)
    tk = kdim if kdim <= tk_pref else _pick(kdim, tk_pref)
    nk = kdim // tk
    grid = (n // tn, nk) if nk > 1 else (n // tn,)
    if nk > 1:
        a_spec = pl.BlockSpec((lp, tk), lambda j, k: (0, k))
        w_spec = pl.BlockSpec((tk, tn), lambda j, k: (k, j))
        o_spec = pl.BlockSpec((lp, tn), lambda j, k: (0, j))
    else:
        a_spec = pl.BlockSpec((lp, tk), lambda j: (0, 0))
        w_spec = pl.BlockSpec((tk, tn), lambda j: (0, j))
        o_spec = pl.BlockSpec((lp, tn), lambda j: (0, j))
    return _matmul(name, "nn", a, w, jax.ShapeDtypeStruct((lp, n), out_dtype), grid, a_spec, w_spec, o_spec, nk,
                   res=res, res_spec=o_spec if res is not None else None, acc_shape=(lp, tn))


def _nt_plain(name, a, w, tn_pref=512):
    lp, kdim = a.shape
    n = w.shape[0]
    tn = _pick(n, tn_pref)
    return _matmul(
        name, "nt", a, w, jax.ShapeDtypeStruct((lp, n), BF16), (n // tn,),
        pl.BlockSpec((lp, kdim), lambda j: (0, 0)),
        pl.BlockSpec((tn, kdim), lambda j: (j, 0)),
        pl.BlockSpec((lp, tn), lambda j: (0, j)), 1)


def _nt_sharded(name, dseg, w4, to_pref=1024, deps=()):
    nseg, lp, segw = dseg.shape
    s, kdim, nloc = w4.shape
    tr = _pick(math.gcd(nloc, segw), 1536)
    ba, bw = segw // tr, nloc // tr
    nr = s * bw
    to = _pick(kdim, to_pref)
    return _matmul(
        name, "nt", dseg, w4, jax.ShapeDtypeStruct((lp, kdim), F32), (kdim // to, nr),
        pl.BlockSpec((None, lp, tr), lambda j, r: (r // ba, 0, r % ba)),
        pl.BlockSpec((None, to, tr), lambda j, r: (r // bw, j, r % bw)),
        pl.BlockSpec((lp, to), lambda j, r: (0, j)), nr, deps=deps)


def _tn_plain(name, a, d, tk_pref=512):
    lp, kdim = a.shape
    n = d.shape[1]
    tk = _pick(kdim, tk_pref)
    return _matmul(
        name, "tn", a, d, jax.ShapeDtypeStruct((kdim, n), BF16), (kdim // tk,),
        pl.BlockSpec((lp, tk), lambda i: (0, i)),
        pl.BlockSpec((lp, n), lambda i: (0, 0)),
        pl.BlockSpec((tk, n), lambda i: (i, 0)), 1)


def _tn_sharded(name, a, dseg, s, tk_pref=512):
    lp, kdim = a.shape
    nseg, _, segw = dseg.shape
    nloc = nseg * segw // s
    tn = _pick(math.gcd(nloc, segw), 1536)
    bd, bo = segw // tn, nloc // tn
    tk = _pick(kdim, tk_pref)

    def body(a_ref, d_ref, o_ref, at_ref):
        @pl.when(pl.program_id(1) == 0)
        def _():
            at_ref[...] = a_ref[...].T

        o_ref[...] = jnp.dot(at_ref[...], d_ref[...], preferred_element_type=F32).astype(BF16)

    return pl.pallas_call(
        body, name=name, out_shape=jax.ShapeDtypeStruct((s, kdim, nloc), BF16), grid=(kdim // tk, s * bo),
        in_specs=[pl.BlockSpec((lp, tk), lambda i, j: (0, i)),
                  pl.BlockSpec((None, lp, tn), lambda i, j: (j // bd, 0, j % bd))],
        out_specs=pl.BlockSpec((None, tk, tn), lambda i, j: (j // bo, i, j % bo)),
        scratch_shapes=[pltpu.VMEM((tk, lp), BF16)], compiler_params=_params(2),
    )(a, dseg)


def _silu_parts(gt):
    sg = jax.nn.sigmoid(gt)
    return gt * sg, sg * (1.0 + gt * (1.0 - sg))


def _gate_up_swiglu(name, a, w4, tn_pref=256):
    lp, kdim = a.shape
    s, _, nloc = w4.shape
    f = s * nloc // 2
    tn = _pick(nloc, tn_pref)
    bw = nloc // tn
    chunk = _row_tile(lp, 1, 1, 1152)

    def body(a_ref, wg_ref, wu_ref, gu_ref, act_ref):
        for m0 in range(0, lp, chunk):
            sl = pl.ds(m0, chunk)
            gt = jnp.dot(a_ref[sl, :], wg_ref[...], preferred_element_type=F32)
            up = jnp.dot(a_ref[sl, :], wu_ref[...], preferred_element_type=F32)
            gu_ref[0, sl, :] = gt.astype(BF16)
            gu_ref[1, sl, :] = up.astype(BF16)
            act_ref[sl, :] = (_silu_parts(gt)[0] * up).astype(BF16)

    return pl.pallas_call(
        body, name=name, grid=(f // tn,),
        out_shape=(jax.ShapeDtypeStruct((2, lp, f), BF16), jax.ShapeDtypeStruct((lp, f), BF16)),
        in_specs=[pl.BlockSpec((lp, kdim), lambda j: (0, 0)),
                  pl.BlockSpec((None, kdim, tn), lambda j: (j // bw, 0, j % bw)),
                  pl.BlockSpec((None, kdim, tn), lambda j: (s // 2 + j // bw, 0, j % bw))],
        out_specs=(pl.BlockSpec((2, lp, tn), lambda j: (0, 0, j)), pl.BlockSpec((lp, tn), lambda j: (0, j))),
        compiler_params=_params(1),
    )(a, w4, w4)


def _dact_swiglu_bwd(name, d, w, gu, tn_pref=512):
    lp, dm = d.shape
    f = w.shape[0]
    tn = _pick(f, tn_pref)
    chunk = _row_tile(lp, 1, 1, 1152)

    def body(d_ref, w_ref, g_ref, u_ref, o_ref):
        for m0 in range(0, lp, chunk):
            sl = pl.ds(m0, chunk)
            dact = lax.dot_general(d_ref[sl, :], w_ref[...], _DIMS["nt"], preferred_element_type=F32)
            silu, dsilu = _silu_parts(g_ref[sl, :].astype(F32))
            o_ref[0, sl, :] = (dact * u_ref[sl, :].astype(F32) * dsilu).astype(BF16)
            o_ref[1, sl, :] = (dact * silu).astype(BF16)

    return pl.pallas_call(
        body, name=name, grid=(f // tn,), out_shape=jax.ShapeDtypeStruct((2, lp, f), BF16),
        in_specs=[pl.BlockSpec((lp, dm), lambda j: (0, 0)), pl.BlockSpec((tn, dm), lambda j: (j, 0)),
                  pl.BlockSpec((None, lp, tn), lambda j: (0, 0, j)), pl.BlockSpec((None, lp, tn), lambda j: (1, 0, j))],
        out_specs=pl.BlockSpec((2, lp, tn), lambda j: (0, 0, j)), compiler_params=_params(1),
    )(d, w, gu, gu)


def _pool_fwd(name, pooled, pw4):
    lp, dm = pooled.shape
    s, g, rs, gw = pw4.shape
    return _matmul(
        name, "nn", pooled, pw4, jax.ShapeDtypeStruct((lp, dm), BF16), (g, s),
        pl.BlockSpec((lp, rs), lambda gi, si: (0, gi * s + si)),
        pl.BlockSpec((None, None, rs, gw), lambda gi, si: (si, gi, 0, 0)),
        pl.BlockSpec((lp, gw), lambda gi, si: (0, gi)), s, acc_shape=(lp, gw))


def _pool_bwd_act(name, dya, pw4, deps=()):
    lp, dm = dya.shape
    s, g, rs, gw = pw4.shape
    return _matmul(
        name, "nt", dya, pw4, jax.ShapeDtypeStruct((lp, dm), BF16), (g, s),
        pl.BlockSpec((lp, gw), lambda gi, si: (0, gi)),
        pl.BlockSpec((None, None, rs, gw), lambda gi, si: (si, gi, 0, 0)),
        pl.BlockSpec((lp, rs), lambda gi, si: (0, gi * s + si)), 1, deps=deps)


def _pool_bwd_w(name, pooled, dya, s):
    lp, dm = pooled.shape
    g = len(POOL_WINDOWS)
    gw = dm // g
    rs = gw // s
    return _matmul(
        name, "tn", pooled, dya, jax.ShapeDtypeStruct((s, g, rs, gw), BF16), (g, s),
        pl.BlockSpec((lp, rs), lambda gi, si: (0, gi * s + si)),
        pl.BlockSpec((lp, gw), lambda gi, si: (0, gi)),
        pl.BlockSpec((None, None, rs, gw), lambda gi, si: (si, gi, 0, 0)), 1)


def _rms_fwd(name, h, g, tm):
    lp, dm = h.shape

    def body(h_ref, g_ref, o_ref):
        hv = h_ref[...]
        r = lax.rsqrt(jnp.mean(hv * hv, axis=-1, keepdims=True) + EPS)
        o_ref[...] = (hv * r * g_ref[...]).astype(BF16)

    row = pl.BlockSpec((tm, dm), lambda i: (i, 0))
    return pl.pallas_call(
        body, name=name, out_shape=jax.ShapeDtypeStruct((lp, dm), BF16), grid=(lp // tm,),
        in_specs=[row, pl.BlockSpec((1, dm), lambda i: (0, 0))], out_specs=row, compiler_params=_params(1),
    )(h, g)


def _rms_bwd(name, dy, h, g, dres, tm, dep):
    lp, dm = h.shape

    def body(dy_ref, h_ref, g_ref, dr_ref, _, dh_ref, dhb_ref, dg_ref):
        hv = h_ref[...]
        r = lax.rsqrt(jnp.mean(hv * hv, axis=-1, keepdims=True) + EPS)
        xhat = hv * r
        dyv = dy_ref[...]
        dxh = dyv * g_ref[...]
        dh = dr_ref[...] + r * (dxh - xhat * jnp.mean(dxh * xhat, axis=-1, keepdims=True))
        dh_ref[...] = dh
        dhb_ref[...] = dh.astype(BF16)

        @pl.when(pl.program_id(0) == 0)
        def _():
            dg_ref[...] = jnp.zeros_like(dg_ref)

        dg_ref[0:1, :] += jnp.sum(dyv * xhat, axis=0, keepdims=True)

    row = pl.BlockSpec((tm, dm), lambda i: (i, 0))
    slab = pl.BlockSpec((SMALL_ROWS, dm), lambda i: (0, 0))
    return pl.pallas_call(
        body, name=name, grid=(lp // tm,),
        out_shape=(jax.ShapeDtypeStruct((lp, dm), F32), jax.ShapeDtypeStruct((lp, dm), BF16),
                   jax.ShapeDtypeStruct((SMALL_ROWS, dm), F32)),
        in_specs=[row, row, pl.BlockSpec((1, dm), lambda i: (0, 0)), row, ANY], out_specs=(row, row, slab),
        compiler_params=_params(1),
    )(dy, h, g, dres, dep)


def _gate_mix(name, proj, b_gate2, ya, pool_scale, yb, tm):
    _, lp, dm = proj.shape

    def body(ga_ref, gr_ref, b_ref, ya_ref, ps_ref, yb_ref, o_ref):
        g_a = jax.nn.sigmoid(ga_ref[...].astype(F32) + b_ref[0:1, :])
        g_b = jax.nn.sigmoid(gr_ref[...].astype(F32) + b_ref[1:2, :])
        y_a = ya_ref[...].astype(F32) * ps_ref[...]
        o_ref[...] = (g_a * y_a + g_b * yb_ref[...].astype(F32)).astype(BF16)

    row = pl.BlockSpec((tm, dm), lambda i: (i, 0))
    return pl.pallas_call(
        body, name=name, out_shape=jax.ShapeDtypeStruct((lp, dm), BF16), grid=(lp // tm,),
        in_specs=[pl.BlockSpec((None, tm, dm), lambda i: (4, i, 0)), pl.BlockSpec((None, tm, dm), lambda i: (5, i, 0)),
                  pl.BlockSpec((2, dm), lambda i: (0, 0)), row, pl.BlockSpec((1, dm), lambda i: (0, 0)), row],
        out_specs=row, compiler_params=_params(1),
    )(proj, proj, b_gate2, ya, pool_scale, yb)


def _gate_bwd(name, dmix, proj, b_gate2, ya, pool_scale, yb, tm):
    _, lp, dm = proj.shape

    def body(dm_ref, ga_ref, gr_ref, b_ref, ya_ref, ps_ref, yb_ref, dp_ref, dyb_ref, dya_ref, db_ref, dps_ref):
        dmx = dm_ref[...].astype(F32)
        g_a = jax.nn.sigmoid(ga_ref[...].astype(F32) + b_ref[0:1, :])
        g_b = jax.nn.sigmoid(gr_ref[...].astype(F32) + b_ref[1:2, :])
        ya_pre = ya_ref[...].astype(F32)
        ybv = yb_ref[...].astype(F32)
        ps = ps_ref[...]
        dga = dmx * (ya_pre * ps) * (g_a * (1.0 - g_a))
        dgr = dmx * ybv * (g_b * (1.0 - g_b))
        dp_ref[0] = dga.astype(BF16)
        dp_ref[1] = dgr.astype(BF16)
        dyb_ref[...] = (dmx * g_b).astype(BF16)
        dya_ref[...] = (dmx * g_a * ps).astype(BF16)

        @pl.when(pl.program_id(0) == 0)
        def _():
            db_ref[...] = jnp.zeros_like(db_ref)
            dps_ref[...] = jnp.zeros_like(dps_ref)

        db_ref[0:1, :] += jnp.sum(dga, axis=0, keepdims=True)
        db_ref[1:2, :] += jnp.sum(dgr, axis=0, keepdims=True)
        dps_ref[0:1, :] += jnp.sum(dmx * g_a * ya_pre, axis=0, keepdims=True)

    row = pl.BlockSpec((tm, dm), lambda i: (i, 0))
    one = pl.BlockSpec((1, dm), lambda i: (0, 0))
    slab = pl.BlockSpec((SMALL_ROWS, dm), lambda i: (0, 0))
    return pl.pallas_call(
        body, name=name, grid=(lp // tm,),
        out_shape=(jax.ShapeDtypeStruct((6, lp, dm), BF16), jax.ShapeDtypeStruct((lp, dm), BF16),
                   jax.ShapeDtypeStruct((lp, dm), BF16), jax.ShapeDtypeStruct((SMALL_ROWS, dm), F32),
                   jax.ShapeDtypeStruct((SMALL_ROWS, dm), F32)),
        in_specs=[row, pl.BlockSpec((None, tm, dm), lambda i: (4, i, 0)), pl.BlockSpec((None, tm, dm), lambda i: (5, i, 0)),
                  pl.BlockSpec((2, dm), lambda i: (0, 0)), row, one, row],
        out_specs=(pl.BlockSpec((2, tm, dm), lambda i: (2, i, 0)), row, row, slab, slab),
        compiler_params=_params(1),
    )(dmix, proj, proj, b_gate2, ya, pool_scale, yb)


def _swiglu_fwd(name, gu, tm):
    _, lp, f = gu.shape

    def body(g_ref, u_ref, o_ref):
        gt = g_ref[...].astype(F32)
        o_ref[...] = (gt * jax.nn.sigmoid(gt) * u_ref[...].astype(F32)).astype(BF16)

    return pl.pallas_call(
        body, name=name, out_shape=jax.ShapeDtypeStruct((lp, f), BF16), grid=(lp // tm,),
        in_specs=[pl.BlockSpec((None, tm, f), lambda i: (0, i, 0)), pl.BlockSpec((None, tm, f), lambda i: (1, i, 0))],
        out_specs=pl.BlockSpec((tm, f), lambda i: (i, 0)), compiler_params=_params(1),
    )(gu, gu)


def _swiglu_bwd(name, dact, gu, tm):
    _, lp, f = gu.shape

    def body(d_ref, g_ref, u_ref, o_ref):
        d = d_ref[...].astype(F32)
        gt = g_ref[...].astype(F32)
        sg = jax.nn.sigmoid(gt)
        o_ref[0] = (d * u_ref[...].astype(F32) * (sg * (1.0 + gt * (1.0 - sg)))).astype(BF16)
        o_ref[1] = (d * (gt * sg)).astype(BF16)

    return pl.pallas_call(
        body, name=name, out_shape=jax.ShapeDtypeStruct((2, lp, f), BF16), grid=(lp // tm,),
        in_specs=[pl.BlockSpec((tm, f), lambda i: (i, 0)), pl.BlockSpec((None, tm, f), lambda i: (0, i, 0)),
                  pl.BlockSpec((None, tm, f), lambda i: (1, i, 0))],
        out_specs=pl.BlockSpec((2, tm, f), lambda i: (0, i, 0)), compiler_params=_params(1),
    )(dact, gu, gu)


def _final_loss(name, h2, g3, target, tm):
    lp, dm = h2.shape
    nx = target.shape[0] // tm

    def body(h_ref, g_ref, t_ref, dh_ref, dhb_ref, ls_ref, dg_ref):
        i = pl.program_id(0)

        @pl.when(i == 0)
        def _():
            ls_ref[...] = jnp.zeros_like(ls_ref)
            dg_ref[...] = jnp.zeros_like(dg_ref)

        @pl.when(i < nx)
        def _():
            hv = h_ref[...]
            gv = g_ref[...]
            r = lax.rsqrt(jnp.mean(hv * hv, axis=-1, keepdims=True) + EPS)
            xhat = hv * r
            err = xhat * gv - t_ref[...]
            dout = err * (1.0 / dm)
            dxh = dout * gv
            dh = r * (dxh - xhat * jnp.mean(dxh * xhat, axis=-1, keepdims=True))
            dh_ref[...] = dh
            dhb_ref[...] = dh.astype(BF16)
            ls_ref[0:1, :] += jnp.sum(err * err, axis=0, keepdims=True)
            dg_ref[0:1, :] += jnp.sum(dout * xhat, axis=0, keepdims=True)

        @pl.when(i >= nx)
        def _():
            dh_ref[...] = jnp.zeros_like(dh_ref)
            dhb_ref[...] = jnp.zeros_like(dhb_ref)

    row = pl.BlockSpec((tm, dm), lambda i: (i, 0))
    slab = pl.BlockSpec((SMALL_ROWS, dm), lambda i: (0, 0))
    return pl.pallas_call(
        body, name=name, grid=(lp // tm,),
        out_shape=(jax.ShapeDtypeStruct((lp, dm), F32), jax.ShapeDtypeStruct((lp, dm), BF16),
                   jax.ShapeDtypeStruct((SMALL_ROWS, dm), F32), jax.ShapeDtypeStruct((SMALL_ROWS, dm), F32)),
        in_specs=[row, pl.BlockSpec((1, dm), lambda i: (0, 0)), pl.BlockSpec((tm, dm), lambda i: (jnp.minimum(i, nx - 1), 0))],
        out_specs=(row, row, slab, slab), compiler_params=_params(1),
    )(h2, g3, target)


def _shift(v, k):
    return pltpu.roll(v, k % v.shape[0], axis=0)


def _window_sum(v, group, sign):
    s2 = v + _shift(v, sign * 1)
    s4 = s2 + _shift(s2, sign * 2)
    s8 = s4 + _shift(s4, sign * 4)
    s16 = s8 + _shift(s8, sign * 8)
    return jnp.where(group == 0, s2, jnp.where(group == 1, s4, jnp.where(group == 2, s8, s16)))


def _pool_count(lp, group):
    row = lax.broadcasted_iota(jnp.int32, (lp, 1), 0)
    window = jnp.left_shift(2, group).astype(F32)
    meta_pos = (row - (lp - N_META) + 1).astype(F32)
    return jnp.where(row >= lp - N_META, jnp.minimum(meta_pos, window), window)


def _mixer_fwd(name, proj, conv_w, tc):
    _, lp, dm = proj.shape
    per_group = dm // len(POOL_WINDOWS) // tc

    def body(u_ref, gb_ref, gc_ref, v_ref, cw_ref, p_ref, z_ref):
        group = pl.program_id(0) // per_group
        u = u_ref[...].astype(F32)
        p_ref[...] = (_window_sum(u, group, 1) / _pool_count(lp, group) - u).astype(BF16)
        cv = gc_ref[...].astype(F32) * v_ref[...].astype(F32)
        conv = cw_ref[0:1, :] * _shift(cv, 2) + cw_ref[1:2, :] * _shift(cv, 1) + cw_ref[2:3, :] * cv
        z_ref[...] = (gb_ref[...].astype(F32) * conv).astype(BF16)

    def seg(s):
        return pl.BlockSpec((None, lp, tc), lambda j: (s, 0, j))

    col = pl.BlockSpec((lp, tc), lambda j: (0, j))
    return pl.pallas_call(
        body, name=name, grid=(dm // tc,),
        out_shape=(jax.ShapeDtypeStruct((lp, dm), BF16), jax.ShapeDtypeStruct((lp, dm), BF16)),
        in_specs=[seg(0), seg(1), seg(2), seg(3), pl.BlockSpec((3, tc), lambda j: (0, j))],
        out_specs=(col, col), compiler_params=_params(1),
    )(proj, proj, proj, proj, conv_w)


def _mixer_bwd(name, dz, dpooled, proj, conv_w, dproj, tc, dep):
    _, lp, dm = proj.shape
    per_group = dm // len(POOL_WINDOWS) // tc

    def body(dz_ref, dp_ref, gb_ref, gc_ref, v_ref, cw_ref, _, __, o_ref, dcw_ref):
        group = pl.program_id(0) // per_group
        dzv = dz_ref[...].astype(F32)
        gb = gb_ref[...].astype(F32)
        gc = gc_ref[...].astype(F32)
        vv = v_ref[...].astype(F32)
        cv = gc * vv
        c1 = _shift(cv, 1)
        c2 = _shift(cv, 2)
        w0, w1, w2 = cw_ref[0:1, :], cw_ref[1:2, :], cw_ref[2:3, :]
        o_ref[1] = (dzv * (w0 * c2 + w1 * c1 + w2 * cv)).astype(BF16)
        dconv = dzv * gb
        dcw_ref[...] = jnp.zeros_like(dcw_ref)
        dcw_ref[0:1, :] = jnp.sum(dconv * c2, axis=0, keepdims=True)
        dcw_ref[1:2, :] = jnp.sum(dconv * c1, axis=0, keepdims=True)
        dcw_ref[2:3, :] = jnp.sum(dconv * cv, axis=0, keepdims=True)
        dcv = w0 * _shift(dconv, -2) + w1 * _shift(dconv, -1) + w2 * dconv
        o_ref[2] = (dcv * vv).astype(BF16)
        o_ref[3] = (dcv * gc).astype(BF16)
        dpv = dp_ref[...].astype(F32)
        o_ref[0] = (_window_sum(dpv / _pool_count(lp, group), group, -1) - dpv).astype(BF16)

    def seg(s):
        return pl.BlockSpec((None, lp, tc), lambda j: (s, 0, j))

    col = pl.BlockSpec((lp, tc), lambda j: (0, j))
    return pl.pallas_call(
        body, name=name, grid=(dm // tc,),
        out_shape=(jax.ShapeDtypeStruct(dproj.shape, BF16), jax.ShapeDtypeStruct((SMALL_ROWS, dm), F32)),
        in_specs=[col, col, seg(1), seg(2), seg(3), pl.BlockSpec((3, tc), lambda j: (0, j)), ANY, ANY],
        out_specs=(pl.BlockSpec((4, lp, tc), lambda j: (0, 0, j)), pl.BlockSpec((SMALL_ROWS, tc), lambda j: (0, j))),
        input_output_aliases={6: 0}, compiler_params=_params(1),
    )(dz, dpooled, proj, proj, proj, conv_w, dproj, dep)


def _row_tile(r, c, bytes_per_row_elem=4, budget=2 * 1024 * 1024):
    best = None
    for t in range(16, r + 1, 16):
        if r % t == 0 and t * c * bytes_per_row_elem <= budget:
            best = t
    return best if best is not None else r


def _pair_add(name, g4, recv, core):
    s, r, c = g4.shape
    h = r // 2
    tr = _row_tile(h, c)
    nb = h // tr

    def body(core_ref, g_ref, r_ref, o_ref):
        o_ref[...] = (g_ref[...].astype(F32) + r_ref[...].astype(F32)).astype(BF16)

    grid_spec = pltpu.PrefetchScalarGridSpec(
        num_scalar_prefetch=1, grid=(s, nb),
        in_specs=[pl.BlockSpec((None, tr, c), lambda si, j, core_ref: (si, core_ref[0] * nb + j, 0)),
                  pl.BlockSpec((None, tr, c), lambda si, j, core_ref: (si, j, 0))],
        out_specs=pl.BlockSpec((None, tr, c), lambda si, j, core_ref: (si, j, 0)))
    return pl.pallas_call(
        body, name=name, out_shape=jax.ShapeDtypeStruct((s, h, c), BF16), grid_spec=grid_spec,
        compiler_params=_params(2),
    )(core, g4, recv)


def _chip_sum(name, parts, recv, chip):
    _, h, c = parts.shape
    tr = _row_tile(h, c)

    def body(chip_ref, p_ref, r_ref, o_ref):
        acc = p_ref[...].astype(F32)
        for i in range(len(CHIP_FLIPS)):
            acc = acc + r_ref[i].astype(F32)
        o_ref[...] = acc

    grid_spec = pltpu.PrefetchScalarGridSpec(
        num_scalar_prefetch=1, grid=(h // tr,),
        in_specs=[pl.BlockSpec((None, tr, c), lambda j, chip_ref: (chip_ref[0], j, 0)),
                  pl.BlockSpec((len(CHIP_FLIPS), tr, c), lambda j, chip_ref: (0, j, 0))],
        out_specs=pl.BlockSpec((tr, c), lambda j, chip_ref: (j, 0)))
    return pl.pallas_call(
        body, name=name, out_shape=jax.ShapeDtypeStruct((h, c), F32), grid_spec=grid_spec, compiler_params=_params(1),
    )(chip, parts, recv)


def _adam_update(w, gv, m, v):
    c1 = 1.0 - ADAM_B1 ** ADAM_STEP
    c2 = 1.0 - ADAM_B2 ** ADAM_STEP
    nm = ADAM_B1 * m + (1.0 - ADAM_B1) * gv
    nv = ADAM_B2 * v + (1.0 - ADAM_B2) * (gv * gv)
    return -ADAM_LR * ((nm / c1) / (jnp.sqrt(nv / c2) + ADAM_EPS) + ADAM_WD * w), nm, nv


def _adamw_halves(name, w, g_own, g_sib, m, v, core):
    r, c = w.shape
    h = r // 2
    tr = _row_tile(h, c, budget=1024 * 1024)
    nbh = h // tr

    def body(core_ref, w_ref, go_ref, gs_ref, m_ref, v_ref, g_ref, d_ref, nm_ref, nv_ref):
        mine = (pl.program_id(0) // nbh) == core_ref[0]
        gv = jnp.where(mine, go_ref[...], gs_ref[...])
        g_ref[...] = gv
        d_ref[...], nm_ref[...], nv_ref[...] = _adam_update(w_ref[...], gv, m_ref[...], v_ref[...])

    def blk(fn):
        return pl.BlockSpec((tr, c), fn)

    full = blk(lambda j, core_ref: (j, 0))
    own = blk(lambda j, core_ref: (jnp.clip(j - core_ref[0] * nbh, 0, nbh - 1), 0))
    sib = blk(lambda j, core_ref: (jnp.clip(j - (1 - core_ref[0]) * nbh, 0, nbh - 1), 0))
    grid_spec = pltpu.PrefetchScalarGridSpec(
        num_scalar_prefetch=1, grid=(r // tr,), in_specs=[full, own, sib, full, full], out_specs=(full,) * 4)
    sds = jax.ShapeDtypeStruct((r, c), F32)
    return pl.pallas_call(
        body, name=name, out_shape=(sds,) * 4, grid_spec=grid_spec, compiler_params=_params(1),
    )(core, w, g_own, g_sib, m, v)


def _adamw(name, w, g, m, v):
    r, c = w.shape

    def body(w_ref, g_ref, m_ref, v_ref, d_ref, nm_ref, nv_ref):
        d_ref[...], nm_ref[...], nv_ref[...] = _adam_update(w_ref[...], g_ref[...], m_ref[...], v_ref[...])

    blk = pl.BlockSpec((r, c), lambda j: (0, 0))
    sds = jax.ShapeDtypeStruct((r, c), F32)
    return pl.pallas_call(
        body, name=name, out_shape=(sds, sds, sds), grid=(1,), in_specs=[blk] * 4, out_specs=(blk,) * 3,
        compiler_params=_params(1),
    )(w, g, m, v)


def _cast_into_slot(name, w, chip, dtype, deps=()):
    r, c = w.shape
    tr = _row_tile(r, c)

    def body(chip_ref, w_ref, *rest):
        rest[-1][...] = w_ref[...].astype(dtype)

    grid_spec = pltpu.PrefetchScalarGridSpec(
        num_scalar_prefetch=1, grid=(r // tr,),
        in_specs=[pl.BlockSpec((tr, c), lambda j, chip_ref: (j, 0))] + [ANY] * len(deps),
        out_specs=pl.BlockSpec((None, tr, c), lambda j, chip_ref: (chip_ref[0], j, 0)))
    return pl.pallas_call(
        body, name=name, out_shape=jax.ShapeDtypeStruct((4, r, c), dtype), grid_spec=grid_spec, compiler_params=_params(1),
    )(chip, w, *deps)


def _place():
    return lax.axis_index("x"), lax.axis_index("y"), lax.axis_index("c")


def _chip_of(x, y, flip):
    px, py = x ^ flip[0], y ^ flip[1]
    return px, py, 2 * px + py


def _half(ref, which):
    rows = ref.shape[0] // 2
    return ref.at[pl.ds(which * rows, rows)]


HBM = pl.BlockSpec(memory_space=pltpu.HBM)
SEM = pl.BlockSpec(memory_space=pltpu.SEMAPHORE)
SPLIT_COPY = pltpu.CompilerParams(has_side_effects=pltpu.SideEffectType.DATAFLOW_SIDE_EFFECTING)


def _in_hbm(arrays):
    return [pltpu.with_memory_space_constraint(t, pltpu.HBM) for t in arrays]


TOKEN = jax.ShapeDtypeStruct((SMALL_ROWS, LANES), F32)
TOKEN_SPEC = pl.BlockSpec(memory_space=pltpu.VMEM)


def _gather_start(name, slabs, groups):
    n = len(slabs)
    ng = len(groups)
    nf = len(CHIP_FLIPS)

    def body(*refs):
        sems, outs = refs[n:n + 2 * ng], refs[n + 2 * ng:2 * n + 2 * ng]
        token = refs[2 * n + 2 * ng]
        token[...] = jnp.zeros_like(token)
        x, y, c = _place()
        k = 2 * x + y
        for g, members in enumerate(groups):
            for i, a in enumerate(members):
                for j, flip in enumerate(CHIP_FLIPS):
                    px, py, _ = _chip_of(x, y, flip)
                    mine = _half(outs[a].at[k], c)
                    pltpu.make_async_remote_copy(
                        src_ref=mine, dst_ref=mine, send_sem=sems[2 * g].at[i * nf + j], recv_sem=sems[2 * g + 1].at[i * nf + j],
                        device_id=(px, py, c), device_id_type=MESH).start()

    sem_shapes = []
    for members in groups:
        sem_shapes += [pltpu.SemaphoreType.DMA((nf * len(members),))] * 2
    res = pl.pallas_call(
        body, name=name,
        out_shape=tuple(sem_shapes) + tuple(pltpu.HBM(t.shape, t.dtype) for t in slabs) + (TOKEN,),
        in_specs=[HBM] * n, out_specs=tuple([SEM] * (2 * ng) + [HBM] * n + [TOKEN_SPEC]),
        input_output_aliases={a: 2 * ng + a for a in range(n)}, compiler_params=SPLIT_COPY,
    )(*_in_hbm(slabs))
    return [(res[2 * g], res[2 * g + 1]) for g in range(ng)], list(res[2 * ng:2 * ng + n]), res[2 * ng + n]


def _gather_wait(name, slabs, sems, after):
    n = len(slabs)
    nf = len(CHIP_FLIPS)

    def body(*refs):
        ins = refs[:n]
        ssem, rsem = refs[n], refs[n + 1]
        x, y, c = _place()
        k = 2 * x + y
        for a in range(n):
            for j, flip in enumerate(CHIP_FLIPS):
                _, _, kj = _chip_of(x, y, flip)
                cp = pltpu.make_async_remote_copy(
                    src_ref=_half(ins[a].at[k], c), dst_ref=_half(ins[a].at[kj], c), send_sem=ssem.at[a * nf + j],
                    recv_sem=rsem.at[a * nf + j], device_id=(x, y, c), device_id_type=MESH)
                cp.wait_send()
                cp.wait_recv()

    return pl.pallas_call(
        body, name=name, out_shape=tuple(pltpu.HBM(t.shape, t.dtype) for t in slabs),
        in_specs=[HBM] * n + [SEM, SEM, ANY], out_specs=tuple([HBM] * n),
        input_output_aliases={a: a for a in range(n)}, compiler_params=SPLIT_COPY,
    )(*slabs, sems[0], sems[1], after)


def _gather_pass(name, slabs, sems, after):
    n = len(slabs)
    nf = len(CHIP_FLIPS)

    def body(*refs):
        ins = refs[:n]
        ssem, rsem = refs[n], refs[n + 1]
        ssem2, rsem2 = refs[n + 3], refs[n + 4]
        x, y, c = _place()
        k = 2 * x + y
        for a in range(n):
            for j, flip in enumerate(CHIP_FLIPS):
                _, _, kj = _chip_of(x, y, flip)
                landed = _half(ins[a].at[kj], c)
                cp = pltpu.make_async_remote_copy(
                    src_ref=_half(ins[a].at[k], c), dst_ref=landed, send_sem=ssem.at[a * nf + j],
                    recv_sem=rsem.at[a * nf + j], device_id=(x, y, c), device_id_type=MESH)
                cp.wait_send()
                cp.wait_recv()
                pltpu.make_async_remote_copy(
                    src_ref=landed, dst_ref=landed, send_sem=ssem2.at[a * nf + j], recv_sem=rsem2.at[a * nf + j],
                    device_id=(x, y, 1 - c), device_id_type=MESH).start()

    sem = pltpu.SemaphoreType.DMA((nf * n,))
    res = pl.pallas_call(
        body, name=name, out_shape=(sem, sem) + tuple(pltpu.HBM(t.shape, t.dtype) for t in slabs),
        in_specs=[HBM] * n + [SEM, SEM, ANY], out_specs=tuple([SEM, SEM] + [HBM] * n),
        input_output_aliases={a: 2 + a for a in range(n)}, compiler_params=SPLIT_COPY,
    )(*slabs, sems[0], sems[1], after)
    return (res[0], res[1]), list(res[2:])


def _pass_wait(name, slabs, sems, after):
    n = len(slabs)
    nf = len(CHIP_FLIPS)

    def body(*refs):
        ins = refs[:n]
        ssem, rsem = refs[n], refs[n + 1]
        x, y, c = _place()
        for a in range(n):
            for j, flip in enumerate(CHIP_FLIPS):
                _, _, kj = _chip_of(x, y, flip)
                cp = pltpu.make_async_remote_copy(
                    src_ref=_half(ins[a].at[kj], c), dst_ref=_half(ins[a].at[kj], 1 - c), send_sem=ssem.at[a * nf + j],
                    recv_sem=rsem.at[a * nf + j], device_id=(x, y, c), device_id_type=MESH)
                cp.wait_send()
                cp.wait_recv()

    return pl.pallas_call(
        body, name=name, out_shape=tuple(pltpu.HBM(t.shape, t.dtype) for t in slabs),
        in_specs=[HBM] * n + [SEM, SEM, ANY], out_specs=tuple([HBM] * n),
        input_output_aliases={a: a for a in range(n)}, compiler_params=SPLIT_COPY,
    )(*slabs, sems[0], sems[1], after)


def _swap_start(name, grads):
    n = len(grads)

    def body(*refs):
        ssem, rsem = refs[2 * n], refs[2 * n + 1]
        src, land = refs[2 * n + 2:3 * n + 2], refs[3 * n + 2:4 * n + 2]
        token = refs[4 * n + 2]
        token[...] = jnp.zeros_like(token)
        x, y, c = _place()
        for a in range(n):
            h = src[a].shape[1] // 2
            pltpu.make_async_remote_copy(
                src_ref=src[a].at[:, pl.ds((1 - c) * h, h)], dst_ref=land[a], send_sem=ssem.at[a], recv_sem=rsem.at[a],
                device_id=(x, y, 1 - c), device_id_type=MESH).start()

    zones = [lax.empty((g.shape[0], g.shape[1] // 2, g.shape[2]), g.dtype) for g in grads]
    sem = pltpu.SemaphoreType.DMA((n,))
    res = pl.pallas_call(
        body, name=name,
        out_shape=(sem, sem) + tuple(pltpu.HBM(t.shape, t.dtype) for t in list(grads) + zones) + (TOKEN,),
        in_specs=[HBM] * (2 * n), out_specs=tuple([SEM, SEM] + [HBM] * (2 * n) + [TOKEN_SPEC]),
        input_output_aliases={i: 2 + i for i in range(2 * n)}, compiler_params=SPLIT_COPY,
    )(*_in_hbm(list(grads) + zones))
    return (res[0], res[1], list(res[2:2 + n]), list(res[2 + n:2 + 2 * n])), res[2 + 2 * n]


def _swap_wait(name, ssem, rsem, grads, zones, after):
    n = len(grads)

    def body(*refs):
        src, land = refs[:n], refs[n:2 * n]
        ss, rs = refs[2 * n], refs[2 * n + 1]
        x, y, c = _place()
        for a in range(n):
            h = src[a].shape[1] // 2
            cp = pltpu.make_async_remote_copy(
                src_ref=src[a].at[:, pl.ds((1 - c) * h, h)], dst_ref=land[a], send_sem=ss.at[a], recv_sem=rs.at[a],
                device_id=(x, y, c), device_id_type=MESH)
            cp.wait_send()
            cp.wait_recv()

    res = pl.pallas_call(
        body, name=name, out_shape=tuple(pltpu.HBM(t.shape, t.dtype) for t in list(grads) + list(zones)),
        in_specs=[HBM] * (2 * n) + [SEM, SEM, ANY], out_specs=tuple([HBM] * (2 * n)),
        input_output_aliases={i: i for i in range(2 * n)}, compiler_params=SPLIT_COPY,
    )(*grads, *zones, ssem, rsem, after)
    return list(res[:n]), list(res[n:])


def _sibling_exchange(name, slabs):
    n = len(slabs)
    nf = len(CHIP_FLIPS)

    def body(*refs):
        outs = refs[n:2 * n]
        ssem, rsem = refs[2 * n:]
        x, y, c = _place()

        def copy(a, j, which, to):
            _, _, kj = _chip_of(x, y, CHIP_FLIPS[j])
            ref = _half(outs[a].at[kj], which)
            return pltpu.make_async_remote_copy(src_ref=ref, dst_ref=ref, send_sem=ssem.at[a * nf + j],
                                                recv_sem=rsem.at[a * nf + j], device_id=to, device_id_type=MESH)

        sends = [copy(a, j, c, (x, y, 1 - c)) for a in range(n) for j in range(nf)]
        for cp in sends:
            cp.start()
        for a in range(n):
            for j in range(nf):
                copy(a, j, 1 - c, (x, y, c)).wait_recv()
        for cp in sends:
            cp.wait_send()

    return pl.pallas_call(
        body, name=name, out_shape=tuple(jax.ShapeDtypeStruct(t.shape, t.dtype) for t in slabs),
        in_specs=[ANY] * n, out_specs=(ANY,) * n, input_output_aliases={a: a for a in range(n)},
        scratch_shapes=[pltpu.SemaphoreType.DMA((nf * n,)), pltpu.SemaphoreType.DMA((nf * n,))],
    )(*slabs)


def _sibling_swap(name, grads):
    n = len(grads)

    def body(*refs):
        ins, outs = refs[:n], refs[n:2 * n]
        ssem, rsem = refs[2 * n:]
        x, y, c = _place()
        cps = []
        for a in range(n):
            h = ins[a].shape[1] // 2
            cps.append(pltpu.make_async_remote_copy(
                src_ref=ins[a].at[:, pl.ds((1 - c) * h, h)], dst_ref=outs[a], send_sem=ssem.at[a], recv_sem=rsem.at[a],
                device_id=(x, y, 1 - c), device_id_type=MESH))
        for cp in cps:
            cp.start()
        for cp in cps:
            cp.wait()

    return pl.pallas_call(
        body, name=name,
        out_shape=tuple(jax.ShapeDtypeStruct((g.shape[0], g.shape[1] // 2, g.shape[2]), g.dtype) for g in grads),
        in_specs=[ANY] * n, out_specs=(ANY,) * n,
        scratch_shapes=[pltpu.SemaphoreType.DMA((n,)), pltpu.SemaphoreType.DMA((n,))],
    )(*grads)


def _scatter_start(name, parts):
    n = len(parts)
    nf = len(CHIP_FLIPS)

    def body(*refs):
        ssem, rsem = refs[2 * n], refs[2 * n + 1]
        src, land = refs[2 * n + 2:3 * n + 2], refs[3 * n + 2:4 * n + 2]
        token = refs[4 * n + 2]
        token[...] = jnp.zeros_like(token)
        x, y, c = _place()
        for a in range(n):
            for j, flip in enumerate(CHIP_FLIPS):
                px, py, kj = _chip_of(x, y, flip)
                pltpu.make_async_remote_copy(
                    src_ref=src[a].at[kj], dst_ref=land[a].at[j], send_sem=ssem.at[a * nf + j], recv_sem=rsem.at[a * nf + j],
                    device_id=(px, py, c), device_id_type=MESH).start()

    zones = [lax.empty((nf,) + p.shape[1:], p.dtype) for p in parts]
    sem = pltpu.SemaphoreType.DMA((nf * n,))
    res = pl.pallas_call(
        body, name=name,
        out_shape=(sem, sem) + tuple(pltpu.HBM(t.shape, t.dtype) for t in list(parts) + zones)
        + (jax.ShapeDtypeStruct((SMALL_ROWS, LANES), F32),),
        in_specs=[HBM] * (2 * n),
        out_specs=tuple([SEM, SEM] + [HBM] * (2 * n) + [pl.BlockSpec(memory_space=pltpu.VMEM)]),
        input_output_aliases={i: 2 + i for i in range(2 * n)}, compiler_params=SPLIT_COPY,
    )(*_in_hbm(list(parts) + zones))
    return (res[0], res[1], list(res[2:2 + n]), list(res[2 + n:2 + 2 * n])), res[2 + 2 * n]


def _scatter_wait(name, ssem, rsem, parts, zones, after):
    n = len(parts)
    nf = len(CHIP_FLIPS)

    def body(*refs):
        src, land = refs[:n], refs[n:2 * n]
        ss, rs = refs[2 * n], refs[2 * n + 1]
        x, y, c = _place()
        for a in range(n):
            for j, flip in enumerate(CHIP_FLIPS):
                _, _, kj = _chip_of(x, y, flip)
                cp = pltpu.make_async_remote_copy(
                    src_ref=src[a].at[kj], dst_ref=land[a].at[j], send_sem=ss.at[a * nf + j], recv_sem=rs.at[a * nf + j],
                    device_id=(x, y, c), device_id_type=MESH)
                cp.wait_send()
                cp.wait_recv()

    res = pl.pallas_call(
        body, name=name, out_shape=tuple(pltpu.HBM(t.shape, t.dtype) for t in list(parts) + list(zones)),
        in_specs=[HBM] * (2 * n) + [SEM, SEM] + [ANY] * len(after), out_specs=tuple([HBM] * (2 * n)),
        input_output_aliases={i: i for i in range(2 * n)}, compiler_params=SPLIT_COPY,
    )(*parts, *zones, ssem, rsem, *after)
    return list(res[:n]), list(res[n:])


def _sibling_send(name, halves):
    n = len(halves)

    def body(*refs):
        ins, outs = refs[:n], refs[n:2 * n]
        ssem, rsem = refs[2 * n:]
        x, y, c = _place()
        cps = [pltpu.make_async_remote_copy(src_ref=ins[a], dst_ref=outs[a], send_sem=ssem.at[a], recv_sem=rsem.at[a],
                                            device_id=(x, y, 1 - c), device_id_type=MESH) for a in range(n)]
        for cp in cps:
            cp.start()
        for cp in cps:
            cp.wait()

    return pl.pallas_call(
        body, name=name,
        out_shape=tuple(jax.ShapeDtypeStruct(h.shape, h.dtype) for h in halves),
        in_specs=[ANY] * n, out_specs=(ANY,) * n,
        scratch_shapes=[pltpu.SemaphoreType.DMA((n,)), pltpu.SemaphoreType.DMA((n,))],
    )(*halves)


def _small_all_reduce(vec, loss_row, loss_scale, after):
    r, dm = vec.shape

    def body(v_ref, _, o_ref, l_ref, buf, ssem, rsem):
        x, y, c = _place()
        me = 4 * x + 2 * y + c
        buf[me] = v_ref[...]
        cps = []
        for mask in range(1, 8):
            fx, fy, fc = (mask >> 2) & 1, (mask >> 1) & 1, mask & 1
            cps.append(pltpu.make_async_remote_copy(
                src_ref=v_ref, dst_ref=buf.at[me], send_sem=ssem.at[mask - 1], recv_sem=rsem.at[mask - 1],
                device_id=(x ^ fx, y ^ fy, c ^ fc), device_id_type=MESH))
        for cp in cps:
            cp.start()
        for mask in range(1, 8):
            fx, fy, fc = (mask >> 2) & 1, (mask >> 1) & 1, mask & 1
            frm = 4 * (x ^ fx) + 2 * (y ^ fy) + (c ^ fc)
            pltpu.make_async_remote_copy(
                src_ref=v_ref, dst_ref=buf.at[frm], send_sem=ssem.at[mask - 1], recv_sem=rsem.at[mask - 1],
                device_id=(x, y, c), device_id_type=MESH).wait_recv()
        for cp in cps:
            cp.wait_send()
        acc = buf[0]
        for i in range(1, 8):
            acc = acc + buf[i]
        o_ref[...] = acc
        l_ref[...] = jnp.sum(acc[loss_row:loss_row + SMALL_ROWS, :], axis=(0, 1), keepdims=True) * loss_scale

    vm = pl.BlockSpec(memory_space=pltpu.VMEM)
    return pl.pallas_call(
        body, name="small_all_reduce",
        out_shape=(jax.ShapeDtypeStruct((r, dm), F32), jax.ShapeDtypeStruct((1, 1), F32)),
        in_specs=[vm, ANY], out_specs=(vm, vm),
        scratch_shapes=[pltpu.VMEM((8, r, dm), F32), pltpu.SemaphoreType.DMA((7,)), pltpu.SemaphoreType.DMA((7,))],
    )(vec, after)


def kernel(x, meta_tokens, norm_mix_g, w_in, b_gate, pool_w, pool_scale, conv_w, conv_out_w, w_o, norm_ffn_g, w_gate_up, w_down, norm_final_g, loss_target, m_meta_tokens, m_norm_mix_g, m_w_in, m_b_gate, m_pool_w, m_pool_scale, m_conv_w, m_conv_out_w, m_w_o, m_norm_ffn_g, m_w_gate_up, m_w_down, m_norm_final_g, v_meta_tokens, v_norm_mix_g, v_w_in, v_b_gate, v_pool_w, v_pool_scale, v_conv_w, v_conv_out_w, v_w_o, v_norm_ffn_g, v_w_gate_up, v_w_down, v_norm_final_g):
    seq, dm = x.shape[1], x.shape[2]
    tail = LANES
    tm = tail
    lp = seq + tail
    n_chips = 4
    n_groups = len(POOL_WINDOWS)
    gw = dm // n_groups
    tc = min(256, gw)
    cx, cy, cc = _place()
    chip = 2 * cx + cy
    dloc = dm // n_chips

    pool2 = pool_w.reshape(n_groups * pool_w.shape[1], gw)
    big = {"w_in": w_in, "w_gate_up": w_gate_up, "pool_w": pool2, "conv_out_w": conv_out_w, "w_o": w_o, "w_down": w_down}
    chip1 = jnp.reshape(chip, (1,)).astype(jnp.int32)
    core = jnp.reshape(cc, (1,)).astype(jnp.int32)
    small_loc = jnp.concatenate([meta_tokens, jnp.pad(conv_w, ((0, 8 - conv_w.shape[0]), (0, 0))),
                                 jnp.zeros((8, dloc), F32)], axis=0)
    first = [_cast_into_slot("cast_w_in", w_in, chip1, BF16), _cast_into_slot("place_small", small_loc, chip1, F32)]
    (sems0,), (w_in4, small4), token0 = _gather_start("gather_start_first", first, ([0, 1],))
    rest_names = ["pool_w", "conv_out_w", "w_o", "w_gate_up", "w_down"]
    rest = [_cast_into_slot("cast_" + nme, big[nme], chip1, BF16, deps=(token0,)) for nme in rest_names]
    groups = ([0, 1, 2], [3], [4])
    sems, rest, _ = _gather_start("gather_start_rest", rest, groups)

    def passing(g, after):
        return _gather_pass("gather_pass_%d" % g, [rest[a] for a in groups[g]], sems[g], after)

    g1, g2, g3 = norm_mix_g.reshape(1, dm), norm_ffn_g.reshape(1, dm), norm_final_g.reshape(1, dm)
    b_gate2 = b_gate.reshape(2, dm)
    ps = pool_scale.reshape(1, dm)
    w_in4, small4 = _sibling_exchange("sibling_exchange_first", list(_gather_wait("gather_wait_first", [w_in4, small4], sems0, rest[0])))
    small_f = jnp.transpose(small4, (1, 0, 2)).reshape(small4.shape[1], dm)
    meta_f = small_f[:N_META]
    conv_w_f = small_f[N_META:N_META + 3]
    h0 = jnp.concatenate([x[0], jnp.zeros((tail - N_META, dm), F32), meta_f], axis=0)
    hn1 = _rms_fwd("rms_mix", h0, g1, tm)
    proj = _nn_sharded("proj", hn1, w_in4, 6)
    pass_sems, passed = passing(0, proj)
    pooled, z = _mixer_fwd("mixer_fwd", proj, conv_w_f, tc)
    pool4, conv_out4, w_o4 = _pass_wait("pass_wait_0", passed, pass_sems, pooled)
    pool4 = pool4.reshape(n_chips, n_groups, gw // n_chips, gw)
    conv_out_f = conv_out4.reshape(dm, dm)
    w_o_f = w_o4.reshape(dm, dm)
    ya = _pool_fwd("pool_proj", pooled, pool4)
    yb = _nn_plain("conv_out", z, conv_out_f, BF16)
    mix = _gate_mix("gate_mix", proj, b_gate2, ya, ps, yb, tm)
    pass_sems, passed = passing(1, mix)
    h1 = _nn_plain("attn_out", mix, w_o_f, F32, res=h0, tn_pref=256)
    hn2 = _rms_fwd("rms_ffn", h1, g2, tm)
    (w_gu4,) = _pass_wait("pass_wait_1", passed, pass_sems, hn2)
    gu, act = _gate_up_swiglu("gate_up", hn2, w_gu4)
    pass_sems, passed = passing(2, act)
    (w_down4,) = _pass_wait("pass_wait_2", passed, pass_sems, act)
    w_down_f = w_down4.reshape(-1, dm)
    h2 = _nn_plain("ffn_down", act, w_down_f, F32, res=h1, tn_pref=512, tk_pref=1536)
    dh2, dh2b, loss_cols, dg3 = _final_loss("final_loss", h2, g3, loss_target[0], tm)

    def scatter(tag, names_g, swap, after):
        grads_g, got = _swap_wait("swap_wait_" + tag, *swap, after)
        pairs = [_pair_add("pair_add_" + nme, g4, rv, core) for nme, g4, rv in zip(names_g, grads_g, got)]
        return _scatter_start("scatter_start_" + tag, pairs)

    dgu = _dact_swiglu_bwd("d_gate_up", dh2b, w_down_f, gu)
    gw_down = _tn_plain("dw_down", act, dh2b)
    gw_gu = _tn_sharded("dw_gate_up", hn2, dgu, n_chips)
    swap_a, token = _swap_start("swap_start_a", [gw_gu, gw_down.reshape(n_chips, -1, dm)])
    dhn2 = _nt_sharded("d_hn2", dgu, w_gu4, deps=(token,))
    flight_a, token = scatter("a", ["w_gate_up", "w_down"], swap_a, dhn2)
    dh1, dh1b, dg2 = _rms_bwd("rms_ffn_bwd", dhn2, h1, g2, dh2, tm, token)
    dmix = _nt_plain("d_mix", dh1b, w_o_f)
    gw_o = _tn_plain("dw_o", mix, dh1b)
    dproj, dyb, dya, db_gate, dps = _gate_bwd("gate_bwd", dmix, proj, b_gate2, ya, ps, yb, tm)
    gw_conv_out = _tn_plain("dw_conv_out", z, dyb)
    gw_pool = _pool_bwd_w("dw_pool", pooled, dya, n_chips)
    swap_b, token = _swap_start("swap_start_b", [gw_o.reshape(n_chips, dloc, dm), gw_conv_out.reshape(n_chips, dloc, dm),
                                                 gw_pool.reshape(n_chips, n_groups * (gw // n_chips), gw)])
    dpooled = _pool_bwd_act("d_pooled", dya, pool4, deps=(token,))
    dz = _nt_plain("d_z", dyb, conv_out_f)
    flight_b, token = scatter("b", ["w_o", "conv_out_w", "pool_w"], swap_b, dz)
    dproj, dconv_w = _mixer_bwd("mixer_bwd", dz, dpooled, proj, conv_w_f, dproj, tc, token)
    gw_in = _tn_sharded("dw_in", hn1, dproj, n_chips)
    swap_c, token = _swap_start("swap_start_c", [gw_in])
    dhn1 = _nt_sharded("d_hn1", dproj, w_in4, deps=(token,))
    flight_c, token = scatter("c", ["w_in"], swap_c, dhn1)
    dh0, _, dg1 = _rms_bwd("rms_mix_bwd", dhn1, h0, g1, dh1, tm, token)
    grad_x = dh0[:seq][None]
    dmeta = dh0[lp - N_META:]

    given = dict(meta_tokens=(meta_tokens, m_meta_tokens, v_meta_tokens), norm_mix_g=(norm_mix_g, m_norm_mix_g, v_norm_mix_g),
                 w_in=(w_in, m_w_in, v_w_in), b_gate=(b_gate, m_b_gate, v_b_gate), pool_w=(pool_w, m_pool_w, v_pool_w),
                 pool_scale=(pool_scale, m_pool_scale, v_pool_scale), conv_w=(conv_w, m_conv_w, v_conv_w),
                 conv_out_w=(conv_out_w, m_conv_out_w, v_conv_out_w), w_o=(w_o, m_w_o, v_w_o),
                 norm_ffn_g=(norm_ffn_g, m_norm_ffn_g, v_norm_ffn_g), w_gate_up=(w_gate_up, m_w_gate_up, v_w_gate_up),
                 w_down=(w_down, m_w_down, v_w_down), norm_final_g=(norm_final_g, m_norm_final_g, v_norm_final_g))
    order = list(given.keys())
    grad, delta, new_m, new_v = {}, {}, {}, {}
    vec = jnp.concatenate([dg1, dg2, dg3, db_gate, dps, loss_cols, dconv_w, dmeta], axis=0)
    loss_row = 5 * SMALL_ROWS
    after = [dh0]
    for tag, names_g, flight in (("a", ["w_gate_up", "w_down"], flight_a), ("b", ["w_o", "conv_out_w", "pool_w"], flight_b),
                                 ("c", ["w_in"], flight_c)):
        pairs, zones = _scatter_wait("scatter_wait_" + tag, *flight, after)
        after = []
        if tag == "c":
            red, loss11 = _small_all_reduce(vec, loss_row, 0.5 / dm, pairs[0])
        halves = [_chip_sum("chip_sum_" + nme, p, rv, chip1) for nme, p, rv in zip(names_g, pairs, zones)]
        sib_halves = _sibling_send("sibling_send_" + tag, halves)
        for nme, g_own, g_sib in zip(names_g, halves, sib_halves):
            w, m, v = given[nme]
            shape2 = (2 * g_own.shape[0], g_own.shape[1])
            res4 = _adamw_halves("adamw_" + nme, w.reshape(shape2), g_own, g_sib, m.reshape(shape2), v.reshape(shape2), core)
            grad[nme], delta[nme], new_m[nme], new_v[nme] = [t.reshape(w.shape) for t in res4]
            after.append(res4[1])
    loss = loss11[0, 0]
    col0 = chip * dloc
    g_small = {
        "norm_mix_g": red[0], "norm_ffn_g": red[SMALL_ROWS], "norm_final_g": red[2 * SMALL_ROWS],
        "b_gate": red[3 * SMALL_ROWS:3 * SMALL_ROWS + 2].reshape(-1), "pool_scale": red[4 * SMALL_ROWS],
        "conv_w": lax.dynamic_slice(red, (6 * SMALL_ROWS, col0), (3, dloc)),
        "meta_tokens": lax.dynamic_slice(red, (7 * SMALL_ROWS, col0), (N_META, dloc)),
    }

    vec_names = ["norm_mix_g", "norm_ffn_g", "norm_final_g", "pool_scale"]

    def slab_vec(pick):
        rows = [pick(nme).reshape(1, dm) for nme in vec_names] + [pick("b_gate").reshape(2, dm), jnp.zeros((2, dm), F32)]
        return jnp.concatenate(rows, axis=0)

    def slab_col(pick):
        return jnp.concatenate([pick("meta_tokens"), pick("conv_w"), jnp.zeros((5, dloc), F32)], axis=0)

    for slab, tag in ((slab_vec, "vec"), (slab_col, "col")):
        d, nm, nv = _adamw("adamw_small_" + tag, slab(lambda nme: given[nme][0]), slab(lambda nme: g_small[nme]),
                           slab(lambda nme: given[nme][1]), slab(lambda nme: given[nme][2]))
        for out, res in ((delta, d), (new_m, nm), (new_v, nv)):
            if tag == "vec":
                for i, nme in enumerate(vec_names):
                    out[nme] = res[i]
                out["b_gate"] = res[4:6].reshape(-1)
            else:
                out["meta_tokens"] = res[:N_META]
                out["conv_w"] = res[N_META:N_META + 3]
    grad.update(g_small)
    return (loss, grad_x, *[grad[nme] for nme in order], *[delta[nme] for nme in order],
            *[new_m[nme] for nme in order], *[new_v[nme] for nme in order])
```

```python
import functools
import math

import jax
import jax.numpy as jnp
from jax import lax
from jax.experimental import pallas as pl
from jax.experimental.pallas import tpu as pltpu

F32 = jnp.float32
BF16 = jnp.bfloat16
N_META = 16
POOL_WINDOWS = (2, 4, 8, 16)
EPS = 1e-6
ADAM_LR, ADAM_B1, ADAM_B2, ADAM_EPS, ADAM_WD, ADAM_STEP = 0.001, 0.9, 0.999, 1e-08, 0.01, 10
LANES = 128
V7X_VMEM_BYTES = 64 * 1024 * 1024
VMEM_LIMIT = V7X_VMEM_BYTES - 8 * 1024 * 1024
MESH = pl.DeviceIdType.MESH
ANY = pl.BlockSpec(memory_space=pl.ANY)
CHIP_FLIPS = ((1, 0), (0, 1), (1, 1))
SMALL_ROWS = 8


def _pick(n, pref):
    best = None
    for t in range(LANES, min(n, pref) + 1, LANES):
        if n % t == 0:
            best = t
    assert best is not None, (n, pref)
    return best


def _params(n_axes=0):
    sem = ("arbitrary",) * n_axes if n_axes else None
    return pltpu.CompilerParams(dimension_semantics=sem, vmem_limit_bytes=VMEM_LIMIT)


_DIMS = {
    "nn": (((1,), (0,)), ((), ())),
    "nt": (((1,), (1,)), ((), ())),
    "tn": (((0,), (0,)), ((), ())),
}


def _matmul(name, mode, a, b, out_sds, grid, a_spec, b_spec, o_spec, nk, res=None, res_spec=None, acc_shape=None, deps=()):
    out_dtype = out_sds.dtype
    in_place = nk > 1 and out_dtype == F32
    use_scratch = nk > 1 and not in_place
    rows = a_spec.block_shape[-2] if mode != "tn" else None
    chunk = _row_tile(rows, 1, 1, 1152) if rows is not None else None
    n_in = 2 + (res is not None) + len(deps)

    def body(*refs):
        a_ref, b_ref = refs[:2]
        r_ref = refs[2] if res is not None else None
        o_ref, *scr = refs[n_in:]
        k = pl.program_id(len(grid) - 1) if nk > 1 else None

        def emit(sl):
            if sl is None:
                part = lax.dot_general(a_ref[...], b_ref[...], _DIMS[mode], preferred_element_type=F32)
                idx = (slice(None), slice(None))
            else:
                part = lax.dot_general(a_ref[sl, :], b_ref[...], _DIMS[mode], preferred_element_type=F32)
                idx = (sl, slice(None))
            if nk == 1:
                if r_ref is not None:
                    part = part + r_ref[idx]
                o_ref[idx] = part.astype(out_dtype)
                return
            acc = scr[0] if use_scratch else o_ref

            @pl.when(k == 0)
            def _():
                first = part
                if r_ref is not None and in_place:
                    first = first + r_ref[idx]
                acc[idx] = first

            @pl.when(k > 0)
            def _():
                acc[idx] += part

            if use_scratch:

                @pl.when(k == nk - 1)
                def _():
                    o_ref[idx] = acc[idx].astype(out_dtype)

        if mode == "tn" or chunk == rows:
            emit(None)
        else:
            for m0 in range(0, rows, chunk):
                emit(pl.ds(m0, chunk))

    ins = [a, b] + ([res] if res is not None else []) + list(deps)
    in_specs = [a_spec, b_spec] + ([res_spec] if res is not None else []) + [ANY] * len(deps)
    scratch = [pltpu.VMEM(acc_shape, F32)] if use_scratch else []
    return pl.pallas_call(
        body, name=name, out_shape=out_sds, grid=grid, in_specs=in_specs, out_specs=o_spec,
        scratch_shapes=scratch, compiler_params=_params(len(grid)),
    )(*ins)


def _nn_sharded(name, a, w4, nseg):
    lp, kdim = a.shape
    s, _, nloc = w4.shape
    segw = s * nloc // nseg
    tn = _pick(math.gcd(nloc, segw), 1536)
    bw, bo = nloc // tn, segw // tn
    return _matmul(
        name, "nn", a, w4, jax.ShapeDtypeStruct((nseg, lp, segw), BF16), (s * bw,),
        pl.BlockSpec((lp, kdim), lambda j: (0, 0)),
        pl.BlockSpec((None, kdim, tn), lambda j: (j // bw, 0, j % bw)),
        pl.BlockSpec((None, lp, tn), lambda j: (j // bo, 0, j % bo)), 1)


def _nn_plain(name, a, w, out_dtype, res=None, tn_pref=512, tk_pref=2048):
    lp, kdim = a.shape
    n = w.shape[1]
    tn = _pick(n, tn_pref)
    tk = kdim if kdim <= tk_pref else _pick(kdim, tk_pref)
    nk = kdim // tk
    grid = (n // tn, nk) if nk > 1 else (n // tn,)
    if nk > 1:
        a_spec = pl.BlockSpec((lp, tk), lambda j, k: (0, k))
        w_spec = pl.BlockSpec((tk, tn), lambda j, k: (k, j))
        o_spec = pl.BlockSpec((lp, tn), lambda j, k: (0, j))
    else:
        a_spec = pl.BlockSpec((lp, tk), lambda j: (0, 0))
        w_spec = pl.BlockSpec((tk, tn), lambda j: (0, j))
        o_spec = pl.BlockSpec((lp, tn), lambda j: (0, j))
    return _matmul(name, "nn", a, w, jax.ShapeDtypeStruct((lp, n), out_dtype), grid, a_spec, w_spec, o_spec, nk,
                   res=res, res_spec=o_spec if res is not None else None, acc_shape=(lp, tn))


def _nt_plain(name, a, w, tn_pref=512):
    lp, kdim = a.shape
    n = w.shape[0]
    tn = _pick(n, tn_pref)
    return _matmul(
        name, "nt", a, w, jax.ShapeDtypeStruct((lp, n), BF16), (n // tn,),
        pl.BlockSpec((lp, kdim), lambda j: (0, 0)),
        pl.BlockSpec((tn, kdim), lambda j: (j, 0)),
        pl.BlockSpec((lp, tn), lambda j: (0, j)), 1)


def _nt_sharded(name, dseg, w4, to_pref=1024, deps=()):
    nseg, lp, segw = dseg.shape
    s, kdim, nloc = w4.shape
    tr = _pick(math.gcd(nloc, segw), 1536)
    ba, bw = segw // tr, nloc // tr
    nr = s * bw
    to = _pick(kdim, to_pref)
    return _matmul(
        name, "nt", dseg, w4, jax.ShapeDtypeStruct((lp, kdim), F32), (kdim // to, nr),
        pl.BlockSpec((None, lp, tr), lambda j, r: (r // ba, 0, r % ba)),
        pl.BlockSpec((None, to, tr), lambda j, r: (r // bw, j, r % bw)),
        pl.BlockSpec((lp, to), lambda j, r: (0, j)), nr, deps=deps)


def _tn_plain(name, a, d, tk_pref=1024):
    lp, kdim = a.shape
    n = d.shape[1]
    tk = _pick(kdim, tk_pref)
    return _matmul(
        name, "tn", a, d, jax.ShapeDtypeStruct((kdim, n), BF16), (kdim // tk,),
        pl.BlockSpec((lp, tk), lambda i: (0, i)),
        pl.BlockSpec((lp, n), lambda i: (0, 0)),
        pl.BlockSpec((tk, n), lambda i: (i, 0)), 1)


def _tn_sharded(name, a, dseg, s, tk_pref=1024):
    lp, kdim = a.shape
    nseg, _, segw = dseg.shape
    nloc = nseg * segw // s
    tn = _pick(math.gcd(nloc, segw), 1536)
    bd, bo = segw // tn, nloc // tn
    tk = _pick(kdim, tk_pref)

    def body(a_ref, d_ref, o_ref, at_ref):
        @pl.when(pl.program_id(1) == 0)
        def _():
            at_ref[...] = a_ref[...].T

        o_ref[...] = jnp.dot(at_ref[...], d_ref[...], preferred_element_type=F32).astype(BF16)

    return pl.pallas_call(
        body, name=name, out_shape=jax.ShapeDtypeStruct((s, kdim, nloc), BF16), grid=(kdim // tk, s * bo),
        in_specs=[pl.BlockSpec((lp, tk), lambda i, j: (0, i)),
                  pl.BlockSpec((None, lp, tn), lambda i, j: (j // bd, 0, j % bd))],
        out_specs=pl.BlockSpec((None, tk, tn), lambda i, j: (j // bo, i, j % bo)),
        scratch_shapes=[pltpu.VMEM((tk, lp), BF16)], compiler_params=_params(2),
    )(a, dseg)


def _silu_parts(gt):
    sg = jax.nn.sigmoid(gt)
    return gt * sg, sg * (1.0 + gt * (1.0 - sg))


def _gate_up_swiglu(name, a, w4, tn_pref=256):
    lp, kdim = a.shape
    s, _, nloc = w4.shape
    f = s * nloc // 2
    tn = _pick(nloc, tn_pref)
    bw = nloc // tn
    chunk = _row_tile(lp, 1, 1, 1152)

    def body(a_ref, wg_ref, wu_ref, gu_ref, act_ref):
        for m0 in range(0, lp, chunk):
            sl = pl.ds(m0, chunk)
            gt = jnp.dot(a_ref[sl, :], wg_ref[...], preferred_element_type=F32)
            up = jnp.dot(a_ref[sl, :], wu_ref[...], preferred_element_type=F32)
            gu_ref[0, sl, :] = gt.astype(BF16)
            gu_ref[1, sl, :] = up.astype(BF16)
            act_ref[sl, :] = (_silu_parts(gt)[0] * up).astype(BF16)

    return pl.pallas_call(
        body, name=name, grid=(f // tn,),
        out_shape=(jax.ShapeDtypeStruct((2, lp, f), BF16), jax.ShapeDtypeStruct((lp, f), BF16)),
        in_specs=[pl.BlockSpec((lp, kdim), lambda j: (0, 0)),
                  pl.BlockSpec((None, kdim, tn), lambda j: (j // bw, 0, j % bw)),
                  pl.BlockSpec((None, kdim, tn), lambda j: (s // 2 + j // bw, 0, j % bw))],
        out_specs=(pl.BlockSpec((2, lp, tn), lambda j: (0, 0, j)), pl.BlockSpec((lp, tn), lambda j: (0, j))),
        compiler_params=_params(1),
    )(a, w4, w4)


def _dact_swiglu_bwd(name, d, w, gu, tn_pref=512):
    lp, dm = d.shape
    f = w.shape[0]
    tn = _pick(f, tn_pref)
    chunk = _row_tile(lp, 1, 1, 1152)

    def body(d_ref, w_ref, g_ref, u_ref, o_ref):
        for m0 in range(0, lp, chunk):
            sl = pl.ds(m0, chunk)
            dact = lax.dot_general(d_ref[sl, :], w_ref[...], _DIMS["nt"], preferred_element_type=F32)
            silu, dsilu = _silu_parts(g_ref[sl, :].astype(F32))
            o_ref[0, sl, :] = (dact * u_ref[sl, :].astype(F32) * dsilu).astype(BF16)
            o_ref[1, sl, :] = (dact * silu).astype(BF16)

    return pl.pallas_call(
        body, name=name, grid=(f // tn,), out_shape=jax.ShapeDtypeStruct((2, lp, f), BF16),
        in_specs=[pl.BlockSpec((lp, dm), lambda j: (0, 0)), pl.BlockSpec((tn, dm), lambda j: (j, 0)),
                  pl.BlockSpec((None, lp, tn), lambda j: (0, 0, j)), pl.BlockSpec((None, lp, tn), lambda j: (1, 0, j))],
        out_specs=pl.BlockSpec((2, lp, tn), lambda j: (0, 0, j)), compiler_params=_params(1),
    )(d, w, gu, gu)


def _pool_fwd(name, pooled, pw4):
    lp, dm = pooled.shape
    s, g, rs, gw = pw4.shape
    return _matmul(
        name, "nn", pooled, pw4, jax.ShapeDtypeStruct((lp, dm), BF16), (g, s),
        pl.BlockSpec((lp, rs), lambda gi, si: (0, gi * s + si)),
        pl.BlockSpec((None, None, rs, gw), lambda gi, si: (si, gi, 0, 0)),
        pl.BlockSpec((lp, gw), lambda gi, si: (0, gi)), s, acc_shape=(lp, gw))


def _pool_bwd_act(name, dya, pw4, deps=()):
    lp, dm = dya.shape
    s, g, rs, gw = pw4.shape
    return _matmul(
        name, "nt", dya, pw4, jax.ShapeDtypeStruct((lp, dm), BF16), (g, s),
        pl.BlockSpec((lp, gw), lambda gi, si: (0, gi)),
        pl.BlockSpec((None, None, rs, gw), lambda gi, si: (si, gi, 0, 0)),
        pl.BlockSpec((lp, rs), lambda gi, si: (0, gi * s + si)), 1, deps=deps)


def _pool_bwd_w(name, pooled, dya, s):
    lp, dm = pooled.shape
    g = len(POOL_WINDOWS)
    gw = dm // g
    rs = gw // s
    return _matmul(
        name, "tn", pooled, dya, jax.ShapeDtypeStruct((s, g, rs, gw), BF16), (g, s),
        pl.BlockSpec((lp, rs), lambda gi, si: (0, gi * s + si)),
        pl.BlockSpec((lp, gw), lambda gi, si: (0, gi)),
        pl.BlockSpec((None, None, rs, gw), lambda gi, si: (si, gi, 0, 0)), 1)


def _rms_fwd(name, h, g, tm):
    lp, dm = h.shape

    def body(h_ref, g_ref, o_ref):
        hv = h_ref[...]
        r = lax.rsqrt(jnp.mean(hv * hv, axis=-1, keepdims=True) + EPS)
        o_ref[...] = (hv * r * g_ref[...]).astype(BF16)

    row = pl.BlockSpec((tm, dm), lambda i: (i, 0))
    return pl.pallas_call(
        body, name=name, out_shape=jax.ShapeDtypeStruct((lp, dm), BF16), grid=(lp // tm,),
        in_specs=[row, pl.BlockSpec((1, dm), lambda i: (0, 0))], out_specs=row, compiler_params=_params(1),
    )(h, g)


def _rms_bwd(name, dy, h, g, dres, tm, dep):
    lp, dm = h.shape

    def body(dy_ref, h_ref, g_ref, dr_ref, _, dh_ref, dhb_ref, dg_ref):
        hv = h_ref[...]
        r = lax.rsqrt(jnp.mean(hv * hv, axis=-1, keepdims=True) + EPS)
        xhat = hv * r
        dyv = dy_ref[...]
        dxh = dyv * g_ref[...]
        dh = dr_ref[...] + r * (dxh - xhat * jnp.mean(dxh * xhat, axis=-1, keepdims=True))
        dh_ref[...] = dh
        dhb_ref[...] = dh.astype(BF16)

        @pl.when(pl.program_id(0) == 0)
        def _():
            dg_ref[...] = jnp.zeros_like(dg_ref)

        dg_ref[0:1, :] += jnp.sum(dyv * xhat, axis=0, keepdims=True)

    row = pl.BlockSpec((tm, dm), lambda i: (i, 0))
    slab = pl.BlockSpec((SMALL_ROWS, dm), lambda i: (0, 0))
    return pl.pallas_call(
        body, name=name, grid=(lp // tm,),
        out_shape=(jax.ShapeDtypeStruct((lp, dm), F32), jax.ShapeDtypeStruct((lp, dm), BF16),
                   jax.ShapeDtypeStruct((SMALL_ROWS, dm), F32)),
        in_specs=[row, row, pl.BlockSpec((1, dm), lambda i: (0, 0)), row, ANY], out_specs=(row, row, slab),
        compiler_params=_params(1),
    )(dy, h, g, dres, dep)


def _gate_mix(name, proj, b_gate2, ya, pool_scale, yb, tm):
    _, lp, dm = proj.shape

    def body(ga_ref, gr_ref, b_ref, ya_ref, ps_ref, yb_ref, o_ref):
        g_a = jax.nn.sigmoid(ga_ref[...].astype(F32) + b_ref[0:1, :])
        g_b = jax.nn.sigmoid(gr_ref[...].astype(F32) + b_ref[1:2, :])
        y_a = ya_ref[...].astype(F32) * ps_ref[...]
        o_ref[...] = (g_a * y_a + g_b * yb_ref[...].astype(F32)).astype(BF16)

    row = pl.BlockSpec((tm, dm), lambda i: (i, 0))
    return pl.pallas_call(
        body, name=name, out_shape=jax.ShapeDtypeStruct((lp, dm), BF16), grid=(lp // tm,),
        in_specs=[pl.BlockSpec((None, tm, dm), lambda i: (4, i, 0)), pl.BlockSpec((None, tm, dm), lambda i: (5, i, 0)),
                  pl.BlockSpec((2, dm), lambda i: (0, 0)), row, pl.BlockSpec((1, dm), lambda i: (0, 0)), row],
        out_specs=row, compiler_params=_params(1),
    )(proj, proj, b_gate2, ya, pool_scale, yb)


def _gate_bwd(name, dmix, proj, b_gate2, ya, pool_scale, yb, tm):
    _, lp, dm = proj.shape

    def body(dm_ref, ga_ref, gr_ref, b_ref, ya_ref, ps_ref, yb_ref, dp_ref, dyb_ref, dya_ref, db_ref, dps_ref):
        dmx = dm_ref[...].astype(F32)
        g_a = jax.nn.sigmoid(ga_ref[...].astype(F32) + b_ref[0:1, :])
        g_b = jax.nn.sigmoid(gr_ref[...].astype(F32) + b_ref[1:2, :])
        ya_pre = ya_ref[...].astype(F32)
        ybv = yb_ref[...].astype(F32)
        ps = ps_ref[...]
        dga = dmx * (ya_pre * ps) * (g_a * (1.0 - g_a))
        dgr = dmx * ybv * (g_b * (1.0 - g_b))
        dp_ref[0] = dga.astype(BF16)
        dp_ref[1] = dgr.astype(BF16)
        dyb_ref[...] = (dmx * g_b).astype(BF16)
        dya_ref[...] = (dmx * g_a * ps).astype(BF16)

        @pl.when(pl.program_id(0) == 0)
        def _():
            db_ref[...] = jnp.zeros_like(db_ref)
            dps_ref[...] = jnp.zeros_like(dps_ref)

        db_ref[0:1, :] += jnp.sum(dga, axis=0, keepdims=True)
        db_ref[1:2, :] += jnp.sum(dgr, axis=0, keepdims=True)
        dps_ref[0:1, :] += jnp.sum(dmx * g_a * ya_pre, axis=0, keepdims=True)

    row = pl.BlockSpec((tm, dm), lambda i: (i, 0))
    one = pl.BlockSpec((1, dm), lambda i: (0, 0))
    slab = pl.BlockSpec((SMALL_ROWS, dm), lambda i: (0, 0))
    return pl.pallas_call(
        body, name=name, grid=(lp // tm,),
        out_shape=(jax.ShapeDtypeStruct((6, lp, dm), BF16), jax.ShapeDtypeStruct((lp, dm), BF16),
                   jax.ShapeDtypeStruct((lp, dm), BF16), jax.ShapeDtypeStruct((SMALL_ROWS, dm), F32),
                   jax.ShapeDtypeStruct((SMALL_ROWS, dm), F32)),
        in_specs=[row, pl.BlockSpec((None, tm, dm), lambda i: (4, i, 0)), pl.BlockSpec((None, tm, dm), lambda i: (5, i, 0)),
                  pl.BlockSpec((2, dm), lambda i: (0, 0)), row, one, row],
        out_specs=(pl.BlockSpec((2, tm, dm), lambda i: (2, i, 0)), row, row, slab, slab),
        compiler_params=_params(1),
    )(dmix, proj, proj, b_gate2, ya, pool_scale, yb)


def _swiglu_fwd(name, gu, tm):
    _, lp, f = gu.shape

    def body(g_ref, u_ref, o_ref):
        gt = g_ref[...].astype(F32)
        o_ref[...] = (gt * jax.nn.sigmoid(gt) * u_ref[...].astype(F32)).astype(BF16)

    return pl.pallas_call(
        body, name=name, out_shape=jax.ShapeDtypeStruct((lp, f), BF16), grid=(lp // tm,),
        in_specs=[pl.BlockSpec((None, tm, f), lambda i: (0, i, 0)), pl.BlockSpec((None, tm, f), lambda i: (1, i, 0))],
        out_specs=pl.BlockSpec((tm, f), lambda i: (i, 0)), compiler_params=_params(1),
    )(gu, gu)


def _swiglu_bwd(name, dact, gu, tm):
    _, lp, f = gu.shape

    def body(d_ref, g_ref, u_ref, o_ref):
        d = d_ref[...].astype(F32)
        gt = g_ref[...].astype(F32)
        sg = jax.nn.sigmoid(gt)
        o_ref[0] = (d * u_ref[...].astype(F32) * (sg * (1.0 + gt * (1.0 - sg)))).astype(BF16)
        o_ref[1] = (d * (gt * sg)).astype(BF16)

    return pl.pallas_call(
        body, name=name, out_shape=jax.ShapeDtypeStruct((2, lp, f), BF16), grid=(lp // tm,),
        in_specs=[pl.BlockSpec((tm, f), lambda i: (i, 0)), pl.BlockSpec((None, tm, f), lambda i: (0, i, 0)),
                  pl.BlockSpec((None, tm, f), lambda i: (1, i, 0))],
        out_specs=pl.BlockSpec((2, tm, f), lambda i: (0, i, 0)), compiler_params=_params(1),
    )(dact, gu, gu)


def _final_loss(name, h2, g3, target, tm):
    lp, dm = h2.shape
    nx = target.shape[0] // tm

    def body(h_ref, g_ref, t_ref, dh_ref, dhb_ref, ls_ref, dg_ref):
        i = pl.program_id(0)

        @pl.when(i == 0)
        def _():
            ls_ref[...] = jnp.zeros_like(ls_ref)
            dg_ref[...] = jnp.zeros_like(dg_ref)

        @pl.when(i < nx)
        def _():
            hv = h_ref[...]
            gv = g_ref[...]
            r = lax.rsqrt(jnp.mean(hv * hv, axis=-1, keepdims=True) + EPS)
            xhat = hv * r
            err = xhat * gv - t_ref[...]
            dout = err * (1.0 / dm)
            dxh = dout * gv
            dh = r * (dxh - xhat * jnp.mean(dxh * xhat, axis=-1, keepdims=True))
            dh_ref[...] = dh
            dhb_ref[...] = dh.astype(BF16)
            ls_ref[0:1, :] += jnp.sum(err * err, axis=0, keepdims=True)
            dg_ref[0:1, :] += jnp.sum(dout * xhat, axis=0, keepdims=True)

        @pl.when(i >= nx)
        def _():
            dh_ref[...] = jnp.zeros_like(dh_ref)
            dhb_ref[...] = jnp.zeros_like(dhb_ref)

    row = pl.BlockSpec((tm, dm), lambda i: (i, 0))
    slab = pl.BlockSpec((SMALL_ROWS, dm), lambda i: (0, 0))
    return pl.pallas_call(
        body, name=name, grid=(lp // tm,),
        out_shape=(jax.ShapeDtypeStruct((lp, dm), F32), jax.ShapeDtypeStruct((lp, dm), BF16),
                   jax.ShapeDtypeStruct((SMALL_ROWS, dm), F32), jax.ShapeDtypeStruct((SMALL_ROWS, dm), F32)),
        in_specs=[row, pl.BlockSpec((1, dm), lambda i: (0, 0)), pl.BlockSpec((tm, dm), lambda i: (jnp.minimum(i, nx - 1), 0))],
        out_specs=(row, row, slab, slab), compiler_params=_params(1),
    )(h2, g3, target)


def _shift(v, k):
    return pltpu.roll(v, k % v.shape[0], axis=0)


def _window_sum(v, group, sign):
    s2 = v + _shift(v, sign * 1)
    s4 = s2 + _shift(s2, sign * 2)
    s8 = s4 + _shift(s4, sign * 4)
    s16 = s8 + _shift(s8, sign * 8)
    return jnp.where(group == 0, s2, jnp.where(group == 1, s4, jnp.where(group == 2, s8, s16)))


def _pool_count(lp, group):
    row = lax.broadcasted_iota(jnp.int32, (lp, 1), 0)
    window = jnp.left_shift(2, group).astype(F32)
    meta_pos = (row - (lp - N_META) + 1).astype(F32)
    return jnp.where(row >= lp - N_META, jnp.minimum(meta_pos, window), window)


def _mixer_fwd(name, proj, conv_w, tc):
    _, lp, dm = proj.shape
    per_group = dm // len(POOL_WINDOWS) // tc

    def body(u_ref, gb_ref, gc_ref, v_ref, cw_ref, p_ref, z_ref):
        group = pl.program_id(0) // per_group
        u = u_ref[...].astype(F32)
        p_ref[...] = (_window_sum(u, group, 1) / _pool_count(lp, group) - u).astype(BF16)
        cv = gc_ref[...].astype(F32) * v_ref[...].astype(F32)
        conv = cw_ref[0:1, :] * _shift(cv, 2) + cw_ref[1:2, :] * _shift(cv, 1) + cw_ref[2:3, :] * cv
        z_ref[...] = (gb_ref[...].astype(F32) * conv).astype(BF16)

    def seg(s):
        return pl.BlockSpec((None, lp, tc), lambda j: (s, 0, j))

    col = pl.BlockSpec((lp, tc), lambda j: (0, j))
    return pl.pallas_call(
        body, name=name, grid=(dm // tc,),
        out_shape=(jax.ShapeDtypeStruct((lp, dm), BF16), jax.ShapeDtypeStruct((lp, dm), BF16)),
        in_specs=[seg(0), seg(1), seg(2), seg(3), pl.BlockSpec((3, tc), lambda j: (0, j))],
        out_specs=(col, col), compiler_params=_params(1),
    )(proj, proj, proj, proj, conv_w)


def _mixer_bwd(name, dz, dpooled, proj, conv_w, dproj, tc, dep):
    _, lp, dm = proj.shape
    per_group = dm // len(POOL_WINDOWS) // tc

    def body(dz_ref, dp_ref, gb_ref, gc_ref, v_ref, cw_ref, _, __, o_ref, dcw_ref):
        group = pl.program_id(0) // per_group
        dzv = dz_ref[...].astype(F32)
        gb = gb_ref[...].astype(F32)
        gc = gc_ref[...].astype(F32)
        vv = v_ref[...].astype(F32)
        cv = gc * vv
        c1 = _shift(cv, 1)
        c2 = _shift(cv, 2)
        w0, w1, w2 = cw_ref[0:1, :], cw_ref[1:2, :], cw_ref[2:3, :]
        o_ref[1] = (dzv * (w0 * c2 + w1 * c1 + w2 * cv)).astype(BF16)
        dconv = dzv * gb
        dcw_ref[...] = jnp.zeros_like(dcw_ref)
        dcw_ref[0:1, :] = jnp.sum(dconv * c2, axis=0, keepdims=True)
        dcw_ref[1:2, :] = jnp.sum(dconv * c1, axis=0, keepdims=True)
        dcw_ref[2:3, :] = jnp.sum(dconv * cv, axis=0, keepdims=True)
        dcv = w0 * _shift(dconv, -2) + w1 * _shift(dconv, -1) + w2 * dconv
        o_ref[2] = (dcv * vv).astype(BF16)
        o_ref[3] = (dcv * gc).astype(BF16)
        dpv = dp_ref[...].astype(F32)
        o_ref[0] = (_window_sum(dpv / _pool_count(lp, group), group, -1) - dpv).astype(BF16)

    def seg(s):
        return pl.BlockSpec((None, lp, tc), lambda j: (s, 0, j))

    col = pl.BlockSpec((lp, tc), lambda j: (0, j))
    return pl.pallas_call(
        body, name=name, grid=(dm // tc,),
        out_shape=(jax.ShapeDtypeStruct(dproj.shape, BF16), jax.ShapeDtypeStruct((SMALL_ROWS, dm), F32)),
        in_specs=[col, col, seg(1), seg(2), seg(3), pl.BlockSpec((3, tc), lambda j: (0, j)), ANY, ANY],
        out_specs=(pl.BlockSpec((4, lp, tc), lambda j: (0, 0, j)), pl.BlockSpec((SMALL_ROWS, tc), lambda j: (0, j))),
        input_output_aliases={6: 0}, compiler_params=_params(1),
    )(dz, dpooled, proj, proj, proj, conv_w, dproj, dep)


def _row_tile(r, c, bytes_per_row_elem=4, budget=2 * 1024 * 1024):
    best = None
    for t in range(16, r + 1, 16):
        if r % t == 0 and t * c * bytes_per_row_elem <= budget:
            best = t
    return best if best is not None else r


def _pair_add(name, g4, recv, core):
    s, r, c = g4.shape
    h = r // 2
    tr = _row_tile(h, c, budget=6 * 1024 * 1024)
    nb = h // tr

    def body(core_ref, g_ref, r_ref, o_ref):
        o_ref[...] = (g_ref[...].astype(F32) + r_ref[...].astype(F32)).astype(BF16)

    grid_spec = pltpu.PrefetchScalarGridSpec(
        num_scalar_prefetch=1, grid=(s, nb),
        in_specs=[pl.BlockSpec((None, tr, c), lambda si, j, core_ref: (si, core_ref[0] * nb + j, 0)),
                  pl.BlockSpec((None, tr, c), lambda si, j, core_ref: (si, j, 0))],
        out_specs=pl.BlockSpec((None, tr, c), lambda si, j, core_ref: (si, j, 0)))
    return pl.pallas_call(
        body, name=name, out_shape=jax.ShapeDtypeStruct((s, h, c), BF16), grid_spec=grid_spec,
        compiler_params=_params(2),
    )(core, g4, recv)


def _chip_sum(name, parts, recv, chip):
    _, h, c = parts.shape
    tr = _row_tile(h, c)

    def body(chip_ref, p_ref, r_ref, o_ref):
        acc = p_ref[...].astype(F32)
        for i in range(len(CHIP_FLIPS)):
            acc = acc + r_ref[i].astype(F32)
        o_ref[...] = acc

    grid_spec = pltpu.PrefetchScalarGridSpec(
        num_scalar_prefetch=1, grid=(h // tr,),
        in_specs=[pl.BlockSpec((None, tr, c), lambda j, chip_ref: (chip_ref[0], j, 0)),
                  pl.BlockSpec((len(CHIP_FLIPS), tr, c), lambda j, chip_ref: (0, j, 0))],
        out_specs=pl.BlockSpec((tr, c), lambda j, chip_ref: (j, 0)))
    return pl.pallas_call(
        body, name=name, out_shape=jax.ShapeDtypeStruct((h, c), F32), grid_spec=grid_spec, compiler_params=_params(1),
    )(chip, parts, recv)


def _adam_update(w, gv, m, v):
    c1 = 1.0 - ADAM_B1 ** ADAM_STEP
    c2 = 1.0 - ADAM_B2 ** ADAM_STEP
    nm = ADAM_B1 * m + (1.0 - ADAM_B1) * gv
    nv = ADAM_B2 * v + (1.0 - ADAM_B2) * (gv * gv)
    return -ADAM_LR * ((nm / c1) / (jnp.sqrt(nv / c2) + ADAM_EPS) + ADAM_WD * w), nm, nv


def _adamw_halves(name, w, g_own, g_sib, m, v, core):
    r, c = w.shape
    h = r // 2
    tr = _row_tile(h, c, budget=1024 * 1024)
    nbh = h // tr

    def body(core_ref, w_ref, go_ref, gs_ref, m_ref, v_ref, g_ref, d_ref, nm_ref, nv_ref):
        mine = (pl.program_id(0) // nbh) == core_ref[0]
        gv = jnp.where(mine, go_ref[...], gs_ref[...])
        g_ref[...] = gv
        d_ref[...], nm_ref[...], nv_ref[...] = _adam_update(w_ref[...], gv, m_ref[...], v_ref[...])

    def blk(fn):
        return pl.BlockSpec((tr, c), fn)

    full = blk(lambda j, core_ref: (j, 0))
    own = blk(lambda j, core_ref: (jnp.clip(j - core_ref[0] * nbh, 0, nbh - 1), 0))
    sib = blk(lambda j, core_ref: (jnp.clip(j - (1 - core_ref[0]) * nbh, 0, nbh - 1), 0))
    grid_spec = pltpu.PrefetchScalarGridSpec(
        num_scalar_prefetch=1, grid=(r // tr,), in_specs=[full, own, sib, full, full], out_specs=(full,) * 4)
    sds = jax.ShapeDtypeStruct((r, c), F32)
    return pl.pallas_call(
        body, name=name, out_shape=(sds,) * 4, grid_spec=grid_spec, compiler_params=_params(1),
    )(core, w, g_own, g_sib, m, v)


def _adamw(name, w, g, m, v):
    r, c = w.shape

    def body(w_ref, g_ref, m_ref, v_ref, d_ref, nm_ref, nv_ref):
        d_ref[...], nm_ref[...], nv_ref[...] = _adam_update(w_ref[...], g_ref[...], m_ref[...], v_ref[...])

    blk = pl.BlockSpec((r, c), lambda j: (0, 0))
    sds = jax.ShapeDtypeStruct((r, c), F32)
    return pl.pallas_call(
        body, name=name, out_shape=(sds, sds, sds), grid=(1,), in_specs=[blk] * 4, out_specs=(blk,) * 3,
        compiler_params=_params(1),
    )(w, g, m, v)


def _cast_into_slot(name, w, chip, dtype, deps=()):
    r, c = w.shape
    tr = _row_tile(r, c)

    def body(chip_ref, w_ref, *rest):
        rest[-1][...] = w_ref[...].astype(dtype)

    grid_spec = pltpu.PrefetchScalarGridSpec(
        num_scalar_prefetch=1, grid=(r // tr,),
        in_specs=[pl.BlockSpec((tr, c), lambda j, chip_ref: (j, 0))] + [ANY] * len(deps),
        out_specs=pl.BlockSpec((None, tr, c), lambda j, chip_ref: (chip_ref[0], j, 0)))
    return pl.pallas_call(
        body, name=name, out_shape=jax.ShapeDtypeStruct((4, r, c), dtype), grid_spec=grid_spec, compiler_params=_params(1),
    )(chip, w, *deps)


def _place():
    return lax.axis_index("x"), lax.axis_index("y"), lax.axis_index("c")


def _chip_of(x, y, flip):
    px, py = x ^ flip[0], y ^ flip[1]
    return px, py, 2 * px + py


def _half(ref, which):
    rows = ref.shape[0] // 2
    return ref.at[pl.ds(which * rows, rows)]


HBM = pl.BlockSpec(memory_space=pltpu.HBM)
SEM = pl.BlockSpec(memory_space=pltpu.SEMAPHORE)
SPLIT_COPY = pltpu.CompilerParams(has_side_effects=pltpu.SideEffectType.DATAFLOW_SIDE_EFFECTING)


def _in_hbm(arrays):
    return [pltpu.with_memory_space_constraint(t, pltpu.HBM) for t in arrays]


TOKEN = jax.ShapeDtypeStruct((SMALL_ROWS, LANES), F32)
TOKEN_SPEC = pl.BlockSpec(memory_space=pltpu.VMEM)


def _gather_start(name, slabs, groups):
    n = len(slabs)
    ng = len(groups)
    nf = len(CHIP_FLIPS)

    def body(*refs):
        sems, outs = refs[n:n + 2 * ng], refs[n + 2 * ng:2 * n + 2 * ng]
        token = refs[2 * n + 2 * ng]
        token[...] = jnp.zeros_like(token)
        x, y, c = _place()
        k = 2 * x + y
        for g, members in enumerate(groups):
            for i, a in enumerate(members):
                for j, flip in enumerate(CHIP_FLIPS):
                    px, py, _ = _chip_of(x, y, flip)
                    mine = _half(outs[a].at[k], c)
                    pltpu.make_async_remote_copy(
                        src_ref=mine, dst_ref=mine, send_sem=sems[2 * g].at[i * nf + j], recv_sem=sems[2 * g + 1].at[i * nf + j],
                        device_id=(px, py, c), device_id_type=MESH).start()

    sem_shapes = []
    for members in groups:
        sem_shapes += [pltpu.SemaphoreType.DMA((nf * len(members),))] * 2
    res = pl.pallas_call(
        body, name=name,
        out_shape=tuple(sem_shapes) + tuple(pltpu.HBM(t.shape, t.dtype) for t in slabs) + (TOKEN,),
        in_specs=[HBM] * n, out_specs=tuple([SEM] * (2 * ng) + [HBM] * n + [TOKEN_SPEC]),
        input_output_aliases={a: 2 * ng + a for a in range(n)}, compiler_params=SPLIT_COPY,
    )(*_in_hbm(slabs))
    return [(res[2 * g], res[2 * g + 1]) for g in range(ng)], list(res[2 * ng:2 * ng + n]), res[2 * ng + n]


def _gather_wait(name, slabs, sems, after):
    n = len(slabs)
    nf = len(CHIP_FLIPS)

    def body(*refs):
        ins = refs[:n]
        ssem, rsem = refs[n], refs[n + 1]
        x, y, c = _place()
        k = 2 * x + y
        for a in range(n):
            for j, flip in enumerate(CHIP_FLIPS):
                _, _, kj = _chip_of(x, y, flip)
                cp = pltpu.make_async_remote_copy(
                    src_ref=_half(ins[a].at[k], c), dst_ref=_half(ins[a].at[kj], c), send_sem=ssem.at[a * nf + j],
                    recv_sem=rsem.at[a * nf + j], device_id=(x, y, c), device_id_type=MESH)
                cp.wait_send()
                cp.wait_recv()

    return pl.pallas_call(
        body, name=name, out_shape=tuple(pltpu.HBM(t.shape, t.dtype) for t in slabs),
        in_specs=[HBM] * n + [SEM, SEM, ANY], out_specs=tuple([HBM] * n),
        input_output_aliases={a: a for a in range(n)}, compiler_params=SPLIT_COPY,
    )(*slabs, sems[0], sems[1], after)


def _gather_pass(name, slabs, sems, after):
    n = len(slabs)
    nf = len(CHIP_FLIPS)

    def body(*refs):
        ins = refs[:n]
        ssem, rsem = refs[n], refs[n + 1]
        ssem2, rsem2 = refs[n + 3], refs[n + 4]
        x, y, c = _place()
        k = 2 * x + y
        for a in range(n):
            for j, flip in enumerate(CHIP_FLIPS):
                _, _, kj = _chip_of(x, y, flip)
                landed = _half(ins[a].at[kj], c)
                cp = pltpu.make_async_remote_copy(
                    src_ref=_half(ins[a].at[k], c), dst_ref=landed, send_sem=ssem.at[a * nf + j],
                    recv_sem=rsem.at[a * nf + j], device_id=(x, y, c), device_id_type=MESH)
                cp.wait_send()
                cp.wait_recv()
                pltpu.make_async_remote_copy(
                    src_ref=landed, dst_ref=landed, send_sem=ssem2.at[a * nf + j], recv_sem=rsem2.at[a * nf + j],
                    device_id=(x, y, 1 - c), device_id_type=MESH).start()

    sem = pltpu.SemaphoreType.DMA((nf * n,))
    res = pl.pallas_call(
        body, name=name, out_shape=(sem, sem) + tuple(pltpu.HBM(t.shape, t.dtype) for t in slabs),
        in_specs=[HBM] * n + [SEM, SEM, ANY], out_specs=tuple([SEM, SEM] + [HBM] * n),
        input_output_aliases={a: 2 + a for a in range(n)}, compiler_params=SPLIT_COPY,
    )(*slabs, sems[0], sems[1], after)
    return (res[0], res[1]), list(res[2:])


def _pass_wait(name, slabs, sems, after):
    n = len(slabs)
    nf = len(CHIP_FLIPS)

    def body(*refs):
        ins = refs[:n]
        ssem, rsem = refs[n], refs[n + 1]
        x, y, c = _place()
        for a in range(n):
            for j, flip in enumerate(CHIP_FLIPS):
                _, _, kj = _chip_of(x, y, flip)
                cp = pltpu.make_async_remote_copy(
                    src_ref=_half(ins[a].at[kj], c), dst_ref=_half(ins[a].at[kj], 1 - c), send_sem=ssem.at[a * nf + j],
                    recv_sem=rsem.at[a * nf + j], device_id=(x, y, c), device_id_type=MESH)
                cp.wait_send()
                cp.wait_recv()

    return pl.pallas_call(
        body, name=name, out_shape=tuple(pltpu.HBM(t.shape, t.dtype) for t in slabs),
        in_specs=[HBM] * n + [SEM, SEM, ANY], out_specs=tuple([HBM] * n),
        input_output_aliases={a: a for a in range(n)}, compiler_params=SPLIT_COPY,
    )(*slabs, sems[0], sems[1], after)


def _sibling_part(ref, c, halves):
    if not halves:
        return ref
    h = ref.shape[1] // 2
    return ref.at[:, pl.ds((1 - c) * h, h)]


def _swap_start(name, grads, halves=True, deps=()):
    n = len(grads)

    def body(*refs):
        no = 2 * n + len(deps)
        ssem, rsem = refs[no], refs[no + 1]
        src, land = refs[no + 2:no + n + 2], refs[no + n + 2:no + 2 * n + 2]
        token = refs[no + 2 * n + 2]
        token[...] = jnp.zeros_like(token)
        x, y, c = _place()
        for a in range(n):
            pltpu.make_async_remote_copy(
                src_ref=_sibling_part(src[a], c, halves), dst_ref=land[a], send_sem=ssem.at[a], recv_sem=rsem.at[a],
                device_id=(x, y, 1 - c), device_id_type=MESH).start()

    zones = [lax.empty((g.shape[0], g.shape[1] // 2, g.shape[2]) if halves else g.shape, g.dtype) for g in grads]
    sem = pltpu.SemaphoreType.DMA((n,))
    res = pl.pallas_call(
        body, name=name,
        out_shape=(sem, sem) + tuple(pltpu.HBM(t.shape, t.dtype) for t in list(grads) + zones) + (TOKEN,),
        in_specs=[HBM] * (2 * n) + [ANY] * len(deps), out_specs=tuple([SEM, SEM] + [HBM] * (2 * n) + [TOKEN_SPEC]),
        input_output_aliases={i: 2 + i for i in range(2 * n)}, compiler_params=SPLIT_COPY,
    )(*_in_hbm(list(grads) + zones), *deps)
    return (res[0], res[1], list(res[2:2 + n]), list(res[2 + n:2 + 2 * n])), res[2 + 2 * n]


def _swap_wait(name, ssem, rsem, grads, zones, after, halves=True):
    n = len(grads)

    def body(*refs):
        src, land = refs[:n], refs[n:2 * n]
        ss, rs = refs[2 * n], refs[2 * n + 1]
        x, y, c = _place()
        for a in range(n):
            cp = pltpu.make_async_remote_copy(
                src_ref=_sibling_part(src[a], c, halves), dst_ref=land[a], send_sem=ss.at[a], recv_sem=rs.at[a],
                device_id=(x, y, c), device_id_type=MESH)
            cp.wait_send()
            cp.wait_recv()

    res = pl.pallas_call(
        body, name=name, out_shape=tuple(pltpu.HBM(t.shape, t.dtype) for t in list(grads) + list(zones)),
        in_specs=[HBM] * (2 * n) + [SEM, SEM] + [ANY] * len(after), out_specs=tuple([HBM] * (2 * n)),
        input_output_aliases={i: i for i in range(2 * n)}, compiler_params=SPLIT_COPY,
    )(*grads, *zones, ssem, rsem, *after)
    return list(res[:n]), list(res[n:])


def _sibling_exchange(name, slabs):
    n = len(slabs)
    nf = len(CHIP_FLIPS)

    def body(*refs):
        outs = refs[n:2 * n]
        ssem, rsem = refs[2 * n:]
        x, y, c = _place()

        def copy(a, j, which, to):
            _, _, kj = _chip_of(x, y, CHIP_FLIPS[j])
            ref = _half(outs[a].at[kj], which)
            return pltpu.make_async_remote_copy(src_ref=ref, dst_ref=ref, send_sem=ssem.at[a * nf + j],
                                                recv_sem=rsem.at[a * nf + j], device_id=to, device_id_type=MESH)

        sends = [copy(a, j, c, (x, y, 1 - c)) for a in range(n) for j in range(nf)]
        for cp in sends:
            cp.start()
        for a in range(n):
            for j in range(nf):
                copy(a, j, 1 - c, (x, y, c)).wait_recv()
        for cp in sends:
            cp.wait_send()

    return pl.pallas_call(
        body, name=name, out_shape=tuple(jax.ShapeDtypeStruct(t.shape, t.dtype) for t in slabs),
        in_specs=[ANY] * n, out_specs=(ANY,) * n, input_output_aliases={a: a for a in range(n)},
        scratch_shapes=[pltpu.SemaphoreType.DMA((nf * n,)), pltpu.SemaphoreType.DMA((nf * n,))],
    )(*slabs)


def _sibling_swap(name, grads):
    n = len(grads)

    def body(*refs):
        ins, outs = refs[:n], refs[n:2 * n]
        ssem, rsem = refs[2 * n:]
        x, y, c = _place()
        cps = []
        for a in range(n):
            h = ins[a].shape[1] // 2
            cps.append(pltpu.make_async_remote_copy(
                src_ref=ins[a].at[:, pl.ds((1 - c) * h, h)], dst_ref=outs[a], send_sem=ssem.at[a], recv_sem=rsem.at[a],
                device_id=(x, y, 1 - c), device_id_type=MESH))
        for cp in cps:
            cp.start()
        for cp in cps:
            cp.wait()

    return pl.pallas_call(
        body, name=name,
        out_shape=tuple(jax.ShapeDtypeStruct((g.shape[0], g.shape[1] // 2, g.shape[2]), g.dtype) for g in grads),
        in_specs=[ANY] * n, out_specs=(ANY,) * n,
        scratch_shapes=[pltpu.SemaphoreType.DMA((n,)), pltpu.SemaphoreType.DMA((n,))],
    )(*grads)


def _scatter_start(name, parts):
    n = len(parts)
    nf = len(CHIP_FLIPS)

    def body(*refs):
        ssem, rsem = refs[2 * n], refs[2 * n + 1]
        src, land = refs[2 * n + 2:3 * n + 2], refs[3 * n + 2:4 * n + 2]
        token = refs[4 * n + 2]
        token[...] = jnp.zeros_like(token)
        x, y, c = _place()
        for a in range(n):
            for j, flip in enumerate(CHIP_FLIPS):
                px, py, kj = _chip_of(x, y, flip)
                pltpu.make_async_remote_copy(
                    src_ref=src[a].at[kj], dst_ref=land[a].at[j], send_sem=ssem.at[a * nf + j], recv_sem=rsem.at[a * nf + j],
                    device_id=(px, py, c), device_id_type=MESH).start()

    zones = [lax.empty((nf,) + p.shape[1:], p.dtype) for p in parts]
    sem = pltpu.SemaphoreType.DMA((nf * n,))
    res = pl.pallas_call(
        body, name=name,
        out_shape=(sem, sem) + tuple(pltpu.HBM(t.shape, t.dtype) for t in list(parts) + zones)
        + (jax.ShapeDtypeStruct((SMALL_ROWS, LANES), F32),),
        in_specs=[HBM] * (2 * n),
        out_specs=tuple([SEM, SEM] + [HBM] * (2 * n) + [pl.BlockSpec(memory_space=pltpu.VMEM)]),
        input_output_aliases={i: 2 + i for i in range(2 * n)}, compiler_params=SPLIT_COPY,
    )(*_in_hbm(list(parts) + zones))
    return (res[0], res[1], list(res[2:2 + n]), list(res[2 + n:2 + 2 * n])), res[2 + 2 * n]


def _scatter_wait(name, ssem, rsem, parts, zones, after):
    n = len(parts)
    nf = len(CHIP_FLIPS)

    def body(*refs):
        src, land = refs[:n], refs[n:2 * n]
        ss, rs = refs[2 * n], refs[2 * n + 1]
        x, y, c = _place()
        for a in range(n):
            for j, flip in enumerate(CHIP_FLIPS):
                _, _, kj = _chip_of(x, y, flip)
                cp = pltpu.make_async_remote_copy(
                    src_ref=src[a].at[kj], dst_ref=land[a].at[j], send_sem=ss.at[a * nf + j], recv_sem=rs.at[a * nf + j],
                    device_id=(x, y, c), device_id_type=MESH)
                cp.wait_send()
                cp.wait_recv()

    res = pl.pallas_call(
        body, name=name, out_shape=tuple(pltpu.HBM(t.shape, t.dtype) for t in list(parts) + list(zones)),
        in_specs=[HBM] * (2 * n) + [SEM, SEM] + [ANY] * len(after), out_specs=tuple([HBM] * (2 * n)),
        input_output_aliases={i: i for i in range(2 * n)}, compiler_params=SPLIT_COPY,
    )(*parts, *zones, ssem, rsem, *after)
    return list(res[:n]), list(res[n:])


N_PEERS = 7


def _peer(x, y, c, mask):
    px, py, pc = x ^ ((mask >> 2) & 1), y ^ ((mask >> 1) & 1), c ^ (mask & 1)
    return (px, py, pc), 4 * px + 2 * py + pc


def _reduce_start(vec, deps):
    nd = len(deps)

    def body(*refs):
        ssem, rsem, src, land, token = refs[2 + nd:]
        token[...] = jnp.zeros_like(token)
        x, y, c = _place()
        me = 4 * x + 2 * y + c
        for mask in range(1, N_PEERS + 1):
            to, _ = _peer(x, y, c, mask)
            pltpu.make_async_remote_copy(src_ref=src, dst_ref=land.at[me], send_sem=ssem.at[mask - 1],
                                         recv_sem=rsem.at[mask - 1], device_id=to, device_id_type=MESH).start()

    zone = lax.empty((N_PEERS + 1,) + vec.shape, vec.dtype)
    sem = pltpu.SemaphoreType.DMA((N_PEERS,))
    res = pl.pallas_call(
        body, name="reduce_start",
        out_shape=(sem, sem, pltpu.HBM(vec.shape, vec.dtype), pltpu.HBM(zone.shape, zone.dtype), TOKEN),
        in_specs=[HBM, HBM] + [ANY] * nd, out_specs=(SEM, SEM, HBM, HBM, TOKEN_SPEC),
        input_output_aliases={0: 2, 1: 3}, compiler_params=SPLIT_COPY,
    )(*_in_hbm([vec, zone]), *deps)
    return res[:4], res[4]


def _reduce_wait(ssem, rsem, vec, zone, after):
    def body(src, land, ss, rs, *_):
        x, y, c = _place()
        for mask in range(1, N_PEERS + 1):
            _, frm = _peer(x, y, c, mask)
            cp = pltpu.make_async_remote_copy(src_ref=src, dst_ref=land.at[frm], send_sem=ss.at[mask - 1],
                                              recv_sem=rs.at[mask - 1], device_id=(x, y, c), device_id_type=MESH)
            cp.wait_send()
            cp.wait_recv()

    return pl.pallas_call(
        body, name="reduce_wait", out_shape=(pltpu.HBM(vec.shape, vec.dtype), pltpu.HBM(zone.shape, zone.dtype)),
        in_specs=[HBM, HBM, SEM, SEM] + [ANY] * len(after), out_specs=(HBM, HBM),
        input_output_aliases={0: 0, 1: 1}, compiler_params=SPLIT_COPY,
    )(vec, zone, ssem, rsem, *after)


def _reduce_sum(vec, zone, me, loss_row, loss_scale):
    r, dm = vec.shape

    def body(me_ref, v_ref, z_ref, o_ref, l_ref):
        acc = None
        for i in range(N_PEERS + 1):
            term = jnp.where(me_ref[0] == i, v_ref[...], z_ref[i])
            acc = term if acc is None else acc + term
        o_ref[...] = acc
        l_ref[...] = jnp.sum(acc[loss_row:loss_row + SMALL_ROWS, :], axis=(0, 1), keepdims=True) * loss_scale

    grid_spec = pltpu.PrefetchScalarGridSpec(
        num_scalar_prefetch=1, grid=(1,),
        in_specs=[pl.BlockSpec((r, dm), lambda i, me_ref: (0, 0)), pl.BlockSpec((N_PEERS + 1, r, dm), lambda i, me_ref: (0, 0, 0))],
        out_specs=(pl.BlockSpec((r, dm), lambda i, me_ref: (0, 0)), pl.BlockSpec((1, 1), lambda i, me_ref: (0, 0))))
    return pl.pallas_call(
        body, name="reduce_sum", out_shape=(jax.ShapeDtypeStruct((r, dm), F32), jax.ShapeDtypeStruct((1, 1), F32)),
        grid_spec=grid_spec, compiler_params=_params(1),
    )(me, vec, zone)


def kernel(x, meta_tokens, norm_mix_g, w_in, b_gate, pool_w, pool_scale, conv_w, conv_out_w, w_o, norm_ffn_g, w_gate_up, w_down, norm_final_g, loss_target, m_meta_tokens, m_norm_mix_g, m_w_in, m_b_gate, m_pool_w, m_pool_scale, m_conv_w, m_conv_out_w, m_w_o, m_norm_ffn_g, m_w_gate_up, m_w_down, m_norm_final_g, v_meta_tokens, v_norm_mix_g, v_w_in, v_b_gate, v_pool_w, v_pool_scale, v_conv_w, v_conv_out_w, v_w_o, v_norm_ffn_g, v_w_gate_up, v_w_down, v_norm_final_g):
    seq, dm = x.shape[1], x.shape[2]
    tail = LANES
    tm = tail
    lp = seq + tail
    n_chips = 4
    n_groups = len(POOL_WINDOWS)
    gw = dm // n_groups
    tc = min(256, gw)
    cx, cy, cc = _place()
    chip = 2 * cx + cy
    dloc = dm // n_chips

    pool2 = pool_w.reshape(n_groups * pool_w.shape[1], gw)
    big = {"w_in": w_in, "w_gate_up": w_gate_up, "pool_w": pool2, "conv_out_w": conv_out_w, "w_o": w_o, "w_down": w_down}
    chip1 = jnp.reshape(chip, (1,)).astype(jnp.int32)
    core = jnp.reshape(cc, (1,)).astype(jnp.int32)
    small_loc = jnp.concatenate([meta_tokens, jnp.pad(conv_w, ((0, 8 - conv_w.shape[0]), (0, 0))),
                                 jnp.zeros((8, dloc), F32)], axis=0)
    first = [_cast_into_slot("cast_w_in", w_in, chip1, BF16), _cast_into_slot("place_small", small_loc, chip1, F32)]
    (sems0,), (w_in4, small4), token0 = _gather_start("gather_start_first", first, ([0, 1],))
    rest_names = ["pool_w", "conv_out_w", "w_o", "w_gate_up", "w_down"]
    rest = [_cast_into_slot("cast_" + nme, big[nme], chip1, BF16, deps=(token0,)) for nme in rest_names]
    groups = ([0, 1, 2], [3], [4])
    sems, rest, _ = _gather_start("gather_start_rest", rest, groups)

    def passing(g, after):
        return _gather_pass("gather_pass_%d" % g, [rest[a] for a in groups[g]], sems[g], after)

    g1, g2, g3 = norm_mix_g.reshape(1, dm), norm_ffn_g.reshape(1, dm), norm_final_g.reshape(1, dm)
    b_gate2 = b_gate.reshape(2, dm)
    ps = pool_scale.reshape(1, dm)
    w_in4, small4 = _sibling_exchange("sibling_exchange_first", list(_gather_wait("gather_wait_first", [w_in4, small4], sems0, rest[0])))
    small_f = jnp.transpose(small4, (1, 0, 2)).reshape(small4.shape[1], dm)
    meta_f = small_f[:N_META]
    conv_w_f = small_f[N_META:N_META + 3]
    h0 = jnp.concatenate([x[0], jnp.zeros((tail - N_META, dm), F32), meta_f], axis=0)
    hn1 = _rms_fwd("rms_mix", h0, g1, tm)
    proj = _nn_sharded("proj", hn1, w_in4, 6)
    pass_sems, passed = passing(0, proj)
    pooled, z = _mixer_fwd("mixer_fwd", proj, conv_w_f, tc)
    pool4, conv_out4, w_o4 = _pass_wait("pass_wait_0", passed, pass_sems, pooled)
    pool4 = pool4.reshape(n_chips, n_groups, gw // n_chips, gw)
    conv_out_f = conv_out4.reshape(dm, dm)
    w_o_f = w_o4.reshape(dm, dm)
    ya = _pool_fwd("pool_proj", pooled, pool4)
    yb = _nn_plain("conv_out", z, conv_out_f, BF16)
    mix = _gate_mix("gate_mix", proj, b_gate2, ya, ps, yb, tm)
    pass_sems, passed = passing(1, mix)
    h1 = _nn_plain("attn_out", mix, w_o_f, F32, res=h0, tn_pref=256)
    hn2 = _rms_fwd("rms_ffn", h1, g2, tm)
    (w_gu4,) = _pass_wait("pass_wait_1", passed, pass_sems, hn2)
    gu, act = _gate_up_swiglu("gate_up", hn2, w_gu4)
    pass_sems, passed = passing(2, act)
    (w_down4,) = _pass_wait("pass_wait_2", passed, pass_sems, act)
    w_down_f = w_down4.reshape(-1, dm)
    h2 = _nn_plain("ffn_down", act, w_down_f, F32, res=h1, tn_pref=512, tk_pref=1536)
    dh2, dh2b, loss_cols, dg3 = _final_loss("final_loss", h2, g3, loss_target[0], tm)

    def scatter(tag, names_g, swap, after):
        grads_g, got = _swap_wait("swap_wait_" + tag, *swap, [after])
        pairs = [_pair_add("pair_add_" + nme, g4, rv, core) for nme, g4, rv in zip(names_g, grads_g, got)]
        return _scatter_start("scatter_start_" + tag, pairs)

    dgu = _dact_swiglu_bwd("d_gate_up", dh2b, w_down_f, gu)
    gw_down = _tn_plain("dw_down", act, dh2b)
    gw_gu = _tn_sharded("dw_gate_up", hn2, dgu, n_chips)
    swap_a, token = _swap_start("swap_start_a", [gw_gu, gw_down.reshape(n_chips, -1, dm)])
    dhn2 = _nt_sharded("d_hn2", dgu, w_gu4, deps=(token,))
    flight_a, token = scatter("a", ["w_gate_up", "w_down"], swap_a, dhn2)
    dh1, dh1b, dg2 = _rms_bwd("rms_ffn_bwd", dhn2, h1, g2, dh2, tm, token)
    dmix = _nt_plain("d_mix", dh1b, w_o_f)
    gw_o = _tn_plain("dw_o", mix, dh1b)
    dproj, dyb, dya, db_gate, dps = _gate_bwd("gate_bwd", dmix, proj, b_gate2, ya, ps, yb, tm)
    gw_conv_out = _tn_plain("dw_conv_out", z, dyb)
    gw_pool = _pool_bwd_w("dw_pool", pooled, dya, n_chips)
    swap_b, token = _swap_start("swap_start_b", [gw_o.reshape(n_chips, dloc, dm), gw_conv_out.reshape(n_chips, dloc, dm),
                                                 gw_pool.reshape(n_chips, n_groups * (gw // n_chips), gw)])
    dpooled = _pool_bwd_act("d_pooled", dya, pool4, deps=(token,))
    dz = _nt_plain("d_z", dyb, conv_out_f)
    flight_b, token = scatter("b", ["w_o", "conv_out_w", "pool_w"], swap_b, dz)
    dproj, dconv_w = _mixer_bwd("mixer_bwd", dz, dpooled, proj, conv_w_f, dproj, tc, token)
    gw_in = _tn_sharded("dw_in", hn1, dproj, n_chips)
    swap_c, token = _swap_start("swap_start_c", [gw_in])
    dhn1 = _nt_sharded("d_hn1", dproj, w_in4, deps=(token,))
    flight_c, token = scatter("c", ["w_in"], swap_c, dhn1)
    dh0, _, dg1 = _rms_bwd("rms_mix_bwd", dhn1, h0, g1, dh1, tm, token)
    grad_x = dh0[:seq][None]
    dmeta = dh0[lp - N_META:]

    given = dict(meta_tokens=(meta_tokens, m_meta_tokens, v_meta_tokens), norm_mix_g=(norm_mix_g, m_norm_mix_g, v_norm_mix_g),
                 w_in=(w_in, m_w_in, v_w_in), b_gate=(b_gate, m_b_gate, v_b_gate), pool_w=(pool_w, m_pool_w, v_pool_w),
                 pool_scale=(pool_scale, m_pool_scale, v_pool_scale), conv_w=(conv_w, m_conv_w, v_conv_w),
                 conv_out_w=(conv_out_w, m_conv_out_w, v_conv_out_w), w_o=(w_o, m_w_o, v_w_o),
                 norm_ffn_g=(norm_ffn_g, m_norm_ffn_g, v_norm_ffn_g), w_gate_up=(w_gate_up, m_w_gate_up, v_w_gate_up),
                 w_down=(w_down, m_w_down, v_w_down), norm_final_g=(norm_final_g, m_norm_final_g, v_norm_final_g))
    order = list(given.keys())
    grad, delta, new_m, new_v = {}, {}, {}, {}
    vec = jnp.concatenate([dg1, dg2, dg3, db_gate, dps, loss_cols, dconv_w, dmeta], axis=0)
    loss_row = 5 * SMALL_ROWS
    groups_g = {"a": ["w_gate_up", "w_down"], "b": ["w_o", "conv_out_w", "pool_w"], "c": ["w_in"]}

    def reduced(tag, flight, after):
        pairs, zones = _scatter_wait("scatter_wait_" + tag, *flight, after)
        halves = [_chip_sum("chip_sum_" + nme, p, rv, chip1) for nme, p, rv in zip(groups_g[tag], pairs, zones)]
        return _swap_start("send_start_" + tag, halves, halves=False)

    def update(tag, send, after):
        halves, sib_halves = _swap_wait("send_wait_" + tag, *send, after, halves=False)
        deltas = []
        for nme, g_own, g_sib in zip(groups_g[tag], halves, sib_halves):
            w, m, v = given[nme]
            shape2 = (2 * g_own.shape[0], g_own.shape[1])
            res4 = _adamw_halves("adamw_" + nme, w.reshape(shape2), g_own, g_sib, m.reshape(shape2), v.reshape(shape2), core)
            grad[nme], delta[nme], new_m[nme], new_v[nme] = [t.reshape(w.shape) for t in res4]
            deltas.append(res4[1])
        return deltas

    send_a, token = reduced("a", flight_a, [dh0])
    send_b, token = reduced("b", flight_b, [token])
    done_a = update("a", send_a, [token])
    send_c, token = reduced("c", flight_c, done_a)
    me1 = jnp.reshape(4 * cx + 2 * cy + cc, (1,)).astype(jnp.int32)
    red_flight, token = _reduce_start(vec, [token])
    done_b = update("b", send_b, [token])
    done_c = update("c", send_c, done_b)
    red, loss11 = _reduce_sum(*_reduce_wait(*red_flight, done_c), me1, loss_row, 0.5 / dm)
    loss = loss11[0, 0]
    col0 = chip * dloc
    g_small = {
        "norm_mix_g": red[0], "norm_ffn_g": red[SMALL_ROWS], "norm_final_g": red[2 * SMALL_ROWS],
        "b_gate": red[3 * SMALL_ROWS:3 * SMALL_ROWS + 2].reshape(-1), "pool_scale": red[4 * SMALL_ROWS],
        "conv_w": lax.dynamic_slice(red, (6 * SMALL_ROWS, col0), (3, dloc)),
        "meta_tokens": lax.dynamic_slice(red, (7 * SMALL_ROWS, col0), (N_META, dloc)),
    }

    vec_names = ["norm_mix_g", "norm_ffn_g", "norm_final_g", "pool_scale"]

    def slab_vec(pick):
        rows = [pick(nme).reshape(1, dm) for nme in vec_names] + [pick("b_gate").reshape(2, dm), jnp.zeros((2, dm), F32)]
        return jnp.concatenate(rows, axis=0)

    def slab_col(pick):
        return jnp.concatenate([pick("meta_tokens"), pick("conv_w"), jnp.zeros((5, dloc), F32)], axis=0)

    for slab, tag in ((slab_vec, "vec"), (slab_col, "col")):
        d, nm, nv = _adamw("adamw_small_" + tag, slab(lambda nme: given[nme][0]), slab(lambda nme: g_small[nme]),
                           slab(lambda nme: given[nme][1]), slab(lambda nme: given[nme][2]))
        for out, res in ((delta, d), (new_m, nm), (new_v, nv)):
            if tag == "vec":
                for i, nme in enumerate(vec_names):
                    out[nme] = res[i]
                out["b_gate"] = res[4:6].reshape(-1)
            else:
                out["meta_tokens"] = res[:N_META]
                out["conv_w"] = res[N_META:N_META + 3]
    grad.update(g_small)
    return (loss, grad_x, *[grad[nme] for nme in order], *[delta[nme] for nme in order],
            *[new_m[nme] for nme in order], *[new_v[nme] for nme in order])
```

```python
import functools
import math

import jax
import jax.numpy as jnp
from jax import lax
from jax.experimental import pallas as pl
from jax.experimental.pallas import tpu as pltpu

F32 = jnp.float32
BF16 = jnp.bfloat16
N_META = 16
POOL_WINDOWS = (2, 4, 8, 16)
EPS = 1e-6
ADAM_LR, ADAM_B1, ADAM_B2, ADAM_EPS, ADAM_WD, ADAM_STEP = 0.001, 0.9, 0.999, 1e-08, 0.01, 10
LANES = 128
V7X_VMEM_BYTES = 64 * 1024 * 1024
VMEM_LIMIT = V7X_VMEM_BYTES - 8 * 1024 * 1024
MESH = pl.DeviceIdType.MESH
ANY = pl.BlockSpec(memory_space=pl.ANY)
CHIP_FLIPS = ((1, 0), (0, 1), (1, 1))
SMALL_ROWS = 8


def _pick(n, pref):
    best = None
    for t in range(LANES, min(n, pref) + 1, LANES):
        if n % t == 0:
            best = t
    assert best is not None, (n, pref)
    return best


def _params(n_axes=0):
    sem = ("arbitrary",) * n_axes if n_axes else None
    return pltpu.CompilerParams(dimension_semantics=sem, vmem_limit_bytes=VMEM_LIMIT)


_DIMS = {
    "nn": (((1,), (0,)), ((), ())),
    "nt": (((1,), (1,)), ((), ())),
    "tn": (((0,), (0,)), ((), ())),
}


def _matmul(name, mode, a, b, out_sds, grid, a_spec, b_spec, o_spec, nk, res=None, res_spec=None, acc_shape=None, deps=()):
    out_dtype = out_sds.dtype
    in_place = nk > 1 and out_dtype == F32
    use_scratch = nk > 1 and not in_place
    rows = a_spec.block_shape[-2] if mode != "tn" else None
    chunk = _row_tile(rows, 1, 1, 1152) if rows is not None else None
    n_in = 2 + (res is not None) + len(deps)

    def body(*refs):
        a_ref, b_ref = refs[:2]
        r_ref = refs[2] if res is not None else None
        o_ref, *scr = refs[n_in:]
        k = pl.program_id(len(grid) - 1) if nk > 1 else None

        def emit(sl):
            if sl is None:
                part = lax.dot_general(a_ref[...], b_ref[...], _DIMS[mode], preferred_element_type=F32)
                idx = (slice(None), slice(None))
            else:
                part = lax.dot_general(a_ref[sl, :], b_ref[...], _DIMS[mode], preferred_element_type=F32)
                idx = (sl, slice(None))
            if nk == 1:
                if r_ref is not None:
                    part = part + r_ref[idx]
                o_ref[idx] = part.astype(out_dtype)
                return
            acc = scr[0] if use_scratch else o_ref

            @pl.when(k == 0)
            def _():
                first = part
                if r_ref is not None and in_place:
                    first = first + r_ref[idx]
                acc[idx] = first

            @pl.when(k > 0)
            def _():
                acc[idx] += part

            if use_scratch:

                @pl.when(k == nk - 1)
                def _():
                    o_ref[idx] = acc[idx].astype(out_dtype)

        if mode == "tn" or chunk == rows:
            emit(None)
        else:
            for m0 in range(0, rows, chunk):
                emit(pl.ds(m0, chunk))

    ins = [a, b] + ([res] if res is not None else []) + list(deps)
    in_specs = [a_spec, b_spec] + ([res_spec] if res is not None else []) + [ANY] * len(deps)
    scratch = [pltpu.VMEM(acc_shape, F32)] if use_scratch else []
    return pl.pallas_call(
        body, name=name, out_shape=out_sds, grid=grid, in_specs=in_specs, out_specs=o_spec,
        scratch_shapes=scratch, compiler_params=_params(len(grid)),
    )(*ins)


def _nn_sharded(name, a, w4, nseg):
    lp, kdim = a.shape
    s, _, nloc = w4.shape
    segw = s * nloc // nseg
    tn = _pick(math.gcd(nloc, segw), 1536)
    bw, bo = nloc // tn, segw // tn
    return _matmul(
        name, "nn", a, w4, jax.ShapeDtypeStruct((nseg, lp, segw), BF16), (s * bw,),
        pl.BlockSpec((lp, kdim), lambda j: (0, 0)),
        pl.BlockSpec((None, kdim, tn), lambda j: (j // bw, 0, j % bw)),
        pl.BlockSpec((None, lp, tn), lambda j: (j // bo, 0, j % bo)), 1)


def _nn_plain(name, a, w, out_dtype, res=None, tn_pref=512, tk_pref=2048):
    lp, kdim = a.shape
    n = w.shape[1]
    tn = _pick(n, tn_pref)
    tk = kdim if kdim <= tk_pref else _pick(kdim, tk_pref)
    nk = kdim // tk
    grid = (n // tn, nk) if nk > 1 else (n // tn,)
    if nk > 1:
        a_spec = pl.BlockSpec((lp, tk), lambda j, k: (0, k))
        w_spec = pl.BlockSpec((tk, tn), lambda j, k: (k, j))
        o_spec = pl.BlockSpec((lp, tn), lambda j, k: (0, j))
    else:
        a_spec = pl.BlockSpec((lp, tk), lambda j: (0, 0))
        w_spec = pl.BlockSpec((tk, tn), lambda j: (0, j))
        o_spec = pl.BlockSpec((lp, tn), lambda j: (0, j))
    return _matmul(name, "nn", a, w, jax.ShapeDtypeStruct((lp, n), out_dtype), grid, a_spec, w_spec, o_spec, nk,
                   res=res, res_spec=o_spec if res is not None else None, acc_shape=(lp, tn))


def _nt_plain(name, a, w, tn_pref=512):
    lp, kdim = a.shape
    n = w.shape[0]
    tn = _pick(n, tn_pref)
    return _matmul(
        name, "nt", a, w, jax.ShapeDtypeStruct((lp, n), BF16), (n // tn,),
        pl.BlockSpec((lp, kdim), lambda j: (0, 0)),
        pl.BlockSpec((tn, kdim), lambda j: (j, 0)),
        pl.BlockSpec((lp, tn), lambda j: (0, j)), 1)


def _nt_sharded(name, dseg, w4, to_pref=1024, deps=()):
    nseg, lp, segw = dseg.shape
    s, kdim, nloc = w4.shape
    tr = _pick(math.gcd(nloc, segw), 1536)
    ba, bw = segw // tr, nloc // tr
    nr = s * bw
    to = _pick(kdim, to_pref)
    return _matmul(
        name, "nt", dseg, w4, jax.ShapeDtypeStruct((lp, kdim), F32), (kdim // to, nr),
        pl.BlockSpec((None, lp, tr), lambda j, r: (r // ba, 0, r % ba)),
        pl.BlockSpec((None, to, tr), lambda j, r: (r // bw, j, r % bw)),
        pl.BlockSpec((lp, to), lambda j, r: (0, j)), nr, deps=deps)


def _tn_plain(name, a, d, tk_pref=1024):
    lp, kdim = a.shape
    n = d.shape[1]
    tk = _pick(kdim, tk_pref)
    return _matmul(
        name, "tn", a, d, jax.ShapeDtypeStruct((kdim, n), BF16), (kdim // tk,),
        pl.BlockSpec((lp, tk), lambda i: (0, i)),
        pl.BlockSpec((lp, n), lambda i: (0, 0)),
        pl.BlockSpec((tk, n), lambda i: (i, 0)), 1)


def _tn_sharded(name, a, dseg, s, part=(0, 1), tk_pref=1024, deps=()):
    lp, kdim = a.shape
    nseg, _, segw = dseg.shape
    nloc = nseg * segw // s
    tn = _pick(math.gcd(nloc, segw), 1536)
    bd, bo = segw // tn, nloc // tn
    kpart = kdim // part[1]
    tk = _pick(kpart, tk_pref)
    i0 = part[0] * (kpart // tk)

    def body(a_ref, d_ref, *rest):
        o_ref, at_ref = rest[len(deps):]

        @pl.when(pl.program_id(1) == 0)
        def _():
            at_ref[...] = a_ref[...].T

        o_ref[...] = jnp.dot(at_ref[...], d_ref[...], preferred_element_type=F32).astype(BF16)

    return pl.pallas_call(
        body, name=name, out_shape=jax.ShapeDtypeStruct((s, kpart, nloc), BF16), grid=(kpart // tk, s * bo),
        in_specs=[pl.BlockSpec((lp, tk), lambda i, j: (0, i0 + i)),
                  pl.BlockSpec((None, lp, tn), lambda i, j: (j // bd, 0, j % bd))] + [ANY] * len(deps),
        out_specs=pl.BlockSpec((None, tk, tn), lambda i, j: (j // bo, i, j % bo)),
        scratch_shapes=[pltpu.VMEM((tk, lp), BF16)], compiler_params=_params(2),
    )(a, dseg, *deps)


def _silu_parts(gt):
    sg = jax.nn.sigmoid(gt)
    return gt * sg, sg * (1.0 + gt * (1.0 - sg))


def _gate_up_swiglu(name, a, w4, tn_pref=256):
    lp, kdim = a.shape
    s, _, nloc = w4.shape
    f = s * nloc // 2
    tn = _pick(nloc, tn_pref)
    bw = nloc // tn
    chunk = _row_tile(lp, 1, 1, 1152)

    def body(a_ref, wg_ref, wu_ref, gu_ref, act_ref):
        for m0 in range(0, lp, chunk):
            sl = pl.ds(m0, chunk)
            gt = jnp.dot(a_ref[sl, :], wg_ref[...], preferred_element_type=F32)
            up = jnp.dot(a_ref[sl, :], wu_ref[...], preferred_element_type=F32)
            gu_ref[0, sl, :] = gt.astype(BF16)
            gu_ref[1, sl, :] = up.astype(BF16)
            act_ref[sl, :] = (_silu_parts(gt)[0] * up).astype(BF16)

    return pl.pallas_call(
        body, name=name, grid=(f // tn,),
        out_shape=(jax.ShapeDtypeStruct((2, lp, f), BF16), jax.ShapeDtypeStruct((lp, f), BF16)),
        in_specs=[pl.BlockSpec((lp, kdim), lambda j: (0, 0)),
                  pl.BlockSpec((None, kdim, tn), lambda j: (j // bw, 0, j % bw)),
                  pl.BlockSpec((None, kdim, tn), lambda j: (s // 2 + j // bw, 0, j % bw))],
        out_specs=(pl.BlockSpec((2, lp, tn), lambda j: (0, 0, j)), pl.BlockSpec((lp, tn), lambda j: (0, j))),
        compiler_params=_params(1),
    )(a, w4, w4)


def _dact_swiglu_bwd(name, d, w, gu, tn_pref=512):
    lp, dm = d.shape
    f = w.shape[0]
    tn = _pick(f, tn_pref)
    chunk = _row_tile(lp, 1, 1, 1152)

    def body(d_ref, w_ref, g_ref, u_ref, o_ref):
        for m0 in range(0, lp, chunk):
            sl = pl.ds(m0, chunk)
            dact = lax.dot_general(d_ref[sl, :], w_ref[...], _DIMS["nt"], preferred_element_type=F32)
            silu, dsilu = _silu_parts(g_ref[sl, :].astype(F32))
            o_ref[0, sl, :] = (dact * u_ref[sl, :].astype(F32) * dsilu).astype(BF16)
            o_ref[1, sl, :] = (dact * silu).astype(BF16)

    return pl.pallas_call(
        body, name=name, grid=(f // tn,), out_shape=jax.ShapeDtypeStruct((2, lp, f), BF16),
        in_specs=[pl.BlockSpec((lp, dm), lambda j: (0, 0)), pl.BlockSpec((tn, dm), lambda j: (j, 0)),
                  pl.BlockSpec((None, lp, tn), lambda j: (0, 0, j)), pl.BlockSpec((None, lp, tn), lambda j: (1, 0, j))],
        out_specs=pl.BlockSpec((2, lp, tn), lambda j: (0, 0, j)), compiler_params=_params(1),
    )(d, w, gu, gu)


def _pool_fwd(name, pooled, pw4):
    lp, dm = pooled.shape
    s, g, rs, gw = pw4.shape
    return _matmul(
        name, "nn", pooled, pw4, jax.ShapeDtypeStruct((lp, dm), BF16), (g, s),
        pl.BlockSpec((lp, rs), lambda gi, si: (0, gi * s + si)),
        pl.BlockSpec((None, None, rs, gw), lambda gi, si: (si, gi, 0, 0)),
        pl.BlockSpec((lp, gw), lambda gi, si: (0, gi)), s, acc_shape=(lp, gw))


def _pool_bwd_act(name, dya, pw4, deps=()):
    lp, dm = dya.shape
    s, g, rs, gw = pw4.shape
    return _matmul(
        name, "nt", dya, pw4, jax.ShapeDtypeStruct((lp, dm), BF16), (g, s),
        pl.BlockSpec((lp, gw), lambda gi, si: (0, gi)),
        pl.BlockSpec((None, None, rs, gw), lambda gi, si: (si, gi, 0, 0)),
        pl.BlockSpec((lp, rs), lambda gi, si: (0, gi * s + si)), 1, deps=deps)


def _pool_bwd_w(name, pooled, dya, s):
    lp, dm = pooled.shape
    g = len(POOL_WINDOWS)
    gw = dm // g
    rs = gw // s
    return _matmul(
        name, "tn", pooled, dya, jax.ShapeDtypeStruct((s, g, rs, gw), BF16), (g, s),
        pl.BlockSpec((lp, rs), lambda gi, si: (0, gi * s + si)),
        pl.BlockSpec((lp, gw), lambda gi, si: (0, gi)),
        pl.BlockSpec((None, None, rs, gw), lambda gi, si: (si, gi, 0, 0)), 1)


def _rms_fwd(name, h, g, tm):
    lp, dm = h.shape

    def body(h_ref, g_ref, o_ref):
        hv = h_ref[...]
        r = lax.rsqrt(jnp.mean(hv * hv, axis=-1, keepdims=True) + EPS)
        o_ref[...] = (hv * r * g_ref[...]).astype(BF16)

    row = pl.BlockSpec((tm, dm), lambda i: (i, 0))
    return pl.pallas_call(
        body, name=name, out_shape=jax.ShapeDtypeStruct((lp, dm), BF16), grid=(lp // tm,),
        in_specs=[row, pl.BlockSpec((1, dm), lambda i: (0, 0))], out_specs=row, compiler_params=_params(1),
    )(h, g)


def _rms_bwd(name, dy, h, g, dres, tm, dep):
    lp, dm = h.shape

    def body(dy_ref, h_ref, g_ref, dr_ref, _, dh_ref, dhb_ref, dg_ref):
        hv = h_ref[...]
        r = lax.rsqrt(jnp.mean(hv * hv, axis=-1, keepdims=True) + EPS)
        xhat = hv * r
        dyv = dy_ref[...]
        dxh = dyv * g_ref[...]
        dh = dr_ref[...] + r * (dxh - xhat * jnp.mean(dxh * xhat, axis=-1, keepdims=True))
        dh_ref[...] = dh
        dhb_ref[...] = dh.astype(BF16)

        @pl.when(pl.program_id(0) == 0)
        def _():
            dg_ref[...] = jnp.zeros_like(dg_ref)

        dg_ref[0:1, :] += jnp.sum(dyv * xhat, axis=0, keepdims=True)

    row = pl.BlockSpec((tm, dm), lambda i: (i, 0))
    slab = pl.BlockSpec((SMALL_ROWS, dm), lambda i: (0, 0))
    return pl.pallas_call(
        body, name=name, grid=(lp // tm,),
        out_shape=(jax.ShapeDtypeStruct((lp, dm), F32), jax.ShapeDtypeStruct((lp, dm), BF16),
                   jax.ShapeDtypeStruct((SMALL_ROWS, dm), F32)),
        in_specs=[row, row, pl.BlockSpec((1, dm), lambda i: (0, 0)), row, ANY], out_specs=(row, row, slab),
        compiler_params=_params(1),
    )(dy, h, g, dres, dep)


def _gate_mix(name, proj, b_gate2, ya, pool_scale, yb, tm):
    _, lp, dm = proj.shape

    def body(ga_ref, gr_ref, b_ref, ya_ref, ps_ref, yb_ref, o_ref):
        g_a = jax.nn.sigmoid(ga_ref[...].astype(F32) + b_ref[0:1, :])
        g_b = jax.nn.sigmoid(gr_ref[...].astype(F32) + b_ref[1:2, :])
        y_a = ya_ref[...].astype(F32) * ps_ref[...]
        o_ref[...] = (g_a * y_a + g_b * yb_ref[...].astype(F32)).astype(BF16)

    row = pl.BlockSpec((tm, dm), lambda i: (i, 0))
    return pl.pallas_call(
        body, name=name, out_shape=jax.ShapeDtypeStruct((lp, dm), BF16), grid=(lp // tm,),
        in_specs=[pl.BlockSpec((None, tm, dm), lambda i: (4, i, 0)), pl.BlockSpec((None, tm, dm), lambda i: (5, i, 0)),
                  pl.BlockSpec((2, dm), lambda i: (0, 0)), row, pl.BlockSpec((1, dm), lambda i: (0, 0)), row],
        out_specs=row, compiler_params=_params(1),
    )(proj, proj, b_gate2, ya, pool_scale, yb)


def _gate_bwd(name, dmix, proj, b_gate2, ya, pool_scale, yb, tm):
    _, lp, dm = proj.shape

    def body(dm_ref, ga_ref, gr_ref, b_ref, ya_ref, ps_ref, yb_ref, dp_ref, dyb_ref, dya_ref, db_ref, dps_ref):
        dmx = dm_ref[...].astype(F32)
        g_a = jax.nn.sigmoid(ga_ref[...].astype(F32) + b_ref[0:1, :])
        g_b = jax.nn.sigmoid(gr_ref[...].astype(F32) + b_ref[1:2, :])
        ya_pre = ya_ref[...].astype(F32)
        ybv = yb_ref[...].astype(F32)
        ps = ps_ref[...]
        dga = dmx * (ya_pre * ps) * (g_a * (1.0 - g_a))
        dgr = dmx * ybv * (g_b * (1.0 - g_b))
        dp_ref[0] = dga.astype(BF16)
        dp_ref[1] = dgr.astype(BF16)
        dyb_ref[...] = (dmx * g_b).astype(BF16)
        dya_ref[...] = (dmx * g_a * ps).astype(BF16)

        @pl.when(pl.program_id(0) == 0)
        def _():
            db_ref[...] = jnp.zeros_like(db_ref)
            dps_ref[...] = jnp.zeros_like(dps_ref)

        db_ref[0:1, :] += jnp.sum(dga, axis=0, keepdims=True)
        db_ref[1:2, :] += jnp.sum(dgr, axis=0, keepdims=True)
        dps_ref[0:1, :] += jnp.sum(dmx * g_a * ya_pre, axis=0, keepdims=True)

    row = pl.BlockSpec((tm, dm), lambda i: (i, 0))
    one = pl.BlockSpec((1, dm), lambda i: (0, 0))
    slab = pl.BlockSpec((SMALL_ROWS, dm), lambda i: (0, 0))
    return pl.pallas_call(
        body, name=name, grid=(lp // tm,),
        out_shape=(jax.ShapeDtypeStruct((6, lp, dm), BF16), jax.ShapeDtypeStruct((lp, dm), BF16),
                   jax.ShapeDtypeStruct((lp, dm), BF16), jax.ShapeDtypeStruct((SMALL_ROWS, dm), F32),
                   jax.ShapeDtypeStruct((SMALL_ROWS, dm), F32)),
        in_specs=[row, pl.BlockSpec((None, tm, dm), lambda i: (4, i, 0)), pl.BlockSpec((None, tm, dm), lambda i: (5, i, 0)),
                  pl.BlockSpec((2, dm), lambda i: (0, 0)), row, one, row],
        out_specs=(pl.BlockSpec((2, tm, dm), lambda i: (2, i, 0)), row, row, slab, slab),
        compiler_params=_params(1),
    )(dmix, proj, proj, b_gate2, ya, pool_scale, yb)


def _swiglu_fwd(name, gu, tm):
    _, lp, f = gu.shape

    def body(g_ref, u_ref, o_ref):
        gt = g_ref[...].astype(F32)
        o_ref[...] = (gt * jax.nn.sigmoid(gt) * u_ref[...].astype(F32)).astype(BF16)

    return pl.pallas_call(
        body, name=name, out_shape=jax.ShapeDtypeStruct((lp, f), BF16), grid=(lp // tm,),
        in_specs=[pl.BlockSpec((None, tm, f), lambda i: (0, i, 0)), pl.BlockSpec((None, tm, f), lambda i: (1, i, 0))],
        out_specs=pl.BlockSpec((tm, f), lambda i: (i, 0)), compiler_params=_params(1),
    )(gu, gu)


def _swiglu_bwd(name, dact, gu, tm):
    _, lp, f = gu.shape

    def body(d_ref, g_ref, u_ref, o_ref):
        d = d_ref[...].astype(F32)
        gt = g_ref[...].astype(F32)
        sg = jax.nn.sigmoid(gt)
        o_ref[0] = (d * u_ref[...].astype(F32) * (sg * (1.0 + gt * (1.0 - sg)))).astype(BF16)
        o_ref[1] = (d * (gt * sg)).astype(BF16)

    return pl.pallas_call(
        body, name=name, out_shape=jax.ShapeDtypeStruct((2, lp, f), BF16), grid=(lp // tm,),
        in_specs=[pl.BlockSpec((tm, f), lambda i: (i, 0)), pl.BlockSpec((None, tm, f), lambda i: (0, i, 0)),
                  pl.BlockSpec((None, tm, f), lambda i: (1, i, 0))],
        out_specs=pl.BlockSpec((2, tm, f), lambda i: (0, i, 0)), compiler_params=_params(1),
    )(dact, gu, gu)


def _final_loss(name, h2, g3, target, tm):
    lp, dm = h2.shape
    nx = target.shape[0] // tm

    def body(h_ref, g_ref, t_ref, dh_ref, dhb_ref, ls_ref, dg_ref):
        i = pl.program_id(0)

        @pl.when(i == 0)
        def _():
            ls_ref[...] = jnp.zeros_like(ls_ref)
            dg_ref[...] = jnp.zeros_like(dg_ref)

        @pl.when(i < nx)
        def _():
            hv = h_ref[...]
            gv = g_ref[...]
            r = lax.rsqrt(jnp.mean(hv * hv, axis=-1, keepdims=True) + EPS)
            xhat = hv * r
            err = xhat * gv - t_ref[...]
            dout = err * (1.0 / dm)
            dxh = dout * gv
            dh = r * (dxh - xhat * jnp.mean(dxh * xhat, axis=-1, keepdims=True))
            dh_ref[...] = dh
            dhb_ref[...] = dh.astype(BF16)
            ls_ref[0:1, :] += jnp.sum(err * err, axis=0, keepdims=True)
            dg_ref[0:1, :] += jnp.sum(dout * xhat, axis=0, keepdims=True)

        @pl.when(i >= nx)
        def _():
            dh_ref[...] = jnp.zeros_like(dh_ref)
            dhb_ref[...] = jnp.zeros_like(dhb_ref)

    row = pl.BlockSpec((tm, dm), lambda i: (i, 0))
    slab = pl.BlockSpec((SMALL_ROWS, dm), lambda i: (0, 0))
    return pl.pallas_call(
        body, name=name, grid=(lp // tm,),
        out_shape=(jax.ShapeDtypeStruct((lp, dm), F32), jax.ShapeDtypeStruct((lp, dm), BF16),
                   jax.ShapeDtypeStruct((SMALL_ROWS, dm), F32), jax.ShapeDtypeStruct((SMALL_ROWS, dm), F32)),
        in_specs=[row, pl.BlockSpec((1, dm), lambda i: (0, 0)), pl.BlockSpec((tm, dm), lambda i: (jnp.minimum(i, nx - 1), 0))],
        out_specs=(row, row, slab, slab), compiler_params=_params(1),
    )(h2, g3, target)


def _shift(v, k):
    return pltpu.roll(v, k % v.shape[0], axis=0)


def _window_sum(v, group, sign):
    s2 = v + _shift(v, sign * 1)
    s4 = s2 + _shift(s2, sign * 2)
    s8 = s4 + _shift(s4, sign * 4)
    s16 = s8 + _shift(s8, sign * 8)
    return jnp.where(group == 0, s2, jnp.where(group == 1, s4, jnp.where(group == 2, s8, s16)))


def _pool_count(lp, group):
    row = lax.broadcasted_iota(jnp.int32, (lp, 1), 0)
    window = jnp.left_shift(2, group).astype(F32)
    meta_pos = (row - (lp - N_META) + 1).astype(F32)
    return jnp.where(row >= lp - N_META, jnp.minimum(meta_pos, window), window)


def _mixer_fwd(name, proj, conv_w, tc):
    _, lp, dm = proj.shape
    per_group = dm // len(POOL_WINDOWS) // tc

    def body(u_ref, gb_ref, gc_ref, v_ref, cw_ref, p_ref, z_ref):
        group = pl.program_id(0) // per_group
        u = u_ref[...].astype(F32)
        p_ref[...] = (_window_sum(u, group, 1) / _pool_count(lp, group) - u).astype(BF16)
        cv = gc_ref[...].astype(F32) * v_ref[...].astype(F32)
        conv = cw_ref[0:1, :] * _shift(cv, 2) + cw_ref[1:2, :] * _shift(cv, 1) + cw_ref[2:3, :] * cv
        z_ref[...] = (gb_ref[...].astype(F32) * conv).astype(BF16)

    def seg(s):
        return pl.BlockSpec((None, lp, tc), lambda j: (s, 0, j))

    col = pl.BlockSpec((lp, tc), lambda j: (0, j))
    return pl.pallas_call(
        body, name=name, grid=(dm // tc,),
        out_shape=(jax.ShapeDtypeStruct((lp, dm), BF16), jax.ShapeDtypeStruct((lp, dm), BF16)),
        in_specs=[seg(0), seg(1), seg(2), seg(3), pl.BlockSpec((3, tc), lambda j: (0, j))],
        out_specs=(col, col), compiler_params=_params(1),
    )(proj, proj, proj, proj, conv_w)


def _mixer_bwd(name, dz, dpooled, proj, conv_w, dproj, tc, dep):
    _, lp, dm = proj.shape
    per_group = dm // len(POOL_WINDOWS) // tc

    def body(dz_ref, dp_ref, gb_ref, gc_ref, v_ref, cw_ref, _, __, o_ref, dcw_ref):
        group = pl.program_id(0) // per_group
        dzv = dz_ref[...].astype(F32)
        gb = gb_ref[...].astype(F32)
        gc = gc_ref[...].astype(F32)
        vv = v_ref[...].astype(F32)
        cv = gc * vv
        c1 = _shift(cv, 1)
        c2 = _shift(cv, 2)
        w0, w1, w2 = cw_ref[0:1, :], cw_ref[1:2, :], cw_ref[2:3, :]
        o_ref[1] = (dzv * (w0 * c2 + w1 * c1 + w2 * cv)).astype(BF16)
        dconv = dzv * gb
        dcw_ref[...] = jnp.zeros_like(dcw_ref)
        dcw_ref[0:1, :] = jnp.sum(dconv * c2, axis=0, keepdims=True)
        dcw_ref[1:2, :] = jnp.sum(dconv * c1, axis=0, keepdims=True)
        dcw_ref[2:3, :] = jnp.sum(dconv * cv, axis=0, keepdims=True)
        dcv = w0 * _shift(dconv, -2) + w1 * _shift(dconv, -1) + w2 * dconv
        o_ref[2] = (dcv * vv).astype(BF16)
        o_ref[3] = (dcv * gc).astype(BF16)
        dpv = dp_ref[...].astype(F32)
        o_ref[0] = (_window_sum(dpv / _pool_count(lp, group), group, -1) - dpv).astype(BF16)

    def seg(s):
        return pl.BlockSpec((None, lp, tc), lambda j: (s, 0, j))

    col = pl.BlockSpec((lp, tc), lambda j: (0, j))
    return pl.pallas_call(
        body, name=name, grid=(dm // tc,),
        out_shape=(jax.ShapeDtypeStruct(dproj.shape, BF16), jax.ShapeDtypeStruct((SMALL_ROWS, dm), F32)),
        in_specs=[col, col, seg(1), seg(2), seg(3), pl.BlockSpec((3, tc), lambda j: (0, j)), ANY, ANY],
        out_specs=(pl.BlockSpec((4, lp, tc), lambda j: (0, 0, j)), pl.BlockSpec((SMALL_ROWS, tc), lambda j: (0, j))),
        input_output_aliases={6: 0}, compiler_params=_params(1),
    )(dz, dpooled, proj, proj, proj, conv_w, dproj, dep)


def _row_tile(r, c, bytes_per_row_elem=4, budget=2 * 1024 * 1024):
    best = None
    for t in range(16, r + 1, 16):
        if r % t == 0 and t * c * bytes_per_row_elem <= budget:
            best = t
    return best if best is not None else r


def _pair_add(name, g4, recv, core):
    s, r, c = g4.shape
    h = r // 2
    tr = _row_tile(h, c, budget=6 * 1024 * 1024)
    nb = h // tr

    def body(core_ref, g_ref, r_ref, o_ref):
        o_ref[...] = (g_ref[...].astype(F32) + r_ref[...].astype(F32)).astype(BF16)

    grid_spec = pltpu.PrefetchScalarGridSpec(
        num_scalar_prefetch=1, grid=(s, nb),
        in_specs=[pl.BlockSpec((None, tr, c), lambda si, j, core_ref: (si, core_ref[0] * nb + j, 0)),
                  pl.BlockSpec((None, tr, c), lambda si, j, core_ref: (si, j, 0))],
        out_specs=pl.BlockSpec((None, tr, c), lambda si, j, core_ref: (si, j, 0)))
    return pl.pallas_call(
        body, name=name, out_shape=jax.ShapeDtypeStruct((s, h, c), BF16), grid_spec=grid_spec,
        compiler_params=_params(2),
    )(core, g4, recv)


def _chip_sum(name, parts, recv, chip):
    _, h, c = parts.shape
    tr = _row_tile(h, c)

    def body(chip_ref, p_ref, r_ref, o_ref):
        acc = p_ref[...].astype(F32)
        for i in range(len(CHIP_FLIPS)):
            acc = acc + r_ref[i].astype(F32)
        o_ref[...] = acc

    grid_spec = pltpu.PrefetchScalarGridSpec(
        num_scalar_prefetch=1, grid=(h // tr,),
        in_specs=[pl.BlockSpec((None, tr, c), lambda j, chip_ref: (chip_ref[0], j, 0)),
                  pl.BlockSpec((len(CHIP_FLIPS), tr, c), lambda j, chip_ref: (0, j, 0))],
        out_specs=pl.BlockSpec((tr, c), lambda j, chip_ref: (j, 0)))
    return pl.pallas_call(
        body, name=name, out_shape=jax.ShapeDtypeStruct((h, c), F32), grid_spec=grid_spec, compiler_params=_params(1),
    )(chip, parts, recv)


def _adam_update(w, gv, m, v):
    c1 = 1.0 - ADAM_B1 ** ADAM_STEP
    c2 = 1.0 - ADAM_B2 ** ADAM_STEP
    nm = ADAM_B1 * m + (1.0 - ADAM_B1) * gv
    nv = ADAM_B2 * v + (1.0 - ADAM_B2) * (gv * gv)
    return -ADAM_LR * ((nm / c1) / (jnp.sqrt(nv / c2) + ADAM_EPS) + ADAM_WD * w), nm, nv


def _adamw_halves(name, w, g_own, g_sib, m, v, core, part=(0, 1), prev=None):
    r, c = w.shape
    rp = r // part[1]
    h = rp // 2
    tr = _row_tile(h, c, budget=1024 * 1024)
    nbh = h // tr
    j0 = part[0] * 2 * nbh
    n_prev = 0 if prev is None else 4

    def body(core_ref, w_ref, go_ref, gs_ref, m_ref, v_ref, *rest):
        g_ref, d_ref, nm_ref, nv_ref = rest[n_prev:]
        mine = (pl.program_id(0) // nbh) == core_ref[0]
        gv = jnp.where(mine, go_ref[...], gs_ref[...])
        g_ref[...] = gv
        d_ref[...], nm_ref[...], nv_ref[...] = _adam_update(w_ref[...], gv, m_ref[...], v_ref[...])

    def blk(fn):
        return pl.BlockSpec((tr, c), fn)

    full = blk(lambda j, core_ref: (j0 + j, 0))
    own = blk(lambda j, core_ref: (jnp.clip(j - core_ref[0] * nbh, 0, nbh - 1), 0))
    sib = blk(lambda j, core_ref: (jnp.clip(j - (1 - core_ref[0]) * nbh, 0, nbh - 1), 0))
    grid_spec = pltpu.PrefetchScalarGridSpec(
        num_scalar_prefetch=1, grid=(2 * nbh,), in_specs=[full, own, sib, full, full] + [ANY] * n_prev, out_specs=(full,) * 4)
    sds = jax.ShapeDtypeStruct((r, c), F32)
    return pl.pallas_call(
        body, name=name, out_shape=(sds,) * 4, grid_spec=grid_spec, compiler_params=_params(1),
        input_output_aliases={6 + i: i for i in range(n_prev)},
    )(core, w, g_own, g_sib, m, v, *(prev or ()))


def _adamw(name, w, g, m, v):
    r, c = w.shape

    def body(w_ref, g_ref, m_ref, v_ref, d_ref, nm_ref, nv_ref):
        d_ref[...], nm_ref[...], nv_ref[...] = _adam_update(w_ref[...], g_ref[...], m_ref[...], v_ref[...])

    blk = pl.BlockSpec((r, c), lambda j: (0, 0))
    sds = jax.ShapeDtypeStruct((r, c), F32)
    return pl.pallas_call(
        body, name=name, out_shape=(sds, sds, sds), grid=(1,), in_specs=[blk] * 4, out_specs=(blk,) * 3,
        compiler_params=_params(1),
    )(w, g, m, v)


def _cast_into_slot(name, w, chip, dtype, deps=()):
    r, c = w.shape
    tr = _row_tile(r, c)

    def body(chip_ref, w_ref, *rest):
        rest[-1][...] = w_ref[...].astype(dtype)

    grid_spec = pltpu.PrefetchScalarGridSpec(
        num_scalar_prefetch=1, grid=(r // tr,),
        in_specs=[pl.BlockSpec((tr, c), lambda j, chip_ref: (j, 0))] + [ANY] * len(deps),
        out_specs=pl.BlockSpec((None, tr, c), lambda j, chip_ref: (chip_ref[0], j, 0)))
    return pl.pallas_call(
        body, name=name, out_shape=jax.ShapeDtypeStruct((4, r, c), dtype), grid_spec=grid_spec, compiler_params=_params(1),
    )(chip, w, *deps)


def _place():
    return lax.axis_index("x"), lax.axis_index("y"), lax.axis_index("c")


def _chip_of(x, y, flip):
    px, py = x ^ flip[0], y ^ flip[1]
    return px, py, 2 * px + py


def _half(ref, which):
    rows = ref.shape[0] // 2
    return ref.at[pl.ds(which * rows, rows)]


HBM = pl.BlockSpec(memory_space=pltpu.HBM)
SEM = pl.BlockSpec(memory_space=pltpu.SEMAPHORE)
SPLIT_COPY = pltpu.CompilerParams(has_side_effects=pltpu.SideEffectType.DATAFLOW_SIDE_EFFECTING)


def _in_hbm(arrays):
    return [pltpu.with_memory_space_constraint(t, pltpu.HBM) for t in arrays]


TOKEN = jax.ShapeDtypeStruct((SMALL_ROWS, LANES), F32)
TOKEN_SPEC = pl.BlockSpec(memory_space=pltpu.VMEM)


def _gather_start(name, slabs, groups):
    n = len(slabs)
    ng = len(groups)
    nf = len(CHIP_FLIPS)

    def body(*refs):
        sems, outs = refs[n:n + 2 * ng], refs[n + 2 * ng:2 * n + 2 * ng]
        token = refs[2 * n + 2 * ng]
        token[...] = jnp.zeros_like(token)
        x, y, c = _place()
        k = 2 * x + y
        for g, members in enumerate(groups):
            for i, a in enumerate(members):
                for j, flip in enumerate(CHIP_FLIPS):
                    px, py, _ = _chip_of(x, y, flip)
                    mine = _half(outs[a].at[k], c)
                    pltpu.make_async_remote_copy(
                        src_ref=mine, dst_ref=mine, send_sem=sems[2 * g].at[i * nf + j], recv_sem=sems[2 * g + 1].at[i * nf + j],
                        device_id=(px, py, c), device_id_type=MESH).start()

    sem_shapes = []
    for members in groups:
        sem_shapes += [pltpu.SemaphoreType.DMA((nf * len(members),))] * 2
    res = pl.pallas_call(
        body, name=name,
        out_shape=tuple(sem_shapes) + tuple(pltpu.HBM(t.shape, t.dtype) for t in slabs) + (TOKEN,),
        in_specs=[HBM] * n, out_specs=tuple([SEM] * (2 * ng) + [HBM] * n + [TOKEN_SPEC]),
        input_output_aliases={a: 2 * ng + a for a in range(n)}, compiler_params=SPLIT_COPY,
    )(*_in_hbm(slabs))
    return [(res[2 * g], res[2 * g + 1]) for g in range(ng)], list(res[2 * ng:2 * ng + n]), res[2 * ng + n]


def _gather_wait(name, slabs, sems, after):
    n = len(slabs)
    nf = len(CHIP_FLIPS)

    def body(*refs):
        ins = refs[:n]
        ssem, rsem = refs[n], refs[n + 1]
        x, y, c = _place()
        k = 2 * x + y
        for a in range(n):
            for j, flip in enumerate(CHIP_FLIPS):
                _, _, kj = _chip_of(x, y, flip)
                cp = pltpu.make_async_remote_copy(
                    src_ref=_half(ins[a].at[k], c), dst_ref=_half(ins[a].at[kj], c), send_sem=ssem.at[a * nf + j],
                    recv_sem=rsem.at[a * nf + j], device_id=(x, y, c), device_id_type=MESH)
                cp.wait_send()
                cp.wait_recv()

    return pl.pallas_call(
        body, name=name, out_shape=tuple(pltpu.HBM(t.shape, t.dtype) for t in slabs),
        in_specs=[HBM] * n + [SEM, SEM, ANY], out_specs=tuple([HBM] * n),
        input_output_aliases={a: a for a in range(n)}, compiler_params=SPLIT_COPY,
    )(*slabs, sems[0], sems[1], after)


def _gather_pass(name, slabs, sems, after):
    n = len(slabs)
    nf = len(CHIP_FLIPS)

    def body(*refs):
        ins = refs[:n]
        ssem, rsem = refs[n], refs[n + 1]
        ssem2, rsem2 = refs[n + 3], refs[n + 4]
        x, y, c = _place()
        k = 2 * x + y
        for a in range(n):
            for j, flip in enumerate(CHIP_FLIPS):
                _, _, kj = _chip_of(x, y, flip)
                landed = _half(ins[a].at[kj], c)
                cp = pltpu.make_async_remote_copy(
                    src_ref=_half(ins[a].at[k], c), dst_ref=landed, send_sem=ssem.at[a * nf + j],
                    recv_sem=rsem.at[a * nf + j], device_id=(x, y, c), device_id_type=MESH)
                cp.wait_send()
                cp.wait_recv()
                pltpu.make_async_remote_copy(
                    src_ref=landed, dst_ref=landed, send_sem=ssem2.at[a * nf + j], recv_sem=rsem2.at[a * nf + j],
                    device_id=(x, y, 1 - c), device_id_type=MESH).start()

    sem = pltpu.SemaphoreType.DMA((nf * n,))
    res = pl.pallas_call(
        body, name=name, out_shape=(sem, sem) + tuple(pltpu.HBM(t.shape, t.dtype) for t in slabs),
        in_specs=[HBM] * n + [SEM, SEM, ANY], out_specs=tuple([SEM, SEM] + [HBM] * n),
        input_output_aliases={a: 2 + a for a in range(n)}, compiler_params=SPLIT_COPY,
    )(*slabs, sems[0], sems[1], after)
    return (res[0], res[1]), list(res[2:])


def _pass_wait(name, slabs, sems, after):
    n = len(slabs)
    nf = len(CHIP_FLIPS)

    def body(*refs):
        ins = refs[:n]
        ssem, rsem = refs[n], refs[n + 1]
        x, y, c = _place()
        for a in range(n):
            for j, flip in enumerate(CHIP_FLIPS):
                _, _, kj = _chip_of(x, y, flip)
                cp = pltpu.make_async_remote_copy(
                    src_ref=_half(ins[a].at[kj], c), dst_ref=_half(ins[a].at[kj], 1 - c), send_sem=ssem.at[a * nf + j],
                    recv_sem=rsem.at[a * nf + j], device_id=(x, y, c), device_id_type=MESH)
                cp.wait_send()
                cp.wait_recv()

    return pl.pallas_call(
        body, name=name, out_shape=tuple(pltpu.HBM(t.shape, t.dtype) for t in slabs),
        in_specs=[HBM] * n + [SEM, SEM, ANY], out_specs=tuple([HBM] * n),
        input_output_aliases={a: a for a in range(n)}, compiler_params=SPLIT_COPY,
    )(*slabs, sems[0], sems[1], after)


def _sibling_part(ref, c, halves):
    if not halves:
        return ref
    h = ref.shape[1] // 2
    return ref.at[:, pl.ds((1 - c) * h, h)]


def _swap_start(name, grads, halves=True, deps=()):
    n = len(grads)

    def body(*refs):
        no = 2 * n + len(deps)
        ssem, rsem = refs[no], refs[no + 1]
        src, land = refs[no + 2:no + n + 2], refs[no + n + 2:no + 2 * n + 2]
        token = refs[no + 2 * n + 2]
        token[...] = jnp.zeros_like(token)
        x, y, c = _place()
        for a in range(n):
            pltpu.make_async_remote_copy(
                src_ref=_sibling_part(src[a], c, halves), dst_ref=land[a], send_sem=ssem.at[a], recv_sem=rsem.at[a],
                device_id=(x, y, 1 - c), device_id_type=MESH).start()

    zones = [lax.empty((g.shape[0], g.shape[1] // 2, g.shape[2]) if halves else g.shape, g.dtype) for g in grads]
    sem = pltpu.SemaphoreType.DMA((n,))
    res = pl.pallas_call(
        body, name=name,
        out_shape=(sem, sem) + tuple(pltpu.HBM(t.shape, t.dtype) for t in list(grads) + zones) + (TOKEN,),
        in_specs=[HBM] * (2 * n) + [ANY] * len(deps), out_specs=tuple([SEM, SEM] + [HBM] * (2 * n) + [TOKEN_SPEC]),
        input_output_aliases={i: 2 + i for i in range(2 * n)}, compiler_params=SPLIT_COPY,
    )(*_in_hbm(list(grads) + zones), *deps)
    return (res[0], res[1], list(res[2:2 + n]), list(res[2 + n:2 + 2 * n])), res[2 + 2 * n]


def _swap_wait(name, ssem, rsem, grads, zones, after, halves=True):
    n = len(grads)

    def body(*refs):
        src, land = refs[:n], refs[n:2 * n]
        ss, rs = refs[2 * n], refs[2 * n + 1]
        x, y, c = _place()
        for a in range(n):
            cp = pltpu.make_async_remote_copy(
                src_ref=_sibling_part(src[a], c, halves), dst_ref=land[a], send_sem=ss.at[a], recv_sem=rs.at[a],
                device_id=(x, y, c), device_id_type=MESH)
            cp.wait_send()
            cp.wait_recv()

    res = pl.pallas_call(
        body, name=name, out_shape=tuple(pltpu.HBM(t.shape, t.dtype) for t in list(grads) + list(zones)),
        in_specs=[HBM] * (2 * n) + [SEM, SEM] + [ANY] * len(after), out_specs=tuple([HBM] * (2 * n)),
        input_output_aliases={i: i for i in range(2 * n)}, compiler_params=SPLIT_COPY,
    )(*grads, *zones, ssem, rsem, *after)
    return list(res[:n]), list(res[n:])


def _sibling_exchange(name, slabs):
    n = len(slabs)
    nf = len(CHIP_FLIPS)

    def body(*refs):
        outs = refs[n:2 * n]
        ssem, rsem = refs[2 * n:]
        x, y, c = _place()

        def copy(a, j, which, to):
            _, _, kj = _chip_of(x, y, CHIP_FLIPS[j])
            ref = _half(outs[a].at[kj], which)
            return pltpu.make_async_remote_copy(src_ref=ref, dst_ref=ref, send_sem=ssem.at[a * nf + j],
                                                recv_sem=rsem.at[a * nf + j], device_id=to, device_id_type=MESH)

        sends = [copy(a, j, c, (x, y, 1 - c)) for a in range(n) for j in range(nf)]
        for cp in sends:
            cp.start()
        for a in range(n):
            for j in range(nf):
                copy(a, j, 1 - c, (x, y, c)).wait_recv()
        for cp in sends:
            cp.wait_send()

    return pl.pallas_call(
        body, name=name, out_shape=tuple(jax.ShapeDtypeStruct(t.shape, t.dtype) for t in slabs),
        in_specs=[ANY] * n, out_specs=(ANY,) * n, input_output_aliases={a: a for a in range(n)},
        scratch_shapes=[pltpu.SemaphoreType.DMA((nf * n,)), pltpu.SemaphoreType.DMA((nf * n,))],
    )(*slabs)


def _sibling_swap(name, grads):
    n = len(grads)

    def body(*refs):
        ins, outs = refs[:n], refs[n:2 * n]
        ssem, rsem = refs[2 * n:]
        x, y, c = _place()
        cps = []
        for a in range(n):
            h = ins[a].shape[1] // 2
            cps.append(pltpu.make_async_remote_copy(
                src_ref=ins[a].at[:, pl.ds((1 - c) * h, h)], dst_ref=outs[a], send_sem=ssem.at[a], recv_sem=rsem.at[a],
                device_id=(x, y, 1 - c), device_id_type=MESH))
        for cp in cps:
            cp.start()
        for cp in cps:
            cp.wait()

    return pl.pallas_call(
        body, name=name,
        out_shape=tuple(jax.ShapeDtypeStruct((g.shape[0], g.shape[1] // 2, g.shape[2]), g.dtype) for g in grads),
        in_specs=[ANY] * n, out_specs=(ANY,) * n,
        scratch_shapes=[pltpu.SemaphoreType.DMA((n,)), pltpu.SemaphoreType.DMA((n,))],
    )(*grads)


def _scatter_start(name, parts):
    n = len(parts)
    nf = len(CHIP_FLIPS)

    def body(*refs):
        ssem, rsem = refs[2 * n], refs[2 * n + 1]
        src, land = refs[2 * n + 2:3 * n + 2], refs[3 * n + 2:4 * n + 2]
        token = refs[4 * n + 2]
        token[...] = jnp.zeros_like(token)
        x, y, c = _place()
        for a in range(n):
            for j, flip in enumerate(CHIP_FLIPS):
                px, py, kj = _chip_of(x, y, flip)
                pltpu.make_async_remote_copy(
                    src_ref=src[a].at[kj], dst_ref=land[a].at[j], send_sem=ssem.at[a * nf + j], recv_sem=rsem.at[a * nf + j],
                    device_id=(px, py, c), device_id_type=MESH).start()

    zones = [lax.empty((nf,) + p.shape[1:], p.dtype) for p in parts]
    sem = pltpu.SemaphoreType.DMA((nf * n,))
    res = pl.pallas_call(
        body, name=name,
        out_shape=(sem, sem) + tuple(pltpu.HBM(t.shape, t.dtype) for t in list(parts) + zones)
        + (jax.ShapeDtypeStruct((SMALL_ROWS, LANES), F32),),
        in_specs=[HBM] * (2 * n),
        out_specs=tuple([SEM, SEM] + [HBM] * (2 * n) + [pl.BlockSpec(memory_space=pltpu.VMEM)]),
        input_output_aliases={i: 2 + i for i in range(2 * n)}, compiler_params=SPLIT_COPY,
    )(*_in_hbm(list(parts) + zones))
    return (res[0], res[1], list(res[2:2 + n]), list(res[2 + n:2 + 2 * n])), res[2 + 2 * n]


def _scatter_wait(name, ssem, rsem, parts, zones, after):
    n = len(parts)
    nf = len(CHIP_FLIPS)

    def body(*refs):
        src, land = refs[:n], refs[n:2 * n]
        ss, rs = refs[2 * n], refs[2 * n + 1]
        x, y, c = _place()
        for a in range(n):
            for j, flip in enumerate(CHIP_FLIPS):
                _, _, kj = _chip_of(x, y, flip)
                cp = pltpu.make_async_remote_copy(
                    src_ref=src[a].at[kj], dst_ref=land[a].at[j], send_sem=ss.at[a * nf + j], recv_sem=rs.at[a * nf + j],
                    device_id=(x, y, c), device_id_type=MESH)
                cp.wait_send()
                cp.wait_recv()

    res = pl.pallas_call(
        body, name=name, out_shape=tuple(pltpu.HBM(t.shape, t.dtype) for t in list(parts) + list(zones)),
        in_specs=[HBM] * (2 * n) + [SEM, SEM] + [ANY] * len(after), out_specs=tuple([HBM] * (2 * n)),
        input_output_aliases={i: i for i in range(2 * n)}, compiler_params=SPLIT_COPY,
    )(*parts, *zones, ssem, rsem, *after)
    return list(res[:n]), list(res[n:])


N_PEERS = 7


def _peer(x, y, c, mask):
    px, py, pc = x ^ ((mask >> 2) & 1), y ^ ((mask >> 1) & 1), c ^ (mask & 1)
    return (px, py, pc), 4 * px + 2 * py + pc


def _reduce_start(vec, deps):
    nd = len(deps)

    def body(*refs):
        ssem, rsem, src, land, token = refs[2 + nd:]
        token[...] = jnp.zeros_like(token)
        x, y, c = _place()
        me = 4 * x + 2 * y + c
        for mask in range(1, N_PEERS + 1):
            to, _ = _peer(x, y, c, mask)
            pltpu.make_async_remote_copy(src_ref=src, dst_ref=land.at[me], send_sem=ssem.at[mask - 1],
                                         recv_sem=rsem.at[mask - 1], device_id=to, device_id_type=MESH).start()

    zone = lax.empty((N_PEERS + 1,) + vec.shape, vec.dtype)
    sem = pltpu.SemaphoreType.DMA((N_PEERS,))
    res = pl.pallas_call(
        body, name="reduce_start",
        out_shape=(sem, sem, pltpu.HBM(vec.shape, vec.dtype), pltpu.HBM(zone.shape, zone.dtype), TOKEN),
        in_specs=[HBM, HBM] + [ANY] * nd, out_specs=(SEM, SEM, HBM, HBM, TOKEN_SPEC),
        input_output_aliases={0: 2, 1: 3}, compiler_params=SPLIT_COPY,
    )(*_in_hbm([vec, zone]), *deps)
    return res[:4], res[4]


def _reduce_wait(ssem, rsem, vec, zone, after):
    def body(src, land, ss, rs, *_):
        x, y, c = _place()
        for mask in range(1, N_PEERS + 1):
            _, frm = _peer(x, y, c, mask)
            cp = pltpu.make_async_remote_copy(src_ref=src, dst_ref=land.at[frm], send_sem=ss.at[mask - 1],
                                              recv_sem=rs.at[mask - 1], device_id=(x, y, c), device_id_type=MESH)
            cp.wait_send()
            cp.wait_recv()

    return pl.pallas_call(
        body, name="reduce_wait", out_shape=(pltpu.HBM(vec.shape, vec.dtype), pltpu.HBM(zone.shape, zone.dtype)),
        in_specs=[HBM, HBM, SEM, SEM] + [ANY] * len(after), out_specs=(HBM, HBM),
        input_output_aliases={0: 0, 1: 1}, compiler_params=SPLIT_COPY,
    )(vec, zone, ssem, rsem, *after)


def _reduce_sum(vec, zone, me, loss_row, loss_scale):
    r, dm = vec.shape

    def body(me_ref, v_ref, z_ref, o_ref, l_ref):
        acc = None
        for i in range(N_PEERS + 1):
            term = jnp.where(me_ref[0] == i, v_ref[...], z_ref[i])
            acc = term if acc is None else acc + term
        o_ref[...] = acc
        l_ref[...] = jnp.sum(acc[loss_row:loss_row + SMALL_ROWS, :], axis=(0, 1), keepdims=True) * loss_scale

    grid_spec = pltpu.PrefetchScalarGridSpec(
        num_scalar_prefetch=1, grid=(1,),
        in_specs=[pl.BlockSpec((r, dm), lambda i, me_ref: (0, 0)), pl.BlockSpec((N_PEERS + 1, r, dm), lambda i, me_ref: (0, 0, 0))],
        out_specs=(pl.BlockSpec((r, dm), lambda i, me_ref: (0, 0)), pl.BlockSpec((1, 1), lambda i, me_ref: (0, 0))))
    return pl.pallas_call(
        body, name="reduce_sum", out_shape=(jax.ShapeDtypeStruct((r, dm), F32), jax.ShapeDtypeStruct((1, 1), F32)),
        grid_spec=grid_spec, compiler_params=_params(1),
    )(me, vec, zone)


def kernel(x, meta_tokens, norm_mix_g, w_in, b_gate, pool_w, pool_scale, conv_w, conv_out_w, w_o, norm_ffn_g, w_gate_up, w_down, norm_final_g, loss_target, m_meta_tokens, m_norm_mix_g, m_w_in, m_b_gate, m_pool_w, m_pool_scale, m_conv_w, m_conv_out_w, m_w_o, m_norm_ffn_g, m_w_gate_up, m_w_down, m_norm_final_g, v_meta_tokens, v_norm_mix_g, v_w_in, v_b_gate, v_pool_w, v_pool_scale, v_conv_w, v_conv_out_w, v_w_o, v_norm_ffn_g, v_w_gate_up, v_w_down, v_norm_final_g):
    seq, dm = x.shape[1], x.shape[2]
    tail = LANES
    tm = tail
    lp = seq + tail
    n_chips = 4
    n_groups = len(POOL_WINDOWS)
    gw = dm // n_groups
    tc = min(256, gw)
    cx, cy, cc = _place()
    chip = 2 * cx + cy
    dloc = dm // n_chips

    pool2 = pool_w.reshape(n_groups * pool_w.shape[1], gw)
    big = {"w_in": w_in, "w_gate_up": w_gate_up, "pool_w": pool2, "conv_out_w": conv_out_w, "w_o": w_o, "w_down": w_down}
    chip1 = jnp.reshape(chip, (1,)).astype(jnp.int32)
    core = jnp.reshape(cc, (1,)).astype(jnp.int32)
    small_loc = jnp.concatenate([meta_tokens, jnp.pad(conv_w, ((0, 8 - conv_w.shape[0]), (0, 0))),
                                 jnp.zeros((8, dloc), F32)], axis=0)
    first = [_cast_into_slot("cast_w_in", w_in, chip1, BF16), _cast_into_slot("place_small", small_loc, chip1, F32)]
    (sems0,), (w_in4, small4), token0 = _gather_start("gather_start_first", first, ([0, 1],))
    rest_names = ["pool_w", "conv_out_w", "w_o", "w_gate_up", "w_down"]
    rest = [_cast_into_slot("cast_" + nme, big[nme], chip1, BF16, deps=(token0,)) for nme in rest_names]
    groups = ([0, 1, 2], [3], [4])
    sems, rest, _ = _gather_start("gather_start_rest", rest, groups)

    def passing(g, after):
        return _gather_pass("gather_pass_%d" % g, [rest[a] for a in groups[g]], sems[g], after)

    g1, g2, g3 = norm_mix_g.reshape(1, dm), norm_ffn_g.reshape(1, dm), norm_final_g.reshape(1, dm)
    b_gate2 = b_gate.reshape(2, dm)
    ps = pool_scale.reshape(1, dm)
    w_in4, small4 = _sibling_exchange("sibling_exchange_first", list(_gather_wait("gather_wait_first", [w_in4, small4], sems0, rest[0])))
    small_f = jnp.transpose(small4, (1, 0, 2)).reshape(small4.shape[1], dm)
    meta_f = small_f[:N_META]
    conv_w_f = small_f[N_META:N_META + 3]
    h0 = jnp.concatenate([x[0], jnp.zeros((tail - N_META, dm), F32), meta_f], axis=0)
    hn1 = _rms_fwd("rms_mix", h0, g1, tm)
    proj = _nn_sharded("proj", hn1, w_in4, 6)
    pass_sems, passed = passing(0, proj)
    pooled, z = _mixer_fwd("mixer_fwd", proj, conv_w_f, tc)
    pool4, conv_out4, w_o4 = _pass_wait("pass_wait_0", passed, pass_sems, pooled)
    pool4 = pool4.reshape(n_chips, n_groups, gw // n_chips, gw)
    conv_out_f = conv_out4.reshape(dm, dm)
    w_o_f = w_o4.reshape(dm, dm)
    ya = _pool_fwd("pool_proj", pooled, pool4)
    yb = _nn_plain("conv_out", z, conv_out_f, BF16)
    mix = _gate_mix("gate_mix", proj, b_gate2, ya, ps, yb, tm)
    pass_sems, passed = passing(1, mix)
    h1 = _nn_plain("attn_out", mix, w_o_f, F32, res=h0, tn_pref=256)
    hn2 = _rms_fwd("rms_ffn", h1, g2, tm)
    (w_gu4,) = _pass_wait("pass_wait_1", passed, pass_sems, hn2)
    gu, act = _gate_up_swiglu("gate_up", hn2, w_gu4)
    pass_sems, passed = passing(2, act)
    (w_down4,) = _pass_wait("pass_wait_2", passed, pass_sems, act)
    w_down_f = w_down4.reshape(-1, dm)
    h2 = _nn_plain("ffn_down", act, w_down_f, F32, res=h1, tn_pref=512, tk_pref=1536)
    dh2, dh2b, loss_cols, dg3 = _final_loss("final_loss", h2, g3, loss_target[0], tm)

    def scatter(tag, names_g, swap, after):
        grads_g, got = _swap_wait("swap_wait_" + tag, *swap, [after])
        pairs = [_pair_add("pair_add_" + nme, g4, rv, core) for nme, g4, rv in zip(names_g, grads_g, got)]
        return _scatter_start("scatter_start_" + tag, pairs)

    dgu = _dact_swiglu_bwd("d_gate_up", dh2b, w_down_f, gu)
    gw_down = _tn_plain("dw_down", act, dh2b)
    gw_gu = _tn_sharded("dw_gate_up", hn2, dgu, n_chips)
    swap_a, token = _swap_start("swap_start_a", [gw_gu, gw_down.reshape(n_chips, -1, dm)])
    dhn2 = _nt_sharded("d_hn2", dgu, w_gu4, deps=(token,))
    flight_a, token = scatter("a", ["w_gate_up", "w_down"], swap_a, dhn2)
    dh1, dh1b, dg2 = _rms_bwd("rms_ffn_bwd", dhn2, h1, g2, dh2, tm, token)
    dmix = _nt_plain("d_mix", dh1b, w_o_f)
    gw_o = _tn_plain("dw_o", mix, dh1b)
    dproj, dyb, dya, db_gate, dps = _gate_bwd("gate_bwd", dmix, proj, b_gate2, ya, ps, yb, tm)
    gw_conv_out = _tn_plain("dw_conv_out", z, dyb)
    gw_pool = _pool_bwd_w("dw_pool", pooled, dya, n_chips)
    swap_b, token = _swap_start("swap_start_b", [gw_o.reshape(n_chips, dloc, dm), gw_conv_out.reshape(n_chips, dloc, dm),
                                                 gw_pool.reshape(n_chips, n_groups * (gw // n_chips), gw)])
    dpooled = _pool_bwd_act("d_pooled", dya, pool4, deps=(token,))
    dz = _nt_plain("d_z", dyb, conv_out_f)
    flight_b, token = scatter("b", ["w_o", "conv_out_w", "pool_w"], swap_b, dz)
    dproj, dconv_w = _mixer_bwd("mixer_bwd", dz, dpooled, proj, conv_w_f, dproj, tc, token)
    gw_in0 = _tn_sharded("dw_in_0", hn1, dproj, n_chips, part=(0, 2))
    swap_c0, token = _swap_start("swap_start_c0", [gw_in0])
    gw_in1 = _tn_sharded("dw_in_1", hn1, dproj, n_chips, part=(1, 2), deps=(token,))
    flight_c0, token = scatter("c0", ["w_in_0"], swap_c0, gw_in1)
    swap_c1, token = _swap_start("swap_start_c1", [gw_in1], deps=(token,))
    dhn1 = _nt_sharded("d_hn1", dproj, w_in4, deps=(token,))
    flight_c1, token = scatter("c1", ["w_in_1"], swap_c1, dhn1)
    dh0, _, dg1 = _rms_bwd("rms_mix_bwd", dhn1, h0, g1, dh1, tm, token)
    grad_x = dh0[:seq][None]
    dmeta = dh0[lp - N_META:]

    given = dict(meta_tokens=(meta_tokens, m_meta_tokens, v_meta_tokens), norm_mix_g=(norm_mix_g, m_norm_mix_g, v_norm_mix_g),
                 w_in=(w_in, m_w_in, v_w_in), b_gate=(b_gate, m_b_gate, v_b_gate), pool_w=(pool_w, m_pool_w, v_pool_w),
                 pool_scale=(pool_scale, m_pool_scale, v_pool_scale), conv_w=(conv_w, m_conv_w, v_conv_w),
                 conv_out_w=(conv_out_w, m_conv_out_w, v_conv_out_w), w_o=(w_o, m_w_o, v_w_o),
                 norm_ffn_g=(norm_ffn_g, m_norm_ffn_g, v_norm_ffn_g), w_gate_up=(w_gate_up, m_w_gate_up, v_w_gate_up),
                 w_down=(w_down, m_w_down, v_w_down), norm_final_g=(norm_final_g, m_norm_final_g, v_norm_final_g))
    order = list(given.keys())
    grad, delta, new_m, new_v = {}, {}, {}, {}
    vec = jnp.concatenate([dg1, dg2, dg3, db_gate, dps, loss_cols, dconv_w, dmeta], axis=0)
    loss_row = 5 * SMALL_ROWS
    groups_g = {"a": [("w_gate_up", (0, 1)), ("w_down", (0, 1))], "b": [("w_o", (0, 1)), ("conv_out_w", (0, 1)), ("pool_w", (0, 1))],
                "c0": [("w_in", (0, 2))], "c1": [("w_in", (1, 2))]}
    results = {}

    def reduced(tag, flight, after):
        pairs, zones = _scatter_wait("scatter_wait_" + tag, *flight, after)
        halves = [_chip_sum("chip_sum_%s_%d" % (nme, part[0]), p, rv, chip1) for (nme, part), p, rv in zip(groups_g[tag], pairs, zones)]
        return _swap_start("send_start_" + tag, halves, halves=False)

    def update(tag, send, after):
        halves, sib_halves = _swap_wait("send_wait_" + tag, *send, after, halves=False)
        deltas = []
        for (nme, part), g_own, g_sib in zip(groups_g[tag], halves, sib_halves):
            w, m, v = given[nme]
            shape2 = (2 * g_own.shape[0] * part[1], g_own.shape[1])
            results[nme] = _adamw_halves("adamw_%s_%d" % (nme, part[0]), w.reshape(shape2), g_own, g_sib, m.reshape(shape2),
                                         v.reshape(shape2), core, part=part, prev=results.get(nme))
            grad[nme], delta[nme], new_m[nme], new_v[nme] = [t.reshape(w.shape) for t in results[nme]]
            deltas.append(results[nme][1])
        return deltas

    send_a, token = reduced("a", flight_a, [dh0])
    send_b, token = reduced("b", flight_b, [token])
    done_a = update("a", send_a, [token])
    send_c0, token = reduced("c0", flight_c0, done_a)
    done_b = update("b", send_b, [token])
    send_c1, token = reduced("c1", flight_c1, done_b)
    me1 = jnp.reshape(4 * cx + 2 * cy + cc, (1,)).astype(jnp.int32)
    red_flight, token = _reduce_start(vec, [token])
    done_c0 = update("c0", send_c0, [token])
    done_c1 = update("c1", send_c1, done_c0)
    red, loss11 = _reduce_sum(*_reduce_wait(*red_flight, done_c1), me1, loss_row, 0.5 / dm)
    loss = loss11[0, 0]
    col0 = chip * dloc
    g_small = {
        "norm_mix_g": red[0], "norm_ffn_g": red[SMALL_ROWS], "norm_final_g": red[2 * SMALL_ROWS],
        "b_gate": red[3 * SMALL_ROWS:3 * SMALL_ROWS + 2].reshape(-1), "pool_scale": red[4 * SMALL_ROWS],
        "conv_w": lax.dynamic_slice(red, (6 * SMALL_ROWS, col0), (3, dloc)),
        "meta_tokens": lax.dynamic_slice(red, (7 * SMALL_ROWS, col0), (N_META, dloc)),
    }

    vec_names = ["norm_mix_g", "norm_ffn_g", "norm_final_g", "pool_scale"]

    def slab_vec(pick):
        rows = [pick(nme).reshape(1, dm) for nme in vec_names] + [pick("b_gate").reshape(2, dm), jnp.zeros((2, dm), F32)]
        return jnp.concatenate(rows, axis=0)

    def slab_col(pick):
        return jnp.concatenate([pick("meta_tokens"), pick("conv_w"), jnp.zeros((5, dloc), F32)], axis=0)

    for slab, tag in ((slab_vec, "vec"), (slab_col, "col")):
        d, nm, nv = _adamw("adamw_small_" + tag, slab(lambda nme: given[nme][0]), slab(lambda nme: g_small[nme]),
                           slab(lambda nme: given[nme][1]), slab(lambda nme: given[nme][2]))
        for out, res in ((delta, d), (new_m, nm), (new_v, nv)):
            if tag == "vec":
                for i, nme in enumerate(vec_names):
                    out[nme] = res[i]
                out["b_gate"] = res[4:6].reshape(-1)
            else:
                out["meta_tokens"] = res[:N_META]
                out["conv_w"] = res[N_META:N_META + 3]
    grad.update(g_small)
    return (loss, grad_x, *[grad[nme] for nme in order], *[delta[nme] for nme in order],
            *[new_m[nme] for nme in order], *[new_v[nme] for nme in order])
```

```python
import functools
import math

import jax
import jax.numpy as jnp
from jax import lax
from jax.experimental import pallas as pl
from jax.experimental.pallas import tpu as pltpu

F32 = jnp.float32
BF16 = jnp.bfloat16
N_META = 16
POOL_WINDOWS = (2, 4, 8, 16)
EPS = 1e-6
ADAM_LR, ADAM_B1, ADAM_B2, ADAM_EPS, ADAM_WD, ADAM_STEP = 0.001, 0.9, 0.999, 1e-08, 0.01, 10
LANES = 128
V7X_VMEM_BYTES = 64 * 1024 * 1024
VMEM_LIMIT = V7X_VMEM_BYTES - 8 * 1024 * 1024
MESH = pl.DeviceIdType.MESH
ANY = pl.BlockSpec(memory_space=pl.ANY)
CHIP_FLIPS = ((1, 0), (0, 1), (1, 1))
SMALL_ROWS = 8


def _pick(n, pref):
    best = None
    for t in range(LANES, min(n, pref) + 1, LANES):
        if n % t == 0:
            best = t
    assert best is not None, (n, pref)
    return best


def _params(n_axes=0):
    sem = ("arbitrary",) * n_axes if n_axes else None
    return pltpu.CompilerParams(dimension_semantics=sem, vmem_limit_bytes=VMEM_LIMIT)


_DIMS = {
    "nn": (((1,), (0,)), ((), ())),
    "nt": (((1,), (1,)), ((), ())),
    "tn": (((0,), (0,)), ((), ())),
}


def _matmul(name, mode, a, b, out_sds, grid, a_spec, b_spec, o_spec, nk, res=None, res_spec=None, acc_shape=None, deps=()):
    out_dtype = out_sds.dtype
    in_place = nk > 1 and out_dtype == F32
    use_scratch = nk > 1 and not in_place
    rows = a_spec.block_shape[-2] if mode != "tn" else None
    chunk = _row_tile(rows, 1, 1, 1152) if rows is not None else None
    n_in = 2 + (res is not None) + len(deps)

    def body(*refs):
        a_ref, b_ref = refs[:2]
        r_ref = refs[2] if res is not None else None
        o_ref, *scr = refs[n_in:]
        k = pl.program_id(len(grid) - 1) if nk > 1 else None

        def emit(sl):
            if sl is None:
                part = lax.dot_general(a_ref[...], b_ref[...], _DIMS[mode], preferred_element_type=F32)
                idx = (slice(None), slice(None))
            else:
                part = lax.dot_general(a_ref[sl, :], b_ref[...], _DIMS[mode], preferred_element_type=F32)
                idx = (sl, slice(None))
            if nk == 1:
                if r_ref is not None:
                    part = part + r_ref[idx]
                o_ref[idx] = part.astype(out_dtype)
                return
            acc = scr[0] if use_scratch else o_ref

            @pl.when(k == 0)
            def _():
                first = part
                if r_ref is not None and in_place:
                    first = first + r_ref[idx]
                acc[idx] = first

            @pl.when(k > 0)
            def _():
                acc[idx] += part

            if use_scratch:

                @pl.when(k == nk - 1)
                def _():
                    o_ref[idx] = acc[idx].astype(out_dtype)

        if mode == "tn" or chunk == rows:
            emit(None)
        else:
            for m0 in range(0, rows, chunk):
                emit(pl.ds(m0, chunk))

    ins = [a, b] + ([res] if res is not None else []) + list(deps)
    in_specs = [a_spec, b_spec] + ([res_spec] if res is not None else []) + [ANY] * len(deps)
    scratch = [pltpu.VMEM(acc_shape, F32)] if use_scratch else []
    return pl.pallas_call(
        body, name=name, out_shape=out_sds, grid=grid, in_specs=in_specs, out_specs=o_spec,
        scratch_shapes=scratch, compiler_params=_params(len(grid)),
    )(*ins)


def _nn_sharded(name, a, w4, nseg):
    lp, kdim = a.shape
    s, _, nloc = w4.shape
    segw = s * nloc // nseg
    tn = _pick(math.gcd(nloc, segw), 1536)
    bw, bo = nloc // tn, segw // tn
    return _matmul(
        name, "nn", a, w4, jax.ShapeDtypeStruct((nseg, lp, segw), BF16), (s * bw,),
        pl.BlockSpec((lp, kdim), lambda j: (0, 0)),
        pl.BlockSpec((None, kdim, tn), lambda j: (j // bw, 0, j % bw)),
        pl.BlockSpec((None, lp, tn), lambda j: (j // bo, 0, j % bo)), 1)


def _nn_plain(name, a, w, out_dtype, res=None, tn_pref=512, tk_pref=2048):
    lp, kdim = a.shape
    n = w.shape[1]
    tn = _pick(n, tn_pref)
    tk = kdim if kdim <= tk_pref else _pick(kdim, tk_pref)
    nk = kdim // tk
    grid = (n // tn, nk) if nk > 1 else (n // tn,)
    if nk > 1:
        a_spec = pl.BlockSpec((lp, tk), lambda j, k: (0, k))
        w_spec = pl.BlockSpec((tk, tn), lambda j, k: (k, j))
        o_spec = pl.BlockSpec((lp, tn), lambda j, k: (0, j))
    else:
        a_spec = pl.BlockSpec((lp, tk), lambda j: (0, 0))
        w_spec = pl.BlockSpec((tk, tn), lambda j: (0, j))
        o_spec = pl.BlockSpec((lp, tn), lambda j: (0, j))
    return _matmul(name, "nn", a, w, jax.ShapeDtypeStruct((lp, n), out_dtype), grid, a_spec, w_spec, o_spec, nk,
                   res=res, res_spec=o_spec if res is not None else None, acc_shape=(lp, tn))


def _nt_plain(name, a, w, tn_pref=512):
    lp, kdim = a.shape
    n = w.shape[0]
    tn = _pick(n, tn_pref)
    return _matmul(
        name, "nt", a, w, jax.ShapeDtypeStruct((lp, n), BF16), (n // tn,),
        pl.BlockSpec((lp, kdim), lambda j: (0, 0)),
        pl.BlockSpec((tn, kdim), lambda j: (j, 0)),
        pl.BlockSpec((lp, tn), lambda j: (0, j)), 1)


def _nt_sharded(name, dseg, w4, to_pref=1024, deps=()):
    nseg, lp, segw = dseg.shape
    s, kdim, nloc = w4.shape
    tr = _pick(math.gcd(nloc, segw), 1536)
    ba, bw = segw // tr, nloc // tr
    nr = s * bw
    to = _pick(kdim, to_pref)
    return _matmul(
        name, "nt", dseg, w4, jax.ShapeDtypeStruct((lp, kdim), F32), (kdim // to, nr),
        pl.BlockSpec((None, lp, tr), lambda j, r: (r // ba, 0, r % ba)),
        pl.BlockSpec((None, to, tr), lambda j, r: (r // bw, j, r % bw)),
        pl.BlockSpec((lp, to), lambda j, r: (0, j)), nr, deps=deps)


def _tn_plain(name, a, d, tk_pref=1024):
    lp, kdim = a.shape
    n = d.shape[1]
    tk = _pick(kdim, tk_pref)
    return _matmul(
        name, "tn", a, d, jax.ShapeDtypeStruct((kdim, n), BF16), (kdim // tk,),
        pl.BlockSpec((lp, tk), lambda i: (0, i)),
        pl.BlockSpec((lp, n), lambda i: (0, 0)),
        pl.BlockSpec((tk, n), lambda i: (i, 0)), 1)


def _tn_sharded(name, a, dseg, s, part=(0, 1), tk_pref=1024, deps=()):
    lp, kdim = a.shape
    nseg, _, segw = dseg.shape
    nloc = nseg * segw // s
    tn = _pick(math.gcd(nloc, segw), 1536)
    bd, bo = segw // tn, nloc // tn
    kpart = kdim // part[1]
    tk = _pick(kpart, tk_pref)
    i0 = part[0] * (kpart // tk)

    def body(a_ref, d_ref, *rest):
        o_ref, at_ref = rest[len(deps):]

        @pl.when(pl.program_id(1) == 0)
        def _():
            at_ref[...] = a_ref[...].T

        o_ref[...] = jnp.dot(at_ref[...], d_ref[...], preferred_element_type=F32).astype(BF16)

    return pl.pallas_call(
        body, name=name, out_shape=jax.ShapeDtypeStruct((s, kpart, nloc), BF16), grid=(kpart // tk, s * bo),
        in_specs=[pl.BlockSpec((lp, tk), lambda i, j: (0, i0 + i)),
                  pl.BlockSpec((None, lp, tn), lambda i, j: (j // bd, 0, j % bd))] + [ANY] * len(deps),
        out_specs=pl.BlockSpec((None, tk, tn), lambda i, j: (j // bo, i, j % bo)),
        scratch_shapes=[pltpu.VMEM((tk, lp), BF16)], compiler_params=_params(2),
    )(a, dseg, *deps)


def _silu_parts(gt):
    sg = jax.nn.sigmoid(gt)
    return gt * sg, sg * (1.0 + gt * (1.0 - sg))


def _gate_up_swiglu(name, a, w4, tn_pref=256):
    lp, kdim = a.shape
    s, _, nloc = w4.shape
    f = s * nloc // 2
    tn = _pick(nloc, tn_pref)
    bw = nloc // tn
    chunk = _row_tile(lp, 1, 1, 1152)

    def body(a_ref, wg_ref, wu_ref, gu_ref, act_ref):
        for m0 in range(0, lp, chunk):
            sl = pl.ds(m0, chunk)
            gt = jnp.dot(a_ref[sl, :], wg_ref[...], preferred_element_type=F32)
            up = jnp.dot(a_ref[sl, :], wu_ref[...], preferred_element_type=F32)
            gu_ref[0, sl, :] = gt.astype(BF16)
            gu_ref[1, sl, :] = up.astype(BF16)
            act_ref[sl, :] = (_silu_parts(gt)[0] * up).astype(BF16)

    return pl.pallas_call(
        body, name=name, grid=(f // tn,),
        out_shape=(jax.ShapeDtypeStruct((2, lp, f), BF16), jax.ShapeDtypeStruct((lp, f), BF16)),
        in_specs=[pl.BlockSpec((lp, kdim), lambda j: (0, 0)),
                  pl.BlockSpec((None, kdim, tn), lambda j: (j // bw, 0, j % bw)),
                  pl.BlockSpec((None, kdim, tn), lambda j: (s // 2 + j // bw, 0, j % bw))],
        out_specs=(pl.BlockSpec((2, lp, tn), lambda j: (0, 0, j)), pl.BlockSpec((lp, tn), lambda j: (0, j))),
        compiler_params=_params(1),
    )(a, w4, w4)


def _dact_swiglu_bwd(name, d, w, gu, tn_pref=512):
    lp, dm = d.shape
    f = w.shape[0]
    tn = _pick(f, tn_pref)
    chunk = _row_tile(lp, 1, 1, 1152)

    def body(d_ref, w_ref, g_ref, u_ref, o_ref):
        for m0 in range(0, lp, chunk):
            sl = pl.ds(m0, chunk)
            dact = lax.dot_general(d_ref[sl, :], w_ref[...], _DIMS["nt"], preferred_element_type=F32)
            silu, dsilu = _silu_parts(g_ref[sl, :].astype(F32))
            o_ref[0, sl, :] = (dact * u_ref[sl, :].astype(F32) * dsilu).astype(BF16)
            o_ref[1, sl, :] = (dact * silu).astype(BF16)

    return pl.pallas_call(
        body, name=name, grid=(f // tn,), out_shape=jax.ShapeDtypeStruct((2, lp, f), BF16),
        in_specs=[pl.BlockSpec((lp, dm), lambda j: (0, 0)), pl.BlockSpec((tn, dm), lambda j: (j, 0)),
                  pl.BlockSpec((None, lp, tn), lambda j: (0, 0, j)), pl.BlockSpec((None, lp, tn), lambda j: (1, 0, j))],
        out_specs=pl.BlockSpec((2, lp, tn), lambda j: (0, 0, j)), compiler_params=_params(1),
    )(d, w, gu, gu)


def _pool_fwd(name, pooled, pw):
    lp, dm = pooled.shape
    g, gw, _ = pw.shape
    return _matmul(
        name, "nn", pooled, pw, jax.ShapeDtypeStruct((lp, dm), BF16), (g,),
        pl.BlockSpec((lp, gw), lambda gi: (0, gi)), pl.BlockSpec((None, gw, gw), lambda gi: (gi, 0, 0)),
        pl.BlockSpec((lp, gw), lambda gi: (0, gi)), 1)


def _pool_bwd_act(name, dya, pw, deps=()):
    lp, dm = dya.shape
    g, gw, _ = pw.shape
    return _matmul(
        name, "nt", dya, pw, jax.ShapeDtypeStruct((lp, dm), BF16), (g,),
        pl.BlockSpec((lp, gw), lambda gi: (0, gi)), pl.BlockSpec((None, gw, gw), lambda gi: (gi, 0, 0)),
        pl.BlockSpec((lp, gw), lambda gi: (0, gi)), 1, deps=deps)


def _pool_bwd_w(name, pooled, dya):
    lp, dm = pooled.shape
    g = len(POOL_WINDOWS)
    gw = dm // g
    return _matmul(
        name, "tn", pooled, dya, jax.ShapeDtypeStruct((g, gw, gw), BF16), (g,),
        pl.BlockSpec((lp, gw), lambda gi: (0, gi)), pl.BlockSpec((lp, gw), lambda gi: (0, gi)),
        pl.BlockSpec((None, gw, gw), lambda gi: (gi, 0, 0)), 1)


def _rms_fwd(name, h, g, tm, deps=()):
    lp, dm = h.shape

    def body(h_ref, g_ref, *rest):
        hv = h_ref[...]
        r = lax.rsqrt(jnp.mean(hv * hv, axis=-1, keepdims=True) + EPS)
        rest[-1][...] = (hv * r * g_ref[...]).astype(BF16)

    row = pl.BlockSpec((tm, dm), lambda i: (i, 0))
    return pl.pallas_call(
        body, name=name, out_shape=jax.ShapeDtypeStruct((lp, dm), BF16), grid=(lp // tm,),
        in_specs=[row, pl.BlockSpec((1, dm), lambda i: (0, 0))] + [ANY] * len(deps), out_specs=row, compiler_params=_params(1),
    )(h, g, *deps)


def _rms_bwd(name, dy, h, g, dres, tm, dep):
    lp, dm = h.shape

    def body(dy_ref, h_ref, g_ref, dr_ref, _, dh_ref, dhb_ref, dg_ref):
        hv = h_ref[...]
        r = lax.rsqrt(jnp.mean(hv * hv, axis=-1, keepdims=True) + EPS)
        xhat = hv * r
        dyv = dy_ref[...]
        dxh = dyv * g_ref[...]
        dh = dr_ref[...] + r * (dxh - xhat * jnp.mean(dxh * xhat, axis=-1, keepdims=True))
        dh_ref[...] = dh
        dhb_ref[...] = dh.astype(BF16)

        @pl.when(pl.program_id(0) == 0)
        def _():
            dg_ref[...] = jnp.zeros_like(dg_ref)

        dg_ref[0:1, :] += jnp.sum(dyv * xhat, axis=0, keepdims=True)

    row = pl.BlockSpec((tm, dm), lambda i: (i, 0))
    slab = pl.BlockSpec((SMALL_ROWS, dm), lambda i: (0, 0))
    return pl.pallas_call(
        body, name=name, grid=(lp // tm,),
        out_shape=(jax.ShapeDtypeStruct((lp, dm), F32), jax.ShapeDtypeStruct((lp, dm), BF16),
                   jax.ShapeDtypeStruct((SMALL_ROWS, dm), F32)),
        in_specs=[row, row, pl.BlockSpec((1, dm), lambda i: (0, 0)), row, ANY], out_specs=(row, row, slab),
        compiler_params=_params(1),
    )(dy, h, g, dres, dep)


def _gate_mix(name, proj, b_gate2, ya, pool_scale, yb, tm):
    _, lp, dm = proj.shape

    def body(ga_ref, gr_ref, b_ref, ya_ref, ps_ref, yb_ref, o_ref):
        g_a = jax.nn.sigmoid(ga_ref[...].astype(F32) + b_ref[0:1, :])
        g_b = jax.nn.sigmoid(gr_ref[...].astype(F32) + b_ref[1:2, :])
        y_a = ya_ref[...].astype(F32) * ps_ref[...]
        o_ref[...] = (g_a * y_a + g_b * yb_ref[...].astype(F32)).astype(BF16)

    row = pl.BlockSpec((tm, dm), lambda i: (i, 0))
    return pl.pallas_call(
        body, name=name, out_shape=jax.ShapeDtypeStruct((lp, dm), BF16), grid=(lp // tm,),
        in_specs=[pl.BlockSpec((None, tm, dm), lambda i: (4, i, 0)), pl.BlockSpec((None, tm, dm), lambda i: (5, i, 0)),
                  pl.BlockSpec((2, dm), lambda i: (0, 0)), row, pl.BlockSpec((1, dm), lambda i: (0, 0)), row],
        out_specs=row, compiler_params=_params(1),
    )(proj, proj, b_gate2, ya, pool_scale, yb)


def _gate_bwd(name, dmix, proj, b_gate2, ya, pool_scale, yb, tm):
    _, lp, dm = proj.shape

    def body(dm_ref, ga_ref, gr_ref, b_ref, ya_ref, ps_ref, yb_ref, dp_ref, dyb_ref, dya_ref, db_ref, dps_ref):
        dmx = dm_ref[...].astype(F32)
        g_a = jax.nn.sigmoid(ga_ref[...].astype(F32) + b_ref[0:1, :])
        g_b = jax.nn.sigmoid(gr_ref[...].astype(F32) + b_ref[1:2, :])
        ya_pre = ya_ref[...].astype(F32)
        ybv = yb_ref[...].astype(F32)
        ps = ps_ref[...]
        dga = dmx * (ya_pre * ps) * (g_a * (1.0 - g_a))
        dgr = dmx * ybv * (g_b * (1.0 - g_b))
        dp_ref[0] = dga.astype(BF16)
        dp_ref[1] = dgr.astype(BF16)
        dyb_ref[...] = (dmx * g_b).astype(BF16)
        dya_ref[...] = (dmx * g_a * ps).astype(BF16)

        @pl.when(pl.program_id(0) == 0)
        def _():
            db_ref[...] = jnp.zeros_like(db_ref)
            dps_ref[...] = jnp.zeros_like(dps_ref)

        db_ref[0:1, :] += jnp.sum(dga, axis=0, keepdims=True)
        db_ref[1:2, :] += jnp.sum(dgr, axis=0, keepdims=True)
        dps_ref[0:1, :] += jnp.sum(dmx * g_a * ya_pre, axis=0, keepdims=True)

    row = pl.BlockSpec((tm, dm), lambda i: (i, 0))
    one = pl.BlockSpec((1, dm), lambda i: (0, 0))
    slab = pl.BlockSpec((SMALL_ROWS, dm), lambda i: (0, 0))
    return pl.pallas_call(
        body, name=name, grid=(lp // tm,),
        out_shape=(jax.ShapeDtypeStruct((6, lp, dm), BF16), jax.ShapeDtypeStruct((lp, dm), BF16),
                   jax.ShapeDtypeStruct((lp, dm), BF16), jax.ShapeDtypeStruct((SMALL_ROWS, dm), F32),
                   jax.ShapeDtypeStruct((SMALL_ROWS, dm), F32)),
        in_specs=[row, pl.BlockSpec((None, tm, dm), lambda i: (4, i, 0)), pl.BlockSpec((None, tm, dm), lambda i: (5, i, 0)),
                  pl.BlockSpec((2, dm), lambda i: (0, 0)), row, one, row],
        out_specs=(pl.BlockSpec((2, tm, dm), lambda i: (2, i, 0)), row, row, slab, slab),
        compiler_params=_params(1),
    )(dmix, proj, proj, b_gate2, ya, pool_scale, yb)


def _swiglu_fwd(name, gu, tm):
    _, lp, f = gu.shape

    def body(g_ref, u_ref, o_ref):
        gt = g_ref[...].astype(F32)
        o_ref[...] = (gt * jax.nn.sigmoid(gt) * u_ref[...].astype(F32)).astype(BF16)

    return pl.pallas_call(
        body, name=name, out_shape=jax.ShapeDtypeStruct((lp, f), BF16), grid=(lp // tm,),
        in_specs=[pl.BlockSpec((None, tm, f), lambda i: (0, i, 0)), pl.BlockSpec((None, tm, f), lambda i: (1, i, 0))],
        out_specs=pl.BlockSpec((tm, f), lambda i: (i, 0)), compiler_params=_params(1),
    )(gu, gu)


def _swiglu_bwd(name, dact, gu, tm):
    _, lp, f = gu.shape

    def body(d_ref, g_ref, u_ref, o_ref):
        d = d_ref[...].astype(F32)
        gt = g_ref[...].astype(F32)
        sg = jax.nn.sigmoid(gt)
        o_ref[0] = (d * u_ref[...].astype(F32) * (sg * (1.0 + gt * (1.0 - sg)))).astype(BF16)
        o_ref[1] = (d * (gt * sg)).astype(BF16)

    return pl.pallas_call(
        body, name=name, out_shape=jax.ShapeDtypeStruct((2, lp, f), BF16), grid=(lp // tm,),
        in_specs=[pl.BlockSpec((tm, f), lambda i: (i, 0)), pl.BlockSpec((None, tm, f), lambda i: (0, i, 0)),
                  pl.BlockSpec((None, tm, f), lambda i: (1, i, 0))],
        out_specs=pl.BlockSpec((2, tm, f), lambda i: (0, i, 0)), compiler_params=_params(1),
    )(dact, gu, gu)


def _final_loss(name, h2, g3, target, tm):
    lp, dm = h2.shape
    nx = target.shape[0] // tm

    def body(h_ref, g_ref, t_ref, dh_ref, dhb_ref, ls_ref, dg_ref):
        i = pl.program_id(0)

        @pl.when(i == 0)
        def _():
            ls_ref[...] = jnp.zeros_like(ls_ref)
            dg_ref[...] = jnp.zeros_like(dg_ref)

        @pl.when(i < nx)
        def _():
            hv = h_ref[...]
            gv = g_ref[...]
            r = lax.rsqrt(jnp.mean(hv * hv, axis=-1, keepdims=True) + EPS)
            xhat = hv * r
            err = xhat * gv - t_ref[...]
            dout = err * (1.0 / dm)
            dxh = dout * gv
            dh = r * (dxh - xhat * jnp.mean(dxh * xhat, axis=-1, keepdims=True))
            dh_ref[...] = dh
            dhb_ref[...] = dh.astype(BF16)
            ls_ref[0:1, :] += jnp.sum(err * err, axis=0, keepdims=True)
            dg_ref[0:1, :] += jnp.sum(dout * xhat, axis=0, keepdims=True)

        @pl.when(i >= nx)
        def _():
            dh_ref[...] = jnp.zeros_like(dh_ref)
            dhb_ref[...] = jnp.zeros_like(dhb_ref)

    row = pl.BlockSpec((tm, dm), lambda i: (i, 0))
    slab = pl.BlockSpec((SMALL_ROWS, dm), lambda i: (0, 0))
    return pl.pallas_call(
        body, name=name, grid=(lp // tm,),
        out_shape=(jax.ShapeDtypeStruct((lp, dm), F32), jax.ShapeDtypeStruct((lp, dm), BF16),
                   jax.ShapeDtypeStruct((SMALL_ROWS, dm), F32), jax.ShapeDtypeStruct((SMALL_ROWS, dm), F32)),
        in_specs=[row, pl.BlockSpec((1, dm), lambda i: (0, 0)), pl.BlockSpec((tm, dm), lambda i: (jnp.minimum(i, nx - 1), 0))],
        out_specs=(row, row, slab, slab), compiler_params=_params(1),
    )(h2, g3, target)


def _shift(v, k):
    return pltpu.roll(v, k % v.shape[0], axis=0)


def _window_sum(v, group, sign):
    s2 = v + _shift(v, sign * 1)
    s4 = s2 + _shift(s2, sign * 2)
    s8 = s4 + _shift(s4, sign * 4)
    s16 = s8 + _shift(s8, sign * 8)
    return jnp.where(group == 0, s2, jnp.where(group == 1, s4, jnp.where(group == 2, s8, s16)))


def _pool_count(lp, group):
    row = lax.broadcasted_iota(jnp.int32, (lp, 1), 0)
    window = jnp.left_shift(2, group).astype(F32)
    meta_pos = (row - (lp - N_META) + 1).astype(F32)
    return jnp.where(row >= lp - N_META, jnp.minimum(meta_pos, window), window)


def _mixer_fwd(name, proj, conv_w, tc, dep):
    _, lp, dm = proj.shape
    per_group = dm // len(POOL_WINDOWS) // tc

    def body(u_ref, gb_ref, gc_ref, v_ref, cw_ref, _, p_ref, z_ref):
        group = pl.program_id(0) // per_group
        u = u_ref[...].astype(F32)
        p_ref[...] = (_window_sum(u, group, 1) / _pool_count(lp, group) - u).astype(BF16)
        cv = gc_ref[...].astype(F32) * v_ref[...].astype(F32)
        conv = cw_ref[0:1, :] * _shift(cv, 2) + cw_ref[1:2, :] * _shift(cv, 1) + cw_ref[2:3, :] * cv
        z_ref[...] = (gb_ref[...].astype(F32) * conv).astype(BF16)

    def seg(s):
        return pl.BlockSpec((None, lp, tc), lambda j: (s, 0, j))

    col = pl.BlockSpec((lp, tc), lambda j: (0, j))
    return pl.pallas_call(
        body, name=name, grid=(dm // tc,),
        out_shape=(jax.ShapeDtypeStruct((lp, dm), BF16), jax.ShapeDtypeStruct((lp, dm), BF16)),
        in_specs=[seg(0), seg(1), seg(2), seg(3), pl.BlockSpec((3, tc), lambda j: (0, j)), ANY],
        out_specs=(col, col), compiler_params=_params(1),
    )(proj, proj, proj, proj, conv_w, dep)


def _mixer_bwd(name, dz, dpooled, proj, conv_w, dproj, tc, dep):
    _, lp, dm = proj.shape
    per_group = dm // len(POOL_WINDOWS) // tc

    def body(dz_ref, dp_ref, gb_ref, gc_ref, v_ref, cw_ref, _, __, o_ref, dcw_ref):
        group = pl.program_id(0) // per_group
        dzv = dz_ref[...].astype(F32)
        gb = gb_ref[...].astype(F32)
        gc = gc_ref[...].astype(F32)
        vv = v_ref[...].astype(F32)
        cv = gc * vv
        c1 = _shift(cv, 1)
        c2 = _shift(cv, 2)
        w0, w1, w2 = cw_ref[0:1, :], cw_ref[1:2, :], cw_ref[2:3, :]
        o_ref[1] = (dzv * (w0 * c2 + w1 * c1 + w2 * cv)).astype(BF16)
        dconv = dzv * gb
        dcw_ref[...] = jnp.zeros_like(dcw_ref)
        dcw_ref[0:1, :] = jnp.sum(dconv * c2, axis=0, keepdims=True)
        dcw_ref[1:2, :] = jnp.sum(dconv * c1, axis=0, keepdims=True)
        dcw_ref[2:3, :] = jnp.sum(dconv * cv, axis=0, keepdims=True)
        dcv = w0 * _shift(dconv, -2) + w1 * _shift(dconv, -1) + w2 * dconv
        o_ref[2] = (dcv * vv).astype(BF16)
        o_ref[3] = (dcv * gc).astype(BF16)
        dpv = dp_ref[...].astype(F32)
        o_ref[0] = (_window_sum(dpv / _pool_count(lp, group), group, -1) - dpv).astype(BF16)

    def seg(s):
        return pl.BlockSpec((None, lp, tc), lambda j: (s, 0, j))

    col = pl.BlockSpec((lp, tc), lambda j: (0, j))
    return pl.pallas_call(
        body, name=name, grid=(dm // tc,),
        out_shape=(jax.ShapeDtypeStruct(dproj.shape, BF16), jax.ShapeDtypeStruct((SMALL_ROWS, dm), F32)),
        in_specs=[col, col, seg(1), seg(2), seg(3), pl.BlockSpec((3, tc), lambda j: (0, j)), ANY, ANY],
        out_specs=(pl.BlockSpec((4, lp, tc), lambda j: (0, 0, j)), pl.BlockSpec((SMALL_ROWS, tc), lambda j: (0, j))),
        input_output_aliases={6: 0}, compiler_params=_params(1),
    )(dz, dpooled, proj, proj, proj, conv_w, dproj, dep)


def _row_tile(r, c, bytes_per_row_elem=4, budget=2 * 1024 * 1024):
    best = None
    for t in range(16, r + 1, 16):
        if r % t == 0 and t * c * bytes_per_row_elem <= budget:
            best = t
    return best if best is not None else r


def _pair_add(name, g4, recv, core):
    s, r, c = g4.shape
    h = r // 2
    tr = _row_tile(h, c, budget=6 * 1024 * 1024)
    nb = h // tr

    def body(core_ref, g_ref, r_ref, o_ref):
        o_ref[...] = (g_ref[...].astype(F32) + r_ref[...].astype(F32)).astype(BF16)

    grid_spec = pltpu.PrefetchScalarGridSpec(
        num_scalar_prefetch=1, grid=(s, nb),
        in_specs=[pl.BlockSpec((None, tr, c), lambda si, j, core_ref: (si, core_ref[0] * nb + j, 0)),
                  pl.BlockSpec((None, tr, c), lambda si, j, core_ref: (si, j, 0))],
        out_specs=pl.BlockSpec((None, tr, c), lambda si, j, core_ref: (si, j, 0)))
    return pl.pallas_call(
        body, name=name, out_shape=jax.ShapeDtypeStruct((s, h, c), BF16), grid_spec=grid_spec,
        compiler_params=_params(2),
    )(core, g4, recv)


def _chip_sum(name, parts, recv, chip):
    _, h, c = parts.shape
    tr = _row_tile(h, c)

    def body(chip_ref, p_ref, r_ref, o_ref):
        acc = p_ref[...].astype(F32)
        for i in range(len(CHIP_FLIPS)):
            acc = acc + r_ref[i].astype(F32)
        o_ref[...] = acc

    grid_spec = pltpu.PrefetchScalarGridSpec(
        num_scalar_prefetch=1, grid=(h // tr,),
        in_specs=[pl.BlockSpec((None, tr, c), lambda j, chip_ref: (chip_ref[0], j, 0)),
                  pl.BlockSpec((len(CHIP_FLIPS), tr, c), lambda j, chip_ref: (0, j, 0))],
        out_specs=pl.BlockSpec((tr, c), lambda j, chip_ref: (j, 0)))
    return pl.pallas_call(
        body, name=name, out_shape=jax.ShapeDtypeStruct((h, c), F32), grid_spec=grid_spec, compiler_params=_params(1),
    )(chip, parts, recv)


def _adam_update(w, gv, m, v):
    c1 = 1.0 - ADAM_B1 ** ADAM_STEP
    c2 = 1.0 - ADAM_B2 ** ADAM_STEP
    nm = ADAM_B1 * m + (1.0 - ADAM_B1) * gv
    nv = ADAM_B2 * v + (1.0 - ADAM_B2) * (gv * gv)
    return -ADAM_LR * ((nm / c1) / (jnp.sqrt(nv / c2) + ADAM_EPS) + ADAM_WD * w), nm, nv


def _adamw_halves(name, w, g_own, g_sib, m, v, core, part=(0, 1), prev=None):
    r, c = w.shape
    rp = r // part[1]
    h = rp // 2
    tr = _row_tile(h, c, budget=1024 * 1024)
    nbh = h // tr
    j0 = part[0] * 2 * nbh
    n_prev = 0 if prev is None else 4

    def body(core_ref, w_ref, go_ref, gs_ref, m_ref, v_ref, *rest):
        g_ref, d_ref, nm_ref, nv_ref = rest[n_prev:]
        mine = (pl.program_id(0) // nbh) == core_ref[0]
        gv = jnp.where(mine, go_ref[...], gs_ref[...])
        g_ref[...] = gv
        d_ref[...], nm_ref[...], nv_ref[...] = _adam_update(w_ref[...], gv, m_ref[...], v_ref[...])

    def blk(fn):
        return pl.BlockSpec((tr, c), fn)

    full = blk(lambda j, core_ref: (j0 + j, 0))
    own = blk(lambda j, core_ref: (jnp.clip(j - core_ref[0] * nbh, 0, nbh - 1), 0))
    sib = blk(lambda j, core_ref: (jnp.clip(j - (1 - core_ref[0]) * nbh, 0, nbh - 1), 0))
    grid_spec = pltpu.PrefetchScalarGridSpec(
        num_scalar_prefetch=1, grid=(2 * nbh,), in_specs=[full, own, sib, full, full] + [ANY] * n_prev, out_specs=(full,) * 4)
    sds = jax.ShapeDtypeStruct((r, c), F32)
    return pl.pallas_call(
        body, name=name, out_shape=(sds,) * 4, grid_spec=grid_spec, compiler_params=_params(1),
        input_output_aliases={6 + i: i for i in range(n_prev)},
    )(core, w, g_own, g_sib, m, v, *(prev or ()))


def _adamw(name, w, g, m, v):
    r, c = w.shape

    def body(w_ref, g_ref, m_ref, v_ref, d_ref, nm_ref, nv_ref):
        d_ref[...], nm_ref[...], nv_ref[...] = _adam_update(w_ref[...], g_ref[...], m_ref[...], v_ref[...])

    blk = pl.BlockSpec((r, c), lambda j: (0, 0))
    sds = jax.ShapeDtypeStruct((r, c), F32)
    return pl.pallas_call(
        body, name=name, out_shape=(sds, sds, sds), grid=(1,), in_specs=[blk] * 4, out_specs=(blk,) * 3,
        compiler_params=_params(1),
    )(w, g, m, v)


def _cast_into_slot(name, w, chip, dtype, deps=()):
    r, c = w.shape
    tr = _row_tile(r, c)

    def body(chip_ref, w_ref, *rest):
        rest[-1][...] = w_ref[...].astype(dtype)

    grid_spec = pltpu.PrefetchScalarGridSpec(
        num_scalar_prefetch=1, grid=(r // tr,),
        in_specs=[pl.BlockSpec((tr, c), lambda j, chip_ref: (j, 0))] + [ANY] * len(deps),
        out_specs=pl.BlockSpec((None, tr, c), lambda j, chip_ref: (chip_ref[0], j, 0)))
    return pl.pallas_call(
        body, name=name, out_shape=jax.ShapeDtypeStruct((4, r, c), dtype), grid_spec=grid_spec, compiler_params=_params(1),
    )(chip, w, *deps)


def _place():
    return lax.axis_index("x"), lax.axis_index("y"), lax.axis_index("c")


def _chip_of(x, y, flip):
    px, py = x ^ flip[0], y ^ flip[1]
    return px, py, 2 * px + py


def _half(ref, which):
    rows = ref.shape[0] // 2
    return ref.at[pl.ds(which * rows, rows)]


HBM = pl.BlockSpec(memory_space=pltpu.HBM)
SEM = pl.BlockSpec(memory_space=pltpu.SEMAPHORE)
SPLIT_COPY = pltpu.CompilerParams(has_side_effects=pltpu.SideEffectType.DATAFLOW_SIDE_EFFECTING)


def _in_hbm(arrays):
    return [pltpu.with_memory_space_constraint(t, pltpu.HBM) for t in arrays]


TOKEN = jax.ShapeDtypeStruct((SMALL_ROWS, LANES), F32)
TOKEN_SPEC = pl.BlockSpec(memory_space=pltpu.VMEM)


def _gather_start(name, slabs, groups):
    n = len(slabs)
    ng = len(groups)
    nf = len(CHIP_FLIPS)

    def body(*refs):
        sems, outs = refs[n:n + 2 * ng], refs[n + 2 * ng:2 * n + 2 * ng]
        token = refs[2 * n + 2 * ng]
        token[...] = jnp.zeros_like(token)
        x, y, c = _place()
        k = 2 * x + y
        for g, members in enumerate(groups):
            for i, a in enumerate(members):
                for j, flip in enumerate(CHIP_FLIPS):
                    px, py, _ = _chip_of(x, y, flip)
                    mine = _half(outs[a].at[k], c)
                    pltpu.make_async_remote_copy(
                        src_ref=mine, dst_ref=mine, send_sem=sems[2 * g].at[i * nf + j], recv_sem=sems[2 * g + 1].at[i * nf + j],
                        device_id=(px, py, c), device_id_type=MESH).start()

    sem_shapes = []
    for members in groups:
        sem_shapes += [pltpu.SemaphoreType.DMA((nf * len(members),))] * 2
    res = pl.pallas_call(
        body, name=name,
        out_shape=tuple(sem_shapes) + tuple(pltpu.HBM(t.shape, t.dtype) for t in slabs) + (TOKEN,),
        in_specs=[HBM] * n, out_specs=tuple([SEM] * (2 * ng) + [HBM] * n + [TOKEN_SPEC]),
        input_output_aliases={a: 2 * ng + a for a in range(n)}, compiler_params=SPLIT_COPY,
    )(*_in_hbm(slabs))
    return [(res[2 * g], res[2 * g + 1]) for g in range(ng)], list(res[2 * ng:2 * ng + n]), res[2 * ng + n]


def _gather_wait(name, slabs, sems, after):
    n = len(slabs)
    nf = len(CHIP_FLIPS)

    def body(*refs):
        ins = refs[:n]
        ssem, rsem = refs[n], refs[n + 1]
        x, y, c = _place()
        k = 2 * x + y
        for a in range(n):
            for j, flip in enumerate(CHIP_FLIPS):
                _, _, kj = _chip_of(x, y, flip)
                cp = pltpu.make_async_remote_copy(
                    src_ref=_half(ins[a].at[k], c), dst_ref=_half(ins[a].at[kj], c), send_sem=ssem.at[a * nf + j],
                    recv_sem=rsem.at[a * nf + j], device_id=(x, y, c), device_id_type=MESH)
                cp.wait_send()
                cp.wait_recv()

    return pl.pallas_call(
        body, name=name, out_shape=tuple(pltpu.HBM(t.shape, t.dtype) for t in slabs),
        in_specs=[HBM] * n + [SEM, SEM, ANY], out_specs=tuple([HBM] * n),
        input_output_aliases={a: a for a in range(n)}, compiler_params=SPLIT_COPY,
    )(*slabs, sems[0], sems[1], after)


def _gather_pass(name, slabs, sems, after):
    n = len(slabs)
    nf = len(CHIP_FLIPS)

    def body(*refs):
        ins = refs[:n]
        ssem, rsem = refs[n], refs[n + 1]
        ssem2, rsem2 = refs[n + 3], refs[n + 4]
        x, y, c = _place()
        k = 2 * x + y
        for a in range(n):
            for j, flip in enumerate(CHIP_FLIPS):
                _, _, kj = _chip_of(x, y, flip)
                landed = _half(ins[a].at[kj], c)
                cp = pltpu.make_async_remote_copy(
                    src_ref=_half(ins[a].at[k], c), dst_ref=landed, send_sem=ssem.at[a * nf + j],
                    recv_sem=rsem.at[a * nf + j], device_id=(x, y, c), device_id_type=MESH)
                cp.wait_send()
                cp.wait_recv()
                pltpu.make_async_remote_copy(
                    src_ref=landed, dst_ref=landed, send_sem=ssem2.at[a * nf + j], recv_sem=rsem2.at[a * nf + j],
                    device_id=(x, y, 1 - c), device_id_type=MESH).start()

    sem = pltpu.SemaphoreType.DMA((nf * n,))
    res = pl.pallas_call(
        body, name=name, out_shape=(sem, sem) + tuple(pltpu.HBM(t.shape, t.dtype) for t in slabs),
        in_specs=[HBM] * n + [SEM, SEM, ANY], out_specs=tuple([SEM, SEM] + [HBM] * n),
        input_output_aliases={a: 2 + a for a in range(n)}, compiler_params=SPLIT_COPY,
    )(*slabs, sems[0], sems[1], after)
    return (res[0], res[1]), list(res[2:])


def _pass_wait(name, slabs, sems, after):
    n = len(slabs)
    nf = len(CHIP_FLIPS)

    def body(*refs):
        ins = refs[:n]
        ssem, rsem = refs[n], refs[n + 1]
        x, y, c = _place()
        for a in range(n):
            for j, flip in enumerate(CHIP_FLIPS):
                _, _, kj = _chip_of(x, y, flip)
                cp = pltpu.make_async_remote_copy(
                    src_ref=_half(ins[a].at[kj], c), dst_ref=_half(ins[a].at[kj], 1 - c), send_sem=ssem.at[a * nf + j],
                    recv_sem=rsem.at[a * nf + j], device_id=(x, y, c), device_id_type=MESH)
                cp.wait_send()
                cp.wait_recv()

    return pl.pallas_call(
        body, name=name, out_shape=tuple(pltpu.HBM(t.shape, t.dtype) for t in slabs),
        in_specs=[HBM] * n + [SEM, SEM, ANY], out_specs=tuple([HBM] * n),
        input_output_aliases={a: a for a in range(n)}, compiler_params=SPLIT_COPY,
    )(*slabs, sems[0], sems[1], after)


NEIGHBOUR_FLIPS = CHIP_FLIPS[:2]


def _relay_chips(x, y, c):
    fx, fy = x ^ c, y ^ (1 - c)
    return (fx, fy), 2 * fx + fy, 2 * (1 - x) + (1 - y)


def _ag_start(name, slabs, deps=()):
    n = len(slabs)
    nn = len(NEIGHBOUR_FLIPS)

    def body(*refs):
        no = n + len(deps)
        ssem, rsem = refs[no], refs[no + 1]
        outs = refs[no + 2:no + 2 + n]
        token = refs[no + 2 + n]
        token[...] = jnp.zeros_like(token)
        x, y, c = _place()
        k = 2 * x + y
        for a in range(n):
            for j, flip in enumerate(NEIGHBOUR_FLIPS):
                px, py, _ = _chip_of(x, y, flip)
                mine = _half(outs[a].at[k], c)
                pltpu.make_async_remote_copy(src_ref=mine, dst_ref=mine, send_sem=ssem.at[a * nn + j],
                                             recv_sem=rsem.at[a * nn + j], device_id=(px, py, c), device_id_type=MESH).start()

    sem = pltpu.SemaphoreType.DMA((nn * n,))
    res = pl.pallas_call(
        body, name=name, out_shape=(sem, sem) + tuple(pltpu.HBM(t.shape, t.dtype) for t in slabs) + (TOKEN,),
        in_specs=[HBM] * n + [ANY] * len(deps), out_specs=tuple([SEM, SEM] + [HBM] * n + [TOKEN_SPEC]),
        input_output_aliases={a: 2 + a for a in range(n)}, compiler_params=SPLIT_COPY,
    )(*_in_hbm(slabs), *deps)
    return (res[0], res[1]), list(res[2:2 + n]), res[2 + n]


def _ag_relay(name, slabs, sems, after):
    n = len(slabs)
    nn = len(NEIGHBOUR_FLIPS)

    def body(*refs):
        no = n + 2 + len(after)
        ins = refs[:n]
        ssem, rsem = refs[n], refs[n + 1]
        r_s, r_r, p_s, p_r = refs[no:no + 4]
        x, y, c = _place()
        k = 2 * x + y
        (fx, fy), _, _ = _relay_chips(x, y, c)
        for a in range(n):
            for j, flip in enumerate(NEIGHBOUR_FLIPS):
                _, _, kj = _chip_of(x, y, flip)
                landed = _half(ins[a].at[kj], c)
                cp = pltpu.make_async_remote_copy(
                    src_ref=_half(ins[a].at[k], c), dst_ref=landed, send_sem=ssem.at[a * nn + j],
                    recv_sem=rsem.at[a * nn + j], device_id=(x, y, c), device_id_type=MESH)
                cp.wait_send()
                cp.wait_recv()
        for a in range(n):
            near = _half(ins[a].at[2 * (x ^ (1 - c)) + (y ^ c)], c)
            pltpu.make_async_remote_copy(src_ref=near, dst_ref=near, send_sem=r_s.at[a], recv_sem=r_r.at[a],
                                         device_id=(fx, fy, c), device_id_type=MESH).start()
            for j, flip in enumerate(NEIGHBOUR_FLIPS):
                _, _, kj = _chip_of(x, y, flip)
                landed = _half(ins[a].at[kj], c)
                pltpu.make_async_remote_copy(src_ref=landed, dst_ref=landed, send_sem=p_s.at[a * nn + j],
                                             recv_sem=p_r.at[a * nn + j], device_id=(x, y, 1 - c), device_id_type=MESH).start()

    rsem_t = pltpu.SemaphoreType.DMA((n,))
    psem_t = pltpu.SemaphoreType.DMA((nn * n,))
    res = pl.pallas_call(
        body, name=name, out_shape=(rsem_t, rsem_t, psem_t, psem_t) + tuple(pltpu.HBM(t.shape, t.dtype) for t in slabs),
        in_specs=[HBM] * n + [SEM, SEM] + [ANY] * len(after), out_specs=tuple([SEM] * 4 + [HBM] * n),
        input_output_aliases={a: 4 + a for a in range(n)}, compiler_params=SPLIT_COPY,
    )(*slabs, sems[0], sems[1], *after)
    return tuple(res[:4]), list(res[4:])


def _ag_relay_wait(name, slabs, sems, after):
    n = len(slabs)
    nn = len(NEIGHBOUR_FLIPS)

    def body(*refs):
        no = n + 4 + len(after)
        ins = refs[:n]
        r_s, r_r, p_s, p_r = refs[n:n + 4]
        f_s, f_r = refs[no], refs[no + 1]
        x, y, c = _place()
        _, _, kd = _relay_chips(x, y, c)
        for a in range(n):
            near = _half(ins[a].at[2 * (x ^ (1 - c)) + (y ^ c)], c)
            diag = _half(ins[a].at[kd], c)
            cp = pltpu.make_async_remote_copy(src_ref=near, dst_ref=diag, send_sem=r_s.at[a], recv_sem=r_r.at[a],
                                              device_id=(x, y, c), device_id_type=MESH)
            cp.wait_send()
            cp.wait_recv()
            for j, flip in enumerate(NEIGHBOUR_FLIPS):
                _, _, kj = _chip_of(x, y, flip)
                cp = pltpu.make_async_remote_copy(
                    src_ref=_half(ins[a].at[kj], c), dst_ref=_half(ins[a].at[kj], 1 - c), send_sem=p_s.at[a * nn + j],
                    recv_sem=p_r.at[a * nn + j], device_id=(x, y, c), device_id_type=MESH)
                cp.wait_send()
                cp.wait_recv()
            pltpu.make_async_remote_copy(src_ref=diag, dst_ref=diag, send_sem=f_s.at[a], recv_sem=f_r.at[a],
                                         device_id=(x, y, 1 - c), device_id_type=MESH).start()

    sem = pltpu.SemaphoreType.DMA((n,))
    res = pl.pallas_call(
        body, name=name, out_shape=(sem, sem) + tuple(pltpu.HBM(t.shape, t.dtype) for t in slabs),
        in_specs=[HBM] * n + [SEM] * 4 + [ANY] * len(after), out_specs=tuple([SEM, SEM] + [HBM] * n),
        input_output_aliases={a: 2 + a for a in range(n)}, compiler_params=SPLIT_COPY,
    )(*slabs, *sems, *after)
    return (res[0], res[1]), list(res[2:])


def _ag_final_wait(name, slabs, sems, after):
    n = len(slabs)

    def body(*refs):
        ins = refs[:n]
        f_s, f_r = refs[n], refs[n + 1]
        x, y, c = _place()
        _, _, kd = _relay_chips(x, y, c)
        for a in range(n):
            cp = pltpu.make_async_remote_copy(
                src_ref=_half(ins[a].at[kd], c), dst_ref=_half(ins[a].at[kd], 1 - c), send_sem=f_s.at[a], recv_sem=f_r.at[a],
                device_id=(x, y, c), device_id_type=MESH)
            cp.wait_send()
            cp.wait_recv()

    return pl.pallas_call(
        body, name=name, out_shape=tuple(pltpu.HBM(t.shape, t.dtype) for t in slabs),
        in_specs=[HBM] * n + [SEM, SEM] + [ANY] * len(after), out_specs=tuple([HBM] * n),
        input_output_aliases={a: a for a in range(n)}, compiler_params=SPLIT_COPY,
    )(*slabs, sems[0], sems[1], *after)


def _sibling_part(ref, c, halves):
    if not halves:
        return ref
    h = ref.shape[1] // 2
    return ref.at[:, pl.ds((1 - c) * h, h)]


def _swap_start(name, grads, halves=True, deps=()):
    n = len(grads)

    def body(*refs):
        no = 2 * n + len(deps)
        ssem, rsem = refs[no], refs[no + 1]
        src, land = refs[no + 2:no + n + 2], refs[no + n + 2:no + 2 * n + 2]
        token = refs[no + 2 * n + 2]
        token[...] = jnp.zeros_like(token)
        x, y, c = _place()
        for a in range(n):
            pltpu.make_async_remote_copy(
                src_ref=_sibling_part(src[a], c, halves), dst_ref=land[a], send_sem=ssem.at[a], recv_sem=rsem.at[a],
                device_id=(x, y, 1 - c), device_id_type=MESH).start()

    zones = [lax.empty((g.shape[0], g.shape[1] // 2, g.shape[2]) if halves else g.shape, g.dtype) for g in grads]
    sem = pltpu.SemaphoreType.DMA((n,))
    res = pl.pallas_call(
        body, name=name,
        out_shape=(sem, sem) + tuple(pltpu.HBM(t.shape, t.dtype) for t in list(grads) + zones) + (TOKEN,),
        in_specs=[HBM] * (2 * n) + [ANY] * len(deps), out_specs=tuple([SEM, SEM] + [HBM] * (2 * n) + [TOKEN_SPEC]),
        input_output_aliases={i: 2 + i for i in range(2 * n)}, compiler_params=SPLIT_COPY,
    )(*_in_hbm(list(grads) + zones), *deps)
    return (res[0], res[1], list(res[2:2 + n]), list(res[2 + n:2 + 2 * n])), res[2 + 2 * n]


def _swap_wait(name, ssem, rsem, grads, zones, after, halves=True):
    n = len(grads)

    def body(*refs):
        src, land = refs[:n], refs[n:2 * n]
        ss, rs = refs[2 * n], refs[2 * n + 1]
        x, y, c = _place()
        for a in range(n):
            cp = pltpu.make_async_remote_copy(
                src_ref=_sibling_part(src[a], c, halves), dst_ref=land[a], send_sem=ss.at[a], recv_sem=rs.at[a],
                device_id=(x, y, c), device_id_type=MESH)
            cp.wait_send()
            cp.wait_recv()

    res = pl.pallas_call(
        body, name=name, out_shape=tuple(pltpu.HBM(t.shape, t.dtype) for t in list(grads) + list(zones)),
        in_specs=[HBM] * (2 * n) + [SEM, SEM] + [ANY] * len(after), out_specs=tuple([HBM] * (2 * n)),
        input_output_aliases={i: i for i in range(2 * n)}, compiler_params=SPLIT_COPY,
    )(*grads, *zones, ssem, rsem, *after)
    return list(res[:n]), list(res[n:])


def _sibling_exchange(name, slabs):
    n = len(slabs)
    nf = len(CHIP_FLIPS)

    def body(*refs):
        outs = refs[n:2 * n]
        ssem, rsem = refs[2 * n:]
        x, y, c = _place()

        def copy(a, j, which, to):
            _, _, kj = _chip_of(x, y, CHIP_FLIPS[j])
            ref = _half(outs[a].at[kj], which)
            return pltpu.make_async_remote_copy(src_ref=ref, dst_ref=ref, send_sem=ssem.at[a * nf + j],
                                                recv_sem=rsem.at[a * nf + j], device_id=to, device_id_type=MESH)

        sends = [copy(a, j, c, (x, y, 1 - c)) for a in range(n) for j in range(nf)]
        for cp in sends:
            cp.start()
        for a in range(n):
            for j in range(nf):
                copy(a, j, 1 - c, (x, y, c)).wait_recv()
        for cp in sends:
            cp.wait_send()

    return pl.pallas_call(
        body, name=name, out_shape=tuple(jax.ShapeDtypeStruct(t.shape, t.dtype) for t in slabs),
        in_specs=[ANY] * n, out_specs=(ANY,) * n, input_output_aliases={a: a for a in range(n)},
        scratch_shapes=[pltpu.SemaphoreType.DMA((nf * n,)), pltpu.SemaphoreType.DMA((nf * n,))],
    )(*slabs)


def _sibling_swap(name, grads):
    n = len(grads)

    def body(*refs):
        ins, outs = refs[:n], refs[n:2 * n]
        ssem, rsem = refs[2 * n:]
        x, y, c = _place()
        cps = []
        for a in range(n):
            h = ins[a].shape[1] // 2
            cps.append(pltpu.make_async_remote_copy(
                src_ref=ins[a].at[:, pl.ds((1 - c) * h, h)], dst_ref=outs[a], send_sem=ssem.at[a], recv_sem=rsem.at[a],
                device_id=(x, y, 1 - c), device_id_type=MESH))
        for cp in cps:
            cp.start()
        for cp in cps:
            cp.wait()

    return pl.pallas_call(
        body, name=name,
        out_shape=tuple(jax.ShapeDtypeStruct((g.shape[0], g.shape[1] // 2, g.shape[2]), g.dtype) for g in grads),
        in_specs=[ANY] * n, out_specs=(ANY,) * n,
        scratch_shapes=[pltpu.SemaphoreType.DMA((n,)), pltpu.SemaphoreType.DMA((n,))],
    )(*grads)


def _scatter_start(name, parts):
    n = len(parts)
    nf = len(CHIP_FLIPS)

    def body(*refs):
        ssem, rsem = refs[2 * n], refs[2 * n + 1]
        src, land = refs[2 * n + 2:3 * n + 2], refs[3 * n + 2:4 * n + 2]
        token = refs[4 * n + 2]
        token[...] = jnp.zeros_like(token)
        x, y, c = _place()
        for a in range(n):
            for j, flip in enumerate(CHIP_FLIPS):
                px, py, kj = _chip_of(x, y, flip)
                pltpu.make_async_remote_copy(
                    src_ref=src[a].at[kj], dst_ref=land[a].at[j], send_sem=ssem.at[a * nf + j], recv_sem=rsem.at[a * nf + j],
                    device_id=(px, py, c), device_id_type=MESH).start()

    zones = [lax.empty((nf,) + p.shape[1:], p.dtype) for p in parts]
    sem = pltpu.SemaphoreType.DMA((nf * n,))
    res = pl.pallas_call(
        body, name=name,
        out_shape=(sem, sem) + tuple(pltpu.HBM(t.shape, t.dtype) for t in list(parts) + zones)
        + (jax.ShapeDtypeStruct((SMALL_ROWS, LANES), F32),),
        in_specs=[HBM] * (2 * n),
        out_specs=tuple([SEM, SEM] + [HBM] * (2 * n) + [pl.BlockSpec(memory_space=pltpu.VMEM)]),
        input_output_aliases={i: 2 + i for i in range(2 * n)}, compiler_params=SPLIT_COPY,
    )(*_in_hbm(list(parts) + zones))
    return (res[0], res[1], list(res[2:2 + n]), list(res[2 + n:2 + 2 * n])), res[2 + 2 * n]


def _scatter_wait(name, ssem, rsem, parts, zones, after):
    n = len(parts)
    nf = len(CHIP_FLIPS)

    def body(*refs):
        src, land = refs[:n], refs[n:2 * n]
        ss, rs = refs[2 * n], refs[2 * n + 1]
        x, y, c = _place()
        for a in range(n):
            for j, flip in enumerate(CHIP_FLIPS):
                _, _, kj = _chip_of(x, y, flip)
                cp = pltpu.make_async_remote_copy(
                    src_ref=src[a].at[kj], dst_ref=land[a].at[j], send_sem=ss.at[a * nf + j], recv_sem=rs.at[a * nf + j],
                    device_id=(x, y, c), device_id_type=MESH)
                cp.wait_send()
                cp.wait_recv()

    res = pl.pallas_call(
        body, name=name, out_shape=tuple(pltpu.HBM(t.shape, t.dtype) for t in list(parts) + list(zones)),
        in_specs=[HBM] * (2 * n) + [SEM, SEM] + [ANY] * len(after), out_specs=tuple([HBM] * (2 * n)),
        input_output_aliases={i: i for i in range(2 * n)}, compiler_params=SPLIT_COPY,
    )(*parts, *zones, ssem, rsem, *after)
    return list(res[:n]), list(res[n:])


N_PEERS = 7


def _peer(x, y, c, mask):
    px, py, pc = x ^ ((mask >> 2) & 1), y ^ ((mask >> 1) & 1), c ^ (mask & 1)
    return (px, py, pc), 4 * px + 2 * py + pc


def _reduce_start(vec, deps):
    nd = len(deps)

    def body(*refs):
        ssem, rsem, src, land, token = refs[2 + nd:]
        token[...] = jnp.zeros_like(token)
        x, y, c = _place()
        me = 4 * x + 2 * y + c
        for mask in range(1, N_PEERS + 1):
            to, _ = _peer(x, y, c, mask)
            pltpu.make_async_remote_copy(src_ref=src, dst_ref=land.at[me], send_sem=ssem.at[mask - 1],
                                         recv_sem=rsem.at[mask - 1], device_id=to, device_id_type=MESH).start()

    zone = lax.empty((N_PEERS + 1,) + vec.shape, vec.dtype)
    sem = pltpu.SemaphoreType.DMA((N_PEERS,))
    res = pl.pallas_call(
        body, name="reduce_start",
        out_shape=(sem, sem, pltpu.HBM(vec.shape, vec.dtype), pltpu.HBM(zone.shape, zone.dtype), TOKEN),
        in_specs=[HBM, HBM] + [ANY] * nd, out_specs=(SEM, SEM, HBM, HBM, TOKEN_SPEC),
        input_output_aliases={0: 2, 1: 3}, compiler_params=SPLIT_COPY,
    )(*_in_hbm([vec, zone]), *deps)
    return res[:4], res[4]


def _reduce_wait(ssem, rsem, vec, zone, after):
    def body(src, land, ss, rs, *_):
        x, y, c = _place()
        for mask in range(1, N_PEERS + 1):
            _, frm = _peer(x, y, c, mask)
            cp = pltpu.make_async_remote_copy(src_ref=src, dst_ref=land.at[frm], send_sem=ss.at[mask - 1],
                                              recv_sem=rs.at[mask - 1], device_id=(x, y, c), device_id_type=MESH)
            cp.wait_send()
            cp.wait_recv()

    return pl.pallas_call(
        body, name="reduce_wait", out_shape=(pltpu.HBM(vec.shape, vec.dtype), pltpu.HBM(zone.shape, zone.dtype)),
        in_specs=[HBM, HBM, SEM, SEM] + [ANY] * len(after), out_specs=(HBM, HBM),
        input_output_aliases={0: 0, 1: 1}, compiler_params=SPLIT_COPY,
    )(vec, zone, ssem, rsem, *after)


def _reduce_sum(vec, zone, me, loss_row, loss_scale):
    r, dm = vec.shape

    def body(me_ref, v_ref, z_ref, o_ref, l_ref):
        acc = None
        for i in range(N_PEERS + 1):
            term = jnp.where(me_ref[0] == i, v_ref[...], z_ref[i])
            acc = term if acc is None else acc + term
        o_ref[...] = acc
        l_ref[...] = jnp.sum(acc[loss_row:loss_row + SMALL_ROWS, :], axis=(0, 1), keepdims=True) * loss_scale

    grid_spec = pltpu.PrefetchScalarGridSpec(
        num_scalar_prefetch=1, grid=(1,),
        in_specs=[pl.BlockSpec((r, dm), lambda i, me_ref: (0, 0)), pl.BlockSpec((N_PEERS + 1, r, dm), lambda i, me_ref: (0, 0, 0))],
        out_specs=(pl.BlockSpec((r, dm), lambda i, me_ref: (0, 0)), pl.BlockSpec((1, 1), lambda i, me_ref: (0, 0))))
    return pl.pallas_call(
        body, name="reduce_sum", out_shape=(jax.ShapeDtypeStruct((r, dm), F32), jax.ShapeDtypeStruct((1, 1), F32)),
        grid_spec=grid_spec, compiler_params=_params(1),
    )(me, vec, zone)


def kernel(x, meta_tokens, norm_mix_g, w_in, b_gate, pool_w, pool_scale, conv_w, conv_out_w, w_o, norm_ffn_g, w_gate_up, w_down, norm_final_g, loss_target, m_meta_tokens, m_norm_mix_g, m_w_in, m_b_gate, m_pool_w, m_pool_scale, m_conv_w, m_conv_out_w, m_w_o, m_norm_ffn_g, m_w_gate_up, m_w_down, m_norm_final_g, v_meta_tokens, v_norm_mix_g, v_w_in, v_b_gate, v_pool_w, v_pool_scale, v_conv_w, v_conv_out_w, v_w_o, v_norm_ffn_g, v_w_gate_up, v_w_down, v_norm_final_g):
    seq, dm = x.shape[1], x.shape[2]
    tail = LANES
    tm = tail
    lp = seq + tail
    n_chips = 4
    n_groups = len(POOL_WINDOWS)
    gw = dm // n_groups
    tc = min(256, gw)
    cx, cy, cc = _place()
    chip = 2 * cx + cy
    dloc = dm // n_chips

    pool2 = pool_w.reshape(n_groups * pool_w.shape[1], gw)
    big = {"w_in": w_in, "w_gate_up": w_gate_up, "pool_w": pool2, "conv_out_w": conv_out_w, "w_o": w_o, "w_down": w_down}
    chip1 = jnp.reshape(chip, (1,)).astype(jnp.int32)
    core = jnp.reshape(cc, (1,)).astype(jnp.int32)
    small_loc = jnp.concatenate([meta_tokens, jnp.pad(conv_w, ((0, 8 - conv_w.shape[0]), (0, 0))),
                                 jnp.zeros((8, dloc), F32)], axis=0)
    g1, g2, g3 = norm_mix_g.reshape(1, dm), norm_ffn_g.reshape(1, dm), norm_final_g.reshape(1, dm)
    b_gate2 = b_gate.reshape(2, dm)
    ps = pool_scale.reshape(1, dm)
    first = [_cast_into_slot("cast_w_in", w_in, chip1, BF16), _cast_into_slot("place_small", small_loc, chip1, F32)]
    sems, first, token = _ag_start("ag_start_first", first)
    cast = {nme: _cast_into_slot("cast_" + nme, big[nme], chip1, BF16, deps=(token,))
            for nme in ["pool_w", "conv_out_w", "w_o", "w_gate_up", "w_down"]}
    sems, first = _ag_relay("ag_relay_first", first, sems, list(cast.values()))
    sems, first = _ag_relay_wait("ag_relay_wait_first", first, sems, [])
    w_in4, small4 = _ag_final_wait("ag_final_wait_first", first, sems, [])
    mixer_w = [cast["pool_w"], cast["conv_out_w"], cast["w_o"]]
    sems_mix, mixer_w, token = _ag_start("ag_start_mixer", mixer_w, deps=(w_in4,))
    sems_gu, (w_gu4,), token = _ag_start("ag_start_gate_up", [cast["w_gate_up"]], deps=(token,))

    small_f = jnp.transpose(small4, (1, 0, 2)).reshape(small4.shape[1], dm)
    meta_f = small_f[:N_META]
    conv_w_f = small_f[N_META:N_META + 3]
    h0 = jnp.concatenate([x[0], jnp.zeros((tail - N_META, dm), F32), meta_f], axis=0)
    hn1 = _rms_fwd("rms_mix", h0, g1, tm, deps=(token,))
    proj = _nn_sharded("proj", hn1, w_in4, 6)
    sems_mix, mixer_w = _ag_relay("ag_relay_mixer", mixer_w, sems_mix, [proj])
    sems_gu, (w_gu4,) = _ag_relay("ag_relay_gate_up", [w_gu4], sems_gu, [mixer_w[0]])
    sems_down, (w_down4,), token = _ag_start("ag_start_down", [cast["w_down"]], deps=(w_gu4,))
    pooled, z = _mixer_fwd("mixer_fwd", proj, conv_w_f, tc, token)
    sems_mix, mixer_w = _ag_relay_wait("ag_relay_wait_mixer", mixer_w, sems_mix, [pooled])
    pool4, conv_out4, w_o4 = _ag_final_wait("ag_final_wait_mixer", mixer_w, sems_mix, [])
    pool_f = jnp.transpose(pool4.reshape(n_chips, n_groups, gw // n_chips, gw), (1, 0, 2, 3)).reshape(n_groups, gw, gw)
    conv_out_f = conv_out4.reshape(dm, dm)
    w_o_f = w_o4.reshape(dm, dm)
    ya = _pool_fwd("pool_proj", pooled, pool_f)
    yb = _nn_plain("conv_out", z, conv_out_f, BF16)
    mix = _gate_mix("gate_mix", proj, b_gate2, ya, ps, yb, tm)
    sems_gu, (w_gu4,) = _ag_relay_wait("ag_relay_wait_gate_up", [w_gu4], sems_gu, [mix])
    h1 = _nn_plain("attn_out", mix, w_o_f, F32, res=h0, tn_pref=256)
    (w_gu4,) = _ag_final_wait("ag_final_wait_gate_up", [w_gu4], sems_gu, [h1])
    hn2 = _rms_fwd("rms_ffn", h1, g2, tm)
    sems_down, (w_down4,) = _ag_relay("ag_relay_down", [w_down4], sems_down, [hn2])
    gu, act = _gate_up_swiglu("gate_up", hn2, w_gu4)
    sems_down, (w_down4,) = _ag_relay_wait("ag_relay_wait_down", [w_down4], sems_down, [act])
    (w_down4,) = _ag_final_wait("ag_final_wait_down", [w_down4], sems_down, [])
    w_down_f = w_down4.reshape(-1, dm)
    h2 = _nn_plain("ffn_down", act, w_down_f, F32, res=h1, tn_pref=512, tk_pref=1536)
    dh2, dh2b, loss_cols, dg3 = _final_loss("final_loss", h2, g3, loss_target[0], tm)

    def scatter(tag, names_g, swap, after):
        grads_g, got = _swap_wait("swap_wait_" + tag, *swap, [after])
        pairs = [_pair_add("pair_add_" + nme, g4, rv, core) for nme, g4, rv in zip(names_g, grads_g, got)]
        return _scatter_start("scatter_start_" + tag, pairs)

    dgu = _dact_swiglu_bwd("d_gate_up", dh2b, w_down_f, gu)
    gw_down = _tn_plain("dw_down", act, dh2b)
    gw_gu = _tn_sharded("dw_gate_up", hn2, dgu, n_chips)
    swap_a, token = _swap_start("swap_start_a", [gw_gu, gw_down.reshape(n_chips, -1, dm)])
    dhn2 = _nt_sharded("d_hn2", dgu, w_gu4, deps=(token,))
    flight_a, token = scatter("a", ["w_gate_up", "w_down"], swap_a, dhn2)
    dh1, dh1b, dg2 = _rms_bwd("rms_ffn_bwd", dhn2, h1, g2, dh2, tm, token)
    dmix = _nt_plain("d_mix", dh1b, w_o_f)
    gw_o = _tn_plain("dw_o", mix, dh1b)
    dproj, dyb, dya, db_gate, dps = _gate_bwd("gate_bwd", dmix, proj, b_gate2, ya, ps, yb, tm)
    gw_conv_out = _tn_plain("dw_conv_out", z, dyb)
    gw_pool = _pool_bwd_w("dw_pool", pooled, dya)
    gw_pool = jnp.transpose(gw_pool.reshape(n_groups, n_chips, gw // n_chips, gw), (1, 0, 2, 3))
    swap_b, token = _swap_start("swap_start_b", [gw_o.reshape(n_chips, dloc, dm), gw_conv_out.reshape(n_chips, dloc, dm),
                                                 gw_pool.reshape(n_chips, n_groups * (gw // n_chips), gw)])
    dpooled = _pool_bwd_act("d_pooled", dya, pool_f, deps=(token,))
    dz = _nt_plain("d_z", dyb, conv_out_f)
    flight_b, token = scatter("b", ["w_o", "conv_out_w", "pool_w"], swap_b, dz)
    dproj, dconv_w = _mixer_bwd("mixer_bwd", dz, dpooled, proj, conv_w_f, dproj, tc, token)
    gw_in0 = _tn_sharded("dw_in_0", hn1, dproj, n_chips, part=(0, 2))
    swap_c0, token = _swap_start("swap_start_c0", [gw_in0])
    gw_in1 = _tn_sharded("dw_in_1", hn1, dproj, n_chips, part=(1, 2), deps=(token,))
    flight_c0, token = scatter("c0", ["w_in_0"], swap_c0, gw_in1)
    swap_c1, token = _swap_start("swap_start_c1", [gw_in1], deps=(token,))
    dhn1 = _nt_sharded("d_hn1", dproj, w_in4, deps=(token,))
    flight_c1, token = scatter("c1", ["w_in_1"], swap_c1, dhn1)
    dh0, _, dg1 = _rms_bwd("rms_mix_bwd", dhn1, h0, g1, dh1, tm, token)
    grad_x = dh0[:seq][None]
    dmeta = dh0[lp - N_META:]

    given = dict(meta_tokens=(meta_tokens, m_meta_tokens, v_meta_tokens), norm_mix_g=(norm_mix_g, m_norm_mix_g, v_norm_mix_g),
                 w_in=(w_in, m_w_in, v_w_in), b_gate=(b_gate, m_b_gate, v_b_gate), pool_w=(pool_w, m_pool_w, v_pool_w),
                 pool_scale=(pool_scale, m_pool_scale, v_pool_scale), conv_w=(conv_w, m_conv_w, v_conv_w),
                 conv_out_w=(conv_out_w, m_conv_out_w, v_conv_out_w), w_o=(w_o, m_w_o, v_w_o),
                 norm_ffn_g=(norm_ffn_g, m_norm_ffn_g, v_norm_ffn_g), w_gate_up=(w_gate_up, m_w_gate_up, v_w_gate_up),
                 w_down=(w_down, m_w_down, v_w_down), norm_final_g=(norm_final_g, m_norm_final_g, v_norm_final_g))
    order = list(given.keys())
    grad, delta, new_m, new_v = {}, {}, {}, {}
    vec = jnp.concatenate([dg1, dg2, dg3, db_gate, dps, loss_cols, dconv_w, dmeta], axis=0)
    loss_row = 5 * SMALL_ROWS
    groups_g = {"a": [("w_gate_up", (0, 1)), ("w_down", (0, 1))], "b": [("w_o", (0, 1)), ("conv_out_w", (0, 1)), ("pool_w", (0, 1))],
                "c0": [("w_in", (0, 2))], "c1": [("w_in", (1, 2))]}
    results = {}

    def reduced(tag, flight, after):
        pairs, zones = _scatter_wait("scatter_wait_" + tag, *flight, after)
        halves = [_chip_sum("chip_sum_%s_%d" % (nme, part[0]), p, rv, chip1) for (nme, part), p, rv in zip(groups_g[tag], pairs, zones)]
        return _swap_start("send_start_" + tag, halves, halves=False)

    def update(tag, send, after):
        halves, sib_halves = _swap_wait("send_wait_" + tag, *send, after, halves=False)
        deltas = []
        for (nme, part), g_own, g_sib in zip(groups_g[tag], halves, sib_halves):
            w, m, v = given[nme]
            shape2 = (2 * g_own.shape[0] * part[1], g_own.shape[1])
            results[nme] = _adamw_halves("adamw_%s_%d" % (nme, part[0]), w.reshape(shape2), g_own, g_sib, m.reshape(shape2),
                                         v.reshape(shape2), core, part=part, prev=results.get(nme))
            grad[nme], delta[nme], new_m[nme], new_v[nme] = [t.reshape(w.shape) for t in results[nme]]
            deltas.append(results[nme][1])
        return deltas

    send_a, token = reduced("a", flight_a, [dh0])
    send_b, token = reduced("b", flight_b, [token])
    done_a = update("a", send_a, [token])
    send_c0, token = reduced("c0", flight_c0, done_a)
    done_b = update("b", send_b, [token])
    send_c1, token = reduced("c1", flight_c1, done_b)
    me1 = jnp.reshape(4 * cx + 2 * cy + cc, (1,)).astype(jnp.int32)
    red_flight, token = _reduce_start(vec, [token])
    done_c0 = update("c0", send_c0, [token])
    done_c1 = update("c1", send_c1, done_c0)
    red, loss11 = _reduce_sum(*_reduce_wait(*red_flight, done_c1), me1, loss_row, 0.5 / dm)
    loss = loss11[0, 0]
    col0 = chip * dloc
    g_small = {
        "norm_mix_g": red[0], "norm_ffn_g": red[SMALL_ROWS], "norm_final_g": red[2 * SMALL_ROWS],
        "b_gate": red[3 * SMALL_ROWS:3 * SMALL_ROWS + 2].reshape(-1), "pool_scale": red[4 * SMALL_ROWS],
        "conv_w": lax.dynamic_slice(red, (6 * SMALL_ROWS, col0), (3, dloc)),
        "meta_tokens": lax.dynamic_slice(red, (7 * SMALL_ROWS, col0), (N_META, dloc)),
    }

    vec_names = ["norm_mix_g", "norm_ffn_g", "norm_final_g", "pool_scale"]

    def slab_vec(pick):
        rows = [pick(nme).reshape(1, dm) for nme in vec_names] + [pick("b_gate").reshape(2, dm), jnp.zeros((2, dm), F32)]
        return jnp.concatenate(rows, axis=0)

    def slab_col(pick):
        return jnp.concatenate([pick("meta_tokens"), pick("conv_w"), jnp.zeros((5, dloc), F32)], axis=0)

    for slab, tag in ((slab_vec, "vec"), (slab_col, "col")):
        d, nm, nv = _adamw("adamw_small_" + tag, slab(lambda nme: given[nme][0]), slab(lambda nme: g_small[nme]),
                           slab(lambda nme: given[nme][1]), slab(lambda nme: given[nme][2]))
        for out, res in ((delta, d), (new_m, nm), (new_v, nv)):
            if tag == "vec":
                for i, nme in enumerate(vec_names):
                    out[nme] = res[i]
                out["b_gate"] = res[4:6].reshape(-1)
            else:
                out["meta_tokens"] = res[:N_META]
                out["conv_w"] = res[N_META:N_META + 3]
    grad.update(g_small)
    return (loss, grad_x, *[grad[nme] for nme in order], *[delta[nme] for nme in order],
            *[new_m[nme] for nme in order], *[new_v[nme] for nme in order])
```

```python
import functools
import math

import jax
import jax.numpy as jnp
from jax import lax
from jax.experimental import pallas as pl
from jax.experimental.pallas import tpu as pltpu

F32 = jnp.float32
BF16 = jnp.bfloat16
N_META = 16
POOL_WINDOWS = (2, 4, 8, 16)
EPS = 1e-6
ADAM_LR, ADAM_B1, ADAM_B2, ADAM_EPS, ADAM_WD, ADAM_STEP = 0.001, 0.9, 0.999, 1e-08, 0.01, 10
LANES = 128
V7X_VMEM_BYTES = 64 * 1024 * 1024
VMEM_LIMIT = V7X_VMEM_BYTES - 8 * 1024 * 1024
MESH = pl.DeviceIdType.MESH
ANY = pl.BlockSpec(memory_space=pl.ANY)
CHIP_FLIPS = ((1, 0), (0, 1), (1, 1))
SMALL_ROWS = 8


def _pick(n, pref):
    best = None
    for t in range(LANES, min(n, pref) + 1, LANES):
        if n % t == 0:
            best = t
    assert best is not None, (n, pref)
    return best


def _params(n_axes=0):
    sem = ("arbitrary",) * n_axes if n_axes else None
    return pltpu.CompilerParams(dimension_semantics=sem, vmem_limit_bytes=VMEM_LIMIT)


_DIMS = {
    "nn": (((1,), (0,)), ((), ())),
    "nt": (((1,), (1,)), ((), ())),
    "tn": (((0,), (0,)), ((), ())),
}


def _matmul(name, mode, a, b, out_sds, grid, a_spec, b_spec, o_spec, nk, res=None, res_spec=None, acc_shape=None, deps=()):
    out_dtype = out_sds.dtype
    in_place = nk > 1 and out_dtype == F32
    use_scratch = nk > 1 and not in_place
    rows = a_spec.block_shape[-2] if mode != "tn" else None
    chunk = _row_tile(rows, 1, 1, 1152) if rows is not None else None
    n_in = 2 + (res is not None) + len(deps)

    def body(*refs):
        a_ref, b_ref = refs[:2]
        r_ref = refs[2] if res is not None else None
        o_ref, *scr = refs[n_in:]
        k = pl.program_id(len(grid) - 1) if nk > 1 else None

        def emit(sl):
            if sl is None:
                part = lax.dot_general(a_ref[...], b_ref[...], _DIMS[mode], preferred_element_type=F32)
                idx = (slice(None), slice(None))
            else:
                part = lax.dot_general(a_ref[sl, :], b_ref[...], _DIMS[mode], preferred_element_type=F32)
                idx = (sl, slice(None))
            if nk == 1:
                if r_ref is not None:
                    part = part + r_ref[idx]
                o_ref[idx] = part.astype(out_dtype)
                return
            acc = scr[0] if use_scratch else o_ref

            @pl.when(k == 0)
            def _():
                first = part
                if r_ref is not None and in_place:
                    first = first + r_ref[idx]
                acc[idx] = first

            @pl.when(k > 0)
            def _():
                acc[idx] += part

            if use_scratch:

                @pl.when(k == nk - 1)
                def _():
                    o_ref[idx] = acc[idx].astype(out_dtype)

        if mode == "tn" or chunk == rows:
            emit(None)
        else:
            for m0 in range(0, rows, chunk):
                emit(pl.ds(m0, chunk))

    ins = [a, b] + ([res] if res is not None else []) + list(deps)
    in_specs = [a_spec, b_spec] + ([res_spec] if res is not None else []) + [ANY] * len(deps)
    scratch = [pltpu.VMEM(acc_shape, F32)] if use_scratch else []
    return pl.pallas_call(
        body, name=name, out_shape=out_sds, grid=grid, in_specs=in_specs, out_specs=o_spec,
        scratch_shapes=scratch, compiler_params=_params(len(grid)),
    )(*ins)


def _nn_sharded(name, a, w4, nseg):
    lp, kdim = a.shape
    s, _, nloc = w4.shape
    segw = s * nloc // nseg
    tn = _pick(math.gcd(nloc, segw), 1536)
    bw, bo = nloc // tn, segw // tn
    return _matmul(
        name, "nn", a, w4, jax.ShapeDtypeStruct((nseg, lp, segw), BF16), (s * bw,),
        pl.BlockSpec((lp, kdim), lambda j: (0, 0)),
        pl.BlockSpec((None, kdim, tn), lambda j: (j // bw, 0, j % bw)),
        pl.BlockSpec((None, lp, tn), lambda j: (j // bo, 0, j % bo)), 1)


def _nn_plain(name, a, w, out_dtype, res=None, tn_pref=512, tk_pref=2048):
    lp, kdim = a.shape
    n = w.shape[1]
    tn = _pick(n, tn_pref)
    tk = kdim if kdim <= tk_pref else _pick(kdim, tk_pref)
    nk = kdim // tk
    grid = (n // tn, nk) if nk > 1 else (n // tn,)
    if nk > 1:
        a_spec = pl.BlockSpec((lp, tk), lambda j, k: (0, k))
        w_spec = pl.BlockSpec((tk, tn), lambda j, k: (k, j))
        o_spec = pl.BlockSpec((lp, tn), lambda j, k: (0, j))
    else:
        a_spec = pl.BlockSpec((lp, tk), lambda j: (0, 0))
        w_spec = pl.BlockSpec((tk, tn), lambda j: (0, j))
        o_spec = pl.BlockSpec((lp, tn), lambda j: (0, j))
    return _matmul(name, "nn", a, w, jax.ShapeDtypeStruct((lp, n), out_dtype), grid, a_spec, w_spec, o_spec, nk,
                   res=res, res_spec=o_spec if res is not None else None, acc_shape=(lp, tn))


def _nt_plain(name, a, w, tn_pref=512):
    lp, kdim = a.shape
    n = w.shape[0]
    tn = _pick(n, tn_pref)
    return _matmul(
        name, "nt", a, w, jax.ShapeDtypeStruct((lp, n), BF16), (n // tn,),
        pl.BlockSpec((lp, kdim), lambda j: (0, 0)),
        pl.BlockSpec((tn, kdim), lambda j: (j, 0)),
        pl.BlockSpec((lp, tn), lambda j: (0, j)), 1)


def _nt_sharded(name, dseg, w4, to_pref=1024, deps=()):
    nseg, lp, segw = dseg.shape
    s, kdim, nloc = w4.shape
    tr = _pick(math.gcd(nloc, segw), 1536)
    ba, bw = segw // tr, nloc // tr
    nr = s * bw
    to = _pick(kdim, to_pref)
    return _matmul(
        name, "nt", dseg, w4, jax.ShapeDtypeStruct((lp, kdim), F32), (kdim // to, nr),
        pl.BlockSpec((None, lp, tr), lambda j, r: (r // ba, 0, r % ba)),
        pl.BlockSpec((None, to, tr), lambda j, r: (r // bw, j, r % bw)),
        pl.BlockSpec((lp, to), lambda j, r: (0, j)), nr, deps=deps)


def _tn_plain(name, a, d, tk_pref=1024):
    lp, kdim = a.shape
    n = d.shape[1]
    tk = _pick(kdim, tk_pref)
    return _matmul(
        name, "tn", a, d, jax.ShapeDtypeStruct((kdim, n), BF16), (kdim // tk,),
        pl.BlockSpec((lp, tk), lambda i: (0, i)),
        pl.BlockSpec((lp, n), lambda i: (0, 0)),
        pl.BlockSpec((tk, n), lambda i: (i, 0)), 1)


def _tn_sharded(name, a, dseg, s, part=(0, 1), tk_pref=1024, deps=()):
    lp, kdim = a.shape
    nseg, _, segw = dseg.shape
    nloc = nseg * segw // s
    tn = _pick(math.gcd(nloc, segw), 1536)
    bd, bo = segw // tn, nloc // tn
    kpart = kdim // part[1]
    tk = _pick(kpart, tk_pref)
    i0 = part[0] * (kpart // tk)

    def body(a_ref, d_ref, *rest):
        o_ref, at_ref = rest[len(deps):]

        @pl.when(pl.program_id(1) == 0)
        def _():
            at_ref[...] = a_ref[...].T

        o_ref[...] = jnp.dot(at_ref[...], d_ref[...], preferred_element_type=F32).astype(BF16)

    return pl.pallas_call(
        body, name=name, out_shape=jax.ShapeDtypeStruct((s, kpart, nloc), BF16), grid=(kpart // tk, s * bo),
        in_specs=[pl.BlockSpec((lp, tk), lambda i, j: (0, i0 + i)),
                  pl.BlockSpec((None, lp, tn), lambda i, j: (j // bd, 0, j % bd))] + [ANY] * len(deps),
        out_specs=pl.BlockSpec((None, tk, tn), lambda i, j: (j // bo, i, j % bo)),
        scratch_shapes=[pltpu.VMEM((tk, lp), BF16)], compiler_params=_params(2),
    )(a, dseg, *deps)


def _silu_parts(gt):
    sg = jax.nn.sigmoid(gt)
    return gt * sg, sg * (1.0 + gt * (1.0 - sg))


def _gate_up_swiglu(name, a, w4, dep, tn_pref=256):
    lp, kdim = a.shape
    s, _, nloc = w4.shape
    f = s * nloc // 2
    tn = _pick(nloc, tn_pref)
    bw = nloc // tn
    chunk = _row_tile(lp, 1, 1, 1152)

    def body(a_ref, wg_ref, wu_ref, _, gu_ref, act_ref):
        for m0 in range(0, lp, chunk):
            sl = pl.ds(m0, chunk)
            gt = jnp.dot(a_ref[sl, :], wg_ref[...], preferred_element_type=F32)
            up = jnp.dot(a_ref[sl, :], wu_ref[...], preferred_element_type=F32)
            gu_ref[0, sl, :] = gt.astype(BF16)
            gu_ref[1, sl, :] = up.astype(BF16)
            act_ref[sl, :] = (_silu_parts(gt)[0] * up).astype(BF16)

    return pl.pallas_call(
        body, name=name, grid=(f // tn,),
        out_shape=(jax.ShapeDtypeStruct((2, lp, f), BF16), jax.ShapeDtypeStruct((lp, f), BF16)),
        in_specs=[pl.BlockSpec((lp, kdim), lambda j: (0, 0)),
                  pl.BlockSpec((None, kdim, tn), lambda j: (j // bw, 0, j % bw)),
                  pl.BlockSpec((None, kdim, tn), lambda j: (s // 2 + j // bw, 0, j % bw)), ANY],
        out_specs=(pl.BlockSpec((2, lp, tn), lambda j: (0, 0, j)), pl.BlockSpec((lp, tn), lambda j: (0, j))),
        compiler_params=_params(1),
    )(a, w4, w4, dep)


def _dact_swiglu_bwd(name, d, w, gu, tn_pref=512):
    lp, dm = d.shape
    f = w.shape[0]
    tn = _pick(f, tn_pref)
    chunk = _row_tile(lp, 1, 1, 1152)

    def body(d_ref, w_ref, g_ref, u_ref, o_ref):
        for m0 in range(0, lp, chunk):
            sl = pl.ds(m0, chunk)
            dact = lax.dot_general(d_ref[sl, :], w_ref[...], _DIMS["nt"], preferred_element_type=F32)
            silu, dsilu = _silu_parts(g_ref[sl, :].astype(F32))
            o_ref[0, sl, :] = (dact * u_ref[sl, :].astype(F32) * dsilu).astype(BF16)
            o_ref[1, sl, :] = (dact * silu).astype(BF16)

    return pl.pallas_call(
        body, name=name, grid=(f // tn,), out_shape=jax.ShapeDtypeStruct((2, lp, f), BF16),
        in_specs=[pl.BlockSpec((lp, dm), lambda j: (0, 0)), pl.BlockSpec((tn, dm), lambda j: (j, 0)),
                  pl.BlockSpec((None, lp, tn), lambda j: (0, 0, j)), pl.BlockSpec((None, lp, tn), lambda j: (1, 0, j))],
        out_specs=pl.BlockSpec((2, lp, tn), lambda j: (0, 0, j)), compiler_params=_params(1),
    )(d, w, gu, gu)


def _pool_fwd(name, pooled, pw):
    lp, dm = pooled.shape
    g, gw, _ = pw.shape
    return _matmul(
        name, "nn", pooled, pw, jax.ShapeDtypeStruct((lp, dm), BF16), (g,),
        pl.BlockSpec((lp, gw), lambda gi: (0, gi)), pl.BlockSpec((None, gw, gw), lambda gi: (gi, 0, 0)),
        pl.BlockSpec((lp, gw), lambda gi: (0, gi)), 1)


def _pool_bwd_act(name, dya, pw, deps=()):
    lp, dm = dya.shape
    g, gw, _ = pw.shape
    return _matmul(
        name, "nt", dya, pw, jax.ShapeDtypeStruct((lp, dm), BF16), (g,),
        pl.BlockSpec((lp, gw), lambda gi: (0, gi)), pl.BlockSpec((None, gw, gw), lambda gi: (gi, 0, 0)),
        pl.BlockSpec((lp, gw), lambda gi: (0, gi)), 1, deps=deps)


def _pool_bwd_w(name, pooled, dya):
    lp, dm = pooled.shape
    g = len(POOL_WINDOWS)
    gw = dm // g
    return _matmul(
        name, "tn", pooled, dya, jax.ShapeDtypeStruct((g, gw, gw), BF16), (g,),
        pl.BlockSpec((lp, gw), lambda gi: (0, gi)), pl.BlockSpec((lp, gw), lambda gi: (0, gi)),
        pl.BlockSpec((None, gw, gw), lambda gi: (gi, 0, 0)), 1)


def _rms_fwd(name, h, g, tm, deps=()):
    lp, dm = h.shape

    def body(h_ref, g_ref, *rest):
        hv = h_ref[...]
        r = lax.rsqrt(jnp.mean(hv * hv, axis=-1, keepdims=True) + EPS)
        rest[-1][...] = (hv * r * g_ref[...]).astype(BF16)

    row = pl.BlockSpec((tm, dm), lambda i: (i, 0))
    return pl.pallas_call(
        body, name=name, out_shape=jax.ShapeDtypeStruct((lp, dm), BF16), grid=(lp // tm,),
        in_specs=[row, pl.BlockSpec((1, dm), lambda i: (0, 0))] + [ANY] * len(deps), out_specs=row, compiler_params=_params(1),
    )(h, g, *deps)


def _rms_bwd(name, dy, h, g, dres, tm, dep):
    lp, dm = h.shape

    def body(dy_ref, h_ref, g_ref, dr_ref, _, dh_ref, dhb_ref, dg_ref):
        hv = h_ref[...]
        r = lax.rsqrt(jnp.mean(hv * hv, axis=-1, keepdims=True) + EPS)
        xhat = hv * r
        dyv = dy_ref[...]
        dxh = dyv * g_ref[...]
        dh = dr_ref[...] + r * (dxh - xhat * jnp.mean(dxh * xhat, axis=-1, keepdims=True))
        dh_ref[...] = dh
        dhb_ref[...] = dh.astype(BF16)

        @pl.when(pl.program_id(0) == 0)
        def _():
            dg_ref[...] = jnp.zeros_like(dg_ref)

        dg_ref[0:1, :] += jnp.sum(dyv * xhat, axis=0, keepdims=True)

    row = pl.BlockSpec((tm, dm), lambda i: (i, 0))
    slab = pl.BlockSpec((SMALL_ROWS, dm), lambda i: (0, 0))
    return pl.pallas_call(
        body, name=name, grid=(lp // tm,),
        out_shape=(jax.ShapeDtypeStruct((lp, dm), F32), jax.ShapeDtypeStruct((lp, dm), BF16),
                   jax.ShapeDtypeStruct((SMALL_ROWS, dm), F32)),
        in_specs=[row, row, pl.BlockSpec((1, dm), lambda i: (0, 0)), row, ANY], out_specs=(row, row, slab),
        compiler_params=_params(1),
    )(dy, h, g, dres, dep)


def _rms_bwd_input(name, dy, h, g, dres, tm, seq, dep):
    lp, dm = h.shape
    nx = seq // tm

    def body(dy_ref, h_ref, g_ref, dr_ref, _, dx_ref, dt_ref, dg_ref):
        i = pl.program_id(0)
        hv = h_ref[...]
        r = lax.rsqrt(jnp.mean(hv * hv, axis=-1, keepdims=True) + EPS)
        xhat = hv * r
        dyv = dy_ref[...]
        dxh = dyv * g_ref[...]
        dh = dr_ref[...] + r * (dxh - xhat * jnp.mean(dxh * xhat, axis=-1, keepdims=True))

        @pl.when(i < nx)
        def _():
            dx_ref[...] = dh

        @pl.when(i >= nx)
        def _():
            dt_ref[...] = dh

        @pl.when(i == 0)
        def _():
            dg_ref[...] = jnp.zeros_like(dg_ref)

        dg_ref[0:1, :] += jnp.sum(dyv * xhat, axis=0, keepdims=True)

    row = pl.BlockSpec((tm, dm), lambda i: (i, 0))
    slab = pl.BlockSpec((SMALL_ROWS, dm), lambda i: (0, 0))
    return pl.pallas_call(
        body, name=name, grid=(lp // tm,),
        out_shape=(jax.ShapeDtypeStruct((seq, dm), F32), jax.ShapeDtypeStruct((tm, dm), F32),
                   jax.ShapeDtypeStruct((SMALL_ROWS, dm), F32)),
        in_specs=[row, row, pl.BlockSpec((1, dm), lambda i: (0, 0)), row, ANY],
        out_specs=(pl.BlockSpec((tm, dm), lambda i: (jnp.minimum(i, nx - 1), 0)), pl.BlockSpec((tm, dm), lambda i: (0, 0)), slab),
        compiler_params=_params(1),
    )(dy, h, g, dres, dep)


def _gate_mix(name, proj, b_gate2, ya, pool_scale, yb, tm):
    _, lp, dm = proj.shape

    def body(ga_ref, gr_ref, b_ref, ya_ref, ps_ref, yb_ref, o_ref):
        g_a = jax.nn.sigmoid(ga_ref[...].astype(F32) + b_ref[0:1, :])
        g_b = jax.nn.sigmoid(gr_ref[...].astype(F32) + b_ref[1:2, :])
        y_a = ya_ref[...].astype(F32) * ps_ref[...]
        o_ref[...] = (g_a * y_a + g_b * yb_ref[...].astype(F32)).astype(BF16)

    row = pl.BlockSpec((tm, dm), lambda i: (i, 0))
    return pl.pallas_call(
        body, name=name, out_shape=jax.ShapeDtypeStruct((lp, dm), BF16), grid=(lp // tm,),
        in_specs=[pl.BlockSpec((None, tm, dm), lambda i: (4, i, 0)), pl.BlockSpec((None, tm, dm), lambda i: (5, i, 0)),
                  pl.BlockSpec((2, dm), lambda i: (0, 0)), row, pl.BlockSpec((1, dm), lambda i: (0, 0)), row],
        out_specs=row, compiler_params=_params(1),
    )(proj, proj, b_gate2, ya, pool_scale, yb)


def _gate_bwd(name, dmix, proj, b_gate2, ya, pool_scale, yb, tm):
    _, lp, dm = proj.shape

    def body(dm_ref, ga_ref, gr_ref, b_ref, ya_ref, ps_ref, yb_ref, dp_ref, dyb_ref, dya_ref, db_ref, dps_ref):
        dmx = dm_ref[...].astype(F32)
        g_a = jax.nn.sigmoid(ga_ref[...].astype(F32) + b_ref[0:1, :])
        g_b = jax.nn.sigmoid(gr_ref[...].astype(F32) + b_ref[1:2, :])
        ya_pre = ya_ref[...].astype(F32)
        ybv = yb_ref[...].astype(F32)
        ps = ps_ref[...]
        dga = dmx * (ya_pre * ps) * (g_a * (1.0 - g_a))
        dgr = dmx * ybv * (g_b * (1.0 - g_b))
        dp_ref[0] = dga.astype(BF16)
        dp_ref[1] = dgr.astype(BF16)
        dyb_ref[...] = (dmx * g_b).astype(BF16)
        dya_ref[...] = (dmx * g_a * ps).astype(BF16)

        @pl.when(pl.program_id(0) == 0)
        def _():
            db_ref[...] = jnp.zeros_like(db_ref)
            dps_ref[...] = jnp.zeros_like(dps_ref)

        db_ref[0:1, :] += jnp.sum(dga, axis=0, keepdims=True)
        db_ref[1:2, :] += jnp.sum(dgr, axis=0, keepdims=True)
        dps_ref[0:1, :] += jnp.sum(dmx * g_a * ya_pre, axis=0, keepdims=True)

    row = pl.BlockSpec((tm, dm), lambda i: (i, 0))
    one = pl.BlockSpec((1, dm), lambda i: (0, 0))
    slab = pl.BlockSpec((SMALL_ROWS, dm), lambda i: (0, 0))
    return pl.pallas_call(
        body, name=name, grid=(lp // tm,),
        out_shape=(jax.ShapeDtypeStruct((6, lp, dm), BF16), jax.ShapeDtypeStruct((lp, dm), BF16),
                   jax.ShapeDtypeStruct((lp, dm), BF16), jax.ShapeDtypeStruct((SMALL_ROWS, dm), F32),
                   jax.ShapeDtypeStruct((SMALL_ROWS, dm), F32)),
        in_specs=[row, pl.BlockSpec((None, tm, dm), lambda i: (4, i, 0)), pl.BlockSpec((None, tm, dm), lambda i: (5, i, 0)),
                  pl.BlockSpec((2, dm), lambda i: (0, 0)), row, one, row],
        out_specs=(pl.BlockSpec((2, tm, dm), lambda i: (2, i, 0)), row, row, slab, slab),
        compiler_params=_params(1),
    )(dmix, proj, proj, b_gate2, ya, pool_scale, yb)


def _swiglu_fwd(name, gu, tm):
    _, lp, f = gu.shape

    def body(g_ref, u_ref, o_ref):
        gt = g_ref[...].astype(F32)
        o_ref[...] = (gt * jax.nn.sigmoid(gt) * u_ref[...].astype(F32)).astype(BF16)

    return pl.pallas_call(
        body, name=name, out_shape=jax.ShapeDtypeStruct((lp, f), BF16), grid=(lp // tm,),
        in_specs=[pl.BlockSpec((None, tm, f), lambda i: (0, i, 0)), pl.BlockSpec((None, tm, f), lambda i: (1, i, 0))],
        out_specs=pl.BlockSpec((tm, f), lambda i: (i, 0)), compiler_params=_params(1),
    )(gu, gu)


def _swiglu_bwd(name, dact, gu, tm):
    _, lp, f = gu.shape

    def body(d_ref, g_ref, u_ref, o_ref):
        d = d_ref[...].astype(F32)
        gt = g_ref[...].astype(F32)
        sg = jax.nn.sigmoid(gt)
        o_ref[0] = (d * u_ref[...].astype(F32) * (sg * (1.0 + gt * (1.0 - sg)))).astype(BF16)
        o_ref[1] = (d * (gt * sg)).astype(BF16)

    return pl.pallas_call(
        body, name=name, out_shape=jax.ShapeDtypeStruct((2, lp, f), BF16), grid=(lp // tm,),
        in_specs=[pl.BlockSpec((tm, f), lambda i: (i, 0)), pl.BlockSpec((None, tm, f), lambda i: (0, i, 0)),
                  pl.BlockSpec((None, tm, f), lambda i: (1, i, 0))],
        out_specs=pl.BlockSpec((2, tm, f), lambda i: (0, i, 0)), compiler_params=_params(1),
    )(dact, gu, gu)


def _final_loss(name, h2, g3, target, tm):
    lp, dm = h2.shape
    nx = target.shape[0] // tm

    def body(h_ref, g_ref, t_ref, dh_ref, dhb_ref, ls_ref, dg_ref):
        i = pl.program_id(0)

        @pl.when(i == 0)
        def _():
            ls_ref[...] = jnp.zeros_like(ls_ref)
            dg_ref[...] = jnp.zeros_like(dg_ref)

        @pl.when(i < nx)
        def _():
            hv = h_ref[...]
            gv = g_ref[...]
            r = lax.rsqrt(jnp.mean(hv * hv, axis=-1, keepdims=True) + EPS)
            xhat = hv * r
            err = xhat * gv - t_ref[...]
            dout = err * (1.0 / dm)
            dxh = dout * gv
            dh = r * (dxh - xhat * jnp.mean(dxh * xhat, axis=-1, keepdims=True))
            dh_ref[...] = dh
            dhb_ref[...] = dh.astype(BF16)
            ls_ref[0:1, :] += jnp.sum(err * err, axis=0, keepdims=True)
            dg_ref[0:1, :] += jnp.sum(dout * xhat, axis=0, keepdims=True)

        @pl.when(i >= nx)
        def _():
            dh_ref[...] = jnp.zeros_like(dh_ref)
            dhb_ref[...] = jnp.zeros_like(dhb_ref)

    row = pl.BlockSpec((tm, dm), lambda i: (i, 0))
    slab = pl.BlockSpec((SMALL_ROWS, dm), lambda i: (0, 0))
    return pl.pallas_call(
        body, name=name, grid=(lp // tm,),
        out_shape=(jax.ShapeDtypeStruct((lp, dm), F32), jax.ShapeDtypeStruct((lp, dm), BF16),
                   jax.ShapeDtypeStruct((SMALL_ROWS, dm), F32), jax.ShapeDtypeStruct((SMALL_ROWS, dm), F32)),
        in_specs=[row, pl.BlockSpec((1, dm), lambda i: (0, 0)), pl.BlockSpec((tm, dm), lambda i: (jnp.minimum(i, nx - 1), 0))],
        out_specs=(row, row, slab, slab), compiler_params=_params(1),
    )(h2, g3, target)


def _shift(v, k):
    return pltpu.roll(v, k % v.shape[0], axis=0)


def _window_sum(v, group, sign):
    s2 = v + _shift(v, sign * 1)
    s4 = s2 + _shift(s2, sign * 2)
    s8 = s4 + _shift(s4, sign * 4)
    s16 = s8 + _shift(s8, sign * 8)
    return jnp.where(group == 0, s2, jnp.where(group == 1, s4, jnp.where(group == 2, s8, s16)))


def _pool_count(lp, group):
    row = lax.broadcasted_iota(jnp.int32, (lp, 1), 0)
    window = jnp.left_shift(2, group).astype(F32)
    meta_pos = (row - (lp - N_META) + 1).astype(F32)
    return jnp.where(row >= lp - N_META, jnp.minimum(meta_pos, window), window)


def _mixer_fwd(name, proj, conv_w, tc, dep):
    _, lp, dm = proj.shape
    per_group = dm // len(POOL_WINDOWS) // tc

    def body(u_ref, gb_ref, gc_ref, v_ref, cw_ref, _, p_ref, z_ref):
        group = pl.program_id(0) // per_group
        u = u_ref[...].astype(F32)
        p_ref[...] = (_window_sum(u, group, 1) / _pool_count(lp, group) - u).astype(BF16)
        cv = gc_ref[...].astype(F32) * v_ref[...].astype(F32)
        conv = cw_ref[0:1, :] * _shift(cv, 2) + cw_ref[1:2, :] * _shift(cv, 1) + cw_ref[2:3, :] * cv
        z_ref[...] = (gb_ref[...].astype(F32) * conv).astype(BF16)

    def seg(s):
        return pl.BlockSpec((None, lp, tc), lambda j: (s, 0, j))

    col = pl.BlockSpec((lp, tc), lambda j: (0, j))
    return pl.pallas_call(
        body, name=name, grid=(dm // tc,),
        out_shape=(jax.ShapeDtypeStruct((lp, dm), BF16), jax.ShapeDtypeStruct((lp, dm), BF16)),
        in_specs=[seg(0), seg(1), seg(2), seg(3), pl.BlockSpec((3, tc), lambda j: (0, j)), ANY],
        out_specs=(col, col), compiler_params=_params(1),
    )(proj, proj, proj, proj, conv_w, dep)


def _mixer_bwd(name, dz, dpooled, proj, conv_w, dproj, tc, dep):
    _, lp, dm = proj.shape
    per_group = dm // len(POOL_WINDOWS) // tc

    def body(dz_ref, dp_ref, gb_ref, gc_ref, v_ref, cw_ref, _, __, o_ref, dcw_ref):
        group = pl.program_id(0) // per_group
        dzv = dz_ref[...].astype(F32)
        gb = gb_ref[...].astype(F32)
        gc = gc_ref[...].astype(F32)
        vv = v_ref[...].astype(F32)
        cv = gc * vv
        c1 = _shift(cv, 1)
        c2 = _shift(cv, 2)
        w0, w1, w2 = cw_ref[0:1, :], cw_ref[1:2, :], cw_ref[2:3, :]
        o_ref[1] = (dzv * (w0 * c2 + w1 * c1 + w2 * cv)).astype(BF16)
        dconv = dzv * gb
        dcw_ref[...] = jnp.zeros_like(dcw_ref)
        dcw_ref[0:1, :] = jnp.sum(dconv * c2, axis=0, keepdims=True)
        dcw_ref[1:2, :] = jnp.sum(dconv * c1, axis=0, keepdims=True)
        dcw_ref[2:3, :] = jnp.sum(dconv * cv, axis=0, keepdims=True)
        dcv = w0 * _shift(dconv, -2) + w1 * _shift(dconv, -1) + w2 * dconv
        o_ref[2] = (dcv * vv).astype(BF16)
        o_ref[3] = (dcv * gc).astype(BF16)
        dpv = dp_ref[...].astype(F32)
        o_ref[0] = (_window_sum(dpv / _pool_count(lp, group), group, -1) - dpv).astype(BF16)

    def seg(s):
        return pl.BlockSpec((None, lp, tc), lambda j: (s, 0, j))

    col = pl.BlockSpec((lp, tc), lambda j: (0, j))
    return pl.pallas_call(
        body, name=name, grid=(dm // tc,),
        out_shape=(jax.ShapeDtypeStruct(dproj.shape, BF16), jax.ShapeDtypeStruct((SMALL_ROWS, dm), F32)),
        in_specs=[col, col, seg(1), seg(2), seg(3), pl.BlockSpec((3, tc), lambda j: (0, j)), ANY, ANY],
        out_specs=(pl.BlockSpec((4, lp, tc), lambda j: (0, 0, j)), pl.BlockSpec((SMALL_ROWS, tc), lambda j: (0, j))),
        input_output_aliases={6: 0}, compiler_params=_params(1),
    )(dz, dpooled, proj, proj, proj, conv_w, dproj, dep)


def _row_tile(r, c, bytes_per_row_elem=4, budget=2 * 1024 * 1024):
    best = None
    for t in range(16, r + 1, 16):
        if r % t == 0 and t * c * bytes_per_row_elem <= budget:
            best = t
    return best if best is not None else r


def _pair_add(name, g4, recv, core):
    s, r, c = g4.shape
    h = r // 2
    tr = _row_tile(h, c, budget=6 * 1024 * 1024)
    nb = h // tr

    def body(core_ref, g_ref, r_ref, o_ref):
        o_ref[...] = (g_ref[...].astype(F32) + r_ref[...].astype(F32)).astype(BF16)

    grid_spec = pltpu.PrefetchScalarGridSpec(
        num_scalar_prefetch=1, grid=(s, nb),
        in_specs=[pl.BlockSpec((None, tr, c), lambda si, j, core_ref: (si, core_ref[0] * nb + j, 0)),
                  pl.BlockSpec((None, tr, c), lambda si, j, core_ref: (si, j, 0))],
        out_specs=pl.BlockSpec((None, tr, c), lambda si, j, core_ref: (si, j, 0)))
    return pl.pallas_call(
        body, name=name, out_shape=jax.ShapeDtypeStruct((s, h, c), BF16), grid_spec=grid_spec,
        compiler_params=_params(2),
    )(core, g4, recv)


def _chip_sum(name, parts, recv, chip):
    _, h, c = parts.shape
    tr = _row_tile(h, c)

    def body(chip_ref, p_ref, r_ref, o_ref):
        acc = p_ref[...].astype(F32)
        for i in range(len(CHIP_FLIPS)):
            acc = acc + r_ref[i].astype(F32)
        o_ref[...] = acc

    grid_spec = pltpu.PrefetchScalarGridSpec(
        num_scalar_prefetch=1, grid=(h // tr,),
        in_specs=[pl.BlockSpec((None, tr, c), lambda j, chip_ref: (chip_ref[0], j, 0)),
                  pl.BlockSpec((len(CHIP_FLIPS), tr, c), lambda j, chip_ref: (0, j, 0))],
        out_specs=pl.BlockSpec((tr, c), lambda j, chip_ref: (j, 0)))
    return pl.pallas_call(
        body, name=name, out_shape=jax.ShapeDtypeStruct((h, c), F32), grid_spec=grid_spec, compiler_params=_params(1),
    )(chip, parts, recv)


def _adam_update(w, gv, m, v):
    c1 = 1.0 - ADAM_B1 ** ADAM_STEP
    c2 = 1.0 - ADAM_B2 ** ADAM_STEP
    nm = ADAM_B1 * m + (1.0 - ADAM_B1) * gv
    nv = ADAM_B2 * v + (1.0 - ADAM_B2) * (gv * gv)
    return -ADAM_LR * ((nm / c1) / (jnp.sqrt(nv / c2) + ADAM_EPS) + ADAM_WD * w), nm, nv


def _adamw_halves(name, w, g_own, g_sib, m, v, core, part=(0, 1), prev=None):
    r, c = w.shape
    rp = r // part[1]
    h = rp // 2
    tr = _row_tile(h, c, budget=1024 * 1024)
    nbh = h // tr
    j0 = part[0] * 2 * nbh
    n_prev = 0 if prev is None else 4

    def body(core_ref, w_ref, go_ref, gs_ref, m_ref, v_ref, *rest):
        g_ref, d_ref, nm_ref, nv_ref = rest[n_prev:]
        mine = (pl.program_id(0) // nbh) == core_ref[0]
        gv = jnp.where(mine, go_ref[...], gs_ref[...])
        g_ref[...] = gv
        d_ref[...], nm_ref[...], nv_ref[...] = _adam_update(w_ref[...], gv, m_ref[...], v_ref[...])

    def blk(fn):
        return pl.BlockSpec((tr, c), fn)

    full = blk(lambda j, core_ref: (j0 + j, 0))
    own = blk(lambda j, core_ref: (jnp.clip(j - core_ref[0] * nbh, 0, nbh - 1), 0))
    sib = blk(lambda j, core_ref: (jnp.clip(j - (1 - core_ref[0]) * nbh, 0, nbh - 1), 0))
    grid_spec = pltpu.PrefetchScalarGridSpec(
        num_scalar_prefetch=1, grid=(2 * nbh,), in_specs=[full, own, sib, full, full] + [ANY] * n_prev, out_specs=(full,) * 4)
    sds = jax.ShapeDtypeStruct((r, c), F32)
    return pl.pallas_call(
        body, name=name, out_shape=(sds,) * 4, grid_spec=grid_spec, compiler_params=_params(1),
        input_output_aliases={6 + i: i for i in range(n_prev)},
    )(core, w, g_own, g_sib, m, v, *(prev or ()))


def _adamw(name, w, g, m, v):
    r, c = w.shape

    def body(w_ref, g_ref, m_ref, v_ref, d_ref, nm_ref, nv_ref):
        d_ref[...], nm_ref[...], nv_ref[...] = _adam_update(w_ref[...], g_ref[...], m_ref[...], v_ref[...])

    blk = pl.BlockSpec((r, c), lambda j: (0, 0))
    sds = jax.ShapeDtypeStruct((r, c), F32)
    return pl.pallas_call(
        body, name=name, out_shape=(sds, sds, sds), grid=(1,), in_specs=[blk] * 4, out_specs=(blk,) * 3,
        compiler_params=_params(1),
    )(w, g, m, v)


def _cast_into_slot(name, w, chip, dtype, deps=()):
    r, c = w.shape
    tr = _row_tile(r, c)

    def body(chip_ref, w_ref, *rest):
        rest[-1][...] = w_ref[...].astype(dtype)

    grid_spec = pltpu.PrefetchScalarGridSpec(
        num_scalar_prefetch=1, grid=(r // tr,),
        in_specs=[pl.BlockSpec((tr, c), lambda j, chip_ref: (j, 0))] + [ANY] * len(deps),
        out_specs=pl.BlockSpec((None, tr, c), lambda j, chip_ref: (chip_ref[0], j, 0)))
    return pl.pallas_call(
        body, name=name, out_shape=jax.ShapeDtypeStruct((4, r, c), dtype), grid_spec=grid_spec, compiler_params=_params(1),
    )(chip, w, *deps)


def _place():
    return lax.axis_index("x"), lax.axis_index("y"), lax.axis_index("c")


def _chip_of(x, y, flip):
    px, py = x ^ flip[0], y ^ flip[1]
    return px, py, 2 * px + py


def _half(ref, which):
    rows = ref.shape[0] // 2
    return ref.at[pl.ds(which * rows, rows)]


HBM = pl.BlockSpec(memory_space=pltpu.HBM)
SEM = pl.BlockSpec(memory_space=pltpu.SEMAPHORE)
SPLIT_COPY = pltpu.CompilerParams(has_side_effects=pltpu.SideEffectType.DATAFLOW_SIDE_EFFECTING)


def _in_hbm(arrays):
    return [pltpu.with_memory_space_constraint(t, pltpu.HBM) for t in arrays]


TOKEN = jax.ShapeDtypeStruct((SMALL_ROWS, LANES), F32)
TOKEN_SPEC = pl.BlockSpec(memory_space=pltpu.VMEM)


def _gather_start(name, slabs, groups):
    n = len(slabs)
    ng = len(groups)
    nf = len(CHIP_FLIPS)

    def body(*refs):
        sems, outs = refs[n:n + 2 * ng], refs[n + 2 * ng:2 * n + 2 * ng]
        token = refs[2 * n + 2 * ng]
        token[...] = jnp.zeros_like(token)
        x, y, c = _place()
        k = 2 * x + y
        for g, members in enumerate(groups):
            for i, a in enumerate(members):
                for j, flip in enumerate(CHIP_FLIPS):
                    px, py, _ = _chip_of(x, y, flip)
                    mine = _half(outs[a].at[k], c)
                    pltpu.make_async_remote_copy(
                        src_ref=mine, dst_ref=mine, send_sem=sems[2 * g].at[i * nf + j], recv_sem=sems[2 * g + 1].at[i * nf + j],
                        device_id=(px, py, c), device_id_type=MESH).start()

    sem_shapes = []
    for members in groups:
        sem_shapes += [pltpu.SemaphoreType.DMA((nf * len(members),))] * 2
    res = pl.pallas_call(
        body, name=name,
        out_shape=tuple(sem_shapes) + tuple(pltpu.HBM(t.shape, t.dtype) for t in slabs) + (TOKEN,),
        in_specs=[HBM] * n, out_specs=tuple([SEM] * (2 * ng) + [HBM] * n + [TOKEN_SPEC]),
        input_output_aliases={a: 2 * ng + a for a in range(n)}, compiler_params=SPLIT_COPY,
    )(*_in_hbm(slabs))
    return [(res[2 * g], res[2 * g + 1]) for g in range(ng)], list(res[2 * ng:2 * ng + n]), res[2 * ng + n]


def _gather_wait(name, slabs, sems, after):
    n = len(slabs)
    nf = len(CHIP_FLIPS)

    def body(*refs):
        ins = refs[:n]
        ssem, rsem = refs[n], refs[n + 1]
        x, y, c = _place()
        k = 2 * x + y
        for a in range(n):
            for j, flip in enumerate(CHIP_FLIPS):
                _, _, kj = _chip_of(x, y, flip)
                cp = pltpu.make_async_remote_copy(
                    src_ref=_half(ins[a].at[k], c), dst_ref=_half(ins[a].at[kj], c), send_sem=ssem.at[a * nf + j],
                    recv_sem=rsem.at[a * nf + j], device_id=(x, y, c), device_id_type=MESH)
                cp.wait_send()
                cp.wait_recv()

    return pl.pallas_call(
        body, name=name, out_shape=tuple(pltpu.HBM(t.shape, t.dtype) for t in slabs),
        in_specs=[HBM] * n + [SEM, SEM, ANY], out_specs=tuple([HBM] * n),
        input_output_aliases={a: a for a in range(n)}, compiler_params=SPLIT_COPY,
    )(*slabs, sems[0], sems[1], after)


def _gather_pass(name, slabs, sems, after):
    n = len(slabs)
    nf = len(CHIP_FLIPS)

    def body(*refs):
        ins = refs[:n]
        ssem, rsem = refs[n], refs[n + 1]
        ssem2, rsem2 = refs[n + 3], refs[n + 4]
        x, y, c = _place()
        k = 2 * x + y
        for a in range(n):
            for j, flip in enumerate(CHIP_FLIPS):
                _, _, kj = _chip_of(x, y, flip)
                landed = _half(ins[a].at[kj], c)
                cp = pltpu.make_async_remote_copy(
                    src_ref=_half(ins[a].at[k], c), dst_ref=landed, send_sem=ssem.at[a * nf + j],
                    recv_sem=rsem.at[a * nf + j], device_id=(x, y, c), device_id_type=MESH)
                cp.wait_send()
                cp.wait_recv()
                pltpu.make_async_remote_copy(
                    src_ref=landed, dst_ref=landed, send_sem=ssem2.at[a * nf + j], recv_sem=rsem2.at[a * nf + j],
                    device_id=(x, y, 1 - c), device_id_type=MESH).start()

    sem = pltpu.SemaphoreType.DMA((nf * n,))
    res = pl.pallas_call(
        body, name=name, out_shape=(sem, sem) + tuple(pltpu.HBM(t.shape, t.dtype) for t in slabs),
        in_specs=[HBM] * n + [SEM, SEM, ANY], out_specs=tuple([SEM, SEM] + [HBM] * n),
        input_output_aliases={a: 2 + a for a in range(n)}, compiler_params=SPLIT_COPY,
    )(*slabs, sems[0], sems[1], after)
    return (res[0], res[1]), list(res[2:])


def _pass_wait(name, slabs, sems, after):
    n = len(slabs)
    nf = len(CHIP_FLIPS)

    def body(*refs):
        ins = refs[:n]
        ssem, rsem = refs[n], refs[n + 1]
        x, y, c = _place()
        for a in range(n):
            for j, flip in enumerate(CHIP_FLIPS):
                _, _, kj = _chip_of(x, y, flip)
                cp = pltpu.make_async_remote_copy(
                    src_ref=_half(ins[a].at[kj], c), dst_ref=_half(ins[a].at[kj], 1 - c), send_sem=ssem.at[a * nf + j],
                    recv_sem=rsem.at[a * nf + j], device_id=(x, y, c), device_id_type=MESH)
                cp.wait_send()
                cp.wait_recv()

    return pl.pallas_call(
        body, name=name, out_shape=tuple(pltpu.HBM(t.shape, t.dtype) for t in slabs),
        in_specs=[HBM] * n + [SEM, SEM, ANY], out_specs=tuple([HBM] * n),
        input_output_aliases={a: a for a in range(n)}, compiler_params=SPLIT_COPY,
    )(*slabs, sems[0], sems[1], after)


NEIGHBOUR_FLIPS = CHIP_FLIPS[:2]


def _relay_chips(x, y, c):
    fx, fy = x ^ c, y ^ (1 - c)
    return (fx, fy), 2 * fx + fy, 2 * (1 - x) + (1 - y)


def _ag_start(name, slabs, deps=()):
    n = len(slabs)
    nn = len(NEIGHBOUR_FLIPS)

    def body(*refs):
        no = n + len(deps)
        ssem, rsem = refs[no], refs[no + 1]
        outs = refs[no + 2:no + 2 + n]
        token = refs[no + 2 + n]
        token[...] = jnp.zeros_like(token)
        x, y, c = _place()
        k = 2 * x + y
        for a in range(n):
            for j, flip in enumerate(NEIGHBOUR_FLIPS):
                px, py, _ = _chip_of(x, y, flip)
                mine = _half(outs[a].at[k], c)
                pltpu.make_async_remote_copy(src_ref=mine, dst_ref=mine, send_sem=ssem.at[a * nn + j],
                                             recv_sem=rsem.at[a * nn + j], device_id=(px, py, c), device_id_type=MESH).start()

    sem = pltpu.SemaphoreType.DMA((nn * n,))
    res = pl.pallas_call(
        body, name=name, out_shape=(sem, sem) + tuple(pltpu.HBM(t.shape, t.dtype) for t in slabs) + (TOKEN,),
        in_specs=[HBM] * n + [ANY] * len(deps), out_specs=tuple([SEM, SEM] + [HBM] * n + [TOKEN_SPEC]),
        input_output_aliases={a: 2 + a for a in range(n)}, compiler_params=SPLIT_COPY,
    )(*_in_hbm(slabs), *deps)
    return (res[0], res[1]), list(res[2:2 + n]), res[2 + n]


def _ag_relay(name, slabs, sems, after):
    n = len(slabs)
    nn = len(NEIGHBOUR_FLIPS)

    def body(*refs):
        no = n + 2 + len(after)
        ins = refs[:n]
        ssem, rsem = refs[n], refs[n + 1]
        r_s, r_r, p_s, p_r = refs[no:no + 4]
        x, y, c = _place()
        k = 2 * x + y
        (fx, fy), _, _ = _relay_chips(x, y, c)
        for a in range(n):
            for j, flip in enumerate(NEIGHBOUR_FLIPS):
                _, _, kj = _chip_of(x, y, flip)
                landed = _half(ins[a].at[kj], c)
                cp = pltpu.make_async_remote_copy(
                    src_ref=_half(ins[a].at[k], c), dst_ref=landed, send_sem=ssem.at[a * nn + j],
                    recv_sem=rsem.at[a * nn + j], device_id=(x, y, c), device_id_type=MESH)
                cp.wait_send()
                cp.wait_recv()
        for a in range(n):
            near = _half(ins[a].at[2 * (x ^ (1 - c)) + (y ^ c)], c)
            pltpu.make_async_remote_copy(src_ref=near, dst_ref=near, send_sem=r_s.at[a], recv_sem=r_r.at[a],
                                         device_id=(fx, fy, c), device_id_type=MESH).start()
            for j, flip in enumerate(NEIGHBOUR_FLIPS):
                _, _, kj = _chip_of(x, y, flip)
                landed = _half(ins[a].at[kj], c)
                pltpu.make_async_remote_copy(src_ref=landed, dst_ref=landed, send_sem=p_s.at[a * nn + j],
                                             recv_sem=p_r.at[a * nn + j], device_id=(x, y, 1 - c), device_id_type=MESH).start()

    rsem_t = pltpu.SemaphoreType.DMA((n,))
    psem_t = pltpu.SemaphoreType.DMA((nn * n,))
    res = pl.pallas_call(
        body, name=name, out_shape=(rsem_t, rsem_t, psem_t, psem_t) + tuple(pltpu.HBM(t.shape, t.dtype) for t in slabs),
        in_specs=[HBM] * n + [SEM, SEM] + [ANY] * len(after), out_specs=tuple([SEM] * 4 + [HBM] * n),
        input_output_aliases={a: 4 + a for a in range(n)}, compiler_params=SPLIT_COPY,
    )(*slabs, sems[0], sems[1], *after)
    return tuple(res[:4]), list(res[4:])


def _ag_relay_wait(name, slabs, sems, after):
    n = len(slabs)
    nn = len(NEIGHBOUR_FLIPS)

    def body(*refs):
        no = n + 4 + len(after)
        ins = refs[:n]
        r_s, r_r, p_s, p_r = refs[n:n + 4]
        f_s, f_r = refs[no], refs[no + 1]
        x, y, c = _place()
        _, _, kd = _relay_chips(x, y, c)
        for a in range(n):
            near = _half(ins[a].at[2 * (x ^ (1 - c)) + (y ^ c)], c)
            diag = _half(ins[a].at[kd], c)
            cp = pltpu.make_async_remote_copy(src_ref=near, dst_ref=diag, send_sem=r_s.at[a], recv_sem=r_r.at[a],
                                              device_id=(x, y, c), device_id_type=MESH)
            cp.wait_send()
            cp.wait_recv()
            for j, flip in enumerate(NEIGHBOUR_FLIPS):
                _, _, kj = _chip_of(x, y, flip)
                cp = pltpu.make_async_remote_copy(
                    src_ref=_half(ins[a].at[kj], c), dst_ref=_half(ins[a].at[kj], 1 - c), send_sem=p_s.at[a * nn + j],
                    recv_sem=p_r.at[a * nn + j], device_id=(x, y, c), device_id_type=MESH)
                cp.wait_send()
                cp.wait_recv()
            pltpu.make_async_remote_copy(src_ref=diag, dst_ref=diag, send_sem=f_s.at[a], recv_sem=f_r.at[a],
                                         device_id=(x, y, 1 - c), device_id_type=MESH).start()

    sem = pltpu.SemaphoreType.DMA((n,))
    res = pl.pallas_call(
        body, name=name, out_shape=(sem, sem) + tuple(pltpu.HBM(t.shape, t.dtype) for t in slabs),
        in_specs=[HBM] * n + [SEM] * 4 + [ANY] * len(after), out_specs=tuple([SEM, SEM] + [HBM] * n),
        input_output_aliases={a: 2 + a for a in range(n)}, compiler_params=SPLIT_COPY,
    )(*slabs, *sems, *after)
    return (res[0], res[1]), list(res[2:])


def _ag_final_wait(name, slabs, sems, after):
    n = len(slabs)

    def body(*refs):
        ins = refs[:n]
        f_s, f_r = refs[n], refs[n + 1]
        x, y, c = _place()
        _, _, kd = _relay_chips(x, y, c)
        for a in range(n):
            cp = pltpu.make_async_remote_copy(
                src_ref=_half(ins[a].at[kd], c), dst_ref=_half(ins[a].at[kd], 1 - c), send_sem=f_s.at[a], recv_sem=f_r.at[a],
                device_id=(x, y, c), device_id_type=MESH)
            cp.wait_send()
            cp.wait_recv()

    return pl.pallas_call(
        body, name=name, out_shape=tuple(pltpu.HBM(t.shape, t.dtype) for t in slabs),
        in_specs=[HBM] * n + [SEM, SEM] + [ANY] * len(after), out_specs=tuple([HBM] * n),
        input_output_aliases={a: a for a in range(n)}, compiler_params=SPLIT_COPY,
    )(*slabs, sems[0], sems[1], *after)


def _sibling_part(ref, c, halves):
    if not halves:
        return ref
    h = ref.shape[1] // 2
    return ref.at[:, pl.ds((1 - c) * h, h)]


def _swap_start(name, grads, halves=True, deps=()):
    n = len(grads)

    def body(*refs):
        no = 2 * n + len(deps)
        ssem, rsem = refs[no], refs[no + 1]
        src, land = refs[no + 2:no + n + 2], refs[no + n + 2:no + 2 * n + 2]
        token = refs[no + 2 * n + 2]
        token[...] = jnp.zeros_like(token)
        x, y, c = _place()
        for a in range(n):
            pltpu.make_async_remote_copy(
                src_ref=_sibling_part(src[a], c, halves), dst_ref=land[a], send_sem=ssem.at[a], recv_sem=rsem.at[a],
                device_id=(x, y, 1 - c), device_id_type=MESH).start()

    zones = [lax.empty((g.shape[0], g.shape[1] // 2, g.shape[2]) if halves else g.shape, g.dtype) for g in grads]
    sem = pltpu.SemaphoreType.DMA((n,))
    res = pl.pallas_call(
        body, name=name,
        out_shape=(sem, sem) + tuple(pltpu.HBM(t.shape, t.dtype) for t in list(grads) + zones) + (TOKEN,),
        in_specs=[HBM] * (2 * n) + [ANY] * len(deps), out_specs=tuple([SEM, SEM] + [HBM] * (2 * n) + [TOKEN_SPEC]),
        input_output_aliases={i: 2 + i for i in range(2 * n)}, compiler_params=SPLIT_COPY,
    )(*_in_hbm(list(grads) + zones), *deps)
    return (res[0], res[1], list(res[2:2 + n]), list(res[2 + n:2 + 2 * n])), res[2 + 2 * n]


def _swap_wait(name, ssem, rsem, grads, zones, after, halves=True):
    n = len(grads)

    def body(*refs):
        src, land = refs[:n], refs[n:2 * n]
        ss, rs = refs[2 * n], refs[2 * n + 1]
        x, y, c = _place()
        for a in range(n):
            cp = pltpu.make_async_remote_copy(
                src_ref=_sibling_part(src[a], c, halves), dst_ref=land[a], send_sem=ss.at[a], recv_sem=rs.at[a],
                device_id=(x, y, c), device_id_type=MESH)
            cp.wait_send()
            cp.wait_recv()

    res = pl.pallas_call(
        body, name=name, out_shape=tuple(pltpu.HBM(t.shape, t.dtype) for t in list(grads) + list(zones)),
        in_specs=[HBM] * (2 * n) + [SEM, SEM] + [ANY] * len(after), out_specs=tuple([HBM] * (2 * n)),
        input_output_aliases={i: i for i in range(2 * n)}, compiler_params=SPLIT_COPY,
    )(*grads, *zones, ssem, rsem, *after)
    return list(res[:n]), list(res[n:])


def _sibling_exchange(name, slabs):
    n = len(slabs)
    nf = len(CHIP_FLIPS)

    def body(*refs):
        outs = refs[n:2 * n]
        ssem, rsem = refs[2 * n:]
        x, y, c = _place()

        def copy(a, j, which, to):
            _, _, kj = _chip_of(x, y, CHIP_FLIPS[j])
            ref = _half(outs[a].at[kj], which)
            return pltpu.make_async_remote_copy(src_ref=ref, dst_ref=ref, send_sem=ssem.at[a * nf + j],
                                                recv_sem=rsem.at[a * nf + j], device_id=to, device_id_type=MESH)

        sends = [copy(a, j, c, (x, y, 1 - c)) for a in range(n) for j in range(nf)]
        for cp in sends:
            cp.start()
        for a in range(n):
            for j in range(nf):
                copy(a, j, 1 - c, (x, y, c)).wait_recv()
        for cp in sends:
            cp.wait_send()

    return pl.pallas_call(
        body, name=name, out_shape=tuple(jax.ShapeDtypeStruct(t.shape, t.dtype) for t in slabs),
        in_specs=[ANY] * n, out_specs=(ANY,) * n, input_output_aliases={a: a for a in range(n)},
        scratch_shapes=[pltpu.SemaphoreType.DMA((nf * n,)), pltpu.SemaphoreType.DMA((nf * n,))],
    )(*slabs)


def _sibling_swap(name, grads):
    n = len(grads)

    def body(*refs):
        ins, outs = refs[:n], refs[n:2 * n]
        ssem, rsem = refs[2 * n:]
        x, y, c = _place()
        cps = []
        for a in range(n):
            h = ins[a].shape[1] // 2
            cps.append(pltpu.make_async_remote_copy(
                src_ref=ins[a].at[:, pl.ds((1 - c) * h, h)], dst_ref=outs[a], send_sem=ssem.at[a], recv_sem=rsem.at[a],
                device_id=(x, y, 1 - c), device_id_type=MESH))
        for cp in cps:
            cp.start()
        for cp in cps:
            cp.wait()

    return pl.pallas_call(
        body, name=name,
        out_shape=tuple(jax.ShapeDtypeStruct((g.shape[0], g.shape[1] // 2, g.shape[2]), g.dtype) for g in grads),
        in_specs=[ANY] * n, out_specs=(ANY,) * n,
        scratch_shapes=[pltpu.SemaphoreType.DMA((n,)), pltpu.SemaphoreType.DMA((n,))],
    )(*grads)


def _scatter_start(name, parts):
    n = len(parts)
    nf = len(CHIP_FLIPS)

    def body(*refs):
        ssem, rsem = refs[2 * n], refs[2 * n + 1]
        src, land = refs[2 * n + 2:3 * n + 2], refs[3 * n + 2:4 * n + 2]
        token = refs[4 * n + 2]
        token[...] = jnp.zeros_like(token)
        x, y, c = _place()
        for a in range(n):
            for j, flip in enumerate(CHIP_FLIPS):
                px, py, kj = _chip_of(x, y, flip)
                pltpu.make_async_remote_copy(
                    src_ref=src[a].at[kj], dst_ref=land[a].at[j], send_sem=ssem.at[a * nf + j], recv_sem=rsem.at[a * nf + j],
                    device_id=(px, py, c), device_id_type=MESH).start()

    zones = [lax.empty((nf,) + p.shape[1:], p.dtype) for p in parts]
    sem = pltpu.SemaphoreType.DMA((nf * n,))
    res = pl.pallas_call(
        body, name=name,
        out_shape=(sem, sem) + tuple(pltpu.HBM(t.shape, t.dtype) for t in list(parts) + zones)
        + (jax.ShapeDtypeStruct((SMALL_ROWS, LANES), F32),),
        in_specs=[HBM] * (2 * n),
        out_specs=tuple([SEM, SEM] + [HBM] * (2 * n) + [pl.BlockSpec(memory_space=pltpu.VMEM)]),
        input_output_aliases={i: 2 + i for i in range(2 * n)}, compiler_params=SPLIT_COPY,
    )(*_in_hbm(list(parts) + zones))
    return (res[0], res[1], list(res[2:2 + n]), list(res[2 + n:2 + 2 * n])), res[2 + 2 * n]


def _scatter_wait(name, ssem, rsem, parts, zones, after):
    n = len(parts)
    nf = len(CHIP_FLIPS)

    def body(*refs):
        src, land = refs[:n], refs[n:2 * n]
        ss, rs = refs[2 * n], refs[2 * n + 1]
        x, y, c = _place()
        for a in range(n):
            for j, flip in enumerate(CHIP_FLIPS):
                _, _, kj = _chip_of(x, y, flip)
                cp = pltpu.make_async_remote_copy(
                    src_ref=src[a].at[kj], dst_ref=land[a].at[j], send_sem=ss.at[a * nf + j], recv_sem=rs.at[a * nf + j],
                    device_id=(x, y, c), device_id_type=MESH)
                cp.wait_send()
                cp.wait_recv()

    res = pl.pallas_call(
        body, name=name, out_shape=tuple(pltpu.HBM(t.shape, t.dtype) for t in list(parts) + list(zones)),
        in_specs=[HBM] * (2 * n) + [SEM, SEM] + [ANY] * len(after), out_specs=tuple([HBM] * (2 * n)),
        input_output_aliases={i: i for i in range(2 * n)}, compiler_params=SPLIT_COPY,
    )(*parts, *zones, ssem, rsem, *after)
    return list(res[:n]), list(res[n:])


N_PEERS = 7


def _peer(x, y, c, mask):
    px, py, pc = x ^ ((mask >> 2) & 1), y ^ ((mask >> 1) & 1), c ^ (mask & 1)
    return (px, py, pc), 4 * px + 2 * py + pc


def _reduce_start(vec, deps):
    nd = len(deps)

    def body(*refs):
        ssem, rsem, src, land, token = refs[2 + nd:]
        token[...] = jnp.zeros_like(token)
        x, y, c = _place()
        me = 4 * x + 2 * y + c
        for mask in range(1, N_PEERS + 1):
            to, _ = _peer(x, y, c, mask)
            pltpu.make_async_remote_copy(src_ref=src, dst_ref=land.at[me], send_sem=ssem.at[mask - 1],
                                         recv_sem=rsem.at[mask - 1], device_id=to, device_id_type=MESH).start()

    zone = lax.empty((N_PEERS + 1,) + vec.shape, vec.dtype)
    sem = pltpu.SemaphoreType.DMA((N_PEERS,))
    res = pl.pallas_call(
        body, name="reduce_start",
        out_shape=(sem, sem, pltpu.HBM(vec.shape, vec.dtype), pltpu.HBM(zone.shape, zone.dtype), TOKEN),
        in_specs=[HBM, HBM] + [ANY] * nd, out_specs=(SEM, SEM, HBM, HBM, TOKEN_SPEC),
        input_output_aliases={0: 2, 1: 3}, compiler_params=SPLIT_COPY,
    )(*_in_hbm([vec, zone]), *deps)
    return res[:4], res[4]


def _reduce_wait(ssem, rsem, vec, zone, after):
    def body(src, land, ss, rs, *_):
        x, y, c = _place()
        for mask in range(1, N_PEERS + 1):
            _, frm = _peer(x, y, c, mask)
            cp = pltpu.make_async_remote_copy(src_ref=src, dst_ref=land.at[frm], send_sem=ss.at[mask - 1],
                                              recv_sem=rs.at[mask - 1], device_id=(x, y, c), device_id_type=MESH)
            cp.wait_send()
            cp.wait_recv()

    return pl.pallas_call(
        body, name="reduce_wait", out_shape=(pltpu.HBM(vec.shape, vec.dtype), pltpu.HBM(zone.shape, zone.dtype)),
        in_specs=[HBM, HBM, SEM, SEM] + [ANY] * len(after), out_specs=(HBM, HBM),
        input_output_aliases={0: 0, 1: 1}, compiler_params=SPLIT_COPY,
    )(vec, zone, ssem, rsem, *after)


def _reduce_sum(vec, zone, me, loss_row, loss_scale):
    r, dm = vec.shape

    def body(me_ref, v_ref, z_ref, o_ref, l_ref):
        acc = None
        for i in range(N_PEERS + 1):
            term = jnp.where(me_ref[0] == i, v_ref[...], z_ref[i])
            acc = term if acc is None else acc + term
        o_ref[...] = acc
        l_ref[...] = jnp.sum(acc[loss_row:loss_row + SMALL_ROWS, :], axis=(0, 1), keepdims=True) * loss_scale

    grid_spec = pltpu.PrefetchScalarGridSpec(
        num_scalar_prefetch=1, grid=(1,),
        in_specs=[pl.BlockSpec((r, dm), lambda i, me_ref: (0, 0)), pl.BlockSpec((N_PEERS + 1, r, dm), lambda i, me_ref: (0, 0, 0))],
        out_specs=(pl.BlockSpec((r, dm), lambda i, me_ref: (0, 0)), pl.BlockSpec((1, 1), lambda i, me_ref: (0, 0))))
    return pl.pallas_call(
        body, name="reduce_sum", out_shape=(jax.ShapeDtypeStruct((r, dm), F32), jax.ShapeDtypeStruct((1, 1), F32)),
        grid_spec=grid_spec, compiler_params=_params(1),
    )(me, vec, zone)


def kernel(x, meta_tokens, norm_mix_g, w_in, b_gate, pool_w, pool_scale, conv_w, conv_out_w, w_o, norm_ffn_g, w_gate_up, w_down, norm_final_g, loss_target, m_meta_tokens, m_norm_mix_g, m_w_in, m_b_gate, m_pool_w, m_pool_scale, m_conv_w, m_conv_out_w, m_w_o, m_norm_ffn_g, m_w_gate_up, m_w_down, m_norm_final_g, v_meta_tokens, v_norm_mix_g, v_w_in, v_b_gate, v_pool_w, v_pool_scale, v_conv_w, v_conv_out_w, v_w_o, v_norm_ffn_g, v_w_gate_up, v_w_down, v_norm_final_g):
    seq, dm = x.shape[1], x.shape[2]
    tail = LANES
    tm = tail
    lp = seq + tail
    n_chips = 4
    n_groups = len(POOL_WINDOWS)
    gw = dm // n_groups
    tc = min(256, gw)
    cx, cy, cc = _place()
    chip = 2 * cx + cy
    dloc = dm // n_chips

    pool2 = pool_w.reshape(n_groups * pool_w.shape[1], gw)
    big = {"w_in": w_in, "w_gate_up": w_gate_up, "pool_w": pool2, "conv_out_w": conv_out_w, "w_o": w_o, "w_down": w_down}
    chip1 = jnp.reshape(chip, (1,)).astype(jnp.int32)
    core = jnp.reshape(cc, (1,)).astype(jnp.int32)
    small_loc = jnp.concatenate([meta_tokens, jnp.pad(conv_w, ((0, 8 - conv_w.shape[0]), (0, 0))),
                                 jnp.zeros((8, dloc), F32)], axis=0)
    g1, g2, g3 = norm_mix_g.reshape(1, dm), norm_ffn_g.reshape(1, dm), norm_final_g.reshape(1, dm)
    b_gate2 = b_gate.reshape(2, dm)
    ps = pool_scale.reshape(1, dm)
    first = [_cast_into_slot("cast_w_in", w_in, chip1, BF16), _cast_into_slot("place_small", small_loc, chip1, F32)]
    sems, first, token = _ag_start("ag_start_first", first)
    cast = {nme: _cast_into_slot("cast_" + nme, big[nme], chip1, BF16, deps=(token,))
            for nme in ["pool_w", "conv_out_w", "w_o", "w_gate_up", "w_down"]}
    sems, first = _ag_relay("ag_relay_first", first, sems, list(cast.values()))
    sems, first = _ag_relay_wait("ag_relay_wait_first", first, sems, [])
    w_in4, small4 = _ag_final_wait("ag_final_wait_first", first, sems, [])
    mixer_w = [cast["pool_w"], cast["conv_out_w"], cast["w_o"]]
    sems_mix, mixer_w, token = _ag_start("ag_start_mixer", mixer_w, deps=(w_in4,))
    sems_gu, (w_gu4,), token = _ag_start("ag_start_gate_up", [cast["w_gate_up"]], deps=(token,))

    small_f = jnp.transpose(small4, (1, 0, 2)).reshape(small4.shape[1], dm)
    meta_f = small_f[:N_META]
    conv_w_f = small_f[N_META:N_META + 3]
    h0 = jnp.concatenate([x[0], jnp.zeros((tail - N_META, dm), F32), meta_f], axis=0)
    hn1 = _rms_fwd("rms_mix", h0, g1, tm, deps=(token,))
    proj = _nn_sharded("proj", hn1, w_in4, 6)
    sems_mix, mixer_w = _ag_relay("ag_relay_mixer", mixer_w, sems_mix, [proj])
    sems_gu, (w_gu4,) = _ag_relay("ag_relay_gate_up", [w_gu4], sems_gu, [mixer_w[0]])
    sems_down, (w_down4,), token = _ag_start("ag_start_down", [cast["w_down"]], deps=(w_gu4,))
    pooled, z = _mixer_fwd("mixer_fwd", proj, conv_w_f, tc, token)
    sems_mix, mixer_w = _ag_relay_wait("ag_relay_wait_mixer", mixer_w, sems_mix, [pooled])
    pool4, conv_out4, w_o4 = _ag_final_wait("ag_final_wait_mixer", mixer_w, sems_mix, [])
    pool_f = jnp.transpose(pool4.reshape(n_chips, n_groups, gw // n_chips, gw), (1, 0, 2, 3)).reshape(n_groups, gw, gw)
    conv_out_f = conv_out4.reshape(dm, dm)
    w_o_f = w_o4.reshape(dm, dm)
    ya = _pool_fwd("pool_proj", pooled, pool_f)
    yb = _nn_plain("conv_out", z, conv_out_f, BF16)
    mix = _gate_mix("gate_mix", proj, b_gate2, ya, ps, yb, tm)
    sems_gu, (w_gu4,) = _ag_relay_wait("ag_relay_wait_gate_up", [w_gu4], sems_gu, [mix])
    h1 = _nn_plain("attn_out", mix, w_o_f, F32, res=h0, tn_pref=256)
    (w_gu4,) = _ag_final_wait("ag_final_wait_gate_up", [w_gu4], sems_gu, [h1])
    hn2 = _rms_fwd("rms_ffn", h1, g2, tm)
    sems_down, (w_down4,) = _ag_relay("ag_relay_down", [w_down4], sems_down, [hn2])
    gu, act = _gate_up_swiglu("gate_up", hn2, w_gu4, w_down4)
    sems_down, (w_down4,) = _ag_relay_wait("ag_relay_wait_down", [w_down4], sems_down, [act])
    (w_down4,) = _ag_final_wait("ag_final_wait_down", [w_down4], sems_down, [])
    w_down_f = w_down4.reshape(-1, dm)
    h2 = _nn_plain("ffn_down", act, w_down_f, F32, res=h1, tn_pref=512, tk_pref=1536)
    dh2, dh2b, loss_cols, dg3 = _final_loss("final_loss", h2, g3, loss_target[0], tm)

    def scatter(tag, names_g, swap, after):
        grads_g, got = _swap_wait("swap_wait_" + tag, *swap, [after])
        pairs = [_pair_add("pair_add_" + nme, g4, rv, core) for nme, g4, rv in zip(names_g, grads_g, got)]
        return _scatter_start("scatter_start_" + tag, pairs)

    dgu = _dact_swiglu_bwd("d_gate_up", dh2b, w_down_f, gu)
    gw_down = _tn_plain("dw_down", act, dh2b)
    gw_gu = _tn_sharded("dw_gate_up", hn2, dgu, n_chips)
    swap_a, token = _swap_start("swap_start_a", [gw_gu, gw_down.reshape(n_chips, -1, dm)])
    dhn2 = _nt_sharded("d_hn2", dgu, w_gu4, deps=(token,))
    flight_a, token = scatter("a", ["w_gate_up", "w_down"], swap_a, dhn2)
    dh1, dh1b, dg2 = _rms_bwd("rms_ffn_bwd", dhn2, h1, g2, dh2, tm, token)
    dmix = _nt_plain("d_mix", dh1b, w_o_f)
    gw_o = _tn_plain("dw_o", mix, dh1b)
    dproj, dyb, dya, db_gate, dps = _gate_bwd("gate_bwd", dmix, proj, b_gate2, ya, ps, yb, tm)
    gw_conv_out = _tn_plain("dw_conv_out", z, dyb)
    gw_pool = _pool_bwd_w("dw_pool", pooled, dya)
    gw_pool = jnp.transpose(gw_pool.reshape(n_groups, n_chips, gw // n_chips, gw), (1, 0, 2, 3))
    swap_b, token = _swap_start("swap_start_b", [gw_o.reshape(n_chips, dloc, dm), gw_conv_out.reshape(n_chips, dloc, dm),
                                                 gw_pool.reshape(n_chips, n_groups * (gw // n_chips), gw)])
    dpooled = _pool_bwd_act("d_pooled", dya, pool_f, deps=(token,))
    dz = _nt_plain("d_z", dyb, conv_out_f)
    flight_b, token = scatter("b", ["w_o", "conv_out_w", "pool_w"], swap_b, dz)
    dproj, dconv_w = _mixer_bwd("mixer_bwd", dz, dpooled, proj, conv_w_f, dproj, tc, token)
    gw_in0 = _tn_sharded("dw_in_0", hn1, dproj, n_chips, part=(0, 2))
    swap_c0, token = _swap_start("swap_start_c0", [gw_in0])
    gw_in1 = _tn_sharded("dw_in_1", hn1, dproj, n_chips, part=(1, 2), deps=(token,))
    flight_c0, token = scatter("c0", ["w_in_0"], swap_c0, gw_in1)
    swap_c1, token = _swap_start("swap_start_c1", [gw_in1], deps=(token,))
    dhn1 = _nt_sharded("d_hn1", dproj, w_in4, deps=(token,))
    flight_c1, token = scatter("c1", ["w_in_1"], swap_c1, dhn1)
    dx, dtail, dg1 = _rms_bwd_input("rms_mix_bwd", dhn1, h0, g1, dh1, tm, seq, token)
    grad_x = dx[None]
    dmeta = dtail[tail - N_META:]

    given = dict(meta_tokens=(meta_tokens, m_meta_tokens, v_meta_tokens), norm_mix_g=(norm_mix_g, m_norm_mix_g, v_norm_mix_g),
                 w_in=(w_in, m_w_in, v_w_in), b_gate=(b_gate, m_b_gate, v_b_gate), pool_w=(pool_w, m_pool_w, v_pool_w),
                 pool_scale=(pool_scale, m_pool_scale, v_pool_scale), conv_w=(conv_w, m_conv_w, v_conv_w),
                 conv_out_w=(conv_out_w, m_conv_out_w, v_conv_out_w), w_o=(w_o, m_w_o, v_w_o),
                 norm_ffn_g=(norm_ffn_g, m_norm_ffn_g, v_norm_ffn_g), w_gate_up=(w_gate_up, m_w_gate_up, v_w_gate_up),
                 w_down=(w_down, m_w_down, v_w_down), norm_final_g=(norm_final_g, m_norm_final_g, v_norm_final_g))
    order = list(given.keys())
    grad, delta, new_m, new_v = {}, {}, {}, {}
    vec = jnp.concatenate([dg1, dg2, dg3, db_gate, dps, loss_cols, dconv_w, dmeta], axis=0)
    loss_row = 5 * SMALL_ROWS
    groups_g = {"a": [("w_gate_up", (0, 1)), ("w_down", (0, 1))], "b": [("w_o", (0, 1)), ("conv_out_w", (0, 1)), ("pool_w", (0, 1))],
                "c0": [("w_in", (0, 2))], "c1": [("w_in", (1, 2))]}
    results = {}

    def reduced(tag, flight, after):
        pairs, zones = _scatter_wait("scatter_wait_" + tag, *flight, after)
        halves = [_chip_sum("chip_sum_%s_%d" % (nme, part[0]), p, rv, chip1) for (nme, part), p, rv in zip(groups_g[tag], pairs, zones)]
        return _swap_start("send_start_" + tag, halves, halves=False)

    def update(tag, send, after):
        halves, sib_halves = _swap_wait("send_wait_" + tag, *send, after, halves=False)
        deltas = []
        for (nme, part), g_own, g_sib in zip(groups_g[tag], halves, sib_halves):
            w, m, v = given[nme]
            shape2 = (2 * g_own.shape[0] * part[1], g_own.shape[1])
            results[nme] = _adamw_halves("adamw_%s_%d" % (nme, part[0]), w.reshape(shape2), g_own, g_sib, m.reshape(shape2),
                                         v.reshape(shape2), core, part=part, prev=results.get(nme))
            grad[nme], delta[nme], new_m[nme], new_v[nme] = [t.reshape(w.shape) for t in results[nme]]
            deltas.append(results[nme][1])
        return deltas

    send_a, token = reduced("a", flight_a, [dx])
    send_b, token = reduced("b", flight_b, [token])
    done_a = update("a", send_a, [token])
    send_c0, token = reduced("c0", flight_c0, done_a)
    done_b = update("b", send_b, [token])
    send_c1, token = reduced("c1", flight_c1, done_b)
    me1 = jnp.reshape(4 * cx + 2 * cy + cc, (1,)).astype(jnp.int32)
    red_flight, token = _reduce_start(vec, [token])
    done_c0 = update("c0", send_c0, [token])
    done_c1 = update("c1", send_c1, done_c0)
    red, loss11 = _reduce_sum(*_reduce_wait(*red_flight, done_c1), me1, loss_row, 0.5 / dm)
    loss = loss11[0, 0]
    col0 = chip * dloc
    g_small = {
        "norm_mix_g": red[0], "norm_ffn_g": red[SMALL_ROWS], "norm_final_g": red[2 * SMALL_ROWS],
        "b_gate": red[3 * SMALL_ROWS:3 * SMALL_ROWS + 2].reshape(-1), "pool_scale": red[4 * SMALL_ROWS],
        "conv_w": lax.dynamic_slice(red, (6 * SMALL_ROWS, col0), (3, dloc)),
        "meta_tokens": lax.dynamic_slice(red, (7 * SMALL_ROWS, col0), (N_META, dloc)),
    }

    vec_names = ["norm_mix_g", "norm_ffn_g", "norm_final_g", "pool_scale"]

    def slab_vec(pick):
        rows = [pick(nme).reshape(1, dm) for nme in vec_names] + [pick("b_gate").reshape(2, dm), jnp.zeros((2, dm), F32)]
        return jnp.concatenate(rows, axis=0)

    def slab_col(pick):
        return jnp.concatenate([pick("meta_tokens"), pick("conv_w"), jnp.zeros((5, dloc), F32)], axis=0)

    for slab, tag in ((slab_vec, "vec"), (slab_col, "col")):
        d, nm, nv = _adamw("adamw_small_" + tag, slab(lambda nme: given[nme][0]), slab(lambda nme: g_small[nme]),
                           slab(lambda nme: given[nme][1]), slab(lambda nme: given[nme][2]))
        for out, res in ((delta, d), (new_m, nm), (new_v, nv)):
            if tag == "vec":
                for i, nme in enumerate(vec_names):
                    out[nme] = res[i]
                out["b_gate"] = res[4:6].reshape(-1)
            else:
                out["meta_tokens"] = res[:N_META]
                out["conv_w"] = res[N_META:N_META + 3]
    grad.update(g_small)
    return (loss, grad_x, *[grad[nme] for nme in order], *[delta[nme] for nme in order],
            *[new_m[nme] for nme in order], *[new_v[nme] for nme in order])
```

```python
import functools
import math

import jax
import jax.numpy as jnp
from jax import lax
from jax.experimental import pallas as pl
from jax.experimental.pallas import tpu as pltpu

F32 = jnp.float32
BF16 = jnp.bfloat16
N_META = 16
POOL_WINDOWS = (2, 4, 8, 16)
EPS = 1e-6
ADAM_LR, ADAM_B1, ADAM_B2, ADAM_EPS, ADAM_WD, ADAM_STEP = 0.001, 0.9, 0.999, 1e-08, 0.01, 10
LANES = 128
V7X_VMEM_BYTES = 64 * 1024 * 1024
VMEM_LIMIT = V7X_VMEM_BYTES - 8 * 1024 * 1024
MESH = pl.DeviceIdType.MESH
ANY = pl.BlockSpec(memory_space=pl.ANY)
CHIP_FLIPS = ((1, 0), (0, 1), (1, 1))
SMALL_ROWS = 8


def _pick(n, pref):
    best = None
    for t in range(LANES, min(n, pref) + 1, LANES):
        if n % t == 0:
            best = t
    assert best is not None, (n, pref)
    return best


def _params(n_axes=0):
    sem = ("arbitrary",) * n_axes if n_axes else None
    return pltpu.CompilerParams(dimension_semantics=sem, vmem_limit_bytes=VMEM_LIMIT)


_DIMS = {
    "nn": (((1,), (0,)), ((), ())),
    "nt": (((1,), (1,)), ((), ())),
    "tn": (((0,), (0,)), ((), ())),
}


def _matmul(name, mode, a, b, out_sds, grid, a_spec, b_spec, o_spec, nk, res=None, res_spec=None, acc_shape=None, deps=()):
    out_dtype = out_sds.dtype
    in_place = nk > 1 and out_dtype == F32
    use_scratch = nk > 1 and not in_place
    rows = a_spec.block_shape[-2] if mode != "tn" else None
    chunk = _row_tile(rows, 1, 1, 1152) if rows is not None else None
    n_in = 2 + (res is not None) + len(deps)

    def body(*refs):
        a_ref, b_ref = refs[:2]
        r_ref = refs[2] if res is not None else None
        o_ref, *scr = refs[n_in:]
        k = pl.program_id(len(grid) - 1) if nk > 1 else None

        def emit(sl):
            if sl is None:
                part = lax.dot_general(a_ref[...], b_ref[...], _DIMS[mode], preferred_element_type=F32)
                idx = (slice(None), slice(None))
            else:
                part = lax.dot_general(a_ref[sl, :], b_ref[...], _DIMS[mode], preferred_element_type=F32)
                idx = (sl, slice(None))
            if nk == 1:
                if r_ref is not None:
                    part = part + r_ref[idx]
                o_ref[idx] = part.astype(out_dtype)
                return
            acc = scr[0] if use_scratch else o_ref

            @pl.when(k == 0)
            def _():
                first = part
                if r_ref is not None and in_place:
                    first = first + r_ref[idx]
                acc[idx] = first

            @pl.when(k > 0)
            def _():
                acc[idx] += part

            if use_scratch:

                @pl.when(k == nk - 1)
                def _():
                    o_ref[idx] = acc[idx].astype(out_dtype)

        if mode == "tn" or chunk == rows:
            emit(None)
        else:
            for m0 in range(0, rows, chunk):
                emit(pl.ds(m0, chunk))

    ins = [a, b] + ([res] if res is not None else []) + list(deps)
    in_specs = [a_spec, b_spec] + ([res_spec] if res is not None else []) + [ANY] * len(deps)
    scratch = [pltpu.VMEM(acc_shape, F32)] if use_scratch else []
    return pl.pallas_call(
        body, name=name, out_shape=out_sds, grid=grid, in_specs=in_specs, out_specs=o_spec,
        scratch_shapes=scratch, compiler_params=_params(len(grid)),
    )(*ins)


def _nn_sharded(name, a, w4, nseg):
    lp, kdim = a.shape
    s, _, nloc = w4.shape
    segw = s * nloc // nseg
    tn = _pick(math.gcd(nloc, segw), 1536)
    bw, bo = nloc // tn, segw // tn
    return _matmul(
        name, "nn", a, w4, jax.ShapeDtypeStruct((nseg, lp, segw), BF16), (s * bw,),
        pl.BlockSpec((lp, kdim), lambda j: (0, 0)),
        pl.BlockSpec((None, kdim, tn), lambda j: (j // bw, 0, j % bw)),
        pl.BlockSpec((None, lp, tn), lambda j: (j // bo, 0, j % bo)), 1)


def _nn_plain(name, a, w, out_dtype, res=None, tn_pref=512, tk_pref=2048):
    lp, kdim = a.shape
    n = w.shape[1]
    tn = _pick(n, tn_pref)
    tk = kdim if kdim <= tk_pref else _pick(kdim, tk_pref)
    nk = kdim // tk
    grid = (n // tn, nk) if nk > 1 else (n // tn,)
    if nk > 1:
        a_spec = pl.BlockSpec((lp, tk), lambda j, k: (0, k))
        w_spec = pl.BlockSpec((tk, tn), lambda j, k: (k, j))
        o_spec = pl.BlockSpec((lp, tn), lambda j, k: (0, j))
    else:
        a_spec = pl.BlockSpec((lp, tk), lambda j: (0, 0))
        w_spec = pl.BlockSpec((tk, tn), lambda j: (0, j))
        o_spec = pl.BlockSpec((lp, tn), lambda j: (0, j))
    return _matmul(name, "nn", a, w, jax.ShapeDtypeStruct((lp, n), out_dtype), grid, a_spec, w_spec, o_spec, nk,
                   res=res, res_spec=o_spec if res is not None else None, acc_shape=(lp, tn))


def _nt_plain(name, a, w, tn_pref=512):
    lp, kdim = a.shape
    n = w.shape[0]
    tn = _pick(n, tn_pref)
    return _matmul(
        name, "nt", a, w, jax.ShapeDtypeStruct((lp, n), BF16), (n // tn,),
        pl.BlockSpec((lp, kdim), lambda j: (0, 0)),
        pl.BlockSpec((tn, kdim), lambda j: (j, 0)),
        pl.BlockSpec((lp, tn), lambda j: (0, j)), 1)


def _nt_sharded(name, dseg, w4, to_pref=1024, tr_pref=1536, row_tiles=1, deps=()):
    nseg, lp, segw = dseg.shape
    s, kdim, nloc = w4.shape
    tr = _pick(math.gcd(nloc, segw), tr_pref)
    ba, bw = segw // tr, nloc // tr
    nr = s * bw
    to = _pick(kdim, to_pref)
    tm = lp // row_tiles
    return _matmul(
        name, "nt", dseg, w4, jax.ShapeDtypeStruct((lp, kdim), F32), (row_tiles, kdim // to, nr),
        pl.BlockSpec((None, tm, tr), lambda m, j, r: (r // ba, m, r % ba)),
        pl.BlockSpec((None, to, tr), lambda m, j, r: (r // bw, j, r % bw)),
        pl.BlockSpec((tm, to), lambda m, j, r: (m, j)), nr, deps=deps)


def _nn_rows(name, a, w, res, row_tiles=2, tn_pref=512):
    lp, kdim = a.shape
    n = w.shape[1]
    tn = _pick(n, tn_pref)
    tm = lp // row_tiles
    blk = pl.BlockSpec((tm, tn), lambda i, j: (i, j))
    return _matmul(name, "nn", a, w, jax.ShapeDtypeStruct((lp, n), F32), (row_tiles, n // tn),
                   pl.BlockSpec((tm, kdim), lambda i, j: (i, 0)), pl.BlockSpec((kdim, tn), lambda i, j: (0, j)), blk, 1,
                   res=res, res_spec=blk)


def _tn_plain(name, a, d, tk_pref=1024):
    lp, kdim = a.shape
    n = d.shape[1]
    tk = _pick(kdim, tk_pref)
    return _matmul(
        name, "tn", a, d, jax.ShapeDtypeStruct((kdim, n), BF16), (kdim // tk,),
        pl.BlockSpec((lp, tk), lambda i: (0, i)),
        pl.BlockSpec((lp, n), lambda i: (0, 0)),
        pl.BlockSpec((tk, n), lambda i: (i, 0)), 1)


def _tn_sharded(name, a, dseg, s, part=(0, 1), tk_pref=1024, deps=()):
    lp, kdim = a.shape
    nseg, _, segw = dseg.shape
    nloc = nseg * segw // s
    tn = _pick(math.gcd(nloc, segw), 1536)
    bd, bo = segw // tn, nloc // tn
    kpart = kdim // part[1]
    tk = _pick(kpart, tk_pref)
    i0 = part[0] * (kpart // tk)

    def body(a_ref, d_ref, *rest):
        o_ref, at_ref = rest[len(deps):]

        @pl.when(pl.program_id(1) == 0)
        def _():
            at_ref[...] = a_ref[...].T

        o_ref[...] = jnp.dot(at_ref[...], d_ref[...], preferred_element_type=F32).astype(BF16)

    return pl.pallas_call(
        body, name=name, out_shape=jax.ShapeDtypeStruct((s, kpart, nloc), BF16), grid=(kpart // tk, s * bo),
        in_specs=[pl.BlockSpec((lp, tk), lambda i, j: (0, i0 + i)),
                  pl.BlockSpec((None, lp, tn), lambda i, j: (j // bd, 0, j % bd))] + [ANY] * len(deps),
        out_specs=pl.BlockSpec((None, tk, tn), lambda i, j: (j // bo, i, j % bo)),
        scratch_shapes=[pltpu.VMEM((tk, lp), BF16)], compiler_params=_params(2),
    )(a, dseg, *deps)


def _silu_parts(gt):
    sg = jax.nn.sigmoid(gt)
    return gt * sg, sg * (1.0 + gt * (1.0 - sg))


def _gate_up_swiglu(name, a, w4, dep, tn_pref=256):
    lp, kdim = a.shape
    s, _, nloc = w4.shape
    f = s * nloc // 2
    tn = _pick(nloc, tn_pref)
    bw = nloc // tn
    chunk = _row_tile(lp, 1, 1, 576)

    def body(a_ref, wg_ref, wu_ref, _, gu_ref, act_ref):
        for m0 in range(0, lp, chunk):
            sl = pl.ds(m0, chunk)
            gt = jnp.dot(a_ref[sl, :], wg_ref[...], preferred_element_type=F32)
            up = jnp.dot(a_ref[sl, :], wu_ref[...], preferred_element_type=F32)
            gu_ref[0, sl, :] = gt.astype(BF16)
            gu_ref[1, sl, :] = up.astype(BF16)
            act_ref[sl, :] = (_silu_parts(gt)[0] * up).astype(BF16)

    return pl.pallas_call(
        body, name=name, grid=(f // tn,),
        out_shape=(jax.ShapeDtypeStruct((2, lp, f), BF16), jax.ShapeDtypeStruct((lp, f), BF16)),
        in_specs=[pl.BlockSpec((lp, kdim), lambda j: (0, 0)),
                  pl.BlockSpec((None, kdim, tn), lambda j: (j // bw, 0, j % bw)),
                  pl.BlockSpec((None, kdim, tn), lambda j: (s // 2 + j // bw, 0, j % bw)), ANY],
        out_specs=(pl.BlockSpec((2, lp, tn), lambda j: (0, 0, j)), pl.BlockSpec((lp, tn), lambda j: (0, j))),
        compiler_params=_params(1),
    )(a, w4, w4, dep)


def _dact_swiglu_bwd(name, d, w, gu, tn_pref=512):
    lp, dm = d.shape
    f = w.shape[0]
    tn = _pick(f, tn_pref)
    chunk = _row_tile(lp, 1, 1, 576)

    def body(d_ref, w_ref, g_ref, u_ref, o_ref):
        for m0 in range(0, lp, chunk):
            sl = pl.ds(m0, chunk)
            dact = lax.dot_general(d_ref[sl, :], w_ref[...], _DIMS["nt"], preferred_element_type=F32)
            silu, dsilu = _silu_parts(g_ref[sl, :].astype(F32))
            o_ref[0, sl, :] = (dact * u_ref[sl, :].astype(F32) * dsilu).astype(BF16)
            o_ref[1, sl, :] = (dact * silu).astype(BF16)

    return pl.pallas_call(
        body, name=name, grid=(f // tn,), out_shape=jax.ShapeDtypeStruct((2, lp, f), BF16),
        in_specs=[pl.BlockSpec((lp, dm), lambda j: (0, 0)), pl.BlockSpec((tn, dm), lambda j: (j, 0)),
                  pl.BlockSpec((None, lp, tn), lambda j: (0, 0, j)), pl.BlockSpec((None, lp, tn), lambda j: (1, 0, j))],
        out_specs=pl.BlockSpec((2, lp, tn), lambda j: (0, 0, j)), compiler_params=_params(1),
    )(d, w, gu, gu)


def _pool_fwd(name, pooled, pw):
    lp, dm = pooled.shape
    g, gw, _ = pw.shape
    return _matmul(
        name, "nn", pooled, pw, jax.ShapeDtypeStruct((lp, dm), BF16), (g,),
        pl.BlockSpec((lp, gw), lambda gi: (0, gi)), pl.BlockSpec((None, gw, gw), lambda gi: (gi, 0, 0)),
        pl.BlockSpec((lp, gw), lambda gi: (0, gi)), 1)


def _pool_bwd_act(name, dya, pw, deps=()):
    lp, dm = dya.shape
    g, gw, _ = pw.shape
    return _matmul(
        name, "nt", dya, pw, jax.ShapeDtypeStruct((lp, dm), BF16), (g,),
        pl.BlockSpec((lp, gw), lambda gi: (0, gi)), pl.BlockSpec((None, gw, gw), lambda gi: (gi, 0, 0)),
        pl.BlockSpec((lp, gw), lambda gi: (0, gi)), 1, deps=deps)


def _pool_bwd_w(name, pooled, dya):
    lp, dm = pooled.shape
    g = len(POOL_WINDOWS)
    gw = dm // g
    return _matmul(
        name, "tn", pooled, dya, jax.ShapeDtypeStruct((g, gw, gw), BF16), (g,),
        pl.BlockSpec((lp, gw), lambda gi: (0, gi)), pl.BlockSpec((lp, gw), lambda gi: (0, gi)),
        pl.BlockSpec((None, gw, gw), lambda gi: (gi, 0, 0)), 1)


def _rms_fwd(name, h, g, tm, deps=()):
    lp, dm = h.shape

    def body(h_ref, g_ref, *rest):
        hv = h_ref[...]
        r = lax.rsqrt(jnp.mean(hv * hv, axis=-1, keepdims=True) + EPS)
        rest[-1][...] = (hv * r * g_ref[...]).astype(BF16)

    row = pl.BlockSpec((tm, dm), lambda i: (i, 0))
    return pl.pallas_call(
        body, name=name, out_shape=jax.ShapeDtypeStruct((lp, dm), BF16), grid=(lp // tm,),
        in_specs=[row, pl.BlockSpec((1, dm), lambda i: (0, 0))] + [ANY] * len(deps), out_specs=row, compiler_params=_params(1),
    )(h, g, *deps)


def _rms_bwd(name, dy, h, g, dres, tm, dep):
    lp, dm = h.shape

    def body(dy_ref, h_ref, g_ref, dr_ref, _, dh_ref, dhb_ref, dg_ref):
        hv = h_ref[...]
        r = lax.rsqrt(jnp.mean(hv * hv, axis=-1, keepdims=True) + EPS)
        xhat = hv * r
        dyv = dy_ref[...]
        dxh = dyv * g_ref[...]
        dh = dr_ref[...] + r * (dxh - xhat * jnp.mean(dxh * xhat, axis=-1, keepdims=True))
        dh_ref[...] = dh
        dhb_ref[...] = dh.astype(BF16)

        @pl.when(pl.program_id(0) == 0)
        def _():
            dg_ref[...] = jnp.zeros_like(dg_ref)

        dg_ref[0:1, :] += jnp.sum(dyv * xhat, axis=0, keepdims=True)

    row = pl.BlockSpec((tm, dm), lambda i: (i, 0))
    slab = pl.BlockSpec((SMALL_ROWS, dm), lambda i: (0, 0))
    return pl.pallas_call(
        body, name=name, grid=(lp // tm,),
        out_shape=(jax.ShapeDtypeStruct((lp, dm), F32), jax.ShapeDtypeStruct((lp, dm), BF16),
                   jax.ShapeDtypeStruct((SMALL_ROWS, dm), F32)),
        in_specs=[row, row, pl.BlockSpec((1, dm), lambda i: (0, 0)), row, ANY], out_specs=(row, row, slab),
        compiler_params=_params(1),
    )(dy, h, g, dres, dep)


def _rms_bwd_input(name, dy, h, g, dres, tm, seq, dep):
    lp, dm = h.shape
    nx = seq // tm

    def body(dy_ref, h_ref, g_ref, dr_ref, _, dx_ref, dt_ref, dg_ref):
        i = pl.program_id(0)
        hv = h_ref[...]
        r = lax.rsqrt(jnp.mean(hv * hv, axis=-1, keepdims=True) + EPS)
        xhat = hv * r
        dyv = dy_ref[...]
        dxh = dyv * g_ref[...]
        dh = dr_ref[...] + r * (dxh - xhat * jnp.mean(dxh * xhat, axis=-1, keepdims=True))

        @pl.when(i < nx)
        def _():
            dx_ref[...] = dh

        @pl.when(i >= nx)
        def _():
            dt_ref[...] = dh

        @pl.when(i == 0)
        def _():
            dg_ref[...] = jnp.zeros_like(dg_ref)

        dg_ref[0:1, :] += jnp.sum(dyv * xhat, axis=0, keepdims=True)

    row = pl.BlockSpec((tm, dm), lambda i: (i, 0))
    slab = pl.BlockSpec((SMALL_ROWS, dm), lambda i: (0, 0))
    return pl.pallas_call(
        body, name=name, grid=(lp // tm,),
        out_shape=(jax.ShapeDtypeStruct((seq, dm), F32), jax.ShapeDtypeStruct((tm, dm), F32),
                   jax.ShapeDtypeStruct((SMALL_ROWS, dm), F32)),
        in_specs=[row, row, pl.BlockSpec((1, dm), lambda i: (0, 0)), row, ANY],
        out_specs=(pl.BlockSpec((tm, dm), lambda i: (jnp.minimum(i, nx - 1), 0)), pl.BlockSpec((tm, dm), lambda i: (0, 0)), slab),
        compiler_params=_params(1),
    )(dy, h, g, dres, dep)


def _gate_mix(name, proj, b_gate2, ya, pool_scale, yb, tm):
    _, lp, dm = proj.shape

    def body(ga_ref, gr_ref, b_ref, ya_ref, ps_ref, yb_ref, o_ref):
        g_a = jax.nn.sigmoid(ga_ref[...].astype(F32) + b_ref[0:1, :])
        g_b = jax.nn.sigmoid(gr_ref[...].astype(F32) + b_ref[1:2, :])
        y_a = ya_ref[...].astype(F32) * ps_ref[...]
        o_ref[...] = (g_a * y_a + g_b * yb_ref[...].astype(F32)).astype(BF16)

    row = pl.BlockSpec((tm, dm), lambda i: (i, 0))
    return pl.pallas_call(
        body, name=name, out_shape=jax.ShapeDtypeStruct((lp, dm), BF16), grid=(lp // tm,),
        in_specs=[pl.BlockSpec((None, tm, dm), lambda i: (4, i, 0)), pl.BlockSpec((None, tm, dm), lambda i: (5, i, 0)),
                  pl.BlockSpec((2, dm), lambda i: (0, 0)), row, pl.BlockSpec((1, dm), lambda i: (0, 0)), row],
        out_specs=row, compiler_params=_params(1),
    )(proj, proj, b_gate2, ya, pool_scale, yb)


def _gate_bwd(name, dmix, proj, b_gate2, ya, pool_scale, yb, tm):
    _, lp, dm = proj.shape

    def body(dm_ref, ga_ref, gr_ref, b_ref, ya_ref, ps_ref, yb_ref, dp_ref, dyb_ref, dya_ref, db_ref, dps_ref):
        dmx = dm_ref[...].astype(F32)
        g_a = jax.nn.sigmoid(ga_ref[...].astype(F32) + b_ref[0:1, :])
        g_b = jax.nn.sigmoid(gr_ref[...].astype(F32) + b_ref[1:2, :])
        ya_pre = ya_ref[...].astype(F32)
        ybv = yb_ref[...].astype(F32)
        ps = ps_ref[...]
        dga = dmx * (ya_pre * ps) * (g_a * (1.0 - g_a))
        dgr = dmx * ybv * (g_b * (1.0 - g_b))
        dp_ref[0] = dga.astype(BF16)
        dp_ref[1] = dgr.astype(BF16)
        dyb_ref[...] = (dmx * g_b).astype(BF16)
        dya_ref[...] = (dmx * g_a * ps).astype(BF16)

        @pl.when(pl.program_id(0) == 0)
        def _():
            db_ref[...] = jnp.zeros_like(db_ref)
            dps_ref[...] = jnp.zeros_like(dps_ref)

        db_ref[0:1, :] += jnp.sum(dga, axis=0, keepdims=True)
        db_ref[1:2, :] += jnp.sum(dgr, axis=0, keepdims=True)
        dps_ref[0:1, :] += jnp.sum(dmx * g_a * ya_pre, axis=0, keepdims=True)

    row = pl.BlockSpec((tm, dm), lambda i: (i, 0))
    one = pl.BlockSpec((1, dm), lambda i: (0, 0))
    slab = pl.BlockSpec((SMALL_ROWS, dm), lambda i: (0, 0))
    return pl.pallas_call(
        body, name=name, grid=(lp // tm,),
        out_shape=(jax.ShapeDtypeStruct((6, lp, dm), BF16), jax.ShapeDtypeStruct((lp, dm), BF16),
                   jax.ShapeDtypeStruct((lp, dm), BF16), jax.ShapeDtypeStruct((SMALL_ROWS, dm), F32),
                   jax.ShapeDtypeStruct((SMALL_ROWS, dm), F32)),
        in_specs=[row, pl.BlockSpec((None, tm, dm), lambda i: (4, i, 0)), pl.BlockSpec((None, tm, dm), lambda i: (5, i, 0)),
                  pl.BlockSpec((2, dm), lambda i: (0, 0)), row, one, row],
        out_specs=(pl.BlockSpec((2, tm, dm), lambda i: (2, i, 0)), row, row, slab, slab),
        compiler_params=_params(1),
    )(dmix, proj, proj, b_gate2, ya, pool_scale, yb)


def _swiglu_fwd(name, gu, tm):
    _, lp, f = gu.shape

    def body(g_ref, u_ref, o_ref):
        gt = g_ref[...].astype(F32)
        o_ref[...] = (gt * jax.nn.sigmoid(gt) * u_ref[...].astype(F32)).astype(BF16)

    return pl.pallas_call(
        body, name=name, out_shape=jax.ShapeDtypeStruct((lp, f), BF16), grid=(lp // tm,),
        in_specs=[pl.BlockSpec((None, tm, f), lambda i: (0, i, 0)), pl.BlockSpec((None, tm, f), lambda i: (1, i, 0))],
        out_specs=pl.BlockSpec((tm, f), lambda i: (i, 0)), compiler_params=_params(1),
    )(gu, gu)


def _swiglu_bwd(name, dact, gu, tm):
    _, lp, f = gu.shape

    def body(d_ref, g_ref, u_ref, o_ref):
        d = d_ref[...].astype(F32)
        gt = g_ref[...].astype(F32)
        sg = jax.nn.sigmoid(gt)
        o_ref[0] = (d * u_ref[...].astype(F32) * (sg * (1.0 + gt * (1.0 - sg)))).astype(BF16)
        o_ref[1] = (d * (gt * sg)).astype(BF16)

    return pl.pallas_call(
        body, name=name, out_shape=jax.ShapeDtypeStruct((2, lp, f), BF16), grid=(lp // tm,),
        in_specs=[pl.BlockSpec((tm, f), lambda i: (i, 0)), pl.BlockSpec((None, tm, f), lambda i: (0, i, 0)),
                  pl.BlockSpec((None, tm, f), lambda i: (1, i, 0))],
        out_specs=pl.BlockSpec((2, tm, f), lambda i: (0, i, 0)), compiler_params=_params(1),
    )(dact, gu, gu)


def _final_loss(name, h2, g3, target, tm):
    lp, dm = h2.shape
    nx = target.shape[0] // tm

    def body(h_ref, g_ref, t_ref, dh_ref, dhb_ref, ls_ref, dg_ref):
        i = pl.program_id(0)

        @pl.when(i == 0)
        def _():
            ls_ref[...] = jnp.zeros_like(ls_ref)
            dg_ref[...] = jnp.zeros_like(dg_ref)

        @pl.when(i < nx)
        def _():
            hv = h_ref[...]
            gv = g_ref[...]
            r = lax.rsqrt(jnp.mean(hv * hv, axis=-1, keepdims=True) + EPS)
            xhat = hv * r
            err = xhat * gv - t_ref[...]
            dout = err * (1.0 / dm)
            dxh = dout * gv
            dh = r * (dxh - xhat * jnp.mean(dxh * xhat, axis=-1, keepdims=True))
            dh_ref[...] = dh
            dhb_ref[...] = dh.astype(BF16)
            ls_ref[0:1, :] += jnp.sum(err * err, axis=0, keepdims=True)
            dg_ref[0:1, :] += jnp.sum(dout * xhat, axis=0, keepdims=True)

        @pl.when(i >= nx)
        def _():
            dh_ref[...] = jnp.zeros_like(dh_ref)
            dhb_ref[...] = jnp.zeros_like(dhb_ref)

    row = pl.BlockSpec((tm, dm), lambda i: (i, 0))
    slab = pl.BlockSpec((SMALL_ROWS, dm), lambda i: (0, 0))
    return pl.pallas_call(
        body, name=name, grid=(lp // tm,),
        out_shape=(jax.ShapeDtypeStruct((lp, dm), F32), jax.ShapeDtypeStruct((lp, dm), BF16),
                   jax.ShapeDtypeStruct((SMALL_ROWS, dm), F32), jax.ShapeDtypeStruct((SMALL_ROWS, dm), F32)),
        in_specs=[row, pl.BlockSpec((1, dm), lambda i: (0, 0)), pl.BlockSpec((tm, dm), lambda i: (jnp.minimum(i, nx - 1), 0))],
        out_specs=(row, row, slab, slab), compiler_params=_params(1),
    )(h2, g3, target)


def _shift(v, k):
    return pltpu.roll(v, k % v.shape[0], axis=0)


def _window_sum(v, group, sign):
    s2 = v + _shift(v, sign * 1)
    s4 = s2 + _shift(s2, sign * 2)
    s8 = s4 + _shift(s4, sign * 4)
    s16 = s8 + _shift(s8, sign * 8)
    return jnp.where(group == 0, s2, jnp.where(group == 1, s4, jnp.where(group == 2, s8, s16)))


def _pool_count(lp, group):
    row = lax.broadcasted_iota(jnp.int32, (lp, 1), 0)
    window = jnp.left_shift(2, group).astype(F32)
    meta_pos = (row - (lp - N_META) + 1).astype(F32)
    return jnp.where(row >= lp - N_META, jnp.minimum(meta_pos, window), window)


def _mixer_fwd(name, proj, conv_w, tc, dep):
    _, lp, dm = proj.shape
    per_group = dm // len(POOL_WINDOWS) // tc

    def body(u_ref, gb_ref, gc_ref, v_ref, cw_ref, _, p_ref, z_ref):
        group = pl.program_id(0) // per_group
        u = u_ref[...].astype(F32)
        p_ref[...] = (_window_sum(u, group, 1) / _pool_count(lp, group) - u).astype(BF16)
        cv = gc_ref[...].astype(F32) * v_ref[...].astype(F32)
        conv = cw_ref[0:1, :] * _shift(cv, 2) + cw_ref[1:2, :] * _shift(cv, 1) + cw_ref[2:3, :] * cv
        z_ref[...] = (gb_ref[...].astype(F32) * conv).astype(BF16)

    def seg(s):
        return pl.BlockSpec((None, lp, tc), lambda j: (s, 0, j))

    col = pl.BlockSpec((lp, tc), lambda j: (0, j))
    return pl.pallas_call(
        body, name=name, grid=(dm // tc,),
        out_shape=(jax.ShapeDtypeStruct((lp, dm), BF16), jax.ShapeDtypeStruct((lp, dm), BF16)),
        in_specs=[seg(0), seg(1), seg(2), seg(3), pl.BlockSpec((3, tc), lambda j: (0, j)), ANY],
        out_specs=(col, col), compiler_params=_params(1),
    )(proj, proj, proj, proj, conv_w, dep)


def _mixer_bwd(name, dz, dpooled, proj, conv_w, dproj, tc, dep):
    _, lp, dm = proj.shape
    per_group = dm // len(POOL_WINDOWS) // tc

    def body(dz_ref, dp_ref, gb_ref, gc_ref, v_ref, cw_ref, _, __, o_ref, dcw_ref):
        group = pl.program_id(0) // per_group
        dzv = dz_ref[...].astype(F32)
        gb = gb_ref[...].astype(F32)
        gc = gc_ref[...].astype(F32)
        vv = v_ref[...].astype(F32)
        cv = gc * vv
        c1 = _shift(cv, 1)
        c2 = _shift(cv, 2)
        w0, w1, w2 = cw_ref[0:1, :], cw_ref[1:2, :], cw_ref[2:3, :]
        o_ref[1] = (dzv * (w0 * c2 + w1 * c1 + w2 * cv)).astype(BF16)
        dconv = dzv * gb
        dcw_ref[...] = jnp.zeros_like(dcw_ref)
        dcw_ref[0:1, :] = jnp.sum(dconv * c2, axis=0, keepdims=True)
        dcw_ref[1:2, :] = jnp.sum(dconv * c1, axis=0, keepdims=True)
        dcw_ref[2:3, :] = jnp.sum(dconv * cv, axis=0, keepdims=True)
        dcv = w0 * _shift(dconv, -2) + w1 * _shift(dconv, -1) + w2 * dconv
        o_ref[2] = (dcv * vv).astype(BF16)
        o_ref[3] = (dcv * gc).astype(BF16)
        dpv = dp_ref[...].astype(F32)
        o_ref[0] = (_window_sum(dpv / _pool_count(lp, group), group, -1) - dpv).astype(BF16)

    def seg(s):
        return pl.BlockSpec((None, lp, tc), lambda j: (s, 0, j))

    col = pl.BlockSpec((lp, tc), lambda j: (0, j))
    return pl.pallas_call(
        body, name=name, grid=(dm // tc,),
        out_shape=(jax.ShapeDtypeStruct(dproj.shape, BF16), jax.ShapeDtypeStruct((SMALL_ROWS, dm), F32)),
        in_specs=[col, col, seg(1), seg(2), seg(3), pl.BlockSpec((3, tc), lambda j: (0, j)), ANY, ANY],
        out_specs=(pl.BlockSpec((4, lp, tc), lambda j: (0, 0, j)), pl.BlockSpec((SMALL_ROWS, tc), lambda j: (0, j))),
        input_output_aliases={6: 0}, compiler_params=_params(1),
    )(dz, dpooled, proj, proj, proj, conv_w, dproj, dep)


def _row_tile(r, c, bytes_per_row_elem=4, budget=2 * 1024 * 1024):
    best = None
    for t in range(16, r + 1, 16):
        if r % t == 0 and t * c * bytes_per_row_elem <= budget:
            best = t
    return best if best is not None else r


def _pair_add(name, g4, recv, core):
    s, r, c = g4.shape
    h = r // 2
    tr = _row_tile(h, c, budget=6 * 1024 * 1024)
    nb = h // tr

    def body(core_ref, g_ref, r_ref, o_ref):
        o_ref[...] = (g_ref[...].astype(F32) + r_ref[...].astype(F32)).astype(BF16)

    grid_spec = pltpu.PrefetchScalarGridSpec(
        num_scalar_prefetch=1, grid=(s, nb),
        in_specs=[pl.BlockSpec((None, tr, c), lambda si, j, core_ref: (si, core_ref[0] * nb + j, 0)),
                  pl.BlockSpec((None, tr, c), lambda si, j, core_ref: (si, j, 0))],
        out_specs=pl.BlockSpec((None, tr, c), lambda si, j, core_ref: (si, j, 0)))
    return pl.pallas_call(
        body, name=name, out_shape=jax.ShapeDtypeStruct((s, h, c), BF16), grid_spec=grid_spec,
        compiler_params=_params(2),
    )(core, g4, recv)


def _chip_sum(name, parts, recv, chip):
    _, h, c = parts.shape
    tr = _row_tile(h, c)

    def body(chip_ref, p_ref, r_ref, o_ref):
        acc = p_ref[...].astype(F32)
        for i in range(len(CHIP_FLIPS)):
            acc = acc + r_ref[i].astype(F32)
        o_ref[...] = acc

    grid_spec = pltpu.PrefetchScalarGridSpec(
        num_scalar_prefetch=1, grid=(h // tr,),
        in_specs=[pl.BlockSpec((None, tr, c), lambda j, chip_ref: (chip_ref[0], j, 0)),
                  pl.BlockSpec((len(CHIP_FLIPS), tr, c), lambda j, chip_ref: (0, j, 0))],
        out_specs=pl.BlockSpec((tr, c), lambda j, chip_ref: (j, 0)))
    return pl.pallas_call(
        body, name=name, out_shape=jax.ShapeDtypeStruct((h, c), F32), grid_spec=grid_spec, compiler_params=_params(1),
    )(chip, parts, recv)


def _adam_update(w, gv, m, v):
    c1 = 1.0 - ADAM_B1 ** ADAM_STEP
    c2 = 1.0 - ADAM_B2 ** ADAM_STEP
    nm = ADAM_B1 * m + (1.0 - ADAM_B1) * gv
    nv = ADAM_B2 * v + (1.0 - ADAM_B2) * (gv * gv)
    return -ADAM_LR * ((nm / c1) / (jnp.sqrt(nv / c2) + ADAM_EPS) + ADAM_WD * w), nm, nv


def _adamw_halves(name, w, g_own, g_sib, m, v, core, part=(0, 1), prev=None):
    r, c = w.shape
    rp = r // part[1]
    h = rp // 2
    tr = _row_tile(h, c, budget=1024 * 1024)
    nbh = h // tr
    j0 = part[0] * 2 * nbh
    n_prev = 0 if prev is None else 4

    def body(core_ref, w_ref, go_ref, gs_ref, m_ref, v_ref, *rest):
        g_ref, d_ref, nm_ref, nv_ref = rest[n_prev:]
        mine = (pl.program_id(0) // nbh) == core_ref[0]
        gv = jnp.where(mine, go_ref[...], gs_ref[...])
        g_ref[...] = gv
        d_ref[...], nm_ref[...], nv_ref[...] = _adam_update(w_ref[...], gv, m_ref[...], v_ref[...])

    def blk(fn):
        return pl.BlockSpec((tr, c), fn)

    full = blk(lambda j, core_ref: (j0 + j, 0))
    own = blk(lambda j, core_ref: (jnp.clip(j - core_ref[0] * nbh, 0, nbh - 1), 0))
    sib = blk(lambda j, core_ref: (jnp.clip(j - (1 - core_ref[0]) * nbh, 0, nbh - 1), 0))
    grid_spec = pltpu.PrefetchScalarGridSpec(
        num_scalar_prefetch=1, grid=(2 * nbh,), in_specs=[full, own, sib, full, full] + [ANY] * n_prev, out_specs=(full,) * 4)
    sds = jax.ShapeDtypeStruct((r, c), F32)
    return pl.pallas_call(
        body, name=name, out_shape=(sds,) * 4, grid_spec=grid_spec, compiler_params=_params(1),
        input_output_aliases={6 + i: i for i in range(n_prev)},
    )(core, w, g_own, g_sib, m, v, *(prev or ()))


def _adamw(name, w, g, m, v):
    r, c = w.shape

    def body(w_ref, g_ref, m_ref, v_ref, d_ref, nm_ref, nv_ref):
        d_ref[...], nm_ref[...], nv_ref[...] = _adam_update(w_ref[...], g_ref[...], m_ref[...], v_ref[...])

    blk = pl.BlockSpec((r, c), lambda j: (0, 0))
    sds = jax.ShapeDtypeStruct((r, c), F32)
    return pl.pallas_call(
        body, name=name, out_shape=(sds, sds, sds), grid=(1,), in_specs=[blk] * 4, out_specs=(blk,) * 3,
        compiler_params=_params(1),
    )(w, g, m, v)


def _cast_into_slot(name, w, chip, dtype, deps=()):
    r, c = w.shape
    tr = _row_tile(r, c)

    def body(chip_ref, w_ref, *rest):
        rest[-1][...] = w_ref[...].astype(dtype)

    grid_spec = pltpu.PrefetchScalarGridSpec(
        num_scalar_prefetch=1, grid=(r // tr,),
        in_specs=[pl.BlockSpec((tr, c), lambda j, chip_ref: (j, 0))] + [ANY] * len(deps),
        out_specs=pl.BlockSpec((None, tr, c), lambda j, chip_ref: (chip_ref[0], j, 0)))
    return pl.pallas_call(
        body, name=name, out_shape=jax.ShapeDtypeStruct((4, r, c), dtype), grid_spec=grid_spec, compiler_params=_params(1),
    )(chip, w, *deps)


def _place():
    return lax.axis_index("x"), lax.axis_index("y"), lax.axis_index("c")


def _chip_of(x, y, flip):
    px, py = x ^ flip[0], y ^ flip[1]
    return px, py, 2 * px + py


def _half(ref, which):
    rows = ref.shape[0] // 2
    return ref.at[pl.ds(which * rows, rows)]


HBM = pl.BlockSpec(memory_space=pltpu.HBM)
SEM = pl.BlockSpec(memory_space=pltpu.SEMAPHORE)
SPLIT_COPY = pltpu.CompilerParams(has_side_effects=pltpu.SideEffectType.DATAFLOW_SIDE_EFFECTING)


def _in_hbm(arrays):
    return [pltpu.with_memory_space_constraint(t, pltpu.HBM) for t in arrays]


TOKEN = jax.ShapeDtypeStruct((SMALL_ROWS, LANES), F32)
TOKEN_SPEC = pl.BlockSpec(memory_space=pltpu.VMEM)


def _gather_start(name, slabs, groups):
    n = len(slabs)
    ng = len(groups)
    nf = len(CHIP_FLIPS)

    def body(*refs):
        sems, outs = refs[n:n + 2 * ng], refs[n + 2 * ng:2 * n + 2 * ng]
        token = refs[2 * n + 2 * ng]
        token[...] = jnp.zeros_like(token)
        x, y, c = _place()
        k = 2 * x + y
        for g, members in enumerate(groups):
            for i, a in enumerate(members):
                for j, flip in enumerate(CHIP_FLIPS):
                    px, py, _ = _chip_of(x, y, flip)
                    mine = _half(outs[a].at[k], c)
                    pltpu.make_async_remote_copy(
                        src_ref=mine, dst_ref=mine, send_sem=sems[2 * g].at[i * nf + j], recv_sem=sems[2 * g + 1].at[i * nf + j],
                        device_id=(px, py, c), device_id_type=MESH).start()

    sem_shapes = []
    for members in groups:
        sem_shapes += [pltpu.SemaphoreType.DMA((nf * len(members),))] * 2
    res = pl.pallas_call(
        body, name=name,
        out_shape=tuple(sem_shapes) + tuple(pltpu.HBM(t.shape, t.dtype) for t in slabs) + (TOKEN,),
        in_specs=[HBM] * n, out_specs=tuple([SEM] * (2 * ng) + [HBM] * n + [TOKEN_SPEC]),
        input_output_aliases={a: 2 * ng + a for a in range(n)}, compiler_params=SPLIT_COPY,
    )(*_in_hbm(slabs))
    return [(res[2 * g], res[2 * g + 1]) for g in range(ng)], list(res[2 * ng:2 * ng + n]), res[2 * ng + n]


def _gather_wait(name, slabs, sems, after):
    n = len(slabs)
    nf = len(CHIP_FLIPS)

    def body(*refs):
        ins = refs[:n]
        ssem, rsem = refs[n], refs[n + 1]
        x, y, c = _place()
        k = 2 * x + y
        for a in range(n):
            for j, flip in enumerate(CHIP_FLIPS):
                _, _, kj = _chip_of(x, y, flip)
                cp = pltpu.make_async_remote_copy(
                    src_ref=_half(ins[a].at[k], c), dst_ref=_half(ins[a].at[kj], c), send_sem=ssem.at[a * nf + j],
                    recv_sem=rsem.at[a * nf + j], device_id=(x, y, c), device_id_type=MESH)
                cp.wait_send()
                cp.wait_recv()

    return pl.pallas_call(
        body, name=name, out_shape=tuple(pltpu.HBM(t.shape, t.dtype) for t in slabs),
        in_specs=[HBM] * n + [SEM, SEM, ANY], out_specs=tuple([HBM] * n),
        input_output_aliases={a: a for a in range(n)}, compiler_params=SPLIT_COPY,
    )(*slabs, sems[0], sems[1], after)


def _gather_pass(name, slabs, sems, after):
    n = len(slabs)
    nf = len(CHIP_FLIPS)

    def body(*refs):
        ins = refs[:n]
        ssem, rsem = refs[n], refs[n + 1]
        ssem2, rsem2 = refs[n + 3], refs[n + 4]
        x, y, c = _place()
        k = 2 * x + y
        for a in range(n):
            for j, flip in enumerate(CHIP_FLIPS):
                _, _, kj = _chip_of(x, y, flip)
                landed = _half(ins[a].at[kj], c)
                cp = pltpu.make_async_remote_copy(
                    src_ref=_half(ins[a].at[k], c), dst_ref=landed, send_sem=ssem.at[a * nf + j],
                    recv_sem=rsem.at[a * nf + j], device_id=(x, y, c), device_id_type=MESH)
                cp.wait_send()
                cp.wait_recv()
                pltpu.make_async_remote_copy(
                    src_ref=landed, dst_ref=landed, send_sem=ssem2.at[a * nf + j], recv_sem=rsem2.at[a * nf + j],
                    device_id=(x, y, 1 - c), device_id_type=MESH).start()

    sem = pltpu.SemaphoreType.DMA((nf * n,))
    res = pl.pallas_call(
        body, name=name, out_shape=(sem, sem) + tuple(pltpu.HBM(t.shape, t.dtype) for t in slabs),
        in_specs=[HBM] * n + [SEM, SEM, ANY], out_specs=tuple([SEM, SEM] + [HBM] * n),
        input_output_aliases={a: 2 + a for a in range(n)}, compiler_params=SPLIT_COPY,
    )(*slabs, sems[0], sems[1], after)
    return (res[0], res[1]), list(res[2:])


def _pass_wait(name, slabs, sems, after):
    n = len(slabs)
    nf = len(CHIP_FLIPS)

    def body(*refs):
        ins = refs[:n]
        ssem, rsem = refs[n], refs[n + 1]
        x, y, c = _place()
        for a in range(n):
            for j, flip in enumerate(CHIP_FLIPS):
                _, _, kj = _chip_of(x, y, flip)
                cp = pltpu.make_async_remote_copy(
                    src_ref=_half(ins[a].at[kj], c), dst_ref=_half(ins[a].at[kj], 1 - c), send_sem=ssem.at[a * nf + j],
                    recv_sem=rsem.at[a * nf + j], device_id=(x, y, c), device_id_type=MESH)
                cp.wait_send()
                cp.wait_recv()

    return pl.pallas_call(
        body, name=name, out_shape=tuple(pltpu.HBM(t.shape, t.dtype) for t in slabs),
        in_specs=[HBM] * n + [SEM, SEM, ANY], out_specs=tuple([HBM] * n),
        input_output_aliases={a: a for a in range(n)}, compiler_params=SPLIT_COPY,
    )(*slabs, sems[0], sems[1], after)


NEIGHBOUR_FLIPS = CHIP_FLIPS[:2]


def _relay_chips(x, y, c):
    fx, fy = x ^ c, y ^ (1 - c)
    return (fx, fy), 2 * fx + fy, 2 * (1 - x) + (1 - y)


def _ag_start(name, slabs, deps=()):
    n = len(slabs)
    nn = len(NEIGHBOUR_FLIPS)

    def body(*refs):
        no = n + len(deps)
        ssem, rsem = refs[no], refs[no + 1]
        outs = refs[no + 2:no + 2 + n]
        token = refs[no + 2 + n]
        token[...] = jnp.zeros_like(token)
        x, y, c = _place()
        k = 2 * x + y
        for a in range(n):
            for j, flip in enumerate(NEIGHBOUR_FLIPS):
                px, py, _ = _chip_of(x, y, flip)
                mine = _half(outs[a].at[k], c)
                pltpu.make_async_remote_copy(src_ref=mine, dst_ref=mine, send_sem=ssem.at[a * nn + j],
                                             recv_sem=rsem.at[a * nn + j], device_id=(px, py, c), device_id_type=MESH).start()

    sem = pltpu.SemaphoreType.DMA((nn * n,))
    res = pl.pallas_call(
        body, name=name, out_shape=(sem, sem) + tuple(pltpu.HBM(t.shape, t.dtype) for t in slabs) + (TOKEN,),
        in_specs=[HBM] * n + [ANY] * len(deps), out_specs=tuple([SEM, SEM] + [HBM] * n + [TOKEN_SPEC]),
        input_output_aliases={a: 2 + a for a in range(n)}, compiler_params=SPLIT_COPY,
    )(*_in_hbm(slabs), *deps)
    return (res[0], res[1]), list(res[2:2 + n]), res[2 + n]


def _ag_relay(name, slabs, sems, after):
    n = len(slabs)
    nn = len(NEIGHBOUR_FLIPS)

    def body(*refs):
        no = n + 2 + len(after)
        ins = refs[:n]
        ssem, rsem = refs[n], refs[n + 1]
        r_s, r_r, p_s, p_r = refs[no:no + 4]
        x, y, c = _place()
        k = 2 * x + y
        (fx, fy), _, _ = _relay_chips(x, y, c)
        for a in range(n):
            for j, flip in enumerate(NEIGHBOUR_FLIPS):
                _, _, kj = _chip_of(x, y, flip)
                landed = _half(ins[a].at[kj], c)
                cp = pltpu.make_async_remote_copy(
                    src_ref=_half(ins[a].at[k], c), dst_ref=landed, send_sem=ssem.at[a * nn + j],
                    recv_sem=rsem.at[a * nn + j], device_id=(x, y, c), device_id_type=MESH)
                cp.wait_send()
                cp.wait_recv()
        for a in range(n):
            near = _half(ins[a].at[2 * (x ^ (1 - c)) + (y ^ c)], c)
            pltpu.make_async_remote_copy(src_ref=near, dst_ref=near, send_sem=r_s.at[a], recv_sem=r_r.at[a],
                                         device_id=(fx, fy, c), device_id_type=MESH).start()
            for j, flip in enumerate(NEIGHBOUR_FLIPS):
                _, _, kj = _chip_of(x, y, flip)
                landed = _half(ins[a].at[kj], c)
                pltpu.make_async_remote_copy(src_ref=landed, dst_ref=landed, send_sem=p_s.at[a * nn + j],
                                             recv_sem=p_r.at[a * nn + j], device_id=(x, y, 1 - c), device_id_type=MESH).start()

    rsem_t = pltpu.SemaphoreType.DMA((n,))
    psem_t = pltpu.SemaphoreType.DMA((nn * n,))
    res = pl.pallas_call(
        body, name=name, out_shape=(rsem_t, rsem_t, psem_t, psem_t) + tuple(pltpu.HBM(t.shape, t.dtype) for t in slabs),
        in_specs=[HBM] * n + [SEM, SEM] + [ANY] * len(after), out_specs=tuple([SEM] * 4 + [HBM] * n),
        input_output_aliases={a: 4 + a for a in range(n)}, compiler_params=SPLIT_COPY,
    )(*slabs, sems[0], sems[1], *after)
    return tuple(res[:4]), list(res[4:])


def _ag_relay_wait(name, slabs, sems, after):
    n = len(slabs)
    nn = len(NEIGHBOUR_FLIPS)

    def body(*refs):
        no = n + 4 + len(after)
        ins = refs[:n]
        r_s, r_r, p_s, p_r = refs[n:n + 4]
        f_s, f_r = refs[no], refs[no + 1]
        x, y, c = _place()
        _, _, kd = _relay_chips(x, y, c)
        for a in range(n):
            near = _half(ins[a].at[2 * (x ^ (1 - c)) + (y ^ c)], c)
            diag = _half(ins[a].at[kd], c)
            cp = pltpu.make_async_remote_copy(src_ref=near, dst_ref=diag, send_sem=r_s.at[a], recv_sem=r_r.at[a],
                                              device_id=(x, y, c), device_id_type=MESH)
            cp.wait_send()
            cp.wait_recv()
            for j, flip in enumerate(NEIGHBOUR_FLIPS):
                _, _, kj = _chip_of(x, y, flip)
                cp = pltpu.make_async_remote_copy(
                    src_ref=_half(ins[a].at[kj], c), dst_ref=_half(ins[a].at[kj], 1 - c), send_sem=p_s.at[a * nn + j],
                    recv_sem=p_r.at[a * nn + j], device_id=(x, y, c), device_id_type=MESH)
                cp.wait_send()
                cp.wait_recv()
            pltpu.make_async_remote_copy(src_ref=diag, dst_ref=diag, send_sem=f_s.at[a], recv_sem=f_r.at[a],
                                         device_id=(x, y, 1 - c), device_id_type=MESH).start()

    sem = pltpu.SemaphoreType.DMA((n,))
    res = pl.pallas_call(
        body, name=name, out_shape=(sem, sem) + tuple(pltpu.HBM(t.shape, t.dtype) for t in slabs),
        in_specs=[HBM] * n + [SEM] * 4 + [ANY] * len(after), out_specs=tuple([SEM, SEM] + [HBM] * n),
        input_output_aliases={a: 2 + a for a in range(n)}, compiler_params=SPLIT_COPY,
    )(*slabs, *sems, *after)
    return (res[0], res[1]), list(res[2:])


def _ag_final_wait(name, slabs, sems, after):
    n = len(slabs)

    def body(*refs):
        ins = refs[:n]
        f_s, f_r = refs[n], refs[n + 1]
        x, y, c = _place()
        _, _, kd = _relay_chips(x, y, c)
        for a in range(n):
            cp = pltpu.make_async_remote_copy(
                src_ref=_half(ins[a].at[kd], c), dst_ref=_half(ins[a].at[kd], 1 - c), send_sem=f_s.at[a], recv_sem=f_r.at[a],
                device_id=(x, y, c), device_id_type=MESH)
            cp.wait_send()
            cp.wait_recv()

    return pl.pallas_call(
        body, name=name, out_shape=tuple(pltpu.HBM(t.shape, t.dtype) for t in slabs),
        in_specs=[HBM] * n + [SEM, SEM] + [ANY] * len(after), out_specs=tuple([HBM] * n),
        input_output_aliases={a: a for a in range(n)}, compiler_params=SPLIT_COPY,
    )(*slabs, sems[0], sems[1], *after)


def _sibling_part(ref, c, halves):
    if not halves:
        return ref
    h = ref.shape[1] // 2
    return ref.at[:, pl.ds((1 - c) * h, h)]


def _swap_start(name, grads, halves=True, deps=()):
    n = len(grads)

    def body(*refs):
        no = 2 * n + len(deps)
        ssem, rsem = refs[no], refs[no + 1]
        src, land = refs[no + 2:no + n + 2], refs[no + n + 2:no + 2 * n + 2]
        token = refs[no + 2 * n + 2]
        token[...] = jnp.zeros_like(token)
        x, y, c = _place()
        for a in range(n):
            pltpu.make_async_remote_copy(
                src_ref=_sibling_part(src[a], c, halves), dst_ref=land[a], send_sem=ssem.at[a], recv_sem=rsem.at[a],
                device_id=(x, y, 1 - c), device_id_type=MESH).start()

    zones = [lax.empty((g.shape[0], g.shape[1] // 2, g.shape[2]) if halves else g.shape, g.dtype) for g in grads]
    sem = pltpu.SemaphoreType.DMA((n,))
    res = pl.pallas_call(
        body, name=name,
        out_shape=(sem, sem) + tuple(pltpu.HBM(t.shape, t.dtype) for t in list(grads) + zones) + (TOKEN,),
        in_specs=[HBM] * (2 * n) + [ANY] * len(deps), out_specs=tuple([SEM, SEM] + [HBM] * (2 * n) + [TOKEN_SPEC]),
        input_output_aliases={i: 2 + i for i in range(2 * n)}, compiler_params=SPLIT_COPY,
    )(*_in_hbm(list(grads) + zones), *deps)
    return (res[0], res[1], list(res[2:2 + n]), list(res[2 + n:2 + 2 * n])), res[2 + 2 * n]


def _swap_wait(name, ssem, rsem, grads, zones, after, halves=True):
    n = len(grads)

    def body(*refs):
        src, land = refs[:n], refs[n:2 * n]
        ss, rs = refs[2 * n], refs[2 * n + 1]
        x, y, c = _place()
        for a in range(n):
            cp = pltpu.make_async_remote_copy(
                src_ref=_sibling_part(src[a], c, halves), dst_ref=land[a], send_sem=ss.at[a], recv_sem=rs.at[a],
                device_id=(x, y, c), device_id_type=MESH)
            cp.wait_send()
            cp.wait_recv()

    res = pl.pallas_call(
        body, name=name, out_shape=tuple(pltpu.HBM(t.shape, t.dtype) for t in list(grads) + list(zones)),
        in_specs=[HBM] * (2 * n) + [SEM, SEM] + [ANY] * len(after), out_specs=tuple([HBM] * (2 * n)),
        input_output_aliases={i: i for i in range(2 * n)}, compiler_params=SPLIT_COPY,
    )(*grads, *zones, ssem, rsem, *after)
    return list(res[:n]), list(res[n:])


def _sibling_exchange(name, slabs):
    n = len(slabs)
    nf = len(CHIP_FLIPS)

    def body(*refs):
        outs = refs[n:2 * n]
        ssem, rsem = refs[2 * n:]
        x, y, c = _place()

        def copy(a, j, which, to):
            _, _, kj = _chip_of(x, y, CHIP_FLIPS[j])
            ref = _half(outs[a].at[kj], which)
            return pltpu.make_async_remote_copy(src_ref=ref, dst_ref=ref, send_sem=ssem.at[a * nf + j],
                                                recv_sem=rsem.at[a * nf + j], device_id=to, device_id_type=MESH)

        sends = [copy(a, j, c, (x, y, 1 - c)) for a in range(n) for j in range(nf)]
        for cp in sends:
            cp.start()
        for a in range(n):
            for j in range(nf):
                copy(a, j, 1 - c, (x, y, c)).wait_recv()
        for cp in sends:
            cp.wait_send()

    return pl.pallas_call(
        body, name=name, out_shape=tuple(jax.ShapeDtypeStruct(t.shape, t.dtype) for t in slabs),
        in_specs=[ANY] * n, out_specs=(ANY,) * n, input_output_aliases={a: a for a in range(n)},
        scratch_shapes=[pltpu.SemaphoreType.DMA((nf * n,)), pltpu.SemaphoreType.DMA((nf * n,))],
    )(*slabs)


def _sibling_swap(name, grads):
    n = len(grads)

    def body(*refs):
        ins, outs = refs[:n], refs[n:2 * n]
        ssem, rsem = refs[2 * n:]
        x, y, c = _place()
        cps = []
        for a in range(n):
            h = ins[a].shape[1] // 2
            cps.append(pltpu.make_async_remote_copy(
                src_ref=ins[a].at[:, pl.ds((1 - c) * h, h)], dst_ref=outs[a], send_sem=ssem.at[a], recv_sem=rsem.at[a],
                device_id=(x, y, 1 - c), device_id_type=MESH))
        for cp in cps:
            cp.start()
        for cp in cps:
            cp.wait()

    return pl.pallas_call(
        body, name=name,
        out_shape=tuple(jax.ShapeDtypeStruct((g.shape[0], g.shape[1] // 2, g.shape[2]), g.dtype) for g in grads),
        in_specs=[ANY] * n, out_specs=(ANY,) * n,
        scratch_shapes=[pltpu.SemaphoreType.DMA((n,)), pltpu.SemaphoreType.DMA((n,))],
    )(*grads)


def _scatter_start(name, parts):
    n = len(parts)
    nf = len(CHIP_FLIPS)

    def body(*refs):
        ssem, rsem = refs[2 * n], refs[2 * n + 1]
        src, land = refs[2 * n + 2:3 * n + 2], refs[3 * n + 2:4 * n + 2]
        token = refs[4 * n + 2]
        token[...] = jnp.zeros_like(token)
        x, y, c = _place()
        for a in range(n):
            for j, flip in enumerate(CHIP_FLIPS):
                px, py, kj = _chip_of(x, y, flip)
                pltpu.make_async_remote_copy(
                    src_ref=src[a].at[kj], dst_ref=land[a].at[j], send_sem=ssem.at[a * nf + j], recv_sem=rsem.at[a * nf + j],
                    device_id=(px, py, c), device_id_type=MESH).start()

    zones = [lax.empty((nf,) + p.shape[1:], p.dtype) for p in parts]
    sem = pltpu.SemaphoreType.DMA((nf * n,))
    res = pl.pallas_call(
        body, name=name,
        out_shape=(sem, sem) + tuple(pltpu.HBM(t.shape, t.dtype) for t in list(parts) + zones)
        + (jax.ShapeDtypeStruct((SMALL_ROWS, LANES), F32),),
        in_specs=[HBM] * (2 * n),
        out_specs=tuple([SEM, SEM] + [HBM] * (2 * n) + [pl.BlockSpec(memory_space=pltpu.VMEM)]),
        input_output_aliases={i: 2 + i for i in range(2 * n)}, compiler_params=SPLIT_COPY,
    )(*_in_hbm(list(parts) + zones))
    return (res[0], res[1], list(res[2:2 + n]), list(res[2 + n:2 + 2 * n])), res[2 + 2 * n]


def _scatter_wait(name, ssem, rsem, parts, zones, after):
    n = len(parts)
    nf = len(CHIP_FLIPS)

    def body(*refs):
        src, land = refs[:n], refs[n:2 * n]
        ss, rs = refs[2 * n], refs[2 * n + 1]
        x, y, c = _place()
        for a in range(n):
            for j, flip in enumerate(CHIP_FLIPS):
                _, _, kj = _chip_of(x, y, flip)
                cp = pltpu.make_async_remote_copy(
                    src_ref=src[a].at[kj], dst_ref=land[a].at[j], send_sem=ss.at[a * nf + j], recv_sem=rs.at[a * nf + j],
                    device_id=(x, y, c), device_id_type=MESH)
                cp.wait_send()
                cp.wait_recv()

    res = pl.pallas_call(
        body, name=name, out_shape=tuple(pltpu.HBM(t.shape, t.dtype) for t in list(parts) + list(zones)),
        in_specs=[HBM] * (2 * n) + [SEM, SEM] + [ANY] * len(after), out_specs=tuple([HBM] * (2 * n)),
        input_output_aliases={i: i for i in range(2 * n)}, compiler_params=SPLIT_COPY,
    )(*parts, *zones, ssem, rsem, *after)
    return list(res[:n]), list(res[n:])


N_PEERS = 7


def _peer(x, y, c, mask):
    px, py, pc = x ^ ((mask >> 2) & 1), y ^ ((mask >> 1) & 1), c ^ (mask & 1)
    return (px, py, pc), 4 * px + 2 * py + pc


def _reduce_start(vec, deps):
    nd = len(deps)

    def body(*refs):
        ssem, rsem, src, land, token = refs[2 + nd:]
        token[...] = jnp.zeros_like(token)
        x, y, c = _place()
        me = 4 * x + 2 * y + c
        for mask in range(1, N_PEERS + 1):
            to, _ = _peer(x, y, c, mask)
            pltpu.make_async_remote_copy(src_ref=src, dst_ref=land.at[me], send_sem=ssem.at[mask - 1],
                                         recv_sem=rsem.at[mask - 1], device_id=to, device_id_type=MESH).start()

    zone = lax.empty((N_PEERS + 1,) + vec.shape, vec.dtype)
    sem = pltpu.SemaphoreType.DMA((N_PEERS,))
    res = pl.pallas_call(
        body, name="reduce_start",
        out_shape=(sem, sem, pltpu.HBM(vec.shape, vec.dtype), pltpu.HBM(zone.shape, zone.dtype), TOKEN),
        in_specs=[HBM, HBM] + [ANY] * nd, out_specs=(SEM, SEM, HBM, HBM, TOKEN_SPEC),
        input_output_aliases={0: 2, 1: 3}, compiler_params=SPLIT_COPY,
    )(*_in_hbm([vec, zone]), *deps)
    return res[:4], res[4]


def _reduce_wait(ssem, rsem, vec, zone, after):
    def body(src, land, ss, rs, *_):
        x, y, c = _place()
        for mask in range(1, N_PEERS + 1):
            _, frm = _peer(x, y, c, mask)
            cp = pltpu.make_async_remote_copy(src_ref=src, dst_ref=land.at[frm], send_sem=ss.at[mask - 1],
                                              recv_sem=rs.at[mask - 1], device_id=(x, y, c), device_id_type=MESH)
            cp.wait_send()
            cp.wait_recv()

    return pl.pallas_call(
        body, name="reduce_wait", out_shape=(pltpu.HBM(vec.shape, vec.dtype), pltpu.HBM(zone.shape, zone.dtype)),
        in_specs=[HBM, HBM, SEM, SEM] + [ANY] * len(after), out_specs=(HBM, HBM),
        input_output_aliases={0: 0, 1: 1}, compiler_params=SPLIT_COPY,
    )(vec, zone, ssem, rsem, *after)


def _reduce_sum(vec, zone, me, loss_row, loss_scale):
    r, dm = vec.shape

    def body(me_ref, v_ref, z_ref, o_ref, l_ref):
        acc = None
        for i in range(N_PEERS + 1):
            term = jnp.where(me_ref[0] == i, v_ref[...], z_ref[i])
            acc = term if acc is None else acc + term
        o_ref[...] = acc
        l_ref[...] = jnp.sum(acc[loss_row:loss_row + SMALL_ROWS, :], axis=(0, 1), keepdims=True) * loss_scale

    grid_spec = pltpu.PrefetchScalarGridSpec(
        num_scalar_prefetch=1, grid=(1,),
        in_specs=[pl.BlockSpec((r, dm), lambda i, me_ref: (0, 0)), pl.BlockSpec((N_PEERS + 1, r, dm), lambda i, me_ref: (0, 0, 0))],
        out_specs=(pl.BlockSpec((r, dm), lambda i, me_ref: (0, 0)), pl.BlockSpec((1, 1), lambda i, me_ref: (0, 0))))
    return pl.pallas_call(
        body, name="reduce_sum", out_shape=(jax.ShapeDtypeStruct((r, dm), F32), jax.ShapeDtypeStruct((1, 1), F32)),
        grid_spec=grid_spec, compiler_params=_params(1),
    )(me, vec, zone)


def kernel(x, meta_tokens, norm_mix_g, w_in, b_gate, pool_w, pool_scale, conv_w, conv_out_w, w_o, norm_ffn_g, w_gate_up, w_down, norm_final_g, loss_target, m_meta_tokens, m_norm_mix_g, m_w_in, m_b_gate, m_pool_w, m_pool_scale, m_conv_w, m_conv_out_w, m_w_o, m_norm_ffn_g, m_w_gate_up, m_w_down, m_norm_final_g, v_meta_tokens, v_norm_mix_g, v_w_in, v_b_gate, v_pool_w, v_pool_scale, v_conv_w, v_conv_out_w, v_w_o, v_norm_ffn_g, v_w_gate_up, v_w_down, v_norm_final_g):
    seq, dm = x.shape[1], x.shape[2]
    tail = LANES
    tm = tail
    lp = seq + tail
    n_chips = 4
    n_groups = len(POOL_WINDOWS)
    gw = dm // n_groups
    tc = min(256, gw)
    cx, cy, cc = _place()
    chip = 2 * cx + cy
    dloc = dm // n_chips

    pool2 = pool_w.reshape(n_groups * pool_w.shape[1], gw)
    big = {"w_in": w_in, "w_gate_up": w_gate_up, "pool_w": pool2, "conv_out_w": conv_out_w, "w_o": w_o, "w_down": w_down}
    chip1 = jnp.reshape(chip, (1,)).astype(jnp.int32)
    core = jnp.reshape(cc, (1,)).astype(jnp.int32)
    small_loc = jnp.concatenate([meta_tokens, jnp.pad(conv_w, ((0, 8 - conv_w.shape[0]), (0, 0))),
                                 jnp.zeros((8, dloc), F32)], axis=0)
    g1, g2, g3 = norm_mix_g.reshape(1, dm), norm_ffn_g.reshape(1, dm), norm_final_g.reshape(1, dm)
    b_gate2 = b_gate.reshape(2, dm)
    ps = pool_scale.reshape(1, dm)
    first = [_cast_into_slot("cast_w_in", w_in, chip1, BF16), _cast_into_slot("place_small", small_loc, chip1, F32)]
    sems, first, token = _ag_start("ag_start_first", first)
    cast = {nme: _cast_into_slot("cast_" + nme, big[nme], chip1, BF16, deps=(token,))
            for nme in ["pool_w", "conv_out_w", "w_o", "w_gate_up", "w_down"]}
    sems, first = _ag_relay("ag_relay_first", first, sems, list(cast.values()))
    sems, first = _ag_relay_wait("ag_relay_wait_first", first, sems, [])
    w_in4, small4 = _ag_final_wait("ag_final_wait_first", first, sems, [])
    mixer_w = [cast["pool_w"], cast["conv_out_w"], cast["w_o"]]
    sems_mix, mixer_w, token = _ag_start("ag_start_mixer", mixer_w, deps=(w_in4,))
    sems_gu, (w_gu4,), token = _ag_start("ag_start_gate_up", [cast["w_gate_up"]], deps=(token,))

    small_f = jnp.transpose(small4, (1, 0, 2)).reshape(small4.shape[1], dm)
    meta_f = small_f[:N_META]
    conv_w_f = small_f[N_META:N_META + 3]
    h0 = jnp.concatenate([x[0], jnp.zeros((tail - N_META, dm), F32), meta_f], axis=0)
    hn1 = _rms_fwd("rms_mix", h0, g1, tm, deps=(token,))
    proj = _nn_sharded("proj", hn1, w_in4, 6)
    sems_mix, mixer_w = _ag_relay("ag_relay_mixer", mixer_w, sems_mix, [proj])
    sems_gu, (w_gu4,) = _ag_relay("ag_relay_gate_up", [w_gu4], sems_gu, [mixer_w[0]])
    sems_down, (w_down4,), token = _ag_start("ag_start_down", [cast["w_down"]], deps=(w_gu4,))
    pooled, z = _mixer_fwd("mixer_fwd", proj, conv_w_f, tc, token)
    sems_mix, mixer_w = _ag_relay_wait("ag_relay_wait_mixer", mixer_w, sems_mix, [pooled])
    pool4, conv_out4, w_o4 = _ag_final_wait("ag_final_wait_mixer", mixer_w, sems_mix, [])
    pool_f = jnp.transpose(pool4.reshape(n_chips, n_groups, gw // n_chips, gw), (1, 0, 2, 3)).reshape(n_groups, gw, gw)
    conv_out_f = conv_out4.reshape(dm, dm)
    w_o_f = w_o4.reshape(dm, dm)
    ya = _pool_fwd("pool_proj", pooled, pool_f)
    yb = _nn_plain("conv_out", z, conv_out_f, BF16)
    mix = _gate_mix("gate_mix", proj, b_gate2, ya, ps, yb, tm)
    sems_gu, (w_gu4,) = _ag_relay_wait("ag_relay_wait_gate_up", [w_gu4], sems_gu, [mix])
    h1 = _nn_plain("attn_out", mix, w_o_f, F32, res=h0, tn_pref=256)
    (w_gu4,) = _ag_final_wait("ag_final_wait_gate_up", [w_gu4], sems_gu, [h1])
    hn2 = _rms_fwd("rms_ffn", h1, g2, tm)
    sems_down, (w_down4,) = _ag_relay("ag_relay_down", [w_down4], sems_down, [hn2])
    gu, act = _gate_up_swiglu("gate_up", hn2, w_gu4, w_down4)
    sems_down, (w_down4,) = _ag_relay_wait("ag_relay_wait_down", [w_down4], sems_down, [act])
    (w_down4,) = _ag_final_wait("ag_final_wait_down", [w_down4], sems_down, [])
    w_down_f = w_down4.reshape(-1, dm)
    h2 = _nn_rows("ffn_down", act, w_down_f, h1)
    dh2, dh2b, loss_cols, dg3 = _final_loss("final_loss", h2, g3, loss_target[0], tm)

    def scatter(tag, names_g, swap, after):
        grads_g, got = _swap_wait("swap_wait_" + tag, *swap, [after])
        pairs = [_pair_add("pair_add_" + nme, g4, rv, core) for nme, g4, rv in zip(names_g, grads_g, got)]
        return _scatter_start("scatter_start_" + tag, pairs)

    dgu = _dact_swiglu_bwd("d_gate_up", dh2b, w_down_f, gu)
    gw_down = _tn_plain("dw_down", act, dh2b)
    gw_gu = _tn_sharded("dw_gate_up", hn2, dgu, n_chips)
    swap_a, token = _swap_start("swap_start_a", [gw_gu, gw_down.reshape(n_chips, -1, dm)])
    dhn2 = _nt_sharded("d_hn2", dgu, w_gu4, tr_pref=2816, row_tiles=2, deps=(token,))
    flight_a, token = scatter("a", ["w_gate_up", "w_down"], swap_a, dhn2)
    dh1, dh1b, dg2 = _rms_bwd("rms_ffn_bwd", dhn2, h1, g2, dh2, tm, token)
    dmix = _nt_plain("d_mix", dh1b, w_o_f)
    gw_o = _tn_plain("dw_o", mix, dh1b)
    dproj, dyb, dya, db_gate, dps = _gate_bwd("gate_bwd", dmix, proj, b_gate2, ya, ps, yb, tm)
    gw_conv_out = _tn_plain("dw_conv_out", z, dyb)
    gw_pool = _pool_bwd_w("dw_pool", pooled, dya)
    gw_pool = jnp.transpose(gw_pool.reshape(n_groups, n_chips, gw // n_chips, gw), (1, 0, 2, 3))
    swap_b, token = _swap_start("swap_start_b", [gw_o.reshape(n_chips, dloc, dm), gw_conv_out.reshape(n_chips, dloc, dm),
                                                 gw_pool.reshape(n_chips, n_groups * (gw // n_chips), gw)])
    dpooled = _pool_bwd_act("d_pooled", dya, pool_f, deps=(token,))
    dz = _nt_plain("d_z", dyb, conv_out_f)
    flight_b, token = scatter("b", ["w_o", "conv_out_w", "pool_w"], swap_b, dz)
    dproj, dconv_w = _mixer_bwd("mixer_bwd", dz, dpooled, proj, conv_w_f, dproj, tc, token)
    gw_in0 = _tn_sharded("dw_in_0", hn1, dproj, n_chips, part=(0, 2))
    swap_c0, token = _swap_start("swap_start_c0", [gw_in0])
    gw_in1 = _tn_sharded("dw_in_1", hn1, dproj, n_chips, part=(1, 2), deps=(token,))
    flight_c0, token = scatter("c0", ["w_in_0"], swap_c0, gw_in1)
    swap_c1, token = _swap_start("swap_start_c1", [gw_in1], deps=(token,))
    dhn1 = _nt_sharded("d_hn1", dproj, w_in4, deps=(token,))
    flight_c1, token = scatter("c1", ["w_in_1"], swap_c1, dhn1)
    dx, dtail, dg1 = _rms_bwd_input("rms_mix_bwd", dhn1, h0, g1, dh1, tm, seq, token)
    grad_x = dx[None]
    dmeta = dtail[tail - N_META:]

    given = dict(meta_tokens=(meta_tokens, m_meta_tokens, v_meta_tokens), norm_mix_g=(norm_mix_g, m_norm_mix_g, v_norm_mix_g),
                 w_in=(w_in, m_w_in, v_w_in), b_gate=(b_gate, m_b_gate, v_b_gate), pool_w=(pool_w, m_pool_w, v_pool_w),
                 pool_scale=(pool_scale, m_pool_scale, v_pool_scale), conv_w=(conv_w, m_conv_w, v_conv_w),
                 conv_out_w=(conv_out_w, m_conv_out_w, v_conv_out_w), w_o=(w_o, m_w_o, v_w_o),
                 norm_ffn_g=(norm_ffn_g, m_norm_ffn_g, v_norm_ffn_g), w_gate_up=(w_gate_up, m_w_gate_up, v_w_gate_up),
                 w_down=(w_down, m_w_down, v_w_down), norm_final_g=(norm_final_g, m_norm_final_g, v_norm_final_g))
    order = list(given.keys())
    grad, delta, new_m, new_v = {}, {}, {}, {}
    vec = jnp.concatenate([dg1, dg2, dg3, db_gate, dps, loss_cols, dconv_w, dmeta], axis=0)
    loss_row = 5 * SMALL_ROWS
    groups_g = {"a": [("w_gate_up", (0, 1)), ("w_down", (0, 1))], "b": [("w_o", (0, 1)), ("conv_out_w", (0, 1)), ("pool_w", (0, 1))],
                "c0": [("w_in", (0, 2))], "c1": [("w_in", (1, 2))]}
    results = {}

    def reduced(tag, flight, after):
        pairs, zones = _scatter_wait("scatter_wait_" + tag, *flight, after)
        halves = [_chip_sum("chip_sum_%s_%d" % (nme, part[0]), p, rv, chip1) for (nme, part), p, rv in zip(groups_g[tag], pairs, zones)]
        return _swap_start("send_start_" + tag, halves, halves=False)

    def update(tag, send, after):
        halves, sib_halves = _swap_wait("send_wait_" + tag, *send, after, halves=False)
        deltas = []
        for (nme, part), g_own, g_sib in zip(groups_g[tag], halves, sib_halves):
            w, m, v = given[nme]
            shape2 = (2 * g_own.shape[0] * part[1], g_own.shape[1])
            results[nme] = _adamw_halves("adamw_%s_%d" % (nme, part[0]), w.reshape(shape2), g_own, g_sib, m.reshape(shape2),
                                         v.reshape(shape2), core, part=part, prev=results.get(nme))
            grad[nme], delta[nme], new_m[nme], new_v[nme] = [t.reshape(w.shape) for t in results[nme]]
            deltas.append(results[nme][1])
        return deltas

    send_a, token = reduced("a", flight_a, [dx])
    send_b, token = reduced("b", flight_b, [token])
    done_a = update("a", send_a, [token])
    send_c0, token = reduced("c0", flight_c0, done_a)
    done_b = update("b", send_b, [token])
    send_c1, token = reduced("c1", flight_c1, done_b)
    me1 = jnp.reshape(4 * cx + 2 * cy + cc, (1,)).astype(jnp.int32)
    red_flight, token = _reduce_start(vec, [token])
    done_c0 = update("c0", send_c0, [token])
    done_c1 = update("c1", send_c1, done_c0)
    red, loss11 = _reduce_sum(*_reduce_wait(*red_flight, done_c1), me1, loss_row, 0.5 / dm)
    loss = loss11[0, 0]
    col0 = chip * dloc
    g_small = {
        "norm_mix_g": red[0], "norm_ffn_g": red[SMALL_ROWS], "norm_final_g": red[2 * SMALL_ROWS],
        "b_gate": red[3 * SMALL_ROWS:3 * SMALL_ROWS + 2].reshape(-1), "pool_scale": red[4 * SMALL_ROWS],
        "conv_w": lax.dynamic_slice(red, (6 * SMALL_ROWS, col0), (3, dloc)),
        "meta_tokens": lax.dynamic_slice(red, (7 * SMALL_ROWS, col0), (N_META, dloc)),
    }

    vec_names = ["norm_mix_g", "norm_ffn_g", "norm_final_g", "pool_scale"]

    def slab_vec(pick):
        rows = [pick(nme).reshape(1, dm) for nme in vec_names] + [pick("b_gate").reshape(2, dm), jnp.zeros((2, dm), F32)]
        return jnp.concatenate(rows, axis=0)

    def slab_col(pick):
        return jnp.concatenate([pick("meta_tokens"), pick("conv_w"), jnp.zeros((5, dloc), F32)], axis=0)

    for slab, tag in ((slab_vec, "vec"), (slab_col, "col")):
        d, nm, nv = _adamw("adamw_small_" + tag, slab(lambda nme: given[nme][0]), slab(lambda nme: g_small[nme]),
                           slab(lambda nme: given[nme][1]), slab(lambda nme: given[nme][2]))
        for out, res in ((delta, d), (new_m, nm), (new_v, nv)):
            if tag == "vec":
                for i, nme in enumerate(vec_names):
                    out[nme] = res[i]
                out["b_gate"] = res[4:6].reshape(-1)
            else:
                out["meta_tokens"] = res[:N_META]
                out["conv_w"] = res[N_META:N_META + 3]
    grad.update(g_small)
    return (loss, grad_x, *[grad[nme] for nme in order], *[delta[nme] for nme in order],
            *[new_m[nme] for nme in order], *[new_v[nme] for nme in order])
```

```python
import functools
import math

import jax
import jax.numpy as jnp
from jax import lax
from jax.experimental import pallas as pl
from jax.experimental.pallas import tpu as pltpu

F32 = jnp.float32
BF16 = jnp.bfloat16
N_META = 16
POOL_WINDOWS = (2, 4, 8, 16)
EPS = 1e-6
ADAM_LR, ADAM_B1, ADAM_B2, ADAM_EPS, ADAM_WD, ADAM_STEP = 0.001, 0.9, 0.999, 1e-08, 0.01, 10
LANES = 128
V7X_VMEM_BYTES = 64 * 1024 * 1024
VMEM_LIMIT = V7X_VMEM_BYTES - 8 * 1024 * 1024
MESH = pl.DeviceIdType.MESH
ANY = pl.BlockSpec(memory_space=pl.ANY)
CHIP_FLIPS = ((1, 0), (0, 1), (1, 1))
SMALL_ROWS = 8


def _pick(n, pref):
    best = None
    for t in range(LANES, min(n, pref) + 1, LANES):
        if n % t == 0:
            best = t
    assert best is not None, (n, pref)
    return best


def _params(n_axes=0):
    sem = ("arbitrary",) * n_axes if n_axes else None
    return pltpu.CompilerParams(dimension_semantics=sem, vmem_limit_bytes=VMEM_LIMIT)


_DIMS = {
    "nn": (((1,), (0,)), ((), ())),
    "nt": (((1,), (1,)), ((), ())),
    "tn": (((0,), (0,)), ((), ())),
}


def _matmul(name, mode, a, b, out_sds, grid, a_spec, b_spec, o_spec, nk, res=None, res_spec=None, acc_shape=None, deps=()):
    out_dtype = out_sds.dtype
    in_place = nk > 1 and out_dtype == F32
    use_scratch = nk > 1 and not in_place
    rows = a_spec.block_shape[-2] if mode != "tn" else None
    chunk = _row_tile(rows, 1, 1, 1152) if rows is not None else None
    n_in = 2 + (res is not None) + len(deps)

    def body(*refs):
        a_ref, b_ref = refs[:2]
        r_ref = refs[2] if res is not None else None
        o_ref, *scr = refs[n_in:]
        k = pl.program_id(len(grid) - 1) if nk > 1 else None

        def emit(sl):
            if sl is None:
                part = lax.dot_general(a_ref[...], b_ref[...], _DIMS[mode], preferred_element_type=F32)
                idx = (slice(None), slice(None))
            else:
                part = lax.dot_general(a_ref[sl, :], b_ref[...], _DIMS[mode], preferred_element_type=F32)
                idx = (sl, slice(None))
            if nk == 1:
                if r_ref is not None:
                    part = part + r_ref[idx]
                o_ref[idx] = part.astype(out_dtype)
                return
            acc = scr[0] if use_scratch else o_ref

            @pl.when(k == 0)
            def _():
                first = part
                if r_ref is not None and in_place:
                    first = first + r_ref[idx]
                acc[idx] = first

            @pl.when(k > 0)
            def _():
                acc[idx] += part

            if use_scratch:

                @pl.when(k == nk - 1)
                def _():
                    o_ref[idx] = acc[idx].astype(out_dtype)

        if mode == "tn" or chunk == rows:
            emit(None)
        else:
            for m0 in range(0, rows, chunk):
                emit(pl.ds(m0, chunk))

    ins = [a, b] + ([res] if res is not None else []) + list(deps)
    in_specs = [a_spec, b_spec] + ([res_spec] if res is not None else []) + [ANY] * len(deps)
    scratch = [pltpu.VMEM(acc_shape, F32)] if use_scratch else []
    return pl.pallas_call(
        body, name=name, out_shape=out_sds, grid=grid, in_specs=in_specs, out_specs=o_spec,
        scratch_shapes=scratch, compiler_params=_params(len(grid)),
    )(*ins)


def _nn_sharded(name, a, w4, nseg):
    lp, kdim = a.shape
    s, _, nloc = w4.shape
    segw = s * nloc // nseg
    tn = _pick(math.gcd(nloc, segw), 1536)
    bw, bo = nloc // tn, segw // tn
    return _matmul(
        name, "nn", a, w4, jax.ShapeDtypeStruct((nseg, lp, segw), BF16), (s * bw,),
        pl.BlockSpec((lp, kdim), lambda j: (0, 0)),
        pl.BlockSpec((None, kdim, tn), lambda j: (j // bw, 0, j % bw)),
        pl.BlockSpec((None, lp, tn), lambda j: (j // bo, 0, j % bo)), 1)


def _nt_in_proj(name, dseg, w4, row_tiles=2, to_pref=1024, deps=()):
    nseg, lp, segw = dseg.shape
    s, kdim, nloc = w4.shape
    assert nseg * segw == s * nloc and 2 * nloc == 3 * segw, (dseg.shape, w4.shape)
    half = segw // 2
    to = _pick(kdim, to_pref)
    tm = lp // row_tiles

    def body(full_ref, half_ref, w_ref, *rest):
        o_ref = rest[len(deps)]
        r = pl.program_id(2)

        def contribution(full_first):
            lo, hi = (pl.ds(0, segw), pl.ds(segw, half)) if full_first else (pl.ds(half, segw), pl.ds(0, half))
            return (lax.dot_general(full_ref[...], w_ref[:, lo], _DIMS["nt"], preferred_element_type=F32)
                    + lax.dot_general(half_ref[...], w_ref[:, hi], _DIMS["nt"], preferred_element_type=F32))

        @pl.when(r == 0)
        def _():
            o_ref[...] = contribution(True)

        for ri in range(1, s):

            @pl.when(r == ri)
            def _(ri=ri):
                o_ref[...] += contribution(ri % 2 == 0)

    return pl.pallas_call(
        body, name=name, out_shape=jax.ShapeDtypeStruct((lp, kdim), F32), grid=(row_tiles, kdim // to, s),
        in_specs=[pl.BlockSpec((None, tm, segw), lambda m, j, r: ((3 * r + 1) // 2, m, 0)),
                  pl.BlockSpec((None, tm, half), lambda m, j, r: (1 + 3 * (r // 2), m, r % 2)),
                  pl.BlockSpec((None, to, nloc), lambda m, j, r: (r, j, 0))] + [ANY] * len(deps),
        out_specs=pl.BlockSpec((tm, to), lambda m, j, r: (m, j)), compiler_params=_params(3),
    )(dseg, dseg, w4, *deps)


def _nn_sharded_part(name, a, wq, part, nseg, prev):
    lp, kdim = a.shape
    s, _, tn = wq.shape
    q, nq = part
    segw = s * nq * tn // nseg
    bo = segw // tn

    chunk = _row_tile(lp, 1, 1, 1152)

    def body(a_ref, w_ref, *rest):
        for m0 in range(0, lp, chunk):
            sl = pl.ds(m0, chunk)
            rest[-1][sl, :] = jnp.dot(a_ref[sl, :], w_ref[...], preferred_element_type=F32).astype(BF16)

    def out_map(j):
        b = j * nq + q
        return b // bo, 0, b % bo

    n_prev = 0 if prev is None else 1
    return pl.pallas_call(
        body, name=name, out_shape=jax.ShapeDtypeStruct((nseg, lp, segw), BF16), grid=(s,),
        in_specs=[pl.BlockSpec((lp, kdim), lambda j: (0, 0)), pl.BlockSpec((None, kdim, tn), lambda j: (j, 0, 0))] + [ANY] * n_prev,
        out_specs=pl.BlockSpec((None, lp, tn), out_map), input_output_aliases={2: 0} if prev is not None else {},
        compiler_params=_params(1),
    )(a, wq, *([prev] if prev is not None else []))


def _nt_sharded_parts(name, dseg, wqs, to_pref=1024, deps=()):
    nseg, lp, segw = dseg.shape
    nq = len(wqs)
    s, kdim, tr = wqs[0].shape
    ba = segw // tr
    nr = s * nq
    to = _pick(kdim, to_pref)
    chunk = _row_tile(lp, 1, 1, 1152)

    def body(d_ref, *rest):
        w_refs = rest[:nq]
        o_ref = rest[nq + len(deps)]
        r = pl.program_id(1)
        for qi in range(nq):

            @pl.when(r % nq == qi)
            def _(qi=qi):
                for m0 in range(0, lp, chunk):
                    sl = pl.ds(m0, chunk)
                    part = lax.dot_general(d_ref[sl, :], w_refs[qi][...], _DIMS["nt"], preferred_element_type=F32)
                    if qi == 0:

                        @pl.when(r == 0)
                        def _():
                            o_ref[sl, :] = part

                        @pl.when(r > 0)
                        def _():
                            o_ref[sl, :] += part

                    else:
                        o_ref[sl, :] += part

    return pl.pallas_call(
        body, name=name, out_shape=jax.ShapeDtypeStruct((lp, kdim), F32), grid=(kdim // to, nr),
        in_specs=[pl.BlockSpec((None, lp, tr), lambda j, r: (r // ba, 0, r % ba))]
        + [pl.BlockSpec((None, to, tr), lambda j, r: (r // nq, j, 0)) for _ in range(nq)] + [ANY] * len(deps),
        out_specs=pl.BlockSpec((lp, to), lambda j, r: (0, j)), compiler_params=_params(2),
    )(dseg, *wqs, *deps)


def _nn_plain(name, a, w, out_dtype, res=None, tn_pref=512, tk_pref=2048):
    lp, kdim = a.shape
    n = w.shape[1]
    tn = _pick(n, tn_pref)
    tk = kdim if kdim <= tk_pref else _pick(kdim, tk_pref)
    nk = kdim // tk
    grid = (n // tn, nk) if nk > 1 else (n // tn,)
    if nk > 1:
        a_spec = pl.BlockSpec((lp, tk), lambda j, k: (0, k))
        w_spec = pl.BlockSpec((tk, tn), lambda j, k: (k, j))
        o_spec = pl.BlockSpec((lp, tn), lambda j, k: (0, j))
    else:
        a_spec = pl.BlockSpec((lp, tk), lambda j: (0, 0))
        w_spec = pl.BlockSpec((tk, tn), lambda j: (0, j))
        o_spec = pl.BlockSpec((lp, tn), lambda j: (0, j))
    return _matmul(name, "nn", a, w, jax.ShapeDtypeStruct((lp, n), out_dtype), grid, a_spec, w_spec, o_spec, nk,
                   res=res, res_spec=o_spec if res is not None else None, acc_shape=(lp, tn))


def _nt_plain(name, a, w, tn_pref=512):
    lp, kdim = a.shape
    n = w.shape[0]
    tn = _pick(n, tn_pref)
    return _matmul(
        name, "nt", a, w, jax.ShapeDtypeStruct((lp, n), BF16), (n // tn,),
        pl.BlockSpec((lp, kdim), lambda j: (0, 0)),
        pl.BlockSpec((tn, kdim), lambda j: (j, 0)),
        pl.BlockSpec((lp, tn), lambda j: (0, j)), 1)


def _nt_sharded(name, dseg, w4, to_pref=1024, tr_pref=1536, row_tiles=1, deps=()):
    nseg, lp, segw = dseg.shape
    s, kdim, nloc = w4.shape
    tr = _pick(math.gcd(nloc, segw), tr_pref)
    ba, bw = segw // tr, nloc // tr
    nr = s * bw
    to = _pick(kdim, to_pref)
    tm = lp // row_tiles
    return _matmul(
        name, "nt", dseg, w4, jax.ShapeDtypeStruct((lp, kdim), F32), (row_tiles, kdim // to, nr),
        pl.BlockSpec((None, tm, tr), lambda m, j, r: (r // ba, m, r % ba)),
        pl.BlockSpec((None, to, tr), lambda m, j, r: (r // bw, j, r % bw)),
        pl.BlockSpec((tm, to), lambda m, j, r: (m, j)), nr, deps=deps)


def _nn_rows(name, a, w, res, row_tiles=2, tn_pref=512):
    lp, kdim = a.shape
    n = w.shape[1]
    tn = _pick(n, tn_pref)
    tm = lp // row_tiles
    blk = pl.BlockSpec((tm, tn), lambda i, j: (i, j))
    return _matmul(name, "nn", a, w, jax.ShapeDtypeStruct((lp, n), F32), (row_tiles, n // tn),
                   pl.BlockSpec((tm, kdim), lambda i, j: (i, 0)), pl.BlockSpec((kdim, tn), lambda i, j: (0, j)), blk, 1,
                   res=res, res_spec=blk)


def _tn_plain(name, a, d, tk_pref=1024):
    lp, kdim = a.shape
    n = d.shape[1]
    tk = _pick(kdim, tk_pref)
    return _matmul(
        name, "tn", a, d, jax.ShapeDtypeStruct((kdim, n), BF16), (kdim // tk,),
        pl.BlockSpec((lp, tk), lambda i: (0, i)),
        pl.BlockSpec((lp, n), lambda i: (0, 0)),
        pl.BlockSpec((tk, n), lambda i: (i, 0)), 1)


def _tn_sharded(name, a, dseg, s, part=(0, 1), tk_pref=1024, deps=()):
    lp, kdim = a.shape
    nseg, _, segw = dseg.shape
    nloc = nseg * segw // s
    tn = _pick(math.gcd(nloc, segw), 1536)
    bd, bo = segw // tn, nloc // tn
    kpart = kdim // part[1]
    tk = _pick(kpart, tk_pref)
    i0 = part[0] * (kpart // tk)

    def body(a_ref, d_ref, *rest):
        o_ref, at_ref = rest[len(deps):]

        @pl.when(pl.program_id(1) == 0)
        def _():
            at_ref[...] = a_ref[...].T

        o_ref[...] = jnp.dot(at_ref[...], d_ref[...], preferred_element_type=F32).astype(BF16)

    return pl.pallas_call(
        body, name=name, out_shape=jax.ShapeDtypeStruct((s, kpart, nloc), BF16), grid=(kpart // tk, s * bo),
        in_specs=[pl.BlockSpec((lp, tk), lambda i, j: (0, i0 + i)),
                  pl.BlockSpec((None, lp, tn), lambda i, j: (j // bd, 0, j % bd))] + [ANY] * len(deps),
        out_specs=pl.BlockSpec((None, tk, tn), lambda i, j: (j // bo, i, j % bo)),
        scratch_shapes=[pltpu.VMEM((tk, lp), BF16)], compiler_params=_params(2),
    )(a, dseg, *deps)


def _silu_parts(gt):
    sg = jax.nn.sigmoid(gt)
    return gt * sg, sg * (1.0 + gt * (1.0 - sg))


def _gate_up_swiglu(name, a, w4, dep, tn_pref=256):
    lp, kdim = a.shape
    s, _, nloc = w4.shape
    f = s * nloc // 2
    tn = _pick(nloc, tn_pref)
    bw = nloc // tn
    chunk = _row_tile(lp, 1, 1, 576)

    def body(a_ref, wg_ref, wu_ref, _, gu_ref, act_ref):
        for m0 in range(0, lp, chunk):
            sl = pl.ds(m0, chunk)
            gt = jnp.dot(a_ref[sl, :], wg_ref[...], preferred_element_type=F32)
            up = jnp.dot(a_ref[sl, :], wu_ref[...], preferred_element_type=F32)
            gu_ref[0, sl, :] = gt.astype(BF16)
            gu_ref[1, sl, :] = up.astype(BF16)
            act_ref[sl, :] = (_silu_parts(gt)[0] * up).astype(BF16)

    return pl.pallas_call(
        body, name=name, grid=(f // tn,),
        out_shape=(jax.ShapeDtypeStruct((2, lp, f), BF16), jax.ShapeDtypeStruct((lp, f), BF16)),
        in_specs=[pl.BlockSpec((lp, kdim), lambda j: (0, 0)),
                  pl.BlockSpec((None, kdim, tn), lambda j: (j // bw, 0, j % bw)),
                  pl.BlockSpec((None, kdim, tn), lambda j: (s // 2 + j // bw, 0, j % bw)), ANY],
        out_specs=(pl.BlockSpec((2, lp, tn), lambda j: (0, 0, j)), pl.BlockSpec((lp, tn), lambda j: (0, j))),
        compiler_params=_params(1),
    )(a, w4, w4, dep)


def _dact_swiglu_bwd(name, d, w, gu, tn_pref=512):
    lp, dm = d.shape
    f = w.shape[0]
    tn = _pick(f, tn_pref)
    chunk = _row_tile(lp, 1, 1, 576)

    def body(d_ref, w_ref, g_ref, u_ref, o_ref):
        for m0 in range(0, lp, chunk):
            sl = pl.ds(m0, chunk)
            dact = lax.dot_general(d_ref[sl, :], w_ref[...], _DIMS["nt"], preferred_element_type=F32)
            silu, dsilu = _silu_parts(g_ref[sl, :].astype(F32))
            o_ref[0, sl, :] = (dact * u_ref[sl, :].astype(F32) * dsilu).astype(BF16)
            o_ref[1, sl, :] = (dact * silu).astype(BF16)

    return pl.pallas_call(
        body, name=name, grid=(f // tn,), out_shape=jax.ShapeDtypeStruct((2, lp, f), BF16),
        in_specs=[pl.BlockSpec((lp, dm), lambda j: (0, 0)), pl.BlockSpec((tn, dm), lambda j: (j, 0)),
                  pl.BlockSpec((None, lp, tn), lambda j: (0, 0, j)), pl.BlockSpec((None, lp, tn), lambda j: (1, 0, j))],
        out_specs=pl.BlockSpec((2, lp, tn), lambda j: (0, 0, j)), compiler_params=_params(1),
    )(d, w, gu, gu)


def _pool_fwd(name, pooled, pw):
    lp, dm = pooled.shape
    g, gw, _ = pw.shape
    return _matmul(
        name, "nn", pooled, pw, jax.ShapeDtypeStruct((lp, dm), BF16), (g,),
        pl.BlockSpec((lp, gw), lambda gi: (0, gi)), pl.BlockSpec((None, gw, gw), lambda gi: (gi, 0, 0)),
        pl.BlockSpec((lp, gw), lambda gi: (0, gi)), 1)


def _pool_bwd_act(name, dya, pw, deps=()):
    lp, dm = dya.shape
    g, gw, _ = pw.shape
    return _matmul(
        name, "nt", dya, pw, jax.ShapeDtypeStruct((lp, dm), BF16), (g,),
        pl.BlockSpec((lp, gw), lambda gi: (0, gi)), pl.BlockSpec((None, gw, gw), lambda gi: (gi, 0, 0)),
        pl.BlockSpec((lp, gw), lambda gi: (0, gi)), 1, deps=deps)


def _pool_bwd_w(name, pooled, dya):
    lp, dm = pooled.shape
    g = len(POOL_WINDOWS)
    gw = dm // g
    return _matmul(
        name, "tn", pooled, dya, jax.ShapeDtypeStruct((g, gw, gw), BF16), (g,),
        pl.BlockSpec((lp, gw), lambda gi: (0, gi)), pl.BlockSpec((lp, gw), lambda gi: (0, gi)),
        pl.BlockSpec((None, gw, gw), lambda gi: (gi, 0, 0)), 1)


def _rms_fwd(name, h, g, tm, deps=()):
    lp, dm = h.shape

    def body(h_ref, g_ref, *rest):
        hv = h_ref[...]
        r = lax.rsqrt(jnp.mean(hv * hv, axis=-1, keepdims=True) + EPS)
        rest[-1][...] = (hv * r * g_ref[...]).astype(BF16)

    row = pl.BlockSpec((tm, dm), lambda i: (i, 0))
    return pl.pallas_call(
        body, name=name, out_shape=jax.ShapeDtypeStruct((lp, dm), BF16), grid=(lp // tm,),
        in_specs=[row, pl.BlockSpec((1, dm), lambda i: (0, 0))] + [ANY] * len(deps), out_specs=row, compiler_params=_params(1),
    )(h, g, *deps)


def _rms_bwd(name, dy, h, g, dres, tm, dep):
    lp, dm = h.shape

    def body(dy_ref, h_ref, g_ref, dr_ref, _, dh_ref, dhb_ref, dg_ref):
        hv = h_ref[...]
        r = lax.rsqrt(jnp.mean(hv * hv, axis=-1, keepdims=True) + EPS)
        xhat = hv * r
        dyv = dy_ref[...]
        dxh = dyv * g_ref[...]
        dh = dr_ref[...] + r * (dxh - xhat * jnp.mean(dxh * xhat, axis=-1, keepdims=True))
        dh_ref[...] = dh
        dhb_ref[...] = dh.astype(BF16)

        @pl.when(pl.program_id(0) == 0)
        def _():
            dg_ref[...] = jnp.zeros_like(dg_ref)

        dg_ref[0:1, :] += jnp.sum(dyv * xhat, axis=0, keepdims=True)

    row = pl.BlockSpec((tm, dm), lambda i: (i, 0))
    slab = pl.BlockSpec((SMALL_ROWS, dm), lambda i: (0, 0))
    return pl.pallas_call(
        body, name=name, grid=(lp // tm,),
        out_shape=(jax.ShapeDtypeStruct((lp, dm), F32), jax.ShapeDtypeStruct((lp, dm), BF16),
                   jax.ShapeDtypeStruct((SMALL_ROWS, dm), F32)),
        in_specs=[row, row, pl.BlockSpec((1, dm), lambda i: (0, 0)), row, ANY], out_specs=(row, row, slab),
        compiler_params=_params(1),
    )(dy, h, g, dres, dep)


def _rms_bwd_input(name, dy, h, g, dres, tm, seq, dep):
    lp, dm = h.shape
    nx = seq // tm

    def body(dy_ref, h_ref, g_ref, dr_ref, _, dx_ref, dt_ref, dg_ref):
        i = pl.program_id(0)
        hv = h_ref[...]
        r = lax.rsqrt(jnp.mean(hv * hv, axis=-1, keepdims=True) + EPS)
        xhat = hv * r
        dyv = dy_ref[...]
        dxh = dyv * g_ref[...]
        dh = dr_ref[...] + r * (dxh - xhat * jnp.mean(dxh * xhat, axis=-1, keepdims=True))

        @pl.when(i < nx)
        def _():
            dx_ref[...] = dh

        @pl.when(i >= nx)
        def _():
            dt_ref[...] = dh

        @pl.when(i == 0)
        def _():
            dg_ref[...] = jnp.zeros_like(dg_ref)

        dg_ref[0:1, :] += jnp.sum(dyv * xhat, axis=0, keepdims=True)

    row = pl.BlockSpec((tm, dm), lambda i: (i, 0))
    slab = pl.BlockSpec((SMALL_ROWS, dm), lambda i: (0, 0))
    return pl.pallas_call(
        body, name=name, grid=(lp // tm,),
        out_shape=(jax.ShapeDtypeStruct((seq, dm), F32), jax.ShapeDtypeStruct((tm, dm), F32),
                   jax.ShapeDtypeStruct((SMALL_ROWS, dm), F32)),
        in_specs=[row, row, pl.BlockSpec((1, dm), lambda i: (0, 0)), row, ANY],
        out_specs=(pl.BlockSpec((tm, dm), lambda i: (jnp.minimum(i, nx - 1), 0)), pl.BlockSpec((tm, dm), lambda i: (0, 0)), slab),
        compiler_params=_params(1),
    )(dy, h, g, dres, dep)


def _gate_mix(name, proj, b_gate2, ya, pool_scale, yb, tm):
    _, lp, dm = proj.shape

    def body(ga_ref, gr_ref, b_ref, ya_ref, ps_ref, yb_ref, o_ref):
        g_a = jax.nn.sigmoid(ga_ref[...].astype(F32) + b_ref[0:1, :])
        g_b = jax.nn.sigmoid(gr_ref[...].astype(F32) + b_ref[1:2, :])
        y_a = ya_ref[...].astype(F32) * ps_ref[...]
        o_ref[...] = (g_a * y_a + g_b * yb_ref[...].astype(F32)).astype(BF16)

    row = pl.BlockSpec((tm, dm), lambda i: (i, 0))
    return pl.pallas_call(
        body, name=name, out_shape=jax.ShapeDtypeStruct((lp, dm), BF16), grid=(lp // tm,),
        in_specs=[pl.BlockSpec((None, tm, dm), lambda i: (4, i, 0)), pl.BlockSpec((None, tm, dm), lambda i: (5, i, 0)),
                  pl.BlockSpec((2, dm), lambda i: (0, 0)), row, pl.BlockSpec((1, dm), lambda i: (0, 0)), row],
        out_specs=row, compiler_params=_params(1),
    )(proj, proj, b_gate2, ya, pool_scale, yb)


def _gate_bwd(name, dmix, proj, b_gate2, ya, pool_scale, yb, tm):
    _, lp, dm = proj.shape

    def body(dm_ref, ga_ref, gr_ref, b_ref, ya_ref, ps_ref, yb_ref, dp_ref, dyb_ref, dya_ref, db_ref, dps_ref):
        dmx = dm_ref[...].astype(F32)
        g_a = jax.nn.sigmoid(ga_ref[...].astype(F32) + b_ref[0:1, :])
        g_b = jax.nn.sigmoid(gr_ref[...].astype(F32) + b_ref[1:2, :])
        ya_pre = ya_ref[...].astype(F32)
        ybv = yb_ref[...].astype(F32)
        ps = ps_ref[...]
        dga = dmx * (ya_pre * ps) * (g_a * (1.0 - g_a))
        dgr = dmx * ybv * (g_b * (1.0 - g_b))
        dp_ref[0] = dga.astype(BF16)
        dp_ref[1] = dgr.astype(BF16)
        dyb_ref[...] = (dmx * g_b).astype(BF16)
        dya_ref[...] = (dmx * g_a * ps).astype(BF16)

        @pl.when(pl.program_id(0) == 0)
        def _():
            db_ref[...] = jnp.zeros_like(db_ref)
            dps_ref[...] = jnp.zeros_like(dps_ref)

        db_ref[0:1, :] += jnp.sum(dga, axis=0, keepdims=True)
        db_ref[1:2, :] += jnp.sum(dgr, axis=0, keepdims=True)
        dps_ref[0:1, :] += jnp.sum(dmx * g_a * ya_pre, axis=0, keepdims=True)

    row = pl.BlockSpec((tm, dm), lambda i: (i, 0))
    one = pl.BlockSpec((1, dm), lambda i: (0, 0))
    slab = pl.BlockSpec((SMALL_ROWS, dm), lambda i: (0, 0))
    return pl.pallas_call(
        body, name=name, grid=(lp // tm,),
        out_shape=(jax.ShapeDtypeStruct((6, lp, dm), BF16), jax.ShapeDtypeStruct((lp, dm), BF16),
                   jax.ShapeDtypeStruct((lp, dm), BF16), jax.ShapeDtypeStruct((SMALL_ROWS, dm), F32),
                   jax.ShapeDtypeStruct((SMALL_ROWS, dm), F32)),
        in_specs=[row, pl.BlockSpec((None, tm, dm), lambda i: (4, i, 0)), pl.BlockSpec((None, tm, dm), lambda i: (5, i, 0)),
                  pl.BlockSpec((2, dm), lambda i: (0, 0)), row, one, row],
        out_specs=(pl.BlockSpec((2, tm, dm), lambda i: (2, i, 0)), row, row, slab, slab),
        compiler_params=_params(1),
    )(dmix, proj, proj, b_gate2, ya, pool_scale, yb)


def _swiglu_fwd(name, gu, tm):
    _, lp, f = gu.shape

    def body(g_ref, u_ref, o_ref):
        gt = g_ref[...].astype(F32)
        o_ref[...] = (gt * jax.nn.sigmoid(gt) * u_ref[...].astype(F32)).astype(BF16)

    return pl.pallas_call(
        body, name=name, out_shape=jax.ShapeDtypeStruct((lp, f), BF16), grid=(lp // tm,),
        in_specs=[pl.BlockSpec((None, tm, f), lambda i: (0, i, 0)), pl.BlockSpec((None, tm, f), lambda i: (1, i, 0))],
        out_specs=pl.BlockSpec((tm, f), lambda i: (i, 0)), compiler_params=_params(1),
    )(gu, gu)


def _swiglu_bwd(name, dact, gu, tm):
    _, lp, f = gu.shape

    def body(d_ref, g_ref, u_ref, o_ref):
        d = d_ref[...].astype(F32)
        gt = g_ref[...].astype(F32)
        sg = jax.nn.sigmoid(gt)
        o_ref[0] = (d * u_ref[...].astype(F32) * (sg * (1.0 + gt * (1.0 - sg)))).astype(BF16)
        o_ref[1] = (d * (gt * sg)).astype(BF16)

    return pl.pallas_call(
        body, name=name, out_shape=jax.ShapeDtypeStruct((2, lp, f), BF16), grid=(lp // tm,),
        in_specs=[pl.BlockSpec((tm, f), lambda i: (i, 0)), pl.BlockSpec((None, tm, f), lambda i: (0, i, 0)),
                  pl.BlockSpec((None, tm, f), lambda i: (1, i, 0))],
        out_specs=pl.BlockSpec((2, tm, f), lambda i: (0, i, 0)), compiler_params=_params(1),
    )(dact, gu, gu)


def _final_loss(name, h2, g3, target, tm):
    lp, dm = h2.shape
    nx = target.shape[0] // tm

    def body(h_ref, g_ref, t_ref, dh_ref, dhb_ref, ls_ref, dg_ref):
        i = pl.program_id(0)

        @pl.when(i == 0)
        def _():
            ls_ref[...] = jnp.zeros_like(ls_ref)
            dg_ref[...] = jnp.zeros_like(dg_ref)

        @pl.when(i < nx)
        def _():
            hv = h_ref[...]
            gv = g_ref[...]
            r = lax.rsqrt(jnp.mean(hv * hv, axis=-1, keepdims=True) + EPS)
            xhat = hv * r
            err = xhat * gv - t_ref[...]
            dout = err * (1.0 / dm)
            dxh = dout * gv
            dh = r * (dxh - xhat * jnp.mean(dxh * xhat, axis=-1, keepdims=True))
            dh_ref[...] = dh
            dhb_ref[...] = dh.astype(BF16)
            ls_ref[0:1, :] += jnp.sum(err * err, axis=0, keepdims=True)
            dg_ref[0:1, :] += jnp.sum(dout * xhat, axis=0, keepdims=True)

        @pl.when(i >= nx)
        def _():
            dh_ref[...] = jnp.zeros_like(dh_ref)
            dhb_ref[...] = jnp.zeros_like(dhb_ref)

    row = pl.BlockSpec((tm, dm), lambda i: (i, 0))
    slab = pl.BlockSpec((SMALL_ROWS, dm), lambda i: (0, 0))
    return pl.pallas_call(
        body, name=name, grid=(lp // tm,),
        out_shape=(jax.ShapeDtypeStruct((lp, dm), F32), jax.ShapeDtypeStruct((lp, dm), BF16),
                   jax.ShapeDtypeStruct((SMALL_ROWS, dm), F32), jax.ShapeDtypeStruct((SMALL_ROWS, dm), F32)),
        in_specs=[row, pl.BlockSpec((1, dm), lambda i: (0, 0)), pl.BlockSpec((tm, dm), lambda i: (jnp.minimum(i, nx - 1), 0))],
        out_specs=(row, row, slab, slab), compiler_params=_params(1),
    )(h2, g3, target)


def _shift(v, k):
    return pltpu.roll(v, k % v.shape[0], axis=0)


def _window_sum(v, group, sign):
    s2 = v + _shift(v, sign * 1)
    s4 = s2 + _shift(s2, sign * 2)
    s8 = s4 + _shift(s4, sign * 4)
    s16 = s8 + _shift(s8, sign * 8)
    return jnp.where(group == 0, s2, jnp.where(group == 1, s4, jnp.where(group == 2, s8, s16)))


def _pool_count(lp, group):
    row = lax.broadcasted_iota(jnp.int32, (lp, 1), 0)
    window = jnp.left_shift(2, group).astype(F32)
    meta_pos = (row - (lp - N_META) + 1).astype(F32)
    return jnp.where(row >= lp - N_META, jnp.minimum(meta_pos, window), window)


def _mixer_fwd(name, proj, conv_w, tc, dep):
    _, lp, dm = proj.shape
    per_group = dm // len(POOL_WINDOWS) // tc

    def body(u_ref, gb_ref, gc_ref, v_ref, cw_ref, _, p_ref, z_ref):
        group = pl.program_id(0) // per_group
        u = u_ref[...].astype(F32)
        p_ref[...] = (_window_sum(u, group, 1) / _pool_count(lp, group) - u).astype(BF16)
        cv = gc_ref[...].astype(F32) * v_ref[...].astype(F32)
        conv = cw_ref[0:1, :] * _shift(cv, 2) + cw_ref[1:2, :] * _shift(cv, 1) + cw_ref[2:3, :] * cv
        z_ref[...] = (gb_ref[...].astype(F32) * conv).astype(BF16)

    def seg(s):
        return pl.BlockSpec((None, lp, tc), lambda j: (s, 0, j))

    col = pl.BlockSpec((lp, tc), lambda j: (0, j))
    return pl.pallas_call(
        body, name=name, grid=(dm // tc,),
        out_shape=(jax.ShapeDtypeStruct((lp, dm), BF16), jax.ShapeDtypeStruct((lp, dm), BF16)),
        in_specs=[seg(0), seg(1), seg(2), seg(3), pl.BlockSpec((3, tc), lambda j: (0, j)), ANY],
        out_specs=(col, col), compiler_params=_params(1),
    )(proj, proj, proj, proj, conv_w, dep)


def _mixer_bwd(name, dz, dpooled, proj, conv_w, dproj, tc, dep):
    _, lp, dm = proj.shape
    per_group = dm // len(POOL_WINDOWS) // tc

    def body(dz_ref, dp_ref, gb_ref, gc_ref, v_ref, cw_ref, _, __, o_ref, dcw_ref):
        group = pl.program_id(0) // per_group
        dzv = dz_ref[...].astype(F32)
        gb = gb_ref[...].astype(F32)
        gc = gc_ref[...].astype(F32)
        vv = v_ref[...].astype(F32)
        cv = gc * vv
        c1 = _shift(cv, 1)
        c2 = _shift(cv, 2)
        w0, w1, w2 = cw_ref[0:1, :], cw_ref[1:2, :], cw_ref[2:3, :]
        o_ref[1] = (dzv * (w0 * c2 + w1 * c1 + w2 * cv)).astype(BF16)
        dconv = dzv * gb
        dcw_ref[...] = jnp.zeros_like(dcw_ref)
        dcw_ref[0:1, :] = jnp.sum(dconv * c2, axis=0, keepdims=True)
        dcw_ref[1:2, :] = jnp.sum(dconv * c1, axis=0, keepdims=True)
        dcw_ref[2:3, :] = jnp.sum(dconv * cv, axis=0, keepdims=True)
        dcv = w0 * _shift(dconv, -2) + w1 * _shift(dconv, -1) + w2 * dconv
        o_ref[2] = (dcv * vv).astype(BF16)
        o_ref[3] = (dcv * gc).astype(BF16)
        dpv = dp_ref[...].astype(F32)
        o_ref[0] = (_window_sum(dpv / _pool_count(lp, group), group, -1) - dpv).astype(BF16)

    def seg(s):
        return pl.BlockSpec((None, lp, tc), lambda j: (s, 0, j))

    col = pl.BlockSpec((lp, tc), lambda j: (0, j))
    return pl.pallas_call(
        body, name=name, grid=(dm // tc,),
        out_shape=(jax.ShapeDtypeStruct(dproj.shape, BF16), jax.ShapeDtypeStruct((SMALL_ROWS, dm), F32)),
        in_specs=[col, col, seg(1), seg(2), seg(3), pl.BlockSpec((3, tc), lambda j: (0, j)), ANY, ANY],
        out_specs=(pl.BlockSpec((4, lp, tc), lambda j: (0, 0, j)), pl.BlockSpec((SMALL_ROWS, tc), lambda j: (0, j))),
        input_output_aliases={6: 0}, compiler_params=_params(1),
    )(dz, dpooled, proj, proj, proj, conv_w, dproj, dep)


def _row_tile(r, c, bytes_per_row_elem=4, budget=2 * 1024 * 1024):
    best = None
    for t in range(16, r + 1, 16):
        if r % t == 0 and t * c * bytes_per_row_elem <= budget:
            best = t
    return best if best is not None else r


def _pair_add(name, g4, recv, core):
    s, r, c = g4.shape
    h = r // 2
    tr = _row_tile(h, c, budget=6 * 1024 * 1024)
    nb = h // tr

    def body(core_ref, g_ref, r_ref, o_ref):
        o_ref[...] = (g_ref[...].astype(F32) + r_ref[...].astype(F32)).astype(BF16)

    grid_spec = pltpu.PrefetchScalarGridSpec(
        num_scalar_prefetch=1, grid=(s, nb),
        in_specs=[pl.BlockSpec((None, tr, c), lambda si, j, core_ref: (si, core_ref[0] * nb + j, 0)),
                  pl.BlockSpec((None, tr, c), lambda si, j, core_ref: (si, j, 0))],
        out_specs=pl.BlockSpec((None, tr, c), lambda si, j, core_ref: (si, j, 0)))
    return pl.pallas_call(
        body, name=name, out_shape=jax.ShapeDtypeStruct((s, h, c), BF16), grid_spec=grid_spec,
        compiler_params=_params(2),
    )(core, g4, recv)


def _chip_sum(name, parts, recv, chip):
    _, h, c = parts.shape
    tr = _row_tile(h, c)

    def body(chip_ref, p_ref, r_ref, o_ref):
        acc = p_ref[...].astype(F32)
        for i in range(len(CHIP_FLIPS)):
            acc = acc + r_ref[i].astype(F32)
        o_ref[...] = acc

    grid_spec = pltpu.PrefetchScalarGridSpec(
        num_scalar_prefetch=1, grid=(h // tr,),
        in_specs=[pl.BlockSpec((None, tr, c), lambda j, chip_ref: (chip_ref[0], j, 0)),
                  pl.BlockSpec((len(CHIP_FLIPS), tr, c), lambda j, chip_ref: (0, j, 0))],
        out_specs=pl.BlockSpec((tr, c), lambda j, chip_ref: (j, 0)))
    return pl.pallas_call(
        body, name=name, out_shape=jax.ShapeDtypeStruct((h, c), F32), grid_spec=grid_spec, compiler_params=_params(1),
    )(chip, parts, recv)


def _adam_update(w, gv, m, v):
    c1 = 1.0 - ADAM_B1 ** ADAM_STEP
    c2 = 1.0 - ADAM_B2 ** ADAM_STEP
    nm = ADAM_B1 * m + (1.0 - ADAM_B1) * gv
    nv = ADAM_B2 * v + (1.0 - ADAM_B2) * (gv * gv)
    return -ADAM_LR * ((nm / c1) / (jnp.sqrt(nv / c2) + ADAM_EPS) + ADAM_WD * w), nm, nv


def _adamw_halves(name, w, g_own, g_sib, m, v, core, part=(0, 1), prev=None):
    r, c = w.shape
    rp = r // part[1]
    h = rp // 2
    tr = _row_tile(h, c, budget=2 * 1024 * 1024)
    nbh = h // tr
    j0 = part[0] * 2 * nbh
    n_prev = 0 if prev is None else 4

    def body(core_ref, w_ref, go_ref, gs_ref, m_ref, v_ref, *rest):
        g_ref, d_ref, nm_ref, nv_ref = rest[n_prev:]
        mine = (pl.program_id(0) // nbh) == core_ref[0]
        gv = jnp.where(mine, go_ref[...], gs_ref[...])
        g_ref[...] = gv
        d_ref[...], nm_ref[...], nv_ref[...] = _adam_update(w_ref[...], gv, m_ref[...], v_ref[...])

    def blk(fn):
        return pl.BlockSpec((tr, c), fn)

    full = blk(lambda j, core_ref: (j0 + j, 0))
    own = blk(lambda j, core_ref: (jnp.clip(j - core_ref[0] * nbh, 0, nbh - 1), 0))
    sib = blk(lambda j, core_ref: (jnp.clip(j - (1 - core_ref[0]) * nbh, 0, nbh - 1), 0))
    grid_spec = pltpu.PrefetchScalarGridSpec(
        num_scalar_prefetch=1, grid=(2 * nbh,), in_specs=[full, own, sib, full, full] + [ANY] * n_prev, out_specs=(full,) * 4)
    sds = jax.ShapeDtypeStruct((r, c), F32)
    return pl.pallas_call(
        body, name=name, out_shape=(sds,) * 4, grid_spec=grid_spec, compiler_params=_params(1),
        input_output_aliases={6 + i: i for i in range(n_prev)},
    )(core, w, g_own, g_sib, m, v, *(prev or ()))


def _adamw(name, w, g, m, v):
    r, c = w.shape

    def body(w_ref, g_ref, m_ref, v_ref, d_ref, nm_ref, nv_ref):
        d_ref[...], nm_ref[...], nv_ref[...] = _adam_update(w_ref[...], g_ref[...], m_ref[...], v_ref[...])

    blk = pl.BlockSpec((r, c), lambda j: (0, 0))
    sds = jax.ShapeDtypeStruct((r, c), F32)
    return pl.pallas_call(
        body, name=name, out_shape=(sds, sds, sds), grid=(1,), in_specs=[blk] * 4, out_specs=(blk,) * 3,
        compiler_params=_params(1),
    )(w, g, m, v)


def _cast_into_slot(name, w, chip, dtype, deps=(), part=(0, 1)):
    r = w.shape[0]
    c = w.shape[1] // part[1]
    tr = _row_tile(r, c)
    q = part[0]

    def body(chip_ref, w_ref, *rest):
        rest[-1][...] = w_ref[...].astype(dtype)

    grid_spec = pltpu.PrefetchScalarGridSpec(
        num_scalar_prefetch=1, grid=(r // tr,),
        in_specs=[pl.BlockSpec((tr, c), lambda j, chip_ref: (j, q))] + [ANY] * len(deps),
        out_specs=pl.BlockSpec((None, tr, c), lambda j, chip_ref: (chip_ref[0], j, 0)))
    return pl.pallas_call(
        body, name=name, out_shape=jax.ShapeDtypeStruct((4, r, c), dtype), grid_spec=grid_spec, compiler_params=_params(1),
    )(chip, w, *deps)


def _place():
    return lax.axis_index("x"), lax.axis_index("y"), lax.axis_index("c")


def _chip_of(x, y, flip):
    px, py = x ^ flip[0], y ^ flip[1]
    return px, py, 2 * px + py


def _half(ref, which):
    rows = ref.shape[0] // 2
    return ref.at[pl.ds(which * rows, rows)]


HBM = pl.BlockSpec(memory_space=pltpu.HBM)
SEM = pl.BlockSpec(memory_space=pltpu.SEMAPHORE)
SPLIT_COPY = pltpu.CompilerParams(has_side_effects=pltpu.SideEffectType.DATAFLOW_SIDE_EFFECTING)


def _in_hbm(arrays):
    return [pltpu.with_memory_space_constraint(t, pltpu.HBM) for t in arrays]


TOKEN = jax.ShapeDtypeStruct((SMALL_ROWS, LANES), F32)
TOKEN_SPEC = pl.BlockSpec(memory_space=pltpu.VMEM)


def _gather_start(name, slabs, groups):
    n = len(slabs)
    ng = len(groups)
    nf = len(CHIP_FLIPS)

    def body(*refs):
        sems, outs = refs[n:n + 2 * ng], refs[n + 2 * ng:2 * n + 2 * ng]
        token = refs[2 * n + 2 * ng]
        token[...] = jnp.zeros_like(token)
        x, y, c = _place()
        k = 2 * x + y
        for g, members in enumerate(groups):
            for i, a in enumerate(members):
                for j, flip in enumerate(CHIP_FLIPS):
                    px, py, _ = _chip_of(x, y, flip)
                    mine = _half(outs[a].at[k], c)
                    pltpu.make_async_remote_copy(
                        src_ref=mine, dst_ref=mine, send_sem=sems[2 * g].at[i * nf + j], recv_sem=sems[2 * g + 1].at[i * nf + j],
                        device_id=(px, py, c), device_id_type=MESH).start()

    sem_shapes = []
    for members in groups:
        sem_shapes += [pltpu.SemaphoreType.DMA((nf * len(members),))] * 2
    res = pl.pallas_call(
        body, name=name,
        out_shape=tuple(sem_shapes) + tuple(pltpu.HBM(t.shape, t.dtype) for t in slabs) + (TOKEN,),
        in_specs=[HBM] * n, out_specs=tuple([SEM] * (2 * ng) + [HBM] * n + [TOKEN_SPEC]),
        input_output_aliases={a: 2 * ng + a for a in range(n)}, compiler_params=SPLIT_COPY,
    )(*_in_hbm(slabs))
    return [(res[2 * g], res[2 * g + 1]) for g in range(ng)], list(res[2 * ng:2 * ng + n]), res[2 * ng + n]


def _gather_wait(name, slabs, sems, after):
    n = len(slabs)
    nf = len(CHIP_FLIPS)

    def body(*refs):
        ins = refs[:n]
        ssem, rsem = refs[n], refs[n + 1]
        x, y, c = _place()
        k = 2 * x + y
        for a in range(n):
            for j, flip in enumerate(CHIP_FLIPS):
                _, _, kj = _chip_of(x, y, flip)
                cp = pltpu.make_async_remote_copy(
                    src_ref=_half(ins[a].at[k], c), dst_ref=_half(ins[a].at[kj], c), send_sem=ssem.at[a * nf + j],
                    recv_sem=rsem.at[a * nf + j], device_id=(x, y, c), device_id_type=MESH)
                cp.wait_send()
                cp.wait_recv()

    return pl.pallas_call(
        body, name=name, out_shape=tuple(pltpu.HBM(t.shape, t.dtype) for t in slabs),
        in_specs=[HBM] * n + [SEM, SEM, ANY], out_specs=tuple([HBM] * n),
        input_output_aliases={a: a for a in range(n)}, compiler_params=SPLIT_COPY,
    )(*slabs, sems[0], sems[1], after)


def _gather_pass(name, slabs, sems, after):
    n = len(slabs)
    nf = len(CHIP_FLIPS)

    def body(*refs):
        ins = refs[:n]
        ssem, rsem = refs[n], refs[n + 1]
        ssem2, rsem2 = refs[n + 3], refs[n + 4]
        x, y, c = _place()
        k = 2 * x + y
        for a in range(n):
            for j, flip in enumerate(CHIP_FLIPS):
                _, _, kj = _chip_of(x, y, flip)
                landed = _half(ins[a].at[kj], c)
                cp = pltpu.make_async_remote_copy(
                    src_ref=_half(ins[a].at[k], c), dst_ref=landed, send_sem=ssem.at[a * nf + j],
                    recv_sem=rsem.at[a * nf + j], device_id=(x, y, c), device_id_type=MESH)
                cp.wait_send()
                cp.wait_recv()
                pltpu.make_async_remote_copy(
                    src_ref=landed, dst_ref=landed, send_sem=ssem2.at[a * nf + j], recv_sem=rsem2.at[a * nf + j],
                    device_id=(x, y, 1 - c), device_id_type=MESH).start()

    sem = pltpu.SemaphoreType.DMA((nf * n,))
    res = pl.pallas_call(
        body, name=name, out_shape=(sem, sem) + tuple(pltpu.HBM(t.shape, t.dtype) for t in slabs),
        in_specs=[HBM] * n + [SEM, SEM, ANY], out_specs=tuple([SEM, SEM] + [HBM] * n),
        input_output_aliases={a: 2 + a for a in range(n)}, compiler_params=SPLIT_COPY,
    )(*slabs, sems[0], sems[1], after)
    return (res[0], res[1]), list(res[2:])


def _pass_wait(name, slabs, sems, after):
    n = len(slabs)
    nf = len(CHIP_FLIPS)

    def body(*refs):
        ins = refs[:n]
        ssem, rsem = refs[n], refs[n + 1]
        x, y, c = _place()
        for a in range(n):
            for j, flip in enumerate(CHIP_FLIPS):
                _, _, kj = _chip_of(x, y, flip)
                cp = pltpu.make_async_remote_copy(
                    src_ref=_half(ins[a].at[kj], c), dst_ref=_half(ins[a].at[kj], 1 - c), send_sem=ssem.at[a * nf + j],
                    recv_sem=rsem.at[a * nf + j], device_id=(x, y, c), device_id_type=MESH)
                cp.wait_send()
                cp.wait_recv()

    return pl.pallas_call(
        body, name=name, out_shape=tuple(pltpu.HBM(t.shape, t.dtype) for t in slabs),
        in_specs=[HBM] * n + [SEM, SEM, ANY], out_specs=tuple([HBM] * n),
        input_output_aliases={a: a for a in range(n)}, compiler_params=SPLIT_COPY,
    )(*slabs, sems[0], sems[1], after)


NEIGHBOUR_FLIPS = CHIP_FLIPS[:2]


def _relay_chips(x, y, c):
    fx, fy = x ^ c, y ^ (1 - c)
    return (fx, fy), 2 * fx + fy, 2 * (1 - x) + (1 - y)


def _ag_start(name, slabs, deps=()):
    n = len(slabs)
    nn = len(NEIGHBOUR_FLIPS)

    def body(*refs):
        no = n + len(deps)
        ssem, rsem = refs[no], refs[no + 1]
        outs = refs[no + 2:no + 2 + n]
        token = refs[no + 2 + n]
        token[...] = jnp.zeros_like(token)
        x, y, c = _place()
        k = 2 * x + y
        for a in range(n):
            for j, flip in enumerate(NEIGHBOUR_FLIPS):
                px, py, _ = _chip_of(x, y, flip)
                mine = _half(outs[a].at[k], c)
                pltpu.make_async_remote_copy(src_ref=mine, dst_ref=mine, send_sem=ssem.at[a * nn + j],
                                             recv_sem=rsem.at[a * nn + j], device_id=(px, py, c), device_id_type=MESH).start()

    sem = pltpu.SemaphoreType.DMA((nn * n,))
    res = pl.pallas_call(
        body, name=name, out_shape=(sem, sem) + tuple(pltpu.HBM(t.shape, t.dtype) for t in slabs) + (TOKEN,),
        in_specs=[HBM] * n + [ANY] * len(deps), out_specs=tuple([SEM, SEM] + [HBM] * n + [TOKEN_SPEC]),
        input_output_aliases={a: 2 + a for a in range(n)}, compiler_params=SPLIT_COPY,
    )(*_in_hbm(slabs), *deps)
    return (res[0], res[1]), list(res[2:2 + n]), res[2 + n]


def _ag_relay(name, slabs, sems, after, then_start=()):
    n = len(slabs)
    m = len(then_start)
    nn = len(NEIGHBOUR_FLIPS)

    def body(*refs):
        no = n + 2 + m + len(after)
        ins = refs[:n]
        ssem, rsem = refs[n], refs[n + 1]
        r_s, r_r, p_s, p_r = refs[no:no + 4]
        x, y, c = _place()
        k = 2 * x + y
        (fx, fy), _, _ = _relay_chips(x, y, c)
        for a in range(n):
            for j, flip in enumerate(NEIGHBOUR_FLIPS):
                _, _, kj = _chip_of(x, y, flip)
                landed = _half(ins[a].at[kj], c)
                cp = pltpu.make_async_remote_copy(
                    src_ref=_half(ins[a].at[k], c), dst_ref=landed, send_sem=ssem.at[a * nn + j],
                    recv_sem=rsem.at[a * nn + j], device_id=(x, y, c), device_id_type=MESH)
                cp.wait_send()
                cp.wait_recv()
        for a in range(n):
            near = _half(ins[a].at[2 * (x ^ (1 - c)) + (y ^ c)], c)
            pltpu.make_async_remote_copy(src_ref=near, dst_ref=near, send_sem=r_s.at[a], recv_sem=r_r.at[a],
                                         device_id=(fx, fy, c), device_id_type=MESH).start()
            for j, flip in enumerate(NEIGHBOUR_FLIPS):
                _, _, kj = _chip_of(x, y, flip)
                landed = _half(ins[a].at[kj], c)
                pltpu.make_async_remote_copy(src_ref=landed, dst_ref=landed, send_sem=p_s.at[a * nn + j],
                                             recv_sem=p_r.at[a * nn + j], device_id=(x, y, 1 - c), device_id_type=MESH).start()
        if m:
            d_s, d_r = refs[no + 4 + n], refs[no + 5 + n]
            nxt = refs[no + 6 + n:]
            for a in range(m):
                for j, flip in enumerate(NEIGHBOUR_FLIPS):
                    px, py, _ = _chip_of(x, y, flip)
                    mine = _half(nxt[a].at[k], c)
                    pltpu.make_async_remote_copy(src_ref=mine, dst_ref=mine, send_sem=d_s.at[a * nn + j],
                                                 recv_sem=d_r.at[a * nn + j], device_id=(px, py, c), device_id_type=MESH).start()

    rsem_t = pltpu.SemaphoreType.DMA((n,))
    psem_t = pltpu.SemaphoreType.DMA((nn * n,))
    out_shape = (rsem_t, rsem_t, psem_t, psem_t) + tuple(pltpu.HBM(t.shape, t.dtype) for t in slabs)
    out_specs = [SEM] * 4 + [HBM] * n
    aliases = {a: 4 + a for a in range(n)}
    if m:
        dsem_t = pltpu.SemaphoreType.DMA((nn * m,))
        out_shape += (dsem_t, dsem_t) + tuple(pltpu.HBM(t.shape, t.dtype) for t in then_start)
        out_specs += [SEM, SEM] + [HBM] * m
        aliases.update({n + 2 + a: 4 + n + 2 + a for a in range(m)})
    res = pl.pallas_call(
        body, name=name, out_shape=out_shape, in_specs=[HBM] * n + [SEM, SEM] + [HBM] * m + [ANY] * len(after),
        out_specs=tuple(out_specs), input_output_aliases=aliases, compiler_params=SPLIT_COPY,
    )(*slabs, sems[0], sems[1], *_in_hbm(list(then_start)), *after)
    if not m:
        return tuple(res[:4]), list(res[4:])
    return (tuple(res[:4]), list(res[4:4 + n])), ((res[4 + n], res[5 + n]), list(res[6 + n:]))


def _ag_relay_wait(name, slabs, sems, after):
    n = len(slabs)
    nn = len(NEIGHBOUR_FLIPS)

    def body(*refs):
        no = n + 4 + len(after)
        ins = refs[:n]
        r_s, r_r, p_s, p_r = refs[n:n + 4]
        f_s, f_r = refs[no], refs[no + 1]
        x, y, c = _place()
        _, _, kd = _relay_chips(x, y, c)
        for a in range(n):
            near = _half(ins[a].at[2 * (x ^ (1 - c)) + (y ^ c)], c)
            diag = _half(ins[a].at[kd], c)
            cp = pltpu.make_async_remote_copy(src_ref=near, dst_ref=diag, send_sem=r_s.at[a], recv_sem=r_r.at[a],
                                              device_id=(x, y, c), device_id_type=MESH)
            cp.wait_send()
            cp.wait_recv()
            for j, flip in enumerate(NEIGHBOUR_FLIPS):
                _, _, kj = _chip_of(x, y, flip)
                cp = pltpu.make_async_remote_copy(
                    src_ref=_half(ins[a].at[kj], c), dst_ref=_half(ins[a].at[kj], 1 - c), send_sem=p_s.at[a * nn + j],
                    recv_sem=p_r.at[a * nn + j], device_id=(x, y, c), device_id_type=MESH)
                cp.wait_send()
                cp.wait_recv()
            pltpu.make_async_remote_copy(src_ref=diag, dst_ref=diag, send_sem=f_s.at[a], recv_sem=f_r.at[a],
                                         device_id=(x, y, 1 - c), device_id_type=MESH).start()

    sem = pltpu.SemaphoreType.DMA((n,))
    res = pl.pallas_call(
        body, name=name, out_shape=(sem, sem) + tuple(pltpu.HBM(t.shape, t.dtype) for t in slabs),
        in_specs=[HBM] * n + [SEM] * 4 + [ANY] * len(after), out_specs=tuple([SEM, SEM] + [HBM] * n),
        input_output_aliases={a: 2 + a for a in range(n)}, compiler_params=SPLIT_COPY,
    )(*slabs, *sems, *after)
    return (res[0], res[1]), list(res[2:])


def _ag_final_wait(name, slabs, sems, after):
    n = len(slabs)

    def body(*refs):
        ins = refs[:n]
        f_s, f_r = refs[n], refs[n + 1]
        x, y, c = _place()
        _, _, kd = _relay_chips(x, y, c)
        for a in range(n):
            cp = pltpu.make_async_remote_copy(
                src_ref=_half(ins[a].at[kd], c), dst_ref=_half(ins[a].at[kd], 1 - c), send_sem=f_s.at[a], recv_sem=f_r.at[a],
                device_id=(x, y, c), device_id_type=MESH)
            cp.wait_send()
            cp.wait_recv()

    return pl.pallas_call(
        body, name=name, out_shape=tuple(pltpu.HBM(t.shape, t.dtype) for t in slabs),
        in_specs=[HBM] * n + [SEM, SEM] + [ANY] * len(after), out_specs=tuple([HBM] * n),
        input_output_aliases={a: a for a in range(n)}, compiler_params=SPLIT_COPY,
    )(*slabs, sems[0], sems[1], *after)


def _sibling_part(ref, c, halves):
    if not halves:
        return ref
    h = ref.shape[1] // 2
    return ref.at[:, pl.ds((1 - c) * h, h)]


def _swap_start(name, grads, halves=True, deps=()):
    n = len(grads)

    def body(*refs):
        no = 2 * n + len(deps)
        ssem, rsem = refs[no], refs[no + 1]
        src, land = refs[no + 2:no + n + 2], refs[no + n + 2:no + 2 * n + 2]
        token = refs[no + 2 * n + 2]
        token[...] = jnp.zeros_like(token)
        x, y, c = _place()
        for a in range(n):
            pltpu.make_async_remote_copy(
                src_ref=_sibling_part(src[a], c, halves), dst_ref=land[a], send_sem=ssem.at[a], recv_sem=rsem.at[a],
                device_id=(x, y, 1 - c), device_id_type=MESH).start()

    zones = [lax.empty((g.shape[0], g.shape[1] // 2, g.shape[2]) if halves else g.shape, g.dtype) for g in grads]
    sem = pltpu.SemaphoreType.DMA((n,))
    res = pl.pallas_call(
        body, name=name,
        out_shape=(sem, sem) + tuple(pltpu.HBM(t.shape, t.dtype) for t in list(grads) + zones) + (TOKEN,),
        in_specs=[HBM] * (2 * n) + [ANY] * len(deps), out_specs=tuple([SEM, SEM] + [HBM] * (2 * n) + [TOKEN_SPEC]),
        input_output_aliases={i: 2 + i for i in range(2 * n)}, compiler_params=SPLIT_COPY,
    )(*_in_hbm(list(grads) + zones), *deps)
    return (res[0], res[1], list(res[2:2 + n]), list(res[2 + n:2 + 2 * n])), res[2 + 2 * n]


def _swap_wait(name, ssem, rsem, grads, zones, after, halves=True):
    n = len(grads)

    def body(*refs):
        src, land = refs[:n], refs[n:2 * n]
        ss, rs = refs[2 * n], refs[2 * n + 1]
        x, y, c = _place()
        for a in range(n):
            cp = pltpu.make_async_remote_copy(
                src_ref=_sibling_part(src[a], c, halves), dst_ref=land[a], send_sem=ss.at[a], recv_sem=rs.at[a],
                device_id=(x, y, c), device_id_type=MESH)
            cp.wait_send()
            cp.wait_recv()

    res = pl.pallas_call(
        body, name=name, out_shape=tuple(pltpu.HBM(t.shape, t.dtype) for t in list(grads) + list(zones)),
        in_specs=[HBM] * (2 * n) + [SEM, SEM] + [ANY] * len(after), out_specs=tuple([HBM] * (2 * n)),
        input_output_aliases={i: i for i in range(2 * n)}, compiler_params=SPLIT_COPY,
    )(*grads, *zones, ssem, rsem, *after)
    return list(res[:n]), list(res[n:])


def _sibling_exchange(name, slabs):
    n = len(slabs)
    nf = len(CHIP_FLIPS)

    def body(*refs):
        outs = refs[n:2 * n]
        ssem, rsem = refs[2 * n:]
        x, y, c = _place()

        def copy(a, j, which, to):
            _, _, kj = _chip_of(x, y, CHIP_FLIPS[j])
            ref = _half(outs[a].at[kj], which)
            return pltpu.make_async_remote_copy(src_ref=ref, dst_ref=ref, send_sem=ssem.at[a * nf + j],
                                                recv_sem=rsem.at[a * nf + j], device_id=to, device_id_type=MESH)

        sends = [copy(a, j, c, (x, y, 1 - c)) for a in range(n) for j in range(nf)]
        for cp in sends:
            cp.start()
        for a in range(n):
            for j in range(nf):
                copy(a, j, 1 - c, (x, y, c)).wait_recv()
        for cp in sends:
            cp.wait_send()

    return pl.pallas_call(
        body, name=name, out_shape=tuple(jax.ShapeDtypeStruct(t.shape, t.dtype) for t in slabs),
        in_specs=[ANY] * n, out_specs=(ANY,) * n, input_output_aliases={a: a for a in range(n)},
        scratch_shapes=[pltpu.SemaphoreType.DMA((nf * n,)), pltpu.SemaphoreType.DMA((nf * n,))],
    )(*slabs)


def _sibling_swap(name, grads):
    n = len(grads)

    def body(*refs):
        ins, outs = refs[:n], refs[n:2 * n]
        ssem, rsem = refs[2 * n:]
        x, y, c = _place()
        cps = []
        for a in range(n):
            h = ins[a].shape[1] // 2
            cps.append(pltpu.make_async_remote_copy(
                src_ref=ins[a].at[:, pl.ds((1 - c) * h, h)], dst_ref=outs[a], send_sem=ssem.at[a], recv_sem=rsem.at[a],
                device_id=(x, y, 1 - c), device_id_type=MESH))
        for cp in cps:
            cp.start()
        for cp in cps:
            cp.wait()

    return pl.pallas_call(
        body, name=name,
        out_shape=tuple(jax.ShapeDtypeStruct((g.shape[0], g.shape[1] // 2, g.shape[2]), g.dtype) for g in grads),
        in_specs=[ANY] * n, out_specs=(ANY,) * n,
        scratch_shapes=[pltpu.SemaphoreType.DMA((n,)), pltpu.SemaphoreType.DMA((n,))],
    )(*grads)


def _scatter_start(name, parts):
    n = len(parts)
    nf = len(CHIP_FLIPS)

    def body(*refs):
        ssem, rsem = refs[2 * n], refs[2 * n + 1]
        src, land = refs[2 * n + 2:3 * n + 2], refs[3 * n + 2:4 * n + 2]
        token = refs[4 * n + 2]
        token[...] = jnp.zeros_like(token)
        x, y, c = _place()
        for a in range(n):
            for j, flip in enumerate(CHIP_FLIPS):
                px, py, kj = _chip_of(x, y, flip)
                pltpu.make_async_remote_copy(
                    src_ref=src[a].at[kj], dst_ref=land[a].at[j], send_sem=ssem.at[a * nf + j], recv_sem=rsem.at[a * nf + j],
                    device_id=(px, py, c), device_id_type=MESH).start()

    zones = [lax.empty((nf,) + p.shape[1:], p.dtype) for p in parts]
    sem = pltpu.SemaphoreType.DMA((nf * n,))
    res = pl.pallas_call(
        body, name=name,
        out_shape=(sem, sem) + tuple(pltpu.HBM(t.shape, t.dtype) for t in list(parts) + zones)
        + (jax.ShapeDtypeStruct((SMALL_ROWS, LANES), F32),),
        in_specs=[HBM] * (2 * n),
        out_specs=tuple([SEM, SEM] + [HBM] * (2 * n) + [pl.BlockSpec(memory_space=pltpu.VMEM)]),
        input_output_aliases={i: 2 + i for i in range(2 * n)}, compiler_params=SPLIT_COPY,
    )(*_in_hbm(list(parts) + zones))
    return (res[0], res[1], list(res[2:2 + n]), list(res[2 + n:2 + 2 * n])), res[2 + 2 * n]


def _scatter_wait(name, ssem, rsem, parts, zones, after):
    n = len(parts)
    nf = len(CHIP_FLIPS)

    def body(*refs):
        src, land = refs[:n], refs[n:2 * n]
        ss, rs = refs[2 * n], refs[2 * n + 1]
        x, y, c = _place()
        for a in range(n):
            for j, flip in enumerate(CHIP_FLIPS):
                _, _, kj = _chip_of(x, y, flip)
                cp = pltpu.make_async_remote_copy(
                    src_ref=src[a].at[kj], dst_ref=land[a].at[j], send_sem=ss.at[a * nf + j], recv_sem=rs.at[a * nf + j],
                    device_id=(x, y, c), device_id_type=MESH)
                cp.wait_send()
                cp.wait_recv()

    res = pl.pallas_call(
        body, name=name, out_shape=tuple(pltpu.HBM(t.shape, t.dtype) for t in list(parts) + list(zones)),
        in_specs=[HBM] * (2 * n) + [SEM, SEM] + [ANY] * len(after), out_specs=tuple([HBM] * (2 * n)),
        input_output_aliases={i: i for i in range(2 * n)}, compiler_params=SPLIT_COPY,
    )(*parts, *zones, ssem, rsem, *after)
    return list(res[:n]), list(res[n:])


N_PEERS = 7


def _peer(x, y, c, mask):
    px, py, pc = x ^ ((mask >> 2) & 1), y ^ ((mask >> 1) & 1), c ^ (mask & 1)
    return (px, py, pc), 4 * px + 2 * py + pc


def _reduce_start(vec, deps):
    nd = len(deps)

    def body(*refs):
        ssem, rsem, src, land, token = refs[2 + nd:]
        token[...] = jnp.zeros_like(token)
        x, y, c = _place()
        me = 4 * x + 2 * y + c
        for mask in range(1, N_PEERS + 1):
            to, _ = _peer(x, y, c, mask)
            pltpu.make_async_remote_copy(src_ref=src, dst_ref=land.at[me], send_sem=ssem.at[mask - 1],
                                         recv_sem=rsem.at[mask - 1], device_id=to, device_id_type=MESH).start()

    zone = lax.empty((N_PEERS + 1,) + vec.shape, vec.dtype)
    sem = pltpu.SemaphoreType.DMA((N_PEERS,))
    res = pl.pallas_call(
        body, name="reduce_start",
        out_shape=(sem, sem, pltpu.HBM(vec.shape, vec.dtype), pltpu.HBM(zone.shape, zone.dtype), TOKEN),
        in_specs=[HBM, HBM] + [ANY] * nd, out_specs=(SEM, SEM, HBM, HBM, TOKEN_SPEC),
        input_output_aliases={0: 2, 1: 3}, compiler_params=SPLIT_COPY,
    )(*_in_hbm([vec, zone]), *deps)
    return res[:4], res[4]


def _reduce_wait(ssem, rsem, vec, zone, after):
    def body(src, land, ss, rs, *_):
        x, y, c = _place()
        for mask in range(1, N_PEERS + 1):
            _, frm = _peer(x, y, c, mask)
            cp = pltpu.make_async_remote_copy(src_ref=src, dst_ref=land.at[frm], send_sem=ss.at[mask - 1],
                                              recv_sem=rs.at[mask - 1], device_id=(x, y, c), device_id_type=MESH)
            cp.wait_send()
            cp.wait_recv()

    return pl.pallas_call(
        body, name="reduce_wait", out_shape=(pltpu.HBM(vec.shape, vec.dtype), pltpu.HBM(zone.shape, zone.dtype)),
        in_specs=[HBM, HBM, SEM, SEM] + [ANY] * len(after), out_specs=(HBM, HBM),
        input_output_aliases={0: 0, 1: 1}, compiler_params=SPLIT_COPY,
    )(vec, zone, ssem, rsem, *after)


def _reduce_sum(vec, zone, me, loss_row, loss_scale):
    r, dm = vec.shape

    def body(me_ref, v_ref, z_ref, o_ref, l_ref):
        acc = None
        for i in range(N_PEERS + 1):
            term = jnp.where(me_ref[0] == i, v_ref[...], z_ref[i])
            acc = term if acc is None else acc + term
        o_ref[...] = acc
        l_ref[...] = jnp.sum(acc[loss_row:loss_row + SMALL_ROWS, :], axis=(0, 1), keepdims=True) * loss_scale

    grid_spec = pltpu.PrefetchScalarGridSpec(
        num_scalar_prefetch=1, grid=(1,),
        in_specs=[pl.BlockSpec((r, dm), lambda i, me_ref: (0, 0)), pl.BlockSpec((N_PEERS + 1, r, dm), lambda i, me_ref: (0, 0, 0))],
        out_specs=(pl.BlockSpec((r, dm), lambda i, me_ref: (0, 0)), pl.BlockSpec((1, 1), lambda i, me_ref: (0, 0))))
    return pl.pallas_call(
        body, name="reduce_sum", out_shape=(jax.ShapeDtypeStruct((r, dm), F32), jax.ShapeDtypeStruct((1, 1), F32)),
        grid_spec=grid_spec, compiler_params=_params(1),
    )(me, vec, zone)


def kernel(x, meta_tokens, norm_mix_g, w_in, b_gate, pool_w, pool_scale, conv_w, conv_out_w, w_o, norm_ffn_g, w_gate_up, w_down, norm_final_g, loss_target, m_meta_tokens, m_norm_mix_g, m_w_in, m_b_gate, m_pool_w, m_pool_scale, m_conv_w, m_conv_out_w, m_w_o, m_norm_ffn_g, m_w_gate_up, m_w_down, m_norm_final_g, v_meta_tokens, v_norm_mix_g, v_w_in, v_b_gate, v_pool_w, v_pool_scale, v_conv_w, v_conv_out_w, v_w_o, v_norm_ffn_g, v_w_gate_up, v_w_down, v_norm_final_g):
    seq, dm = x.shape[1], x.shape[2]
    tail = LANES
    tm = tail
    lp = seq + tail
    n_chips = 4
    n_groups = len(POOL_WINDOWS)
    gw = dm // n_groups
    tc = min(256, gw)
    cx, cy, cc = _place()
    chip = 2 * cx + cy
    dloc = dm // n_chips

    pool2 = pool_w.reshape(n_groups * pool_w.shape[1], gw)
    big = {"w_in": w_in, "w_gate_up": w_gate_up, "pool_w": pool2, "conv_out_w": conv_out_w, "w_o": w_o, "w_down": w_down}
    chip1 = jnp.reshape(chip, (1,)).astype(jnp.int32)
    core = jnp.reshape(cc, (1,)).astype(jnp.int32)
    small_loc = jnp.concatenate([meta_tokens, jnp.pad(conv_w, ((0, 8 - conv_w.shape[0]), (0, 0))),
                                 jnp.zeros((8, dloc), F32)], axis=0)
    g1, g2, g3 = norm_mix_g.reshape(1, dm), norm_ffn_g.reshape(1, dm), norm_final_g.reshape(1, dm)
    b_gate2 = b_gate.reshape(2, dm)
    ps = pool_scale.reshape(1, dm)
    first = [_cast_into_slot("cast_w_in", w_in, chip1, BF16), _cast_into_slot("place_small", small_loc, chip1, F32)]
    sems, first, token = _ag_start("ag_start_first", first)
    cast = {nme: _cast_into_slot("cast_" + nme, big[nme], chip1, BF16, deps=(token,))
            for nme in ["pool_w", "conv_out_w", "w_o", "w_gate_up", "w_down"]}
    sems, first = _ag_relay("ag_relay_first", first, sems, list(cast.values()))
    sems, first = _ag_relay_wait("ag_relay_wait_first", first, sems, [])
    w_in4, small4 = _ag_final_wait("ag_final_wait_first", first, sems, [])
    mixer_w = [cast["pool_w"], cast["conv_out_w"], cast["w_o"]]
    sems_mix, mixer_w, token = _ag_start("ag_start_mixer", mixer_w, deps=(w_in4,))
    sems_gu, (w_gu4,), token = _ag_start("ag_start_gate_up", [cast["w_gate_up"]], deps=(token,))

    small_f = jnp.transpose(small4, (1, 0, 2)).reshape(small4.shape[1], dm)
    meta_f = small_f[:N_META]
    conv_w_f = small_f[N_META:N_META + 3]
    h0 = jnp.concatenate([x[0], jnp.zeros((tail - N_META, dm), F32), meta_f], axis=0)
    hn1 = _rms_fwd("rms_mix", h0, g1, tm, deps=(token,))
    proj = _nn_sharded("proj", hn1, w_in4, 6)
    sems_mix, mixer_w = _ag_relay("ag_relay_mixer", mixer_w, sems_mix, [proj])
    sems_gu, (w_gu4,) = _ag_relay("ag_relay_gate_up", [w_gu4], sems_gu, [mixer_w[0]])
    sems_down, (w_down4,), token = _ag_start("ag_start_down", [cast["w_down"]], deps=(w_gu4,))
    pooled, z = _mixer_fwd("mixer_fwd", proj, conv_w_f, tc, token)
    sems_mix, mixer_w = _ag_relay_wait("ag_relay_wait_mixer", mixer_w, sems_mix, [pooled])
    pool4, conv_out4, w_o4 = _ag_final_wait("ag_final_wait_mixer", mixer_w, sems_mix, [])
    pool_f = jnp.transpose(pool4.reshape(n_chips, n_groups, gw // n_chips, gw), (1, 0, 2, 3)).reshape(n_groups, gw, gw)
    conv_out_f = conv_out4.reshape(dm, dm)
    w_o_f = w_o4.reshape(dm, dm)
    ya = _pool_fwd("pool_proj", pooled, pool_f)
    yb = _nn_plain("conv_out", z, conv_out_f, BF16)
    mix = _gate_mix("gate_mix", proj, b_gate2, ya, ps, yb, tm)
    sems_gu, (w_gu4,) = _ag_relay_wait("ag_relay_wait_gate_up", [w_gu4], sems_gu, [mix])
    h1 = _nn_plain("attn_out", mix, w_o_f, F32, res=h0, tn_pref=256)
    (w_gu4,) = _ag_final_wait("ag_final_wait_gate_up", [w_gu4], sems_gu, [h1])
    hn2 = _rms_fwd("rms_ffn", h1, g2, tm)
    sems_down, (w_down4,) = _ag_relay("ag_relay_down", [w_down4], sems_down, [hn2])
    gu, act = _gate_up_swiglu("gate_up", hn2, w_gu4, w_down4)
    sems_down, (w_down4,) = _ag_relay_wait("ag_relay_wait_down", [w_down4], sems_down, [act])
    (w_down4,) = _ag_final_wait("ag_final_wait_down", [w_down4], sems_down, [])
    w_down_f = w_down4.reshape(-1, dm)
    h2 = _nn_rows("ffn_down", act, w_down_f, h1)
    dh2, dh2b, loss_cols, dg3 = _final_loss("final_loss", h2, g3, loss_target[0], tm)

    def scatter(tag, names_g, swap, after):
        grads_g, got = _swap_wait("swap_wait_" + tag, *swap, [after])
        pairs = [_pair_add("pair_add_" + nme, g4, rv, core) for nme, g4, rv in zip(names_g, grads_g, got)]
        return _scatter_start("scatter_start_" + tag, pairs)

    dgu = _dact_swiglu_bwd("d_gate_up", dh2b, w_down_f, gu)
    gw_down = _tn_plain("dw_down", act, dh2b)
    gw_gu = _tn_sharded("dw_gate_up", hn2, dgu, n_chips)
    swap_a, token = _swap_start("swap_start_a", [gw_gu, gw_down.reshape(n_chips, -1, dm)])
    dhn2 = _nt_sharded("d_hn2", dgu, w_gu4, tr_pref=2816, row_tiles=2, deps=(token,))
    flight_a, token = scatter("a", ["w_gate_up", "w_down"], swap_a, dhn2)
    dh1, dh1b, dg2 = _rms_bwd("rms_ffn_bwd", dhn2, h1, g2, dh2, tm, token)
    dmix = _nt_plain("d_mix", dh1b, w_o_f)
    gw_o = _tn_plain("dw_o", mix, dh1b)
    dproj, dyb, dya, db_gate, dps = _gate_bwd("gate_bwd", dmix, proj, b_gate2, ya, ps, yb, tm)
    gw_conv_out = _tn_plain("dw_conv_out", z, dyb)
    gw_pool = _pool_bwd_w("dw_pool", pooled, dya)
    gw_pool = jnp.transpose(gw_pool.reshape(n_groups, n_chips, gw // n_chips, gw), (1, 0, 2, 3))
    swap_b, token = _swap_start("swap_start_b", [gw_o.reshape(n_chips, dloc, dm), gw_conv_out.reshape(n_chips, dloc, dm),
                                                 gw_pool.reshape(n_chips, n_groups * (gw // n_chips), gw)])
    dpooled = _pool_bwd_act("d_pooled", dya, pool_f, deps=(token,))
    dz = _nt_plain("d_z", dyb, conv_out_f)
    flight_b, token = scatter("b", ["w_o", "conv_out_w", "pool_w"], swap_b, dz)
    dproj, dconv_w = _mixer_bwd("mixer_bwd", dz, dpooled, proj, conv_w_f, dproj, tc, token)
    gw_in0 = _tn_sharded("dw_in_0", hn1, dproj, n_chips, part=(0, 2))
    swap_c0, token = _swap_start("swap_start_c0", [gw_in0])
    gw_in1 = _tn_sharded("dw_in_1", hn1, dproj, n_chips, part=(1, 2), deps=(token,))
    flight_c0, token = scatter("c0", ["w_in_0"], swap_c0, gw_in1)
    swap_c1, token = _swap_start("swap_start_c1", [gw_in1], deps=(token,))
    dhn1 = _nt_in_proj("d_hn1", dproj, w_in4, deps=(token,))
    flight_c1, token = scatter("c1", ["w_in_1"], swap_c1, dhn1)
    dx, dtail, dg1 = _rms_bwd_input("rms_mix_bwd", dhn1, h0, g1, dh1, tm, seq, token)
    grad_x = dx[None]
    dmeta = dtail[tail - N_META:]

    given = dict(meta_tokens=(meta_tokens, m_meta_tokens, v_meta_tokens), norm_mix_g=(norm_mix_g, m_norm_mix_g, v_norm_mix_g),
                 w_in=(w_in, m_w_in, v_w_in), b_gate=(b_gate, m_b_gate, v_b_gate), pool_w=(pool_w, m_pool_w, v_pool_w),
                 pool_scale=(pool_scale, m_pool_scale, v_pool_scale), conv_w=(conv_w, m_conv_w, v_conv_w),
                 conv_out_w=(conv_out_w, m_conv_out_w, v_conv_out_w), w_o=(w_o, m_w_o, v_w_o),
                 norm_ffn_g=(norm_ffn_g, m_norm_ffn_g, v_norm_ffn_g), w_gate_up=(w_gate_up, m_w_gate_up, v_w_gate_up),
                 w_down=(w_down, m_w_down, v_w_down), norm_final_g=(norm_final_g, m_norm_final_g, v_norm_final_g))
    order = list(given.keys())
    grad, delta, new_m, new_v = {}, {}, {}, {}
    vec = jnp.concatenate([dg1, dg2, dg3, db_gate, dps, loss_cols, dconv_w, dmeta], axis=0)
    loss_row = 5 * SMALL_ROWS
    groups_g = {"a": [("w_gate_up", (0, 1)), ("w_down", (0, 1))], "b": [("w_o", (0, 1)), ("conv_out_w", (0, 1)), ("pool_w", (0, 1))],
                "c0": [("w_in", (0, 2))], "c1": [("w_in", (1, 2))]}
    results = {}

    def reduced(tag, flight, after):
        pairs, zones = _scatter_wait("scatter_wait_" + tag, *flight, after)
        halves = [_chip_sum("chip_sum_%s_%d" % (nme, part[0]), p, rv, chip1) for (nme, part), p, rv in zip(groups_g[tag], pairs, zones)]
        return _swap_start("send_start_" + tag, halves, halves=False)

    def update(tag, send, after):
        halves, sib_halves = _swap_wait("send_wait_" + tag, *send, after, halves=False)
        deltas = []
        for (nme, part), g_own, g_sib in zip(groups_g[tag], halves, sib_halves):
            w, m, v = given[nme]
            shape2 = (2 * g_own.shape[0] * part[1], g_own.shape[1])
            results[nme] = _adamw_halves("adamw_%s_%d" % (nme, part[0]), w.reshape(shape2), g_own, g_sib, m.reshape(shape2),
                                         v.reshape(shape2), core, part=part, prev=results.get(nme))
            grad[nme], delta[nme], new_m[nme], new_v[nme] = [t.reshape(w.shape) for t in results[nme]]
            deltas.append(results[nme][1])
        return deltas

    send_a, token = reduced("a", flight_a, [dx])
    send_b, token = reduced("b", flight_b, [token])
    done_a = update("a", send_a, [token])
    send_c0, token = reduced("c0", flight_c0, done_a)
    done_b = update("b", send_b, [token])
    send_c1, token = reduced("c1", flight_c1, done_b)
    me1 = jnp.reshape(4 * cx + 2 * cy + cc, (1,)).astype(jnp.int32)
    red_flight, token = _reduce_start(vec, [token])
    done_c0 = update("c0", send_c0, [token])
    done_c1 = update("c1", send_c1, done_c0)
    red, loss11 = _reduce_sum(*_reduce_wait(*red_flight, done_c1), me1, loss_row, 0.5 / dm)
    loss = loss11[0, 0]
    col0 = chip * dloc
    g_small = {
        "norm_mix_g": red[0], "norm_ffn_g": red[SMALL_ROWS], "norm_final_g": red[2 * SMALL_ROWS],
        "b_gate": red[3 * SMALL_ROWS:3 * SMALL_ROWS + 2].reshape(-1), "pool_scale": red[4 * SMALL_ROWS],
        "conv_w": lax.dynamic_slice(red, (6 * SMALL_ROWS, col0), (3, dloc)),
        "meta_tokens": lax.dynamic_slice(red, (7 * SMALL_ROWS, col0), (N_META, dloc)),
    }

    vec_names = ["norm_mix_g", "norm_ffn_g", "norm_final_g", "pool_scale"]

    def slab_vec(pick):
        rows = [pick(nme).reshape(1, dm) for nme in vec_names] + [pick("b_gate").reshape(2, dm), jnp.zeros((2, dm), F32)]
        return jnp.concatenate(rows, axis=0)

    def slab_col(pick):
        return jnp.concatenate([pick("meta_tokens"), pick("conv_w"), jnp.zeros((5, dloc), F32)], axis=0)

    for slab, tag in ((slab_vec, "vec"), (slab_col, "col")):
        d, nm, nv = _adamw("adamw_small_" + tag, slab(lambda nme: given[nme][0]), slab(lambda nme: g_small[nme]),
                           slab(lambda nme: given[nme][1]), slab(lambda nme: given[nme][2]))
        for out, res in ((delta, d), (new_m, nm), (new_v, nv)):
            if tag == "vec":
                for i, nme in enumerate(vec_names):
                    out[nme] = res[i]
                out["b_gate"] = res[4:6].reshape(-1)
            else:
                out["meta_tokens"] = res[:N_META]
                out["conv_w"] = res[N_META:N_META + 3]
    grad.update(g_small)
    return (loss, grad_x, *[grad[nme] for nme in order], *[delta[nme] for nme in order],
            *[new_m[nme] for nme in order], *[new_v[nme] for nme in order])
```

```python
import math

import jax
import jax.numpy as jnp
from jax import lax
from jax.experimental import pallas as pl
from jax.experimental.pallas import tpu as pltpu

F32 = jnp.float32
BF16 = jnp.bfloat16
N_META = 16
POOL_WINDOWS = (2, 4, 8, 16)
EPS = 1e-6
ADAM_LR, ADAM_B1, ADAM_B2, ADAM_EPS, ADAM_WD, ADAM_STEP = 0.001, 0.9, 0.999, 1e-08, 0.01, 10
LANES = 128
V7X_VMEM_BYTES = 64 * 1024 * 1024
VMEM_LIMIT = V7X_VMEM_BYTES - 8 * 1024 * 1024
MESH = pl.DeviceIdType.MESH
ANY = pl.BlockSpec(memory_space=pl.ANY)
CHIP_FLIPS = ((1, 0), (0, 1), (1, 1))
SMALL_ROWS = 8


def _pick(n, pref):
    best = None
    for t in range(LANES, min(n, pref) + 1, LANES):
        if n % t == 0:
            best = t
    assert best is not None, (n, pref)
    return best


def _params(n_axes=0):
    sem = ("arbitrary",) * n_axes if n_axes else None
    return pltpu.CompilerParams(dimension_semantics=sem, vmem_limit_bytes=VMEM_LIMIT)


_DIMS = {
    "nn": (((1,), (0,)), ((), ())),
    "nt": (((1,), (1,)), ((), ())),
    "tn": (((0,), (0,)), ((), ())),
}


def _matmul(name, mode, a, b, out_sds, grid, a_spec, b_spec, o_spec, nk, res=None, res_spec=None, acc_shape=None, deps=()):
    out_dtype = out_sds.dtype
    in_place = nk > 1 and out_dtype == F32
    use_scratch = nk > 1 and not in_place
    rows = a_spec.block_shape[-2] if mode != "tn" else None
    chunk = _row_tile(rows, 1, 1, 1152) if rows is not None else None
    n_in = 2 + (res is not None) + len(deps)

    def body(*refs):
        a_ref, b_ref = refs[:2]
        r_ref = refs[2] if res is not None else None
        o_ref, *scr = refs[n_in:]
        k = pl.program_id(len(grid) - 1) if nk > 1 else None

        def emit(sl):
            if sl is None:
                part = lax.dot_general(a_ref[...], b_ref[...], _DIMS[mode], preferred_element_type=F32)
                idx = (slice(None), slice(None))
            else:
                part = lax.dot_general(a_ref[sl, :], b_ref[...], _DIMS[mode], preferred_element_type=F32)
                idx = (sl, slice(None))
            if nk == 1:
                if r_ref is not None:
                    part = part + r_ref[idx]
                o_ref[idx] = part.astype(out_dtype)
                return
            acc = scr[0] if use_scratch else o_ref

            @pl.when(k == 0)
            def _():
                first = part
                if r_ref is not None and in_place:
                    first = first + r_ref[idx]
                acc[idx] = first

            @pl.when(k > 0)
            def _():
                acc[idx] += part

            if use_scratch:

                @pl.when(k == nk - 1)
                def _():
                    o_ref[idx] = acc[idx].astype(out_dtype)

        if mode == "tn" or chunk == rows:
            emit(None)
        else:
            for m0 in range(0, rows, chunk):
                emit(pl.ds(m0, chunk))

    ins = [a, b] + ([res] if res is not None else []) + list(deps)
    in_specs = [a_spec, b_spec] + ([res_spec] if res is not None else []) + [ANY] * len(deps)
    scratch = [pltpu.VMEM(acc_shape, F32)] if use_scratch else []
    return pl.pallas_call(
        body, name=name, out_shape=out_sds, grid=grid, in_specs=in_specs, out_specs=o_spec,
        scratch_shapes=scratch, compiler_params=_params(len(grid)),
    )(*ins)


def _nn_sharded(name, a, w4, nseg):
    lp, kdim = a.shape
    s, _, nloc = w4.shape
    segw = s * nloc // nseg
    tn = _pick(math.gcd(nloc, segw), 1536)
    bw, bo = nloc // tn, segw // tn
    return _matmul(
        name, "nn", a, w4, jax.ShapeDtypeStruct((nseg, lp, segw), BF16), (s * bw,),
        pl.BlockSpec((lp, kdim), lambda j: (0, 0)),
        pl.BlockSpec((None, kdim, tn), lambda j: (j // bw, 0, j % bw)),
        pl.BlockSpec((None, lp, tn), lambda j: (j // bo, 0, j % bo)), 1)


def _nt_in_proj(name, dseg, w4, row_tiles=2, to_pref=1024, deps=()):
    nseg, lp, segw = dseg.shape
    s, kdim, nloc = w4.shape
    assert nseg * segw == s * nloc and 2 * nloc == 3 * segw, (dseg.shape, w4.shape)
    half = segw // 2
    to = _pick(kdim, to_pref)
    tm = lp // row_tiles

    def body(full_ref, half_ref, w_ref, *rest):
        o_ref = rest[len(deps)]
        r = pl.program_id(2)

        def contribution(full_first):
            lo, hi = (pl.ds(0, segw), pl.ds(segw, half)) if full_first else (pl.ds(half, segw), pl.ds(0, half))
            return (lax.dot_general(full_ref[...], w_ref[:, lo], _DIMS["nt"], preferred_element_type=F32)
                    + lax.dot_general(half_ref[...], w_ref[:, hi], _DIMS["nt"], preferred_element_type=F32))

        @pl.when(r == 0)
        def _():
            o_ref[...] = contribution(True)

        for ri in range(1, s):

            @pl.when(r == ri)
            def _(ri=ri):
                o_ref[...] += contribution(ri % 2 == 0)

    return pl.pallas_call(
        body, name=name, out_shape=jax.ShapeDtypeStruct((lp, kdim), F32), grid=(row_tiles, kdim // to, s),
        in_specs=[pl.BlockSpec((None, tm, segw), lambda m, j, r: ((3 * r + 1) // 2, m, 0)),
                  pl.BlockSpec((None, tm, half), lambda m, j, r: (1 + 3 * (r // 2), m, r % 2)),
                  pl.BlockSpec((None, to, nloc), lambda m, j, r: (r, j, 0))] + [ANY] * len(deps),
        out_specs=pl.BlockSpec((tm, to), lambda m, j, r: (m, j)), compiler_params=_params(3),
    )(dseg, dseg, w4, *deps)


def _nn_plain(name, a, w, out_dtype, res=None, tn_pref=512, tk_pref=2048, deps=()):
    lp, kdim = a.shape
    n = w.shape[1]
    tn = _pick(n, tn_pref)
    tk = kdim if kdim <= tk_pref else _pick(kdim, tk_pref)
    nk = kdim // tk
    grid = (n // tn, nk) if nk > 1 else (n // tn,)
    if nk > 1:
        a_spec = pl.BlockSpec((lp, tk), lambda j, k: (0, k))
        w_spec = pl.BlockSpec((tk, tn), lambda j, k: (k, j))
        o_spec = pl.BlockSpec((lp, tn), lambda j, k: (0, j))
    else:
        a_spec = pl.BlockSpec((lp, tk), lambda j: (0, 0))
        w_spec = pl.BlockSpec((tk, tn), lambda j: (0, j))
        o_spec = pl.BlockSpec((lp, tn), lambda j: (0, j))
    return _matmul(name, "nn", a, w, jax.ShapeDtypeStruct((lp, n), out_dtype), grid, a_spec, w_spec, o_spec, nk,
                   res=res, res_spec=o_spec if res is not None else None, acc_shape=(lp, tn), deps=deps)


def _nt_plain(name, a, w, tn_pref=512):
    lp, kdim = a.shape
    n = w.shape[0]
    tn = _pick(n, tn_pref)
    return _matmul(
        name, "nt", a, w, jax.ShapeDtypeStruct((lp, n), BF16), (n // tn,),
        pl.BlockSpec((lp, kdim), lambda j: (0, 0)),
        pl.BlockSpec((tn, kdim), lambda j: (j, 0)),
        pl.BlockSpec((lp, tn), lambda j: (0, j)), 1)


def _nt_sharded(name, dseg, w4, to_pref=1024, tr_pref=1536, row_tiles=1, deps=()):
    nseg, lp, segw = dseg.shape
    s, kdim, nloc = w4.shape
    tr = _pick(math.gcd(nloc, segw), tr_pref)
    ba, bw = segw // tr, nloc // tr
    nr = s * bw
    to = _pick(kdim, to_pref)
    tm = lp // row_tiles
    return _matmul(
        name, "nt", dseg, w4, jax.ShapeDtypeStruct((lp, kdim), F32), (row_tiles, kdim // to, nr),
        pl.BlockSpec((None, tm, tr), lambda m, j, r: (r // ba, m, r % ba)),
        pl.BlockSpec((None, to, tr), lambda m, j, r: (r // bw, j, r % bw)),
        pl.BlockSpec((tm, to), lambda m, j, r: (m, j)), nr, deps=deps)


def _nn_rows(name, a, w, res, row_tiles=2, tn_pref=512):
    lp, kdim = a.shape
    n = w.shape[1]
    tn = _pick(n, tn_pref)
    tm = lp // row_tiles
    blk = pl.BlockSpec((tm, tn), lambda i, j: (i, j))
    return _matmul(name, "nn", a, w, jax.ShapeDtypeStruct((lp, n), F32), (row_tiles, n // tn),
                   pl.BlockSpec((tm, kdim), lambda i, j: (i, 0)), pl.BlockSpec((kdim, tn), lambda i, j: (0, j)), blk, 1,
                   res=res, res_spec=blk)


def _tn_plain(name, a, d, tk_pref=1024):
    lp, kdim = a.shape
    n = d.shape[1]
    tk = _pick(kdim, tk_pref)
    return _matmul(
        name, "tn", a, d, jax.ShapeDtypeStruct((kdim, n), BF16), (kdim // tk,),
        pl.BlockSpec((lp, tk), lambda i: (0, i)),
        pl.BlockSpec((lp, n), lambda i: (0, 0)),
        pl.BlockSpec((tk, n), lambda i: (i, 0)), 1)


def _tn_sharded(name, a, dseg, s, part=(0, 1), tk_pref=1024, deps=()):
    lp, kdim = a.shape
    nseg, _, segw = dseg.shape
    nloc = nseg * segw // s
    tn = _pick(math.gcd(nloc, segw), 1536)
    bd, bo = segw // tn, nloc // tn
    kpart = kdim // part[1]
    tk = _pick(kpart, tk_pref)
    i0 = part[0] * (kpart // tk)

    def body(a_ref, d_ref, *rest):
        o_ref, at_ref = rest[len(deps):]

        @pl.when(pl.program_id(1) == 0)
        def _():
            at_ref[...] = a_ref[...].T

        o_ref[...] = jnp.dot(at_ref[...], d_ref[...], preferred_element_type=F32).astype(BF16)

    return pl.pallas_call(
        body, name=name, out_shape=jax.ShapeDtypeStruct((s, kpart, nloc), BF16), grid=(kpart // tk, s * bo),
        in_specs=[pl.BlockSpec((lp, tk), lambda i, j: (0, i0 + i)),
                  pl.BlockSpec((None, lp, tn), lambda i, j: (j // bd, 0, j % bd))] + [ANY] * len(deps),
        out_specs=pl.BlockSpec((None, tk, tn), lambda i, j: (j // bo, i, j % bo)),
        scratch_shapes=[pltpu.VMEM((tk, lp), BF16)], compiler_params=_params(2),
    )(a, dseg, *deps)


def _silu_parts(gt):
    sg = jax.nn.sigmoid(gt)
    return gt * sg, sg * (1.0 + gt * (1.0 - sg))


def _gate_up_swiglu(name, a, w4, dep, tn_pref=256):
    lp, kdim = a.shape
    s, _, nloc = w4.shape
    f = s * nloc // 2
    tn = _pick(nloc, tn_pref)
    bw = nloc // tn
    chunk = _row_tile(lp, 1, 1, 576)

    def body(a_ref, wg_ref, wu_ref, _, gu_ref, act_ref):
        for m0 in range(0, lp, chunk):
            sl = pl.ds(m0, chunk)
            gt = jnp.dot(a_ref[sl, :], wg_ref[...], preferred_element_type=F32)
            up = jnp.dot(a_ref[sl, :], wu_ref[...], preferred_element_type=F32)
            gu_ref[0, sl, :] = gt.astype(BF16)
            gu_ref[1, sl, :] = up.astype(BF16)
            act_ref[sl, :] = (_silu_parts(gt)[0] * up).astype(BF16)

    return pl.pallas_call(
        body, name=name, grid=(f // tn,),
        out_shape=(jax.ShapeDtypeStruct((2, lp, f), BF16), jax.ShapeDtypeStruct((lp, f), BF16)),
        in_specs=[pl.BlockSpec((lp, kdim), lambda j: (0, 0)),
                  pl.BlockSpec((None, kdim, tn), lambda j: (j // bw, 0, j % bw)),
                  pl.BlockSpec((None, kdim, tn), lambda j: (s // 2 + j // bw, 0, j % bw)), ANY],
        out_specs=(pl.BlockSpec((2, lp, tn), lambda j: (0, 0, j)), pl.BlockSpec((lp, tn), lambda j: (0, j))),
        compiler_params=_params(1),
    )(a, w4, w4, dep)


def _dact_swiglu_bwd(name, d, w, gu, tn_pref=512):
    lp, dm = d.shape
    f = w.shape[0]
    tn = _pick(f, tn_pref)
    chunk = _row_tile(lp, 1, 1, 576)

    def body(d_ref, w_ref, g_ref, u_ref, o_ref):
        for m0 in range(0, lp, chunk):
            sl = pl.ds(m0, chunk)
            dact = lax.dot_general(d_ref[sl, :], w_ref[...], _DIMS["nt"], preferred_element_type=F32)
            silu, dsilu = _silu_parts(g_ref[sl, :].astype(F32))
            o_ref[0, sl, :] = (dact * u_ref[sl, :].astype(F32) * dsilu).astype(BF16)
            o_ref[1, sl, :] = (dact * silu).astype(BF16)

    return pl.pallas_call(
        body, name=name, grid=(f // tn,), out_shape=jax.ShapeDtypeStruct((2, lp, f), BF16),
        in_specs=[pl.BlockSpec((lp, dm), lambda j: (0, 0)), pl.BlockSpec((tn, dm), lambda j: (j, 0)),
                  pl.BlockSpec((None, lp, tn), lambda j: (0, 0, j)), pl.BlockSpec((None, lp, tn), lambda j: (1, 0, j))],
        out_specs=pl.BlockSpec((2, lp, tn), lambda j: (0, 0, j)), compiler_params=_params(1),
    )(d, w, gu, gu)


def _pool_fwd(name, pooled, pw):
    lp, dm = pooled.shape
    g, gw, _ = pw.shape
    return _matmul(
        name, "nn", pooled, pw, jax.ShapeDtypeStruct((lp, dm), BF16), (g,),
        pl.BlockSpec((lp, gw), lambda gi: (0, gi)), pl.BlockSpec((None, gw, gw), lambda gi: (gi, 0, 0)),
        pl.BlockSpec((lp, gw), lambda gi: (0, gi)), 1)


def _pool_bwd_act(name, dya, pw, deps=()):
    lp, dm = dya.shape
    g, gw, _ = pw.shape
    return _matmul(
        name, "nt", dya, pw, jax.ShapeDtypeStruct((lp, dm), BF16), (g,),
        pl.BlockSpec((lp, gw), lambda gi: (0, gi)), pl.BlockSpec((None, gw, gw), lambda gi: (gi, 0, 0)),
        pl.BlockSpec((lp, gw), lambda gi: (0, gi)), 1, deps=deps)


def _pool_bwd_w(name, pooled, dya):
    lp, dm = pooled.shape
    g = len(POOL_WINDOWS)
    gw = dm // g
    return _matmul(
        name, "tn", pooled, dya, jax.ShapeDtypeStruct((g, gw, gw), BF16), (g,),
        pl.BlockSpec((lp, gw), lambda gi: (0, gi)), pl.BlockSpec((lp, gw), lambda gi: (0, gi)),
        pl.BlockSpec((None, gw, gw), lambda gi: (gi, 0, 0)), 1)


def _rms_fwd(name, h, g, tm, deps=()):
    lp, dm = h.shape

    def body(h_ref, g_ref, *rest):
        hv = h_ref[...]
        r = lax.rsqrt(jnp.mean(hv * hv, axis=-1, keepdims=True) + EPS)
        rest[-1][...] = (hv * r * g_ref[...]).astype(BF16)

    row = pl.BlockSpec((tm, dm), lambda i: (i, 0))
    return pl.pallas_call(
        body, name=name, out_shape=jax.ShapeDtypeStruct((lp, dm), BF16), grid=(lp // tm,),
        in_specs=[row, pl.BlockSpec((1, dm), lambda i: (0, 0))] + [ANY] * len(deps), out_specs=row, compiler_params=_params(1),
    )(h, g, *deps)


def _rms_fwd_input(name, x2d, tail_rows, g, deps=()):
    seq, dm = x2d.shape
    tm = tail_rows.shape[0]
    nx = seq // tm
    lp = seq + tm

    def body(x_ref, t_ref, g_ref, *rest):
        h_ref, o_ref = rest[len(deps):]

        def emit(hv):
            r = lax.rsqrt(jnp.mean(hv * hv, axis=-1, keepdims=True) + EPS)
            h_ref[...] = hv
            o_ref[...] = (hv * r * g_ref[...]).astype(BF16)

        @pl.when(pl.program_id(0) < nx)
        def _():
            emit(x_ref[...])

        @pl.when(pl.program_id(0) >= nx)
        def _():
            emit(t_ref[...])

    row = pl.BlockSpec((tm, dm), lambda i: (i, 0))
    return pl.pallas_call(
        body, name=name, grid=(lp // tm,),
        out_shape=(jax.ShapeDtypeStruct((lp, dm), F32), jax.ShapeDtypeStruct((lp, dm), BF16)),
        in_specs=[pl.BlockSpec((tm, dm), lambda i: (jnp.minimum(i, nx - 1), 0)), pl.BlockSpec((tm, dm), lambda i: (0, 0)),
                  pl.BlockSpec((1, dm), lambda i: (0, 0))] + [ANY] * len(deps),
        out_specs=(row, row), compiler_params=_params(1),
    )(x2d, tail_rows, g, *deps)


def _rms_bwd(name, dy, h, g, dres, tm, dep):
    lp, dm = h.shape

    def body(dy_ref, h_ref, g_ref, dr_ref, _, dh_ref, dhb_ref, dg_ref):
        hv = h_ref[...]
        r = lax.rsqrt(jnp.mean(hv * hv, axis=-1, keepdims=True) + EPS)
        xhat = hv * r
        dyv = dy_ref[...]
        dxh = dyv * g_ref[...]
        dh = dr_ref[...] + r * (dxh - xhat * jnp.mean(dxh * xhat, axis=-1, keepdims=True))
        dh_ref[...] = dh
        dhb_ref[...] = dh.astype(BF16)

        @pl.when(pl.program_id(0) == 0)
        def _():
            dg_ref[...] = jnp.zeros_like(dg_ref)

        dg_ref[0:1, :] += jnp.sum(dyv * xhat, axis=0, keepdims=True)

    row = pl.BlockSpec((tm, dm), lambda i: (i, 0))
    slab = pl.BlockSpec((SMALL_ROWS, dm), lambda i: (0, 0))
    return pl.pallas_call(
        body, name=name, grid=(lp // tm,),
        out_shape=(jax.ShapeDtypeStruct((lp, dm), F32), jax.ShapeDtypeStruct((lp, dm), BF16),
                   jax.ShapeDtypeStruct((SMALL_ROWS, dm), F32)),
        in_specs=[row, row, pl.BlockSpec((1, dm), lambda i: (0, 0)), row, ANY], out_specs=(row, row, slab),
        compiler_params=_params(1),
    )(dy, h, g, dres, dep)


def _rms_bwd_input(name, dy, h, g, dres, tm, seq, dep):
    lp, dm = h.shape
    nx = seq // tm

    def body(dy_ref, h_ref, g_ref, dr_ref, _, dx_ref, dt_ref, dg_ref):
        i = pl.program_id(0)
        hv = h_ref[...]
        r = lax.rsqrt(jnp.mean(hv * hv, axis=-1, keepdims=True) + EPS)
        xhat = hv * r
        dyv = dy_ref[...]
        dxh = dyv * g_ref[...]
        dh = dr_ref[...] + r * (dxh - xhat * jnp.mean(dxh * xhat, axis=-1, keepdims=True))

        @pl.when(i < nx)
        def _():
            dx_ref[...] = dh

        @pl.when(i >= nx)
        def _():
            dt_ref[...] = dh

        @pl.when(i == 0)
        def _():
            dg_ref[...] = jnp.zeros_like(dg_ref)

        dg_ref[0:1, :] += jnp.sum(dyv * xhat, axis=0, keepdims=True)

    row = pl.BlockSpec((tm, dm), lambda i: (i, 0))
    slab = pl.BlockSpec((SMALL_ROWS, dm), lambda i: (0, 0))
    return pl.pallas_call(
        body, name=name, grid=(lp // tm,),
        out_shape=(jax.ShapeDtypeStruct((seq, dm), F32), jax.ShapeDtypeStruct((tm, dm), F32),
                   jax.ShapeDtypeStruct((SMALL_ROWS, dm), F32)),
        in_specs=[row, row, pl.BlockSpec((1, dm), lambda i: (0, 0)), row, ANY],
        out_specs=(pl.BlockSpec((tm, dm), lambda i: (jnp.minimum(i, nx - 1), 0)), pl.BlockSpec((tm, dm), lambda i: (0, 0)), slab),
        compiler_params=_params(1),
    )(dy, h, g, dres, dep)


def _gate_mix(name, proj, b_gate2, ya, pool_scale, yb, tm):
    _, lp, dm = proj.shape

    def body(ga_ref, gr_ref, b_ref, ya_ref, ps_ref, yb_ref, o_ref):
        g_a = jax.nn.sigmoid(ga_ref[...].astype(F32) + b_ref[0:1, :])
        g_b = jax.nn.sigmoid(gr_ref[...].astype(F32) + b_ref[1:2, :])
        y_a = ya_ref[...].astype(F32) * ps_ref[...]
        o_ref[...] = (g_a * y_a + g_b * yb_ref[...].astype(F32)).astype(BF16)

    row = pl.BlockSpec((tm, dm), lambda i: (i, 0))
    return pl.pallas_call(
        body, name=name, out_shape=jax.ShapeDtypeStruct((lp, dm), BF16), grid=(lp // tm,),
        in_specs=[pl.BlockSpec((None, tm, dm), lambda i: (4, i, 0)), pl.BlockSpec((None, tm, dm), lambda i: (5, i, 0)),
                  pl.BlockSpec((2, dm), lambda i: (0, 0)), row, pl.BlockSpec((1, dm), lambda i: (0, 0)), row],
        out_specs=row, compiler_params=_params(1),
    )(proj, proj, b_gate2, ya, pool_scale, yb)


def _gate_bwd(name, dmix, proj, b_gate2, ya, pool_scale, yb, tm):
    _, lp, dm = proj.shape

    def body(dm_ref, ga_ref, gr_ref, b_ref, ya_ref, ps_ref, yb_ref, dp_ref, dyb_ref, dya_ref, db_ref, dps_ref):
        dmx = dm_ref[...].astype(F32)
        g_a = jax.nn.sigmoid(ga_ref[...].astype(F32) + b_ref[0:1, :])
        g_b = jax.nn.sigmoid(gr_ref[...].astype(F32) + b_ref[1:2, :])
        ya_pre = ya_ref[...].astype(F32)
        ybv = yb_ref[...].astype(F32)
        ps = ps_ref[...]
        dga = dmx * (ya_pre * ps) * (g_a * (1.0 - g_a))
        dgr = dmx * ybv * (g_b * (1.0 - g_b))
        dp_ref[0] = dga.astype(BF16)
        dp_ref[1] = dgr.astype(BF16)
        dyb_ref[...] = (dmx * g_b).astype(BF16)
        dya_ref[...] = (dmx * g_a * ps).astype(BF16)

        @pl.when(pl.program_id(0) == 0)
        def _():
            db_ref[...] = jnp.zeros_like(db_ref)
            dps_ref[...] = jnp.zeros_like(dps_ref)

        db_ref[0:1, :] += jnp.sum(dga, axis=0, keepdims=True)
        db_ref[1:2, :] += jnp.sum(dgr, axis=0, keepdims=True)
        dps_ref[0:1, :] += jnp.sum(dmx * g_a * ya_pre, axis=0, keepdims=True)

    row = pl.BlockSpec((tm, dm), lambda i: (i, 0))
    one = pl.BlockSpec((1, dm), lambda i: (0, 0))
    slab = pl.BlockSpec((SMALL_ROWS, dm), lambda i: (0, 0))
    return pl.pallas_call(
        body, name=name, grid=(lp // tm,),
        out_shape=(jax.ShapeDtypeStruct((6, lp, dm), BF16), jax.ShapeDtypeStruct((lp, dm), BF16),
                   jax.ShapeDtypeStruct((lp, dm), BF16), jax.ShapeDtypeStruct((SMALL_ROWS, dm), F32),
                   jax.ShapeDtypeStruct((SMALL_ROWS, dm), F32)),
        in_specs=[row, pl.BlockSpec((None, tm, dm), lambda i: (4, i, 0)), pl.BlockSpec((None, tm, dm), lambda i: (5, i, 0)),
                  pl.BlockSpec((2, dm), lambda i: (0, 0)), row, one, row],
        out_specs=(pl.BlockSpec((2, tm, dm), lambda i: (2, i, 0)), row, row, slab, slab),
        compiler_params=_params(1),
    )(dmix, proj, proj, b_gate2, ya, pool_scale, yb)


def _final_loss(name, h2, g3, target, tm):
    lp, dm = h2.shape
    nx = target.shape[0] // tm

    def body(h_ref, g_ref, t_ref, dh_ref, dhb_ref, ls_ref, dg_ref):
        i = pl.program_id(0)

        @pl.when(i == 0)
        def _():
            ls_ref[...] = jnp.zeros_like(ls_ref)
            dg_ref[...] = jnp.zeros_like(dg_ref)

        @pl.when(i < nx)
        def _():
            hv = h_ref[...]
            gv = g_ref[...]
            r = lax.rsqrt(jnp.mean(hv * hv, axis=-1, keepdims=True) + EPS)
            xhat = hv * r
            err = xhat * gv - t_ref[...]
            dout = err * (1.0 / dm)
            dxh = dout * gv
            dh = r * (dxh - xhat * jnp.mean(dxh * xhat, axis=-1, keepdims=True))
            dh_ref[...] = dh
            dhb_ref[...] = dh.astype(BF16)
            ls_ref[0:1, :] += jnp.sum(err * err, axis=0, keepdims=True)
            dg_ref[0:1, :] += jnp.sum(dout * xhat, axis=0, keepdims=True)

        @pl.when(i >= nx)
        def _():
            dh_ref[...] = jnp.zeros_like(dh_ref)
            dhb_ref[...] = jnp.zeros_like(dhb_ref)

    row = pl.BlockSpec((tm, dm), lambda i: (i, 0))
    slab = pl.BlockSpec((SMALL_ROWS, dm), lambda i: (0, 0))
    return pl.pallas_call(
        body, name=name, grid=(lp // tm,),
        out_shape=(jax.ShapeDtypeStruct((lp, dm), F32), jax.ShapeDtypeStruct((lp, dm), BF16),
                   jax.ShapeDtypeStruct((SMALL_ROWS, dm), F32), jax.ShapeDtypeStruct((SMALL_ROWS, dm), F32)),
        in_specs=[row, pl.BlockSpec((1, dm), lambda i: (0, 0)), pl.BlockSpec((tm, dm), lambda i: (jnp.minimum(i, nx - 1), 0))],
        out_specs=(row, row, slab, slab), compiler_params=_params(1),
    )(h2, g3, target)


def _shift(v, k):
    return pltpu.roll(v, k % v.shape[0], axis=0)


def _window_sum(v, group, sign):
    s2 = v + _shift(v, sign * 1)
    s4 = s2 + _shift(s2, sign * 2)
    s8 = s4 + _shift(s4, sign * 4)
    s16 = s8 + _shift(s8, sign * 8)
    return jnp.where(group == 0, s2, jnp.where(group == 1, s4, jnp.where(group == 2, s8, s16)))


def _pool_count(lp, group):
    row = lax.broadcasted_iota(jnp.int32, (lp, 1), 0)
    window = jnp.left_shift(2, group).astype(F32)
    meta_pos = (row - (lp - N_META) + 1).astype(F32)
    return jnp.where(row >= lp - N_META, jnp.minimum(meta_pos, window), window)


def _mixer_fwd(name, proj, conv_w, tc, dep):
    _, lp, dm = proj.shape
    per_group = dm // len(POOL_WINDOWS) // tc

    def body(u_ref, gb_ref, gc_ref, v_ref, cw_ref, _, p_ref, z_ref):
        group = pl.program_id(0) // per_group
        u = u_ref[...].astype(F32)
        p_ref[...] = (_window_sum(u, group, 1) / _pool_count(lp, group) - u).astype(BF16)
        cv = gc_ref[...].astype(F32) * v_ref[...].astype(F32)
        conv = cw_ref[0:1, :] * _shift(cv, 2) + cw_ref[1:2, :] * _shift(cv, 1) + cw_ref[2:3, :] * cv
        z_ref[...] = (gb_ref[...].astype(F32) * conv).astype(BF16)

    def seg(s):
        return pl.BlockSpec((None, lp, tc), lambda j: (s, 0, j))

    col = pl.BlockSpec((lp, tc), lambda j: (0, j))
    return pl.pallas_call(
        body, name=name, grid=(dm // tc,),
        out_shape=(jax.ShapeDtypeStruct((lp, dm), BF16), jax.ShapeDtypeStruct((lp, dm), BF16)),
        in_specs=[seg(0), seg(1), seg(2), seg(3), pl.BlockSpec((3, tc), lambda j: (0, j)), ANY],
        out_specs=(col, col), compiler_params=_params(1),
    )(proj, proj, proj, proj, conv_w, dep)


def _mixer_bwd(name, dz, dpooled, proj, conv_w, dproj, tc, dep):
    _, lp, dm = proj.shape
    per_group = dm // len(POOL_WINDOWS) // tc

    def body(dz_ref, dp_ref, gb_ref, gc_ref, v_ref, cw_ref, _, __, o_ref, dcw_ref):
        group = pl.program_id(0) // per_group
        dzv = dz_ref[...].astype(F32)
        gb = gb_ref[...].astype(F32)
        gc = gc_ref[...].astype(F32)
        vv = v_ref[...].astype(F32)
        cv = gc * vv
        c1 = _shift(cv, 1)
        c2 = _shift(cv, 2)
        w0, w1, w2 = cw_ref[0:1, :], cw_ref[1:2, :], cw_ref[2:3, :]
        o_ref[1] = (dzv * (w0 * c2 + w1 * c1 + w2 * cv)).astype(BF16)
        dconv = dzv * gb
        dcw_ref[...] = jnp.zeros_like(dcw_ref)
        dcw_ref[0:1, :] = jnp.sum(dconv * c2, axis=0, keepdims=True)
        dcw_ref[1:2, :] = jnp.sum(dconv * c1, axis=0, keepdims=True)
        dcw_ref[2:3, :] = jnp.sum(dconv * cv, axis=0, keepdims=True)
        dcv = w0 * _shift(dconv, -2) + w1 * _shift(dconv, -1) + w2 * dconv
        o_ref[2] = (dcv * vv).astype(BF16)
        o_ref[3] = (dcv * gc).astype(BF16)
        dpv = dp_ref[...].astype(F32)
        o_ref[0] = (_window_sum(dpv / _pool_count(lp, group), group, -1) - dpv).astype(BF16)

    def seg(s):
        return pl.BlockSpec((None, lp, tc), lambda j: (s, 0, j))

    col = pl.BlockSpec((lp, tc), lambda j: (0, j))
    return pl.pallas_call(
        body, name=name, grid=(dm // tc,),
        out_shape=(jax.ShapeDtypeStruct(dproj.shape, BF16), jax.ShapeDtypeStruct((SMALL_ROWS, dm), F32)),
        in_specs=[col, col, seg(1), seg(2), seg(3), pl.BlockSpec((3, tc), lambda j: (0, j)), ANY, ANY],
        out_specs=(pl.BlockSpec((4, lp, tc), lambda j: (0, 0, j)), pl.BlockSpec((SMALL_ROWS, tc), lambda j: (0, j))),
        input_output_aliases={6: 0}, compiler_params=_params(1),
    )(dz, dpooled, proj, proj, proj, conv_w, dproj, dep)


def _row_tile(r, c, bytes_per_row_elem=4, budget=2 * 1024 * 1024):
    best = None
    for t in range(16, r + 1, 16):
        if r % t == 0 and t * c * bytes_per_row_elem <= budget:
            best = t
    return best if best is not None else r


def _pair_add(name, g4, recv, core):
    s, r, c = g4.shape
    h = r // 2
    tr = _row_tile(h, c, budget=6 * 1024 * 1024)
    nb = h // tr

    def body(core_ref, g_ref, r_ref, o_ref):
        o_ref[...] = (g_ref[...].astype(F32) + r_ref[...].astype(F32)).astype(BF16)

    grid_spec = pltpu.PrefetchScalarGridSpec(
        num_scalar_prefetch=1, grid=(s, nb),
        in_specs=[pl.BlockSpec((None, tr, c), lambda si, j, core_ref: (si, core_ref[0] * nb + j, 0)),
                  pl.BlockSpec((None, tr, c), lambda si, j, core_ref: (si, j, 0))],
        out_specs=pl.BlockSpec((None, tr, c), lambda si, j, core_ref: (si, j, 0)))
    return pl.pallas_call(
        body, name=name, out_shape=jax.ShapeDtypeStruct((s, h, c), BF16), grid_spec=grid_spec,
        compiler_params=_params(2),
    )(core, g4, recv)


def _chip_sum(name, parts, recv, chip):
    _, h, c = parts.shape
    tr = _row_tile(h, c)

    def body(chip_ref, p_ref, r_ref, o_ref):
        acc = p_ref[...].astype(F32)
        for i in range(len(CHIP_FLIPS)):
            acc = acc + r_ref[i].astype(F32)
        o_ref[...] = acc

    grid_spec = pltpu.PrefetchScalarGridSpec(
        num_scalar_prefetch=1, grid=(h // tr,),
        in_specs=[pl.BlockSpec((None, tr, c), lambda j, chip_ref: (chip_ref[0], j, 0)),
                  pl.BlockSpec((len(CHIP_FLIPS), tr, c), lambda j, chip_ref: (0, j, 0))],
        out_specs=pl.BlockSpec((tr, c), lambda j, chip_ref: (j, 0)))
    return pl.pallas_call(
        body, name=name, out_shape=jax.ShapeDtypeStruct((h, c), F32), grid_spec=grid_spec, compiler_params=_params(1),
    )(chip, parts, recv)


def _adam_update(w, gv, m, v):
    c1 = 1.0 - ADAM_B1 ** ADAM_STEP
    c2 = 1.0 - ADAM_B2 ** ADAM_STEP
    nm = ADAM_B1 * m + (1.0 - ADAM_B1) * gv
    nv = ADAM_B2 * v + (1.0 - ADAM_B2) * (gv * gv)
    return -ADAM_LR * ((nm / c1) / (jnp.sqrt(nv / c2) + ADAM_EPS) + ADAM_WD * w), nm, nv


def _adamw_halves(name, w, g_own, g_sib, m, v, core, part=(0, 1), prev=None):
    r, c = w.shape
    rp = r // part[1]
    h = rp // 2
    tr = _row_tile(h, c, budget=2 * 1024 * 1024)
    nbh = h // tr
    j0 = part[0] * 2 * nbh
    n_prev = 0 if prev is None else 4

    def body(core_ref, w_ref, go_ref, gs_ref, m_ref, v_ref, *rest):
        g_ref, d_ref, nm_ref, nv_ref = rest[n_prev:]
        mine = (pl.program_id(0) // nbh) == core_ref[0]
        gv = jnp.where(mine, go_ref[...], gs_ref[...])
        g_ref[...] = gv
        d_ref[...], nm_ref[...], nv_ref[...] = _adam_update(w_ref[...], gv, m_ref[...], v_ref[...])

    def blk(fn):
        return pl.BlockSpec((tr, c), fn)

    full = blk(lambda j, core_ref: (j0 + j, 0))
    own = blk(lambda j, core_ref: (jnp.clip(j - core_ref[0] * nbh, 0, nbh - 1), 0))
    sib = blk(lambda j, core_ref: (jnp.clip(j - (1 - core_ref[0]) * nbh, 0, nbh - 1), 0))
    grid_spec = pltpu.PrefetchScalarGridSpec(
        num_scalar_prefetch=1, grid=(2 * nbh,), in_specs=[full, own, sib, full, full] + [ANY] * n_prev, out_specs=(full,) * 4)
    sds = jax.ShapeDtypeStruct((r, c), F32)
    return pl.pallas_call(
        body, name=name, out_shape=(sds,) * 4, grid_spec=grid_spec, compiler_params=_params(1),
        input_output_aliases={6 + i: i for i in range(n_prev)},
    )(core, w, g_own, g_sib, m, v, *(prev or ()))


def _adamw(name, w, g, m, v):
    r, c = w.shape

    def body(w_ref, g_ref, m_ref, v_ref, d_ref, nm_ref, nv_ref):
        d_ref[...], nm_ref[...], nv_ref[...] = _adam_update(w_ref[...], g_ref[...], m_ref[...], v_ref[...])

    blk = pl.BlockSpec((r, c), lambda j: (0, 0))
    sds = jax.ShapeDtypeStruct((r, c), F32)
    return pl.pallas_call(
        body, name=name, out_shape=(sds, sds, sds), grid=(1,), in_specs=[blk] * 4, out_specs=(blk,) * 3,
        compiler_params=_params(1),
    )(w, g, m, v)


def _cast_into_slot(name, w, chip, dtype, deps=()):
    r, c = w.shape
    tr = _row_tile(r, c)

    def body(chip_ref, w_ref, *rest):
        rest[-1][...] = w_ref[...].astype(dtype)

    grid_spec = pltpu.PrefetchScalarGridSpec(
        num_scalar_prefetch=1, grid=(r // tr,),
        in_specs=[pl.BlockSpec((tr, c), lambda j, chip_ref: (j, 0))] + [ANY] * len(deps),
        out_specs=pl.BlockSpec((None, tr, c), lambda j, chip_ref: (chip_ref[0], j, 0)))
    return pl.pallas_call(
        body, name=name, out_shape=jax.ShapeDtypeStruct((4, r, c), dtype), grid_spec=grid_spec, compiler_params=_params(1),
    )(chip, w, *deps)


def _place():
    return lax.axis_index("x"), lax.axis_index("y"), lax.axis_index("c")


def _chip_of(x, y, flip):
    px, py = x ^ flip[0], y ^ flip[1]
    return px, py, 2 * px + py


def _half(ref, which):
    rows = ref.shape[0] // 2
    return ref.at[pl.ds(which * rows, rows)]


HBM = pl.BlockSpec(memory_space=pltpu.HBM)
SEM = pl.BlockSpec(memory_space=pltpu.SEMAPHORE)
SPLIT_COPY = pltpu.CompilerParams(has_side_effects=pltpu.SideEffectType.DATAFLOW_SIDE_EFFECTING)


def _in_hbm(arrays):
    return [pltpu.with_memory_space_constraint(t, pltpu.HBM) for t in arrays]


TOKEN = jax.ShapeDtypeStruct((SMALL_ROWS, LANES), F32)
TOKEN_SPEC = pl.BlockSpec(memory_space=pltpu.VMEM)


NEIGHBOUR_FLIPS = CHIP_FLIPS[:2]


def _relay_chips(x, y, c):
    fx, fy = x ^ c, y ^ (1 - c)
    return (fx, fy), 2 * fx + fy, 2 * (1 - x) + (1 - y)


def _ag_start(name, slabs, deps=()):
    n = len(slabs)
    nn = len(NEIGHBOUR_FLIPS)

    def body(*refs):
        no = n + len(deps)
        ssem, rsem = refs[no], refs[no + 1]
        outs = refs[no + 2:no + 2 + n]
        token = refs[no + 2 + n]
        token[...] = jnp.zeros_like(token)
        x, y, c = _place()
        k = 2 * x + y
        for a in range(n):
            for j, flip in enumerate(NEIGHBOUR_FLIPS):
                px, py, _ = _chip_of(x, y, flip)
                mine = _half(outs[a].at[k], c)
                pltpu.make_async_remote_copy(src_ref=mine, dst_ref=mine, send_sem=ssem.at[a * nn + j],
                                             recv_sem=rsem.at[a * nn + j], device_id=(px, py, c), device_id_type=MESH).start()

    sem = pltpu.SemaphoreType.DMA((nn * n,))
    res = pl.pallas_call(
        body, name=name, out_shape=(sem, sem) + tuple(pltpu.HBM(t.shape, t.dtype) for t in slabs) + (TOKEN,),
        in_specs=[HBM] * n + [ANY] * len(deps), out_specs=tuple([SEM, SEM] + [HBM] * n + [TOKEN_SPEC]),
        input_output_aliases={a: 2 + a for a in range(n)}, compiler_params=SPLIT_COPY,
    )(*_in_hbm(slabs), *deps)
    return (res[0], res[1]), list(res[2:2 + n]), res[2 + n]


def _ag_relay(name, slabs, sems, after, then_start=()):
    n = len(slabs)
    m = len(then_start)
    nn = len(NEIGHBOUR_FLIPS)

    def body(*refs):
        no = n + 2 + m + len(after)
        ins = refs[:n]
        ssem, rsem = refs[n], refs[n + 1]
        r_s, r_r, p_s, p_r = refs[no:no + 4]
        x, y, c = _place()
        k = 2 * x + y
        (fx, fy), _, _ = _relay_chips(x, y, c)
        for a in range(n):
            for j, flip in enumerate(NEIGHBOUR_FLIPS):
                _, _, kj = _chip_of(x, y, flip)
                landed = _half(ins[a].at[kj], c)
                cp = pltpu.make_async_remote_copy(
                    src_ref=_half(ins[a].at[k], c), dst_ref=landed, send_sem=ssem.at[a * nn + j],
                    recv_sem=rsem.at[a * nn + j], device_id=(x, y, c), device_id_type=MESH)
                cp.wait_send()
                cp.wait_recv()
        for a in range(n):
            near = _half(ins[a].at[2 * (x ^ (1 - c)) + (y ^ c)], c)
            pltpu.make_async_remote_copy(src_ref=near, dst_ref=near, send_sem=r_s.at[a], recv_sem=r_r.at[a],
                                         device_id=(fx, fy, c), device_id_type=MESH).start()
            for j, flip in enumerate(NEIGHBOUR_FLIPS):
                _, _, kj = _chip_of(x, y, flip)
                landed = _half(ins[a].at[kj], c)
                pltpu.make_async_remote_copy(src_ref=landed, dst_ref=landed, send_sem=p_s.at[a * nn + j],
                                             recv_sem=p_r.at[a * nn + j], device_id=(x, y, 1 - c), device_id_type=MESH).start()
        if m:
            d_s, d_r = refs[no + 4 + n], refs[no + 5 + n]
            nxt = refs[no + 6 + n:]
            for a in range(m):
                for j, flip in enumerate(NEIGHBOUR_FLIPS):
                    px, py, _ = _chip_of(x, y, flip)
                    mine = _half(nxt[a].at[k], c)
                    pltpu.make_async_remote_copy(src_ref=mine, dst_ref=mine, send_sem=d_s.at[a * nn + j],
                                                 recv_sem=d_r.at[a * nn + j], device_id=(px, py, c), device_id_type=MESH).start()

    rsem_t = pltpu.SemaphoreType.DMA((n,))
    psem_t = pltpu.SemaphoreType.DMA((nn * n,))
    out_shape = (rsem_t, rsem_t, psem_t, psem_t) + tuple(pltpu.HBM(t.shape, t.dtype) for t in slabs)
    out_specs = [SEM] * 4 + [HBM] * n
    aliases = {a: 4 + a for a in range(n)}
    if m:
        dsem_t = pltpu.SemaphoreType.DMA((nn * m,))
        out_shape += (dsem_t, dsem_t) + tuple(pltpu.HBM(t.shape, t.dtype) for t in then_start)
        out_specs += [SEM, SEM] + [HBM] * m
        aliases.update({n + 2 + a: 4 + n + 2 + a for a in range(m)})
    res = pl.pallas_call(
        body, name=name, out_shape=out_shape, in_specs=[HBM] * n + [SEM, SEM] + [HBM] * m + [ANY] * len(after),
        out_specs=tuple(out_specs), input_output_aliases=aliases, compiler_params=SPLIT_COPY,
    )(*slabs, sems[0], sems[1], *_in_hbm(list(then_start)), *after)
    if not m:
        return tuple(res[:4]), list(res[4:])
    return (tuple(res[:4]), list(res[4:4 + n])), ((res[4 + n], res[5 + n]), list(res[6 + n:]))


def _wait_passes(ins, p_s, p_r, x, y, c):
    nn = len(NEIGHBOUR_FLIPS)
    for a in range(len(ins)):
        for j, flip in enumerate(NEIGHBOUR_FLIPS):
            _, _, kj = _chip_of(x, y, flip)
            cp = pltpu.make_async_remote_copy(
                src_ref=_half(ins[a].at[kj], c), dst_ref=_half(ins[a].at[kj], 1 - c), send_sem=p_s.at[a * nn + j],
                recv_sem=p_r.at[a * nn + j], device_id=(x, y, c), device_id_type=MESH)
            cp.wait_send()
            cp.wait_recv()


def _ag_relay_wait(name, slabs, sems, after):
    n = len(slabs)
    ns = len(sems)

    def body(*refs):
        no = n + ns + len(after)
        ins = refs[:n]
        r_s, r_r = refs[n], refs[n + 1]
        f_s, f_r = refs[no], refs[no + 1]
        x, y, c = _place()
        _, _, kd = _relay_chips(x, y, c)
        for a in range(n):
            near = _half(ins[a].at[2 * (x ^ (1 - c)) + (y ^ c)], c)
            cp = pltpu.make_async_remote_copy(src_ref=near, dst_ref=_half(ins[a].at[kd], c), send_sem=r_s.at[a],
                                              recv_sem=r_r.at[a], device_id=(x, y, c), device_id_type=MESH)
            cp.wait_send()
            cp.wait_recv()
        if ns == 4:
            _wait_passes(ins, refs[n + 2], refs[n + 3], x, y, c)
        for a in range(n):
            diag = _half(ins[a].at[kd], c)
            pltpu.make_async_remote_copy(src_ref=diag, dst_ref=diag, send_sem=f_s.at[a], recv_sem=f_r.at[a],
                                         device_id=(x, y, 1 - c), device_id_type=MESH).start()

    sem = pltpu.SemaphoreType.DMA((n,))
    res = pl.pallas_call(
        body, name=name, out_shape=(sem, sem) + tuple(pltpu.HBM(t.shape, t.dtype) for t in slabs),
        in_specs=[HBM] * n + [SEM] * ns + [ANY] * len(after), out_specs=tuple([SEM, SEM] + [HBM] * n),
        input_output_aliases={a: 2 + a for a in range(n)}, compiler_params=SPLIT_COPY,
    )(*slabs, *sems, *after)
    return (res[0], res[1]), list(res[2:])


def _ag_final_wait(name, slabs, sems, after):
    n = len(slabs)

    def body(*refs):
        ins = refs[:n]
        f_s, f_r = refs[n], refs[n + 1]
        x, y, c = _place()
        _, _, kd = _relay_chips(x, y, c)
        for a in range(n):
            cp = pltpu.make_async_remote_copy(
                src_ref=_half(ins[a].at[kd], c), dst_ref=_half(ins[a].at[kd], 1 - c), send_sem=f_s.at[a], recv_sem=f_r.at[a],
                device_id=(x, y, c), device_id_type=MESH)
            cp.wait_send()
            cp.wait_recv()

    return pl.pallas_call(
        body, name=name, out_shape=tuple(pltpu.HBM(t.shape, t.dtype) for t in slabs),
        in_specs=[HBM] * n + [SEM, SEM] + [ANY] * len(after), out_specs=tuple([HBM] * n),
        input_output_aliases={a: a for a in range(n)}, compiler_params=SPLIT_COPY,
    )(*slabs, sems[0], sems[1], *after)


def _sibling_part(ref, c, halves):
    if not halves:
        return ref
    h = ref.shape[1] // 2
    return ref.at[:, pl.ds((1 - c) * h, h)]


def _swap_start(name, grads, halves=True, deps=()):
    n = len(grads)

    def body(*refs):
        no = 2 * n + len(deps)
        ssem, rsem = refs[no], refs[no + 1]
        src, land = refs[no + 2:no + n + 2], refs[no + n + 2:no + 2 * n + 2]
        token = refs[no + 2 * n + 2]
        token[...] = jnp.zeros_like(token)
        x, y, c = _place()
        for a in range(n):
            pltpu.make_async_remote_copy(
                src_ref=_sibling_part(src[a], c, halves), dst_ref=land[a], send_sem=ssem.at[a], recv_sem=rsem.at[a],
                device_id=(x, y, 1 - c), device_id_type=MESH).start()

    zones = [lax.empty((g.shape[0], g.shape[1] // 2, g.shape[2]) if halves else g.shape, g.dtype) for g in grads]
    sem = pltpu.SemaphoreType.DMA((n,))
    res = pl.pallas_call(
        body, name=name,
        out_shape=(sem, sem) + tuple(pltpu.HBM(t.shape, t.dtype) for t in list(grads) + zones) + (TOKEN,),
        in_specs=[HBM] * (2 * n) + [ANY] * len(deps), out_specs=tuple([SEM, SEM] + [HBM] * (2 * n) + [TOKEN_SPEC]),
        input_output_aliases={i: 2 + i for i in range(2 * n)}, compiler_params=SPLIT_COPY,
    )(*_in_hbm(list(grads) + zones), *deps)
    return (res[0], res[1], list(res[2:2 + n]), list(res[2 + n:2 + 2 * n])), res[2 + 2 * n]


def _swap_wait(name, ssem, rsem, grads, zones, after, halves=True):
    n = len(grads)

    def body(*refs):
        src, land = refs[:n], refs[n:2 * n]
        ss, rs = refs[2 * n], refs[2 * n + 1]
        x, y, c = _place()
        for a in range(n):
            cp = pltpu.make_async_remote_copy(
                src_ref=_sibling_part(src[a], c, halves), dst_ref=land[a], send_sem=ss.at[a], recv_sem=rs.at[a],
                device_id=(x, y, c), device_id_type=MESH)
            cp.wait_send()
            cp.wait_recv()

    res = pl.pallas_call(
        body, name=name, out_shape=tuple(pltpu.HBM(t.shape, t.dtype) for t in list(grads) + list(zones)),
        in_specs=[HBM] * (2 * n) + [SEM, SEM] + [ANY] * len(after), out_specs=tuple([HBM] * (2 * n)),
        input_output_aliases={i: i for i in range(2 * n)}, compiler_params=SPLIT_COPY,
    )(*grads, *zones, ssem, rsem, *after)
    return list(res[:n]), list(res[n:])


def _scatter_start(name, parts):
    n = len(parts)
    nf = len(CHIP_FLIPS)

    def body(*refs):
        ssem, rsem = refs[2 * n], refs[2 * n + 1]
        src, land = refs[2 * n + 2:3 * n + 2], refs[3 * n + 2:4 * n + 2]
        token = refs[4 * n + 2]
        token[...] = jnp.zeros_like(token)
        x, y, c = _place()
        for a in range(n):
            for j, flip in enumerate(CHIP_FLIPS):
                px, py, kj = _chip_of(x, y, flip)
                pltpu.make_async_remote_copy(
                    src_ref=src[a].at[kj], dst_ref=land[a].at[j], send_sem=ssem.at[a * nf + j], recv_sem=rsem.at[a * nf + j],
                    device_id=(px, py, c), device_id_type=MESH).start()

    zones = [lax.empty((nf,) + p.shape[1:], p.dtype) for p in parts]
    sem = pltpu.SemaphoreType.DMA((nf * n,))
    res = pl.pallas_call(
        body, name=name,
        out_shape=(sem, sem) + tuple(pltpu.HBM(t.shape, t.dtype) for t in list(parts) + zones)
        + (jax.ShapeDtypeStruct((SMALL_ROWS, LANES), F32),),
        in_specs=[HBM] * (2 * n),
        out_specs=tuple([SEM, SEM] + [HBM] * (2 * n) + [pl.BlockSpec(memory_space=pltpu.VMEM)]),
        input_output_aliases={i: 2 + i for i in range(2 * n)}, compiler_params=SPLIT_COPY,
    )(*_in_hbm(list(parts) + zones))
    return (res[0], res[1], list(res[2:2 + n]), list(res[2 + n:2 + 2 * n])), res[2 + 2 * n]


def _scatter_wait(name, ssem, rsem, parts, zones, after):
    n = len(parts)
    nf = len(CHIP_FLIPS)

    def body(*refs):
        src, land = refs[:n], refs[n:2 * n]
        ss, rs = refs[2 * n], refs[2 * n + 1]
        x, y, c = _place()
        for a in range(n):
            for j, flip in enumerate(CHIP_FLIPS):
                _, _, kj = _chip_of(x, y, flip)
                cp = pltpu.make_async_remote_copy(
                    src_ref=src[a].at[kj], dst_ref=land[a].at[j], send_sem=ss.at[a * nf + j], recv_sem=rs.at[a * nf + j],
                    device_id=(x, y, c), device_id_type=MESH)
                cp.wait_send()
                cp.wait_recv()

    res = pl.pallas_call(
        body, name=name, out_shape=tuple(pltpu.HBM(t.shape, t.dtype) for t in list(parts) + list(zones)),
        in_specs=[HBM] * (2 * n) + [SEM, SEM] + [ANY] * len(after), out_specs=tuple([HBM] * (2 * n)),
        input_output_aliases={i: i for i in range(2 * n)}, compiler_params=SPLIT_COPY,
    )(*parts, *zones, ssem, rsem, *after)
    return list(res[:n]), list(res[n:])


N_PEERS = 7


def _peer(x, y, c, mask):
    px, py, pc = x ^ ((mask >> 2) & 1), y ^ ((mask >> 1) & 1), c ^ (mask & 1)
    return (px, py, pc), 4 * px + 2 * py + pc


def _reduce_start(vec, deps):
    nd = len(deps)

    def body(*refs):
        ssem, rsem, src, land, token = refs[2 + nd:]
        token[...] = jnp.zeros_like(token)
        x, y, c = _place()
        me = 4 * x + 2 * y + c
        for mask in range(1, N_PEERS + 1):
            to, _ = _peer(x, y, c, mask)
            pltpu.make_async_remote_copy(src_ref=src, dst_ref=land.at[me], send_sem=ssem.at[mask - 1],
                                         recv_sem=rsem.at[mask - 1], device_id=to, device_id_type=MESH).start()

    zone = lax.empty((N_PEERS + 1,) + vec.shape, vec.dtype)
    sem = pltpu.SemaphoreType.DMA((N_PEERS,))
    res = pl.pallas_call(
        body, name="reduce_start",
        out_shape=(sem, sem, pltpu.HBM(vec.shape, vec.dtype), pltpu.HBM(zone.shape, zone.dtype), TOKEN),
        in_specs=[HBM, HBM] + [ANY] * nd, out_specs=(SEM, SEM, HBM, HBM, TOKEN_SPEC),
        input_output_aliases={0: 2, 1: 3}, compiler_params=SPLIT_COPY,
    )(*_in_hbm([vec, zone]), *deps)
    return res[:4], res[4]


def _reduce_wait(ssem, rsem, vec, zone, after):
    def body(src, land, ss, rs, *_):
        x, y, c = _place()
        for mask in range(1, N_PEERS + 1):
            _, frm = _peer(x, y, c, mask)
            cp = pltpu.make_async_remote_copy(src_ref=src, dst_ref=land.at[frm], send_sem=ss.at[mask - 1],
                                              recv_sem=rs.at[mask - 1], device_id=(x, y, c), device_id_type=MESH)
            cp.wait_send()
            cp.wait_recv()

    return pl.pallas_call(
        body, name="reduce_wait", out_shape=(pltpu.HBM(vec.shape, vec.dtype), pltpu.HBM(zone.shape, zone.dtype)),
        in_specs=[HBM, HBM, SEM, SEM] + [ANY] * len(after), out_specs=(HBM, HBM),
        input_output_aliases={0: 0, 1: 1}, compiler_params=SPLIT_COPY,
    )(vec, zone, ssem, rsem, *after)


def _reduce_sum(vec, zone, me, loss_row, loss_scale):
    r, dm = vec.shape

    def body(me_ref, v_ref, z_ref, o_ref, l_ref):
        acc = None
        for i in range(N_PEERS + 1):
            term = jnp.where(me_ref[0] == i, v_ref[...], z_ref[i])
            acc = term if acc is None else acc + term
        o_ref[...] = acc
        l_ref[...] = jnp.sum(acc[loss_row:loss_row + SMALL_ROWS, :], axis=(0, 1), keepdims=True) * loss_scale

    grid_spec = pltpu.PrefetchScalarGridSpec(
        num_scalar_prefetch=1, grid=(1,),
        in_specs=[pl.BlockSpec((r, dm), lambda i, me_ref: (0, 0)), pl.BlockSpec((N_PEERS + 1, r, dm), lambda i, me_ref: (0, 0, 0))],
        out_specs=(pl.BlockSpec((r, dm), lambda i, me_ref: (0, 0)), pl.BlockSpec((1, 1), lambda i, me_ref: (0, 0))))
    return pl.pallas_call(
        body, name="reduce_sum", out_shape=(jax.ShapeDtypeStruct((r, dm), F32), jax.ShapeDtypeStruct((1, 1), F32)),
        grid_spec=grid_spec, compiler_params=_params(1),
    )(me, vec, zone)


def kernel(x, meta_tokens, norm_mix_g, w_in, b_gate, pool_w, pool_scale, conv_w, conv_out_w, w_o, norm_ffn_g, w_gate_up, w_down, norm_final_g, loss_target, m_meta_tokens, m_norm_mix_g, m_w_in, m_b_gate, m_pool_w, m_pool_scale, m_conv_w, m_conv_out_w, m_w_o, m_norm_ffn_g, m_w_gate_up, m_w_down, m_norm_final_g, v_meta_tokens, v_norm_mix_g, v_w_in, v_b_gate, v_pool_w, v_pool_scale, v_conv_w, v_conv_out_w, v_w_o, v_norm_ffn_g, v_w_gate_up, v_w_down, v_norm_final_g):
    seq, dm = x.shape[1], x.shape[2]
    tail = LANES
    tm = tail
    lp = seq + tail
    tm_row = _row_tile(lp, dm, 4, 3 * 1024 * 1024)
    n_chips = 4
    n_groups = len(POOL_WINDOWS)
    gw = dm // n_groups
    tc = min(256, gw)
    cx, cy, cc = _place()
    chip = 2 * cx + cy
    dloc = dm // n_chips

    pool2 = pool_w.reshape(n_groups * pool_w.shape[1], gw)
    big = {"w_in": w_in, "w_gate_up": w_gate_up, "pool_w": pool2, "conv_out_w": conv_out_w, "w_o": w_o, "w_down": w_down}
    chip1 = jnp.reshape(chip, (1,)).astype(jnp.int32)
    core = jnp.reshape(cc, (1,)).astype(jnp.int32)
    small_loc = jnp.concatenate([meta_tokens, jnp.pad(conv_w, ((0, 8 - conv_w.shape[0]), (0, 0))),
                                 jnp.zeros((8, dloc), F32)], axis=0)
    g1, g2, g3 = norm_mix_g.reshape(1, dm), norm_ffn_g.reshape(1, dm), norm_final_g.reshape(1, dm)
    b_gate2 = b_gate.reshape(2, dm)
    ps = pool_scale.reshape(1, dm)
    first = [_cast_into_slot("cast_w_in", w_in, chip1, BF16), _cast_into_slot("place_small", small_loc, chip1, F32)]
    sems, first, token = _ag_start("ag_start_first", first)
    cast = {nme: _cast_into_slot("cast_" + nme, big[nme], chip1, BF16, deps=(token,))
            for nme in ["pool_w", "conv_out_w", "w_o", "w_gate_up", "w_down"]}
    sems, first = _ag_relay("ag_relay_first", first, sems, list(cast.values()))
    sems, first = _ag_relay_wait("ag_relay_wait_first", first, sems, [])
    w_in4, small4 = _ag_final_wait("ag_final_wait_first", first, sems, [])
    mixer_w = [cast["pool_w"], cast["conv_out_w"], cast["w_o"]]
    sems_mix, mixer_w, token = _ag_start("ag_start_mixer", mixer_w, deps=(w_in4,))
    sems_gu, (w_gu4,), token = _ag_start("ag_start_gate_up", [cast["w_gate_up"]], deps=(token,))

    small_f = jnp.transpose(small4, (1, 0, 2)).reshape(small4.shape[1], dm)
    meta_f = small_f[:N_META]
    conv_w_f = small_f[N_META:N_META + 3]
    tail_rows = jnp.concatenate([jnp.zeros((tail - N_META, dm), F32), meta_f], axis=0)
    h0, hn1 = _rms_fwd_input("rms_mix", x[0], tail_rows, g1, deps=(token,))
    proj = _nn_sharded("proj", hn1, w_in4, 6)
    sems_mix, mixer_w = _ag_relay("ag_relay_mixer", mixer_w, sems_mix, [proj])
    (sems_gu, (w_gu4,)), (sems_down, (w_down4,)) = _ag_relay("ag_relay_gate_up", [w_gu4], sems_gu, [mixer_w[0]],
                                                              then_start=[cast["w_down"]])
    pooled, z = _mixer_fwd("mixer_fwd", proj, conv_w_f, tc, w_down4)
    sems_mix, mixer_w = _ag_relay_wait("ag_relay_wait_mixer", mixer_w, sems_mix, [pooled])
    pool4, conv_out4, w_o4 = _ag_final_wait("ag_final_wait_mixer", mixer_w, sems_mix, [])
    pool_f = jnp.transpose(pool4.reshape(n_chips, n_groups, gw // n_chips, gw), (1, 0, 2, 3)).reshape(n_groups, gw, gw)
    conv_out_f = conv_out4.reshape(dm, dm)
    w_o_f = w_o4.reshape(dm, dm)
    ya = _pool_fwd("pool_proj", pooled, pool_f)
    yb = _nn_plain("conv_out", z, conv_out_f, BF16)
    mix = _gate_mix("gate_mix", proj, b_gate2, ya, ps, yb, tm_row)
    sems_gu, (w_gu4,) = _ag_relay_wait("ag_relay_wait_gate_up", [w_gu4], sems_gu, [mix])
    h1 = _nn_plain("attn_out", mix, w_o_f, F32, res=h0, tn_pref=256)
    (w_gu4,) = _ag_final_wait("ag_final_wait_gate_up", [w_gu4], sems_gu, [h1])
    hn2 = _rms_fwd("rms_ffn", h1, g2, tm_row)
    sems_down, (w_down4,) = _ag_relay("ag_relay_down", [w_down4], sems_down, [hn2])
    gu, act = _gate_up_swiglu("gate_up", hn2, w_gu4, w_down4)
    sems_down, (w_down4,) = _ag_relay_wait("ag_relay_wait_down", [w_down4], sems_down, [act])
    (w_down4,) = _ag_final_wait("ag_final_wait_down", [w_down4], sems_down, [])
    w_down_f = w_down4.reshape(-1, dm)
    h2 = _nn_rows("ffn_down", act, w_down_f, h1)
    dh2, dh2b, loss_cols, dg3 = _final_loss("final_loss", h2, g3, loss_target[0], tm)

    def scatter(tag, names_g, swap, after):
        grads_g, got = _swap_wait("swap_wait_" + tag, *swap, [after])
        pairs = [_pair_add("pair_add_" + nme, g4, rv, core) for nme, g4, rv in zip(names_g, grads_g, got)]
        return _scatter_start("scatter_start_" + tag, pairs)

    dgu = _dact_swiglu_bwd("d_gate_up", dh2b, w_down_f, gu)
    gw_down = _tn_plain("dw_down", act, dh2b)
    gw_gu = _tn_sharded("dw_gate_up", hn2, dgu, n_chips)
    swap_a, token = _swap_start("swap_start_a", [gw_gu, gw_down.reshape(n_chips, -1, dm)])
    dhn2 = _nt_sharded("d_hn2", dgu, w_gu4, tr_pref=2816, row_tiles=2, deps=(token,))
    flight_a, token = scatter("a", ["w_gate_up", "w_down"], swap_a, dhn2)
    dh1, dh1b, dg2 = _rms_bwd("rms_ffn_bwd", dhn2, h1, g2, dh2, tm_row, token)
    dmix = _nt_plain("d_mix", dh1b, w_o_f)
    gw_o = _tn_plain("dw_o", mix, dh1b)
    dproj, dyb, dya, db_gate, dps = _gate_bwd("gate_bwd", dmix, proj, b_gate2, ya, ps, yb, tm_row)
    gw_conv_out = _tn_plain("dw_conv_out", z, dyb)
    gw_pool = _pool_bwd_w("dw_pool", pooled, dya)
    gw_pool = jnp.transpose(gw_pool.reshape(n_groups, n_chips, gw // n_chips, gw), (1, 0, 2, 3))
    swap_b, token = _swap_start("swap_start_b", [gw_o.reshape(n_chips, dloc, dm), gw_conv_out.reshape(n_chips, dloc, dm),
                                                 gw_pool.reshape(n_chips, n_groups * (gw // n_chips), gw)])
    dpooled = _pool_bwd_act("d_pooled", dya, pool_f, deps=(token,))
    dz = _nt_plain("d_z", dyb, conv_out_f)
    flight_b, token = scatter("b", ["w_o", "conv_out_w", "pool_w"], swap_b, dz)
    dproj, dconv_w = _mixer_bwd("mixer_bwd", dz, dpooled, proj, conv_w_f, dproj, tc, token)
    gw_in0 = _tn_sharded("dw_in_0", hn1, dproj, n_chips, part=(0, 2))
    swap_c0, token = _swap_start("swap_start_c0", [gw_in0])
    gw_in1 = _tn_sharded("dw_in_1", hn1, dproj, n_chips, part=(1, 2), deps=(token,))
    flight_c0, token = scatter("c0", ["w_in_0"], swap_c0, gw_in1)
    swap_c1, token = _swap_start("swap_start_c1", [gw_in1], deps=(token,))
    dhn1 = _nt_in_proj("d_hn1", dproj, w_in4, deps=(token,))
    flight_c1, token = scatter("c1", ["w_in_1"], swap_c1, dhn1)
    dx, dtail, dg1 = _rms_bwd_input("rms_mix_bwd", dhn1, h0, g1, dh1, tm, seq, token)
    grad_x = dx[None]
    dmeta = dtail[tail - N_META:]

    given = dict(meta_tokens=(meta_tokens, m_meta_tokens, v_meta_tokens), norm_mix_g=(norm_mix_g, m_norm_mix_g, v_norm_mix_g),
                 w_in=(w_in, m_w_in, v_w_in), b_gate=(b_gate, m_b_gate, v_b_gate), pool_w=(pool_w, m_pool_w, v_pool_w),
                 pool_scale=(pool_scale, m_pool_scale, v_pool_scale), conv_w=(conv_w, m_conv_w, v_conv_w),
                 conv_out_w=(conv_out_w, m_conv_out_w, v_conv_out_w), w_o=(w_o, m_w_o, v_w_o),
                 norm_ffn_g=(norm_ffn_g, m_norm_ffn_g, v_norm_ffn_g), w_gate_up=(w_gate_up, m_w_gate_up, v_w_gate_up),
                 w_down=(w_down, m_w_down, v_w_down), norm_final_g=(norm_final_g, m_norm_final_g, v_norm_final_g))
    order = list(given.keys())
    grad, delta, new_m, new_v = {}, {}, {}, {}
    vec = jnp.concatenate([dg1, dg2, dg3, db_gate, dps, loss_cols, dconv_w, dmeta], axis=0)
    loss_row = 5 * SMALL_ROWS
    groups_g = {"a": [("w_gate_up", (0, 1)), ("w_down", (0, 1))], "b": [("w_o", (0, 1)), ("conv_out_w", (0, 1)), ("pool_w", (0, 1))],
                "c0": [("w_in", (0, 2))], "c1": [("w_in", (1, 2))]}
    results = {}

    def reduced(tag, flight, after):
        pairs, zones = _scatter_wait("scatter_wait_" + tag, *flight, after)
        halves = [_chip_sum("chip_sum_%s_%d" % (nme, part[0]), p, rv, chip1) for (nme, part), p, rv in zip(groups_g[tag], pairs, zones)]
        return _swap_start("send_start_" + tag, halves, halves=False)

    def update(tag, send, after):
        halves, sib_halves = _swap_wait("send_wait_" + tag, *send, after, halves=False)
        deltas = []
        for (nme, part), g_own, g_sib in zip(groups_g[tag], halves, sib_halves):
            w, m, v = given[nme]
            shape2 = (2 * g_own.shape[0] * part[1], g_own.shape[1])
            results[nme] = _adamw_halves("adamw_%s_%d" % (nme, part[0]), w.reshape(shape2), g_own, g_sib, m.reshape(shape2),
                                         v.reshape(shape2), core, part=part, prev=results.get(nme))
            grad[nme], delta[nme], new_m[nme], new_v[nme] = [t.reshape(w.shape) for t in results[nme]]
            deltas.append(results[nme][1])
        return deltas

    send_a, token = reduced("a", flight_a, [dx])
    send_b, token = reduced("b", flight_b, [token])
    done_a = update("a", send_a, [token])
    send_c0, token = reduced("c0", flight_c0, done_a)
    done_b = update("b", send_b, [token])
    send_c1, token = reduced("c1", flight_c1, done_b)
    me1 = jnp.reshape(4 * cx + 2 * cy + cc, (1,)).astype(jnp.int32)
    red_flight, token = _reduce_start(vec, [token])
    done_c0 = update("c0", send_c0, [token])
    done_c1 = update("c1", send_c1, done_c0)
    red, loss11 = _reduce_sum(*_reduce_wait(*red_flight, done_c1), me1, loss_row, 0.5 / dm)
    loss = loss11[0, 0]
    col0 = chip * dloc
    g_small = {
        "norm_mix_g": red[0], "norm_ffn_g": red[SMALL_ROWS], "norm_final_g": red[2 * SMALL_ROWS],
        "b_gate": red[3 * SMALL_ROWS:3 * SMALL_ROWS + 2].reshape(-1), "pool_scale": red[4 * SMALL_ROWS],
        "conv_w": lax.dynamic_slice(red, (6 * SMALL_ROWS, col0), (3, dloc)),
        "meta_tokens": lax.dynamic_slice(red, (7 * SMALL_ROWS, col0), (N_META, dloc)),
    }

    vec_names = ["norm_mix_g", "norm_ffn_g", "norm_final_g", "pool_scale"]

    def slab_vec(pick):
        rows = [pick(nme).reshape(1, dm) for nme in vec_names] + [pick("b_gate").reshape(2, dm), jnp.zeros((2, dm), F32)]
        return jnp.concatenate(rows, axis=0)

    def slab_col(pick):
        return jnp.concatenate([pick("meta_tokens"), pick("conv_w"), jnp.zeros((5, dloc), F32)], axis=0)

    for slab, tag in ((slab_vec, "vec"), (slab_col, "col")):
        d, nm, nv = _adamw("adamw_small_" + tag, slab(lambda nme: given[nme][0]), slab(lambda nme: g_small[nme]),
                           slab(lambda nme: given[nme][1]), slab(lambda nme: given[nme][2]))
        for out, res in ((delta, d), (new_m, nm), (new_v, nv)):
            if tag == "vec":
                for i, nme in enumerate(vec_names):
                    out[nme] = res[i]
                out["b_gate"] = res[4:6].reshape(-1)
            else:
                out["meta_tokens"] = res[:N_META]
                out["conv_w"] = res[N_META:N_META + 3]
    grad.update(g_small)
    return (loss, grad_x, *[grad[nme] for nme in order], *[delta[nme] for nme in order],
            *[new_m[nme] for nme in order], *[new_v[nme] for nme in order])
```

```python
import math

import jax
import jax.numpy as jnp
from jax import lax
from jax.experimental import pallas as pl
from jax.experimental.pallas import tpu as pltpu

F32 = jnp.float32
BF16 = jnp.bfloat16
N_META = 16
POOL_WINDOWS = (2, 4, 8, 16)
EPS = 1e-6
ADAM_LR, ADAM_B1, ADAM_B2, ADAM_EPS, ADAM_WD, ADAM_STEP = 0.001, 0.9, 0.999, 1e-08, 0.01, 10
LANES = 128
V7X_VMEM_BYTES = 64 * 1024 * 1024
VMEM_LIMIT = V7X_VMEM_BYTES - 8 * 1024 * 1024
MESH = pl.DeviceIdType.MESH
ANY = pl.BlockSpec(memory_space=pl.ANY)
CHIP_FLIPS = ((1, 0), (0, 1), (1, 1))
SMALL_ROWS = 8
TAIL_ROWS = 64


def _pick(n, pref):
    best = None
    for t in range(LANES, min(n, pref) + 1, LANES):
        if n % t == 0:
            best = t
    assert best is not None, (n, pref)
    return best


def _params(n_axes=0):
    sem = ("arbitrary",) * n_axes if n_axes else None
    return pltpu.CompilerParams(dimension_semantics=sem, vmem_limit_bytes=VMEM_LIMIT)


_DIMS = {
    "nn": (((1,), (0,)), ((), ())),
    "nt": (((1,), (1,)), ((), ())),
    "tn": (((0,), (0,)), ((), ())),
}


def _matmul(name, mode, a, b, out_sds, grid, a_spec, b_spec, o_spec, nk, res=None, res_spec=None, acc_shape=None, deps=()):
    out_dtype = out_sds.dtype
    in_place = nk > 1 and out_dtype == F32
    use_scratch = nk > 1 and not in_place
    rows = a_spec.block_shape[-2] if mode != "tn" else None
    chunk = _row_tile(rows, 1, 1, 1152) if rows is not None else None
    n_in = 2 + (res is not None) + len(deps)

    def body(*refs):
        a_ref, b_ref = refs[:2]
        r_ref = refs[2] if res is not None else None
        o_ref, *scr = refs[n_in:]
        k = pl.program_id(len(grid) - 1) if nk > 1 else None

        def emit(sl):
            if sl is None:
                part = lax.dot_general(a_ref[...], b_ref[...], _DIMS[mode], preferred_element_type=F32)
                idx = (slice(None), slice(None))
            else:
                part = lax.dot_general(a_ref[sl, :], b_ref[...], _DIMS[mode], preferred_element_type=F32)
                idx = (sl, slice(None))
            if nk == 1:
                if r_ref is not None:
                    part = part + r_ref[idx]
                o_ref[idx] = part.astype(out_dtype)
                return
            acc = scr[0] if use_scratch else o_ref

            @pl.when(k == 0)
            def _():
                first = part
                if r_ref is not None and in_place:
                    first = first + r_ref[idx]
                acc[idx] = first

            @pl.when(k > 0)
            def _():
                acc[idx] += part

            if use_scratch:

                @pl.when(k == nk - 1)
                def _():
                    o_ref[idx] = acc[idx].astype(out_dtype)

        if mode == "tn" or chunk == rows:
            emit(None)
        else:
            for m0 in range(0, rows, chunk):
                emit(pl.ds(m0, chunk))

    ins = [a, b] + ([res] if res is not None else []) + list(deps)
    in_specs = [a_spec, b_spec] + ([res_spec] if res is not None else []) + [ANY] * len(deps)
    scratch = [pltpu.VMEM(acc_shape, F32)] if use_scratch else []
    return pl.pallas_call(
        body, name=name, out_shape=out_sds, grid=grid, in_specs=in_specs, out_specs=o_spec,
        scratch_shapes=scratch, compiler_params=_params(len(grid)),
    )(*ins)


def _nn_sharded(name, a, w4, nseg):
    lp, kdim = a.shape
    s, _, nloc = w4.shape
    segw = s * nloc // nseg
    tn = _pick(math.gcd(nloc, segw), 1536)
    bw, bo = nloc // tn, segw // tn
    return _matmul(
        name, "nn", a, w4, jax.ShapeDtypeStruct((nseg, lp, segw), BF16), (s * bw,),
        pl.BlockSpec((lp, kdim), lambda j: (0, 0)),
        pl.BlockSpec((None, kdim, tn), lambda j: (j // bw, 0, j % bw)),
        pl.BlockSpec((None, lp, tn), lambda j: (j // bo, 0, j % bo)), 1)


def _nt_in_proj(name, dseg, w4, row_tiles=2, to_pref=1024, deps=()):
    nseg, lp, segw = dseg.shape
    s, kdim, nloc = w4.shape
    assert nseg * segw == s * nloc and 2 * nloc == 3 * segw, (dseg.shape, w4.shape)
    half = segw // 2
    to = _pick(kdim, to_pref)
    tm = lp // row_tiles

    def body(full_ref, half_ref, w_ref, *rest):
        o_ref = rest[len(deps)]
        r = pl.program_id(2)

        def contribution(full_first):
            lo, hi = (pl.ds(0, segw), pl.ds(segw, half)) if full_first else (pl.ds(half, segw), pl.ds(0, half))
            return (lax.dot_general(full_ref[...], w_ref[:, lo], _DIMS["nt"], preferred_element_type=F32)
                    + lax.dot_general(half_ref[...], w_ref[:, hi], _DIMS["nt"], preferred_element_type=F32))

        @pl.when(r == 0)
        def _():
            o_ref[...] = contribution(True)

        for ri in range(1, s):

            @pl.when(r == ri)
            def _(ri=ri):
                o_ref[...] += contribution(ri % 2 == 0)

    return pl.pallas_call(
        body, name=name, out_shape=jax.ShapeDtypeStruct((lp, kdim), F32), grid=(row_tiles, kdim // to, s),
        in_specs=[pl.BlockSpec((None, tm, segw), lambda m, j, r: ((3 * r + 1) // 2, m, 0)),
                  pl.BlockSpec((None, tm, half), lambda m, j, r: (1 + 3 * (r // 2), m, r % 2)),
                  pl.BlockSpec((None, to, nloc), lambda m, j, r: (r, j, 0))] + [ANY] * len(deps),
        out_specs=pl.BlockSpec((tm, to), lambda m, j, r: (m, j)), compiler_params=_params(3),
    )(dseg, dseg, w4, *deps)


def _nn_plain(name, a, w, out_dtype, res=None, tn_pref=512, tk_pref=2048, deps=()):
    lp, kdim = a.shape
    n = w.shape[1]
    tn = _pick(n, tn_pref)
    tk = kdim if kdim <= tk_pref else _pick(kdim, tk_pref)
    nk = kdim // tk
    grid = (n // tn, nk) if nk > 1 else (n // tn,)
    if nk > 1:
        a_spec = pl.BlockSpec((lp, tk), lambda j, k: (0, k))
        w_spec = pl.BlockSpec((tk, tn), lambda j, k: (k, j))
        o_spec = pl.BlockSpec((lp, tn), lambda j, k: (0, j))
    else:
        a_spec = pl.BlockSpec((lp, tk), lambda j: (0, 0))
        w_spec = pl.BlockSpec((tk, tn), lambda j: (0, j))
        o_spec = pl.BlockSpec((lp, tn), lambda j: (0, j))
    return _matmul(name, "nn", a, w, jax.ShapeDtypeStruct((lp, n), out_dtype), grid, a_spec, w_spec, o_spec, nk,
                   res=res, res_spec=o_spec if res is not None else None, acc_shape=(lp, tn), deps=deps)


def _nt_plain(name, a, w, tn_pref=512):
    lp, kdim = a.shape
    n = w.shape[0]
    tn = _pick(n, tn_pref)
    return _matmul(
        name, "nt", a, w, jax.ShapeDtypeStruct((lp, n), BF16), (n // tn,),
        pl.BlockSpec((lp, kdim), lambda j: (0, 0)),
        pl.BlockSpec((tn, kdim), lambda j: (j, 0)),
        pl.BlockSpec((lp, tn), lambda j: (0, j)), 1)


def _nt_sharded(name, dseg, w4, to_pref=1024, tr_pref=1536, row_tiles=1, deps=()):
    nseg, lp, segw = dseg.shape
    s, kdim, nloc = w4.shape
    tr = _pick(math.gcd(nloc, segw), tr_pref)
    ba, bw = segw // tr, nloc // tr
    nr = s * bw
    to = _pick(kdim, to_pref)
    tm = lp // row_tiles
    return _matmul(
        name, "nt", dseg, w4, jax.ShapeDtypeStruct((lp, kdim), F32), (row_tiles, kdim // to, nr),
        pl.BlockSpec((None, tm, tr), lambda m, j, r: (r // ba, m, r % ba)),
        pl.BlockSpec((None, to, tr), lambda m, j, r: (r // bw, j, r % bw)),
        pl.BlockSpec((tm, to), lambda m, j, r: (m, j)), nr, deps=deps)


def _nn_rows(name, a, w, res, row_tiles=2, tn_pref=512):
    lp, kdim = a.shape
    n = w.shape[1]
    tn = _pick(n, tn_pref)
    tm = lp // row_tiles
    blk = pl.BlockSpec((tm, tn), lambda i, j: (i, j))
    return _matmul(name, "nn", a, w, jax.ShapeDtypeStruct((lp, n), F32), (row_tiles, n // tn),
                   pl.BlockSpec((tm, kdim), lambda i, j: (i, 0)), pl.BlockSpec((kdim, tn), lambda i, j: (0, j)), blk, 1,
                   res=res, res_spec=blk)


def _tn_plain(name, a, d, tk_pref=1024):
    lp, kdim = a.shape
    n = d.shape[1]
    tk = _pick(kdim, tk_pref)
    return _matmul(
        name, "tn", a, d, jax.ShapeDtypeStruct((kdim, n), BF16), (kdim // tk,),
        pl.BlockSpec((lp, tk), lambda i: (0, i)),
        pl.BlockSpec((lp, n), lambda i: (0, 0)),
        pl.BlockSpec((tk, n), lambda i: (i, 0)), 1)


def _tn_sharded(name, a, dseg, s, part=(0, 1), tk_pref=1024, deps=()):
    lp, kdim = a.shape
    nseg, _, segw = dseg.shape
    nloc = nseg * segw // s
    tn = _pick(math.gcd(nloc, segw), 1536)
    bd, bo = segw // tn, nloc // tn
    kpart = kdim // part[1]
    tk = _pick(kpart, tk_pref)
    i0 = part[0] * (kpart // tk)

    def body(a_ref, d_ref, *rest):
        o_ref, at_ref = rest[len(deps):]

        @pl.when(pl.program_id(1) == 0)
        def _():
            at_ref[...] = a_ref[...].T

        o_ref[...] = jnp.dot(at_ref[...], d_ref[...], preferred_element_type=F32).astype(BF16)

    return pl.pallas_call(
        body, name=name, out_shape=jax.ShapeDtypeStruct((s, kpart, nloc), BF16), grid=(kpart // tk, s * bo),
        in_specs=[pl.BlockSpec((lp, tk), lambda i, j: (0, i0 + i)),
                  pl.BlockSpec((None, lp, tn), lambda i, j: (j // bd, 0, j % bd))] + [ANY] * len(deps),
        out_specs=pl.BlockSpec((None, tk, tn), lambda i, j: (j // bo, i, j % bo)),
        scratch_shapes=[pltpu.VMEM((tk, lp), BF16)], compiler_params=_params(2),
    )(a, dseg, *deps)


def _silu_parts(gt):
    sg = jax.nn.sigmoid(gt)
    return gt * sg, sg * (1.0 + gt * (1.0 - sg))


def _gate_up_swiglu(name, a, w4, dep, tn_pref=256):
    lp, kdim = a.shape
    s, _, nloc = w4.shape
    f = s * nloc // 2
    tn = _pick(nloc, tn_pref)
    bw = nloc // tn
    chunk = _row_tile(lp, 1, 1, 576)

    def body(a_ref, wg_ref, wu_ref, _, gu_ref, act_ref):
        for m0 in range(0, lp, chunk):
            sl = pl.ds(m0, chunk)
            gt = jnp.dot(a_ref[sl, :], wg_ref[...], preferred_element_type=F32)
            up = jnp.dot(a_ref[sl, :], wu_ref[...], preferred_element_type=F32)
            gu_ref[0, sl, :] = gt.astype(BF16)
            gu_ref[1, sl, :] = up.astype(BF16)
            act_ref[sl, :] = (_silu_parts(gt)[0] * up).astype(BF16)

    return pl.pallas_call(
        body, name=name, grid=(f // tn,),
        out_shape=(jax.ShapeDtypeStruct((2, lp, f), BF16), jax.ShapeDtypeStruct((lp, f), BF16)),
        in_specs=[pl.BlockSpec((lp, kdim), lambda j: (0, 0)),
                  pl.BlockSpec((None, kdim, tn), lambda j: (j // bw, 0, j % bw)),
                  pl.BlockSpec((None, kdim, tn), lambda j: (s // 2 + j // bw, 0, j % bw)), ANY],
        out_specs=(pl.BlockSpec((2, lp, tn), lambda j: (0, 0, j)), pl.BlockSpec((lp, tn), lambda j: (0, j))),
        compiler_params=_params(1),
    )(a, w4, w4, dep)


def _dact_swiglu_bwd(name, d, w, gu, tn_pref=512):
    lp, dm = d.shape
    f = w.shape[0]
    tn = _pick(f, tn_pref)
    chunk = _row_tile(lp, 1, 1, 576)

    def body(d_ref, w_ref, g_ref, u_ref, o_ref):
        for m0 in range(0, lp, chunk):
            sl = pl.ds(m0, chunk)
            dact = lax.dot_general(d_ref[sl, :], w_ref[...], _DIMS["nt"], preferred_element_type=F32)
            silu, dsilu = _silu_parts(g_ref[sl, :].astype(F32))
            o_ref[0, sl, :] = (dact * u_ref[sl, :].astype(F32) * dsilu).astype(BF16)
            o_ref[1, sl, :] = (dact * silu).astype(BF16)

    return pl.pallas_call(
        body, name=name, grid=(f // tn,), out_shape=jax.ShapeDtypeStruct((2, lp, f), BF16),
        in_specs=[pl.BlockSpec((lp, dm), lambda j: (0, 0)), pl.BlockSpec((tn, dm), lambda j: (j, 0)),
                  pl.BlockSpec((None, lp, tn), lambda j: (0, 0, j)), pl.BlockSpec((None, lp, tn), lambda j: (1, 0, j))],
        out_specs=pl.BlockSpec((2, lp, tn), lambda j: (0, 0, j)), compiler_params=_params(1),
    )(d, w, gu, gu)


def _pool_fwd(name, pooled, pw):
    lp, dm = pooled.shape
    g, gw, _ = pw.shape
    return _matmul(
        name, "nn", pooled, pw, jax.ShapeDtypeStruct((lp, dm), BF16), (g,),
        pl.BlockSpec((lp, gw), lambda gi: (0, gi)), pl.BlockSpec((None, gw, gw), lambda gi: (gi, 0, 0)),
        pl.BlockSpec((lp, gw), lambda gi: (0, gi)), 1)


def _pool_bwd_act(name, dya, pw, deps=()):
    lp, dm = dya.shape
    g, gw, _ = pw.shape
    return _matmul(
        name, "nt", dya, pw, jax.ShapeDtypeStruct((lp, dm), BF16), (g,),
        pl.BlockSpec((lp, gw), lambda gi: (0, gi)), pl.BlockSpec((None, gw, gw), lambda gi: (gi, 0, 0)),
        pl.BlockSpec((lp, gw), lambda gi: (0, gi)), 1, deps=deps)


def _pool_bwd_w(name, pooled, dya):
    lp, dm = pooled.shape
    g = len(POOL_WINDOWS)
    gw = dm // g
    return _matmul(
        name, "tn", pooled, dya, jax.ShapeDtypeStruct((g, gw, gw), BF16), (g,),
        pl.BlockSpec((lp, gw), lambda gi: (0, gi)), pl.BlockSpec((lp, gw), lambda gi: (0, gi)),
        pl.BlockSpec((None, gw, gw), lambda gi: (gi, 0, 0)), 1)


def _rms_fwd(name, h, g, tm, deps=()):
    lp, dm = h.shape

    def body(h_ref, g_ref, *rest):
        hv = h_ref[...]
        r = lax.rsqrt(jnp.mean(hv * hv, axis=-1, keepdims=True) + EPS)
        rest[-1][...] = (hv * r * g_ref[...]).astype(BF16)

    row = pl.BlockSpec((tm, dm), lambda i: (i, 0))
    return pl.pallas_call(
        body, name=name, out_shape=jax.ShapeDtypeStruct((lp, dm), BF16), grid=(lp // tm,),
        in_specs=[row, pl.BlockSpec((1, dm), lambda i: (0, 0))] + [ANY] * len(deps), out_specs=row, compiler_params=_params(1),
    )(h, g, *deps)


def _rms_fwd_input(name, x2d, tail_rows, g, deps=()):
    seq, dm = x2d.shape
    tm = tail_rows.shape[0]
    nx = seq // tm
    lp = seq + tm

    def body(x_ref, t_ref, g_ref, *rest):
        h_ref, o_ref = rest[len(deps):]

        def emit(hv):
            r = lax.rsqrt(jnp.mean(hv * hv, axis=-1, keepdims=True) + EPS)
            h_ref[...] = hv
            o_ref[...] = (hv * r * g_ref[...]).astype(BF16)

        @pl.when(pl.program_id(0) < nx)
        def _():
            emit(x_ref[...])

        @pl.when(pl.program_id(0) >= nx)
        def _():
            emit(t_ref[...])

    row = pl.BlockSpec((tm, dm), lambda i: (i, 0))
    return pl.pallas_call(
        body, name=name, grid=(lp // tm,),
        out_shape=(jax.ShapeDtypeStruct((lp, dm), F32), jax.ShapeDtypeStruct((lp, dm), BF16)),
        in_specs=[pl.BlockSpec((tm, dm), lambda i: (jnp.minimum(i, nx - 1), 0)), pl.BlockSpec((tm, dm), lambda i: (0, 0)),
                  pl.BlockSpec((1, dm), lambda i: (0, 0))] + [ANY] * len(deps),
        out_specs=(row, row), compiler_params=_params(1),
    )(x2d, tail_rows, g, *deps)


def _rms_bwd(name, dy, h, g, dres, tm, dep):
    lp, dm = h.shape

    def body(dy_ref, h_ref, g_ref, dr_ref, _, dh_ref, dhb_ref, dg_ref):
        hv = h_ref[...]
        r = lax.rsqrt(jnp.mean(hv * hv, axis=-1, keepdims=True) + EPS)
        xhat = hv * r
        dyv = dy_ref[...]
        dxh = dyv * g_ref[...]
        dh = dr_ref[...] + r * (dxh - xhat * jnp.mean(dxh * xhat, axis=-1, keepdims=True))
        dh_ref[...] = dh
        dhb_ref[...] = dh.astype(BF16)

        @pl.when(pl.program_id(0) == 0)
        def _():
            dg_ref[...] = jnp.zeros_like(dg_ref)

        dg_ref[0:1, :] += jnp.sum(dyv * xhat, axis=0, keepdims=True)

    row = pl.BlockSpec((tm, dm), lambda i: (i, 0))
    slab = pl.BlockSpec((SMALL_ROWS, dm), lambda i: (0, 0))
    return pl.pallas_call(
        body, name=name, grid=(lp // tm,),
        out_shape=(jax.ShapeDtypeStruct((lp, dm), F32), jax.ShapeDtypeStruct((lp, dm), BF16),
                   jax.ShapeDtypeStruct((SMALL_ROWS, dm), F32)),
        in_specs=[row, row, pl.BlockSpec((1, dm), lambda i: (0, 0)), row, ANY], out_specs=(row, row, slab),
        compiler_params=_params(1),
    )(dy, h, g, dres, dep)


def _rms_bwd_input(name, dy, h, g, dres, tm, seq, dep):
    lp, dm = h.shape
    nx = seq // tm

    def body(dy_ref, h_ref, g_ref, dr_ref, _, dx_ref, dt_ref, dg_ref):
        i = pl.program_id(0)
        hv = h_ref[...]
        r = lax.rsqrt(jnp.mean(hv * hv, axis=-1, keepdims=True) + EPS)
        xhat = hv * r
        dyv = dy_ref[...]
        dxh = dyv * g_ref[...]
        dh = dr_ref[...] + r * (dxh - xhat * jnp.mean(dxh * xhat, axis=-1, keepdims=True))

        @pl.when(i < nx)
        def _():
            dx_ref[...] = dh

        @pl.when(i >= nx)
        def _():
            dt_ref[...] = dh

        @pl.when(i == 0)
        def _():
            dg_ref[...] = jnp.zeros_like(dg_ref)

        dg_ref[0:1, :] += jnp.sum(dyv * xhat, axis=0, keepdims=True)

    row = pl.BlockSpec((tm, dm), lambda i: (i, 0))
    slab = pl.BlockSpec((SMALL_ROWS, dm), lambda i: (0, 0))
    return pl.pallas_call(
        body, name=name, grid=(lp // tm,),
        out_shape=(jax.ShapeDtypeStruct((seq, dm), F32), jax.ShapeDtypeStruct((tm, dm), F32),
                   jax.ShapeDtypeStruct((SMALL_ROWS, dm), F32)),
        in_specs=[row, row, pl.BlockSpec((1, dm), lambda i: (0, 0)), row, ANY],
        out_specs=(pl.BlockSpec((tm, dm), lambda i: (jnp.minimum(i, nx - 1), 0)), pl.BlockSpec((tm, dm), lambda i: (0, 0)), slab),
        compiler_params=_params(1),
    )(dy, h, g, dres, dep)


def _gate_mix(name, proj, b_gate2, ya, pool_scale, yb, tm):
    _, lp, dm = proj.shape

    def body(ga_ref, gr_ref, b_ref, ya_ref, ps_ref, yb_ref, o_ref):
        g_a = jax.nn.sigmoid(ga_ref[...].astype(F32) + b_ref[0:1, :])
        g_b = jax.nn.sigmoid(gr_ref[...].astype(F32) + b_ref[1:2, :])
        y_a = ya_ref[...].astype(F32) * ps_ref[...]
        o_ref[...] = (g_a * y_a + g_b * yb_ref[...].astype(F32)).astype(BF16)

    row = pl.BlockSpec((tm, dm), lambda i: (i, 0))
    return pl.pallas_call(
        body, name=name, out_shape=jax.ShapeDtypeStruct((lp, dm), BF16), grid=(lp // tm,),
        in_specs=[pl.BlockSpec((None, tm, dm), lambda i: (4, i, 0)), pl.BlockSpec((None, tm, dm), lambda i: (5, i, 0)),
                  pl.BlockSpec((2, dm), lambda i: (0, 0)), row, pl.BlockSpec((1, dm), lambda i: (0, 0)), row],
        out_specs=row, compiler_params=_params(1),
    )(proj, proj, b_gate2, ya, pool_scale, yb)


def _gate_bwd(name, dmix, proj, b_gate2, ya, pool_scale, yb, tm):
    _, lp, dm = proj.shape

    def body(dm_ref, ga_ref, gr_ref, b_ref, ya_ref, ps_ref, yb_ref, dp_ref, dyb_ref, dya_ref, db_ref, dps_ref):
        dmx = dm_ref[...].astype(F32)
        g_a = jax.nn.sigmoid(ga_ref[...].astype(F32) + b_ref[0:1, :])
        g_b = jax.nn.sigmoid(gr_ref[...].astype(F32) + b_ref[1:2, :])
        ya_pre = ya_ref[...].astype(F32)
        ybv = yb_ref[...].astype(F32)
        ps = ps_ref[...]
        dga = dmx * (ya_pre * ps) * (g_a * (1.0 - g_a))
        dgr = dmx * ybv * (g_b * (1.0 - g_b))
        dp_ref[0] = dga.astype(BF16)
        dp_ref[1] = dgr.astype(BF16)
        dyb_ref[...] = (dmx * g_b).astype(BF16)
        dya_ref[...] = (dmx * g_a * ps).astype(BF16)

        @pl.when(pl.program_id(0) == 0)
        def _():
            db_ref[...] = jnp.zeros_like(db_ref)
            dps_ref[...] = jnp.zeros_like(dps_ref)

        db_ref[0:1, :] += jnp.sum(dga, axis=0, keepdims=True)
        db_ref[1:2, :] += jnp.sum(dgr, axis=0, keepdims=True)
        dps_ref[0:1, :] += jnp.sum(dmx * g_a * ya_pre, axis=0, keepdims=True)

    row = pl.BlockSpec((tm, dm), lambda i: (i, 0))
    one = pl.BlockSpec((1, dm), lambda i: (0, 0))
    slab = pl.BlockSpec((SMALL_ROWS, dm), lambda i: (0, 0))
    return pl.pallas_call(
        body, name=name, grid=(lp // tm,),
        out_shape=(jax.ShapeDtypeStruct((6, lp, dm), BF16), jax.ShapeDtypeStruct((lp, dm), BF16),
                   jax.ShapeDtypeStruct((lp, dm), BF16), jax.ShapeDtypeStruct((SMALL_ROWS, dm), F32),
                   jax.ShapeDtypeStruct((SMALL_ROWS, dm), F32)),
        in_specs=[row, pl.BlockSpec((None, tm, dm), lambda i: (4, i, 0)), pl.BlockSpec((None, tm, dm), lambda i: (5, i, 0)),
                  pl.BlockSpec((2, dm), lambda i: (0, 0)), row, one, row],
        out_specs=(pl.BlockSpec((2, tm, dm), lambda i: (2, i, 0)), row, row, slab, slab),
        compiler_params=_params(1),
    )(dmix, proj, proj, b_gate2, ya, pool_scale, yb)


def _final_loss(name, h2, g3, target, tm):
    lp, dm = h2.shape
    nx = target.shape[0] // tm

    def body(h_ref, g_ref, t_ref, dh_ref, dhb_ref, ls_ref, dg_ref):
        i = pl.program_id(0)

        @pl.when(i == 0)
        def _():
            ls_ref[...] = jnp.zeros_like(ls_ref)
            dg_ref[...] = jnp.zeros_like(dg_ref)

        @pl.when(i < nx)
        def _():
            hv = h_ref[...]
            gv = g_ref[...]
            r = lax.rsqrt(jnp.mean(hv * hv, axis=-1, keepdims=True) + EPS)
            xhat = hv * r
            err = xhat * gv - t_ref[...]
            dout = err * (1.0 / dm)
            dxh = dout * gv
            dh = r * (dxh - xhat * jnp.mean(dxh * xhat, axis=-1, keepdims=True))
            dh_ref[...] = dh
            dhb_ref[...] = dh.astype(BF16)
            ls_ref[0:1, :] += jnp.sum(err * err, axis=0, keepdims=True)
            dg_ref[0:1, :] += jnp.sum(dout * xhat, axis=0, keepdims=True)

        @pl.when(i >= nx)
        def _():
            dh_ref[...] = jnp.zeros_like(dh_ref)
            dhb_ref[...] = jnp.zeros_like(dhb_ref)

    row = pl.BlockSpec((tm, dm), lambda i: (i, 0))
    slab = pl.BlockSpec((SMALL_ROWS, dm), lambda i: (0, 0))
    return pl.pallas_call(
        body, name=name, grid=(lp // tm,),
        out_shape=(jax.ShapeDtypeStruct((lp, dm), F32), jax.ShapeDtypeStruct((lp, dm), BF16),
                   jax.ShapeDtypeStruct((SMALL_ROWS, dm), F32), jax.ShapeDtypeStruct((SMALL_ROWS, dm), F32)),
        in_specs=[row, pl.BlockSpec((1, dm), lambda i: (0, 0)), pl.BlockSpec((tm, dm), lambda i: (jnp.minimum(i, nx - 1), 0))],
        out_specs=(row, row, slab, slab), compiler_params=_params(1),
    )(h2, g3, target)


def _shift(v, k):
    return pltpu.roll(v, k % v.shape[0], axis=0)


def _window_sum(v, group, sign):
    s2 = v + _shift(v, sign * 1)
    s4 = s2 + _shift(s2, sign * 2)
    s8 = s4 + _shift(s4, sign * 4)
    s16 = s8 + _shift(s8, sign * 8)
    return jnp.where(group == 0, s2, jnp.where(group == 1, s4, jnp.where(group == 2, s8, s16)))


def _pool_count(lp, group):
    row = lax.broadcasted_iota(jnp.int32, (lp, 1), 0)
    window = jnp.left_shift(2, group).astype(F32)
    meta_pos = (row - (lp - N_META) + 1).astype(F32)
    return jnp.where(row >= lp - N_META, jnp.minimum(meta_pos, window), window)


def _mixer_fwd(name, proj, conv_w, tc, dep):
    _, lp, dm = proj.shape
    per_group = dm // len(POOL_WINDOWS) // tc

    def body(u_ref, gb_ref, gc_ref, v_ref, cw_ref, _, p_ref, z_ref):
        group = pl.program_id(0) // per_group
        u = u_ref[...].astype(F32)
        p_ref[...] = (_window_sum(u, group, 1) / _pool_count(lp, group) - u).astype(BF16)
        cv = gc_ref[...].astype(F32) * v_ref[...].astype(F32)
        conv = cw_ref[0:1, :] * _shift(cv, 2) + cw_ref[1:2, :] * _shift(cv, 1) + cw_ref[2:3, :] * cv
        z_ref[...] = (gb_ref[...].astype(F32) * conv).astype(BF16)

    def seg(s):
        return pl.BlockSpec((None, lp, tc), lambda j: (s, 0, j))

    col = pl.BlockSpec((lp, tc), lambda j: (0, j))
    return pl.pallas_call(
        body, name=name, grid=(dm // tc,),
        out_shape=(jax.ShapeDtypeStruct((lp, dm), BF16), jax.ShapeDtypeStruct((lp, dm), BF16)),
        in_specs=[seg(0), seg(1), seg(2), seg(3), pl.BlockSpec((3, tc), lambda j: (0, j)), ANY],
        out_specs=(col, col), compiler_params=_params(1),
    )(proj, proj, proj, proj, conv_w, dep)


def _mixer_bwd(name, dz, dpooled, proj, conv_w, dproj, tc, dep):
    _, lp, dm = proj.shape
    per_group = dm // len(POOL_WINDOWS) // tc

    def body(dz_ref, dp_ref, gb_ref, gc_ref, v_ref, cw_ref, _, __, o_ref, dcw_ref):
        group = pl.program_id(0) // per_group
        dzv = dz_ref[...].astype(F32)
        gb = gb_ref[...].astype(F32)
        gc = gc_ref[...].astype(F32)
        vv = v_ref[...].astype(F32)
        cv = gc * vv
        c1 = _shift(cv, 1)
        c2 = _shift(cv, 2)
        w0, w1, w2 = cw_ref[0:1, :], cw_ref[1:2, :], cw_ref[2:3, :]
        o_ref[1] = (dzv * (w0 * c2 + w1 * c1 + w2 * cv)).astype(BF16)
        dconv = dzv * gb
        dcw_ref[...] = jnp.zeros_like(dcw_ref)
        dcw_ref[0:1, :] = jnp.sum(dconv * c2, axis=0, keepdims=True)
        dcw_ref[1:2, :] = jnp.sum(dconv * c1, axis=0, keepdims=True)
        dcw_ref[2:3, :] = jnp.sum(dconv * cv, axis=0, keepdims=True)
        dcv = w0 * _shift(dconv, -2) + w1 * _shift(dconv, -1) + w2 * dconv
        o_ref[2] = (dcv * vv).astype(BF16)
        o_ref[3] = (dcv * gc).astype(BF16)
        dpv = dp_ref[...].astype(F32)
        o_ref[0] = (_window_sum(dpv / _pool_count(lp, group), group, -1) - dpv).astype(BF16)

    def seg(s):
        return pl.BlockSpec((None, lp, tc), lambda j: (s, 0, j))

    col = pl.BlockSpec((lp, tc), lambda j: (0, j))
    return pl.pallas_call(
        body, name=name, grid=(dm // tc,),
        out_shape=(jax.ShapeDtypeStruct(dproj.shape, BF16), jax.ShapeDtypeStruct((SMALL_ROWS, dm), F32)),
        in_specs=[col, col, seg(1), seg(2), seg(3), pl.BlockSpec((3, tc), lambda j: (0, j)), ANY, ANY],
        out_specs=(pl.BlockSpec((4, lp, tc), lambda j: (0, 0, j)), pl.BlockSpec((SMALL_ROWS, tc), lambda j: (0, j))),
        input_output_aliases={6: 0}, compiler_params=_params(1),
    )(dz, dpooled, proj, proj, proj, conv_w, dproj, dep)


def _row_tile(r, c, bytes_per_row_elem=4, budget=2 * 1024 * 1024):
    best = None
    for t in range(16, r + 1, 16):
        if r % t == 0 and t * c * bytes_per_row_elem <= budget:
            best = t
    return best if best is not None else r


def _pair_add(name, g4, recv, core):
    s, r, c = g4.shape
    h = r // 2
    tr = _row_tile(h, c, budget=6 * 1024 * 1024)
    nb = h // tr

    def body(core_ref, g_ref, r_ref, o_ref):
        o_ref[...] = (g_ref[...].astype(F32) + r_ref[...].astype(F32)).astype(BF16)

    grid_spec = pltpu.PrefetchScalarGridSpec(
        num_scalar_prefetch=1, grid=(s, nb),
        in_specs=[pl.BlockSpec((None, tr, c), lambda si, j, core_ref: (si, core_ref[0] * nb + j, 0)),
                  pl.BlockSpec((None, tr, c), lambda si, j, core_ref: (si, j, 0))],
        out_specs=pl.BlockSpec((None, tr, c), lambda si, j, core_ref: (si, j, 0)))
    return pl.pallas_call(
        body, name=name, out_shape=jax.ShapeDtypeStruct((s, h, c), BF16), grid_spec=grid_spec,
        compiler_params=_params(2),
    )(core, g4, recv)


def _chip_sum(name, parts, recv, chip):
    _, h, c = parts.shape
    tr = _row_tile(h, c)

    def body(chip_ref, p_ref, r_ref, o_ref):
        acc = p_ref[...].astype(F32)
        for i in range(len(CHIP_FLIPS)):
            acc = acc + r_ref[i].astype(F32)
        o_ref[...] = acc

    grid_spec = pltpu.PrefetchScalarGridSpec(
        num_scalar_prefetch=1, grid=(h // tr,),
        in_specs=[pl.BlockSpec((None, tr, c), lambda j, chip_ref: (chip_ref[0], j, 0)),
                  pl.BlockSpec((len(CHIP_FLIPS), tr, c), lambda j, chip_ref: (0, j, 0))],
        out_specs=pl.BlockSpec((tr, c), lambda j, chip_ref: (j, 0)))
    return pl.pallas_call(
        body, name=name, out_shape=jax.ShapeDtypeStruct((h, c), F32), grid_spec=grid_spec, compiler_params=_params(1),
    )(chip, parts, recv)


def _adam_update(w, gv, m, v):
    c1 = 1.0 - ADAM_B1 ** ADAM_STEP
    c2 = 1.0 - ADAM_B2 ** ADAM_STEP
    nm = ADAM_B1 * m + (1.0 - ADAM_B1) * gv
    nv = ADAM_B2 * v + (1.0 - ADAM_B2) * (gv * gv)
    return -ADAM_LR * ((nm / c1) / (jnp.sqrt(nv / c2) + ADAM_EPS) + ADAM_WD * w), nm, nv


def _adamw_halves(name, w, g_own, g_sib, m, v, core, part=(0, 1), prev=None):
    r, c = w.shape
    rp = r // part[1]
    h = rp // 2
    tr = _row_tile(h, c, budget=2 * 1024 * 1024)
    nbh = h // tr
    j0 = part[0] * 2 * nbh
    n_prev = 0 if prev is None else 4

    def body(core_ref, w_ref, go_ref, gs_ref, m_ref, v_ref, *rest):
        g_ref, d_ref, nm_ref, nv_ref = rest[n_prev:]
        mine = (pl.program_id(0) // nbh) == core_ref[0]
        gv = jnp.where(mine, go_ref[...], gs_ref[...])
        g_ref[...] = gv
        d_ref[...], nm_ref[...], nv_ref[...] = _adam_update(w_ref[...], gv, m_ref[...], v_ref[...])

    def blk(fn):
        return pl.BlockSpec((tr, c), fn)

    full = blk(lambda j, core_ref: (j0 + j, 0))
    own = blk(lambda j, core_ref: (jnp.clip(j - core_ref[0] * nbh, 0, nbh - 1), 0))
    sib = blk(lambda j, core_ref: (jnp.clip(j - (1 - core_ref[0]) * nbh, 0, nbh - 1), 0))
    grid_spec = pltpu.PrefetchScalarGridSpec(
        num_scalar_prefetch=1, grid=(2 * nbh,), in_specs=[full, own, sib, full, full] + [ANY] * n_prev, out_specs=(full,) * 4)
    sds = jax.ShapeDtypeStruct((r, c), F32)
    return pl.pallas_call(
        body, name=name, out_shape=(sds,) * 4, grid_spec=grid_spec, compiler_params=_params(1),
        input_output_aliases={6 + i: i for i in range(n_prev)},
    )(core, w, g_own, g_sib, m, v, *(prev or ()))


def _adamw(name, w, g, m, v):
    r, c = w.shape

    def body(w_ref, g_ref, m_ref, v_ref, d_ref, nm_ref, nv_ref):
        d_ref[...], nm_ref[...], nv_ref[...] = _adam_update(w_ref[...], g_ref[...], m_ref[...], v_ref[...])

    blk = pl.BlockSpec((r, c), lambda j: (0, 0))
    sds = jax.ShapeDtypeStruct((r, c), F32)
    return pl.pallas_call(
        body, name=name, out_shape=(sds, sds, sds), grid=(1,), in_specs=[blk] * 4, out_specs=(blk,) * 3,
        compiler_params=_params(1),
    )(w, g, m, v)


def _cast_into_slot(name, w, chip, dtype, deps=()):
    r, c = w.shape
    tr = _row_tile(r, c)

    def body(chip_ref, w_ref, *rest):
        rest[-1][...] = w_ref[...].astype(dtype)

    grid_spec = pltpu.PrefetchScalarGridSpec(
        num_scalar_prefetch=1, grid=(r // tr,),
        in_specs=[pl.BlockSpec((tr, c), lambda j, chip_ref: (j, 0))] + [ANY] * len(deps),
        out_specs=pl.BlockSpec((None, tr, c), lambda j, chip_ref: (chip_ref[0], j, 0)))
    return pl.pallas_call(
        body, name=name, out_shape=jax.ShapeDtypeStruct((4, r, c), dtype), grid_spec=grid_spec, compiler_params=_params(1),
    )(chip, w, *deps)


def _place():
    return lax.axis_index("x"), lax.axis_index("y"), lax.axis_index("c")


def _chip_of(x, y, flip):
    px, py = x ^ flip[0], y ^ flip[1]
    return px, py, 2 * px + py


def _half(ref, which):
    rows = ref.shape[0] // 2
    return ref.at[pl.ds(which * rows, rows)]


HBM = pl.BlockSpec(memory_space=pltpu.HBM)
SEM = pl.BlockSpec(memory_space=pltpu.SEMAPHORE)
SPLIT_COPY = pltpu.CompilerParams(has_side_effects=pltpu.SideEffectType.DATAFLOW_SIDE_EFFECTING)


def _in_hbm(arrays):
    return [pltpu.with_memory_space_constraint(t, pltpu.HBM) for t in arrays]


TOKEN = jax.ShapeDtypeStruct((SMALL_ROWS, LANES), F32)
TOKEN_SPEC = pl.BlockSpec(memory_space=pltpu.VMEM)


NEIGHBOUR_FLIPS = CHIP_FLIPS[:2]


def _relay_chips(x, y, c):
    fx, fy = x ^ c, y ^ (1 - c)
    return (fx, fy), 2 * fx + fy, 2 * (1 - x) + (1 - y)


def _ag_start(name, slabs, deps=()):
    n = len(slabs)
    nn = len(NEIGHBOUR_FLIPS)

    def body(*refs):
        no = n + len(deps)
        ssem, rsem = refs[no], refs[no + 1]
        outs = refs[no + 2:no + 2 + n]
        token = refs[no + 2 + n]
        token[...] = jnp.zeros_like(token)
        x, y, c = _place()
        k = 2 * x + y
        for a in range(n):
            for j, flip in enumerate(NEIGHBOUR_FLIPS):
                px, py, _ = _chip_of(x, y, flip)
                mine = _half(outs[a].at[k], c)
                pltpu.make_async_remote_copy(src_ref=mine, dst_ref=mine, send_sem=ssem.at[a * nn + j],
                                             recv_sem=rsem.at[a * nn + j], device_id=(px, py, c), device_id_type=MESH).start()

    sem = pltpu.SemaphoreType.DMA((nn * n,))
    res = pl.pallas_call(
        body, name=name, out_shape=(sem, sem) + tuple(pltpu.HBM(t.shape, t.dtype) for t in slabs) + (TOKEN,),
        in_specs=[HBM] * n + [ANY] * len(deps), out_specs=tuple([SEM, SEM] + [HBM] * n + [TOKEN_SPEC]),
        input_output_aliases={a: 2 + a for a in range(n)}, compiler_params=SPLIT_COPY,
    )(*_in_hbm(slabs), *deps)
    return (res[0], res[1]), list(res[2:2 + n]), res[2 + n]


def _ag_relay(name, slabs, sems, after, then_start=()):
    n = len(slabs)
    m = len(then_start)
    nn = len(NEIGHBOUR_FLIPS)

    def body(*refs):
        no = n + 2 + m + len(after)
        ins = refs[:n]
        ssem, rsem = refs[n], refs[n + 1]
        r_s, r_r, p_s, p_r = refs[no:no + 4]
        x, y, c = _place()
        k = 2 * x + y
        (fx, fy), _, _ = _relay_chips(x, y, c)
        for a in range(n):
            for j, flip in enumerate(NEIGHBOUR_FLIPS):
                _, _, kj = _chip_of(x, y, flip)
                landed = _half(ins[a].at[kj], c)
                cp = pltpu.make_async_remote_copy(
                    src_ref=_half(ins[a].at[k], c), dst_ref=landed, send_sem=ssem.at[a * nn + j],
                    recv_sem=rsem.at[a * nn + j], device_id=(x, y, c), device_id_type=MESH)
                cp.wait_send()
                cp.wait_recv()
        for a in range(n):
            near = _half(ins[a].at[2 * (x ^ (1 - c)) + (y ^ c)], c)
            pltpu.make_async_remote_copy(src_ref=near, dst_ref=near, send_sem=r_s.at[a], recv_sem=r_r.at[a],
                                         device_id=(fx, fy, c), device_id_type=MESH).start()
            for j, flip in enumerate(NEIGHBOUR_FLIPS):
                _, _, kj = _chip_of(x, y, flip)
                landed = _half(ins[a].at[kj], c)
                pltpu.make_async_remote_copy(src_ref=landed, dst_ref=landed, send_sem=p_s.at[a * nn + j],
                                             recv_sem=p_r.at[a * nn + j], device_id=(x, y, 1 - c), device_id_type=MESH).start()
        if m:
            d_s, d_r = refs[no + 4 + n], refs[no + 5 + n]
            nxt = refs[no + 6 + n:]
            for a in range(m):
                for j, flip in enumerate(NEIGHBOUR_FLIPS):
                    px, py, _ = _chip_of(x, y, flip)
                    mine = _half(nxt[a].at[k], c)
                    pltpu.make_async_remote_copy(src_ref=mine, dst_ref=mine, send_sem=d_s.at[a * nn + j],
                                                 recv_sem=d_r.at[a * nn + j], device_id=(px, py, c), device_id_type=MESH).start()

    rsem_t = pltpu.SemaphoreType.DMA((n,))
    psem_t = pltpu.SemaphoreType.DMA((nn * n,))
    out_shape = (rsem_t, rsem_t, psem_t, psem_t) + tuple(pltpu.HBM(t.shape, t.dtype) for t in slabs)
    out_specs = [SEM] * 4 + [HBM] * n
    aliases = {a: 4 + a for a in range(n)}
    if m:
        dsem_t = pltpu.SemaphoreType.DMA((nn * m,))
        out_shape += (dsem_t, dsem_t) + tuple(pltpu.HBM(t.shape, t.dtype) for t in then_start)
        out_specs += [SEM, SEM] + [HBM] * m
        aliases.update({n + 2 + a: 4 + n + 2 + a for a in range(m)})
    res = pl.pallas_call(
        body, name=name, out_shape=out_shape, in_specs=[HBM] * n + [SEM, SEM] + [HBM] * m + [ANY] * len(after),
        out_specs=tuple(out_specs), input_output_aliases=aliases, compiler_params=SPLIT_COPY,
    )(*slabs, sems[0], sems[1], *_in_hbm(list(then_start)), *after)
    if not m:
        return tuple(res[:4]), list(res[4:])
    return (tuple(res[:4]), list(res[4:4 + n])), ((res[4 + n], res[5 + n]), list(res[6 + n:]))


def _wait_passes(ins, p_s, p_r, x, y, c):
    nn = len(NEIGHBOUR_FLIPS)
    for a in range(len(ins)):
        for j, flip in enumerate(NEIGHBOUR_FLIPS):
            _, _, kj = _chip_of(x, y, flip)
            cp = pltpu.make_async_remote_copy(
                src_ref=_half(ins[a].at[kj], c), dst_ref=_half(ins[a].at[kj], 1 - c), send_sem=p_s.at[a * nn + j],
                recv_sem=p_r.at[a * nn + j], device_id=(x, y, c), device_id_type=MESH)
            cp.wait_send()
            cp.wait_recv()


def _ag_relay_wait(name, slabs, sems, after):
    n = len(slabs)
    ns = len(sems)

    def body(*refs):
        no = n + ns + len(after)
        ins = refs[:n]
        r_s, r_r = refs[n], refs[n + 1]
        f_s, f_r = refs[no], refs[no + 1]
        x, y, c = _place()
        _, _, kd = _relay_chips(x, y, c)
        for a in range(n):
            near = _half(ins[a].at[2 * (x ^ (1 - c)) + (y ^ c)], c)
            cp = pltpu.make_async_remote_copy(src_ref=near, dst_ref=_half(ins[a].at[kd], c), send_sem=r_s.at[a],
                                              recv_sem=r_r.at[a], device_id=(x, y, c), device_id_type=MESH)
            cp.wait_send()
            cp.wait_recv()
        if ns == 4:
            _wait_passes(ins, refs[n + 2], refs[n + 3], x, y, c)
        for a in range(n):
            diag = _half(ins[a].at[kd], c)
            pltpu.make_async_remote_copy(src_ref=diag, dst_ref=diag, send_sem=f_s.at[a], recv_sem=f_r.at[a],
                                         device_id=(x, y, 1 - c), device_id_type=MESH).start()

    sem = pltpu.SemaphoreType.DMA((n,))
    res = pl.pallas_call(
        body, name=name, out_shape=(sem, sem) + tuple(pltpu.HBM(t.shape, t.dtype) for t in slabs),
        in_specs=[HBM] * n + [SEM] * ns + [ANY] * len(after), out_specs=tuple([SEM, SEM] + [HBM] * n),
        input_output_aliases={a: 2 + a for a in range(n)}, compiler_params=SPLIT_COPY,
    )(*slabs, *sems, *after)
    return (res[0], res[1]), list(res[2:])


def _ag_final_wait(name, slabs, sems, after):
    n = len(slabs)

    def body(*refs):
        ins = refs[:n]
        f_s, f_r = refs[n], refs[n + 1]
        x, y, c = _place()
        _, _, kd = _relay_chips(x, y, c)
        for a in range(n):
            cp = pltpu.make_async_remote_copy(
                src_ref=_half(ins[a].at[kd], c), dst_ref=_half(ins[a].at[kd], 1 - c), send_sem=f_s.at[a], recv_sem=f_r.at[a],
                device_id=(x, y, c), device_id_type=MESH)
            cp.wait_send()
            cp.wait_recv()

    return pl.pallas_call(
        body, name=name, out_shape=tuple(pltpu.HBM(t.shape, t.dtype) for t in slabs),
        in_specs=[HBM] * n + [SEM, SEM] + [ANY] * len(after), out_specs=tuple([HBM] * n),
        input_output_aliases={a: a for a in range(n)}, compiler_params=SPLIT_COPY,
    )(*slabs, sems[0], sems[1], *after)


def _sibling_part(ref, c, halves):
    if not halves:
        return ref
    h = ref.shape[1] // 2
    return ref.at[:, pl.ds((1 - c) * h, h)]


def _swap_start(name, grads, halves=True, deps=()):
    n = len(grads)

    def body(*refs):
        no = 2 * n + len(deps)
        ssem, rsem = refs[no], refs[no + 1]
        src, land = refs[no + 2:no + n + 2], refs[no + n + 2:no + 2 * n + 2]
        token = refs[no + 2 * n + 2]
        token[...] = jnp.zeros_like(token)
        x, y, c = _place()
        for a in range(n):
            pltpu.make_async_remote_copy(
                src_ref=_sibling_part(src[a], c, halves), dst_ref=land[a], send_sem=ssem.at[a], recv_sem=rsem.at[a],
                device_id=(x, y, 1 - c), device_id_type=MESH).start()

    zones = [lax.empty((g.shape[0], g.shape[1] // 2, g.shape[2]) if halves else g.shape, g.dtype) for g in grads]
    sem = pltpu.SemaphoreType.DMA((n,))
    res = pl.pallas_call(
        body, name=name,
        out_shape=(sem, sem) + tuple(pltpu.HBM(t.shape, t.dtype) for t in list(grads) + zones) + (TOKEN,),
        in_specs=[HBM] * (2 * n) + [ANY] * len(deps), out_specs=tuple([SEM, SEM] + [HBM] * (2 * n) + [TOKEN_SPEC]),
        input_output_aliases={i: 2 + i for i in range(2 * n)}, compiler_params=SPLIT_COPY,
    )(*_in_hbm(list(grads) + zones), *deps)
    return (res[0], res[1], list(res[2:2 + n]), list(res[2 + n:2 + 2 * n])), res[2 + 2 * n]


def _swap_wait(name, ssem, rsem, grads, zones, after, halves=True):
    n = len(grads)

    def body(*refs):
        src, land = refs[:n], refs[n:2 * n]
        ss, rs = refs[2 * n], refs[2 * n + 1]
        x, y, c = _place()
        for a in range(n):
            cp = pltpu.make_async_remote_copy(
                src_ref=_sibling_part(src[a], c, halves), dst_ref=land[a], send_sem=ss.at[a], recv_sem=rs.at[a],
                device_id=(x, y, c), device_id_type=MESH)
            cp.wait_send()
            cp.wait_recv()

    res = pl.pallas_call(
        body, name=name, out_shape=tuple(pltpu.HBM(t.shape, t.dtype) for t in list(grads) + list(zones)),
        in_specs=[HBM] * (2 * n) + [SEM, SEM] + [ANY] * len(after), out_specs=tuple([HBM] * (2 * n)),
        input_output_aliases={i: i for i in range(2 * n)}, compiler_params=SPLIT_COPY,
    )(*grads, *zones, ssem, rsem, *after)
    return list(res[:n]), list(res[n:])


def _scatter_start(name, parts):
    n = len(parts)
    nf = len(CHIP_FLIPS)

    def body(*refs):
        ssem, rsem = refs[2 * n], refs[2 * n + 1]
        src, land = refs[2 * n + 2:3 * n + 2], refs[3 * n + 2:4 * n + 2]
        token = refs[4 * n + 2]
        token[...] = jnp.zeros_like(token)
        x, y, c = _place()
        for a in range(n):
            for j, flip in enumerate(CHIP_FLIPS):
                px, py, kj = _chip_of(x, y, flip)
                pltpu.make_async_remote_copy(
                    src_ref=src[a].at[kj], dst_ref=land[a].at[j], send_sem=ssem.at[a * nf + j], recv_sem=rsem.at[a * nf + j],
                    device_id=(px, py, c), device_id_type=MESH).start()

    zones = [lax.empty((nf,) + p.shape[1:], p.dtype) for p in parts]
    sem = pltpu.SemaphoreType.DMA((nf * n,))
    res = pl.pallas_call(
        body, name=name,
        out_shape=(sem, sem) + tuple(pltpu.HBM(t.shape, t.dtype) for t in list(parts) + zones)
        + (jax.ShapeDtypeStruct((SMALL_ROWS, LANES), F32),),
        in_specs=[HBM] * (2 * n),
        out_specs=tuple([SEM, SEM] + [HBM] * (2 * n) + [pl.BlockSpec(memory_space=pltpu.VMEM)]),
        input_output_aliases={i: 2 + i for i in range(2 * n)}, compiler_params=SPLIT_COPY,
    )(*_in_hbm(list(parts) + zones))
    return (res[0], res[1], list(res[2:2 + n]), list(res[2 + n:2 + 2 * n])), res[2 + 2 * n]


def _scatter_wait(name, ssem, rsem, parts, zones, after):
    n = len(parts)
    nf = len(CHIP_FLIPS)

    def body(*refs):
        src, land = refs[:n], refs[n:2 * n]
        ss, rs = refs[2 * n], refs[2 * n + 1]
        x, y, c = _place()
        for a in range(n):
            for j, flip in enumerate(CHIP_FLIPS):
                _, _, kj = _chip_of(x, y, flip)
                cp = pltpu.make_async_remote_copy(
                    src_ref=src[a].at[kj], dst_ref=land[a].at[j], send_sem=ss.at[a * nf + j], recv_sem=rs.at[a * nf + j],
                    device_id=(x, y, c), device_id_type=MESH)
                cp.wait_send()
                cp.wait_recv()

    res = pl.pallas_call(
        body, name=name, out_shape=tuple(pltpu.HBM(t.shape, t.dtype) for t in list(parts) + list(zones)),
        in_specs=[HBM] * (2 * n) + [SEM, SEM] + [ANY] * len(after), out_specs=tuple([HBM] * (2 * n)),
        input_output_aliases={i: i for i in range(2 * n)}, compiler_params=SPLIT_COPY,
    )(*parts, *zones, ssem, rsem, *after)
    return list(res[:n]), list(res[n:])


N_PEERS = 7


def _peer(x, y, c, mask):
    px, py, pc = x ^ ((mask >> 2) & 1), y ^ ((mask >> 1) & 1), c ^ (mask & 1)
    return (px, py, pc), 4 * px + 2 * py + pc


def _reduce_start(vec, deps):
    nd = len(deps)

    def body(*refs):
        ssem, rsem, src, land, token = refs[2 + nd:]
        token[...] = jnp.zeros_like(token)
        x, y, c = _place()
        me = 4 * x + 2 * y + c
        for mask in range(1, N_PEERS + 1):
            to, _ = _peer(x, y, c, mask)
            pltpu.make_async_remote_copy(src_ref=src, dst_ref=land.at[me], send_sem=ssem.at[mask - 1],
                                         recv_sem=rsem.at[mask - 1], device_id=to, device_id_type=MESH).start()

    zone = lax.empty((N_PEERS + 1,) + vec.shape, vec.dtype)
    sem = pltpu.SemaphoreType.DMA((N_PEERS,))
    res = pl.pallas_call(
        body, name="reduce_start",
        out_shape=(sem, sem, pltpu.HBM(vec.shape, vec.dtype), pltpu.HBM(zone.shape, zone.dtype), TOKEN),
        in_specs=[HBM, HBM] + [ANY] * nd, out_specs=(SEM, SEM, HBM, HBM, TOKEN_SPEC),
        input_output_aliases={0: 2, 1: 3}, compiler_params=SPLIT_COPY,
    )(*_in_hbm([vec, zone]), *deps)
    return res[:4], res[4]


def _reduce_wait(ssem, rsem, vec, zone, after):
    def body(src, land, ss, rs, *_):
        x, y, c = _place()
        for mask in range(1, N_PEERS + 1):
            _, frm = _peer(x, y, c, mask)
            cp = pltpu.make_async_remote_copy(src_ref=src, dst_ref=land.at[frm], send_sem=ss.at[mask - 1],
                                              recv_sem=rs.at[mask - 1], device_id=(x, y, c), device_id_type=MESH)
            cp.wait_send()
            cp.wait_recv()

    return pl.pallas_call(
        body, name="reduce_wait", out_shape=(pltpu.HBM(vec.shape, vec.dtype), pltpu.HBM(zone.shape, zone.dtype)),
        in_specs=[HBM, HBM, SEM, SEM] + [ANY] * len(after), out_specs=(HBM, HBM),
        input_output_aliases={0: 0, 1: 1}, compiler_params=SPLIT_COPY,
    )(vec, zone, ssem, rsem, *after)


def _reduce_sum(vec, zone, me, loss_row, loss_scale):
    r, dm = vec.shape

    def body(me_ref, v_ref, z_ref, o_ref, l_ref):
        acc = None
        for i in range(N_PEERS + 1):
            term = jnp.where(me_ref[0] == i, v_ref[...], z_ref[i])
            acc = term if acc is None else acc + term
        o_ref[...] = acc
        l_ref[...] = jnp.sum(acc[loss_row:loss_row + SMALL_ROWS, :], axis=(0, 1), keepdims=True) * loss_scale

    grid_spec = pltpu.PrefetchScalarGridSpec(
        num_scalar_prefetch=1, grid=(1,),
        in_specs=[pl.BlockSpec((r, dm), lambda i, me_ref: (0, 0)), pl.BlockSpec((N_PEERS + 1, r, dm), lambda i, me_ref: (0, 0, 0))],
        out_specs=(pl.BlockSpec((r, dm), lambda i, me_ref: (0, 0)), pl.BlockSpec((1, 1), lambda i, me_ref: (0, 0))))
    return pl.pallas_call(
        body, name="reduce_sum", out_shape=(jax.ShapeDtypeStruct((r, dm), F32), jax.ShapeDtypeStruct((1, 1), F32)),
        grid_spec=grid_spec, compiler_params=_params(1),
    )(me, vec, zone)


def kernel(x, meta_tokens, norm_mix_g, w_in, b_gate, pool_w, pool_scale, conv_w, conv_out_w, w_o, norm_ffn_g, w_gate_up, w_down, norm_final_g, loss_target, m_meta_tokens, m_norm_mix_g, m_w_in, m_b_gate, m_pool_w, m_pool_scale, m_conv_w, m_conv_out_w, m_w_o, m_norm_ffn_g, m_w_gate_up, m_w_down, m_norm_final_g, v_meta_tokens, v_norm_mix_g, v_w_in, v_b_gate, v_pool_w, v_pool_scale, v_conv_w, v_conv_out_w, v_w_o, v_norm_ffn_g, v_w_gate_up, v_w_down, v_norm_final_g):
    seq, dm = x.shape[1], x.shape[2]
    tail = TAIL_ROWS
    tm = tail
    lp = seq + tail
    tm_row = _row_tile(lp, dm, 4, 3 * 1024 * 1024)
    n_chips = 4
    n_groups = len(POOL_WINDOWS)
    gw = dm // n_groups
    tc = min(256, gw)
    cx, cy, cc = _place()
    chip = 2 * cx + cy
    dloc = dm // n_chips

    pool2 = pool_w.reshape(n_groups * pool_w.shape[1], gw)
    big = {"w_in": w_in, "w_gate_up": w_gate_up, "pool_w": pool2, "conv_out_w": conv_out_w, "w_o": w_o, "w_down": w_down}
    chip1 = jnp.reshape(chip, (1,)).astype(jnp.int32)
    core = jnp.reshape(cc, (1,)).astype(jnp.int32)
    small_loc = jnp.concatenate([meta_tokens, jnp.pad(conv_w, ((0, 8 - conv_w.shape[0]), (0, 0))),
                                 jnp.zeros((8, dloc), F32)], axis=0)
    g1, g2, g3 = norm_mix_g.reshape(1, dm), norm_ffn_g.reshape(1, dm), norm_final_g.reshape(1, dm)
    b_gate2 = b_gate.reshape(2, dm)
    ps = pool_scale.reshape(1, dm)
    first = [_cast_into_slot("cast_w_in", w_in, chip1, BF16), _cast_into_slot("place_small", small_loc, chip1, F32)]
    sems, first, token = _ag_start("ag_start_first", first)
    cast = {nme: _cast_into_slot("cast_" + nme, big[nme], chip1, BF16, deps=(token,))
            for nme in ["pool_w", "conv_out_w", "w_o", "w_gate_up", "w_down"]}
    sems, first = _ag_relay("ag_relay_first", first, sems, list(cast.values()))
    sems, first = _ag_relay_wait("ag_relay_wait_first", first, sems, [])
    w_in4, small4 = _ag_final_wait("ag_final_wait_first", first, sems, [])
    mixer_w = [cast["pool_w"], cast["conv_out_w"], cast["w_o"]]
    sems_mix, mixer_w, token = _ag_start("ag_start_mixer", mixer_w, deps=(w_in4,))
    sems_gu, (w_gu4,), token = _ag_start("ag_start_gate_up", [cast["w_gate_up"]], deps=(token,))

    small_f = jnp.transpose(small4, (1, 0, 2)).reshape(small4.shape[1], dm)
    meta_f = small_f[:N_META]
    conv_w_f = small_f[N_META:N_META + 3]
    tail_rows = jnp.concatenate([jnp.zeros((tail - N_META, dm), F32), meta_f], axis=0)
    h0, hn1 = _rms_fwd_input("rms_mix", x[0], tail_rows, g1, deps=(token,))
    proj = _nn_sharded("proj", hn1, w_in4, 6)
    sems_mix, mixer_w = _ag_relay("ag_relay_mixer", mixer_w, sems_mix, [proj])
    (sems_gu, (w_gu4,)), (sems_down, (w_down4,)) = _ag_relay("ag_relay_gate_up", [w_gu4], sems_gu, [mixer_w[0]],
                                                              then_start=[cast["w_down"]])
    pooled, z = _mixer_fwd("mixer_fwd", proj, conv_w_f, tc, w_down4)
    sems_mix, mixer_w = _ag_relay_wait("ag_relay_wait_mixer", mixer_w, sems_mix, [pooled])
    pool4, conv_out4, w_o4 = _ag_final_wait("ag_final_wait_mixer", mixer_w, sems_mix, [])
    pool_f = jnp.transpose(pool4.reshape(n_chips, n_groups, gw // n_chips, gw), (1, 0, 2, 3)).reshape(n_groups, gw, gw)
    conv_out_f = conv_out4.reshape(dm, dm)
    w_o_f = w_o4.reshape(dm, dm)
    ya = _pool_fwd("pool_proj", pooled, pool_f)
    yb = _nn_plain("conv_out", z, conv_out_f, BF16)
    mix = _gate_mix("gate_mix", proj, b_gate2, ya, ps, yb, tm_row)
    sems_gu, (w_gu4,) = _ag_relay_wait("ag_relay_wait_gate_up", [w_gu4], sems_gu, [mix])
    h1 = _nn_plain("attn_out", mix, w_o_f, F32, res=h0, tn_pref=256)
    (w_gu4,) = _ag_final_wait("ag_final_wait_gate_up", [w_gu4], sems_gu, [h1])
    hn2 = _rms_fwd("rms_ffn", h1, g2, tm_row)
    sems_down, (w_down4,) = _ag_relay("ag_relay_down", [w_down4], sems_down, [hn2])
    gu, act = _gate_up_swiglu("gate_up", hn2, w_gu4, w_down4)
    sems_down, (w_down4,) = _ag_relay_wait("ag_relay_wait_down", [w_down4], sems_down, [act])
    (w_down4,) = _ag_final_wait("ag_final_wait_down", [w_down4], sems_down, [])
    w_down_f = w_down4.reshape(-1, dm)
    h2 = _nn_rows("ffn_down", act, w_down_f, h1)
    dh2, dh2b, loss_cols, dg3 = _final_loss("final_loss", h2, g3, loss_target[0], tm)

    def scatter(tag, names_g, swap, after):
        grads_g, got = _swap_wait("swap_wait_" + tag, *swap, [after])
        pairs = [_pair_add("pair_add_" + nme, g4, rv, core) for nme, g4, rv in zip(names_g, grads_g, got)]
        return _scatter_start("scatter_start_" + tag, pairs)

    dgu = _dact_swiglu_bwd("d_gate_up", dh2b, w_down_f, gu)
    gw_down = _tn_plain("dw_down", act, dh2b)
    gw_gu = _tn_sharded("dw_gate_up", hn2, dgu, n_chips)
    swap_a, token = _swap_start("swap_start_a", [gw_gu, gw_down.reshape(n_chips, -1, dm)])
    dhn2 = _nt_sharded("d_hn2", dgu, w_gu4, tr_pref=2816, row_tiles=2, deps=(token,))
    flight_a, token = scatter("a", ["w_gate_up", "w_down"], swap_a, dhn2)
    dh1, dh1b, dg2 = _rms_bwd("rms_ffn_bwd", dhn2, h1, g2, dh2, tm_row, token)
    dmix = _nt_plain("d_mix", dh1b, w_o_f)
    gw_o = _tn_plain("dw_o", mix, dh1b)
    dproj, dyb, dya, db_gate, dps = _gate_bwd("gate_bwd", dmix, proj, b_gate2, ya, ps, yb, tm_row)
    gw_conv_out = _tn_plain("dw_conv_out", z, dyb)
    gw_pool = _pool_bwd_w("dw_pool", pooled, dya)
    gw_pool = jnp.transpose(gw_pool.reshape(n_groups, n_chips, gw // n_chips, gw), (1, 0, 2, 3))
    dpooled = _pool_bwd_act("d_pooled", dya, pool_f)
    dz = _nt_plain("d_z", dyb, conv_out_f)
    dproj, dconv_w = _mixer_bwd("mixer_bwd", dz, dpooled, proj, conv_w_f, dproj, tc, dpooled)
    gw_in0 = _tn_sharded("dw_in_0", hn1, dproj, n_chips, part=(0, 2))
    swap_b, token = _swap_start("swap_start_b", [gw_o.reshape(n_chips, dloc, dm), gw_conv_out.reshape(n_chips, dloc, dm),
                                                 gw_pool.reshape(n_chips, n_groups * (gw // n_chips), gw), gw_in0])
    gw_in1 = _tn_sharded("dw_in_1", hn1, dproj, n_chips, part=(1, 2), deps=(token,))
    flight_b, token = scatter("b", ["w_o", "conv_out_w", "pool_w", "w_in_0"], swap_b, gw_in1)
    swap_c, token = _swap_start("swap_start_c", [gw_in1], deps=(token,))
    dhn1 = _nt_in_proj("d_hn1", dproj, w_in4, deps=(token,))
    flight_c, token = scatter("c", ["w_in_1"], swap_c, dhn1)
    dx, dtail, dg1 = _rms_bwd_input("rms_mix_bwd", dhn1, h0, g1, dh1, tm, seq, token)
    grad_x = dx[None]
    dmeta = dtail[tail - N_META:]

    given = dict(meta_tokens=(meta_tokens, m_meta_tokens, v_meta_tokens), norm_mix_g=(norm_mix_g, m_norm_mix_g, v_norm_mix_g),
                 w_in=(w_in, m_w_in, v_w_in), b_gate=(b_gate, m_b_gate, v_b_gate), pool_w=(pool_w, m_pool_w, v_pool_w),
                 pool_scale=(pool_scale, m_pool_scale, v_pool_scale), conv_w=(conv_w, m_conv_w, v_conv_w),
                 conv_out_w=(conv_out_w, m_conv_out_w, v_conv_out_w), w_o=(w_o, m_w_o, v_w_o),
                 norm_ffn_g=(norm_ffn_g, m_norm_ffn_g, v_norm_ffn_g), w_gate_up=(w_gate_up, m_w_gate_up, v_w_gate_up),
                 w_down=(w_down, m_w_down, v_w_down), norm_final_g=(norm_final_g, m_norm_final_g, v_norm_final_g))
    order = list(given.keys())
    grad, delta, new_m, new_v = {}, {}, {}, {}
    vec = jnp.concatenate([dg1, dg2, dg3, db_gate, dps, loss_cols, dconv_w, dmeta], axis=0)
    loss_row = 5 * SMALL_ROWS
    groups_g = {"a": [("w_gate_up", (0, 1)), ("w_down", (0, 1))],
                "b": [("w_o", (0, 1)), ("conv_out_w", (0, 1)), ("pool_w", (0, 1)), ("w_in", (0, 2))], "c": [("w_in", (1, 2))]}
    results = {}

    def reduced(tag, flight, after):
        pairs, zones = _scatter_wait("scatter_wait_" + tag, *flight, after)
        halves = [_chip_sum("chip_sum_%s_%d" % (nme, part[0]), p, rv, chip1) for (nme, part), p, rv in zip(groups_g[tag], pairs, zones)]
        return _swap_start("send_start_" + tag, halves, halves=False)

    def update(tag, send, after):
        halves, sib_halves = _swap_wait("send_wait_" + tag, *send, after, halves=False)
        deltas = []
        for (nme, part), g_own, g_sib in zip(groups_g[tag], halves, sib_halves):
            w, m, v = given[nme]
            shape2 = (2 * g_own.shape[0] * part[1], g_own.shape[1])
            results[nme] = _adamw_halves("adamw_%s_%d" % (nme, part[0]), w.reshape(shape2), g_own, g_sib, m.reshape(shape2),
                                         v.reshape(shape2), core, part=part, prev=results.get(nme))
            grad[nme], delta[nme], new_m[nme], new_v[nme] = [t.reshape(w.shape) for t in results[nme]]
            deltas.append(results[nme][1])
        return deltas

    send_a, token = reduced("a", flight_a, [dx])
    send_b, token = reduced("b", flight_b, [token])
    done_a = update("a", send_a, [token])
    send_c, token = reduced("c", flight_c, done_a)
    me1 = jnp.reshape(4 * cx + 2 * cy + cc, (1,)).astype(jnp.int32)
    red_flight, token = _reduce_start(vec, [token])
    done_b = update("b", send_b, [token])
    done_c = update("c", send_c, done_b)
    red, loss11 = _reduce_sum(*_reduce_wait(*red_flight, done_c), me1, loss_row, 0.5 / dm)
    loss = loss11[0, 0]
    col0 = chip * dloc
    g_small = {
        "norm_mix_g": red[0], "norm_ffn_g": red[SMALL_ROWS], "norm_final_g": red[2 * SMALL_ROWS],
        "b_gate": red[3 * SMALL_ROWS:3 * SMALL_ROWS + 2].reshape(-1), "pool_scale": red[4 * SMALL_ROWS],
        "conv_w": lax.dynamic_slice(red, (6 * SMALL_ROWS, col0), (3, dloc)),
        "meta_tokens": lax.dynamic_slice(red, (7 * SMALL_ROWS, col0), (N_META, dloc)),
    }

    vec_names = ["norm_mix_g", "norm_ffn_g", "norm_final_g", "pool_scale"]

    def slab_vec(pick):
        rows = [pick(nme).reshape(1, dm) for nme in vec_names] + [pick("b_gate").reshape(2, dm), jnp.zeros((2, dm), F32)]
        return jnp.concatenate(rows, axis=0)

    def slab_col(pick):
        return jnp.concatenate([pick("meta_tokens"), pick("conv_w"), jnp.zeros((5, dloc), F32)], axis=0)

    for slab, tag in ((slab_vec, "vec"), (slab_col, "col")):
        d, nm, nv = _adamw("adamw_small_" + tag, slab(lambda nme: given[nme][0]), slab(lambda nme: g_small[nme]),
                           slab(lambda nme: given[nme][1]), slab(lambda nme: given[nme][2]))
        for out, res in ((delta, d), (new_m, nm), (new_v, nv)):
            if tag == "vec":
                for i, nme in enumerate(vec_names):
                    out[nme] = res[i]
                out["b_gate"] = res[4:6].reshape(-1)
            else:
                out["meta_tokens"] = res[:N_META]
                out["conv_w"] = res[N_META:N_META + 3]
    grad.update(g_small)
    return (loss, grad_x, *[grad[nme] for nme in order], *[delta[nme] for nme in order],
            *[new_m[nme] for nme in order], *[new_v[nme] for nme in order])
```

```python
import math

import jax
import jax.numpy as jnp
from jax import lax
from jax.experimental import pallas as pl
from jax.experimental.pallas import tpu as pltpu

F32 = jnp.float32
BF16 = jnp.bfloat16
N_META = 16
POOL_WINDOWS = (2, 4, 8, 16)
EPS = 1e-6
ADAM_LR, ADAM_B1, ADAM_B2, ADAM_EPS, ADAM_WD, ADAM_STEP = 0.001, 0.9, 0.999, 1e-08, 0.01, 10
LANES = 128
V7X_VMEM_BYTES = 64 * 1024 * 1024
VMEM_LIMIT = V7X_VMEM_BYTES - 8 * 1024 * 1024
MESH = pl.DeviceIdType.MESH
ANY = pl.BlockSpec(memory_space=pl.ANY)
CHIP_FLIPS = ((1, 0), (0, 1), (1, 1))
SMALL_ROWS = 8
TAIL_ROWS = 64


def _pick(n, pref):
    best = None
    for t in range(LANES, min(n, pref) + 1, LANES):
        if n % t == 0:
            best = t
    assert best is not None, (n, pref)
    return best


def _params(n_axes=0):
    sem = ("arbitrary",) * n_axes if n_axes else None
    return pltpu.CompilerParams(dimension_semantics=sem, vmem_limit_bytes=VMEM_LIMIT)


_DIMS = {
    "nn": (((1,), (0,)), ((), ())),
    "nt": (((1,), (1,)), ((), ())),
    "tn": (((0,), (0,)), ((), ())),
}


def _matmul(name, mode, a, b, out_sds, grid, a_spec, b_spec, o_spec, nk, res=None, res_spec=None, acc_shape=None, deps=()):
    out_dtype = out_sds.dtype
    in_place = nk > 1 and out_dtype == F32
    use_scratch = nk > 1 and not in_place
    rows = a_spec.block_shape[-2] if mode != "tn" else None
    chunk = _row_tile(rows, 1, 1, 1152) if rows is not None else None
    n_in = 2 + (res is not None) + len(deps)

    def body(*refs):
        a_ref, b_ref = refs[:2]
        r_ref = refs[2] if res is not None else None
        o_ref, *scr = refs[n_in:]
        k = pl.program_id(len(grid) - 1) if nk > 1 else None

        def emit(sl):
            if sl is None:
                part = lax.dot_general(a_ref[...], b_ref[...], _DIMS[mode], preferred_element_type=F32)
                idx = (slice(None), slice(None))
            else:
                part = lax.dot_general(a_ref[sl, :], b_ref[...], _DIMS[mode], preferred_element_type=F32)
                idx = (sl, slice(None))
            if nk == 1:
                if r_ref is not None:
                    part = part + r_ref[idx]
                o_ref[idx] = part.astype(out_dtype)
                return
            acc = scr[0] if use_scratch else o_ref

            @pl.when(k == 0)
            def _():
                first = part
                if r_ref is not None and in_place:
                    first = first + r_ref[idx]
                acc[idx] = first

            @pl.when(k > 0)
            def _():
                acc[idx] += part

            if use_scratch:

                @pl.when(k == nk - 1)
                def _():
                    o_ref[idx] = acc[idx].astype(out_dtype)

        if mode == "tn" or chunk == rows:
            emit(None)
        else:
            for m0 in range(0, rows, chunk):
                emit(pl.ds(m0, chunk))

    ins = [a, b] + ([res] if res is not None else []) + list(deps)
    in_specs = [a_spec, b_spec] + ([res_spec] if res is not None else []) + [ANY] * len(deps)
    scratch = [pltpu.VMEM(acc_shape, F32)] if use_scratch else []
    return pl.pallas_call(
        body, name=name, out_shape=out_sds, grid=grid, in_specs=in_specs, out_specs=o_spec,
        scratch_shapes=scratch, compiler_params=_params(len(grid)),
    )(*ins)


def _nn_sharded(name, a, w4, nseg):
    lp, kdim = a.shape
    s, _, nloc = w4.shape
    segw = s * nloc // nseg
    tn = _pick(math.gcd(nloc, segw), 1536)
    bw, bo = nloc // tn, segw // tn
    return _matmul(
        name, "nn", a, w4, jax.ShapeDtypeStruct((nseg, lp, segw), BF16), (s * bw,),
        pl.BlockSpec((lp, kdim), lambda j: (0, 0)),
        pl.BlockSpec((None, kdim, tn), lambda j: (j // bw, 0, j % bw)),
        pl.BlockSpec((None, lp, tn), lambda j: (j // bo, 0, j % bo)), 1)


def _nt_in_proj(name, dseg, w4, row_tiles=2, to_pref=1024, deps=()):
    nseg, lp, segw = dseg.shape
    s, kdim, nloc = w4.shape
    assert nseg * segw == s * nloc and 2 * nloc == 3 * segw, (dseg.shape, w4.shape)
    half = segw // 2
    to = _pick(kdim, to_pref)
    tm = lp // row_tiles

    def body(full_ref, half_ref, w_ref, *rest):
        o_ref = rest[len(deps)]
        r = pl.program_id(2)

        def contribution(full_first):
            lo, hi = (pl.ds(0, segw), pl.ds(segw, half)) if full_first else (pl.ds(half, segw), pl.ds(0, half))
            return (lax.dot_general(full_ref[...], w_ref[:, lo], _DIMS["nt"], preferred_element_type=F32)
                    + lax.dot_general(half_ref[...], w_ref[:, hi], _DIMS["nt"], preferred_element_type=F32))

        @pl.when(r == 0)
        def _():
            o_ref[...] = contribution(True)

        for ri in range(1, s):

            @pl.when(r == ri)
            def _(ri=ri):
                o_ref[...] += contribution(ri % 2 == 0)

    return pl.pallas_call(
        body, name=name, out_shape=jax.ShapeDtypeStruct((lp, kdim), F32), grid=(row_tiles, kdim // to, s),
        in_specs=[pl.BlockSpec((None, tm, segw), lambda m, j, r: ((3 * r + 1) // 2, m, 0)),
                  pl.BlockSpec((None, tm, half), lambda m, j, r: (1 + 3 * (r // 2), m, r % 2)),
                  pl.BlockSpec((None, to, nloc), lambda m, j, r: (r, j, 0))] + [ANY] * len(deps),
        out_specs=pl.BlockSpec((tm, to), lambda m, j, r: (m, j)), compiler_params=_params(3),
    )(dseg, dseg, w4, *deps)


def _nn_plain(name, a, w, out_dtype, res=None, tn_pref=512, tk_pref=2048, deps=()):
    lp, kdim = a.shape
    n = w.shape[1]
    tn = _pick(n, tn_pref)
    tk = kdim if kdim <= tk_pref else _pick(kdim, tk_pref)
    nk = kdim // tk
    grid = (n // tn, nk) if nk > 1 else (n // tn,)
    if nk > 1:
        a_spec = pl.BlockSpec((lp, tk), lambda j, k: (0, k))
        w_spec = pl.BlockSpec((tk, tn), lambda j, k: (k, j))
        o_spec = pl.BlockSpec((lp, tn), lambda j, k: (0, j))
    else:
        a_spec = pl.BlockSpec((lp, tk), lambda j: (0, 0))
        w_spec = pl.BlockSpec((tk, tn), lambda j: (0, j))
        o_spec = pl.BlockSpec((lp, tn), lambda j: (0, j))
    return _matmul(name, "nn", a, w, jax.ShapeDtypeStruct((lp, n), out_dtype), grid, a_spec, w_spec, o_spec, nk,
                   res=res, res_spec=o_spec if res is not None else None, acc_shape=(lp, tn), deps=deps)


def _nt_plain(name, a, w, tn_pref=512):
    lp, kdim = a.shape
    n = w.shape[0]
    tn = _pick(n, tn_pref)
    return _matmul(
        name, "nt", a, w, jax.ShapeDtypeStruct((lp, n), BF16), (n // tn,),
        pl.BlockSpec((lp, kdim), lambda j: (0, 0)),
        pl.BlockSpec((tn, kdim), lambda j: (j, 0)),
        pl.BlockSpec((lp, tn), lambda j: (0, j)), 1)


def _nt_sharded(name, dseg, w4, to_pref=1024, tr_pref=1536, row_tiles=1, deps=()):
    nseg, lp, segw = dseg.shape
    s, kdim, nloc = w4.shape
    tr = _pick(math.gcd(nloc, segw), tr_pref)
    ba, bw = segw // tr, nloc // tr
    nr = s * bw
    to = _pick(kdim, to_pref)
    tm = lp // row_tiles
    return _matmul(
        name, "nt", dseg, w4, jax.ShapeDtypeStruct((lp, kdim), F32), (row_tiles, kdim // to, nr),
        pl.BlockSpec((None, tm, tr), lambda m, j, r: (r // ba, m, r % ba)),
        pl.BlockSpec((None, to, tr), lambda m, j, r: (r // bw, j, r % bw)),
        pl.BlockSpec((tm, to), lambda m, j, r: (m, j)), nr, deps=deps)


def _nn_rows(name, a, w, res, row_tiles=2, tn_pref=512):
    lp, kdim = a.shape
    n = w.shape[1]
    tn = _pick(n, tn_pref)
    tm = lp // row_tiles
    blk = pl.BlockSpec((tm, tn), lambda i, j: (i, j))
    return _matmul(name, "nn", a, w, jax.ShapeDtypeStruct((lp, n), F32), (row_tiles, n // tn),
                   pl.BlockSpec((tm, kdim), lambda i, j: (i, 0)), pl.BlockSpec((kdim, tn), lambda i, j: (0, j)), blk, 1,
                   res=res, res_spec=blk)


def _tn_plain(name, a, d, tk_pref=1024):
    lp, kdim = a.shape
    n = d.shape[1]
    tk = _pick(kdim, tk_pref)
    return _matmul(
        name, "tn", a, d, jax.ShapeDtypeStruct((kdim, n), BF16), (kdim // tk,),
        pl.BlockSpec((lp, tk), lambda i: (0, i)),
        pl.BlockSpec((lp, n), lambda i: (0, 0)),
        pl.BlockSpec((tk, n), lambda i: (i, 0)), 1)


def _tn_sharded(name, a, dseg, s, part=(0, 1), tk_pref=1024, deps=()):
    lp, kdim = a.shape
    nseg, _, segw = dseg.shape
    nloc = nseg * segw // s
    tn = _pick(math.gcd(nloc, segw), 1536)
    bd, bo = segw // tn, nloc // tn
    kpart = kdim // part[1]
    tk = _pick(kpart, tk_pref)
    i0 = part[0] * (kpart // tk)

    def body(a_ref, d_ref, *rest):
        o_ref, at_ref = rest[len(deps):]

        @pl.when(pl.program_id(1) == 0)
        def _():
            at_ref[...] = a_ref[...].T

        o_ref[...] = jnp.dot(at_ref[...], d_ref[...], preferred_element_type=F32).astype(BF16)

    return pl.pallas_call(
        body, name=name, out_shape=jax.ShapeDtypeStruct((s, kpart, nloc), BF16), grid=(kpart // tk, s * bo),
        in_specs=[pl.BlockSpec((lp, tk), lambda i, j: (0, i0 + i)),
                  pl.BlockSpec((None, lp, tn), lambda i, j: (j // bd, 0, j % bd))] + [ANY] * len(deps),
        out_specs=pl.BlockSpec((None, tk, tn), lambda i, j: (j // bo, i, j % bo)),
        scratch_shapes=[pltpu.VMEM((tk, lp), BF16)], compiler_params=_params(2),
    )(a, dseg, *deps)


def _silu_parts(gt):
    sg = jax.nn.sigmoid(gt)
    return gt * sg, sg * (1.0 + gt * (1.0 - sg))


def _gate_up_swiglu(name, a, w4, dep, tn_pref=256):
    lp, kdim = a.shape
    s, _, nloc = w4.shape
    f = s * nloc // 2
    tn = _pick(nloc, tn_pref)
    bw = nloc // tn
    chunk = _row_tile(lp, 1, 1, 576)

    def body(a_ref, wg_ref, wu_ref, _, fac_ref, act_ref):
        for m0 in range(0, lp, chunk):
            sl = pl.ds(m0, chunk)
            gt = jnp.dot(a_ref[sl, :], wg_ref[...], preferred_element_type=F32)
            up = jnp.dot(a_ref[sl, :], wu_ref[...], preferred_element_type=F32)
            silu, dsilu = _silu_parts(gt)
            fac_ref[0, sl, :] = (up * dsilu).astype(BF16)
            fac_ref[1, sl, :] = silu.astype(BF16)
            act_ref[sl, :] = (silu * up).astype(BF16)

    return pl.pallas_call(
        body, name=name, grid=(f // tn,),
        out_shape=(jax.ShapeDtypeStruct((2, lp, f), BF16), jax.ShapeDtypeStruct((lp, f), BF16)),
        in_specs=[pl.BlockSpec((lp, kdim), lambda j: (0, 0)),
                  pl.BlockSpec((None, kdim, tn), lambda j: (j // bw, 0, j % bw)),
                  pl.BlockSpec((None, kdim, tn), lambda j: (s // 2 + j // bw, 0, j % bw)), ANY],
        out_specs=(pl.BlockSpec((2, lp, tn), lambda j: (0, 0, j)), pl.BlockSpec((lp, tn), lambda j: (0, j))),
        compiler_params=_params(1),
    )(a, w4, w4, dep)


def _dact_swiglu_bwd(name, d, w, gu, tn_pref=512):
    lp, dm = d.shape
    f = w.shape[0]
    tn = _pick(f, tn_pref)
    chunk = _row_tile(lp, 1, 1, 576)

    def body(d_ref, w_ref, g_ref, u_ref, o_ref):
        for m0 in range(0, lp, chunk):
            sl = pl.ds(m0, chunk)
            dact = lax.dot_general(d_ref[sl, :], w_ref[...], _DIMS["nt"], preferred_element_type=F32)
            o_ref[0, sl, :] = (dact * g_ref[sl, :].astype(F32)).astype(BF16)
            o_ref[1, sl, :] = (dact * u_ref[sl, :].astype(F32)).astype(BF16)

    return pl.pallas_call(
        body, name=name, grid=(f // tn,), out_shape=jax.ShapeDtypeStruct((2, lp, f), BF16),
        in_specs=[pl.BlockSpec((lp, dm), lambda j: (0, 0)), pl.BlockSpec((tn, dm), lambda j: (j, 0)),
                  pl.BlockSpec((None, lp, tn), lambda j: (0, 0, j)), pl.BlockSpec((None, lp, tn), lambda j: (1, 0, j))],
        out_specs=pl.BlockSpec((2, lp, tn), lambda j: (0, 0, j)), compiler_params=_params(1),
    )(d, w, gu, gu)


def _pool_fwd(name, pooled, pw):
    lp, dm = pooled.shape
    g, gw, _ = pw.shape
    return _matmul(
        name, "nn", pooled, pw, jax.ShapeDtypeStruct((lp, dm), BF16), (g,),
        pl.BlockSpec((lp, gw), lambda gi: (0, gi)), pl.BlockSpec((None, gw, gw), lambda gi: (gi, 0, 0)),
        pl.BlockSpec((lp, gw), lambda gi: (0, gi)), 1)


def _pool_bwd_act(name, dya, pw, deps=()):
    lp, dm = dya.shape
    g, gw, _ = pw.shape
    return _matmul(
        name, "nt", dya, pw, jax.ShapeDtypeStruct((lp, dm), BF16), (g,),
        pl.BlockSpec((lp, gw), lambda gi: (0, gi)), pl.BlockSpec((None, gw, gw), lambda gi: (gi, 0, 0)),
        pl.BlockSpec((lp, gw), lambda gi: (0, gi)), 1, deps=deps)


def _pool_bwd_w(name, pooled, dya):
    lp, dm = pooled.shape
    g = len(POOL_WINDOWS)
    gw = dm // g
    return _matmul(
        name, "tn", pooled, dya, jax.ShapeDtypeStruct((g, gw, gw), BF16), (g,),
        pl.BlockSpec((lp, gw), lambda gi: (0, gi)), pl.BlockSpec((lp, gw), lambda gi: (0, gi)),
        pl.BlockSpec((None, gw, gw), lambda gi: (gi, 0, 0)), 1)


def _rms_fwd(name, h, g, tm, deps=()):
    lp, dm = h.shape

    def body(h_ref, g_ref, *rest):
        hv = h_ref[...]
        r = lax.rsqrt(jnp.mean(hv * hv, axis=-1, keepdims=True) + EPS)
        rest[-1][...] = (hv * r * g_ref[...]).astype(BF16)

    row = pl.BlockSpec((tm, dm), lambda i: (i, 0))
    return pl.pallas_call(
        body, name=name, out_shape=jax.ShapeDtypeStruct((lp, dm), BF16), grid=(lp // tm,),
        in_specs=[row, pl.BlockSpec((1, dm), lambda i: (0, 0))] + [ANY] * len(deps), out_specs=row, compiler_params=_params(1),
    )(h, g, *deps)


def _rms_fwd_input(name, x2d, tail_rows, g, deps=()):
    seq, dm = x2d.shape
    tm = tail_rows.shape[0]
    nx = seq // tm
    lp = seq + tm

    def body(x_ref, t_ref, g_ref, *rest):
        h_ref, o_ref = rest[len(deps):]

        def emit(hv):
            r = lax.rsqrt(jnp.mean(hv * hv, axis=-1, keepdims=True) + EPS)
            h_ref[...] = hv
            o_ref[...] = (hv * r * g_ref[...]).astype(BF16)

        @pl.when(pl.program_id(0) < nx)
        def _():
            emit(x_ref[...])

        @pl.when(pl.program_id(0) >= nx)
        def _():
            emit(t_ref[...])

    row = pl.BlockSpec((tm, dm), lambda i: (i, 0))
    return pl.pallas_call(
        body, name=name, grid=(lp // tm,),
        out_shape=(jax.ShapeDtypeStruct((lp, dm), F32), jax.ShapeDtypeStruct((lp, dm), BF16)),
        in_specs=[pl.BlockSpec((tm, dm), lambda i: (jnp.minimum(i, nx - 1), 0)), pl.BlockSpec((tm, dm), lambda i: (0, 0)),
                  pl.BlockSpec((1, dm), lambda i: (0, 0))] + [ANY] * len(deps),
        out_specs=(row, row), compiler_params=_params(1),
    )(x2d, tail_rows, g, *deps)


def _rms_bwd(name, dy, h, g, dres, tm, dep):
    lp, dm = h.shape

    def body(dy_ref, h_ref, g_ref, dr_ref, _, dh_ref, dhb_ref, dg_ref):
        hv = h_ref[...]
        r = lax.rsqrt(jnp.mean(hv * hv, axis=-1, keepdims=True) + EPS)
        xhat = hv * r
        dyv = dy_ref[...]
        dxh = dyv * g_ref[...]
        dh = dr_ref[...] + r * (dxh - xhat * jnp.mean(dxh * xhat, axis=-1, keepdims=True))
        dh_ref[...] = dh
        dhb_ref[...] = dh.astype(BF16)

        @pl.when(pl.program_id(0) == 0)
        def _():
            dg_ref[...] = jnp.zeros_like(dg_ref)

        dg_ref[0:1, :] += jnp.sum(dyv * xhat, axis=0, keepdims=True)

    row = pl.BlockSpec((tm, dm), lambda i: (i, 0))
    slab = pl.BlockSpec((SMALL_ROWS, dm), lambda i: (0, 0))
    return pl.pallas_call(
        body, name=name, grid=(lp // tm,),
        out_shape=(jax.ShapeDtypeStruct((lp, dm), F32), jax.ShapeDtypeStruct((lp, dm), BF16),
                   jax.ShapeDtypeStruct((SMALL_ROWS, dm), F32)),
        in_specs=[row, row, pl.BlockSpec((1, dm), lambda i: (0, 0)), row, ANY], out_specs=(row, row, slab),
        compiler_params=_params(1),
    )(dy, h, g, dres, dep)


def _rms_bwd_input(name, dy, h, g, dres, tm, seq, dep):
    lp, dm = h.shape
    nx = seq // tm

    def body(dy_ref, h_ref, g_ref, dr_ref, _, dx_ref, dt_ref, dg_ref):
        i = pl.program_id(0)
        hv = h_ref[...]
        r = lax.rsqrt(jnp.mean(hv * hv, axis=-1, keepdims=True) + EPS)
        xhat = hv * r
        dyv = dy_ref[...]
        dxh = dyv * g_ref[...]
        dh = dr_ref[...] + r * (dxh - xhat * jnp.mean(dxh * xhat, axis=-1, keepdims=True))

        @pl.when(i < nx)
        def _():
            dx_ref[...] = dh

        @pl.when(i >= nx)
        def _():
            dt_ref[...] = dh

        @pl.when(i == 0)
        def _():
            dg_ref[...] = jnp.zeros_like(dg_ref)

        dg_ref[0:1, :] += jnp.sum(dyv * xhat, axis=0, keepdims=True)

    row = pl.BlockSpec((tm, dm), lambda i: (i, 0))
    slab = pl.BlockSpec((SMALL_ROWS, dm), lambda i: (0, 0))
    return pl.pallas_call(
        body, name=name, grid=(lp // tm,),
        out_shape=(jax.ShapeDtypeStruct((seq, dm), F32), jax.ShapeDtypeStruct((tm, dm), F32),
                   jax.ShapeDtypeStruct((SMALL_ROWS, dm), F32)),
        in_specs=[row, row, pl.BlockSpec((1, dm), lambda i: (0, 0)), row, ANY],
        out_specs=(pl.BlockSpec((tm, dm), lambda i: (jnp.minimum(i, nx - 1), 0)), pl.BlockSpec((tm, dm), lambda i: (0, 0)), slab),
        compiler_params=_params(1),
    )(dy, h, g, dres, dep)


def _gate_mix(name, proj, b_gate2, ya, pool_scale, yb, tm):
    _, lp, dm = proj.shape

    def body(ga_ref, gr_ref, b_ref, ya_ref, ps_ref, yb_ref, o_ref):
        g_a = jax.nn.sigmoid(ga_ref[...].astype(F32) + b_ref[0:1, :])
        g_b = jax.nn.sigmoid(gr_ref[...].astype(F32) + b_ref[1:2, :])
        y_a = ya_ref[...].astype(F32) * ps_ref[...]
        o_ref[...] = (g_a * y_a + g_b * yb_ref[...].astype(F32)).astype(BF16)

    row = pl.BlockSpec((tm, dm), lambda i: (i, 0))
    return pl.pallas_call(
        body, name=name, out_shape=jax.ShapeDtypeStruct((lp, dm), BF16), grid=(lp // tm,),
        in_specs=[pl.BlockSpec((None, tm, dm), lambda i: (4, i, 0)), pl.BlockSpec((None, tm, dm), lambda i: (5, i, 0)),
                  pl.BlockSpec((2, dm), lambda i: (0, 0)), row, pl.BlockSpec((1, dm), lambda i: (0, 0)), row],
        out_specs=row, compiler_params=_params(1),
    )(proj, proj, b_gate2, ya, pool_scale, yb)


def _gate_bwd(name, dmix, proj, b_gate2, ya, pool_scale, yb, tm):
    _, lp, dm = proj.shape

    def body(dm_ref, ga_ref, gr_ref, b_ref, ya_ref, ps_ref, yb_ref, dp_ref, dyb_ref, dya_ref, db_ref, dps_ref):
        dmx = dm_ref[...].astype(F32)
        g_a = jax.nn.sigmoid(ga_ref[...].astype(F32) + b_ref[0:1, :])
        g_b = jax.nn.sigmoid(gr_ref[...].astype(F32) + b_ref[1:2, :])
        ya_pre = ya_ref[...].astype(F32)
        ybv = yb_ref[...].astype(F32)
        ps = ps_ref[...]
        dga = dmx * (ya_pre * ps) * (g_a * (1.0 - g_a))
        dgr = dmx * ybv * (g_b * (1.0 - g_b))
        dp_ref[0] = dga.astype(BF16)
        dp_ref[1] = dgr.astype(BF16)
        dyb_ref[...] = (dmx * g_b).astype(BF16)
        dya_ref[...] = (dmx * g_a * ps).astype(BF16)

        @pl.when(pl.program_id(0) == 0)
        def _():
            db_ref[...] = jnp.zeros_like(db_ref)
            dps_ref[...] = jnp.zeros_like(dps_ref)

        db_ref[0:1, :] += jnp.sum(dga, axis=0, keepdims=True)
        db_ref[1:2, :] += jnp.sum(dgr, axis=0, keepdims=True)
        dps_ref[0:1, :] += jnp.sum(dmx * g_a * ya_pre, axis=0, keepdims=True)

    row = pl.BlockSpec((tm, dm), lambda i: (i, 0))
    one = pl.BlockSpec((1, dm), lambda i: (0, 0))
    slab = pl.BlockSpec((SMALL_ROWS, dm), lambda i: (0, 0))
    return pl.pallas_call(
        body, name=name, grid=(lp // tm,),
        out_shape=(jax.ShapeDtypeStruct((6, lp, dm), BF16), jax.ShapeDtypeStruct((lp, dm), BF16),
                   jax.ShapeDtypeStruct((lp, dm), BF16), jax.ShapeDtypeStruct((SMALL_ROWS, dm), F32),
                   jax.ShapeDtypeStruct((SMALL_ROWS, dm), F32)),
        in_specs=[row, pl.BlockSpec((None, tm, dm), lambda i: (4, i, 0)), pl.BlockSpec((None, tm, dm), lambda i: (5, i, 0)),
                  pl.BlockSpec((2, dm), lambda i: (0, 0)), row, one, row],
        out_specs=(pl.BlockSpec((2, tm, dm), lambda i: (2, i, 0)), row, row, slab, slab),
        compiler_params=_params(1),
    )(dmix, proj, proj, b_gate2, ya, pool_scale, yb)


def _final_loss(name, h2, g3, target, tm):
    lp, dm = h2.shape
    nx = target.shape[0] // tm

    def body(h_ref, g_ref, t_ref, dh_ref, dhb_ref, ls_ref, dg_ref):
        i = pl.program_id(0)

        @pl.when(i == 0)
        def _():
            ls_ref[...] = jnp.zeros_like(ls_ref)
            dg_ref[...] = jnp.zeros_like(dg_ref)

        @pl.when(i < nx)
        def _():
            hv = h_ref[...]
            gv = g_ref[...]
            r = lax.rsqrt(jnp.mean(hv * hv, axis=-1, keepdims=True) + EPS)
            xhat = hv * r
            err = xhat * gv - t_ref[...]
            dout = err * (1.0 / dm)
            dxh = dout * gv
            dh = r * (dxh - xhat * jnp.mean(dxh * xhat, axis=-1, keepdims=True))
            dh_ref[...] = dh
            dhb_ref[...] = dh.astype(BF16)
            ls_ref[0:1, :] += jnp.sum(err * err, axis=0, keepdims=True)
            dg_ref[0:1, :] += jnp.sum(dout * xhat, axis=0, keepdims=True)

        @pl.when(i >= nx)
        def _():
            dh_ref[...] = jnp.zeros_like(dh_ref)
            dhb_ref[...] = jnp.zeros_like(dhb_ref)

    row = pl.BlockSpec((tm, dm), lambda i: (i, 0))
    slab = pl.BlockSpec((SMALL_ROWS, dm), lambda i: (0, 0))
    return pl.pallas_call(
        body, name=name, grid=(lp // tm,),
        out_shape=(jax.ShapeDtypeStruct((lp, dm), F32), jax.ShapeDtypeStruct((lp, dm), BF16),
                   jax.ShapeDtypeStruct((SMALL_ROWS, dm), F32), jax.ShapeDtypeStruct((SMALL_ROWS, dm), F32)),
        in_specs=[row, pl.BlockSpec((1, dm), lambda i: (0, 0)), pl.BlockSpec((tm, dm), lambda i: (jnp.minimum(i, nx - 1), 0))],
        out_specs=(row, row, slab, slab), compiler_params=_params(1),
    )(h2, g3, target)


def _shift(v, k):
    return pltpu.roll(v, k % v.shape[0], axis=0)


def _window_sum(v, group, sign):
    s2 = v + _shift(v, sign * 1)
    s4 = s2 + _shift(s2, sign * 2)
    s8 = s4 + _shift(s4, sign * 4)
    s16 = s8 + _shift(s8, sign * 8)
    return jnp.where(group == 0, s2, jnp.where(group == 1, s4, jnp.where(group == 2, s8, s16)))


def _pool_count(lp, group):
    row = lax.broadcasted_iota(jnp.int32, (lp, 1), 0)
    window = jnp.left_shift(2, group).astype(F32)
    meta_pos = (row - (lp - N_META) + 1).astype(F32)
    return jnp.where(row >= lp - N_META, jnp.minimum(meta_pos, window), window)


def _mixer_fwd(name, proj, conv_w, tc, dep):
    _, lp, dm = proj.shape
    per_group = dm // len(POOL_WINDOWS) // tc

    def body(u_ref, gb_ref, gc_ref, v_ref, cw_ref, _, p_ref, z_ref):
        group = pl.program_id(0) // per_group
        u = u_ref[...].astype(F32)
        p_ref[...] = (_window_sum(u, group, 1) / _pool_count(lp, group) - u).astype(BF16)
        cv = gc_ref[...].astype(F32) * v_ref[...].astype(F32)
        conv = cw_ref[0:1, :] * _shift(cv, 2) + cw_ref[1:2, :] * _shift(cv, 1) + cw_ref[2:3, :] * cv
        z_ref[...] = (gb_ref[...].astype(F32) * conv).astype(BF16)

    def seg(s):
        return pl.BlockSpec((None, lp, tc), lambda j: (s, 0, j))

    col = pl.BlockSpec((lp, tc), lambda j: (0, j))
    return pl.pallas_call(
        body, name=name, grid=(dm // tc,),
        out_shape=(jax.ShapeDtypeStruct((lp, dm), BF16), jax.ShapeDtypeStruct((lp, dm), BF16)),
        in_specs=[seg(0), seg(1), seg(2), seg(3), pl.BlockSpec((3, tc), lambda j: (0, j)), ANY],
        out_specs=(col, col), compiler_params=_params(1),
    )(proj, proj, proj, proj, conv_w, dep)


def _mixer_bwd(name, dz, dpooled, proj, conv_w, dproj, tc, dep):
    _, lp, dm = proj.shape
    per_group = dm // len(POOL_WINDOWS) // tc

    def body(dz_ref, dp_ref, gb_ref, gc_ref, v_ref, cw_ref, _, __, o_ref, dcw_ref):
        group = pl.program_id(0) // per_group
        dzv = dz_ref[...].astype(F32)
        gb = gb_ref[...].astype(F32)
        gc = gc_ref[...].astype(F32)
        vv = v_ref[...].astype(F32)
        cv = gc * vv
        c1 = _shift(cv, 1)
        c2 = _shift(cv, 2)
        w0, w1, w2 = cw_ref[0:1, :], cw_ref[1:2, :], cw_ref[2:3, :]
        o_ref[1] = (dzv * (w0 * c2 + w1 * c1 + w2 * cv)).astype(BF16)
        dconv = dzv * gb
        dcw_ref[...] = jnp.zeros_like(dcw_ref)
        dcw_ref[0:1, :] = jnp.sum(dconv * c2, axis=0, keepdims=True)
        dcw_ref[1:2, :] = jnp.sum(dconv * c1, axis=0, keepdims=True)
        dcw_ref[2:3, :] = jnp.sum(dconv * cv, axis=0, keepdims=True)
        dcv = w0 * _shift(dconv, -2) + w1 * _shift(dconv, -1) + w2 * dconv
        o_ref[2] = (dcv * vv).astype(BF16)
        o_ref[3] = (dcv * gc).astype(BF16)
        dpv = dp_ref[...].astype(F32)
        o_ref[0] = (_window_sum(dpv / _pool_count(lp, group), group, -1) - dpv).astype(BF16)

    def seg(s):
        return pl.BlockSpec((None, lp, tc), lambda j: (s, 0, j))

    col = pl.BlockSpec((lp, tc), lambda j: (0, j))
    return pl.pallas_call(
        body, name=name, grid=(dm // tc,),
        out_shape=(jax.ShapeDtypeStruct(dproj.shape, BF16), jax.ShapeDtypeStruct((SMALL_ROWS, dm), F32)),
        in_specs=[col, col, seg(1), seg(2), seg(3), pl.BlockSpec((3, tc), lambda j: (0, j)), ANY, ANY],
        out_specs=(pl.BlockSpec((4, lp, tc), lambda j: (0, 0, j)), pl.BlockSpec((SMALL_ROWS, tc), lambda j: (0, j))),
        input_output_aliases={6: 0}, compiler_params=_params(1),
    )(dz, dpooled, proj, proj, proj, conv_w, dproj, dep)


def _row_tile(r, c, bytes_per_row_elem=4, budget=2 * 1024 * 1024):
    best = None
    for t in range(16, r + 1, 16):
        if r % t == 0 and t * c * bytes_per_row_elem <= budget:
            best = t
    return best if best is not None else r


def _pair_add(name, g4, recv, core):
    s, r, c = g4.shape
    h = r // 2
    tr = _row_tile(h, c, budget=6 * 1024 * 1024)
    nb = h // tr

    def body(core_ref, g_ref, r_ref, o_ref):
        o_ref[...] = (g_ref[...].astype(F32) + r_ref[...].astype(F32)).astype(BF16)

    grid_spec = pltpu.PrefetchScalarGridSpec(
        num_scalar_prefetch=1, grid=(s, nb),
        in_specs=[pl.BlockSpec((None, tr, c), lambda si, j, core_ref: (si, core_ref[0] * nb + j, 0)),
                  pl.BlockSpec((None, tr, c), lambda si, j, core_ref: (si, j, 0))],
        out_specs=pl.BlockSpec((None, tr, c), lambda si, j, core_ref: (si, j, 0)))
    return pl.pallas_call(
        body, name=name, out_shape=jax.ShapeDtypeStruct((s, h, c), BF16), grid_spec=grid_spec,
        compiler_params=_params(2),
    )(core, g4, recv)


def _chip_sum(name, parts, recv, chip):
    _, h, c = parts.shape
    tr = _row_tile(h, c)

    def body(chip_ref, p_ref, r_ref, o_ref):
        acc = p_ref[...].astype(F32)
        for i in range(len(CHIP_FLIPS)):
            acc = acc + r_ref[i].astype(F32)
        o_ref[...] = acc

    grid_spec = pltpu.PrefetchScalarGridSpec(
        num_scalar_prefetch=1, grid=(h // tr,),
        in_specs=[pl.BlockSpec((None, tr, c), lambda j, chip_ref: (chip_ref[0], j, 0)),
                  pl.BlockSpec((len(CHIP_FLIPS), tr, c), lambda j, chip_ref: (0, j, 0))],
        out_specs=pl.BlockSpec((tr, c), lambda j, chip_ref: (j, 0)))
    return pl.pallas_call(
        body, name=name, out_shape=jax.ShapeDtypeStruct((h, c), F32), grid_spec=grid_spec, compiler_params=_params(1),
    )(chip, parts, recv)


def _adam_update(w, gv, m, v):
    c1 = 1.0 - ADAM_B1 ** ADAM_STEP
    c2 = 1.0 - ADAM_B2 ** ADAM_STEP
    nm = ADAM_B1 * m + (1.0 - ADAM_B1) * gv
    nv = ADAM_B2 * v + (1.0 - ADAM_B2) * (gv * gv)
    return -ADAM_LR * ((nm / c1) / (jnp.sqrt(nv / c2) + ADAM_EPS) + ADAM_WD * w), nm, nv


def _adamw_halves(name, w, g_own, g_sib, m, v, core, part=(0, 1), prev=None):
    r, c = w.shape
    rp = r // part[1]
    h = rp // 2
    tr = _row_tile(h, c, budget=2 * 1024 * 1024)
    nbh = h // tr
    j0 = part[0] * 2 * nbh
    n_prev = 0 if prev is None else 4

    def body(core_ref, w_ref, go_ref, gs_ref, m_ref, v_ref, *rest):
        g_ref, d_ref, nm_ref, nv_ref = rest[n_prev:]
        mine = (pl.program_id(0) // nbh) == core_ref[0]
        gv = jnp.where(mine, go_ref[...], gs_ref[...])
        g_ref[...] = gv
        d_ref[...], nm_ref[...], nv_ref[...] = _adam_update(w_ref[...], gv, m_ref[...], v_ref[...])

    def blk(fn):
        return pl.BlockSpec((tr, c), fn)

    full = blk(lambda j, core_ref: (j0 + j, 0))
    own = blk(lambda j, core_ref: (jnp.clip(j - core_ref[0] * nbh, 0, nbh - 1), 0))
    sib = blk(lambda j, core_ref: (jnp.clip(j - (1 - core_ref[0]) * nbh, 0, nbh - 1), 0))
    grid_spec = pltpu.PrefetchScalarGridSpec(
        num_scalar_prefetch=1, grid=(2 * nbh,), in_specs=[full, own, sib, full, full] + [ANY] * n_prev, out_specs=(full,) * 4)
    sds = jax.ShapeDtypeStruct((r, c), F32)
    return pl.pallas_call(
        body, name=name, out_shape=(sds,) * 4, grid_spec=grid_spec, compiler_params=_params(1),
        input_output_aliases={6 + i: i for i in range(n_prev)},
    )(core, w, g_own, g_sib, m, v, *(prev or ()))


def _adamw(name, w, g, m, v):
    r, c = w.shape

    def body(w_ref, g_ref, m_ref, v_ref, d_ref, nm_ref, nv_ref):
        d_ref[...], nm_ref[...], nv_ref[...] = _adam_update(w_ref[...], g_ref[...], m_ref[...], v_ref[...])

    blk = pl.BlockSpec((r, c), lambda j: (0, 0))
    sds = jax.ShapeDtypeStruct((r, c), F32)
    return pl.pallas_call(
        body, name=name, out_shape=(sds, sds, sds), grid=(1,), in_specs=[blk] * 4, out_specs=(blk,) * 3,
        compiler_params=_params(1),
    )(w, g, m, v)


def _cast_into_slot(name, w, chip, dtype, deps=()):
    r, c = w.shape
    tr = _row_tile(r, c)

    def body(chip_ref, w_ref, *rest):
        rest[-1][...] = w_ref[...].astype(dtype)

    grid_spec = pltpu.PrefetchScalarGridSpec(
        num_scalar_prefetch=1, grid=(r // tr,),
        in_specs=[pl.BlockSpec((tr, c), lambda j, chip_ref: (j, 0))] + [ANY] * len(deps),
        out_specs=pl.BlockSpec((None, tr, c), lambda j, chip_ref: (chip_ref[0], j, 0)))
    return pl.pallas_call(
        body, name=name, out_shape=jax.ShapeDtypeStruct((4, r, c), dtype), grid_spec=grid_spec, compiler_params=_params(1),
    )(chip, w, *deps)


def _place():
    return lax.axis_index("x"), lax.axis_index("y"), lax.axis_index("c")


def _chip_of(x, y, flip):
    px, py = x ^ flip[0], y ^ flip[1]
    return px, py, 2 * px + py


def _half(ref, which):
    rows = ref.shape[0] // 2
    return ref.at[pl.ds(which * rows, rows)]


HBM = pl.BlockSpec(memory_space=pltpu.HBM)
SEM = pl.BlockSpec(memory_space=pltpu.SEMAPHORE)
SPLIT_COPY = pltpu.CompilerParams(has_side_effects=pltpu.SideEffectType.DATAFLOW_SIDE_EFFECTING)


def _in_hbm(arrays):
    return [pltpu.with_memory_space_constraint(t, pltpu.HBM) for t in arrays]


TOKEN = jax.ShapeDtypeStruct((SMALL_ROWS, LANES), F32)
TOKEN_SPEC = pl.BlockSpec(memory_space=pltpu.VMEM)


NEIGHBOUR_FLIPS = CHIP_FLIPS[:2]


def _relay_chips(x, y, c):
    fx, fy = x ^ c, y ^ (1 - c)
    return (fx, fy), 2 * fx + fy, 2 * (1 - x) + (1 - y)


def _ag_start(name, slabs, deps=()):
    n = len(slabs)
    nn = len(NEIGHBOUR_FLIPS)

    def body(*refs):
        no = n + len(deps)
        ssem, rsem = refs[no], refs[no + 1]
        outs = refs[no + 2:no + 2 + n]
        token = refs[no + 2 + n]
        token[...] = jnp.zeros_like(token)
        x, y, c = _place()
        k = 2 * x + y
        for a in range(n):
            for j, flip in enumerate(NEIGHBOUR_FLIPS):
                px, py, _ = _chip_of(x, y, flip)
                mine = _half(outs[a].at[k], c)
                pltpu.make_async_remote_copy(src_ref=mine, dst_ref=mine, send_sem=ssem.at[a * nn + j],
                                             recv_sem=rsem.at[a * nn + j], device_id=(px, py, c), device_id_type=MESH).start()

    sem = pltpu.SemaphoreType.DMA((nn * n,))
    res = pl.pallas_call(
        body, name=name, out_shape=(sem, sem) + tuple(pltpu.HBM(t.shape, t.dtype) for t in slabs) + (TOKEN,),
        in_specs=[HBM] * n + [ANY] * len(deps), out_specs=tuple([SEM, SEM] + [HBM] * n + [TOKEN_SPEC]),
        input_output_aliases={a: 2 + a for a in range(n)}, compiler_params=SPLIT_COPY,
    )(*_in_hbm(slabs), *deps)
    return (res[0], res[1]), list(res[2:2 + n]), res[2 + n]


def _ag_relay(name, slabs, sems, after, then_start=()):
    n = len(slabs)
    m = len(then_start)
    nn = len(NEIGHBOUR_FLIPS)

    def body(*refs):
        no = n + 2 + m + len(after)
        ins = refs[:n]
        ssem, rsem = refs[n], refs[n + 1]
        r_s, r_r, p_s, p_r = refs[no:no + 4]
        x, y, c = _place()
        k = 2 * x + y
        (fx, fy), _, _ = _relay_chips(x, y, c)
        for a in range(n):
            for j, flip in enumerate(NEIGHBOUR_FLIPS):
                _, _, kj = _chip_of(x, y, flip)
                landed = _half(ins[a].at[kj], c)
                cp = pltpu.make_async_remote_copy(
                    src_ref=_half(ins[a].at[k], c), dst_ref=landed, send_sem=ssem.at[a * nn + j],
                    recv_sem=rsem.at[a * nn + j], device_id=(x, y, c), device_id_type=MESH)
                cp.wait_send()
                cp.wait_recv()
        for a in range(n):
            near = _half(ins[a].at[2 * (x ^ (1 - c)) + (y ^ c)], c)
            pltpu.make_async_remote_copy(src_ref=near, dst_ref=near, send_sem=r_s.at[a], recv_sem=r_r.at[a],
                                         device_id=(fx, fy, c), device_id_type=MESH).start()
            for j, flip in enumerate(NEIGHBOUR_FLIPS):
                _, _, kj = _chip_of(x, y, flip)
                landed = _half(ins[a].at[kj], c)
                pltpu.make_async_remote_copy(src_ref=landed, dst_ref=landed, send_sem=p_s.at[a * nn + j],
                                             recv_sem=p_r.at[a * nn + j], device_id=(x, y, 1 - c), device_id_type=MESH).start()
        if m:
            d_s, d_r = refs[no + 4 + n], refs[no + 5 + n]
            nxt = refs[no + 6 + n:]
            for a in range(m):
                for j, flip in enumerate(NEIGHBOUR_FLIPS):
                    px, py, _ = _chip_of(x, y, flip)
                    mine = _half(nxt[a].at[k], c)
                    pltpu.make_async_remote_copy(src_ref=mine, dst_ref=mine, send_sem=d_s.at[a * nn + j],
                                                 recv_sem=d_r.at[a * nn + j], device_id=(px, py, c), device_id_type=MESH).start()

    rsem_t = pltpu.SemaphoreType.DMA((n,))
    psem_t = pltpu.SemaphoreType.DMA((nn * n,))
    out_shape = (rsem_t, rsem_t, psem_t, psem_t) + tuple(pltpu.HBM(t.shape, t.dtype) for t in slabs)
    out_specs = [SEM] * 4 + [HBM] * n
    aliases = {a: 4 + a for a in range(n)}
    if m:
        dsem_t = pltpu.SemaphoreType.DMA((nn * m,))
        out_shape += (dsem_t, dsem_t) + tuple(pltpu.HBM(t.shape, t.dtype) for t in then_start)
        out_specs += [SEM, SEM] + [HBM] * m
        aliases.update({n + 2 + a: 4 + n + 2 + a for a in range(m)})
    res = pl.pallas_call(
        body, name=name, out_shape=out_shape, in_specs=[HBM] * n + [SEM, SEM] + [HBM] * m + [ANY] * len(after),
        out_specs=tuple(out_specs), input_output_aliases=aliases, compiler_params=SPLIT_COPY,
    )(*slabs, sems[0], sems[1], *_in_hbm(list(then_start)), *after)
    if not m:
        return tuple(res[:4]), list(res[4:])
    return (tuple(res[:4]), list(res[4:4 + n])), ((res[4 + n], res[5 + n]), list(res[6 + n:]))


def _wait_passes(ins, p_s, p_r, x, y, c):
    nn = len(NEIGHBOUR_FLIPS)
    for a in range(len(ins)):
        for j, flip in enumerate(NEIGHBOUR_FLIPS):
            _, _, kj = _chip_of(x, y, flip)
            cp = pltpu.make_async_remote_copy(
                src_ref=_half(ins[a].at[kj], c), dst_ref=_half(ins[a].at[kj], 1 - c), send_sem=p_s.at[a * nn + j],
                recv_sem=p_r.at[a * nn + j], device_id=(x, y, c), device_id_type=MESH)
            cp.wait_send()
            cp.wait_recv()


def _ag_relay_wait(name, slabs, sems, after):
    n = len(slabs)
    ns = len(sems)

    def body(*refs):
        no = n + ns + len(after)
        ins = refs[:n]
        r_s, r_r = refs[n], refs[n + 1]
        f_s, f_r = refs[no], refs[no + 1]
        x, y, c = _place()
        _, _, kd = _relay_chips(x, y, c)
        for a in range(n):
            near = _half(ins[a].at[2 * (x ^ (1 - c)) + (y ^ c)], c)
            cp = pltpu.make_async_remote_copy(src_ref=near, dst_ref=_half(ins[a].at[kd], c), send_sem=r_s.at[a],
                                              recv_sem=r_r.at[a], device_id=(x, y, c), device_id_type=MESH)
            cp.wait_send()
            cp.wait_recv()
        if ns == 4:
            _wait_passes(ins, refs[n + 2], refs[n + 3], x, y, c)
        for a in range(n):
            diag = _half(ins[a].at[kd], c)
            pltpu.make_async_remote_copy(src_ref=diag, dst_ref=diag, send_sem=f_s.at[a], recv_sem=f_r.at[a],
                                         device_id=(x, y, 1 - c), device_id_type=MESH).start()

    sem = pltpu.SemaphoreType.DMA((n,))
    res = pl.pallas_call(
        body, name=name, out_shape=(sem, sem) + tuple(pltpu.HBM(t.shape, t.dtype) for t in slabs),
        in_specs=[HBM] * n + [SEM] * ns + [ANY] * len(after), out_specs=tuple([SEM, SEM] + [HBM] * n),
        input_output_aliases={a: 2 + a for a in range(n)}, compiler_params=SPLIT_COPY,
    )(*slabs, *sems, *after)
    return (res[0], res[1]), list(res[2:])


def _ag_final_wait(name, slabs, sems, after):
    n = len(slabs)

    def body(*refs):
        ins = refs[:n]
        f_s, f_r = refs[n], refs[n + 1]
        x, y, c = _place()
        _, _, kd = _relay_chips(x, y, c)
        for a in range(n):
            cp = pltpu.make_async_remote_copy(
                src_ref=_half(ins[a].at[kd], c), dst_ref=_half(ins[a].at[kd], 1 - c), send_sem=f_s.at[a], recv_sem=f_r.at[a],
                device_id=(x, y, c), device_id_type=MESH)
            cp.wait_send()
            cp.wait_recv()

    return pl.pallas_call(
        body, name=name, out_shape=tuple(pltpu.HBM(t.shape, t.dtype) for t in slabs),
        in_specs=[HBM] * n + [SEM, SEM] + [ANY] * len(after), out_specs=tuple([HBM] * n),
        input_output_aliases={a: a for a in range(n)}, compiler_params=SPLIT_COPY,
    )(*slabs, sems[0], sems[1], *after)


def _sibling_part(ref, c, halves):
    if not halves:
        return ref
    h = ref.shape[1] // 2
    return ref.at[:, pl.ds((1 - c) * h, h)]


def _swap_start(name, grads, halves=True, deps=()):
    n = len(grads)

    def body(*refs):
        no = 2 * n + len(deps)
        ssem, rsem = refs[no], refs[no + 1]
        src, land = refs[no + 2:no + n + 2], refs[no + n + 2:no + 2 * n + 2]
        token = refs[no + 2 * n + 2]
        token[...] = jnp.zeros_like(token)
        x, y, c = _place()
        for a in range(n):
            pltpu.make_async_remote_copy(
                src_ref=_sibling_part(src[a], c, halves), dst_ref=land[a], send_sem=ssem.at[a], recv_sem=rsem.at[a],
                device_id=(x, y, 1 - c), device_id_type=MESH).start()

    zones = [lax.empty((g.shape[0], g.shape[1] // 2, g.shape[2]) if halves else g.shape, g.dtype) for g in grads]
    sem = pltpu.SemaphoreType.DMA((n,))
    res = pl.pallas_call(
        body, name=name,
        out_shape=(sem, sem) + tuple(pltpu.HBM(t.shape, t.dtype) for t in list(grads) + zones) + (TOKEN,),
        in_specs=[HBM] * (2 * n) + [ANY] * len(deps), out_specs=tuple([SEM, SEM] + [HBM] * (2 * n) + [TOKEN_SPEC]),
        input_output_aliases={i: 2 + i for i in range(2 * n)}, compiler_params=SPLIT_COPY,
    )(*_in_hbm(list(grads) + zones), *deps)
    return (res[0], res[1], list(res[2:2 + n]), list(res[2 + n:2 + 2 * n])), res[2 + 2 * n]


def _swap_wait(name, ssem, rsem, grads, zones, after, halves=True):
    n = len(grads)

    def body(*refs):
        src, land = refs[:n], refs[n:2 * n]
        ss, rs = refs[2 * n], refs[2 * n + 1]
        x, y, c = _place()
        for a in range(n):
            cp = pltpu.make_async_remote_copy(
                src_ref=_sibling_part(src[a], c, halves), dst_ref=land[a], send_sem=ss.at[a], recv_sem=rs.at[a],
                device_id=(x, y, c), device_id_type=MESH)
            cp.wait_send()
            cp.wait_recv()

    res = pl.pallas_call(
        body, name=name, out_shape=tuple(pltpu.HBM(t.shape, t.dtype) for t in list(grads) + list(zones)),
        in_specs=[HBM] * (2 * n) + [SEM, SEM] + [ANY] * len(after), out_specs=tuple([HBM] * (2 * n)),
        input_output_aliases={i: i for i in range(2 * n)}, compiler_params=SPLIT_COPY,
    )(*grads, *zones, ssem, rsem, *after)
    return list(res[:n]), list(res[n:])


def _scatter_start(name, parts):
    n = len(parts)
    nf = len(CHIP_FLIPS)

    def body(*refs):
        ssem, rsem = refs[2 * n], refs[2 * n + 1]
        src, land = refs[2 * n + 2:3 * n + 2], refs[3 * n + 2:4 * n + 2]
        token = refs[4 * n + 2]
        token[...] = jnp.zeros_like(token)
        x, y, c = _place()
        for a in range(n):
            for j, flip in enumerate(CHIP_FLIPS):
                px, py, kj = _chip_of(x, y, flip)
                pltpu.make_async_remote_copy(
                    src_ref=src[a].at[kj], dst_ref=land[a].at[j], send_sem=ssem.at[a * nf + j], recv_sem=rsem.at[a * nf + j],
                    device_id=(px, py, c), device_id_type=MESH).start()

    zones = [lax.empty((nf,) + p.shape[1:], p.dtype) for p in parts]
    sem = pltpu.SemaphoreType.DMA((nf * n,))
    res = pl.pallas_call(
        body, name=name,
        out_shape=(sem, sem) + tuple(pltpu.HBM(t.shape, t.dtype) for t in list(parts) + zones)
        + (jax.ShapeDtypeStruct((SMALL_ROWS, LANES), F32),),
        in_specs=[HBM] * (2 * n),
        out_specs=tuple([SEM, SEM] + [HBM] * (2 * n) + [pl.BlockSpec(memory_space=pltpu.VMEM)]),
        input_output_aliases={i: 2 + i for i in range(2 * n)}, compiler_params=SPLIT_COPY,
    )(*_in_hbm(list(parts) + zones))
    return (res[0], res[1], list(res[2:2 + n]), list(res[2 + n:2 + 2 * n])), res[2 + 2 * n]


def _scatter_wait(name, ssem, rsem, parts, zones, after):
    n = len(parts)
    nf = len(CHIP_FLIPS)

    def body(*refs):
        src, land = refs[:n], refs[n:2 * n]
        ss, rs = refs[2 * n], refs[2 * n + 1]
        x, y, c = _place()
        for a in range(n):
            for j, flip in enumerate(CHIP_FLIPS):
                _, _, kj = _chip_of(x, y, flip)
                cp = pltpu.make_async_remote_copy(
                    src_ref=src[a].at[kj], dst_ref=land[a].at[j], send_sem=ss.at[a * nf + j], recv_sem=rs.at[a * nf + j],
                    device_id=(x, y, c), device_id_type=MESH)
                cp.wait_send()
                cp.wait_recv()

    res = pl.pallas_call(
        body, name=name, out_shape=tuple(pltpu.HBM(t.shape, t.dtype) for t in list(parts) + list(zones)),
        in_specs=[HBM] * (2 * n) + [SEM, SEM] + [ANY] * len(after), out_specs=tuple([HBM] * (2 * n)),
        input_output_aliases={i: i for i in range(2 * n)}, compiler_params=SPLIT_COPY,
    )(*parts, *zones, ssem, rsem, *after)
    return list(res[:n]), list(res[n:])


N_PEERS = 7


def _peer(x, y, c, mask):
    px, py, pc = x ^ ((mask >> 2) & 1), y ^ ((mask >> 1) & 1), c ^ (mask & 1)
    return (px, py, pc), 4 * px + 2 * py + pc


def _reduce_start(vec, deps):
    nd = len(deps)

    def body(*refs):
        ssem, rsem, src, land, token = refs[2 + nd:]
        token[...] = jnp.zeros_like(token)
        x, y, c = _place()
        me = 4 * x + 2 * y + c
        for mask in range(1, N_PEERS + 1):
            to, _ = _peer(x, y, c, mask)
            pltpu.make_async_remote_copy(src_ref=src, dst_ref=land.at[me], send_sem=ssem.at[mask - 1],
                                         recv_sem=rsem.at[mask - 1], device_id=to, device_id_type=MESH).start()

    zone = lax.empty((N_PEERS + 1,) + vec.shape, vec.dtype)
    sem = pltpu.SemaphoreType.DMA((N_PEERS,))
    res = pl.pallas_call(
        body, name="reduce_start",
        out_shape=(sem, sem, pltpu.HBM(vec.shape, vec.dtype), pltpu.HBM(zone.shape, zone.dtype), TOKEN),
        in_specs=[HBM, HBM] + [ANY] * nd, out_specs=(SEM, SEM, HBM, HBM, TOKEN_SPEC),
        input_output_aliases={0: 2, 1: 3}, compiler_params=SPLIT_COPY,
    )(*_in_hbm([vec, zone]), *deps)
    return res[:4], res[4]


def _reduce_wait(ssem, rsem, vec, zone, after):
    def body(src, land, ss, rs, *_):
        x, y, c = _place()
        for mask in range(1, N_PEERS + 1):
            _, frm = _peer(x, y, c, mask)
            cp = pltpu.make_async_remote_copy(src_ref=src, dst_ref=land.at[frm], send_sem=ss.at[mask - 1],
                                              recv_sem=rs.at[mask - 1], device_id=(x, y, c), device_id_type=MESH)
            cp.wait_send()
            cp.wait_recv()

    return pl.pallas_call(
        body, name="reduce_wait", out_shape=(pltpu.HBM(vec.shape, vec.dtype), pltpu.HBM(zone.shape, zone.dtype)),
        in_specs=[HBM, HBM, SEM, SEM] + [ANY] * len(after), out_specs=(HBM, HBM),
        input_output_aliases={0: 0, 1: 1}, compiler_params=SPLIT_COPY,
    )(vec, zone, ssem, rsem, *after)


def _reduce_sum(vec, zone, me, loss_row, loss_scale):
    r, dm = vec.shape

    def body(me_ref, v_ref, z_ref, o_ref, l_ref):
        acc = None
        for i in range(N_PEERS + 1):
            term = jnp.where(me_ref[0] == i, v_ref[...], z_ref[i])
            acc = term if acc is None else acc + term
        o_ref[...] = acc
        l_ref[...] = jnp.sum(acc[loss_row:loss_row + SMALL_ROWS, :], axis=(0, 1), keepdims=True) * loss_scale

    grid_spec = pltpu.PrefetchScalarGridSpec(
        num_scalar_prefetch=1, grid=(1,),
        in_specs=[pl.BlockSpec((r, dm), lambda i, me_ref: (0, 0)), pl.BlockSpec((N_PEERS + 1, r, dm), lambda i, me_ref: (0, 0, 0))],
        out_specs=(pl.BlockSpec((r, dm), lambda i, me_ref: (0, 0)), pl.BlockSpec((1, 1), lambda i, me_ref: (0, 0))))
    return pl.pallas_call(
        body, name="reduce_sum", out_shape=(jax.ShapeDtypeStruct((r, dm), F32), jax.ShapeDtypeStruct((1, 1), F32)),
        grid_spec=grid_spec, compiler_params=_params(1),
    )(me, vec, zone)


def kernel(x, meta_tokens, norm_mix_g, w_in, b_gate, pool_w, pool_scale, conv_w, conv_out_w, w_o, norm_ffn_g, w_gate_up, w_down, norm_final_g, loss_target, m_meta_tokens, m_norm_mix_g, m_w_in, m_b_gate, m_pool_w, m_pool_scale, m_conv_w, m_conv_out_w, m_w_o, m_norm_ffn_g, m_w_gate_up, m_w_down, m_norm_final_g, v_meta_tokens, v_norm_mix_g, v_w_in, v_b_gate, v_pool_w, v_pool_scale, v_conv_w, v_conv_out_w, v_w_o, v_norm_ffn_g, v_w_gate_up, v_w_down, v_norm_final_g):
    seq, dm = x.shape[1], x.shape[2]
    tail = TAIL_ROWS
    tm = tail
    lp = seq + tail
    tm_row = _row_tile(lp, dm, 4, 3 * 1024 * 1024)
    n_chips = 4
    n_groups = len(POOL_WINDOWS)
    gw = dm // n_groups
    tc = min(256, gw)
    cx, cy, cc = _place()
    chip = 2 * cx + cy
    dloc = dm // n_chips

    pool2 = pool_w.reshape(n_groups * pool_w.shape[1], gw)
    big = {"w_in": w_in, "w_gate_up": w_gate_up, "pool_w": pool2, "conv_out_w": conv_out_w, "w_o": w_o, "w_down": w_down}
    chip1 = jnp.reshape(chip, (1,)).astype(jnp.int32)
    core = jnp.reshape(cc, (1,)).astype(jnp.int32)
    small_loc = jnp.concatenate([meta_tokens, jnp.pad(conv_w, ((0, 8 - conv_w.shape[0]), (0, 0))),
                                 jnp.zeros((8, dloc), F32)], axis=0)
    g1, g2, g3 = norm_mix_g.reshape(1, dm), norm_ffn_g.reshape(1, dm), norm_final_g.reshape(1, dm)
    b_gate2 = b_gate.reshape(2, dm)
    ps = pool_scale.reshape(1, dm)
    first = [_cast_into_slot("cast_w_in", w_in, chip1, BF16), _cast_into_slot("place_small", small_loc, chip1, F32)]
    sems, first, token = _ag_start("ag_start_first", first)
    cast = {nme: _cast_into_slot("cast_" + nme, big[nme], chip1, BF16, deps=(token,))
            for nme in ["pool_w", "conv_out_w", "w_o", "w_gate_up", "w_down"]}
    sems, first = _ag_relay("ag_relay_first", first, sems, list(cast.values()))
    sems, first = _ag_relay_wait("ag_relay_wait_first", first, sems, [])
    w_in4, small4 = _ag_final_wait("ag_final_wait_first", first, sems, [])
    mixer_w = [cast["pool_w"], cast["conv_out_w"], cast["w_o"]]
    sems_mix, mixer_w, token = _ag_start("ag_start_mixer", mixer_w, deps=(w_in4,))
    sems_gu, (w_gu4,), token = _ag_start("ag_start_gate_up", [cast["w_gate_up"]], deps=(token,))

    small_f = jnp.transpose(small4, (1, 0, 2)).reshape(small4.shape[1], dm)
    meta_f = small_f[:N_META]
    conv_w_f = small_f[N_META:N_META + 3]
    tail_rows = jnp.concatenate([jnp.zeros((tail - N_META, dm), F32), meta_f], axis=0)
    h0, hn1 = _rms_fwd_input("rms_mix", x[0], tail_rows, g1, deps=(token,))
    proj = _nn_sharded("proj", hn1, w_in4, 6)
    sems_mix, mixer_w = _ag_relay("ag_relay_mixer", mixer_w, sems_mix, [proj])
    (sems_gu, (w_gu4,)), (sems_down, (w_down4,)) = _ag_relay("ag_relay_gate_up", [w_gu4], sems_gu, [mixer_w[0]],
                                                              then_start=[cast["w_down"]])
    pooled, z = _mixer_fwd("mixer_fwd", proj, conv_w_f, tc, w_down4)
    sems_mix, mixer_w = _ag_relay_wait("ag_relay_wait_mixer", mixer_w, sems_mix, [pooled])
    pool4, conv_out4, w_o4 = _ag_final_wait("ag_final_wait_mixer", mixer_w, sems_mix, [])
    pool_f = jnp.transpose(pool4.reshape(n_chips, n_groups, gw // n_chips, gw), (1, 0, 2, 3)).reshape(n_groups, gw, gw)
    conv_out_f = conv_out4.reshape(dm, dm)
    w_o_f = w_o4.reshape(dm, dm)
    ya = _pool_fwd("pool_proj", pooled, pool_f)
    yb = _nn_plain("conv_out", z, conv_out_f, BF16)
    mix = _gate_mix("gate_mix", proj, b_gate2, ya, ps, yb, tm_row)
    sems_gu, (w_gu4,) = _ag_relay_wait("ag_relay_wait_gate_up", [w_gu4], sems_gu, [mix])
    h1 = _nn_plain("attn_out", mix, w_o_f, F32, res=h0, tn_pref=256)
    (w_gu4,) = _ag_final_wait("ag_final_wait_gate_up", [w_gu4], sems_gu, [h1])
    hn2 = _rms_fwd("rms_ffn", h1, g2, tm_row)
    sems_down, (w_down4,) = _ag_relay("ag_relay_down", [w_down4], sems_down, [hn2])
    gu, act = _gate_up_swiglu("gate_up", hn2, w_gu4, w_down4)
    sems_down, (w_down4,) = _ag_relay_wait("ag_relay_wait_down", [w_down4], sems_down, [act])
    (w_down4,) = _ag_final_wait("ag_final_wait_down", [w_down4], sems_down, [])
    w_down_f = w_down4.reshape(-1, dm)
    h2 = _nn_rows("ffn_down", act, w_down_f, h1)
    dh2, dh2b, loss_cols, dg3 = _final_loss("final_loss", h2, g3, loss_target[0], tm)

    def scatter(tag, names_g, swap, after):
        grads_g, got = _swap_wait("swap_wait_" + tag, *swap, [after])
        pairs = [_pair_add("pair_add_" + nme, g4, rv, core) for nme, g4, rv in zip(names_g, grads_g, got)]
        return _scatter_start("scatter_start_" + tag, pairs)

    dgu = _dact_swiglu_bwd("d_gate_up", dh2b, w_down_f, gu)
    gw_down = _tn_plain("dw_down", act, dh2b)
    gw_gu = _tn_sharded("dw_gate_up", hn2, dgu, n_chips)
    swap_a, token = _swap_start("swap_start_a", [gw_gu, gw_down.reshape(n_chips, -1, dm)])
    dhn2 = _nt_sharded("d_hn2", dgu, w_gu4, tr_pref=2816, row_tiles=2, deps=(token,))
    flight_a, token = scatter("a", ["w_gate_up", "w_down"], swap_a, dhn2)
    dh1, dh1b, dg2 = _rms_bwd("rms_ffn_bwd", dhn2, h1, g2, dh2, tm_row, token)
    dmix = _nt_plain("d_mix", dh1b, w_o_f)
    gw_o = _tn_plain("dw_o", mix, dh1b)
    dproj, dyb, dya, db_gate, dps = _gate_bwd("gate_bwd", dmix, proj, b_gate2, ya, ps, yb, tm_row)
    gw_conv_out = _tn_plain("dw_conv_out", z, dyb)
    gw_pool = _pool_bwd_w("dw_pool", pooled, dya)
    gw_pool = jnp.transpose(gw_pool.reshape(n_groups, n_chips, gw // n_chips, gw), (1, 0, 2, 3))
    dpooled = _pool_bwd_act("d_pooled", dya, pool_f)
    dz = _nt_plain("d_z", dyb, conv_out_f)
    dproj, dconv_w = _mixer_bwd("mixer_bwd", dz, dpooled, proj, conv_w_f, dproj, tc, dpooled)
    gw_in0 = _tn_sharded("dw_in_0", hn1, dproj, n_chips, part=(0, 2))
    swap_b, token = _swap_start("swap_start_b", [gw_o.reshape(n_chips, dloc, dm), gw_conv_out.reshape(n_chips, dloc, dm),
                                                 gw_pool.reshape(n_chips, n_groups * (gw // n_chips), gw), gw_in0])
    gw_in1 = _tn_sharded("dw_in_1", hn1, dproj, n_chips, part=(1, 2), deps=(token,))
    flight_b, token = scatter("b", ["w_o", "conv_out_w", "pool_w", "w_in_0"], swap_b, gw_in1)
    swap_c, token = _swap_start("swap_start_c", [gw_in1], deps=(token,))
    dhn1 = _nt_in_proj("d_hn1", dproj, w_in4, deps=(token,))
    flight_c, token = scatter("c", ["w_in_1"], swap_c, dhn1)
    dx, dtail, dg1 = _rms_bwd_input("rms_mix_bwd", dhn1, h0, g1, dh1, tm, seq, token)
    grad_x = dx[None]
    dmeta = dtail[tail - N_META:]

    given = dict(meta_tokens=(meta_tokens, m_meta_tokens, v_meta_tokens), norm_mix_g=(norm_mix_g, m_norm_mix_g, v_norm_mix_g),
                 w_in=(w_in, m_w_in, v_w_in), b_gate=(b_gate, m_b_gate, v_b_gate), pool_w=(pool_w, m_pool_w, v_pool_w),
                 pool_scale=(pool_scale, m_pool_scale, v_pool_scale), conv_w=(conv_w, m_conv_w, v_conv_w),
                 conv_out_w=(conv_out_w, m_conv_out_w, v_conv_out_w), w_o=(w_o, m_w_o, v_w_o),
                 norm_ffn_g=(norm_ffn_g, m_norm_ffn_g, v_norm_ffn_g), w_gate_up=(w_gate_up, m_w_gate_up, v_w_gate_up),
                 w_down=(w_down, m_w_down, v_w_down), norm_final_g=(norm_final_g, m_norm_final_g, v_norm_final_g))
    order = list(given.keys())
    grad, delta, new_m, new_v = {}, {}, {}, {}
    vec = jnp.concatenate([dg1, dg2, dg3, db_gate, dps, loss_cols, dconv_w, dmeta], axis=0)
    loss_row = 5 * SMALL_ROWS
    groups_g = {"a": [("w_gate_up", (0, 1)), ("w_down", (0, 1))],
                "b": [("w_o", (0, 1)), ("conv_out_w", (0, 1)), ("pool_w", (0, 1)), ("w_in", (0, 2))], "c": [("w_in", (1, 2))]}
    results = {}

    def reduced(tag, flight, after):
        pairs, zones = _scatter_wait("scatter_wait_" + tag, *flight, after)
        halves = [_chip_sum("chip_sum_%s_%d" % (nme, part[0]), p, rv, chip1) for (nme, part), p, rv in zip(groups_g[tag], pairs, zones)]
        return _swap_start("send_start_" + tag, halves, halves=False)

    def update(tag, send, after):
        halves, sib_halves = _swap_wait("send_wait_" + tag, *send, after, halves=False)
        deltas = []
        for (nme, part), g_own, g_sib in zip(groups_g[tag], halves, sib_halves):
            w, m, v = given[nme]
            shape2 = (2 * g_own.shape[0] * part[1], g_own.shape[1])
            results[nme] = _adamw_halves("adamw_%s_%d" % (nme, part[0]), w.reshape(shape2), g_own, g_sib, m.reshape(shape2),
                                         v.reshape(shape2), core, part=part, prev=results.get(nme))
            grad[nme], delta[nme], new_m[nme], new_v[nme] = [t.reshape(w.shape) for t in results[nme]]
            deltas.append(results[nme][1])
        return deltas

    send_a, token = reduced("a", flight_a, [dx])
    send_b, token = reduced("b", flight_b, [token])
    done_a = update("a", send_a, [token])
    send_c, token = reduced("c", flight_c, done_a)
    me1 = jnp.reshape(4 * cx + 2 * cy + cc, (1,)).astype(jnp.int32)
    red_flight, token = _reduce_start(vec, [token])
    done_b = update("b", send_b, [token])
    done_c = update("c", send_c, done_b)
    red, loss11 = _reduce_sum(*_reduce_wait(*red_flight, done_c), me1, loss_row, 0.5 / dm)
    loss = loss11[0, 0]
    col0 = chip * dloc
    g_small = {
        "norm_mix_g": red[0], "norm_ffn_g": red[SMALL_ROWS], "norm_final_g": red[2 * SMALL_ROWS],
        "b_gate": red[3 * SMALL_ROWS:3 * SMALL_ROWS + 2].reshape(-1), "pool_scale": red[4 * SMALL_ROWS],
        "conv_w": lax.dynamic_slice(red, (6 * SMALL_ROWS, col0), (3, dloc)),
        "meta_tokens": lax.dynamic_slice(red, (7 * SMALL_ROWS, col0), (N_META, dloc)),
    }

    vec_names = ["norm_mix_g", "norm_ffn_g", "norm_final_g", "pool_scale"]

    def slab_vec(pick):
        rows = [pick(nme).reshape(1, dm) for nme in vec_names] + [pick("b_gate").reshape(2, dm), jnp.zeros((2, dm), F32)]
        return jnp.concatenate(rows, axis=0)

    def slab_col(pick):
        return jnp.concatenate([pick("meta_tokens"), pick("conv_w"), jnp.zeros((5, dloc), F32)], axis=0)

    for slab, tag in ((slab_vec, "vec"), (slab_col, "col")):
        d, nm, nv = _adamw("adamw_small_" + tag, slab(lambda nme: given[nme][0]), slab(lambda nme: g_small[nme]),
                           slab(lambda nme: given[nme][1]), slab(lambda nme: given[nme][2]))
        for out, res in ((delta, d), (new_m, nm), (new_v, nv)):
            if tag == "vec":
                for i, nme in enumerate(vec_names):
                    out[nme] = res[i]
                out["b_gate"] = res[4:6].reshape(-1)
            else:
                out["meta_tokens"] = res[:N_META]
                out["conv_w"] = res[N_META:N_META + 3]
    grad.update(g_small)
    return (loss, grad_x, *[grad[nme] for nme in order], *[delta[nme] for nme in order],
            *[new_m[nme] for nme in order], *[new_v[nme] for nme in order])
```

```python
import math

import jax
import jax.numpy as jnp
from jax import lax
from jax.experimental import pallas as pl
from jax.experimental.pallas import tpu as pltpu

F32 = jnp.float32
BF16 = jnp.bfloat16
N_META = 16
POOL_WINDOWS = (2, 4, 8, 16)
EPS = 1e-6
ADAM_LR, ADAM_B1, ADAM_B2, ADAM_EPS, ADAM_WD, ADAM_STEP = 0.001, 0.9, 0.999, 1e-08, 0.01, 10
LANES = 128
V7X_VMEM_BYTES = 64 * 1024 * 1024
VMEM_LIMIT = V7X_VMEM_BYTES - 8 * 1024 * 1024
MESH = pl.DeviceIdType.MESH
ANY = pl.BlockSpec(memory_space=pl.ANY)
CHIP_FLIPS = ((1, 0), (0, 1), (1, 1))
SMALL_ROWS = 8
TAIL_ROWS = 64


def _pick(n, pref):
    best = None
    for t in range(LANES, min(n, pref) + 1, LANES):
        if n % t == 0:
            best = t
    assert best is not None, (n, pref)
    return best


def _params(n_axes=0):
    sem = ("arbitrary",) * n_axes if n_axes else None
    return pltpu.CompilerParams(dimension_semantics=sem, vmem_limit_bytes=VMEM_LIMIT)


_DIMS = {
    "nn": (((1,), (0,)), ((), ())),
    "nt": (((1,), (1,)), ((), ())),
    "tn": (((0,), (0,)), ((), ())),
}


def _matmul(name, mode, a, b, out_sds, grid, a_spec, b_spec, o_spec, nk, res=None, res_spec=None, acc_shape=None, deps=()):
    out_dtype = out_sds.dtype
    in_place = nk > 1 and out_dtype == F32
    use_scratch = nk > 1 and not in_place
    rows = a_spec.block_shape[-2] if mode != "tn" else None
    chunk = _row_tile(rows, 1, 1, 1152) if rows is not None else None
    n_in = 2 + (res is not None) + len(deps)

    def body(*refs):
        a_ref, b_ref = refs[:2]
        r_ref = refs[2] if res is not None else None
        o_ref, *scr = refs[n_in:]
        k = pl.program_id(len(grid) - 1) if nk > 1 else None

        def emit(sl):
            if sl is None:
                part = lax.dot_general(a_ref[...], b_ref[...], _DIMS[mode], preferred_element_type=F32)
                idx = (slice(None), slice(None))
            else:
                part = lax.dot_general(a_ref[sl, :], b_ref[...], _DIMS[mode], preferred_element_type=F32)
                idx = (sl, slice(None))
            if nk == 1:
                if r_ref is not None:
                    part = part + r_ref[idx]
                o_ref[idx] = part.astype(out_dtype)
                return
            acc = scr[0] if use_scratch else o_ref

            @pl.when(k == 0)
            def _():
                first = part
                if r_ref is not None and in_place:
                    first = first + r_ref[idx]
                acc[idx] = first

            @pl.when(k > 0)
            def _():
                acc[idx] += part

            if use_scratch:

                @pl.when(k == nk - 1)
                def _():
                    o_ref[idx] = acc[idx].astype(out_dtype)

        if mode == "tn" or chunk == rows:
            emit(None)
        else:
            for m0 in range(0, rows, chunk):
                emit(pl.ds(m0, chunk))

    ins = [a, b] + ([res] if res is not None else []) + list(deps)
    in_specs = [a_spec, b_spec] + ([res_spec] if res is not None else []) + [ANY] * len(deps)
    scratch = [pltpu.VMEM(acc_shape, F32)] if use_scratch else []
    return pl.pallas_call(
        body, name=name, out_shape=out_sds, grid=grid, in_specs=in_specs, out_specs=o_spec,
        scratch_shapes=scratch, compiler_params=_params(len(grid)),
    )(*ins)


def _nn_sharded(name, a, w4, nseg):
    lp, kdim = a.shape
    s, _, nloc = w4.shape
    segw = s * nloc // nseg
    tn = _pick(math.gcd(nloc, segw), 1536)
    bw, bo = nloc // tn, segw // tn
    return _matmul(
        name, "nn", a, w4, jax.ShapeDtypeStruct((nseg, lp, segw), BF16), (s * bw,),
        pl.BlockSpec((lp, kdim), lambda j: (0, 0)),
        pl.BlockSpec((None, kdim, tn), lambda j: (j // bw, 0, j % bw)),
        pl.BlockSpec((None, lp, tn), lambda j: (j // bo, 0, j % bo)), 1)


def _nt_in_proj(name, dseg, w4, row_tiles=2, to_pref=1024, deps=()):
    nseg, lp, segw = dseg.shape
    s, kdim, nloc = w4.shape
    assert nseg * segw == s * nloc and 2 * nloc == 3 * segw, (dseg.shape, w4.shape)
    half = segw // 2
    to = _pick(kdim, to_pref)
    tm = lp // row_tiles

    def body(full_ref, half_ref, w_ref, *rest):
        o_ref = rest[len(deps)]
        r = pl.program_id(2)

        def contribution(full_first):
            lo, hi = (pl.ds(0, segw), pl.ds(segw, half)) if full_first else (pl.ds(half, segw), pl.ds(0, half))
            return (lax.dot_general(full_ref[...], w_ref[:, lo], _DIMS["nt"], preferred_element_type=F32)
                    + lax.dot_general(half_ref[...], w_ref[:, hi], _DIMS["nt"], preferred_element_type=F32))

        @pl.when(r == 0)
        def _():
            o_ref[...] = contribution(True)

        for ri in range(1, s):

            @pl.when(r == ri)
            def _(ri=ri):
                o_ref[...] += contribution(ri % 2 == 0)

    return pl.pallas_call(
        body, name=name, out_shape=jax.ShapeDtypeStruct((lp, kdim), F32), grid=(row_tiles, kdim // to, s),
        in_specs=[pl.BlockSpec((None, tm, segw), lambda m, j, r: ((3 * r + 1) // 2, m, 0)),
                  pl.BlockSpec((None, tm, half), lambda m, j, r: (1 + 3 * (r // 2), m, r % 2)),
                  pl.BlockSpec((None, to, nloc), lambda m, j, r: (r, j, 0))] + [ANY] * len(deps),
        out_specs=pl.BlockSpec((tm, to), lambda m, j, r: (m, j)), compiler_params=_params(3),
    )(dseg, dseg, w4, *deps)


def _nn_plain(name, a, w, out_dtype, res=None, tn_pref=512, tk_pref=2048, deps=()):
    lp, kdim = a.shape
    n = w.shape[1]
    tn = _pick(n, tn_pref)
    tk = kdim if kdim <= tk_pref else _pick(kdim, tk_pref)
    nk = kdim // tk
    grid = (n // tn, nk) if nk > 1 else (n // tn,)
    if nk > 1:
        a_spec = pl.BlockSpec((lp, tk), lambda j, k: (0, k))
        w_spec = pl.BlockSpec((tk, tn), lambda j, k: (k, j))
        o_spec = pl.BlockSpec((lp, tn), lambda j, k: (0, j))
    else:
        a_spec = pl.BlockSpec((lp, tk), lambda j: (0, 0))
        w_spec = pl.BlockSpec((tk, tn), lambda j: (0, j))
        o_spec = pl.BlockSpec((lp, tn), lambda j: (0, j))
    return _matmul(name, "nn", a, w, jax.ShapeDtypeStruct((lp, n), out_dtype), grid, a_spec, w_spec, o_spec, nk,
                   res=res, res_spec=o_spec if res is not None else None, acc_shape=(lp, tn), deps=deps)


def _nt_plain(name, a, w, tn_pref=512):
    lp, kdim = a.shape
    n = w.shape[0]
    tn = _pick(n, tn_pref)
    return _matmul(
        name, "nt", a, w, jax.ShapeDtypeStruct((lp, n), BF16), (n // tn,),
        pl.BlockSpec((lp, kdim), lambda j: (0, 0)),
        pl.BlockSpec((tn, kdim), lambda j: (j, 0)),
        pl.BlockSpec((lp, tn), lambda j: (0, j)), 1)


def _nt_sharded(name, dseg, w4, to_pref=1024, tr_pref=1536, row_tiles=1, deps=()):
    nseg, lp, segw = dseg.shape
    s, kdim, nloc = w4.shape
    tr = _pick(math.gcd(nloc, segw), tr_pref)
    ba, bw = segw // tr, nloc // tr
    nr = s * bw
    to = _pick(kdim, to_pref)
    tm = lp // row_tiles
    return _matmul(
        name, "nt", dseg, w4, jax.ShapeDtypeStruct((lp, kdim), F32), (row_tiles, kdim // to, nr),
        pl.BlockSpec((None, tm, tr), lambda m, j, r: (r // ba, m, r % ba)),
        pl.BlockSpec((None, to, tr), lambda m, j, r: (r // bw, j, r % bw)),
        pl.BlockSpec((tm, to), lambda m, j, r: (m, j)), nr, deps=deps)


def _nn_rows(name, a, w, res, row_tiles=2, tn_pref=512):
    lp, kdim = a.shape
    n = w.shape[1]
    tn = _pick(n, tn_pref)
    tm = lp // row_tiles
    blk = pl.BlockSpec((tm, tn), lambda i, j: (i, j))
    return _matmul(name, "nn", a, w, jax.ShapeDtypeStruct((lp, n), F32), (row_tiles, n // tn),
                   pl.BlockSpec((tm, kdim), lambda i, j: (i, 0)), pl.BlockSpec((kdim, tn), lambda i, j: (0, j)), blk, 1,
                   res=res, res_spec=blk)


def _tn_plain(name, a, d, tk_pref=1024):
    lp, kdim = a.shape
    n = d.shape[1]
    tk = _pick(kdim, tk_pref)
    return _matmul(
        name, "tn", a, d, jax.ShapeDtypeStruct((kdim, n), BF16), (kdim // tk,),
        pl.BlockSpec((lp, tk), lambda i: (0, i)),
        pl.BlockSpec((lp, n), lambda i: (0, 0)),
        pl.BlockSpec((tk, n), lambda i: (i, 0)), 1)


def _tn_sharded(name, a, dseg, s, part=(0, 1), tk_pref=1024, deps=()):
    lp, kdim = a.shape
    nseg, _, segw = dseg.shape
    nloc = nseg * segw // s
    tn = _pick(math.gcd(nloc, segw), 1536)
    bd, bo = segw // tn, nloc // tn
    kpart = kdim // part[1]
    tk = _pick(kpart, tk_pref)
    i0 = part[0] * (kpart // tk)

    def body(a_ref, d_ref, *rest):
        o_ref, at_ref = rest[len(deps):]

        @pl.when(pl.program_id(1) == 0)
        def _():
            at_ref[...] = a_ref[...].T

        o_ref[...] = jnp.dot(at_ref[...], d_ref[...], preferred_element_type=F32).astype(BF16)

    return pl.pallas_call(
        body, name=name, out_shape=jax.ShapeDtypeStruct((s, kpart, nloc), BF16), grid=(kpart // tk, s * bo),
        in_specs=[pl.BlockSpec((lp, tk), lambda i, j: (0, i0 + i)),
                  pl.BlockSpec((None, lp, tn), lambda i, j: (j // bd, 0, j % bd))] + [ANY] * len(deps),
        out_specs=pl.BlockSpec((None, tk, tn), lambda i, j: (j // bo, i, j % bo)),
        scratch_shapes=[pltpu.VMEM((tk, lp), BF16)], compiler_params=_params(2),
    )(a, dseg, *deps)


def _silu_parts(gt):
    sg = jax.nn.sigmoid(gt)
    return gt * sg, sg * (1.0 + gt * (1.0 - sg))


def _gate_up_swiglu(name, a, w4, dep, tn_pref=256):
    lp, kdim = a.shape
    s, _, nloc = w4.shape
    f = s * nloc // 2
    tn = _pick(nloc, tn_pref)
    bw = nloc // tn
    chunk = _row_tile(lp, 1, 1, 576)

    def body(a_ref, wg_ref, wu_ref, _, fac_ref, act_ref):
        for m0 in range(0, lp, chunk):
            sl = pl.ds(m0, chunk)
            gt = jnp.dot(a_ref[sl, :], wg_ref[...], preferred_element_type=F32)
            up = jnp.dot(a_ref[sl, :], wu_ref[...], preferred_element_type=F32)
            silu, dsilu = _silu_parts(gt)
            fac_ref[0, sl, :] = (up * dsilu).astype(BF16)
            fac_ref[1, sl, :] = silu.astype(BF16)
            act_ref[sl, :] = (silu * up).astype(BF16)

    return pl.pallas_call(
        body, name=name, grid=(f // tn,),
        out_shape=(jax.ShapeDtypeStruct((2, lp, f), BF16), jax.ShapeDtypeStruct((lp, f), BF16)),
        in_specs=[pl.BlockSpec((lp, kdim), lambda j: (0, 0)),
                  pl.BlockSpec((None, kdim, tn), lambda j: (j // bw, 0, j % bw)),
                  pl.BlockSpec((None, kdim, tn), lambda j: (s // 2 + j // bw, 0, j % bw)), ANY],
        out_specs=(pl.BlockSpec((2, lp, tn), lambda j: (0, 0, j)), pl.BlockSpec((lp, tn), lambda j: (0, j))),
        compiler_params=_params(1),
    )(a, w4, w4, dep)


def _dact_swiglu_bwd(name, d, w, gu, tn_pref=512):
    lp, dm = d.shape
    f = w.shape[0]
    tn = _pick(f, tn_pref)
    chunk = _row_tile(lp, 1, 1, 576)

    def body(d_ref, w_ref, g_ref, u_ref, o_ref):
        for m0 in range(0, lp, chunk):
            sl = pl.ds(m0, chunk)
            dact = lax.dot_general(d_ref[sl, :], w_ref[...], _DIMS["nt"], preferred_element_type=F32)
            o_ref[0, sl, :] = (dact * g_ref[sl, :].astype(F32)).astype(BF16)
            o_ref[1, sl, :] = (dact * u_ref[sl, :].astype(F32)).astype(BF16)

    return pl.pallas_call(
        body, name=name, grid=(f // tn,), out_shape=jax.ShapeDtypeStruct((2, lp, f), BF16),
        in_specs=[pl.BlockSpec((lp, dm), lambda j: (0, 0)), pl.BlockSpec((tn, dm), lambda j: (j, 0)),
                  pl.BlockSpec((None, lp, tn), lambda j: (0, 0, j)), pl.BlockSpec((None, lp, tn), lambda j: (1, 0, j))],
        out_specs=pl.BlockSpec((2, lp, tn), lambda j: (0, 0, j)), compiler_params=_params(1),
    )(d, w, gu, gu)


def _pool_fwd(name, pooled, pw):
    lp, dm = pooled.shape
    g, gw, _ = pw.shape
    return _matmul(
        name, "nn", pooled, pw, jax.ShapeDtypeStruct((lp, dm), BF16), (g,),
        pl.BlockSpec((lp, gw), lambda gi: (0, gi)), pl.BlockSpec((None, gw, gw), lambda gi: (gi, 0, 0)),
        pl.BlockSpec((lp, gw), lambda gi: (0, gi)), 1)


def _pool_bwd_act(name, dya, pw, deps=()):
    lp, dm = dya.shape
    g, gw, _ = pw.shape
    return _matmul(
        name, "nt", dya, pw, jax.ShapeDtypeStruct((lp, dm), BF16), (g,),
        pl.BlockSpec((lp, gw), lambda gi: (0, gi)), pl.BlockSpec((None, gw, gw), lambda gi: (gi, 0, 0)),
        pl.BlockSpec((lp, gw), lambda gi: (0, gi)), 1, deps=deps)


def _pool_bwd_w(name, pooled, dya):
    lp, dm = pooled.shape
    g = len(POOL_WINDOWS)
    gw = dm // g
    return _matmul(
        name, "tn", pooled, dya, jax.ShapeDtypeStruct((g, gw, gw), BF16), (g,),
        pl.BlockSpec((lp, gw), lambda gi: (0, gi)), pl.BlockSpec((lp, gw), lambda gi: (0, gi)),
        pl.BlockSpec((None, gw, gw), lambda gi: (gi, 0, 0)), 1)


def _rms_fwd(name, h, g, tm, deps=()):
    lp, dm = h.shape

    def body(h_ref, g_ref, *rest):
        hv = h_ref[...]
        r = lax.rsqrt(jnp.mean(hv * hv, axis=-1, keepdims=True) + EPS)
        rest[-1][...] = (hv * r * g_ref[...]).astype(BF16)

    row = pl.BlockSpec((tm, dm), lambda i: (i, 0))
    return pl.pallas_call(
        body, name=name, out_shape=jax.ShapeDtypeStruct((lp, dm), BF16), grid=(lp // tm,),
        in_specs=[row, pl.BlockSpec((1, dm), lambda i: (0, 0))] + [ANY] * len(deps), out_specs=row, compiler_params=_params(1),
    )(h, g, *deps)


def _rms_fwd_input(name, x2d, tail_rows, g, deps=()):
    seq, dm = x2d.shape
    tm = tail_rows.shape[0]
    nx = seq // tm
    lp = seq + tm

    def body(x_ref, t_ref, g_ref, *rest):
        h_ref, o_ref = rest[len(deps):]

        def emit(hv):
            r = lax.rsqrt(jnp.mean(hv * hv, axis=-1, keepdims=True) + EPS)
            h_ref[...] = hv
            o_ref[...] = (hv * r * g_ref[...]).astype(BF16)

        @pl.when(pl.program_id(0) < nx)
        def _():
            emit(x_ref[...])

        @pl.when(pl.program_id(0) >= nx)
        def _():
            emit(t_ref[...])

    row = pl.BlockSpec((tm, dm), lambda i: (i, 0))
    return pl.pallas_call(
        body, name=name, grid=(lp // tm,),
        out_shape=(jax.ShapeDtypeStruct((lp, dm), F32), jax.ShapeDtypeStruct((lp, dm), BF16)),
        in_specs=[pl.BlockSpec((tm, dm), lambda i: (jnp.minimum(i, nx - 1), 0)), pl.BlockSpec((tm, dm), lambda i: (0, 0)),
                  pl.BlockSpec((1, dm), lambda i: (0, 0))] + [ANY] * len(deps),
        out_specs=(row, row), compiler_params=_params(1),
    )(x2d, tail_rows, g, *deps)


def _rms_bwd(name, dy, h, g, dres, tm, dep):
    lp, dm = h.shape

    def body(dy_ref, h_ref, g_ref, dr_ref, _, dh_ref, dhb_ref, dg_ref):
        hv = h_ref[...]
        r = lax.rsqrt(jnp.mean(hv * hv, axis=-1, keepdims=True) + EPS)
        xhat = hv * r
        dyv = dy_ref[...]
        dxh = dyv * g_ref[...]
        dh = dr_ref[...] + r * (dxh - xhat * jnp.mean(dxh * xhat, axis=-1, keepdims=True))
        dh_ref[...] = dh
        dhb_ref[...] = dh.astype(BF16)

        @pl.when(pl.program_id(0) == 0)
        def _():
            dg_ref[...] = jnp.zeros_like(dg_ref)

        dg_ref[0:1, :] += jnp.sum(dyv * xhat, axis=0, keepdims=True)

    row = pl.BlockSpec((tm, dm), lambda i: (i, 0))
    slab = pl.BlockSpec((SMALL_ROWS, dm), lambda i: (0, 0))
    return pl.pallas_call(
        body, name=name, grid=(lp // tm,),
        out_shape=(jax.ShapeDtypeStruct((lp, dm), F32), jax.ShapeDtypeStruct((lp, dm), BF16),
                   jax.ShapeDtypeStruct((SMALL_ROWS, dm), F32)),
        in_specs=[row, row, pl.BlockSpec((1, dm), lambda i: (0, 0)), row, ANY], out_specs=(row, row, slab),
        compiler_params=_params(1),
    )(dy, h, g, dres, dep)


def _rms_bwd_input(name, dy, h, g, dres, tm, seq, dep):
    lp, dm = h.shape
    nx = seq // tm

    def body(dy_ref, h_ref, g_ref, dr_ref, _, dx_ref, dt_ref, dg_ref):
        i = pl.program_id(0)
        hv = h_ref[...]
        r = lax.rsqrt(jnp.mean(hv * hv, axis=-1, keepdims=True) + EPS)
        xhat = hv * r
        dyv = dy_ref[...]
        dxh = dyv * g_ref[...]
        dh = dr_ref[...] + r * (dxh - xhat * jnp.mean(dxh * xhat, axis=-1, keepdims=True))

        @pl.when(i < nx)
        def _():
            dx_ref[...] = dh

        @pl.when(i >= nx)
        def _():
            dt_ref[...] = dh

        @pl.when(i == 0)
        def _():
            dg_ref[...] = jnp.zeros_like(dg_ref)

        dg_ref[0:1, :] += jnp.sum(dyv * xhat, axis=0, keepdims=True)

    row = pl.BlockSpec((tm, dm), lambda i: (i, 0))
    slab = pl.BlockSpec((SMALL_ROWS, dm), lambda i: (0, 0))
    return pl.pallas_call(
        body, name=name, grid=(lp // tm,),
        out_shape=(jax.ShapeDtypeStruct((seq, dm), F32), jax.ShapeDtypeStruct((tm, dm), F32),
                   jax.ShapeDtypeStruct((SMALL_ROWS, dm), F32)),
        in_specs=[row, row, pl.BlockSpec((1, dm), lambda i: (0, 0)), row, ANY],
        out_specs=(pl.BlockSpec((tm, dm), lambda i: (jnp.minimum(i, nx - 1), 0)), pl.BlockSpec((tm, dm), lambda i: (0, 0)), slab),
        compiler_params=_params(1),
    )(dy, h, g, dres, dep)


def _gate_mix(name, proj, b_gate2, ya, pool_scale, yb, tm):
    _, lp, dm = proj.shape

    def body(ga_ref, gr_ref, b_ref, ya_ref, ps_ref, yb_ref, o_ref):
        g_a = jax.nn.sigmoid(ga_ref[...].astype(F32) + b_ref[0:1, :])
        g_b = jax.nn.sigmoid(gr_ref[...].astype(F32) + b_ref[1:2, :])
        y_a = ya_ref[...].astype(F32) * ps_ref[...]
        o_ref[...] = (g_a * y_a + g_b * yb_ref[...].astype(F32)).astype(BF16)

    row = pl.BlockSpec((tm, dm), lambda i: (i, 0))
    return pl.pallas_call(
        body, name=name, out_shape=jax.ShapeDtypeStruct((lp, dm), BF16), grid=(lp // tm,),
        in_specs=[pl.BlockSpec((None, tm, dm), lambda i: (4, i, 0)), pl.BlockSpec((None, tm, dm), lambda i: (5, i, 0)),
                  pl.BlockSpec((2, dm), lambda i: (0, 0)), row, pl.BlockSpec((1, dm), lambda i: (0, 0)), row],
        out_specs=row, compiler_params=_params(1),
    )(proj, proj, b_gate2, ya, pool_scale, yb)


def _gate_bwd(name, dmix, proj, b_gate2, ya, pool_scale, yb, tm):
    _, lp, dm = proj.shape

    def body(dm_ref, ga_ref, gr_ref, b_ref, ya_ref, ps_ref, yb_ref, dp_ref, dyb_ref, dya_ref, db_ref, dps_ref):
        dmx = dm_ref[...].astype(F32)
        g_a = jax.nn.sigmoid(ga_ref[...].astype(F32) + b_ref[0:1, :])
        g_b = jax.nn.sigmoid(gr_ref[...].astype(F32) + b_ref[1:2, :])
        ya_pre = ya_ref[...].astype(F32)
        ybv = yb_ref[...].astype(F32)
        ps = ps_ref[...]
        dga = dmx * (ya_pre * ps) * (g_a * (1.0 - g_a))
        dgr = dmx * ybv * (g_b * (1.0 - g_b))
        dp_ref[0] = dga.astype(BF16)
        dp_ref[1] = dgr.astype(BF16)
        dyb_ref[...] = (dmx * g_b).astype(BF16)
        dya_ref[...] = (dmx * g_a * ps).astype(BF16)

        @pl.when(pl.program_id(0) == 0)
        def _():
            db_ref[...] = jnp.zeros_like(db_ref)
            dps_ref[...] = jnp.zeros_like(dps_ref)

        db_ref[0:1, :] += jnp.sum(dga, axis=0, keepdims=True)
        db_ref[1:2, :] += jnp.sum(dgr, axis=0, keepdims=True)
        dps_ref[0:1, :] += jnp.sum(dmx * g_a * ya_pre, axis=0, keepdims=True)

    row = pl.BlockSpec((tm, dm), lambda i: (i, 0))
    one = pl.BlockSpec((1, dm), lambda i: (0, 0))
    slab = pl.BlockSpec((SMALL_ROWS, dm), lambda i: (0, 0))
    return pl.pallas_call(
        body, name=name, grid=(lp // tm,),
        out_shape=(jax.ShapeDtypeStruct((6, lp, dm), BF16), jax.ShapeDtypeStruct((lp, dm), BF16),
                   jax.ShapeDtypeStruct((lp, dm), BF16), jax.ShapeDtypeStruct((SMALL_ROWS, dm), F32),
                   jax.ShapeDtypeStruct((SMALL_ROWS, dm), F32)),
        in_specs=[row, pl.BlockSpec((None, tm, dm), lambda i: (4, i, 0)), pl.BlockSpec((None, tm, dm), lambda i: (5, i, 0)),
                  pl.BlockSpec((2, dm), lambda i: (0, 0)), row, one, row],
        out_specs=(pl.BlockSpec((2, tm, dm), lambda i: (2, i, 0)), row, row, slab, slab),
        compiler_params=_params(1),
    )(dmix, proj, proj, b_gate2, ya, pool_scale, yb)


def _final_loss(name, h2, g3, target, tm):
    lp, dm = h2.shape
    nx = target.shape[0] // tm

    def body(h_ref, g_ref, t_ref, dh_ref, dhb_ref, ls_ref, dg_ref):
        i = pl.program_id(0)

        @pl.when(i == 0)
        def _():
            ls_ref[...] = jnp.zeros_like(ls_ref)
            dg_ref[...] = jnp.zeros_like(dg_ref)

        @pl.when(i < nx)
        def _():
            hv = h_ref[...]
            gv = g_ref[...]
            r = lax.rsqrt(jnp.mean(hv * hv, axis=-1, keepdims=True) + EPS)
            xhat = hv * r
            err = xhat * gv - t_ref[...]
            dout = err * (1.0 / dm)
            dxh = dout * gv
            dh = r * (dxh - xhat * jnp.mean(dxh * xhat, axis=-1, keepdims=True))
            dh_ref[...] = dh
            dhb_ref[...] = dh.astype(BF16)
            ls_ref[0:1, :] += jnp.sum(err * err, axis=0, keepdims=True)
            dg_ref[0:1, :] += jnp.sum(dout * xhat, axis=0, keepdims=True)

        @pl.when(i >= nx)
        def _():
            dh_ref[...] = jnp.zeros_like(dh_ref)
            dhb_ref[...] = jnp.zeros_like(dhb_ref)

    row = pl.BlockSpec((tm, dm), lambda i: (i, 0))
    slab = pl.BlockSpec((SMALL_ROWS, dm), lambda i: (0, 0))
    return pl.pallas_call(
        body, name=name, grid=(lp // tm,),
        out_shape=(jax.ShapeDtypeStruct((lp, dm), F32), jax.ShapeDtypeStruct((lp, dm), BF16),
                   jax.ShapeDtypeStruct((SMALL_ROWS, dm), F32), jax.ShapeDtypeStruct((SMALL_ROWS, dm), F32)),
        in_specs=[row, pl.BlockSpec((1, dm), lambda i: (0, 0)), pl.BlockSpec((tm, dm), lambda i: (jnp.minimum(i, nx - 1), 0))],
        out_specs=(row, row, slab, slab), compiler_params=_params(1),
    )(h2, g3, target)


def _shift(v, k):
    return pltpu.roll(v, k % v.shape[0], axis=0)


def _window_sum(v, group, sign):
    s2 = v + _shift(v, sign * 1)
    s4 = s2 + _shift(s2, sign * 2)
    s8 = s4 + _shift(s4, sign * 4)
    s16 = s8 + _shift(s8, sign * 8)
    return jnp.where(group == 0, s2, jnp.where(group == 1, s4, jnp.where(group == 2, s8, s16)))


def _pool_count(lp, group):
    row = lax.broadcasted_iota(jnp.int32, (lp, 1), 0)
    window = jnp.left_shift(2, group).astype(F32)
    meta_pos = (row - (lp - N_META) + 1).astype(F32)
    return jnp.where(row >= lp - N_META, jnp.minimum(meta_pos, window), window)


def _mixer_fwd(name, proj, conv_w, tc, dep):
    _, lp, dm = proj.shape
    per_group = dm // len(POOL_WINDOWS) // tc

    def body(u_ref, gb_ref, gc_ref, v_ref, cw_ref, _, p_ref, z_ref):
        group = pl.program_id(0) // per_group
        u = u_ref[...].astype(F32)
        p_ref[...] = (_window_sum(u, group, 1) / _pool_count(lp, group) - u).astype(BF16)
        cv = gc_ref[...].astype(F32) * v_ref[...].astype(F32)
        conv = cw_ref[0:1, :] * _shift(cv, 2) + cw_ref[1:2, :] * _shift(cv, 1) + cw_ref[2:3, :] * cv
        z_ref[...] = (gb_ref[...].astype(F32) * conv).astype(BF16)

    def seg(s):
        return pl.BlockSpec((None, lp, tc), lambda j: (s, 0, j))

    col = pl.BlockSpec((lp, tc), lambda j: (0, j))
    return pl.pallas_call(
        body, name=name, grid=(dm // tc,),
        out_shape=(jax.ShapeDtypeStruct((lp, dm), BF16), jax.ShapeDtypeStruct((lp, dm), BF16)),
        in_specs=[seg(0), seg(1), seg(2), seg(3), pl.BlockSpec((3, tc), lambda j: (0, j)), ANY],
        out_specs=(col, col), compiler_params=_params(1),
    )(proj, proj, proj, proj, conv_w, dep)


def _mixer_bwd(name, dz, dpooled, proj, conv_w, dproj, tc, dep):
    _, lp, dm = proj.shape
    per_group = dm // len(POOL_WINDOWS) // tc

    def body(dz_ref, dp_ref, gb_ref, gc_ref, v_ref, cw_ref, _, __, o_ref, dcw_ref):
        group = pl.program_id(0) // per_group
        dzv = dz_ref[...].astype(F32)
        gb = gb_ref[...].astype(F32)
        gc = gc_ref[...].astype(F32)
        vv = v_ref[...].astype(F32)
        cv = gc * vv
        c1 = _shift(cv, 1)
        c2 = _shift(cv, 2)
        w0, w1, w2 = cw_ref[0:1, :], cw_ref[1:2, :], cw_ref[2:3, :]
        o_ref[1] = (dzv * (w0 * c2 + w1 * c1 + w2 * cv)).astype(BF16)
        dconv = dzv * gb
        dcw_ref[...] = jnp.zeros_like(dcw_ref)
        dcw_ref[0:1, :] = jnp.sum(dconv * c2, axis=0, keepdims=True)
        dcw_ref[1:2, :] = jnp.sum(dconv * c1, axis=0, keepdims=True)
        dcw_ref[2:3, :] = jnp.sum(dconv * cv, axis=0, keepdims=True)
        dcv = w0 * _shift(dconv, -2) + w1 * _shift(dconv, -1) + w2 * dconv
        o_ref[2] = (dcv * vv).astype(BF16)
        o_ref[3] = (dcv * gc).astype(BF16)
        dpv = dp_ref[...].astype(F32)
        o_ref[0] = (_window_sum(dpv / _pool_count(lp, group), group, -1) - dpv).astype(BF16)

    def seg(s):
        return pl.BlockSpec((None, lp, tc), lambda j: (s, 0, j))

    col = pl.BlockSpec((lp, tc), lambda j: (0, j))
    return pl.pallas_call(
        body, name=name, grid=(dm // tc,),
        out_shape=(jax.ShapeDtypeStruct(dproj.shape, BF16), jax.ShapeDtypeStruct((SMALL_ROWS, dm), F32)),
        in_specs=[col, col, seg(1), seg(2), seg(3), pl.BlockSpec((3, tc), lambda j: (0, j)), ANY, ANY],
        out_specs=(pl.BlockSpec((4, lp, tc), lambda j: (0, 0, j)), pl.BlockSpec((SMALL_ROWS, tc), lambda j: (0, j))),
        input_output_aliases={6: 0}, compiler_params=_params(1),
    )(dz, dpooled, proj, proj, proj, conv_w, dproj, dep)


def _row_tile(r, c, bytes_per_row_elem=4, budget=2 * 1024 * 1024):
    best = None
    for t in range(16, r + 1, 16):
        if r % t == 0 and t * c * bytes_per_row_elem <= budget:
            best = t
    return best if best is not None else r


def _pair_add(name, g4, recv, core):
    s, r, c = g4.shape
    h = r // 2
    tr = _row_tile(h, c, budget=6 * 1024 * 1024)
    nb = h // tr

    def body(core_ref, g_ref, r_ref, o_ref):
        o_ref[...] = (g_ref[...].astype(F32) + r_ref[...].astype(F32)).astype(BF16)

    grid_spec = pltpu.PrefetchScalarGridSpec(
        num_scalar_prefetch=1, grid=(s, nb),
        in_specs=[pl.BlockSpec((None, tr, c), lambda si, j, core_ref: (si, core_ref[0] * nb + j, 0)),
                  pl.BlockSpec((None, tr, c), lambda si, j, core_ref: (si, j, 0))],
        out_specs=pl.BlockSpec((None, tr, c), lambda si, j, core_ref: (si, j, 0)))
    return pl.pallas_call(
        body, name=name, out_shape=jax.ShapeDtypeStruct((s, h, c), BF16), grid_spec=grid_spec,
        compiler_params=_params(2),
    )(core, g4, recv)


def _chip_sum(name, parts, recv, chip):
    _, h, c = parts.shape
    tr = _row_tile(h, c)

    def body(chip_ref, p_ref, r_ref, o_ref):
        acc = p_ref[...].astype(F32)
        for i in range(len(CHIP_FLIPS)):
            acc = acc + r_ref[i].astype(F32)
        o_ref[...] = acc

    grid_spec = pltpu.PrefetchScalarGridSpec(
        num_scalar_prefetch=1, grid=(h // tr,),
        in_specs=[pl.BlockSpec((None, tr, c), lambda j, chip_ref: (chip_ref[0], j, 0)),
                  pl.BlockSpec((len(CHIP_FLIPS), tr, c), lambda j, chip_ref: (0, j, 0))],
        out_specs=pl.BlockSpec((tr, c), lambda j, chip_ref: (j, 0)))
    return pl.pallas_call(
        body, name=name, out_shape=jax.ShapeDtypeStruct((h, c), F32), grid_spec=grid_spec, compiler_params=_params(1),
    )(chip, parts, recv)


def _adam_update(w, gv, m, v):
    c1 = 1.0 - ADAM_B1 ** ADAM_STEP
    c2 = 1.0 - ADAM_B2 ** ADAM_STEP
    nm = ADAM_B1 * m + (1.0 - ADAM_B1) * gv
    nv = ADAM_B2 * v + (1.0 - ADAM_B2) * (gv * gv)
    return -ADAM_LR * ((nm / c1) / (jnp.sqrt(nv / c2) + ADAM_EPS) + ADAM_WD * w), nm, nv


def _adamw_halves(name, w, g_own, g_sib, m, v, core, part=(0, 1), prev=None):
    r, c = w.shape
    rp = r // part[1]
    h = rp // 2
    tr = _row_tile(h, c, budget=2 * 1024 * 1024)
    nbh = h // tr
    j0 = part[0] * 2 * nbh
    n_prev = 0 if prev is None else 4

    def body(core_ref, w_ref, go_ref, gs_ref, m_ref, v_ref, *rest):
        g_ref, d_ref, nm_ref, nv_ref = rest[n_prev:]
        mine = (pl.program_id(0) // nbh) == core_ref[0]
        gv = jnp.where(mine, go_ref[...], gs_ref[...])
        g_ref[...] = gv
        d_ref[...], nm_ref[...], nv_ref[...] = _adam_update(w_ref[...], gv, m_ref[...], v_ref[...])

    def blk(fn):
        return pl.BlockSpec((tr, c), fn)

    full = blk(lambda j, core_ref: (j0 + j, 0))
    own = blk(lambda j, core_ref: (jnp.clip(j - core_ref[0] * nbh, 0, nbh - 1), 0))
    sib = blk(lambda j, core_ref: (jnp.clip(j - (1 - core_ref[0]) * nbh, 0, nbh - 1), 0))
    grid_spec = pltpu.PrefetchScalarGridSpec(
        num_scalar_prefetch=1, grid=(2 * nbh,), in_specs=[full, own, sib, full, full] + [ANY] * n_prev, out_specs=(full,) * 4)
    sds = jax.ShapeDtypeStruct((r, c), F32)
    return pl.pallas_call(
        body, name=name, out_shape=(sds,) * 4, grid_spec=grid_spec, compiler_params=_params(1),
        input_output_aliases={6 + i: i for i in range(n_prev)},
    )(core, w, g_own, g_sib, m, v, *(prev or ()))


def _adamw(name, w, g, m, v):
    r, c = w.shape

    def body(w_ref, g_ref, m_ref, v_ref, d_ref, nm_ref, nv_ref):
        d_ref[...], nm_ref[...], nv_ref[...] = _adam_update(w_ref[...], g_ref[...], m_ref[...], v_ref[...])

    blk = pl.BlockSpec((r, c), lambda j: (0, 0))
    sds = jax.ShapeDtypeStruct((r, c), F32)
    return pl.pallas_call(
        body, name=name, out_shape=(sds, sds, sds), grid=(1,), in_specs=[blk] * 4, out_specs=(blk,) * 3,
        compiler_params=_params(1),
    )(w, g, m, v)


def _cast_into_slot(name, w, chip, dtype, deps=()):
    r, c = w.shape
    tr = _row_tile(r, c)

    def body(chip_ref, w_ref, *rest):
        rest[-1][...] = w_ref[...].astype(dtype)

    grid_spec = pltpu.PrefetchScalarGridSpec(
        num_scalar_prefetch=1, grid=(r // tr,),
        in_specs=[pl.BlockSpec((tr, c), lambda j, chip_ref: (j, 0))] + [ANY] * len(deps),
        out_specs=pl.BlockSpec((None, tr, c), lambda j, chip_ref: (chip_ref[0], j, 0)))
    return pl.pallas_call(
        body, name=name, out_shape=jax.ShapeDtypeStruct((4, r, c), dtype), grid_spec=grid_spec, compiler_params=_params(1),
    )(chip, w, *deps)


def _cast_half_into_slot(name, w, chip_half, dtype, into=None):
    r, c = w.shape
    h = r // 2
    tr = _row_tile(h, c)
    nb = h // tr
    n_prev = 0 if into is None else 1

    def body(ids_ref, w_ref, *rest):
        rest[-1][...] = w_ref[...].astype(dtype)

    grid_spec = pltpu.PrefetchScalarGridSpec(
        num_scalar_prefetch=1, grid=(nb,),
        in_specs=[pl.BlockSpec((tr, c), lambda j, ids_ref: (ids_ref[1] * nb + j, 0))] + [ANY] * n_prev,
        out_specs=pl.BlockSpec((None, tr, c), lambda j, ids_ref: (ids_ref[0], ids_ref[1] * nb + j, 0)))
    return pl.pallas_call(
        body, name=name, out_shape=jax.ShapeDtypeStruct((4, r, c), dtype), grid_spec=grid_spec, compiler_params=_params(1),
        input_output_aliases={2: 0} if into is not None else {},
    )(chip_half, w, *([into] if into is not None else []))


def _place():
    return lax.axis_index("x"), lax.axis_index("y"), lax.axis_index("c")


def _chip_of(x, y, flip):
    px, py = x ^ flip[0], y ^ flip[1]
    return px, py, 2 * px + py


def _half(ref, which):
    rows = ref.shape[0] // 2
    return ref.at[pl.ds(which * rows, rows)]


HBM = pl.BlockSpec(memory_space=pltpu.HBM)
SEM = pl.BlockSpec(memory_space=pltpu.SEMAPHORE)
SPLIT_COPY = pltpu.CompilerParams(has_side_effects=pltpu.SideEffectType.DATAFLOW_SIDE_EFFECTING)


def _in_hbm(arrays):
    return [pltpu.with_memory_space_constraint(t, pltpu.HBM) for t in arrays]


TOKEN = jax.ShapeDtypeStruct((SMALL_ROWS, LANES), F32)
TOKEN_SPEC = pl.BlockSpec(memory_space=pltpu.VMEM)


NEIGHBOUR_FLIPS = CHIP_FLIPS[:2]


def _relay_chips(x, y, c):
    fx, fy = x ^ c, y ^ (1 - c)
    return (fx, fy), 2 * fx + fy, 2 * (1 - x) + (1 - y)


def _ag_start(name, slabs, deps=()):
    n = len(slabs)
    nn = len(NEIGHBOUR_FLIPS)

    def body(*refs):
        no = n + len(deps)
        ssem, rsem = refs[no], refs[no + 1]
        outs = refs[no + 2:no + 2 + n]
        token = refs[no + 2 + n]
        token[...] = jnp.zeros_like(token)
        x, y, c = _place()
        k = 2 * x + y
        for a in range(n):
            for j, flip in enumerate(NEIGHBOUR_FLIPS):
                px, py, _ = _chip_of(x, y, flip)
                mine = _half(outs[a].at[k], c)
                pltpu.make_async_remote_copy(src_ref=mine, dst_ref=mine, send_sem=ssem.at[a * nn + j],
                                             recv_sem=rsem.at[a * nn + j], device_id=(px, py, c), device_id_type=MESH).start()

    sem = pltpu.SemaphoreType.DMA((nn * n,))
    res = pl.pallas_call(
        body, name=name, out_shape=(sem, sem) + tuple(pltpu.HBM(t.shape, t.dtype) for t in slabs) + (TOKEN,),
        in_specs=[HBM] * n + [ANY] * len(deps), out_specs=tuple([SEM, SEM] + [HBM] * n + [TOKEN_SPEC]),
        input_output_aliases={a: 2 + a for a in range(n)}, compiler_params=SPLIT_COPY,
    )(*_in_hbm(slabs), *deps)
    return (res[0], res[1]), list(res[2:2 + n]), res[2 + n]


def _ag_relay(name, slabs, sems, after, then_start=()):
    n = len(slabs)
    m = len(then_start)
    nn = len(NEIGHBOUR_FLIPS)

    def body(*refs):
        no = n + 2 + m + len(after)
        ins = refs[:n]
        ssem, rsem = refs[n], refs[n + 1]
        r_s, r_r, p_s, p_r = refs[no:no + 4]
        x, y, c = _place()
        k = 2 * x + y
        (fx, fy), _, _ = _relay_chips(x, y, c)
        for a in range(n):
            for j, flip in enumerate(NEIGHBOUR_FLIPS):
                _, _, kj = _chip_of(x, y, flip)
                landed = _half(ins[a].at[kj], c)
                cp = pltpu.make_async_remote_copy(
                    src_ref=_half(ins[a].at[k], c), dst_ref=landed, send_sem=ssem.at[a * nn + j],
                    recv_sem=rsem.at[a * nn + j], device_id=(x, y, c), device_id_type=MESH)
                cp.wait_send()
                cp.wait_recv()
        for a in range(n):
            near = _half(ins[a].at[2 * (x ^ (1 - c)) + (y ^ c)], c)
            pltpu.make_async_remote_copy(src_ref=near, dst_ref=near, send_sem=r_s.at[a], recv_sem=r_r.at[a],
                                         device_id=(fx, fy, c), device_id_type=MESH).start()
            for j, flip in enumerate(NEIGHBOUR_FLIPS):
                _, _, kj = _chip_of(x, y, flip)
                landed = _half(ins[a].at[kj], c)
                pltpu.make_async_remote_copy(src_ref=landed, dst_ref=landed, send_sem=p_s.at[a * nn + j],
                                             recv_sem=p_r.at[a * nn + j], device_id=(x, y, 1 - c), device_id_type=MESH).start()
        if m:
            d_s, d_r = refs[no + 4 + n], refs[no + 5 + n]
            nxt = refs[no + 6 + n:]
            for a in range(m):
                for j, flip in enumerate(NEIGHBOUR_FLIPS):
                    px, py, _ = _chip_of(x, y, flip)
                    mine = _half(nxt[a].at[k], c)
                    pltpu.make_async_remote_copy(src_ref=mine, dst_ref=mine, send_sem=d_s.at[a * nn + j],
                                                 recv_sem=d_r.at[a * nn + j], device_id=(px, py, c), device_id_type=MESH).start()

    rsem_t = pltpu.SemaphoreType.DMA((n,))
    psem_t = pltpu.SemaphoreType.DMA((nn * n,))
    out_shape = (rsem_t, rsem_t, psem_t, psem_t) + tuple(pltpu.HBM(t.shape, t.dtype) for t in slabs)
    out_specs = [SEM] * 4 + [HBM] * n
    aliases = {a: 4 + a for a in range(n)}
    if m:
        dsem_t = pltpu.SemaphoreType.DMA((nn * m,))
        out_shape += (dsem_t, dsem_t) + tuple(pltpu.HBM(t.shape, t.dtype) for t in then_start)
        out_specs += [SEM, SEM] + [HBM] * m
        aliases.update({n + 2 + a: 4 + n + 2 + a for a in range(m)})
    res = pl.pallas_call(
        body, name=name, out_shape=out_shape, in_specs=[HBM] * n + [SEM, SEM] + [HBM] * m + [ANY] * len(after),
        out_specs=tuple(out_specs), input_output_aliases=aliases, compiler_params=SPLIT_COPY,
    )(*slabs, sems[0], sems[1], *_in_hbm(list(then_start)), *after)
    if not m:
        return tuple(res[:4]), list(res[4:])
    return (tuple(res[:4]), list(res[4:4 + n])), ((res[4 + n], res[5 + n]), list(res[6 + n:]))


def _wait_passes(ins, p_s, p_r, x, y, c):
    nn = len(NEIGHBOUR_FLIPS)
    for a in range(len(ins)):
        for j, flip in enumerate(NEIGHBOUR_FLIPS):
            _, _, kj = _chip_of(x, y, flip)
            cp = pltpu.make_async_remote_copy(
                src_ref=_half(ins[a].at[kj], c), dst_ref=_half(ins[a].at[kj], 1 - c), send_sem=p_s.at[a * nn + j],
                recv_sem=p_r.at[a * nn + j], device_id=(x, y, c), device_id_type=MESH)
            cp.wait_send()
            cp.wait_recv()


def _ag_relay_wait(name, slabs, sems, after):
    n = len(slabs)
    ns = len(sems)

    def body(*refs):
        no = n + ns + len(after)
        ins = refs[:n]
        r_s, r_r = refs[n], refs[n + 1]
        f_s, f_r = refs[no], refs[no + 1]
        x, y, c = _place()
        _, _, kd = _relay_chips(x, y, c)
        for a in range(n):
            near = _half(ins[a].at[2 * (x ^ (1 - c)) + (y ^ c)], c)
            cp = pltpu.make_async_remote_copy(src_ref=near, dst_ref=_half(ins[a].at[kd], c), send_sem=r_s.at[a],
                                              recv_sem=r_r.at[a], device_id=(x, y, c), device_id_type=MESH)
            cp.wait_send()
            cp.wait_recv()
        if ns == 4:
            _wait_passes(ins, refs[n + 2], refs[n + 3], x, y, c)
        for a in range(n):
            diag = _half(ins[a].at[kd], c)
            pltpu.make_async_remote_copy(src_ref=diag, dst_ref=diag, send_sem=f_s.at[a], recv_sem=f_r.at[a],
                                         device_id=(x, y, 1 - c), device_id_type=MESH).start()

    sem = pltpu.SemaphoreType.DMA((n,))
    res = pl.pallas_call(
        body, name=name, out_shape=(sem, sem) + tuple(pltpu.HBM(t.shape, t.dtype) for t in slabs),
        in_specs=[HBM] * n + [SEM] * ns + [ANY] * len(after), out_specs=tuple([SEM, SEM] + [HBM] * n),
        input_output_aliases={a: 2 + a for a in range(n)}, compiler_params=SPLIT_COPY,
    )(*slabs, *sems, *after)
    return (res[0], res[1]), list(res[2:])


def _ag_final_wait(name, slabs, sems, after, first=0):
    n = len(slabs)

    def body(*refs):
        ins = refs[:n]
        f_s, f_r = refs[n], refs[n + 1]
        x, y, c = _place()
        _, _, kd = _relay_chips(x, y, c)
        for a in range(n):
            cp = pltpu.make_async_remote_copy(
                src_ref=_half(ins[a].at[kd], c), dst_ref=_half(ins[a].at[kd], 1 - c), send_sem=f_s.at[first + a],
                recv_sem=f_r.at[first + a], device_id=(x, y, c), device_id_type=MESH)
            cp.wait_send()
            cp.wait_recv()

    return pl.pallas_call(
        body, name=name, out_shape=tuple(pltpu.HBM(t.shape, t.dtype) for t in slabs),
        in_specs=[HBM] * n + [SEM, SEM] + [ANY] * len(after), out_specs=tuple([HBM] * n),
        input_output_aliases={a: a for a in range(n)}, compiler_params=SPLIT_COPY,
    )(*slabs, sems[0], sems[1], *after)


def _sibling_part(ref, c, halves):
    if not halves:
        return ref
    h = ref.shape[1] // 2
    return ref.at[:, pl.ds((1 - c) * h, h)]


def _swap_start(name, grads, halves=True, deps=()):
    n = len(grads)

    def body(*refs):
        no = 2 * n + len(deps)
        ssem, rsem = refs[no], refs[no + 1]
        src, land = refs[no + 2:no + n + 2], refs[no + n + 2:no + 2 * n + 2]
        token = refs[no + 2 * n + 2]
        token[...] = jnp.zeros_like(token)
        x, y, c = _place()
        for a in range(n):
            pltpu.make_async_remote_copy(
                src_ref=_sibling_part(src[a], c, halves), dst_ref=land[a], send_sem=ssem.at[a], recv_sem=rsem.at[a],
                device_id=(x, y, 1 - c), device_id_type=MESH).start()

    zones = [lax.empty((g.shape[0], g.shape[1] // 2, g.shape[2]) if halves else g.shape, g.dtype) for g in grads]
    sem = pltpu.SemaphoreType.DMA((n,))
    res = pl.pallas_call(
        body, name=name,
        out_shape=(sem, sem) + tuple(pltpu.HBM(t.shape, t.dtype) for t in list(grads) + zones) + (TOKEN,),
        in_specs=[HBM] * (2 * n) + [ANY] * len(deps), out_specs=tuple([SEM, SEM] + [HBM] * (2 * n) + [TOKEN_SPEC]),
        input_output_aliases={i: 2 + i for i in range(2 * n)}, compiler_params=SPLIT_COPY,
    )(*_in_hbm(list(grads) + zones), *deps)
    return (res[0], res[1], list(res[2:2 + n]), list(res[2 + n:2 + 2 * n])), res[2 + 2 * n]


def _swap_wait(name, ssem, rsem, grads, zones, after, halves=True):
    n = len(grads)

    def body(*refs):
        src, land = refs[:n], refs[n:2 * n]
        ss, rs = refs[2 * n], refs[2 * n + 1]
        x, y, c = _place()
        for a in range(n):
            cp = pltpu.make_async_remote_copy(
                src_ref=_sibling_part(src[a], c, halves), dst_ref=land[a], send_sem=ss.at[a], recv_sem=rs.at[a],
                device_id=(x, y, c), device_id_type=MESH)
            cp.wait_send()
            cp.wait_recv()

    res = pl.pallas_call(
        body, name=name, out_shape=tuple(pltpu.HBM(t.shape, t.dtype) for t in list(grads) + list(zones)),
        in_specs=[HBM] * (2 * n) + [SEM, SEM] + [ANY] * len(after), out_specs=tuple([HBM] * (2 * n)),
        input_output_aliases={i: i for i in range(2 * n)}, compiler_params=SPLIT_COPY,
    )(*grads, *zones, ssem, rsem, *after)
    return list(res[:n]), list(res[n:])


def _scatter_start(name, parts):
    n = len(parts)
    nf = len(CHIP_FLIPS)

    def body(*refs):
        ssem, rsem = refs[2 * n], refs[2 * n + 1]
        src, land = refs[2 * n + 2:3 * n + 2], refs[3 * n + 2:4 * n + 2]
        token = refs[4 * n + 2]
        token[...] = jnp.zeros_like(token)
        x, y, c = _place()
        for a in range(n):
            for j, flip in enumerate(CHIP_FLIPS):
                px, py, kj = _chip_of(x, y, flip)
                pltpu.make_async_remote_copy(
                    src_ref=src[a].at[kj], dst_ref=land[a].at[j], send_sem=ssem.at[a * nf + j], recv_sem=rsem.at[a * nf + j],
                    device_id=(px, py, c), device_id_type=MESH).start()

    zones = [lax.empty((nf,) + p.shape[1:], p.dtype) for p in parts]
    sem = pltpu.SemaphoreType.DMA((nf * n,))
    res = pl.pallas_call(
        body, name=name,
        out_shape=(sem, sem) + tuple(pltpu.HBM(t.shape, t.dtype) for t in list(parts) + zones)
        + (jax.ShapeDtypeStruct((SMALL_ROWS, LANES), F32),),
        in_specs=[HBM] * (2 * n),
        out_specs=tuple([SEM, SEM] + [HBM] * (2 * n) + [pl.BlockSpec(memory_space=pltpu.VMEM)]),
        input_output_aliases={i: 2 + i for i in range(2 * n)}, compiler_params=SPLIT_COPY,
    )(*_in_hbm(list(parts) + zones))
    return (res[0], res[1], list(res[2:2 + n]), list(res[2 + n:2 + 2 * n])), res[2 + 2 * n]


def _scatter_wait(name, ssem, rsem, parts, zones, after):
    n = len(parts)
    nf = len(CHIP_FLIPS)

    def body(*refs):
        src, land = refs[:n], refs[n:2 * n]
        ss, rs = refs[2 * n], refs[2 * n + 1]
        x, y, c = _place()
        for a in range(n):
            for j, flip in enumerate(CHIP_FLIPS):
                _, _, kj = _chip_of(x, y, flip)
                cp = pltpu.make_async_remote_copy(
                    src_ref=src[a].at[kj], dst_ref=land[a].at[j], send_sem=ss.at[a * nf + j], recv_sem=rs.at[a * nf + j],
                    device_id=(x, y, c), device_id_type=MESH)
                cp.wait_send()
                cp.wait_recv()

    res = pl.pallas_call(
        body, name=name, out_shape=tuple(pltpu.HBM(t.shape, t.dtype) for t in list(parts) + list(zones)),
        in_specs=[HBM] * (2 * n) + [SEM, SEM] + [ANY] * len(after), out_specs=tuple([HBM] * (2 * n)),
        input_output_aliases={i: i for i in range(2 * n)}, compiler_params=SPLIT_COPY,
    )(*parts, *zones, ssem, rsem, *after)
    return list(res[:n]), list(res[n:])


N_PEERS = 7


def _peer(x, y, c, mask):
    px, py, pc = x ^ ((mask >> 2) & 1), y ^ ((mask >> 1) & 1), c ^ (mask & 1)
    return (px, py, pc), 4 * px + 2 * py + pc


def _reduce_start(vec, deps):
    nd = len(deps)

    def body(*refs):
        ssem, rsem, src, land, token = refs[2 + nd:]
        token[...] = jnp.zeros_like(token)
        x, y, c = _place()
        me = 4 * x + 2 * y + c
        for mask in range(1, N_PEERS + 1):
            to, _ = _peer(x, y, c, mask)
            pltpu.make_async_remote_copy(src_ref=src, dst_ref=land.at[me], send_sem=ssem.at[mask - 1],
                                         recv_sem=rsem.at[mask - 1], device_id=to, device_id_type=MESH).start()

    zone = lax.empty((N_PEERS + 1,) + vec.shape, vec.dtype)
    sem = pltpu.SemaphoreType.DMA((N_PEERS,))
    res = pl.pallas_call(
        body, name="reduce_start",
        out_shape=(sem, sem, pltpu.HBM(vec.shape, vec.dtype), pltpu.HBM(zone.shape, zone.dtype), TOKEN),
        in_specs=[HBM, HBM] + [ANY] * nd, out_specs=(SEM, SEM, HBM, HBM, TOKEN_SPEC),
        input_output_aliases={0: 2, 1: 3}, compiler_params=SPLIT_COPY,
    )(*_in_hbm([vec, zone]), *deps)
    return res[:4], res[4]


def _reduce_wait(ssem, rsem, vec, zone, after):
    def body(src, land, ss, rs, *_):
        x, y, c = _place()
        for mask in range(1, N_PEERS + 1):
            _, frm = _peer(x, y, c, mask)
            cp = pltpu.make_async_remote_copy(src_ref=src, dst_ref=land.at[frm], send_sem=ss.at[mask - 1],
                                              recv_sem=rs.at[mask - 1], device_id=(x, y, c), device_id_type=MESH)
            cp.wait_send()
            cp.wait_recv()

    return pl.pallas_call(
        body, name="reduce_wait", out_shape=(pltpu.HBM(vec.shape, vec.dtype), pltpu.HBM(zone.shape, zone.dtype)),
        in_specs=[HBM, HBM, SEM, SEM] + [ANY] * len(after), out_specs=(HBM, HBM),
        input_output_aliases={0: 0, 1: 1}, compiler_params=SPLIT_COPY,
    )(vec, zone, ssem, rsem, *after)


def _reduce_sum(vec, zone, me, loss_row, loss_scale):
    r, dm = vec.shape

    def body(me_ref, v_ref, z_ref, o_ref, l_ref):
        acc = None
        for i in range(N_PEERS + 1):
            term = jnp.where(me_ref[0] == i, v_ref[...], z_ref[i])
            acc = term if acc is None else acc + term
        o_ref[...] = acc
        l_ref[...] = jnp.sum(acc[loss_row:loss_row + SMALL_ROWS, :], axis=(0, 1), keepdims=True) * loss_scale

    grid_spec = pltpu.PrefetchScalarGridSpec(
        num_scalar_prefetch=1, grid=(1,),
        in_specs=[pl.BlockSpec((r, dm), lambda i, me_ref: (0, 0)), pl.BlockSpec((N_PEERS + 1, r, dm), lambda i, me_ref: (0, 0, 0))],
        out_specs=(pl.BlockSpec((r, dm), lambda i, me_ref: (0, 0)), pl.BlockSpec((1, 1), lambda i, me_ref: (0, 0))))
    return pl.pallas_call(
        body, name="reduce_sum", out_shape=(jax.ShapeDtypeStruct((r, dm), F32), jax.ShapeDtypeStruct((1, 1), F32)),
        grid_spec=grid_spec, compiler_params=_params(1),
    )(me, vec, zone)


def kernel(x, meta_tokens, norm_mix_g, w_in, b_gate, pool_w, pool_scale, conv_w, conv_out_w, w_o, norm_ffn_g, w_gate_up, w_down, norm_final_g, loss_target, m_meta_tokens, m_norm_mix_g, m_w_in, m_b_gate, m_pool_w, m_pool_scale, m_conv_w, m_conv_out_w, m_w_o, m_norm_ffn_g, m_w_gate_up, m_w_down, m_norm_final_g, v_meta_tokens, v_norm_mix_g, v_w_in, v_b_gate, v_pool_w, v_pool_scale, v_conv_w, v_conv_out_w, v_w_o, v_norm_ffn_g, v_w_gate_up, v_w_down, v_norm_final_g):
    seq, dm = x.shape[1], x.shape[2]
    tail = TAIL_ROWS
    tm = tail
    lp = seq + tail
    tm_row = _row_tile(lp, dm, 4, 3 * 1024 * 1024)
    n_chips = 4
    n_groups = len(POOL_WINDOWS)
    gw = dm // n_groups
    tc = min(256, gw)
    cx, cy, cc = _place()
    chip = 2 * cx + cy
    dloc = dm // n_chips

    pool2 = pool_w.reshape(n_groups * pool_w.shape[1], gw)
    big = {"w_in": w_in, "w_gate_up": w_gate_up, "pool_w": pool2, "conv_out_w": conv_out_w, "w_o": w_o, "w_down": w_down}
    chip1 = jnp.reshape(chip, (1,)).astype(jnp.int32)
    core = jnp.reshape(cc, (1,)).astype(jnp.int32)
    small_loc = jnp.concatenate([meta_tokens, jnp.pad(conv_w, ((0, 8 - conv_w.shape[0]), (0, 0))),
                                 jnp.zeros((8, dloc), F32)], axis=0)
    g1, g2, g3 = norm_mix_g.reshape(1, dm), norm_ffn_g.reshape(1, dm), norm_final_g.reshape(1, dm)
    b_gate2 = b_gate.reshape(2, dm)
    ps = pool_scale.reshape(1, dm)
    mine = jnp.stack([chip, cc]).astype(jnp.int32)
    other = jnp.stack([chip, 1 - cc]).astype(jnp.int32)
    first = [_cast_into_slot("place_small", small_loc, chip1, F32), _cast_half_into_slot("cast_w_in_sent", w_in, mine, BF16)]
    sems, first, token = _ag_start("ag_start_first", first)
    first[1] = _cast_half_into_slot("cast_w_in_kept", w_in, other, BF16, into=first[1])
    cast = {nme: _cast_into_slot("cast_" + nme, big[nme], chip1, BF16, deps=(token,))
            for nme in ["pool_w", "conv_out_w", "w_o", "w_gate_up", "w_down"]}
    sems, first = _ag_relay("ag_relay_first", first, sems, list(cast.values()))
    sems, (small4, w_in4) = _ag_relay_wait("ag_relay_wait_first", first, sems, [])
    (small4,) = _ag_final_wait("ag_final_wait_small", [small4], sems, [])
    mixer_w = [cast["pool_w"], cast["conv_out_w"], cast["w_o"]]
    sems_mix, mixer_w, token = _ag_start("ag_start_mixer", mixer_w, deps=(small4,))
    sems_gu, (w_gu4,), token = _ag_start("ag_start_gate_up", [cast["w_gate_up"]], deps=(token,))

    small_f = jnp.transpose(small4, (1, 0, 2)).reshape(small4.shape[1], dm)
    meta_f = small_f[:N_META]
    conv_w_f = small_f[N_META:N_META + 3]
    tail_rows = jnp.concatenate([jnp.zeros((tail - N_META, dm), F32), meta_f], axis=0)
    h0, hn1 = _rms_fwd_input("rms_mix", x[0], tail_rows, g1, deps=(token,))
    (w_in4,) = _ag_final_wait("ag_final_wait_first", [w_in4], sems, [hn1], first=1)
    proj = _nn_sharded("proj", hn1, w_in4, 6)
    sems_mix, mixer_w = _ag_relay("ag_relay_mixer", mixer_w, sems_mix, [proj])
    (sems_gu, (w_gu4,)), (sems_down, (w_down4,)) = _ag_relay("ag_relay_gate_up", [w_gu4], sems_gu, [mixer_w[0]],
                                                              then_start=[cast["w_down"]])
    pooled, z = _mixer_fwd("mixer_fwd", proj, conv_w_f, tc, w_down4)
    sems_mix, mixer_w = _ag_relay_wait("ag_relay_wait_mixer", mixer_w, sems_mix, [pooled])
    pool4, conv_out4, w_o4 = _ag_final_wait("ag_final_wait_mixer", mixer_w, sems_mix, [])
    pool_f = jnp.transpose(pool4.reshape(n_chips, n_groups, gw // n_chips, gw), (1, 0, 2, 3)).reshape(n_groups, gw, gw)
    conv_out_f = conv_out4.reshape(dm, dm)
    w_o_f = w_o4.reshape(dm, dm)
    ya = _pool_fwd("pool_proj", pooled, pool_f)
    yb = _nn_plain("conv_out", z, conv_out_f, BF16)
    mix = _gate_mix("gate_mix", proj, b_gate2, ya, ps, yb, tm_row)
    sems_gu, (w_gu4,) = _ag_relay_wait("ag_relay_wait_gate_up", [w_gu4], sems_gu, [mix])
    h1 = _nn_plain("attn_out", mix, w_o_f, F32, res=h0, tn_pref=256)
    (w_gu4,) = _ag_final_wait("ag_final_wait_gate_up", [w_gu4], sems_gu, [h1])
    hn2 = _rms_fwd("rms_ffn", h1, g2, tm_row)
    sems_down, (w_down4,) = _ag_relay("ag_relay_down", [w_down4], sems_down, [hn2])
    gu, act = _gate_up_swiglu("gate_up", hn2, w_gu4, w_down4)
    sems_down, (w_down4,) = _ag_relay_wait("ag_relay_wait_down", [w_down4], sems_down, [act])
    (w_down4,) = _ag_final_wait("ag_final_wait_down", [w_down4], sems_down, [])
    w_down_f = w_down4.reshape(-1, dm)
    h2 = _nn_rows("ffn_down", act, w_down_f, h1)
    dh2, dh2b, loss_cols, dg3 = _final_loss("final_loss", h2, g3, loss_target[0], tm)

    def scatter(tag, names_g, swap, after):
        grads_g, got = _swap_wait("swap_wait_" + tag, *swap, [after])
        pairs = [_pair_add("pair_add_" + nme, g4, rv, core) for nme, g4, rv in zip(names_g, grads_g, got)]
        return _scatter_start("scatter_start_" + tag, pairs)

    dgu = _dact_swiglu_bwd("d_gate_up", dh2b, w_down_f, gu)
    gw_down = _tn_plain("dw_down", act, dh2b)
    gw_gu = _tn_sharded("dw_gate_up", hn2, dgu, n_chips)
    swap_a, token = _swap_start("swap_start_a", [gw_gu, gw_down.reshape(n_chips, -1, dm)])
    dhn2 = _nt_sharded("d_hn2", dgu, w_gu4, tr_pref=2816, row_tiles=2, deps=(token,))
    flight_a, token = scatter("a", ["w_gate_up", "w_down"], swap_a, dhn2)
    dh1, dh1b, dg2 = _rms_bwd("rms_ffn_bwd", dhn2, h1, g2, dh2, tm_row, token)
    dmix = _nt_plain("d_mix", dh1b, w_o_f)
    gw_o = _tn_plain("dw_o", mix, dh1b)
    dproj, dyb, dya, db_gate, dps = _gate_bwd("gate_bwd", dmix, proj, b_gate2, ya, ps, yb, tm_row)
    gw_conv_out = _tn_plain("dw_conv_out", z, dyb)
    gw_pool = _pool_bwd_w("dw_pool", pooled, dya)
    gw_pool = jnp.transpose(gw_pool.reshape(n_groups, n_chips, gw // n_chips, gw), (1, 0, 2, 3))
    dpooled = _pool_bwd_act("d_pooled", dya, pool_f)
    dz = _nt_plain("d_z", dyb, conv_out_f)
    dproj, dconv_w = _mixer_bwd("mixer_bwd", dz, dpooled, proj, conv_w_f, dproj, tc, dpooled)
    gw_in0 = _tn_sharded("dw_in_0", hn1, dproj, n_chips, part=(0, 2))
    swap_b, token = _swap_start("swap_start_b", [gw_o.reshape(n_chips, dloc, dm), gw_conv_out.reshape(n_chips, dloc, dm),
                                                 gw_pool.reshape(n_chips, n_groups * (gw // n_chips), gw), gw_in0])
    gw_in1 = _tn_sharded("dw_in_1", hn1, dproj, n_chips, part=(1, 2), deps=(token,))
    flight_b, token = scatter("b", ["w_o", "conv_out_w", "pool_w", "w_in_0"], swap_b, gw_in1)
    swap_c, token = _swap_start("swap_start_c", [gw_in1], deps=(token,))
    dhn1 = _nt_in_proj("d_hn1", dproj, w_in4, deps=(token,))
    flight_c, token = scatter("c", ["w_in_1"], swap_c, dhn1)
    dx, dtail, dg1 = _rms_bwd_input("rms_mix_bwd", dhn1, h0, g1, dh1, tm, seq, token)
    grad_x = dx[None]
    dmeta = dtail[tail - N_META:]

    given = dict(meta_tokens=(meta_tokens, m_meta_tokens, v_meta_tokens), norm_mix_g=(norm_mix_g, m_norm_mix_g, v_norm_mix_g),
                 w_in=(w_in, m_w_in, v_w_in), b_gate=(b_gate, m_b_gate, v_b_gate), pool_w=(pool_w, m_pool_w, v_pool_w),
                 pool_scale=(pool_scale, m_pool_scale, v_pool_scale), conv_w=(conv_w, m_conv_w, v_conv_w),
                 conv_out_w=(conv_out_w, m_conv_out_w, v_conv_out_w), w_o=(w_o, m_w_o, v_w_o),
                 norm_ffn_g=(norm_ffn_g, m_norm_ffn_g, v_norm_ffn_g), w_gate_up=(w_gate_up, m_w_gate_up, v_w_gate_up),
                 w_down=(w_down, m_w_down, v_w_down), norm_final_g=(norm_final_g, m_norm_final_g, v_norm_final_g))
    order = list(given.keys())
    grad, delta, new_m, new_v = {}, {}, {}, {}
    vec = jnp.concatenate([dg1, dg2, dg3, db_gate, dps, loss_cols, dconv_w, dmeta], axis=0)
    loss_row = 5 * SMALL_ROWS
    groups_g = {"a": [("w_gate_up", (0, 1)), ("w_down", (0, 1))],
                "b": [("w_o", (0, 1)), ("conv_out_w", (0, 1)), ("pool_w", (0, 1)), ("w_in", (0, 2))], "c": [("w_in", (1, 2))]}
    results = {}

    def reduced(tag, flight, after):
        pairs, zones = _scatter_wait("scatter_wait_" + tag, *flight, after)
        halves = [_chip_sum("chip_sum_%s_%d" % (nme, part[0]), p, rv, chip1) for (nme, part), p, rv in zip(groups_g[tag], pairs, zones)]
        return _swap_start("send_start_" + tag, halves, halves=False)

    def update(tag, send, after):
        halves, sib_halves = _swap_wait("send_wait_" + tag, *send, after, halves=False)
        deltas = []
        for (nme, part), g_own, g_sib in zip(groups_g[tag], halves, sib_halves):
            w, m, v = given[nme]
            shape2 = (2 * g_own.shape[0] * part[1], g_own.shape[1])
            results[nme] = _adamw_halves("adamw_%s_%d" % (nme, part[0]), w.reshape(shape2), g_own, g_sib, m.reshape(shape2),
                                         v.reshape(shape2), core, part=part, prev=results.get(nme))
            grad[nme], delta[nme], new_m[nme], new_v[nme] = [t.reshape(w.shape) for t in results[nme]]
            deltas.append(results[nme][1])
        return deltas

    send_a, token = reduced("a", flight_a, [dx])
    send_b, token = reduced("b", flight_b, [token])
    done_a = update("a", send_a, [token])
    send_c, token = reduced("c", flight_c, done_a)
    me1 = jnp.reshape(4 * cx + 2 * cy + cc, (1,)).astype(jnp.int32)
    red_flight, token = _reduce_start(vec, [token])
    done_b = update("b", send_b, [token])
    done_c = update("c", send_c, done_b)
    red, loss11 = _reduce_sum(*_reduce_wait(*red_flight, done_c), me1, loss_row, 0.5 / dm)
    loss = loss11[0, 0]
    col0 = chip * dloc
    g_small = {
        "norm_mix_g": red[0], "norm_ffn_g": red[SMALL_ROWS], "norm_final_g": red[2 * SMALL_ROWS],
        "b_gate": red[3 * SMALL_ROWS:3 * SMALL_ROWS + 2].reshape(-1), "pool_scale": red[4 * SMALL_ROWS],
        "conv_w": lax.dynamic_slice(red, (6 * SMALL_ROWS, col0), (3, dloc)),
        "meta_tokens": lax.dynamic_slice(red, (7 * SMALL_ROWS, col0), (N_META, dloc)),
    }

    vec_names = ["norm_mix_g", "norm_ffn_g", "norm_final_g", "pool_scale"]

    def slab_vec(pick):
        rows = [pick(nme).reshape(1, dm) for nme in vec_names] + [pick("b_gate").reshape(2, dm), jnp.zeros((2, dm), F32)]
        return jnp.concatenate(rows, axis=0)

    def slab_col(pick):
        return jnp.concatenate([pick("meta_tokens"), pick("conv_w"), jnp.zeros((5, dloc), F32)], axis=0)

    for slab, tag in ((slab_vec, "vec"), (slab_col, "col")):
        d, nm, nv = _adamw("adamw_small_" + tag, slab(lambda nme: given[nme][0]), slab(lambda nme: g_small[nme]),
                           slab(lambda nme: given[nme][1]), slab(lambda nme: given[nme][2]))
        for out, res in ((delta, d), (new_m, nm), (new_v, nv)):
            if tag == "vec":
                for i, nme in enumerate(vec_names):
                    out[nme] = res[i]
                out["b_gate"] = res[4:6].reshape(-1)
            else:
                out["meta_tokens"] = res[:N_META]
                out["conv_w"] = res[N_META:N_META + 3]
    grad.update(g_small)
    return (loss, grad_x, *[grad[nme] for nme in order], *[delta[nme] for nme in order],
            *[new_m[nme] for nme in order], *[new_v[nme] for nme in order])
```

```python
import math

import jax
import jax.numpy as jnp
from jax import lax
from jax.experimental import pallas as pl
from jax.experimental.pallas import tpu as pltpu

F32 = jnp.float32
BF16 = jnp.bfloat16
N_META = 16
POOL_WINDOWS = (2, 4, 8, 16)
EPS = 1e-6
ADAM_LR, ADAM_B1, ADAM_B2, ADAM_EPS, ADAM_WD, ADAM_STEP = 0.001, 0.9, 0.999, 1e-08, 0.01, 10
LANES = 128
V7X_VMEM_BYTES = 64 * 1024 * 1024
VMEM_LIMIT = V7X_VMEM_BYTES - 8 * 1024 * 1024
MESH = pl.DeviceIdType.MESH
ANY = pl.BlockSpec(memory_space=pl.ANY)
CHIP_FLIPS = ((1, 0), (0, 1), (1, 1))
SMALL_ROWS = 8
TAIL_ROWS = 32


def _pick(n, pref):
    best = None
    for t in range(LANES, min(n, pref) + 1, LANES):
        if n % t == 0:
            best = t
    assert best is not None, (n, pref)
    return best


def _params(n_axes=0):
    sem = ("arbitrary",) * n_axes if n_axes else None
    return pltpu.CompilerParams(dimension_semantics=sem, vmem_limit_bytes=VMEM_LIMIT)


_DIMS = {
    "nn": (((1,), (0,)), ((), ())),
    "nt": (((1,), (1,)), ((), ())),
    "tn": (((0,), (0,)), ((), ())),
}


def _matmul(name, mode, a, b, out_sds, grid, a_spec, b_spec, o_spec, nk, res=None, res_spec=None, acc_shape=None, deps=()):
    out_dtype = out_sds.dtype
    in_place = nk > 1 and out_dtype == F32
    use_scratch = nk > 1 and not in_place
    rows = a_spec.block_shape[-2] if mode != "tn" else None
    chunk = _row_tile(rows, 1, 1, 1152) if rows is not None else None
    n_in = 2 + (res is not None) + len(deps)

    def body(*refs):
        a_ref, b_ref = refs[:2]
        r_ref = refs[2] if res is not None else None
        o_ref, *scr = refs[n_in:]
        k = pl.program_id(len(grid) - 1) if nk > 1 else None

        def emit(sl):
            if sl is None:
                part = lax.dot_general(a_ref[...], b_ref[...], _DIMS[mode], preferred_element_type=F32)
                idx = (slice(None), slice(None))
            else:
                part = lax.dot_general(a_ref[sl, :], b_ref[...], _DIMS[mode], preferred_element_type=F32)
                idx = (sl, slice(None))
            if nk == 1:
                if r_ref is not None:
                    part = part + r_ref[idx]
                o_ref[idx] = part.astype(out_dtype)
                return
            acc = scr[0] if use_scratch else o_ref

            @pl.when(k == 0)
            def _():
                first = part
                if r_ref is not None and in_place:
                    first = first + r_ref[idx]
                acc[idx] = first

            @pl.when(k > 0)
            def _():
                acc[idx] += part

            if use_scratch:

                @pl.when(k == nk - 1)
                def _():
                    o_ref[idx] = acc[idx].astype(out_dtype)

        if mode == "tn" or chunk == rows:
            emit(None)
        else:
            for m0 in range(0, rows, chunk):
                emit(pl.ds(m0, chunk))

    ins = [a, b] + ([res] if res is not None else []) + list(deps)
    in_specs = [a_spec, b_spec] + ([res_spec] if res is not None else []) + [ANY] * len(deps)
    scratch = [pltpu.VMEM(acc_shape, F32)] if use_scratch else []
    return pl.pallas_call(
        body, name=name, out_shape=out_sds, grid=grid, in_specs=in_specs, out_specs=o_spec,
        scratch_shapes=scratch, compiler_params=_params(len(grid)),
    )(*ins)


def _nn_sharded(name, a, w4, nseg):
    lp, kdim = a.shape
    s, _, nloc = w4.shape
    segw = s * nloc // nseg
    tn = _pick(math.gcd(nloc, segw), 1536)
    bw, bo = nloc // tn, segw // tn
    return _matmul(
        name, "nn", a, w4, jax.ShapeDtypeStruct((nseg, lp, segw), BF16), (s * bw,),
        pl.BlockSpec((lp, kdim), lambda j: (0, 0)),
        pl.BlockSpec((None, kdim, tn), lambda j: (j // bw, 0, j % bw)),
        pl.BlockSpec((None, lp, tn), lambda j: (j // bo, 0, j % bo)), 1)


def _nt_in_proj(name, dseg, w4, row_tiles=2, to_pref=1024, deps=()):
    nseg, lp, segw = dseg.shape
    s, kdim, nloc = w4.shape
    assert nseg * segw == s * nloc and 2 * nloc == 3 * segw, (dseg.shape, w4.shape)
    half = segw // 2
    to = _pick(kdim, to_pref)
    tm = lp // row_tiles

    def body(full_ref, half_ref, w_ref, *rest):
        o_ref = rest[len(deps)]
        r = pl.program_id(2)

        def contribution(full_first):
            lo, hi = (pl.ds(0, segw), pl.ds(segw, half)) if full_first else (pl.ds(half, segw), pl.ds(0, half))
            return (lax.dot_general(full_ref[...], w_ref[:, lo], _DIMS["nt"], preferred_element_type=F32)
                    + lax.dot_general(half_ref[...], w_ref[:, hi], _DIMS["nt"], preferred_element_type=F32))

        @pl.when(r == 0)
        def _():
            o_ref[...] = contribution(True)

        for ri in range(1, s):

            @pl.when(r == ri)
            def _(ri=ri):
                o_ref[...] += contribution(ri % 2 == 0)

    return pl.pallas_call(
        body, name=name, out_shape=jax.ShapeDtypeStruct((lp, kdim), F32), grid=(row_tiles, kdim // to, s),
        in_specs=[pl.BlockSpec((None, tm, segw), lambda m, j, r: ((3 * r + 1) // 2, m, 0)),
                  pl.BlockSpec((None, tm, half), lambda m, j, r: (1 + 3 * (r // 2), m, r % 2)),
                  pl.BlockSpec((None, to, nloc), lambda m, j, r: (r, j, 0))] + [ANY] * len(deps),
        out_specs=pl.BlockSpec((tm, to), lambda m, j, r: (m, j)), compiler_params=_params(3),
    )(dseg, dseg, w4, *deps)


def _nn_plain(name, a, w, out_dtype, res=None, tn_pref=512, tk_pref=2048, deps=()):
    lp, kdim = a.shape
    n = w.shape[1]
    tn = _pick(n, tn_pref)
    tk = kdim if kdim <= tk_pref else _pick(kdim, tk_pref)
    nk = kdim // tk
    grid = (n // tn, nk) if nk > 1 else (n // tn,)
    if nk > 1:
        a_spec = pl.BlockSpec((lp, tk), lambda j, k: (0, k))
        w_spec = pl.BlockSpec((tk, tn), lambda j, k: (k, j))
        o_spec = pl.BlockSpec((lp, tn), lambda j, k: (0, j))
    else:
        a_spec = pl.BlockSpec((lp, tk), lambda j: (0, 0))
        w_spec = pl.BlockSpec((tk, tn), lambda j: (0, j))
        o_spec = pl.BlockSpec((lp, tn), lambda j: (0, j))
    return _matmul(name, "nn", a, w, jax.ShapeDtypeStruct((lp, n), out_dtype), grid, a_spec, w_spec, o_spec, nk,
                   res=res, res_spec=o_spec if res is not None else None, acc_shape=(lp, tn), deps=deps)


def _nt_plain(name, a, w, tn_pref=512):
    lp, kdim = a.shape
    n = w.shape[0]
    tn = _pick(n, tn_pref)
    return _matmul(
        name, "nt", a, w, jax.ShapeDtypeStruct((lp, n), BF16), (n // tn,),
        pl.BlockSpec((lp, kdim), lambda j: (0, 0)),
        pl.BlockSpec((tn, kdim), lambda j: (j, 0)),
        pl.BlockSpec((lp, tn), lambda j: (0, j)), 1)


def _nt_sharded(name, dseg, w4, to_pref=1024, tr_pref=1536, row_tiles=1, deps=()):
    nseg, lp, segw = dseg.shape
    s, kdim, nloc = w4.shape
    tr = _pick(math.gcd(nloc, segw), tr_pref)
    ba, bw = segw // tr, nloc // tr
    nr = s * bw
    to = _pick(kdim, to_pref)
    tm = lp // row_tiles
    return _matmul(
        name, "nt", dseg, w4, jax.ShapeDtypeStruct((lp, kdim), F32), (row_tiles, kdim // to, nr),
        pl.BlockSpec((None, tm, tr), lambda m, j, r: (r // ba, m, r % ba)),
        pl.BlockSpec((None, to, tr), lambda m, j, r: (r // bw, j, r % bw)),
        pl.BlockSpec((tm, to), lambda m, j, r: (m, j)), nr, deps=deps)


def _nn_rows(name, a, w, res, row_tiles=2, tn_pref=512):
    lp, kdim = a.shape
    n = w.shape[1]
    tn = _pick(n, tn_pref)
    tm = lp // row_tiles
    blk = pl.BlockSpec((tm, tn), lambda i, j: (i, j))
    return _matmul(name, "nn", a, w, jax.ShapeDtypeStruct((lp, n), F32), (row_tiles, n // tn),
                   pl.BlockSpec((tm, kdim), lambda i, j: (i, 0)), pl.BlockSpec((kdim, tn), lambda i, j: (0, j)), blk, 1,
                   res=res, res_spec=blk)


def _tn_plain(name, a, d, tk_pref=1024):
    lp, kdim = a.shape
    n = d.shape[1]
    tk = _pick(kdim, tk_pref)
    return _matmul(
        name, "tn", a, d, jax.ShapeDtypeStruct((kdim, n), BF16), (kdim // tk,),
        pl.BlockSpec((lp, tk), lambda i: (0, i)),
        pl.BlockSpec((lp, n), lambda i: (0, 0)),
        pl.BlockSpec((tk, n), lambda i: (i, 0)), 1)


def _tn_sharded(name, a, dseg, s, part=(0, 1), tk_pref=1024, deps=()):
    lp, kdim = a.shape
    nseg, _, segw = dseg.shape
    nloc = nseg * segw // s
    tn = _pick(math.gcd(nloc, segw), 1536)
    bd, bo = segw // tn, nloc // tn
    kpart = kdim // part[1]
    tk = _pick(kpart, tk_pref)
    i0 = part[0] * (kpart // tk)

    def body(a_ref, d_ref, *rest):
        o_ref, at_ref = rest[len(deps):]

        @pl.when(pl.program_id(1) == 0)
        def _():
            at_ref[...] = a_ref[...].T

        o_ref[...] = jnp.dot(at_ref[...], d_ref[...], preferred_element_type=F32).astype(BF16)

    return pl.pallas_call(
        body, name=name, out_shape=jax.ShapeDtypeStruct((s, kpart, nloc), BF16), grid=(kpart // tk, s * bo),
        in_specs=[pl.BlockSpec((lp, tk), lambda i, j: (0, i0 + i)),
                  pl.BlockSpec((None, lp, tn), lambda i, j: (j // bd, 0, j % bd))] + [ANY] * len(deps),
        out_specs=pl.BlockSpec((None, tk, tn), lambda i, j: (j // bo, i, j % bo)),
        scratch_shapes=[pltpu.VMEM((tk, lp), BF16)], compiler_params=_params(2),
    )(a, dseg, *deps)


def _silu_parts(gt):
    sg = jax.nn.sigmoid(gt)
    return gt * sg, sg * (1.0 + gt * (1.0 - sg))


def _gate_up_swiglu(name, a, w4, dep, tn_pref=256):
    lp, kdim = a.shape
    s, _, nloc = w4.shape
    f = s * nloc // 2
    tn = _pick(nloc, tn_pref)
    bw = nloc // tn
    chunk = _row_tile(lp, 1, 1, 576)

    def body(a_ref, wg_ref, wu_ref, _, fac_ref, act_ref):
        for m0 in range(0, lp, chunk):
            sl = pl.ds(m0, chunk)
            gt = jnp.dot(a_ref[sl, :], wg_ref[...], preferred_element_type=F32)
            up = jnp.dot(a_ref[sl, :], wu_ref[...], preferred_element_type=F32)
            silu, dsilu = _silu_parts(gt)
            fac_ref[0, sl, :] = (up * dsilu).astype(BF16)
            fac_ref[1, sl, :] = silu.astype(BF16)
            act_ref[sl, :] = (silu * up).astype(BF16)

    return pl.pallas_call(
        body, name=name, grid=(f // tn,),
        out_shape=(jax.ShapeDtypeStruct((2, lp, f), BF16), jax.ShapeDtypeStruct((lp, f), BF16)),
        in_specs=[pl.BlockSpec((lp, kdim), lambda j: (0, 0)),
                  pl.BlockSpec((None, kdim, tn), lambda j: (j // bw, 0, j % bw)),
                  pl.BlockSpec((None, kdim, tn), lambda j: (s // 2 + j // bw, 0, j % bw)), ANY],
        out_specs=(pl.BlockSpec((2, lp, tn), lambda j: (0, 0, j)), pl.BlockSpec((lp, tn), lambda j: (0, j))),
        compiler_params=_params(1),
    )(a, w4, w4, dep)


def _dact_swiglu_bwd(name, d, w, gu, tn_pref=512):
    lp, dm = d.shape
    f = w.shape[0]
    tn = _pick(f, tn_pref)
    chunk = _row_tile(lp, 1, 1, 576)

    def body(d_ref, w_ref, g_ref, u_ref, o_ref):
        for m0 in range(0, lp, chunk):
            sl = pl.ds(m0, chunk)
            dact = lax.dot_general(d_ref[sl, :], w_ref[...], _DIMS["nt"], preferred_element_type=F32)
            o_ref[0, sl, :] = (dact * g_ref[sl, :].astype(F32)).astype(BF16)
            o_ref[1, sl, :] = (dact * u_ref[sl, :].astype(F32)).astype(BF16)

    return pl.pallas_call(
        body, name=name, grid=(f // tn,), out_shape=jax.ShapeDtypeStruct((2, lp, f), BF16),
        in_specs=[pl.BlockSpec((lp, dm), lambda j: (0, 0)), pl.BlockSpec((tn, dm), lambda j: (j, 0)),
                  pl.BlockSpec((None, lp, tn), lambda j: (0, 0, j)), pl.BlockSpec((None, lp, tn), lambda j: (1, 0, j))],
        out_specs=pl.BlockSpec((2, lp, tn), lambda j: (0, 0, j)), compiler_params=_params(1),
    )(d, w, gu, gu)


def _pool_fwd(name, pooled, pw):
    lp, dm = pooled.shape
    g, gw, _ = pw.shape
    return _matmul(
        name, "nn", pooled, pw, jax.ShapeDtypeStruct((lp, dm), BF16), (g,),
        pl.BlockSpec((lp, gw), lambda gi: (0, gi)), pl.BlockSpec((None, gw, gw), lambda gi: (gi, 0, 0)),
        pl.BlockSpec((lp, gw), lambda gi: (0, gi)), 1)


def _pool_bwd_act(name, dya, pw, deps=()):
    lp, dm = dya.shape
    g, gw, _ = pw.shape
    return _matmul(
        name, "nt", dya, pw, jax.ShapeDtypeStruct((lp, dm), BF16), (g,),
        pl.BlockSpec((lp, gw), lambda gi: (0, gi)), pl.BlockSpec((None, gw, gw), lambda gi: (gi, 0, 0)),
        pl.BlockSpec((lp, gw), lambda gi: (0, gi)), 1, deps=deps)


def _pool_bwd_w(name, pooled, dya):
    lp, dm = pooled.shape
    g = len(POOL_WINDOWS)
    gw = dm // g
    return _matmul(
        name, "tn", pooled, dya, jax.ShapeDtypeStruct((g, gw, gw), BF16), (g,),
        pl.BlockSpec((lp, gw), lambda gi: (0, gi)), pl.BlockSpec((lp, gw), lambda gi: (0, gi)),
        pl.BlockSpec((None, gw, gw), lambda gi: (gi, 0, 0)), 1)


def _rms_fwd(name, h, g, tm, deps=()):
    lp, dm = h.shape

    def body(h_ref, g_ref, *rest):
        hv = h_ref[...]
        r = lax.rsqrt(jnp.mean(hv * hv, axis=-1, keepdims=True) + EPS)
        rest[-1][...] = (hv * r * g_ref[...]).astype(BF16)

    row = pl.BlockSpec((tm, dm), lambda i: (i, 0))
    return pl.pallas_call(
        body, name=name, out_shape=jax.ShapeDtypeStruct((lp, dm), BF16), grid=(lp // tm,),
        in_specs=[row, pl.BlockSpec((1, dm), lambda i: (0, 0))] + [ANY] * len(deps), out_specs=row, compiler_params=_params(1),
    )(h, g, *deps)


def _rms_fwd_into(name, src, g, lp, row0, tm, prev=None, deps=()):
    n, dm = src.shape
    b0 = row0 // tm
    n_in = 2 + len(deps)

    def body(s_ref, g_ref, *rest):
        h_ref, o_ref = rest[-2:]
        hv = s_ref[...]
        r = lax.rsqrt(jnp.mean(hv * hv, axis=-1, keepdims=True) + EPS)
        h_ref[...] = hv
        o_ref[...] = (hv * r * g_ref[...]).astype(BF16)

    row = pl.BlockSpec((tm, dm), lambda i: (b0 + i, 0))
    return pl.pallas_call(
        body, name=name, grid=(n // tm,),
        out_shape=(jax.ShapeDtypeStruct((lp, dm), F32), jax.ShapeDtypeStruct((lp, dm), BF16)),
        in_specs=[pl.BlockSpec((tm, dm), lambda i: (i, 0)), pl.BlockSpec((1, dm), lambda i: (0, 0))]
        + [ANY] * (len(deps) + (0 if prev is None else 2)),
        out_specs=(row, row), input_output_aliases={} if prev is None else {n_in: 0, n_in + 1: 1},
        compiler_params=_params(1),
    )(src, g, *deps, *(prev or ()))


def _rms_bwd(name, dy, h, g, dres, tm, dep):
    lp, dm = h.shape

    def body(dy_ref, h_ref, g_ref, dr_ref, _, dh_ref, dhb_ref, dg_ref):
        hv = h_ref[...]
        r = lax.rsqrt(jnp.mean(hv * hv, axis=-1, keepdims=True) + EPS)
        xhat = hv * r
        dyv = dy_ref[...]
        dxh = dyv * g_ref[...]
        dh = dr_ref[...] + r * (dxh - xhat * jnp.mean(dxh * xhat, axis=-1, keepdims=True))
        dh_ref[...] = dh
        dhb_ref[...] = dh.astype(BF16)

        @pl.when(pl.program_id(0) == 0)
        def _():
            dg_ref[...] = jnp.zeros_like(dg_ref)

        dg_ref[0:1, :] += jnp.sum(dyv * xhat, axis=0, keepdims=True)

    row = pl.BlockSpec((tm, dm), lambda i: (i, 0))
    slab = pl.BlockSpec((SMALL_ROWS, dm), lambda i: (0, 0))
    return pl.pallas_call(
        body, name=name, grid=(lp // tm,),
        out_shape=(jax.ShapeDtypeStruct((lp, dm), F32), jax.ShapeDtypeStruct((lp, dm), BF16),
                   jax.ShapeDtypeStruct((SMALL_ROWS, dm), F32)),
        in_specs=[row, row, pl.BlockSpec((1, dm), lambda i: (0, 0)), row, ANY], out_specs=(row, row, slab),
        compiler_params=_params(1),
    )(dy, h, g, dres, dep)


def _rms_bwd_rows(name, dy, h, g, dres, row0, nrows, tm, dep, dg_prev=None):
    dm = h.shape[1]
    b0 = row0 // tm

    def body(dy_ref, h_ref, g_ref, dr_ref, *rest):
        d_ref, dg_ref = rest[-2:]
        hv = h_ref[...]
        r = lax.rsqrt(jnp.mean(hv * hv, axis=-1, keepdims=True) + EPS)
        xhat = hv * r
        dyv = dy_ref[...]
        dxh = dyv * g_ref[...]
        d_ref[...] = dr_ref[...] + r * (dxh - xhat * jnp.mean(dxh * xhat, axis=-1, keepdims=True))

        @pl.when(pl.program_id(0) == 0)
        def _():
            dg_ref[...] = jnp.zeros_like(dg_ref) if dg_prev is None else rest[1][...]

        dg_ref[0:1, :] += jnp.sum(dyv * xhat, axis=0, keepdims=True)

    row = pl.BlockSpec((tm, dm), lambda i: (b0 + i, 0))
    slab = pl.BlockSpec((SMALL_ROWS, dm), lambda i: (0, 0))
    extra = [dep] + ([dg_prev] if dg_prev is not None else [])
    return pl.pallas_call(
        body, name=name, grid=(nrows // tm,),
        out_shape=(jax.ShapeDtypeStruct((nrows, dm), F32), jax.ShapeDtypeStruct((SMALL_ROWS, dm), F32)),
        in_specs=[row, row, pl.BlockSpec((1, dm), lambda i: (0, 0)), row, ANY] + ([slab] if dg_prev is not None else []),
        out_specs=(pl.BlockSpec((tm, dm), lambda i: (i, 0)), slab), compiler_params=_params(1),
    )(dy, h, g, dres, *extra)


def _gate_mix(name, proj, b_gate2, ya, pool_scale, yb, tm):
    _, lp, dm = proj.shape

    def body(ga_ref, gr_ref, b_ref, ya_ref, ps_ref, yb_ref, o_ref):
        g_a = jax.nn.sigmoid(ga_ref[...].astype(F32) + b_ref[0:1, :])
        g_b = jax.nn.sigmoid(gr_ref[...].astype(F32) + b_ref[1:2, :])
        y_a = ya_ref[...].astype(F32) * ps_ref[...]
        o_ref[...] = (g_a * y_a + g_b * yb_ref[...].astype(F32)).astype(BF16)

    row = pl.BlockSpec((tm, dm), lambda i: (i, 0))
    return pl.pallas_call(
        body, name=name, out_shape=jax.ShapeDtypeStruct((lp, dm), BF16), grid=(lp // tm,),
        in_specs=[pl.BlockSpec((None, tm, dm), lambda i: (4, i, 0)), pl.BlockSpec((None, tm, dm), lambda i: (5, i, 0)),
                  pl.BlockSpec((2, dm), lambda i: (0, 0)), row, pl.BlockSpec((1, dm), lambda i: (0, 0)), row],
        out_specs=row, compiler_params=_params(1),
    )(proj, proj, b_gate2, ya, pool_scale, yb)


def _gate_bwd(name, dmix, proj, b_gate2, ya, pool_scale, yb, tm):
    _, lp, dm = proj.shape

    def body(dm_ref, ga_ref, gr_ref, b_ref, ya_ref, ps_ref, yb_ref, dp_ref, dyb_ref, dya_ref, db_ref, dps_ref):
        dmx = dm_ref[...].astype(F32)
        g_a = jax.nn.sigmoid(ga_ref[...].astype(F32) + b_ref[0:1, :])
        g_b = jax.nn.sigmoid(gr_ref[...].astype(F32) + b_ref[1:2, :])
        ya_pre = ya_ref[...].astype(F32)
        ybv = yb_ref[...].astype(F32)
        ps = ps_ref[...]
        dga = dmx * (ya_pre * ps) * (g_a * (1.0 - g_a))
        dgr = dmx * ybv * (g_b * (1.0 - g_b))
        dp_ref[0] = dga.astype(BF16)
        dp_ref[1] = dgr.astype(BF16)
        dyb_ref[...] = (dmx * g_b).astype(BF16)
        dya_ref[...] = (dmx * g_a * ps).astype(BF16)

        @pl.when(pl.program_id(0) == 0)
        def _():
            db_ref[...] = jnp.zeros_like(db_ref)
            dps_ref[...] = jnp.zeros_like(dps_ref)

        db_ref[0:1, :] += jnp.sum(dga, axis=0, keepdims=True)
        db_ref[1:2, :] += jnp.sum(dgr, axis=0, keepdims=True)
        dps_ref[0:1, :] += jnp.sum(dmx * g_a * ya_pre, axis=0, keepdims=True)

    row = pl.BlockSpec((tm, dm), lambda i: (i, 0))
    one = pl.BlockSpec((1, dm), lambda i: (0, 0))
    slab = pl.BlockSpec((SMALL_ROWS, dm), lambda i: (0, 0))
    return pl.pallas_call(
        body, name=name, grid=(lp // tm,),
        out_shape=(jax.ShapeDtypeStruct((6, lp, dm), BF16), jax.ShapeDtypeStruct((lp, dm), BF16),
                   jax.ShapeDtypeStruct((lp, dm), BF16), jax.ShapeDtypeStruct((SMALL_ROWS, dm), F32),
                   jax.ShapeDtypeStruct((SMALL_ROWS, dm), F32)),
        in_specs=[row, pl.BlockSpec((None, tm, dm), lambda i: (4, i, 0)), pl.BlockSpec((None, tm, dm), lambda i: (5, i, 0)),
                  pl.BlockSpec((2, dm), lambda i: (0, 0)), row, one, row],
        out_specs=(pl.BlockSpec((2, tm, dm), lambda i: (2, i, 0)), row, row, slab, slab),
        compiler_params=_params(1),
    )(dmix, proj, proj, b_gate2, ya, pool_scale, yb)


def _final_loss(name, h2, g3, target, tm):
    lp, dm = h2.shape
    seq = target.shape[0]

    def body(h_ref, g_ref, t_ref, dh_ref, dhb_ref, ls_ref, dg_ref):
        @pl.when(pl.program_id(0) == 0)
        def _():
            ls_ref[...] = jnp.zeros_like(ls_ref)
            dg_ref[...] = jnp.zeros_like(dg_ref)

        hv = h_ref[...]
        gv = g_ref[...]
        r = lax.rsqrt(jnp.mean(hv * hv, axis=-1, keepdims=True) + EPS)
        xhat = hv * r
        err = xhat * gv - t_ref[...]
        dout = err * (1.0 / dm)
        dxh = dout * gv
        dh = r * (dxh - xhat * jnp.mean(dxh * xhat, axis=-1, keepdims=True))
        dh_ref[...] = dh
        dhb_ref[...] = dh.astype(BF16)
        ls_ref[0:1, :] += jnp.sum(err * err, axis=0, keepdims=True)
        dg_ref[0:1, :] += jnp.sum(dout * xhat, axis=0, keepdims=True)

    row = pl.BlockSpec((tm, dm), lambda i: (i, 0))
    slab = pl.BlockSpec((SMALL_ROWS, dm), lambda i: (0, 0))
    return pl.pallas_call(
        body, name=name, grid=(seq // tm,),
        out_shape=(jax.ShapeDtypeStruct((lp, dm), F32), jax.ShapeDtypeStruct((lp, dm), BF16),
                   jax.ShapeDtypeStruct((SMALL_ROWS, dm), F32), jax.ShapeDtypeStruct((SMALL_ROWS, dm), F32)),
        in_specs=[row, pl.BlockSpec((1, dm), lambda i: (0, 0)), row],
        out_specs=(row, row, slab, slab), compiler_params=_params(1),
    )(h2, g3, target)


def _zero_tail(name, arrays, tail):
    n = len(arrays)
    lp, dm = arrays[0].shape
    last = lp // tail - 1

    def body(*refs):
        for o_ref in refs[n:]:
            o_ref[...] = jnp.zeros_like(o_ref)

    return pl.pallas_call(
        body, name=name, grid=(1,), out_shape=tuple(jax.ShapeDtypeStruct(a.shape, a.dtype) for a in arrays),
        in_specs=[ANY] * n, out_specs=tuple(pl.BlockSpec((tail, dm), lambda i: (last, 0)) for _ in arrays),
        input_output_aliases={a: a for a in range(n)}, compiler_params=_params(1),
    )(*arrays)


def _shift(v, k):
    return pltpu.roll(v, k % v.shape[0], axis=0)


def _window_sum(v, group, sign):
    s2 = v + _shift(v, sign * 1)
    s4 = s2 + _shift(s2, sign * 2)
    s8 = s4 + _shift(s4, sign * 4)
    s16 = s8 + _shift(s8, sign * 8)
    return jnp.where(group == 0, s2, jnp.where(group == 1, s4, jnp.where(group == 2, s8, s16)))


def _pool_count(lp, group):
    row = lax.broadcasted_iota(jnp.int32, (lp, 1), 0)
    window = jnp.left_shift(2, group).astype(F32)
    meta_pos = (row - (lp - N_META) + 1).astype(F32)
    return jnp.where(row >= lp - N_META, jnp.minimum(meta_pos, window), window)


def _mixer_fwd(name, proj, conv_w, tc, dep):
    _, lp, dm = proj.shape
    per_group = dm // len(POOL_WINDOWS) // tc

    def body(u_ref, gb_ref, gc_ref, v_ref, cw_ref, _, p_ref, z_ref):
        group = pl.program_id(0) // per_group
        u = u_ref[...].astype(F32)
        p_ref[...] = (_window_sum(u, group, 1) / _pool_count(lp, group) - u).astype(BF16)
        cv = gc_ref[...].astype(F32) * v_ref[...].astype(F32)
        conv = cw_ref[0:1, :] * _shift(cv, 2) + cw_ref[1:2, :] * _shift(cv, 1) + cw_ref[2:3, :] * cv
        z_ref[...] = (gb_ref[...].astype(F32) * conv).astype(BF16)

    def seg(s):
        return pl.BlockSpec((None, lp, tc), lambda j: (s, 0, j))

    col = pl.BlockSpec((lp, tc), lambda j: (0, j))
    return pl.pallas_call(
        body, name=name, grid=(dm // tc,),
        out_shape=(jax.ShapeDtypeStruct((lp, dm), BF16), jax.ShapeDtypeStruct((lp, dm), BF16)),
        in_specs=[seg(0), seg(1), seg(2), seg(3), pl.BlockSpec((3, tc), lambda j: (0, j)), ANY],
        out_specs=(col, col), compiler_params=_params(1),
    )(proj, proj, proj, proj, conv_w, dep)


def _mixer_bwd(name, dz, dpooled, proj, conv_w, dproj, tc, dep):
    _, lp, dm = proj.shape
    per_group = dm // len(POOL_WINDOWS) // tc

    def body(dz_ref, dp_ref, gb_ref, gc_ref, v_ref, cw_ref, _, __, o_ref, dcw_ref):
        group = pl.program_id(0) // per_group
        dzv = dz_ref[...].astype(F32)
        gb = gb_ref[...].astype(F32)
        gc = gc_ref[...].astype(F32)
        vv = v_ref[...].astype(F32)
        cv = gc * vv
        c1 = _shift(cv, 1)
        c2 = _shift(cv, 2)
        w0, w1, w2 = cw_ref[0:1, :], cw_ref[1:2, :], cw_ref[2:3, :]
        o_ref[1] = (dzv * (w0 * c2 + w1 * c1 + w2 * cv)).astype(BF16)
        dconv = dzv * gb
        dcw_ref[...] = jnp.zeros_like(dcw_ref)
        dcw_ref[0:1, :] = jnp.sum(dconv * c2, axis=0, keepdims=True)
        dcw_ref[1:2, :] = jnp.sum(dconv * c1, axis=0, keepdims=True)
        dcw_ref[2:3, :] = jnp.sum(dconv * cv, axis=0, keepdims=True)
        dcv = w0 * _shift(dconv, -2) + w1 * _shift(dconv, -1) + w2 * dconv
        o_ref[2] = (dcv * vv).astype(BF16)
        o_ref[3] = (dcv * gc).astype(BF16)
        dpv = dp_ref[...].astype(F32)
        o_ref[0] = (_window_sum(dpv / _pool_count(lp, group), group, -1) - dpv).astype(BF16)

    def seg(s):
        return pl.BlockSpec((None, lp, tc), lambda j: (s, 0, j))

    col = pl.BlockSpec((lp, tc), lambda j: (0, j))
    return pl.pallas_call(
        body, name=name, grid=(dm // tc,),
        out_shape=(jax.ShapeDtypeStruct(dproj.shape, BF16), jax.ShapeDtypeStruct((SMALL_ROWS, dm), F32)),
        in_specs=[col, col, seg(1), seg(2), seg(3), pl.BlockSpec((3, tc), lambda j: (0, j)), ANY, ANY],
        out_specs=(pl.BlockSpec((4, lp, tc), lambda j: (0, 0, j)), pl.BlockSpec((SMALL_ROWS, tc), lambda j: (0, j))),
        input_output_aliases={6: 0}, compiler_params=_params(1),
    )(dz, dpooled, proj, proj, proj, conv_w, dproj, dep)


def _row_tile(r, c, bytes_per_row_elem=4, budget=2 * 1024 * 1024):
    best = None
    for t in range(16, r + 1, 16):
        if r % t == 0 and t * c * bytes_per_row_elem <= budget:
            best = t
    return best if best is not None else r


def _pair_add(name, g4, recv, core):
    s, r, c = g4.shape
    h = r // 2
    tr = _row_tile(h, c, budget=6 * 1024 * 1024)
    nb = h // tr

    def body(core_ref, g_ref, r_ref, o_ref):
        o_ref[...] = (g_ref[...].astype(F32) + r_ref[...].astype(F32)).astype(BF16)

    grid_spec = pltpu.PrefetchScalarGridSpec(
        num_scalar_prefetch=1, grid=(s, nb),
        in_specs=[pl.BlockSpec((None, tr, c), lambda si, j, core_ref: (si, core_ref[0] * nb + j, 0)),
                  pl.BlockSpec((None, tr, c), lambda si, j, core_ref: (si, j, 0))],
        out_specs=pl.BlockSpec((None, tr, c), lambda si, j, core_ref: (si, j, 0)))
    return pl.pallas_call(
        body, name=name, out_shape=jax.ShapeDtypeStruct((s, h, c), BF16), grid_spec=grid_spec,
        compiler_params=_params(2),
    )(core, g4, recv)


def _chip_sum(name, parts, recv, chip):
    _, h, c = parts.shape
    tr = _row_tile(h, c)

    def body(chip_ref, p_ref, r_ref, o_ref):
        acc = p_ref[...].astype(F32)
        for i in range(len(CHIP_FLIPS)):
            acc = acc + r_ref[i].astype(F32)
        o_ref[...] = acc

    grid_spec = pltpu.PrefetchScalarGridSpec(
        num_scalar_prefetch=1, grid=(h // tr,),
        in_specs=[pl.BlockSpec((None, tr, c), lambda j, chip_ref: (chip_ref[0], j, 0)),
                  pl.BlockSpec((len(CHIP_FLIPS), tr, c), lambda j, chip_ref: (0, j, 0))],
        out_specs=pl.BlockSpec((tr, c), lambda j, chip_ref: (j, 0)))
    return pl.pallas_call(
        body, name=name, out_shape=jax.ShapeDtypeStruct((h, c), F32), grid_spec=grid_spec, compiler_params=_params(1),
    )(chip, parts, recv)


def _adam_update(w, gv, m, v):
    c1 = 1.0 - ADAM_B1 ** ADAM_STEP
    c2 = 1.0 - ADAM_B2 ** ADAM_STEP
    nm = ADAM_B1 * m + (1.0 - ADAM_B1) * gv
    nv = ADAM_B2 * v + (1.0 - ADAM_B2) * (gv * gv)
    return -ADAM_LR * ((nm / c1) / (jnp.sqrt(nv / c2) + ADAM_EPS) + ADAM_WD * w), nm, nv


def _adamw_halves(name, w, g_own, g_sib, m, v, core, part=(0, 1), prev=None):
    r, c = w.shape
    rp = r // part[1]
    h = rp // 2
    tr = _row_tile(h, c, budget=2 * 1024 * 1024)
    nbh = h // tr
    j0 = part[0] * 2 * nbh
    n_prev = 0 if prev is None else 4

    def body(core_ref, w_ref, go_ref, gs_ref, m_ref, v_ref, *rest):
        g_ref, d_ref, nm_ref, nv_ref = rest[n_prev:]
        mine = (pl.program_id(0) // nbh) == core_ref[0]
        gv = jnp.where(mine, go_ref[...], gs_ref[...])
        g_ref[...] = gv
        d_ref[...], nm_ref[...], nv_ref[...] = _adam_update(w_ref[...], gv, m_ref[...], v_ref[...])

    def blk(fn):
        return pl.BlockSpec((tr, c), fn)

    full = blk(lambda j, core_ref: (j0 + j, 0))
    own = blk(lambda j, core_ref: (jnp.clip(j - core_ref[0] * nbh, 0, nbh - 1), 0))
    sib = blk(lambda j, core_ref: (jnp.clip(j - (1 - core_ref[0]) * nbh, 0, nbh - 1), 0))
    grid_spec = pltpu.PrefetchScalarGridSpec(
        num_scalar_prefetch=1, grid=(2 * nbh,), in_specs=[full, own, sib, full, full] + [ANY] * n_prev, out_specs=(full,) * 4)
    sds = jax.ShapeDtypeStruct((r, c), F32)
    return pl.pallas_call(
        body, name=name, out_shape=(sds,) * 4, grid_spec=grid_spec, compiler_params=_params(1),
        input_output_aliases={6 + i: i for i in range(n_prev)},
    )(core, w, g_own, g_sib, m, v, *(prev or ()))


def _adamw(name, w, g, m, v):
    r, c = w.shape

    def body(w_ref, g_ref, m_ref, v_ref, d_ref, nm_ref, nv_ref):
        d_ref[...], nm_ref[...], nv_ref[...] = _adam_update(w_ref[...], g_ref[...], m_ref[...], v_ref[...])

    blk = pl.BlockSpec((r, c), lambda j: (0, 0))
    sds = jax.ShapeDtypeStruct((r, c), F32)
    return pl.pallas_call(
        body, name=name, out_shape=(sds, sds, sds), grid=(1,), in_specs=[blk] * 4, out_specs=(blk,) * 3,
        compiler_params=_params(1),
    )(w, g, m, v)


def _cast_into_slot(name, w, chip, dtype, deps=()):
    r, c = w.shape
    tr = _row_tile(r, c)

    def body(chip_ref, w_ref, *rest):
        rest[-1][...] = w_ref[...].astype(dtype)

    grid_spec = pltpu.PrefetchScalarGridSpec(
        num_scalar_prefetch=1, grid=(r // tr,),
        in_specs=[pl.BlockSpec((tr, c), lambda j, chip_ref: (j, 0))] + [ANY] * len(deps),
        out_specs=pl.BlockSpec((None, tr, c), lambda j, chip_ref: (chip_ref[0], j, 0)))
    return pl.pallas_call(
        body, name=name, out_shape=jax.ShapeDtypeStruct((4, r, c), dtype), grid_spec=grid_spec, compiler_params=_params(1),
    )(chip, w, *deps)


def _cast_half_into_slot(name, w, chip_half, dtype, into=None):
    r, c = w.shape
    h = r // 2
    tr = _row_tile(h, c)
    nb = h // tr
    n_prev = 0 if into is None else 1

    def body(ids_ref, w_ref, *rest):
        rest[-1][...] = w_ref[...].astype(dtype)

    grid_spec = pltpu.PrefetchScalarGridSpec(
        num_scalar_prefetch=1, grid=(nb,),
        in_specs=[pl.BlockSpec((tr, c), lambda j, ids_ref: (ids_ref[1] * nb + j, 0))] + [ANY] * n_prev,
        out_specs=pl.BlockSpec((None, tr, c), lambda j, ids_ref: (ids_ref[0], ids_ref[1] * nb + j, 0)))
    return pl.pallas_call(
        body, name=name, out_shape=jax.ShapeDtypeStruct((4, r, c), dtype), grid_spec=grid_spec, compiler_params=_params(1),
        input_output_aliases={2: 0} if into is not None else {},
    )(chip_half, w, *([into] if into is not None else []))


def _place():
    return lax.axis_index("x"), lax.axis_index("y"), lax.axis_index("c")


def _chip_of(x, y, flip):
    px, py = x ^ flip[0], y ^ flip[1]
    return px, py, 2 * px + py


def _half(ref, which):
    rows = ref.shape[0] // 2
    return ref.at[pl.ds(which * rows, rows)]


HBM = pl.BlockSpec(memory_space=pltpu.HBM)
SEM = pl.BlockSpec(memory_space=pltpu.SEMAPHORE)
SPLIT_COPY = pltpu.CompilerParams(has_side_effects=pltpu.SideEffectType.DATAFLOW_SIDE_EFFECTING)


def _in_hbm(arrays):
    return [pltpu.with_memory_space_constraint(t, pltpu.HBM) for t in arrays]


TOKEN = jax.ShapeDtypeStruct((SMALL_ROWS, LANES), F32)
TOKEN_SPEC = pl.BlockSpec(memory_space=pltpu.VMEM)


NEIGHBOUR_FLIPS = CHIP_FLIPS[:2]


def _relay_chips(x, y, c):
    fx, fy = x ^ c, y ^ (1 - c)
    return (fx, fy), 2 * fx + fy, 2 * (1 - x) + (1 - y)


def _ag_start(name, slabs, deps=()):
    n = len(slabs)
    nn = len(NEIGHBOUR_FLIPS)

    def body(*refs):
        no = n + len(deps)
        ssem, rsem = refs[no], refs[no + 1]
        outs = refs[no + 2:no + 2 + n]
        token = refs[no + 2 + n]
        token[...] = jnp.zeros_like(token)
        x, y, c = _place()
        k = 2 * x + y
        for a in range(n):
            for j, flip in enumerate(NEIGHBOUR_FLIPS):
                px, py, _ = _chip_of(x, y, flip)
                mine = _half(outs[a].at[k], c)
                pltpu.make_async_remote_copy(src_ref=mine, dst_ref=mine, send_sem=ssem.at[a * nn + j],
                                             recv_sem=rsem.at[a * nn + j], device_id=(px, py, c), device_id_type=MESH).start()

    sem = pltpu.SemaphoreType.DMA((nn * n,))
    res = pl.pallas_call(
        body, name=name, out_shape=(sem, sem) + tuple(pltpu.HBM(t.shape, t.dtype) for t in slabs) + (TOKEN,),
        in_specs=[HBM] * n + [ANY] * len(deps), out_specs=tuple([SEM, SEM] + [HBM] * n + [TOKEN_SPEC]),
        input_output_aliases={a: 2 + a for a in range(n)}, compiler_params=SPLIT_COPY,
    )(*_in_hbm(slabs), *deps)
    return (res[0], res[1]), list(res[2:2 + n]), res[2 + n]


def _ag_relay(name, slabs, sems, after, then_start=()):
    n = len(slabs)
    m = len(then_start)
    nn = len(NEIGHBOUR_FLIPS)

    def body(*refs):
        no = n + 2 + m + len(after)
        ins = refs[:n]
        ssem, rsem = refs[n], refs[n + 1]
        r_s, r_r, p_s, p_r = refs[no:no + 4]
        x, y, c = _place()
        k = 2 * x + y
        (fx, fy), _, _ = _relay_chips(x, y, c)
        for a in range(n):
            for j, flip in enumerate(NEIGHBOUR_FLIPS):
                _, _, kj = _chip_of(x, y, flip)
                landed = _half(ins[a].at[kj], c)
                cp = pltpu.make_async_remote_copy(
                    src_ref=_half(ins[a].at[k], c), dst_ref=landed, send_sem=ssem.at[a * nn + j],
                    recv_sem=rsem.at[a * nn + j], device_id=(x, y, c), device_id_type=MESH)
                cp.wait_send()
                cp.wait_recv()
        for a in range(n):
            near = _half(ins[a].at[2 * (x ^ (1 - c)) + (y ^ c)], c)
            pltpu.make_async_remote_copy(src_ref=near, dst_ref=near, send_sem=r_s.at[a], recv_sem=r_r.at[a],
                                         device_id=(fx, fy, c), device_id_type=MESH).start()
            for j, flip in enumerate(NEIGHBOUR_FLIPS):
                _, _, kj = _chip_of(x, y, flip)
                landed = _half(ins[a].at[kj], c)
                pltpu.make_async_remote_copy(src_ref=landed, dst_ref=landed, send_sem=p_s.at[a * nn + j],
                                             recv_sem=p_r.at[a * nn + j], device_id=(x, y, 1 - c), device_id_type=MESH).start()
        if m:
            d_s, d_r = refs[no + 4 + n], refs[no + 5 + n]
            nxt = refs[no + 6 + n:]
            for a in range(m):
                for j, flip in enumerate(NEIGHBOUR_FLIPS):
                    px, py, _ = _chip_of(x, y, flip)
                    mine = _half(nxt[a].at[k], c)
                    pltpu.make_async_remote_copy(src_ref=mine, dst_ref=mine, send_sem=d_s.at[a * nn + j],
                                                 recv_sem=d_r.at[a * nn + j], device_id=(px, py, c), device_id_type=MESH).start()

    rsem_t = pltpu.SemaphoreType.DMA((n,))
    psem_t = pltpu.SemaphoreType.DMA((nn * n,))
    out_shape = (rsem_t, rsem_t, psem_t, psem_t) + tuple(pltpu.HBM(t.shape, t.dtype) for t in slabs)
    out_specs = [SEM] * 4 + [HBM] * n
    aliases = {a: 4 + a for a in range(n)}
    if m:
        dsem_t = pltpu.SemaphoreType.DMA((nn * m,))
        out_shape += (dsem_t, dsem_t) + tuple(pltpu.HBM(t.shape, t.dtype) for t in then_start)
        out_specs += [SEM, SEM] + [HBM] * m
        aliases.update({n + 2 + a: 4 + n + 2 + a for a in range(m)})
    res = pl.pallas_call(
        body, name=name, out_shape=out_shape, in_specs=[HBM] * n + [SEM, SEM] + [HBM] * m + [ANY] * len(after),
        out_specs=tuple(out_specs), input_output_aliases=aliases, compiler_params=SPLIT_COPY,
    )(*slabs, sems[0], sems[1], *_in_hbm(list(then_start)), *after)
    if not m:
        return tuple(res[:4]), list(res[4:])
    return (tuple(res[:4]), list(res[4:4 + n])), ((res[4 + n], res[5 + n]), list(res[6 + n:]))


def _wait_passes(ins, p_s, p_r, x, y, c):
    nn = len(NEIGHBOUR_FLIPS)
    for a in range(len(ins)):
        for j, flip in enumerate(NEIGHBOUR_FLIPS):
            _, _, kj = _chip_of(x, y, flip)
            cp = pltpu.make_async_remote_copy(
                src_ref=_half(ins[a].at[kj], c), dst_ref=_half(ins[a].at[kj], 1 - c), send_sem=p_s.at[a * nn + j],
                recv_sem=p_r.at[a * nn + j], device_id=(x, y, c), device_id_type=MESH)
            cp.wait_send()
            cp.wait_recv()


def _ag_relay_wait(name, slabs, sems, after):
    n = len(slabs)
    ns = len(sems)

    def body(*refs):
        no = n + ns + len(after)
        ins = refs[:n]
        r_s, r_r = refs[n], refs[n + 1]
        f_s, f_r = refs[no], refs[no + 1]
        x, y, c = _place()
        _, _, kd = _relay_chips(x, y, c)
        for a in range(n):
            near = _half(ins[a].at[2 * (x ^ (1 - c)) + (y ^ c)], c)
            cp = pltpu.make_async_remote_copy(src_ref=near, dst_ref=_half(ins[a].at[kd], c), send_sem=r_s.at[a],
                                              recv_sem=r_r.at[a], device_id=(x, y, c), device_id_type=MESH)
            cp.wait_send()
            cp.wait_recv()
        if ns == 4:
            _wait_passes(ins, refs[n + 2], refs[n + 3], x, y, c)
        for a in range(n):
            diag = _half(ins[a].at[kd], c)
            pltpu.make_async_remote_copy(src_ref=diag, dst_ref=diag, send_sem=f_s.at[a], recv_sem=f_r.at[a],
                                         device_id=(x, y, 1 - c), device_id_type=MESH).start()

    sem = pltpu.SemaphoreType.DMA((n,))
    res = pl.pallas_call(
        body, name=name, out_shape=(sem, sem) + tuple(pltpu.HBM(t.shape, t.dtype) for t in slabs),
        in_specs=[HBM] * n + [SEM] * ns + [ANY] * len(after), out_specs=tuple([SEM, SEM] + [HBM] * n),
        input_output_aliases={a: 2 + a for a in range(n)}, compiler_params=SPLIT_COPY,
    )(*slabs, *sems, *after)
    return (res[0], res[1]), list(res[2:])


def _ag_final_wait(name, slabs, sems, after, first=0):
    n = len(slabs)

    def body(*refs):
        ins = refs[:n]
        f_s, f_r = refs[n], refs[n + 1]
        x, y, c = _place()
        _, _, kd = _relay_chips(x, y, c)
        for a in range(n):
            cp = pltpu.make_async_remote_copy(
                src_ref=_half(ins[a].at[kd], c), dst_ref=_half(ins[a].at[kd], 1 - c), send_sem=f_s.at[first + a],
                recv_sem=f_r.at[first + a], device_id=(x, y, c), device_id_type=MESH)
            cp.wait_send()
            cp.wait_recv()

    return pl.pallas_call(
        body, name=name, out_shape=tuple(pltpu.HBM(t.shape, t.dtype) for t in slabs),
        in_specs=[HBM] * n + [SEM, SEM] + [ANY] * len(after), out_specs=tuple([HBM] * n),
        input_output_aliases={a: a for a in range(n)}, compiler_params=SPLIT_COPY,
    )(*slabs, sems[0], sems[1], *after)


def _sibling_part(ref, c, halves):
    if not halves:
        return ref
    h = ref.shape[1] // 2
    return ref.at[:, pl.ds((1 - c) * h, h)]


def _swap_start(name, grads, halves=True, deps=()):
    n = len(grads)

    def body(*refs):
        no = 2 * n + len(deps)
        ssem, rsem = refs[no], refs[no + 1]
        src, land = refs[no + 2:no + n + 2], refs[no + n + 2:no + 2 * n + 2]
        token = refs[no + 2 * n + 2]
        token[...] = jnp.zeros_like(token)
        x, y, c = _place()
        for a in range(n):
            pltpu.make_async_remote_copy(
                src_ref=_sibling_part(src[a], c, halves), dst_ref=land[a], send_sem=ssem.at[a], recv_sem=rsem.at[a],
                device_id=(x, y, 1 - c), device_id_type=MESH).start()

    zones = [lax.empty((g.shape[0], g.shape[1] // 2, g.shape[2]) if halves else g.shape, g.dtype) for g in grads]
    sem = pltpu.SemaphoreType.DMA((n,))
    res = pl.pallas_call(
        body, name=name,
        out_shape=(sem, sem) + tuple(pltpu.HBM(t.shape, t.dtype) for t in list(grads) + zones) + (TOKEN,),
        in_specs=[HBM] * (2 * n) + [ANY] * len(deps), out_specs=tuple([SEM, SEM] + [HBM] * (2 * n) + [TOKEN_SPEC]),
        input_output_aliases={i: 2 + i for i in range(2 * n)}, compiler_params=SPLIT_COPY,
    )(*_in_hbm(list(grads) + zones), *deps)
    return (res[0], res[1], list(res[2:2 + n]), list(res[2 + n:2 + 2 * n])), res[2 + 2 * n]


def _swap_wait(name, ssem, rsem, grads, zones, after, halves=True):
    n = len(grads)

    def body(*refs):
        src, land = refs[:n], refs[n:2 * n]
        ss, rs = refs[2 * n], refs[2 * n + 1]
        x, y, c = _place()
        for a in range(n):
            cp = pltpu.make_async_remote_copy(
                src_ref=_sibling_part(src[a], c, halves), dst_ref=land[a], send_sem=ss.at[a], recv_sem=rs.at[a],
                device_id=(x, y, c), device_id_type=MESH)
            cp.wait_send()
            cp.wait_recv()

    res = pl.pallas_call(
        body, name=name, out_shape=tuple(pltpu.HBM(t.shape, t.dtype) for t in list(grads) + list(zones)),
        in_specs=[HBM] * (2 * n) + [SEM, SEM] + [ANY] * len(after), out_specs=tuple([HBM] * (2 * n)),
        input_output_aliases={i: i for i in range(2 * n)}, compiler_params=SPLIT_COPY,
    )(*grads, *zones, ssem, rsem, *after)
    return list(res[:n]), list(res[n:])


def _scatter_start(name, parts):
    n = len(parts)
    nf = len(CHIP_FLIPS)

    def body(*refs):
        ssem, rsem = refs[2 * n], refs[2 * n + 1]
        src, land = refs[2 * n + 2:3 * n + 2], refs[3 * n + 2:4 * n + 2]
        token = refs[4 * n + 2]
        token[...] = jnp.zeros_like(token)
        x, y, c = _place()
        for a in range(n):
            for j, flip in enumerate(CHIP_FLIPS):
                px, py, kj = _chip_of(x, y, flip)
                pltpu.make_async_remote_copy(
                    src_ref=src[a].at[kj], dst_ref=land[a].at[j], send_sem=ssem.at[a * nf + j], recv_sem=rsem.at[a * nf + j],
                    device_id=(px, py, c), device_id_type=MESH).start()

    zones = [lax.empty((nf,) + p.shape[1:], p.dtype) for p in parts]
    sem = pltpu.SemaphoreType.DMA((nf * n,))
    res = pl.pallas_call(
        body, name=name,
        out_shape=(sem, sem) + tuple(pltpu.HBM(t.shape, t.dtype) for t in list(parts) + zones)
        + (jax.ShapeDtypeStruct((SMALL_ROWS, LANES), F32),),
        in_specs=[HBM] * (2 * n),
        out_specs=tuple([SEM, SEM] + [HBM] * (2 * n) + [pl.BlockSpec(memory_space=pltpu.VMEM)]),
        input_output_aliases={i: 2 + i for i in range(2 * n)}, compiler_params=SPLIT_COPY,
    )(*_in_hbm(list(parts) + zones))
    return (res[0], res[1], list(res[2:2 + n]), list(res[2 + n:2 + 2 * n])), res[2 + 2 * n]


def _scatter_wait(name, ssem, rsem, parts, zones, after):
    n = len(parts)
    nf = len(CHIP_FLIPS)

    def body(*refs):
        src, land = refs[:n], refs[n:2 * n]
        ss, rs = refs[2 * n], refs[2 * n + 1]
        x, y, c = _place()
        for a in range(n):
            for j, flip in enumerate(CHIP_FLIPS):
                _, _, kj = _chip_of(x, y, flip)
                cp = pltpu.make_async_remote_copy(
                    src_ref=src[a].at[kj], dst_ref=land[a].at[j], send_sem=ss.at[a * nf + j], recv_sem=rs.at[a * nf + j],
                    device_id=(x, y, c), device_id_type=MESH)
                cp.wait_send()
                cp.wait_recv()

    res = pl.pallas_call(
        body, name=name, out_shape=tuple(pltpu.HBM(t.shape, t.dtype) for t in list(parts) + list(zones)),
        in_specs=[HBM] * (2 * n) + [SEM, SEM] + [ANY] * len(after), out_specs=tuple([HBM] * (2 * n)),
        input_output_aliases={i: i for i in range(2 * n)}, compiler_params=SPLIT_COPY,
    )(*parts, *zones, ssem, rsem, *after)
    return list(res[:n]), list(res[n:])


N_PEERS = 7


def _peer(x, y, c, mask):
    px, py, pc = x ^ ((mask >> 2) & 1), y ^ ((mask >> 1) & 1), c ^ (mask & 1)
    return (px, py, pc), 4 * px + 2 * py + pc


def _reduce_start(vec, deps):
    nd = len(deps)

    def body(*refs):
        ssem, rsem, src, land, token = refs[2 + nd:]
        token[...] = jnp.zeros_like(token)
        x, y, c = _place()
        me = 4 * x + 2 * y + c
        for mask in range(1, N_PEERS + 1):
            to, _ = _peer(x, y, c, mask)
            pltpu.make_async_remote_copy(src_ref=src, dst_ref=land.at[me], send_sem=ssem.at[mask - 1],
                                         recv_sem=rsem.at[mask - 1], device_id=to, device_id_type=MESH).start()

    zone = lax.empty((N_PEERS + 1,) + vec.shape, vec.dtype)
    sem = pltpu.SemaphoreType.DMA((N_PEERS,))
    res = pl.pallas_call(
        body, name="reduce_start",
        out_shape=(sem, sem, pltpu.HBM(vec.shape, vec.dtype), pltpu.HBM(zone.shape, zone.dtype), TOKEN),
        in_specs=[HBM, HBM] + [ANY] * nd, out_specs=(SEM, SEM, HBM, HBM, TOKEN_SPEC),
        input_output_aliases={0: 2, 1: 3}, compiler_params=SPLIT_COPY,
    )(*_in_hbm([vec, zone]), *deps)
    return res[:4], res[4]


def _reduce_wait(ssem, rsem, vec, zone, after):
    def body(src, land, ss, rs, *_):
        x, y, c = _place()
        for mask in range(1, N_PEERS + 1):
            _, frm = _peer(x, y, c, mask)
            cp = pltpu.make_async_remote_copy(src_ref=src, dst_ref=land.at[frm], send_sem=ss.at[mask - 1],
                                              recv_sem=rs.at[mask - 1], device_id=(x, y, c), device_id_type=MESH)
            cp.wait_send()
            cp.wait_recv()

    return pl.pallas_call(
        body, name="reduce_wait", out_shape=(pltpu.HBM(vec.shape, vec.dtype), pltpu.HBM(zone.shape, zone.dtype)),
        in_specs=[HBM, HBM, SEM, SEM] + [ANY] * len(after), out_specs=(HBM, HBM),
        input_output_aliases={0: 0, 1: 1}, compiler_params=SPLIT_COPY,
    )(vec, zone, ssem, rsem, *after)


def _reduce_sum(vec, zone, me, loss_row, loss_scale):
    r, dm = vec.shape

    def body(me_ref, v_ref, z_ref, o_ref, l_ref):
        acc = None
        for i in range(N_PEERS + 1):
            term = jnp.where(me_ref[0] == i, v_ref[...], z_ref[i])
            acc = term if acc is None else acc + term
        o_ref[...] = acc
        l_ref[...] = jnp.sum(acc[loss_row:loss_row + SMALL_ROWS, :], axis=(0, 1), keepdims=True) * loss_scale

    grid_spec = pltpu.PrefetchScalarGridSpec(
        num_scalar_prefetch=1, grid=(1,),
        in_specs=[pl.BlockSpec((r, dm), lambda i, me_ref: (0, 0)), pl.BlockSpec((N_PEERS + 1, r, dm), lambda i, me_ref: (0, 0, 0))],
        out_specs=(pl.BlockSpec((r, dm), lambda i, me_ref: (0, 0)), pl.BlockSpec((1, 1), lambda i, me_ref: (0, 0))))
    return pl.pallas_call(
        body, name="reduce_sum", out_shape=(jax.ShapeDtypeStruct((r, dm), F32), jax.ShapeDtypeStruct((1, 1), F32)),
        grid_spec=grid_spec, compiler_params=_params(1),
    )(me, vec, zone)


def kernel(x, meta_tokens, norm_mix_g, w_in, b_gate, pool_w, pool_scale, conv_w, conv_out_w, w_o, norm_ffn_g, w_gate_up, w_down, norm_final_g, loss_target, m_meta_tokens, m_norm_mix_g, m_w_in, m_b_gate, m_pool_w, m_pool_scale, m_conv_w, m_conv_out_w, m_w_o, m_norm_ffn_g, m_w_gate_up, m_w_down, m_norm_final_g, v_meta_tokens, v_norm_mix_g, v_w_in, v_b_gate, v_pool_w, v_pool_scale, v_conv_w, v_conv_out_w, v_w_o, v_norm_ffn_g, v_w_gate_up, v_w_down, v_norm_final_g):
    seq, dm = x.shape[1], x.shape[2]
    tail = TAIL_ROWS
    tm = tail
    lp = seq + tail
    tm_row = _row_tile(lp, dm, 4, 3 * 1024 * 1024)
    tm_seq = _row_tile(seq, dm, 4, 3 * 1024 * 1024)
    n_chips = 4
    n_groups = len(POOL_WINDOWS)
    gw = dm // n_groups
    tc = min(256, gw)
    cx, cy, cc = _place()
    chip = 2 * cx + cy
    dloc = dm // n_chips

    pool2 = pool_w.reshape(n_groups * pool_w.shape[1], gw)
    big = {"w_in": w_in, "w_gate_up": w_gate_up, "pool_w": pool2, "conv_out_w": conv_out_w, "w_o": w_o, "w_down": w_down}
    chip1 = jnp.reshape(chip, (1,)).astype(jnp.int32)
    core = jnp.reshape(cc, (1,)).astype(jnp.int32)
    small_loc = jnp.concatenate([meta_tokens, jnp.pad(conv_w, ((0, 8 - conv_w.shape[0]), (0, 0))),
                                 jnp.zeros((8, dloc), F32)], axis=0)
    g1, g2, g3 = norm_mix_g.reshape(1, dm), norm_ffn_g.reshape(1, dm), norm_final_g.reshape(1, dm)
    b_gate2 = b_gate.reshape(2, dm)
    ps = pool_scale.reshape(1, dm)
    mine = jnp.stack([chip, cc]).astype(jnp.int32)
    other = jnp.stack([chip, 1 - cc]).astype(jnp.int32)
    first = [_cast_into_slot("place_small", small_loc, chip1, F32), _cast_half_into_slot("cast_w_in_sent", w_in, mine, BF16)]
    sems, first, token = _ag_start("ag_start_first", first)
    first[1] = _cast_half_into_slot("cast_w_in_kept", w_in, other, BF16, into=first[1])
    cast = {nme: _cast_into_slot("cast_" + nme, big[nme], chip1, BF16, deps=(token,))
            for nme in ["pool_w", "conv_out_w", "w_o", "w_gate_up", "w_down"]}
    sems, first = _ag_relay("ag_relay_first", first, sems, list(cast.values()))
    sems, (small4, w_in4) = _ag_relay_wait("ag_relay_wait_first", first, sems, [])
    (small4,) = _ag_final_wait("ag_final_wait_small", [small4], sems, [])
    mixer_w = [cast["pool_w"], cast["conv_out_w"], cast["w_o"]]
    sems_mix, mixer_w, token = _ag_start("ag_start_mixer", mixer_w, deps=(small4,))
    sems_gu, (w_gu4,), token = _ag_start("ag_start_gate_up", [cast["w_gate_up"]], deps=(token,))

    small_f = jnp.transpose(small4, (1, 0, 2)).reshape(small4.shape[1], dm)
    meta_f = small_f[:N_META]
    conv_w_f = small_f[N_META:N_META + 3]
    tail_rows = jnp.concatenate([jnp.zeros((tail - N_META, dm), F32), meta_f], axis=0)
    h0_hn1 = _rms_fwd_into("rms_mix", x[0], g1, lp, 0, tm_seq, deps=(token,))
    h0, hn1 = _rms_fwd_into("rms_mix_tail", tail_rows, g1, lp, seq, tail, prev=h0_hn1)
    (w_in4,) = _ag_final_wait("ag_final_wait_first", [w_in4], sems, [hn1], first=1)
    proj = _nn_sharded("proj", hn1, w_in4, 6)
    sems_mix, mixer_w = _ag_relay("ag_relay_mixer", mixer_w, sems_mix, [proj])
    (sems_gu, (w_gu4,)), (sems_down, (w_down4,)) = _ag_relay("ag_relay_gate_up", [w_gu4], sems_gu, [mixer_w[0]],
                                                              then_start=[cast["w_down"]])
    pooled, z = _mixer_fwd("mixer_fwd", proj, conv_w_f, tc, w_down4)
    sems_mix, mixer_w = _ag_relay_wait("ag_relay_wait_mixer", mixer_w, sems_mix, [pooled])
    pool4, conv_out4, w_o4 = _ag_final_wait("ag_final_wait_mixer", mixer_w, sems_mix, [])
    pool_f = jnp.transpose(pool4.reshape(n_chips, n_groups, gw // n_chips, gw), (1, 0, 2, 3)).reshape(n_groups, gw, gw)
    conv_out_f = conv_out4.reshape(dm, dm)
    w_o_f = w_o4.reshape(dm, dm)
    ya = _pool_fwd("pool_proj", pooled, pool_f)
    yb = _nn_plain("conv_out", z, conv_out_f, BF16)
    mix = _gate_mix("gate_mix", proj, b_gate2, ya, ps, yb, tm_row)
    sems_gu, (w_gu4,) = _ag_relay_wait("ag_relay_wait_gate_up", [w_gu4], sems_gu, [mix])
    h1 = _nn_plain("attn_out", mix, w_o_f, F32, res=h0, tn_pref=256)
    (w_gu4,) = _ag_final_wait("ag_final_wait_gate_up", [w_gu4], sems_gu, [h1])
    hn2 = _rms_fwd("rms_ffn", h1, g2, tm_row)
    sems_down, (w_down4,) = _ag_relay("ag_relay_down", [w_down4], sems_down, [hn2])
    gu, act = _gate_up_swiglu("gate_up", hn2, w_gu4, w_down4)
    sems_down, (w_down4,) = _ag_relay_wait("ag_relay_wait_down", [w_down4], sems_down, [act])
    (w_down4,) = _ag_final_wait("ag_final_wait_down", [w_down4], sems_down, [])
    w_down_f = w_down4.reshape(-1, dm)
    h2 = _nn_rows("ffn_down", act, w_down_f, h1)
    dh2, dh2b, loss_cols, dg3 = _final_loss("final_loss", h2, g3, loss_target[0], tm_seq)
    dh2, dh2b = _zero_tail("final_loss_tail", [dh2, dh2b], tail)

    def scatter(tag, names_g, swap, after):
        grads_g, got = _swap_wait("swap_wait_" + tag, *swap, [after])
        pairs = [_pair_add("pair_add_" + nme, g4, rv, core) for nme, g4, rv in zip(names_g, grads_g, got)]
        return _scatter_start("scatter_start_" + tag, pairs)

    dgu = _dact_swiglu_bwd("d_gate_up", dh2b, w_down_f, gu)
    gw_down = _tn_plain("dw_down", act, dh2b)
    gw_gu = _tn_sharded("dw_gate_up", hn2, dgu, n_chips)
    swap_a, token = _swap_start("swap_start_a", [gw_gu, gw_down.reshape(n_chips, -1, dm)])
    dhn2 = _nt_sharded("d_hn2", dgu, w_gu4, tr_pref=2816, row_tiles=2, deps=(token,))
    flight_a, token = scatter("a", ["w_gate_up", "w_down"], swap_a, dhn2)
    dh1, dh1b, dg2 = _rms_bwd("rms_ffn_bwd", dhn2, h1, g2, dh2, tm_row, token)
    dmix = _nt_plain("d_mix", dh1b, w_o_f)
    gw_o = _tn_plain("dw_o", mix, dh1b)
    dproj, dyb, dya, db_gate, dps = _gate_bwd("gate_bwd", dmix, proj, b_gate2, ya, ps, yb, tm_row)
    gw_conv_out = _tn_plain("dw_conv_out", z, dyb)
    gw_pool = _pool_bwd_w("dw_pool", pooled, dya)
    gw_pool = jnp.transpose(gw_pool.reshape(n_groups, n_chips, gw // n_chips, gw), (1, 0, 2, 3))
    dpooled = _pool_bwd_act("d_pooled", dya, pool_f)
    dz = _nt_plain("d_z", dyb, conv_out_f)
    dproj, dconv_w = _mixer_bwd("mixer_bwd", dz, dpooled, proj, conv_w_f, dproj, tc, dpooled)
    gw_in0 = _tn_sharded("dw_in_0", hn1, dproj, n_chips, part=(0, 2))
    swap_b, token = _swap_start("swap_start_b", [gw_o.reshape(n_chips, dloc, dm), gw_conv_out.reshape(n_chips, dloc, dm),
                                                 gw_pool.reshape(n_chips, n_groups * (gw // n_chips), gw), gw_in0])
    gw_in1 = _tn_sharded("dw_in_1", hn1, dproj, n_chips, part=(1, 2), deps=(token,))
    flight_b, token = scatter("b", ["w_o", "conv_out_w", "pool_w", "w_in_0"], swap_b, gw_in1)
    swap_c, token = _swap_start("swap_start_c", [gw_in1], deps=(token,))
    dhn1 = _nt_in_proj("d_hn1", dproj, w_in4, deps=(token,))
    flight_c, token = scatter("c", ["w_in_1"], swap_c, dhn1)
    dx, dg1 = _rms_bwd_rows("rms_mix_bwd", dhn1, h0, g1, dh1, 0, seq, tm_seq, token)
    dtail, dg1 = _rms_bwd_rows("rms_mix_bwd_tail", dhn1, h0, g1, dh1, seq, tail, tail, dx, dg_prev=dg1)
    grad_x = dx[None]
    dmeta = dtail[tail - N_META:]

    given = dict(meta_tokens=(meta_tokens, m_meta_tokens, v_meta_tokens), norm_mix_g=(norm_mix_g, m_norm_mix_g, v_norm_mix_g),
                 w_in=(w_in, m_w_in, v_w_in), b_gate=(b_gate, m_b_gate, v_b_gate), pool_w=(pool_w, m_pool_w, v_pool_w),
                 pool_scale=(pool_scale, m_pool_scale, v_pool_scale), conv_w=(conv_w, m_conv_w, v_conv_w),
                 conv_out_w=(conv_out_w, m_conv_out_w, v_conv_out_w), w_o=(w_o, m_w_o, v_w_o),
                 norm_ffn_g=(norm_ffn_g, m_norm_ffn_g, v_norm_ffn_g), w_gate_up=(w_gate_up, m_w_gate_up, v_w_gate_up),
                 w_down=(w_down, m_w_down, v_w_down), norm_final_g=(norm_final_g, m_norm_final_g, v_norm_final_g))
    order = list(given.keys())
    grad, delta, new_m, new_v = {}, {}, {}, {}
    vec = jnp.concatenate([dg1, dg2, dg3, db_gate, dps, loss_cols, dconv_w, dmeta], axis=0)
    loss_row = 5 * SMALL_ROWS
    groups_g = {"a": [("w_gate_up", (0, 1)), ("w_down", (0, 1))],
                "b": [("w_o", (0, 1)), ("conv_out_w", (0, 1)), ("pool_w", (0, 1)), ("w_in", (0, 2))], "c": [("w_in", (1, 2))]}
    results = {}

    def reduced(tag, flight, after):
        pairs, zones = _scatter_wait("scatter_wait_" + tag, *flight, after)
        halves = [_chip_sum("chip_sum_%s_%d" % (nme, part[0]), p, rv, chip1) for (nme, part), p, rv in zip(groups_g[tag], pairs, zones)]
        return _swap_start("send_start_" + tag, halves, halves=False)

    def update(tag, send, after):
        halves, sib_halves = _swap_wait("send_wait_" + tag, *send, after, halves=False)
        deltas = []
        for (nme, part), g_own, g_sib in zip(groups_g[tag], halves, sib_halves):
            w, m, v = given[nme]
            shape2 = (2 * g_own.shape[0] * part[1], g_own.shape[1])
            results[nme] = _adamw_halves("adamw_%s_%d" % (nme, part[0]), w.reshape(shape2), g_own, g_sib, m.reshape(shape2),
                                         v.reshape(shape2), core, part=part, prev=results.get(nme))
            grad[nme], delta[nme], new_m[nme], new_v[nme] = [t.reshape(w.shape) for t in results[nme]]
            deltas.append(results[nme][1])
        return deltas

    send_a, token = reduced("a", flight_a, [dx])
    send_b, token = reduced("b", flight_b, [token])
    done_a = update("a", send_a, [token])
    send_c, token = reduced("c", flight_c, done_a)
    me1 = jnp.reshape(4 * cx + 2 * cy + cc, (1,)).astype(jnp.int32)
    red_flight, token = _reduce_start(vec, [token])
    done_b = update("b", send_b, [token])
    done_c = update("c", send_c, done_b)
    red, loss11 = _reduce_sum(*_reduce_wait(*red_flight, done_c), me1, loss_row, 0.5 / dm)
    loss = loss11[0, 0]
    col0 = chip * dloc
    g_small = {
        "norm_mix_g": red[0], "norm_ffn_g": red[SMALL_ROWS], "norm_final_g": red[2 * SMALL_ROWS],
        "b_gate": red[3 * SMALL_ROWS:3 * SMALL_ROWS + 2].reshape(-1), "pool_scale": red[4 * SMALL_ROWS],
        "conv_w": lax.dynamic_slice(red, (6 * SMALL_ROWS, col0), (3, dloc)),
        "meta_tokens": lax.dynamic_slice(red, (7 * SMALL_ROWS, col0), (N_META, dloc)),
    }

    vec_names = ["norm_mix_g", "norm_ffn_g", "norm_final_g", "pool_scale"]

    def slab_vec(pick):
        rows = [pick(nme).reshape(1, dm) for nme in vec_names] + [pick("b_gate").reshape(2, dm), jnp.zeros((2, dm), F32)]
        return jnp.concatenate(rows, axis=0)

    def slab_col(pick):
        return jnp.concatenate([pick("meta_tokens"), pick("conv_w"), jnp.zeros((5, dloc), F32)], axis=0)

    for slab, tag in ((slab_vec, "vec"), (slab_col, "col")):
        d, nm, nv = _adamw("adamw_small_" + tag, slab(lambda nme: given[nme][0]), slab(lambda nme: g_small[nme]),
                           slab(lambda nme: given[nme][1]), slab(lambda nme: given[nme][2]))
        for out, res in ((delta, d), (new_m, nm), (new_v, nv)):
            if tag == "vec":
                for i, nme in enumerate(vec_names):
                    out[nme] = res[i]
                out["b_gate"] = res[4:6].reshape(-1)
            else:
                out["meta_tokens"] = res[:N_META]
                out["conv_w"] = res[N_META:N_META + 3]
    grad.update(g_small)
    return (loss, grad_x, *[grad[nme] for nme in order], *[delta[nme] for nme in order],
            *[new_m[nme] for nme in order], *[new_v[nme] for nme in order])
```

```python
import math

import jax
import jax.numpy as jnp
from jax import lax
from jax.experimental import pallas as pl
from jax.experimental.pallas import tpu as pltpu

F32 = jnp.float32
BF16 = jnp.bfloat16
N_META = 16
POOL_WINDOWS = (2, 4, 8, 16)
EPS = 1e-6
ADAM_LR, ADAM_B1, ADAM_B2, ADAM_EPS, ADAM_WD, ADAM_STEP = 0.001, 0.9, 0.999, 1e-08, 0.01, 10
LANES = 128
V7X_VMEM_BYTES = 64 * 1024 * 1024
VMEM_LIMIT = V7X_VMEM_BYTES - 8 * 1024 * 1024
MESH = pl.DeviceIdType.MESH
ANY = pl.BlockSpec(memory_space=pl.ANY)
CHIP_FLIPS = ((1, 0), (0, 1), (1, 1))
SMALL_ROWS = 8
TAIL_ROWS = 32


def _pick(n, pref):
    best = None
    for t in range(LANES, min(n, pref) + 1, LANES):
        if n % t == 0:
            best = t
    assert best is not None, (n, pref)
    return best


def _params(n_axes=0):
    sem = ("arbitrary",) * n_axes if n_axes else None
    return pltpu.CompilerParams(dimension_semantics=sem, vmem_limit_bytes=VMEM_LIMIT)


_DIMS = {
    "nn": (((1,), (0,)), ((), ())),
    "nt": (((1,), (1,)), ((), ())),
    "tn": (((0,), (0,)), ((), ())),
}


def _matmul(name, mode, a, b, out_sds, grid, a_spec, b_spec, o_spec, nk, res=None, res_spec=None, acc_shape=None, deps=()):
    out_dtype = out_sds.dtype
    in_place = nk > 1 and out_dtype == F32
    use_scratch = nk > 1 and not in_place
    rows = a_spec.block_shape[-2] if mode != "tn" else None
    chunk = _row_tile(rows, 1, 1, 1152) if rows is not None else None
    n_in = 2 + (res is not None) + len(deps)

    def body(*refs):
        a_ref, b_ref = refs[:2]
        r_ref = refs[2] if res is not None else None
        o_ref, *scr = refs[n_in:]
        k = pl.program_id(len(grid) - 1) if nk > 1 else None

        def emit(sl):
            if sl is None:
                part = lax.dot_general(a_ref[...], b_ref[...], _DIMS[mode], preferred_element_type=F32)
                idx = (slice(None), slice(None))
            else:
                part = lax.dot_general(a_ref[sl, :], b_ref[...], _DIMS[mode], preferred_element_type=F32)
                idx = (sl, slice(None))
            if nk == 1:
                if r_ref is not None:
                    part = part + r_ref[idx]
                o_ref[idx] = part.astype(out_dtype)
                return
            acc = scr[0] if use_scratch else o_ref

            @pl.when(k == 0)
            def _():
                first = part
                if r_ref is not None and in_place:
                    first = first + r_ref[idx]
                acc[idx] = first

            @pl.when(k > 0)
            def _():
                acc[idx] += part

            if use_scratch:

                @pl.when(k == nk - 1)
                def _():
                    o_ref[idx] = acc[idx].astype(out_dtype)

        if mode == "tn" or chunk == rows:
            emit(None)
        else:
            for m0 in range(0, rows, chunk):
                emit(pl.ds(m0, chunk))

    ins = [a, b] + ([res] if res is not None else []) + list(deps)
    in_specs = [a_spec, b_spec] + ([res_spec] if res is not None else []) + [ANY] * len(deps)
    scratch = [pltpu.VMEM(acc_shape, F32)] if use_scratch else []
    return pl.pallas_call(
        body, name=name, out_shape=out_sds, grid=grid, in_specs=in_specs, out_specs=o_spec,
        scratch_shapes=scratch, compiler_params=_params(len(grid)),
    )(*ins)


def _nn_sharded(name, a, w4, nseg):
    lp, kdim = a.shape
    s, _, nloc = w4.shape
    segw = s * nloc // nseg
    tn = _pick(math.gcd(nloc, segw), 1536)
    bw, bo = nloc // tn, segw // tn
    return _matmul(
        name, "nn", a, w4, jax.ShapeDtypeStruct((nseg, lp, segw), BF16), (s * bw,),
        pl.BlockSpec((lp, kdim), lambda j: (0, 0)),
        pl.BlockSpec((None, kdim, tn), lambda j: (j // bw, 0, j % bw)),
        pl.BlockSpec((None, lp, tn), lambda j: (j // bo, 0, j % bo)), 1)


def _nt_in_proj(name, dseg, w4, row_tiles=2, to_pref=1024, deps=()):
    nseg, lp, segw = dseg.shape
    s, kdim, nloc = w4.shape
    assert nseg * segw == s * nloc and 2 * nloc == 3 * segw, (dseg.shape, w4.shape)
    half = segw // 2
    to = _pick(kdim, to_pref)
    tm = lp // row_tiles

    def body(full_ref, half_ref, w_ref, *rest):
        o_ref = rest[len(deps)]
        r = pl.program_id(2)

        def contribution(full_first):
            lo, hi = (pl.ds(0, segw), pl.ds(segw, half)) if full_first else (pl.ds(half, segw), pl.ds(0, half))
            return (lax.dot_general(full_ref[...], w_ref[:, lo], _DIMS["nt"], preferred_element_type=F32)
                    + lax.dot_general(half_ref[...], w_ref[:, hi], _DIMS["nt"], preferred_element_type=F32))

        @pl.when(r == 0)
        def _():
            o_ref[...] = contribution(True)

        for ri in range(1, s):

            @pl.when(r == ri)
            def _(ri=ri):
                o_ref[...] += contribution(ri % 2 == 0)

    return pl.pallas_call(
        body, name=name, out_shape=jax.ShapeDtypeStruct((lp, kdim), F32), grid=(row_tiles, kdim // to, s),
        in_specs=[pl.BlockSpec((None, tm, segw), lambda m, j, r: ((3 * r + 1) // 2, m, 0)),
                  pl.BlockSpec((None, tm, half), lambda m, j, r: (1 + 3 * (r // 2), m, r % 2)),
                  pl.BlockSpec((None, to, nloc), lambda m, j, r: (r, j, 0))] + [ANY] * len(deps),
        out_specs=pl.BlockSpec((tm, to), lambda m, j, r: (m, j)), compiler_params=_params(3),
    )(dseg, dseg, w4, *deps)


def _nn_plain(name, a, w, out_dtype, res=None, tn_pref=512, tk_pref=2048, deps=()):
    lp, kdim = a.shape
    n = w.shape[1]
    tn = _pick(n, tn_pref)
    tk = kdim if kdim <= tk_pref else _pick(kdim, tk_pref)
    nk = kdim // tk
    grid = (n // tn, nk) if nk > 1 else (n // tn,)
    if nk > 1:
        a_spec = pl.BlockSpec((lp, tk), lambda j, k: (0, k))
        w_spec = pl.BlockSpec((tk, tn), lambda j, k: (k, j))
        o_spec = pl.BlockSpec((lp, tn), lambda j, k: (0, j))
    else:
        a_spec = pl.BlockSpec((lp, tk), lambda j: (0, 0))
        w_spec = pl.BlockSpec((tk, tn), lambda j: (0, j))
        o_spec = pl.BlockSpec((lp, tn), lambda j: (0, j))
    return _matmul(name, "nn", a, w, jax.ShapeDtypeStruct((lp, n), out_dtype), grid, a_spec, w_spec, o_spec, nk,
                   res=res, res_spec=o_spec if res is not None else None, acc_shape=(lp, tn), deps=deps)


def _nt_plain(name, a, w, tn_pref=512):
    lp, kdim = a.shape
    n = w.shape[0]
    tn = _pick(n, tn_pref)
    return _matmul(
        name, "nt", a, w, jax.ShapeDtypeStruct((lp, n), BF16), (n // tn,),
        pl.BlockSpec((lp, kdim), lambda j: (0, 0)),
        pl.BlockSpec((tn, kdim), lambda j: (j, 0)),
        pl.BlockSpec((lp, tn), lambda j: (0, j)), 1)


def _nt_sharded(name, dseg, w4, to_pref=1024, tr_pref=1536, row_tiles=1, deps=()):
    nseg, lp, segw = dseg.shape
    s, kdim, nloc = w4.shape
    tr = _pick(math.gcd(nloc, segw), tr_pref)
    ba, bw = segw // tr, nloc // tr
    nr = s * bw
    to = _pick(kdim, to_pref)
    tm = lp // row_tiles
    return _matmul(
        name, "nt", dseg, w4, jax.ShapeDtypeStruct((lp, kdim), F32), (row_tiles, kdim // to, nr),
        pl.BlockSpec((None, tm, tr), lambda m, j, r: (r // ba, m, r % ba)),
        pl.BlockSpec((None, to, tr), lambda m, j, r: (r // bw, j, r % bw)),
        pl.BlockSpec((tm, to), lambda m, j, r: (m, j)), nr, deps=deps)


def _nn_rows(name, a, w, res, row_tiles=2, tn_pref=512):
    lp, kdim = a.shape
    n = w.shape[1]
    tn = _pick(n, tn_pref)
    tm = lp // row_tiles
    blk = pl.BlockSpec((tm, tn), lambda i, j: (i, j))
    return _matmul(name, "nn", a, w, jax.ShapeDtypeStruct((lp, n), F32), (row_tiles, n // tn),
                   pl.BlockSpec((tm, kdim), lambda i, j: (i, 0)), pl.BlockSpec((kdim, tn), lambda i, j: (0, j)), blk, 1,
                   res=res, res_spec=blk)


def _tn_plain(name, a, d, tk_pref=1024):
    lp, kdim = a.shape
    n = d.shape[1]
    tk = _pick(kdim, tk_pref)
    return _matmul(
        name, "tn", a, d, jax.ShapeDtypeStruct((kdim, n), BF16), (kdim // tk,),
        pl.BlockSpec((lp, tk), lambda i: (0, i)),
        pl.BlockSpec((lp, n), lambda i: (0, 0)),
        pl.BlockSpec((tk, n), lambda i: (i, 0)), 1)


def _tn_sharded(name, a, dseg, s, part=(0, 1), tk_pref=1024, deps=()):
    lp, kdim = a.shape
    nseg, _, segw = dseg.shape
    nloc = nseg * segw // s
    tn = _pick(math.gcd(nloc, segw), 1536)
    bd, bo = segw // tn, nloc // tn
    kpart = kdim // part[1]
    tk = _pick(kpart, tk_pref)
    i0 = part[0] * (kpart // tk)

    def body(a_ref, d_ref, *rest):
        o_ref, at_ref = rest[len(deps):]

        @pl.when(pl.program_id(1) == 0)
        def _():
            at_ref[...] = a_ref[...].T

        o_ref[...] = jnp.dot(at_ref[...], d_ref[...], preferred_element_type=F32).astype(BF16)

    return pl.pallas_call(
        body, name=name, out_shape=jax.ShapeDtypeStruct((s, kpart, nloc), BF16), grid=(kpart // tk, s * bo),
        in_specs=[pl.BlockSpec((lp, tk), lambda i, j: (0, i0 + i)),
                  pl.BlockSpec((None, lp, tn), lambda i, j: (j // bd, 0, j % bd))] + [ANY] * len(deps),
        out_specs=pl.BlockSpec((None, tk, tn), lambda i, j: (j // bo, i, j % bo)),
        scratch_shapes=[pltpu.VMEM((tk, lp), BF16)], compiler_params=_params(2),
    )(a, dseg, *deps)


def _silu_parts(gt):
    sg = jax.nn.sigmoid(gt)
    return gt * sg, sg * (1.0 + gt * (1.0 - sg))


def _gate_up_swiglu(name, a, w4, dep, tn_pref=256):
    lp, kdim = a.shape
    s, _, nloc = w4.shape
    f = s * nloc // 2
    tn = _pick(nloc, tn_pref)
    bw = nloc // tn
    chunk = _row_tile(lp, 1, 1, 576)

    def body(a_ref, wg_ref, wu_ref, _, fac_ref, act_ref):
        for m0 in range(0, lp, chunk):
            sl = pl.ds(m0, chunk)
            gt = jnp.dot(a_ref[sl, :], wg_ref[...], preferred_element_type=F32)
            up = jnp.dot(a_ref[sl, :], wu_ref[...], preferred_element_type=F32)
            silu, dsilu = _silu_parts(gt)
            fac_ref[0, sl, :] = (up * dsilu).astype(BF16)
            fac_ref[1, sl, :] = silu.astype(BF16)
            act_ref[sl, :] = (silu * up).astype(BF16)

    return pl.pallas_call(
        body, name=name, grid=(f // tn,),
        out_shape=(jax.ShapeDtypeStruct((2, lp, f), BF16), jax.ShapeDtypeStruct((lp, f), BF16)),
        in_specs=[pl.BlockSpec((lp, kdim), lambda j: (0, 0)),
                  pl.BlockSpec((None, kdim, tn), lambda j: (j // bw, 0, j % bw)),
                  pl.BlockSpec((None, kdim, tn), lambda j: (s // 2 + j // bw, 0, j % bw)), ANY],
        out_specs=(pl.BlockSpec((2, lp, tn), lambda j: (0, 0, j)), pl.BlockSpec((lp, tn), lambda j: (0, j))),
        compiler_params=_params(1),
    )(a, w4, w4, dep)


def _dact_swiglu_bwd(name, d, w, gu, tn_pref=512):
    lp, dm = d.shape
    f = w.shape[0]
    tn = _pick(f, tn_pref)
    chunk = _row_tile(lp, 1, 1, 576)

    def body(d_ref, w_ref, g_ref, u_ref, o_ref):
        for m0 in range(0, lp, chunk):
            sl = pl.ds(m0, chunk)
            dact = lax.dot_general(d_ref[sl, :], w_ref[...], _DIMS["nt"], preferred_element_type=F32)
            o_ref[0, sl, :] = (dact * g_ref[sl, :].astype(F32)).astype(BF16)
            o_ref[1, sl, :] = (dact * u_ref[sl, :].astype(F32)).astype(BF16)

    return pl.pallas_call(
        body, name=name, grid=(f // tn,), out_shape=jax.ShapeDtypeStruct((2, lp, f), BF16),
        in_specs=[pl.BlockSpec((lp, dm), lambda j: (0, 0)), pl.BlockSpec((tn, dm), lambda j: (j, 0)),
                  pl.BlockSpec((None, lp, tn), lambda j: (0, 0, j)), pl.BlockSpec((None, lp, tn), lambda j: (1, 0, j))],
        out_specs=pl.BlockSpec((2, lp, tn), lambda j: (0, 0, j)), compiler_params=_params(1),
    )(d, w, gu, gu)


def _pool_fwd(name, pooled, pw):
    lp, dm = pooled.shape
    g, gw, _ = pw.shape
    return _matmul(
        name, "nn", pooled, pw, jax.ShapeDtypeStruct((lp, dm), BF16), (g,),
        pl.BlockSpec((lp, gw), lambda gi: (0, gi)), pl.BlockSpec((None, gw, gw), lambda gi: (gi, 0, 0)),
        pl.BlockSpec((lp, gw), lambda gi: (0, gi)), 1)


def _pool_bwd_act(name, dya, pw, deps=()):
    lp, dm = dya.shape
    g, gw, _ = pw.shape
    return _matmul(
        name, "nt", dya, pw, jax.ShapeDtypeStruct((lp, dm), BF16), (g,),
        pl.BlockSpec((lp, gw), lambda gi: (0, gi)), pl.BlockSpec((None, gw, gw), lambda gi: (gi, 0, 0)),
        pl.BlockSpec((lp, gw), lambda gi: (0, gi)), 1, deps=deps)


def _pool_bwd_w(name, pooled, dya):
    lp, dm = pooled.shape
    g = len(POOL_WINDOWS)
    gw = dm // g
    return _matmul(
        name, "tn", pooled, dya, jax.ShapeDtypeStruct((g, gw, gw), BF16), (g,),
        pl.BlockSpec((lp, gw), lambda gi: (0, gi)), pl.BlockSpec((lp, gw), lambda gi: (0, gi)),
        pl.BlockSpec((None, gw, gw), lambda gi: (gi, 0, 0)), 1)


def _rms_fwd(name, h, g, tm, deps=()):
    lp, dm = h.shape

    def body(h_ref, g_ref, *rest):
        hv = h_ref[...]
        r = lax.rsqrt(jnp.mean(hv * hv, axis=-1, keepdims=True) + EPS)
        rest[-1][...] = (hv * r * g_ref[...]).astype(BF16)

    row = pl.BlockSpec((tm, dm), lambda i: (i, 0))
    return pl.pallas_call(
        body, name=name, out_shape=jax.ShapeDtypeStruct((lp, dm), BF16), grid=(lp // tm,),
        in_specs=[row, pl.BlockSpec((1, dm), lambda i: (0, 0))] + [ANY] * len(deps), out_specs=row, compiler_params=_params(1),
    )(h, g, *deps)


def _rms_fwd_into(name, src, g, lp, row0, tm, prev=None, deps=()):
    n, dm = src.shape
    b0 = row0 // tm
    n_in = 2 + len(deps)

    def body(s_ref, g_ref, *rest):
        h_ref, o_ref = rest[-2:]
        hv = s_ref[...]
        r = lax.rsqrt(jnp.mean(hv * hv, axis=-1, keepdims=True) + EPS)
        h_ref[...] = hv
        o_ref[...] = (hv * r * g_ref[...]).astype(BF16)

    row = pl.BlockSpec((tm, dm), lambda i: (b0 + i, 0))
    return pl.pallas_call(
        body, name=name, grid=(n // tm,),
        out_shape=(jax.ShapeDtypeStruct((lp, dm), F32), jax.ShapeDtypeStruct((lp, dm), BF16)),
        in_specs=[pl.BlockSpec((tm, dm), lambda i: (i, 0)), pl.BlockSpec((1, dm), lambda i: (0, 0))]
        + [ANY] * (len(deps) + (0 if prev is None else 2)),
        out_specs=(row, row), input_output_aliases={} if prev is None else {n_in: 0, n_in + 1: 1},
        compiler_params=_params(1),
    )(src, g, *deps, *(prev or ()))


def _rms_bwd(name, dy, h, g, dres, tm, dep):
    lp, dm = h.shape

    def body(dy_ref, h_ref, g_ref, dr_ref, _, dh_ref, dhb_ref, dg_ref):
        hv = h_ref[...]
        r = lax.rsqrt(jnp.mean(hv * hv, axis=-1, keepdims=True) + EPS)
        xhat = hv * r
        dyv = dy_ref[...]
        dxh = dyv * g_ref[...]
        dh = dr_ref[...] + r * (dxh - xhat * jnp.mean(dxh * xhat, axis=-1, keepdims=True))
        dh_ref[...] = dh
        dhb_ref[...] = dh.astype(BF16)

        @pl.when(pl.program_id(0) == 0)
        def _():
            dg_ref[...] = jnp.zeros_like(dg_ref)

        dg_ref[0:1, :] += jnp.sum(dyv * xhat, axis=0, keepdims=True)

    row = pl.BlockSpec((tm, dm), lambda i: (i, 0))
    slab = pl.BlockSpec((SMALL_ROWS, dm), lambda i: (0, 0))
    return pl.pallas_call(
        body, name=name, grid=(lp // tm,),
        out_shape=(jax.ShapeDtypeStruct((lp, dm), F32), jax.ShapeDtypeStruct((lp, dm), BF16),
                   jax.ShapeDtypeStruct((SMALL_ROWS, dm), F32)),
        in_specs=[row, row, pl.BlockSpec((1, dm), lambda i: (0, 0)), row, ANY], out_specs=(row, row, slab),
        compiler_params=_params(1),
    )(dy, h, g, dres, dep)


def _rms_bwd_rows(name, dy, h, g, dres, row0, nrows, tm, dep, dg_prev=None):
    dm = h.shape[1]
    b0 = row0 // tm

    def body(dy_ref, h_ref, g_ref, dr_ref, *rest):
        d_ref, dg_ref = rest[-2:]
        hv = h_ref[...]
        r = lax.rsqrt(jnp.mean(hv * hv, axis=-1, keepdims=True) + EPS)
        xhat = hv * r
        dyv = dy_ref[...]
        dxh = dyv * g_ref[...]
        d_ref[...] = dr_ref[...] + r * (dxh - xhat * jnp.mean(dxh * xhat, axis=-1, keepdims=True))

        @pl.when(pl.program_id(0) == 0)
        def _():
            dg_ref[...] = jnp.zeros_like(dg_ref) if dg_prev is None else rest[1][...]

        dg_ref[0:1, :] += jnp.sum(dyv * xhat, axis=0, keepdims=True)

    row = pl.BlockSpec((tm, dm), lambda i: (b0 + i, 0))
    slab = pl.BlockSpec((SMALL_ROWS, dm), lambda i: (0, 0))
    extra = [dep] + ([dg_prev] if dg_prev is not None else [])
    return pl.pallas_call(
        body, name=name, grid=(nrows // tm,),
        out_shape=(jax.ShapeDtypeStruct((nrows, dm), F32), jax.ShapeDtypeStruct((SMALL_ROWS, dm), F32)),
        in_specs=[row, row, pl.BlockSpec((1, dm), lambda i: (0, 0)), row, ANY] + ([slab] if dg_prev is not None else []),
        out_specs=(pl.BlockSpec((tm, dm), lambda i: (i, 0)), slab), compiler_params=_params(1),
    )(dy, h, g, dres, *extra)


def _gate_mix(name, proj, b_gate2, ya, pool_scale, yb, tm):
    _, lp, dm = proj.shape

    def body(ga_ref, gr_ref, b_ref, ya_ref, ps_ref, yb_ref, o_ref):
        g_a = jax.nn.sigmoid(ga_ref[...].astype(F32) + b_ref[0:1, :])
        g_b = jax.nn.sigmoid(gr_ref[...].astype(F32) + b_ref[1:2, :])
        y_a = ya_ref[...].astype(F32) * ps_ref[...]
        o_ref[...] = (g_a * y_a + g_b * yb_ref[...].astype(F32)).astype(BF16)

    row = pl.BlockSpec((tm, dm), lambda i: (i, 0))
    return pl.pallas_call(
        body, name=name, out_shape=jax.ShapeDtypeStruct((lp, dm), BF16), grid=(lp // tm,),
        in_specs=[pl.BlockSpec((None, tm, dm), lambda i: (4, i, 0)), pl.BlockSpec((None, tm, dm), lambda i: (5, i, 0)),
                  pl.BlockSpec((2, dm), lambda i: (0, 0)), row, pl.BlockSpec((1, dm), lambda i: (0, 0)), row],
        out_specs=row, compiler_params=_params(1),
    )(proj, proj, b_gate2, ya, pool_scale, yb)


def _gate_bwd(name, dmix, proj, b_gate2, ya, pool_scale, yb, tm):
    _, lp, dm = proj.shape

    def body(dm_ref, ga_ref, gr_ref, b_ref, ya_ref, ps_ref, yb_ref, dp_ref, dyb_ref, dya_ref, db_ref, dps_ref):
        dmx = dm_ref[...].astype(F32)
        g_a = jax.nn.sigmoid(ga_ref[...].astype(F32) + b_ref[0:1, :])
        g_b = jax.nn.sigmoid(gr_ref[...].astype(F32) + b_ref[1:2, :])
        ya_pre = ya_ref[...].astype(F32)
        ybv = yb_ref[...].astype(F32)
        ps = ps_ref[...]
        dga = dmx * (ya_pre * ps) * (g_a * (1.0 - g_a))
        dgr = dmx * ybv * (g_b * (1.0 - g_b))
        dp_ref[0] = dga.astype(BF16)
        dp_ref[1] = dgr.astype(BF16)
        dyb_ref[...] = (dmx * g_b).astype(BF16)
        dya_ref[...] = (dmx * g_a * ps).astype(BF16)

        @pl.when(pl.program_id(0) == 0)
        def _():
            db_ref[...] = jnp.zeros_like(db_ref)
            dps_ref[...] = jnp.zeros_like(dps_ref)

        db_ref[0:1, :] += jnp.sum(dga, axis=0, keepdims=True)
        db_ref[1:2, :] += jnp.sum(dgr, axis=0, keepdims=True)
        dps_ref[0:1, :] += jnp.sum(dmx * g_a * ya_pre, axis=0, keepdims=True)

    row = pl.BlockSpec((tm, dm), lambda i: (i, 0))
    one = pl.BlockSpec((1, dm), lambda i: (0, 0))
    slab = pl.BlockSpec((SMALL_ROWS, dm), lambda i: (0, 0))
    return pl.pallas_call(
        body, name=name, grid=(lp // tm,),
        out_shape=(jax.ShapeDtypeStruct((6, lp, dm), BF16), jax.ShapeDtypeStruct((lp, dm), BF16),
                   jax.ShapeDtypeStruct((lp, dm), BF16), jax.ShapeDtypeStruct((SMALL_ROWS, dm), F32),
                   jax.ShapeDtypeStruct((SMALL_ROWS, dm), F32)),
        in_specs=[row, pl.BlockSpec((None, tm, dm), lambda i: (4, i, 0)), pl.BlockSpec((None, tm, dm), lambda i: (5, i, 0)),
                  pl.BlockSpec((2, dm), lambda i: (0, 0)), row, one, row],
        out_specs=(pl.BlockSpec((2, tm, dm), lambda i: (2, i, 0)), row, row, slab, slab),
        compiler_params=_params(1),
    )(dmix, proj, proj, b_gate2, ya, pool_scale, yb)


def _final_loss(name, h2, g3, target, tm):
    lp, dm = h2.shape
    seq = target.shape[0]

    def body(h_ref, g_ref, t_ref, dh_ref, dhb_ref, ls_ref, dg_ref):
        @pl.when(pl.program_id(0) == 0)
        def _():
            ls_ref[...] = jnp.zeros_like(ls_ref)
            dg_ref[...] = jnp.zeros_like(dg_ref)

        hv = h_ref[...]
        gv = g_ref[...]
        r = lax.rsqrt(jnp.mean(hv * hv, axis=-1, keepdims=True) + EPS)
        xhat = hv * r
        err = xhat * gv - t_ref[...]
        dout = err * (1.0 / dm)
        dxh = dout * gv
        dh = r * (dxh - xhat * jnp.mean(dxh * xhat, axis=-1, keepdims=True))
        dh_ref[...] = dh
        dhb_ref[...] = dh.astype(BF16)
        ls_ref[0:1, :] += jnp.sum(err * err, axis=0, keepdims=True)
        dg_ref[0:1, :] += jnp.sum(dout * xhat, axis=0, keepdims=True)

    row = pl.BlockSpec((tm, dm), lambda i: (i, 0))
    slab = pl.BlockSpec((SMALL_ROWS, dm), lambda i: (0, 0))
    return pl.pallas_call(
        body, name=name, grid=(seq // tm,),
        out_shape=(jax.ShapeDtypeStruct((lp, dm), F32), jax.ShapeDtypeStruct((lp, dm), BF16),
                   jax.ShapeDtypeStruct((SMALL_ROWS, dm), F32), jax.ShapeDtypeStruct((SMALL_ROWS, dm), F32)),
        in_specs=[row, pl.BlockSpec((1, dm), lambda i: (0, 0)), row],
        out_specs=(row, row, slab, slab), compiler_params=_params(1),
    )(h2, g3, target)


def _zero_tail(name, arrays, tail):
    n = len(arrays)
    lp, dm = arrays[0].shape
    last = lp // tail - 1

    def body(*refs):
        for o_ref in refs[n:]:
            o_ref[...] = jnp.zeros_like(o_ref)

    return pl.pallas_call(
        body, name=name, grid=(1,), out_shape=tuple(jax.ShapeDtypeStruct(a.shape, a.dtype) for a in arrays),
        in_specs=[ANY] * n, out_specs=tuple(pl.BlockSpec((tail, dm), lambda i: (last, 0)) for _ in arrays),
        input_output_aliases={a: a for a in range(n)}, compiler_params=_params(1),
    )(*arrays)


def _shift(v, k):
    return pltpu.roll(v, k % v.shape[0], axis=0)


def _window_sum(v, group, sign):
    s2 = v + _shift(v, sign * 1)
    s4 = s2 + _shift(s2, sign * 2)
    s8 = s4 + _shift(s4, sign * 4)
    s16 = s8 + _shift(s8, sign * 8)
    return jnp.where(group == 0, s2, jnp.where(group == 1, s4, jnp.where(group == 2, s8, s16)))


def _pool_count(lp, group):
    row = lax.broadcasted_iota(jnp.int32, (lp, 1), 0)
    window = jnp.left_shift(2, group).astype(F32)
    meta_pos = (row - (lp - N_META) + 1).astype(F32)
    return jnp.where(row >= lp - N_META, jnp.minimum(meta_pos, window), window)


def _mixer_fwd(name, proj, conv_w, tc, dep):
    _, lp, dm = proj.shape
    per_group = dm // len(POOL_WINDOWS) // tc

    def body(u_ref, gb_ref, gc_ref, v_ref, cw_ref, _, p_ref, z_ref):
        group = pl.program_id(0) // per_group
        u = u_ref[...].astype(F32)
        p_ref[...] = (_window_sum(u, group, 1) / _pool_count(lp, group) - u).astype(BF16)
        cv = gc_ref[...].astype(F32) * v_ref[...].astype(F32)
        conv = cw_ref[0:1, :] * _shift(cv, 2) + cw_ref[1:2, :] * _shift(cv, 1) + cw_ref[2:3, :] * cv
        z_ref[...] = (gb_ref[...].astype(F32) * conv).astype(BF16)

    def seg(s):
        return pl.BlockSpec((None, lp, tc), lambda j: (s, 0, j))

    col = pl.BlockSpec((lp, tc), lambda j: (0, j))
    return pl.pallas_call(
        body, name=name, grid=(dm // tc,),
        out_shape=(jax.ShapeDtypeStruct((lp, dm), BF16), jax.ShapeDtypeStruct((lp, dm), BF16)),
        in_specs=[seg(0), seg(1), seg(2), seg(3), pl.BlockSpec((3, tc), lambda j: (0, j)), ANY],
        out_specs=(col, col), compiler_params=_params(1),
    )(proj, proj, proj, proj, conv_w, dep)


def _mixer_bwd(name, dz, dpooled, proj, conv_w, dproj, tc, dep):
    _, lp, dm = proj.shape
    per_group = dm // len(POOL_WINDOWS) // tc

    def body(dz_ref, dp_ref, gb_ref, gc_ref, v_ref, cw_ref, _, __, o_ref, dcw_ref):
        group = pl.program_id(0) // per_group
        dzv = dz_ref[...].astype(F32)
        gb = gb_ref[...].astype(F32)
        gc = gc_ref[...].astype(F32)
        vv = v_ref[...].astype(F32)
        cv = gc * vv
        c1 = _shift(cv, 1)
        c2 = _shift(cv, 2)
        w0, w1, w2 = cw_ref[0:1, :], cw_ref[1:2, :], cw_ref[2:3, :]
        o_ref[1] = (dzv * (w0 * c2 + w1 * c1 + w2 * cv)).astype(BF16)
        dconv = dzv * gb
        dcw_ref[...] = jnp.zeros_like(dcw_ref)
        dcw_ref[0:1, :] = jnp.sum(dconv * c2, axis=0, keepdims=True)
        dcw_ref[1:2, :] = jnp.sum(dconv * c1, axis=0, keepdims=True)
        dcw_ref[2:3, :] = jnp.sum(dconv * cv, axis=0, keepdims=True)
        dcv = w0 * _shift(dconv, -2) + w1 * _shift(dconv, -1) + w2 * dconv
        o_ref[2] = (dcv * vv).astype(BF16)
        o_ref[3] = (dcv * gc).astype(BF16)
        dpv = dp_ref[...].astype(F32)
        o_ref[0] = (_window_sum(dpv / _pool_count(lp, group), group, -1) - dpv).astype(BF16)

    def seg(s):
        return pl.BlockSpec((None, lp, tc), lambda j: (s, 0, j))

    col = pl.BlockSpec((lp, tc), lambda j: (0, j))
    return pl.pallas_call(
        body, name=name, grid=(dm // tc,),
        out_shape=(jax.ShapeDtypeStruct(dproj.shape, BF16), jax.ShapeDtypeStruct((SMALL_ROWS, dm), F32)),
        in_specs=[col, col, seg(1), seg(2), seg(3), pl.BlockSpec((3, tc), lambda j: (0, j)), ANY, ANY],
        out_specs=(pl.BlockSpec((4, lp, tc), lambda j: (0, 0, j)), pl.BlockSpec((SMALL_ROWS, tc), lambda j: (0, j))),
        input_output_aliases={6: 0}, compiler_params=_params(1),
    )(dz, dpooled, proj, proj, proj, conv_w, dproj, dep)


def _row_tile(r, c, bytes_per_row_elem=4, budget=2 * 1024 * 1024):
    best = None
    for t in range(16, r + 1, 16):
        if r % t == 0 and t * c * bytes_per_row_elem <= budget:
            best = t
    return best if best is not None else r


def _pair_add(name, g4, recv, core):
    s, r, c = g4.shape
    h = r // 2
    tr = _row_tile(h, c, budget=6 * 1024 * 1024)
    nb = h // tr

    def body(core_ref, g_ref, r_ref, o_ref):
        o_ref[...] = (g_ref[...].astype(F32) + r_ref[...].astype(F32)).astype(BF16)

    grid_spec = pltpu.PrefetchScalarGridSpec(
        num_scalar_prefetch=1, grid=(s, nb),
        in_specs=[pl.BlockSpec((None, tr, c), lambda si, j, core_ref: (si, core_ref[0] * nb + j, 0)),
                  pl.BlockSpec((None, tr, c), lambda si, j, core_ref: (si, j, 0))],
        out_specs=pl.BlockSpec((None, tr, c), lambda si, j, core_ref: (si, j, 0)))
    return pl.pallas_call(
        body, name=name, out_shape=jax.ShapeDtypeStruct((s, h, c), BF16), grid_spec=grid_spec,
        compiler_params=_params(2),
    )(core, g4, recv)


def _chip_sum(name, parts, recv, chip):
    _, h, c = parts.shape
    tr = _row_tile(h, c)

    def body(chip_ref, p_ref, r_ref, o_ref):
        acc = p_ref[...].astype(F32)
        for i in range(len(CHIP_FLIPS)):
            acc = acc + r_ref[i].astype(F32)
        o_ref[...] = acc

    grid_spec = pltpu.PrefetchScalarGridSpec(
        num_scalar_prefetch=1, grid=(h // tr,),
        in_specs=[pl.BlockSpec((None, tr, c), lambda j, chip_ref: (chip_ref[0], j, 0)),
                  pl.BlockSpec((len(CHIP_FLIPS), tr, c), lambda j, chip_ref: (0, j, 0))],
        out_specs=pl.BlockSpec((tr, c), lambda j, chip_ref: (j, 0)))
    return pl.pallas_call(
        body, name=name, out_shape=jax.ShapeDtypeStruct((h, c), F32), grid_spec=grid_spec, compiler_params=_params(1),
    )(chip, parts, recv)


def _adam_update(w, gv, m, v):
    c1 = 1.0 - ADAM_B1 ** ADAM_STEP
    c2 = 1.0 - ADAM_B2 ** ADAM_STEP
    nm = ADAM_B1 * m + (1.0 - ADAM_B1) * gv
    nv = ADAM_B2 * v + (1.0 - ADAM_B2) * (gv * gv)
    return -ADAM_LR * ((nm / c1) / (jnp.sqrt(nv / c2) + ADAM_EPS) + ADAM_WD * w), nm, nv


def _adamw_halves(name, w, g_own, g_sib, m, v, core, part=(0, 1), prev=None):
    r, c = w.shape
    rp = r // part[1]
    h = rp // 2
    tr = _row_tile(h, c, budget=2 * 1024 * 1024)
    nbh = h // tr
    j0 = part[0] * 2 * nbh
    n_prev = 0 if prev is None else 4

    def body(core_ref, w_ref, go_ref, gs_ref, m_ref, v_ref, *rest):
        g_ref, d_ref, nm_ref, nv_ref = rest[n_prev:]
        mine = (pl.program_id(0) // nbh) == core_ref[0]
        gv = jnp.where(mine, go_ref[...], gs_ref[...])
        g_ref[...] = gv
        d_ref[...], nm_ref[...], nv_ref[...] = _adam_update(w_ref[...], gv, m_ref[...], v_ref[...])

    def blk(fn):
        return pl.BlockSpec((tr, c), fn)

    full = blk(lambda j, core_ref: (j0 + j, 0))
    own = blk(lambda j, core_ref: (jnp.clip(j - core_ref[0] * nbh, 0, nbh - 1), 0))
    sib = blk(lambda j, core_ref: (jnp.clip(j - (1 - core_ref[0]) * nbh, 0, nbh - 1), 0))
    grid_spec = pltpu.PrefetchScalarGridSpec(
        num_scalar_prefetch=1, grid=(2 * nbh,), in_specs=[full, own, sib, full, full] + [ANY] * n_prev, out_specs=(full,) * 4)
    sds = jax.ShapeDtypeStruct((r, c), F32)
    return pl.pallas_call(
        body, name=name, out_shape=(sds,) * 4, grid_spec=grid_spec, compiler_params=_params(1),
        input_output_aliases={6 + i: i for i in range(n_prev)},
    )(core, w, g_own, g_sib, m, v, *(prev or ()))


def _adamw(name, w, g, m, v):
    r, c = w.shape

    def body(w_ref, g_ref, m_ref, v_ref, d_ref, nm_ref, nv_ref):
        d_ref[...], nm_ref[...], nv_ref[...] = _adam_update(w_ref[...], g_ref[...], m_ref[...], v_ref[...])

    blk = pl.BlockSpec((r, c), lambda j: (0, 0))
    sds = jax.ShapeDtypeStruct((r, c), F32)
    return pl.pallas_call(
        body, name=name, out_shape=(sds, sds, sds), grid=(1,), in_specs=[blk] * 4, out_specs=(blk,) * 3,
        compiler_params=_params(1),
    )(w, g, m, v)


def _cast_into_slot(name, w, chip, dtype, deps=()):
    r, c = w.shape
    tr = _row_tile(r, c)

    def body(chip_ref, w_ref, *rest):
        rest[-1][...] = w_ref[...].astype(dtype)

    grid_spec = pltpu.PrefetchScalarGridSpec(
        num_scalar_prefetch=1, grid=(r // tr,),
        in_specs=[pl.BlockSpec((tr, c), lambda j, chip_ref: (j, 0))] + [ANY] * len(deps),
        out_specs=pl.BlockSpec((None, tr, c), lambda j, chip_ref: (chip_ref[0], j, 0)))
    return pl.pallas_call(
        body, name=name, out_shape=jax.ShapeDtypeStruct((4, r, c), dtype), grid_spec=grid_spec, compiler_params=_params(1),
    )(chip, w, *deps)


def _cast_half_into_slot(name, w, chip_half, dtype, into=None):
    r, c = w.shape
    h = r // 2
    tr = _row_tile(h, c)
    nb = h // tr
    n_prev = 0 if into is None else 1

    def body(ids_ref, w_ref, *rest):
        rest[-1][...] = w_ref[...].astype(dtype)

    grid_spec = pltpu.PrefetchScalarGridSpec(
        num_scalar_prefetch=1, grid=(nb,),
        in_specs=[pl.BlockSpec((tr, c), lambda j, ids_ref: (ids_ref[1] * nb + j, 0))] + [ANY] * n_prev,
        out_specs=pl.BlockSpec((None, tr, c), lambda j, ids_ref: (ids_ref[0], ids_ref[1] * nb + j, 0)))
    return pl.pallas_call(
        body, name=name, out_shape=jax.ShapeDtypeStruct((4, r, c), dtype), grid_spec=grid_spec, compiler_params=_params(1),
        input_output_aliases={2: 0} if into is not None else {},
    )(chip_half, w, *([into] if into is not None else []))


def _place():
    return lax.axis_index("x"), lax.axis_index("y"), lax.axis_index("c")


def _chip_of(x, y, flip):
    px, py = x ^ flip[0], y ^ flip[1]
    return px, py, 2 * px + py


def _half(ref, which):
    rows = ref.shape[0] // 2
    return ref.at[pl.ds(which * rows, rows)]


HBM = pl.BlockSpec(memory_space=pltpu.HBM)
SEM = pl.BlockSpec(memory_space=pltpu.SEMAPHORE)
SPLIT_COPY = pltpu.CompilerParams(has_side_effects=pltpu.SideEffectType.DATAFLOW_SIDE_EFFECTING)


def _in_hbm(arrays):
    return [pltpu.with_memory_space_constraint(t, pltpu.HBM) for t in arrays]


TOKEN = jax.ShapeDtypeStruct((SMALL_ROWS, LANES), F32)
TOKEN_SPEC = pl.BlockSpec(memory_space=pltpu.VMEM)


NEIGHBOUR_FLIPS = CHIP_FLIPS[:2]


def _relay_chips(x, y, c):
    fx, fy = x ^ c, y ^ (1 - c)
    return (fx, fy), 2 * fx + fy, 2 * (1 - x) + (1 - y)


def _ag_start(name, slabs, deps=()):
    n = len(slabs)
    nn = len(NEIGHBOUR_FLIPS)

    def body(*refs):
        no = n + len(deps)
        ssem, rsem = refs[no], refs[no + 1]
        outs = refs[no + 2:no + 2 + n]
        token = refs[no + 2 + n]
        token[...] = jnp.zeros_like(token)
        x, y, c = _place()
        k = 2 * x + y
        for a in range(n):
            for j, flip in enumerate(NEIGHBOUR_FLIPS):
                px, py, _ = _chip_of(x, y, flip)
                mine = _half(outs[a].at[k], c)
                pltpu.make_async_remote_copy(src_ref=mine, dst_ref=mine, send_sem=ssem.at[a * nn + j],
                                             recv_sem=rsem.at[a * nn + j], device_id=(px, py, c), device_id_type=MESH).start()

    sem = pltpu.SemaphoreType.DMA((nn * n,))
    res = pl.pallas_call(
        body, name=name, out_shape=(sem, sem) + tuple(pltpu.HBM(t.shape, t.dtype) for t in slabs) + (TOKEN,),
        in_specs=[HBM] * n + [ANY] * len(deps), out_specs=tuple([SEM, SEM] + [HBM] * n + [TOKEN_SPEC]),
        input_output_aliases={a: 2 + a for a in range(n)}, compiler_params=SPLIT_COPY,
    )(*_in_hbm(slabs), *deps)
    return (res[0], res[1]), list(res[2:2 + n]), res[2 + n]


def _ag_relay(name, slabs, sems, after, then_start=()):
    n = len(slabs)
    m = len(then_start)
    nn = len(NEIGHBOUR_FLIPS)

    def body(*refs):
        no = n + 2 + m + len(after)
        ins = refs[:n]
        ssem, rsem = refs[n], refs[n + 1]
        r_s, r_r, p_s, p_r = refs[no:no + 4]
        x, y, c = _place()
        k = 2 * x + y
        (fx, fy), _, _ = _relay_chips(x, y, c)
        for a in range(n):
            for j, flip in enumerate(NEIGHBOUR_FLIPS):
                _, _, kj = _chip_of(x, y, flip)
                landed = _half(ins[a].at[kj], c)
                cp = pltpu.make_async_remote_copy(
                    src_ref=_half(ins[a].at[k], c), dst_ref=landed, send_sem=ssem.at[a * nn + j],
                    recv_sem=rsem.at[a * nn + j], device_id=(x, y, c), device_id_type=MESH)
                cp.wait_send()
                cp.wait_recv()
        for a in range(n):
            near = _half(ins[a].at[2 * (x ^ (1 - c)) + (y ^ c)], c)
            pltpu.make_async_remote_copy(src_ref=near, dst_ref=near, send_sem=r_s.at[a], recv_sem=r_r.at[a],
                                         device_id=(fx, fy, c), device_id_type=MESH).start()
            for j, flip in enumerate(NEIGHBOUR_FLIPS):
                _, _, kj = _chip_of(x, y, flip)
                landed = _half(ins[a].at[kj], c)
                pltpu.make_async_remote_copy(src_ref=landed, dst_ref=landed, send_sem=p_s.at[a * nn + j],
                                             recv_sem=p_r.at[a * nn + j], device_id=(x, y, 1 - c), device_id_type=MESH).start()
        if m:
            d_s, d_r = refs[no + 4 + n], refs[no + 5 + n]
            nxt = refs[no + 6 + n:]
            for a in range(m):
                for j, flip in enumerate(NEIGHBOUR_FLIPS):
                    px, py, _ = _chip_of(x, y, flip)
                    mine = _half(nxt[a].at[k], c)
                    pltpu.make_async_remote_copy(src_ref=mine, dst_ref=mine, send_sem=d_s.at[a * nn + j],
                                                 recv_sem=d_r.at[a * nn + j], device_id=(px, py, c), device_id_type=MESH).start()

    rsem_t = pltpu.SemaphoreType.DMA((n,))
    psem_t = pltpu.SemaphoreType.DMA((nn * n,))
    out_shape = (rsem_t, rsem_t, psem_t, psem_t) + tuple(pltpu.HBM(t.shape, t.dtype) for t in slabs)
    out_specs = [SEM] * 4 + [HBM] * n
    aliases = {a: 4 + a for a in range(n)}
    if m:
        dsem_t = pltpu.SemaphoreType.DMA((nn * m,))
        out_shape += (dsem_t, dsem_t) + tuple(pltpu.HBM(t.shape, t.dtype) for t in then_start)
        out_specs += [SEM, SEM] + [HBM] * m
        aliases.update({n + 2 + a: 4 + n + 2 + a for a in range(m)})
    res = pl.pallas_call(
        body, name=name, out_shape=out_shape, in_specs=[HBM] * n + [SEM, SEM] + [HBM] * m + [ANY] * len(after),
        out_specs=tuple(out_specs), input_output_aliases=aliases, compiler_params=SPLIT_COPY,
    )(*slabs, sems[0], sems[1], *_in_hbm(list(then_start)), *after)
    if not m:
        return tuple(res[:4]), list(res[4:])
    return (tuple(res[:4]), list(res[4:4 + n])), ((res[4 + n], res[5 + n]), list(res[6 + n:]))


def _wait_passes(ins, p_s, p_r, x, y, c):
    nn = len(NEIGHBOUR_FLIPS)
    for a in range(len(ins)):
        for j, flip in enumerate(NEIGHBOUR_FLIPS):
            _, _, kj = _chip_of(x, y, flip)
            cp = pltpu.make_async_remote_copy(
                src_ref=_half(ins[a].at[kj], c), dst_ref=_half(ins[a].at[kj], 1 - c), send_sem=p_s.at[a * nn + j],
                recv_sem=p_r.at[a * nn + j], device_id=(x, y, c), device_id_type=MESH)
            cp.wait_send()
            cp.wait_recv()


def _ag_relay_wait(name, slabs, sems, after):
    n = len(slabs)
    ns = len(sems)

    def body(*refs):
        no = n + ns + len(after)
        ins = refs[:n]
        r_s, r_r = refs[n], refs[n + 1]
        f_s, f_r = refs[no], refs[no + 1]
        x, y, c = _place()
        _, _, kd = _relay_chips(x, y, c)
        for a in range(n):
            near = _half(ins[a].at[2 * (x ^ (1 - c)) + (y ^ c)], c)
            cp = pltpu.make_async_remote_copy(src_ref=near, dst_ref=_half(ins[a].at[kd], c), send_sem=r_s.at[a],
                                              recv_sem=r_r.at[a], device_id=(x, y, c), device_id_type=MESH)
            cp.wait_send()
            cp.wait_recv()
        if ns == 4:
            _wait_passes(ins, refs[n + 2], refs[n + 3], x, y, c)
        for a in range(n):
            diag = _half(ins[a].at[kd], c)
            pltpu.make_async_remote_copy(src_ref=diag, dst_ref=diag, send_sem=f_s.at[a], recv_sem=f_r.at[a],
                                         device_id=(x, y, 1 - c), device_id_type=MESH).start()

    sem = pltpu.SemaphoreType.DMA((n,))
    res = pl.pallas_call(
        body, name=name, out_shape=(sem, sem) + tuple(pltpu.HBM(t.shape, t.dtype) for t in slabs),
        in_specs=[HBM] * n + [SEM] * ns + [ANY] * len(after), out_specs=tuple([SEM, SEM] + [HBM] * n),
        input_output_aliases={a: 2 + a for a in range(n)}, compiler_params=SPLIT_COPY,
    )(*slabs, *sems, *after)
    return (res[0], res[1]), list(res[2:])


def _ag_final_wait(name, slabs, sems, after, first=0):
    n = len(slabs)

    def body(*refs):
        ins = refs[:n]
        f_s, f_r = refs[n], refs[n + 1]
        x, y, c = _place()
        _, _, kd = _relay_chips(x, y, c)
        for a in range(n):
            cp = pltpu.make_async_remote_copy(
                src_ref=_half(ins[a].at[kd], c), dst_ref=_half(ins[a].at[kd], 1 - c), send_sem=f_s.at[first + a],
                recv_sem=f_r.at[first + a], device_id=(x, y, c), device_id_type=MESH)
            cp.wait_send()
            cp.wait_recv()

    return pl.pallas_call(
        body, name=name, out_shape=tuple(pltpu.HBM(t.shape, t.dtype) for t in slabs),
        in_specs=[HBM] * n + [SEM, SEM] + [ANY] * len(after), out_specs=tuple([HBM] * n),
        input_output_aliases={a: a for a in range(n)}, compiler_params=SPLIT_COPY,
    )(*slabs, sems[0], sems[1], *after)


def _sibling_part(ref, c, halves):
    if not halves:
        return ref
    h = ref.shape[1] // 2
    return ref.at[:, pl.ds((1 - c) * h, h)]


def _swap_start(name, grads, halves=True, deps=()):
    n = len(grads)

    def body(*refs):
        no = 2 * n + len(deps)
        ssem, rsem = refs[no], refs[no + 1]
        src, land = refs[no + 2:no + n + 2], refs[no + n + 2:no + 2 * n + 2]
        token = refs[no + 2 * n + 2]
        token[...] = jnp.zeros_like(token)
        x, y, c = _place()
        for a in range(n):
            pltpu.make_async_remote_copy(
                src_ref=_sibling_part(src[a], c, halves), dst_ref=land[a], send_sem=ssem.at[a], recv_sem=rsem.at[a],
                device_id=(x, y, 1 - c), device_id_type=MESH).start()

    zones = [lax.empty((g.shape[0], g.shape[1] // 2, g.shape[2]) if halves else g.shape, g.dtype) for g in grads]
    sem = pltpu.SemaphoreType.DMA((n,))
    res = pl.pallas_call(
        body, name=name,
        out_shape=(sem, sem) + tuple(pltpu.HBM(t.shape, t.dtype) for t in list(grads) + zones) + (TOKEN,),
        in_specs=[HBM] * (2 * n) + [ANY] * len(deps), out_specs=tuple([SEM, SEM] + [HBM] * (2 * n) + [TOKEN_SPEC]),
        input_output_aliases={i: 2 + i for i in range(2 * n)}, compiler_params=SPLIT_COPY,
    )(*_in_hbm(list(grads) + zones), *deps)
    return (res[0], res[1], list(res[2:2 + n]), list(res[2 + n:2 + 2 * n])), res[2 + 2 * n]


def _swap_wait(name, ssem, rsem, grads, zones, after, halves=True):
    n = len(grads)

    def body(*refs):
        src, land = refs[:n], refs[n:2 * n]
        ss, rs = refs[2 * n], refs[2 * n + 1]
        x, y, c = _place()
        for a in range(n):
            cp = pltpu.make_async_remote_copy(
                src_ref=_sibling_part(src[a], c, halves), dst_ref=land[a], send_sem=ss.at[a], recv_sem=rs.at[a],
                device_id=(x, y, c), device_id_type=MESH)
            cp.wait_send()
            cp.wait_recv()

    res = pl.pallas_call(
        body, name=name, out_shape=tuple(pltpu.HBM(t.shape, t.dtype) for t in list(grads) + list(zones)),
        in_specs=[HBM] * (2 * n) + [SEM, SEM] + [ANY] * len(after), out_specs=tuple([HBM] * (2 * n)),
        input_output_aliases={i: i for i in range(2 * n)}, compiler_params=SPLIT_COPY,
    )(*grads, *zones, ssem, rsem, *after)
    return list(res[:n]), list(res[n:])


def _scatter_start(name, parts):
    n = len(parts)
    nf = len(CHIP_FLIPS)

    def body(*refs):
        ssem, rsem = refs[2 * n], refs[2 * n + 1]
        src, land = refs[2 * n + 2:3 * n + 2], refs[3 * n + 2:4 * n + 2]
        token = refs[4 * n + 2]
        token[...] = jnp.zeros_like(token)
        x, y, c = _place()
        for a in range(n):
            for j, flip in enumerate(CHIP_FLIPS):
                px, py, kj = _chip_of(x, y, flip)
                pltpu.make_async_remote_copy(
                    src_ref=src[a].at[kj], dst_ref=land[a].at[j], send_sem=ssem.at[a * nf + j], recv_sem=rsem.at[a * nf + j],
                    device_id=(px, py, c), device_id_type=MESH).start()

    zones = [lax.empty((nf,) + p.shape[1:], p.dtype) for p in parts]
    sem = pltpu.SemaphoreType.DMA((nf * n,))
    res = pl.pallas_call(
        body, name=name,
        out_shape=(sem, sem) + tuple(pltpu.HBM(t.shape, t.dtype) for t in list(parts) + zones)
        + (jax.ShapeDtypeStruct((SMALL_ROWS, LANES), F32),),
        in_specs=[HBM] * (2 * n),
        out_specs=tuple([SEM, SEM] + [HBM] * (2 * n) + [pl.BlockSpec(memory_space=pltpu.VMEM)]),
        input_output_aliases={i: 2 + i for i in range(2 * n)}, compiler_params=SPLIT_COPY,
    )(*_in_hbm(list(parts) + zones))
    return (res[0], res[1], list(res[2:2 + n]), list(res[2 + n:2 + 2 * n])), res[2 + 2 * n]


def _scatter_wait(name, ssem, rsem, parts, zones, after):
    n = len(parts)
    nf = len(CHIP_FLIPS)

    def body(*refs):
        src, land = refs[:n], refs[n:2 * n]
        ss, rs = refs[2 * n], refs[2 * n + 1]
        x, y, c = _place()
        for a in range(n):
            for j, flip in enumerate(CHIP_FLIPS):
                _, _, kj = _chip_of(x, y, flip)
                cp = pltpu.make_async_remote_copy(
                    src_ref=src[a].at[kj], dst_ref=land[a].at[j], send_sem=ss.at[a * nf + j], recv_sem=rs.at[a * nf + j],
                    device_id=(x, y, c), device_id_type=MESH)
                cp.wait_send()
                cp.wait_recv()

    res = pl.pallas_call(
        body, name=name, out_shape=tuple(pltpu.HBM(t.shape, t.dtype) for t in list(parts) + list(zones)),
        in_specs=[HBM] * (2 * n) + [SEM, SEM] + [ANY] * len(after), out_specs=tuple([HBM] * (2 * n)),
        input_output_aliases={i: i for i in range(2 * n)}, compiler_params=SPLIT_COPY,
    )(*parts, *zones, ssem, rsem, *after)
    return list(res[:n]), list(res[n:])


N_PEERS = 7


def _peer(x, y, c, mask):
    px, py, pc = x ^ ((mask >> 2) & 1), y ^ ((mask >> 1) & 1), c ^ (mask & 1)
    return (px, py, pc), 4 * px + 2 * py + pc


def _reduce_start(vec, deps):
    nd = len(deps)

    def body(*refs):
        ssem, rsem, src, land, token = refs[2 + nd:]
        token[...] = jnp.zeros_like(token)
        x, y, c = _place()
        me = 4 * x + 2 * y + c
        for mask in range(1, N_PEERS + 1):
            to, _ = _peer(x, y, c, mask)
            pltpu.make_async_remote_copy(src_ref=src, dst_ref=land.at[me], send_sem=ssem.at[mask - 1],
                                         recv_sem=rsem.at[mask - 1], device_id=to, device_id_type=MESH).start()

    zone = lax.empty((N_PEERS + 1,) + vec.shape, vec.dtype)
    sem = pltpu.SemaphoreType.DMA((N_PEERS,))
    res = pl.pallas_call(
        body, name="reduce_start",
        out_shape=(sem, sem, pltpu.HBM(vec.shape, vec.dtype), pltpu.HBM(zone.shape, zone.dtype), TOKEN),
        in_specs=[HBM, HBM] + [ANY] * nd, out_specs=(SEM, SEM, HBM, HBM, TOKEN_SPEC),
        input_output_aliases={0: 2, 1: 3}, compiler_params=SPLIT_COPY,
    )(*_in_hbm([vec, zone]), *deps)
    return res[:4], res[4]


def _reduce_wait(ssem, rsem, vec, zone, after):
    def body(src, land, ss, rs, *_):
        x, y, c = _place()
        for mask in range(1, N_PEERS + 1):
            _, frm = _peer(x, y, c, mask)
            cp = pltpu.make_async_remote_copy(src_ref=src, dst_ref=land.at[frm], send_sem=ss.at[mask - 1],
                                              recv_sem=rs.at[mask - 1], device_id=(x, y, c), device_id_type=MESH)
            cp.wait_send()
            cp.wait_recv()

    return pl.pallas_call(
        body, name="reduce_wait", out_shape=(pltpu.HBM(vec.shape, vec.dtype), pltpu.HBM(zone.shape, zone.dtype)),
        in_specs=[HBM, HBM, SEM, SEM] + [ANY] * len(after), out_specs=(HBM, HBM),
        input_output_aliases={0: 0, 1: 1}, compiler_params=SPLIT_COPY,
    )(vec, zone, ssem, rsem, *after)


def _reduce_sum(vec, zone, me, loss_row, loss_scale):
    r, dm = vec.shape

    def body(me_ref, v_ref, z_ref, o_ref, l_ref):
        acc = None
        for i in range(N_PEERS + 1):
            term = jnp.where(me_ref[0] == i, v_ref[...], z_ref[i])
            acc = term if acc is None else acc + term
        o_ref[...] = acc
        l_ref[...] = jnp.sum(acc[loss_row:loss_row + SMALL_ROWS, :], axis=(0, 1), keepdims=True) * loss_scale

    grid_spec = pltpu.PrefetchScalarGridSpec(
        num_scalar_prefetch=1, grid=(1,),
        in_specs=[pl.BlockSpec((r, dm), lambda i, me_ref: (0, 0)), pl.BlockSpec((N_PEERS + 1, r, dm), lambda i, me_ref: (0, 0, 0))],
        out_specs=(pl.BlockSpec((r, dm), lambda i, me_ref: (0, 0)), pl.BlockSpec((1, 1), lambda i, me_ref: (0, 0))))
    return pl.pallas_call(
        body, name="reduce_sum", out_shape=(jax.ShapeDtypeStruct((r, dm), F32), jax.ShapeDtypeStruct((1, 1), F32)),
        grid_spec=grid_spec, compiler_params=_params(1),
    )(me, vec, zone)


def kernel(x, meta_tokens, norm_mix_g, w_in, b_gate, pool_w, pool_scale, conv_w, conv_out_w, w_o, norm_ffn_g, w_gate_up, w_down, norm_final_g, loss_target, m_meta_tokens, m_norm_mix_g, m_w_in, m_b_gate, m_pool_w, m_pool_scale, m_conv_w, m_conv_out_w, m_w_o, m_norm_ffn_g, m_w_gate_up, m_w_down, m_norm_final_g, v_meta_tokens, v_norm_mix_g, v_w_in, v_b_gate, v_pool_w, v_pool_scale, v_conv_w, v_conv_out_w, v_w_o, v_norm_ffn_g, v_w_gate_up, v_w_down, v_norm_final_g):
    seq, dm = x.shape[1], x.shape[2]
    tail = TAIL_ROWS
    lp = seq + tail
    tm_row = _row_tile(lp, dm, 4, 3 * 1024 * 1024)
    tm_seq = _row_tile(seq, dm, 4, 3 * 1024 * 1024)
    n_chips = 4
    n_groups = len(POOL_WINDOWS)
    gw = dm // n_groups
    tc = min(256, gw)
    cx, cy, cc = _place()
    chip = 2 * cx + cy
    dloc = dm // n_chips

    pool2 = pool_w.reshape(n_groups * pool_w.shape[1], gw)
    big = {"w_in": w_in, "w_gate_up": w_gate_up, "pool_w": pool2, "conv_out_w": conv_out_w, "w_o": w_o, "w_down": w_down}
    chip1 = jnp.reshape(chip, (1,)).astype(jnp.int32)
    core = jnp.reshape(cc, (1,)).astype(jnp.int32)
    small_loc = jnp.concatenate([meta_tokens, jnp.pad(conv_w, ((0, 8 - conv_w.shape[0]), (0, 0))),
                                 jnp.zeros((8, dloc), F32)], axis=0)
    g1, g2, g3 = norm_mix_g.reshape(1, dm), norm_ffn_g.reshape(1, dm), norm_final_g.reshape(1, dm)
    b_gate2 = b_gate.reshape(2, dm)
    ps = pool_scale.reshape(1, dm)
    mine = jnp.stack([chip, cc]).astype(jnp.int32)
    other = jnp.stack([chip, 1 - cc]).astype(jnp.int32)
    first = [_cast_into_slot("place_small", small_loc, chip1, F32), _cast_half_into_slot("cast_w_in_sent", w_in, mine, BF16)]
    sems, first, token = _ag_start("ag_start_first", first)
    first[1] = _cast_half_into_slot("cast_w_in_kept", w_in, other, BF16, into=first[1])
    cast = {nme: _cast_into_slot("cast_" + nme, big[nme], chip1, BF16, deps=(token,))
            for nme in ["pool_w", "conv_out_w", "w_o", "w_gate_up", "w_down"]}
    sems, first = _ag_relay("ag_relay_first", first, sems, list(cast.values()))
    sems, (small4, w_in4) = _ag_relay_wait("ag_relay_wait_first", first, sems, [])
    (small4,) = _ag_final_wait("ag_final_wait_small", [small4], sems, [])
    mixer_w = [cast["pool_w"], cast["conv_out_w"], cast["w_o"]]
    sems_mix, mixer_w, token = _ag_start("ag_start_mixer", mixer_w, deps=(small4,))
    sems_gu, (w_gu4,), token = _ag_start("ag_start_gate_up", [cast["w_gate_up"]], deps=(token,))

    small_f = jnp.transpose(small4, (1, 0, 2)).reshape(small4.shape[1], dm)
    meta_f = small_f[:N_META]
    conv_w_f = small_f[N_META:N_META + 3]
    tail_rows = jnp.concatenate([jnp.zeros((tail - N_META, dm), F32), meta_f], axis=0)
    h0_hn1 = _rms_fwd_into("rms_mix", x[0], g1, lp, 0, tm_seq, deps=(token,))
    h0, hn1 = _rms_fwd_into("rms_mix_tail", tail_rows, g1, lp, seq, tail, prev=h0_hn1)
    (w_in4,) = _ag_final_wait("ag_final_wait_first", [w_in4], sems, [hn1], first=1)
    proj = _nn_sharded("proj", hn1, w_in4, 6)
    sems_mix, mixer_w = _ag_relay("ag_relay_mixer", mixer_w, sems_mix, [proj])
    (sems_gu, (w_gu4,)), (sems_down, (w_down4,)) = _ag_relay("ag_relay_gate_up", [w_gu4], sems_gu, [mixer_w[0]],
                                                              then_start=[cast["w_down"]])
    pooled, z = _mixer_fwd("mixer_fwd", proj, conv_w_f, tc, w_down4)
    sems_mix, mixer_w = _ag_relay_wait("ag_relay_wait_mixer", mixer_w, sems_mix, [pooled])
    pool4, conv_out4, w_o4 = _ag_final_wait("ag_final_wait_mixer", mixer_w, sems_mix, [])
    pool_f = jnp.transpose(pool4.reshape(n_chips, n_groups, gw // n_chips, gw), (1, 0, 2, 3)).reshape(n_groups, gw, gw)
    conv_out_f = conv_out4.reshape(dm, dm)
    w_o_f = w_o4.reshape(dm, dm)
    ya = _pool_fwd("pool_proj", pooled, pool_f)
    yb = _nn_plain("conv_out", z, conv_out_f, BF16)
    mix = _gate_mix("gate_mix", proj, b_gate2, ya, ps, yb, tm_row)
    sems_gu, (w_gu4,) = _ag_relay_wait("ag_relay_wait_gate_up", [w_gu4], sems_gu, [mix])
    h1 = _nn_plain("attn_out", mix, w_o_f, F32, res=h0, tn_pref=256)
    (w_gu4,) = _ag_final_wait("ag_final_wait_gate_up", [w_gu4], sems_gu, [h1])
    hn2 = _rms_fwd("rms_ffn", h1, g2, tm_row)
    sems_down, (w_down4,) = _ag_relay("ag_relay_down", [w_down4], sems_down, [hn2])
    gu, act = _gate_up_swiglu("gate_up", hn2, w_gu4, w_down4)
    sems_down, (w_down4,) = _ag_relay_wait("ag_relay_wait_down", [w_down4], sems_down, [act])
    (w_down4,) = _ag_final_wait("ag_final_wait_down", [w_down4], sems_down, [])
    w_down_f = w_down4.reshape(-1, dm)
    h2 = _nn_rows("ffn_down", act, w_down_f, h1)
    dh2, dh2b, loss_cols, dg3 = _final_loss("final_loss", h2, g3, loss_target[0], tm_seq)
    dh2, dh2b = _zero_tail("final_loss_tail", [dh2, dh2b], tail)

    def scatter(tag, names_g, swap, after):
        grads_g, got = _swap_wait("swap_wait_" + tag, *swap, [after])
        pairs = [_pair_add("pair_add_" + nme, g4, rv, core) for nme, g4, rv in zip(names_g, grads_g, got)]
        return _scatter_start("scatter_start_" + tag, pairs)

    dgu = _dact_swiglu_bwd("d_gate_up", dh2b, w_down_f, gu)
    gw_down = _tn_plain("dw_down", act, dh2b)
    gw_gu = _tn_sharded("dw_gate_up", hn2, dgu, n_chips)
    swap_a, token = _swap_start("swap_start_a", [gw_gu, gw_down.reshape(n_chips, -1, dm)])
    dhn2 = _nt_sharded("d_hn2", dgu, w_gu4, tr_pref=2816, row_tiles=2, deps=(token,))
    flight_a, token = scatter("a", ["w_gate_up", "w_down"], swap_a, dhn2)
    dh1, dh1b, dg2 = _rms_bwd("rms_ffn_bwd", dhn2, h1, g2, dh2, tm_row, token)
    dmix = _nt_plain("d_mix", dh1b, w_o_f)
    gw_o = _tn_plain("dw_o", mix, dh1b)
    dproj, dyb, dya, db_gate, dps = _gate_bwd("gate_bwd", dmix, proj, b_gate2, ya, ps, yb, tm_row)
    gw_conv_out = _tn_plain("dw_conv_out", z, dyb)
    gw_pool = _pool_bwd_w("dw_pool", pooled, dya)
    gw_pool = jnp.transpose(gw_pool.reshape(n_groups, n_chips, gw // n_chips, gw), (1, 0, 2, 3))
    swap_b, token = _swap_start("swap_start_b", [gw_o.reshape(n_chips, dloc, dm), gw_conv_out.reshape(n_chips, dloc, dm),
                                                 gw_pool.reshape(n_chips, n_groups * (gw // n_chips), gw)])
    dpooled = _pool_bwd_act("d_pooled", dya, pool_f, deps=(token,))
    dz = _nt_plain("d_z", dyb, conv_out_f)
    flight_b, token = scatter("b", ["w_o", "conv_out_w", "pool_w"], swap_b, dz)
    dproj, dconv_w = _mixer_bwd("mixer_bwd", dz, dpooled, proj, conv_w_f, dproj, tc, token)
    gw_in0 = _tn_sharded("dw_in_0", hn1, dproj, n_chips, part=(0, 2))
    swap_c0, token = _swap_start("swap_start_c0", [gw_in0])
    gw_in1 = _tn_sharded("dw_in_1", hn1, dproj, n_chips, part=(1, 2), deps=(token,))
    flight_c0, token = scatter("c0", ["w_in_0"], swap_c0, gw_in1)
    swap_c, token = _swap_start("swap_start_c", [gw_in1], deps=(token,))
    groups_g = {"a": [("w_gate_up", (0, 1)), ("w_down", (0, 1))], "b": [("w_o", (0, 1)), ("conv_out_w", (0, 1)), ("pool_w", (0, 1))],
                "c0": [("w_in", (0, 2))], "c": [("w_in", (1, 2))]}

    def reduced(tag, flight, after):
        pairs, zones = _scatter_wait("scatter_wait_" + tag, *flight, after)
        halves = [_chip_sum("chip_sum_%s_%d" % (nme, part[0]), p, rv, chip1) for (nme, part), p, rv in zip(groups_g[tag], pairs, zones)]
        return _swap_start("send_start_" + tag, halves, halves=False)

    send_a, token = reduced("a", flight_a, [token])
    flight_c, token = scatter("c", ["w_in_1"], swap_c, token)
    dhn1 = _nt_in_proj("d_hn1", dproj, w_in4, deps=(token,))
    dx, dg1 = _rms_bwd_rows("rms_mix_bwd", dhn1, h0, g1, dh1, 0, seq, tm_seq, token)
    dtail, dg1 = _rms_bwd_rows("rms_mix_bwd_tail", dhn1, h0, g1, dh1, seq, tail, tail, dx, dg_prev=dg1)
    grad_x = dx[None]
    dmeta = dtail[tail - N_META:]

    given = dict(meta_tokens=(meta_tokens, m_meta_tokens, v_meta_tokens), norm_mix_g=(norm_mix_g, m_norm_mix_g, v_norm_mix_g),
                 w_in=(w_in, m_w_in, v_w_in), b_gate=(b_gate, m_b_gate, v_b_gate), pool_w=(pool_w, m_pool_w, v_pool_w),
                 pool_scale=(pool_scale, m_pool_scale, v_pool_scale), conv_w=(conv_w, m_conv_w, v_conv_w),
                 conv_out_w=(conv_out_w, m_conv_out_w, v_conv_out_w), w_o=(w_o, m_w_o, v_w_o),
                 norm_ffn_g=(norm_ffn_g, m_norm_ffn_g, v_norm_ffn_g), w_gate_up=(w_gate_up, m_w_gate_up, v_w_gate_up),
                 w_down=(w_down, m_w_down, v_w_down), norm_final_g=(norm_final_g, m_norm_final_g, v_norm_final_g))
    order = list(given.keys())
    grad, delta, new_m, new_v = {}, {}, {}, {}
    vec = jnp.concatenate([dg1, dg2, dg3, db_gate, dps, loss_cols, dconv_w, dmeta], axis=0)
    loss_row = 5 * SMALL_ROWS
    results = {}

    def update(tag, send, after):
        halves, sib_halves = _swap_wait("send_wait_" + tag, *send, after, halves=False)
        deltas = []
        for (nme, part), g_own, g_sib in zip(groups_g[tag], halves, sib_halves):
            w, m, v = given[nme]
            shape2 = (2 * g_own.shape[0] * part[1], g_own.shape[1])
            results[nme] = _adamw_halves("adamw_%s_%d" % (nme, part[0]), w.reshape(shape2), g_own, g_sib, m.reshape(shape2),
                                         v.reshape(shape2), core, part=part, prev=results.get(nme))
            grad[nme], delta[nme], new_m[nme], new_v[nme] = [t.reshape(w.shape) for t in results[nme]]
            deltas.append(results[nme][1])
        return deltas

    done_a = update("a", send_a, [dx])
    send_b, token = reduced("b", flight_b, done_a)
    send_c0, token = reduced("c0", flight_c0, [token])
    done_b = update("b", send_b, [token])
    send_c, token = reduced("c", flight_c, done_b)
    me1 = jnp.reshape(4 * cx + 2 * cy + cc, (1,)).astype(jnp.int32)
    red_flight, token = _reduce_start(vec, [token])
    done_c0 = update("c0", send_c0, [token])
    done_c = update("c", send_c, done_c0)
    red, loss11 = _reduce_sum(*_reduce_wait(*red_flight, done_c), me1, loss_row, 0.5 / dm)
    loss = loss11[0, 0]
    col0 = chip * dloc
    g_small = {
        "norm_mix_g": red[0], "norm_ffn_g": red[SMALL_ROWS], "norm_final_g": red[2 * SMALL_ROWS],
        "b_gate": red[3 * SMALL_ROWS:3 * SMALL_ROWS + 2].reshape(-1), "pool_scale": red[4 * SMALL_ROWS],
        "conv_w": lax.dynamic_slice(red, (6 * SMALL_ROWS, col0), (3, dloc)),
        "meta_tokens": lax.dynamic_slice(red, (7 * SMALL_ROWS, col0), (N_META, dloc)),
    }

    vec_names = ["norm_mix_g", "norm_ffn_g", "norm_final_g", "pool_scale"]

    def slab_vec(pick):
        rows = [pick(nme).reshape(1, dm) for nme in vec_names] + [pick("b_gate").reshape(2, dm), jnp.zeros((2, dm), F32)]
        return jnp.concatenate(rows, axis=0)

    def slab_col(pick):
        return jnp.concatenate([pick("meta_tokens"), pick("conv_w"), jnp.zeros((5, dloc), F32)], axis=0)

    for slab, tag in ((slab_vec, "vec"), (slab_col, "col")):
        d, nm, nv = _adamw("adamw_small_" + tag, slab(lambda nme: given[nme][0]), slab(lambda nme: g_small[nme]),
                           slab(lambda nme: given[nme][1]), slab(lambda nme: given[nme][2]))
        for out, res in ((delta, d), (new_m, nm), (new_v, nv)):
            if tag == "vec":
                for i, nme in enumerate(vec_names):
                    out[nme] = res[i]
                out["b_gate"] = res[4:6].reshape(-1)
            else:
                out["meta_tokens"] = res[:N_META]
                out["conv_w"] = res[N_META:N_META + 3]
    grad.update(g_small)
    return (loss, grad_x, *[grad[nme] for nme in order], *[delta[nme] for nme in order],
            *[new_m[nme] for nme in order], *[new_v[nme] for nme in order])
```

```python
import math

import jax
import jax.numpy as jnp
from jax import lax
from jax.experimental import pallas as pl
from jax.experimental.pallas import tpu as pltpu

F32 = jnp.float32
BF16 = jnp.bfloat16
N_META = 16
POOL_WINDOWS = (2, 4, 8, 16)
EPS = 1e-6
ADAM_LR, ADAM_B1, ADAM_B2, ADAM_EPS, ADAM_WD, ADAM_STEP = 0.001, 0.9, 0.999, 1e-08, 0.01, 10
LANES = 128
V7X_VMEM_BYTES = 64 * 1024 * 1024
VMEM_LIMIT = V7X_VMEM_BYTES - 8 * 1024 * 1024
MESH = pl.DeviceIdType.MESH
ANY = pl.BlockSpec(memory_space=pl.ANY)
CHIP_FLIPS = ((1, 0), (0, 1), (1, 1))
SMALL_ROWS = 8
TAIL_ROWS = 32


def _pick(n, pref):
    best = None
    for t in range(LANES, min(n, pref) + 1, LANES):
        if n % t == 0:
            best = t
    assert best is not None, (n, pref)
    return best


def _params(n_axes=0):
    sem = ("arbitrary",) * n_axes if n_axes else None
    return pltpu.CompilerParams(dimension_semantics=sem, vmem_limit_bytes=VMEM_LIMIT)


_DIMS = {
    "nn": (((1,), (0,)), ((), ())),
    "nt": (((1,), (1,)), ((), ())),
    "tn": (((0,), (0,)), ((), ())),
}


def _matmul(name, mode, a, b, out_sds, grid, a_spec, b_spec, o_spec, nk, res=None, res_spec=None, acc_shape=None, deps=()):
    out_dtype = out_sds.dtype
    in_place = nk > 1 and out_dtype == F32
    use_scratch = nk > 1 and not in_place
    rows = a_spec.block_shape[-2] if mode != "tn" else None
    chunk = _row_tile(rows, 1, 1, 1152) if rows is not None else None
    n_in = 2 + (res is not None) + len(deps)

    def body(*refs):
        a_ref, b_ref = refs[:2]
        r_ref = refs[2] if res is not None else None
        o_ref, *scr = refs[n_in:]
        k = pl.program_id(len(grid) - 1) if nk > 1 else None

        def emit(sl):
            if sl is None:
                part = lax.dot_general(a_ref[...], b_ref[...], _DIMS[mode], preferred_element_type=F32)
                idx = (slice(None), slice(None))
            else:
                part = lax.dot_general(a_ref[sl, :], b_ref[...], _DIMS[mode], preferred_element_type=F32)
                idx = (sl, slice(None))
            if nk == 1:
                if r_ref is not None:
                    part = part + r_ref[idx]
                o_ref[idx] = part.astype(out_dtype)
                return
            acc = scr[0] if use_scratch else o_ref

            @pl.when(k == 0)
            def _():
                first = part
                if r_ref is not None and in_place:
                    first = first + r_ref[idx]
                acc[idx] = first

            @pl.when(k > 0)
            def _():
                acc[idx] += part

            if use_scratch:

                @pl.when(k == nk - 1)
                def _():
                    o_ref[idx] = acc[idx].astype(out_dtype)

        if mode == "tn" or chunk == rows:
            emit(None)
        else:
            for m0 in range(0, rows, chunk):
                emit(pl.ds(m0, chunk))

    ins = [a, b] + ([res] if res is not None else []) + list(deps)
    in_specs = [a_spec, b_spec] + ([res_spec] if res is not None else []) + [ANY] * len(deps)
    scratch = [pltpu.VMEM(acc_shape, F32)] if use_scratch else []
    return pl.pallas_call(
        body, name=name, out_shape=out_sds, grid=grid, in_specs=in_specs, out_specs=o_spec,
        scratch_shapes=scratch, compiler_params=_params(len(grid)),
    )(*ins)


def _nn_sharded(name, a, w4, nseg):
    lp, kdim = a.shape
    s, _, nloc = w4.shape
    segw = s * nloc // nseg
    tn = _pick(math.gcd(nloc, segw), 1536)
    bw, bo = nloc // tn, segw // tn
    return _matmul(
        name, "nn", a, w4, jax.ShapeDtypeStruct((nseg, lp, segw), BF16), (s * bw,),
        pl.BlockSpec((lp, kdim), lambda j: (0, 0)),
        pl.BlockSpec((None, kdim, tn), lambda j: (j // bw, 0, j % bw)),
        pl.BlockSpec((None, lp, tn), lambda j: (j // bo, 0, j % bo)), 1)


def _nt_in_proj(name, dseg, w4, row_tiles=2, to_pref=1024, deps=()):
    nseg, lp, segw = dseg.shape
    s, kdim, nloc = w4.shape
    assert nseg * segw == s * nloc and 2 * nloc == 3 * segw, (dseg.shape, w4.shape)
    half = segw // 2
    to = _pick(kdim, to_pref)
    tm = lp // row_tiles

    def body(full_ref, half_ref, w_ref, *rest):
        o_ref = rest[len(deps)]
        r = pl.program_id(2)

        def contribution(full_first):
            lo, hi = (pl.ds(0, segw), pl.ds(segw, half)) if full_first else (pl.ds(half, segw), pl.ds(0, half))
            return (lax.dot_general(full_ref[...], w_ref[:, lo], _DIMS["nt"], preferred_element_type=F32)
                    + lax.dot_general(half_ref[...], w_ref[:, hi], _DIMS["nt"], preferred_element_type=F32))

        @pl.when(r == 0)
        def _():
            o_ref[...] = contribution(True)

        for ri in range(1, s):

            @pl.when(r == ri)
            def _(ri=ri):
                o_ref[...] += contribution(ri % 2 == 0)

    return pl.pallas_call(
        body, name=name, out_shape=jax.ShapeDtypeStruct((lp, kdim), F32), grid=(row_tiles, kdim // to, s),
        in_specs=[pl.BlockSpec((None, tm, segw), lambda m, j, r: ((3 * r + 1) // 2, m, 0)),
                  pl.BlockSpec((None, tm, half), lambda m, j, r: (1 + 3 * (r // 2), m, r % 2)),
                  pl.BlockSpec((None, to, nloc), lambda m, j, r: (r, j, 0))] + [ANY] * len(deps),
        out_specs=pl.BlockSpec((tm, to), lambda m, j, r: (m, j)), compiler_params=_params(3),
    )(dseg, dseg, w4, *deps)


def _nn_plain(name, a, w, out_dtype, res=None, tn_pref=512, tk_pref=2048, deps=()):
    lp, kdim = a.shape
    n = w.shape[1]
    tn = _pick(n, tn_pref)
    tk = kdim if kdim <= tk_pref else _pick(kdim, tk_pref)
    nk = kdim // tk
    grid = (n // tn, nk) if nk > 1 else (n // tn,)
    if nk > 1:
        a_spec = pl.BlockSpec((lp, tk), lambda j, k: (0, k))
        w_spec = pl.BlockSpec((tk, tn), lambda j, k: (k, j))
        o_spec = pl.BlockSpec((lp, tn), lambda j, k: (0, j))
    else:
        a_spec = pl.BlockSpec((lp, tk), lambda j: (0, 0))
        w_spec = pl.BlockSpec((tk, tn), lambda j: (0, j))
        o_spec = pl.BlockSpec((lp, tn), lambda j: (0, j))
    return _matmul(name, "nn", a, w, jax.ShapeDtypeStruct((lp, n), out_dtype), grid, a_spec, w_spec, o_spec, nk,
                   res=res, res_spec=o_spec if res is not None else None, acc_shape=(lp, tn), deps=deps)


def _nt_plain(name, a, w, tn_pref=512):
    lp, kdim = a.shape
    n = w.shape[0]
    tn = _pick(n, tn_pref)
    return _matmul(
        name, "nt", a, w, jax.ShapeDtypeStruct((lp, n), BF16), (n // tn,),
        pl.BlockSpec((lp, kdim), lambda j: (0, 0)),
        pl.BlockSpec((tn, kdim), lambda j: (j, 0)),
        pl.BlockSpec((lp, tn), lambda j: (0, j)), 1)


def _nt_sharded(name, dseg, w4, to_pref=1024, tr_pref=1536, row_tiles=1, deps=()):
    nseg, lp, segw = dseg.shape
    s, kdim, nloc = w4.shape
    tr = _pick(math.gcd(nloc, segw), tr_pref)
    ba, bw = segw // tr, nloc // tr
    nr = s * bw
    to = _pick(kdim, to_pref)
    tm = lp // row_tiles
    return _matmul(
        name, "nt", dseg, w4, jax.ShapeDtypeStruct((lp, kdim), F32), (row_tiles, kdim // to, nr),
        pl.BlockSpec((None, tm, tr), lambda m, j, r: (r // ba, m, r % ba)),
        pl.BlockSpec((None, to, tr), lambda m, j, r: (r // bw, j, r % bw)),
        pl.BlockSpec((tm, to), lambda m, j, r: (m, j)), nr, deps=deps)


def _nn_rows(name, a, w, res, row_tiles=2, tn_pref=512):
    lp, kdim = a.shape
    n = w.shape[1]
    tn = _pick(n, tn_pref)
    tm = lp // row_tiles
    blk = pl.BlockSpec((tm, tn), lambda i, j: (i, j))
    return _matmul(name, "nn", a, w, jax.ShapeDtypeStruct((lp, n), F32), (row_tiles, n // tn),
                   pl.BlockSpec((tm, kdim), lambda i, j: (i, 0)), pl.BlockSpec((kdim, tn), lambda i, j: (0, j)), blk, 1,
                   res=res, res_spec=blk)


def _tn_plain(name, a, d, tk_pref=1024):
    lp, kdim = a.shape
    n = d.shape[1]
    tk = _pick(kdim, tk_pref)
    return _matmul(
        name, "tn", a, d, jax.ShapeDtypeStruct((kdim, n), BF16), (kdim // tk,),
        pl.BlockSpec((lp, tk), lambda i: (0, i)),
        pl.BlockSpec((lp, n), lambda i: (0, 0)),
        pl.BlockSpec((tk, n), lambda i: (i, 0)), 1)


def _tn_sharded(name, a, dseg, s, part=(0, 1), tk_pref=1024, deps=()):
    lp, kdim = a.shape
    nseg, _, segw = dseg.shape
    nloc = nseg * segw // s
    tn = _pick(math.gcd(nloc, segw), 1536)
    bd, bo = segw // tn, nloc // tn
    kpart = kdim // part[1]
    tk = _pick(kpart, tk_pref)
    i0 = part[0] * (kpart // tk)

    def body(a_ref, d_ref, *rest):
        o_ref, at_ref = rest[len(deps):]

        @pl.when(pl.program_id(1) == 0)
        def _():
            at_ref[...] = a_ref[...].T

        o_ref[...] = jnp.dot(at_ref[...], d_ref[...], preferred_element_type=F32).astype(BF16)

    return pl.pallas_call(
        body, name=name, out_shape=jax.ShapeDtypeStruct((s, kpart, nloc), BF16), grid=(kpart // tk, s * bo),
        in_specs=[pl.BlockSpec((lp, tk), lambda i, j: (0, i0 + i)),
                  pl.BlockSpec((None, lp, tn), lambda i, j: (j // bd, 0, j % bd))] + [ANY] * len(deps),
        out_specs=pl.BlockSpec((None, tk, tn), lambda i, j: (j // bo, i, j % bo)),
        scratch_shapes=[pltpu.VMEM((tk, lp), BF16)], compiler_params=_params(2),
    )(a, dseg, *deps)


def _silu_parts(gt):
    sg = jax.nn.sigmoid(gt)
    return gt * sg, sg * (1.0 + gt * (1.0 - sg))


def _gate_up_swiglu(name, a, w4, dep, tn_pref=256):
    lp, kdim = a.shape
    s, _, nloc = w4.shape
    f = s * nloc // 2
    tn = _pick(nloc, tn_pref)
    bw = nloc // tn
    chunk = _row_tile(lp, 1, 1, 576)

    def body(a_ref, wg_ref, wu_ref, _, fac_ref, act_ref):
        for m0 in range(0, lp, chunk):
            sl = pl.ds(m0, chunk)
            gt = jnp.dot(a_ref[sl, :], wg_ref[...], preferred_element_type=F32)
            up = jnp.dot(a_ref[sl, :], wu_ref[...], preferred_element_type=F32)
            silu, dsilu = _silu_parts(gt)
            fac_ref[0, sl, :] = (up * dsilu).astype(BF16)
            fac_ref[1, sl, :] = silu.astype(BF16)
            act_ref[sl, :] = (silu * up).astype(BF16)

    return pl.pallas_call(
        body, name=name, grid=(f // tn,),
        out_shape=(jax.ShapeDtypeStruct((2, lp, f), BF16), jax.ShapeDtypeStruct((lp, f), BF16)),
        in_specs=[pl.BlockSpec((lp, kdim), lambda j: (0, 0)),
                  pl.BlockSpec((None, kdim, tn), lambda j: (j // bw, 0, j % bw)),
                  pl.BlockSpec((None, kdim, tn), lambda j: (s // 2 + j // bw, 0, j % bw)), ANY],
        out_specs=(pl.BlockSpec((2, lp, tn), lambda j: (0, 0, j)), pl.BlockSpec((lp, tn), lambda j: (0, j))),
        compiler_params=_params(1),
    )(a, w4, w4, dep)


def _dact_swiglu_bwd(name, d, w, gu, tn_pref=512):
    lp, dm = d.shape
    f = w.shape[0]
    tn = _pick(f, tn_pref)
    chunk = _row_tile(lp, 1, 1, 576)

    def body(d_ref, w_ref, g_ref, u_ref, o_ref):
        for m0 in range(0, lp, chunk):
            sl = pl.ds(m0, chunk)
            dact = lax.dot_general(d_ref[sl, :], w_ref[...], _DIMS["nt"], preferred_element_type=F32)
            o_ref[0, sl, :] = (dact * g_ref[sl, :].astype(F32)).astype(BF16)
            o_ref[1, sl, :] = (dact * u_ref[sl, :].astype(F32)).astype(BF16)

    return pl.pallas_call(
        body, name=name, grid=(f // tn,), out_shape=jax.ShapeDtypeStruct((2, lp, f), BF16),
        in_specs=[pl.BlockSpec((lp, dm), lambda j: (0, 0)), pl.BlockSpec((tn, dm), lambda j: (j, 0)),
                  pl.BlockSpec((None, lp, tn), lambda j: (0, 0, j)), pl.BlockSpec((None, lp, tn), lambda j: (1, 0, j))],
        out_specs=pl.BlockSpec((2, lp, tn), lambda j: (0, 0, j)), compiler_params=_params(1),
    )(d, w, gu, gu)


def _pool_fwd(name, pooled, pw):
    lp, dm = pooled.shape
    g, gw, _ = pw.shape
    return _matmul(
        name, "nn", pooled, pw, jax.ShapeDtypeStruct((lp, dm), BF16), (g,),
        pl.BlockSpec((lp, gw), lambda gi: (0, gi)), pl.BlockSpec((None, gw, gw), lambda gi: (gi, 0, 0)),
        pl.BlockSpec((lp, gw), lambda gi: (0, gi)), 1)


def _pool_bwd_act(name, dya, pw, deps=()):
    lp, dm = dya.shape
    g, gw, _ = pw.shape
    return _matmul(
        name, "nt", dya, pw, jax.ShapeDtypeStruct((lp, dm), BF16), (g,),
        pl.BlockSpec((lp, gw), lambda gi: (0, gi)), pl.BlockSpec((None, gw, gw), lambda gi: (gi, 0, 0)),
        pl.BlockSpec((lp, gw), lambda gi: (0, gi)), 1, deps=deps)


def _pool_bwd_w(name, pooled, dya):
    lp, dm = pooled.shape
    g = len(POOL_WINDOWS)
    gw = dm // g
    return _matmul(
        name, "tn", pooled, dya, jax.ShapeDtypeStruct((g, gw, gw), BF16), (g,),
        pl.BlockSpec((lp, gw), lambda gi: (0, gi)), pl.BlockSpec((lp, gw), lambda gi: (0, gi)),
        pl.BlockSpec((None, gw, gw), lambda gi: (gi, 0, 0)), 1)


def _rms_fwd(name, h, g, tm, deps=()):
    lp, dm = h.shape

    def body(h_ref, g_ref, *rest):
        hv = h_ref[...]
        r = lax.rsqrt(jnp.mean(hv * hv, axis=-1, keepdims=True) + EPS)
        rest[-1][...] = (hv * r * g_ref[...]).astype(BF16)

    row = pl.BlockSpec((tm, dm), lambda i: (i, 0))
    return pl.pallas_call(
        body, name=name, out_shape=jax.ShapeDtypeStruct((lp, dm), BF16), grid=(lp // tm,),
        in_specs=[row, pl.BlockSpec((1, dm), lambda i: (0, 0))] + [ANY] * len(deps), out_specs=row, compiler_params=_params(1),
    )(h, g, *deps)


def _rms_fwd_into(name, src, g, lp, row0, tm, prev=None, deps=()):
    n, dm = src.shape
    b0 = row0 // tm
    n_in = 2 + len(deps)

    def body(s_ref, g_ref, *rest):
        h_ref, o_ref = rest[-2:]
        hv = s_ref[...]
        r = lax.rsqrt(jnp.mean(hv * hv, axis=-1, keepdims=True) + EPS)
        h_ref[...] = hv
        o_ref[...] = (hv * r * g_ref[...]).astype(BF16)

    row = pl.BlockSpec((tm, dm), lambda i: (b0 + i, 0))
    return pl.pallas_call(
        body, name=name, grid=(n // tm,),
        out_shape=(jax.ShapeDtypeStruct((lp, dm), F32), jax.ShapeDtypeStruct((lp, dm), BF16)),
        in_specs=[pl.BlockSpec((tm, dm), lambda i: (i, 0)), pl.BlockSpec((1, dm), lambda i: (0, 0))]
        + [ANY] * (len(deps) + (0 if prev is None else 2)),
        out_specs=(row, row), input_output_aliases={} if prev is None else {n_in: 0, n_in + 1: 1},
        compiler_params=_params(1),
    )(src, g, *deps, *(prev or ()))


def _rms_bwd(name, dy, h, g, dres, tm, dep):
    lp, dm = h.shape

    def body(dy_ref, h_ref, g_ref, dr_ref, _, dh_ref, dhb_ref, dg_ref):
        hv = h_ref[...]
        r = lax.rsqrt(jnp.mean(hv * hv, axis=-1, keepdims=True) + EPS)
        xhat = hv * r
        dyv = dy_ref[...]
        dxh = dyv * g_ref[...]
        dh = dr_ref[...] + r * (dxh - xhat * jnp.mean(dxh * xhat, axis=-1, keepdims=True))
        dh_ref[...] = dh
        dhb_ref[...] = dh.astype(BF16)

        @pl.when(pl.program_id(0) == 0)
        def _():
            dg_ref[...] = jnp.zeros_like(dg_ref)

        dg_ref[0:1, :] += jnp.sum(dyv * xhat, axis=0, keepdims=True)

    row = pl.BlockSpec((tm, dm), lambda i: (i, 0))
    slab = pl.BlockSpec((SMALL_ROWS, dm), lambda i: (0, 0))
    return pl.pallas_call(
        body, name=name, grid=(lp // tm,),
        out_shape=(jax.ShapeDtypeStruct((lp, dm), F32), jax.ShapeDtypeStruct((lp, dm), BF16),
                   jax.ShapeDtypeStruct((SMALL_ROWS, dm), F32)),
        in_specs=[row, row, pl.BlockSpec((1, dm), lambda i: (0, 0)), row, ANY], out_specs=(row, row, slab),
        compiler_params=_params(1),
    )(dy, h, g, dres, dep)


def _rms_bwd_rows(name, dy, h, g, dres, row0, nrows, tm, dep, dg_prev=None):
    dm = h.shape[1]
    b0 = row0 // tm

    def body(dy_ref, h_ref, g_ref, dr_ref, *rest):
        d_ref, dg_ref = rest[-2:]
        hv = h_ref[...]
        r = lax.rsqrt(jnp.mean(hv * hv, axis=-1, keepdims=True) + EPS)
        xhat = hv * r
        dyv = dy_ref[...]
        dxh = dyv * g_ref[...]
        d_ref[...] = dr_ref[...] + r * (dxh - xhat * jnp.mean(dxh * xhat, axis=-1, keepdims=True))

        @pl.when(pl.program_id(0) == 0)
        def _():
            dg_ref[...] = jnp.zeros_like(dg_ref) if dg_prev is None else rest[1][...]

        dg_ref[0:1, :] += jnp.sum(dyv * xhat, axis=0, keepdims=True)

    row = pl.BlockSpec((tm, dm), lambda i: (b0 + i, 0))
    slab = pl.BlockSpec((SMALL_ROWS, dm), lambda i: (0, 0))
    extra = [dep] + ([dg_prev] if dg_prev is not None else [])
    return pl.pallas_call(
        body, name=name, grid=(nrows // tm,),
        out_shape=(jax.ShapeDtypeStruct((nrows, dm), F32), jax.ShapeDtypeStruct((SMALL_ROWS, dm), F32)),
        in_specs=[row, row, pl.BlockSpec((1, dm), lambda i: (0, 0)), row, ANY] + ([slab] if dg_prev is not None else []),
        out_specs=(pl.BlockSpec((tm, dm), lambda i: (i, 0)), slab), compiler_params=_params(1),
    )(dy, h, g, dres, *extra)


def _gate_mix(name, proj, b_gate2, ya, pool_scale, yb, tm):
    _, lp, dm = proj.shape

    def body(ga_ref, gr_ref, b_ref, ya_ref, ps_ref, yb_ref, o_ref):
        g_a = jax.nn.sigmoid(ga_ref[...].astype(F32) + b_ref[0:1, :])
        g_b = jax.nn.sigmoid(gr_ref[...].astype(F32) + b_ref[1:2, :])
        y_a = ya_ref[...].astype(F32) * ps_ref[...]
        o_ref[...] = (g_a * y_a + g_b * yb_ref[...].astype(F32)).astype(BF16)

    row = pl.BlockSpec((tm, dm), lambda i: (i, 0))
    return pl.pallas_call(
        body, name=name, out_shape=jax.ShapeDtypeStruct((lp, dm), BF16), grid=(lp // tm,),
        in_specs=[pl.BlockSpec((None, tm, dm), lambda i: (4, i, 0)), pl.BlockSpec((None, tm, dm), lambda i: (5, i, 0)),
                  pl.BlockSpec((2, dm), lambda i: (0, 0)), row, pl.BlockSpec((1, dm), lambda i: (0, 0)), row],
        out_specs=row, compiler_params=_params(1),
    )(proj, proj, b_gate2, ya, pool_scale, yb)


def _gate_bwd(name, dmix, proj, b_gate2, ya, pool_scale, yb, tm):
    _, lp, dm = proj.shape

    def body(dm_ref, ga_ref, gr_ref, b_ref, ya_ref, ps_ref, yb_ref, dp_ref, dyb_ref, dya_ref, db_ref, dps_ref):
        dmx = dm_ref[...].astype(F32)
        g_a = jax.nn.sigmoid(ga_ref[...].astype(F32) + b_ref[0:1, :])
        g_b = jax.nn.sigmoid(gr_ref[...].astype(F32) + b_ref[1:2, :])
        ya_pre = ya_ref[...].astype(F32)
        ybv = yb_ref[...].astype(F32)
        ps = ps_ref[...]
        dga = dmx * (ya_pre * ps) * (g_a * (1.0 - g_a))
        dgr = dmx * ybv * (g_b * (1.0 - g_b))
        dp_ref[0] = dga.astype(BF16)
        dp_ref[1] = dgr.astype(BF16)
        dyb_ref[...] = (dmx * g_b).astype(BF16)
        dya_ref[...] = (dmx * g_a * ps).astype(BF16)

        @pl.when(pl.program_id(0) == 0)
        def _():
            db_ref[...] = jnp.zeros_like(db_ref)
            dps_ref[...] = jnp.zeros_like(dps_ref)

        db_ref[0:1, :] += jnp.sum(dga, axis=0, keepdims=True)
        db_ref[1:2, :] += jnp.sum(dgr, axis=0, keepdims=True)
        dps_ref[0:1, :] += jnp.sum(dmx * g_a * ya_pre, axis=0, keepdims=True)

    row = pl.BlockSpec((tm, dm), lambda i: (i, 0))
    one = pl.BlockSpec((1, dm), lambda i: (0, 0))
    slab = pl.BlockSpec((SMALL_ROWS, dm), lambda i: (0, 0))
    return pl.pallas_call(
        body, name=name, grid=(lp // tm,),
        out_shape=(jax.ShapeDtypeStruct((6, lp, dm), BF16), jax.ShapeDtypeStruct((lp, dm), BF16),
                   jax.ShapeDtypeStruct((lp, dm), BF16), jax.ShapeDtypeStruct((SMALL_ROWS, dm), F32),
                   jax.ShapeDtypeStruct((SMALL_ROWS, dm), F32)),
        in_specs=[row, pl.BlockSpec((None, tm, dm), lambda i: (4, i, 0)), pl.BlockSpec((None, tm, dm), lambda i: (5, i, 0)),
                  pl.BlockSpec((2, dm), lambda i: (0, 0)), row, one, row],
        out_specs=(pl.BlockSpec((2, tm, dm), lambda i: (2, i, 0)), row, row, slab, slab),
        compiler_params=_params(1),
    )(dmix, proj, proj, b_gate2, ya, pool_scale, yb)


def _final_loss(name, h2, g3, target, tm):
    lp, dm = h2.shape
    seq = target.shape[0]

    def body(h_ref, g_ref, t_ref, dh_ref, dhb_ref, ls_ref, dg_ref):
        @pl.when(pl.program_id(0) == 0)
        def _():
            ls_ref[...] = jnp.zeros_like(ls_ref)
            dg_ref[...] = jnp.zeros_like(dg_ref)

        hv = h_ref[...]
        gv = g_ref[...]
        r = lax.rsqrt(jnp.mean(hv * hv, axis=-1, keepdims=True) + EPS)
        xhat = hv * r
        err = xhat * gv - t_ref[...]
        dout = err * (1.0 / dm)
        dxh = dout * gv
        dh = r * (dxh - xhat * jnp.mean(dxh * xhat, axis=-1, keepdims=True))
        dh_ref[...] = dh
        dhb_ref[...] = dh.astype(BF16)
        ls_ref[0:1, :] += jnp.sum(err * err, axis=0, keepdims=True)
        dg_ref[0:1, :] += jnp.sum(dout * xhat, axis=0, keepdims=True)

    row = pl.BlockSpec((tm, dm), lambda i: (i, 0))
    slab = pl.BlockSpec((SMALL_ROWS, dm), lambda i: (0, 0))
    return pl.pallas_call(
        body, name=name, grid=(seq // tm,),
        out_shape=(jax.ShapeDtypeStruct((lp, dm), F32), jax.ShapeDtypeStruct((lp, dm), BF16),
                   jax.ShapeDtypeStruct((SMALL_ROWS, dm), F32), jax.ShapeDtypeStruct((SMALL_ROWS, dm), F32)),
        in_specs=[row, pl.BlockSpec((1, dm), lambda i: (0, 0)), row],
        out_specs=(row, row, slab, slab), compiler_params=_params(1),
    )(h2, g3, target)


def _zero_tail(name, arrays, tail):
    n = len(arrays)
    lp, dm = arrays[0].shape
    last = lp // tail - 1

    def body(*refs):
        for o_ref in refs[n:]:
            o_ref[...] = jnp.zeros_like(o_ref)

    return pl.pallas_call(
        body, name=name, grid=(1,), out_shape=tuple(jax.ShapeDtypeStruct(a.shape, a.dtype) for a in arrays),
        in_specs=[ANY] * n, out_specs=tuple(pl.BlockSpec((tail, dm), lambda i: (last, 0)) for _ in arrays),
        input_output_aliases={a: a for a in range(n)}, compiler_params=_params(1),
    )(*arrays)


def _shift(v, k):
    return pltpu.roll(v, k % v.shape[0], axis=0)


def _window_sum(v, group, sign):
    s2 = v + _shift(v, sign * 1)
    s4 = s2 + _shift(s2, sign * 2)
    s8 = s4 + _shift(s4, sign * 4)
    s16 = s8 + _shift(s8, sign * 8)
    return jnp.where(group == 0, s2, jnp.where(group == 1, s4, jnp.where(group == 2, s8, s16)))


def _pool_count(lp, group):
    row = lax.broadcasted_iota(jnp.int32, (lp, 1), 0)
    window = jnp.left_shift(2, group).astype(F32)
    meta_pos = (row - (lp - N_META) + 1).astype(F32)
    return jnp.where(row >= lp - N_META, jnp.minimum(meta_pos, window), window)


def _mixer_fwd(name, proj, conv_w, tc, dep):
    _, lp, dm = proj.shape
    per_group = dm // len(POOL_WINDOWS) // tc

    def body(u_ref, gb_ref, gc_ref, v_ref, cw_ref, _, p_ref, z_ref):
        group = pl.program_id(0) // per_group
        u = u_ref[...].astype(F32)
        p_ref[...] = (_window_sum(u, group, 1) / _pool_count(lp, group) - u).astype(BF16)
        cv = gc_ref[...].astype(F32) * v_ref[...].astype(F32)
        conv = cw_ref[0:1, :] * _shift(cv, 2) + cw_ref[1:2, :] * _shift(cv, 1) + cw_ref[2:3, :] * cv
        z_ref[...] = (gb_ref[...].astype(F32) * conv).astype(BF16)

    def seg(s):
        return pl.BlockSpec((None, lp, tc), lambda j: (s, 0, j))

    col = pl.BlockSpec((lp, tc), lambda j: (0, j))
    return pl.pallas_call(
        body, name=name, grid=(dm // tc,),
        out_shape=(jax.ShapeDtypeStruct((lp, dm), BF16), jax.ShapeDtypeStruct((lp, dm), BF16)),
        in_specs=[seg(0), seg(1), seg(2), seg(3), pl.BlockSpec((3, tc), lambda j: (0, j)), ANY],
        out_specs=(col, col), compiler_params=_params(1),
    )(proj, proj, proj, proj, conv_w, dep)


def _mixer_bwd(name, dz, dpooled, proj, conv_w, dproj, tc, dep):
    _, lp, dm = proj.shape
    per_group = dm // len(POOL_WINDOWS) // tc

    def body(dz_ref, dp_ref, gb_ref, gc_ref, v_ref, cw_ref, _, __, o_ref, dcw_ref):
        group = pl.program_id(0) // per_group
        dzv = dz_ref[...].astype(F32)
        gb = gb_ref[...].astype(F32)
        gc = gc_ref[...].astype(F32)
        vv = v_ref[...].astype(F32)
        cv = gc * vv
        c1 = _shift(cv, 1)
        c2 = _shift(cv, 2)
        w0, w1, w2 = cw_ref[0:1, :], cw_ref[1:2, :], cw_ref[2:3, :]
        o_ref[1] = (dzv * (w0 * c2 + w1 * c1 + w2 * cv)).astype(BF16)
        dconv = dzv * gb
        dcw_ref[...] = jnp.zeros_like(dcw_ref)
        dcw_ref[0:1, :] = jnp.sum(dconv * c2, axis=0, keepdims=True)
        dcw_ref[1:2, :] = jnp.sum(dconv * c1, axis=0, keepdims=True)
        dcw_ref[2:3, :] = jnp.sum(dconv * cv, axis=0, keepdims=True)
        dcv = w0 * _shift(dconv, -2) + w1 * _shift(dconv, -1) + w2 * dconv
        o_ref[2] = (dcv * vv).astype(BF16)
        o_ref[3] = (dcv * gc).astype(BF16)
        dpv = dp_ref[...].astype(F32)
        o_ref[0] = (_window_sum(dpv / _pool_count(lp, group), group, -1) - dpv).astype(BF16)

    def seg(s):
        return pl.BlockSpec((None, lp, tc), lambda j: (s, 0, j))

    col = pl.BlockSpec((lp, tc), lambda j: (0, j))
    return pl.pallas_call(
        body, name=name, grid=(dm // tc,),
        out_shape=(jax.ShapeDtypeStruct(dproj.shape, BF16), jax.ShapeDtypeStruct((SMALL_ROWS, dm), F32)),
        in_specs=[col, col, seg(1), seg(2), seg(3), pl.BlockSpec((3, tc), lambda j: (0, j)), ANY, ANY],
        out_specs=(pl.BlockSpec((4, lp, tc), lambda j: (0, 0, j)), pl.BlockSpec((SMALL_ROWS, tc), lambda j: (0, j))),
        input_output_aliases={6: 0}, compiler_params=_params(1),
    )(dz, dpooled, proj, proj, proj, conv_w, dproj, dep)


def _row_tile(r, c, bytes_per_row_elem=4, budget=2 * 1024 * 1024):
    best = None
    for t in range(16, r + 1, 16):
        if r % t == 0 and t * c * bytes_per_row_elem <= budget:
            best = t
    return best if best is not None else r


def _pair_add(name, g4, recv, core):
    s, r, c = g4.shape
    h = r // 2
    tr = _row_tile(h, c, budget=6 * 1024 * 1024)
    nb = h // tr

    def body(core_ref, g_ref, r_ref, o_ref):
        o_ref[...] = (g_ref[...].astype(F32) + r_ref[...].astype(F32)).astype(BF16)

    grid_spec = pltpu.PrefetchScalarGridSpec(
        num_scalar_prefetch=1, grid=(s, nb),
        in_specs=[pl.BlockSpec((None, tr, c), lambda si, j, core_ref: (si, core_ref[0] * nb + j, 0)),
                  pl.BlockSpec((None, tr, c), lambda si, j, core_ref: (si, j, 0))],
        out_specs=pl.BlockSpec((None, tr, c), lambda si, j, core_ref: (si, j, 0)))
    return pl.pallas_call(
        body, name=name, out_shape=jax.ShapeDtypeStruct((s, h, c), BF16), grid_spec=grid_spec,
        compiler_params=_params(2),
    )(core, g4, recv)


def _chip_sum(name, parts, recv, chip):
    _, h, c = parts.shape
    tr = _row_tile(h, c)

    def body(chip_ref, p_ref, r_ref, o_ref):
        acc = p_ref[...].astype(F32)
        for i in range(len(CHIP_FLIPS)):
            acc = acc + r_ref[i].astype(F32)
        o_ref[...] = acc

    grid_spec = pltpu.PrefetchScalarGridSpec(
        num_scalar_prefetch=1, grid=(h // tr,),
        in_specs=[pl.BlockSpec((None, tr, c), lambda j, chip_ref: (chip_ref[0], j, 0)),
                  pl.BlockSpec((len(CHIP_FLIPS), tr, c), lambda j, chip_ref: (0, j, 0))],
        out_specs=pl.BlockSpec((tr, c), lambda j, chip_ref: (j, 0)))
    return pl.pallas_call(
        body, name=name, out_shape=jax.ShapeDtypeStruct((h, c), F32), grid_spec=grid_spec, compiler_params=_params(1),
    )(chip, parts, recv)


def _adam_update(w, gv, m, v):
    c1 = 1.0 - ADAM_B1 ** ADAM_STEP
    c2 = 1.0 - ADAM_B2 ** ADAM_STEP
    nm = ADAM_B1 * m + (1.0 - ADAM_B1) * gv
    nv = ADAM_B2 * v + (1.0 - ADAM_B2) * (gv * gv)
    return -ADAM_LR * ((nm / c1) / (jnp.sqrt(nv / c2) + ADAM_EPS) + ADAM_WD * w), nm, nv


def _adamw_halves(name, w, g_own, g_sib, m, v, core, part=(0, 1), prev=None):
    r, c = w.shape
    rp = r // part[1]
    h = rp // 2
    tr = _row_tile(h, c, budget=2 * 1024 * 1024)
    nbh = h // tr
    j0 = part[0] * 2 * nbh
    n_prev = 0 if prev is None else 4

    def body(core_ref, w_ref, go_ref, gs_ref, m_ref, v_ref, *rest):
        g_ref, d_ref, nm_ref, nv_ref = rest[n_prev:]
        mine = (pl.program_id(0) // nbh) == core_ref[0]
        gv = jnp.where(mine, go_ref[...], gs_ref[...])
        g_ref[...] = gv
        d_ref[...], nm_ref[...], nv_ref[...] = _adam_update(w_ref[...], gv, m_ref[...], v_ref[...])

    def blk(fn):
        return pl.BlockSpec((tr, c), fn)

    full = blk(lambda j, core_ref: (j0 + j, 0))
    own = blk(lambda j, core_ref: (jnp.clip(j - core_ref[0] * nbh, 0, nbh - 1), 0))
    sib = blk(lambda j, core_ref: (jnp.clip(j - (1 - core_ref[0]) * nbh, 0, nbh - 1), 0))
    grid_spec = pltpu.PrefetchScalarGridSpec(
        num_scalar_prefetch=1, grid=(2 * nbh,), in_specs=[full, own, sib, full, full] + [ANY] * n_prev, out_specs=(full,) * 4)
    sds = jax.ShapeDtypeStruct((r, c), F32)
    return pl.pallas_call(
        body, name=name, out_shape=(sds,) * 4, grid_spec=grid_spec, compiler_params=_params(1),
        input_output_aliases={6 + i: i for i in range(n_prev)},
    )(core, w, g_own, g_sib, m, v, *(prev or ()))


def _adamw_small(name, red, rows, cols, params):
    n = len(params)
    whole = pl.BlockSpec(memory_space=pltpu.VMEM)

    def body(red_ref, *refs):
        ins, outs = refs[:3 * n], refs[3 * n:]
        chip = 2 * lax.axis_index("x") + lax.axis_index("y")
        for i in range(n):
            w_ref, m_ref, v_ref = ins[3 * i:3 * i + 3]
            r, c = w_ref.shape
            g = red_ref[pl.ds(rows[i], r), pl.ds(pl.multiple_of(chip * c, LANES), c)] if cols[i] else red_ref[pl.ds(rows[i], r), :]
            outs[4 * i][...] = g
            outs[4 * i + 1][...], outs[4 * i + 2][...], outs[4 * i + 3][...] = _adam_update(w_ref[...], g, m_ref[...], v_ref[...])

    flat = [t for p in params for t in p]
    res = pl.pallas_call(
        body, name=name, out_shape=tuple(jax.ShapeDtypeStruct(p[0].shape, F32) for p in params for _ in range(4)),
        in_specs=[whole] * (1 + 3 * n), out_specs=(whole,) * (4 * n), compiler_params=_params(),
    )(red, *flat)
    return [res[4 * i:4 * i + 4] for i in range(n)]


def _cast_into_slot(name, w, chip, dtype, deps=()):
    r, c = w.shape
    tr = _row_tile(r, c)

    def body(chip_ref, w_ref, *rest):
        rest[-1][...] = w_ref[...].astype(dtype)

    grid_spec = pltpu.PrefetchScalarGridSpec(
        num_scalar_prefetch=1, grid=(r // tr,),
        in_specs=[pl.BlockSpec((tr, c), lambda j, chip_ref: (j, 0))] + [ANY] * len(deps),
        out_specs=pl.BlockSpec((None, tr, c), lambda j, chip_ref: (chip_ref[0], j, 0)))
    return pl.pallas_call(
        body, name=name, out_shape=jax.ShapeDtypeStruct((4, r, c), dtype), grid_spec=grid_spec, compiler_params=_params(1),
    )(chip, w, *deps)


def _cast_half_into_slot(name, w, chip_half, dtype, into=None):
    r, c = w.shape
    h = r // 2
    tr = _row_tile(h, c)
    nb = h // tr
    n_prev = 0 if into is None else 1

    def body(ids_ref, w_ref, *rest):
        rest[-1][...] = w_ref[...].astype(dtype)

    grid_spec = pltpu.PrefetchScalarGridSpec(
        num_scalar_prefetch=1, grid=(nb,),
        in_specs=[pl.BlockSpec((tr, c), lambda j, ids_ref: (ids_ref[1] * nb + j, 0))] + [ANY] * n_prev,
        out_specs=pl.BlockSpec((None, tr, c), lambda j, ids_ref: (ids_ref[0], ids_ref[1] * nb + j, 0)))
    return pl.pallas_call(
        body, name=name, out_shape=jax.ShapeDtypeStruct((4, r, c), dtype), grid_spec=grid_spec, compiler_params=_params(1),
        input_output_aliases={2: 0} if into is not None else {},
    )(chip_half, w, *([into] if into is not None else []))


def _place():
    return lax.axis_index("x"), lax.axis_index("y"), lax.axis_index("c")


def _chip_of(x, y, flip):
    px, py = x ^ flip[0], y ^ flip[1]
    return px, py, 2 * px + py


def _half(ref, which):
    rows = ref.shape[0] // 2
    return ref.at[pl.ds(which * rows, rows)]


HBM = pl.BlockSpec(memory_space=pltpu.HBM)
SEM = pl.BlockSpec(memory_space=pltpu.SEMAPHORE)
SPLIT_COPY = pltpu.CompilerParams(has_side_effects=pltpu.SideEffectType.DATAFLOW_SIDE_EFFECTING)


def _in_hbm(arrays):
    return [pltpu.with_memory_space_constraint(t, pltpu.HBM) for t in arrays]


TOKEN = jax.ShapeDtypeStruct((SMALL_ROWS, LANES), F32)
TOKEN_SPEC = pl.BlockSpec(memory_space=pltpu.VMEM)


NEIGHBOUR_FLIPS = CHIP_FLIPS[:2]


def _relay_chips(x, y, c):
    fx, fy = x ^ c, y ^ (1 - c)
    return (fx, fy), 2 * fx + fy, 2 * (1 - x) + (1 - y)


def _ag_start(name, slabs, deps=()):
    n = len(slabs)
    nn = len(NEIGHBOUR_FLIPS)

    def body(*refs):
        no = n + len(deps)
        ssem, rsem = refs[no], refs[no + 1]
        outs = refs[no + 2:no + 2 + n]
        token = refs[no + 2 + n]
        token[...] = jnp.zeros_like(token)
        x, y, c = _place()
        k = 2 * x + y
        for a in range(n):
            for j, flip in enumerate(NEIGHBOUR_FLIPS):
                px, py, _ = _chip_of(x, y, flip)
                mine = _half(outs[a].at[k], c)
                pltpu.make_async_remote_copy(src_ref=mine, dst_ref=mine, send_sem=ssem.at[a * nn + j],
                                             recv_sem=rsem.at[a * nn + j], device_id=(px, py, c), device_id_type=MESH).start()

    sem = pltpu.SemaphoreType.DMA((nn * n,))
    res = pl.pallas_call(
        body, name=name, out_shape=(sem, sem) + tuple(pltpu.HBM(t.shape, t.dtype) for t in slabs) + (TOKEN,),
        in_specs=[HBM] * n + [ANY] * len(deps), out_specs=tuple([SEM, SEM] + [HBM] * n + [TOKEN_SPEC]),
        input_output_aliases={a: 2 + a for a in range(n)}, compiler_params=SPLIT_COPY,
    )(*_in_hbm(slabs), *deps)
    return (res[0], res[1]), list(res[2:2 + n]), res[2 + n]


def _ag_relay(name, slabs, sems, after, then_start=()):
    n = len(slabs)
    m = len(then_start)
    nn = len(NEIGHBOUR_FLIPS)

    def body(*refs):
        no = n + 2 + m + len(after)
        ins = refs[:n]
        ssem, rsem = refs[n], refs[n + 1]
        r_s, r_r, p_s, p_r = refs[no:no + 4]
        x, y, c = _place()
        k = 2 * x + y
        (fx, fy), _, _ = _relay_chips(x, y, c)
        for a in range(n):
            for j, flip in enumerate(NEIGHBOUR_FLIPS):
                _, _, kj = _chip_of(x, y, flip)
                landed = _half(ins[a].at[kj], c)
                cp = pltpu.make_async_remote_copy(
                    src_ref=_half(ins[a].at[k], c), dst_ref=landed, send_sem=ssem.at[a * nn + j],
                    recv_sem=rsem.at[a * nn + j], device_id=(x, y, c), device_id_type=MESH)
                cp.wait_send()
                cp.wait_recv()
        for a in range(n):
            near = _half(ins[a].at[2 * (x ^ (1 - c)) + (y ^ c)], c)
            pltpu.make_async_remote_copy(src_ref=near, dst_ref=near, send_sem=r_s.at[a], recv_sem=r_r.at[a],
                                         device_id=(fx, fy, c), device_id_type=MESH).start()
            for j, flip in enumerate(NEIGHBOUR_FLIPS):
                _, _, kj = _chip_of(x, y, flip)
                landed = _half(ins[a].at[kj], c)
                pltpu.make_async_remote_copy(src_ref=landed, dst_ref=landed, send_sem=p_s.at[a * nn + j],
                                             recv_sem=p_r.at[a * nn + j], device_id=(x, y, 1 - c), device_id_type=MESH).start()
        if m:
            d_s, d_r = refs[no + 4 + n], refs[no + 5 + n]
            nxt = refs[no + 6 + n:]
            for a in range(m):
                for j, flip in enumerate(NEIGHBOUR_FLIPS):
                    px, py, _ = _chip_of(x, y, flip)
                    mine = _half(nxt[a].at[k], c)
                    pltpu.make_async_remote_copy(src_ref=mine, dst_ref=mine, send_sem=d_s.at[a * nn + j],
                                                 recv_sem=d_r.at[a * nn + j], device_id=(px, py, c), device_id_type=MESH).start()

    rsem_t = pltpu.SemaphoreType.DMA((n,))
    psem_t = pltpu.SemaphoreType.DMA((nn * n,))
    out_shape = (rsem_t, rsem_t, psem_t, psem_t) + tuple(pltpu.HBM(t.shape, t.dtype) for t in slabs)
    out_specs = [SEM] * 4 + [HBM] * n
    aliases = {a: 4 + a for a in range(n)}
    if m:
        dsem_t = pltpu.SemaphoreType.DMA((nn * m,))
        out_shape += (dsem_t, dsem_t) + tuple(pltpu.HBM(t.shape, t.dtype) for t in then_start)
        out_specs += [SEM, SEM] + [HBM] * m
        aliases.update({n + 2 + a: 4 + n + 2 + a for a in range(m)})
    res = pl.pallas_call(
        body, name=name, out_shape=out_shape, in_specs=[HBM] * n + [SEM, SEM] + [HBM] * m + [ANY] * len(after),
        out_specs=tuple(out_specs), input_output_aliases=aliases, compiler_params=SPLIT_COPY,
    )(*slabs, sems[0], sems[1], *_in_hbm(list(then_start)), *after)
    if not m:
        return tuple(res[:4]), list(res[4:])
    return (tuple(res[:4]), list(res[4:4 + n])), ((res[4 + n], res[5 + n]), list(res[6 + n:]))


def _wait_passes(ins, p_s, p_r, x, y, c):
    nn = len(NEIGHBOUR_FLIPS)
    for a in range(len(ins)):
        for j, flip in enumerate(NEIGHBOUR_FLIPS):
            _, _, kj = _chip_of(x, y, flip)
            cp = pltpu.make_async_remote_copy(
                src_ref=_half(ins[a].at[kj], c), dst_ref=_half(ins[a].at[kj], 1 - c), send_sem=p_s.at[a * nn + j],
                recv_sem=p_r.at[a * nn + j], device_id=(x, y, c), device_id_type=MESH)
            cp.wait_send()
            cp.wait_recv()


def _ag_relay_wait(name, slabs, sems, after):
    n = len(slabs)
    ns = len(sems)

    def body(*refs):
        no = n + ns + len(after)
        ins = refs[:n]
        r_s, r_r = refs[n], refs[n + 1]
        f_s, f_r = refs[no], refs[no + 1]
        x, y, c = _place()
        _, _, kd = _relay_chips(x, y, c)
        for a in range(n):
            near = _half(ins[a].at[2 * (x ^ (1 - c)) + (y ^ c)], c)
            cp = pltpu.make_async_remote_copy(src_ref=near, dst_ref=_half(ins[a].at[kd], c), send_sem=r_s.at[a],
                                              recv_sem=r_r.at[a], device_id=(x, y, c), device_id_type=MESH)
            cp.wait_send()
            cp.wait_recv()
        if ns == 4:
            _wait_passes(ins, refs[n + 2], refs[n + 3], x, y, c)
        for a in range(n):
            diag = _half(ins[a].at[kd], c)
            pltpu.make_async_remote_copy(src_ref=diag, dst_ref=diag, send_sem=f_s.at[a], recv_sem=f_r.at[a],
                                         device_id=(x, y, 1 - c), device_id_type=MESH).start()

    sem = pltpu.SemaphoreType.DMA((n,))
    res = pl.pallas_call(
        body, name=name, out_shape=(sem, sem) + tuple(pltpu.HBM(t.shape, t.dtype) for t in slabs),
        in_specs=[HBM] * n + [SEM] * ns + [ANY] * len(after), out_specs=tuple([SEM, SEM] + [HBM] * n),
        input_output_aliases={a: 2 + a for a in range(n)}, compiler_params=SPLIT_COPY,
    )(*slabs, *sems, *after)
    return (res[0], res[1]), list(res[2:])


def _ag_final_wait(name, slabs, sems, after, first=0):
    n = len(slabs)

    def body(*refs):
        ins = refs[:n]
        f_s, f_r = refs[n], refs[n + 1]
        x, y, c = _place()
        _, _, kd = _relay_chips(x, y, c)
        for a in range(n):
            cp = pltpu.make_async_remote_copy(
                src_ref=_half(ins[a].at[kd], c), dst_ref=_half(ins[a].at[kd], 1 - c), send_sem=f_s.at[first + a],
                recv_sem=f_r.at[first + a], device_id=(x, y, c), device_id_type=MESH)
            cp.wait_send()
            cp.wait_recv()

    return pl.pallas_call(
        body, name=name, out_shape=tuple(pltpu.HBM(t.shape, t.dtype) for t in slabs),
        in_specs=[HBM] * n + [SEM, SEM] + [ANY] * len(after), out_specs=tuple([HBM] * n),
        input_output_aliases={a: a for a in range(n)}, compiler_params=SPLIT_COPY,
    )(*slabs, sems[0], sems[1], *after)


def _sibling_part(ref, c, halves):
    if not halves:
        return ref
    h = ref.shape[1] // 2
    return ref.at[:, pl.ds((1 - c) * h, h)]


def _swap_start(name, grads, halves=True, deps=()):
    n = len(grads)

    def body(*refs):
        no = 2 * n + len(deps)
        ssem, rsem = refs[no], refs[no + 1]
        src, land = refs[no + 2:no + n + 2], refs[no + n + 2:no + 2 * n + 2]
        token = refs[no + 2 * n + 2]
        token[...] = jnp.zeros_like(token)
        x, y, c = _place()
        for a in range(n):
            pltpu.make_async_remote_copy(
                src_ref=_sibling_part(src[a], c, halves), dst_ref=land[a], send_sem=ssem.at[a], recv_sem=rsem.at[a],
                device_id=(x, y, 1 - c), device_id_type=MESH).start()

    zones = [lax.empty((g.shape[0], g.shape[1] // 2, g.shape[2]) if halves else g.shape, g.dtype) for g in grads]
    sem = pltpu.SemaphoreType.DMA((n,))
    res = pl.pallas_call(
        body, name=name,
        out_shape=(sem, sem) + tuple(pltpu.HBM(t.shape, t.dtype) for t in list(grads) + zones) + (TOKEN,),
        in_specs=[HBM] * (2 * n) + [ANY] * len(deps), out_specs=tuple([SEM, SEM] + [HBM] * (2 * n) + [TOKEN_SPEC]),
        input_output_aliases={i: 2 + i for i in range(2 * n)}, compiler_params=SPLIT_COPY,
    )(*_in_hbm(list(grads) + zones), *deps)
    return (res[0], res[1], list(res[2:2 + n]), list(res[2 + n:2 + 2 * n])), res[2 + 2 * n]


def _swap_wait(name, ssem, rsem, grads, zones, after, halves=True):
    n = len(grads)

    def body(*refs):
        src, land = refs[:n], refs[n:2 * n]
        ss, rs = refs[2 * n], refs[2 * n + 1]
        x, y, c = _place()
        for a in range(n):
            cp = pltpu.make_async_remote_copy(
                src_ref=_sibling_part(src[a], c, halves), dst_ref=land[a], send_sem=ss.at[a], recv_sem=rs.at[a],
                device_id=(x, y, c), device_id_type=MESH)
            cp.wait_send()
            cp.wait_recv()

    res = pl.pallas_call(
        body, name=name, out_shape=tuple(pltpu.HBM(t.shape, t.dtype) for t in list(grads) + list(zones)),
        in_specs=[HBM] * (2 * n) + [SEM, SEM] + [ANY] * len(after), out_specs=tuple([HBM] * (2 * n)),
        input_output_aliases={i: i for i in range(2 * n)}, compiler_params=SPLIT_COPY,
    )(*grads, *zones, ssem, rsem, *after)
    return list(res[:n]), list(res[n:])


def _scatter_start(name, parts):
    n = len(parts)
    nf = len(CHIP_FLIPS)

    def body(*refs):
        ssem, rsem = refs[2 * n], refs[2 * n + 1]
        src, land = refs[2 * n + 2:3 * n + 2], refs[3 * n + 2:4 * n + 2]
        token = refs[4 * n + 2]
        token[...] = jnp.zeros_like(token)
        x, y, c = _place()
        for a in range(n):
            for j, flip in enumerate(CHIP_FLIPS):
                px, py, kj = _chip_of(x, y, flip)
                pltpu.make_async_remote_copy(
                    src_ref=src[a].at[kj], dst_ref=land[a].at[j], send_sem=ssem.at[a * nf + j], recv_sem=rsem.at[a * nf + j],
                    device_id=(px, py, c), device_id_type=MESH).start()

    zones = [lax.empty((nf,) + p.shape[1:], p.dtype) for p in parts]
    sem = pltpu.SemaphoreType.DMA((nf * n,))
    res = pl.pallas_call(
        body, name=name,
        out_shape=(sem, sem) + tuple(pltpu.HBM(t.shape, t.dtype) for t in list(parts) + zones)
        + (jax.ShapeDtypeStruct((SMALL_ROWS, LANES), F32),),
        in_specs=[HBM] * (2 * n),
        out_specs=tuple([SEM, SEM] + [HBM] * (2 * n) + [pl.BlockSpec(memory_space=pltpu.VMEM)]),
        input_output_aliases={i: 2 + i for i in range(2 * n)}, compiler_params=SPLIT_COPY,
    )(*_in_hbm(list(parts) + zones))
    return (res[0], res[1], list(res[2:2 + n]), list(res[2 + n:2 + 2 * n])), res[2 + 2 * n]


def _scatter_wait(name, ssem, rsem, parts, zones, after):
    n = len(parts)
    nf = len(CHIP_FLIPS)

    def body(*refs):
        src, land = refs[:n], refs[n:2 * n]
        ss, rs = refs[2 * n], refs[2 * n + 1]
        x, y, c = _place()
        for a in range(n):
            for j, flip in enumerate(CHIP_FLIPS):
                _, _, kj = _chip_of(x, y, flip)
                cp = pltpu.make_async_remote_copy(
                    src_ref=src[a].at[kj], dst_ref=land[a].at[j], send_sem=ss.at[a * nf + j], recv_sem=rs.at[a * nf + j],
                    device_id=(x, y, c), device_id_type=MESH)
                cp.wait_send()
                cp.wait_recv()

    res = pl.pallas_call(
        body, name=name, out_shape=tuple(pltpu.HBM(t.shape, t.dtype) for t in list(parts) + list(zones)),
        in_specs=[HBM] * (2 * n) + [SEM, SEM] + [ANY] * len(after), out_specs=tuple([HBM] * (2 * n)),
        input_output_aliases={i: i for i in range(2 * n)}, compiler_params=SPLIT_COPY,
    )(*parts, *zones, ssem, rsem, *after)
    return list(res[:n]), list(res[n:])


N_PEERS = 7


def _peer(x, y, c, mask):
    px, py, pc = x ^ ((mask >> 2) & 1), y ^ ((mask >> 1) & 1), c ^ (mask & 1)
    return (px, py, pc), 4 * px + 2 * py + pc


def _reduce_start(vec, deps):
    nd = len(deps)

    def body(*refs):
        ssem, rsem, src, land, token = refs[2 + nd:]
        token[...] = jnp.zeros_like(token)
        x, y, c = _place()
        me = 4 * x + 2 * y + c
        for mask in range(1, N_PEERS + 1):
            to, _ = _peer(x, y, c, mask)
            pltpu.make_async_remote_copy(src_ref=src, dst_ref=land.at[me], send_sem=ssem.at[mask - 1],
                                         recv_sem=rsem.at[mask - 1], device_id=to, device_id_type=MESH).start()

    zone = lax.empty((N_PEERS + 1,) + vec.shape, vec.dtype)
    sem = pltpu.SemaphoreType.DMA((N_PEERS,))
    res = pl.pallas_call(
        body, name="reduce_start",
        out_shape=(sem, sem, pltpu.HBM(vec.shape, vec.dtype), pltpu.HBM(zone.shape, zone.dtype), TOKEN),
        in_specs=[HBM, HBM] + [ANY] * nd, out_specs=(SEM, SEM, HBM, HBM, TOKEN_SPEC),
        input_output_aliases={0: 2, 1: 3}, compiler_params=SPLIT_COPY,
    )(*_in_hbm([vec, zone]), *deps)
    return res[:4], res[4]


def _reduce_wait(ssem, rsem, vec, zone, after):
    def body(src, land, ss, rs, *_):
        x, y, c = _place()
        for mask in range(1, N_PEERS + 1):
            _, frm = _peer(x, y, c, mask)
            cp = pltpu.make_async_remote_copy(src_ref=src, dst_ref=land.at[frm], send_sem=ss.at[mask - 1],
                                              recv_sem=rs.at[mask - 1], device_id=(x, y, c), device_id_type=MESH)
            cp.wait_send()
            cp.wait_recv()

    return pl.pallas_call(
        body, name="reduce_wait", out_shape=(pltpu.HBM(vec.shape, vec.dtype), pltpu.HBM(zone.shape, zone.dtype)),
        in_specs=[HBM, HBM, SEM, SEM] + [ANY] * len(after), out_specs=(HBM, HBM),
        input_output_aliases={0: 0, 1: 1}, compiler_params=SPLIT_COPY,
    )(vec, zone, ssem, rsem, *after)


def _reduce_sum(vec, zone, me, loss_row, loss_scale):
    r, dm = vec.shape

    def body(me_ref, v_ref, z_ref, o_ref, l_ref):
        acc = None
        for i in range(N_PEERS + 1):
            term = jnp.where(me_ref[0] == i, v_ref[...], z_ref[i])
            acc = term if acc is None else acc + term
        o_ref[...] = acc
        l_ref[...] = jnp.sum(acc[loss_row:loss_row + SMALL_ROWS, :], axis=(0, 1), keepdims=True) * loss_scale

    grid_spec = pltpu.PrefetchScalarGridSpec(
        num_scalar_prefetch=1, grid=(1,),
        in_specs=[pl.BlockSpec((r, dm), lambda i, me_ref: (0, 0)), pl.BlockSpec((N_PEERS + 1, r, dm), lambda i, me_ref: (0, 0, 0))],
        out_specs=(pl.BlockSpec((r, dm), lambda i, me_ref: (0, 0)), pl.BlockSpec((1, 1), lambda i, me_ref: (0, 0))))
    return pl.pallas_call(
        body, name="reduce_sum", out_shape=(jax.ShapeDtypeStruct((r, dm), F32), jax.ShapeDtypeStruct((1, 1), F32)),
        grid_spec=grid_spec, compiler_params=_params(1),
    )(me, vec, zone)


def kernel(x, meta_tokens, norm_mix_g, w_in, b_gate, pool_w, pool_scale, conv_w, conv_out_w, w_o, norm_ffn_g, w_gate_up, w_down, norm_final_g, loss_target, m_meta_tokens, m_norm_mix_g, m_w_in, m_b_gate, m_pool_w, m_pool_scale, m_conv_w, m_conv_out_w, m_w_o, m_norm_ffn_g, m_w_gate_up, m_w_down, m_norm_final_g, v_meta_tokens, v_norm_mix_g, v_w_in, v_b_gate, v_pool_w, v_pool_scale, v_conv_w, v_conv_out_w, v_w_o, v_norm_ffn_g, v_w_gate_up, v_w_down, v_norm_final_g):
    seq, dm = x.shape[1], x.shape[2]
    tail = TAIL_ROWS
    lp = seq + tail
    tm_row = _row_tile(lp, dm, 4, 3 * 1024 * 1024)
    tm_seq = _row_tile(seq, dm, 4, 3 * 1024 * 1024)
    n_chips = 4
    n_groups = len(POOL_WINDOWS)
    gw = dm // n_groups
    tc = min(256, gw)
    cx, cy, cc = _place()
    chip = 2 * cx + cy
    dloc = dm // n_chips

    pool2 = pool_w.reshape(n_groups * pool_w.shape[1], gw)
    big = {"w_in": w_in, "w_gate_up": w_gate_up, "pool_w": pool2, "conv_out_w": conv_out_w, "w_o": w_o, "w_down": w_down}
    chip1 = jnp.reshape(chip, (1,)).astype(jnp.int32)
    core = jnp.reshape(cc, (1,)).astype(jnp.int32)
    small_loc = jnp.concatenate([meta_tokens, jnp.pad(conv_w, ((0, 8 - conv_w.shape[0]), (0, 0))),
                                 jnp.zeros((8, dloc), F32)], axis=0)
    g1, g2, g3 = norm_mix_g.reshape(1, dm), norm_ffn_g.reshape(1, dm), norm_final_g.reshape(1, dm)
    b_gate2 = b_gate.reshape(2, dm)
    ps = pool_scale.reshape(1, dm)
    mine = jnp.stack([chip, cc]).astype(jnp.int32)
    other = jnp.stack([chip, 1 - cc]).astype(jnp.int32)
    first = [_cast_into_slot("place_small", small_loc, chip1, F32), _cast_half_into_slot("cast_w_in_sent", w_in, mine, BF16)]
    sems, first, token = _ag_start("ag_start_first", first)
    first[1] = _cast_half_into_slot("cast_w_in_kept", w_in, other, BF16, into=first[1])
    cast = {nme: _cast_into_slot("cast_" + nme, big[nme], chip1, BF16, deps=(token,))
            for nme in ["pool_w", "conv_out_w", "w_o", "w_gate_up", "w_down"]}
    sems, first = _ag_relay("ag_relay_first", first, sems, list(cast.values()))
    sems, (small4, w_in4) = _ag_relay_wait("ag_relay_wait_first", first, sems, [])
    (small4,) = _ag_final_wait("ag_final_wait_small", [small4], sems, [])
    mixer_w = [cast["pool_w"], cast["conv_out_w"], cast["w_o"]]
    sems_mix, mixer_w, token = _ag_start("ag_start_mixer", mixer_w, deps=(small4,))
    sems_gu, (w_gu4,), token = _ag_start("ag_start_gate_up", [cast["w_gate_up"]], deps=(token,))

    small_f = jnp.transpose(small4, (1, 0, 2)).reshape(small4.shape[1], dm)
    meta_f = small_f[:N_META]
    conv_w_f = small_f[N_META:N_META + 3]
    tail_rows = jnp.concatenate([jnp.zeros((tail - N_META, dm), F32), meta_f], axis=0)
    h0_hn1 = _rms_fwd_into("rms_mix", x[0], g1, lp, 0, tm_seq, deps=(token,))
    h0, hn1 = _rms_fwd_into("rms_mix_tail", tail_rows, g1, lp, seq, tail, prev=h0_hn1)
    (w_in4,) = _ag_final_wait("ag_final_wait_first", [w_in4], sems, [hn1], first=1)
    proj = _nn_sharded("proj", hn1, w_in4, 6)
    sems_mix, mixer_w = _ag_relay("ag_relay_mixer", mixer_w, sems_mix, [proj])
    (sems_gu, (w_gu4,)), (sems_down, (w_down4,)) = _ag_relay("ag_relay_gate_up", [w_gu4], sems_gu, [mixer_w[0]],
                                                              then_start=[cast["w_down"]])
    pooled, z = _mixer_fwd("mixer_fwd", proj, conv_w_f, tc, w_down4)
    sems_mix, mixer_w = _ag_relay_wait("ag_relay_wait_mixer", mixer_w, sems_mix, [pooled])
    pool4, conv_out4, w_o4 = _ag_final_wait("ag_final_wait_mixer", mixer_w, sems_mix, [])
    pool_f = jnp.transpose(pool4.reshape(n_chips, n_groups, gw // n_chips, gw), (1, 0, 2, 3)).reshape(n_groups, gw, gw)
    conv_out_f = conv_out4.reshape(dm, dm)
    w_o_f = w_o4.reshape(dm, dm)
    ya = _pool_fwd("pool_proj", pooled, pool_f)
    yb = _nn_plain("conv_out", z, conv_out_f, BF16)
    mix = _gate_mix("gate_mix", proj, b_gate2, ya, ps, yb, tm_row)
    sems_gu, (w_gu4,) = _ag_relay_wait("ag_relay_wait_gate_up", [w_gu4], sems_gu, [mix])
    h1 = _nn_plain("attn_out", mix, w_o_f, F32, res=h0, tn_pref=256)
    (w_gu4,) = _ag_final_wait("ag_final_wait_gate_up", [w_gu4], sems_gu, [h1])
    hn2 = _rms_fwd("rms_ffn", h1, g2, tm_row)
    sems_down, (w_down4,) = _ag_relay("ag_relay_down", [w_down4], sems_down, [hn2])
    gu, act = _gate_up_swiglu("gate_up", hn2, w_gu4, w_down4)
    sems_down, (w_down4,) = _ag_relay_wait("ag_relay_wait_down", [w_down4], sems_down, [act])
    (w_down4,) = _ag_final_wait("ag_final_wait_down", [w_down4], sems_down, [])
    w_down_f = w_down4.reshape(-1, dm)
    h2 = _nn_rows("ffn_down", act, w_down_f, h1)
    dh2, dh2b, loss_cols, dg3 = _final_loss("final_loss", h2, g3, loss_target[0], tm_seq)
    dh2, dh2b = _zero_tail("final_loss_tail", [dh2, dh2b], tail)

    def scatter(tag, names_g, swap, after):
        grads_g, got = _swap_wait("swap_wait_" + tag, *swap, [after])
        pairs = [_pair_add("pair_add_" + nme, g4, rv, core) for nme, g4, rv in zip(names_g, grads_g, got)]
        return _scatter_start("scatter_start_" + tag, pairs)

    dgu = _dact_swiglu_bwd("d_gate_up", dh2b, w_down_f, gu)
    gw_down = _tn_plain("dw_down", act, dh2b)
    gw_gu = _tn_sharded("dw_gate_up", hn2, dgu, n_chips)
    swap_a, token = _swap_start("swap_start_a", [gw_gu, gw_down.reshape(n_chips, -1, dm)])
    dhn2 = _nt_sharded("d_hn2", dgu, w_gu4, tr_pref=2816, row_tiles=2, deps=(token,))
    flight_a, token = scatter("a", ["w_gate_up", "w_down"], swap_a, dhn2)
    dh1, dh1b, dg2 = _rms_bwd("rms_ffn_bwd", dhn2, h1, g2, dh2, tm_row, token)
    dmix = _nt_plain("d_mix", dh1b, w_o_f)
    gw_o = _tn_plain("dw_o", mix, dh1b)
    dproj, dyb, dya, db_gate, dps = _gate_bwd("gate_bwd", dmix, proj, b_gate2, ya, ps, yb, tm_row)
    gw_conv_out = _tn_plain("dw_conv_out", z, dyb)
    gw_pool = _pool_bwd_w("dw_pool", pooled, dya)
    gw_pool = jnp.transpose(gw_pool.reshape(n_groups, n_chips, gw // n_chips, gw), (1, 0, 2, 3))
    swap_b, token = _swap_start("swap_start_b", [gw_o.reshape(n_chips, dloc, dm), gw_conv_out.reshape(n_chips, dloc, dm),
                                                 gw_pool.reshape(n_chips, n_groups * (gw // n_chips), gw)])
    dpooled = _pool_bwd_act("d_pooled", dya, pool_f, deps=(token,))
    dz = _nt_plain("d_z", dyb, conv_out_f)
    flight_b, token = scatter("b", ["w_o", "conv_out_w", "pool_w"], swap_b, dz)
    dproj, dconv_w = _mixer_bwd("mixer_bwd", dz, dpooled, proj, conv_w_f, dproj, tc, token)
    gw_in0 = _tn_sharded("dw_in_0", hn1, dproj, n_chips, part=(0, 2))
    swap_c0, token = _swap_start("swap_start_c0", [gw_in0])
    gw_in1 = _tn_sharded("dw_in_1", hn1, dproj, n_chips, part=(1, 2), deps=(token,))
    flight_c0, token = scatter("c0", ["w_in_0"], swap_c0, gw_in1)
    swap_c, token = _swap_start("swap_start_c", [gw_in1], deps=(token,))
    groups_g = {"a": [("w_gate_up", (0, 1)), ("w_down", (0, 1))], "b": [("w_o", (0, 1)), ("conv_out_w", (0, 1)), ("pool_w", (0, 1))],
                "c0": [("w_in", (0, 2))], "c": [("w_in", (1, 2))]}

    def reduced(tag, flight, after):
        pairs, zones = _scatter_wait("scatter_wait_" + tag, *flight, after)
        halves = [_chip_sum("chip_sum_%s_%d" % (nme, part[0]), p, rv, chip1) for (nme, part), p, rv in zip(groups_g[tag], pairs, zones)]
        return _swap_start("send_start_" + tag, halves, halves=False)

    send_a, token = reduced("a", flight_a, [token])
    flight_c, token = scatter("c", ["w_in_1"], swap_c, token)
    dhn1 = _nt_in_proj("d_hn1", dproj, w_in4, deps=(token,))
    dx, dg1 = _rms_bwd_rows("rms_mix_bwd", dhn1, h0, g1, dh1, 0, seq, tm_seq, token)
    dtail, dg1 = _rms_bwd_rows("rms_mix_bwd_tail", dhn1, h0, g1, dh1, seq, tail, tail, dx, dg_prev=dg1)
    grad_x = dx[None]
    dmeta = dtail[tail - N_META:]

    given = dict(meta_tokens=(meta_tokens, m_meta_tokens, v_meta_tokens), norm_mix_g=(norm_mix_g, m_norm_mix_g, v_norm_mix_g),
                 w_in=(w_in, m_w_in, v_w_in), b_gate=(b_gate, m_b_gate, v_b_gate), pool_w=(pool_w, m_pool_w, v_pool_w),
                 pool_scale=(pool_scale, m_pool_scale, v_pool_scale), conv_w=(conv_w, m_conv_w, v_conv_w),
                 conv_out_w=(conv_out_w, m_conv_out_w, v_conv_out_w), w_o=(w_o, m_w_o, v_w_o),
                 norm_ffn_g=(norm_ffn_g, m_norm_ffn_g, v_norm_ffn_g), w_gate_up=(w_gate_up, m_w_gate_up, v_w_gate_up),
                 w_down=(w_down, m_w_down, v_w_down), norm_final_g=(norm_final_g, m_norm_final_g, v_norm_final_g))
    order = list(given.keys())
    grad, delta, new_m, new_v = {}, {}, {}, {}
    vec = jnp.concatenate([dg1, dg2, dg3, db_gate, dps, loss_cols, dconv_w, dmeta], axis=0)
    loss_row = 5 * SMALL_ROWS
    results = {}

    def update(tag, send, after):
        halves, sib_halves = _swap_wait("send_wait_" + tag, *send, after, halves=False)
        deltas = []
        for (nme, part), g_own, g_sib in zip(groups_g[tag], halves, sib_halves):
            w, m, v = given[nme]
            shape2 = (2 * g_own.shape[0] * part[1], g_own.shape[1])
            results[nme] = _adamw_halves("adamw_%s_%d" % (nme, part[0]), w.reshape(shape2), g_own, g_sib, m.reshape(shape2),
                                         v.reshape(shape2), core, part=part, prev=results.get(nme))
            grad[nme], delta[nme], new_m[nme], new_v[nme] = [t.reshape(w.shape) for t in results[nme]]
            deltas.append(results[nme][1])
        return deltas

    done_a = update("a", send_a, [dx])
    send_b, token = reduced("b", flight_b, done_a)
    send_c0, token = reduced("c0", flight_c0, [token])
    done_b = update("b", send_b, [token])
    send_c, token = reduced("c", flight_c, done_b)
    me1 = jnp.reshape(4 * cx + 2 * cy + cc, (1,)).astype(jnp.int32)
    red_flight, token = _reduce_start(vec, [token])
    done_c0 = update("c0", send_c0, [token])
    done_c = update("c", send_c, done_c0)
    red, loss11 = _reduce_sum(*_reduce_wait(*red_flight, done_c), me1, loss_row, 0.5 / dm)
    loss = loss11[0, 0]

    small = {"norm_mix_g": (0, False), "norm_ffn_g": (SMALL_ROWS, False), "norm_final_g": (2 * SMALL_ROWS, False),
             "b_gate": (3 * SMALL_ROWS, False), "pool_scale": (4 * SMALL_ROWS, False), "conv_w": (6 * SMALL_ROWS, True),
             "meta_tokens": (7 * SMALL_ROWS, True)}

    def two_d(t, nme):
        return t if t.ndim == 2 else t.reshape(2 if nme == "b_gate" else 1, dm)

    res = _adamw_small("adamw_small", red, [small[nme][0] for nme in small], [small[nme][1] for nme in small],
                       [[two_d(t, nme) for t in given[nme]] for nme in small])
    for nme, res4 in zip(small, res):
        grad[nme], delta[nme], new_m[nme], new_v[nme] = [t.reshape(given[nme][0].shape) for t in res4]
    return (loss, grad_x, *[grad[nme] for nme in order], *[delta[nme] for nme in order],
            *[new_m[nme] for nme in order], *[new_v[nme] for nme in order])
```

```python
import math

import jax
import jax.numpy as jnp
from jax import lax
from jax.experimental import pallas as pl
from jax.experimental.pallas import tpu as pltpu

F32 = jnp.float32
BF16 = jnp.bfloat16
N_META = 16
POOL_WINDOWS = (2, 4, 8, 16)
EPS = 1e-6
ADAM_LR, ADAM_B1, ADAM_B2, ADAM_EPS, ADAM_WD, ADAM_STEP = 0.001, 0.9, 0.999, 1e-08, 0.01, 10
LANES = 128
V7X_VMEM_BYTES = 64 * 1024 * 1024
VMEM_LIMIT = V7X_VMEM_BYTES - 8 * 1024 * 1024
MESH = pl.DeviceIdType.MESH
ANY = pl.BlockSpec(memory_space=pl.ANY)
CHIP_FLIPS = ((1, 0), (0, 1), (1, 1))
SMALL_ROWS = 8
TAIL_ROWS = 32


def _pick(n, pref):
    best = None
    for t in range(LANES, min(n, pref) + 1, LANES):
        if n % t == 0:
            best = t
    assert best is not None, (n, pref)
    return best


def _params(n_axes=0):
    sem = ("arbitrary",) * n_axes if n_axes else None
    return pltpu.CompilerParams(dimension_semantics=sem, vmem_limit_bytes=VMEM_LIMIT)


_DIMS = {
    "nn": (((1,), (0,)), ((), ())),
    "nt": (((1,), (1,)), ((), ())),
    "tn": (((0,), (0,)), ((), ())),
}


def _matmul(name, mode, a, b, out_sds, grid, a_spec, b_spec, o_spec, nk, res=None, res_spec=None, acc_shape=None, deps=()):
    out_dtype = out_sds.dtype
    in_place = nk > 1 and out_dtype == F32
    use_scratch = nk > 1 and not in_place
    rows = a_spec.block_shape[-2] if mode != "tn" else None
    chunk = _row_tile(rows, 1, 1, 1152) if rows is not None else None
    n_in = 2 + (res is not None) + len(deps)

    def body(*refs):
        a_ref, b_ref = refs[:2]
        r_ref = refs[2] if res is not None else None
        o_ref, *scr = refs[n_in:]
        k = pl.program_id(len(grid) - 1) if nk > 1 else None

        def emit(sl):
            if sl is None:
                part = lax.dot_general(a_ref[...], b_ref[...], _DIMS[mode], preferred_element_type=F32)
                idx = (slice(None), slice(None))
            else:
                part = lax.dot_general(a_ref[sl, :], b_ref[...], _DIMS[mode], preferred_element_type=F32)
                idx = (sl, slice(None))
            if nk == 1:
                if r_ref is not None:
                    part = part + r_ref[idx]
                o_ref[idx] = part.astype(out_dtype)
                return
            acc = scr[0] if use_scratch else o_ref

            @pl.when(k == 0)
            def _():
                first = part
                if r_ref is not None and in_place:
                    first = first + r_ref[idx]
                acc[idx] = first

            @pl.when(k > 0)
            def _():
                acc[idx] += part

            if use_scratch:

                @pl.when(k == nk - 1)
                def _():
                    o_ref[idx] = acc[idx].astype(out_dtype)

        if mode == "tn" or chunk == rows:
            emit(None)
        else:
            for m0 in range(0, rows, chunk):
                emit(pl.ds(m0, chunk))

    ins = [a, b] + ([res] if res is not None else []) + list(deps)
    in_specs = [a_spec, b_spec] + ([res_spec] if res is not None else []) + [ANY] * len(deps)
    scratch = [pltpu.VMEM(acc_shape, F32)] if use_scratch else []
    return pl.pallas_call(
        body, name=name, out_shape=out_sds, grid=grid, in_specs=in_specs, out_specs=o_spec,
        scratch_shapes=scratch, compiler_params=_params(len(grid)),
    )(*ins)


def _nn_sharded(name, a, w4, nseg):
    lp, kdim = a.shape
    s, _, nloc = w4.shape
    segw = s * nloc // nseg
    tn = _pick(math.gcd(nloc, segw), 1536)
    bw, bo = nloc // tn, segw // tn
    return _matmul(
        name, "nn", a, w4, jax.ShapeDtypeStruct((nseg, lp, segw), BF16), (s * bw,),
        pl.BlockSpec((lp, kdim), lambda j: (0, 0)),
        pl.BlockSpec((None, kdim, tn), lambda j: (j // bw, 0, j % bw)),
        pl.BlockSpec((None, lp, tn), lambda j: (j // bo, 0, j % bo)), 1)


def _nt_in_proj(name, dseg, w4, row_tiles=2, to_pref=1024, deps=()):
    nseg, lp, segw = dseg.shape
    s, kdim, nloc = w4.shape
    assert nseg * segw == s * nloc and 2 * nloc == 3 * segw, (dseg.shape, w4.shape)
    half = segw // 2
    to = _pick(kdim, to_pref)
    tm = lp // row_tiles

    def body(full_ref, half_ref, w_ref, *rest):
        o_ref = rest[len(deps)]
        r = pl.program_id(2)

        def contribution(full_first):
            lo, hi = (pl.ds(0, segw), pl.ds(segw, half)) if full_first else (pl.ds(half, segw), pl.ds(0, half))
            return (lax.dot_general(full_ref[...], w_ref[:, lo], _DIMS["nt"], preferred_element_type=F32)
                    + lax.dot_general(half_ref[...], w_ref[:, hi], _DIMS["nt"], preferred_element_type=F32))

        @pl.when(r == 0)
        def _():
            o_ref[...] = contribution(True)

        for ri in range(1, s):

            @pl.when(r == ri)
            def _(ri=ri):
                o_ref[...] += contribution(ri % 2 == 0)

    return pl.pallas_call(
        body, name=name, out_shape=jax.ShapeDtypeStruct((lp, kdim), F32), grid=(row_tiles, kdim // to, s),
        in_specs=[pl.BlockSpec((None, tm, segw), lambda m, j, r: ((3 * r + 1) // 2, m, 0)),
                  pl.BlockSpec((None, tm, half), lambda m, j, r: (1 + 3 * (r // 2), m, r % 2)),
                  pl.BlockSpec((None, to, nloc), lambda m, j, r: (r, j, 0))] + [ANY] * len(deps),
        out_specs=pl.BlockSpec((tm, to), lambda m, j, r: (m, j)), compiler_params=_params(3),
    )(dseg, dseg, w4, *deps)


def _nn_plain(name, a, w, out_dtype, res=None, tn_pref=512, tk_pref=2048, deps=()):
    lp, kdim = a.shape
    n = w.shape[1]
    tn = _pick(n, tn_pref)
    tk = kdim if kdim <= tk_pref else _pick(kdim, tk_pref)
    nk = kdim // tk
    grid = (n // tn, nk) if nk > 1 else (n // tn,)
    if nk > 1:
        a_spec = pl.BlockSpec((lp, tk), lambda j, k: (0, k))
        w_spec = pl.BlockSpec((tk, tn), lambda j, k: (k, j))
        o_spec = pl.BlockSpec((lp, tn), lambda j, k: (0, j))
    else:
        a_spec = pl.BlockSpec((lp, tk), lambda j: (0, 0))
        w_spec = pl.BlockSpec((tk, tn), lambda j: (0, j))
        o_spec = pl.BlockSpec((lp, tn), lambda j: (0, j))
    return _matmul(name, "nn", a, w, jax.ShapeDtypeStruct((lp, n), out_dtype), grid, a_spec, w_spec, o_spec, nk,
                   res=res, res_spec=o_spec if res is not None else None, acc_shape=(lp, tn), deps=deps)


def _nt_plain(name, a, w, tn_pref=512):
    lp, kdim = a.shape
    n = w.shape[0]
    tn = _pick(n, tn_pref)
    return _matmul(
        name, "nt", a, w, jax.ShapeDtypeStruct((lp, n), BF16), (n // tn,),
        pl.BlockSpec((lp, kdim), lambda j: (0, 0)),
        pl.BlockSpec((tn, kdim), lambda j: (j, 0)),
        pl.BlockSpec((lp, tn), lambda j: (0, j)), 1)


def _nt_sharded(name, dseg, w4, to_pref=1024, tr_pref=1536, row_tiles=1, deps=()):
    nseg, lp, segw = dseg.shape
    s, kdim, nloc = w4.shape
    tr = _pick(math.gcd(nloc, segw), tr_pref)
    ba, bw = segw // tr, nloc // tr
    nr = s * bw
    to = _pick(kdim, to_pref)
    tm = lp // row_tiles
    return _matmul(
        name, "nt", dseg, w4, jax.ShapeDtypeStruct((lp, kdim), F32), (row_tiles, kdim // to, nr),
        pl.BlockSpec((None, tm, tr), lambda m, j, r: (r // ba, m, r % ba)),
        pl.BlockSpec((None, to, tr), lambda m, j, r: (r // bw, j, r % bw)),
        pl.BlockSpec((tm, to), lambda m, j, r: (m, j)), nr, deps=deps)


def _nn_rows(name, a, w, res, row_tiles=2, tn_pref=512):
    lp, kdim = a.shape
    n = w.shape[1]
    tn = _pick(n, tn_pref)
    tm = lp // row_tiles
    blk = pl.BlockSpec((tm, tn), lambda i, j: (i, j))
    return _matmul(name, "nn", a, w, jax.ShapeDtypeStruct((lp, n), F32), (row_tiles, n // tn),
                   pl.BlockSpec((tm, kdim), lambda i, j: (i, 0)), pl.BlockSpec((kdim, tn), lambda i, j: (0, j)), blk, 1,
                   res=res, res_spec=blk)


def _tn_plain(name, a, d, tk_pref=1024):
    lp, kdim = a.shape
    n = d.shape[1]
    tk = _pick(kdim, tk_pref)
    return _matmul(
        name, "tn", a, d, jax.ShapeDtypeStruct((kdim, n), BF16), (kdim // tk,),
        pl.BlockSpec((lp, tk), lambda i: (0, i)),
        pl.BlockSpec((lp, n), lambda i: (0, 0)),
        pl.BlockSpec((tk, n), lambda i: (i, 0)), 1)


def _tn_sharded(name, a, dseg, s, part=(0, 1), tk_pref=1024, deps=()):
    lp, kdim = a.shape
    nseg, _, segw = dseg.shape
    nloc = nseg * segw // s
    tn = _pick(math.gcd(nloc, segw), 1536)
    bd, bo = segw // tn, nloc // tn
    kpart = kdim // part[1]
    tk = _pick(kpart, tk_pref)
    i0 = part[0] * (kpart // tk)

    def body(a_ref, d_ref, *rest):
        o_ref, at_ref = rest[len(deps):]

        @pl.when(pl.program_id(1) == 0)
        def _():
            at_ref[...] = a_ref[...].T

        o_ref[...] = jnp.dot(at_ref[...], d_ref[...], preferred_element_type=F32).astype(BF16)

    return pl.pallas_call(
        body, name=name, out_shape=jax.ShapeDtypeStruct((s, kpart, nloc), BF16), grid=(kpart // tk, s * bo),
        in_specs=[pl.BlockSpec((lp, tk), lambda i, j: (0, i0 + i)),
                  pl.BlockSpec((None, lp, tn), lambda i, j: (j // bd, 0, j % bd))] + [ANY] * len(deps),
        out_specs=pl.BlockSpec((None, tk, tn), lambda i, j: (j // bo, i, j % bo)),
        scratch_shapes=[pltpu.VMEM((tk, lp), BF16)], compiler_params=_params(2),
    )(a, dseg, *deps)


def _silu_parts(gt):
    sg = jax.nn.sigmoid(gt)
    return gt * sg, sg * (1.0 + gt * (1.0 - sg))


def _gate_up_swiglu(name, a, w4, dep, tn_pref=256):
    lp, kdim = a.shape
    s, _, nloc = w4.shape
    f = s * nloc // 2
    tn = _pick(nloc, tn_pref)
    bw = nloc // tn
    chunk = _row_tile(lp, 1, 1, 576)

    def body(a_ref, wg_ref, wu_ref, _, fac_ref, act_ref):
        for m0 in range(0, lp, chunk):
            sl = pl.ds(m0, chunk)
            gt = jnp.dot(a_ref[sl, :], wg_ref[...], preferred_element_type=F32)
            up = jnp.dot(a_ref[sl, :], wu_ref[...], preferred_element_type=F32)
            silu, dsilu = _silu_parts(gt)
            fac_ref[0, sl, :] = (up * dsilu).astype(BF16)
            fac_ref[1, sl, :] = silu.astype(BF16)
            act_ref[sl, :] = (silu * up).astype(BF16)

    return pl.pallas_call(
        body, name=name, grid=(f // tn,),
        out_shape=(jax.ShapeDtypeStruct((2, lp, f), BF16), jax.ShapeDtypeStruct((lp, f), BF16)),
        in_specs=[pl.BlockSpec((lp, kdim), lambda j: (0, 0)),
                  pl.BlockSpec((None, kdim, tn), lambda j: (j // bw, 0, j % bw)),
                  pl.BlockSpec((None, kdim, tn), lambda j: (s // 2 + j // bw, 0, j % bw)), ANY],
        out_specs=(pl.BlockSpec((2, lp, tn), lambda j: (0, 0, j)), pl.BlockSpec((lp, tn), lambda j: (0, j))),
        compiler_params=_params(1),
    )(a, w4, w4, dep)


def _dact_swiglu_bwd(name, d, w, gu, tn_pref=512):
    lp, dm = d.shape
    f = w.shape[0]
    tn = _pick(f, tn_pref)
    chunk = _row_tile(lp, 1, 1, 576)

    def body(d_ref, w_ref, g_ref, u_ref, o_ref):
        for m0 in range(0, lp, chunk):
            sl = pl.ds(m0, chunk)
            dact = lax.dot_general(d_ref[sl, :], w_ref[...], _DIMS["nt"], preferred_element_type=F32)
            o_ref[0, sl, :] = (dact * g_ref[sl, :].astype(F32)).astype(BF16)
            o_ref[1, sl, :] = (dact * u_ref[sl, :].astype(F32)).astype(BF16)

    return pl.pallas_call(
        body, name=name, grid=(f // tn,), out_shape=jax.ShapeDtypeStruct((2, lp, f), BF16),
        in_specs=[pl.BlockSpec((lp, dm), lambda j: (0, 0)), pl.BlockSpec((tn, dm), lambda j: (j, 0)),
                  pl.BlockSpec((None, lp, tn), lambda j: (0, 0, j)), pl.BlockSpec((None, lp, tn), lambda j: (1, 0, j))],
        out_specs=pl.BlockSpec((2, lp, tn), lambda j: (0, 0, j)), compiler_params=_params(1),
    )(d, w, gu, gu)


def _pool_fwd(name, pooled, pw):
    lp, dm = pooled.shape
    g, gw, _ = pw.shape
    return _matmul(
        name, "nn", pooled, pw, jax.ShapeDtypeStruct((lp, dm), BF16), (g,),
        pl.BlockSpec((lp, gw), lambda gi: (0, gi)), pl.BlockSpec((None, gw, gw), lambda gi: (gi, 0, 0)),
        pl.BlockSpec((lp, gw), lambda gi: (0, gi)), 1)


def _pool_bwd_act(name, dya, pw, deps=()):
    lp, dm = dya.shape
    g, gw, _ = pw.shape
    return _matmul(
        name, "nt", dya, pw, jax.ShapeDtypeStruct((lp, dm), BF16), (g,),
        pl.BlockSpec((lp, gw), lambda gi: (0, gi)), pl.BlockSpec((None, gw, gw), lambda gi: (gi, 0, 0)),
        pl.BlockSpec((lp, gw), lambda gi: (0, gi)), 1, deps=deps)


def _pool_bwd_w(name, pooled, dya):
    lp, dm = pooled.shape
    g = len(POOL_WINDOWS)
    gw = dm // g
    return _matmul(
        name, "tn", pooled, dya, jax.ShapeDtypeStruct((g, gw, gw), BF16), (g,),
        pl.BlockSpec((lp, gw), lambda gi: (0, gi)), pl.BlockSpec((lp, gw), lambda gi: (0, gi)),
        pl.BlockSpec((None, gw, gw), lambda gi: (gi, 0, 0)), 1)


def _rms_fwd(name, h, g, tm, deps=()):
    lp, dm = h.shape

    def body(h_ref, g_ref, *rest):
        hv = h_ref[...]
        r = lax.rsqrt(jnp.mean(hv * hv, axis=-1, keepdims=True) + EPS)
        rest[-1][...] = (hv * r * g_ref[...]).astype(BF16)

    row = pl.BlockSpec((tm, dm), lambda i: (i, 0))
    return pl.pallas_call(
        body, name=name, out_shape=jax.ShapeDtypeStruct((lp, dm), BF16), grid=(lp // tm,),
        in_specs=[row, pl.BlockSpec((1, dm), lambda i: (0, 0))] + [ANY] * len(deps), out_specs=row, compiler_params=_params(1),
    )(h, g, *deps)


def _rms_fwd_into(name, src, g, lp, row0, tm, prev=None, deps=()):
    n, dm = src.shape
    b0 = row0 // tm
    n_in = 2 + len(deps)

    def body(s_ref, g_ref, *rest):
        h_ref, o_ref = rest[-2:]
        hv = s_ref[...]
        r = lax.rsqrt(jnp.mean(hv * hv, axis=-1, keepdims=True) + EPS)
        h_ref[...] = hv
        o_ref[...] = (hv * r * g_ref[...]).astype(BF16)

    row = pl.BlockSpec((tm, dm), lambda i: (b0 + i, 0))
    return pl.pallas_call(
        body, name=name, grid=(n // tm,),
        out_shape=(jax.ShapeDtypeStruct((lp, dm), F32), jax.ShapeDtypeStruct((lp, dm), BF16)),
        in_specs=[pl.BlockSpec((tm, dm), lambda i: (i, 0)), pl.BlockSpec((1, dm), lambda i: (0, 0))]
        + [ANY] * (len(deps) + (0 if prev is None else 2)),
        out_specs=(row, row), input_output_aliases={} if prev is None else {n_in: 0, n_in + 1: 1},
        compiler_params=_params(1),
    )(src, g, *deps, *(prev or ()))


def _rms_bwd(name, dy, h, g, dres, tm, dep):
    lp, dm = h.shape

    def body(dy_ref, h_ref, g_ref, dr_ref, _, dh_ref, dhb_ref, dg_ref):
        hv = h_ref[...]
        r = lax.rsqrt(jnp.mean(hv * hv, axis=-1, keepdims=True) + EPS)
        xhat = hv * r
        dyv = dy_ref[...]
        dxh = dyv * g_ref[...]
        dh = dr_ref[...] + r * (dxh - xhat * jnp.mean(dxh * xhat, axis=-1, keepdims=True))
        dh_ref[...] = dh
        dhb_ref[...] = dh.astype(BF16)

        @pl.when(pl.program_id(0) == 0)
        def _():
            dg_ref[...] = jnp.zeros_like(dg_ref)

        dg_ref[0:1, :] += jnp.sum(dyv * xhat, axis=0, keepdims=True)

    row = pl.BlockSpec((tm, dm), lambda i: (i, 0))
    slab = pl.BlockSpec((SMALL_ROWS, dm), lambda i: (0, 0))
    return pl.pallas_call(
        body, name=name, grid=(lp // tm,),
        out_shape=(jax.ShapeDtypeStruct((lp, dm), F32), jax.ShapeDtypeStruct((lp, dm), BF16),
                   jax.ShapeDtypeStruct((SMALL_ROWS, dm), F32)),
        in_specs=[row, row, pl.BlockSpec((1, dm), lambda i: (0, 0)), row, ANY], out_specs=(row, row, slab),
        compiler_params=_params(1),
    )(dy, h, g, dres, dep)


def _rms_bwd_rows(name, dy, h, g, dres, row0, nrows, tm, dep, dg_prev=None):
    dm = h.shape[1]
    b0 = row0 // tm

    def body(dy_ref, h_ref, g_ref, dr_ref, *rest):
        d_ref, dg_ref = rest[-2:]
        hv = h_ref[...]
        r = lax.rsqrt(jnp.mean(hv * hv, axis=-1, keepdims=True) + EPS)
        xhat = hv * r
        dyv = dy_ref[...]
        dxh = dyv * g_ref[...]
        d_ref[...] = dr_ref[...] + r * (dxh - xhat * jnp.mean(dxh * xhat, axis=-1, keepdims=True))

        @pl.when(pl.program_id(0) == 0)
        def _():
            dg_ref[...] = jnp.zeros_like(dg_ref) if dg_prev is None else rest[1][...]

        dg_ref[0:1, :] += jnp.sum(dyv * xhat, axis=0, keepdims=True)

    row = pl.BlockSpec((tm, dm), lambda i: (b0 + i, 0))
    slab = pl.BlockSpec((SMALL_ROWS, dm), lambda i: (0, 0))
    extra = [dep] + ([dg_prev] if dg_prev is not None else [])
    return pl.pallas_call(
        body, name=name, grid=(nrows // tm,),
        out_shape=(jax.ShapeDtypeStruct((nrows, dm), F32), jax.ShapeDtypeStruct((SMALL_ROWS, dm), F32)),
        in_specs=[row, row, pl.BlockSpec((1, dm), lambda i: (0, 0)), row, ANY] + ([slab] if dg_prev is not None else []),
        out_specs=(pl.BlockSpec((tm, dm), lambda i: (i, 0)), slab), compiler_params=_params(1),
    )(dy, h, g, dres, *extra)


def _gate_mix(name, proj, b_gate2, ya, pool_scale, yb, tm):
    _, lp, dm = proj.shape

    def body(ga_ref, gr_ref, b_ref, ya_ref, ps_ref, yb_ref, o_ref):
        g_a = jax.nn.sigmoid(ga_ref[...].astype(F32) + b_ref[0:1, :])
        g_b = jax.nn.sigmoid(gr_ref[...].astype(F32) + b_ref[1:2, :])
        y_a = ya_ref[...].astype(F32) * ps_ref[...]
        o_ref[...] = (g_a * y_a + g_b * yb_ref[...].astype(F32)).astype(BF16)

    row = pl.BlockSpec((tm, dm), lambda i: (i, 0))
    return pl.pallas_call(
        body, name=name, out_shape=jax.ShapeDtypeStruct((lp, dm), BF16), grid=(lp // tm,),
        in_specs=[pl.BlockSpec((None, tm, dm), lambda i: (4, i, 0)), pl.BlockSpec((None, tm, dm), lambda i: (5, i, 0)),
                  pl.BlockSpec((2, dm), lambda i: (0, 0)), row, pl.BlockSpec((1, dm), lambda i: (0, 0)), row],
        out_specs=row, compiler_params=_params(1),
    )(proj, proj, b_gate2, ya, pool_scale, yb)


def _gate_bwd(name, dmix, proj, b_gate2, ya, pool_scale, yb, tm):
    _, lp, dm = proj.shape

    def body(dm_ref, ga_ref, gr_ref, b_ref, ya_ref, ps_ref, yb_ref, dp_ref, dyb_ref, dya_ref, db_ref, dps_ref):
        dmx = dm_ref[...].astype(F32)
        g_a = jax.nn.sigmoid(ga_ref[...].astype(F32) + b_ref[0:1, :])
        g_b = jax.nn.sigmoid(gr_ref[...].astype(F32) + b_ref[1:2, :])
        ya_pre = ya_ref[...].astype(F32)
        ybv = yb_ref[...].astype(F32)
        ps = ps_ref[...]
        dga = dmx * (ya_pre * ps) * (g_a * (1.0 - g_a))
        dgr = dmx * ybv * (g_b * (1.0 - g_b))
        dp_ref[0] = dga.astype(BF16)
        dp_ref[1] = dgr.astype(BF16)
        dyb_ref[...] = (dmx * g_b).astype(BF16)
        dya_ref[...] = (dmx * g_a * ps).astype(BF16)

        @pl.when(pl.program_id(0) == 0)
        def _():
            db_ref[...] = jnp.zeros_like(db_ref)
            dps_ref[...] = jnp.zeros_like(dps_ref)

        db_ref[0:1, :] += jnp.sum(dga, axis=0, keepdims=True)
        db_ref[1:2, :] += jnp.sum(dgr, axis=0, keepdims=True)
        dps_ref[0:1, :] += jnp.sum(dmx * g_a * ya_pre, axis=0, keepdims=True)

    row = pl.BlockSpec((tm, dm), lambda i: (i, 0))
    one = pl.BlockSpec((1, dm), lambda i: (0, 0))
    slab = pl.BlockSpec((SMALL_ROWS, dm), lambda i: (0, 0))
    return pl.pallas_call(
        body, name=name, grid=(lp // tm,),
        out_shape=(jax.ShapeDtypeStruct((6, lp, dm), BF16), jax.ShapeDtypeStruct((lp, dm), BF16),
                   jax.ShapeDtypeStruct((lp, dm), BF16), jax.ShapeDtypeStruct((SMALL_ROWS, dm), F32),
                   jax.ShapeDtypeStruct((SMALL_ROWS, dm), F32)),
        in_specs=[row, pl.BlockSpec((None, tm, dm), lambda i: (4, i, 0)), pl.BlockSpec((None, tm, dm), lambda i: (5, i, 0)),
                  pl.BlockSpec((2, dm), lambda i: (0, 0)), row, one, row],
        out_specs=(pl.BlockSpec((2, tm, dm), lambda i: (2, i, 0)), row, row, slab, slab),
        compiler_params=_params(1),
    )(dmix, proj, proj, b_gate2, ya, pool_scale, yb)


def _final_loss(name, h2, g3, target, tm):
    lp, dm = h2.shape
    seq = target.shape[0]

    def body(h_ref, g_ref, t_ref, dh_ref, dhb_ref, ls_ref, dg_ref):
        @pl.when(pl.program_id(0) == 0)
        def _():
            ls_ref[...] = jnp.zeros_like(ls_ref)
            dg_ref[...] = jnp.zeros_like(dg_ref)

        hv = h_ref[...]
        gv = g_ref[...]
        r = lax.rsqrt(jnp.mean(hv * hv, axis=-1, keepdims=True) + EPS)
        xhat = hv * r
        err = xhat * gv - t_ref[...]
        dout = err * (1.0 / dm)
        dxh = dout * gv
        dh = r * (dxh - xhat * jnp.mean(dxh * xhat, axis=-1, keepdims=True))
        dh_ref[...] = dh
        dhb_ref[...] = dh.astype(BF16)
        ls_ref[0:1, :] += jnp.sum(err * err, axis=0, keepdims=True)
        dg_ref[0:1, :] += jnp.sum(dout * xhat, axis=0, keepdims=True)

    row = pl.BlockSpec((tm, dm), lambda i: (i, 0))
    slab = pl.BlockSpec((SMALL_ROWS, dm), lambda i: (0, 0))
    return pl.pallas_call(
        body, name=name, grid=(seq // tm,),
        out_shape=(jax.ShapeDtypeStruct((lp, dm), F32), jax.ShapeDtypeStruct((lp, dm), BF16),
                   jax.ShapeDtypeStruct((SMALL_ROWS, dm), F32), jax.ShapeDtypeStruct((SMALL_ROWS, dm), F32)),
        in_specs=[row, pl.BlockSpec((1, dm), lambda i: (0, 0)), row],
        out_specs=(row, row, slab, slab), compiler_params=_params(1),
    )(h2, g3, target)


def _zero_tail(name, arrays, tail):
    n = len(arrays)
    lp, dm = arrays[0].shape
    last = lp // tail - 1

    def body(*refs):
        for o_ref in refs[n:]:
            o_ref[...] = jnp.zeros_like(o_ref)

    return pl.pallas_call(
        body, name=name, grid=(1,), out_shape=tuple(jax.ShapeDtypeStruct(a.shape, a.dtype) for a in arrays),
        in_specs=[ANY] * n, out_specs=tuple(pl.BlockSpec((tail, dm), lambda i: (last, 0)) for _ in arrays),
        input_output_aliases={a: a for a in range(n)}, compiler_params=_params(1),
    )(*arrays)


def _shift(v, k):
    return pltpu.roll(v, k % v.shape[0], axis=0)


def _window_sum(v, group, sign):
    s2 = v + _shift(v, sign * 1)
    s4 = s2 + _shift(s2, sign * 2)
    s8 = s4 + _shift(s4, sign * 4)
    s16 = s8 + _shift(s8, sign * 8)
    return jnp.where(group == 0, s2, jnp.where(group == 1, s4, jnp.where(group == 2, s8, s16)))


def _pool_count(lp, group):
    row = lax.broadcasted_iota(jnp.int32, (lp, 1), 0)
    window = jnp.left_shift(2, group).astype(F32)
    meta_pos = (row - (lp - N_META) + 1).astype(F32)
    return jnp.where(row >= lp - N_META, jnp.minimum(meta_pos, window), window)


def _mixer_fwd(name, proj, conv_w, tc, dep):
    _, lp, dm = proj.shape
    per_group = dm // len(POOL_WINDOWS) // tc

    def body(u_ref, gb_ref, gc_ref, v_ref, cw_ref, _, p_ref, z_ref):
        group = pl.program_id(0) // per_group
        u = u_ref[...].astype(F32)
        p_ref[...] = (_window_sum(u, group, 1) / _pool_count(lp, group) - u).astype(BF16)
        cv = gc_ref[...].astype(F32) * v_ref[...].astype(F32)
        conv = cw_ref[0:1, :] * _shift(cv, 2) + cw_ref[1:2, :] * _shift(cv, 1) + cw_ref[2:3, :] * cv
        z_ref[...] = (gb_ref[...].astype(F32) * conv).astype(BF16)

    def seg(s):
        return pl.BlockSpec((None, lp, tc), lambda j: (s, 0, j))

    col = pl.BlockSpec((lp, tc), lambda j: (0, j))
    return pl.pallas_call(
        body, name=name, grid=(dm // tc,),
        out_shape=(jax.ShapeDtypeStruct((lp, dm), BF16), jax.ShapeDtypeStruct((lp, dm), BF16)),
        in_specs=[seg(0), seg(1), seg(2), seg(3), pl.BlockSpec((3, tc), lambda j: (0, j)), ANY],
        out_specs=(col, col), compiler_params=_params(1),
    )(proj, proj, proj, proj, conv_w, dep)


def _mixer_bwd(name, dz, dpooled, proj, conv_w, dproj, tc, dep):
    _, lp, dm = proj.shape
    per_group = dm // len(POOL_WINDOWS) // tc

    def body(dz_ref, dp_ref, gb_ref, gc_ref, v_ref, cw_ref, _, __, o_ref, dcw_ref):
        group = pl.program_id(0) // per_group
        dzv = dz_ref[...].astype(F32)
        gb = gb_ref[...].astype(F32)
        gc = gc_ref[...].astype(F32)
        vv = v_ref[...].astype(F32)
        cv = gc * vv
        c1 = _shift(cv, 1)
        c2 = _shift(cv, 2)
        w0, w1, w2 = cw_ref[0:1, :], cw_ref[1:2, :], cw_ref[2:3, :]
        o_ref[1] = (dzv * (w0 * c2 + w1 * c1 + w2 * cv)).astype(BF16)
        dconv = dzv * gb
        dcw_ref[...] = jnp.zeros_like(dcw_ref)
        dcw_ref[0:1, :] = jnp.sum(dconv * c2, axis=0, keepdims=True)
        dcw_ref[1:2, :] = jnp.sum(dconv * c1, axis=0, keepdims=True)
        dcw_ref[2:3, :] = jnp.sum(dconv * cv, axis=0, keepdims=True)
        dcv = w0 * _shift(dconv, -2) + w1 * _shift(dconv, -1) + w2 * dconv
        o_ref[2] = (dcv * vv).astype(BF16)
        o_ref[3] = (dcv * gc).astype(BF16)
        dpv = dp_ref[...].astype(F32)
        o_ref[0] = (_window_sum(dpv / _pool_count(lp, group), group, -1) - dpv).astype(BF16)

    def seg(s):
        return pl.BlockSpec((None, lp, tc), lambda j: (s, 0, j))

    col = pl.BlockSpec((lp, tc), lambda j: (0, j))
    return pl.pallas_call(
        body, name=name, grid=(dm // tc,),
        out_shape=(jax.ShapeDtypeStruct(dproj.shape, BF16), jax.ShapeDtypeStruct((SMALL_ROWS, dm), F32)),
        in_specs=[col, col, seg(1), seg(2), seg(3), pl.BlockSpec((3, tc), lambda j: (0, j)), ANY, ANY],
        out_specs=(pl.BlockSpec((4, lp, tc), lambda j: (0, 0, j)), pl.BlockSpec((SMALL_ROWS, tc), lambda j: (0, j))),
        input_output_aliases={6: 0}, compiler_params=_params(1),
    )(dz, dpooled, proj, proj, proj, conv_w, dproj, dep)


def _row_tile(r, c, bytes_per_row_elem=4, budget=2 * 1024 * 1024):
    best = None
    for t in range(16, r + 1, 16):
        if r % t == 0 and t * c * bytes_per_row_elem <= budget:
            best = t
    return best if best is not None else r


def _pair_add(name, g4, recv, core):
    s, r, c = g4.shape
    h = r // 2
    tr = _row_tile(h, c, budget=6 * 1024 * 1024)
    nb = h // tr

    def body(core_ref, g_ref, r_ref, o_ref):
        o_ref[...] = (g_ref[...].astype(F32) + r_ref[...].astype(F32)).astype(BF16)

    grid_spec = pltpu.PrefetchScalarGridSpec(
        num_scalar_prefetch=1, grid=(s, nb),
        in_specs=[pl.BlockSpec((None, tr, c), lambda si, j, core_ref: (si, core_ref[0] * nb + j, 0)),
                  pl.BlockSpec((None, tr, c), lambda si, j, core_ref: (si, j, 0))],
        out_specs=pl.BlockSpec((None, tr, c), lambda si, j, core_ref: (si, j, 0)))
    return pl.pallas_call(
        body, name=name, out_shape=jax.ShapeDtypeStruct((s, h, c), BF16), grid_spec=grid_spec,
        compiler_params=_params(2),
    )(core, g4, recv)


def _chip_sum(name, parts, recv, chip):
    _, h, c = parts.shape
    tr = _row_tile(h, c)

    def body(chip_ref, p_ref, r_ref, o_ref):
        acc = p_ref[...].astype(F32)
        for i in range(len(CHIP_FLIPS)):
            acc = acc + r_ref[i].astype(F32)
        o_ref[...] = acc

    grid_spec = pltpu.PrefetchScalarGridSpec(
        num_scalar_prefetch=1, grid=(h // tr,),
        in_specs=[pl.BlockSpec((None, tr, c), lambda j, chip_ref: (chip_ref[0], j, 0)),
                  pl.BlockSpec((len(CHIP_FLIPS), tr, c), lambda j, chip_ref: (0, j, 0))],
        out_specs=pl.BlockSpec((tr, c), lambda j, chip_ref: (j, 0)))
    return pl.pallas_call(
        body, name=name, out_shape=jax.ShapeDtypeStruct((h, c), F32), grid_spec=grid_spec, compiler_params=_params(1),
    )(chip, parts, recv)


def _adam_update(w, gv, m, v):
    c1 = 1.0 - ADAM_B1 ** ADAM_STEP
    c2 = 1.0 - ADAM_B2 ** ADAM_STEP
    nm = ADAM_B1 * m + (1.0 - ADAM_B1) * gv
    nv = ADAM_B2 * v + (1.0 - ADAM_B2) * (gv * gv)
    return -ADAM_LR * ((nm / c1) / (jnp.sqrt(nv / c2) + ADAM_EPS) + ADAM_WD * w), nm, nv


def _adamw_halves(name, w, g_own, g_sib, m, v, core, part=(0, 1), prev=None):
    r, c = w.shape
    rp = r // part[1]
    h = rp // 2
    tr = _row_tile(h, c, budget=2 * 1024 * 1024)
    nbh = h // tr
    j0 = part[0] * 2 * nbh
    n_prev = 0 if prev is None else 4

    def body(core_ref, w_ref, go_ref, gs_ref, m_ref, v_ref, *rest):
        g_ref, d_ref, nm_ref, nv_ref = rest[n_prev:]
        mine = (pl.program_id(0) // nbh) == core_ref[0]
        gv = jnp.where(mine, go_ref[...], gs_ref[...])
        g_ref[...] = gv
        d_ref[...], nm_ref[...], nv_ref[...] = _adam_update(w_ref[...], gv, m_ref[...], v_ref[...])

    def blk(fn):
        return pl.BlockSpec((tr, c), fn)

    full = blk(lambda j, core_ref: (j0 + j, 0))
    own = blk(lambda j, core_ref: (jnp.clip(j - core_ref[0] * nbh, 0, nbh - 1), 0))
    sib = blk(lambda j, core_ref: (jnp.clip(j - (1 - core_ref[0]) * nbh, 0, nbh - 1), 0))
    grid_spec = pltpu.PrefetchScalarGridSpec(
        num_scalar_prefetch=1, grid=(2 * nbh,), in_specs=[full, own, sib, full, full] + [ANY] * n_prev, out_specs=(full,) * 4)
    sds = jax.ShapeDtypeStruct((r, c), F32)
    return pl.pallas_call(
        body, name=name, out_shape=(sds,) * 4, grid_spec=grid_spec, compiler_params=_params(1),
        input_output_aliases={6 + i: i for i in range(n_prev)},
    )(core, w, g_own, g_sib, m, v, *(prev or ()))


def _adamw_small(name, red, rows, cols, params):
    n = len(params)
    whole = pl.BlockSpec(memory_space=pltpu.VMEM)

    def body(red_ref, *refs):
        ins, outs = refs[:3 * n], refs[3 * n:]
        chip = 2 * lax.axis_index("x") + lax.axis_index("y")
        for i in range(n):
            w_ref, m_ref, v_ref = ins[3 * i:3 * i + 3]
            r, c = w_ref.shape
            g = red_ref[pl.ds(rows[i], r), pl.ds(pl.multiple_of(chip * c, LANES), c)] if cols[i] else red_ref[pl.ds(rows[i], r), :]
            outs[4 * i][...] = g
            outs[4 * i + 1][...], outs[4 * i + 2][...], outs[4 * i + 3][...] = _adam_update(w_ref[...], g, m_ref[...], v_ref[...])

    flat = [t for p in params for t in p]
    res = pl.pallas_call(
        body, name=name, out_shape=tuple(jax.ShapeDtypeStruct(p[0].shape, F32) for p in params for _ in range(4)),
        in_specs=[whole] * (1 + 3 * n), out_specs=(whole,) * (4 * n), compiler_params=_params(),
    )(red, *flat)
    return [res[4 * i:4 * i + 4] for i in range(n)]


def _cast_into_slot(name, w, chip, dtype, deps=()):
    r, c = w.shape
    tr = _row_tile(r, c)

    def body(chip_ref, w_ref, *rest):
        rest[-1][...] = w_ref[...].astype(dtype)

    grid_spec = pltpu.PrefetchScalarGridSpec(
        num_scalar_prefetch=1, grid=(r // tr,),
        in_specs=[pl.BlockSpec((tr, c), lambda j, chip_ref: (j, 0))] + [ANY] * len(deps),
        out_specs=pl.BlockSpec((None, tr, c), lambda j, chip_ref: (chip_ref[0], j, 0)))
    return pl.pallas_call(
        body, name=name, out_shape=jax.ShapeDtypeStruct((4, r, c), dtype), grid_spec=grid_spec, compiler_params=_params(1),
    )(chip, w, *deps)


def _cast_half_into_slot(name, w, chip_half, dtype, into=None):
    r, c = w.shape
    h = r // 2
    tr = _row_tile(h, c)
    nb = h // tr
    n_prev = 0 if into is None else 1

    def body(ids_ref, w_ref, *rest):
        rest[-1][...] = w_ref[...].astype(dtype)

    grid_spec = pltpu.PrefetchScalarGridSpec(
        num_scalar_prefetch=1, grid=(nb,),
        in_specs=[pl.BlockSpec((tr, c), lambda j, ids_ref: (ids_ref[1] * nb + j, 0))] + [ANY] * n_prev,
        out_specs=pl.BlockSpec((None, tr, c), lambda j, ids_ref: (ids_ref[0], ids_ref[1] * nb + j, 0)))
    return pl.pallas_call(
        body, name=name, out_shape=jax.ShapeDtypeStruct((4, r, c), dtype), grid_spec=grid_spec, compiler_params=_params(1),
        input_output_aliases={2: 0} if into is not None else {},
    )(chip_half, w, *([into] if into is not None else []))


def _place():
    return lax.axis_index("x"), lax.axis_index("y"), lax.axis_index("c")


def _chip_of(x, y, flip):
    px, py = x ^ flip[0], y ^ flip[1]
    return px, py, 2 * px + py


def _half(ref, which):
    rows = ref.shape[0] // 2
    return ref.at[pl.ds(which * rows, rows)]


HBM = pl.BlockSpec(memory_space=pltpu.HBM)
SEM = pl.BlockSpec(memory_space=pltpu.SEMAPHORE)
SPLIT_COPY = pltpu.CompilerParams(has_side_effects=pltpu.SideEffectType.DATAFLOW_SIDE_EFFECTING)


def _in_hbm(arrays):
    return [pltpu.with_memory_space_constraint(t, pltpu.HBM) for t in arrays]


TOKEN = jax.ShapeDtypeStruct((SMALL_ROWS, LANES), F32)
TOKEN_SPEC = pl.BlockSpec(memory_space=pltpu.VMEM)


NEIGHBOUR_FLIPS = CHIP_FLIPS[:2]


def _relay_chips(x, y, c):
    fx, fy = x ^ c, y ^ (1 - c)
    return (fx, fy), 2 * fx + fy, 2 * (1 - x) + (1 - y)


def _ag_start(name, slabs, deps=()):
    n = len(slabs)
    nn = len(NEIGHBOUR_FLIPS)

    def body(*refs):
        no = n + len(deps)
        ssem, rsem = refs[no], refs[no + 1]
        outs = refs[no + 2:no + 2 + n]
        token = refs[no + 2 + n]
        token[...] = jnp.zeros_like(token)
        x, y, c = _place()
        k = 2 * x + y
        for a in range(n):
            for j, flip in enumerate(NEIGHBOUR_FLIPS):
                px, py, _ = _chip_of(x, y, flip)
                mine = _half(outs[a].at[k], c)
                pltpu.make_async_remote_copy(src_ref=mine, dst_ref=mine, send_sem=ssem.at[a * nn + j],
                                             recv_sem=rsem.at[a * nn + j], device_id=(px, py, c), device_id_type=MESH).start()

    sem = pltpu.SemaphoreType.DMA((nn * n,))
    res = pl.pallas_call(
        body, name=name, out_shape=(sem, sem) + tuple(pltpu.HBM(t.shape, t.dtype) for t in slabs) + (TOKEN,),
        in_specs=[HBM] * n + [ANY] * len(deps), out_specs=tuple([SEM, SEM] + [HBM] * n + [TOKEN_SPEC]),
        input_output_aliases={a: 2 + a for a in range(n)}, compiler_params=SPLIT_COPY,
    )(*_in_hbm(slabs), *deps)
    return (res[0], res[1]), list(res[2:2 + n]), res[2 + n]


def _ag_relay(name, slabs, sems, after, then_start=()):
    n = len(slabs)
    m = len(then_start)
    nn = len(NEIGHBOUR_FLIPS)

    def body(*refs):
        no = n + 2 + m + len(after)
        ins = refs[:n]
        ssem, rsem = refs[n], refs[n + 1]
        r_s, r_r, p_s, p_r = refs[no:no + 4]
        x, y, c = _place()
        k = 2 * x + y
        (fx, fy), _, _ = _relay_chips(x, y, c)
        for a in range(n):
            for j, flip in enumerate(NEIGHBOUR_FLIPS):
                _, _, kj = _chip_of(x, y, flip)
                landed = _half(ins[a].at[kj], c)
                cp = pltpu.make_async_remote_copy(
                    src_ref=_half(ins[a].at[k], c), dst_ref=landed, send_sem=ssem.at[a * nn + j],
                    recv_sem=rsem.at[a * nn + j], device_id=(x, y, c), device_id_type=MESH)
                cp.wait_send()
                cp.wait_recv()
        for a in range(n):
            near = _half(ins[a].at[2 * (x ^ (1 - c)) + (y ^ c)], c)
            pltpu.make_async_remote_copy(src_ref=near, dst_ref=near, send_sem=r_s.at[a], recv_sem=r_r.at[a],
                                         device_id=(fx, fy, c), device_id_type=MESH).start()
            for j, flip in enumerate(NEIGHBOUR_FLIPS):
                _, _, kj = _chip_of(x, y, flip)
                landed = _half(ins[a].at[kj], c)
                pltpu.make_async_remote_copy(src_ref=landed, dst_ref=landed, send_sem=p_s.at[a * nn + j],
                                             recv_sem=p_r.at[a * nn + j], device_id=(x, y, 1 - c), device_id_type=MESH).start()
        if m:
            d_s, d_r = refs[no + 4 + n], refs[no + 5 + n]
            nxt = refs[no + 6 + n:]
            for a in range(m):
                for j, flip in enumerate(NEIGHBOUR_FLIPS):
                    px, py, _ = _chip_of(x, y, flip)
                    mine = _half(nxt[a].at[k], c)
                    pltpu.make_async_remote_copy(src_ref=mine, dst_ref=mine, send_sem=d_s.at[a * nn + j],
                                                 recv_sem=d_r.at[a * nn + j], device_id=(px, py, c), device_id_type=MESH).start()

    rsem_t = pltpu.SemaphoreType.DMA((n,))
    psem_t = pltpu.SemaphoreType.DMA((nn * n,))
    out_shape = (rsem_t, rsem_t, psem_t, psem_t) + tuple(pltpu.HBM(t.shape, t.dtype) for t in slabs)
    out_specs = [SEM] * 4 + [HBM] * n
    aliases = {a: 4 + a for a in range(n)}
    if m:
        dsem_t = pltpu.SemaphoreType.DMA((nn * m,))
        out_shape += (dsem_t, dsem_t) + tuple(pltpu.HBM(t.shape, t.dtype) for t in then_start)
        out_specs += [SEM, SEM] + [HBM] * m
        aliases.update({n + 2 + a: 4 + n + 2 + a for a in range(m)})
    res = pl.pallas_call(
        body, name=name, out_shape=out_shape, in_specs=[HBM] * n + [SEM, SEM] + [HBM] * m + [ANY] * len(after),
        out_specs=tuple(out_specs), input_output_aliases=aliases, compiler_params=SPLIT_COPY,
    )(*slabs, sems[0], sems[1], *_in_hbm(list(then_start)), *after)
    if not m:
        return tuple(res[:4]), list(res[4:])
    return (tuple(res[:4]), list(res[4:4 + n])), ((res[4 + n], res[5 + n]), list(res[6 + n:]))


def _wait_passes(ins, p_s, p_r, x, y, c):
    nn = len(NEIGHBOUR_FLIPS)
    for a in range(len(ins)):
        for j, flip in enumerate(NEIGHBOUR_FLIPS):
            _, _, kj = _chip_of(x, y, flip)
            cp = pltpu.make_async_remote_copy(
                src_ref=_half(ins[a].at[kj], c), dst_ref=_half(ins[a].at[kj], 1 - c), send_sem=p_s.at[a * nn + j],
                recv_sem=p_r.at[a * nn + j], device_id=(x, y, c), device_id_type=MESH)
            cp.wait_send()
            cp.wait_recv()


def _ag_relay_wait(name, slabs, sems, after):
    n = len(slabs)
    ns = len(sems)

    def body(*refs):
        no = n + ns + len(after)
        ins = refs[:n]
        r_s, r_r = refs[n], refs[n + 1]
        f_s, f_r = refs[no], refs[no + 1]
        x, y, c = _place()
        _, _, kd = _relay_chips(x, y, c)
        for a in range(n):
            near = _half(ins[a].at[2 * (x ^ (1 - c)) + (y ^ c)], c)
            cp = pltpu.make_async_remote_copy(src_ref=near, dst_ref=_half(ins[a].at[kd], c), send_sem=r_s.at[a],
                                              recv_sem=r_r.at[a], device_id=(x, y, c), device_id_type=MESH)
            cp.wait_send()
            cp.wait_recv()
        if ns == 4:
            _wait_passes(ins, refs[n + 2], refs[n + 3], x, y, c)
        for a in range(n):
            diag = _half(ins[a].at[kd], c)
            pltpu.make_async_remote_copy(src_ref=diag, dst_ref=diag, send_sem=f_s.at[a], recv_sem=f_r.at[a],
                                         device_id=(x, y, 1 - c), device_id_type=MESH).start()

    sem = pltpu.SemaphoreType.DMA((n,))
    res = pl.pallas_call(
        body, name=name, out_shape=(sem, sem) + tuple(pltpu.HBM(t.shape, t.dtype) for t in slabs),
        in_specs=[HBM] * n + [SEM] * ns + [ANY] * len(after), out_specs=tuple([SEM, SEM] + [HBM] * n),
        input_output_aliases={a: 2 + a for a in range(n)}, compiler_params=SPLIT_COPY,
    )(*slabs, *sems, *after)
    return (res[0], res[1]), list(res[2:])


def _ag_final_wait(name, slabs, sems, after, first=0):
    n = len(slabs)

    def body(*refs):
        ins = refs[:n]
        f_s, f_r = refs[n], refs[n + 1]
        x, y, c = _place()
        _, _, kd = _relay_chips(x, y, c)
        for a in range(n):
            cp = pltpu.make_async_remote_copy(
                src_ref=_half(ins[a].at[kd], c), dst_ref=_half(ins[a].at[kd], 1 - c), send_sem=f_s.at[first + a],
                recv_sem=f_r.at[first + a], device_id=(x, y, c), device_id_type=MESH)
            cp.wait_send()
            cp.wait_recv()

    return pl.pallas_call(
        body, name=name, out_shape=tuple(pltpu.HBM(t.shape, t.dtype) for t in slabs),
        in_specs=[HBM] * n + [SEM, SEM] + [ANY] * len(after), out_specs=tuple([HBM] * n),
        input_output_aliases={a: a for a in range(n)}, compiler_params=SPLIT_COPY,
    )(*slabs, sems[0], sems[1], *after)


def _sibling_part(ref, c, halves):
    if not halves:
        return ref
    h = ref.shape[1] // 2
    return ref.at[:, pl.ds((1 - c) * h, h)]


def _swap_start(name, grads, halves=True, deps=()):
    n = len(grads)

    def body(*refs):
        no = 2 * n + len(deps)
        ssem, rsem = refs[no], refs[no + 1]
        src, land = refs[no + 2:no + n + 2], refs[no + n + 2:no + 2 * n + 2]
        token = refs[no + 2 * n + 2]
        token[...] = jnp.zeros_like(token)
        x, y, c = _place()
        for a in range(n):
            pltpu.make_async_remote_copy(
                src_ref=_sibling_part(src[a], c, halves), dst_ref=land[a], send_sem=ssem.at[a], recv_sem=rsem.at[a],
                device_id=(x, y, 1 - c), device_id_type=MESH).start()

    zones = [lax.empty((g.shape[0], g.shape[1] // 2, g.shape[2]) if halves else g.shape, g.dtype) for g in grads]
    sem = pltpu.SemaphoreType.DMA((n,))
    res = pl.pallas_call(
        body, name=name,
        out_shape=(sem, sem) + tuple(pltpu.HBM(t.shape, t.dtype) for t in list(grads) + zones) + (TOKEN,),
        in_specs=[HBM] * (2 * n) + [ANY] * len(deps), out_specs=tuple([SEM, SEM] + [HBM] * (2 * n) + [TOKEN_SPEC]),
        input_output_aliases={i: 2 + i for i in range(2 * n)}, compiler_params=SPLIT_COPY,
    )(*_in_hbm(list(grads) + zones), *deps)
    return (res[0], res[1], list(res[2:2 + n]), list(res[2 + n:2 + 2 * n])), res[2 + 2 * n]


def _swap_wait(name, ssem, rsem, grads, zones, after, halves=True):
    n = len(grads)

    def body(*refs):
        src, land = refs[:n], refs[n:2 * n]
        ss, rs = refs[2 * n], refs[2 * n + 1]
        x, y, c = _place()
        for a in range(n):
            cp = pltpu.make_async_remote_copy(
                src_ref=_sibling_part(src[a], c, halves), dst_ref=land[a], send_sem=ss.at[a], recv_sem=rs.at[a],
                device_id=(x, y, c), device_id_type=MESH)
            cp.wait_send()
            cp.wait_recv()

    res = pl.pallas_call(
        body, name=name, out_shape=tuple(pltpu.HBM(t.shape, t.dtype) for t in list(grads) + list(zones)),
        in_specs=[HBM] * (2 * n) + [SEM, SEM] + [ANY] * len(after), out_specs=tuple([HBM] * (2 * n)),
        input_output_aliases={i: i for i in range(2 * n)}, compiler_params=SPLIT_COPY,
    )(*grads, *zones, ssem, rsem, *after)
    return list(res[:n]), list(res[n:])


def _scatter_start(name, parts):
    n = len(parts)
    nf = len(CHIP_FLIPS)

    def body(*refs):
        ssem, rsem = refs[2 * n], refs[2 * n + 1]
        src, land = refs[2 * n + 2:3 * n + 2], refs[3 * n + 2:4 * n + 2]
        token = refs[4 * n + 2]
        token[...] = jnp.zeros_like(token)
        x, y, c = _place()
        for a in range(n):
            for j, flip in enumerate(CHIP_FLIPS):
                px, py, kj = _chip_of(x, y, flip)
                pltpu.make_async_remote_copy(
                    src_ref=src[a].at[kj], dst_ref=land[a].at[j], send_sem=ssem.at[a * nf + j], recv_sem=rsem.at[a * nf + j],
                    device_id=(px, py, c), device_id_type=MESH).start()

    zones = [lax.empty((nf,) + p.shape[1:], p.dtype) for p in parts]
    sem = pltpu.SemaphoreType.DMA((nf * n,))
    res = pl.pallas_call(
        body, name=name,
        out_shape=(sem, sem) + tuple(pltpu.HBM(t.shape, t.dtype) for t in list(parts) + zones)
        + (jax.ShapeDtypeStruct((SMALL_ROWS, LANES), F32),),
        in_specs=[HBM] * (2 * n),
        out_specs=tuple([SEM, SEM] + [HBM] * (2 * n) + [pl.BlockSpec(memory_space=pltpu.VMEM)]),
        input_output_aliases={i: 2 + i for i in range(2 * n)}, compiler_params=SPLIT_COPY,
    )(*_in_hbm(list(parts) + zones))
    return (res[0], res[1], list(res[2:2 + n]), list(res[2 + n:2 + 2 * n])), res[2 + 2 * n]


def _scatter_wait(name, ssem, rsem, parts, zones, after):
    n = len(parts)
    nf = len(CHIP_FLIPS)

    def body(*refs):
        src, land = refs[:n], refs[n:2 * n]
        ss, rs = refs[2 * n], refs[2 * n + 1]
        x, y, c = _place()
        for a in range(n):
            for j, flip in enumerate(CHIP_FLIPS):
                _, _, kj = _chip_of(x, y, flip)
                cp = pltpu.make_async_remote_copy(
                    src_ref=src[a].at[kj], dst_ref=land[a].at[j], send_sem=ss.at[a * nf + j], recv_sem=rs.at[a * nf + j],
                    device_id=(x, y, c), device_id_type=MESH)
                cp.wait_send()
                cp.wait_recv()

    res = pl.pallas_call(
        body, name=name, out_shape=tuple(pltpu.HBM(t.shape, t.dtype) for t in list(parts) + list(zones)),
        in_specs=[HBM] * (2 * n) + [SEM, SEM] + [ANY] * len(after), out_specs=tuple([HBM] * (2 * n)),
        input_output_aliases={i: i for i in range(2 * n)}, compiler_params=SPLIT_COPY,
    )(*parts, *zones, ssem, rsem, *after)
    return list(res[:n]), list(res[n:])


N_PEERS = 7


def _peer(x, y, c, mask):
    px, py, pc = x ^ ((mask >> 2) & 1), y ^ ((mask >> 1) & 1), c ^ (mask & 1)
    return (px, py, pc), 4 * px + 2 * py + pc


def _reduce_start(vec, deps):
    nd = len(deps)

    def body(*refs):
        ssem, rsem, src, land, token = refs[2 + nd:]
        token[...] = jnp.zeros_like(token)
        x, y, c = _place()
        me = 4 * x + 2 * y + c
        for mask in range(1, N_PEERS + 1):
            to, _ = _peer(x, y, c, mask)
            pltpu.make_async_remote_copy(src_ref=src, dst_ref=land.at[me], send_sem=ssem.at[mask - 1],
                                         recv_sem=rsem.at[mask - 1], device_id=to, device_id_type=MESH).start()

    zone = lax.empty((N_PEERS + 1,) + vec.shape, vec.dtype)
    sem = pltpu.SemaphoreType.DMA((N_PEERS,))
    res = pl.pallas_call(
        body, name="reduce_start",
        out_shape=(sem, sem, pltpu.HBM(vec.shape, vec.dtype), pltpu.HBM(zone.shape, zone.dtype), TOKEN),
        in_specs=[HBM, HBM] + [ANY] * nd, out_specs=(SEM, SEM, HBM, HBM, TOKEN_SPEC),
        input_output_aliases={0: 2, 1: 3}, compiler_params=SPLIT_COPY,
    )(*_in_hbm([vec, zone]), *deps)
    return res[:4], res[4]


def _reduce_wait(ssem, rsem, vec, zone, after):
    def body(src, land, ss, rs, *_):
        x, y, c = _place()
        for mask in range(1, N_PEERS + 1):
            _, frm = _peer(x, y, c, mask)
            cp = pltpu.make_async_remote_copy(src_ref=src, dst_ref=land.at[frm], send_sem=ss.at[mask - 1],
                                              recv_sem=rs.at[mask - 1], device_id=(x, y, c), device_id_type=MESH)
            cp.wait_send()
            cp.wait_recv()

    return pl.pallas_call(
        body, name="reduce_wait", out_shape=(pltpu.HBM(vec.shape, vec.dtype), pltpu.HBM(zone.shape, zone.dtype)),
        in_specs=[HBM, HBM, SEM, SEM] + [ANY] * len(after), out_specs=(HBM, HBM),
        input_output_aliases={0: 0, 1: 1}, compiler_params=SPLIT_COPY,
    )(vec, zone, ssem, rsem, *after)


def _reduce_sum(vec, zone, me, loss_row, loss_scale):
    r, dm = vec.shape

    def body(me_ref, v_ref, z_ref, o_ref, l_ref):
        acc = None
        for i in range(N_PEERS + 1):
            term = jnp.where(me_ref[0] == i, v_ref[...], z_ref[i])
            acc = term if acc is None else acc + term
        o_ref[...] = acc
        l_ref[...] = jnp.sum(acc[loss_row:loss_row + SMALL_ROWS, :], axis=(0, 1), keepdims=True) * loss_scale

    grid_spec = pltpu.PrefetchScalarGridSpec(
        num_scalar_prefetch=1, grid=(1,),
        in_specs=[pl.BlockSpec((r, dm), lambda i, me_ref: (0, 0)), pl.BlockSpec((N_PEERS + 1, r, dm), lambda i, me_ref: (0, 0, 0))],
        out_specs=(pl.BlockSpec((r, dm), lambda i, me_ref: (0, 0)), pl.BlockSpec((1, 1), lambda i, me_ref: (0, 0))))
    return pl.pallas_call(
        body, name="reduce_sum", out_shape=(jax.ShapeDtypeStruct((r, dm), F32), jax.ShapeDtypeStruct((1, 1), F32)),
        grid_spec=grid_spec, compiler_params=_params(1),
    )(me, vec, zone)


def kernel(x, meta_tokens, norm_mix_g, w_in, b_gate, pool_w, pool_scale, conv_w, conv_out_w, w_o, norm_ffn_g, w_gate_up, w_down, norm_final_g, loss_target, m_meta_tokens, m_norm_mix_g, m_w_in, m_b_gate, m_pool_w, m_pool_scale, m_conv_w, m_conv_out_w, m_w_o, m_norm_ffn_g, m_w_gate_up, m_w_down, m_norm_final_g, v_meta_tokens, v_norm_mix_g, v_w_in, v_b_gate, v_pool_w, v_pool_scale, v_conv_w, v_conv_out_w, v_w_o, v_norm_ffn_g, v_w_gate_up, v_w_down, v_norm_final_g):
    seq, dm = x.shape[1], x.shape[2]
    tail = TAIL_ROWS
    lp = seq + tail
    tm_row = _row_tile(lp, dm, 4, 3 * 1024 * 1024)
    tm_seq = _row_tile(seq, dm, 4, 3 * 1024 * 1024)
    n_chips = 4
    n_groups = len(POOL_WINDOWS)
    gw = dm // n_groups
    tc = min(256, gw)
    cx, cy, cc = _place()
    chip = 2 * cx + cy
    dloc = dm // n_chips

    pool2 = pool_w.reshape(n_groups * pool_w.shape[1], gw)
    big = {"w_in": w_in, "w_gate_up": w_gate_up, "pool_w": pool2, "conv_out_w": conv_out_w, "w_o": w_o, "w_down": w_down}
    chip1 = jnp.reshape(chip, (1,)).astype(jnp.int32)
    core = jnp.reshape(cc, (1,)).astype(jnp.int32)
    small_loc = jnp.concatenate([meta_tokens, jnp.pad(conv_w, ((0, 8 - conv_w.shape[0]), (0, 0))),
                                 jnp.zeros((8, dloc), F32)], axis=0)
    g1, g2, g3 = norm_mix_g.reshape(1, dm), norm_ffn_g.reshape(1, dm), norm_final_g.reshape(1, dm)
    b_gate2 = b_gate.reshape(2, dm)
    ps = pool_scale.reshape(1, dm)
    mine = jnp.stack([chip, cc]).astype(jnp.int32)
    other = jnp.stack([chip, 1 - cc]).astype(jnp.int32)
    first = [_cast_into_slot("place_small", small_loc, chip1, F32), _cast_half_into_slot("cast_w_in_sent", w_in, mine, BF16)]
    sems, first, token = _ag_start("ag_start_first", first)
    first[1] = _cast_half_into_slot("cast_w_in_kept", w_in, other, BF16, into=first[1])
    cast = {nme: _cast_into_slot("cast_" + nme, big[nme], chip1, BF16, deps=(token,))
            for nme in ["pool_w", "conv_out_w", "w_o", "w_gate_up", "w_down"]}
    sems, first = _ag_relay("ag_relay_first", first, sems, list(cast.values()))
    sems, (small4, w_in4) = _ag_relay_wait("ag_relay_wait_first", first, sems, [])
    (small4,) = _ag_final_wait("ag_final_wait_small", [small4], sems, [])
    mixer_w = [cast["pool_w"], cast["conv_out_w"], cast["w_o"]]
    sems_mix, mixer_w, token = _ag_start("ag_start_mixer", mixer_w, deps=(small4,))
    sems_gu, (w_gu4,), token = _ag_start("ag_start_gate_up", [cast["w_gate_up"]], deps=(token,))

    small_f = jnp.transpose(small4, (1, 0, 2)).reshape(small4.shape[1], dm)
    meta_f = small_f[:N_META]
    conv_w_f = small_f[N_META:N_META + 3]
    tail_rows = jnp.concatenate([jnp.zeros((tail - N_META, dm), F32), meta_f], axis=0)
    h0_hn1 = _rms_fwd_into("rms_mix", x[0], g1, lp, 0, tm_seq, deps=(token,))
    h0, hn1 = _rms_fwd_into("rms_mix_tail", tail_rows, g1, lp, seq, tail, prev=h0_hn1)
    (w_in4,) = _ag_final_wait("ag_final_wait_first", [w_in4], sems, [hn1], first=1)
    proj = _nn_sharded("proj", hn1, w_in4, 6)
    sems_mix, mixer_w = _ag_relay("ag_relay_mixer", mixer_w, sems_mix, [proj])
    (sems_gu, (w_gu4,)), (sems_down, (w_down4,)) = _ag_relay("ag_relay_gate_up", [w_gu4], sems_gu, [mixer_w[0]],
                                                              then_start=[cast["w_down"]])
    pooled, z = _mixer_fwd("mixer_fwd", proj, conv_w_f, tc, w_down4)
    sems_mix, mixer_w = _ag_relay_wait("ag_relay_wait_mixer", mixer_w, sems_mix, [pooled])
    pool4, conv_out4, w_o4 = _ag_final_wait("ag_final_wait_mixer", mixer_w, sems_mix, [])
    pool_f = jnp.transpose(pool4.reshape(n_chips, n_groups, gw // n_chips, gw), (1, 0, 2, 3)).reshape(n_groups, gw, gw)
    conv_out_f = conv_out4.reshape(dm, dm)
    w_o_f = w_o4.reshape(dm, dm)
    ya = _pool_fwd("pool_proj", pooled, pool_f)
    yb = _nn_plain("conv_out", z, conv_out_f, BF16)
    mix = _gate_mix("gate_mix", proj, b_gate2, ya, ps, yb, tm_row)
    sems_gu, (w_gu4,) = _ag_relay_wait("ag_relay_wait_gate_up", [w_gu4], sems_gu, [mix])
    h1 = _nn_plain("attn_out", mix, w_o_f, F32, res=h0, tn_pref=256, deps=(w_gu4,))
    (w_gu4,) = _ag_final_wait("ag_final_wait_gate_up", [w_gu4], sems_gu, [h1])
    hn2 = _rms_fwd("rms_ffn", h1, g2, tm_row)
    sems_down, (w_down4,) = _ag_relay("ag_relay_down", [w_down4], sems_down, [hn2])
    gu, act = _gate_up_swiglu("gate_up", hn2, w_gu4, w_down4)
    sems_down, (w_down4,) = _ag_relay_wait("ag_relay_wait_down", [w_down4], sems_down, [act])
    (w_down4,) = _ag_final_wait("ag_final_wait_down", [w_down4], sems_down, [])
    w_down_f = w_down4.reshape(-1, dm)
    h2 = _nn_rows("ffn_down", act, w_down_f, h1)
    dh2, dh2b, loss_cols, dg3 = _final_loss("final_loss", h2, g3, loss_target[0], tm_seq)
    dh2, dh2b = _zero_tail("final_loss_tail", [dh2, dh2b], tail)

    def scatter(tag, names_g, swap, after):
        grads_g, got = _swap_wait("swap_wait_" + tag, *swap, [after])
        pairs = [_pair_add("pair_add_" + nme, g4, rv, core) for nme, g4, rv in zip(names_g, grads_g, got)]
        return _scatter_start("scatter_start_" + tag, pairs)

    dgu = _dact_swiglu_bwd("d_gate_up", dh2b, w_down_f, gu)
    gw_down = _tn_plain("dw_down", act, dh2b)
    gw_gu = _tn_sharded("dw_gate_up", hn2, dgu, n_chips)
    swap_a, token = _swap_start("swap_start_a", [gw_gu, gw_down.reshape(n_chips, -1, dm)])
    dhn2 = _nt_sharded("d_hn2", dgu, w_gu4, tr_pref=2816, row_tiles=2, deps=(token,))
    flight_a, token = scatter("a", ["w_gate_up", "w_down"], swap_a, dhn2)
    dh1, dh1b, dg2 = _rms_bwd("rms_ffn_bwd", dhn2, h1, g2, dh2, tm_row, token)
    dmix = _nt_plain("d_mix", dh1b, w_o_f)
    gw_o = _tn_plain("dw_o", mix, dh1b)
    dproj, dyb, dya, db_gate, dps = _gate_bwd("gate_bwd", dmix, proj, b_gate2, ya, ps, yb, tm_row)
    gw_conv_out = _tn_plain("dw_conv_out", z, dyb)
    gw_pool = _pool_bwd_w("dw_pool", pooled, dya)
    gw_pool = jnp.transpose(gw_pool.reshape(n_groups, n_chips, gw // n_chips, gw), (1, 0, 2, 3))
    swap_b, token = _swap_start("swap_start_b", [gw_o.reshape(n_chips, dloc, dm), gw_conv_out.reshape(n_chips, dloc, dm),
                                                 gw_pool.reshape(n_chips, n_groups * (gw // n_chips), gw)])
    dpooled = _pool_bwd_act("d_pooled", dya, pool_f, deps=(token,))
    dz = _nt_plain("d_z", dyb, conv_out_f)
    flight_b, token = scatter("b", ["w_o", "conv_out_w", "pool_w"], swap_b, dz)
    dproj, dconv_w = _mixer_bwd("mixer_bwd", dz, dpooled, proj, conv_w_f, dproj, tc, token)
    gw_in0 = _tn_sharded("dw_in_0", hn1, dproj, n_chips, part=(0, 2))
    swap_c0, token = _swap_start("swap_start_c0", [gw_in0])
    gw_in1 = _tn_sharded("dw_in_1", hn1, dproj, n_chips, part=(1, 2), deps=(token,))
    flight_c0, token = scatter("c0", ["w_in_0"], swap_c0, gw_in1)
    swap_c, token = _swap_start("swap_start_c", [gw_in1], deps=(token,))
    groups_g = {"a": [("w_gate_up", (0, 1)), ("w_down", (0, 1))], "b": [("w_o", (0, 1)), ("conv_out_w", (0, 1)), ("pool_w", (0, 1))],
                "c0": [("w_in", (0, 2))], "c": [("w_in", (1, 2))]}

    def reduced(tag, flight, after):
        pairs, zones = _scatter_wait("scatter_wait_" + tag, *flight, after)
        halves = [_chip_sum("chip_sum_%s_%d" % (nme, part[0]), p, rv, chip1) for (nme, part), p, rv in zip(groups_g[tag], pairs, zones)]
        return _swap_start("send_start_" + tag, halves, halves=False)

    send_a, token = reduced("a", flight_a, [token])
    flight_c, token = scatter("c", ["w_in_1"], swap_c, token)
    dhn1 = _nt_in_proj("d_hn1", dproj, w_in4, deps=(token,))
    dx, dg1 = _rms_bwd_rows("rms_mix_bwd", dhn1, h0, g1, dh1, 0, seq, tm_seq, token)
    dtail, dg1 = _rms_bwd_rows("rms_mix_bwd_tail", dhn1, h0, g1, dh1, seq, tail, tail, dx, dg_prev=dg1)
    grad_x = dx[None]
    dmeta = dtail[tail - N_META:]

    given = dict(meta_tokens=(meta_tokens, m_meta_tokens, v_meta_tokens), norm_mix_g=(norm_mix_g, m_norm_mix_g, v_norm_mix_g),
                 w_in=(w_in, m_w_in, v_w_in), b_gate=(b_gate, m_b_gate, v_b_gate), pool_w=(pool_w, m_pool_w, v_pool_w),
                 pool_scale=(pool_scale, m_pool_scale, v_pool_scale), conv_w=(conv_w, m_conv_w, v_conv_w),
                 conv_out_w=(conv_out_w, m_conv_out_w, v_conv_out_w), w_o=(w_o, m_w_o, v_w_o),
                 norm_ffn_g=(norm_ffn_g, m_norm_ffn_g, v_norm_ffn_g), w_gate_up=(w_gate_up, m_w_gate_up, v_w_gate_up),
                 w_down=(w_down, m_w_down, v_w_down), norm_final_g=(norm_final_g, m_norm_final_g, v_norm_final_g))
    order = list(given.keys())
    grad, delta, new_m, new_v = {}, {}, {}, {}
    vec = jnp.concatenate([dg1, dg2, dg3, db_gate, dps, loss_cols, dconv_w, dmeta], axis=0)
    loss_row = 5 * SMALL_ROWS
    results = {}

    def update(tag, send, after):
        halves, sib_halves = _swap_wait("send_wait_" + tag, *send, after, halves=False)
        deltas = []
        for (nme, part), g_own, g_sib in zip(groups_g[tag], halves, sib_halves):
            w, m, v = given[nme]
            shape2 = (2 * g_own.shape[0] * part[1], g_own.shape[1])
            results[nme] = _adamw_halves("adamw_%s_%d" % (nme, part[0]), w.reshape(shape2), g_own, g_sib, m.reshape(shape2),
                                         v.reshape(shape2), core, part=part, prev=results.get(nme))
            grad[nme], delta[nme], new_m[nme], new_v[nme] = [t.reshape(w.shape) for t in results[nme]]
            deltas.append(results[nme][1])
        return deltas

    done_a = update("a", send_a, [dx])
    send_b, token = reduced("b", flight_b, done_a)
    send_c0, token = reduced("c0", flight_c0, [token])
    done_b = update("b", send_b, [token])
    send_c, token = reduced("c", flight_c, done_b)
    me1 = jnp.reshape(4 * cx + 2 * cy + cc, (1,)).astype(jnp.int32)
    red_flight, token = _reduce_start(vec, [token])
    done_c0 = update("c0", send_c0, [token])
    done_c = update("c", send_c, done_c0)
    red, loss11 = _reduce_sum(*_reduce_wait(*red_flight, done_c), me1, loss_row, 0.5 / dm)
    loss = loss11[0, 0]

    small = {"norm_mix_g": (0, False), "norm_ffn_g": (SMALL_ROWS, False), "norm_final_g": (2 * SMALL_ROWS, False),
             "b_gate": (3 * SMALL_ROWS, False), "pool_scale": (4 * SMALL_ROWS, False), "conv_w": (6 * SMALL_ROWS, True),
             "meta_tokens": (7 * SMALL_ROWS, True)}

    def two_d(t, nme):
        return t if t.ndim == 2 else t.reshape(2 if nme == "b_gate" else 1, dm)

    res = _adamw_small("adamw_small", red, [small[nme][0] for nme in small], [small[nme][1] for nme in small],
                       [[two_d(t, nme) for t in given[nme]] for nme in small])
    for nme, res4 in zip(small, res):
        grad[nme], delta[nme], new_m[nme], new_v[nme] = [t.reshape(given[nme][0].shape) for t in res4]
    return (loss, grad_x, *[grad[nme] for nme in order], *[delta[nme] for nme in order],
            *[new_m[nme] for nme in order], *[new_v[nme] for nme in order])
```

```python
import math

import jax
import jax.numpy as jnp
from jax import lax
from jax.experimental import pallas as pl
from jax.experimental.pallas import tpu as pltpu

F32 = jnp.float32
BF16 = jnp.bfloat16
N_META = 16
POOL_WINDOWS = (2, 4, 8, 16)
EPS = 1e-6
ADAM_LR, ADAM_B1, ADAM_B2, ADAM_EPS, ADAM_WD, ADAM_STEP = 0.001, 0.9, 0.999, 1e-08, 0.01, 10
LANES = 128
V7X_VMEM_BYTES = 64 * 1024 * 1024
VMEM_LIMIT = V7X_VMEM_BYTES - 8 * 1024 * 1024
MESH = pl.DeviceIdType.MESH
ANY = pl.BlockSpec(memory_space=pl.ANY)
CHIP_FLIPS = ((1, 0), (0, 1), (1, 1))
SMALL_ROWS = 8
TAIL_ROWS = 32


def _pick(n, pref):
    best = None
    for t in range(LANES, min(n, pref) + 1, LANES):
        if n % t == 0:
            best = t
    assert best is not None, (n, pref)
    return best


def _params(n_axes=0):
    sem = ("arbitrary",) * n_axes if n_axes else None
    return pltpu.CompilerParams(dimension_semantics=sem, vmem_limit_bytes=VMEM_LIMIT)


_DIMS = {
    "nn": (((1,), (0,)), ((), ())),
    "nt": (((1,), (1,)), ((), ())),
    "tn": (((0,), (0,)), ((), ())),
}


def _matmul(name, mode, a, b, out_sds, grid, a_spec, b_spec, o_spec, nk, res=None, res_spec=None, acc_shape=None, deps=(),
            prev=None):
    out_dtype = out_sds.dtype
    in_place = nk > 1 and out_dtype == F32
    use_scratch = nk > 1 and not in_place
    rows = a_spec.block_shape[-2] if mode != "tn" else None
    chunk = _row_tile(rows, 1, 1, 1152) if rows is not None else None
    untouched = list(deps) + ([prev] if prev is not None else [])
    n_in = 2 + (res is not None) + len(untouched)

    def body(*refs):
        a_ref, b_ref = refs[:2]
        r_ref = refs[2] if res is not None else None
        o_ref, *scr = refs[n_in:]
        k = pl.program_id(len(grid) - 1) if nk > 1 else None

        def emit(sl):
            if sl is None:
                part = lax.dot_general(a_ref[...], b_ref[...], _DIMS[mode], preferred_element_type=F32)
                idx = (slice(None), slice(None))
            else:
                part = lax.dot_general(a_ref[sl, :], b_ref[...], _DIMS[mode], preferred_element_type=F32)
                idx = (sl, slice(None))
            if nk == 1:
                if r_ref is not None:
                    part = part + r_ref[idx]
                o_ref[idx] = part.astype(out_dtype)
                return
            acc = scr[0] if use_scratch else o_ref

            @pl.when(k == 0)
            def _():
                first = part
                if r_ref is not None and in_place:
                    first = first + r_ref[idx]
                acc[idx] = first

            @pl.when(k > 0)
            def _():
                acc[idx] += part

            if use_scratch:

                @pl.when(k == nk - 1)
                def _():
                    o_ref[idx] = acc[idx].astype(out_dtype)

        if mode == "tn" or chunk == rows:
            emit(None)
        else:
            for m0 in range(0, rows, chunk):
                emit(pl.ds(m0, chunk))

    ins = [a, b] + ([res] if res is not None else []) + untouched
    in_specs = [a_spec, b_spec] + ([res_spec] if res is not None else []) + [ANY] * len(untouched)
    scratch = [pltpu.VMEM(acc_shape, F32)] if use_scratch else []
    return pl.pallas_call(
        body, name=name, out_shape=out_sds, grid=grid, in_specs=in_specs, out_specs=o_spec,
        scratch_shapes=scratch, input_output_aliases={} if prev is None else {n_in - 1: 0},
        compiler_params=_params(len(grid)),
    )(*ins)


def _nn_sharded(name, a, w4, nseg, part=(0, 1), prev=None, deps=()):
    lp, kdim = a.shape
    s, _, nloc = w4.shape
    segw = s * nloc // nseg
    tn = _pick(math.gcd(nloc, segw), 1536)
    bw, bo = nloc // tn, segw // tn
    steps = s * bw // part[1]
    j0 = part[0] * steps
    return _matmul(
        name, "nn", a, w4, jax.ShapeDtypeStruct((nseg, lp, segw), BF16), (steps,),
        pl.BlockSpec((lp, kdim), lambda j: (0, 0)),
        pl.BlockSpec((None, kdim, tn), lambda j: ((j0 + j) // bw, 0, (j0 + j) % bw)),
        pl.BlockSpec((None, lp, tn), lambda j: ((j0 + j) // bo, 0, (j0 + j) % bo)), 1, deps=deps, prev=prev)


def _nt_in_proj(name, dseg, w4, row_tiles=2, to_pref=1024, deps=()):
    nseg, lp, segw = dseg.shape
    s, kdim, nloc = w4.shape
    assert nseg * segw == s * nloc and 2 * nloc == 3 * segw, (dseg.shape, w4.shape)
    half = segw // 2
    to = _pick(kdim, to_pref)
    tm = lp // row_tiles

    def body(full_ref, half_ref, w_ref, *rest):
        o_ref = rest[len(deps)]
        r = pl.program_id(2)

        def contribution(full_first):
            lo, hi = (pl.ds(0, segw), pl.ds(segw, half)) if full_first else (pl.ds(half, segw), pl.ds(0, half))
            return (lax.dot_general(full_ref[...], w_ref[:, lo], _DIMS["nt"], preferred_element_type=F32)
                    + lax.dot_general(half_ref[...], w_ref[:, hi], _DIMS["nt"], preferred_element_type=F32))

        @pl.when(r == 0)
        def _():
            o_ref[...] = contribution(True)

        for ri in range(1, s):

            @pl.when(r == ri)
            def _(ri=ri):
                o_ref[...] += contribution(ri % 2 == 0)

    return pl.pallas_call(
        body, name=name, out_shape=jax.ShapeDtypeStruct((lp, kdim), F32), grid=(row_tiles, kdim // to, s),
        in_specs=[pl.BlockSpec((None, tm, segw), lambda m, j, r: ((3 * r + 1) // 2, m, 0)),
                  pl.BlockSpec((None, tm, half), lambda m, j, r: (1 + 3 * (r // 2), m, r % 2)),
                  pl.BlockSpec((None, to, nloc), lambda m, j, r: (r, j, 0))] + [ANY] * len(deps),
        out_specs=pl.BlockSpec((tm, to), lambda m, j, r: (m, j)), compiler_params=_params(3),
    )(dseg, dseg, w4, *deps)


def _nn_plain(name, a, w, out_dtype, res=None, tn_pref=512, tk_pref=2048, deps=()):
    lp, kdim = a.shape
    n = w.shape[1]
    tn = _pick(n, tn_pref)
    tk = kdim if kdim <= tk_pref else _pick(kdim, tk_pref)
    nk = kdim // tk
    grid = (n // tn, nk) if nk > 1 else (n // tn,)
    if nk > 1:
        a_spec = pl.BlockSpec((lp, tk), lambda j, k: (0, k))
        w_spec = pl.BlockSpec((tk, tn), lambda j, k: (k, j))
        o_spec = pl.BlockSpec((lp, tn), lambda j, k: (0, j))
    else:
        a_spec = pl.BlockSpec((lp, tk), lambda j: (0, 0))
        w_spec = pl.BlockSpec((tk, tn), lambda j: (0, j))
        o_spec = pl.BlockSpec((lp, tn), lambda j: (0, j))
    return _matmul(name, "nn", a, w, jax.ShapeDtypeStruct((lp, n), out_dtype), grid, a_spec, w_spec, o_spec, nk,
                   res=res, res_spec=o_spec if res is not None else None, acc_shape=(lp, tn), deps=deps)


def _nt_plain(name, a, w, tn_pref=512):
    lp, kdim = a.shape
    n = w.shape[0]
    tn = _pick(n, tn_pref)
    return _matmul(
        name, "nt", a, w, jax.ShapeDtypeStruct((lp, n), BF16), (n // tn,),
        pl.BlockSpec((lp, kdim), lambda j: (0, 0)),
        pl.BlockSpec((tn, kdim), lambda j: (j, 0)),
        pl.BlockSpec((lp, tn), lambda j: (0, j)), 1)


def _nt_sharded(name, dseg, w4, to_pref=1024, tr_pref=1536, row_tiles=1, deps=()):
    nseg, lp, segw = dseg.shape
    s, kdim, nloc = w4.shape
    tr = _pick(math.gcd(nloc, segw), tr_pref)
    ba, bw = segw // tr, nloc // tr
    nr = s * bw
    to = _pick(kdim, to_pref)
    tm = lp // row_tiles
    return _matmul(
        name, "nt", dseg, w4, jax.ShapeDtypeStruct((lp, kdim), F32), (row_tiles, kdim // to, nr),
        pl.BlockSpec((None, tm, tr), lambda m, j, r: (r // ba, m, r % ba)),
        pl.BlockSpec((None, to, tr), lambda m, j, r: (r // bw, j, r % bw)),
        pl.BlockSpec((tm, to), lambda m, j, r: (m, j)), nr, deps=deps)


def _nn_rows(name, a, w, res, row_tiles=2, tn_pref=512):
    lp, kdim = a.shape
    n = w.shape[1]
    tn = _pick(n, tn_pref)
    tm = lp // row_tiles
    blk = pl.BlockSpec((tm, tn), lambda i, j: (i, j))
    return _matmul(name, "nn", a, w, jax.ShapeDtypeStruct((lp, n), F32), (row_tiles, n // tn),
                   pl.BlockSpec((tm, kdim), lambda i, j: (i, 0)), pl.BlockSpec((kdim, tn), lambda i, j: (0, j)), blk, 1,
                   res=res, res_spec=blk)


def _tn_plain(name, a, d, tk_pref=1024):
    lp, kdim = a.shape
    n = d.shape[1]
    tk = _pick(kdim, tk_pref)
    return _matmul(
        name, "tn", a, d, jax.ShapeDtypeStruct((kdim, n), BF16), (kdim // tk,),
        pl.BlockSpec((lp, tk), lambda i: (0, i)),
        pl.BlockSpec((lp, n), lambda i: (0, 0)),
        pl.BlockSpec((tk, n), lambda i: (i, 0)), 1)


def _tn_sharded(name, a, dseg, s, part=(0, 1), tk_pref=1024, deps=()):
    lp, kdim = a.shape
    nseg, _, segw = dseg.shape
    nloc = nseg * segw // s
    tn = _pick(math.gcd(nloc, segw), 1536)
    bd, bo = segw // tn, nloc // tn
    kpart = kdim // part[1]
    tk = _pick(kpart, tk_pref)
    i0 = part[0] * (kpart // tk)

    def body(a_ref, d_ref, *rest):
        o_ref, at_ref = rest[len(deps):]

        @pl.when(pl.program_id(1) == 0)
        def _():
            at_ref[...] = a_ref[...].T

        o_ref[...] = jnp.dot(at_ref[...], d_ref[...], preferred_element_type=F32).astype(BF16)

    return pl.pallas_call(
        body, name=name, out_shape=jax.ShapeDtypeStruct((s, kpart, nloc), BF16), grid=(kpart // tk, s * bo),
        in_specs=[pl.BlockSpec((lp, tk), lambda i, j: (0, i0 + i)),
                  pl.BlockSpec((None, lp, tn), lambda i, j: (j // bd, 0, j % bd))] + [ANY] * len(deps),
        out_specs=pl.BlockSpec((None, tk, tn), lambda i, j: (j // bo, i, j % bo)),
        scratch_shapes=[pltpu.VMEM((tk, lp), BF16)], compiler_params=_params(2),
    )(a, dseg, *deps)


def _silu_parts(gt):
    sg = jax.nn.sigmoid(gt)
    return gt * sg, sg * (1.0 + gt * (1.0 - sg))


def _gate_up_swiglu(name, a, w4, dep, tn_pref=256):
    lp, kdim = a.shape
    s, _, nloc = w4.shape
    f = s * nloc // 2
    tn = _pick(nloc, tn_pref)
    bw = nloc // tn
    chunk = _row_tile(lp, 1, 1, 576)

    def body(a_ref, wg_ref, wu_ref, _, fac_ref, act_ref):
        for m0 in range(0, lp, chunk):
            sl = pl.ds(m0, chunk)
            gt = jnp.dot(a_ref[sl, :], wg_ref[...], preferred_element_type=F32)
            up = jnp.dot(a_ref[sl, :], wu_ref[...], preferred_element_type=F32)
            silu, dsilu = _silu_parts(gt)
            fac_ref[0, sl, :] = (up * dsilu).astype(BF16)
            fac_ref[1, sl, :] = silu.astype(BF16)
            act_ref[sl, :] = (silu * up).astype(BF16)

    return pl.pallas_call(
        body, name=name, grid=(f // tn,),
        out_shape=(jax.ShapeDtypeStruct((2, lp, f), BF16), jax.ShapeDtypeStruct((lp, f), BF16)),
        in_specs=[pl.BlockSpec((lp, kdim), lambda j: (0, 0)),
                  pl.BlockSpec((None, kdim, tn), lambda j: (j // bw, 0, j % bw)),
                  pl.BlockSpec((None, kdim, tn), lambda j: (s // 2 + j // bw, 0, j % bw)), ANY],
        out_specs=(pl.BlockSpec((2, lp, tn), lambda j: (0, 0, j)), pl.BlockSpec((lp, tn), lambda j: (0, j))),
        compiler_params=_params(1),
    )(a, w4, w4, dep)


def _dact_swiglu_bwd(name, d, w, gu, tn_pref=512):
    lp, dm = d.shape
    f = w.shape[0]
    tn = _pick(f, tn_pref)
    chunk = _row_tile(lp, 1, 1, 576)

    def body(d_ref, w_ref, g_ref, u_ref, o_ref):
        for m0 in range(0, lp, chunk):
            sl = pl.ds(m0, chunk)
            dact = lax.dot_general(d_ref[sl, :], w_ref[...], _DIMS["nt"], preferred_element_type=F32)
            o_ref[0, sl, :] = (dact * g_ref[sl, :].astype(F32)).astype(BF16)
            o_ref[1, sl, :] = (dact * u_ref[sl, :].astype(F32)).astype(BF16)

    return pl.pallas_call(
        body, name=name, grid=(f // tn,), out_shape=jax.ShapeDtypeStruct((2, lp, f), BF16),
        in_specs=[pl.BlockSpec((lp, dm), lambda j: (0, 0)), pl.BlockSpec((tn, dm), lambda j: (j, 0)),
                  pl.BlockSpec((None, lp, tn), lambda j: (0, 0, j)), pl.BlockSpec((None, lp, tn), lambda j: (1, 0, j))],
        out_specs=pl.BlockSpec((2, lp, tn), lambda j: (0, 0, j)), compiler_params=_params(1),
    )(d, w, gu, gu)


def _pool_fwd(name, pooled, pw):
    lp, dm = pooled.shape
    g, gw, _ = pw.shape
    return _matmul(
        name, "nn", pooled, pw, jax.ShapeDtypeStruct((lp, dm), BF16), (g,),
        pl.BlockSpec((lp, gw), lambda gi: (0, gi)), pl.BlockSpec((None, gw, gw), lambda gi: (gi, 0, 0)),
        pl.BlockSpec((lp, gw), lambda gi: (0, gi)), 1)


def _pool_bwd_act(name, dya, pw, deps=()):
    lp, dm = dya.shape
    g, gw, _ = pw.shape
    return _matmul(
        name, "nt", dya, pw, jax.ShapeDtypeStruct((lp, dm), BF16), (g,),
        pl.BlockSpec((lp, gw), lambda gi: (0, gi)), pl.BlockSpec((None, gw, gw), lambda gi: (gi, 0, 0)),
        pl.BlockSpec((lp, gw), lambda gi: (0, gi)), 1, deps=deps)


def _pool_bwd_w(name, pooled, dya):
    lp, dm = pooled.shape
    g = len(POOL_WINDOWS)
    gw = dm // g
    return _matmul(
        name, "tn", pooled, dya, jax.ShapeDtypeStruct((g, gw, gw), BF16), (g,),
        pl.BlockSpec((lp, gw), lambda gi: (0, gi)), pl.BlockSpec((lp, gw), lambda gi: (0, gi)),
        pl.BlockSpec((None, gw, gw), lambda gi: (gi, 0, 0)), 1)


def _rms_fwd(name, h, g, tm, deps=()):
    lp, dm = h.shape

    def body(h_ref, g_ref, *rest):
        hv = h_ref[...]
        r = lax.rsqrt(jnp.mean(hv * hv, axis=-1, keepdims=True) + EPS)
        rest[-1][...] = (hv * r * g_ref[...]).astype(BF16)

    row = pl.BlockSpec((tm, dm), lambda i: (i, 0))
    return pl.pallas_call(
        body, name=name, out_shape=jax.ShapeDtypeStruct((lp, dm), BF16), grid=(lp // tm,),
        in_specs=[row, pl.BlockSpec((1, dm), lambda i: (0, 0))] + [ANY] * len(deps), out_specs=row, compiler_params=_params(1),
    )(h, g, *deps)


def _rms_fwd_into(name, src, g, lp, row0, tm, prev=None, deps=()):
    n, dm = src.shape
    b0 = row0 // tm
    n_in = 2 + len(deps)

    def body(s_ref, g_ref, *rest):
        h_ref, o_ref = rest[-2:]
        hv = s_ref[...]
        r = lax.rsqrt(jnp.mean(hv * hv, axis=-1, keepdims=True) + EPS)
        h_ref[...] = hv
        o_ref[...] = (hv * r * g_ref[...]).astype(BF16)

    row = pl.BlockSpec((tm, dm), lambda i: (b0 + i, 0))
    return pl.pallas_call(
        body, name=name, grid=(n // tm,),
        out_shape=(jax.ShapeDtypeStruct((lp, dm), F32), jax.ShapeDtypeStruct((lp, dm), BF16)),
        in_specs=[pl.BlockSpec((tm, dm), lambda i: (i, 0)), pl.BlockSpec((1, dm), lambda i: (0, 0))]
        + [ANY] * (len(deps) + (0 if prev is None else 2)),
        out_specs=(row, row), input_output_aliases={} if prev is None else {n_in: 0, n_in + 1: 1},
        compiler_params=_params(1),
    )(src, g, *deps, *(prev or ()))


def _rms_bwd(name, dy, h, g, dres, tm, dep):
    lp, dm = h.shape

    def body(dy_ref, h_ref, g_ref, dr_ref, _, dh_ref, dhb_ref, dg_ref):
        hv = h_ref[...]
        r = lax.rsqrt(jnp.mean(hv * hv, axis=-1, keepdims=True) + EPS)
        xhat = hv * r
        dyv = dy_ref[...]
        dxh = dyv * g_ref[...]
        dh = dr_ref[...] + r * (dxh - xhat * jnp.mean(dxh * xhat, axis=-1, keepdims=True))
        dh_ref[...] = dh
        dhb_ref[...] = dh.astype(BF16)

        @pl.when(pl.program_id(0) == 0)
        def _():
            dg_ref[...] = jnp.zeros_like(dg_ref)

        dg_ref[0:1, :] += jnp.sum(dyv * xhat, axis=0, keepdims=True)

    row = pl.BlockSpec((tm, dm), lambda i: (i, 0))
    slab = pl.BlockSpec((SMALL_ROWS, dm), lambda i: (0, 0))
    return pl.pallas_call(
        body, name=name, grid=(lp // tm,),
        out_shape=(jax.ShapeDtypeStruct((lp, dm), F32), jax.ShapeDtypeStruct((lp, dm), BF16),
                   jax.ShapeDtypeStruct((SMALL_ROWS, dm), F32)),
        in_specs=[row, row, pl.BlockSpec((1, dm), lambda i: (0, 0)), row, ANY], out_specs=(row, row, slab),
        compiler_params=_params(1),
    )(dy, h, g, dres, dep)


def _rms_bwd_rows(name, dy, h, g, dres, row0, nrows, tm, dep, dg_prev=None):
    dm = h.shape[1]
    b0 = row0 // tm

    def body(dy_ref, h_ref, g_ref, dr_ref, *rest):
        d_ref, dg_ref = rest[-2:]
        hv = h_ref[...]
        r = lax.rsqrt(jnp.mean(hv * hv, axis=-1, keepdims=True) + EPS)
        xhat = hv * r
        dyv = dy_ref[...]
        dxh = dyv * g_ref[...]
        d_ref[...] = dr_ref[...] + r * (dxh - xhat * jnp.mean(dxh * xhat, axis=-1, keepdims=True))

        @pl.when(pl.program_id(0) == 0)
        def _():
            dg_ref[...] = jnp.zeros_like(dg_ref) if dg_prev is None else rest[1][...]

        dg_ref[0:1, :] += jnp.sum(dyv * xhat, axis=0, keepdims=True)

    row = pl.BlockSpec((tm, dm), lambda i: (b0 + i, 0))
    slab = pl.BlockSpec((SMALL_ROWS, dm), lambda i: (0, 0))
    extra = [dep] + ([dg_prev] if dg_prev is not None else [])
    return pl.pallas_call(
        body, name=name, grid=(nrows // tm,),
        out_shape=(jax.ShapeDtypeStruct((nrows, dm), F32), jax.ShapeDtypeStruct((SMALL_ROWS, dm), F32)),
        in_specs=[row, row, pl.BlockSpec((1, dm), lambda i: (0, 0)), row, ANY] + ([slab] if dg_prev is not None else []),
        out_specs=(pl.BlockSpec((tm, dm), lambda i: (i, 0)), slab), compiler_params=_params(1),
    )(dy, h, g, dres, *extra)


def _gate_mix(name, proj, b_gate2, ya, pool_scale, yb, tm):
    _, lp, dm = proj.shape

    def body(ga_ref, gr_ref, b_ref, ya_ref, ps_ref, yb_ref, o_ref):
        g_a = jax.nn.sigmoid(ga_ref[...].astype(F32) + b_ref[0:1, :])
        g_b = jax.nn.sigmoid(gr_ref[...].astype(F32) + b_ref[1:2, :])
        y_a = ya_ref[...].astype(F32) * ps_ref[...]
        o_ref[...] = (g_a * y_a + g_b * yb_ref[...].astype(F32)).astype(BF16)

    row = pl.BlockSpec((tm, dm), lambda i: (i, 0))
    return pl.pallas_call(
        body, name=name, out_shape=jax.ShapeDtypeStruct((lp, dm), BF16), grid=(lp // tm,),
        in_specs=[pl.BlockSpec((None, tm, dm), lambda i: (4, i, 0)), pl.BlockSpec((None, tm, dm), lambda i: (5, i, 0)),
                  pl.BlockSpec((2, dm), lambda i: (0, 0)), row, pl.BlockSpec((1, dm), lambda i: (0, 0)), row],
        out_specs=row, compiler_params=_params(1),
    )(proj, proj, b_gate2, ya, pool_scale, yb)


def _gate_bwd(name, dmix, proj, b_gate2, ya, pool_scale, yb, tm):
    _, lp, dm = proj.shape

    def body(dm_ref, ga_ref, gr_ref, b_ref, ya_ref, ps_ref, yb_ref, dp_ref, dyb_ref, dya_ref, db_ref, dps_ref):
        dmx = dm_ref[...].astype(F32)
        g_a = jax.nn.sigmoid(ga_ref[...].astype(F32) + b_ref[0:1, :])
        g_b = jax.nn.sigmoid(gr_ref[...].astype(F32) + b_ref[1:2, :])
        ya_pre = ya_ref[...].astype(F32)
        ybv = yb_ref[...].astype(F32)
        ps = ps_ref[...]
        dga = dmx * (ya_pre * ps) * (g_a * (1.0 - g_a))
        dgr = dmx * ybv * (g_b * (1.0 - g_b))
        dp_ref[0] = dga.astype(BF16)
        dp_ref[1] = dgr.astype(BF16)
        dyb_ref[...] = (dmx * g_b).astype(BF16)
        dya_ref[...] = (dmx * g_a * ps).astype(BF16)

        @pl.when(pl.program_id(0) == 0)
        def _():
            db_ref[...] = jnp.zeros_like(db_ref)
            dps_ref[...] = jnp.zeros_like(dps_ref)

        db_ref[0:1, :] += jnp.sum(dga, axis=0, keepdims=True)
        db_ref[1:2, :] += jnp.sum(dgr, axis=0, keepdims=True)
        dps_ref[0:1, :] += jnp.sum(dmx * g_a * ya_pre, axis=0, keepdims=True)

    row = pl.BlockSpec((tm, dm), lambda i: (i, 0))
    one = pl.BlockSpec((1, dm), lambda i: (0, 0))
    slab = pl.BlockSpec((SMALL_ROWS, dm), lambda i: (0, 0))
    return pl.pallas_call(
        body, name=name, grid=(lp // tm,),
        out_shape=(jax.ShapeDtypeStruct((6, lp, dm), BF16), jax.ShapeDtypeStruct((lp, dm), BF16),
                   jax.ShapeDtypeStruct((lp, dm), BF16), jax.ShapeDtypeStruct((SMALL_ROWS, dm), F32),
                   jax.ShapeDtypeStruct((SMALL_ROWS, dm), F32)),
        in_specs=[row, pl.BlockSpec((None, tm, dm), lambda i: (4, i, 0)), pl.BlockSpec((None, tm, dm), lambda i: (5, i, 0)),
                  pl.BlockSpec((2, dm), lambda i: (0, 0)), row, one, row],
        out_specs=(pl.BlockSpec((2, tm, dm), lambda i: (2, i, 0)), row, row, slab, slab),
        compiler_params=_params(1),
    )(dmix, proj, proj, b_gate2, ya, pool_scale, yb)


def _final_loss(name, h2, g3, target, tm):
    lp, dm = h2.shape
    seq = target.shape[0]

    def body(h_ref, g_ref, t_ref, dh_ref, dhb_ref, ls_ref, dg_ref):
        @pl.when(pl.program_id(0) == 0)
        def _():
            ls_ref[...] = jnp.zeros_like(ls_ref)
            dg_ref[...] = jnp.zeros_like(dg_ref)

        hv = h_ref[...]
        gv = g_ref[...]
        r = lax.rsqrt(jnp.mean(hv * hv, axis=-1, keepdims=True) + EPS)
        xhat = hv * r
        err = xhat * gv - t_ref[...]
        dout = err * (1.0 / dm)
        dxh = dout * gv
        dh = r * (dxh - xhat * jnp.mean(dxh * xhat, axis=-1, keepdims=True))
        dh_ref[...] = dh
        dhb_ref[...] = dh.astype(BF16)
        ls_ref[0:1, :] += jnp.sum(err * err, axis=0, keepdims=True)
        dg_ref[0:1, :] += jnp.sum(dout * xhat, axis=0, keepdims=True)

    row = pl.BlockSpec((tm, dm), lambda i: (i, 0))
    slab = pl.BlockSpec((SMALL_ROWS, dm), lambda i: (0, 0))
    return pl.pallas_call(
        body, name=name, grid=(seq // tm,),
        out_shape=(jax.ShapeDtypeStruct((lp, dm), F32), jax.ShapeDtypeStruct((lp, dm), BF16),
                   jax.ShapeDtypeStruct((SMALL_ROWS, dm), F32), jax.ShapeDtypeStruct((SMALL_ROWS, dm), F32)),
        in_specs=[row, pl.BlockSpec((1, dm), lambda i: (0, 0)), row],
        out_specs=(row, row, slab, slab), compiler_params=_params(1),
    )(h2, g3, target)


def _zero_tail(name, arrays, tail):
    n = len(arrays)
    lp, dm = arrays[0].shape
    last = lp // tail - 1

    def body(*refs):
        for o_ref in refs[n:]:
            o_ref[...] = jnp.zeros_like(o_ref)

    return pl.pallas_call(
        body, name=name, grid=(1,), out_shape=tuple(jax.ShapeDtypeStruct(a.shape, a.dtype) for a in arrays),
        in_specs=[ANY] * n, out_specs=tuple(pl.BlockSpec((tail, dm), lambda i: (last, 0)) for _ in arrays),
        input_output_aliases={a: a for a in range(n)}, compiler_params=_params(1),
    )(*arrays)


def _shift(v, k):
    return pltpu.roll(v, k % v.shape[0], axis=0)


def _window_sum(v, group, sign):
    s2 = v + _shift(v, sign * 1)
    s4 = s2 + _shift(s2, sign * 2)
    s8 = s4 + _shift(s4, sign * 4)
    s16 = s8 + _shift(s8, sign * 8)
    return jnp.where(group == 0, s2, jnp.where(group == 1, s4, jnp.where(group == 2, s8, s16)))


def _pool_count(lp, group):
    row = lax.broadcasted_iota(jnp.int32, (lp, 1), 0)
    window = jnp.left_shift(2, group).astype(F32)
    meta_pos = (row - (lp - N_META) + 1).astype(F32)
    return jnp.where(row >= lp - N_META, jnp.minimum(meta_pos, window), window)


def _mixer_fwd(name, proj, conv_w, tc, dep):
    _, lp, dm = proj.shape
    per_group = dm // len(POOL_WINDOWS) // tc

    def body(u_ref, gb_ref, gc_ref, v_ref, cw_ref, _, p_ref, z_ref):
        group = pl.program_id(0) // per_group
        u = u_ref[...].astype(F32)
        p_ref[...] = (_window_sum(u, group, 1) / _pool_count(lp, group) - u).astype(BF16)
        cv = gc_ref[...].astype(F32) * v_ref[...].astype(F32)
        conv = cw_ref[0:1, :] * _shift(cv, 2) + cw_ref[1:2, :] * _shift(cv, 1) + cw_ref[2:3, :] * cv
        z_ref[...] = (gb_ref[...].astype(F32) * conv).astype(BF16)

    def seg(s):
        return pl.BlockSpec((None, lp, tc), lambda j: (s, 0, j))

    col = pl.BlockSpec((lp, tc), lambda j: (0, j))
    return pl.pallas_call(
        body, name=name, grid=(dm // tc,),
        out_shape=(jax.ShapeDtypeStruct((lp, dm), BF16), jax.ShapeDtypeStruct((lp, dm), BF16)),
        in_specs=[seg(0), seg(1), seg(2), seg(3), pl.BlockSpec((3, tc), lambda j: (0, j)), ANY],
        out_specs=(col, col), compiler_params=_params(1),
    )(proj, proj, proj, proj, conv_w, dep)


def _mixer_bwd(name, dz, dpooled, proj, conv_w, dproj, tc, dep):
    _, lp, dm = proj.shape
    per_group = dm // len(POOL_WINDOWS) // tc

    def body(dz_ref, dp_ref, gb_ref, gc_ref, v_ref, cw_ref, _, __, o_ref, dcw_ref):
        group = pl.program_id(0) // per_group
        dzv = dz_ref[...].astype(F32)
        gb = gb_ref[...].astype(F32)
        gc = gc_ref[...].astype(F32)
        vv = v_ref[...].astype(F32)
        cv = gc * vv
        c1 = _shift(cv, 1)
        c2 = _shift(cv, 2)
        w0, w1, w2 = cw_ref[0:1, :], cw_ref[1:2, :], cw_ref[2:3, :]
        o_ref[1] = (dzv * (w0 * c2 + w1 * c1 + w2 * cv)).astype(BF16)
        dconv = dzv * gb
        dcw_ref[...] = jnp.zeros_like(dcw_ref)
        dcw_ref[0:1, :] = jnp.sum(dconv * c2, axis=0, keepdims=True)
        dcw_ref[1:2, :] = jnp.sum(dconv * c1, axis=0, keepdims=True)
        dcw_ref[2:3, :] = jnp.sum(dconv * cv, axis=0, keepdims=True)
        dcv = w0 * _shift(dconv, -2) + w1 * _shift(dconv, -1) + w2 * dconv
        o_ref[2] = (dcv * vv).astype(BF16)
        o_ref[3] = (dcv * gc).astype(BF16)
        dpv = dp_ref[...].astype(F32)
        o_ref[0] = (_window_sum(dpv / _pool_count(lp, group), group, -1) - dpv).astype(BF16)

    def seg(s):
        return pl.BlockSpec((None, lp, tc), lambda j: (s, 0, j))

    col = pl.BlockSpec((lp, tc), lambda j: (0, j))
    return pl.pallas_call(
        body, name=name, grid=(dm // tc,),
        out_shape=(jax.ShapeDtypeStruct(dproj.shape, BF16), jax.ShapeDtypeStruct((SMALL_ROWS, dm), F32)),
        in_specs=[col, col, seg(1), seg(2), seg(3), pl.BlockSpec((3, tc), lambda j: (0, j)), ANY, ANY],
        out_specs=(pl.BlockSpec((4, lp, tc), lambda j: (0, 0, j)), pl.BlockSpec((SMALL_ROWS, tc), lambda j: (0, j))),
        input_output_aliases={6: 0}, compiler_params=_params(1),
    )(dz, dpooled, proj, proj, proj, conv_w, dproj, dep)


def _row_tile(r, c, bytes_per_row_elem=4, budget=2 * 1024 * 1024):
    best = None
    for t in range(16, r + 1, 16):
        if r % t == 0 and t * c * bytes_per_row_elem <= budget:
            best = t
    return best if best is not None else r


def _pair_add(name, g4, recv, core):
    s, r, c = g4.shape
    h = r // 2
    tr = _row_tile(h, c, budget=6 * 1024 * 1024)
    nb = h // tr

    def body(core_ref, g_ref, r_ref, o_ref):
        o_ref[...] = (g_ref[...].astype(F32) + r_ref[...].astype(F32)).astype(BF16)

    grid_spec = pltpu.PrefetchScalarGridSpec(
        num_scalar_prefetch=1, grid=(s, nb),
        in_specs=[pl.BlockSpec((None, tr, c), lambda si, j, core_ref: (si, core_ref[0] * nb + j, 0)),
                  pl.BlockSpec((None, tr, c), lambda si, j, core_ref: (si, j, 0))],
        out_specs=pl.BlockSpec((None, tr, c), lambda si, j, core_ref: (si, j, 0)))
    return pl.pallas_call(
        body, name=name, out_shape=jax.ShapeDtypeStruct((s, h, c), BF16), grid_spec=grid_spec,
        compiler_params=_params(2),
    )(core, g4, recv)


def _chip_sum(name, parts, recv, chip):
    _, h, c = parts.shape
    tr = _row_tile(h, c)

    def body(chip_ref, p_ref, r_ref, o_ref):
        acc = p_ref[...].astype(F32)
        for i in range(len(CHIP_FLIPS)):
            acc = acc + r_ref[i].astype(F32)
        o_ref[...] = acc

    grid_spec = pltpu.PrefetchScalarGridSpec(
        num_scalar_prefetch=1, grid=(h // tr,),
        in_specs=[pl.BlockSpec((None, tr, c), lambda j, chip_ref: (chip_ref[0], j, 0)),
                  pl.BlockSpec((len(CHIP_FLIPS), tr, c), lambda j, chip_ref: (0, j, 0))],
        out_specs=pl.BlockSpec((tr, c), lambda j, chip_ref: (j, 0)))
    return pl.pallas_call(
        body, name=name, out_shape=jax.ShapeDtypeStruct((h, c), F32), grid_spec=grid_spec, compiler_params=_params(1),
    )(chip, parts, recv)


def _adam_update(w, gv, m, v):
    c1 = 1.0 - ADAM_B1 ** ADAM_STEP
    c2 = 1.0 - ADAM_B2 ** ADAM_STEP
    nm = ADAM_B1 * m + (1.0 - ADAM_B1) * gv
    nv = ADAM_B2 * v + (1.0 - ADAM_B2) * (gv * gv)
    return -ADAM_LR * ((nm / c1) / (jnp.sqrt(nv / c2) + ADAM_EPS) + ADAM_WD * w), nm, nv


def _adamw_halves(name, w, g_own, g_sib, m, v, core, part=(0, 1), prev=None):
    r, c = w.shape
    rp = r // part[1]
    h = rp // 2
    tr = _row_tile(h, c, budget=2 * 1024 * 1024)
    nbh = h // tr
    j0 = part[0] * 2 * nbh
    n_prev = 0 if prev is None else 4

    def body(core_ref, w_ref, go_ref, gs_ref, m_ref, v_ref, *rest):
        g_ref, d_ref, nm_ref, nv_ref = rest[n_prev:]
        mine = (pl.program_id(0) // nbh) == core_ref[0]
        gv = jnp.where(mine, go_ref[...], gs_ref[...])
        g_ref[...] = gv
        d_ref[...], nm_ref[...], nv_ref[...] = _adam_update(w_ref[...], gv, m_ref[...], v_ref[...])

    def blk(fn):
        return pl.BlockSpec((tr, c), fn)

    full = blk(lambda j, core_ref: (j0 + j, 0))
    own = blk(lambda j, core_ref: (jnp.clip(j - core_ref[0] * nbh, 0, nbh - 1), 0))
    sib = blk(lambda j, core_ref: (jnp.clip(j - (1 - core_ref[0]) * nbh, 0, nbh - 1), 0))
    grid_spec = pltpu.PrefetchScalarGridSpec(
        num_scalar_prefetch=1, grid=(2 * nbh,), in_specs=[full, own, sib, full, full] + [ANY] * n_prev, out_specs=(full,) * 4)
    sds = jax.ShapeDtypeStruct((r, c), F32)
    return pl.pallas_call(
        body, name=name, out_shape=(sds,) * 4, grid_spec=grid_spec, compiler_params=_params(1),
        input_output_aliases={6 + i: i for i in range(n_prev)},
    )(core, w, g_own, g_sib, m, v, *(prev or ()))


def _adamw_small(name, red, rows, cols, params):
    n = len(params)
    whole = pl.BlockSpec(memory_space=pltpu.VMEM)

    def body(red_ref, *refs):
        ins, outs = refs[:3 * n], refs[3 * n:]
        chip = 2 * lax.axis_index("x") + lax.axis_index("y")
        for i in range(n):
            w_ref, m_ref, v_ref = ins[3 * i:3 * i + 3]
            r, c = w_ref.shape
            g = red_ref[pl.ds(rows[i], r), pl.ds(pl.multiple_of(chip * c, LANES), c)] if cols[i] else red_ref[pl.ds(rows[i], r), :]
            outs[4 * i][...] = g
            outs[4 * i + 1][...], outs[4 * i + 2][...], outs[4 * i + 3][...] = _adam_update(w_ref[...], g, m_ref[...], v_ref[...])

    flat = [t for p in params for t in p]
    res = pl.pallas_call(
        body, name=name, out_shape=tuple(jax.ShapeDtypeStruct(p[0].shape, F32) for p in params for _ in range(4)),
        in_specs=[whole] * (1 + 3 * n), out_specs=(whole,) * (4 * n), compiler_params=_params(),
    )(red, *flat)
    return [res[4 * i:4 * i + 4] for i in range(n)]


def _cast_into_slot(name, w, chip, dtype, deps=()):
    r, c = w.shape
    tr = _row_tile(r, c)

    def body(chip_ref, w_ref, *rest):
        rest[-1][...] = w_ref[...].astype(dtype)

    grid_spec = pltpu.PrefetchScalarGridSpec(
        num_scalar_prefetch=1, grid=(r // tr,),
        in_specs=[pl.BlockSpec((tr, c), lambda j, chip_ref: (j, 0))] + [ANY] * len(deps),
        out_specs=pl.BlockSpec((None, tr, c), lambda j, chip_ref: (chip_ref[0], j, 0)))
    return pl.pallas_call(
        body, name=name, out_shape=jax.ShapeDtypeStruct((4, r, c), dtype), grid_spec=grid_spec, compiler_params=_params(1),
    )(chip, w, *deps)


def _cast_half_into_slot(name, w, chip_half, dtype, into=None):
    r, c = w.shape
    h = r // 2
    tr = _row_tile(h, c)
    nb = h // tr
    n_prev = 0 if into is None else 1

    def body(ids_ref, w_ref, *rest):
        rest[-1][...] = w_ref[...].astype(dtype)

    grid_spec = pltpu.PrefetchScalarGridSpec(
        num_scalar_prefetch=1, grid=(nb,),
        in_specs=[pl.BlockSpec((tr, c), lambda j, ids_ref: (ids_ref[1] * nb + j, 0))] + [ANY] * n_prev,
        out_specs=pl.BlockSpec((None, tr, c), lambda j, ids_ref: (ids_ref[0], ids_ref[1] * nb + j, 0)))
    return pl.pallas_call(
        body, name=name, out_shape=jax.ShapeDtypeStruct((4, r, c), dtype), grid_spec=grid_spec, compiler_params=_params(1),
        input_output_aliases={2: 0} if into is not None else {},
    )(chip_half, w, *([into] if into is not None else []))


def _place():
    return lax.axis_index("x"), lax.axis_index("y"), lax.axis_index("c")


def _chip_of(x, y, flip):
    px, py = x ^ flip[0], y ^ flip[1]
    return px, py, 2 * px + py


def _half(ref, which):
    rows = ref.shape[0] // 2
    return ref.at[pl.ds(which * rows, rows)]


HBM = pl.BlockSpec(memory_space=pltpu.HBM)
SEM = pl.BlockSpec(memory_space=pltpu.SEMAPHORE)
SPLIT_COPY = pltpu.CompilerParams(has_side_effects=pltpu.SideEffectType.DATAFLOW_SIDE_EFFECTING)


def _in_hbm(arrays):
    return [pltpu.with_memory_space_constraint(t, pltpu.HBM) for t in arrays]


TOKEN = jax.ShapeDtypeStruct((SMALL_ROWS, LANES), F32)
TOKEN_SPEC = pl.BlockSpec(memory_space=pltpu.VMEM)


NEIGHBOUR_FLIPS = CHIP_FLIPS[:2]


def _relay_chips(x, y, c):
    fx, fy = x ^ c, y ^ (1 - c)
    return (fx, fy), 2 * fx + fy, 2 * (1 - x) + (1 - y)


def _ag_start(name, slabs, deps=()):
    n = len(slabs)
    nn = len(NEIGHBOUR_FLIPS)

    def body(*refs):
        no = n + len(deps)
        ssem, rsem = refs[no], refs[no + 1]
        outs = refs[no + 2:no + 2 + n]
        token = refs[no + 2 + n]
        token[...] = jnp.zeros_like(token)
        x, y, c = _place()
        k = 2 * x + y
        for a in range(n):
            for j, flip in enumerate(NEIGHBOUR_FLIPS):
                px, py, _ = _chip_of(x, y, flip)
                mine = _half(outs[a].at[k], c)
                pltpu.make_async_remote_copy(src_ref=mine, dst_ref=mine, send_sem=ssem.at[a * nn + j],
                                             recv_sem=rsem.at[a * nn + j], device_id=(px, py, c), device_id_type=MESH).start()

    sem = pltpu.SemaphoreType.DMA((nn * n,))
    res = pl.pallas_call(
        body, name=name, out_shape=(sem, sem) + tuple(pltpu.HBM(t.shape, t.dtype) for t in slabs) + (TOKEN,),
        in_specs=[HBM] * n + [ANY] * len(deps), out_specs=tuple([SEM, SEM] + [HBM] * n + [TOKEN_SPEC]),
        input_output_aliases={a: 2 + a for a in range(n)}, compiler_params=SPLIT_COPY,
    )(*_in_hbm(slabs), *deps)
    return (res[0], res[1]), list(res[2:2 + n]), res[2 + n]


def _ag_relay(name, slabs, sems, after, then_start=()):
    n = len(slabs)
    m = len(then_start)
    nn = len(NEIGHBOUR_FLIPS)

    def body(*refs):
        no = n + 2 + m + len(after)
        ins = refs[:n]
        ssem, rsem = refs[n], refs[n + 1]
        r_s, r_r, p_s, p_r = refs[no:no + 4]
        x, y, c = _place()
        k = 2 * x + y
        (fx, fy), _, _ = _relay_chips(x, y, c)
        for a in range(n):
            for j, flip in enumerate(NEIGHBOUR_FLIPS):
                _, _, kj = _chip_of(x, y, flip)
                landed = _half(ins[a].at[kj], c)
                cp = pltpu.make_async_remote_copy(
                    src_ref=_half(ins[a].at[k], c), dst_ref=landed, send_sem=ssem.at[a * nn + j],
                    recv_sem=rsem.at[a * nn + j], device_id=(x, y, c), device_id_type=MESH)
                cp.wait_send()
                cp.wait_recv()
        for a in range(n):
            near = _half(ins[a].at[2 * (x ^ (1 - c)) + (y ^ c)], c)
            pltpu.make_async_remote_copy(src_ref=near, dst_ref=near, send_sem=r_s.at[a], recv_sem=r_r.at[a],
                                         device_id=(fx, fy, c), device_id_type=MESH).start()
            for j, flip in enumerate(NEIGHBOUR_FLIPS):
                _, _, kj = _chip_of(x, y, flip)
                landed = _half(ins[a].at[kj], c)
                pltpu.make_async_remote_copy(src_ref=landed, dst_ref=landed, send_sem=p_s.at[a * nn + j],
                                             recv_sem=p_r.at[a * nn + j], device_id=(x, y, 1 - c), device_id_type=MESH).start()
        if m:
            d_s, d_r = refs[no + 4 + n], refs[no + 5 + n]
            nxt = refs[no + 6 + n:]
            for a in range(m):
                for j, flip in enumerate(NEIGHBOUR_FLIPS):
                    px, py, _ = _chip_of(x, y, flip)
                    mine = _half(nxt[a].at[k], c)
                    pltpu.make_async_remote_copy(src_ref=mine, dst_ref=mine, send_sem=d_s.at[a * nn + j],
                                                 recv_sem=d_r.at[a * nn + j], device_id=(px, py, c), device_id_type=MESH).start()

    rsem_t = pltpu.SemaphoreType.DMA((n,))
    psem_t = pltpu.SemaphoreType.DMA((nn * n,))
    out_shape = (rsem_t, rsem_t, psem_t, psem_t) + tuple(pltpu.HBM(t.shape, t.dtype) for t in slabs)
    out_specs = [SEM] * 4 + [HBM] * n
    aliases = {a: 4 + a for a in range(n)}
    if m:
        dsem_t = pltpu.SemaphoreType.DMA((nn * m,))
        out_shape += (dsem_t, dsem_t) + tuple(pltpu.HBM(t.shape, t.dtype) for t in then_start)
        out_specs += [SEM, SEM] + [HBM] * m
        aliases.update({n + 2 + a: 4 + n + 2 + a for a in range(m)})
    res = pl.pallas_call(
        body, name=name, out_shape=out_shape, in_specs=[HBM] * n + [SEM, SEM] + [HBM] * m + [ANY] * len(after),
        out_specs=tuple(out_specs), input_output_aliases=aliases, compiler_params=SPLIT_COPY,
    )(*slabs, sems[0], sems[1], *_in_hbm(list(then_start)), *after)
    if not m:
        return tuple(res[:4]), list(res[4:])
    return (tuple(res[:4]), list(res[4:4 + n])), ((res[4 + n], res[5 + n]), list(res[6 + n:]))


def _wait_passes(ins, p_s, p_r, x, y, c):
    nn = len(NEIGHBOUR_FLIPS)
    for a in range(len(ins)):
        for j, flip in enumerate(NEIGHBOUR_FLIPS):
            _, _, kj = _chip_of(x, y, flip)
            cp = pltpu.make_async_remote_copy(
                src_ref=_half(ins[a].at[kj], c), dst_ref=_half(ins[a].at[kj], 1 - c), send_sem=p_s.at[a * nn + j],
                recv_sem=p_r.at[a * nn + j], device_id=(x, y, c), device_id_type=MESH)
            cp.wait_send()
            cp.wait_recv()


def _ag_relay_wait(name, slabs, sems, after):
    n = len(slabs)
    ns = len(sems)

    def body(*refs):
        no = n + ns + len(after)
        ins = refs[:n]
        r_s, r_r = refs[n], refs[n + 1]
        f_s, f_r = refs[no], refs[no + 1]
        x, y, c = _place()
        _, _, kd = _relay_chips(x, y, c)
        for a in range(n):
            near = _half(ins[a].at[2 * (x ^ (1 - c)) + (y ^ c)], c)
            cp = pltpu.make_async_remote_copy(src_ref=near, dst_ref=_half(ins[a].at[kd], c), send_sem=r_s.at[a],
                                              recv_sem=r_r.at[a], device_id=(x, y, c), device_id_type=MESH)
            cp.wait_send()
            cp.wait_recv()
        if ns == 4:
            _wait_passes(ins, refs[n + 2], refs[n + 3], x, y, c)
        for a in range(n):
            diag = _half(ins[a].at[kd], c)
            pltpu.make_async_remote_copy(src_ref=diag, dst_ref=diag, send_sem=f_s.at[a], recv_sem=f_r.at[a],
                                         device_id=(x, y, 1 - c), device_id_type=MESH).start()

    sem = pltpu.SemaphoreType.DMA((n,))
    res = pl.pallas_call(
        body, name=name, out_shape=(sem, sem) + tuple(pltpu.HBM(t.shape, t.dtype) for t in slabs),
        in_specs=[HBM] * n + [SEM] * ns + [ANY] * len(after), out_specs=tuple([SEM, SEM] + [HBM] * n),
        input_output_aliases={a: 2 + a for a in range(n)}, compiler_params=SPLIT_COPY,
    )(*slabs, *sems, *after)
    return (res[0], res[1]), list(res[2:])


def _ag_final_wait(name, slabs, sems, after, first=0):
    n = len(slabs)

    def body(*refs):
        ins = refs[:n]
        f_s, f_r = refs[n], refs[n + 1]
        x, y, c = _place()
        _, _, kd = _relay_chips(x, y, c)
        for a in range(n):
            cp = pltpu.make_async_remote_copy(
                src_ref=_half(ins[a].at[kd], c), dst_ref=_half(ins[a].at[kd], 1 - c), send_sem=f_s.at[first + a],
                recv_sem=f_r.at[first + a], device_id=(x, y, c), device_id_type=MESH)
            cp.wait_send()
            cp.wait_recv()

    return pl.pallas_call(
        body, name=name, out_shape=tuple(pltpu.HBM(t.shape, t.dtype) for t in slabs),
        in_specs=[HBM] * n + [SEM, SEM] + [ANY] * len(after), out_specs=tuple([HBM] * n),
        input_output_aliases={a: a for a in range(n)}, compiler_params=SPLIT_COPY,
    )(*slabs, sems[0], sems[1], *after)


def _sibling_part(ref, c, halves):
    if not halves:
        return ref
    h = ref.shape[1] // 2
    return ref.at[:, pl.ds((1 - c) * h, h)]


def _swap_start(name, grads, halves=True, deps=()):
    n = len(grads)

    def body(*refs):
        no = 2 * n + len(deps)
        ssem, rsem = refs[no], refs[no + 1]
        src, land = refs[no + 2:no + n + 2], refs[no + n + 2:no + 2 * n + 2]
        token = refs[no + 2 * n + 2]
        token[...] = jnp.zeros_like(token)
        x, y, c = _place()
        for a in range(n):
            pltpu.make_async_remote_copy(
                src_ref=_sibling_part(src[a], c, halves), dst_ref=land[a], send_sem=ssem.at[a], recv_sem=rsem.at[a],
                device_id=(x, y, 1 - c), device_id_type=MESH).start()

    zones = [lax.empty((g.shape[0], g.shape[1] // 2, g.shape[2]) if halves else g.shape, g.dtype) for g in grads]
    sem = pltpu.SemaphoreType.DMA((n,))
    res = pl.pallas_call(
        body, name=name,
        out_shape=(sem, sem) + tuple(pltpu.HBM(t.shape, t.dtype) for t in list(grads) + zones) + (TOKEN,),
        in_specs=[HBM] * (2 * n) + [ANY] * len(deps), out_specs=tuple([SEM, SEM] + [HBM] * (2 * n) + [TOKEN_SPEC]),
        input_output_aliases={i: 2 + i for i in range(2 * n)}, compiler_params=SPLIT_COPY,
    )(*_in_hbm(list(grads) + zones), *deps)
    return (res[0], res[1], list(res[2:2 + n]), list(res[2 + n:2 + 2 * n])), res[2 + 2 * n]


def _swap_wait(name, ssem, rsem, grads, zones, after, halves=True):
    n = len(grads)

    def body(*refs):
        src, land = refs[:n], refs[n:2 * n]
        ss, rs = refs[2 * n], refs[2 * n + 1]
        x, y, c = _place()
        for a in range(n):
            cp = pltpu.make_async_remote_copy(
                src_ref=_sibling_part(src[a], c, halves), dst_ref=land[a], send_sem=ss.at[a], recv_sem=rs.at[a],
                device_id=(x, y, c), device_id_type=MESH)
            cp.wait_send()
            cp.wait_recv()

    res = pl.pallas_call(
        body, name=name, out_shape=tuple(pltpu.HBM(t.shape, t.dtype) for t in list(grads) + list(zones)),
        in_specs=[HBM] * (2 * n) + [SEM, SEM] + [ANY] * len(after), out_specs=tuple([HBM] * (2 * n)),
        input_output_aliases={i: i for i in range(2 * n)}, compiler_params=SPLIT_COPY,
    )(*grads, *zones, ssem, rsem, *after)
    return list(res[:n]), list(res[n:])


def _scatter_start(name, parts):
    n = len(parts)
    nf = len(CHIP_FLIPS)

    def body(*refs):
        ssem, rsem = refs[2 * n], refs[2 * n + 1]
        src, land = refs[2 * n + 2:3 * n + 2], refs[3 * n + 2:4 * n + 2]
        token = refs[4 * n + 2]
        token[...] = jnp.zeros_like(token)
        x, y, c = _place()
        for a in range(n):
            for j, flip in enumerate(CHIP_FLIPS):
                px, py, kj = _chip_of(x, y, flip)
                pltpu.make_async_remote_copy(
                    src_ref=src[a].at[kj], dst_ref=land[a].at[j], send_sem=ssem.at[a * nf + j], recv_sem=rsem.at[a * nf + j],
                    device_id=(px, py, c), device_id_type=MESH).start()

    zones = [lax.empty((nf,) + p.shape[1:], p.dtype) for p in parts]
    sem = pltpu.SemaphoreType.DMA((nf * n,))
    res = pl.pallas_call(
        body, name=name,
        out_shape=(sem, sem) + tuple(pltpu.HBM(t.shape, t.dtype) for t in list(parts) + zones)
        + (jax.ShapeDtypeStruct((SMALL_ROWS, LANES), F32),),
        in_specs=[HBM] * (2 * n),
        out_specs=tuple([SEM, SEM] + [HBM] * (2 * n) + [pl.BlockSpec(memory_space=pltpu.VMEM)]),
        input_output_aliases={i: 2 + i for i in range(2 * n)}, compiler_params=SPLIT_COPY,
    )(*_in_hbm(list(parts) + zones))
    return (res[0], res[1], list(res[2:2 + n]), list(res[2 + n:2 + 2 * n])), res[2 + 2 * n]


def _scatter_wait(name, ssem, rsem, parts, zones, after):
    n = len(parts)
    nf = len(CHIP_FLIPS)

    def body(*refs):
        src, land = refs[:n], refs[n:2 * n]
        ss, rs = refs[2 * n], refs[2 * n + 1]
        x, y, c = _place()
        for a in range(n):
            for j, flip in enumerate(CHIP_FLIPS):
                _, _, kj = _chip_of(x, y, flip)
                cp = pltpu.make_async_remote_copy(
                    src_ref=src[a].at[kj], dst_ref=land[a].at[j], send_sem=ss.at[a * nf + j], recv_sem=rs.at[a * nf + j],
                    device_id=(x, y, c), device_id_type=MESH)
                cp.wait_send()
                cp.wait_recv()

    res = pl.pallas_call(
        body, name=name, out_shape=tuple(pltpu.HBM(t.shape, t.dtype) for t in list(parts) + list(zones)),
        in_specs=[HBM] * (2 * n) + [SEM, SEM] + [ANY] * len(after), out_specs=tuple([HBM] * (2 * n)),
        input_output_aliases={i: i for i in range(2 * n)}, compiler_params=SPLIT_COPY,
    )(*parts, *zones, ssem, rsem, *after)
    return list(res[:n]), list(res[n:])


N_PEERS = 7


def _peer(x, y, c, mask):
    px, py, pc = x ^ ((mask >> 2) & 1), y ^ ((mask >> 1) & 1), c ^ (mask & 1)
    return (px, py, pc), 4 * px + 2 * py + pc


def _reduce_start(vec, deps):
    nd = len(deps)

    def body(*refs):
        ssem, rsem, src, land, token = refs[2 + nd:]
        token[...] = jnp.zeros_like(token)
        x, y, c = _place()
        me = 4 * x + 2 * y + c
        for mask in range(1, N_PEERS + 1):
            to, _ = _peer(x, y, c, mask)
            pltpu.make_async_remote_copy(src_ref=src, dst_ref=land.at[me], send_sem=ssem.at[mask - 1],
                                         recv_sem=rsem.at[mask - 1], device_id=to, device_id_type=MESH).start()

    zone = lax.empty((N_PEERS + 1,) + vec.shape, vec.dtype)
    sem = pltpu.SemaphoreType.DMA((N_PEERS,))
    res = pl.pallas_call(
        body, name="reduce_start",
        out_shape=(sem, sem, pltpu.HBM(vec.shape, vec.dtype), pltpu.HBM(zone.shape, zone.dtype), TOKEN),
        in_specs=[HBM, HBM] + [ANY] * nd, out_specs=(SEM, SEM, HBM, HBM, TOKEN_SPEC),
        input_output_aliases={0: 2, 1: 3}, compiler_params=SPLIT_COPY,
    )(*_in_hbm([vec, zone]), *deps)
    return res[:4], res[4]


def _reduce_wait(ssem, rsem, vec, zone, after):
    def body(src, land, ss, rs, *_):
        x, y, c = _place()
        for mask in range(1, N_PEERS + 1):
            _, frm = _peer(x, y, c, mask)
            cp = pltpu.make_async_remote_copy(src_ref=src, dst_ref=land.at[frm], send_sem=ss.at[mask - 1],
                                              recv_sem=rs.at[mask - 1], device_id=(x, y, c), device_id_type=MESH)
            cp.wait_send()
            cp.wait_recv()

    return pl.pallas_call(
        body, name="reduce_wait", out_shape=(pltpu.HBM(vec.shape, vec.dtype), pltpu.HBM(zone.shape, zone.dtype)),
        in_specs=[HBM, HBM, SEM, SEM] + [ANY] * len(after), out_specs=(HBM, HBM),
        input_output_aliases={0: 0, 1: 1}, compiler_params=SPLIT_COPY,
    )(vec, zone, ssem, rsem, *after)


def _reduce_sum(vec, zone, me, loss_row, loss_scale):
    r, dm = vec.shape

    def body(me_ref, v_ref, z_ref, o_ref, l_ref):
        acc = None
        for i in range(N_PEERS + 1):
            term = jnp.where(me_ref[0] == i, v_ref[...], z_ref[i])
            acc = term if acc is None else acc + term
        o_ref[...] = acc
        l_ref[...] = jnp.sum(acc[loss_row:loss_row + SMALL_ROWS, :], axis=(0, 1), keepdims=True) * loss_scale

    grid_spec = pltpu.PrefetchScalarGridSpec(
        num_scalar_prefetch=1, grid=(1,),
        in_specs=[pl.BlockSpec((r, dm), lambda i, me_ref: (0, 0)), pl.BlockSpec((N_PEERS + 1, r, dm), lambda i, me_ref: (0, 0, 0))],
        out_specs=(pl.BlockSpec((r, dm), lambda i, me_ref: (0, 0)), pl.BlockSpec((1, 1), lambda i, me_ref: (0, 0))))
    return pl.pallas_call(
        body, name="reduce_sum", out_shape=(jax.ShapeDtypeStruct((r, dm), F32), jax.ShapeDtypeStruct((1, 1), F32)),
        grid_spec=grid_spec, compiler_params=_params(1),
    )(me, vec, zone)


def kernel(x, meta_tokens, norm_mix_g, w_in, b_gate, pool_w, pool_scale, conv_w, conv_out_w, w_o, norm_ffn_g, w_gate_up, w_down, norm_final_g, loss_target, m_meta_tokens, m_norm_mix_g, m_w_in, m_b_gate, m_pool_w, m_pool_scale, m_conv_w, m_conv_out_w, m_w_o, m_norm_ffn_g, m_w_gate_up, m_w_down, m_norm_final_g, v_meta_tokens, v_norm_mix_g, v_w_in, v_b_gate, v_pool_w, v_pool_scale, v_conv_w, v_conv_out_w, v_w_o, v_norm_ffn_g, v_w_gate_up, v_w_down, v_norm_final_g):
    seq, dm = x.shape[1], x.shape[2]
    tail = TAIL_ROWS
    lp = seq + tail
    tm_row = _row_tile(lp, dm, 4, 3 * 1024 * 1024)
    tm_seq = _row_tile(seq, dm, 4, 3 * 1024 * 1024)
    n_chips = 4
    n_groups = len(POOL_WINDOWS)
    gw = dm // n_groups
    tc = min(256, gw)
    cx, cy, cc = _place()
    chip = 2 * cx + cy
    dloc = dm // n_chips

    pool2 = pool_w.reshape(n_groups * pool_w.shape[1], gw)
    big = {"w_in": w_in, "w_gate_up": w_gate_up, "pool_w": pool2, "conv_out_w": conv_out_w, "w_o": w_o, "w_down": w_down}
    chip1 = jnp.reshape(chip, (1,)).astype(jnp.int32)
    core = jnp.reshape(cc, (1,)).astype(jnp.int32)
    small_loc = jnp.concatenate([meta_tokens, jnp.pad(conv_w, ((0, 8 - conv_w.shape[0]), (0, 0))),
                                 jnp.zeros((8, dloc), F32)], axis=0)
    g1, g2, g3 = norm_mix_g.reshape(1, dm), norm_ffn_g.reshape(1, dm), norm_final_g.reshape(1, dm)
    b_gate2 = b_gate.reshape(2, dm)
    ps = pool_scale.reshape(1, dm)
    mine = jnp.stack([chip, cc]).astype(jnp.int32)
    other = jnp.stack([chip, 1 - cc]).astype(jnp.int32)
    first = [_cast_into_slot("place_small", small_loc, chip1, F32), _cast_half_into_slot("cast_w_in_sent", w_in, mine, BF16)]
    sems, first, token = _ag_start("ag_start_first", first)
    first[1] = _cast_half_into_slot("cast_w_in_kept", w_in, other, BF16, into=first[1])
    cast = {nme: _cast_into_slot("cast_" + nme, big[nme], chip1, BF16, deps=(token,))
            for nme in ["pool_w", "conv_out_w", "w_o", "w_gate_up", "w_down"]}
    sems, first = _ag_relay("ag_relay_first", first, sems, list(cast.values()))
    sems, (small4, w_in4) = _ag_relay_wait("ag_relay_wait_first", first, sems, [])
    (small4,) = _ag_final_wait("ag_final_wait_small", [small4], sems, [])
    mixer_w = [cast["pool_w"], cast["conv_out_w"], cast["w_o"]]
    sems_mix, mixer_w, token = _ag_start("ag_start_mixer", mixer_w, deps=(small4,))
    sems_gu, (w_gu4,), token = _ag_start("ag_start_gate_up", [cast["w_gate_up"]], deps=(token,))

    small_f = jnp.transpose(small4, (1, 0, 2)).reshape(small4.shape[1], dm)
    meta_f = small_f[:N_META]
    conv_w_f = small_f[N_META:N_META + 3]
    tail_rows = jnp.concatenate([jnp.zeros((tail - N_META, dm), F32), meta_f], axis=0)
    h0_hn1 = _rms_fwd_into("rms_mix", x[0], g1, lp, 0, tm_seq, deps=(token,))
    h0, hn1 = _rms_fwd_into("rms_mix_tail", tail_rows, g1, lp, seq, tail, prev=h0_hn1)
    (w_in4,) = _ag_final_wait("ag_final_wait_first", [w_in4], sems, [hn1], first=1)
    proj = _nn_sharded("proj_0", hn1, w_in4, 6, part=(0, 2))
    sems_mix, mixer_w = _ag_relay("ag_relay_mixer", mixer_w, sems_mix, [proj])
    proj = _nn_sharded("proj_1", hn1, w_in4, 6, part=(1, 2), prev=proj, deps=(mixer_w[0],))
    pooled, z = _mixer_fwd("mixer_fwd", proj, conv_w_f, tc, mixer_w[0])
    (sems_gu, (w_gu4,)), (sems_down, (w_down4,)) = _ag_relay("ag_relay_gate_up", [w_gu4], sems_gu, [pooled],
                                                              then_start=[cast["w_down"]])
    sems_mix, mixer_w = _ag_relay_wait("ag_relay_wait_mixer", mixer_w, sems_mix, [w_down4])
    pool4, conv_out4, w_o4 = _ag_final_wait("ag_final_wait_mixer", mixer_w, sems_mix, [])
    pool_f = jnp.transpose(pool4.reshape(n_chips, n_groups, gw // n_chips, gw), (1, 0, 2, 3)).reshape(n_groups, gw, gw)
    conv_out_f = conv_out4.reshape(dm, dm)
    w_o_f = w_o4.reshape(dm, dm)
    ya = _pool_fwd("pool_proj", pooled, pool_f)
    yb = _nn_plain("conv_out", z, conv_out_f, BF16)
    mix = _gate_mix("gate_mix", proj, b_gate2, ya, ps, yb, tm_row)
    h1 = _nn_plain("attn_out", mix, w_o_f, F32, res=h0, tn_pref=256)
    sems_gu, (w_gu4,) = _ag_relay_wait("ag_relay_wait_gate_up", [w_gu4], sems_gu, [h1])
    hn2 = _rms_fwd("rms_ffn", h1, g2, tm_row, deps=(w_gu4,))
    (w_gu4,) = _ag_final_wait("ag_final_wait_gate_up", [w_gu4], sems_gu, [hn2])
    sems_down, (w_down4,) = _ag_relay("ag_relay_down", [w_down4], sems_down, [hn2])
    gu, act = _gate_up_swiglu("gate_up", hn2, w_gu4, w_down4)
    sems_down, (w_down4,) = _ag_relay_wait("ag_relay_wait_down", [w_down4], sems_down, [act])
    (w_down4,) = _ag_final_wait("ag_final_wait_down", [w_down4], sems_down, [])
    w_down_f = w_down4.reshape(-1, dm)
    h2 = _nn_rows("ffn_down", act, w_down_f, h1)
    dh2, dh2b, loss_cols, dg3 = _final_loss("final_loss", h2, g3, loss_target[0], tm_seq)
    dh2, dh2b = _zero_tail("final_loss_tail", [dh2, dh2b], tail)

    def scatter(tag, names_g, swap, after):
        grads_g, got = _swap_wait("swap_wait_" + tag, *swap, [after])
        pairs = [_pair_add("pair_add_" + nme, g4, rv, core) for nme, g4, rv in zip(names_g, grads_g, got)]
        return _scatter_start("scatter_start_" + tag, pairs)

    dgu = _dact_swiglu_bwd("d_gate_up", dh2b, w_down_f, gu)
    gw_down = _tn_plain("dw_down", act, dh2b)
    gw_gu = _tn_sharded("dw_gate_up", hn2, dgu, n_chips)
    swap_a, token = _swap_start("swap_start_a", [gw_gu, gw_down.reshape(n_chips, -1, dm)])
    dhn2 = _nt_sharded("d_hn2", dgu, w_gu4, tr_pref=2816, row_tiles=2, deps=(token,))
    flight_a, token = scatter("a", ["w_gate_up", "w_down"], swap_a, dhn2)
    dh1, dh1b, dg2 = _rms_bwd("rms_ffn_bwd", dhn2, h1, g2, dh2, tm_row, token)
    dmix = _nt_plain("d_mix", dh1b, w_o_f)
    gw_o = _tn_plain("dw_o", mix, dh1b)
    dproj, dyb, dya, db_gate, dps = _gate_bwd("gate_bwd", dmix, proj, b_gate2, ya, ps, yb, tm_row)
    gw_conv_out = _tn_plain("dw_conv_out", z, dyb)
    gw_pool = _pool_bwd_w("dw_pool", pooled, dya)
    gw_pool = jnp.transpose(gw_pool.reshape(n_groups, n_chips, gw // n_chips, gw), (1, 0, 2, 3))
    swap_b, token = _swap_start("swap_start_b", [gw_o.reshape(n_chips, dloc, dm), gw_conv_out.reshape(n_chips, dloc, dm),
                                                 gw_pool.reshape(n_chips, n_groups * (gw // n_chips), gw)])
    dpooled = _pool_bwd_act("d_pooled", dya, pool_f, deps=(token,))
    dz = _nt_plain("d_z", dyb, conv_out_f)
    flight_b, token = scatter("b", ["w_o", "conv_out_w", "pool_w"], swap_b, dz)
    dproj, dconv_w = _mixer_bwd("mixer_bwd", dz, dpooled, proj, conv_w_f, dproj, tc, token)
    gw_in0 = _tn_sharded("dw_in_0", hn1, dproj, n_chips, part=(0, 2))
    swap_c0, token = _swap_start("swap_start_c0", [gw_in0])
    gw_in1 = _tn_sharded("dw_in_1", hn1, dproj, n_chips, part=(1, 2), deps=(token,))
    flight_c0, token = scatter("c0", ["w_in_0"], swap_c0, gw_in1)
    swap_c, token = _swap_start("swap_start_c", [gw_in1], deps=(token,))
    groups_g = {"a": [("w_gate_up", (0, 1)), ("w_down", (0, 1))], "b": [("w_o", (0, 1)), ("conv_out_w", (0, 1)), ("pool_w", (0, 1))],
                "c0": [("w_in", (0, 2))], "c": [("w_in", (1, 2))]}

    def reduced(tag, flight, after):
        pairs, zones = _scatter_wait("scatter_wait_" + tag, *flight, after)
        halves = [_chip_sum("chip_sum_%s_%d" % (nme, part[0]), p, rv, chip1) for (nme, part), p, rv in zip(groups_g[tag], pairs, zones)]
        return _swap_start("send_start_" + tag, halves, halves=False)

    send_a, token = reduced("a", flight_a, [token])
    flight_c, token = scatter("c", ["w_in_1"], swap_c, token)
    dhn1 = _nt_in_proj("d_hn1", dproj, w_in4, deps=(token,))
    dx, dg1 = _rms_bwd_rows("rms_mix_bwd", dhn1, h0, g1, dh1, 0, seq, tm_seq, token)
    dtail, dg1 = _rms_bwd_rows("rms_mix_bwd_tail", dhn1, h0, g1, dh1, seq, tail, tail, dx, dg_prev=dg1)
    grad_x = dx[None]
    dmeta = dtail[tail - N_META:]

    given = dict(meta_tokens=(meta_tokens, m_meta_tokens, v_meta_tokens), norm_mix_g=(norm_mix_g, m_norm_mix_g, v_norm_mix_g),
                 w_in=(w_in, m_w_in, v_w_in), b_gate=(b_gate, m_b_gate, v_b_gate), pool_w=(pool_w, m_pool_w, v_pool_w),
                 pool_scale=(pool_scale, m_pool_scale, v_pool_scale), conv_w=(conv_w, m_conv_w, v_conv_w),
                 conv_out_w=(conv_out_w, m_conv_out_w, v_conv_out_w), w_o=(w_o, m_w_o, v_w_o),
                 norm_ffn_g=(norm_ffn_g, m_norm_ffn_g, v_norm_ffn_g), w_gate_up=(w_gate_up, m_w_gate_up, v_w_gate_up),
                 w_down=(w_down, m_w_down, v_w_down), norm_final_g=(norm_final_g, m_norm_final_g, v_norm_final_g))
    order = list(given.keys())
    grad, delta, new_m, new_v = {}, {}, {}, {}
    vec = jnp.concatenate([dg1, dg2, dg3, db_gate, dps, loss_cols, dconv_w, dmeta], axis=0)
    loss_row = 5 * SMALL_ROWS
    results = {}

    def update(tag, send, after):
        halves, sib_halves = _swap_wait("send_wait_" + tag, *send, after, halves=False)
        deltas = []
        for (nme, part), g_own, g_sib in zip(groups_g[tag], halves, sib_halves):
            w, m, v = given[nme]
            shape2 = (2 * g_own.shape[0] * part[1], g_own.shape[1])
            results[nme] = _adamw_halves("adamw_%s_%d" % (nme, part[0]), w.reshape(shape2), g_own, g_sib, m.reshape(shape2),
                                         v.reshape(shape2), core, part=part, prev=results.get(nme))
            grad[nme], delta[nme], new_m[nme], new_v[nme] = [t.reshape(w.shape) for t in results[nme]]
            deltas.append(results[nme][1])
        return deltas

    done_a = update("a", send_a, [dx])
    send_b, token = reduced("b", flight_b, done_a)
    send_c0, token = reduced("c0", flight_c0, [token])
    done_b = update("b", send_b, [token])
    send_c, token = reduced("c", flight_c, done_b)
    me1 = jnp.reshape(4 * cx + 2 * cy + cc, (1,)).astype(jnp.int32)
    red_flight, token = _reduce_start(vec, [token])
    done_c0 = update("c0", send_c0, [token])
    done_c = update("c", send_c, done_c0)
    red, loss11 = _reduce_sum(*_reduce_wait(*red_flight, done_c), me1, loss_row, 0.5 / dm)
    loss = loss11[0, 0]

    small = {"norm_mix_g": (0, False), "norm_ffn_g": (SMALL_ROWS, False), "norm_final_g": (2 * SMALL_ROWS, False),
             "b_gate": (3 * SMALL_ROWS, False), "pool_scale": (4 * SMALL_ROWS, False), "conv_w": (6 * SMALL_ROWS, True),
             "meta_tokens": (7 * SMALL_ROWS, True)}

    def two_d(t, nme):
        return t if t.ndim == 2 else t.reshape(2 if nme == "b_gate" else 1, dm)

    res = _adamw_small("adamw_small", red, [small[nme][0] for nme in small], [small[nme][1] for nme in small],
                       [[two_d(t, nme) for t in given[nme]] for nme in small])
    for nme, res4 in zip(small, res):
        grad[nme], delta[nme], new_m[nme], new_v[nme] = [t.reshape(given[nme][0].shape) for t in res4]
    return (loss, grad_x, *[grad[nme] for nme in order], *[delta[nme] for nme in order],
            *[new_m[nme] for nme in order], *[new_v[nme] for nme in order])
```

```python
import math

import jax
import jax.numpy as jnp
from jax import lax
from jax.experimental import pallas as pl
from jax.experimental.pallas import tpu as pltpu

F32 = jnp.float32
BF16 = jnp.bfloat16
N_META = 16
POOL_WINDOWS = (2, 4, 8, 16)
EPS = 1e-6
ADAM_LR, ADAM_B1, ADAM_B2, ADAM_EPS, ADAM_WD, ADAM_STEP = 0.001, 0.9, 0.999, 1e-08, 0.01, 10
LANES = 128
V7X_VMEM_BYTES = 64 * 1024 * 1024
VMEM_LIMIT = V7X_VMEM_BYTES - 8 * 1024 * 1024
MESH = pl.DeviceIdType.MESH
ANY = pl.BlockSpec(memory_space=pl.ANY)
CHIP_FLIPS = ((1, 0), (0, 1), (1, 1))
SMALL_ROWS = 8
TAIL_ROWS = 32


def _pick(n, pref):
    best = None
    for t in range(LANES, min(n, pref) + 1, LANES):
        if n % t == 0:
            best = t
    assert best is not None, (n, pref)
    return best


def _params(n_axes=0):
    sem = ("arbitrary",) * n_axes if n_axes else None
    return pltpu.CompilerParams(dimension_semantics=sem, vmem_limit_bytes=VMEM_LIMIT)


_DIMS = {
    "nn": (((1,), (0,)), ((), ())),
    "nt": (((1,), (1,)), ((), ())),
    "tn": (((0,), (0,)), ((), ())),
}


def _matmul(name, mode, a, b, out_sds, grid, a_spec, b_spec, o_spec, nk, res=None, res_spec=None, acc_shape=None, deps=(),
            prev=None):
    out_dtype = out_sds.dtype
    in_place = nk > 1 and out_dtype == F32
    use_scratch = nk > 1 and not in_place
    rows = a_spec.block_shape[-2] if mode != "tn" else None
    chunk = _row_tile(rows, 1, 1, 1152) if rows is not None else None
    untouched = list(deps) + ([prev] if prev is not None else [])
    n_in = 2 + (res is not None) + len(untouched)

    def body(*refs):
        a_ref, b_ref = refs[:2]
        r_ref = refs[2] if res is not None else None
        o_ref, *scr = refs[n_in:]
        k = pl.program_id(len(grid) - 1) if nk > 1 else None

        def emit(sl):
            if sl is None:
                part = lax.dot_general(a_ref[...], b_ref[...], _DIMS[mode], preferred_element_type=F32)
                idx = (slice(None), slice(None))
            else:
                part = lax.dot_general(a_ref[sl, :], b_ref[...], _DIMS[mode], preferred_element_type=F32)
                idx = (sl, slice(None))
            if nk == 1:
                if r_ref is not None:
                    part = part + r_ref[idx]
                o_ref[idx] = part.astype(out_dtype)
                return
            acc = scr[0] if use_scratch else o_ref

            @pl.when(k == 0)
            def _():
                first = part
                if r_ref is not None and in_place:
                    first = first + r_ref[idx]
                acc[idx] = first

            @pl.when(k > 0)
            def _():
                acc[idx] += part

            if use_scratch:

                @pl.when(k == nk - 1)
                def _():
                    o_ref[idx] = acc[idx].astype(out_dtype)

        if mode == "tn" or chunk == rows:
            emit(None)
        else:
            for m0 in range(0, rows, chunk):
                emit(pl.ds(m0, chunk))

    ins = [a, b] + ([res] if res is not None else []) + untouched
    in_specs = [a_spec, b_spec] + ([res_spec] if res is not None else []) + [ANY] * len(untouched)
    scratch = [pltpu.VMEM(acc_shape, F32)] if use_scratch else []
    return pl.pallas_call(
        body, name=name, out_shape=out_sds, grid=grid, in_specs=in_specs, out_specs=o_spec,
        scratch_shapes=scratch, input_output_aliases={} if prev is None else {n_in - 1: 0},
        compiler_params=_params(len(grid)),
    )(*ins)


def _nn_sharded(name, a, w4, nseg, part=(0, 1), prev=None, deps=()):
    lp, kdim = a.shape
    s, _, nloc = w4.shape
    segw = s * nloc // nseg
    tn = _pick(math.gcd(nloc, segw), 1536)
    bw, bo = nloc // tn, segw // tn
    steps = s * bw // part[1]
    j0 = part[0] * steps
    return _matmul(
        name, "nn", a, w4, jax.ShapeDtypeStruct((nseg, lp, segw), BF16), (steps,),
        pl.BlockSpec((lp, kdim), lambda j: (0, 0)),
        pl.BlockSpec((None, kdim, tn), lambda j: ((j0 + j) // bw, 0, (j0 + j) % bw)),
        pl.BlockSpec((None, lp, tn), lambda j: ((j0 + j) // bo, 0, (j0 + j) % bo)), 1, deps=deps, prev=prev)


def _nt_in_proj(name, dseg, w4, row_tiles=2, to_pref=1024, deps=()):
    nseg, lp, segw = dseg.shape
    s, kdim, nloc = w4.shape
    assert nseg * segw == s * nloc and 2 * nloc == 3 * segw, (dseg.shape, w4.shape)
    half = segw // 2
    to = _pick(kdim, to_pref)
    tm = lp // row_tiles

    def body(full_ref, half_ref, w_ref, *rest):
        o_ref = rest[len(deps)]
        r = pl.program_id(2)

        def contribution(full_first):
            lo, hi = (pl.ds(0, segw), pl.ds(segw, half)) if full_first else (pl.ds(half, segw), pl.ds(0, half))
            return (lax.dot_general(full_ref[...], w_ref[:, lo], _DIMS["nt"], preferred_element_type=F32)
                    + lax.dot_general(half_ref[...], w_ref[:, hi], _DIMS["nt"], preferred_element_type=F32))

        @pl.when(r == 0)
        def _():
            o_ref[...] = contribution(True)

        for ri in range(1, s):

            @pl.when(r == ri)
            def _(ri=ri):
                o_ref[...] += contribution(ri % 2 == 0)

    return pl.pallas_call(
        body, name=name, out_shape=jax.ShapeDtypeStruct((lp, kdim), F32), grid=(row_tiles, kdim // to, s),
        in_specs=[pl.BlockSpec((None, tm, segw), lambda m, j, r: ((3 * r + 1) // 2, m, 0)),
                  pl.BlockSpec((None, tm, half), lambda m, j, r: (1 + 3 * (r // 2), m, r % 2)),
                  pl.BlockSpec((None, to, nloc), lambda m, j, r: (r, j, 0))] + [ANY] * len(deps),
        out_specs=pl.BlockSpec((tm, to), lambda m, j, r: (m, j)), compiler_params=_params(3),
    )(dseg, dseg, w4, *deps)


def _nn_plain(name, a, w, out_dtype, res=None, tn_pref=512, tk_pref=2048, deps=()):
    lp, kdim = a.shape
    n = w.shape[1]
    tn = _pick(n, tn_pref)
    tk = kdim if kdim <= tk_pref else _pick(kdim, tk_pref)
    nk = kdim // tk
    grid = (n // tn, nk) if nk > 1 else (n // tn,)
    if nk > 1:
        a_spec = pl.BlockSpec((lp, tk), lambda j, k: (0, k))
        w_spec = pl.BlockSpec((tk, tn), lambda j, k: (k, j))
        o_spec = pl.BlockSpec((lp, tn), lambda j, k: (0, j))
    else:
        a_spec = pl.BlockSpec((lp, tk), lambda j: (0, 0))
        w_spec = pl.BlockSpec((tk, tn), lambda j: (0, j))
        o_spec = pl.BlockSpec((lp, tn), lambda j: (0, j))
    return _matmul(name, "nn", a, w, jax.ShapeDtypeStruct((lp, n), out_dtype), grid, a_spec, w_spec, o_spec, nk,
                   res=res, res_spec=o_spec if res is not None else None, acc_shape=(lp, tn), deps=deps)


def _nt_plain(name, a, w, tn_pref=512):
    lp, kdim = a.shape
    n = w.shape[0]
    tn = _pick(n, tn_pref)
    return _matmul(
        name, "nt", a, w, jax.ShapeDtypeStruct((lp, n), BF16), (n // tn,),
        pl.BlockSpec((lp, kdim), lambda j: (0, 0)),
        pl.BlockSpec((tn, kdim), lambda j: (j, 0)),
        pl.BlockSpec((lp, tn), lambda j: (0, j)), 1)


def _nt_sharded(name, dseg, w4, to_pref=1024, tr_pref=1536, row_tiles=1, deps=()):
    nseg, lp, segw = dseg.shape
    s, kdim, nloc = w4.shape
    tr = _pick(math.gcd(nloc, segw), tr_pref)
    ba, bw = segw // tr, nloc // tr
    nr = s * bw
    to = _pick(kdim, to_pref)
    tm = lp // row_tiles
    return _matmul(
        name, "nt", dseg, w4, jax.ShapeDtypeStruct((lp, kdim), F32), (row_tiles, kdim // to, nr),
        pl.BlockSpec((None, tm, tr), lambda m, j, r: (r // ba, m, r % ba)),
        pl.BlockSpec((None, to, tr), lambda m, j, r: (r // bw, j, r % bw)),
        pl.BlockSpec((tm, to), lambda m, j, r: (m, j)), nr, deps=deps)


def _nn_rows(name, a, w, res, row_tiles=2, tn_pref=512):
    lp, kdim = a.shape
    n = w.shape[1]
    tn = _pick(n, tn_pref)
    tm = lp // row_tiles
    blk = pl.BlockSpec((tm, tn), lambda i, j: (i, j))
    return _matmul(name, "nn", a, w, jax.ShapeDtypeStruct((lp, n), F32), (row_tiles, n // tn),
                   pl.BlockSpec((tm, kdim), lambda i, j: (i, 0)), pl.BlockSpec((kdim, tn), lambda i, j: (0, j)), blk, 1,
                   res=res, res_spec=blk)


def _tn_plain(name, a, d, tk_pref=1024):
    lp, kdim = a.shape
    n = d.shape[1]
    tk = _pick(kdim, tk_pref)
    return _matmul(
        name, "tn", a, d, jax.ShapeDtypeStruct((kdim, n), BF16), (kdim // tk,),
        pl.BlockSpec((lp, tk), lambda i: (0, i)),
        pl.BlockSpec((lp, n), lambda i: (0, 0)),
        pl.BlockSpec((tk, n), lambda i: (i, 0)), 1)


def _tn_sharded(name, a, dseg, s, part=(0, 1), tk_pref=1024, deps=()):
    lp, kdim = a.shape
    nseg, _, segw = dseg.shape
    nloc = nseg * segw // s
    tn = _pick(math.gcd(nloc, segw), 1536)
    bd, bo = segw // tn, nloc // tn
    kpart = kdim // part[1]
    tk = _pick(kpart, tk_pref)
    i0 = part[0] * (kpart // tk)

    def body(a_ref, d_ref, *rest):
        o_ref, at_ref = rest[len(deps):]

        @pl.when(pl.program_id(1) == 0)
        def _():
            at_ref[...] = a_ref[...].T

        o_ref[...] = jnp.dot(at_ref[...], d_ref[...], preferred_element_type=F32).astype(BF16)

    return pl.pallas_call(
        body, name=name, out_shape=jax.ShapeDtypeStruct((s, kpart, nloc), BF16), grid=(kpart // tk, s * bo),
        in_specs=[pl.BlockSpec((lp, tk), lambda i, j: (0, i0 + i)),
                  pl.BlockSpec((None, lp, tn), lambda i, j: (j // bd, 0, j % bd))] + [ANY] * len(deps),
        out_specs=pl.BlockSpec((None, tk, tn), lambda i, j: (j // bo, i, j % bo)),
        scratch_shapes=[pltpu.VMEM((tk, lp), BF16)], compiler_params=_params(2),
    )(a, dseg, *deps)


def _silu_parts(gt):
    sg = jax.nn.sigmoid(gt)
    return gt * sg, sg * (1.0 + gt * (1.0 - sg))


def _gate_up_swiglu(name, a, w4, dep, tn_pref=256):
    lp, kdim = a.shape
    s, _, nloc = w4.shape
    f = s * nloc // 2
    tn = _pick(nloc, tn_pref)
    bw = nloc // tn
    chunk = _row_tile(lp, 1, 1, 576)

    def body(a_ref, wg_ref, wu_ref, _, fac_ref, act_ref):
        for m0 in range(0, lp, chunk):
            sl = pl.ds(m0, chunk)
            gt = jnp.dot(a_ref[sl, :], wg_ref[...], preferred_element_type=F32)
            up = jnp.dot(a_ref[sl, :], wu_ref[...], preferred_element_type=F32)
            silu, dsilu = _silu_parts(gt)
            fac_ref[0, sl, :] = (up * dsilu).astype(BF16)
            fac_ref[1, sl, :] = silu.astype(BF16)
            act_ref[sl, :] = (silu * up).astype(BF16)

    return pl.pallas_call(
        body, name=name, grid=(f // tn,),
        out_shape=(jax.ShapeDtypeStruct((2, lp, f), BF16), jax.ShapeDtypeStruct((lp, f), BF16)),
        in_specs=[pl.BlockSpec((lp, kdim), lambda j: (0, 0)),
                  pl.BlockSpec((None, kdim, tn), lambda j: (j // bw, 0, j % bw)),
                  pl.BlockSpec((None, kdim, tn), lambda j: (s // 2 + j // bw, 0, j % bw)), ANY],
        out_specs=(pl.BlockSpec((2, lp, tn), lambda j: (0, 0, j)), pl.BlockSpec((lp, tn), lambda j: (0, j))),
        compiler_params=_params(1),
    )(a, w4, w4, dep)


def _dact_swiglu_bwd(name, d, w, gu, tn_pref=512):
    lp, dm = d.shape
    f = w.shape[0]
    tn = _pick(f, tn_pref)
    chunk = _row_tile(lp, 1, 1, 576)

    def body(d_ref, w_ref, g_ref, u_ref, o_ref):
        for m0 in range(0, lp, chunk):
            sl = pl.ds(m0, chunk)
            dact = lax.dot_general(d_ref[sl, :], w_ref[...], _DIMS["nt"], preferred_element_type=F32)
            o_ref[0, sl, :] = (dact * g_ref[sl, :].astype(F32)).astype(BF16)
            o_ref[1, sl, :] = (dact * u_ref[sl, :].astype(F32)).astype(BF16)

    return pl.pallas_call(
        body, name=name, grid=(f // tn,), out_shape=jax.ShapeDtypeStruct((2, lp, f), BF16),
        in_specs=[pl.BlockSpec((lp, dm), lambda j: (0, 0)), pl.BlockSpec((tn, dm), lambda j: (j, 0)),
                  pl.BlockSpec((None, lp, tn), lambda j: (0, 0, j)), pl.BlockSpec((None, lp, tn), lambda j: (1, 0, j))],
        out_specs=pl.BlockSpec((2, lp, tn), lambda j: (0, 0, j)), compiler_params=_params(1),
    )(d, w, gu, gu)


def _pool_fwd(name, pooled, pw):
    lp, dm = pooled.shape
    g, gw, _ = pw.shape
    return _matmul(
        name, "nn", pooled, pw, jax.ShapeDtypeStruct((lp, dm), BF16), (g,),
        pl.BlockSpec((lp, gw), lambda gi: (0, gi)), pl.BlockSpec((None, gw, gw), lambda gi: (gi, 0, 0)),
        pl.BlockSpec((lp, gw), lambda gi: (0, gi)), 1)


def _pool_bwd_act(name, dya, pw, deps=()):
    lp, dm = dya.shape
    g, gw, _ = pw.shape
    return _matmul(
        name, "nt", dya, pw, jax.ShapeDtypeStruct((lp, dm), BF16), (g,),
        pl.BlockSpec((lp, gw), lambda gi: (0, gi)), pl.BlockSpec((None, gw, gw), lambda gi: (gi, 0, 0)),
        pl.BlockSpec((lp, gw), lambda gi: (0, gi)), 1, deps=deps)


def _pool_bwd_w(name, pooled, dya):
    lp, dm = pooled.shape
    g = len(POOL_WINDOWS)
    gw = dm // g
    return _matmul(
        name, "tn", pooled, dya, jax.ShapeDtypeStruct((g, gw, gw), BF16), (g,),
        pl.BlockSpec((lp, gw), lambda gi: (0, gi)), pl.BlockSpec((lp, gw), lambda gi: (0, gi)),
        pl.BlockSpec((None, gw, gw), lambda gi: (gi, 0, 0)), 1)


def _rms_fwd(name, h, g, tm, deps=()):
    lp, dm = h.shape

    def body(h_ref, g_ref, *rest):
        hv = h_ref[...]
        r = lax.rsqrt(jnp.mean(hv * hv, axis=-1, keepdims=True) + EPS)
        rest[-1][...] = (hv * r * g_ref[...]).astype(BF16)

    row = pl.BlockSpec((tm, dm), lambda i: (i, 0))
    return pl.pallas_call(
        body, name=name, out_shape=jax.ShapeDtypeStruct((lp, dm), BF16), grid=(lp // tm,),
        in_specs=[row, pl.BlockSpec((1, dm), lambda i: (0, 0))] + [ANY] * len(deps), out_specs=row, compiler_params=_params(1),
    )(h, g, *deps)


def _rms_fwd_into(name, src, g, lp, row0, tm, prev=None, deps=()):
    n, dm = src.shape
    b0 = row0 // tm
    n_in = 2 + len(deps)

    def body(s_ref, g_ref, *rest):
        h_ref, o_ref = rest[-2:]
        hv = s_ref[...]
        r = lax.rsqrt(jnp.mean(hv * hv, axis=-1, keepdims=True) + EPS)
        h_ref[...] = hv
        o_ref[...] = (hv * r * g_ref[...]).astype(BF16)

    row = pl.BlockSpec((tm, dm), lambda i: (b0 + i, 0))
    return pl.pallas_call(
        body, name=name, grid=(n // tm,),
        out_shape=(jax.ShapeDtypeStruct((lp, dm), F32), jax.ShapeDtypeStruct((lp, dm), BF16)),
        in_specs=[pl.BlockSpec((tm, dm), lambda i: (i, 0)), pl.BlockSpec((1, dm), lambda i: (0, 0))]
        + [ANY] * (len(deps) + (0 if prev is None else 2)),
        out_specs=(row, row), input_output_aliases={} if prev is None else {n_in: 0, n_in + 1: 1},
        compiler_params=_params(1),
    )(src, g, *deps, *(prev or ()))


def _rms_bwd(name, dy, h, g, dres, tm, dep):
    lp, dm = h.shape

    def body(dy_ref, h_ref, g_ref, dr_ref, _, dh_ref, dhb_ref, dg_ref):
        hv = h_ref[...]
        r = lax.rsqrt(jnp.mean(hv * hv, axis=-1, keepdims=True) + EPS)
        xhat = hv * r
        dyv = dy_ref[...]
        dxh = dyv * g_ref[...]
        dh = dr_ref[...] + r * (dxh - xhat * jnp.mean(dxh * xhat, axis=-1, keepdims=True))
        dh_ref[...] = dh
        dhb_ref[...] = dh.astype(BF16)

        @pl.when(pl.program_id(0) == 0)
        def _():
            dg_ref[...] = jnp.zeros_like(dg_ref)

        dg_ref[0:1, :] += jnp.sum(dyv * xhat, axis=0, keepdims=True)

    row = pl.BlockSpec((tm, dm), lambda i: (i, 0))
    slab = pl.BlockSpec((SMALL_ROWS, dm), lambda i: (0, 0))
    return pl.pallas_call(
        body, name=name, grid=(lp // tm,),
        out_shape=(jax.ShapeDtypeStruct((lp, dm), F32), jax.ShapeDtypeStruct((lp, dm), BF16),
                   jax.ShapeDtypeStruct((SMALL_ROWS, dm), F32)),
        in_specs=[row, row, pl.BlockSpec((1, dm), lambda i: (0, 0)), row, ANY], out_specs=(row, row, slab),
        compiler_params=_params(1),
    )(dy, h, g, dres, dep)


def _rms_bwd_rows(name, dy, h, g, dres, row0, nrows, tm, dep, dg_prev=None):
    dm = h.shape[1]
    b0 = row0 // tm

    def body(dy_ref, h_ref, g_ref, dr_ref, *rest):
        d_ref, dg_ref = rest[-2:]
        hv = h_ref[...]
        r = lax.rsqrt(jnp.mean(hv * hv, axis=-1, keepdims=True) + EPS)
        xhat = hv * r
        dyv = dy_ref[...]
        dxh = dyv * g_ref[...]
        d_ref[...] = dr_ref[...] + r * (dxh - xhat * jnp.mean(dxh * xhat, axis=-1, keepdims=True))

        @pl.when(pl.program_id(0) == 0)
        def _():
            dg_ref[...] = jnp.zeros_like(dg_ref) if dg_prev is None else rest[1][...]

        dg_ref[0:1, :] += jnp.sum(dyv * xhat, axis=0, keepdims=True)

    row = pl.BlockSpec((tm, dm), lambda i: (b0 + i, 0))
    slab = pl.BlockSpec((SMALL_ROWS, dm), lambda i: (0, 0))
    extra = [dep] + ([dg_prev] if dg_prev is not None else [])
    return pl.pallas_call(
        body, name=name, grid=(nrows // tm,),
        out_shape=(jax.ShapeDtypeStruct((nrows, dm), F32), jax.ShapeDtypeStruct((SMALL_ROWS, dm), F32)),
        in_specs=[row, row, pl.BlockSpec((1, dm), lambda i: (0, 0)), row, ANY] + ([slab] if dg_prev is not None else []),
        out_specs=(pl.BlockSpec((tm, dm), lambda i: (i, 0)), slab), compiler_params=_params(1),
    )(dy, h, g, dres, *extra)


def _gate_mix(name, proj, b_gate2, ya, pool_scale, yb, tm):
    _, lp, dm = proj.shape

    def body(ga_ref, gr_ref, b_ref, ya_ref, ps_ref, yb_ref, o_ref):
        g_a = jax.nn.sigmoid(ga_ref[...].astype(F32) + b_ref[0:1, :])
        g_b = jax.nn.sigmoid(gr_ref[...].astype(F32) + b_ref[1:2, :])
        y_a = ya_ref[...].astype(F32) * ps_ref[...]
        o_ref[...] = (g_a * y_a + g_b * yb_ref[...].astype(F32)).astype(BF16)

    row = pl.BlockSpec((tm, dm), lambda i: (i, 0))
    return pl.pallas_call(
        body, name=name, out_shape=jax.ShapeDtypeStruct((lp, dm), BF16), grid=(lp // tm,),
        in_specs=[pl.BlockSpec((None, tm, dm), lambda i: (4, i, 0)), pl.BlockSpec((None, tm, dm), lambda i: (5, i, 0)),
                  pl.BlockSpec((2, dm), lambda i: (0, 0)), row, pl.BlockSpec((1, dm), lambda i: (0, 0)), row],
        out_specs=row, compiler_params=_params(1),
    )(proj, proj, b_gate2, ya, pool_scale, yb)


def _gate_bwd(name, dmix, proj, b_gate2, ya, pool_scale, yb, tm):
    _, lp, dm = proj.shape

    def body(dm_ref, ga_ref, gr_ref, b_ref, ya_ref, ps_ref, yb_ref, dp_ref, dyb_ref, dya_ref, db_ref, dps_ref):
        dmx = dm_ref[...].astype(F32)
        g_a = jax.nn.sigmoid(ga_ref[...].astype(F32) + b_ref[0:1, :])
        g_b = jax.nn.sigmoid(gr_ref[...].astype(F32) + b_ref[1:2, :])
        ya_pre = ya_ref[...].astype(F32)
        ybv = yb_ref[...].astype(F32)
        ps = ps_ref[...]
        dga = dmx * (ya_pre * ps) * (g_a * (1.0 - g_a))
        dgr = dmx * ybv * (g_b * (1.0 - g_b))
        dp_ref[0] = dga.astype(BF16)
        dp_ref[1] = dgr.astype(BF16)
        dyb_ref[...] = (dmx * g_b).astype(BF16)
        dya_ref[...] = (dmx * g_a * ps).astype(BF16)

        @pl.when(pl.program_id(0) == 0)
        def _():
            db_ref[...] = jnp.zeros_like(db_ref)
            dps_ref[...] = jnp.zeros_like(dps_ref)

        db_ref[0:1, :] += jnp.sum(dga, axis=0, keepdims=True)
        db_ref[1:2, :] += jnp.sum(dgr, axis=0, keepdims=True)
        dps_ref[0:1, :] += jnp.sum(dmx * g_a * ya_pre, axis=0, keepdims=True)

    row = pl.BlockSpec((tm, dm), lambda i: (i, 0))
    one = pl.BlockSpec((1, dm), lambda i: (0, 0))
    slab = pl.BlockSpec((SMALL_ROWS, dm), lambda i: (0, 0))
    return pl.pallas_call(
        body, name=name, grid=(lp // tm,),
        out_shape=(jax.ShapeDtypeStruct((6, lp, dm), BF16), jax.ShapeDtypeStruct((lp, dm), BF16),
                   jax.ShapeDtypeStruct((lp, dm), BF16), jax.ShapeDtypeStruct((SMALL_ROWS, dm), F32),
                   jax.ShapeDtypeStruct((SMALL_ROWS, dm), F32)),
        in_specs=[row, pl.BlockSpec((None, tm, dm), lambda i: (4, i, 0)), pl.BlockSpec((None, tm, dm), lambda i: (5, i, 0)),
                  pl.BlockSpec((2, dm), lambda i: (0, 0)), row, one, row],
        out_specs=(pl.BlockSpec((2, tm, dm), lambda i: (2, i, 0)), row, row, slab, slab),
        compiler_params=_params(1),
    )(dmix, proj, proj, b_gate2, ya, pool_scale, yb)


def _final_loss(name, h2, g3, target, tm):
    lp, dm = h2.shape
    seq = target.shape[0]

    def body(h_ref, g_ref, t_ref, dh_ref, dhb_ref, ls_ref, dg_ref):
        @pl.when(pl.program_id(0) == 0)
        def _():
            ls_ref[...] = jnp.zeros_like(ls_ref)
            dg_ref[...] = jnp.zeros_like(dg_ref)

        hv = h_ref[...]
        gv = g_ref[...]
        r = lax.rsqrt(jnp.mean(hv * hv, axis=-1, keepdims=True) + EPS)
        xhat = hv * r
        err = xhat * gv - t_ref[...]
        dout = err * (1.0 / dm)
        dxh = dout * gv
        dh = r * (dxh - xhat * jnp.mean(dxh * xhat, axis=-1, keepdims=True))
        dh_ref[...] = dh
        dhb_ref[...] = dh.astype(BF16)
        ls_ref[0:1, :] += jnp.sum(err * err, axis=0, keepdims=True)
        dg_ref[0:1, :] += jnp.sum(dout * xhat, axis=0, keepdims=True)

    row = pl.BlockSpec((tm, dm), lambda i: (i, 0))
    slab = pl.BlockSpec((SMALL_ROWS, dm), lambda i: (0, 0))
    return pl.pallas_call(
        body, name=name, grid=(seq // tm,),
        out_shape=(jax.ShapeDtypeStruct((lp, dm), F32), jax.ShapeDtypeStruct((lp, dm), BF16),
                   jax.ShapeDtypeStruct((SMALL_ROWS, dm), F32), jax.ShapeDtypeStruct((SMALL_ROWS, dm), F32)),
        in_specs=[row, pl.BlockSpec((1, dm), lambda i: (0, 0)), row],
        out_specs=(row, row, slab, slab), compiler_params=_params(1),
    )(h2, g3, target)


def _zero_tail(name, arrays, tail):
    n = len(arrays)
    lp, dm = arrays[0].shape
    last = lp // tail - 1

    def body(*refs):
        for o_ref in refs[n:]:
            o_ref[...] = jnp.zeros_like(o_ref)

    return pl.pallas_call(
        body, name=name, grid=(1,), out_shape=tuple(jax.ShapeDtypeStruct(a.shape, a.dtype) for a in arrays),
        in_specs=[ANY] * n, out_specs=tuple(pl.BlockSpec((tail, dm), lambda i: (last, 0)) for _ in arrays),
        input_output_aliases={a: a for a in range(n)}, compiler_params=_params(1),
    )(*arrays)


def _shift(v, k):
    return pltpu.roll(v, k % v.shape[0], axis=0)


def _window_sum(v, group, sign):
    s2 = v + _shift(v, sign * 1)
    s4 = s2 + _shift(s2, sign * 2)
    s8 = s4 + _shift(s4, sign * 4)
    s16 = s8 + _shift(s8, sign * 8)
    return jnp.where(group == 0, s2, jnp.where(group == 1, s4, jnp.where(group == 2, s8, s16)))


def _pool_count(lp, group):
    row = lax.broadcasted_iota(jnp.int32, (lp, 1), 0)
    window = jnp.left_shift(2, group).astype(F32)
    meta_pos = (row - (lp - N_META) + 1).astype(F32)
    return jnp.where(row >= lp - N_META, jnp.minimum(meta_pos, window), window)


def _mixer_fwd(name, proj, conv_w, tc, dep):
    _, lp, dm = proj.shape
    per_group = dm // len(POOL_WINDOWS) // tc

    def body(u_ref, gb_ref, gc_ref, v_ref, cw_ref, _, p_ref, z_ref):
        group = pl.program_id(0) // per_group
        u = u_ref[...].astype(F32)
        p_ref[...] = (_window_sum(u, group, 1) / _pool_count(lp, group) - u).astype(BF16)
        cv = gc_ref[...].astype(F32) * v_ref[...].astype(F32)
        conv = cw_ref[0:1, :] * _shift(cv, 2) + cw_ref[1:2, :] * _shift(cv, 1) + cw_ref[2:3, :] * cv
        z_ref[...] = (gb_ref[...].astype(F32) * conv).astype(BF16)

    def seg(s):
        return pl.BlockSpec((None, lp, tc), lambda j: (s, 0, j))

    col = pl.BlockSpec((lp, tc), lambda j: (0, j))
    return pl.pallas_call(
        body, name=name, grid=(dm // tc,),
        out_shape=(jax.ShapeDtypeStruct((lp, dm), BF16), jax.ShapeDtypeStruct((lp, dm), BF16)),
        in_specs=[seg(0), seg(1), seg(2), seg(3), pl.BlockSpec((3, tc), lambda j: (0, j)), ANY],
        out_specs=(col, col), compiler_params=_params(1),
    )(proj, proj, proj, proj, conv_w, dep)


def _mixer_bwd(name, dz, dpooled, proj, conv_w, dproj, tc, dep):
    _, lp, dm = proj.shape
    per_group = dm // len(POOL_WINDOWS) // tc

    def body(dz_ref, dp_ref, gb_ref, gc_ref, v_ref, cw_ref, _, __, o_ref, dcw_ref):
        group = pl.program_id(0) // per_group
        dzv = dz_ref[...].astype(F32)
        gb = gb_ref[...].astype(F32)
        gc = gc_ref[...].astype(F32)
        vv = v_ref[...].astype(F32)
        cv = gc * vv
        c1 = _shift(cv, 1)
        c2 = _shift(cv, 2)
        w0, w1, w2 = cw_ref[0:1, :], cw_ref[1:2, :], cw_ref[2:3, :]
        o_ref[1] = (dzv * (w0 * c2 + w1 * c1 + w2 * cv)).astype(BF16)
        dconv = dzv * gb
        dcw_ref[...] = jnp.zeros_like(dcw_ref)
        dcw_ref[0:1, :] = jnp.sum(dconv * c2, axis=0, keepdims=True)
        dcw_ref[1:2, :] = jnp.sum(dconv * c1, axis=0, keepdims=True)
        dcw_ref[2:3, :] = jnp.sum(dconv * cv, axis=0, keepdims=True)
        dcv = w0 * _shift(dconv, -2) + w1 * _shift(dconv, -1) + w2 * dconv
        o_ref[2] = (dcv * vv).astype(BF16)
        o_ref[3] = (dcv * gc).astype(BF16)
        dpv = dp_ref[...].astype(F32)
        o_ref[0] = (_window_sum(dpv / _pool_count(lp, group), group, -1) - dpv).astype(BF16)

    def seg(s):
        return pl.BlockSpec((None, lp, tc), lambda j: (s, 0, j))

    col = pl.BlockSpec((lp, tc), lambda j: (0, j))
    return pl.pallas_call(
        body, name=name, grid=(dm // tc,),
        out_shape=(jax.ShapeDtypeStruct(dproj.shape, BF16), jax.ShapeDtypeStruct((SMALL_ROWS, dm), F32)),
        in_specs=[col, col, seg(1), seg(2), seg(3), pl.BlockSpec((3, tc), lambda j: (0, j)), ANY, ANY],
        out_specs=(pl.BlockSpec((4, lp, tc), lambda j: (0, 0, j)), pl.BlockSpec((SMALL_ROWS, tc), lambda j: (0, j))),
        input_output_aliases={6: 0}, compiler_params=_params(1),
    )(dz, dpooled, proj, proj, proj, conv_w, dproj, dep)


def _row_tile(r, c, bytes_per_row_elem=4, budget=2 * 1024 * 1024):
    best = None
    for t in range(16, r + 1, 16):
        if r % t == 0 and t * c * bytes_per_row_elem <= budget:
            best = t
    return best if best is not None else r


def _pair_add(name, g4, recv, core):
    s, r, c = g4.shape
    h = r // 2
    tr = _row_tile(h, c, budget=6 * 1024 * 1024)
    nb = h // tr

    def body(core_ref, g_ref, r_ref, o_ref):
        o_ref[...] = (g_ref[...].astype(F32) + r_ref[...].astype(F32)).astype(BF16)

    grid_spec = pltpu.PrefetchScalarGridSpec(
        num_scalar_prefetch=1, grid=(s, nb),
        in_specs=[pl.BlockSpec((None, tr, c), lambda si, j, core_ref: (si, core_ref[0] * nb + j, 0)),
                  pl.BlockSpec((None, tr, c), lambda si, j, core_ref: (si, j, 0))],
        out_specs=pl.BlockSpec((None, tr, c), lambda si, j, core_ref: (si, j, 0)))
    return pl.pallas_call(
        body, name=name, out_shape=jax.ShapeDtypeStruct((s, h, c), BF16), grid_spec=grid_spec,
        compiler_params=_params(2),
    )(core, g4, recv)


def _chip_sum(name, parts, recv, chip):
    _, h, c = parts.shape
    tr = _row_tile(h, c)

    def body(chip_ref, p_ref, r_ref, o_ref):
        acc = p_ref[...].astype(F32)
        for i in range(len(CHIP_FLIPS)):
            acc = acc + r_ref[i].astype(F32)
        o_ref[...] = acc

    grid_spec = pltpu.PrefetchScalarGridSpec(
        num_scalar_prefetch=1, grid=(h // tr,),
        in_specs=[pl.BlockSpec((None, tr, c), lambda j, chip_ref: (chip_ref[0], j, 0)),
                  pl.BlockSpec((len(CHIP_FLIPS), tr, c), lambda j, chip_ref: (0, j, 0))],
        out_specs=pl.BlockSpec((tr, c), lambda j, chip_ref: (j, 0)))
    return pl.pallas_call(
        body, name=name, out_shape=jax.ShapeDtypeStruct((h, c), F32), grid_spec=grid_spec, compiler_params=_params(1),
    )(chip, parts, recv)


def _adam_update(w, gv, m, v):
    c1 = 1.0 - ADAM_B1 ** ADAM_STEP
    c2 = 1.0 - ADAM_B2 ** ADAM_STEP
    nm = ADAM_B1 * m + (1.0 - ADAM_B1) * gv
    nv = ADAM_B2 * v + (1.0 - ADAM_B2) * (gv * gv)
    return -ADAM_LR * ((nm / c1) / (jnp.sqrt(nv / c2) + ADAM_EPS) + ADAM_WD * w), nm, nv


def _adamw_halves(name, w, g_own, g_sib, m, v, core, part=(0, 1), prev=None):
    r, c = w.shape
    rp = r // part[1]
    h = rp // 2
    tr = _row_tile(h, c, budget=2 * 1024 * 1024)
    nbh = h // tr
    j0 = part[0] * 2 * nbh
    n_prev = 0 if prev is None else 4

    def step(w_ref, gh_ref, m_ref, v_ref, g_ref, d_ref, nm_ref, nv_ref):
        gv = gh_ref[...]
        g_ref[...] = gv
        d_ref[...], nm_ref[...], nv_ref[...] = _adam_update(w_ref[...], gv, m_ref[...], v_ref[...])

    def body(core_ref, w_ref, go_ref, gs_ref, m_ref, v_ref, *rest):
        outs = rest[n_prev:]
        for gh_ref, first in ((go_ref, j0 + core_ref[0] * nbh), (gs_ref, j0 + (1 - core_ref[0]) * nbh)):
            def rows(j, first=first):
                return (first + j, 0)

            def deep(fn):
                return pl.BlockSpec((tr, c), fn, pipeline_mode=pl.Buffered(3))

            pltpu.emit_pipeline(
                step, grid=(nbh,), in_specs=[deep(rows), deep(lambda j: (j, 0)), deep(rows), deep(rows)],
                out_specs=[pl.BlockSpec((tr, c), rows)] * 4,
            )(w_ref, gh_ref, m_ref, v_ref, *outs)

    sds = jax.ShapeDtypeStruct((r, c), F32)
    return pl.pallas_call(
        body, name=name, out_shape=(sds,) * 4,
        in_specs=[pl.BlockSpec(memory_space=pltpu.SMEM)] + [ANY] * (5 + n_prev), out_specs=(ANY,) * 4,
        compiler_params=_params(), input_output_aliases={6 + i: i for i in range(n_prev)},
    )(core, w, g_own, g_sib, m, v, *(prev or ()))


def _adamw_small(name, red, rows, cols, params):
    n = len(params)
    whole = pl.BlockSpec(memory_space=pltpu.VMEM)

    def body(red_ref, *refs):
        ins, outs = refs[:3 * n], refs[3 * n:]
        chip = 2 * lax.axis_index("x") + lax.axis_index("y")
        for i in range(n):
            w_ref, m_ref, v_ref = ins[3 * i:3 * i + 3]
            r, c = w_ref.shape
            g = red_ref[pl.ds(rows[i], r), pl.ds(pl.multiple_of(chip * c, LANES), c)] if cols[i] else red_ref[pl.ds(rows[i], r), :]
            outs[4 * i][...] = g
            outs[4 * i + 1][...], outs[4 * i + 2][...], outs[4 * i + 3][...] = _adam_update(w_ref[...], g, m_ref[...], v_ref[...])

    flat = [t for p in params for t in p]
    res = pl.pallas_call(
        body, name=name, out_shape=tuple(jax.ShapeDtypeStruct(p[0].shape, F32) for p in params for _ in range(4)),
        in_specs=[whole] * (1 + 3 * n), out_specs=(whole,) * (4 * n), compiler_params=_params(),
    )(red, *flat)
    return [res[4 * i:4 * i + 4] for i in range(n)]


def _cast_into_slot(name, w, chip, dtype, deps=()):
    r, c = w.shape
    tr = _row_tile(r, c)

    def body(chip_ref, w_ref, *rest):
        rest[-1][...] = w_ref[...].astype(dtype)

    grid_spec = pltpu.PrefetchScalarGridSpec(
        num_scalar_prefetch=1, grid=(r // tr,),
        in_specs=[pl.BlockSpec((tr, c), lambda j, chip_ref: (j, 0))] + [ANY] * len(deps),
        out_specs=pl.BlockSpec((None, tr, c), lambda j, chip_ref: (chip_ref[0], j, 0)))
    return pl.pallas_call(
        body, name=name, out_shape=jax.ShapeDtypeStruct((4, r, c), dtype), grid_spec=grid_spec, compiler_params=_params(1),
    )(chip, w, *deps)


def _cast_half_into_slot(name, w, chip_half, dtype, into=None):
    r, c = w.shape
    h = r // 2
    tr = _row_tile(h, c)
    nb = h // tr
    n_prev = 0 if into is None else 1

    def body(ids_ref, w_ref, *rest):
        rest[-1][...] = w_ref[...].astype(dtype)

    grid_spec = pltpu.PrefetchScalarGridSpec(
        num_scalar_prefetch=1, grid=(nb,),
        in_specs=[pl.BlockSpec((tr, c), lambda j, ids_ref: (ids_ref[1] * nb + j, 0))] + [ANY] * n_prev,
        out_specs=pl.BlockSpec((None, tr, c), lambda j, ids_ref: (ids_ref[0], ids_ref[1] * nb + j, 0)))
    return pl.pallas_call(
        body, name=name, out_shape=jax.ShapeDtypeStruct((4, r, c), dtype), grid_spec=grid_spec, compiler_params=_params(1),
        input_output_aliases={2: 0} if into is not None else {},
    )(chip_half, w, *([into] if into is not None else []))


def _place():
    return lax.axis_index("x"), lax.axis_index("y"), lax.axis_index("c")


def _chip_of(x, y, flip):
    px, py = x ^ flip[0], y ^ flip[1]
    return px, py, 2 * px + py


def _half(ref, which):
    rows = ref.shape[0] // 2
    return ref.at[pl.ds(which * rows, rows)]


HBM = pl.BlockSpec(memory_space=pltpu.HBM)
SEM = pl.BlockSpec(memory_space=pltpu.SEMAPHORE)
SPLIT_COPY = pltpu.CompilerParams(has_side_effects=pltpu.SideEffectType.DATAFLOW_SIDE_EFFECTING)


def _in_hbm(arrays):
    return [pltpu.with_memory_space_constraint(t, pltpu.HBM) for t in arrays]


TOKEN = jax.ShapeDtypeStruct((SMALL_ROWS, LANES), F32)
TOKEN_SPEC = pl.BlockSpec(memory_space=pltpu.VMEM)


NEIGHBOUR_FLIPS = CHIP_FLIPS[:2]


def _relay_chips(x, y, c):
    fx, fy = x ^ c, y ^ (1 - c)
    return (fx, fy), 2 * fx + fy, 2 * (1 - x) + (1 - y)


def _ag_start(name, slabs, deps=()):
    n = len(slabs)
    nn = len(NEIGHBOUR_FLIPS)

    def body(*refs):
        no = n + len(deps)
        ssem, rsem = refs[no], refs[no + 1]
        outs = refs[no + 2:no + 2 + n]
        token = refs[no + 2 + n]
        token[...] = jnp.zeros_like(token)
        x, y, c = _place()
        k = 2 * x + y
        for a in range(n):
            for j, flip in enumerate(NEIGHBOUR_FLIPS):
                px, py, _ = _chip_of(x, y, flip)
                mine = _half(outs[a].at[k], c)
                pltpu.make_async_remote_copy(src_ref=mine, dst_ref=mine, send_sem=ssem.at[a * nn + j],
                                             recv_sem=rsem.at[a * nn + j], device_id=(px, py, c), device_id_type=MESH).start()

    sem = pltpu.SemaphoreType.DMA((nn * n,))
    res = pl.pallas_call(
        body, name=name, out_shape=(sem, sem) + tuple(pltpu.HBM(t.shape, t.dtype) for t in slabs) + (TOKEN,),
        in_specs=[HBM] * n + [ANY] * len(deps), out_specs=tuple([SEM, SEM] + [HBM] * n + [TOKEN_SPEC]),
        input_output_aliases={a: 2 + a for a in range(n)}, compiler_params=SPLIT_COPY,
    )(*_in_hbm(slabs), *deps)
    return (res[0], res[1]), list(res[2:2 + n]), res[2 + n]


def _ag_relay(name, slabs, sems, after, then_start=()):
    n = len(slabs)
    m = len(then_start)
    nn = len(NEIGHBOUR_FLIPS)

    def body(*refs):
        no = n + 2 + m + len(after)
        ins = refs[:n]
        ssem, rsem = refs[n], refs[n + 1]
        r_s, r_r, p_s, p_r = refs[no:no + 4]
        x, y, c = _place()
        k = 2 * x + y
        (fx, fy), _, _ = _relay_chips(x, y, c)
        for a in range(n):
            for j, flip in enumerate(NEIGHBOUR_FLIPS):
                _, _, kj = _chip_of(x, y, flip)
                landed = _half(ins[a].at[kj], c)
                cp = pltpu.make_async_remote_copy(
                    src_ref=_half(ins[a].at[k], c), dst_ref=landed, send_sem=ssem.at[a * nn + j],
                    recv_sem=rsem.at[a * nn + j], device_id=(x, y, c), device_id_type=MESH)
                cp.wait_send()
                cp.wait_recv()
        for a in range(n):
            near = _half(ins[a].at[2 * (x ^ (1 - c)) + (y ^ c)], c)
            pltpu.make_async_remote_copy(src_ref=near, dst_ref=near, send_sem=r_s.at[a], recv_sem=r_r.at[a],
                                         device_id=(fx, fy, c), device_id_type=MESH).start()
            for j, flip in enumerate(NEIGHBOUR_FLIPS):
                _, _, kj = _chip_of(x, y, flip)
                landed = _half(ins[a].at[kj], c)
                pltpu.make_async_remote_copy(src_ref=landed, dst_ref=landed, send_sem=p_s.at[a * nn + j],
                                             recv_sem=p_r.at[a * nn + j], device_id=(x, y, 1 - c), device_id_type=MESH).start()
        if m:
            d_s, d_r = refs[no + 4 + n], refs[no + 5 + n]
            nxt = refs[no + 6 + n:]
            for a in range(m):
                for j, flip in enumerate(NEIGHBOUR_FLIPS):
                    px, py, _ = _chip_of(x, y, flip)
                    mine = _half(nxt[a].at[k], c)
                    pltpu.make_async_remote_copy(src_ref=mine, dst_ref=mine, send_sem=d_s.at[a * nn + j],
                                                 recv_sem=d_r.at[a * nn + j], device_id=(px, py, c), device_id_type=MESH).start()

    rsem_t = pltpu.SemaphoreType.DMA((n,))
    psem_t = pltpu.SemaphoreType.DMA((nn * n,))
    out_shape = (rsem_t, rsem_t, psem_t, psem_t) + tuple(pltpu.HBM(t.shape, t.dtype) for t in slabs)
    out_specs = [SEM] * 4 + [HBM] * n
    aliases = {a: 4 + a for a in range(n)}
    if m:
        dsem_t = pltpu.SemaphoreType.DMA((nn * m,))
        out_shape += (dsem_t, dsem_t) + tuple(pltpu.HBM(t.shape, t.dtype) for t in then_start)
        out_specs += [SEM, SEM] + [HBM] * m
        aliases.update({n + 2 + a: 4 + n + 2 + a for a in range(m)})
    res = pl.pallas_call(
        body, name=name, out_shape=out_shape, in_specs=[HBM] * n + [SEM, SEM] + [HBM] * m + [ANY] * len(after),
        out_specs=tuple(out_specs), input_output_aliases=aliases, compiler_params=SPLIT_COPY,
    )(*slabs, sems[0], sems[1], *_in_hbm(list(then_start)), *after)
    if not m:
        return tuple(res[:4]), list(res[4:])
    return (tuple(res[:4]), list(res[4:4 + n])), ((res[4 + n], res[5 + n]), list(res[6 + n:]))


def _wait_passes(ins, p_s, p_r, x, y, c):
    nn = len(NEIGHBOUR_FLIPS)
    for a in range(len(ins)):
        for j, flip in enumerate(NEIGHBOUR_FLIPS):
            _, _, kj = _chip_of(x, y, flip)
            cp = pltpu.make_async_remote_copy(
                src_ref=_half(ins[a].at[kj], c), dst_ref=_half(ins[a].at[kj], 1 - c), send_sem=p_s.at[a * nn + j],
                recv_sem=p_r.at[a * nn + j], device_id=(x, y, c), device_id_type=MESH)
            cp.wait_send()
            cp.wait_recv()


def _ag_relay_wait(name, slabs, sems, after):
    n = len(slabs)
    ns = len(sems)

    def body(*refs):
        no = n + ns + len(after)
        ins = refs[:n]
        r_s, r_r = refs[n], refs[n + 1]
        f_s, f_r = refs[no], refs[no + 1]
        x, y, c = _place()
        _, _, kd = _relay_chips(x, y, c)
        for a in range(n):
            near = _half(ins[a].at[2 * (x ^ (1 - c)) + (y ^ c)], c)
            cp = pltpu.make_async_remote_copy(src_ref=near, dst_ref=_half(ins[a].at[kd], c), send_sem=r_s.at[a],
                                              recv_sem=r_r.at[a], device_id=(x, y, c), device_id_type=MESH)
            cp.wait_send()
            cp.wait_recv()
        if ns == 4:
            _wait_passes(ins, refs[n + 2], refs[n + 3], x, y, c)
        for a in range(n):
            diag = _half(ins[a].at[kd], c)
            pltpu.make_async_remote_copy(src_ref=diag, dst_ref=diag, send_sem=f_s.at[a], recv_sem=f_r.at[a],
                                         device_id=(x, y, 1 - c), device_id_type=MESH).start()

    sem = pltpu.SemaphoreType.DMA((n,))
    res = pl.pallas_call(
        body, name=name, out_shape=(sem, sem) + tuple(pltpu.HBM(t.shape, t.dtype) for t in slabs),
        in_specs=[HBM] * n + [SEM] * ns + [ANY] * len(after), out_specs=tuple([SEM, SEM] + [HBM] * n),
        input_output_aliases={a: 2 + a for a in range(n)}, compiler_params=SPLIT_COPY,
    )(*slabs, *sems, *after)
    return (res[0], res[1]), list(res[2:])


def _ag_final_wait(name, slabs, sems, after, first=0):
    n = len(slabs)

    def body(*refs):
        ins = refs[:n]
        f_s, f_r = refs[n], refs[n + 1]
        x, y, c = _place()
        _, _, kd = _relay_chips(x, y, c)
        for a in range(n):
            cp = pltpu.make_async_remote_copy(
                src_ref=_half(ins[a].at[kd], c), dst_ref=_half(ins[a].at[kd], 1 - c), send_sem=f_s.at[first + a],
                recv_sem=f_r.at[first + a], device_id=(x, y, c), device_id_type=MESH)
            cp.wait_send()
            cp.wait_recv()

    return pl.pallas_call(
        body, name=name, out_shape=tuple(pltpu.HBM(t.shape, t.dtype) for t in slabs),
        in_specs=[HBM] * n + [SEM, SEM] + [ANY] * len(after), out_specs=tuple([HBM] * n),
        input_output_aliases={a: a for a in range(n)}, compiler_params=SPLIT_COPY,
    )(*slabs, sems[0], sems[1], *after)


def _sibling_part(ref, c, halves):
    if not halves:
        return ref
    h = ref.shape[1] // 2
    return ref.at[:, pl.ds((1 - c) * h, h)]


def _swap_start(name, grads, halves=True, deps=()):
    n = len(grads)

    def body(*refs):
        no = 2 * n + len(deps)
        ssem, rsem = refs[no], refs[no + 1]
        src, land = refs[no + 2:no + n + 2], refs[no + n + 2:no + 2 * n + 2]
        token = refs[no + 2 * n + 2]
        token[...] = jnp.zeros_like(token)
        x, y, c = _place()
        for a in range(n):
            pltpu.make_async_remote_copy(
                src_ref=_sibling_part(src[a], c, halves), dst_ref=land[a], send_sem=ssem.at[a], recv_sem=rsem.at[a],
                device_id=(x, y, 1 - c), device_id_type=MESH).start()

    zones = [lax.empty((g.shape[0], g.shape[1] // 2, g.shape[2]) if halves else g.shape, g.dtype) for g in grads]
    sem = pltpu.SemaphoreType.DMA((n,))
    res = pl.pallas_call(
        body, name=name,
        out_shape=(sem, sem) + tuple(pltpu.HBM(t.shape, t.dtype) for t in list(grads) + zones) + (TOKEN,),
        in_specs=[HBM] * (2 * n) + [ANY] * len(deps), out_specs=tuple([SEM, SEM] + [HBM] * (2 * n) + [TOKEN_SPEC]),
        input_output_aliases={i: 2 + i for i in range(2 * n)}, compiler_params=SPLIT_COPY,
    )(*_in_hbm(list(grads) + zones), *deps)
    return (res[0], res[1], list(res[2:2 + n]), list(res[2 + n:2 + 2 * n])), res[2 + 2 * n]


def _swap_wait(name, ssem, rsem, grads, zones, after, halves=True):
    n = len(grads)

    def body(*refs):
        src, land = refs[:n], refs[n:2 * n]
        ss, rs = refs[2 * n], refs[2 * n + 1]
        x, y, c = _place()
        for a in range(n):
            cp = pltpu.make_async_remote_copy(
                src_ref=_sibling_part(src[a], c, halves), dst_ref=land[a], send_sem=ss.at[a], recv_sem=rs.at[a],
                device_id=(x, y, c), device_id_type=MESH)
            cp.wait_send()
            cp.wait_recv()

    res = pl.pallas_call(
        body, name=name, out_shape=tuple(pltpu.HBM(t.shape, t.dtype) for t in list(grads) + list(zones)),
        in_specs=[HBM] * (2 * n) + [SEM, SEM] + [ANY] * len(after), out_specs=tuple([HBM] * (2 * n)),
        input_output_aliases={i: i for i in range(2 * n)}, compiler_params=SPLIT_COPY,
    )(*grads, *zones, ssem, rsem, *after)
    return list(res[:n]), list(res[n:])


def _scatter_start(name, parts):
    n = len(parts)
    nf = len(CHIP_FLIPS)

    def body(*refs):
        ssem, rsem = refs[2 * n], refs[2 * n + 1]
        src, land = refs[2 * n + 2:3 * n + 2], refs[3 * n + 2:4 * n + 2]
        token = refs[4 * n + 2]
        token[...] = jnp.zeros_like(token)
        x, y, c = _place()
        for a in range(n):
            for j, flip in enumerate(CHIP_FLIPS):
                px, py, kj = _chip_of(x, y, flip)
                pltpu.make_async_remote_copy(
                    src_ref=src[a].at[kj], dst_ref=land[a].at[j], send_sem=ssem.at[a * nf + j], recv_sem=rsem.at[a * nf + j],
                    device_id=(px, py, c), device_id_type=MESH).start()

    zones = [lax.empty((nf,) + p.shape[1:], p.dtype) for p in parts]
    sem = pltpu.SemaphoreType.DMA((nf * n,))
    res = pl.pallas_call(
        body, name=name,
        out_shape=(sem, sem) + tuple(pltpu.HBM(t.shape, t.dtype) for t in list(parts) + zones)
        + (jax.ShapeDtypeStruct((SMALL_ROWS, LANES), F32),),
        in_specs=[HBM] * (2 * n),
        out_specs=tuple([SEM, SEM] + [HBM] * (2 * n) + [pl.BlockSpec(memory_space=pltpu.VMEM)]),
        input_output_aliases={i: 2 + i for i in range(2 * n)}, compiler_params=SPLIT_COPY,
    )(*_in_hbm(list(parts) + zones))
    return (res[0], res[1], list(res[2:2 + n]), list(res[2 + n:2 + 2 * n])), res[2 + 2 * n]


def _scatter_wait(name, ssem, rsem, parts, zones, after):
    n = len(parts)
    nf = len(CHIP_FLIPS)

    def body(*refs):
        src, land = refs[:n], refs[n:2 * n]
        ss, rs = refs[2 * n], refs[2 * n + 1]
        x, y, c = _place()
        for a in range(n):
            for j, flip in enumerate(CHIP_FLIPS):
                _, _, kj = _chip_of(x, y, flip)
                cp = pltpu.make_async_remote_copy(
                    src_ref=src[a].at[kj], dst_ref=land[a].at[j], send_sem=ss.at[a * nf + j], recv_sem=rs.at[a * nf + j],
                    device_id=(x, y, c), device_id_type=MESH)
                cp.wait_send()
                cp.wait_recv()

    res = pl.pallas_call(
        body, name=name, out_shape=tuple(pltpu.HBM(t.shape, t.dtype) for t in list(parts) + list(zones)),
        in_specs=[HBM] * (2 * n) + [SEM, SEM] + [ANY] * len(after), out_specs=tuple([HBM] * (2 * n)),
        input_output_aliases={i: i for i in range(2 * n)}, compiler_params=SPLIT_COPY,
    )(*parts, *zones, ssem, rsem, *after)
    return list(res[:n]), list(res[n:])


N_PEERS = 7


def _peer(x, y, c, mask):
    px, py, pc = x ^ ((mask >> 2) & 1), y ^ ((mask >> 1) & 1), c ^ (mask & 1)
    return (px, py, pc), 4 * px + 2 * py + pc


def _reduce_start(vec, deps):
    nd = len(deps)

    def body(*refs):
        ssem, rsem, src, land, token = refs[2 + nd:]
        token[...] = jnp.zeros_like(token)
        x, y, c = _place()
        me = 4 * x + 2 * y + c
        for mask in range(1, N_PEERS + 1):
            to, _ = _peer(x, y, c, mask)
            pltpu.make_async_remote_copy(src_ref=src, dst_ref=land.at[me], send_sem=ssem.at[mask - 1],
                                         recv_sem=rsem.at[mask - 1], device_id=to, device_id_type=MESH).start()

    zone = lax.empty((N_PEERS + 1,) + vec.shape, vec.dtype)
    sem = pltpu.SemaphoreType.DMA((N_PEERS,))
    res = pl.pallas_call(
        body, name="reduce_start",
        out_shape=(sem, sem, pltpu.HBM(vec.shape, vec.dtype), pltpu.HBM(zone.shape, zone.dtype), TOKEN),
        in_specs=[HBM, HBM] + [ANY] * nd, out_specs=(SEM, SEM, HBM, HBM, TOKEN_SPEC),
        input_output_aliases={0: 2, 1: 3}, compiler_params=SPLIT_COPY,
    )(*_in_hbm([vec, zone]), *deps)
    return res[:4], res[4]


def _reduce_wait(ssem, rsem, vec, zone, after):
    def body(src, land, ss, rs, *_):
        x, y, c = _place()
        for mask in range(1, N_PEERS + 1):
            _, frm = _peer(x, y, c, mask)
            cp = pltpu.make_async_remote_copy(src_ref=src, dst_ref=land.at[frm], send_sem=ss.at[mask - 1],
                                              recv_sem=rs.at[mask - 1], device_id=(x, y, c), device_id_type=MESH)
            cp.wait_send()
            cp.wait_recv()

    return pl.pallas_call(
        body, name="reduce_wait", out_shape=(pltpu.HBM(vec.shape, vec.dtype), pltpu.HBM(zone.shape, zone.dtype)),
        in_specs=[HBM, HBM, SEM, SEM] + [ANY] * len(after), out_specs=(HBM, HBM),
        input_output_aliases={0: 0, 1: 1}, compiler_params=SPLIT_COPY,
    )(vec, zone, ssem, rsem, *after)


def _reduce_sum(vec, zone, me, loss_row, loss_scale):
    r, dm = vec.shape

    def body(me_ref, v_ref, z_ref, o_ref, l_ref):
        acc = None
        for i in range(N_PEERS + 1):
            term = jnp.where(me_ref[0] == i, v_ref[...], z_ref[i])
            acc = term if acc is None else acc + term
        o_ref[...] = acc
        l_ref[...] = jnp.sum(acc[loss_row:loss_row + SMALL_ROWS, :], axis=(0, 1), keepdims=True) * loss_scale

    grid_spec = pltpu.PrefetchScalarGridSpec(
        num_scalar_prefetch=1, grid=(1,),
        in_specs=[pl.BlockSpec((r, dm), lambda i, me_ref: (0, 0)), pl.BlockSpec((N_PEERS + 1, r, dm), lambda i, me_ref: (0, 0, 0))],
        out_specs=(pl.BlockSpec((r, dm), lambda i, me_ref: (0, 0)), pl.BlockSpec((1, 1), lambda i, me_ref: (0, 0))))
    return pl.pallas_call(
        body, name="reduce_sum", out_shape=(jax.ShapeDtypeStruct((r, dm), F32), jax.ShapeDtypeStruct((1, 1), F32)),
        grid_spec=grid_spec, compiler_params=_params(1),
    )(me, vec, zone)


def kernel(x, meta_tokens, norm_mix_g, w_in, b_gate, pool_w, pool_scale, conv_w, conv_out_w, w_o, norm_ffn_g, w_gate_up, w_down, norm_final_g, loss_target, m_meta_tokens, m_norm_mix_g, m_w_in, m_b_gate, m_pool_w, m_pool_scale, m_conv_w, m_conv_out_w, m_w_o, m_norm_ffn_g, m_w_gate_up, m_w_down, m_norm_final_g, v_meta_tokens, v_norm_mix_g, v_w_in, v_b_gate, v_pool_w, v_pool_scale, v_conv_w, v_conv_out_w, v_w_o, v_norm_ffn_g, v_w_gate_up, v_w_down, v_norm_final_g):
    seq, dm = x.shape[1], x.shape[2]
    tail = TAIL_ROWS
    lp = seq + tail
    tm_row = _row_tile(lp, dm, 4, 3 * 1024 * 1024)
    tm_seq = _row_tile(seq, dm, 4, 3 * 1024 * 1024)
    n_chips = 4
    n_groups = len(POOL_WINDOWS)
    gw = dm // n_groups
    tc = min(256, gw)
    cx, cy, cc = _place()
    chip = 2 * cx + cy
    dloc = dm // n_chips

    pool2 = pool_w.reshape(n_groups * pool_w.shape[1], gw)
    big = {"w_in": w_in, "w_gate_up": w_gate_up, "pool_w": pool2, "conv_out_w": conv_out_w, "w_o": w_o, "w_down": w_down}
    chip1 = jnp.reshape(chip, (1,)).astype(jnp.int32)
    core = jnp.reshape(cc, (1,)).astype(jnp.int32)
    small_loc = jnp.concatenate([meta_tokens, jnp.pad(conv_w, ((0, 8 - conv_w.shape[0]), (0, 0))),
                                 jnp.zeros((8, dloc), F32)], axis=0)
    g1, g2, g3 = norm_mix_g.reshape(1, dm), norm_ffn_g.reshape(1, dm), norm_final_g.reshape(1, dm)
    b_gate2 = b_gate.reshape(2, dm)
    ps = pool_scale.reshape(1, dm)
    mine = jnp.stack([chip, cc]).astype(jnp.int32)
    other = jnp.stack([chip, 1 - cc]).astype(jnp.int32)
    first = [_cast_into_slot("place_small", small_loc, chip1, F32), _cast_half_into_slot("cast_w_in_sent", w_in, mine, BF16)]
    sems, first, token = _ag_start("ag_start_first", first)
    first[1] = _cast_half_into_slot("cast_w_in_kept", w_in, other, BF16, into=first[1])
    cast = {nme: _cast_into_slot("cast_" + nme, big[nme], chip1, BF16, deps=(token,))
            for nme in ["pool_w", "conv_out_w", "w_o", "w_gate_up", "w_down"]}
    sems, first = _ag_relay("ag_relay_first", first, sems, list(cast.values()))
    sems, (small4, w_in4) = _ag_relay_wait("ag_relay_wait_first", first, sems, [])
    (small4,) = _ag_final_wait("ag_final_wait_small", [small4], sems, [])
    mixer_w = [cast["pool_w"], cast["conv_out_w"], cast["w_o"]]
    sems_mix, mixer_w, token = _ag_start("ag_start_mixer", mixer_w, deps=(small4,))
    sems_gu, (w_gu4,), token = _ag_start("ag_start_gate_up", [cast["w_gate_up"]], deps=(token,))

    small_f = jnp.transpose(small4, (1, 0, 2)).reshape(small4.shape[1], dm)
    meta_f = small_f[:N_META]
    conv_w_f = small_f[N_META:N_META + 3]
    tail_rows = jnp.concatenate([jnp.zeros((tail - N_META, dm), F32), meta_f], axis=0)
    h0_hn1 = _rms_fwd_into("rms_mix", x[0], g1, lp, 0, tm_seq, deps=(token,))
    h0, hn1 = _rms_fwd_into("rms_mix_tail", tail_rows, g1, lp, seq, tail, prev=h0_hn1)
    (w_in4,) = _ag_final_wait("ag_final_wait_first", [w_in4], sems, [hn1], first=1)
    proj = _nn_sharded("proj_0", hn1, w_in4, 6, part=(0, 2))
    sems_mix, mixer_w = _ag_relay("ag_relay_mixer", mixer_w, sems_mix, [proj])
    proj = _nn_sharded("proj_1", hn1, w_in4, 6, part=(1, 2), prev=proj, deps=(mixer_w[0],))
    pooled, z = _mixer_fwd("mixer_fwd", proj, conv_w_f, tc, mixer_w[0])
    (sems_gu, (w_gu4,)), (sems_down, (w_down4,)) = _ag_relay("ag_relay_gate_up", [w_gu4], sems_gu, [pooled],
                                                              then_start=[cast["w_down"]])
    sems_mix, mixer_w = _ag_relay_wait("ag_relay_wait_mixer", mixer_w, sems_mix, [w_down4])
    pool4, conv_out4, w_o4 = _ag_final_wait("ag_final_wait_mixer", mixer_w, sems_mix, [])
    pool_f = jnp.transpose(pool4.reshape(n_chips, n_groups, gw // n_chips, gw), (1, 0, 2, 3)).reshape(n_groups, gw, gw)
    conv_out_f = conv_out4.reshape(dm, dm)
    w_o_f = w_o4.reshape(dm, dm)
    ya = _pool_fwd("pool_proj", pooled, pool_f)
    yb = _nn_plain("conv_out", z, conv_out_f, BF16)
    mix = _gate_mix("gate_mix", proj, b_gate2, ya, ps, yb, tm_row)
    h1 = _nn_plain("attn_out", mix, w_o_f, F32, res=h0, tn_pref=256)
    sems_gu, (w_gu4,) = _ag_relay_wait("ag_relay_wait_gate_up", [w_gu4], sems_gu, [h1])
    hn2 = _rms_fwd("rms_ffn", h1, g2, tm_row, deps=(w_gu4,))
    (w_gu4,) = _ag_final_wait("ag_final_wait_gate_up", [w_gu4], sems_gu, [hn2])
    sems_down, (w_down4,) = _ag_relay("ag_relay_down", [w_down4], sems_down, [hn2])
    gu, act = _gate_up_swiglu("gate_up", hn2, w_gu4, w_down4)
    sems_down, (w_down4,) = _ag_relay_wait("ag_relay_wait_down", [w_down4], sems_down, [act])
    (w_down4,) = _ag_final_wait("ag_final_wait_down", [w_down4], sems_down, [])
    w_down_f = w_down4.reshape(-1, dm)
    h2 = _nn_rows("ffn_down", act, w_down_f, h1)
    dh2, dh2b, loss_cols, dg3 = _final_loss("final_loss", h2, g3, loss_target[0], tm_seq)
    dh2, dh2b = _zero_tail("final_loss_tail", [dh2, dh2b], tail)

    def scatter(tag, names_g, swap, after):
        grads_g, got = _swap_wait("swap_wait_" + tag, *swap, [after])
        pairs = [_pair_add("pair_add_" + nme, g4, rv, core) for nme, g4, rv in zip(names_g, grads_g, got)]
        return _scatter_start("scatter_start_" + tag, pairs)

    dgu = _dact_swiglu_bwd("d_gate_up", dh2b, w_down_f, gu)
    gw_down = _tn_plain("dw_down", act, dh2b)
    gw_gu = _tn_sharded("dw_gate_up", hn2, dgu, n_chips)
    swap_a, token = _swap_start("swap_start_a", [gw_gu, gw_down.reshape(n_chips, -1, dm)])
    dhn2 = _nt_sharded("d_hn2", dgu, w_gu4, tr_pref=2816, row_tiles=2, deps=(token,))
    flight_a, token = scatter("a", ["w_gate_up", "w_down"], swap_a, dhn2)
    dh1, dh1b, dg2 = _rms_bwd("rms_ffn_bwd", dhn2, h1, g2, dh2, tm_row, token)
    dmix = _nt_plain("d_mix", dh1b, w_o_f)
    gw_o = _tn_plain("dw_o", mix, dh1b)
    dproj, dyb, dya, db_gate, dps = _gate_bwd("gate_bwd", dmix, proj, b_gate2, ya, ps, yb, tm_row)
    gw_conv_out = _tn_plain("dw_conv_out", z, dyb)
    gw_pool = _pool_bwd_w("dw_pool", pooled, dya)
    gw_pool = jnp.transpose(gw_pool.reshape(n_groups, n_chips, gw // n_chips, gw), (1, 0, 2, 3))
    swap_b, token = _swap_start("swap_start_b", [gw_o.reshape(n_chips, dloc, dm), gw_conv_out.reshape(n_chips, dloc, dm),
                                                 gw_pool.reshape(n_chips, n_groups * (gw // n_chips), gw)])
    dpooled = _pool_bwd_act("d_pooled", dya, pool_f, deps=(token,))
    dz = _nt_plain("d_z", dyb, conv_out_f)
    flight_b, token = scatter("b", ["w_o", "conv_out_w", "pool_w"], swap_b, dz)
    dproj, dconv_w = _mixer_bwd("mixer_bwd", dz, dpooled, proj, conv_w_f, dproj, tc, token)
    gw_in0 = _tn_sharded("dw_in_0", hn1, dproj, n_chips, part=(0, 2))
    swap_c0, token = _swap_start("swap_start_c0", [gw_in0])
    gw_in1 = _tn_sharded("dw_in_1", hn1, dproj, n_chips, part=(1, 2), deps=(token,))
    flight_c0, token = scatter("c0", ["w_in_0"], swap_c0, gw_in1)
    swap_c, token = _swap_start("swap_start_c", [gw_in1], deps=(token,))
    groups_g = {"a": [("w_gate_up", (0, 1)), ("w_down", (0, 1))], "b": [("w_o", (0, 1)), ("conv_out_w", (0, 1)), ("pool_w", (0, 1))],
                "c0": [("w_in", (0, 2))], "c": [("w_in", (1, 2))]}

    def reduced(tag, flight, after):
        pairs, zones = _scatter_wait("scatter_wait_" + tag, *flight, after)
        halves = [_chip_sum("chip_sum_%s_%d" % (nme, part[0]), p, rv, chip1) for (nme, part), p, rv in zip(groups_g[tag], pairs, zones)]
        return _swap_start("send_start_" + tag, halves, halves=False)

    send_a, token = reduced("a", flight_a, [token])
    flight_c, token = scatter("c", ["w_in_1"], swap_c, token)
    dhn1 = _nt_in_proj("d_hn1", dproj, w_in4, deps=(token,))
    dx, dg1 = _rms_bwd_rows("rms_mix_bwd", dhn1, h0, g1, dh1, 0, seq, tm_seq, token)
    dtail, dg1 = _rms_bwd_rows("rms_mix_bwd_tail", dhn1, h0, g1, dh1, seq, tail, tail, dx, dg_prev=dg1)
    grad_x = dx[None]
    dmeta = dtail[tail - N_META:]

    given = dict(meta_tokens=(meta_tokens, m_meta_tokens, v_meta_tokens), norm_mix_g=(norm_mix_g, m_norm_mix_g, v_norm_mix_g),
                 w_in=(w_in, m_w_in, v_w_in), b_gate=(b_gate, m_b_gate, v_b_gate), pool_w=(pool_w, m_pool_w, v_pool_w),
                 pool_scale=(pool_scale, m_pool_scale, v_pool_scale), conv_w=(conv_w, m_conv_w, v_conv_w),
                 conv_out_w=(conv_out_w, m_conv_out_w, v_conv_out_w), w_o=(w_o, m_w_o, v_w_o),
                 norm_ffn_g=(norm_ffn_g, m_norm_ffn_g, v_norm_ffn_g), w_gate_up=(w_gate_up, m_w_gate_up, v_w_gate_up),
                 w_down=(w_down, m_w_down, v_w_down), norm_final_g=(norm_final_g, m_norm_final_g, v_norm_final_g))
    order = list(given.keys())
    grad, delta, new_m, new_v = {}, {}, {}, {}
    vec = jnp.concatenate([dg1, dg2, dg3, db_gate, dps, loss_cols, dconv_w, dmeta], axis=0)
    loss_row = 5 * SMALL_ROWS
    results = {}

    def update(tag, send, after):
        halves, sib_halves = _swap_wait("send_wait_" + tag, *send, after, halves=False)
        deltas = []
        for (nme, part), g_own, g_sib in zip(groups_g[tag], halves, sib_halves):
            w, m, v = given[nme]
            shape2 = (2 * g_own.shape[0] * part[1], g_own.shape[1])
            results[nme] = _adamw_halves("adamw_%s_%d" % (nme, part[0]), w.reshape(shape2), g_own, g_sib, m.reshape(shape2),
                                         v.reshape(shape2), core, part=part, prev=results.get(nme))
            grad[nme], delta[nme], new_m[nme], new_v[nme] = [t.reshape(w.shape) for t in results[nme]]
            deltas.append(results[nme][1])
        return deltas

    done_a = update("a", send_a, [dx])
    send_b, token = reduced("b", flight_b, done_a)
    send_c0, token = reduced("c0", flight_c0, [token])
    done_b = update("b", send_b, [token])
    send_c, token = reduced("c", flight_c, done_b)
    me1 = jnp.reshape(4 * cx + 2 * cy + cc, (1,)).astype(jnp.int32)
    red_flight, token = _reduce_start(vec, [token])
    done_c0 = update("c0", send_c0, [token])
    done_c = update("c", send_c, done_c0)
    red, loss11 = _reduce_sum(*_reduce_wait(*red_flight, done_c), me1, loss_row, 0.5 / dm)
    loss = loss11[0, 0]

    small = {"norm_mix_g": (0, False), "norm_ffn_g": (SMALL_ROWS, False), "norm_final_g": (2 * SMALL_ROWS, False),
             "b_gate": (3 * SMALL_ROWS, False), "pool_scale": (4 * SMALL_ROWS, False), "conv_w": (6 * SMALL_ROWS, True),
             "meta_tokens": (7 * SMALL_ROWS, True)}

    def two_d(t, nme):
        return t if t.ndim == 2 else t.reshape(2 if nme == "b_gate" else 1, dm)

    res = _adamw_small("adamw_small", red, [small[nme][0] for nme in small], [small[nme][1] for nme in small],
                       [[two_d(t, nme) for t in given[nme]] for nme in small])
    for nme, res4 in zip(small, res):
        grad[nme], delta[nme], new_m[nme], new_v[nme] = [t.reshape(given[nme][0].shape) for t in res4]
    return (loss, grad_x, *[grad[nme] for nme in order], *[delta[nme] for nme in order],
            *[new_m[nme] for nme in order], *[new_v[nme] for nme in order])
```

```python
import math

import jax
import jax.numpy as jnp
from jax import lax
from jax.experimental import pallas as pl
from jax.experimental.pallas import tpu as pltpu

F32 = jnp.float32
BF16 = jnp.bfloat16
N_META = 16
POOL_WINDOWS = (2, 4, 8, 16)
EPS = 1e-6
ADAM_LR, ADAM_B1, ADAM_B2, ADAM_EPS, ADAM_WD, ADAM_STEP = 0.001, 0.9, 0.999, 1e-08, 0.01, 10
LANES = 128
V7X_VMEM_BYTES = 64 * 1024 * 1024
VMEM_LIMIT = V7X_VMEM_BYTES - 8 * 1024 * 1024
MESH = pl.DeviceIdType.MESH
ANY = pl.BlockSpec(memory_space=pl.ANY)
CHIP_FLIPS = ((1, 0), (0, 1), (1, 1))
SMALL_ROWS = 8
TAIL_ROWS = 32


def _pick(n, pref):
    best = None
    for t in range(LANES, min(n, pref) + 1, LANES):
        if n % t == 0:
            best = t
    assert best is not None, (n, pref)
    return best


def _params(n_axes=0):
    sem = ("arbitrary",) * n_axes if n_axes else None
    return pltpu.CompilerParams(dimension_semantics=sem, vmem_limit_bytes=VMEM_LIMIT)


_DIMS = {
    "nn": (((1,), (0,)), ((), ())),
    "nt": (((1,), (1,)), ((), ())),
    "tn": (((0,), (0,)), ((), ())),
}


def _matmul(name, mode, a, b, out_sds, grid, a_spec, b_spec, o_spec, nk, res=None, res_spec=None, acc_shape=None, deps=(),
            prev=None):
    out_dtype = out_sds.dtype
    in_place = nk > 1 and out_dtype == F32
    use_scratch = nk > 1 and not in_place
    rows = a_spec.block_shape[-2] if mode != "tn" else None
    chunk = _row_tile(rows, 1, 1, 1152) if rows is not None else None
    untouched = list(deps) + ([prev] if prev is not None else [])
    n_in = 2 + (res is not None) + len(untouched)

    def body(*refs):
        a_ref, b_ref = refs[:2]
        r_ref = refs[2] if res is not None else None
        o_ref, *scr = refs[n_in:]
        k = pl.program_id(len(grid) - 1) if nk > 1 else None

        def emit(sl):
            if sl is None:
                part = lax.dot_general(a_ref[...], b_ref[...], _DIMS[mode], preferred_element_type=F32)
                idx = (slice(None), slice(None))
            else:
                part = lax.dot_general(a_ref[sl, :], b_ref[...], _DIMS[mode], preferred_element_type=F32)
                idx = (sl, slice(None))
            if nk == 1:
                if r_ref is not None:
                    part = part + r_ref[idx]
                o_ref[idx] = part.astype(out_dtype)
                return
            acc = scr[0] if use_scratch else o_ref

            @pl.when(k == 0)
            def _():
                first = part
                if r_ref is not None and in_place:
                    first = first + r_ref[idx]
                acc[idx] = first

            @pl.when(k > 0)
            def _():
                acc[idx] += part

            if use_scratch:

                @pl.when(k == nk - 1)
                def _():
                    o_ref[idx] = acc[idx].astype(out_dtype)

        if mode == "tn" or chunk == rows:
            emit(None)
        else:
            for m0 in range(0, rows, chunk):
                emit(pl.ds(m0, chunk))

    ins = [a, b] + ([res] if res is not None else []) + untouched
    in_specs = [a_spec, b_spec] + ([res_spec] if res is not None else []) + [ANY] * len(untouched)
    scratch = [pltpu.VMEM(acc_shape, F32)] if use_scratch else []
    return pl.pallas_call(
        body, name=name, out_shape=out_sds, grid=grid, in_specs=in_specs, out_specs=o_spec,
        scratch_shapes=scratch, input_output_aliases={} if prev is None else {n_in - 1: 0},
        compiler_params=_params(len(grid)),
    )(*ins)


def _nn_sharded(name, a, w4, nseg, part=(0, 1), prev=None, deps=()):
    lp, kdim = a.shape
    s, _, nloc = w4.shape
    segw = s * nloc // nseg
    tn = _pick(math.gcd(nloc, segw), 1536)
    bw, bo = nloc // tn, segw // tn
    steps = s * bw // part[1]
    j0 = part[0] * steps
    return _matmul(
        name, "nn", a, w4, jax.ShapeDtypeStruct((nseg, lp, segw), BF16), (steps,),
        pl.BlockSpec((lp, kdim), lambda j: (0, 0)),
        pl.BlockSpec((None, kdim, tn), lambda j: ((j0 + j) // bw, 0, (j0 + j) % bw)),
        pl.BlockSpec((None, lp, tn), lambda j: ((j0 + j) // bo, 0, (j0 + j) % bo)), 1, deps=deps, prev=prev)


def _nt_in_proj(name, dseg, w4, row_tiles=2, to_pref=1024, deps=()):
    nseg, lp, segw = dseg.shape
    s, kdim, nloc = w4.shape
    assert nseg * segw == s * nloc and 2 * nloc == 3 * segw, (dseg.shape, w4.shape)
    half = segw // 2
    to = _pick(kdim, to_pref)
    tm = lp // row_tiles

    def body(full_ref, half_ref, w_ref, *rest):
        o_ref = rest[len(deps)]
        r = pl.program_id(2)

        def contribution(full_first):
            lo, hi = (pl.ds(0, segw), pl.ds(segw, half)) if full_first else (pl.ds(half, segw), pl.ds(0, half))
            return (lax.dot_general(full_ref[...], w_ref[:, lo], _DIMS["nt"], preferred_element_type=F32)
                    + lax.dot_general(half_ref[...], w_ref[:, hi], _DIMS["nt"], preferred_element_type=F32))

        @pl.when(r == 0)
        def _():
            o_ref[...] = contribution(True)

        for ri in range(1, s):

            @pl.when(r == ri)
            def _(ri=ri):
                o_ref[...] += contribution(ri % 2 == 0)

    return pl.pallas_call(
        body, name=name, out_shape=jax.ShapeDtypeStruct((lp, kdim), F32), grid=(row_tiles, kdim // to, s),
        in_specs=[pl.BlockSpec((None, tm, segw), lambda m, j, r: ((3 * r + 1) // 2, m, 0)),
                  pl.BlockSpec((None, tm, half), lambda m, j, r: (1 + 3 * (r // 2), m, r % 2)),
                  pl.BlockSpec((None, to, nloc), lambda m, j, r: (r, j, 0))] + [ANY] * len(deps),
        out_specs=pl.BlockSpec((tm, to), lambda m, j, r: (m, j)), compiler_params=_params(3),
    )(dseg, dseg, w4, *deps)


def _nn_plain(name, a, w, out_dtype, res=None, tn_pref=512, tk_pref=2048, deps=()):
    lp, kdim = a.shape
    n = w.shape[1]
    tn = _pick(n, tn_pref)
    tk = kdim if kdim <= tk_pref else _pick(kdim, tk_pref)
    nk = kdim // tk
    grid = (n // tn, nk) if nk > 1 else (n // tn,)
    if nk > 1:
        a_spec = pl.BlockSpec((lp, tk), lambda j, k: (0, k))
        w_spec = pl.BlockSpec((tk, tn), lambda j, k: (k, j))
        o_spec = pl.BlockSpec((lp, tn), lambda j, k: (0, j))
    else:
        a_spec = pl.BlockSpec((lp, tk), lambda j: (0, 0))
        w_spec = pl.BlockSpec((tk, tn), lambda j: (0, j))
        o_spec = pl.BlockSpec((lp, tn), lambda j: (0, j))
    return _matmul(name, "nn", a, w, jax.ShapeDtypeStruct((lp, n), out_dtype), grid, a_spec, w_spec, o_spec, nk,
                   res=res, res_spec=o_spec if res is not None else None, acc_shape=(lp, tn), deps=deps)


def _nt_plain(name, a, w, tn_pref=512):
    lp, kdim = a.shape
    n = w.shape[0]
    tn = _pick(n, tn_pref)
    return _matmul(
        name, "nt", a, w, jax.ShapeDtypeStruct((lp, n), BF16), (n // tn,),
        pl.BlockSpec((lp, kdim), lambda j: (0, 0)),
        pl.BlockSpec((tn, kdim), lambda j: (j, 0)),
        pl.BlockSpec((lp, tn), lambda j: (0, j)), 1)


def _nt_sharded(name, dseg, w4, to_pref=1024, tr_pref=1536, row_tiles=1, deps=()):
    nseg, lp, segw = dseg.shape
    s, kdim, nloc = w4.shape
    tr = _pick(math.gcd(nloc, segw), tr_pref)
    ba, bw = segw // tr, nloc // tr
    nr = s * bw
    to = _pick(kdim, to_pref)
    tm = lp // row_tiles
    return _matmul(
        name, "nt", dseg, w4, jax.ShapeDtypeStruct((lp, kdim), F32), (row_tiles, kdim // to, nr),
        pl.BlockSpec((None, tm, tr), lambda m, j, r: (r // ba, m, r % ba)),
        pl.BlockSpec((None, to, tr), lambda m, j, r: (r // bw, j, r % bw)),
        pl.BlockSpec((tm, to), lambda m, j, r: (m, j)), nr, deps=deps)


def _nn_rows(name, a, w, res, row_tiles=2, tn_pref=512):
    lp, kdim = a.shape
    n = w.shape[1]
    tn = _pick(n, tn_pref)
    tm = lp // row_tiles
    blk = pl.BlockSpec((tm, tn), lambda i, j: (i, j))
    return _matmul(name, "nn", a, w, jax.ShapeDtypeStruct((lp, n), F32), (row_tiles, n // tn),
                   pl.BlockSpec((tm, kdim), lambda i, j: (i, 0)), pl.BlockSpec((kdim, tn), lambda i, j: (0, j)), blk, 1,
                   res=res, res_spec=blk)


def _tn_plain(name, a, d, tk_pref=1024):
    lp, kdim = a.shape
    n = d.shape[1]
    tk = _pick(kdim, tk_pref)
    return _matmul(
        name, "tn", a, d, jax.ShapeDtypeStruct((kdim, n), BF16), (kdim // tk,),
        pl.BlockSpec((lp, tk), lambda i: (0, i)),
        pl.BlockSpec((lp, n), lambda i: (0, 0)),
        pl.BlockSpec((tk, n), lambda i: (i, 0)), 1)


def _tn_sharded(name, a, dseg, s, part=(0, 1), tk_pref=1024, deps=()):
    lp, kdim = a.shape
    nseg, _, segw = dseg.shape
    nloc = nseg * segw // s
    tn = _pick(math.gcd(nloc, segw), 1536)
    bd, bo = segw // tn, nloc // tn
    kpart = kdim // part[1]
    tk = _pick(kpart, tk_pref)
    i0 = part[0] * (kpart // tk)

    def body(a_ref, d_ref, *rest):
        o_ref, at_ref = rest[len(deps):]

        @pl.when(pl.program_id(1) == 0)
        def _():
            at_ref[...] = a_ref[...].T

        o_ref[...] = jnp.dot(at_ref[...], d_ref[...], preferred_element_type=F32).astype(BF16)

    return pl.pallas_call(
        body, name=name, out_shape=jax.ShapeDtypeStruct((s, kpart, nloc), BF16), grid=(kpart // tk, s * bo),
        in_specs=[pl.BlockSpec((lp, tk), lambda i, j: (0, i0 + i)),
                  pl.BlockSpec((None, lp, tn), lambda i, j: (j // bd, 0, j % bd))] + [ANY] * len(deps),
        out_specs=pl.BlockSpec((None, tk, tn), lambda i, j: (j // bo, i, j % bo)),
        scratch_shapes=[pltpu.VMEM((tk, lp), BF16)], compiler_params=_params(2),
    )(a, dseg, *deps)


def _silu_parts(gt):
    sg = jax.nn.sigmoid(gt)
    return gt * sg, sg * (1.0 + gt * (1.0 - sg))


def _gate_up_swiglu(name, a, w4, dep, tn_pref=256):
    lp, kdim = a.shape
    s, _, nloc = w4.shape
    f = s * nloc // 2
    tn = _pick(nloc, tn_pref)
    bw = nloc // tn
    chunk = _row_tile(lp, 1, 1, 576)

    def body(a_ref, wg_ref, wu_ref, _, fac_ref, act_ref):
        for m0 in range(0, lp, chunk):
            sl = pl.ds(m0, chunk)
            gt = jnp.dot(a_ref[sl, :], wg_ref[...], preferred_element_type=F32)
            up = jnp.dot(a_ref[sl, :], wu_ref[...], preferred_element_type=F32)
            silu, dsilu = _silu_parts(gt)
            fac_ref[0, sl, :] = (up * dsilu).astype(BF16)
            fac_ref[1, sl, :] = silu.astype(BF16)
            act_ref[sl, :] = (silu * up).astype(BF16)

    return pl.pallas_call(
        body, name=name, grid=(f // tn,),
        out_shape=(jax.ShapeDtypeStruct((2, lp, f), BF16), jax.ShapeDtypeStruct((lp, f), BF16)),
        in_specs=[pl.BlockSpec((lp, kdim), lambda j: (0, 0)),
                  pl.BlockSpec((None, kdim, tn), lambda j: (j // bw, 0, j % bw)),
                  pl.BlockSpec((None, kdim, tn), lambda j: (s // 2 + j // bw, 0, j % bw)), ANY],
        out_specs=(pl.BlockSpec((2, lp, tn), lambda j: (0, 0, j)), pl.BlockSpec((lp, tn), lambda j: (0, j))),
        compiler_params=_params(1),
    )(a, w4, w4, dep)


def _dact_swiglu_bwd(name, d, w, gu, tn_pref=512):
    lp, dm = d.shape
    f = w.shape[0]
    tn = _pick(f, tn_pref)
    chunk = _row_tile(lp, 1, 1, 576)

    def body(d_ref, w_ref, g_ref, u_ref, o_ref):
        for m0 in range(0, lp, chunk):
            sl = pl.ds(m0, chunk)
            dact = lax.dot_general(d_ref[sl, :], w_ref[...], _DIMS["nt"], preferred_element_type=F32)
            o_ref[0, sl, :] = (dact * g_ref[sl, :].astype(F32)).astype(BF16)
            o_ref[1, sl, :] = (dact * u_ref[sl, :].astype(F32)).astype(BF16)

    return pl.pallas_call(
        body, name=name, grid=(f // tn,), out_shape=jax.ShapeDtypeStruct((2, lp, f), BF16),
        in_specs=[pl.BlockSpec((lp, dm), lambda j: (0, 0)), pl.BlockSpec((tn, dm), lambda j: (j, 0)),
                  pl.BlockSpec((None, lp, tn), lambda j: (0, 0, j)), pl.BlockSpec((None, lp, tn), lambda j: (1, 0, j))],
        out_specs=pl.BlockSpec((2, lp, tn), lambda j: (0, 0, j)), compiler_params=_params(1),
    )(d, w, gu, gu)


def _pool_fwd(name, pooled, pw):
    lp, dm = pooled.shape
    g, gw, _ = pw.shape
    return _matmul(
        name, "nn", pooled, pw, jax.ShapeDtypeStruct((lp, dm), BF16), (g,),
        pl.BlockSpec((lp, gw), lambda gi: (0, gi)), pl.BlockSpec((None, gw, gw), lambda gi: (gi, 0, 0)),
        pl.BlockSpec((lp, gw), lambda gi: (0, gi)), 1)


def _pool_bwd_act(name, dya, pw, deps=()):
    lp, dm = dya.shape
    g, gw, _ = pw.shape
    return _matmul(
        name, "nt", dya, pw, jax.ShapeDtypeStruct((lp, dm), BF16), (g,),
        pl.BlockSpec((lp, gw), lambda gi: (0, gi)), pl.BlockSpec((None, gw, gw), lambda gi: (gi, 0, 0)),
        pl.BlockSpec((lp, gw), lambda gi: (0, gi)), 1, deps=deps)


def _pool_bwd_w(name, pooled, dya):
    lp, dm = pooled.shape
    g = len(POOL_WINDOWS)
    gw = dm // g
    return _matmul(
        name, "tn", pooled, dya, jax.ShapeDtypeStruct((g, gw, gw), BF16), (g,),
        pl.BlockSpec((lp, gw), lambda gi: (0, gi)), pl.BlockSpec((lp, gw), lambda gi: (0, gi)),
        pl.BlockSpec((None, gw, gw), lambda gi: (gi, 0, 0)), 1)


def _rms_fwd(name, h, g, tm, deps=()):
    lp, dm = h.shape

    def body(h_ref, g_ref, *rest):
        hv = h_ref[...]
        r = lax.rsqrt(jnp.mean(hv * hv, axis=-1, keepdims=True) + EPS)
        rest[-1][...] = (hv * r * g_ref[...]).astype(BF16)

    row = pl.BlockSpec((tm, dm), lambda i: (i, 0))
    return pl.pallas_call(
        body, name=name, out_shape=jax.ShapeDtypeStruct((lp, dm), BF16), grid=(lp // tm,),
        in_specs=[row, pl.BlockSpec((1, dm), lambda i: (0, 0))] + [ANY] * len(deps), out_specs=row, compiler_params=_params(1),
    )(h, g, *deps)


def _rms_fwd_into(name, src, g, lp, row0, tm, prev=None, deps=()):
    n, dm = src.shape
    b0 = row0 // tm
    n_in = 2 + len(deps)

    def body(s_ref, g_ref, *rest):
        h_ref, o_ref = rest[-2:]
        hv = s_ref[...]
        r = lax.rsqrt(jnp.mean(hv * hv, axis=-1, keepdims=True) + EPS)
        h_ref[...] = hv
        o_ref[...] = (hv * r * g_ref[...]).astype(BF16)

    row = pl.BlockSpec((tm, dm), lambda i: (b0 + i, 0))
    return pl.pallas_call(
        body, name=name, grid=(n // tm,),
        out_shape=(jax.ShapeDtypeStruct((lp, dm), F32), jax.ShapeDtypeStruct((lp, dm), BF16)),
        in_specs=[pl.BlockSpec((tm, dm), lambda i: (i, 0)), pl.BlockSpec((1, dm), lambda i: (0, 0))]
        + [ANY] * (len(deps) + (0 if prev is None else 2)),
        out_specs=(row, row), input_output_aliases={} if prev is None else {n_in: 0, n_in + 1: 1},
        compiler_params=_params(1),
    )(src, g, *deps, *(prev or ()))


def _rms_bwd(name, dy, h, g, dres, tm, dep):
    lp, dm = h.shape

    def body(dy_ref, h_ref, g_ref, dr_ref, _, dh_ref, dhb_ref, dg_ref):
        hv = h_ref[...]
        r = lax.rsqrt(jnp.mean(hv * hv, axis=-1, keepdims=True) + EPS)
        xhat = hv * r
        dyv = dy_ref[...]
        dxh = dyv * g_ref[...]
        dh = dr_ref[...] + r * (dxh - xhat * jnp.mean(dxh * xhat, axis=-1, keepdims=True))
        dh_ref[...] = dh
        dhb_ref[...] = dh.astype(BF16)

        @pl.when(pl.program_id(0) == 0)
        def _():
            dg_ref[...] = jnp.zeros_like(dg_ref)

        dg_ref[0:1, :] += jnp.sum(dyv * xhat, axis=0, keepdims=True)

    row = pl.BlockSpec((tm, dm), lambda i: (i, 0))
    slab = pl.BlockSpec((SMALL_ROWS, dm), lambda i: (0, 0))
    return pl.pallas_call(
        body, name=name, grid=(lp // tm,),
        out_shape=(jax.ShapeDtypeStruct((lp, dm), F32), jax.ShapeDtypeStruct((lp, dm), BF16),
                   jax.ShapeDtypeStruct((SMALL_ROWS, dm), F32)),
        in_specs=[row, row, pl.BlockSpec((1, dm), lambda i: (0, 0)), row, ANY], out_specs=(row, row, slab),
        compiler_params=_params(1),
    )(dy, h, g, dres, dep)


def _rms_bwd_rows(name, dy, h, g, dres, row0, nrows, tm, dep, dg_prev=None):
    dm = h.shape[1]
    b0 = row0 // tm

    def body(dy_ref, h_ref, g_ref, dr_ref, *rest):
        d_ref, dg_ref = rest[-2:]
        hv = h_ref[...]
        r = lax.rsqrt(jnp.mean(hv * hv, axis=-1, keepdims=True) + EPS)
        xhat = hv * r
        dyv = dy_ref[...]
        dxh = dyv * g_ref[...]
        d_ref[...] = dr_ref[...] + r * (dxh - xhat * jnp.mean(dxh * xhat, axis=-1, keepdims=True))

        @pl.when(pl.program_id(0) == 0)
        def _():
            dg_ref[...] = jnp.zeros_like(dg_ref) if dg_prev is None else rest[1][...]

        dg_ref[0:1, :] += jnp.sum(dyv * xhat, axis=0, keepdims=True)

    row = pl.BlockSpec((tm, dm), lambda i: (b0 + i, 0))
    slab = pl.BlockSpec((SMALL_ROWS, dm), lambda i: (0, 0))
    extra = [dep] + ([dg_prev] if dg_prev is not None else [])
    return pl.pallas_call(
        body, name=name, grid=(nrows // tm,),
        out_shape=(jax.ShapeDtypeStruct((nrows, dm), F32), jax.ShapeDtypeStruct((SMALL_ROWS, dm), F32)),
        in_specs=[row, row, pl.BlockSpec((1, dm), lambda i: (0, 0)), row, ANY] + ([slab] if dg_prev is not None else []),
        out_specs=(pl.BlockSpec((tm, dm), lambda i: (i, 0)), slab), compiler_params=_params(1),
    )(dy, h, g, dres, *extra)


def _gate_mix(name, proj, b_gate2, ya, pool_scale, yb, tm):
    _, lp, dm = proj.shape

    def body(ga_ref, gr_ref, b_ref, ya_ref, ps_ref, yb_ref, o_ref):
        g_a = jax.nn.sigmoid(ga_ref[...].astype(F32) + b_ref[0:1, :])
        g_b = jax.nn.sigmoid(gr_ref[...].astype(F32) + b_ref[1:2, :])
        y_a = ya_ref[...].astype(F32) * ps_ref[...]
        o_ref[...] = (g_a * y_a + g_b * yb_ref[...].astype(F32)).astype(BF16)

    row = pl.BlockSpec((tm, dm), lambda i: (i, 0))
    return pl.pallas_call(
        body, name=name, out_shape=jax.ShapeDtypeStruct((lp, dm), BF16), grid=(lp // tm,),
        in_specs=[pl.BlockSpec((None, tm, dm), lambda i: (4, i, 0)), pl.BlockSpec((None, tm, dm), lambda i: (5, i, 0)),
                  pl.BlockSpec((2, dm), lambda i: (0, 0)), row, pl.BlockSpec((1, dm), lambda i: (0, 0)), row],
        out_specs=row, compiler_params=_params(1),
    )(proj, proj, b_gate2, ya, pool_scale, yb)


def _gate_bwd(name, dmix, proj, b_gate2, ya, pool_scale, yb, tm):
    _, lp, dm = proj.shape

    def body(dm_ref, ga_ref, gr_ref, b_ref, ya_ref, ps_ref, yb_ref, dp_ref, dyb_ref, dya_ref, db_ref, dps_ref):
        dmx = dm_ref[...].astype(F32)
        g_a = jax.nn.sigmoid(ga_ref[...].astype(F32) + b_ref[0:1, :])
        g_b = jax.nn.sigmoid(gr_ref[...].astype(F32) + b_ref[1:2, :])
        ya_pre = ya_ref[...].astype(F32)
        ybv = yb_ref[...].astype(F32)
        ps = ps_ref[...]
        dga = dmx * (ya_pre * ps) * (g_a * (1.0 - g_a))
        dgr = dmx * ybv * (g_b * (1.0 - g_b))
        dp_ref[0] = dga.astype(BF16)
        dp_ref[1] = dgr.astype(BF16)
        dyb_ref[...] = (dmx * g_b).astype(BF16)
        dya_ref[...] = (dmx * g_a * ps).astype(BF16)

        @pl.when(pl.program_id(0) == 0)
        def _():
            db_ref[...] = jnp.zeros_like(db_ref)
            dps_ref[...] = jnp.zeros_like(dps_ref)

        db_ref[0:1, :] += jnp.sum(dga, axis=0, keepdims=True)
        db_ref[1:2, :] += jnp.sum(dgr, axis=0, keepdims=True)
        dps_ref[0:1, :] += jnp.sum(dmx * g_a * ya_pre, axis=0, keepdims=True)

    row = pl.BlockSpec((tm, dm), lambda i: (i, 0))
    one = pl.BlockSpec((1, dm), lambda i: (0, 0))
    slab = pl.BlockSpec((SMALL_ROWS, dm), lambda i: (0, 0))
    return pl.pallas_call(
        body, name=name, grid=(lp // tm,),
        out_shape=(jax.ShapeDtypeStruct((6, lp, dm), BF16), jax.ShapeDtypeStruct((lp, dm), BF16),
                   jax.ShapeDtypeStruct((lp, dm), BF16), jax.ShapeDtypeStruct((SMALL_ROWS, dm), F32),
                   jax.ShapeDtypeStruct((SMALL_ROWS, dm), F32)),
        in_specs=[row, pl.BlockSpec((None, tm, dm), lambda i: (4, i, 0)), pl.BlockSpec((None, tm, dm), lambda i: (5, i, 0)),
                  pl.BlockSpec((2, dm), lambda i: (0, 0)), row, one, row],
        out_specs=(pl.BlockSpec((2, tm, dm), lambda i: (2, i, 0)), row, row, slab, slab),
        compiler_params=_params(1),
    )(dmix, proj, proj, b_gate2, ya, pool_scale, yb)


def _final_loss(name, h2, g3, target, tm):
    lp, dm = h2.shape
    seq = target.shape[0]

    def body(h_ref, g_ref, t_ref, dh_ref, dhb_ref, ls_ref, dg_ref):
        @pl.when(pl.program_id(0) == 0)
        def _():
            ls_ref[...] = jnp.zeros_like(ls_ref)
            dg_ref[...] = jnp.zeros_like(dg_ref)

        hv = h_ref[...]
        gv = g_ref[...]
        r = lax.rsqrt(jnp.mean(hv * hv, axis=-1, keepdims=True) + EPS)
        xhat = hv * r
        err = xhat * gv - t_ref[...]
        dout = err * (1.0 / dm)
        dxh = dout * gv
        dh = r * (dxh - xhat * jnp.mean(dxh * xhat, axis=-1, keepdims=True))
        dh_ref[...] = dh
        dhb_ref[...] = dh.astype(BF16)
        ls_ref[0:1, :] += jnp.sum(err * err, axis=0, keepdims=True)
        dg_ref[0:1, :] += jnp.sum(dout * xhat, axis=0, keepdims=True)

    row = pl.BlockSpec((tm, dm), lambda i: (i, 0))
    slab = pl.BlockSpec((SMALL_ROWS, dm), lambda i: (0, 0))
    return pl.pallas_call(
        body, name=name, grid=(seq // tm,),
        out_shape=(jax.ShapeDtypeStruct((lp, dm), F32), jax.ShapeDtypeStruct((lp, dm), BF16),
                   jax.ShapeDtypeStruct((SMALL_ROWS, dm), F32), jax.ShapeDtypeStruct((SMALL_ROWS, dm), F32)),
        in_specs=[row, pl.BlockSpec((1, dm), lambda i: (0, 0)), row],
        out_specs=(row, row, slab, slab), compiler_params=_params(1),
    )(h2, g3, target)


def _zero_tail(name, arrays, tail):
    n = len(arrays)
    lp, dm = arrays[0].shape
    last = lp // tail - 1

    def body(*refs):
        for o_ref in refs[n:]:
            o_ref[...] = jnp.zeros_like(o_ref)

    return pl.pallas_call(
        body, name=name, grid=(1,), out_shape=tuple(jax.ShapeDtypeStruct(a.shape, a.dtype) for a in arrays),
        in_specs=[ANY] * n, out_specs=tuple(pl.BlockSpec((tail, dm), lambda i: (last, 0)) for _ in arrays),
        input_output_aliases={a: a for a in range(n)}, compiler_params=_params(1),
    )(*arrays)


def _shift(v, k):
    return pltpu.roll(v, k % v.shape[0], axis=0)


def _window_sum(v, group, sign):
    s2 = v + _shift(v, sign * 1)
    s4 = s2 + _shift(s2, sign * 2)
    s8 = s4 + _shift(s4, sign * 4)
    s16 = s8 + _shift(s8, sign * 8)
    return jnp.where(group == 0, s2, jnp.where(group == 1, s4, jnp.where(group == 2, s8, s16)))


def _pool_count(lp, group):
    row = lax.broadcasted_iota(jnp.int32, (lp, 1), 0)
    window = jnp.left_shift(2, group).astype(F32)
    meta_pos = (row - (lp - N_META) + 1).astype(F32)
    return jnp.where(row >= lp - N_META, jnp.minimum(meta_pos, window), window)


def _mixer_fwd(name, proj, conv_w, tc, dep):
    _, lp, dm = proj.shape
    per_group = dm // len(POOL_WINDOWS) // tc

    def body(u_ref, gb_ref, gc_ref, v_ref, cw_ref, _, p_ref, z_ref):
        group = pl.program_id(0) // per_group
        u = u_ref[...].astype(F32)
        p_ref[...] = (_window_sum(u, group, 1) / _pool_count(lp, group) - u).astype(BF16)
        cv = gc_ref[...].astype(F32) * v_ref[...].astype(F32)
        conv = cw_ref[0:1, :] * _shift(cv, 2) + cw_ref[1:2, :] * _shift(cv, 1) + cw_ref[2:3, :] * cv
        z_ref[...] = (gb_ref[...].astype(F32) * conv).astype(BF16)

    def seg(s):
        return pl.BlockSpec((None, lp, tc), lambda j: (s, 0, j))

    col = pl.BlockSpec((lp, tc), lambda j: (0, j))
    return pl.pallas_call(
        body, name=name, grid=(dm // tc,),
        out_shape=(jax.ShapeDtypeStruct((lp, dm), BF16), jax.ShapeDtypeStruct((lp, dm), BF16)),
        in_specs=[seg(0), seg(1), seg(2), seg(3), pl.BlockSpec((3, tc), lambda j: (0, j)), ANY],
        out_specs=(col, col), compiler_params=_params(1),
    )(proj, proj, proj, proj, conv_w, dep)


def _mixer_bwd(name, dz, dpooled, proj, conv_w, dproj, tc, dep):
    _, lp, dm = proj.shape
    per_group = dm // len(POOL_WINDOWS) // tc

    def body(dz_ref, dp_ref, gb_ref, gc_ref, v_ref, cw_ref, _, __, o_ref, dcw_ref):
        group = pl.program_id(0) // per_group
        dzv = dz_ref[...].astype(F32)
        gb = gb_ref[...].astype(F32)
        gc = gc_ref[...].astype(F32)
        vv = v_ref[...].astype(F32)
        cv = gc * vv
        c1 = _shift(cv, 1)
        c2 = _shift(cv, 2)
        w0, w1, w2 = cw_ref[0:1, :], cw_ref[1:2, :], cw_ref[2:3, :]
        o_ref[1] = (dzv * (w0 * c2 + w1 * c1 + w2 * cv)).astype(BF16)
        dconv = dzv * gb
        dcw_ref[...] = jnp.zeros_like(dcw_ref)
        dcw_ref[0:1, :] = jnp.sum(dconv * c2, axis=0, keepdims=True)
        dcw_ref[1:2, :] = jnp.sum(dconv * c1, axis=0, keepdims=True)
        dcw_ref[2:3, :] = jnp.sum(dconv * cv, axis=0, keepdims=True)
        dcv = w0 * _shift(dconv, -2) + w1 * _shift(dconv, -1) + w2 * dconv
        o_ref[2] = (dcv * vv).astype(BF16)
        o_ref[3] = (dcv * gc).astype(BF16)
        dpv = dp_ref[...].astype(F32)
        o_ref[0] = (_window_sum(dpv / _pool_count(lp, group), group, -1) - dpv).astype(BF16)

    def seg(s):
        return pl.BlockSpec((None, lp, tc), lambda j: (s, 0, j))

    col = pl.BlockSpec((lp, tc), lambda j: (0, j))
    return pl.pallas_call(
        body, name=name, grid=(dm // tc,),
        out_shape=(jax.ShapeDtypeStruct(dproj.shape, BF16), jax.ShapeDtypeStruct((SMALL_ROWS, dm), F32)),
        in_specs=[col, col, seg(1), seg(2), seg(3), pl.BlockSpec((3, tc), lambda j: (0, j)), ANY, ANY],
        out_specs=(pl.BlockSpec((4, lp, tc), lambda j: (0, 0, j)), pl.BlockSpec((SMALL_ROWS, tc), lambda j: (0, j))),
        input_output_aliases={6: 0}, compiler_params=_params(1),
    )(dz, dpooled, proj, proj, proj, conv_w, dproj, dep)


def _row_tile(r, c, bytes_per_row_elem=4, budget=2 * 1024 * 1024):
    best = None
    for t in range(16, r + 1, 16):
        if r % t == 0 and t * c * bytes_per_row_elem <= budget:
            best = t
    return best if best is not None else r


def _pair_add(name, g4, recv, core):
    s, r, c = g4.shape
    h = r // 2
    tr = _row_tile(h, c, budget=6 * 1024 * 1024)
    nb = h // tr

    def body(core_ref, g_ref, r_ref, o_ref):
        o_ref[...] = (g_ref[...].astype(F32) + r_ref[...].astype(F32)).astype(BF16)

    grid_spec = pltpu.PrefetchScalarGridSpec(
        num_scalar_prefetch=1, grid=(s, nb),
        in_specs=[pl.BlockSpec((None, tr, c), lambda si, j, core_ref: (si, core_ref[0] * nb + j, 0)),
                  pl.BlockSpec((None, tr, c), lambda si, j, core_ref: (si, j, 0))],
        out_specs=pl.BlockSpec((None, tr, c), lambda si, j, core_ref: (si, j, 0)))
    return pl.pallas_call(
        body, name=name, out_shape=jax.ShapeDtypeStruct((s, h, c), BF16), grid_spec=grid_spec,
        compiler_params=_params(2),
    )(core, g4, recv)


def _chip_sum(name, parts, recv, chip):
    _, h, c = parts.shape
    tr = _row_tile(h, c)

    def body(chip_ref, p_ref, r_ref, o_ref):
        acc = p_ref[...].astype(F32)
        for i in range(len(CHIP_FLIPS)):
            acc = acc + r_ref[i].astype(F32)
        o_ref[...] = acc

    grid_spec = pltpu.PrefetchScalarGridSpec(
        num_scalar_prefetch=1, grid=(h // tr,),
        in_specs=[pl.BlockSpec((None, tr, c), lambda j, chip_ref: (chip_ref[0], j, 0)),
                  pl.BlockSpec((len(CHIP_FLIPS), tr, c), lambda j, chip_ref: (0, j, 0))],
        out_specs=pl.BlockSpec((tr, c), lambda j, chip_ref: (j, 0)))
    return pl.pallas_call(
        body, name=name, out_shape=jax.ShapeDtypeStruct((h, c), F32), grid_spec=grid_spec, compiler_params=_params(1),
    )(chip, parts, recv)


def _adam_update(w, gv, m, v):
    c1 = 1.0 - ADAM_B1 ** ADAM_STEP
    c2 = 1.0 - ADAM_B2 ** ADAM_STEP
    nm = ADAM_B1 * m + (1.0 - ADAM_B1) * gv
    nv = ADAM_B2 * v + (1.0 - ADAM_B2) * (gv * gv)
    return -ADAM_LR * ((nm / c1) / (jnp.sqrt(nv / c2) + ADAM_EPS) + ADAM_WD * w), nm, nv


def _adamw_halves(name, w, g_own, g_sib, m, v, core, part=(0, 1), prev=None):
    r, c = w.shape
    rp = r // part[1]
    h = rp // 2
    tr = _row_tile(h, c, budget=2 * 1024 * 1024)
    nbh = h // tr
    j0 = part[0] * 2 * nbh
    n_prev = 0 if prev is None else 4

    def body(core_ref, w_ref, go_ref, gs_ref, m_ref, v_ref, *rest):
        g_ref, d_ref, nm_ref, nv_ref = rest[n_prev:]
        mine = (pl.program_id(0) // nbh) == core_ref[0]
        gv = jnp.where(mine, go_ref[...], gs_ref[...])
        g_ref[...] = gv
        d_ref[...], nm_ref[...], nv_ref[...] = _adam_update(w_ref[...], gv, m_ref[...], v_ref[...])

    def blk(fn):
        return pl.BlockSpec((tr, c), fn)

    full = blk(lambda j, core_ref: (j0 + j, 0))
    own = blk(lambda j, core_ref: (jnp.clip(j - core_ref[0] * nbh, 0, nbh - 1), 0))
    sib = blk(lambda j, core_ref: (jnp.clip(j - (1 - core_ref[0]) * nbh, 0, nbh - 1), 0))
    grid_spec = pltpu.PrefetchScalarGridSpec(
        num_scalar_prefetch=1, grid=(2 * nbh,), in_specs=[full, own, sib, full, full] + [ANY] * n_prev, out_specs=(full,) * 4)
    sds = jax.ShapeDtypeStruct((r, c), F32)
    return pl.pallas_call(
        body, name=name, out_shape=(sds,) * 4, grid_spec=grid_spec, compiler_params=_params(1),
        input_output_aliases={6 + i: i for i in range(n_prev)},
    )(core, w, g_own, g_sib, m, v, *(prev or ()))


def _adamw_small(name, vec, zone, me, loss_row, loss_scale, rows, cols, params):
    n = len(params)
    whole = pl.BlockSpec(memory_space=pltpu.VMEM)

    def body(me_ref, v_ref, z_ref, *refs):
        ins, l_ref, outs, red_ref = refs[:3 * n], refs[3 * n], refs[3 * n + 1:7 * n + 1], refs[7 * n + 1]
        acc = None
        for i in range(N_PEERS + 1):
            term = jnp.where(me_ref[0] == i, v_ref[...], z_ref[i])
            acc = term if acc is None else acc + term
        red_ref[...] = acc
        l_ref[...] = jnp.sum(acc[loss_row:loss_row + SMALL_ROWS, :], axis=(0, 1), keepdims=True) * loss_scale
        chip = 2 * lax.axis_index("x") + lax.axis_index("y")
        for i in range(n):
            w_ref, m_ref, v_ref = ins[3 * i:3 * i + 3]
            r, c = w_ref.shape
            g = red_ref[pl.ds(rows[i], r), pl.ds(pl.multiple_of(chip * c, LANES), c)] if cols[i] else red_ref[pl.ds(rows[i], r), :]
            outs[4 * i][...] = g
            outs[4 * i + 1][...], outs[4 * i + 2][...], outs[4 * i + 3][...] = _adam_update(w_ref[...], g, m_ref[...], v_ref[...])

    flat = [t for p in params for t in p]
    res = pl.pallas_call(
        body, name=name,
        out_shape=(jax.ShapeDtypeStruct((1, 1), F32),) + tuple(jax.ShapeDtypeStruct(p[0].shape, F32) for p in params for _ in range(4)),
        in_specs=[pl.BlockSpec(memory_space=pltpu.SMEM)] + [whole] * (2 + 3 * n), out_specs=(whole,) * (1 + 4 * n),
        scratch_shapes=[pltpu.VMEM(vec.shape, F32)], compiler_params=_params(),
    )(me, vec, zone, *flat)
    return res[0], [res[1 + 4 * i:5 + 4 * i] for i in range(n)]


def _cast_into_slot(name, w, chip, dtype, deps=()):
    r, c = w.shape
    tr = _row_tile(r, c)

    def body(chip_ref, w_ref, *rest):
        rest[-1][...] = w_ref[...].astype(dtype)

    grid_spec = pltpu.PrefetchScalarGridSpec(
        num_scalar_prefetch=1, grid=(r // tr,),
        in_specs=[pl.BlockSpec((tr, c), lambda j, chip_ref: (j, 0))] + [ANY] * len(deps),
        out_specs=pl.BlockSpec((None, tr, c), lambda j, chip_ref: (chip_ref[0], j, 0)))
    return pl.pallas_call(
        body, name=name, out_shape=jax.ShapeDtypeStruct((4, r, c), dtype), grid_spec=grid_spec, compiler_params=_params(1),
    )(chip, w, *deps)


def _cast_half_into_slot(name, w, chip_half, dtype, into=None):
    r, c = w.shape
    h = r // 2
    tr = _row_tile(h, c)
    nb = h // tr
    n_prev = 0 if into is None else 1

    def body(ids_ref, w_ref, *rest):
        rest[-1][...] = w_ref[...].astype(dtype)

    grid_spec = pltpu.PrefetchScalarGridSpec(
        num_scalar_prefetch=1, grid=(nb,),
        in_specs=[pl.BlockSpec((tr, c), lambda j, ids_ref: (ids_ref[1] * nb + j, 0))] + [ANY] * n_prev,
        out_specs=pl.BlockSpec((None, tr, c), lambda j, ids_ref: (ids_ref[0], ids_ref[1] * nb + j, 0)))
    return pl.pallas_call(
        body, name=name, out_shape=jax.ShapeDtypeStruct((4, r, c), dtype), grid_spec=grid_spec, compiler_params=_params(1),
        input_output_aliases={2: 0} if into is not None else {},
    )(chip_half, w, *([into] if into is not None else []))


def _place():
    return lax.axis_index("x"), lax.axis_index("y"), lax.axis_index("c")


def _chip_of(x, y, flip):
    px, py = x ^ flip[0], y ^ flip[1]
    return px, py, 2 * px + py


def _half(ref, which):
    rows = ref.shape[0] // 2
    return ref.at[pl.ds(which * rows, rows)]


HBM = pl.BlockSpec(memory_space=pltpu.HBM)
SEM = pl.BlockSpec(memory_space=pltpu.SEMAPHORE)
SPLIT_COPY = pltpu.CompilerParams(has_side_effects=pltpu.SideEffectType.DATAFLOW_SIDE_EFFECTING)


def _in_hbm(arrays):
    return [pltpu.with_memory_space_constraint(t, pltpu.HBM) for t in arrays]


TOKEN = jax.ShapeDtypeStruct((SMALL_ROWS, LANES), F32)
TOKEN_SPEC = pl.BlockSpec(memory_space=pltpu.VMEM)


NEIGHBOUR_FLIPS = CHIP_FLIPS[:2]


def _relay_chips(x, y, c):
    fx, fy = x ^ c, y ^ (1 - c)
    return (fx, fy), 2 * fx + fy, 2 * (1 - x) + (1 - y)


def _ag_start(name, slabs, deps=()):
    n = len(slabs)
    nn = len(NEIGHBOUR_FLIPS)

    def body(*refs):
        no = n + len(deps)
        ssem, rsem = refs[no], refs[no + 1]
        outs = refs[no + 2:no + 2 + n]
        token = refs[no + 2 + n]
        token[...] = jnp.zeros_like(token)
        x, y, c = _place()
        k = 2 * x + y
        for a in range(n):
            for j, flip in enumerate(NEIGHBOUR_FLIPS):
                px, py, _ = _chip_of(x, y, flip)
                mine = _half(outs[a].at[k], c)
                pltpu.make_async_remote_copy(src_ref=mine, dst_ref=mine, send_sem=ssem.at[a * nn + j],
                                             recv_sem=rsem.at[a * nn + j], device_id=(px, py, c), device_id_type=MESH).start()

    sem = pltpu.SemaphoreType.DMA((nn * n,))
    res = pl.pallas_call(
        body, name=name, out_shape=(sem, sem) + tuple(pltpu.HBM(t.shape, t.dtype) for t in slabs) + (TOKEN,),
        in_specs=[HBM] * n + [ANY] * len(deps), out_specs=tuple([SEM, SEM] + [HBM] * n + [TOKEN_SPEC]),
        input_output_aliases={a: 2 + a for a in range(n)}, compiler_params=SPLIT_COPY,
    )(*_in_hbm(slabs), *deps)
    return (res[0], res[1]), list(res[2:2 + n]), res[2 + n]


def _ag_relay(name, slabs, sems, after, then_start=()):
    n = len(slabs)
    m = len(then_start)
    nn = len(NEIGHBOUR_FLIPS)

    def body(*refs):
        no = n + 2 + m + len(after)
        ins = refs[:n]
        ssem, rsem = refs[n], refs[n + 1]
        r_s, r_r, p_s, p_r = refs[no:no + 4]
        x, y, c = _place()
        k = 2 * x + y
        (fx, fy), _, _ = _relay_chips(x, y, c)
        for a in range(n):
            for j, flip in enumerate(NEIGHBOUR_FLIPS):
                _, _, kj = _chip_of(x, y, flip)
                landed = _half(ins[a].at[kj], c)
                cp = pltpu.make_async_remote_copy(
                    src_ref=_half(ins[a].at[k], c), dst_ref=landed, send_sem=ssem.at[a * nn + j],
                    recv_sem=rsem.at[a * nn + j], device_id=(x, y, c), device_id_type=MESH)
                cp.wait_send()
                cp.wait_recv()
        for a in range(n):
            near = _half(ins[a].at[2 * (x ^ (1 - c)) + (y ^ c)], c)
            pltpu.make_async_remote_copy(src_ref=near, dst_ref=near, send_sem=r_s.at[a], recv_sem=r_r.at[a],
                                         device_id=(fx, fy, c), device_id_type=MESH).start()
            for j, flip in enumerate(NEIGHBOUR_FLIPS):
                _, _, kj = _chip_of(x, y, flip)
                landed = _half(ins[a].at[kj], c)
                pltpu.make_async_remote_copy(src_ref=landed, dst_ref=landed, send_sem=p_s.at[a * nn + j],
                                             recv_sem=p_r.at[a * nn + j], device_id=(x, y, 1 - c), device_id_type=MESH).start()
        if m:
            d_s, d_r = refs[no + 4 + n], refs[no + 5 + n]
            nxt = refs[no + 6 + n:]
            for a in range(m):
                for j, flip in enumerate(NEIGHBOUR_FLIPS):
                    px, py, _ = _chip_of(x, y, flip)
                    mine = _half(nxt[a].at[k], c)
                    pltpu.make_async_remote_copy(src_ref=mine, dst_ref=mine, send_sem=d_s.at[a * nn + j],
                                                 recv_sem=d_r.at[a * nn + j], device_id=(px, py, c), device_id_type=MESH).start()

    rsem_t = pltpu.SemaphoreType.DMA((n,))
    psem_t = pltpu.SemaphoreType.DMA((nn * n,))
    out_shape = (rsem_t, rsem_t, psem_t, psem_t) + tuple(pltpu.HBM(t.shape, t.dtype) for t in slabs)
    out_specs = [SEM] * 4 + [HBM] * n
    aliases = {a: 4 + a for a in range(n)}
    if m:
        dsem_t = pltpu.SemaphoreType.DMA((nn * m,))
        out_shape += (dsem_t, dsem_t) + tuple(pltpu.HBM(t.shape, t.dtype) for t in then_start)
        out_specs += [SEM, SEM] + [HBM] * m
        aliases.update({n + 2 + a: 4 + n + 2 + a for a in range(m)})
    res = pl.pallas_call(
        body, name=name, out_shape=out_shape, in_specs=[HBM] * n + [SEM, SEM] + [HBM] * m + [ANY] * len(after),
        out_specs=tuple(out_specs), input_output_aliases=aliases, compiler_params=SPLIT_COPY,
    )(*slabs, sems[0], sems[1], *_in_hbm(list(then_start)), *after)
    if not m:
        return tuple(res[:4]), list(res[4:])
    return (tuple(res[:4]), list(res[4:4 + n])), ((res[4 + n], res[5 + n]), list(res[6 + n:]))


def _wait_passes(ins, p_s, p_r, x, y, c):
    nn = len(NEIGHBOUR_FLIPS)
    for a in range(len(ins)):
        for j, flip in enumerate(NEIGHBOUR_FLIPS):
            _, _, kj = _chip_of(x, y, flip)
            cp = pltpu.make_async_remote_copy(
                src_ref=_half(ins[a].at[kj], c), dst_ref=_half(ins[a].at[kj], 1 - c), send_sem=p_s.at[a * nn + j],
                recv_sem=p_r.at[a * nn + j], device_id=(x, y, c), device_id_type=MESH)
            cp.wait_send()
            cp.wait_recv()


def _ag_relay_wait(name, slabs, sems, after):
    n = len(slabs)
    ns = len(sems)

    def body(*refs):
        no = n + ns + len(after)
        ins = refs[:n]
        r_s, r_r = refs[n], refs[n + 1]
        f_s, f_r = refs[no], refs[no + 1]
        x, y, c = _place()
        _, _, kd = _relay_chips(x, y, c)
        for a in range(n):
            near = _half(ins[a].at[2 * (x ^ (1 - c)) + (y ^ c)], c)
            cp = pltpu.make_async_remote_copy(src_ref=near, dst_ref=_half(ins[a].at[kd], c), send_sem=r_s.at[a],
                                              recv_sem=r_r.at[a], device_id=(x, y, c), device_id_type=MESH)
            cp.wait_send()
            cp.wait_recv()
        if ns == 4:
            _wait_passes(ins, refs[n + 2], refs[n + 3], x, y, c)
        for a in range(n):
            diag = _half(ins[a].at[kd], c)
            pltpu.make_async_remote_copy(src_ref=diag, dst_ref=diag, send_sem=f_s.at[a], recv_sem=f_r.at[a],
                                         device_id=(x, y, 1 - c), device_id_type=MESH).start()

    sem = pltpu.SemaphoreType.DMA((n,))
    res = pl.pallas_call(
        body, name=name, out_shape=(sem, sem) + tuple(pltpu.HBM(t.shape, t.dtype) for t in slabs),
        in_specs=[HBM] * n + [SEM] * ns + [ANY] * len(after), out_specs=tuple([SEM, SEM] + [HBM] * n),
        input_output_aliases={a: 2 + a for a in range(n)}, compiler_params=SPLIT_COPY,
    )(*slabs, *sems, *after)
    return (res[0], res[1]), list(res[2:])


def _ag_final_wait(name, slabs, sems, after, first=0):
    n = len(slabs)

    def body(*refs):
        ins = refs[:n]
        f_s, f_r = refs[n], refs[n + 1]
        x, y, c = _place()
        _, _, kd = _relay_chips(x, y, c)
        for a in range(n):
            cp = pltpu.make_async_remote_copy(
                src_ref=_half(ins[a].at[kd], c), dst_ref=_half(ins[a].at[kd], 1 - c), send_sem=f_s.at[first + a],
                recv_sem=f_r.at[first + a], device_id=(x, y, c), device_id_type=MESH)
            cp.wait_send()
            cp.wait_recv()

    return pl.pallas_call(
        body, name=name, out_shape=tuple(pltpu.HBM(t.shape, t.dtype) for t in slabs),
        in_specs=[HBM] * n + [SEM, SEM] + [ANY] * len(after), out_specs=tuple([HBM] * n),
        input_output_aliases={a: a for a in range(n)}, compiler_params=SPLIT_COPY,
    )(*slabs, sems[0], sems[1], *after)


def _sibling_part(ref, c, halves):
    if not halves:
        return ref
    h = ref.shape[1] // 2
    return ref.at[:, pl.ds((1 - c) * h, h)]


def _swap_start(name, grads, halves=True, deps=()):
    n = len(grads)

    def body(*refs):
        no = 2 * n + len(deps)
        ssem, rsem = refs[no], refs[no + 1]
        src, land = refs[no + 2:no + n + 2], refs[no + n + 2:no + 2 * n + 2]
        token = refs[no + 2 * n + 2]
        token[...] = jnp.zeros_like(token)
        x, y, c = _place()
        for a in range(n):
            pltpu.make_async_remote_copy(
                src_ref=_sibling_part(src[a], c, halves), dst_ref=land[a], send_sem=ssem.at[a], recv_sem=rsem.at[a],
                device_id=(x, y, 1 - c), device_id_type=MESH).start()

    zones = [lax.empty((g.shape[0], g.shape[1] // 2, g.shape[2]) if halves else g.shape, g.dtype) for g in grads]
    sem = pltpu.SemaphoreType.DMA((n,))
    res = pl.pallas_call(
        body, name=name,
        out_shape=(sem, sem) + tuple(pltpu.HBM(t.shape, t.dtype) for t in list(grads) + zones) + (TOKEN,),
        in_specs=[HBM] * (2 * n) + [ANY] * len(deps), out_specs=tuple([SEM, SEM] + [HBM] * (2 * n) + [TOKEN_SPEC]),
        input_output_aliases={i: 2 + i for i in range(2 * n)}, compiler_params=SPLIT_COPY,
    )(*_in_hbm(list(grads) + zones), *deps)
    return (res[0], res[1], list(res[2:2 + n]), list(res[2 + n:2 + 2 * n])), res[2 + 2 * n]


def _swap_wait(name, ssem, rsem, grads, zones, after, halves=True):
    n = len(grads)

    def body(*refs):
        src, land = refs[:n], refs[n:2 * n]
        ss, rs = refs[2 * n], refs[2 * n + 1]
        x, y, c = _place()
        for a in range(n):
            cp = pltpu.make_async_remote_copy(
                src_ref=_sibling_part(src[a], c, halves), dst_ref=land[a], send_sem=ss.at[a], recv_sem=rs.at[a],
                device_id=(x, y, c), device_id_type=MESH)
            cp.wait_send()
            cp.wait_recv()

    res = pl.pallas_call(
        body, name=name, out_shape=tuple(pltpu.HBM(t.shape, t.dtype) for t in list(grads) + list(zones)),
        in_specs=[HBM] * (2 * n) + [SEM, SEM] + [ANY] * len(after), out_specs=tuple([HBM] * (2 * n)),
        input_output_aliases={i: i for i in range(2 * n)}, compiler_params=SPLIT_COPY,
    )(*grads, *zones, ssem, rsem, *after)
    return list(res[:n]), list(res[n:])


def _scatter_start(name, parts):
    n = len(parts)
    nf = len(CHIP_FLIPS)

    def body(*refs):
        ssem, rsem = refs[2 * n], refs[2 * n + 1]
        src, land = refs[2 * n + 2:3 * n + 2], refs[3 * n + 2:4 * n + 2]
        token = refs[4 * n + 2]
        token[...] = jnp.zeros_like(token)
        x, y, c = _place()
        for a in range(n):
            for j, flip in enumerate(CHIP_FLIPS):
                px, py, kj = _chip_of(x, y, flip)
                pltpu.make_async_remote_copy(
                    src_ref=src[a].at[kj], dst_ref=land[a].at[j], send_sem=ssem.at[a * nf + j], recv_sem=rsem.at[a * nf + j],
                    device_id=(px, py, c), device_id_type=MESH).start()

    zones = [lax.empty((nf,) + p.shape[1:], p.dtype) for p in parts]
    sem = pltpu.SemaphoreType.DMA((nf * n,))
    res = pl.pallas_call(
        body, name=name,
        out_shape=(sem, sem) + tuple(pltpu.HBM(t.shape, t.dtype) for t in list(parts) + zones)
        + (jax.ShapeDtypeStruct((SMALL_ROWS, LANES), F32),),
        in_specs=[HBM] * (2 * n),
        out_specs=tuple([SEM, SEM] + [HBM] * (2 * n) + [pl.BlockSpec(memory_space=pltpu.VMEM)]),
        input_output_aliases={i: 2 + i for i in range(2 * n)}, compiler_params=SPLIT_COPY,
    )(*_in_hbm(list(parts) + zones))
    return (res[0], res[1], list(res[2:2 + n]), list(res[2 + n:2 + 2 * n])), res[2 + 2 * n]


def _scatter_wait(name, ssem, rsem, parts, zones, after):
    n = len(parts)
    nf = len(CHIP_FLIPS)

    def body(*refs):
        src, land = refs[:n], refs[n:2 * n]
        ss, rs = refs[2 * n], refs[2 * n + 1]
        x, y, c = _place()
        for a in range(n):
            for j, flip in enumerate(CHIP_FLIPS):
                _, _, kj = _chip_of(x, y, flip)
                cp = pltpu.make_async_remote_copy(
                    src_ref=src[a].at[kj], dst_ref=land[a].at[j], send_sem=ss.at[a * nf + j], recv_sem=rs.at[a * nf + j],
                    device_id=(x, y, c), device_id_type=MESH)
                cp.wait_send()
                cp.wait_recv()

    res = pl.pallas_call(
        body, name=name, out_shape=tuple(pltpu.HBM(t.shape, t.dtype) for t in list(parts) + list(zones)),
        in_specs=[HBM] * (2 * n) + [SEM, SEM] + [ANY] * len(after), out_specs=tuple([HBM] * (2 * n)),
        input_output_aliases={i: i for i in range(2 * n)}, compiler_params=SPLIT_COPY,
    )(*parts, *zones, ssem, rsem, *after)
    return list(res[:n]), list(res[n:])


N_PEERS = 7


def _peer(x, y, c, mask):
    px, py, pc = x ^ ((mask >> 2) & 1), y ^ ((mask >> 1) & 1), c ^ (mask & 1)
    return (px, py, pc), 4 * px + 2 * py + pc


def _reduce_start(vec, deps):
    nd = len(deps)

    def body(*refs):
        ssem, rsem, src, land, token = refs[2 + nd:]
        token[...] = jnp.zeros_like(token)
        x, y, c = _place()
        me = 4 * x + 2 * y + c
        for mask in range(1, N_PEERS + 1):
            to, _ = _peer(x, y, c, mask)
            pltpu.make_async_remote_copy(src_ref=src, dst_ref=land.at[me], send_sem=ssem.at[mask - 1],
                                         recv_sem=rsem.at[mask - 1], device_id=to, device_id_type=MESH).start()

    zone = lax.empty((N_PEERS + 1,) + vec.shape, vec.dtype)
    sem = pltpu.SemaphoreType.DMA((N_PEERS,))
    res = pl.pallas_call(
        body, name="reduce_start",
        out_shape=(sem, sem, pltpu.HBM(vec.shape, vec.dtype), pltpu.HBM(zone.shape, zone.dtype), TOKEN),
        in_specs=[HBM, HBM] + [ANY] * nd, out_specs=(SEM, SEM, HBM, HBM, TOKEN_SPEC),
        input_output_aliases={0: 2, 1: 3}, compiler_params=SPLIT_COPY,
    )(*_in_hbm([vec, zone]), *deps)
    return res[:4], res[4]


def _reduce_wait(ssem, rsem, vec, zone, after):
    def body(src, land, ss, rs, *_):
        x, y, c = _place()
        for mask in range(1, N_PEERS + 1):
            _, frm = _peer(x, y, c, mask)
            cp = pltpu.make_async_remote_copy(src_ref=src, dst_ref=land.at[frm], send_sem=ss.at[mask - 1],
                                              recv_sem=rs.at[mask - 1], device_id=(x, y, c), device_id_type=MESH)
            cp.wait_send()
            cp.wait_recv()

    return pl.pallas_call(
        body, name="reduce_wait", out_shape=(pltpu.HBM(vec.shape, vec.dtype), pltpu.HBM(zone.shape, zone.dtype)),
        in_specs=[HBM, HBM, SEM, SEM] + [ANY] * len(after), out_specs=(HBM, HBM),
        input_output_aliases={0: 0, 1: 1}, compiler_params=SPLIT_COPY,
    )(vec, zone, ssem, rsem, *after)


def kernel(x, meta_tokens, norm_mix_g, w_in, b_gate, pool_w, pool_scale, conv_w, conv_out_w, w_o, norm_ffn_g, w_gate_up, w_down, norm_final_g, loss_target, m_meta_tokens, m_norm_mix_g, m_w_in, m_b_gate, m_pool_w, m_pool_scale, m_conv_w, m_conv_out_w, m_w_o, m_norm_ffn_g, m_w_gate_up, m_w_down, m_norm_final_g, v_meta_tokens, v_norm_mix_g, v_w_in, v_b_gate, v_pool_w, v_pool_scale, v_conv_w, v_conv_out_w, v_w_o, v_norm_ffn_g, v_w_gate_up, v_w_down, v_norm_final_g):
    seq, dm = x.shape[1], x.shape[2]
    tail = TAIL_ROWS
    lp = seq + tail
    tm_row = _row_tile(lp, dm, 4, 3 * 1024 * 1024)
    tm_seq = _row_tile(seq, dm, 4, 3 * 1024 * 1024)
    n_chips = 4
    n_groups = len(POOL_WINDOWS)
    gw = dm // n_groups
    tc = min(256, gw)
    cx, cy, cc = _place()
    chip = 2 * cx + cy
    dloc = dm // n_chips

    pool2 = pool_w.reshape(n_groups * pool_w.shape[1], gw)
    big = {"w_in": w_in, "w_gate_up": w_gate_up, "pool_w": pool2, "conv_out_w": conv_out_w, "w_o": w_o, "w_down": w_down}
    chip1 = jnp.reshape(chip, (1,)).astype(jnp.int32)
    core = jnp.reshape(cc, (1,)).astype(jnp.int32)
    small_loc = jnp.concatenate([meta_tokens, jnp.pad(conv_w, ((0, 8 - conv_w.shape[0]), (0, 0))),
                                 jnp.zeros((8, dloc), F32)], axis=0)
    g1, g2, g3 = norm_mix_g.reshape(1, dm), norm_ffn_g.reshape(1, dm), norm_final_g.reshape(1, dm)
    b_gate2 = b_gate.reshape(2, dm)
    ps = pool_scale.reshape(1, dm)
    mine = jnp.stack([chip, cc]).astype(jnp.int32)
    other = jnp.stack([chip, 1 - cc]).astype(jnp.int32)
    first = [_cast_into_slot("place_small", small_loc, chip1, F32), _cast_half_into_slot("cast_w_in_sent", w_in, mine, BF16)]
    sems, first, token = _ag_start("ag_start_first", first)
    first[1] = _cast_half_into_slot("cast_w_in_kept", w_in, other, BF16, into=first[1])
    cast = {nme: _cast_into_slot("cast_" + nme, big[nme], chip1, BF16, deps=(token,))
            for nme in ["pool_w", "conv_out_w", "w_o", "w_gate_up", "w_down"]}
    sems, first = _ag_relay("ag_relay_first", first, sems, list(cast.values()))
    sems, (small4, w_in4) = _ag_relay_wait("ag_relay_wait_first", first, sems, [])
    (small4,) = _ag_final_wait("ag_final_wait_small", [small4], sems, [])
    mixer_w = [cast["pool_w"], cast["conv_out_w"], cast["w_o"]]
    sems_mix, mixer_w, token = _ag_start("ag_start_mixer", mixer_w, deps=(small4,))
    sems_gu, (w_gu4,), token = _ag_start("ag_start_gate_up", [cast["w_gate_up"]], deps=(token,))

    small_f = jnp.transpose(small4, (1, 0, 2)).reshape(small4.shape[1], dm)
    meta_f = small_f[:N_META]
    conv_w_f = small_f[N_META:N_META + 3]
    tail_rows = jnp.concatenate([jnp.zeros((tail - N_META, dm), F32), meta_f], axis=0)
    h0_hn1 = _rms_fwd_into("rms_mix", x[0], g1, lp, 0, tm_seq, deps=(token,))
    h0, hn1 = _rms_fwd_into("rms_mix_tail", tail_rows, g1, lp, seq, tail, prev=h0_hn1)
    (w_in4,) = _ag_final_wait("ag_final_wait_first", [w_in4], sems, [hn1], first=1)
    proj = _nn_sharded("proj_0", hn1, w_in4, 6, part=(0, 2))
    sems_mix, mixer_w = _ag_relay("ag_relay_mixer", mixer_w, sems_mix, [proj])
    proj = _nn_sharded("proj_1", hn1, w_in4, 6, part=(1, 2), prev=proj, deps=(mixer_w[0],))
    pooled, z = _mixer_fwd("mixer_fwd", proj, conv_w_f, tc, mixer_w[0])
    (sems_gu, (w_gu4,)), (sems_down, (w_down4,)) = _ag_relay("ag_relay_gate_up", [w_gu4], sems_gu, [pooled],
                                                              then_start=[cast["w_down"]])
    sems_mix, mixer_w = _ag_relay_wait("ag_relay_wait_mixer", mixer_w, sems_mix, [w_down4])
    pool4, conv_out4, w_o4 = _ag_final_wait("ag_final_wait_mixer", mixer_w, sems_mix, [])
    pool_f = jnp.transpose(pool4.reshape(n_chips, n_groups, gw // n_chips, gw), (1, 0, 2, 3)).reshape(n_groups, gw, gw)
    conv_out_f = conv_out4.reshape(dm, dm)
    w_o_f = w_o4.reshape(dm, dm)
    ya = _pool_fwd("pool_proj", pooled, pool_f)
    yb = _nn_plain("conv_out", z, conv_out_f, BF16)
    mix = _gate_mix("gate_mix", proj, b_gate2, ya, ps, yb, tm_row)
    h1 = _nn_plain("attn_out", mix, w_o_f, F32, res=h0, tn_pref=256)
    sems_gu, (w_gu4,) = _ag_relay_wait("ag_relay_wait_gate_up", [w_gu4], sems_gu, [h1])
    hn2 = _rms_fwd("rms_ffn", h1, g2, tm_row, deps=(w_gu4,))
    (w_gu4,) = _ag_final_wait("ag_final_wait_gate_up", [w_gu4], sems_gu, [hn2])
    sems_down, (w_down4,) = _ag_relay("ag_relay_down", [w_down4], sems_down, [hn2])
    gu, act = _gate_up_swiglu("gate_up", hn2, w_gu4, w_down4)
    sems_down, (w_down4,) = _ag_relay_wait("ag_relay_wait_down", [w_down4], sems_down, [act])
    (w_down4,) = _ag_final_wait("ag_final_wait_down", [w_down4], sems_down, [])
    w_down_f = w_down4.reshape(-1, dm)
    h2 = _nn_rows("ffn_down", act, w_down_f, h1)
    dh2, dh2b, loss_cols, dg3 = _final_loss("final_loss", h2, g3, loss_target[0], tm_seq)
    dh2, dh2b = _zero_tail("final_loss_tail", [dh2, dh2b], tail)

    def scatter(tag, names_g, swap, after):
        grads_g, got = _swap_wait("swap_wait_" + tag, *swap, [after])
        pairs = [_pair_add("pair_add_" + nme, g4, rv, core) for nme, g4, rv in zip(names_g, grads_g, got)]
        return _scatter_start("scatter_start_" + tag, pairs)

    dgu = _dact_swiglu_bwd("d_gate_up", dh2b, w_down_f, gu)
    gw_down = _tn_plain("dw_down", act, dh2b)
    gw_gu = _tn_sharded("dw_gate_up", hn2, dgu, n_chips)
    swap_a, token = _swap_start("swap_start_a", [gw_gu, gw_down.reshape(n_chips, -1, dm)])
    dhn2 = _nt_sharded("d_hn2", dgu, w_gu4, tr_pref=2816, row_tiles=2, deps=(token,))
    flight_a, token = scatter("a", ["w_gate_up", "w_down"], swap_a, dhn2)
    dh1, dh1b, dg2 = _rms_bwd("rms_ffn_bwd", dhn2, h1, g2, dh2, tm_row, token)
    dmix = _nt_plain("d_mix", dh1b, w_o_f)
    gw_o = _tn_plain("dw_o", mix, dh1b)
    dproj, dyb, dya, db_gate, dps = _gate_bwd("gate_bwd", dmix, proj, b_gate2, ya, ps, yb, tm_row)
    gw_conv_out = _tn_plain("dw_conv_out", z, dyb)
    gw_pool = _pool_bwd_w("dw_pool", pooled, dya)
    gw_pool = jnp.transpose(gw_pool.reshape(n_groups, n_chips, gw // n_chips, gw), (1, 0, 2, 3))
    swap_b, token = _swap_start("swap_start_b", [gw_o.reshape(n_chips, dloc, dm), gw_conv_out.reshape(n_chips, dloc, dm),
                                                 gw_pool.reshape(n_chips, n_groups * (gw // n_chips), gw)])
    dpooled = _pool_bwd_act("d_pooled", dya, pool_f, deps=(token,))
    dz = _nt_plain("d_z", dyb, conv_out_f)
    flight_b, token = scatter("b", ["w_o", "conv_out_w", "pool_w"], swap_b, dz)
    dproj, dconv_w = _mixer_bwd("mixer_bwd", dz, dpooled, proj, conv_w_f, dproj, tc, token)
    gw_in0 = _tn_sharded("dw_in_0", hn1, dproj, n_chips, part=(0, 2))
    swap_c0, token = _swap_start("swap_start_c0", [gw_in0])
    gw_in1 = _tn_sharded("dw_in_1", hn1, dproj, n_chips, part=(1, 2), deps=(token,))
    flight_c0, token = scatter("c0", ["w_in_0"], swap_c0, gw_in1)
    swap_c, token = _swap_start("swap_start_c", [gw_in1], deps=(token,))
    groups_g = {"a": [("w_gate_up", (0, 1)), ("w_down", (0, 1))], "b": [("w_o", (0, 1)), ("conv_out_w", (0, 1)), ("pool_w", (0, 1))],
                "c0": [("w_in", (0, 2))], "c": [("w_in", (1, 2))]}

    def reduced(tag, flight, after):
        pairs, zones = _scatter_wait("scatter_wait_" + tag, *flight, after)
        halves = [_chip_sum("chip_sum_%s_%d" % (nme, part[0]), p, rv, chip1) for (nme, part), p, rv in zip(groups_g[tag], pairs, zones)]
        return _swap_start("send_start_" + tag, halves, halves=False)

    send_a, token = reduced("a", flight_a, [token])
    flight_c, token = scatter("c", ["w_in_1"], swap_c, token)
    dhn1 = _nt_in_proj("d_hn1", dproj, w_in4, deps=(token,))
    dx, dg1 = _rms_bwd_rows("rms_mix_bwd", dhn1, h0, g1, dh1, 0, seq, tm_seq, token)
    dtail, dg1 = _rms_bwd_rows("rms_mix_bwd_tail", dhn1, h0, g1, dh1, seq, tail, tail, dx, dg_prev=dg1)
    grad_x = dx[None]
    dmeta = dtail[tail - N_META:]

    given = dict(meta_tokens=(meta_tokens, m_meta_tokens, v_meta_tokens), norm_mix_g=(norm_mix_g, m_norm_mix_g, v_norm_mix_g),
                 w_in=(w_in, m_w_in, v_w_in), b_gate=(b_gate, m_b_gate, v_b_gate), pool_w=(pool_w, m_pool_w, v_pool_w),
                 pool_scale=(pool_scale, m_pool_scale, v_pool_scale), conv_w=(conv_w, m_conv_w, v_conv_w),
                 conv_out_w=(conv_out_w, m_conv_out_w, v_conv_out_w), w_o=(w_o, m_w_o, v_w_o),
                 norm_ffn_g=(norm_ffn_g, m_norm_ffn_g, v_norm_ffn_g), w_gate_up=(w_gate_up, m_w_gate_up, v_w_gate_up),
                 w_down=(w_down, m_w_down, v_w_down), norm_final_g=(norm_final_g, m_norm_final_g, v_norm_final_g))
    order = list(given.keys())
    grad, delta, new_m, new_v = {}, {}, {}, {}
    vec = jnp.concatenate([dg1, dg2, dg3, db_gate, dps, loss_cols, dconv_w, dmeta], axis=0)
    loss_row = 5 * SMALL_ROWS
    results = {}

    def update(tag, send, after):
        halves, sib_halves = _swap_wait("send_wait_" + tag, *send, after, halves=False)
        deltas = []
        for (nme, part), g_own, g_sib in zip(groups_g[tag], halves, sib_halves):
            w, m, v = given[nme]
            shape2 = (2 * g_own.shape[0] * part[1], g_own.shape[1])
            results[nme] = _adamw_halves("adamw_%s_%d" % (nme, part[0]), w.reshape(shape2), g_own, g_sib, m.reshape(shape2),
                                         v.reshape(shape2), core, part=part, prev=results.get(nme))
            grad[nme], delta[nme], new_m[nme], new_v[nme] = [t.reshape(w.shape) for t in results[nme]]
            deltas.append(results[nme][1])
        return deltas

    done_a = update("a", send_a, [dx])
    send_b, token = reduced("b", flight_b, done_a)
    send_c0, token = reduced("c0", flight_c0, [token])
    done_b = update("b", send_b, [token])
    send_c, token = reduced("c", flight_c, done_b)
    me1 = jnp.reshape(4 * cx + 2 * cy + cc, (1,)).astype(jnp.int32)
    red_flight, token = _reduce_start(vec, [token])
    done_c0 = update("c0", send_c0, [token])
    done_c = update("c", send_c, done_c0)
    vec, zone = _reduce_wait(*red_flight, done_c)

    small = {"norm_mix_g": (0, False), "norm_ffn_g": (SMALL_ROWS, False), "norm_final_g": (2 * SMALL_ROWS, False),
             "b_gate": (3 * SMALL_ROWS, False), "pool_scale": (4 * SMALL_ROWS, False), "conv_w": (6 * SMALL_ROWS, True),
             "meta_tokens": (7 * SMALL_ROWS, True)}

    def two_d(t, nme):
        return t if t.ndim == 2 else t.reshape(2 if nme == "b_gate" else 1, dm)

    loss11, res = _adamw_small("adamw_small", vec, zone, me1, loss_row, 0.5 / dm, [small[nme][0] for nme in small],
                               [small[nme][1] for nme in small], [[two_d(t, nme) for t in given[nme]] for nme in small])
    loss = loss11[0, 0]
    for nme, res4 in zip(small, res):
        grad[nme], delta[nme], new_m[nme], new_v[nme] = [t.reshape(given[nme][0].shape) for t in res4]
    return (loss, grad_x, *[grad[nme] for nme in order], *[delta[nme] for nme in order],
            *[new_m[nme] for nme in order], *[new_v[nme] for nme in order])
```
